```python
import jax, jax.numpy as jnp
from jax import lax
import numpy as np

D_MODEL = 1024
BATCH = 8
SEQ = 4096
DEPTH = 1

D_MIX = D_MODEL
SC_WIDTH = D_MIX // 2
LRU_WIDTH = D_MIX - SC_WIDTH
SC_GROUPS = 8
LRU_HEADS = 8
LRU_HEAD_DIM = LRU_WIDTH // LRU_HEADS
SC_CONV_W = 3
LRU_CONV_W = 4
RG_C = 8.0
D_FF = 4 * D_MODEL
N_MOD = 6
EPS = 1e-6
D_IN = 3 * SC_WIDTH + 2 * LRU_WIDTH

kernel_name = "hybrid_shortconv_rglru_adaln_block"


def rmsnorm(x, g):
    xf = x.astype(jnp.float32)
    y = xf * lax.rsqrt(jnp.mean(xf * xf, axis=-1, keepdims=True) + EPS)
    return y.astype(x.dtype) * g


def causal_dwconv(u, w):
    k_w = w.shape[0]
    s = u.shape[1]
    up = jnp.pad(u, ((0, 0), (k_w - 1, 0), (0, 0)))
    out = up[:, 0:s] * w[0]
    for k in range(1, k_w):
        out = out + up[:, k:k + s] * w[k]
    return out


def _lru_combine(left, right):
    a_l, b_l = left
    a_r, b_r = right
    return a_l * a_r, a_r * b_l + b_r


def rg_lru(u, w_a, b_a, w_x, b_x, lam):
    bsz, s, _ = u.shape
    uh = u.reshape(bsz, s, LRU_HEADS, LRU_HEAD_DIM)
    r = jax.nn.sigmoid(jnp.einsum('bshi,hij->bshj', uh, w_a) + b_a)
    i = jax.nn.sigmoid(jnp.einsum('bshi,hij->bshj', uh, w_x) + b_x)
    r32 = r.astype(jnp.float32)
    log_a = -RG_C * r32 * jax.nn.softplus(-lam.astype(jnp.float32).reshape(LRU_HEADS, LRU_HEAD_DIM))
    a = jnp.exp(log_a)
    mult = jnp.sqrt(-jnp.expm1(2.0 * log_a))
    b = mult * (i * uh).astype(jnp.float32)
    _, h = lax.associative_scan(_lru_combine, (a, b), axis=1)
    return h.astype(u.dtype).reshape(bsz, s, LRU_WIDTH)


def _fwd_setup_inputs(seed: int = 0) -> dict:
    key = jax.random.key(seed)
    ks = jax.random.split(key, 20)
    f32 = jnp.float32
    nrm = lambda k, shape, scale: jax.random.normal(k, shape, f32) * scale
    u = jax.random.uniform(ks[13], (DEPTH, LRU_WIDTH), f32, 0.9, 0.999)
    a0 = u ** (1.0 / RG_C)
    lam = jnp.log(a0) - jnp.log1p(-a0)
    return {
        "x": nrm(ks[0], (BATCH, SEQ, D_MODEL), 1.0),
        "c": nrm(ks[1], (BATCH, D_MODEL), 1.0),
        "w_ada": nrm(ks[2], (DEPTH, D_MODEL, N_MOD * D_MODEL), 0.5 * D_MODEL ** -0.5),
        "b_ada": nrm(ks[3], (DEPTH, N_MOD * D_MODEL), 0.01),
        "g_mix": 1.0 + nrm(ks[4], (DEPTH, D_MODEL), 0.01),
        "w_in": nrm(ks[5], (DEPTH, D_MODEL, D_IN), D_MODEL ** -0.5),
        "conv_w_sc": nrm(ks[6], (DEPTH, SC_CONV_W, SC_WIDTH), SC_CONV_W ** -0.5),
        "conv_w_lru": nrm(ks[7], (DEPTH, LRU_CONV_W, LRU_WIDTH), LRU_CONV_W ** -0.5),
        "conv_b_lru": nrm(ks[8], (DEPTH, LRU_WIDTH), 0.01),
        "w_rg_a": nrm(ks[9], (DEPTH, LRU_HEADS, LRU_HEAD_DIM, LRU_HEAD_DIM), LRU_HEAD_DIM ** -0.5),
        "b_rg_a": nrm(ks[10], (DEPTH, LRU_HEADS, LRU_HEAD_DIM), 0.01),
        "w_rg_x": nrm(ks[11], (DEPTH, LRU_HEADS, LRU_HEAD_DIM, LRU_HEAD_DIM), LRU_HEAD_DIM ** -0.5),
        "b_rg_x": nrm(ks[12], (DEPTH, LRU_HEADS, LRU_HEAD_DIM), 0.01),
        "lru_lambda": lam,
        "w_out": nrm(ks[14], (DEPTH, D_MIX, D_MODEL), D_MIX ** -0.5),
        "g_mlp": 1.0 + nrm(ks[15], (DEPTH, D_MODEL), 0.01),
        "w_up": nrm(ks[16], (DEPTH, D_MODEL, D_FF), D_MODEL ** -0.5),
        "w_down": nrm(ks[17], (DEPTH, D_FF, D_MODEL), D_FF ** -0.5),
        "g_final": 1.0 + nrm(ks[18], (D_MODEL,), 0.01),
    }


def _fwd_reference(x, c, w_ada, b_ada, g_mix, w_in, conv_w_sc, conv_w_lru, conv_b_lru,
              w_rg_a, b_rg_a, w_rg_x, b_rg_x, lru_lambda, w_out, g_mlp, w_up, w_down,
              g_final):
    c_act = jax.nn.silu(c)
    for l in range(DEPTH):
        mod = c_act @ w_ada[l] + b_ada[l]
        sh_m, sc_m, gt_m, sh_f, sc_f, gt_f = [m[:, None, :] for m in jnp.split(mod, N_MOD, axis=-1)]

        hn = rmsnorm(x, g_mix[l]) * (1.0 + sc_m) + sh_m
        proj = hn @ w_in[l]
        o1 = SC_WIDTH
        o2 = 2 * SC_WIDTH
        o3 = 3 * SC_WIDTH
        o4 = 3 * SC_WIDTH + LRU_WIDTH
        sc_b, sc_c, sc_x = proj[..., :o1], proj[..., o1:o2], proj[..., o2:o3]
        lru_y, lru_x = proj[..., o3:o4], proj[..., o4:]

        y_sc = sc_b * causal_dwconv(sc_c * sc_x, conv_w_sc[l])

        u = causal_dwconv(lru_x, conv_w_lru[l]) + conv_b_lru[l]
        h = rg_lru(u, w_rg_a[l], b_rg_a[l], w_rg_x[l], b_rg_x[l], lru_lambda[l])
        y_lru = jax.nn.gelu(lru_y, approximate=True) * h

        mix = jnp.concatenate([y_sc, y_lru], axis=-1) @ w_out[l]
        x = x + gt_m * mix

        hn = rmsnorm(x, g_mlp[l]) * (1.0 + sc_f) + sh_f
        z = jax.nn.relu(hn @ w_up[l])
        x = x + gt_f * ((z * z) @ w_down[l])
    return rmsnorm(x, g_final)


import jax as _jax
import jax.numpy as _jnp

TWIN_FORMAT = 'train_step'
FWD_PARAMS = ['x', 'c', 'w_ada', 'b_ada', 'g_mix', 'w_in', 'conv_w_sc', 'conv_w_lru', 'conv_b_lru', 'w_rg_a', 'b_rg_a', 'w_rg_x', 'b_rg_x', 'lru_lambda', 'w_out', 'g_mlp', 'w_up', 'w_down', 'g_final']
TWIN_WEIGHTS = ['w_ada', 'b_ada', 'g_mix', 'w_in', 'conv_w_sc', 'conv_w_lru', 'conv_b_lru', 'w_rg_a', 'b_rg_a', 'w_rg_x', 'b_rg_x', 'lru_lambda', 'w_out', 'g_mlp', 'w_up', 'w_down', 'g_final']
TWIN_DIFF_INPUT = 'x'
TWIN_INPUTS = ['x', 'c', 'w_ada', 'b_ada', 'g_mix', 'w_in', 'conv_w_sc', 'conv_w_lru', 'conv_b_lru', 'w_rg_a', 'b_rg_a', 'w_rg_x', 'b_rg_x', 'lru_lambda', 'w_out', 'g_mlp', 'w_up', 'w_down', 'g_final', 'loss_target', 'm_w_ada', 'm_b_ada', 'm_g_mix', 'm_w_in', 'm_conv_w_sc', 'm_conv_w_lru', 'm_conv_b_lru', 'm_w_rg_a', 'm_b_rg_a', 'm_w_rg_x', 'm_b_rg_x', 'm_lru_lambda', 'm_w_out', 'm_g_mlp', 'm_w_up', 'm_w_down', 'm_g_final', 'v_w_ada', 'v_b_ada', 'v_g_mix', 'v_w_in', 'v_conv_w_sc', 'v_conv_w_lru', 'v_conv_b_lru', 'v_w_rg_a', 'v_b_rg_a', 'v_w_rg_x', 'v_b_rg_x', 'v_lru_lambda', 'v_w_out', 'v_g_mlp', 'v_w_up', 'v_w_down', 'v_g_final']
TWIN_OUTPUTS = ['loss', 'grad_x', 'grad_w_ada', 'grad_b_ada', 'grad_g_mix', 'grad_w_in', 'grad_conv_w_sc', 'grad_conv_w_lru', 'grad_conv_b_lru', 'grad_w_rg_a', 'grad_b_rg_a', 'grad_w_rg_x', 'grad_b_rg_x', 'grad_lru_lambda', 'grad_w_out', 'grad_g_mlp', 'grad_w_up', 'grad_w_down', 'grad_g_final', 'delta_w_ada', 'delta_b_ada', 'delta_g_mix', 'delta_w_in', 'delta_conv_w_sc', 'delta_conv_w_lru', 'delta_conv_b_lru', 'delta_w_rg_a', 'delta_b_rg_a', 'delta_w_rg_x', 'delta_b_rg_x', 'delta_lru_lambda', 'delta_w_out', 'delta_g_mlp', 'delta_w_up', 'delta_w_down', 'delta_g_final', 'new_m_w_ada', 'new_m_b_ada', 'new_m_g_mix', 'new_m_w_in', 'new_m_conv_w_sc', 'new_m_conv_w_lru', 'new_m_conv_b_lru', 'new_m_w_rg_a', 'new_m_b_rg_a', 'new_m_w_rg_x', 'new_m_b_rg_x', 'new_m_lru_lambda', 'new_m_w_out', 'new_m_g_mlp', 'new_m_w_up', 'new_m_w_down', 'new_m_g_final', 'new_v_w_ada', 'new_v_b_ada', 'new_v_g_mix', 'new_v_w_in', 'new_v_conv_w_sc', 'new_v_conv_w_lru', 'new_v_conv_b_lru', 'new_v_w_rg_a', 'new_v_b_rg_a', 'new_v_w_rg_x', 'new_v_b_rg_x', 'new_v_lru_lambda', 'new_v_w_out', 'new_v_g_mlp', 'new_v_w_up', 'new_v_w_down', 'new_v_g_final']
TWIN_LEAF_KINDS = {'loss': 'loss', 'grad_x': 'grad_x', 'grad_w_ada': 'grad_w', 'grad_b_ada': 'grad_w', 'grad_g_mix': 'grad_w', 'grad_w_in': 'grad_w', 'grad_conv_w_sc': 'grad_w', 'grad_conv_w_lru': 'grad_w', 'grad_conv_b_lru': 'grad_w', 'grad_w_rg_a': 'grad_w', 'grad_b_rg_a': 'grad_w', 'grad_w_rg_x': 'grad_w', 'grad_b_rg_x': 'grad_w', 'grad_lru_lambda': 'grad_w', 'grad_w_out': 'grad_w', 'grad_g_mlp': 'grad_w', 'grad_w_up': 'grad_w', 'grad_w_down': 'grad_w', 'grad_g_final': 'grad_w', 'delta_w_ada': 'delta_w', 'delta_b_ada': 'delta_w', 'delta_g_mix': 'delta_w', 'delta_w_in': 'delta_w', 'delta_conv_w_sc': 'delta_w', 'delta_conv_w_lru': 'delta_w', 'delta_conv_b_lru': 'delta_w', 'delta_w_rg_a': 'delta_w', 'delta_b_rg_a': 'delta_w', 'delta_w_rg_x': 'delta_w', 'delta_b_rg_x': 'delta_w', 'delta_lru_lambda': 'delta_w', 'delta_w_out': 'delta_w', 'delta_g_mlp': 'delta_w', 'delta_w_up': 'delta_w', 'delta_w_down': 'delta_w', 'delta_g_final': 'delta_w', 'new_m_w_ada': 'new_m', 'new_m_b_ada': 'new_m', 'new_m_g_mix': 'new_m', 'new_m_w_in': 'new_m', 'new_m_conv_w_sc': 'new_m', 'new_m_conv_w_lru': 'new_m', 'new_m_conv_b_lru': 'new_m', 'new_m_w_rg_a': 'new_m', 'new_m_b_rg_a': 'new_m', 'new_m_w_rg_x': 'new_m', 'new_m_b_rg_x': 'new_m', 'new_m_lru_lambda': 'new_m', 'new_m_w_out': 'new_m', 'new_m_g_mlp': 'new_m', 'new_m_w_up': 'new_m', 'new_m_w_down': 'new_m', 'new_m_g_final': 'new_m', 'new_v_w_ada': 'new_v', 'new_v_b_ada': 'new_v', 'new_v_g_mix': 'new_v', 'new_v_w_in': 'new_v', 'new_v_conv_w_sc': 'new_v', 'new_v_conv_w_lru': 'new_v', 'new_v_conv_b_lru': 'new_v', 'new_v_w_rg_a': 'new_v', 'new_v_b_rg_a': 'new_v', 'new_v_w_rg_x': 'new_v', 'new_v_b_rg_x': 'new_v', 'new_v_lru_lambda': 'new_v', 'new_v_w_out': 'new_v', 'new_v_g_mlp': 'new_v', 'new_v_w_up': 'new_v', 'new_v_w_down': 'new_v', 'new_v_g_final': 'new_v'}


def _forward(args):
    return _fwd_reference(*[args[k] for k in FWD_PARAMS])


def _output_shape():
    out = _jax.eval_shape(lambda: _forward(_fwd_setup_inputs(0)))
    return out.shape, out.dtype

N_MICROBATCH = 1
ADAM_LR = 0.001
ADAM_B1 = 0.9
ADAM_B2 = 0.999
ADAM_EPS = 1e-08
ADAM_WD = 0.01
ADAM_STEP = 10
PER_EXAMPLE_BATCH_AXIS = {'x': 0, 'c': 0, 'loss_target': 0}
SHARED_INPUTS = []
_WEIGHT_DTYPES = {'w_ada': _jnp.float32, 'b_ada': _jnp.float32, 'g_mix': _jnp.float32, 'w_in': _jnp.float32, 'conv_w_sc': _jnp.float32, 'conv_w_lru': _jnp.float32, 'conv_b_lru': _jnp.float32, 'w_rg_a': _jnp.float32, 'b_rg_a': _jnp.float32, 'w_rg_x': _jnp.float32, 'b_rg_x': _jnp.float32, 'lru_lambda': _jnp.float32, 'w_out': _jnp.float32, 'g_mlp': _jnp.float32, 'w_up': _jnp.float32, 'w_down': _jnp.float32, 'g_final': _jnp.float32}
MOMENT_SCALE = {'w_ada': 1.074302e-01, 'b_ada': 1.812867e-01, 'g_mix': 1.015291e-01, 'w_in': 7.345223e-02, 'conv_w_sc': 7.014472e-02, 'conv_w_lru': 1.146784e-01, 'conv_b_lru': 3.063543e-01, 'w_rg_a': 1.596113e-02, 'b_rg_a': 1.482936e-02, 'w_rg_x': 3.264563e-02, 'b_rg_x': 3.748165e-02, 'lru_lambda': 4.277065e-02, 'w_out': 7.375915e-02, 'g_mlp': 7.648900e-02, 'w_up': 3.977897e-02, 'w_down': 6.891855e-02, 'g_final': 3.234001e+01}


def _to_microbatches(a, axis):
    t = _jnp.moveaxis(a, axis, 0)
    t = t.reshape((N_MICROBATCH, t.shape[0] // N_MICROBATCH) + t.shape[1:])
    return _jnp.moveaxis(t, 1, axis + 1)


def setup_inputs(seed: int = 0) -> dict:
    inp = _fwd_setup_inputs(seed)
    key = _jax.random.fold_in(_jax.random.key(seed), 7919)
    shape, _ = _output_shape()
    out = dict(inp)
    out["loss_target"] = _jax.random.normal(_jax.random.fold_in(key, 0), shape, _jnp.float32)
    for i, name in enumerate(TWIN_WEIGHTS):
        w = inp[name].astype(_jnp.float32)
        if MOMENT_SCALE is None:
            s = _jnp.sqrt(_jnp.mean(_jnp.square(w)) + 1e-30)
        else:
            s = MOMENT_SCALE[name]
        km, kv = _jax.random.split(_jax.random.fold_in(key, i + 1))
        out[name] = w
        out["m_" + name] = s * _jax.random.normal(km, w.shape, _jnp.float32)
        out["v_" + name] = (s * s) * _jax.random.uniform(kv, w.shape, _jnp.float32, 0.5, 1.5)
    if N_MICROBATCH > 1:
        for name, axis in PER_EXAMPLE_BATCH_AXIS.items():
            out[name] = _to_microbatches(out[name], axis)
    return {'x': out['x'], 'c': out['c'], 'w_ada': out['w_ada'], 'b_ada': out['b_ada'], 'g_mix': out['g_mix'], 'w_in': out['w_in'], 'conv_w_sc': out['conv_w_sc'], 'conv_w_lru': out['conv_w_lru'], 'conv_b_lru': out['conv_b_lru'], 'w_rg_a': out['w_rg_a'], 'b_rg_a': out['b_rg_a'], 'w_rg_x': out['w_rg_x'], 'b_rg_x': out['b_rg_x'], 'lru_lambda': out['lru_lambda'], 'w_out': out['w_out'], 'g_mlp': out['g_mlp'], 'w_up': out['w_up'], 'w_down': out['w_down'], 'g_final': out['g_final'], 'loss_target': out['loss_target'], 'm_w_ada': out['m_w_ada'], 'm_b_ada': out['m_b_ada'], 'm_g_mix': out['m_g_mix'], 'm_w_in': out['m_w_in'], 'm_conv_w_sc': out['m_conv_w_sc'], 'm_conv_w_lru': out['m_conv_w_lru'], 'm_conv_b_lru': out['m_conv_b_lru'], 'm_w_rg_a': out['m_w_rg_a'], 'm_b_rg_a': out['m_b_rg_a'], 'm_w_rg_x': out['m_w_rg_x'], 'm_b_rg_x': out['m_b_rg_x'], 'm_lru_lambda': out['m_lru_lambda'], 'm_w_out': out['m_w_out'], 'm_g_mlp': out['m_g_mlp'], 'm_w_up': out['m_w_up'], 'm_w_down': out['m_w_down'], 'm_g_final': out['m_g_final'], 'v_w_ada': out['v_w_ada'], 'v_b_ada': out['v_b_ada'], 'v_g_mix': out['v_g_mix'], 'v_w_in': out['v_w_in'], 'v_conv_w_sc': out['v_conv_w_sc'], 'v_conv_w_lru': out['v_conv_w_lru'], 'v_conv_b_lru': out['v_conv_b_lru'], 'v_w_rg_a': out['v_w_rg_a'], 'v_b_rg_a': out['v_b_rg_a'], 'v_w_rg_x': out['v_w_rg_x'], 'v_b_rg_x': out['v_b_rg_x'], 'v_lru_lambda': out['v_lru_lambda'], 'v_w_out': out['v_w_out'], 'v_g_mlp': out['v_g_mlp'], 'v_w_up': out['v_w_up'], 'v_w_down': out['v_w_down'], 'v_g_final': out['v_g_final']}


def _loss(weights, diff, rest, loss_target):
    with _jax.named_scope("forward"):
        args = {**rest, TWIN_DIFF_INPUT: diff, **{k: w.astype(_WEIGHT_DTYPES[k]) for k, w in weights.items()}}
        y = _forward(args)
    with _jax.named_scope("loss_head"):
        err = _jnp.square(y.astype(_jnp.float32) - loss_target)
        return 0.5 * _jnp.sum(_jnp.mean(err, axis=-1)) if err.ndim else 0.5 * err


def _adamw(w, g, m, v):
    m = ADAM_B1 * m + (1.0 - ADAM_B1) * g
    v = ADAM_B2 * v + (1.0 - ADAM_B2) * _jnp.square(g)
    m_hat = m / (1.0 - ADAM_B1 ** ADAM_STEP)
    v_hat = v / (1.0 - ADAM_B2 ** ADAM_STEP)
    delta = -ADAM_LR * (m_hat / (_jnp.sqrt(v_hat) + ADAM_EPS) + ADAM_WD * w)
    return delta, m, v


def reference(x, c, w_ada, b_ada, g_mix, w_in, conv_w_sc, conv_w_lru, conv_b_lru, w_rg_a, b_rg_a, w_rg_x, b_rg_x, lru_lambda, w_out, g_mlp, w_up, w_down, g_final, loss_target, m_w_ada, m_b_ada, m_g_mix, m_w_in, m_conv_w_sc, m_conv_w_lru, m_conv_b_lru, m_w_rg_a, m_b_rg_a, m_w_rg_x, m_b_rg_x, m_lru_lambda, m_w_out, m_g_mlp, m_w_up, m_w_down, m_g_final, v_w_ada, v_b_ada, v_g_mix, v_w_in, v_conv_w_sc, v_conv_w_lru, v_conv_b_lru, v_w_rg_a, v_b_rg_a, v_w_rg_x, v_b_rg_x, v_lru_lambda, v_w_out, v_g_mlp, v_w_up, v_w_down, v_g_final):
    given = dict(x=x, c=c, w_ada=w_ada, b_ada=b_ada, g_mix=g_mix, w_in=w_in, conv_w_sc=conv_w_sc, conv_w_lru=conv_w_lru, conv_b_lru=conv_b_lru, w_rg_a=w_rg_a, b_rg_a=b_rg_a, w_rg_x=w_rg_x, b_rg_x=b_rg_x, lru_lambda=lru_lambda, w_out=w_out, g_mlp=g_mlp, w_up=w_up, w_down=w_down, g_final=g_final, loss_target=loss_target, m_w_ada=m_w_ada, m_b_ada=m_b_ada, m_g_mix=m_g_mix, m_w_in=m_w_in, m_conv_w_sc=m_conv_w_sc, m_conv_w_lru=m_conv_w_lru, m_conv_b_lru=m_conv_b_lru, m_w_rg_a=m_w_rg_a, m_b_rg_a=m_b_rg_a, m_w_rg_x=m_w_rg_x, m_b_rg_x=m_b_rg_x, m_lru_lambda=m_lru_lambda, m_w_out=m_w_out, m_g_mlp=m_g_mlp, m_w_up=m_w_up, m_w_down=m_w_down, m_g_final=m_g_final, v_w_ada=v_w_ada, v_b_ada=v_b_ada, v_g_mix=v_g_mix, v_w_in=v_w_in, v_conv_w_sc=v_conv_w_sc, v_conv_w_lru=v_conv_w_lru, v_conv_b_lru=v_conv_b_lru, v_w_rg_a=v_w_rg_a, v_b_rg_a=v_b_rg_a, v_w_rg_x=v_w_rg_x, v_b_rg_x=v_b_rg_x, v_lru_lambda=v_lru_lambda, v_w_out=v_w_out, v_g_mlp=v_g_mlp, v_w_up=v_w_up, v_w_down=v_w_down, v_g_final=v_g_final)
    weights = {n: given[n] for n in TWIN_WEIGHTS}
    shared = {n: given[n] for n in SHARED_INPUTS}
    per_example = {n: given[n] for n in ['x', 'c']}
    grad_fn = _jax.value_and_grad(_loss, argnums=(0, 1))

    def one_microbatch(ex, loss_target):
        ex = dict(ex)
        diff = ex.pop(TWIN_DIFF_INPUT)
        return grad_fn(weights, diff, {**shared, **ex}, loss_target)

    if N_MICROBATCH == 1:
        loss, (grad_w, grad_x) = one_microbatch(per_example, given["loss_target"])
    else:
        def body(carry, xs):
            loss_sum, grad_sum = carry
            l_k, (gw_k, gx_k) = one_microbatch(xs[0], xs[1])
            with _jax.named_scope("update"):
                return (loss_sum + l_k, _jax.tree.map(_jnp.add, grad_sum, gw_k)), gx_k

        init = (_jnp.zeros((), _jnp.float32), _jax.tree.map(_jnp.zeros_like, weights))
        (loss, grad_w), grad_x = _jax.lax.scan(body, init, (per_example, given["loss_target"]))
    with _jax.named_scope("update"):
        delta_w, new_m, new_v = {}, {}, {}
        for n in TWIN_WEIGHTS:
            delta_w[n], new_m[n], new_v[n] = _adamw(weights[n], grad_w[n], given["m_" + n], given["v_" + n])
    return (loss, grad_x, *[grad_w[n] for n in TWIN_WEIGHTS], *[delta_w[n] for n in TWIN_WEIGHTS],
            *[new_m[n] for n in TWIN_WEIGHTS], *[new_v[n] for n in TWIN_WEIGHTS])
```

```python
import functools

import jax
import jax.numpy as jnp
from jax import lax
from jax.experimental import pallas as pl
from jax.experimental.pallas import tpu as pltpu

F32 = jnp.float32
BF16 = jnp.bfloat16
N_DEV = 8
EPS = 1e-6
RG_C = 8.0
GELU_K0 = 0.7978845608028654
GELU_K1 = 0.044715
ADAM_LR = 0.001
ADAM_B1 = 0.9
ADAM_B2 = 0.999
ADAM_EPS = 1e-08
ADAM_WD = 0.01
ADAM_STEP = 10
LANES = 128
SUBLANES = 8
VMEM_LIMIT = 52 * 1024 * 1024
MIX_ROWS = 256
SMALL_ROWS = 80

MESH = pl.DeviceIdType.MESH
ANY = pl.BlockSpec(memory_space=pl.ANY)
NN = ((1,), (0,))
NT = ((1,), (1,))
TN = ((0,), (0,))


def _dot(a, b, dims):
    return lax.dot_general(a, b, (dims, ((), ())), preferred_element_type=F32)


def _params(sem=None):
    return pltpu.CompilerParams(dimension_semantics=sem, vmem_limit_bytes=VMEM_LIMIT)


def _full(shape):
    nd = len(shape)
    return pl.BlockSpec(shape, lambda *_: (0,) * nd)


def _exchange(name, gathers, scatters):
    n_g = len(gathers)
    arrs = list(gathers) + list(scatters)
    n = len(arrs)
    out_shape = [jax.ShapeDtypeStruct((N_DEV,) + a.shape, a.dtype) for a in gathers]
    out_shape += [jax.ShapeDtypeStruct(a.shape, a.dtype) for a in scatters]

    def body(*refs):
        ins, outs = refs[:n], refs[n:2 * n]
        send_sems, recv_sems, local_sems = refs[2 * n:]
        x, y, c = lax.axis_index("x"), lax.axis_index("y"), lax.axis_index("c")
        me = 4 * x + 2 * y + c

        def src(a, dev):
            return ins[a] if a < n_g else ins[a].at[dev]

        def peer_of(k):
            px = 1 - x if (k >> 2) & 1 else x
            py = 1 - y if (k >> 1) & 1 else y
            pc = 1 - c if k & 1 else c
            return (px, py, pc), 4 * px + 2 * py + pc

        local = [pltpu.make_async_copy(src(a, me), outs[a].at[me], local_sems.at[a]) for a in range(n)]
        for cp in local:
            cp.start()
        sends = []
        for k in range(1, N_DEV):
            peer, pidx = peer_of(k)
            for a in range(n):
                cp = pltpu.make_async_remote_copy(
                    src_ref=src(a, pidx), dst_ref=outs[a].at[me],
                    send_sem=send_sems.at[a * (N_DEV - 1) + k - 1], recv_sem=recv_sems.at[a * (N_DEV - 1) + k - 1],
                    device_id=peer, device_id_type=MESH)
                cp.start()
                sends.append(cp)
        for k in range(1, N_DEV):
            peer, pidx = peer_of(k)
            for a in range(n):
                pltpu.make_async_remote_copy(
                    src_ref=src(a, pidx), dst_ref=outs[a].at[pidx],
                    send_sem=send_sems.at[a * (N_DEV - 1) + k - 1], recv_sem=recv_sems.at[a * (N_DEV - 1) + k - 1],
                    device_id=peer, device_id_type=MESH).wait_recv()
        for cp in sends:
            cp.wait_send()
        for cp in local:
            cp.wait()

    return pl.pallas_call(
        body, name=name, out_shape=out_shape,
        in_specs=[ANY] * n, out_specs=[ANY] * n,
        scratch_shapes=[pltpu.SemaphoreType.DMA((n * (N_DEV - 1),)),
                        pltpu.SemaphoreType.DMA((n * (N_DEV - 1),)),
                        pltpu.SemaphoreType.DMA((n,))],
    )(*arrs)


def _ada_fwd(c_all, w_ada_sh, b_ada_sh):
    nb, d = c_all.shape
    ncol = w_ada_sh.shape[1]

    def body(c_ref, w_ref, b_ref, mod_ref, cact_ref):
        cc = c_ref[...]
        ca = cc * jax.nn.sigmoid(cc)
        cact_ref[...] = ca
        mod_ref[...] = _dot(ca.astype(BF16), w_ref[...].astype(BF16), NN) + b_ref[...]

    return pl.pallas_call(
        body, name="ada_fwd",
        out_shape=[jax.ShapeDtypeStruct((nb, ncol), F32), jax.ShapeDtypeStruct((nb, d), F32)],
        compiler_params=_params(),
    )(c_all, w_ada_sh, b_ada_sh)


def _rms(xv):
    rstd = lax.rsqrt(jnp.mean(xv * xv, axis=-1, keepdims=True) + EPS)
    return xv * rstd, rstd


def _rms_bwd(dxhat, xhat, rstd):
    return rstd * (dxhat - xhat * jnp.mean(dxhat * xhat, axis=-1, keepdims=True))


def _colsum(v):
    return jnp.sum(v, axis=0, keepdims=True)


def _expm1(v):
    series = v * (1.0 + v * (0.5 + v * (1.0 / 6.0 + v * (1.0 / 24.0 + v * (1.0 / 120.0 + v * (1.0 / 720.0))))))
    return jnp.where(jnp.abs(v) < 0.3, series, jnp.exp(v) - 1.0)


def _softplus(v):
    return jnp.maximum(v, 0.0) + jnp.log1p(jnp.exp(-jnp.abs(v)))


def _gelu(v):
    t = jnp.tanh(GELU_K0 * (v + GELU_K1 * v * v * v))
    return 0.5 * v * (1.0 + t), t


def _dgelu(v, t):
    return 0.5 * (1.0 + t) + 0.5 * v * (1.0 - t * t) * GELU_K0 * (1.0 + 3.0 * GELU_K1 * v * v)


def _shift_down(v, k, prev8):
    r = pltpu.roll(v, k, 0)
    pr = pltpu.roll(prev8, k, 0)
    row8 = lax.broadcasted_iota(jnp.int32, prev8.shape, 0)
    top = jnp.where(row8 < k, pr, r[0:SUBLANES])
    return jnp.concatenate([top, r[SUBLANES:]], axis=0)


def _shift_up(v, k, next8):
    t = v.shape[0]
    r = pltpu.roll(v, t - k, 0)
    nr = pltpu.roll(next8, SUBLANES - k, 0)
    row8 = lax.broadcasted_iota(jnp.int32, next8.shape, 0)
    bot = jnp.where(row8 >= SUBLANES - k, nr, r[t - SUBLANES:t])
    return jnp.concatenate([r[:t - SUBLANES], bot], axis=0)


def _scan_fwd(a, b, h0):
    t = a.shape[0]
    row = lax.broadcasted_iota(jnp.int32, a.shape, 0)
    s = 1
    while s < t:
        a_sh = pltpu.roll(a, s, 0)
        b_sh = pltpu.roll(b, s, 0)
        m = row >= s
        b = jnp.where(m, a * b_sh + b, b)
        a = jnp.where(m, a * a_sh, a)
        s *= 2
    return b + a * h0


def _scan_rev(m, b, g_next):
    t = m.shape[0]
    row = lax.broadcasted_iota(jnp.int32, m.shape, 0)
    s = 1
    while s < t:
        m_sh = pltpu.roll(m, t - s, 0)
        b_sh = pltpu.roll(b, t - s, 0)
        msk = row < t - s
        b = jnp.where(msk, m * b_sh + b, b)
        m = jnp.where(msk, m * m_sh, m)
        s *= 2
    return b + m * g_next


def _lru_gates(u, wa, wx, ba, bx, sp):
    ub = u.astype(BF16)
    r = jax.nn.sigmoid(_dot(ub, wa, NN) + ba)
    i = jax.nn.sigmoid(_dot(ub, wx, NN) + bx)
    log_a = (-RG_C * r) * sp
    a = jnp.exp(log_a)
    mult = jnp.sqrt(-_expm1(2.0 * log_a))
    return ub, r, i, a, mult


def _conv3(p, pp, w_ref, lo):
    p1 = _shift_down(p, 1, pp)
    p2 = _shift_down(p, 2, pp)
    q = (w_ref[0:1, lo:lo + LANES] * p2 + w_ref[1:2, lo:lo + LANES] * p1) + w_ref[2:3, lo:lo + LANES] * p
    return q, p1, p2


def _conv4(xv, xp, w_ref, b_ref, lo):
    x1 = _shift_down(xv, 1, xp)
    x2 = _shift_down(xv, 2, xp)
    x3 = _shift_down(xv, 3, xp)
    u = (((w_ref[0:1, lo:lo + LANES] * x3 + w_ref[1:2, lo:lo + LANES] * x2) + w_ref[2:3, lo:lo + LANES] * x1)
         + w_ref[3:4, lo:lo + LANES] * xv) + b_ref[:, lo:lo + LANES]
    return u, x1, x2, x3


def _mix_in_fwd(x2d, mod6, g_mix, w_in_t, tm):
    s, d = x2d.shape
    din = w_in_t.shape[0]

    def body(x_ref, mod_ref, g_ref, w_ref, hn_ref, proj_ref):
        xhat, _ = _rms(x_ref[...])
        hn = ((xhat * g_ref[...]) * (1.0 + mod_ref[1:2, :]) + mod_ref[0:1, :]).astype(BF16)
        hn_ref[...] = hn
        proj_ref[...] = _dot(hn, w_ref[...], NT)

    return pl.pallas_call(
        body, name="mix_in_fwd", grid=(s // tm,),
        in_specs=[pl.BlockSpec((tm, d), lambda i: (i, 0)), _full(mod6.shape), _full(g_mix.shape), _full(w_in_t.shape)],
        out_specs=[pl.BlockSpec((tm, d), lambda i: (i, 0)), pl.BlockSpec((tm, din), lambda i: (i, 0))],
        out_shape=[jax.ShapeDtypeStruct((s, d), BF16), jax.ShapeDtypeStruct((s, din), F32)],
        compiler_params=_params(("parallel",)),
    )(x2d, mod6, g_mix, w_in_t)


def _mixer_fwd(proj, conv_sc, conv_lru, conv_b, wa_bd, wx_bd, ba, bx, lam, width):
    s, din = proj.shape
    t = min(MIX_ROWS, s)
    nblk = width // LANES
    hb = t // SUBLANES

    def body(proj_ref, projp_ref, wsc_ref, wlru_ref, blru_ref, wa_ref, wx_ref, ba_ref, bx_ref, lam_ref,
             ymix_ref, h_ref, hc_ref):
        i = pl.program_id(0)

        @pl.when(i == 0)
        def _():
            hc_ref[...] = jnp.zeros_like(hc_ref)

        has_prev = i > 0
        for j in range(nblk):
            lo = j * LANES

            def col(p, ref=proj_ref):
                return ref[:, p * width + lo:p * width + lo + LANES]

            def prev(p):
                return jnp.where(has_prev, col(p, projp_ref), 0.0)

            p = col(1) * col(2)
            q, _, _ = _conv3(p, prev(1) * prev(2), wsc_ref, lo)
            ymix_ref[:, lo:lo + LANES] = (col(0) * q).astype(BF16)

            u, _, _, _ = _conv4(col(4), prev(4), wlru_ref, blru_ref, lo)
            sp = _softplus(-lam_ref[:, lo:lo + LANES])
            _, r, ig, a, mult = _lru_gates(u, wa_ref[j], wx_ref[j], ba_ref[:, lo:lo + LANES], bx_ref[:, lo:lo + LANES], sp)
            h = _scan_fwd(a, mult * (ig * u), hc_ref[0:1, lo:lo + LANES])
            h_ref[:, lo:lo + LANES] = h
            hc_ref[0:1, lo:lo + LANES] = h[t - 1:t, :]
            gel, _ = _gelu(col(3))
            ymix_ref[:, width + lo:width + lo + LANES] = (gel * h).astype(BF16)

    small = [conv_sc, conv_lru, conv_b, wa_bd, wx_bd, ba, bx, lam]
    return pl.pallas_call(
        body, name="mixer_fwd", grid=(s // t,),
        in_specs=[pl.BlockSpec((t, din), lambda i: (i, 0)),
                  pl.BlockSpec((SUBLANES, din), lambda i: (jnp.maximum(i * hb - 1, 0), 0))]
        + [_full(a.shape) for a in small],
        out_specs=[pl.BlockSpec((t, 2 * width), lambda i: (i, 0)), pl.BlockSpec((t, width), lambda i: (i, 0))],
        out_shape=[jax.ShapeDtypeStruct((s, 2 * width), BF16), jax.ShapeDtypeStruct((s, width), F32)],
        scratch_shapes=[pltpu.VMEM((SUBLANES, width), F32)],
        compiler_params=_params(("arbitrary",)),
    )(proj, proj, *small)


def _mix_out_fwd(ymix, x2d, w_out, mod6, g_mlp, tm):
    s, d = x2d.shape

    def body(y_ref, x_ref, w_ref, mod_ref, g_ref, mix_ref, x2_ref, hn_ref):
        mix = _dot(y_ref[...], w_ref[...], NN)
        mix_ref[...] = mix
        x2 = x_ref[...] + mod_ref[2:3, :] * mix
        x2_ref[...] = x2
        xhat, _ = _rms(x2)
        hn_ref[...] = ((xhat * g_ref[...]) * (1.0 + mod_ref[4:5, :]) + mod_ref[3:4, :]).astype(BF16)

    tile = pl.BlockSpec((tm, d), lambda i: (i, 0))
    return pl.pallas_call(
        body, name="mix_out_fwd", grid=(s // tm,),
        in_specs=[tile, tile, _full(w_out.shape), _full(mod6.shape), _full(g_mlp.shape)],
        out_specs=[tile, tile, tile],
        out_shape=[jax.ShapeDtypeStruct((s, d), F32), jax.ShapeDtypeStruct((s, d), F32), jax.ShapeDtypeStruct((s, d), BF16)],
        compiler_params=_params(("parallel",)),
    )(ymix, x2d, w_out, mod6, g_mlp)


def _mlp_fwd(hn2, w_up_t, w_down, tm, tk):
    s, d = hn2.shape
    f = w_up_t.shape[0]

    def body(hn_ref, wu_ref, wd_ref, z_ref, y_ref):
        k = pl.program_id(1)
        z = jnp.maximum(_dot(hn_ref[...], wu_ref[...], NT), 0.0)
        z_ref[...] = z.astype(BF16)
        part = _dot((z * z).astype(BF16), wd_ref[...], NN)

        @pl.when(k == 0)
        def _():
            y_ref[...] = part

        @pl.when(k > 0)
        def _():
            y_ref[...] += part

    return pl.pallas_call(
        body, name="mlp_fwd", grid=(s // tm, f // tk),
        in_specs=[pl.BlockSpec((tm, d), lambda i, k: (i, 0)), pl.BlockSpec((tk, d), lambda i, k: (k, 0)),
                  pl.BlockSpec((tk, d), lambda i, k: (k, 0))],
        out_specs=[pl.BlockSpec((tm, tk), lambda i, k: (i, k)), pl.BlockSpec((tm, d), lambda i, k: (i, 0))],
        out_shape=[jax.ShapeDtypeStruct((s, f), BF16), jax.ShapeDtypeStruct((s, d), F32)],
        compiler_params=_params(("parallel", "arbitrary")),
    )(hn2, w_up_t, w_down)


def _final(x2, y, target, mod6, g_final, tm):
    s, d = x2.shape

    def body(x2_ref, y_ref, t_ref, mod_ref, g_ref, dx3_ref, dyb_ref, st_ref):
        i = pl.program_id(0)

        @pl.when(i == 0)
        def _():
            st_ref[...] = jnp.zeros_like(st_ref)

        gate = mod_ref[5:6, :]
        yv = y_ref[...]
        xhat, rstd = _rms(x2_ref[...] + gate * yv)
        diff = xhat * g_ref[...] - t_ref[...]
        dyo = diff * (1.0 / d)
        dx3 = _rms_bwd(dyo * g_ref[...], xhat, rstd)
        dx3_ref[...] = dx3
        dyb_ref[...] = (gate * dx3).astype(BF16)
        st_ref[0:1, :] += _colsum(dyo * xhat)
        st_ref[1:2, :] += _colsum(dx3 * yv)
        st_ref[2:3, :] += _colsum(diff * diff)

    tile = pl.BlockSpec((tm, d), lambda i: (i, 0))
    return pl.pallas_call(
        body, name="final_loss", grid=(s // tm,),
        in_specs=[tile, tile, tile, _full(mod6.shape), _full(g_final.shape)],
        out_specs=[tile, tile, _full((SUBLANES, d))],
        out_shape=[jax.ShapeDtypeStruct((s, d), F32), jax.ShapeDtypeStruct((s, d), BF16),
                   jax.ShapeDtypeStruct((SUBLANES, d), F32)],
        compiler_params=_params(("arbitrary",)),
    )(x2, y, target, mod6, g_final)


def _mlp_bwd_dx(dyb, z, w_down, w_up_t, tm, tk):
    s, d = dyb.shape
    f = z.shape[1]

    def body(dy_ref, z_ref, wd_ref, wu_ref, dz_ref, dh_ref):
        k = pl.program_id(1)
        dz = ((2.0 * z_ref[...].astype(F32)) * _dot(dy_ref[...], wd_ref[...], NT)).astype(BF16)
        dz_ref[...] = dz
        part = _dot(dz, wu_ref[...], NN)

        @pl.when(k == 0)
        def _():
            dh_ref[...] = part

        @pl.when(k > 0)
        def _():
            dh_ref[...] += part

    return pl.pallas_call(
        body, name="mlp_bwd_dx", grid=(s // tm, f // tk),
        in_specs=[pl.BlockSpec((tm, d), lambda i, k: (i, 0)), pl.BlockSpec((tm, tk), lambda i, k: (i, k)),
                  pl.BlockSpec((tk, d), lambda i, k: (k, 0)), pl.BlockSpec((tk, d), lambda i, k: (k, 0))],
        out_specs=[pl.BlockSpec((tm, tk), lambda i, k: (i, k)), pl.BlockSpec((tm, d), lambda i, k: (i, 0))],
        out_shape=[jax.ShapeDtypeStruct((s, f), BF16), jax.ShapeDtypeStruct((s, d), F32)],
        compiler_params=_params(("parallel", "arbitrary")),
    )(dyb, z, w_down, w_up_t)


def _mlp_bwd_dw(z, dz, dyb, hn2, tm, tk):
    s, d = dyb.shape
    f = z.shape[1]

    def body(z_ref, dz_ref, dy_ref, hn_ref, gd_ref, gu_ref):
        i = pl.program_id(1)

        @pl.when(i == 0)
        def _():
            gd_ref[...] = jnp.zeros_like(gd_ref)
            gu_ref[...] = jnp.zeros_like(gu_ref)

        zf = z_ref[...].astype(F32)
        gd_ref[...] += _dot((zf * zf).astype(BF16), dy_ref[...], TN)
        gu_ref[...] += _dot(dz_ref[...], hn_ref[...], TN)

    return pl.pallas_call(
        body, name="mlp_bwd_dw", grid=(f // tk, s // tm),
        in_specs=[pl.BlockSpec((tm, tk), lambda k, i: (i, k)), pl.BlockSpec((tm, tk), lambda k, i: (i, k)),
                  pl.BlockSpec((tm, d), lambda k, i: (i, 0)), pl.BlockSpec((tm, d), lambda k, i: (i, 0))],
        out_specs=[pl.BlockSpec((tk, d), lambda k, i: (k, 0)), pl.BlockSpec((tk, d), lambda k, i: (k, 0))],
        out_shape=[jax.ShapeDtypeStruct((f, d), F32), jax.ShapeDtypeStruct((f, d), F32)],
        compiler_params=_params(("parallel", "arbitrary")),
    )(z, dz, dyb, hn2)


def _mix_out_bwd(dhn2, x2, dx3, mix, ymix, w_out, mod6, g_mlp, tm):
    s, d = x2.shape

    def body(dh_ref, x2_ref, dx3_ref, mix_ref, y_ref, w_ref, mod_ref, g_ref, dx2_ref, dym_ref, gw_ref, st_ref):
        i = pl.program_id(0)

        @pl.when(i == 0)
        def _():
            st_ref[...] = jnp.zeros_like(st_ref)
            gw_ref[...] = jnp.zeros_like(gw_ref)

        dh = dh_ref[...]
        xhat, rstd = _rms(x2_ref[...])
        dn = dh * (1.0 + mod_ref[4:5, :])
        dx2 = dx3_ref[...] + _rms_bwd(dn * g_ref[...], xhat, rstd)
        dx2_ref[...] = dx2
        st_ref[0:1, :] += _colsum(dh)
        st_ref[1:2, :] += _colsum(dh * (xhat * g_ref[...]))
        st_ref[2:3, :] += _colsum(dn * xhat)
        st_ref[3:4, :] += _colsum(dx2 * mix_ref[...])
        dmix = (mod_ref[2:3, :] * dx2).astype(BF16)
        dym_ref[...] = _dot(dmix, w_ref[...], NT)
        gw_ref[...] += _dot(y_ref[...], dmix, TN)

    tile = pl.BlockSpec((tm, d), lambda i: (i, 0))
    return pl.pallas_call(
        body, name="mix_out_bwd", grid=(s // tm,),
        in_specs=[tile, tile, tile, tile, tile, _full(w_out.shape), _full(mod6.shape), _full(g_mlp.shape)],
        out_specs=[tile, tile, _full((d, d)), _full((SUBLANES, d))],
        out_shape=[jax.ShapeDtypeStruct((s, d), F32), jax.ShapeDtypeStruct((s, d), F32),
                   jax.ShapeDtypeStruct((d, d), F32), jax.ShapeDtypeStruct((SUBLANES, d), F32)],
        compiler_params=_params(("arbitrary",)),
    )(dhn2, x2, dx3, mix, ymix, w_out, mod6, g_mlp)


def _mixer_bwd(proj, dymix, h_all, conv_sc, conv_lru, conv_b, wa_bd, wx_bd, ba, bx, lam, width):
    s, din = proj.shape
    t = min(MIX_ROWS, s)
    nt = s // t
    nblk = width // LANES
    hb = t // SUBLANES
    last8 = s // SUBLANES - 1

    def body(proj_ref, projp_ref, projn_ref, dy_ref, dyn_ref, h_ref, hp_ref,
             wsc_ref, wlru_ref, blru_ref, wa_ref, wx_ref, ba_ref, bx_ref, lam_ref,
             dproj_ref, small_ref, gwa_ref, gwx_ref, an_ref, gn_ref, dun_ref):
        i = pl.program_id(0)

        @pl.when(i == 0)
        def _():
            small_ref[...] = jnp.zeros_like(small_ref)
            gwa_ref[...] = jnp.zeros_like(gwa_ref)
            gwx_ref[...] = jnp.zeros_like(gwx_ref)
            an_ref[...] = jnp.zeros_like(an_ref)
            gn_ref[...] = jnp.zeros_like(gn_ref)
            dun_ref[...] = jnp.zeros_like(dun_ref)

        has_prev = i < nt - 1
        has_next = i > 0
        for j in range(nblk):
            lo = j * LANES
            ls = slice(lo, lo + LANES)

            def col(p, ref=proj_ref):
                return ref[:, p * width + lo:p * width + lo + LANES]

            def prev(p):
                return jnp.where(has_prev, col(p, projp_ref), 0.0)

            def nxt(p):
                return jnp.where(has_next, col(p, projn_ref), 0.0)

            def add_row(r, v):
                small_ref[r:r + 1, ls] += _colsum(v)

            sc_b, sc_c, sc_x = col(0), col(1), col(2)
            p = sc_c * sc_x
            q, p1, p2 = _conv3(p, prev(1) * prev(2), wsc_ref, lo)
            dys = dy_ref[:, ls]
            dproj_ref[:, ls] = (dys * q).astype(BF16)
            dq = dys * sc_b
            dqn = jnp.where(has_next, dyn_ref[:, ls], 0.0) * nxt(0)
            dp = (wsc_ref[2:3, ls] * dq + wsc_ref[1:2, ls] * _shift_up(dq, 1, dqn)) + wsc_ref[0:1, ls] * _shift_up(dq, 2, dqn)
            dproj_ref[:, width + lo:width + lo + LANES] = (dp * sc_x).astype(BF16)
            dproj_ref[:, 2 * width + lo:2 * width + lo + LANES] = (dp * sc_c).astype(BF16)
            add_row(0, dq * p2)
            add_row(1, dq * p1)
            add_row(2, dq * p)

            xv = col(4)
            u, x1, x2, x3 = _conv4(xv, prev(4), wlru_ref, blru_ref, lo)
            lam_v = lam_ref[:, ls]
            sp = _softplus(-lam_v)
            wa, wx = wa_ref[j], wx_ref[j]
            ub, r, ig, a, mult = _lru_gates(u, wa, wx, ba_ref[:, ls], bx_ref[:, ls], sp)
            iu = ig * u
            h = h_ref[:, ls]
            hm1 = _shift_down(h, 1, jnp.where(has_prev, hp_ref[:, ls], 0.0))
            lyv = col(3)
            gel, th = _gelu(lyv)
            dyl = dy_ref[:, width + lo:width + lo + LANES]
            dproj_ref[:, 3 * width + lo:3 * width + lo + LANES] = (dyl * h * _dgelu(lyv, th)).astype(BF16)
            a_next = jnp.broadcast_to(an_ref[0:1, ls], (SUBLANES, LANES))
            g = _scan_rev(_shift_up(a, 1, a_next), dyl * gel, gn_ref[0:1, ls])
            an_ref[0:1, ls] = a[0:1, :]
            gn_ref[0:1, ls] = g[0:1, :]
            da = g * hm1
            dmult = g * iu
            diu = g * mult
            dlog_a = da * a - dmult * ((a * a) / mult)
            dpre_a = (dlog_a * (-RG_C * sp)) * (r * (1.0 - r))
            dpre_x = (diu * u) * (ig * (1.0 - ig))
            dab, dxb = dpre_a.astype(BF16), dpre_x.astype(BF16)
            du = diu * ig + _dot(dab, wa, NT) + _dot(dxb, wx, NT)
            gwa_ref[j] += _dot(ub, dab, TN)
            gwx_ref[j] += _dot(ub, dxb, TN)
            dun = dun_ref[:, ls]
            dun_ref[:, ls] = du[0:SUBLANES, :]
            dlx = (((wlru_ref[3:4, ls] * du + wlru_ref[2:3, ls] * _shift_up(du, 1, dun))
                    + wlru_ref[1:2, ls] * _shift_up(du, 2, dun)) + wlru_ref[0:1, ls] * _shift_up(du, 3, dun))
            dproj_ref[:, 4 * width + lo:4 * width + lo + LANES] = dlx.astype(BF16)
            add_row(3, du * x3)
            add_row(4, du * x2)
            add_row(5, du * x1)
            add_row(6, du * xv)
            add_row(7, du)
            add_row(8, dpre_a)
            add_row(9, dpre_x)
            add_row(10, (dlog_a * (RG_C * r)) * jax.nn.sigmoid(-lam_v))

    small = [conv_sc, conv_lru, conv_b, wa_bd, wx_bd, ba, bx, lam]
    rev = lambda i: nt - 1 - i
    return pl.pallas_call(
        body, name="mixer_bwd", grid=(nt,),
        in_specs=[pl.BlockSpec((t, din), lambda i: (rev(i), 0)),
                  pl.BlockSpec((SUBLANES, din), lambda i: (jnp.maximum(rev(i) * hb - 1, 0), 0)),
                  pl.BlockSpec((SUBLANES, din), lambda i: (jnp.minimum((rev(i) + 1) * hb, last8), 0)),
                  pl.BlockSpec((t, 2 * width), lambda i: (rev(i), 0)),
                  pl.BlockSpec((SUBLANES, 2 * width), lambda i: (jnp.minimum((rev(i) + 1) * hb, last8), 0)),
                  pl.BlockSpec((t, width), lambda i: (rev(i), 0)),
                  pl.BlockSpec((SUBLANES, width), lambda i: (jnp.maximum(rev(i) * hb - 1, 0), 0))]
        + [_full(a.shape) for a in small],
        out_specs=[pl.BlockSpec((t, din), lambda i: (rev(i), 0)), _full((2 * SUBLANES, width)),
                   _full(wa_bd.shape), _full(wx_bd.shape)],
        out_shape=[jax.ShapeDtypeStruct((s, din), BF16), jax.ShapeDtypeStruct((2 * SUBLANES, width), F32),
                   jax.ShapeDtypeStruct(wa_bd.shape, F32), jax.ShapeDtypeStruct(wx_bd.shape, F32)],
        scratch_shapes=[pltpu.VMEM((SUBLANES, width), F32), pltpu.VMEM((SUBLANES, width), F32),
                        pltpu.VMEM((SUBLANES, width), F32)],
        compiler_params=_params(("arbitrary",)),
    )(proj, proj, proj, dymix, dymix, h_all, h_all, *small)


def _mix_in_bwd_dx(dproj, x2d, dx2, w_in_t, mod6, g_mix, tm):
    s, d = x2d.shape
    din = dproj.shape[1]

    def body(dp_ref, x_ref, dx2_ref, w_ref, mod_ref, g_ref, gx_ref, st_ref):
        i = pl.program_id(0)

        @pl.when(i == 0)
        def _():
            st_ref[...] = jnp.zeros_like(st_ref)

        dh = _dot(dp_ref[...], w_ref[...], NN)
        xhat, rstd = _rms(x_ref[...])
        dn = dh * (1.0 + mod_ref[1:2, :])
        gx_ref[...] = dx2_ref[...] + _rms_bwd(dn * g_ref[...], xhat, rstd)
        st_ref[0:1, :] += _colsum(dh)
        st_ref[1:2, :] += _colsum(dh * (xhat * g_ref[...]))
        st_ref[2:3, :] += _colsum(dn * xhat)

    tile = pl.BlockSpec((tm, d), lambda i: (i, 0))
    return pl.pallas_call(
        body, name="mix_in_bwd_dx", grid=(s // tm,),
        in_specs=[pl.BlockSpec((tm, din), lambda i: (i, 0)), tile, tile, _full(w_in_t.shape), _full(mod6.shape),
                  _full(g_mix.shape)],
        out_specs=[tile, _full((SUBLANES, d))],
        out_shape=[jax.ShapeDtypeStruct((s, d), F32), jax.ShapeDtypeStruct((SUBLANES, d), F32)],
        compiler_params=_params(("arbitrary",)),
    )(dproj, x2d, dx2, w_in_t, mod6, g_mix)


def _mix_in_bwd_dw(dproj, hn1, tm, tn):
    s, d = hn1.shape
    din = dproj.shape[1]

    def body(dp_ref, hn_ref, gw_ref):
        i = pl.program_id(1)

        @pl.when(i == 0)
        def _():
            gw_ref[...] = jnp.zeros_like(gw_ref)

        gw_ref[...] += _dot(dp_ref[...], hn_ref[...], TN)

    return pl.pallas_call(
        body, name="mix_in_bwd_dw", grid=(din // tn, s // tm),
        in_specs=[pl.BlockSpec((tm, tn), lambda p, i: (i, p)), pl.BlockSpec((tm, d), lambda p, i: (i, 0))],
        out_specs=pl.BlockSpec((tn, d), lambda p, i: (p, 0)),
        out_shape=jax.ShapeDtypeStruct((din, d), F32),
        compiler_params=_params(("parallel", "arbitrary")),
    )(dproj, hn1)


def _adamw(w, g, m, v):
    m = ADAM_B1 * m + (1.0 - ADAM_B1) * g
    v = ADAM_B2 * v + (1.0 - ADAM_B2) * (g * g)
    m_hat = m / (1.0 - ADAM_B1 ** ADAM_STEP)
    v_hat = v / (1.0 - ADAM_B2 ** ADAM_STEP)
    delta = -ADAM_LR * (m_hat / (jnp.sqrt(v_hat) + ADAM_EPS) + ADAM_WD * w)
    return delta, m, v


def _sum8(parts, tr, name):
    _, rows, n = parts.shape

    def body(p_ref, o_ref):
        acc = p_ref[0]
        for k in range(1, N_DEV):
            acc = acc + p_ref[k]
        o_ref[...] = acc

    return pl.pallas_call(
        body, name=name, grid=(rows // tr,),
        in_specs=[pl.BlockSpec((N_DEV, tr, n), lambda i: (0, i, 0))],
        out_specs=pl.BlockSpec((tr, n), lambda i: (i, 0)),
        out_shape=jax.ShapeDtypeStruct((rows, n), F32),
        compiler_params=_params(("parallel",)),
    )(parts)


def _adam_rows(w, g, m, v, tr, name):
    rows, n = w.shape

    def body(w_ref, g_ref, m_ref, v_ref, d_ref, nm_ref, nv_ref):
        d_ref[...], nm_ref[...], nv_ref[...] = _adamw(w_ref[...], g_ref[...], m_ref[...], v_ref[...])

    tile = pl.BlockSpec((tr, n), lambda i: (i, 0))
    return pl.pallas_call(
        body, name=name, grid=(rows // tr,),
        in_specs=[tile] * 4, out_specs=[tile] * 3,
        out_shape=[jax.ShapeDtypeStruct((rows, n), F32)] * 3,
        compiler_params=_params(("parallel",)),
    )(w, g, m, v)


def _ada_bwd_adam(cact_t, dmod_cols, w, m, v, tr):
    rows, n = w.shape

    def body(c_ref, d_ref, w_ref, m_ref, v_ref, g_ref, dl_ref, nm_ref, nv_ref):
        def term(b):
            return c_ref[b].astype(BF16).astype(F32) * d_ref[b:b + 1, :].astype(BF16).astype(F32)

        g = term(0)
        for b in range(1, N_DEV):
            g = g + term(b)
        g_ref[...] = g
        dl_ref[...], nm_ref[...], nv_ref[...] = _adamw(w_ref[...], g, m_ref[...], v_ref[...])

    tile = pl.BlockSpec((tr, n), lambda i: (i, 0))
    return pl.pallas_call(
        body, name="ada_bwd_adam", grid=(rows // tr,),
        in_specs=[pl.BlockSpec((N_DEV, tr, 1), lambda i: (0, i, 0)), _full(dmod_cols.shape), tile, tile, tile],
        out_specs=[tile] * 4,
        out_shape=[jax.ShapeDtypeStruct((rows, n), F32)] * 4,
        compiler_params=_params(("parallel",)),
    )(cact_t, dmod_cols, w, m, v)


def _adam_small(ws, gs, ms, vs):
    n = len(ws)

    def body(*refs):
        w_r, g_r, m_r, v_r = refs[:n], refs[n:2 * n], refs[2 * n:3 * n], refs[3 * n:4 * n]
        d_r, nm_r, nv_r = refs[4 * n:5 * n], refs[5 * n:6 * n], refs[6 * n:7 * n]
        for k in range(n):
            d_r[k][...], nm_r[k][...], nv_r[k][...] = _adamw(w_r[k][...], g_r[k][...], m_r[k][...], v_r[k][...])

    shapes = [jax.ShapeDtypeStruct(w.shape, F32) for w in ws]
    outs = pl.pallas_call(
        body, name="adam_small", out_shape=shapes * 3, compiler_params=_params(),
    )(*ws, *gs, *ms, *vs)
    return outs[:n], outs[n:2 * n], outs[2 * n:]


def _block_diag(w):
    h, hd, _ = w.shape
    per = LANES // hd
    eye = jnp.eye(per, dtype=w.dtype)
    w5 = w.reshape(h // per, per, hd, 1, hd) * eye[None, :, None, :, None]
    return w5.reshape(h // per, LANES, LANES)


def _block_diag_grad(g, h, hd):
    per = LANES // hd
    g5 = g.reshape(h // per, per, hd, per, hd)
    return jnp.stack([g5[:, a, :, a, :] for a in range(per)], axis=1).reshape(h, hd, hd)


def kernel(x, c, w_ada, b_ada, g_mix, w_in, conv_w_sc, conv_w_lru, conv_b_lru, w_rg_a, b_rg_a, w_rg_x, b_rg_x, lru_lambda, w_out, g_mlp, w_up, w_down, g_final, loss_target, m_w_ada, m_b_ada, m_g_mix, m_w_in, m_conv_w_sc, m_conv_w_lru, m_conv_b_lru, m_w_rg_a, m_b_rg_a, m_w_rg_x, m_b_rg_x, m_lru_lambda, m_w_out, m_g_mlp, m_w_up, m_w_down, m_g_final, v_w_ada, v_b_ada, v_g_mix, v_w_in, v_conv_w_sc, v_conv_w_lru, v_conv_b_lru, v_w_rg_a, v_b_rg_a, v_w_rg_x, v_b_rg_x, v_lru_lambda, v_w_out, v_g_mlp, v_w_up, v_w_down, v_g_final):
    s, d = x.shape[1], x.shape[2]
    width = conv_b_lru.shape[1]
    heads, hd = w_rg_a.shape[1], w_rg_a.shape[2]
    f = w_down.shape[1] * N_DEV
    n_ada = w_ada.shape[2]
    csh = conv_w_sc.shape[2]
    me = 4 * lax.axis_index("x") + 2 * lax.axis_index("y") + lax.axis_index("c")
    tm = min(512, s)
    tm_mlp = min(1024, s)
    tk = 512

    x2d = x[0]
    tgt = loss_target[0]

    pay = jnp.zeros((SUBLANES, d), F32)
    pay = pay.at[0:1, :].set(c)
    pay = pay.at[1:4, 0:csh].set(conv_w_sc[0])
    pay = pay.at[4:8, 0:csh].set(conv_w_lru[0])
    w_in_t_sh = w_in[0].T.astype(BF16)
    w_up_t_sh = w_up[0].T.astype(BF16)
    w_out_sh = w_out[0].astype(BF16)
    w_down_sh = w_down[0].astype(BF16)
    pay_all, w_in_t, w_up_t, w_out_b, w_down_b = _exchange(
        "gather_weights", [pay, w_in_t_sh, w_up_t_sh, w_out_sh, w_down_sh], [])
    w_in_t = w_in_t.reshape(-1, d)
    w_up_t = w_up_t.reshape(-1, d)
    w_out_b = w_out_b.reshape(-1, d)
    w_down_b = w_down_b.reshape(-1, d)
    c_all = pay_all[:, 0, :]
    conv_sc = pay_all[:, 1:4, 0:csh].transpose(1, 0, 2).reshape(3, width)
    conv_lru = pay_all[:, 4:8, 0:csh].transpose(1, 0, 2).reshape(4, width)

    b_ada_sh = lax.dynamic_slice(b_ada, (0, me * n_ada), (1, n_ada))
    mod_cols, c_act = _ada_fwd(c_all, w_ada[0], b_ada_sh)
    (mod_rows,) = _exchange("scatter_mod", [], [mod_cols.reshape(N_DEV, 1, n_ada)])
    mod6 = jnp.zeros((SUBLANES, d), F32).at[0:6, :].set(mod_rows.reshape(6, d))

    wa_bd = _block_diag(w_rg_a[0]).astype(BF16)
    wx_bd = _block_diag(w_rg_x[0]).astype(BF16)
    ba = b_rg_a.reshape(1, width)
    bx = b_rg_x.reshape(1, width)
    g_fin = g_final.reshape(1, d)

    hn1, proj = _mix_in_fwd(x2d, mod6, g_mix, w_in_t, tm)
    ymix, h_all = _mixer_fwd(proj, conv_sc, conv_lru, conv_b_lru, wa_bd, wx_bd, ba, bx, lru_lambda, width)
    mix, x2, hn2 = _mix_out_fwd(ymix, x2d, w_out_b, mod6, g_mlp, tm)
    z, y = _mlp_fwd(hn2, w_up_t, w_down_b, tm_mlp, tk)
    dx3, dyb, st_fin = _final(x2, y, tgt, mod6, g_fin, tm)
    loss = lax.psum((0.5 / d) * jnp.sum(st_fin[2]), ("x", "y", "c"))

    dz, dhn2 = _mlp_bwd_dx(dyb, z, w_down_b, w_up_t, tm_mlp, tk)
    g_down, g_up_t = _mlp_bwd_dw(z, dz, dyb, hn2, tm_mlp, tk)
    dx2, dymix, g_out, st_out = _mix_out_bwd(dhn2, x2, dx3, mix, ymix, w_out_b, mod6, g_mlp, tm)
    dproj, g_small, g_wa, g_wx = _mixer_bwd(proj, dymix, h_all, conv_sc, conv_lru, conv_b_lru, wa_bd, wx_bd,
                                            ba, bx, lru_lambda, width)
    grad_x, st_in = _mix_in_bwd_dx(dproj, x2d, dx2, w_in_t, mod6, g_mix, tm)
    g_in_t = _mix_in_bwd_dw(dproj, hn1, tm, 512)

    zrow = jnp.zeros((1, d), F32)
    small = jnp.concatenate([
        st_in[0:2], st_out[3:4], st_out[0:2], st_fin[1:2],
        st_in[2:3], st_out[2:3], st_fin[0:1],
        jnp.concatenate([g_small[7:8], g_small[10:11]], axis=1),
        jnp.concatenate([g_small[8:9], g_small[9:10]], axis=1),
        jnp.concatenate([jnp.concatenate([g_small[0:3], jnp.zeros((1, width), F32)], axis=0), g_small[3:7]], axis=1),
        zrow,
        _block_diag_grad(g_wa, heads, hd).reshape(-1, d),
        _block_diag_grad(g_wx, heads, hd).reshape(-1, d),
    ], axis=0)

    small_all, r_in, r_up, r_out, r_down = _exchange(
        "exchange_grads", [small],
        [g_in_t.reshape(N_DEV, -1, d), g_up_t.reshape(N_DEV, -1, d), g_out.reshape(N_DEV, -1, d),
         g_down.reshape(N_DEV, -1, d)])

    gs_in = _sum8(r_in, r_in.shape[1], "sum_w_in").T
    gs_up = _sum8(r_up, r_up.shape[1] // 2, "sum_w_up").T
    gs_out = _sum8(r_out, r_out.shape[1], "sum_w_out")
    gs_down = _sum8(r_down, r_down.shape[1] // 2, "sum_w_down")
    ad_in = _adam_rows(w_in[0], gs_in, m_w_in[0], v_w_in[0], 256, "adam_w_in")
    ad_up = _adam_rows(w_up[0], gs_up, m_w_up[0], v_w_up[0], 256, "adam_w_up")
    ad_out = _adam_rows(w_out[0], gs_out, m_w_out[0], v_w_out[0], w_out.shape[1], "adam_w_out")
    ad_down = _adam_rows(w_down[0], gs_down, m_w_down[0], v_w_down[0], 256, "adam_w_down")

    gsum = _sum8(small_all, SMALL_ROWS, "sum_small")
    dmod_cols = lax.dynamic_slice(small_all[:, 0:6, :].reshape(N_DEV, 6 * d), (0, me * n_ada), (N_DEV, n_ada))
    g_ada, d_ada, nm_ada, nv_ada = _ada_bwd_adam(c_act[:, :, None], dmod_cols, w_ada[0], m_w_ada[0], v_w_ada[0], 256)

    g_conv = lax.dynamic_slice(gsum[11:15, 0:width], (0, me * csh), (4, csh))
    g_conv_l = lax.dynamic_slice(gsum[11:15, width:2 * width], (0, me * csh), (4, csh))
    small_g = [
        gsum[0:6].reshape(1, 6 * d),
        gsum[6:7],
        g_conv[0:3].reshape(1, 3, csh),
        g_conv_l.reshape(1, 4, csh),
        gsum[9:10, 0:width],
        gsum[16:48].reshape(1, heads, hd, hd),
        gsum[10:11, 0:width].reshape(1, heads, hd),
        gsum[48:80].reshape(1, heads, hd, hd),
        gsum[10:11, width:].reshape(1, heads, hd),
        gsum[9:10, width:],
        gsum[7:8],
        gsum[8],
    ]
    small_w = [b_ada, g_mix, conv_w_sc, conv_w_lru, conv_b_lru, w_rg_a, b_rg_a, w_rg_x, b_rg_x, lru_lambda, g_mlp, g_final]
    small_m = [m_b_ada, m_g_mix, m_conv_w_sc, m_conv_w_lru, m_conv_b_lru, m_w_rg_a, m_b_rg_a, m_w_rg_x, m_b_rg_x,
               m_lru_lambda, m_g_mlp, m_g_final]
    small_v = [v_b_ada, v_g_mix, v_conv_w_sc, v_conv_w_lru, v_conv_b_lru, v_w_rg_a, v_b_rg_a, v_w_rg_x, v_b_rg_x,
               v_lru_lambda, v_g_mlp, v_g_final]
    sd, snm, snv = _adam_small(small_w, small_g, small_m, small_v)

    def order(ada, w_in_, w_out_, w_up_, w_down_, sm):
        return [ada[None], sm[0], sm[1], w_in_[None], sm[2], sm[3], sm[4], sm[5], sm[6], sm[7], sm[8], sm[9],
                w_out_[None], sm[10], w_up_[None], w_down_[None], sm[11]]

    grads = order(g_ada, gs_in, gs_out, gs_up, gs_down, small_g)
    deltas = order(d_ada, ad_in[0], ad_out[0], ad_up[0], ad_down[0], sd)
    new_m = order(nm_ada, ad_in[1], ad_out[1], ad_up[1], ad_down[1], snm)
    new_v = order(nv_ada, ad_in[2], ad_out[2], ad_up[2], ad_down[2], snv)
    return (loss, grad_x[None], *grads, *deltas, *new_m, *new_v)
```

```python
import functools

import jax
import jax.numpy as jnp
from jax import lax
from jax.experimental import pallas as pl
from jax.experimental.pallas import tpu as pltpu

F32 = jnp.float32
BF16 = jnp.bfloat16
N_DEV = 8
EPS = 1e-6
RG_C = 8.0
GELU_K0 = 0.7978845608028654
GELU_K1 = 0.044715
ADAM_LR = 0.001
ADAM_B1 = 0.9
ADAM_B2 = 0.999
ADAM_EPS = 1e-08
ADAM_WD = 0.01
ADAM_STEP = 10
LANES = 128
SUBLANES = 8
VMEM_LIMIT = 52 * 1024 * 1024
MIX_ROWS = 256
SMALL_ROWS = 80

MESH = pl.DeviceIdType.MESH
ANY = pl.BlockSpec(memory_space=pl.ANY)
NN = ((1,), (0,))
NT = ((1,), (1,))
TN = ((0,), (0,))


def _dot(a, b, dims):
    return lax.dot_general(a, b, (dims, ((), ())), preferred_element_type=F32)


def _params(sem=None):
    return pltpu.CompilerParams(dimension_semantics=sem, vmem_limit_bytes=VMEM_LIMIT)


def _full(shape):
    nd = len(shape)
    return pl.BlockSpec(shape, lambda *_: (0,) * nd)


def _exchange(name, gathers, scatters):
    n_g = len(gathers)
    arrs = list(gathers) + list(scatters)
    n = len(arrs)
    out_shape = [jax.ShapeDtypeStruct((N_DEV,) + a.shape, a.dtype) for a in gathers]
    out_shape += [jax.ShapeDtypeStruct(a.shape, a.dtype) for a in scatters]

    def body(*refs):
        ins, outs = refs[:n], refs[n:2 * n]
        send_sems, recv_sems, local_sems = refs[2 * n:]
        x, y, c = lax.axis_index("x"), lax.axis_index("y"), lax.axis_index("c")
        me = 4 * x + 2 * y + c

        def src(a, dev):
            return ins[a] if a < n_g else ins[a].at[dev]

        def peer_of(k):
            px = 1 - x if (k >> 2) & 1 else x
            py = 1 - y if (k >> 1) & 1 else y
            pc = 1 - c if k & 1 else c
            return (px, py, pc), 4 * px + 2 * py + pc

        local = [pltpu.make_async_copy(src(a, me), outs[a].at[me], local_sems.at[a]) for a in range(n)]
        for cp in local:
            cp.start()
        sends = []
        for k in range(1, N_DEV):
            peer, pidx = peer_of(k)
            for a in range(n):
                cp = pltpu.make_async_remote_copy(
                    src_ref=src(a, pidx), dst_ref=outs[a].at[me],
                    send_sem=send_sems.at[a * (N_DEV - 1) + k - 1], recv_sem=recv_sems.at[a * (N_DEV - 1) + k - 1],
                    device_id=peer, device_id_type=MESH)
                cp.start()
                sends.append(cp)
        for k in range(1, N_DEV):
            peer, pidx = peer_of(k)
            for a in range(n):
                pltpu.make_async_remote_copy(
                    src_ref=src(a, pidx), dst_ref=outs[a].at[pidx],
                    send_sem=send_sems.at[a * (N_DEV - 1) + k - 1], recv_sem=recv_sems.at[a * (N_DEV - 1) + k - 1],
                    device_id=peer, device_id_type=MESH).wait_recv()
        for cp in sends:
            cp.wait_send()
        for cp in local:
            cp.wait()

    return pl.pallas_call(
        body, name=name, out_shape=out_shape,
        in_specs=[ANY] * n, out_specs=[ANY] * n,
        scratch_shapes=[pltpu.SemaphoreType.DMA((n * (N_DEV - 1),)),
                        pltpu.SemaphoreType.DMA((n * (N_DEV - 1),)),
                        pltpu.SemaphoreType.DMA((n,))],
    )(*arrs)


def _gather2(name, arrs):
    n = len(arrs)
    per = 7
    out_shape = [jax.ShapeDtypeStruct((N_DEV,) + a.shape, a.dtype) for a in arrs]

    def body(*refs):
        ins, outs = refs[:n], refs[n:2 * n]
        send_sems, recv_sems, local_sems = refs[2 * n:]
        x, y, c = lax.axis_index("x"), lax.axis_index("y"), lax.axis_index("c")
        sib = (x, y, 1 - c)
        chips = [(1 - x, y), (x, 1 - y), (1 - x, 1 - y)]

        def slot(a, px, py, pc):
            return outs[a].at[4 * px + 2 * py + pc]

        def copy(a, k, block, to, src=None):
            return pltpu.make_async_remote_copy(
                src_ref=slot(a, *block) if src is None else src, dst_ref=slot(a, *block),
                send_sem=send_sems.at[a * per + k], recv_sem=recv_sems.at[a * per + k],
                device_id=to, device_id_type=MESH)

        local = [pltpu.make_async_copy(ins[a], slot(a, x, y, c), local_sems.at[a]) for a in range(n)]
        for cp in local:
            cp.start()
        first = []
        for a in range(n):
            first += [copy(a, 1 + j, (x, y, c), (*chip, c), src=ins[a]) for j, chip in enumerate(chips)]
        for a in range(n):
            first.append(copy(a, 0, (x, y, c), sib, src=ins[a]))
        for cp in first:
            cp.start()
        passed = []
        for a in range(n):
            for j, chip in enumerate(chips):
                copy(a, 1 + j, (*chip, c), (x, y, c)).wait_recv()
                cp = copy(a, 4 + j, (*chip, c), sib)
                cp.start()
                passed.append(cp)
        for a in range(n):
            copy(a, 0, sib, (x, y, c)).wait_recv()
            for j, chip in enumerate(chips):
                copy(a, 4 + j, (*chip, 1 - c), (x, y, c)).wait_recv()
        for cp in first + passed:
            cp.wait_send()
        for cp in local:
            cp.wait()

    return pl.pallas_call(
        body, name=name, out_shape=out_shape,
        in_specs=[ANY] * n, out_specs=[ANY] * n,
        scratch_shapes=[pltpu.SemaphoreType.DMA((n * per,)), pltpu.SemaphoreType.DMA((n * per,)),
                        pltpu.SemaphoreType.DMA((n,))],
    )(*arrs)


def _pair_swap(name, arrs):
    n = len(arrs)
    out_shape = [jax.ShapeDtypeStruct((4,) + a.shape[2:], a.dtype) for a in arrs]

    def body(*refs):
        ins, outs = refs[:n], refs[n:2 * n]
        send_sems, recv_sems = refs[2 * n:]
        x, y, c = lax.axis_index("x"), lax.axis_index("y"), lax.axis_index("c")

        def copy(a, q):
            return pltpu.make_async_remote_copy(
                src_ref=ins[a].at[q, 1 - c], dst_ref=outs[a].at[q],
                send_sem=send_sems.at[a * 4 + q], recv_sem=recv_sems.at[a * 4 + q],
                device_id=(x, y, 1 - c), device_id_type=MESH)

        cps = [copy(a, q) for a in range(n) for q in range(4)]
        for cp in cps:
            cp.start()
        for cp in cps:
            cp.wait_recv()
        for cp in cps:
            cp.wait_send()

    return pl.pallas_call(
        body, name=name, out_shape=out_shape,
        in_specs=[ANY] * n, out_specs=[ANY] * n,
        scratch_shapes=[pltpu.SemaphoreType.DMA((n * 4,)), pltpu.SemaphoreType.DMA((n * 4,))],
    )(*arrs)


def _chip_exchange(name, arrs):
    n = len(arrs)
    out_shape = [jax.ShapeDtypeStruct((3,) + a.shape[1:], a.dtype) for a in arrs]

    def body(*refs):
        ins, outs = refs[:n], refs[n:2 * n]
        send_sems, recv_sems = refs[2 * n:]
        x, y, c = lax.axis_index("x"), lax.axis_index("y"), lax.axis_index("c")

        def copy(a, k):
            px = 1 - x if (k >> 1) & 1 else x
            py = 1 - y if k & 1 else y
            return pltpu.make_async_remote_copy(
                src_ref=ins[a].at[2 * px + py], dst_ref=outs[a].at[k - 1],
                send_sem=send_sems.at[a * 3 + k - 1], recv_sem=recv_sems.at[a * 3 + k - 1],
                device_id=(px, py, c), device_id_type=MESH)

        cps = [copy(a, k) for a in range(n) for k in (1, 2, 3)]
        for cp in cps:
            cp.start()
        for cp in cps:
            cp.wait_recv()
        for cp in cps:
            cp.wait_send()

    return pl.pallas_call(
        body, name=name, out_shape=out_shape,
        in_specs=[ANY] * n, out_specs=[ANY] * n,
        scratch_shapes=[pltpu.SemaphoreType.DMA((n * 3,)), pltpu.SemaphoreType.DMA((n * 3,))],
    )(*arrs)


def _ada_fwd(c_all, w_ada_sh, b_ada_sh):
    nb, d = c_all.shape
    ncol = w_ada_sh.shape[1]

    def body(c_ref, w_ref, b_ref, mod_ref, cact_ref):
        cc = c_ref[...]
        ca = cc * jax.nn.sigmoid(cc)
        cact_ref[...] = ca
        mod_ref[...] = _dot(ca.astype(BF16), w_ref[...].astype(BF16), NN) + b_ref[...]

    return pl.pallas_call(
        body, name="ada_fwd",
        out_shape=[jax.ShapeDtypeStruct((nb, ncol), F32), jax.ShapeDtypeStruct((nb, d), F32)],
        compiler_params=_params(),
    )(c_all, w_ada_sh, b_ada_sh)


def _rms(xv):
    rstd = lax.rsqrt(jnp.mean(xv * xv, axis=-1, keepdims=True) + EPS)
    return xv * rstd, rstd


def _rms_bwd(dxhat, xhat, rstd):
    return rstd * (dxhat - xhat * jnp.mean(dxhat * xhat, axis=-1, keepdims=True))


def _colsum(v):
    return jnp.sum(v, axis=0, keepdims=True)


def _expm1(v):
    series = v * (1.0 + v * (0.5 + v * (1.0 / 6.0 + v * (1.0 / 24.0 + v * (1.0 / 120.0 + v * (1.0 / 720.0))))))
    return jnp.where(jnp.abs(v) < 0.3, series, jnp.exp(v) - 1.0)


def _softplus(v):
    return jnp.maximum(v, 0.0) + jnp.log1p(jnp.exp(-jnp.abs(v)))


def _gelu(v):
    t = jnp.tanh(GELU_K0 * (v + GELU_K1 * v * v * v))
    return 0.5 * v * (1.0 + t), t


def _dgelu(v, t):
    return 0.5 * (1.0 + t) + 0.5 * v * (1.0 - t * t) * GELU_K0 * (1.0 + 3.0 * GELU_K1 * v * v)


def _shift_down(v, k, prev8):
    r = pltpu.roll(v, k, 0)
    pr = pltpu.roll(prev8, k, 0)
    row8 = lax.broadcasted_iota(jnp.int32, prev8.shape, 0)
    top = jnp.where(row8 < k, pr, r[0:SUBLANES])
    return jnp.concatenate([top, r[SUBLANES:]], axis=0)


def _shift_up(v, k, next8):
    t = v.shape[0]
    r = pltpu.roll(v, t - k, 0)
    nr = pltpu.roll(next8, SUBLANES - k, 0)
    row8 = lax.broadcasted_iota(jnp.int32, next8.shape, 0)
    bot = jnp.where(row8 >= SUBLANES - k, nr, r[t - SUBLANES:t])
    return jnp.concatenate([r[:t - SUBLANES], bot], axis=0)


def _scan_fwd(a, b, h0):
    t = a.shape[0]
    row = lax.broadcasted_iota(jnp.int32, a.shape, 0)
    s = 1
    while s < t:
        a_sh = pltpu.roll(a, s, 0)
        b_sh = pltpu.roll(b, s, 0)
        m = row >= s
        b = jnp.where(m, a * b_sh + b, b)
        a = jnp.where(m, a * a_sh, a)
        s *= 2
    return b + a * h0


def _scan_rev(m, b, g_next):
    t = m.shape[0]
    row = lax.broadcasted_iota(jnp.int32, m.shape, 0)
    s = 1
    while s < t:
        m_sh = pltpu.roll(m, t - s, 0)
        b_sh = pltpu.roll(b, t - s, 0)
        msk = row < t - s
        b = jnp.where(msk, m * b_sh + b, b)
        m = jnp.where(msk, m * m_sh, m)
        s *= 2
    return b + m * g_next


def _lru_gates(u, wa, wx, ba, bx, sp):
    ub = u.astype(BF16)
    r = jax.nn.sigmoid(_dot(ub, wa, NN) + ba)
    i = jax.nn.sigmoid(_dot(ub, wx, NN) + bx)
    log_a = (-RG_C * r) * sp
    a = jnp.exp(log_a)
    mult = jnp.sqrt(-_expm1(2.0 * log_a))
    return ub, r, i, a, mult


def _conv3(p, pp, w_ref, lo):
    p1 = _shift_down(p, 1, pp)
    p2 = _shift_down(p, 2, pp)
    q = (w_ref[0:1, lo:lo + LANES] * p2 + w_ref[1:2, lo:lo + LANES] * p1) + w_ref[2:3, lo:lo + LANES] * p
    return q, p1, p2


def _conv4(xv, xp, w_ref, b_ref, lo):
    x1 = _shift_down(xv, 1, xp)
    x2 = _shift_down(xv, 2, xp)
    x3 = _shift_down(xv, 3, xp)
    u = (((w_ref[0:1, lo:lo + LANES] * x3 + w_ref[1:2, lo:lo + LANES] * x2) + w_ref[2:3, lo:lo + LANES] * x1)
         + w_ref[3:4, lo:lo + LANES] * xv) + b_ref[:, lo:lo + LANES]
    return u, x1, x2, x3


def _mix_in_fwd(x2d, mod6, g_mix, w_in_t, tm):
    s, d = x2d.shape
    din = w_in_t.shape[0]

    def body(x_ref, mod_ref, g_ref, w_ref, hn_ref, proj_ref):
        xhat, _ = _rms(x_ref[...])
        hn = ((xhat * g_ref[...]) * (1.0 + mod_ref[1:2, :]) + mod_ref[0:1, :]).astype(BF16)
        hn_ref[...] = hn
        proj_ref[...] = _dot(hn, w_ref[...], NT)

    return pl.pallas_call(
        body, name="mix_in_fwd", grid=(s // tm,),
        in_specs=[pl.BlockSpec((tm, d), lambda i: (i, 0)), _full(mod6.shape), _full(g_mix.shape), _full(w_in_t.shape)],
        out_specs=[pl.BlockSpec((tm, d), lambda i: (i, 0)), pl.BlockSpec((tm, din), lambda i: (i, 0))],
        out_shape=[jax.ShapeDtypeStruct((s, d), BF16), jax.ShapeDtypeStruct((s, din), F32)],
        compiler_params=_params(("parallel",)),
    )(x2d, mod6, g_mix, w_in_t)


def _mixer_fwd(proj, conv_sc, conv_lru, conv_b, wa_bd, wx_bd, ba, bx, lam, width):
    s, din = proj.shape
    t = min(MIX_ROWS, s)
    nblk = width // LANES
    hb = t // SUBLANES

    def body(proj_ref, projp_ref, wsc_ref, wlru_ref, blru_ref, wa_ref, wx_ref, ba_ref, bx_ref, lam_ref,
             ymix_ref, h_ref, hc_ref):
        i = pl.program_id(0)

        @pl.when(i == 0)
        def _():
            hc_ref[...] = jnp.zeros_like(hc_ref)

        has_prev = i > 0
        for j in range(nblk):
            lo = j * LANES

            def col(p, ref=proj_ref):
                return ref[:, p * width + lo:p * width + lo + LANES]

            def prev(p):
                return jnp.where(has_prev, col(p, projp_ref), 0.0)

            p = col(1) * col(2)
            q, _, _ = _conv3(p, prev(1) * prev(2), wsc_ref, lo)
            ymix_ref[:, lo:lo + LANES] = (col(0) * q).astype(BF16)

            u, _, _, _ = _conv4(col(4), prev(4), wlru_ref, blru_ref, lo)
            sp = _softplus(-lam_ref[:, lo:lo + LANES])
            _, r, ig, a, mult = _lru_gates(u, wa_ref[j], wx_ref[j], ba_ref[:, lo:lo + LANES], bx_ref[:, lo:lo + LANES], sp)
            h = _scan_fwd(a, mult * (ig * u), hc_ref[0:1, lo:lo + LANES])
            h_ref[:, lo:lo + LANES] = h
            hc_ref[0:1, lo:lo + LANES] = h[t - 1:t, :]
            gel, _ = _gelu(col(3))
            ymix_ref[:, width + lo:width + lo + LANES] = (gel * h).astype(BF16)

    small = [conv_sc, conv_lru, conv_b, wa_bd, wx_bd, ba, bx, lam]
    return pl.pallas_call(
        body, name="mixer_fwd", grid=(s // t,),
        in_specs=[pl.BlockSpec((t, din), lambda i: (i, 0)),
                  pl.BlockSpec((SUBLANES, din), lambda i: (jnp.maximum(i * hb - 1, 0), 0))]
        + [_full(a.shape) for a in small],
        out_specs=[pl.BlockSpec((t, 2 * width), lambda i: (i, 0)), pl.BlockSpec((t, width), lambda i: (i, 0))],
        out_shape=[jax.ShapeDtypeStruct((s, 2 * width), BF16), jax.ShapeDtypeStruct((s, width), F32)],
        scratch_shapes=[pltpu.VMEM((SUBLANES, width), F32)],
        compiler_params=_params(("arbitrary",)),
    )(proj, proj, *small)


def _mix_out_fwd(ymix, x2d, w_out, mod6, g_mlp, tm):
    s, d = x2d.shape

    def body(y_ref, x_ref, w_ref, mod_ref, g_ref, mix_ref, x2_ref, hn_ref):
        mix = _dot(y_ref[...], w_ref[...], NN)
        mix_ref[...] = mix
        x2 = x_ref[...] + mod_ref[2:3, :] * mix
        x2_ref[...] = x2
        xhat, _ = _rms(x2)
        hn_ref[...] = ((xhat * g_ref[...]) * (1.0 + mod_ref[4:5, :]) + mod_ref[3:4, :]).astype(BF16)

    tile = pl.BlockSpec((tm, d), lambda i: (i, 0))
    return pl.pallas_call(
        body, name="mix_out_fwd", grid=(s // tm,),
        in_specs=[tile, tile, _full(w_out.shape), _full(mod6.shape), _full(g_mlp.shape)],
        out_specs=[tile, tile, tile],
        out_shape=[jax.ShapeDtypeStruct((s, d), F32), jax.ShapeDtypeStruct((s, d), F32), jax.ShapeDtypeStruct((s, d), BF16)],
        compiler_params=_params(("parallel",)),
    )(ymix, x2d, w_out, mod6, g_mlp)


def _mlp_fwd(hn2, w_up_t, w_down, tm, tk):
    s, d = hn2.shape
    f = w_up_t.shape[0]

    def body(hn_ref, wu_ref, wd_ref, z_ref, y_ref):
        k = pl.program_id(1)
        z = jnp.maximum(_dot(hn_ref[...], wu_ref[...], NT), 0.0)
        z_ref[...] = z.astype(BF16)
        part = _dot((z * z).astype(BF16), wd_ref[...], NN)

        @pl.when(k == 0)
        def _():
            y_ref[...] = part

        @pl.when(k > 0)
        def _():
            y_ref[...] += part

    return pl.pallas_call(
        body, name="mlp_fwd", grid=(s // tm, f // tk),
        in_specs=[pl.BlockSpec((tm, d), lambda i, k: (i, 0)), pl.BlockSpec((tk, d), lambda i, k: (k, 0)),
                  pl.BlockSpec((tk, d), lambda i, k: (k, 0))],
        out_specs=[pl.BlockSpec((tm, tk), lambda i, k: (i, k)), pl.BlockSpec((tm, d), lambda i, k: (i, 0))],
        out_shape=[jax.ShapeDtypeStruct((s, f), BF16), jax.ShapeDtypeStruct((s, d), F32)],
        compiler_params=_params(("parallel", "arbitrary")),
    )(hn2, w_up_t, w_down)


def _final(x2, y, target, mod6, g_final, tm):
    s, d = x2.shape

    def body(x2_ref, y_ref, t_ref, mod_ref, g_ref, dx3_ref, dyb_ref, st_ref):
        i = pl.program_id(0)

        @pl.when(i == 0)
        def _():
            st_ref[...] = jnp.zeros_like(st_ref)

        gate = mod_ref[5:6, :]
        yv = y_ref[...]
        xhat, rstd = _rms(x2_ref[...] + gate * yv)
        diff = xhat * g_ref[...] - t_ref[...]
        dyo = diff * (1.0 / d)
        dx3 = _rms_bwd(dyo * g_ref[...], xhat, rstd)
        dx3_ref[...] = dx3
        dyb_ref[...] = (gate * dx3).astype(BF16)
        st_ref[0:1, :] += _colsum(dyo * xhat)
        st_ref[1:2, :] += _colsum(dx3 * yv)
        st_ref[2:3, :] += _colsum(diff * diff)

    tile = pl.BlockSpec((tm, d), lambda i: (i, 0))
    return pl.pallas_call(
        body, name="final_loss", grid=(s // tm,),
        in_specs=[tile, tile, tile, _full(mod6.shape), _full(g_final.shape)],
        out_specs=[tile, tile, _full((SUBLANES, d))],
        out_shape=[jax.ShapeDtypeStruct((s, d), F32), jax.ShapeDtypeStruct((s, d), BF16),
                   jax.ShapeDtypeStruct((SUBLANES, d), F32)],
        compiler_params=_params(("arbitrary",)),
    )(x2, y, target, mod6, g_final)


def _mlp_bwd_dx(dyb, z, w_down, w_up_t, tm, tk):
    s, d = dyb.shape
    f = z.shape[1]

    def body(dy_ref, z_ref, wd_ref, wu_ref, dz_ref, dh_ref):
        k = pl.program_id(1)
        dz = ((2.0 * z_ref[...].astype(F32)) * _dot(dy_ref[...], wd_ref[...], NT)).astype(BF16)
        dz_ref[...] = dz
        part = _dot(dz, wu_ref[...], NN)

        @pl.when(k == 0)
        def _():
            dh_ref[...] = part

        @pl.when(k > 0)
        def _():
            dh_ref[...] += part

    return pl.pallas_call(
        body, name="mlp_bwd_dx", grid=(s // tm, f // tk),
        in_specs=[pl.BlockSpec((tm, d), lambda i, k: (i, 0)), pl.BlockSpec((tm, tk), lambda i, k: (i, k)),
                  pl.BlockSpec((tk, d), lambda i, k: (k, 0)), pl.BlockSpec((tk, d), lambda i, k: (k, 0))],
        out_specs=[pl.BlockSpec((tm, tk), lambda i, k: (i, k)), pl.BlockSpec((tm, d), lambda i, k: (i, 0))],
        out_shape=[jax.ShapeDtypeStruct((s, f), BF16), jax.ShapeDtypeStruct((s, d), F32)],
        compiler_params=_params(("parallel", "arbitrary")),
    )(dyb, z, w_down, w_up_t)


def _mlp_bwd_dw(z, dz, dyb, hn2, tm, tk):
    s, d = dyb.shape
    f = z.shape[1]

    def body(z_ref, dz_ref, dy_ref, hn_ref, gd_ref, gu_ref):
        i = pl.program_id(1)

        @pl.when(i == 0)
        def _():
            gd_ref[...] = jnp.zeros_like(gd_ref)
            gu_ref[...] = jnp.zeros_like(gu_ref)

        zf = z_ref[...].astype(F32)
        gd_ref[...] += _dot((zf * zf).astype(BF16), dy_ref[...], TN)
        gu_ref[...] += _dot(dz_ref[...], hn_ref[...], TN)

    return pl.pallas_call(
        body, name="mlp_bwd_dw", grid=(f // tk, s // tm),
        in_specs=[pl.BlockSpec((tm, tk), lambda k, i: (i, k)), pl.BlockSpec((tm, tk), lambda k, i: (i, k)),
                  pl.BlockSpec((tm, d), lambda k, i: (i, 0)), pl.BlockSpec((tm, d), lambda k, i: (i, 0))],
        out_specs=[pl.BlockSpec((tk, d), lambda k, i: (k, 0)), pl.BlockSpec((tk, d), lambda k, i: (k, 0))],
        out_shape=[jax.ShapeDtypeStruct((f, d), F32), jax.ShapeDtypeStruct((f, d), F32)],
        compiler_params=_params(("parallel", "arbitrary")),
    )(z, dz, dyb, hn2)


def _mix_out_bwd(dhn2, x2, dx3, mix, ymix, w_out, mod6, g_mlp, tm):
    s, d = x2.shape

    def body(dh_ref, x2_ref, dx3_ref, mix_ref, y_ref, w_ref, mod_ref, g_ref, dx2_ref, dym_ref, gw_ref, st_ref):
        i = pl.program_id(0)

        @pl.when(i == 0)
        def _():
            st_ref[...] = jnp.zeros_like(st_ref)
            gw_ref[...] = jnp.zeros_like(gw_ref)

        dh = dh_ref[...]
        xhat, rstd = _rms(x2_ref[...])
        dn = dh * (1.0 + mod_ref[4:5, :])
        dx2 = dx3_ref[...] + _rms_bwd(dn * g_ref[...], xhat, rstd)
        dx2_ref[...] = dx2
        st_ref[0:1, :] += _colsum(dh)
        st_ref[1:2, :] += _colsum(dh * (xhat * g_ref[...]))
        st_ref[2:3, :] += _colsum(dn * xhat)
        st_ref[3:4, :] += _colsum(dx2 * mix_ref[...])
        dmix = (mod_ref[2:3, :] * dx2).astype(BF16)
        dym_ref[...] = _dot(dmix, w_ref[...], NT)
        gw_ref[...] += _dot(y_ref[...], dmix, TN)

    tile = pl.BlockSpec((tm, d), lambda i: (i, 0))
    return pl.pallas_call(
        body, name="mix_out_bwd", grid=(s // tm,),
        in_specs=[tile, tile, tile, tile, tile, _full(w_out.shape), _full(mod6.shape), _full(g_mlp.shape)],
        out_specs=[tile, tile, _full((d, d)), _full((SUBLANES, d))],
        out_shape=[jax.ShapeDtypeStruct((s, d), F32), jax.ShapeDtypeStruct((s, d), F32),
                   jax.ShapeDtypeStruct((d, d), F32), jax.ShapeDtypeStruct((SUBLANES, d), F32)],
        compiler_params=_params(("arbitrary",)),
    )(dhn2, x2, dx3, mix, ymix, w_out, mod6, g_mlp)


def _mixer_bwd(proj, dymix, h_all, conv_sc, conv_lru, conv_b, wa_bd, wx_bd, ba, bx, lam, width):
    s, din = proj.shape
    t = min(MIX_ROWS, s)
    nt = s // t
    nblk = width // LANES
    hb = t // SUBLANES
    last8 = s // SUBLANES - 1

    def body(proj_ref, projp_ref, projn_ref, dy_ref, dyn_ref, h_ref, hp_ref,
             wsc_ref, wlru_ref, blru_ref, wa_ref, wx_ref, ba_ref, bx_ref, lam_ref,
             dproj_ref, small_ref, gwa_ref, gwx_ref, an_ref, gn_ref, dun_ref):
        i = pl.program_id(0)

        @pl.when(i == 0)
        def _():
            small_ref[...] = jnp.zeros_like(small_ref)
            gwa_ref[...] = jnp.zeros_like(gwa_ref)
            gwx_ref[...] = jnp.zeros_like(gwx_ref)
            an_ref[...] = jnp.zeros_like(an_ref)
            gn_ref[...] = jnp.zeros_like(gn_ref)
            dun_ref[...] = jnp.zeros_like(dun_ref)

        has_prev = i < nt - 1
        has_next = i > 0
        for j in range(nblk):
            lo = j * LANES
            ls = slice(lo, lo + LANES)

            def col(p, ref=proj_ref):
                return ref[:, p * width + lo:p * width + lo + LANES]

            def prev(p):
                return jnp.where(has_prev, col(p, projp_ref), 0.0)

            def nxt(p):
                return jnp.where(has_next, col(p, projn_ref), 0.0)

            def add_row(r, v):
                small_ref[r:r + 1, ls] += _colsum(v)

            sc_b, sc_c, sc_x = col(0), col(1), col(2)
            p = sc_c * sc_x
            q, p1, p2 = _conv3(p, prev(1) * prev(2), wsc_ref, lo)
            dys = dy_ref[:, ls]
            dproj_ref[:, ls] = (dys * q).astype(BF16)
            dq = dys * sc_b
            dqn = jnp.where(has_next, dyn_ref[:, ls], 0.0) * nxt(0)
            dp = (wsc_ref[2:3, ls] * dq + wsc_ref[1:2, ls] * _shift_up(dq, 1, dqn)) + wsc_ref[0:1, ls] * _shift_up(dq, 2, dqn)
            dproj_ref[:, width + lo:width + lo + LANES] = (dp * sc_x).astype(BF16)
            dproj_ref[:, 2 * width + lo:2 * width + lo + LANES] = (dp * sc_c).astype(BF16)
            add_row(0, dq * p2)
            add_row(1, dq * p1)
            add_row(2, dq * p)

            xv = col(4)
            u, x1, x2, x3 = _conv4(xv, prev(4), wlru_ref, blru_ref, lo)
            lam_v = lam_ref[:, ls]
            sp = _softplus(-lam_v)
            wa, wx = wa_ref[j], wx_ref[j]
            ub, r, ig, a, mult = _lru_gates(u, wa, wx, ba_ref[:, ls], bx_ref[:, ls], sp)
            iu = ig * u
            h = h_ref[:, ls]
            hm1 = _shift_down(h, 1, jnp.where(has_prev, hp_ref[:, ls], 0.0))
            lyv = col(3)
            gel, th = _gelu(lyv)
            dyl = dy_ref[:, width + lo:width + lo + LANES]
            dproj_ref[:, 3 * width + lo:3 * width + lo + LANES] = (dyl * h * _dgelu(lyv, th)).astype(BF16)
            a_next = jnp.broadcast_to(an_ref[0:1, ls], (SUBLANES, LANES))
            g = _scan_rev(_shift_up(a, 1, a_next), dyl * gel, gn_ref[0:1, ls])
            an_ref[0:1, ls] = a[0:1, :]
            gn_ref[0:1, ls] = g[0:1, :]
            da = g * hm1
            dmult = g * iu
            diu = g * mult
            dlog_a = da * a - dmult * ((a * a) / mult)
            dpre_a = (dlog_a * (-RG_C * sp)) * (r * (1.0 - r))
            dpre_x = (diu * u) * (ig * (1.0 - ig))
            dab, dxb = dpre_a.astype(BF16), dpre_x.astype(BF16)
            du = diu * ig + _dot(dab, wa, NT) + _dot(dxb, wx, NT)
            gwa_ref[j] += _dot(ub, dab, TN)
            gwx_ref[j] += _dot(ub, dxb, TN)
            dun = dun_ref[:, ls]
            dun_ref[:, ls] = du[0:SUBLANES, :]
            dlx = (((wlru_ref[3:4, ls] * du + wlru_ref[2:3, ls] * _shift_up(du, 1, dun))
                    + wlru_ref[1:2, ls] * _shift_up(du, 2, dun)) + wlru_ref[0:1, ls] * _shift_up(du, 3, dun))
            dproj_ref[:, 4 * width + lo:4 * width + lo + LANES] = dlx.astype(BF16)
            add_row(3, du * x3)
            add_row(4, du * x2)
            add_row(5, du * x1)
            add_row(6, du * xv)
            add_row(7, du)
            add_row(8, dpre_a)
            add_row(9, dpre_x)
            add_row(10, (dlog_a * (RG_C * r)) * jax.nn.sigmoid(-lam_v))

    small = [conv_sc, conv_lru, conv_b, wa_bd, wx_bd, ba, bx, lam]
    rev = lambda i: nt - 1 - i
    return pl.pallas_call(
        body, name="mixer_bwd", grid=(nt,),
        in_specs=[pl.BlockSpec((t, din), lambda i: (rev(i), 0)),
                  pl.BlockSpec((SUBLANES, din), lambda i: (jnp.maximum(rev(i) * hb - 1, 0), 0)),
                  pl.BlockSpec((SUBLANES, din), lambda i: (jnp.minimum((rev(i) + 1) * hb, last8), 0)),
                  pl.BlockSpec((t, 2 * width), lambda i: (rev(i), 0)),
                  pl.BlockSpec((SUBLANES, 2 * width), lambda i: (jnp.minimum((rev(i) + 1) * hb, last8), 0)),
                  pl.BlockSpec((t, width), lambda i: (rev(i), 0)),
                  pl.BlockSpec((SUBLANES, width), lambda i: (jnp.maximum(rev(i) * hb - 1, 0), 0))]
        + [_full(a.shape) for a in small],
        out_specs=[pl.BlockSpec((t, din), lambda i: (rev(i), 0)), _full((2 * SUBLANES, width)),
                   _full(wa_bd.shape), _full(wx_bd.shape)],
        out_shape=[jax.ShapeDtypeStruct((s, din), BF16), jax.ShapeDtypeStruct((2 * SUBLANES, width), F32),
                   jax.ShapeDtypeStruct(wa_bd.shape, F32), jax.ShapeDtypeStruct(wx_bd.shape, F32)],
        scratch_shapes=[pltpu.VMEM((SUBLANES, width), F32), pltpu.VMEM((SUBLANES, width), F32),
                        pltpu.VMEM((SUBLANES, width), F32)],
        compiler_params=_params(("arbitrary",)),
    )(proj, proj, proj, dymix, dymix, h_all, h_all, *small)


def _mix_in_bwd_dx(dproj, x2d, dx2, w_in_t, mod6, g_mix, tm):
    s, d = x2d.shape
    din = dproj.shape[1]

    def body(dp_ref, x_ref, dx2_ref, w_ref, mod_ref, g_ref, gx_ref, st_ref):
        i = pl.program_id(0)

        @pl.when(i == 0)
        def _():
            st_ref[...] = jnp.zeros_like(st_ref)

        dh = _dot(dp_ref[...], w_ref[...], NN)
        xhat, rstd = _rms(x_ref[...])
        dn = dh * (1.0 + mod_ref[1:2, :])
        gx_ref[...] = dx2_ref[...] + _rms_bwd(dn * g_ref[...], xhat, rstd)
        st_ref[0:1, :] += _colsum(dh)
        st_ref[1:2, :] += _colsum(dh * (xhat * g_ref[...]))
        st_ref[2:3, :] += _colsum(dn * xhat)

    tile = pl.BlockSpec((tm, d), lambda i: (i, 0))
    return pl.pallas_call(
        body, name="mix_in_bwd_dx", grid=(s // tm,),
        in_specs=[pl.BlockSpec((tm, din), lambda i: (i, 0)), tile, tile, _full(w_in_t.shape), _full(mod6.shape),
                  _full(g_mix.shape)],
        out_specs=[tile, _full((SUBLANES, d))],
        out_shape=[jax.ShapeDtypeStruct((s, d), F32), jax.ShapeDtypeStruct((SUBLANES, d), F32)],
        compiler_params=_params(("arbitrary",)),
    )(dproj, x2d, dx2, w_in_t, mod6, g_mix)


def _mix_in_bwd_dw(dproj, hn1, tm, tn):
    s, d = hn1.shape
    din = dproj.shape[1]

    def body(dp_ref, hn_ref, gw_ref):
        i = pl.program_id(1)

        @pl.when(i == 0)
        def _():
            gw_ref[...] = jnp.zeros_like(gw_ref)

        gw_ref[...] += _dot(dp_ref[...], hn_ref[...], TN)

    return pl.pallas_call(
        body, name="mix_in_bwd_dw", grid=(din // tn, s // tm),
        in_specs=[pl.BlockSpec((tm, tn), lambda p, i: (i, p)), pl.BlockSpec((tm, d), lambda p, i: (i, 0))],
        out_specs=pl.BlockSpec((tn, d), lambda p, i: (p, 0)),
        out_shape=jax.ShapeDtypeStruct((din, d), F32),
        compiler_params=_params(("parallel", "arbitrary")),
    )(dproj, hn1)


def _adamw(w, g, m, v):
    m = ADAM_B1 * m + (1.0 - ADAM_B1) * g
    v = ADAM_B2 * v + (1.0 - ADAM_B2) * (g * g)
    m_hat = m / (1.0 - ADAM_B1 ** ADAM_STEP)
    v_hat = v / (1.0 - ADAM_B2 ** ADAM_STEP)
    delta = -ADAM_LR * (m_hat / (jnp.sqrt(v_hat) + ADAM_EPS) + ADAM_WD * w)
    return delta, m, v


def _pair_sum(g4, h4, core_chip, tr, name):
    _, _, r, n = g4.shape

    def body(sc_ref, g_ref, h_ref, sb_ref, own_ref):
        q = pl.program_id(1)
        ssum = g_ref[...] + h_ref[...]
        sb_ref[...] = ssum.astype(BF16)

        @pl.when(q == sc_ref[1])
        def _():
            own_ref[...] = ssum

    grid_spec = pltpu.PrefetchScalarGridSpec(
        num_scalar_prefetch=1, grid=(r // tr, 4),
        in_specs=[pl.BlockSpec((None, None, tr, n), lambda i, q, sc: (q, sc[0], i, 0)),
                  pl.BlockSpec((None, tr, n), lambda i, q, sc: (q, i, 0))],
        out_specs=[pl.BlockSpec((None, tr, n), lambda i, q, sc: (q, i, 0)),
                   pl.BlockSpec((tr, n), lambda i, q, sc: (i, 0))])
    return pl.pallas_call(
        body, name=name, grid_spec=grid_spec,
        out_shape=[jax.ShapeDtypeStruct((4, r, n), BF16), jax.ShapeDtypeStruct((r, n), F32)],
        compiler_params=_params(("parallel", "arbitrary")),
    )(core_chip, g4, h4)


def _sum4(own, parts, tr, name):
    r, n = own.shape

    def body(o_ref, p_ref, out_ref):
        acc = o_ref[...]
        for k in range(3):
            acc = acc + p_ref[k].astype(F32)
        out_ref[...] = acc

    return pl.pallas_call(
        body, name=name, grid=(r // tr,),
        in_specs=[pl.BlockSpec((tr, n), lambda i: (i, 0)), pl.BlockSpec((3, tr, n), lambda i: (0, i, 0))],
        out_specs=pl.BlockSpec((tr, n), lambda i: (i, 0)),
        out_shape=jax.ShapeDtypeStruct((r, n), F32),
        compiler_params=_params(("parallel",)),
    )(own, parts)


def _sum8(parts, tr, name):
    _, rows, n = parts.shape

    def body(p_ref, o_ref):
        acc = p_ref[0]
        for k in range(1, N_DEV):
            acc = acc + p_ref[k]
        o_ref[...] = acc

    return pl.pallas_call(
        body, name=name, grid=(rows // tr,),
        in_specs=[pl.BlockSpec((N_DEV, tr, n), lambda i: (0, i, 0))],
        out_specs=pl.BlockSpec((tr, n), lambda i: (i, 0)),
        out_shape=jax.ShapeDtypeStruct((rows, n), F32),
        compiler_params=_params(("parallel",)),
    )(parts)


def _adam_rows(w, g, m, v, tr, name):
    rows, n = w.shape

    def body(w_ref, g_ref, m_ref, v_ref, d_ref, nm_ref, nv_ref):
        d_ref[...], nm_ref[...], nv_ref[...] = _adamw(w_ref[...], g_ref[...], m_ref[...], v_ref[...])

    tile = pl.BlockSpec((tr, n), lambda i: (i, 0))
    return pl.pallas_call(
        body, name=name, grid=(rows // tr,),
        in_specs=[tile] * 4, out_specs=[tile] * 3,
        out_shape=[jax.ShapeDtypeStruct((rows, n), F32)] * 3,
        compiler_params=_params(("parallel",)),
    )(w, g, m, v)


def _ada_bwd_adam(cact_t, dmod_cols, w, m, v, tr):
    rows, n = w.shape

    def body(c_ref, d_ref, w_ref, m_ref, v_ref, g_ref, dl_ref, nm_ref, nv_ref):
        def term(b):
            return c_ref[b].astype(BF16).astype(F32) * d_ref[b:b + 1, :].astype(BF16).astype(F32)

        g = term(0)
        for b in range(1, N_DEV):
            g = g + term(b)
        g_ref[...] = g
        dl_ref[...], nm_ref[...], nv_ref[...] = _adamw(w_ref[...], g, m_ref[...], v_ref[...])

    tile = pl.BlockSpec((tr, n), lambda i: (i, 0))
    return pl.pallas_call(
        body, name="ada_bwd_adam", grid=(rows // tr,),
        in_specs=[pl.BlockSpec((N_DEV, tr, 1), lambda i: (0, i, 0)), _full(dmod_cols.shape), tile, tile, tile],
        out_specs=[tile] * 4,
        out_shape=[jax.ShapeDtypeStruct((rows, n), F32)] * 4,
        compiler_params=_params(("parallel",)),
    )(cact_t, dmod_cols, w, m, v)


def _adam_small(ws, gs, ms, vs):
    n = len(ws)

    def body(*refs):
        w_r, g_r, m_r, v_r = refs[:n], refs[n:2 * n], refs[2 * n:3 * n], refs[3 * n:4 * n]
        d_r, nm_r, nv_r = refs[4 * n:5 * n], refs[5 * n:6 * n], refs[6 * n:7 * n]
        for k in range(n):
            d_r[k][...], nm_r[k][...], nv_r[k][...] = _adamw(w_r[k][...], g_r[k][...], m_r[k][...], v_r[k][...])

    shapes = [jax.ShapeDtypeStruct(w.shape, F32) for w in ws]
    outs = pl.pallas_call(
        body, name="adam_small", out_shape=shapes * 3, compiler_params=_params(),
    )(*ws, *gs, *ms, *vs)
    return outs[:n], outs[n:2 * n], outs[2 * n:]


def _block_diag(w):
    h, hd, _ = w.shape
    per = LANES // hd
    eye = jnp.eye(per, dtype=w.dtype)
    w5 = w.reshape(h // per, per, hd, 1, hd) * eye[None, :, None, :, None]
    return w5.reshape(h // per, LANES, LANES)


def _block_diag_grad(g, h, hd):
    per = LANES // hd
    g5 = g.reshape(h // per, per, hd, per, hd)
    return jnp.stack([g5[:, a, :, a, :] for a in range(per)], axis=1).reshape(h, hd, hd)


def kernel(x, c, w_ada, b_ada, g_mix, w_in, conv_w_sc, conv_w_lru, conv_b_lru, w_rg_a, b_rg_a, w_rg_x, b_rg_x, lru_lambda, w_out, g_mlp, w_up, w_down, g_final, loss_target, m_w_ada, m_b_ada, m_g_mix, m_w_in, m_conv_w_sc, m_conv_w_lru, m_conv_b_lru, m_w_rg_a, m_b_rg_a, m_w_rg_x, m_b_rg_x, m_lru_lambda, m_w_out, m_g_mlp, m_w_up, m_w_down, m_g_final, v_w_ada, v_b_ada, v_g_mix, v_w_in, v_conv_w_sc, v_conv_w_lru, v_conv_b_lru, v_w_rg_a, v_b_rg_a, v_w_rg_x, v_b_rg_x, v_lru_lambda, v_w_out, v_g_mlp, v_w_up, v_w_down, v_g_final):
    s, d = x.shape[1], x.shape[2]
    width = conv_b_lru.shape[1]
    heads, hd = w_rg_a.shape[1], w_rg_a.shape[2]
    f = w_down.shape[1] * N_DEV
    n_ada = w_ada.shape[2]
    csh = conv_w_sc.shape[2]
    me = 4 * lax.axis_index("x") + 2 * lax.axis_index("y") + lax.axis_index("c")
    tm = min(512, s)
    tm_mlp = min(1024, s)
    tk = 512

    x2d = x[0]
    tgt = loss_target[0]

    pay = jnp.zeros((SUBLANES, d), F32)
    pay = pay.at[0:1, :].set(c)
    pay = pay.at[1:4, 0:csh].set(conv_w_sc[0])
    pay = pay.at[4:8, 0:csh].set(conv_w_lru[0])
    w_in_t_sh = w_in[0].T.astype(BF16)
    w_up_t_sh = w_up[0].T.astype(BF16)
    w_out_sh = w_out[0].astype(BF16)
    w_down_sh = w_down[0].astype(BF16)
    pay_all, w_in_t, w_out_b, w_up_t, w_down_b = _gather2(
        "gather_weights", [pay, w_in_t_sh, w_out_sh, w_up_t_sh, w_down_sh])
    w_in_t = w_in_t.reshape(-1, d)
    w_up_t = w_up_t.reshape(-1, d)
    w_out_b = w_out_b.reshape(-1, d)
    w_down_b = w_down_b.reshape(-1, d)
    c_all = pay_all[:, 0, :]
    conv_sc = pay_all[:, 1:4, 0:csh].transpose(1, 0, 2).reshape(3, width)
    conv_lru = pay_all[:, 4:8, 0:csh].transpose(1, 0, 2).reshape(4, width)

    b_ada_sh = lax.dynamic_slice(b_ada, (0, me * n_ada), (1, n_ada))
    mod_cols, c_act = _ada_fwd(c_all, w_ada[0], b_ada_sh)
    (mod_rows,) = _exchange("scatter_mod", [], [mod_cols.reshape(N_DEV, 1, n_ada)])
    mod6 = jnp.zeros((SUBLANES, d), F32).at[0:6, :].set(mod_rows.reshape(6, d))

    wa_bd = _block_diag(w_rg_a[0]).astype(BF16)
    wx_bd = _block_diag(w_rg_x[0]).astype(BF16)
    ba = b_rg_a.reshape(1, width)
    bx = b_rg_x.reshape(1, width)
    g_fin = g_final.reshape(1, d)

    hn1, proj = _mix_in_fwd(x2d, mod6, g_mix, w_in_t, tm)
    ymix, h_all = _mixer_fwd(proj, conv_sc, conv_lru, conv_b_lru, wa_bd, wx_bd, ba, bx, lru_lambda, width)
    mix, x2, hn2 = _mix_out_fwd(ymix, x2d, w_out_b, mod6, g_mlp, tm)
    z, y = _mlp_fwd(hn2, w_up_t, w_down_b, tm_mlp, tk)
    dx3, dyb, st_fin = _final(x2, y, tgt, mod6, g_fin, tm)
    loss = lax.psum((0.5 / d) * jnp.sum(st_fin[2]), ("x", "y", "c"))

    dz, dhn2 = _mlp_bwd_dx(dyb, z, w_down_b, w_up_t, tm_mlp, tk)
    g_down, g_up_t = _mlp_bwd_dw(z, dz, dyb, hn2, tm_mlp, tk)
    dx2, dymix, g_out, st_out = _mix_out_bwd(dhn2, x2, dx3, mix, ymix, w_out_b, mod6, g_mlp, tm)
    dproj, g_small, g_wa, g_wx = _mixer_bwd(proj, dymix, h_all, conv_sc, conv_lru, conv_b_lru, wa_bd, wx_bd,
                                            ba, bx, lru_lambda, width)
    grad_x, st_in = _mix_in_bwd_dx(dproj, x2d, dx2, w_in_t, mod6, g_mix, tm)
    g_in_t = _mix_in_bwd_dw(dproj, hn1, tm, 512)

    zrow = jnp.zeros((1, d), F32)
    small = jnp.concatenate([
        st_in[0:2], st_out[3:4], st_out[0:2], st_fin[1:2],
        st_in[2:3], st_out[2:3], st_fin[0:1],
        jnp.concatenate([g_small[7:8], g_small[10:11]], axis=1),
        jnp.concatenate([g_small[8:9], g_small[9:10]], axis=1),
        jnp.concatenate([jnp.concatenate([g_small[0:3], jnp.zeros((1, width), F32)], axis=0), g_small[3:7]], axis=1),
        zrow,
        _block_diag_grad(g_wa, heads, hd).reshape(-1, d),
        _block_diag_grad(g_wx, heads, hd).reshape(-1, d),
    ], axis=0)

    core_chip = jnp.stack([lax.axis_index("c"), 2 * lax.axis_index("x") + lax.axis_index("y")]).astype(jnp.int32)
    g4 = [g.reshape(4, 2, -1, d) for g in (g_in_t, g_up_t, g_out, g_down)]
    h4 = _pair_swap("swap_grads", g4)
    names = ("w_in", "w_up", "w_out", "w_down")
    tiles = (g4[0].shape[2], 256, g4[2].shape[2], 256)
    sums = [_pair_sum(g, h, core_chip, tr, "pair_sum_" + nm) for g, h, tr, nm in zip(g4, h4, tiles, names)]
    parts = _chip_exchange("exchange_grads", [sb for sb, _ in sums])
    (small_all,) = _gather2("gather_small_grads", [small])
    gs_in, gs_up, gs_out, gs_down = [_sum4(own, p, tr, "sum_" + nm)
                                     for (_, own), p, tr, nm in zip(sums, parts, tiles, names)]
    gs_in = gs_in.T
    gs_up = gs_up.T
    ad_in = _adam_rows(w_in[0], gs_in, m_w_in[0], v_w_in[0], 256, "adam_w_in")
    ad_up = _adam_rows(w_up[0], gs_up, m_w_up[0], v_w_up[0], 256, "adam_w_up")
    ad_out = _adam_rows(w_out[0], gs_out, m_w_out[0], v_w_out[0], w_out.shape[1], "adam_w_out")
    ad_down = _adam_rows(w_down[0], gs_down, m_w_down[0], v_w_down[0], 256, "adam_w_down")

    gsum = _sum8(small_all, SMALL_ROWS, "sum_small")
    dmod_cols = lax.dynamic_slice(small_all[:, 0:6, :].reshape(N_DEV, 6 * d), (0, me * n_ada), (N_DEV, n_ada))
    g_ada, d_ada, nm_ada, nv_ada = _ada_bwd_adam(c_act[:, :, None], dmod_cols, w_ada[0], m_w_ada[0], v_w_ada[0], 256)

    g_conv = lax.dynamic_slice(gsum[11:15, 0:width], (0, me * csh), (4, csh))
    g_conv_l = lax.dynamic_slice(gsum[11:15, width:2 * width], (0, me * csh), (4, csh))
    small_g = [
        gsum[0:6].reshape(1, 6 * d),
        gsum[6:7],
        g_conv[0:3].reshape(1, 3, csh),
        g_conv_l.reshape(1, 4, csh),
        gsum[9:10, 0:width],
        gsum[16:48].reshape(1, heads, hd, hd),
        gsum[10:11, 0:width].reshape(1, heads, hd),
        gsum[48:80].reshape(1, heads, hd, hd),
        gsum[10:11, width:].reshape(1, heads, hd),
        gsum[9:10, width:],
        gsum[7:8],
        gsum[8],
    ]
    small_w = [b_ada, g_mix, conv_w_sc, conv_w_lru, conv_b_lru, w_rg_a, b_rg_a, w_rg_x, b_rg_x, lru_lambda, g_mlp, g_final]
    small_m = [m_b_ada, m_g_mix, m_conv_w_sc, m_conv_w_lru, m_conv_b_lru, m_w_rg_a, m_b_rg_a, m_w_rg_x, m_b_rg_x,
               m_lru_lambda, m_g_mlp, m_g_final]
    small_v = [v_b_ada, v_g_mix, v_conv_w_sc, v_conv_w_lru, v_conv_b_lru, v_w_rg_a, v_b_rg_a, v_w_rg_x, v_b_rg_x,
               v_lru_lambda, v_g_mlp, v_g_final]
    sd, snm, snv = _adam_small(small_w, small_g, small_m, small_v)

    def order(ada, w_in_, w_out_, w_up_, w_down_, sm):
        return [ada[None], sm[0], sm[1], w_in_[None], sm[2], sm[3], sm[4], sm[5], sm[6], sm[7], sm[8], sm[9],
                w_out_[None], sm[10], w_up_[None], w_down_[None], sm[11]]

    grads = order(g_ada, gs_in, gs_out, gs_up, gs_down, small_g)
    deltas = order(d_ada, ad_in[0], ad_out[0], ad_up[0], ad_down[0], sd)
    new_m = order(nm_ada, ad_in[1], ad_out[1], ad_up[1], ad_down[1], snm)
    new_v = order(nv_ada, ad_in[2], ad_out[2], ad_up[2], ad_down[2], snv)
    return (loss, grad_x[None], *grads, *deltas, *new_m, *new_v)
```

```python
import functools

import jax
import jax.numpy as jnp
from jax import lax
from jax.experimental import pallas as pl
from jax.experimental.pallas import tpu as pltpu

F32 = jnp.float32
BF16 = jnp.bfloat16
N_DEV = 8
EPS = 1e-6
RG_C = 8.0
GELU_K0 = 0.7978845608028654
GELU_K1 = 0.044715
ADAM_LR = 0.001
ADAM_B1 = 0.9
ADAM_B2 = 0.999
ADAM_EPS = 1e-08
ADAM_WD = 0.01
ADAM_STEP = 10
LANES = 128
SUBLANES = 8
VMEM_LIMIT = 52 * 1024 * 1024
MIX_ROWS = 256
SMALL_ROWS = 80

MESH = pl.DeviceIdType.MESH
ANY = pl.BlockSpec(memory_space=pl.ANY)
NN = ((1,), (0,))
NT = ((1,), (1,))
TN = ((0,), (0,))


def _dot(a, b, dims):
    return lax.dot_general(a, b, (dims, ((), ())), preferred_element_type=F32)


def _params(sem=None):
    return pltpu.CompilerParams(dimension_semantics=sem, vmem_limit_bytes=VMEM_LIMIT)


def _full(shape):
    nd = len(shape)
    return pl.BlockSpec(shape, lambda *_: (0,) * nd)


def _exchange(name, gathers, scatters):
    n_g = len(gathers)
    arrs = list(gathers) + list(scatters)
    n = len(arrs)
    out_shape = [jax.ShapeDtypeStruct((N_DEV,) + a.shape, a.dtype) for a in gathers]
    out_shape += [jax.ShapeDtypeStruct(a.shape, a.dtype) for a in scatters]

    def body(*refs):
        ins, outs = refs[:n], refs[n:2 * n]
        send_sems, recv_sems, local_sems = refs[2 * n:]
        x, y, c = lax.axis_index("x"), lax.axis_index("y"), lax.axis_index("c")
        me = 4 * x + 2 * y + c

        def src(a, dev):
            return ins[a] if a < n_g else ins[a].at[dev]

        def peer_of(k):
            px = 1 - x if (k >> 2) & 1 else x
            py = 1 - y if (k >> 1) & 1 else y
            pc = 1 - c if k & 1 else c
            return (px, py, pc), 4 * px + 2 * py + pc

        local = [pltpu.make_async_copy(src(a, me), outs[a].at[me], local_sems.at[a]) for a in range(n)]
        for cp in local:
            cp.start()
        sends = []
        for k in range(1, N_DEV):
            peer, pidx = peer_of(k)
            for a in range(n):
                cp = pltpu.make_async_remote_copy(
                    src_ref=src(a, pidx), dst_ref=outs[a].at[me],
                    send_sem=send_sems.at[a * (N_DEV - 1) + k - 1], recv_sem=recv_sems.at[a * (N_DEV - 1) + k - 1],
                    device_id=peer, device_id_type=MESH)
                cp.start()
                sends.append(cp)
        for k in range(1, N_DEV):
            peer, pidx = peer_of(k)
            for a in range(n):
                pltpu.make_async_remote_copy(
                    src_ref=src(a, pidx), dst_ref=outs[a].at[pidx],
                    send_sem=send_sems.at[a * (N_DEV - 1) + k - 1], recv_sem=recv_sems.at[a * (N_DEV - 1) + k - 1],
                    device_id=peer, device_id_type=MESH).wait_recv()
        for cp in sends:
            cp.wait_send()
        for cp in local:
            cp.wait()

    return pl.pallas_call(
        body, name=name, out_shape=out_shape,
        in_specs=[ANY] * n, out_specs=[ANY] * n,
        scratch_shapes=[pltpu.SemaphoreType.DMA((n * (N_DEV - 1),)),
                        pltpu.SemaphoreType.DMA((n * (N_DEV - 1),)),
                        pltpu.SemaphoreType.DMA((n,))],
    )(*arrs)


def _gather2(name, arrs):
    n = len(arrs)
    per = 7
    out_shape = [jax.ShapeDtypeStruct((N_DEV,) + a.shape, a.dtype) for a in arrs]

    def body(*refs):
        ins, outs = refs[:n], refs[n:2 * n]
        send_sems, recv_sems, local_sems = refs[2 * n:]
        x, y, c = lax.axis_index("x"), lax.axis_index("y"), lax.axis_index("c")
        sib = (x, y, 1 - c)
        chips = [(1 - x, y), (x, 1 - y), (1 - x, 1 - y)]

        def slot(a, px, py, pc):
            return outs[a].at[4 * px + 2 * py + pc]

        def copy(a, k, block, to, src=None):
            return pltpu.make_async_remote_copy(
                src_ref=slot(a, *block) if src is None else src, dst_ref=slot(a, *block),
                send_sem=send_sems.at[a * per + k], recv_sem=recv_sems.at[a * per + k],
                device_id=to, device_id_type=MESH)

        local = [pltpu.make_async_copy(ins[a], slot(a, x, y, c), local_sems.at[a]) for a in range(n)]
        for cp in local:
            cp.start()
        first = []
        for a in range(n):
            first += [copy(a, 1 + j, (x, y, c), (*chip, c), src=ins[a]) for j, chip in enumerate(chips)]
        for a in range(n):
            first.append(copy(a, 0, (x, y, c), sib, src=ins[a]))
        for cp in first:
            cp.start()
        passed = []
        for a in range(n):
            for j, chip in enumerate(chips):
                copy(a, 1 + j, (*chip, c), (x, y, c)).wait_recv()
                cp = copy(a, 4 + j, (*chip, c), sib)
                cp.start()
                passed.append(cp)
        for a in range(n):
            copy(a, 0, sib, (x, y, c)).wait_recv()
            for j, chip in enumerate(chips):
                copy(a, 4 + j, (*chip, 1 - c), (x, y, c)).wait_recv()
        for cp in first + passed:
            cp.wait_send()
        for cp in local:
            cp.wait()

    return pl.pallas_call(
        body, name=name, out_shape=out_shape,
        in_specs=[ANY] * n, out_specs=[ANY] * n,
        scratch_shapes=[pltpu.SemaphoreType.DMA((n * per,)), pltpu.SemaphoreType.DMA((n * per,)),
                        pltpu.SemaphoreType.DMA((n,))],
    )(*arrs)


def _pair_swap(name, arrs):
    n = len(arrs)
    out_shape = [jax.ShapeDtypeStruct((4,) + a.shape[2:], a.dtype) for a in arrs]

    def body(*refs):
        ins, outs = refs[:n], refs[n:2 * n]
        send_sems, recv_sems = refs[2 * n:]
        x, y, c = lax.axis_index("x"), lax.axis_index("y"), lax.axis_index("c")

        def copy(a, q):
            return pltpu.make_async_remote_copy(
                src_ref=ins[a].at[q, 1 - c], dst_ref=outs[a].at[q],
                send_sem=send_sems.at[a * 4 + q], recv_sem=recv_sems.at[a * 4 + q],
                device_id=(x, y, 1 - c), device_id_type=MESH)

        cps = [copy(a, q) for a in range(n) for q in range(4)]
        for cp in cps:
            cp.start()
        for cp in cps:
            cp.wait_recv()
        for cp in cps:
            cp.wait_send()

    return pl.pallas_call(
        body, name=name, out_shape=out_shape,
        in_specs=[ANY] * n, out_specs=[ANY] * n,
        scratch_shapes=[pltpu.SemaphoreType.DMA((n * 4,)), pltpu.SemaphoreType.DMA((n * 4,))],
    )(*arrs)


def _chip_exchange(name, arrs):
    n = len(arrs)
    out_shape = [jax.ShapeDtypeStruct((3,) + a.shape[1:], a.dtype) for a in arrs]

    def body(*refs):
        ins, outs = refs[:n], refs[n:2 * n]
        send_sems, recv_sems = refs[2 * n:]
        x, y, c = lax.axis_index("x"), lax.axis_index("y"), lax.axis_index("c")

        def copy(a, k):
            px = 1 - x if (k >> 1) & 1 else x
            py = 1 - y if k & 1 else y
            return pltpu.make_async_remote_copy(
                src_ref=ins[a].at[2 * px + py], dst_ref=outs[a].at[k - 1],
                send_sem=send_sems.at[a * 3 + k - 1], recv_sem=recv_sems.at[a * 3 + k - 1],
                device_id=(px, py, c), device_id_type=MESH)

        cps = [copy(a, k) for a in range(n) for k in (1, 2, 3)]
        for cp in cps:
            cp.start()
        for cp in cps:
            cp.wait_recv()
        for cp in cps:
            cp.wait_send()

    return pl.pallas_call(
        body, name=name, out_shape=out_shape,
        in_specs=[ANY] * n, out_specs=[ANY] * n,
        scratch_shapes=[pltpu.SemaphoreType.DMA((n * 3,)), pltpu.SemaphoreType.DMA((n * 3,))],
    )(*arrs)


class _Rider:
    def __init__(self, arrays, out_shapes, n_sems, build, aliases=None):
        self.arrays, self.out_shapes, self.n_sems, self.build = list(arrays), list(out_shapes), n_sems, build
        self.aliases = dict(aliases or {})


def _merge_riders(r1, r2):
    n1i, n1o, n1s = len(r1.arrays), len(r1.out_shapes), r1.n_sems

    def build(ins, outs, send_sems, recv_sems):
        a = r1.build(ins[:n1i], outs[:n1o], send_sems.at[pl.ds(0, n1s)], recv_sems.at[pl.ds(0, n1s)])
        b = r2.build(ins[n1i:], outs[n1o:], send_sems.at[pl.ds(n1s, r2.n_sems)], recv_sems.at[pl.ds(n1s, r2.n_sems)])
        return tuple(p + q for p, q in zip(a, b))

    aliases = dict(r1.aliases)
    aliases.update({k + n1i: v + n1o for k, v in r2.aliases.items()})
    return _Rider(r1.arrays + r2.arrays, r1.out_shapes + r2.out_shapes, n1s + r2.n_sems, build, aliases)


def _place():
    x, y, c = lax.axis_index("x"), lax.axis_index("y"), lax.axis_index("c")
    chips = [(1 - x, y), (x, 1 - y), (1 - x, 1 - y)]
    return x, y, c, chips


def _ride_gather_ici(arrs):
    n = len(arrs)

    def build(ins, outs, send_sems, recv_sems):
        x, y, c, chips = _place()
        peers = [(*chip, c) for chip in chips] + [(x, y, 1 - c)]
        me = 4 * x + 2 * y + c
        local = [pltpu.make_async_copy(ins[a], outs[a].at[me], send_sems.at[a * 5 + 4]) for a in range(n)]
        sends, recvs = [], []
        for a in range(n):
            for j, (px, py, pc) in enumerate(peers):
                sends.append(pltpu.make_async_remote_copy(
                    src_ref=ins[a], dst_ref=outs[a].at[me], send_sem=send_sems.at[a * 5 + j],
                    recv_sem=recv_sems.at[a * 5 + j], device_id=(px, py, pc), device_id_type=MESH))
                recvs.append(pltpu.make_async_remote_copy(
                    src_ref=ins[a], dst_ref=outs[a].at[4 * px + 2 * py + pc], send_sem=send_sems.at[a * 5 + j],
                    recv_sem=recv_sems.at[a * 5 + j], device_id=(px, py, pc), device_id_type=MESH))
        return local, sends, recvs

    shapes = [jax.ShapeDtypeStruct((N_DEV,) + a.shape, a.dtype) for a in arrs]
    return _Rider(arrs, shapes, n * 5, build)


def _ride_gather_d2d(gathered):
    n = len(gathered)

    def build(ins, outs, send_sems, recv_sems):
        x, y, c, chips = _place()
        sends, recvs = [], []
        for a in range(n):
            for j, (px, py) in enumerate(chips):
                mine = outs[a].at[4 * px + 2 * py + c]
                theirs = outs[a].at[4 * px + 2 * py + 1 - c]
                sends.append(pltpu.make_async_remote_copy(
                    src_ref=mine, dst_ref=mine, send_sem=send_sems.at[a * 3 + j], recv_sem=recv_sems.at[a * 3 + j],
                    device_id=(x, y, 1 - c), device_id_type=MESH))
                recvs.append(pltpu.make_async_remote_copy(
                    src_ref=mine, dst_ref=theirs, send_sem=send_sems.at[a * 3 + j], recv_sem=recv_sems.at[a * 3 + j],
                    device_id=(x, y, 1 - c), device_id_type=MESH))
        return [], sends, recvs

    shapes = [jax.ShapeDtypeStruct(a.shape, a.dtype) for a in gathered]
    return _Rider(gathered, shapes, n * 3, build, aliases={a: a for a in range(n)})


def _ride_pair_swap(arrs):
    n = len(arrs)

    def build(ins, outs, send_sems, recv_sems):
        x, y, c, _ = _place()
        cps = [pltpu.make_async_remote_copy(
            src_ref=ins[a].at[q, 1 - c], dst_ref=outs[a].at[q], send_sem=send_sems.at[a * 4 + q],
            recv_sem=recv_sems.at[a * 4 + q], device_id=(x, y, 1 - c), device_id_type=MESH)
            for a in range(n) for q in range(4)]
        return [], cps, cps

    shapes = [jax.ShapeDtypeStruct((4,) + a.shape[2:], a.dtype) for a in arrs]
    return _Rider(arrs, shapes, n * 4, build)


def _ride_chip_exchange(arrs):
    n = len(arrs)

    def build(ins, outs, send_sems, recv_sems):
        x, y, c, _ = _place()
        cps = []
        for a in range(n):
            for k in (1, 2, 3):
                px = 1 - x if (k >> 1) & 1 else x
                py = 1 - y if k & 1 else y
                cps.append(pltpu.make_async_remote_copy(
                    src_ref=ins[a].at[2 * px + py], dst_ref=outs[a].at[k - 1], send_sem=send_sems.at[a * 3 + k - 1],
                    recv_sem=recv_sems.at[a * 3 + k - 1], device_id=(px, py, c), device_id_type=MESH))
        return [], cps, cps

    shapes = [jax.ShapeDtypeStruct((3,) + a.shape[1:], a.dtype) for a in arrs]
    return _Rider(arrs, shapes, n * 3, build)


def _call(body, name, grid, in_specs, out_specs, out_shape, args, scratch=(), rider=None):
    n_in, n_out, n_scr = len(in_specs), len(out_specs), len(scratch)
    sem = ("arbitrary",) * len(grid)
    if rider is None:
        outs = pl.pallas_call(
            body, name=name, grid=grid, in_specs=in_specs, out_specs=out_specs, out_shape=out_shape,
            scratch_shapes=list(scratch), compiler_params=_params(sem))(*args)
        return outs, []
    ri, ro = len(rider.arrays), len(rider.out_shapes)

    def riding(*refs):
        ins, r_ins = refs[:n_in], refs[n_in:n_in + ri]
        outs = refs[n_in + ri:n_in + ri + n_out]
        r_outs = refs[n_in + ri + n_out:n_in + ri + n_out + ro]
        scr = refs[n_in + ri + n_out + ro:n_in + ri + n_out + ro + n_scr]
        send_sems, recv_sems = refs[-2:]
        first = functools.reduce(jnp.logical_and, [pl.program_id(k) == 0 for k in range(len(grid))])
        last = functools.reduce(jnp.logical_and, [pl.program_id(k) == grid[k] - 1 for k in range(len(grid))])

        @pl.when(first)
        def _():
            local, sends, _ = rider.build(r_ins, r_outs, send_sems, recv_sems)
            for cp in local + sends:
                cp.start()

        body(*ins, *outs, *scr)

        @pl.when(last)
        def _():
            local, sends, recvs = rider.build(r_ins, r_outs, send_sems, recv_sems)
            for cp in recvs:
                cp.wait_recv()
            for cp in sends:
                cp.wait_send()
            for cp in local:
                cp.wait()

    outs = pl.pallas_call(
        riding, name=name, grid=grid,
        in_specs=list(in_specs) + [ANY] * ri, out_specs=list(out_specs) + [ANY] * ro,
        out_shape=list(out_shape) + rider.out_shapes,
        scratch_shapes=list(scratch) + [pltpu.SemaphoreType.DMA((rider.n_sems,)), pltpu.SemaphoreType.DMA((rider.n_sems,))],
        input_output_aliases={n_in + k: n_out + v for k, v in rider.aliases.items()},
        compiler_params=_params(sem))(*args, *rider.arrays)
    return outs[:n_out], outs[n_out:]


def _comm(name, rider):
    def body(dummy_ref, out_ref):
        out_ref[...] = dummy_ref[...]

    dummy = jnp.zeros((SUBLANES, LANES), F32)
    spec = pl.BlockSpec((SUBLANES, LANES), lambda i: (0, 0))
    _, r_outs = _call(body, name, (1,), [spec], [spec], [jax.ShapeDtypeStruct(dummy.shape, F32)], [dummy], rider=rider)
    return r_outs


def _ada_fwd(c_all, w_ada_sh, b_ada_sh):
    nb, d = c_all.shape
    ncol = w_ada_sh.shape[1]

    def body(c_ref, w_ref, b_ref, mod_ref, cact_ref):
        cc = c_ref[...]
        ca = cc * jax.nn.sigmoid(cc)
        cact_ref[...] = ca
        mod_ref[...] = _dot(ca.astype(BF16), w_ref[...].astype(BF16), NN) + b_ref[...]

    return pl.pallas_call(
        body, name="ada_fwd",
        out_shape=[jax.ShapeDtypeStruct((nb, ncol), F32), jax.ShapeDtypeStruct((nb, d), F32)],
        compiler_params=_params(),
    )(c_all, w_ada_sh, b_ada_sh)


def _rms(xv):
    rstd = lax.rsqrt(jnp.mean(xv * xv, axis=-1, keepdims=True) + EPS)
    return xv * rstd, rstd


def _rms_bwd(dxhat, xhat, rstd):
    return rstd * (dxhat - xhat * jnp.mean(dxhat * xhat, axis=-1, keepdims=True))


def _colsum(v):
    return jnp.sum(v, axis=0, keepdims=True)


def _expm1(v):
    series = v * (1.0 + v * (0.5 + v * (1.0 / 6.0 + v * (1.0 / 24.0 + v * (1.0 / 120.0 + v * (1.0 / 720.0))))))
    return jnp.where(jnp.abs(v) < 0.3, series, jnp.exp(v) - 1.0)


def _softplus(v):
    return jnp.maximum(v, 0.0) + jnp.log1p(jnp.exp(-jnp.abs(v)))


def _gelu(v):
    t = jnp.tanh(GELU_K0 * (v + GELU_K1 * v * v * v))
    return 0.5 * v * (1.0 + t), t


def _dgelu(v, t):
    return 0.5 * (1.0 + t) + 0.5 * v * (1.0 - t * t) * GELU_K0 * (1.0 + 3.0 * GELU_K1 * v * v)


def _shift_down(v, k, prev8):
    r = pltpu.roll(v, k, 0)
    pr = pltpu.roll(prev8, k, 0)
    row8 = lax.broadcasted_iota(jnp.int32, prev8.shape, 0)
    top = jnp.where(row8 < k, pr, r[0:SUBLANES])
    return jnp.concatenate([top, r[SUBLANES:]], axis=0)


def _shift_up(v, k, next8):
    t = v.shape[0]
    r = pltpu.roll(v, t - k, 0)
    nr = pltpu.roll(next8, SUBLANES - k, 0)
    row8 = lax.broadcasted_iota(jnp.int32, next8.shape, 0)
    bot = jnp.where(row8 >= SUBLANES - k, nr, r[t - SUBLANES:t])
    return jnp.concatenate([r[:t - SUBLANES], bot], axis=0)


def _scan_fwd(a, b, h0):
    t = a.shape[0]
    row = lax.broadcasted_iota(jnp.int32, a.shape, 0)
    s = 1
    while s < t:
        a_sh = pltpu.roll(a, s, 0)
        b_sh = pltpu.roll(b, s, 0)
        m = row >= s
        b = jnp.where(m, a * b_sh + b, b)
        a = jnp.where(m, a * a_sh, a)
        s *= 2
    return b + a * h0


def _scan_rev(m, b, g_next):
    t = m.shape[0]
    row = lax.broadcasted_iota(jnp.int32, m.shape, 0)
    s = 1
    while s < t:
        m_sh = pltpu.roll(m, t - s, 0)
        b_sh = pltpu.roll(b, t - s, 0)
        msk = row < t - s
        b = jnp.where(msk, m * b_sh + b, b)
        m = jnp.where(msk, m * m_sh, m)
        s *= 2
    return b + m * g_next


def _lru_gates(u, wa, wx, ba, bx, sp):
    ub = u.astype(BF16)
    r = jax.nn.sigmoid(_dot(ub, wa, NN) + ba)
    i = jax.nn.sigmoid(_dot(ub, wx, NN) + bx)
    log_a = (-RG_C * r) * sp
    a = jnp.exp(log_a)
    mult = jnp.sqrt(-_expm1(2.0 * log_a))
    return ub, r, i, a, mult


def _conv3(p, pp, w_ref, lo):
    p1 = _shift_down(p, 1, pp)
    p2 = _shift_down(p, 2, pp)
    q = (w_ref[0:1, lo:lo + LANES] * p2 + w_ref[1:2, lo:lo + LANES] * p1) + w_ref[2:3, lo:lo + LANES] * p
    return q, p1, p2


def _conv4(xv, xp, w_ref, b_ref, lo):
    x1 = _shift_down(xv, 1, xp)
    x2 = _shift_down(xv, 2, xp)
    x3 = _shift_down(xv, 3, xp)
    u = (((w_ref[0:1, lo:lo + LANES] * x3 + w_ref[1:2, lo:lo + LANES] * x2) + w_ref[2:3, lo:lo + LANES] * x1)
         + w_ref[3:4, lo:lo + LANES] * xv) + b_ref[:, lo:lo + LANES]
    return u, x1, x2, x3


def _mix_in_fwd(x2d, mod6, g_mix, w_in_t, tm, rider=None):
    s, d = x2d.shape
    din = w_in_t.shape[0]

    def body(x_ref, mod_ref, g_ref, w_ref, hn_ref, proj_ref):
        xhat, _ = _rms(x_ref[...])
        hn = ((xhat * g_ref[...]) * (1.0 + mod_ref[1:2, :]) + mod_ref[0:1, :]).astype(BF16)
        hn_ref[...] = hn
        proj_ref[...] = _dot(hn, w_ref[...], NT)

    return _call(
        body, "mix_in_fwd", (s // tm,),
        [pl.BlockSpec((tm, d), lambda i: (i, 0)), _full(mod6.shape), _full(g_mix.shape), _full(w_in_t.shape)],
        [pl.BlockSpec((tm, d), lambda i: (i, 0)), pl.BlockSpec((tm, din), lambda i: (i, 0))],
        [jax.ShapeDtypeStruct((s, d), BF16), jax.ShapeDtypeStruct((s, din), F32)],
        [x2d, mod6, g_mix, w_in_t], rider=rider)


def _mixer_fwd(proj, conv_sc, conv_lru, conv_b, wa_bd, wx_bd, ba, bx, lam, width, rider=None):
    s, din = proj.shape
    t = min(MIX_ROWS, s)
    nblk = width // LANES
    hb = t // SUBLANES

    def body(proj_ref, projp_ref, wsc_ref, wlru_ref, blru_ref, wa_ref, wx_ref, ba_ref, bx_ref, lam_ref,
             ymix_ref, h_ref, hc_ref):
        i = pl.program_id(0)

        @pl.when(i == 0)
        def _():
            hc_ref[...] = jnp.zeros_like(hc_ref)

        has_prev = i > 0
        for j in range(nblk):
            lo = j * LANES

            def col(p, ref=proj_ref):
                return ref[:, p * width + lo:p * width + lo + LANES]

            def prev(p):
                return jnp.where(has_prev, col(p, projp_ref), 0.0)

            p = col(1) * col(2)
            q, _, _ = _conv3(p, prev(1) * prev(2), wsc_ref, lo)
            ymix_ref[:, lo:lo + LANES] = (col(0) * q).astype(BF16)

            u, _, _, _ = _conv4(col(4), prev(4), wlru_ref, blru_ref, lo)
            sp = _softplus(-lam_ref[:, lo:lo + LANES])
            _, r, ig, a, mult = _lru_gates(u, wa_ref[j], wx_ref[j], ba_ref[:, lo:lo + LANES], bx_ref[:, lo:lo + LANES], sp)
            h = _scan_fwd(a, mult * (ig * u), hc_ref[0:1, lo:lo + LANES])
            h_ref[:, lo:lo + LANES] = h
            hc_ref[0:1, lo:lo + LANES] = h[t - 1:t, :]
            gel, _ = _gelu(col(3))
            ymix_ref[:, width + lo:width + lo + LANES] = (gel * h).astype(BF16)

    small = [conv_sc, conv_lru, conv_b, wa_bd, wx_bd, ba, bx, lam]
    return _call(
        body, "mixer_fwd", (s // t,),
        [pl.BlockSpec((t, din), lambda i: (i, 0)),
         pl.BlockSpec((SUBLANES, din), lambda i: (jnp.maximum(i * hb - 1, 0), 0))]
        + [_full(a.shape) for a in small],
        [pl.BlockSpec((t, 2 * width), lambda i: (i, 0)), pl.BlockSpec((t, width), lambda i: (i, 0))],
        [jax.ShapeDtypeStruct((s, 2 * width), BF16), jax.ShapeDtypeStruct((s, width), F32)],
        [proj, proj, *small], scratch=[pltpu.VMEM((SUBLANES, width), F32)], rider=rider)


def _mix_out_fwd(ymix, x2d, w_out, mod6, g_mlp, tm, rider=None):
    s, d = x2d.shape

    def body(y_ref, x_ref, w_ref, mod_ref, g_ref, mix_ref, x2_ref, hn_ref):
        mix = _dot(y_ref[...], w_ref[...], NN)
        mix_ref[...] = mix
        x2 = x_ref[...] + mod_ref[2:3, :] * mix
        x2_ref[...] = x2
        xhat, _ = _rms(x2)
        hn_ref[...] = ((xhat * g_ref[...]) * (1.0 + mod_ref[4:5, :]) + mod_ref[3:4, :]).astype(BF16)

    tile = pl.BlockSpec((tm, d), lambda i: (i, 0))
    return _call(
        body, "mix_out_fwd", (s // tm,),
        [tile, tile, _full(w_out.shape), _full(mod6.shape), _full(g_mlp.shape)],
        [tile, tile, tile],
        [jax.ShapeDtypeStruct((s, d), F32), jax.ShapeDtypeStruct((s, d), F32), jax.ShapeDtypeStruct((s, d), BF16)],
        [ymix, x2d, w_out, mod6, g_mlp], rider=rider)


def _mlp_fwd(hn2, w_up_t, w_down, tm, tk):
    s, d = hn2.shape
    f = w_up_t.shape[0]

    def body(hn_ref, wu_ref, wd_ref, z_ref, y_ref):
        k = pl.program_id(1)
        z = jnp.maximum(_dot(hn_ref[...], wu_ref[...], NT), 0.0)
        z_ref[...] = z.astype(BF16)
        part = _dot((z * z).astype(BF16), wd_ref[...], NN)

        @pl.when(k == 0)
        def _():
            y_ref[...] = part

        @pl.when(k > 0)
        def _():
            y_ref[...] += part

    return pl.pallas_call(
        body, name="mlp_fwd", grid=(s // tm, f // tk),
        in_specs=[pl.BlockSpec((tm, d), lambda i, k: (i, 0)), pl.BlockSpec((tk, d), lambda i, k: (k, 0)),
                  pl.BlockSpec((tk, d), lambda i, k: (k, 0))],
        out_specs=[pl.BlockSpec((tm, tk), lambda i, k: (i, k)), pl.BlockSpec((tm, d), lambda i, k: (i, 0))],
        out_shape=[jax.ShapeDtypeStruct((s, f), BF16), jax.ShapeDtypeStruct((s, d), F32)],
        compiler_params=_params(("parallel", "arbitrary")),
    )(hn2, w_up_t, w_down)


def _final(x2, y, target, mod6, g_final, tm):
    s, d = x2.shape

    def body(x2_ref, y_ref, t_ref, mod_ref, g_ref, dx3_ref, dyb_ref, st_ref):
        i = pl.program_id(0)

        @pl.when(i == 0)
        def _():
            st_ref[...] = jnp.zeros_like(st_ref)

        gate = mod_ref[5:6, :]
        yv = y_ref[...]
        xhat, rstd = _rms(x2_ref[...] + gate * yv)
        diff = xhat * g_ref[...] - t_ref[...]
        dyo = diff * (1.0 / d)
        dx3 = _rms_bwd(dyo * g_ref[...], xhat, rstd)
        dx3_ref[...] = dx3
        dyb_ref[...] = (gate * dx3).astype(BF16)
        st_ref[0:1, :] += _colsum(dyo * xhat)
        st_ref[1:2, :] += _colsum(dx3 * yv)
        st_ref[2:3, :] += _colsum(diff * diff)

    tile = pl.BlockSpec((tm, d), lambda i: (i, 0))
    return pl.pallas_call(
        body, name="final_loss", grid=(s // tm,),
        in_specs=[tile, tile, tile, _full(mod6.shape), _full(g_final.shape)],
        out_specs=[tile, tile, _full((SUBLANES, d))],
        out_shape=[jax.ShapeDtypeStruct((s, d), F32), jax.ShapeDtypeStruct((s, d), BF16),
                   jax.ShapeDtypeStruct((SUBLANES, d), F32)],
        compiler_params=_params(("arbitrary",)),
    )(x2, y, target, mod6, g_final)


def _mlp_bwd_dx(dyb, z, w_down, w_up_t, tm, tk):
    s, d = dyb.shape
    f = z.shape[1]

    def body(dy_ref, z_ref, wd_ref, wu_ref, dz_ref, dh_ref):
        k = pl.program_id(1)
        dz = ((2.0 * z_ref[...].astype(F32)) * _dot(dy_ref[...], wd_ref[...], NT)).astype(BF16)
        dz_ref[...] = dz
        part = _dot(dz, wu_ref[...], NN)

        @pl.when(k == 0)
        def _():
            dh_ref[...] = part

        @pl.when(k > 0)
        def _():
            dh_ref[...] += part

    return pl.pallas_call(
        body, name="mlp_bwd_dx", grid=(s // tm, f // tk),
        in_specs=[pl.BlockSpec((tm, d), lambda i, k: (i, 0)), pl.BlockSpec((tm, tk), lambda i, k: (i, k)),
                  pl.BlockSpec((tk, d), lambda i, k: (k, 0)), pl.BlockSpec((tk, d), lambda i, k: (k, 0))],
        out_specs=[pl.BlockSpec((tm, tk), lambda i, k: (i, k)), pl.BlockSpec((tm, d), lambda i, k: (i, 0))],
        out_shape=[jax.ShapeDtypeStruct((s, f), BF16), jax.ShapeDtypeStruct((s, d), F32)],
        compiler_params=_params(("parallel", "arbitrary")),
    )(dyb, z, w_down, w_up_t)


def _mlp_bwd_dw(z, dz, dyb, hn2, tm, tk):
    s, d = dyb.shape
    f = z.shape[1]

    def body(z_ref, dz_ref, dy_ref, hn_ref, gd_ref, gu_ref):
        i = pl.program_id(1)

        @pl.when(i == 0)
        def _():
            gd_ref[...] = jnp.zeros_like(gd_ref)
            gu_ref[...] = jnp.zeros_like(gu_ref)

        zf = z_ref[...].astype(F32)
        gd_ref[...] += _dot((zf * zf).astype(BF16), dy_ref[...], TN)
        gu_ref[...] += _dot(dz_ref[...], hn_ref[...], TN)

    return pl.pallas_call(
        body, name="mlp_bwd_dw", grid=(f // tk, s // tm),
        in_specs=[pl.BlockSpec((tm, tk), lambda k, i: (i, k)), pl.BlockSpec((tm, tk), lambda k, i: (i, k)),
                  pl.BlockSpec((tm, d), lambda k, i: (i, 0)), pl.BlockSpec((tm, d), lambda k, i: (i, 0))],
        out_specs=[pl.BlockSpec((tk, d), lambda k, i: (k, 0)), pl.BlockSpec((tk, d), lambda k, i: (k, 0))],
        out_shape=[jax.ShapeDtypeStruct((f, d), F32), jax.ShapeDtypeStruct((f, d), F32)],
        compiler_params=_params(("parallel", "arbitrary")),
    )(z, dz, dyb, hn2)


def _mix_out_bwd(dhn2, x2, dx3, mix, ymix, w_out, mod6, g_mlp, tm, rider=None):
    s, d = x2.shape

    def body(dh_ref, x2_ref, dx3_ref, mix_ref, y_ref, w_ref, mod_ref, g_ref, dx2_ref, dym_ref, gw_ref, st_ref):
        i = pl.program_id(0)

        @pl.when(i == 0)
        def _():
            st_ref[...] = jnp.zeros_like(st_ref)
            gw_ref[...] = jnp.zeros_like(gw_ref)

        dh = dh_ref[...]
        xhat, rstd = _rms(x2_ref[...])
        dn = dh * (1.0 + mod_ref[4:5, :])
        dx2 = dx3_ref[...] + _rms_bwd(dn * g_ref[...], xhat, rstd)
        dx2_ref[...] = dx2
        st_ref[0:1, :] += _colsum(dh)
        st_ref[1:2, :] += _colsum(dh * (xhat * g_ref[...]))
        st_ref[2:3, :] += _colsum(dn * xhat)
        st_ref[3:4, :] += _colsum(dx2 * mix_ref[...])
        dmix = (mod_ref[2:3, :] * dx2).astype(BF16)
        dym_ref[...] = _dot(dmix, w_ref[...], NT)
        gw_ref[...] += _dot(y_ref[...], dmix, TN)

    tile = pl.BlockSpec((tm, d), lambda i: (i, 0))
    return _call(
        body, "mix_out_bwd", (s // tm,),
        [tile, tile, tile, tile, tile, _full(w_out.shape), _full(mod6.shape), _full(g_mlp.shape)],
        [tile, tile, _full((d, d)), _full((SUBLANES, d))],
        [jax.ShapeDtypeStruct((s, d), F32), jax.ShapeDtypeStruct((s, d), F32),
         jax.ShapeDtypeStruct((d, d), F32), jax.ShapeDtypeStruct((SUBLANES, d), F32)],
        [dhn2, x2, dx3, mix, ymix, w_out, mod6, g_mlp], rider=rider)


def _mixer_bwd(proj, dymix, h_all, conv_sc, conv_lru, conv_b, wa_bd, wx_bd, ba, bx, lam, width, rider=None):
    s, din = proj.shape
    t = min(MIX_ROWS, s)
    nt = s // t
    nblk = width // LANES
    hb = t // SUBLANES
    last8 = s // SUBLANES - 1

    def body(proj_ref, projp_ref, projn_ref, dy_ref, dyn_ref, h_ref, hp_ref,
             wsc_ref, wlru_ref, blru_ref, wa_ref, wx_ref, ba_ref, bx_ref, lam_ref,
             dproj_ref, small_ref, gwa_ref, gwx_ref, an_ref, gn_ref, dun_ref):
        i = pl.program_id(0)

        @pl.when(i == 0)
        def _():
            small_ref[...] = jnp.zeros_like(small_ref)
            gwa_ref[...] = jnp.zeros_like(gwa_ref)
            gwx_ref[...] = jnp.zeros_like(gwx_ref)
            an_ref[...] = jnp.zeros_like(an_ref)
            gn_ref[...] = jnp.zeros_like(gn_ref)
            dun_ref[...] = jnp.zeros_like(dun_ref)

        has_prev = i < nt - 1
        has_next = i > 0
        for j in range(nblk):
            lo = j * LANES
            ls = slice(lo, lo + LANES)

            def col(p, ref=proj_ref):
                return ref[:, p * width + lo:p * width + lo + LANES]

            def prev(p):
                return jnp.where(has_prev, col(p, projp_ref), 0.0)

            def nxt(p):
                return jnp.where(has_next, col(p, projn_ref), 0.0)

            def add_row(r, v):
                small_ref[r:r + 1, ls] += _colsum(v)

            sc_b, sc_c, sc_x = col(0), col(1), col(2)
            p = sc_c * sc_x
            q, p1, p2 = _conv3(p, prev(1) * prev(2), wsc_ref, lo)
            dys = dy_ref[:, ls]
            dproj_ref[:, ls] = (dys * q).astype(BF16)
            dq = dys * sc_b
            dqn = jnp.where(has_next, dyn_ref[:, ls], 0.0) * nxt(0)
            dp = (wsc_ref[2:3, ls] * dq + wsc_ref[1:2, ls] * _shift_up(dq, 1, dqn)) + wsc_ref[0:1, ls] * _shift_up(dq, 2, dqn)
            dproj_ref[:, width + lo:width + lo + LANES] = (dp * sc_x).astype(BF16)
            dproj_ref[:, 2 * width + lo:2 * width + lo + LANES] = (dp * sc_c).astype(BF16)
            add_row(0, dq * p2)
            add_row(1, dq * p1)
            add_row(2, dq * p)

            xv = col(4)
            u, x1, x2, x3 = _conv4(xv, prev(4), wlru_ref, blru_ref, lo)
            lam_v = lam_ref[:, ls]
            sp = _softplus(-lam_v)
            wa, wx = wa_ref[j], wx_ref[j]
            ub, r, ig, a, mult = _lru_gates(u, wa, wx, ba_ref[:, ls], bx_ref[:, ls], sp)
            iu = ig * u
            h = h_ref[:, ls]
            hm1 = _shift_down(h, 1, jnp.where(has_prev, hp_ref[:, ls], 0.0))
            lyv = col(3)
            gel, th = _gelu(lyv)
            dyl = dy_ref[:, width + lo:width + lo + LANES]
            dproj_ref[:, 3 * width + lo:3 * width + lo + LANES] = (dyl * h * _dgelu(lyv, th)).astype(BF16)
            a_next = jnp.broadcast_to(an_ref[0:1, ls], (SUBLANES, LANES))
            g = _scan_rev(_shift_up(a, 1, a_next), dyl * gel, gn_ref[0:1, ls])
            an_ref[0:1, ls] = a[0:1, :]
            gn_ref[0:1, ls] = g[0:1, :]
            da = g * hm1
            dmult = g * iu
            diu = g * mult
            dlog_a = da * a - dmult * ((a * a) / mult)
            dpre_a = (dlog_a * (-RG_C * sp)) * (r * (1.0 - r))
            dpre_x = (diu * u) * (ig * (1.0 - ig))
            dab, dxb = dpre_a.astype(BF16), dpre_x.astype(BF16)
            du = diu * ig + _dot(dab, wa, NT) + _dot(dxb, wx, NT)
            gwa_ref[j] += _dot(ub, dab, TN)
            gwx_ref[j] += _dot(ub, dxb, TN)
            dun = dun_ref[:, ls]
            dun_ref[:, ls] = du[0:SUBLANES, :]
            dlx = (((wlru_ref[3:4, ls] * du + wlru_ref[2:3, ls] * _shift_up(du, 1, dun))
                    + wlru_ref[1:2, ls] * _shift_up(du, 2, dun)) + wlru_ref[0:1, ls] * _shift_up(du, 3, dun))
            dproj_ref[:, 4 * width + lo:4 * width + lo + LANES] = dlx.astype(BF16)
            add_row(3, du * x3)
            add_row(4, du * x2)
            add_row(5, du * x1)
            add_row(6, du * xv)
            add_row(7, du)
            add_row(8, dpre_a)
            add_row(9, dpre_x)
            add_row(10, (dlog_a * (RG_C * r)) * jax.nn.sigmoid(-lam_v))

    small = [conv_sc, conv_lru, conv_b, wa_bd, wx_bd, ba, bx, lam]
    rev = lambda i: nt - 1 - i
    return _call(
        body, "mixer_bwd", (nt,),
        [pl.BlockSpec((t, din), lambda i: (rev(i), 0)),
         pl.BlockSpec((SUBLANES, din), lambda i: (jnp.maximum(rev(i) * hb - 1, 0), 0)),
         pl.BlockSpec((SUBLANES, din), lambda i: (jnp.minimum((rev(i) + 1) * hb, last8), 0)),
         pl.BlockSpec((t, 2 * width), lambda i: (rev(i), 0)),
         pl.BlockSpec((SUBLANES, 2 * width), lambda i: (jnp.minimum((rev(i) + 1) * hb, last8), 0)),
         pl.BlockSpec((t, width), lambda i: (rev(i), 0)),
         pl.BlockSpec((SUBLANES, width), lambda i: (jnp.maximum(rev(i) * hb - 1, 0), 0))]
        + [_full(a.shape) for a in small],
        [pl.BlockSpec((t, din), lambda i: (rev(i), 0)), _full((2 * SUBLANES, width)),
         _full(wa_bd.shape), _full(wx_bd.shape)],
        [jax.ShapeDtypeStruct((s, din), BF16), jax.ShapeDtypeStruct((2 * SUBLANES, width), F32),
         jax.ShapeDtypeStruct(wa_bd.shape, F32), jax.ShapeDtypeStruct(wx_bd.shape, F32)],
        [proj, proj, proj, dymix, dymix, h_all, h_all, *small],
        scratch=[pltpu.VMEM((SUBLANES, width), F32), pltpu.VMEM((SUBLANES, width), F32),
                 pltpu.VMEM((SUBLANES, width), F32)], rider=rider)


def _mix_in_bwd_dx(dproj, x2d, dx2, w_in_t, mod6, g_mix, tm, rider=None):
    s, d = x2d.shape
    din = dproj.shape[1]

    def body(dp_ref, x_ref, dx2_ref, w_ref, mod_ref, g_ref, gx_ref, st_ref):
        i = pl.program_id(0)

        @pl.when(i == 0)
        def _():
            st_ref[...] = jnp.zeros_like(st_ref)

        dh = _dot(dp_ref[...], w_ref[...], NN)
        xhat, rstd = _rms(x_ref[...])
        dn = dh * (1.0 + mod_ref[1:2, :])
        gx_ref[...] = dx2_ref[...] + _rms_bwd(dn * g_ref[...], xhat, rstd)
        st_ref[0:1, :] += _colsum(dh)
        st_ref[1:2, :] += _colsum(dh * (xhat * g_ref[...]))
        st_ref[2:3, :] += _colsum(dn * xhat)

    tile = pl.BlockSpec((tm, d), lambda i: (i, 0))
    return _call(
        body, "mix_in_bwd_dx", (s // tm,),
        [pl.BlockSpec((tm, din), lambda i: (i, 0)), tile, tile, _full(w_in_t.shape), _full(mod6.shape),
         _full(g_mix.shape)],
        [tile, _full((SUBLANES, d))],
        [jax.ShapeDtypeStruct((s, d), F32), jax.ShapeDtypeStruct((SUBLANES, d), F32)],
        [dproj, x2d, dx2, w_in_t, mod6, g_mix], rider=rider)


def _mix_in_bwd_dw(dproj, hn1, tm, tn, rider=None):
    s, d = hn1.shape
    din = dproj.shape[1]

    def body(dp_ref, hn_ref, gw_ref):
        i = pl.program_id(1)

        @pl.when(i == 0)
        def _():
            gw_ref[...] = jnp.zeros_like(gw_ref)

        gw_ref[...] += _dot(dp_ref[...], hn_ref[...], TN)

    return _call(
        body, "mix_in_bwd_dw", (din // tn, s // tm),
        [pl.BlockSpec((tm, tn), lambda p, i: (i, p)), pl.BlockSpec((tm, d), lambda p, i: (i, 0))],
        [pl.BlockSpec((tn, d), lambda p, i: (p, 0))],
        [jax.ShapeDtypeStruct((din, d), F32)],
        [dproj, hn1], rider=rider)


def _adamw(w, g, m, v):
    m = ADAM_B1 * m + (1.0 - ADAM_B1) * g
    v = ADAM_B2 * v + (1.0 - ADAM_B2) * (g * g)
    m_hat = m / (1.0 - ADAM_B1 ** ADAM_STEP)
    v_hat = v / (1.0 - ADAM_B2 ** ADAM_STEP)
    delta = -ADAM_LR * (m_hat / (jnp.sqrt(v_hat) + ADAM_EPS) + ADAM_WD * w)
    return delta, m, v


def _pair_sum(g4, h4, core_chip, tr, name):
    _, _, r, n = g4.shape

    def body(sc_ref, g_ref, h_ref, sb_ref, own_ref):
        q = pl.program_id(1)
        ssum = g_ref[...] + h_ref[...]
        sb_ref[...] = ssum.astype(BF16)

        @pl.when(q == sc_ref[1])
        def _():
            own_ref[...] = ssum

    grid_spec = pltpu.PrefetchScalarGridSpec(
        num_scalar_prefetch=1, grid=(r // tr, 4),
        in_specs=[pl.BlockSpec((None, None, tr, n), lambda i, q, sc: (q, sc[0], i, 0)),
                  pl.BlockSpec((None, tr, n), lambda i, q, sc: (q, i, 0))],
        out_specs=[pl.BlockSpec((None, tr, n), lambda i, q, sc: (q, i, 0)),
                   pl.BlockSpec((tr, n), lambda i, q, sc: (i, 0))])
    return pl.pallas_call(
        body, name=name, grid_spec=grid_spec,
        out_shape=[jax.ShapeDtypeStruct((4, r, n), BF16), jax.ShapeDtypeStruct((r, n), F32)],
        compiler_params=_params(("parallel", "arbitrary")),
    )(core_chip, g4, h4)


def _sum4(own, parts, tr, name):
    r, n = own.shape

    def body(o_ref, p_ref, out_ref):
        acc = o_ref[...]
        for k in range(3):
            acc = acc + p_ref[k].astype(F32)
        out_ref[...] = acc

    return pl.pallas_call(
        body, name=name, grid=(r // tr,),
        in_specs=[pl.BlockSpec((tr, n), lambda i: (i, 0)), pl.BlockSpec((3, tr, n), lambda i: (0, i, 0))],
        out_specs=pl.BlockSpec((tr, n), lambda i: (i, 0)),
        out_shape=jax.ShapeDtypeStruct((r, n), F32),
        compiler_params=_params(("parallel",)),
    )(own, parts)


def _sum8(parts, tr, name):
    _, rows, n = parts.shape

    def body(p_ref, o_ref):
        acc = p_ref[0]
        for k in range(1, N_DEV):
            acc = acc + p_ref[k]
        o_ref[...] = acc

    return pl.pallas_call(
        body, name=name, grid=(rows // tr,),
        in_specs=[pl.BlockSpec((N_DEV, tr, n), lambda i: (0, i, 0))],
        out_specs=pl.BlockSpec((tr, n), lambda i: (i, 0)),
        out_shape=jax.ShapeDtypeStruct((rows, n), F32),
        compiler_params=_params(("parallel",)),
    )(parts)


def _adam_rows(w, g, m, v, tr, name):
    rows, n = w.shape

    def body(w_ref, g_ref, m_ref, v_ref, d_ref, nm_ref, nv_ref):
        d_ref[...], nm_ref[...], nv_ref[...] = _adamw(w_ref[...], g_ref[...], m_ref[...], v_ref[...])

    tile = pl.BlockSpec((tr, n), lambda i: (i, 0))
    return pl.pallas_call(
        body, name=name, grid=(rows // tr,),
        in_specs=[tile] * 4, out_specs=[tile] * 3,
        out_shape=[jax.ShapeDtypeStruct((rows, n), F32)] * 3,
        compiler_params=_params(("parallel",)),
    )(w, g, m, v)


def _ada_bwd_adam(cact_t, dmod_cols, w, m, v, tr):
    rows, n = w.shape

    def body(c_ref, d_ref, w_ref, m_ref, v_ref, g_ref, dl_ref, nm_ref, nv_ref):
        def term(b):
            return c_ref[b].astype(BF16).astype(F32) * d_ref[b:b + 1, :].astype(BF16).astype(F32)

        g = term(0)
        for b in range(1, N_DEV):
            g = g + term(b)
        g_ref[...] = g
        dl_ref[...], nm_ref[...], nv_ref[...] = _adamw(w_ref[...], g, m_ref[...], v_ref[...])

    tile = pl.BlockSpec((tr, n), lambda i: (i, 0))
    return pl.pallas_call(
        body, name="ada_bwd_adam", grid=(rows // tr,),
        in_specs=[pl.BlockSpec((N_DEV, tr, 1), lambda i: (0, i, 0)), _full(dmod_cols.shape), tile, tile, tile],
        out_specs=[tile] * 4,
        out_shape=[jax.ShapeDtypeStruct((rows, n), F32)] * 4,
        compiler_params=_params(("parallel",)),
    )(cact_t, dmod_cols, w, m, v)


def _adam_small(ws, gs, ms, vs):
    n = len(ws)

    def body(*refs):
        w_r, g_r, m_r, v_r = refs[:n], refs[n:2 * n], refs[2 * n:3 * n], refs[3 * n:4 * n]
        d_r, nm_r, nv_r = refs[4 * n:5 * n], refs[5 * n:6 * n], refs[6 * n:7 * n]
        for k in range(n):
            d_r[k][...], nm_r[k][...], nv_r[k][...] = _adamw(w_r[k][...], g_r[k][...], m_r[k][...], v_r[k][...])

    shapes = [jax.ShapeDtypeStruct(w.shape, F32) for w in ws]
    outs = pl.pallas_call(
        body, name="adam_small", out_shape=shapes * 3, compiler_params=_params(),
    )(*ws, *gs, *ms, *vs)
    return outs[:n], outs[n:2 * n], outs[2 * n:]


def _block_diag(w):
    h, hd, _ = w.shape
    per = LANES // hd
    eye = jnp.eye(per, dtype=w.dtype)
    w5 = w.reshape(h // per, per, hd, 1, hd) * eye[None, :, None, :, None]
    return w5.reshape(h // per, LANES, LANES)


def _block_diag_grad(g, h, hd):
    per = LANES // hd
    g5 = g.reshape(h // per, per, hd, per, hd)
    return jnp.stack([g5[:, a, :, a, :] for a in range(per)], axis=1).reshape(h, hd, hd)


def kernel(x, c, w_ada, b_ada, g_mix, w_in, conv_w_sc, conv_w_lru, conv_b_lru, w_rg_a, b_rg_a, w_rg_x, b_rg_x, lru_lambda, w_out, g_mlp, w_up, w_down, g_final, loss_target, m_w_ada, m_b_ada, m_g_mix, m_w_in, m_conv_w_sc, m_conv_w_lru, m_conv_b_lru, m_w_rg_a, m_b_rg_a, m_w_rg_x, m_b_rg_x, m_lru_lambda, m_w_out, m_g_mlp, m_w_up, m_w_down, m_g_final, v_w_ada, v_b_ada, v_g_mix, v_w_in, v_conv_w_sc, v_conv_w_lru, v_conv_b_lru, v_w_rg_a, v_b_rg_a, v_w_rg_x, v_b_rg_x, v_lru_lambda, v_w_out, v_g_mlp, v_w_up, v_w_down, v_g_final):
    s, d = x.shape[1], x.shape[2]
    width = conv_b_lru.shape[1]
    heads, hd = w_rg_a.shape[1], w_rg_a.shape[2]
    f = w_down.shape[1] * N_DEV
    n_ada = w_ada.shape[2]
    csh = conv_w_sc.shape[2]
    me = 4 * lax.axis_index("x") + 2 * lax.axis_index("y") + lax.axis_index("c")
    tm = min(512, s)
    tm_mlp = min(1024, s)
    tk = 512

    x2d = x[0]
    tgt = loss_target[0]

    pay = jnp.zeros((SUBLANES, d), F32)
    pay = pay.at[0:1, :].set(c)
    pay = pay.at[1:4, 0:csh].set(conv_w_sc[0])
    pay = pay.at[4:8, 0:csh].set(conv_w_lru[0])
    w_in_t_sh = w_in[0].T.astype(BF16)
    w_up_t_sh = w_up[0].T.astype(BF16)
    w_out_sh = w_out[0].astype(BF16)
    w_down_sh = w_down[0].astype(BF16)
    pay_all, w_in_t = _gather2("gather_in", [pay, w_in_t_sh])
    w_in_t = w_in_t.reshape(-1, d)
    c_all = pay_all[:, 0, :]
    conv_sc = pay_all[:, 1:4, 0:csh].transpose(1, 0, 2).reshape(3, width)
    conv_lru = pay_all[:, 4:8, 0:csh].transpose(1, 0, 2).reshape(4, width)

    b_ada_sh = lax.dynamic_slice(b_ada, (0, me * n_ada), (1, n_ada))
    mod_cols, c_act = _ada_fwd(c_all, w_ada[0], b_ada_sh)
    (mod_rows,) = _exchange("scatter_mod", [], [mod_cols.reshape(N_DEV, 1, n_ada)])
    mod6 = jnp.zeros((SUBLANES, d), F32).at[0:6, :].set(mod_rows.reshape(6, d))

    wa_bd = _block_diag(w_rg_a[0]).astype(BF16)
    wx_bd = _block_diag(w_rg_x[0]).astype(BF16)
    ba = b_rg_a.reshape(1, width)
    bx = b_rg_x.reshape(1, width)
    g_fin = g_final.reshape(1, d)

    (hn1, proj), (w_out_g, w_up_g) = _mix_in_fwd(
        x2d, mod6, g_mix, w_in_t, tm, rider=_ride_gather_ici([w_out_sh, w_up_t_sh]))
    (ymix, h_all), (w_out_g, w_up_g, w_down_g) = _mixer_fwd(
        proj, conv_sc, conv_lru, conv_b_lru, wa_bd, wx_bd, ba, bx, lru_lambda, width,
        rider=_merge_riders(_ride_gather_d2d([w_out_g, w_up_g]), _ride_gather_ici([w_down_sh])))
    w_out_b = w_out_g.reshape(-1, d)
    (mix, x2, hn2), (w_down_g,) = _mix_out_fwd(ymix, x2d, w_out_b, mod6, g_mlp, tm,
                                               rider=_ride_gather_d2d([w_down_g]))
    w_up_t = w_up_g.reshape(-1, d)
    w_down_b = w_down_g.reshape(-1, d)
    z, y = _mlp_fwd(hn2, w_up_t, w_down_b, tm_mlp, tk)
    dx3, dyb, st_fin = _final(x2, y, tgt, mod6, g_fin, tm)
    loss = lax.psum((0.5 / d) * jnp.sum(st_fin[2]), ("x", "y", "c"))

    core_chip = jnp.stack([lax.axis_index("c"), 2 * lax.axis_index("x") + lax.axis_index("y")]).astype(jnp.int32)
    dz, dhn2 = _mlp_bwd_dx(dyb, z, w_down_b, w_up_t, tm_mlp, tk)
    g_down, g_up_t = _mlp_bwd_dw(z, dz, dyb, hn2, tm_mlp, tk)
    g_up4, g_down4 = g_up_t.reshape(4, 2, -1, d), g_down.reshape(4, 2, -1, d)
    (dx2, dymix, g_out, st_out), (h_up, h_down) = _mix_out_bwd(
        dhn2, x2, dx3, mix, ymix, w_out_b, mod6, g_mlp, tm, rider=_ride_pair_swap([g_up4, g_down4]))
    sb_up, own_up = _pair_sum(g_up4, h_up, core_chip, 256, "pair_sum_w_up")
    sb_down, own_down = _pair_sum(g_down4, h_down, core_chip, 256, "pair_sum_w_down")
    g_out4 = g_out.reshape(4, 2, -1, d)
    (dproj, g_small, g_wa, g_wx), (p_up, p_down, h_out) = _mixer_bwd(
        proj, dymix, h_all, conv_sc, conv_lru, conv_b_lru, wa_bd, wx_bd, ba, bx, lru_lambda, width,
        rider=_merge_riders(_ride_chip_exchange([sb_up, sb_down]), _ride_pair_swap([g_out4])))
    sb_out, own_out = _pair_sum(g_out4, h_out, core_chip, g_out4.shape[2], "pair_sum_w_out")
    (g_in_t,), (p_out,) = _mix_in_bwd_dw(dproj, hn1, tm, 512, rider=_ride_chip_exchange([sb_out]))
    g_in4 = g_in_t.reshape(4, 2, -1, d)
    (grad_x, st_in), (h_in,) = _mix_in_bwd_dx(dproj, x2d, dx2, w_in_t, mod6, g_mix, tm,
                                              rider=_ride_pair_swap([g_in4]))
    sb_in, own_in = _pair_sum(g_in4, h_in, core_chip, g_in4.shape[2], "pair_sum_w_in")
    (p_in,) = _comm("exchange_w_in", _ride_chip_exchange([sb_in]))

    zrow = jnp.zeros((1, d), F32)
    small = jnp.concatenate([
        st_in[0:2], st_out[3:4], st_out[0:2], st_fin[1:2],
        st_in[2:3], st_out[2:3], st_fin[0:1],
        jnp.concatenate([g_small[7:8], g_small[10:11]], axis=1),
        jnp.concatenate([g_small[8:9], g_small[9:10]], axis=1),
        jnp.concatenate([jnp.concatenate([g_small[0:3], jnp.zeros((1, width), F32)], axis=0), g_small[3:7]], axis=1),
        zrow,
        _block_diag_grad(g_wa, heads, hd).reshape(-1, d),
        _block_diag_grad(g_wx, heads, hd).reshape(-1, d),
    ], axis=0)

    (small_all,) = _gather2("gather_small_grads", [small])

    gs_in = _sum4(own_in, p_in, own_in.shape[0], "sum_w_in").T
    gs_up = _sum4(own_up, p_up, 256, "sum_w_up").T
    gs_out = _sum4(own_out, p_out, own_out.shape[0], "sum_w_out")
    gs_down = _sum4(own_down, p_down, 256, "sum_w_down")
    ad_in = _adam_rows(w_in[0], gs_in, m_w_in[0], v_w_in[0], 256, "adam_w_in")
    ad_up = _adam_rows(w_up[0], gs_up, m_w_up[0], v_w_up[0], 256, "adam_w_up")
    ad_out = _adam_rows(w_out[0], gs_out, m_w_out[0], v_w_out[0], w_out.shape[1], "adam_w_out")
    ad_down = _adam_rows(w_down[0], gs_down, m_w_down[0], v_w_down[0], 256, "adam_w_down")

    gsum = _sum8(small_all, SMALL_ROWS, "sum_small")
    dmod_cols = lax.dynamic_slice(small_all[:, 0:6, :].reshape(N_DEV, 6 * d), (0, me * n_ada), (N_DEV, n_ada))
    g_ada, d_ada, nm_ada, nv_ada = _ada_bwd_adam(c_act[:, :, None], dmod_cols, w_ada[0], m_w_ada[0], v_w_ada[0], 256)

    g_conv = lax.dynamic_slice(gsum[11:15, 0:width], (0, me * csh), (4, csh))
    g_conv_l = lax.dynamic_slice(gsum[11:15, width:2 * width], (0, me * csh), (4, csh))
    small_g = [
        gsum[0:6].reshape(1, 6 * d),
        gsum[6:7],
        g_conv[0:3].reshape(1, 3, csh),
        g_conv_l.reshape(1, 4, csh),
        gsum[9:10, 0:width],
        gsum[16:48].reshape(1, heads, hd, hd),
        gsum[10:11, 0:width].reshape(1, heads, hd),
        gsum[48:80].reshape(1, heads, hd, hd),
        gsum[10:11, width:].reshape(1, heads, hd),
        gsum[9:10, width:],
        gsum[7:8],
        gsum[8],
    ]
    small_w = [b_ada, g_mix, conv_w_sc, conv_w_lru, conv_b_lru, w_rg_a, b_rg_a, w_rg_x, b_rg_x, lru_lambda, g_mlp, g_final]
    small_m = [m_b_ada, m_g_mix, m_conv_w_sc, m_conv_w_lru, m_conv_b_lru, m_w_rg_a, m_b_rg_a, m_w_rg_x, m_b_rg_x,
               m_lru_lambda, m_g_mlp, m_g_final]
    small_v = [v_b_ada, v_g_mix, v_conv_w_sc, v_conv_w_lru, v_conv_b_lru, v_w_rg_a, v_b_rg_a, v_w_rg_x, v_b_rg_x,
               v_lru_lambda, v_g_mlp, v_g_final]
    sd, snm, snv = _adam_small(small_w, small_g, small_m, small_v)

    def order(ada, w_in_, w_out_, w_up_, w_down_, sm):
        return [ada[None], sm[0], sm[1], w_in_[None], sm[2], sm[3], sm[4], sm[5], sm[6], sm[7], sm[8], sm[9],
                w_out_[None], sm[10], w_up_[None], w_down_[None], sm[11]]

    grads = order(g_ada, gs_in, gs_out, gs_up, gs_down, small_g)
    deltas = order(d_ada, ad_in[0], ad_out[0], ad_up[0], ad_down[0], sd)
    new_m = order(nm_ada, ad_in[1], ad_out[1], ad_up[1], ad_down[1], snm)
    new_v = order(nv_ada, ad_in[2], ad_out[2], ad_up[2], ad_down[2], snv)
    return (loss, grad_x[None], *grads, *deltas, *new_m, *new_v)
```

```python
import functools

import jax
import jax.numpy as jnp
from jax import lax
from jax.experimental import pallas as pl
from jax.experimental.pallas import tpu as pltpu

F32 = jnp.float32
BF16 = jnp.bfloat16
N_DEV = 8
EPS = 1e-6
RG_C = 8.0
GELU_K0 = 0.7978845608028654
GELU_K1 = 0.044715
ADAM_LR = 0.001
ADAM_B1 = 0.9
ADAM_B2 = 0.999
ADAM_EPS = 1e-08
ADAM_WD = 0.01
ADAM_STEP = 10
LANES = 128
SUBLANES = 8
VMEM_LIMIT = 52 * 1024 * 1024
MIX_ROWS = 256
SMALL_ROWS = 80

MESH = pl.DeviceIdType.MESH
ANY = pl.BlockSpec(memory_space=pl.ANY)
NN = ((1,), (0,))
NT = ((1,), (1,))
TN = ((0,), (0,))


def _dot(a, b, dims):
    return lax.dot_general(a, b, (dims, ((), ())), preferred_element_type=F32)


def _params(sem=None):
    return pltpu.CompilerParams(dimension_semantics=sem, vmem_limit_bytes=VMEM_LIMIT)


def _full(shape):
    nd = len(shape)
    return pl.BlockSpec(shape, lambda *_: (0,) * nd)


def _exchange(name, gathers, scatters):
    n_g = len(gathers)
    arrs = list(gathers) + list(scatters)
    n = len(arrs)
    out_shape = [jax.ShapeDtypeStruct((N_DEV,) + a.shape, a.dtype) for a in gathers]
    out_shape += [jax.ShapeDtypeStruct(a.shape, a.dtype) for a in scatters]

    def body(*refs):
        ins, outs = refs[:n], refs[n:2 * n]
        send_sems, recv_sems, local_sems = refs[2 * n:]
        x, y, c = lax.axis_index("x"), lax.axis_index("y"), lax.axis_index("c")
        me = 4 * x + 2 * y + c

        def src(a, dev):
            return ins[a] if a < n_g else ins[a].at[dev]

        def peer_of(k):
            px = 1 - x if (k >> 2) & 1 else x
            py = 1 - y if (k >> 1) & 1 else y
            pc = 1 - c if k & 1 else c
            return (px, py, pc), 4 * px + 2 * py + pc

        local = [pltpu.make_async_copy(src(a, me), outs[a].at[me], local_sems.at[a]) for a in range(n)]
        for cp in local:
            cp.start()
        sends = []
        for k in range(1, N_DEV):
            peer, pidx = peer_of(k)
            for a in range(n):
                cp = pltpu.make_async_remote_copy(
                    src_ref=src(a, pidx), dst_ref=outs[a].at[me],
                    send_sem=send_sems.at[a * (N_DEV - 1) + k - 1], recv_sem=recv_sems.at[a * (N_DEV - 1) + k - 1],
                    device_id=peer, device_id_type=MESH)
                cp.start()
                sends.append(cp)
        for k in range(1, N_DEV):
            peer, pidx = peer_of(k)
            for a in range(n):
                pltpu.make_async_remote_copy(
                    src_ref=src(a, pidx), dst_ref=outs[a].at[pidx],
                    send_sem=send_sems.at[a * (N_DEV - 1) + k - 1], recv_sem=recv_sems.at[a * (N_DEV - 1) + k - 1],
                    device_id=peer, device_id_type=MESH).wait_recv()
        for cp in sends:
            cp.wait_send()
        for cp in local:
            cp.wait()

    return pl.pallas_call(
        body, name=name, out_shape=out_shape,
        in_specs=[ANY] * n, out_specs=[ANY] * n,
        scratch_shapes=[pltpu.SemaphoreType.DMA((n * (N_DEV - 1),)),
                        pltpu.SemaphoreType.DMA((n * (N_DEV - 1),)),
                        pltpu.SemaphoreType.DMA((n,))],
    )(*arrs)


def _gather2(name, arrs):
    n = len(arrs)
    per = 7
    out_shape = [jax.ShapeDtypeStruct((N_DEV,) + a.shape, a.dtype) for a in arrs]

    def body(*refs):
        ins, outs = refs[:n], refs[n:2 * n]
        send_sems, recv_sems, local_sems = refs[2 * n:]
        x, y, c = lax.axis_index("x"), lax.axis_index("y"), lax.axis_index("c")
        sib = (x, y, 1 - c)
        chips = [(1 - x, y), (x, 1 - y), (1 - x, 1 - y)]

        def slot(a, px, py, pc):
            return outs[a].at[4 * px + 2 * py + pc]

        def copy(a, k, block, to, src=None):
            return pltpu.make_async_remote_copy(
                src_ref=slot(a, *block) if src is None else src, dst_ref=slot(a, *block),
                send_sem=send_sems.at[a * per + k], recv_sem=recv_sems.at[a * per + k],
                device_id=to, device_id_type=MESH)

        local = [pltpu.make_async_copy(ins[a], slot(a, x, y, c), local_sems.at[a]) for a in range(n)]
        for cp in local:
            cp.start()
        first = []
        for a in range(n):
            first += [copy(a, 1 + j, (x, y, c), (*chip, c), src=ins[a]) for j, chip in enumerate(chips)]
        for a in range(n):
            first.append(copy(a, 0, (x, y, c), sib, src=ins[a]))
        for cp in first:
            cp.start()
        passed = []
        for a in range(n):
            for j, chip in enumerate(chips):
                copy(a, 1 + j, (*chip, c), (x, y, c)).wait_recv()
                cp = copy(a, 4 + j, (*chip, c), sib)
                cp.start()
                passed.append(cp)
        for a in range(n):
            copy(a, 0, sib, (x, y, c)).wait_recv()
            for j, chip in enumerate(chips):
                copy(a, 4 + j, (*chip, 1 - c), (x, y, c)).wait_recv()
        for cp in first + passed:
            cp.wait_send()
        for cp in local:
            cp.wait()

    return pl.pallas_call(
        body, name=name, out_shape=out_shape,
        in_specs=[ANY] * n, out_specs=[ANY] * n,
        scratch_shapes=[pltpu.SemaphoreType.DMA((n * per,)), pltpu.SemaphoreType.DMA((n * per,)),
                        pltpu.SemaphoreType.DMA((n,))],
    )(*arrs)


def _pair_swap(name, arrs):
    n = len(arrs)
    out_shape = [jax.ShapeDtypeStruct((4,) + a.shape[2:], a.dtype) for a in arrs]

    def body(*refs):
        ins, outs = refs[:n], refs[n:2 * n]
        send_sems, recv_sems = refs[2 * n:]
        x, y, c = lax.axis_index("x"), lax.axis_index("y"), lax.axis_index("c")

        def copy(a, q):
            return pltpu.make_async_remote_copy(
                src_ref=ins[a].at[q, 1 - c], dst_ref=outs[a].at[q],
                send_sem=send_sems.at[a * 4 + q], recv_sem=recv_sems.at[a * 4 + q],
                device_id=(x, y, 1 - c), device_id_type=MESH)

        cps = [copy(a, q) for a in range(n) for q in range(4)]
        for cp in cps:
            cp.start()
        for cp in cps:
            cp.wait_recv()
        for cp in cps:
            cp.wait_send()

    return pl.pallas_call(
        body, name=name, out_shape=out_shape,
        in_specs=[ANY] * n, out_specs=[ANY] * n,
        scratch_shapes=[pltpu.SemaphoreType.DMA((n * 4,)), pltpu.SemaphoreType.DMA((n * 4,))],
    )(*arrs)


def _chip_exchange(name, arrs):
    n = len(arrs)
    out_shape = [jax.ShapeDtypeStruct((3,) + a.shape[1:], a.dtype) for a in arrs]

    def body(*refs):
        ins, outs = refs[:n], refs[n:2 * n]
        send_sems, recv_sems = refs[2 * n:]
        x, y, c = lax.axis_index("x"), lax.axis_index("y"), lax.axis_index("c")

        def copy(a, k):
            px = 1 - x if (k >> 1) & 1 else x
            py = 1 - y if k & 1 else y
            return pltpu.make_async_remote_copy(
                src_ref=ins[a].at[2 * px + py], dst_ref=outs[a].at[k - 1],
                send_sem=send_sems.at[a * 3 + k - 1], recv_sem=recv_sems.at[a * 3 + k - 1],
                device_id=(px, py, c), device_id_type=MESH)

        cps = [copy(a, k) for a in range(n) for k in (1, 2, 3)]
        for cp in cps:
            cp.start()
        for cp in cps:
            cp.wait_recv()
        for cp in cps:
            cp.wait_send()

    return pl.pallas_call(
        body, name=name, out_shape=out_shape,
        in_specs=[ANY] * n, out_specs=[ANY] * n,
        scratch_shapes=[pltpu.SemaphoreType.DMA((n * 3,)), pltpu.SemaphoreType.DMA((n * 3,))],
    )(*arrs)


class _Rider:
    def __init__(self, arrays, out_shapes, n_sems, build, aliases=None):
        self.arrays, self.out_shapes, self.n_sems, self.build = list(arrays), list(out_shapes), n_sems, build
        self.aliases = dict(aliases or {})


def _merge_riders(r1, r2):
    n1i, n1o, n1s = len(r1.arrays), len(r1.out_shapes), r1.n_sems

    def build(ins, outs, send_sems, recv_sems):
        a = r1.build(ins[:n1i], outs[:n1o], send_sems.at[pl.ds(0, n1s)], recv_sems.at[pl.ds(0, n1s)])
        b = r2.build(ins[n1i:], outs[n1o:], send_sems.at[pl.ds(n1s, r2.n_sems)], recv_sems.at[pl.ds(n1s, r2.n_sems)])
        return tuple(p + q for p, q in zip(a, b))

    aliases = dict(r1.aliases)
    aliases.update({k + n1i: v + n1o for k, v in r2.aliases.items()})
    return _Rider(r1.arrays + r2.arrays, r1.out_shapes + r2.out_shapes, n1s + r2.n_sems, build, aliases)


def _place():
    x, y, c = lax.axis_index("x"), lax.axis_index("y"), lax.axis_index("c")
    chips = [(1 - x, y), (x, 1 - y), (1 - x, 1 - y)]
    return x, y, c, chips


def _ride_gather_ici(arrs):
    n = len(arrs)

    def build(ins, outs, send_sems, recv_sems):
        x, y, c, chips = _place()
        peers = [(*chip, c) for chip in chips] + [(x, y, 1 - c)]
        me = 4 * x + 2 * y + c
        local = [pltpu.make_async_copy(ins[a], outs[a].at[me], send_sems.at[a * 5 + 4]) for a in range(n)]
        sends, recvs = [], []
        for a in range(n):
            for j, (px, py, pc) in enumerate(peers):
                sends.append(pltpu.make_async_remote_copy(
                    src_ref=ins[a], dst_ref=outs[a].at[me], send_sem=send_sems.at[a * 5 + j],
                    recv_sem=recv_sems.at[a * 5 + j], device_id=(px, py, pc), device_id_type=MESH))
                recvs.append(pltpu.make_async_remote_copy(
                    src_ref=ins[a], dst_ref=outs[a].at[4 * px + 2 * py + pc], send_sem=send_sems.at[a * 5 + j],
                    recv_sem=recv_sems.at[a * 5 + j], device_id=(px, py, pc), device_id_type=MESH))
        return local, sends, recvs

    shapes = [jax.ShapeDtypeStruct((N_DEV,) + a.shape, a.dtype) for a in arrs]
    return _Rider(arrs, shapes, n * 5, build)


def _ride_gather_direct(arrs):
    n = len(arrs)

    def build(ins, outs, send_sems, recv_sems):
        x, y, c, _ = _place()
        me = 4 * x + 2 * y + c
        local = [pltpu.make_async_copy(ins[a], outs[a].at[me], send_sems.at[a * N_DEV + 7]) for a in range(n)]
        sends, recvs = [], []
        for a in range(n):
            for k in range(1, N_DEV):
                px = 1 - x if (k >> 2) & 1 else x
                py = 1 - y if (k >> 1) & 1 else y
                pc = 1 - c if k & 1 else c
                sem = a * N_DEV + k - 1
                sends.append(pltpu.make_async_remote_copy(
                    src_ref=ins[a], dst_ref=outs[a].at[me], send_sem=send_sems.at[sem], recv_sem=recv_sems.at[sem],
                    device_id=(px, py, pc), device_id_type=MESH))
                recvs.append(pltpu.make_async_remote_copy(
                    src_ref=ins[a], dst_ref=outs[a].at[4 * px + 2 * py + pc], send_sem=send_sems.at[sem],
                    recv_sem=recv_sems.at[sem], device_id=(px, py, pc), device_id_type=MESH))
        return local, sends, recvs

    shapes = [jax.ShapeDtypeStruct((N_DEV,) + a.shape, a.dtype) for a in arrs]
    return _Rider(arrs, shapes, n * N_DEV, build)


def _ride_gather_d2d(gathered):
    n = len(gathered)

    def build(ins, outs, send_sems, recv_sems):
        x, y, c, chips = _place()
        sends, recvs = [], []
        for a in range(n):
            for j, (px, py) in enumerate(chips):
                mine = outs[a].at[4 * px + 2 * py + c]
                theirs = outs[a].at[4 * px + 2 * py + 1 - c]
                sends.append(pltpu.make_async_remote_copy(
                    src_ref=mine, dst_ref=mine, send_sem=send_sems.at[a * 3 + j], recv_sem=recv_sems.at[a * 3 + j],
                    device_id=(x, y, 1 - c), device_id_type=MESH))
                recvs.append(pltpu.make_async_remote_copy(
                    src_ref=mine, dst_ref=theirs, send_sem=send_sems.at[a * 3 + j], recv_sem=recv_sems.at[a * 3 + j],
                    device_id=(x, y, 1 - c), device_id_type=MESH))
        return [], sends, recvs

    shapes = [jax.ShapeDtypeStruct(a.shape, a.dtype) for a in gathered]
    return _Rider(gathered, shapes, n * 3, build, aliases={a: a for a in range(n)})


def _ride_pair_swap(arrs):
    n = len(arrs)

    def build(ins, outs, send_sems, recv_sems):
        x, y, c, _ = _place()
        cps = [pltpu.make_async_remote_copy(
            src_ref=ins[a].at[q, 1 - c], dst_ref=outs[a].at[q], send_sem=send_sems.at[a * 4 + q],
            recv_sem=recv_sems.at[a * 4 + q], device_id=(x, y, 1 - c), device_id_type=MESH)
            for a in range(n) for q in range(4)]
        return [], cps, cps

    shapes = [jax.ShapeDtypeStruct((4,) + a.shape[2:], a.dtype) for a in arrs]
    return _Rider(arrs, shapes, n * 4, build)


def _ride_chip_exchange(arrs):
    n = len(arrs)

    def build(ins, outs, send_sems, recv_sems):
        x, y, c, _ = _place()
        cps = []
        for a in range(n):
            for k in (1, 2, 3):
                px = 1 - x if (k >> 1) & 1 else x
                py = 1 - y if k & 1 else y
                cps.append(pltpu.make_async_remote_copy(
                    src_ref=ins[a].at[2 * px + py], dst_ref=outs[a].at[k - 1], send_sem=send_sems.at[a * 3 + k - 1],
                    recv_sem=recv_sems.at[a * 3 + k - 1], device_id=(px, py, c), device_id_type=MESH))
        return [], cps, cps

    shapes = [jax.ShapeDtypeStruct((3,) + a.shape[1:], a.dtype) for a in arrs]
    return _Rider(arrs, shapes, n * 3, build)


def _call(body, name, grid, in_specs, out_specs, out_shape, args, scratch=(), rider=None):
    n_in, n_out, n_scr = len(in_specs), len(out_specs), len(scratch)
    sem = ("arbitrary",) * len(grid)
    if rider is None:
        outs = pl.pallas_call(
            body, name=name, grid=grid, in_specs=in_specs, out_specs=out_specs, out_shape=out_shape,
            scratch_shapes=list(scratch), compiler_params=_params(sem))(*args)
        return outs, []
    ri, ro = len(rider.arrays), len(rider.out_shapes)

    def riding(*refs):
        ins, r_ins = refs[:n_in], refs[n_in:n_in + ri]
        outs = refs[n_in + ri:n_in + ri + n_out]
        r_outs = refs[n_in + ri + n_out:n_in + ri + n_out + ro]
        scr = refs[n_in + ri + n_out + ro:n_in + ri + n_out + ro + n_scr]
        send_sems, recv_sems = refs[-2:]
        first = functools.reduce(jnp.logical_and, [pl.program_id(k) == 0 for k in range(len(grid))])
        last = functools.reduce(jnp.logical_and, [pl.program_id(k) == grid[k] - 1 for k in range(len(grid))])

        @pl.when(first)
        def _():
            local, sends, _ = rider.build(r_ins, r_outs, send_sems, recv_sems)
            for cp in local + sends:
                cp.start()

        body(*ins, *outs, *scr)

        @pl.when(last)
        def _():
            local, sends, recvs = rider.build(r_ins, r_outs, send_sems, recv_sems)
            for cp in recvs:
                cp.wait_recv()
            for cp in sends:
                cp.wait_send()
            for cp in local:
                cp.wait()

    outs = pl.pallas_call(
        riding, name=name, grid=grid,
        in_specs=list(in_specs) + [ANY] * ri, out_specs=list(out_specs) + [ANY] * ro,
        out_shape=list(out_shape) + rider.out_shapes,
        scratch_shapes=list(scratch) + [pltpu.SemaphoreType.DMA((rider.n_sems,)), pltpu.SemaphoreType.DMA((rider.n_sems,))],
        input_output_aliases={n_in + k: n_out + v for k, v in rider.aliases.items()},
        compiler_params=_params(sem))(*args, *rider.arrays)
    return outs[:n_out], outs[n_out:]


def _comm(name, rider):
    def body(dummy_ref, out_ref):
        out_ref[...] = dummy_ref[...]

    dummy = jnp.zeros((SUBLANES, LANES), F32)
    spec = pl.BlockSpec((SUBLANES, LANES), lambda i: (0, 0))
    _, r_outs = _call(body, name, (1,), [spec], [spec], [jax.ShapeDtypeStruct(dummy.shape, F32)], [dummy], rider=rider)
    return r_outs


def _ada_fwd(c_all, w_ada_sh, b_ada_sh):
    nb, d = c_all.shape
    ncol = w_ada_sh.shape[1]

    def body(c_ref, w_ref, b_ref, mod_ref, cact_ref):
        cc = c_ref[...]
        ca = cc * jax.nn.sigmoid(cc)
        cact_ref[...] = ca
        mod_ref[...] = _dot(ca.astype(BF16), w_ref[...].astype(BF16), NN) + b_ref[...]

    return pl.pallas_call(
        body, name="ada_fwd",
        out_shape=[jax.ShapeDtypeStruct((nb, ncol), F32), jax.ShapeDtypeStruct((nb, d), F32)],
        compiler_params=_params(),
    )(c_all, w_ada_sh, b_ada_sh)


def _rms(xv):
    rstd = lax.rsqrt(jnp.mean(xv * xv, axis=-1, keepdims=True) + EPS)
    return xv * rstd, rstd


def _rms_bwd(dxhat, xhat, rstd):
    return rstd * (dxhat - xhat * jnp.mean(dxhat * xhat, axis=-1, keepdims=True))


def _colsum(v):
    return jnp.sum(v, axis=0, keepdims=True)


def _expm1(v):
    series = v * (1.0 + v * (0.5 + v * (1.0 / 6.0 + v * (1.0 / 24.0 + v * (1.0 / 120.0 + v * (1.0 / 720.0))))))
    return jnp.where(jnp.abs(v) < 0.3, series, jnp.exp(v) - 1.0)


def _softplus(v):
    return jnp.maximum(v, 0.0) + jnp.log1p(jnp.exp(-jnp.abs(v)))


def _gelu(v):
    t = jnp.tanh(GELU_K0 * (v + GELU_K1 * v * v * v))
    return 0.5 * v * (1.0 + t), t


def _dgelu(v, t):
    return 0.5 * (1.0 + t) + 0.5 * v * (1.0 - t * t) * GELU_K0 * (1.0 + 3.0 * GELU_K1 * v * v)


def _shift_down(v, k, prev8):
    r = pltpu.roll(v, k, 0)
    pr = pltpu.roll(prev8, k, 0)
    row8 = lax.broadcasted_iota(jnp.int32, prev8.shape, 0)
    top = jnp.where(row8 < k, pr, r[0:SUBLANES])
    return jnp.concatenate([top, r[SUBLANES:]], axis=0)


def _shift_up(v, k, next8):
    t = v.shape[0]
    r = pltpu.roll(v, t - k, 0)
    nr = pltpu.roll(next8, SUBLANES - k, 0)
    row8 = lax.broadcasted_iota(jnp.int32, next8.shape, 0)
    bot = jnp.where(row8 >= SUBLANES - k, nr, r[t - SUBLANES:t])
    return jnp.concatenate([r[:t - SUBLANES], bot], axis=0)


def _scan_fwd(a, b, h0):
    t = a.shape[0]
    row = lax.broadcasted_iota(jnp.int32, a.shape, 0)
    s = 1
    while s < t:
        a_sh = pltpu.roll(a, s, 0)
        b_sh = pltpu.roll(b, s, 0)
        m = row >= s
        b = jnp.where(m, a * b_sh + b, b)
        a = jnp.where(m, a * a_sh, a)
        s *= 2
    return b + a * h0


def _scan_rev(m, b, g_next):
    t = m.shape[0]
    row = lax.broadcasted_iota(jnp.int32, m.shape, 0)
    s = 1
    while s < t:
        m_sh = pltpu.roll(m, t - s, 0)
        b_sh = pltpu.roll(b, t - s, 0)
        msk = row < t - s
        b = jnp.where(msk, m * b_sh + b, b)
        m = jnp.where(msk, m * m_sh, m)
        s *= 2
    return b + m * g_next


def _lru_gates(u, wa, wx, ba, bx, sp):
    ub = u.astype(BF16)
    r = jax.nn.sigmoid(_dot(ub, wa, NN) + ba)
    i = jax.nn.sigmoid(_dot(ub, wx, NN) + bx)
    log_a = (-RG_C * r) * sp
    a = jnp.exp(log_a)
    mult = jnp.sqrt(-_expm1(2.0 * log_a))
    return ub, r, i, a, mult


def _conv3(p, pp, w_ref, lo):
    p1 = _shift_down(p, 1, pp)
    p2 = _shift_down(p, 2, pp)
    q = (w_ref[0:1, lo:lo + LANES] * p2 + w_ref[1:2, lo:lo + LANES] * p1) + w_ref[2:3, lo:lo + LANES] * p
    return q, p1, p2


def _conv4(xv, xp, w_ref, b_ref, lo):
    x1 = _shift_down(xv, 1, xp)
    x2 = _shift_down(xv, 2, xp)
    x3 = _shift_down(xv, 3, xp)
    u = (((w_ref[0:1, lo:lo + LANES] * x3 + w_ref[1:2, lo:lo + LANES] * x2) + w_ref[2:3, lo:lo + LANES] * x1)
         + w_ref[3:4, lo:lo + LANES] * xv) + b_ref[:, lo:lo + LANES]
    return u, x1, x2, x3


def _mix_in_fwd(x2d, mod6, g_mix, w_in_t, tm, rider=None):
    s, d = x2d.shape
    din = w_in_t.shape[0]

    def body(x_ref, mod_ref, g_ref, w_ref, hn_ref, proj_ref):
        xhat, _ = _rms(x_ref[...])
        hn = ((xhat * g_ref[...]) * (1.0 + mod_ref[1:2, :]) + mod_ref[0:1, :]).astype(BF16)
        hn_ref[...] = hn
        proj_ref[...] = _dot(hn, w_ref[...], NT)

    return _call(
        body, "mix_in_fwd", (s // tm,),
        [pl.BlockSpec((tm, d), lambda i: (i, 0)), _full(mod6.shape), _full(g_mix.shape), _full(w_in_t.shape)],
        [pl.BlockSpec((tm, d), lambda i: (i, 0)), pl.BlockSpec((tm, din), lambda i: (i, 0))],
        [jax.ShapeDtypeStruct((s, d), BF16), jax.ShapeDtypeStruct((s, din), F32)],
        [x2d, mod6, g_mix, w_in_t], rider=rider)


def _mixer_fwd(proj, conv_sc, conv_lru, conv_b, wa_bd, wx_bd, ba, bx, lam, width, rider=None):
    s, din = proj.shape
    t = min(MIX_ROWS, s)
    nblk = width // LANES
    hb = t // SUBLANES

    def body(proj_ref, projp_ref, wsc_ref, wlru_ref, blru_ref, wa_ref, wx_ref, ba_ref, bx_ref, lam_ref,
             ymix_ref, h_ref, hc_ref):
        i = pl.program_id(0)

        @pl.when(i == 0)
        def _():
            hc_ref[...] = jnp.zeros_like(hc_ref)

        has_prev = i > 0
        for j in range(nblk):
            lo = j * LANES

            def col(p, ref=proj_ref):
                return ref[:, p * width + lo:p * width + lo + LANES]

            def prev(p):
                return jnp.where(has_prev, col(p, projp_ref), 0.0)

            p = col(1) * col(2)
            q, _, _ = _conv3(p, prev(1) * prev(2), wsc_ref, lo)
            ymix_ref[:, lo:lo + LANES] = (col(0) * q).astype(BF16)

            u, _, _, _ = _conv4(col(4), prev(4), wlru_ref, blru_ref, lo)
            sp = _softplus(-lam_ref[:, lo:lo + LANES])
            _, r, ig, a, mult = _lru_gates(u, wa_ref[j], wx_ref[j], ba_ref[:, lo:lo + LANES], bx_ref[:, lo:lo + LANES], sp)
            h = _scan_fwd(a, mult * (ig * u), hc_ref[0:1, lo:lo + LANES])
            h_ref[:, lo:lo + LANES] = h
            hc_ref[0:1, lo:lo + LANES] = h[t - 1:t, :]
            gel, _ = _gelu(col(3))
            ymix_ref[:, width + lo:width + lo + LANES] = (gel * h).astype(BF16)

    small = [conv_sc, conv_lru, conv_b, wa_bd, wx_bd, ba, bx, lam]
    return _call(
        body, "mixer_fwd", (s // t,),
        [pl.BlockSpec((t, din), lambda i: (i, 0)),
         pl.BlockSpec((SUBLANES, din), lambda i: (jnp.maximum(i * hb - 1, 0), 0))]
        + [_full(a.shape) for a in small],
        [pl.BlockSpec((t, 2 * width), lambda i: (i, 0)), pl.BlockSpec((t, width), lambda i: (i, 0))],
        [jax.ShapeDtypeStruct((s, 2 * width), BF16), jax.ShapeDtypeStruct((s, width), F32)],
        [proj, proj, *small], scratch=[pltpu.VMEM((SUBLANES, width), F32)], rider=rider)


def _mix_out_fwd(ymix, x2d, w_out, mod6, g_mlp, tm, rider=None):
    s, d = x2d.shape

    def body(y_ref, x_ref, w_ref, mod_ref, g_ref, mix_ref, x2_ref, hn_ref):
        mix = _dot(y_ref[...], w_ref[...], NN)
        mix_ref[...] = mix
        x2 = x_ref[...] + mod_ref[2:3, :] * mix
        x2_ref[...] = x2
        xhat, _ = _rms(x2)
        hn_ref[...] = ((xhat * g_ref[...]) * (1.0 + mod_ref[4:5, :]) + mod_ref[3:4, :]).astype(BF16)

    tile = pl.BlockSpec((tm, d), lambda i: (i, 0))
    return _call(
        body, "mix_out_fwd", (s // tm,),
        [tile, tile, _full(w_out.shape), _full(mod6.shape), _full(g_mlp.shape)],
        [tile, tile, tile],
        [jax.ShapeDtypeStruct((s, d), F32), jax.ShapeDtypeStruct((s, d), F32), jax.ShapeDtypeStruct((s, d), BF16)],
        [ymix, x2d, w_out, mod6, g_mlp], rider=rider)


def _mlp_fwd_loss(hn2, w_up_t, w_down, x2, target, mod6, g_final, tm, tk):
    s, d = hn2.shape
    f = w_up_t.shape[0]
    nk = f // tk

    def body(hn_ref, wu_ref, wd_ref, x2_ref, t_ref, mod_ref, g_ref, z_ref, dx3_ref, dyb_ref, st_ref, y_ref):
        i, k = pl.program_id(0), pl.program_id(1)

        @pl.when(jnp.logical_and(i == 0, k == 0))
        def _():
            st_ref[...] = jnp.zeros_like(st_ref)

        z = jnp.maximum(_dot(hn_ref[...], wu_ref[...], NT), 0.0)
        z_ref[...] = z.astype(BF16)
        part = _dot((z * z).astype(BF16), wd_ref[...], NN)

        @pl.when(k == 0)
        def _():
            y_ref[...] = part

        @pl.when(k > 0)
        def _():
            y_ref[...] += part

        @pl.when(k == nk - 1)
        def _():
            gate = mod_ref[5:6, :]
            yv = y_ref[...]
            xhat, rstd = _rms(x2_ref[...] + gate * yv)
            diff = xhat * g_ref[...] - t_ref[...]
            dyo = diff * (1.0 / d)
            dx3 = _rms_bwd(dyo * g_ref[...], xhat, rstd)
            dx3_ref[...] = dx3
            dyb_ref[...] = (gate * dx3).astype(BF16)
            st_ref[0:1, :] += _colsum(dyo * xhat)
            st_ref[1:2, :] += _colsum(dx3 * yv)
            st_ref[2:3, :] += _colsum(diff * diff)

    tile = pl.BlockSpec((tm, d), lambda i, k: (i, 0))
    wblk = pl.BlockSpec((tk, d), lambda i, k: (k, 0))
    return pl.pallas_call(
        body, name="mlp_fwd_loss", grid=(s // tm, nk),
        in_specs=[tile, wblk, wblk, tile, tile, _full(mod6.shape), _full(g_final.shape)],
        out_specs=[pl.BlockSpec((tm, tk), lambda i, k: (i, k)), tile, tile, _full((SUBLANES, d))],
        out_shape=[jax.ShapeDtypeStruct((s, f), BF16), jax.ShapeDtypeStruct((s, d), F32),
                   jax.ShapeDtypeStruct((s, d), BF16), jax.ShapeDtypeStruct((SUBLANES, d), F32)],
        scratch_shapes=[pltpu.VMEM((tm, d), F32)],
        compiler_params=_params(("arbitrary", "arbitrary")),
    )(hn2, w_up_t, w_down, x2, target, mod6, g_final)


def _mlp_bwd_dx(dyb, z, w_down, w_up_t, tm, tk):
    s, d = dyb.shape
    f = z.shape[1]

    def body(dy_ref, z_ref, wd_ref, wu_ref, dz_ref, dh_ref):
        k = pl.program_id(1)
        dz = ((2.0 * z_ref[...].astype(F32)) * _dot(dy_ref[...], wd_ref[...], NT)).astype(BF16)
        dz_ref[...] = dz
        part = _dot(dz, wu_ref[...], NN)

        @pl.when(k == 0)
        def _():
            dh_ref[...] = part

        @pl.when(k > 0)
        def _():
            dh_ref[...] += part

    return pl.pallas_call(
        body, name="mlp_bwd_dx", grid=(s // tm, f // tk),
        in_specs=[pl.BlockSpec((tm, d), lambda i, k: (i, 0)), pl.BlockSpec((tm, tk), lambda i, k: (i, k)),
                  pl.BlockSpec((tk, d), lambda i, k: (k, 0)), pl.BlockSpec((tk, d), lambda i, k: (k, 0))],
        out_specs=[pl.BlockSpec((tm, tk), lambda i, k: (i, k)), pl.BlockSpec((tm, d), lambda i, k: (i, 0))],
        out_shape=[jax.ShapeDtypeStruct((s, f), BF16), jax.ShapeDtypeStruct((s, d), F32)],
        compiler_params=_params(("parallel", "arbitrary")),
    )(dyb, z, w_down, w_up_t)


def _mlp_bwd_dw(z, dz, dyb, hn2, tm, tk):
    s, d = dyb.shape
    f = z.shape[1]

    def body(z_ref, dz_ref, dy_ref, hn_ref, gd_ref, gu_ref):
        i = pl.program_id(1)

        @pl.when(i == 0)
        def _():
            gd_ref[...] = jnp.zeros_like(gd_ref)
            gu_ref[...] = jnp.zeros_like(gu_ref)

        zf = z_ref[...].astype(F32)
        gd_ref[...] += _dot((zf * zf).astype(BF16), dy_ref[...], TN)
        gu_ref[...] += _dot(dz_ref[...], hn_ref[...], TN)

    return pl.pallas_call(
        body, name="mlp_bwd_dw", grid=(f // tk, s // tm),
        in_specs=[pl.BlockSpec((tm, tk), lambda k, i: (i, k)), pl.BlockSpec((tm, tk), lambda k, i: (i, k)),
                  pl.BlockSpec((tm, d), lambda k, i: (i, 0)), pl.BlockSpec((tm, d), lambda k, i: (i, 0))],
        out_specs=[pl.BlockSpec((tk, d), lambda k, i: (k, 0)), pl.BlockSpec((tk, d), lambda k, i: (k, 0))],
        out_shape=[jax.ShapeDtypeStruct((f, d), F32), jax.ShapeDtypeStruct((f, d), F32)],
        compiler_params=_params(("parallel", "arbitrary")),
    )(z, dz, dyb, hn2)


def _mix_out_bwd(dhn2, x2, dx3, mix, ymix, w_out, mod6, g_mlp, tm, rider=None):
    s, d = x2.shape

    def body(dh_ref, x2_ref, dx3_ref, mix_ref, y_ref, w_ref, mod_ref, g_ref, dx2_ref, dym_ref, gw_ref, st_ref):
        i = pl.program_id(0)

        @pl.when(i == 0)
        def _():
            st_ref[...] = jnp.zeros_like(st_ref)
            gw_ref[...] = jnp.zeros_like(gw_ref)

        dh = dh_ref[...]
        xhat, rstd = _rms(x2_ref[...])
        dn = dh * (1.0 + mod_ref[4:5, :])
        dx2 = dx3_ref[...] + _rms_bwd(dn * g_ref[...], xhat, rstd)
        dx2_ref[...] = dx2
        st_ref[0:1, :] += _colsum(dh)
        st_ref[1:2, :] += _colsum(dh * (xhat * g_ref[...]))
        st_ref[2:3, :] += _colsum(dn * xhat)
        st_ref[3:4, :] += _colsum(dx2 * mix_ref[...])
        dmix = (mod_ref[2:3, :] * dx2).astype(BF16)
        dym_ref[...] = _dot(dmix, w_ref[...], NT)
        gw_ref[...] += _dot(y_ref[...], dmix, TN)

    tile = pl.BlockSpec((tm, d), lambda i: (i, 0))
    return _call(
        body, "mix_out_bwd", (s // tm,),
        [tile, tile, tile, tile, tile, _full(w_out.shape), _full(mod6.shape), _full(g_mlp.shape)],
        [tile, tile, _full((d, d)), _full((SUBLANES, d))],
        [jax.ShapeDtypeStruct((s, d), F32), jax.ShapeDtypeStruct((s, d), F32),
         jax.ShapeDtypeStruct((d, d), F32), jax.ShapeDtypeStruct((SUBLANES, d), F32)],
        [dhn2, x2, dx3, mix, ymix, w_out, mod6, g_mlp], rider=rider)


def _mixer_bwd(proj, dymix, h_all, conv_sc, conv_lru, conv_b, wa_bd, wx_bd, ba, bx, lam, width, rider=None):
    s, din = proj.shape
    t = min(MIX_ROWS, s)
    nt = s // t
    nblk = width // LANES
    hb = t // SUBLANES
    last8 = s // SUBLANES - 1

    def body(proj_ref, projp_ref, projn_ref, dy_ref, dyn_ref, h_ref, hp_ref,
             wsc_ref, wlru_ref, blru_ref, wa_ref, wx_ref, ba_ref, bx_ref, lam_ref,
             dproj_ref, small_ref, gwa_ref, gwx_ref, an_ref, gn_ref, dun_ref):
        i = pl.program_id(0)

        @pl.when(i == 0)
        def _():
            small_ref[...] = jnp.zeros_like(small_ref)
            gwa_ref[...] = jnp.zeros_like(gwa_ref)
            gwx_ref[...] = jnp.zeros_like(gwx_ref)
            an_ref[...] = jnp.zeros_like(an_ref)
            gn_ref[...] = jnp.zeros_like(gn_ref)
            dun_ref[...] = jnp.zeros_like(dun_ref)

        has_prev = i < nt - 1
        has_next = i > 0
        for j in range(nblk):
            lo = j * LANES
            ls = slice(lo, lo + LANES)

            def col(p, ref=proj_ref):
                return ref[:, p * width + lo:p * width + lo + LANES]

            def prev(p):
                return jnp.where(has_prev, col(p, projp_ref), 0.0)

            def nxt(p):
                return jnp.where(has_next, col(p, projn_ref), 0.0)

            def add_row(r, v):
                small_ref[r:r + 1, ls] += _colsum(v)

            sc_b, sc_c, sc_x = col(0), col(1), col(2)
            p = sc_c * sc_x
            q, p1, p2 = _conv3(p, prev(1) * prev(2), wsc_ref, lo)
            dys = dy_ref[:, ls]
            dproj_ref[:, ls] = (dys * q).astype(BF16)
            dq = dys * sc_b
            dqn = jnp.where(has_next, dyn_ref[:, ls], 0.0) * nxt(0)
            dp = (wsc_ref[2:3, ls] * dq + wsc_ref[1:2, ls] * _shift_up(dq, 1, dqn)) + wsc_ref[0:1, ls] * _shift_up(dq, 2, dqn)
            dproj_ref[:, width + lo:width + lo + LANES] = (dp * sc_x).astype(BF16)
            dproj_ref[:, 2 * width + lo:2 * width + lo + LANES] = (dp * sc_c).astype(BF16)
            add_row(0, dq * p2)
            add_row(1, dq * p1)
            add_row(2, dq * p)

            xv = col(4)
            u, x1, x2, x3 = _conv4(xv, prev(4), wlru_ref, blru_ref, lo)
            lam_v = lam_ref[:, ls]
            sp = _softplus(-lam_v)
            wa, wx = wa_ref[j], wx_ref[j]
            ub, r, ig, a, mult = _lru_gates(u, wa, wx, ba_ref[:, ls], bx_ref[:, ls], sp)
            iu = ig * u
            h = h_ref[:, ls]
            hm1 = _shift_down(h, 1, jnp.where(has_prev, hp_ref[:, ls], 0.0))
            lyv = col(3)
            gel, th = _gelu(lyv)
            dyl = dy_ref[:, width + lo:width + lo + LANES]
            dproj_ref[:, 3 * width + lo:3 * width + lo + LANES] = (dyl * h * _dgelu(lyv, th)).astype(BF16)
            a_next = jnp.broadcast_to(an_ref[0:1, ls], (SUBLANES, LANES))
            g = _scan_rev(_shift_up(a, 1, a_next), dyl * gel, gn_ref[0:1, ls])
            an_ref[0:1, ls] = a[0:1, :]
            gn_ref[0:1, ls] = g[0:1, :]
            da = g * hm1
            dmult = g * iu
            diu = g * mult
            dlog_a = da * a - dmult * ((a * a) / mult)
            dpre_a = (dlog_a * (-RG_C * sp)) * (r * (1.0 - r))
            dpre_x = (diu * u) * (ig * (1.0 - ig))
            dab, dxb = dpre_a.astype(BF16), dpre_x.astype(BF16)
            du = diu * ig + _dot(dab, wa, NT) + _dot(dxb, wx, NT)
            gwa_ref[j] += _dot(ub, dab, TN)
            gwx_ref[j] += _dot(ub, dxb, TN)
            dun = dun_ref[:, ls]
            dun_ref[:, ls] = du[0:SUBLANES, :]
            dlx = (((wlru_ref[3:4, ls] * du + wlru_ref[2:3, ls] * _shift_up(du, 1, dun))
                    + wlru_ref[1:2, ls] * _shift_up(du, 2, dun)) + wlru_ref[0:1, ls] * _shift_up(du, 3, dun))
            dproj_ref[:, 4 * width + lo:4 * width + lo + LANES] = dlx.astype(BF16)
            add_row(3, du * x3)
            add_row(4, du * x2)
            add_row(5, du * x1)
            add_row(6, du * xv)
            add_row(7, du)
            add_row(8, dpre_a)
            add_row(9, dpre_x)
            add_row(10, (dlog_a * (RG_C * r)) * jax.nn.sigmoid(-lam_v))

    small = [conv_sc, conv_lru, conv_b, wa_bd, wx_bd, ba, bx, lam]
    rev = lambda i: nt - 1 - i
    return _call(
        body, "mixer_bwd", (nt,),
        [pl.BlockSpec((t, din), lambda i: (rev(i), 0)),
         pl.BlockSpec((SUBLANES, din), lambda i: (jnp.maximum(rev(i) * hb - 1, 0), 0)),
         pl.BlockSpec((SUBLANES, din), lambda i: (jnp.minimum((rev(i) + 1) * hb, last8), 0)),
         pl.BlockSpec((t, 2 * width), lambda i: (rev(i), 0)),
         pl.BlockSpec((SUBLANES, 2 * width), lambda i: (jnp.minimum((rev(i) + 1) * hb, last8), 0)),
         pl.BlockSpec((t, width), lambda i: (rev(i), 0)),
         pl.BlockSpec((SUBLANES, width), lambda i: (jnp.maximum(rev(i) * hb - 1, 0), 0))]
        + [_full(a.shape) for a in small],
        [pl.BlockSpec((t, din), lambda i: (rev(i), 0)), _full((2 * SUBLANES, width)),
         _full(wa_bd.shape), _full(wx_bd.shape)],
        [jax.ShapeDtypeStruct((s, din), BF16), jax.ShapeDtypeStruct((2 * SUBLANES, width), F32),
         jax.ShapeDtypeStruct(wa_bd.shape, F32), jax.ShapeDtypeStruct(wx_bd.shape, F32)],
        [proj, proj, proj, dymix, dymix, h_all, h_all, *small],
        scratch=[pltpu.VMEM((SUBLANES, width), F32), pltpu.VMEM((SUBLANES, width), F32),
                 pltpu.VMEM((SUBLANES, width), F32)], rider=rider)


def _mix_in_bwd_dx(dproj, x2d, dx2, w_in_t, mod6, g_mix, tm, rider=None):
    s, d = x2d.shape
    din = dproj.shape[1]

    def body(dp_ref, x_ref, dx2_ref, w_ref, mod_ref, g_ref, gx_ref, st_ref):
        i = pl.program_id(0)

        @pl.when(i == 0)
        def _():
            st_ref[...] = jnp.zeros_like(st_ref)

        dh = _dot(dp_ref[...], w_ref[...], NN)
        xhat, rstd = _rms(x_ref[...])
        dn = dh * (1.0 + mod_ref[1:2, :])
        gx_ref[...] = dx2_ref[...] + _rms_bwd(dn * g_ref[...], xhat, rstd)
        st_ref[0:1, :] += _colsum(dh)
        st_ref[1:2, :] += _colsum(dh * (xhat * g_ref[...]))
        st_ref[2:3, :] += _colsum(dn * xhat)

    tile = pl.BlockSpec((tm, d), lambda i: (i, 0))
    return _call(
        body, "mix_in_bwd_dx", (s // tm,),
        [pl.BlockSpec((tm, din), lambda i: (i, 0)), tile, tile, _full(w_in_t.shape), _full(mod6.shape),
         _full(g_mix.shape)],
        [tile, _full((SUBLANES, d))],
        [jax.ShapeDtypeStruct((s, d), F32), jax.ShapeDtypeStruct((SUBLANES, d), F32)],
        [dproj, x2d, dx2, w_in_t, mod6, g_mix], rider=rider)


def _mix_in_bwd_dw(dproj, hn1, tm, tn, rider=None):
    s, d = hn1.shape
    din = dproj.shape[1]

    def body(dp_ref, hn_ref, gw_ref):
        i = pl.program_id(1)

        @pl.when(i == 0)
        def _():
            gw_ref[...] = jnp.zeros_like(gw_ref)

        gw_ref[...] += _dot(dp_ref[...], hn_ref[...], TN)

    return _call(
        body, "mix_in_bwd_dw", (din // tn, s // tm),
        [pl.BlockSpec((tm, tn), lambda p, i: (i, p)), pl.BlockSpec((tm, d), lambda p, i: (i, 0))],
        [pl.BlockSpec((tn, d), lambda p, i: (p, 0))],
        [jax.ShapeDtypeStruct((din, d), F32)],
        [dproj, hn1], rider=rider)


def _adamw(w, g, m, v):
    m = ADAM_B1 * m + (1.0 - ADAM_B1) * g
    v = ADAM_B2 * v + (1.0 - ADAM_B2) * (g * g)
    m_hat = m / (1.0 - ADAM_B1 ** ADAM_STEP)
    v_hat = v / (1.0 - ADAM_B2 ** ADAM_STEP)
    delta = -ADAM_LR * (m_hat / (jnp.sqrt(v_hat) + ADAM_EPS) + ADAM_WD * w)
    return delta, m, v


def _pair_sum(g4, h4, core_chip, tr, name):
    _, _, r, n = g4.shape

    def body(sc_ref, g_ref, h_ref, sb_ref, own_ref):
        q = pl.program_id(1)
        ssum = g_ref[...] + h_ref[...]
        sb_ref[...] = ssum.astype(BF16)

        @pl.when(q == sc_ref[1])
        def _():
            own_ref[...] = ssum

    grid_spec = pltpu.PrefetchScalarGridSpec(
        num_scalar_prefetch=1, grid=(r // tr, 4),
        in_specs=[pl.BlockSpec((None, None, tr, n), lambda i, q, sc: (q, sc[0], i, 0)),
                  pl.BlockSpec((None, tr, n), lambda i, q, sc: (q, i, 0))],
        out_specs=[pl.BlockSpec((None, tr, n), lambda i, q, sc: (q, i, 0)),
                   pl.BlockSpec((tr, n), lambda i, q, sc: (i, 0))])
    return pl.pallas_call(
        body, name=name, grid_spec=grid_spec,
        out_shape=[jax.ShapeDtypeStruct((4, r, n), BF16), jax.ShapeDtypeStruct((r, n), F32)],
        compiler_params=_params(("parallel", "arbitrary")),
    )(core_chip, g4, h4)


def _sum4(own, parts, tr, name):
    r, n = own.shape

    def body(o_ref, p_ref, out_ref):
        acc = o_ref[...]
        for k in range(3):
            acc = acc + p_ref[k].astype(F32)
        out_ref[...] = acc

    return pl.pallas_call(
        body, name=name, grid=(r // tr,),
        in_specs=[pl.BlockSpec((tr, n), lambda i: (i, 0)), pl.BlockSpec((3, tr, n), lambda i: (0, i, 0))],
        out_specs=pl.BlockSpec((tr, n), lambda i: (i, 0)),
        out_shape=jax.ShapeDtypeStruct((r, n), F32),
        compiler_params=_params(("parallel",)),
    )(own, parts)


def _sum8(parts, tr, name):
    _, rows, n = parts.shape

    def body(p_ref, o_ref):
        acc = p_ref[0]
        for k in range(1, N_DEV):
            acc = acc + p_ref[k]
        o_ref[...] = acc

    return pl.pallas_call(
        body, name=name, grid=(rows // tr,),
        in_specs=[pl.BlockSpec((N_DEV, tr, n), lambda i: (0, i, 0))],
        out_specs=pl.BlockSpec((tr, n), lambda i: (i, 0)),
        out_shape=jax.ShapeDtypeStruct((rows, n), F32),
        compiler_params=_params(("parallel",)),
    )(parts)


def _adam_rows(w, g, m, v, tr, name):
    rows, n = w.shape

    def body(w_ref, g_ref, m_ref, v_ref, d_ref, nm_ref, nv_ref):
        d_ref[...], nm_ref[...], nv_ref[...] = _adamw(w_ref[...], g_ref[...], m_ref[...], v_ref[...])

    tile = pl.BlockSpec((tr, n), lambda i: (i, 0))
    return pl.pallas_call(
        body, name=name, grid=(rows // tr,),
        in_specs=[tile] * 4, out_specs=[tile] * 3,
        out_shape=[jax.ShapeDtypeStruct((rows, n), F32)] * 3,
        compiler_params=_params(("parallel",)),
    )(w, g, m, v)


def _ada_bwd_adam(cact_t, dmod_cols, w, m, v, tr):
    rows, n = w.shape

    def body(c_ref, d_ref, w_ref, m_ref, v_ref, g_ref, dl_ref, nm_ref, nv_ref):
        def term(b):
            return c_ref[b].astype(BF16).astype(F32) * d_ref[b:b + 1, :].astype(BF16).astype(F32)

        g = term(0)
        for b in range(1, N_DEV):
            g = g + term(b)
        g_ref[...] = g
        dl_ref[...], nm_ref[...], nv_ref[...] = _adamw(w_ref[...], g, m_ref[...], v_ref[...])

    tile = pl.BlockSpec((tr, n), lambda i: (i, 0))
    return pl.pallas_call(
        body, name="ada_bwd_adam", grid=(rows // tr,),
        in_specs=[pl.BlockSpec((N_DEV, tr, 1), lambda i: (0, i, 0)), _full(dmod_cols.shape), tile, tile, tile],
        out_specs=[tile] * 4,
        out_shape=[jax.ShapeDtypeStruct((rows, n), F32)] * 4,
        compiler_params=_params(("parallel",)),
    )(cact_t, dmod_cols, w, m, v)


def _adam_small(ws, gs, ms, vs):
    n = len(ws)

    def body(*refs):
        w_r, g_r, m_r, v_r = refs[:n], refs[n:2 * n], refs[2 * n:3 * n], refs[3 * n:4 * n]
        d_r, nm_r, nv_r = refs[4 * n:5 * n], refs[5 * n:6 * n], refs[6 * n:7 * n]
        for k in range(n):
            d_r[k][...], nm_r[k][...], nv_r[k][...] = _adamw(w_r[k][...], g_r[k][...], m_r[k][...], v_r[k][...])

    shapes = [jax.ShapeDtypeStruct(w.shape, F32) for w in ws]
    outs = pl.pallas_call(
        body, name="adam_small", out_shape=shapes * 3, compiler_params=_params(),
    )(*ws, *gs, *ms, *vs)
    return outs[:n], outs[n:2 * n], outs[2 * n:]


def _block_diag(w):
    h, hd, _ = w.shape
    per = LANES // hd
    eye = jnp.eye(per, dtype=w.dtype)
    w5 = w.reshape(h // per, per, hd, 1, hd) * eye[None, :, None, :, None]
    return w5.reshape(h // per, LANES, LANES)


def _block_diag_grad(g, h, hd):
    per = LANES // hd
    g5 = g.reshape(h // per, per, hd, per, hd)
    return jnp.stack([g5[:, a, :, a, :] for a in range(per)], axis=1).reshape(h, hd, hd)


def kernel(x, c, w_ada, b_ada, g_mix, w_in, conv_w_sc, conv_w_lru, conv_b_lru, w_rg_a, b_rg_a, w_rg_x, b_rg_x, lru_lambda, w_out, g_mlp, w_up, w_down, g_final, loss_target, m_w_ada, m_b_ada, m_g_mix, m_w_in, m_conv_w_sc, m_conv_w_lru, m_conv_b_lru, m_w_rg_a, m_b_rg_a, m_w_rg_x, m_b_rg_x, m_lru_lambda, m_w_out, m_g_mlp, m_w_up, m_w_down, m_g_final, v_w_ada, v_b_ada, v_g_mix, v_w_in, v_conv_w_sc, v_conv_w_lru, v_conv_b_lru, v_w_rg_a, v_b_rg_a, v_w_rg_x, v_b_rg_x, v_lru_lambda, v_w_out, v_g_mlp, v_w_up, v_w_down, v_g_final):
    s, d = x.shape[1], x.shape[2]
    width = conv_b_lru.shape[1]
    heads, hd = w_rg_a.shape[1], w_rg_a.shape[2]
    f = w_down.shape[1] * N_DEV
    n_ada = w_ada.shape[2]
    csh = conv_w_sc.shape[2]
    me = 4 * lax.axis_index("x") + 2 * lax.axis_index("y") + lax.axis_index("c")
    tm = min(512, s)
    tm_mlp = min(1024, s)
    tk = 512

    x2d = x[0]
    tgt = loss_target[0]

    pay = jnp.zeros((SUBLANES, d), F32)
    pay = pay.at[0:1, :].set(c)
    pay = pay.at[1:4, 0:csh].set(conv_w_sc[0])
    pay = pay.at[4:8, 0:csh].set(conv_w_lru[0])
    w_in_t_sh = w_in[0].T.astype(BF16)
    w_up_t_sh = w_up[0].T.astype(BF16)
    w_out_sh = w_out[0].astype(BF16)
    w_down_sh = w_down[0].astype(BF16)
    pay_all, w_in_t = _gather2("gather_in", [pay, w_in_t_sh])
    w_in_t = w_in_t.reshape(-1, d)
    c_all = pay_all[:, 0, :]
    conv_sc = pay_all[:, 1:4, 0:csh].transpose(1, 0, 2).reshape(3, width)
    conv_lru = pay_all[:, 4:8, 0:csh].transpose(1, 0, 2).reshape(4, width)

    b_ada_sh = lax.dynamic_slice(b_ada, (0, me * n_ada), (1, n_ada))
    mod_cols, c_act = _ada_fwd(c_all, w_ada[0], b_ada_sh)
    (mod_rows,) = _exchange("scatter_mod", [], [mod_cols.reshape(N_DEV, 1, n_ada)])
    mod6 = jnp.zeros((SUBLANES, d), F32).at[0:6, :].set(mod_rows.reshape(6, d))

    wa_bd = _block_diag(w_rg_a[0]).astype(BF16)
    wx_bd = _block_diag(w_rg_x[0]).astype(BF16)
    ba = b_rg_a.reshape(1, width)
    bx = b_rg_x.reshape(1, width)
    g_fin = g_final.reshape(1, d)

    (hn1, proj), (w_out_g, w_up_g) = _mix_in_fwd(
        x2d, mod6, g_mix, w_in_t, tm, rider=_ride_gather_ici([w_out_sh, w_up_t_sh]))
    (ymix, h_all), (w_out_g, w_up_g, w_down_g) = _mixer_fwd(
        proj, conv_sc, conv_lru, conv_b_lru, wa_bd, wx_bd, ba, bx, lru_lambda, width,
        rider=_merge_riders(_ride_gather_d2d([w_out_g, w_up_g]), _ride_gather_ici([w_down_sh])))
    w_out_b = w_out_g.reshape(-1, d)
    (mix, x2, hn2), (w_down_g,) = _mix_out_fwd(ymix, x2d, w_out_b, mod6, g_mlp, tm,
                                               rider=_ride_gather_d2d([w_down_g]))
    w_up_t = w_up_g.reshape(-1, d)
    w_down_b = w_down_g.reshape(-1, d)
    z, dx3, dyb, st_fin = _mlp_fwd_loss(hn2, w_up_t, w_down_b, x2, tgt, mod6, g_fin, tm_mlp, tk)
    loss = lax.psum((0.5 / d) * jnp.sum(st_fin[2]), ("x", "y", "c"))

    core_chip = jnp.stack([lax.axis_index("c"), 2 * lax.axis_index("x") + lax.axis_index("y")]).astype(jnp.int32)
    dz, dhn2 = _mlp_bwd_dx(dyb, z, w_down_b, w_up_t, tm_mlp, tk)
    g_down, g_up_t = _mlp_bwd_dw(z, dz, dyb, hn2, tm_mlp, tk)
    g_up4, g_down4 = g_up_t.reshape(4, 2, -1, d), g_down.reshape(4, 2, -1, d)
    (dx2, dymix, g_out, st_out), (h_up, h_down) = _mix_out_bwd(
        dhn2, x2, dx3, mix, ymix, w_out_b, mod6, g_mlp, tm, rider=_ride_pair_swap([g_up4, g_down4]))
    sb_up, own_up = _pair_sum(g_up4, h_up, core_chip, 256, "pair_sum_w_up")
    sb_down, own_down = _pair_sum(g_down4, h_down, core_chip, 256, "pair_sum_w_down")
    g_out4 = g_out.reshape(4, 2, -1, d)
    (dproj, g_small, g_wa, g_wx), (p_up, p_down, h_out) = _mixer_bwd(
        proj, dymix, h_all, conv_sc, conv_lru, conv_b_lru, wa_bd, wx_bd, ba, bx, lru_lambda, width,
        rider=_merge_riders(_ride_chip_exchange([sb_up, sb_down]), _ride_pair_swap([g_out4])))
    sb_out, own_out = _pair_sum(g_out4, h_out, core_chip, g_out4.shape[2], "pair_sum_w_out")
    (grad_x, st_in), _ = _mix_in_bwd_dx(dproj, x2d, dx2, w_in_t, mod6, g_mix, tm)

    zrow = jnp.zeros((1, d), F32)
    small = jnp.concatenate([
        st_in[0:2], st_out[3:4], st_out[0:2], st_fin[1:2],
        st_in[2:3], st_out[2:3], st_fin[0:1],
        jnp.concatenate([g_small[7:8], g_small[10:11]], axis=1),
        jnp.concatenate([g_small[8:9], g_small[9:10]], axis=1),
        jnp.concatenate([jnp.concatenate([g_small[0:3], jnp.zeros((1, width), F32)], axis=0), g_small[3:7]], axis=1),
        zrow,
        _block_diag_grad(g_wa, heads, hd).reshape(-1, d),
        _block_diag_grad(g_wx, heads, hd).reshape(-1, d),
    ], axis=0)

    (g_in_t,), (p_out, small_all) = _mix_in_bwd_dw(
        dproj, hn1, tm_mlp, 512, rider=_merge_riders(_ride_chip_exchange([sb_out]), _ride_gather_direct([small])))
    g_in4 = g_in_t.reshape(4, 2, -1, d)
    (h_in,) = _comm("swap_w_in", _ride_pair_swap([g_in4]))
    sb_in, own_in = _pair_sum(g_in4, h_in, core_chip, g_in4.shape[2], "pair_sum_w_in")
    (p_in,) = _comm("exchange_w_in", _ride_chip_exchange([sb_in]))

    gs_in = _sum4(own_in, p_in, own_in.shape[0], "sum_w_in").T
    gs_up = _sum4(own_up, p_up, 256, "sum_w_up").T
    gs_out = _sum4(own_out, p_out, own_out.shape[0], "sum_w_out")
    gs_down = _sum4(own_down, p_down, 256, "sum_w_down")
    ad_in = _adam_rows(w_in[0], gs_in, m_w_in[0], v_w_in[0], 256, "adam_w_in")
    ad_up = _adam_rows(w_up[0], gs_up, m_w_up[0], v_w_up[0], 256, "adam_w_up")
    ad_out = _adam_rows(w_out[0], gs_out, m_w_out[0], v_w_out[0], w_out.shape[1], "adam_w_out")
    ad_down = _adam_rows(w_down[0], gs_down, m_w_down[0], v_w_down[0], 256, "adam_w_down")

    gsum = _sum8(small_all, SMALL_ROWS, "sum_small")
    dmod_cols = lax.dynamic_slice(small_all[:, 0:6, :].reshape(N_DEV, 6 * d), (0, me * n_ada), (N_DEV, n_ada))
    g_ada, d_ada, nm_ada, nv_ada = _ada_bwd_adam(c_act[:, :, None], dmod_cols, w_ada[0], m_w_ada[0], v_w_ada[0], 256)

    g_conv = lax.dynamic_slice(gsum[11:15, 0:width], (0, me * csh), (4, csh))
    g_conv_l = lax.dynamic_slice(gsum[11:15, width:2 * width], (0, me * csh), (4, csh))
    small_g = [
        gsum[0:6].reshape(1, 6 * d),
        gsum[6:7],
        g_conv[0:3].reshape(1, 3, csh),
        g_conv_l.reshape(1, 4, csh),
        gsum[9:10, 0:width],
        gsum[16:48].reshape(1, heads, hd, hd),
        gsum[10:11, 0:width].reshape(1, heads, hd),
        gsum[48:80].reshape(1, heads, hd, hd),
        gsum[10:11, width:].reshape(1, heads, hd),
        gsum[9:10, width:],
        gsum[7:8],
        gsum[8],
    ]
    small_w = [b_ada, g_mix, conv_w_sc, conv_w_lru, conv_b_lru, w_rg_a, b_rg_a, w_rg_x, b_rg_x, lru_lambda, g_mlp, g_final]
    small_m = [m_b_ada, m_g_mix, m_conv_w_sc, m_conv_w_lru, m_conv_b_lru, m_w_rg_a, m_b_rg_a, m_w_rg_x, m_b_rg_x,
               m_lru_lambda, m_g_mlp, m_g_final]
    small_v = [v_b_ada, v_g_mix, v_conv_w_sc, v_conv_w_lru, v_conv_b_lru, v_w_rg_a, v_b_rg_a, v_w_rg_x, v_b_rg_x,
               v_lru_lambda, v_g_mlp, v_g_final]
    sd, snm, snv = _adam_small(small_w, small_g, small_m, small_v)

    def order(ada, w_in_, w_out_, w_up_, w_down_, sm):
        return [ada[None], sm[0], sm[1], w_in_[None], sm[2], sm[3], sm[4], sm[5], sm[6], sm[7], sm[8], sm[9],
                w_out_[None], sm[10], w_up_[None], w_down_[None], sm[11]]

    grads = order(g_ada, gs_in, gs_out, gs_up, gs_down, small_g)
    deltas = order(d_ada, ad_in[0], ad_out[0], ad_up[0], ad_down[0], sd)
    new_m = order(nm_ada, ad_in[1], ad_out[1], ad_up[1], ad_down[1], snm)
    new_v = order(nv_ada, ad_in[2], ad_out[2], ad_up[2], ad_down[2], snv)
    return (loss, grad_x[None], *grads, *deltas, *new_m, *new_v)
```

```python
import functools

import jax
import jax.numpy as jnp
from jax import lax
from jax.experimental import pallas as pl
from jax.experimental.pallas import tpu as pltpu

F32 = jnp.float32
BF16 = jnp.bfloat16
N_DEV = 8
EPS = 1e-6
RG_C = 8.0
GELU_K0 = 0.7978845608028654
GELU_K1 = 0.044715
ADAM_LR = 0.001
ADAM_B1 = 0.9
ADAM_B2 = 0.999
ADAM_EPS = 1e-08
ADAM_WD = 0.01
ADAM_STEP = 10
LANES = 128
SUBLANES = 8
VMEM_LIMIT = 52 * 1024 * 1024
MIX_ROWS = 256
SMALL_ROWS = 80

MESH = pl.DeviceIdType.MESH
ANY = pl.BlockSpec(memory_space=pl.ANY)
NN = ((1,), (0,))
NT = ((1,), (1,))
TN = ((0,), (0,))


def _dot(a, b, dims):
    return lax.dot_general(a, b, (dims, ((), ())), preferred_element_type=F32)


def _params(sem=None):
    return pltpu.CompilerParams(dimension_semantics=sem, vmem_limit_bytes=VMEM_LIMIT)


def _full(shape):
    nd = len(shape)
    return pl.BlockSpec(shape, lambda *_: (0,) * nd)


def _exchange(name, gathers, scatters):
    n_g = len(gathers)
    arrs = list(gathers) + list(scatters)
    n = len(arrs)
    out_shape = [jax.ShapeDtypeStruct((N_DEV,) + a.shape, a.dtype) for a in gathers]
    out_shape += [jax.ShapeDtypeStruct(a.shape, a.dtype) for a in scatters]

    def body(*refs):
        ins, outs = refs[:n], refs[n:2 * n]
        send_sems, recv_sems, local_sems = refs[2 * n:]
        x, y, c = lax.axis_index("x"), lax.axis_index("y"), lax.axis_index("c")
        me = 4 * x + 2 * y + c

        def src(a, dev):
            return ins[a] if a < n_g else ins[a].at[dev]

        def peer_of(k):
            px = 1 - x if (k >> 2) & 1 else x
            py = 1 - y if (k >> 1) & 1 else y
            pc = 1 - c if k & 1 else c
            return (px, py, pc), 4 * px + 2 * py + pc

        local = [pltpu.make_async_copy(src(a, me), outs[a].at[me], local_sems.at[a]) for a in range(n)]
        for cp in local:
            cp.start()
        sends = []
        for k in range(1, N_DEV):
            peer, pidx = peer_of(k)
            for a in range(n):
                cp = pltpu.make_async_remote_copy(
                    src_ref=src(a, pidx), dst_ref=outs[a].at[me],
                    send_sem=send_sems.at[a * (N_DEV - 1) + k - 1], recv_sem=recv_sems.at[a * (N_DEV - 1) + k - 1],
                    device_id=peer, device_id_type=MESH)
                cp.start()
                sends.append(cp)
        for k in range(1, N_DEV):
            peer, pidx = peer_of(k)
            for a in range(n):
                pltpu.make_async_remote_copy(
                    src_ref=src(a, pidx), dst_ref=outs[a].at[pidx],
                    send_sem=send_sems.at[a * (N_DEV - 1) + k - 1], recv_sem=recv_sems.at[a * (N_DEV - 1) + k - 1],
                    device_id=peer, device_id_type=MESH).wait_recv()
        for cp in sends:
            cp.wait_send()
        for cp in local:
            cp.wait()

    return pl.pallas_call(
        body, name=name, out_shape=out_shape,
        in_specs=[ANY] * n, out_specs=[ANY] * n,
        scratch_shapes=[pltpu.SemaphoreType.DMA((n * (N_DEV - 1),)),
                        pltpu.SemaphoreType.DMA((n * (N_DEV - 1),)),
                        pltpu.SemaphoreType.DMA((n,))],
    )(*arrs)


def _gather2(name, arrs):
    n = len(arrs)
    per = 7
    out_shape = [jax.ShapeDtypeStruct((N_DEV,) + a.shape, a.dtype) for a in arrs]

    def body(*refs):
        ins, outs = refs[:n], refs[n:2 * n]
        send_sems, recv_sems, local_sems = refs[2 * n:]
        x, y, c = lax.axis_index("x"), lax.axis_index("y"), lax.axis_index("c")
        sib = (x, y, 1 - c)
        chips = [(1 - x, y), (x, 1 - y), (1 - x, 1 - y)]

        def slot(a, px, py, pc):
            return outs[a].at[4 * px + 2 * py + pc]

        def copy(a, k, block, to, src=None):
            return pltpu.make_async_remote_copy(
                src_ref=slot(a, *block) if src is None else src, dst_ref=slot(a, *block),
                send_sem=send_sems.at[a * per + k], recv_sem=recv_sems.at[a * per + k],
                device_id=to, device_id_type=MESH)

        local = [pltpu.make_async_copy(ins[a], slot(a, x, y, c), local_sems.at[a]) for a in range(n)]
        for cp in local:
            cp.start()
        first = []
        for a in range(n):
            first += [copy(a, 1 + j, (x, y, c), (*chip, c), src=ins[a]) for j, chip in enumerate(chips)]
        for a in range(n):
            first.append(copy(a, 0, (x, y, c), sib, src=ins[a]))
        for cp in first:
            cp.start()
        passed = []
        for a in range(n):
            for j, chip in enumerate(chips):
                copy(a, 1 + j, (*chip, c), (x, y, c)).wait_recv()
                cp = copy(a, 4 + j, (*chip, c), sib)
                cp.start()
                passed.append(cp)
        for a in range(n):
            copy(a, 0, sib, (x, y, c)).wait_recv()
            for j, chip in enumerate(chips):
                copy(a, 4 + j, (*chip, 1 - c), (x, y, c)).wait_recv()
        for cp in first + passed:
            cp.wait_send()
        for cp in local:
            cp.wait()

    return pl.pallas_call(
        body, name=name, out_shape=out_shape,
        in_specs=[ANY] * n, out_specs=[ANY] * n,
        scratch_shapes=[pltpu.SemaphoreType.DMA((n * per,)), pltpu.SemaphoreType.DMA((n * per,)),
                        pltpu.SemaphoreType.DMA((n,))],
    )(*arrs)


def _pair_swap(name, arrs):
    n = len(arrs)
    out_shape = [jax.ShapeDtypeStruct((4,) + a.shape[2:], a.dtype) for a in arrs]

    def body(*refs):
        ins, outs = refs[:n], refs[n:2 * n]
        send_sems, recv_sems = refs[2 * n:]
        x, y, c = lax.axis_index("x"), lax.axis_index("y"), lax.axis_index("c")

        def copy(a, q):
            return pltpu.make_async_remote_copy(
                src_ref=ins[a].at[q, 1 - c], dst_ref=outs[a].at[q],
                send_sem=send_sems.at[a * 4 + q], recv_sem=recv_sems.at[a * 4 + q],
                device_id=(x, y, 1 - c), device_id_type=MESH)

        cps = [copy(a, q) for a in range(n) for q in range(4)]
        for cp in cps:
            cp.start()
        for cp in cps:
            cp.wait_recv()
        for cp in cps:
            cp.wait_send()

    return pl.pallas_call(
        body, name=name, out_shape=out_shape,
        in_specs=[ANY] * n, out_specs=[ANY] * n,
        scratch_shapes=[pltpu.SemaphoreType.DMA((n * 4,)), pltpu.SemaphoreType.DMA((n * 4,))],
    )(*arrs)


def _chip_exchange(name, arrs):
    n = len(arrs)
    out_shape = [jax.ShapeDtypeStruct((3,) + a.shape[1:], a.dtype) for a in arrs]

    def body(*refs):
        ins, outs = refs[:n], refs[n:2 * n]
        send_sems, recv_sems = refs[2 * n:]
        x, y, c = lax.axis_index("x"), lax.axis_index("y"), lax.axis_index("c")

        def copy(a, k):
            px = 1 - x if (k >> 1) & 1 else x
            py = 1 - y if k & 1 else y
            return pltpu.make_async_remote_copy(
                src_ref=ins[a].at[2 * px + py], dst_ref=outs[a].at[k - 1],
                send_sem=send_sems.at[a * 3 + k - 1], recv_sem=recv_sems.at[a * 3 + k - 1],
                device_id=(px, py, c), device_id_type=MESH)

        cps = [copy(a, k) for a in range(n) for k in (1, 2, 3)]
        for cp in cps:
            cp.start()
        for cp in cps:
            cp.wait_recv()
        for cp in cps:
            cp.wait_send()

    return pl.pallas_call(
        body, name=name, out_shape=out_shape,
        in_specs=[ANY] * n, out_specs=[ANY] * n,
        scratch_shapes=[pltpu.SemaphoreType.DMA((n * 3,)), pltpu.SemaphoreType.DMA((n * 3,))],
    )(*arrs)


class _Rider:
    def __init__(self, arrays, out_shapes, n_sems, build, aliases=None):
        self.arrays, self.out_shapes, self.n_sems, self.build = list(arrays), list(out_shapes), n_sems, build
        self.aliases = dict(aliases or {})


def _merge_riders(r1, r2):
    n1i, n1o, n1s = len(r1.arrays), len(r1.out_shapes), r1.n_sems

    def build(ins, outs, send_sems, recv_sems):
        a = r1.build(ins[:n1i], outs[:n1o], send_sems.at[pl.ds(0, n1s)], recv_sems.at[pl.ds(0, n1s)])
        b = r2.build(ins[n1i:], outs[n1o:], send_sems.at[pl.ds(n1s, r2.n_sems)], recv_sems.at[pl.ds(n1s, r2.n_sems)])
        return tuple(p + q for p, q in zip(a, b))

    aliases = dict(r1.aliases)
    aliases.update({k + n1i: v + n1o for k, v in r2.aliases.items()})
    return _Rider(r1.arrays + r2.arrays, r1.out_shapes + r2.out_shapes, n1s + r2.n_sems, build, aliases)


def _place():
    x, y, c = lax.axis_index("x"), lax.axis_index("y"), lax.axis_index("c")
    chips = [(1 - x, y), (x, 1 - y), (1 - x, 1 - y)]
    return x, y, c, chips


def _ride_gather_ici(arrs):
    n = len(arrs)

    def build(ins, outs, send_sems, recv_sems):
        x, y, c, chips = _place()
        peers = [(*chip, c) for chip in chips] + [(x, y, 1 - c)]
        me = 4 * x + 2 * y + c
        local = [pltpu.make_async_copy(ins[a], outs[a].at[me], send_sems.at[a * 5 + 4]) for a in range(n)]
        sends, recvs = [], []
        for a in range(n):
            for j, (px, py, pc) in enumerate(peers):
                sends.append(pltpu.make_async_remote_copy(
                    src_ref=ins[a], dst_ref=outs[a].at[me], send_sem=send_sems.at[a * 5 + j],
                    recv_sem=recv_sems.at[a * 5 + j], device_id=(px, py, pc), device_id_type=MESH))
                recvs.append(pltpu.make_async_remote_copy(
                    src_ref=ins[a], dst_ref=outs[a].at[4 * px + 2 * py + pc], send_sem=send_sems.at[a * 5 + j],
                    recv_sem=recv_sems.at[a * 5 + j], device_id=(px, py, pc), device_id_type=MESH))
        return local, sends, recvs

    shapes = [jax.ShapeDtypeStruct((N_DEV,) + a.shape, a.dtype) for a in arrs]
    return _Rider(arrs, shapes, n * 5, build)


def _ride_gather_direct(arrs):
    n = len(arrs)

    def build(ins, outs, send_sems, recv_sems):
        x, y, c, _ = _place()
        me = 4 * x + 2 * y + c
        local = [pltpu.make_async_copy(ins[a], outs[a].at[me], send_sems.at[a * N_DEV + 7]) for a in range(n)]
        sends, recvs = [], []
        for a in range(n):
            for k in range(1, N_DEV):
                px = 1 - x if (k >> 2) & 1 else x
                py = 1 - y if (k >> 1) & 1 else y
                pc = 1 - c if k & 1 else c
                sem = a * N_DEV + k - 1
                sends.append(pltpu.make_async_remote_copy(
                    src_ref=ins[a], dst_ref=outs[a].at[me], send_sem=send_sems.at[sem], recv_sem=recv_sems.at[sem],
                    device_id=(px, py, pc), device_id_type=MESH))
                recvs.append(pltpu.make_async_remote_copy(
                    src_ref=ins[a], dst_ref=outs[a].at[4 * px + 2 * py + pc], send_sem=send_sems.at[sem],
                    recv_sem=recv_sems.at[sem], device_id=(px, py, pc), device_id_type=MESH))
        return local, sends, recvs

    shapes = [jax.ShapeDtypeStruct((N_DEV,) + a.shape, a.dtype) for a in arrs]
    return _Rider(arrs, shapes, n * N_DEV, build)


def _ride_gather_d2d(gathered):
    n = len(gathered)

    def build(ins, outs, send_sems, recv_sems):
        x, y, c, chips = _place()
        sends, recvs = [], []
        for a in range(n):
            for j, (px, py) in enumerate(chips):
                mine = outs[a].at[4 * px + 2 * py + c]
                theirs = outs[a].at[4 * px + 2 * py + 1 - c]
                sends.append(pltpu.make_async_remote_copy(
                    src_ref=mine, dst_ref=mine, send_sem=send_sems.at[a * 3 + j], recv_sem=recv_sems.at[a * 3 + j],
                    device_id=(x, y, 1 - c), device_id_type=MESH))
                recvs.append(pltpu.make_async_remote_copy(
                    src_ref=mine, dst_ref=theirs, send_sem=send_sems.at[a * 3 + j], recv_sem=recv_sems.at[a * 3 + j],
                    device_id=(x, y, 1 - c), device_id_type=MESH))
        return [], sends, recvs

    shapes = [jax.ShapeDtypeStruct(a.shape, a.dtype) for a in gathered]
    return _Rider(gathered, shapes, n * 3, build, aliases={a: a for a in range(n)})


def _ride_pair_swap(arrs):
    n = len(arrs)

    def build(ins, outs, send_sems, recv_sems):
        x, y, c, _ = _place()
        cps = [pltpu.make_async_remote_copy(
            src_ref=ins[a].at[q, 1 - c], dst_ref=outs[a].at[q], send_sem=send_sems.at[a * 4 + q],
            recv_sem=recv_sems.at[a * 4 + q], device_id=(x, y, 1 - c), device_id_type=MESH)
            for a in range(n) for q in range(4)]
        return [], cps, cps

    shapes = [jax.ShapeDtypeStruct((4,) + a.shape[2:], a.dtype) for a in arrs]
    return _Rider(arrs, shapes, n * 4, build)


def _ride_chip_exchange(arrs):
    n = len(arrs)

    def build(ins, outs, send_sems, recv_sems):
        x, y, c, _ = _place()
        cps = []
        for a in range(n):
            for k in (1, 2, 3):
                px = 1 - x if (k >> 1) & 1 else x
                py = 1 - y if k & 1 else y
                cps.append(pltpu.make_async_remote_copy(
                    src_ref=ins[a].at[2 * px + py], dst_ref=outs[a].at[k - 1], send_sem=send_sems.at[a * 3 + k - 1],
                    recv_sem=recv_sems.at[a * 3 + k - 1], device_id=(px, py, c), device_id_type=MESH))
        return [], cps, cps

    shapes = [jax.ShapeDtypeStruct((3,) + a.shape[1:], a.dtype) for a in arrs]
    return _Rider(arrs, shapes, n * 3, build)


def _call(body, name, grid, in_specs, out_specs, out_shape, args, scratch=(), rider=None):
    n_in, n_out, n_scr = len(in_specs), len(out_specs), len(scratch)
    sem = ("arbitrary",) * len(grid)
    if rider is None:
        outs = pl.pallas_call(
            body, name=name, grid=grid, in_specs=in_specs, out_specs=out_specs, out_shape=out_shape,
            scratch_shapes=list(scratch), compiler_params=_params(sem))(*args)
        return outs, []
    ri, ro = len(rider.arrays), len(rider.out_shapes)

    def riding(*refs):
        ins, r_ins = refs[:n_in], refs[n_in:n_in + ri]
        outs = refs[n_in + ri:n_in + ri + n_out]
        r_outs = refs[n_in + ri + n_out:n_in + ri + n_out + ro]
        scr = refs[n_in + ri + n_out + ro:n_in + ri + n_out + ro + n_scr]
        send_sems, recv_sems = refs[-2:]
        first = functools.reduce(jnp.logical_and, [pl.program_id(k) == 0 for k in range(len(grid))])
        last = functools.reduce(jnp.logical_and, [pl.program_id(k) == grid[k] - 1 for k in range(len(grid))])

        @pl.when(first)
        def _():
            local, sends, _ = rider.build(r_ins, r_outs, send_sems, recv_sems)
            for cp in local + sends:
                cp.start()

        body(*ins, *outs, *scr)

        @pl.when(last)
        def _():
            local, sends, recvs = rider.build(r_ins, r_outs, send_sems, recv_sems)
            for cp in recvs:
                cp.wait_recv()
            for cp in sends:
                cp.wait_send()
            for cp in local:
                cp.wait()

    outs = pl.pallas_call(
        riding, name=name, grid=grid,
        in_specs=list(in_specs) + [ANY] * ri, out_specs=list(out_specs) + [ANY] * ro,
        out_shape=list(out_shape) + rider.out_shapes,
        scratch_shapes=list(scratch) + [pltpu.SemaphoreType.DMA((rider.n_sems,)), pltpu.SemaphoreType.DMA((rider.n_sems,))],
        input_output_aliases={n_in + k: n_out + v for k, v in rider.aliases.items()},
        compiler_params=_params(sem))(*args, *rider.arrays)
    return outs[:n_out], outs[n_out:]


def _comm(name, rider):
    def body(dummy_ref, out_ref):
        out_ref[...] = dummy_ref[...]

    dummy = jnp.zeros((SUBLANES, LANES), F32)
    spec = pl.BlockSpec((SUBLANES, LANES), lambda i: (0, 0))
    _, r_outs = _call(body, name, (1,), [spec], [spec], [jax.ShapeDtypeStruct(dummy.shape, F32)], [dummy], rider=rider)
    return r_outs


def _ada_fwd(c_all, w_ada_sh, b_ada_sh):
    nb, d = c_all.shape
    ncol = w_ada_sh.shape[1]

    def body(c_ref, w_ref, b_ref, mod_ref, cact_ref):
        cc = c_ref[...]
        ca = cc * jax.nn.sigmoid(cc)
        cact_ref[...] = ca
        mod_ref[...] = _dot(ca.astype(BF16), w_ref[...].astype(BF16), NN) + b_ref[...]

    return pl.pallas_call(
        body, name="ada_fwd",
        out_shape=[jax.ShapeDtypeStruct((nb, ncol), F32), jax.ShapeDtypeStruct((nb, d), F32)],
        compiler_params=_params(),
    )(c_all, w_ada_sh, b_ada_sh)


def _rms(xv):
    rstd = lax.rsqrt(jnp.mean(xv * xv, axis=-1, keepdims=True) + EPS)
    return xv * rstd, rstd


def _rms_bwd(dxhat, xhat, rstd):
    return rstd * (dxhat - xhat * jnp.mean(dxhat * xhat, axis=-1, keepdims=True))


def _colsum(v):
    return jnp.sum(v, axis=0, keepdims=True)


def _expm1(v):
    series = v * (1.0 + v * (0.5 + v * (1.0 / 6.0 + v * (1.0 / 24.0 + v * (1.0 / 120.0 + v * (1.0 / 720.0))))))
    return jnp.where(jnp.abs(v) < 0.3, series, jnp.exp(v) - 1.0)


def _softplus(v):
    return jnp.maximum(v, 0.0) + jnp.log1p(jnp.exp(-jnp.abs(v)))


def _gelu(v):
    t = jnp.tanh(GELU_K0 * (v + GELU_K1 * v * v * v))
    return 0.5 * v * (1.0 + t), t


def _dgelu(v, t):
    return 0.5 * (1.0 + t) + 0.5 * v * (1.0 - t * t) * GELU_K0 * (1.0 + 3.0 * GELU_K1 * v * v)


def _shift_down(v, k, prev8):
    r = pltpu.roll(v, k, 0)
    pr = pltpu.roll(prev8, k, 0)
    row8 = lax.broadcasted_iota(jnp.int32, prev8.shape, 0)
    top = jnp.where(row8 < k, pr, r[0:SUBLANES])
    return jnp.concatenate([top, r[SUBLANES:]], axis=0)


def _shift_up(v, k, next8):
    t = v.shape[0]
    r = pltpu.roll(v, t - k, 0)
    nr = pltpu.roll(next8, SUBLANES - k, 0)
    row8 = lax.broadcasted_iota(jnp.int32, next8.shape, 0)
    bot = jnp.where(row8 >= SUBLANES - k, nr, r[t - SUBLANES:t])
    return jnp.concatenate([r[:t - SUBLANES], bot], axis=0)


def _scan_fwd(a, b, h0):
    t = a.shape[0]
    row = lax.broadcasted_iota(jnp.int32, a.shape, 0)
    s = 1
    while s < t:
        a_sh = pltpu.roll(a, s, 0)
        b_sh = pltpu.roll(b, s, 0)
        m = row >= s
        b = jnp.where(m, a * b_sh + b, b)
        a = jnp.where(m, a * a_sh, a)
        s *= 2
    return b + a * h0


def _scan_rev(m, b, g_next):
    t = m.shape[0]
    row = lax.broadcasted_iota(jnp.int32, m.shape, 0)
    s = 1
    while s < t:
        m_sh = pltpu.roll(m, t - s, 0)
        b_sh = pltpu.roll(b, t - s, 0)
        msk = row < t - s
        b = jnp.where(msk, m * b_sh + b, b)
        m = jnp.where(msk, m * m_sh, m)
        s *= 2
    return b + m * g_next


def _lru_gates(u, wa, wx, ba, bx, sp):
    ub = u.astype(BF16)
    r = jax.nn.sigmoid(_dot(ub, wa, NN) + ba)
    i = jax.nn.sigmoid(_dot(ub, wx, NN) + bx)
    log_a = (-RG_C * r) * sp
    a = jnp.exp(log_a)
    mult = jnp.sqrt(-_expm1(2.0 * log_a))
    return ub, r, i, a, mult


def _conv3(p, pp, w_ref, lo):
    p1 = _shift_down(p, 1, pp)
    p2 = _shift_down(p, 2, pp)
    q = (w_ref[0:1, lo:lo + LANES] * p2 + w_ref[1:2, lo:lo + LANES] * p1) + w_ref[2:3, lo:lo + LANES] * p
    return q, p1, p2


def _conv4(xv, xp, w_ref, b_ref, lo):
    x1 = _shift_down(xv, 1, xp)
    x2 = _shift_down(xv, 2, xp)
    x3 = _shift_down(xv, 3, xp)
    u = (((w_ref[0:1, lo:lo + LANES] * x3 + w_ref[1:2, lo:lo + LANES] * x2) + w_ref[2:3, lo:lo + LANES] * x1)
         + w_ref[3:4, lo:lo + LANES] * xv) + b_ref[:, lo:lo + LANES]
    return u, x1, x2, x3


def _mix_in_fwd(x2d, mod6, g_mix, w_in_t, tm, rider=None):
    s, d = x2d.shape
    din = w_in_t.shape[0]

    def body(x_ref, mod_ref, g_ref, w_ref, hn_ref, proj_ref):
        xhat, _ = _rms(x_ref[...])
        hn = ((xhat * g_ref[...]) * (1.0 + mod_ref[1:2, :]) + mod_ref[0:1, :]).astype(BF16)
        hn_ref[...] = hn
        proj_ref[...] = _dot(hn, w_ref[...], NT)

    return _call(
        body, "mix_in_fwd", (s // tm,),
        [pl.BlockSpec((tm, d), lambda i: (i, 0)), _full(mod6.shape), _full(g_mix.shape), _full(w_in_t.shape)],
        [pl.BlockSpec((tm, d), lambda i: (i, 0)), pl.BlockSpec((tm, din), lambda i: (i, 0))],
        [jax.ShapeDtypeStruct((s, d), BF16), jax.ShapeDtypeStruct((s, din), F32)],
        [x2d, mod6, g_mix, w_in_t], rider=rider)


def _mixer_fwd(proj, conv_sc, conv_lru, conv_b, wa_bd, wx_bd, ba, bx, lam, width, rider=None):
    s, din = proj.shape
    t = min(MIX_ROWS, s)
    nblk = width // LANES
    hb = t // SUBLANES

    def body(proj_ref, projp_ref, wsc_ref, wlru_ref, blru_ref, wa_ref, wx_ref, ba_ref, bx_ref, lam_ref,
             ymix_ref, h_ref, hc_ref):
        i = pl.program_id(0)

        @pl.when(i == 0)
        def _():
            hc_ref[...] = jnp.zeros_like(hc_ref)

        has_prev = i > 0
        for j in range(nblk):
            lo = j * LANES

            def col(p, ref=proj_ref):
                return ref[:, p * width + lo:p * width + lo + LANES]

            def prev(p):
                return jnp.where(has_prev, col(p, projp_ref), 0.0)

            p = col(1) * col(2)
            q, _, _ = _conv3(p, prev(1) * prev(2), wsc_ref, lo)
            ymix_ref[:, lo:lo + LANES] = (col(0) * q).astype(BF16)

            u, _, _, _ = _conv4(col(4), prev(4), wlru_ref, blru_ref, lo)
            sp = _softplus(-lam_ref[:, lo:lo + LANES])
            _, r, ig, a, mult = _lru_gates(u, wa_ref[j], wx_ref[j], ba_ref[:, lo:lo + LANES], bx_ref[:, lo:lo + LANES], sp)
            h = _scan_fwd(a, mult * (ig * u), hc_ref[0:1, lo:lo + LANES])
            h_ref[:, lo:lo + LANES] = h
            hc_ref[0:1, lo:lo + LANES] = h[t - 1:t, :]
            gel, _ = _gelu(col(3))
            ymix_ref[:, width + lo:width + lo + LANES] = (gel * h).astype(BF16)

    small = [conv_sc, conv_lru, conv_b, wa_bd, wx_bd, ba, bx, lam]
    return _call(
        body, "mixer_fwd", (s // t,),
        [pl.BlockSpec((t, din), lambda i: (i, 0)),
         pl.BlockSpec((SUBLANES, din), lambda i: (jnp.maximum(i * hb - 1, 0), 0))]
        + [_full(a.shape) for a in small],
        [pl.BlockSpec((t, 2 * width), lambda i: (i, 0)), pl.BlockSpec((t, width), lambda i: (i, 0))],
        [jax.ShapeDtypeStruct((s, 2 * width), BF16), jax.ShapeDtypeStruct((s, width), F32)],
        [proj, proj, *small], scratch=[pltpu.VMEM((SUBLANES, width), F32)], rider=rider)


def _mix_out_fwd(ymix, x2d, w_out, mod6, g_mlp, tm, rider=None):
    s, d = x2d.shape

    def body(y_ref, x_ref, w_ref, mod_ref, g_ref, mix_ref, x2_ref, hn_ref):
        mix = _dot(y_ref[...], w_ref[...], NN)
        mix_ref[...] = mix
        x2 = x_ref[...] + mod_ref[2:3, :] * mix
        x2_ref[...] = x2
        xhat, _ = _rms(x2)
        hn_ref[...] = ((xhat * g_ref[...]) * (1.0 + mod_ref[4:5, :]) + mod_ref[3:4, :]).astype(BF16)

    tile = pl.BlockSpec((tm, d), lambda i: (i, 0))
    return _call(
        body, "mix_out_fwd", (s // tm,),
        [tile, tile, _full(w_out.shape), _full(mod6.shape), _full(g_mlp.shape)],
        [tile, tile, tile],
        [jax.ShapeDtypeStruct((s, d), F32), jax.ShapeDtypeStruct((s, d), F32), jax.ShapeDtypeStruct((s, d), BF16)],
        [ymix, x2d, w_out, mod6, g_mlp], rider=rider)


def _mlp_fwd_loss(hn2, w_up_t, w_down, x2, target, mod6, g_final, tm, tk):
    s, d = hn2.shape
    f = w_up_t.shape[0]
    nk = f // tk

    def body(hn_ref, wu_ref, wd_ref, x2_ref, t_ref, mod_ref, g_ref, z_ref, dx3_ref, dyb_ref, st_ref, y_ref):
        i, k = pl.program_id(0), pl.program_id(1)

        @pl.when(jnp.logical_and(i == 0, k == 0))
        def _():
            st_ref[...] = jnp.zeros_like(st_ref)

        z = jnp.maximum(_dot(hn_ref[...], wu_ref[...], NT), 0.0)
        z_ref[...] = z.astype(BF16)
        part = _dot((z * z).astype(BF16), wd_ref[...], NN)

        @pl.when(k == 0)
        def _():
            y_ref[...] = part

        @pl.when(k > 0)
        def _():
            y_ref[...] += part

        @pl.when(k == nk - 1)
        def _():
            gate = mod_ref[5:6, :]
            yv = y_ref[...]
            xhat, rstd = _rms(x2_ref[...] + gate * yv)
            diff = xhat * g_ref[...] - t_ref[...]
            dyo = diff * (1.0 / d)
            dx3 = _rms_bwd(dyo * g_ref[...], xhat, rstd)
            dx3_ref[...] = dx3
            dyb_ref[...] = (gate * dx3).astype(BF16)
            st_ref[0:1, :] += _colsum(dyo * xhat)
            st_ref[1:2, :] += _colsum(dx3 * yv)
            st_ref[2:3, :] += _colsum(diff * diff)

    tile = pl.BlockSpec((tm, d), lambda i, k: (i, 0))
    wblk = pl.BlockSpec((tk, d), lambda i, k: (k, 0))
    return pl.pallas_call(
        body, name="mlp_fwd_loss", grid=(s // tm, nk),
        in_specs=[tile, wblk, wblk, tile, tile, _full(mod6.shape), _full(g_final.shape)],
        out_specs=[pl.BlockSpec((tm, tk), lambda i, k: (i, k)), tile, tile, _full((SUBLANES, d))],
        out_shape=[jax.ShapeDtypeStruct((s, f), BF16), jax.ShapeDtypeStruct((s, d), F32),
                   jax.ShapeDtypeStruct((s, d), BF16), jax.ShapeDtypeStruct((SUBLANES, d), F32)],
        scratch_shapes=[pltpu.VMEM((tm, d), F32)],
        compiler_params=_params(("arbitrary", "arbitrary")),
    )(hn2, w_up_t, w_down, x2, target, mod6, g_final)


def _mlp_bwd_dx(dyb, z, w_down, w_up_t, tm, tk):
    s, d = dyb.shape
    f = z.shape[1]

    def body(dy_ref, z_ref, wd_ref, wu_ref, dz_ref, dh_ref):
        k = pl.program_id(1)
        dz = ((2.0 * z_ref[...].astype(F32)) * _dot(dy_ref[...], wd_ref[...], NT)).astype(BF16)
        dz_ref[...] = dz
        part = _dot(dz, wu_ref[...], NN)

        @pl.when(k == 0)
        def _():
            dh_ref[...] = part

        @pl.when(k > 0)
        def _():
            dh_ref[...] += part

    return pl.pallas_call(
        body, name="mlp_bwd_dx", grid=(s // tm, f // tk),
        in_specs=[pl.BlockSpec((tm, d), lambda i, k: (i, 0)), pl.BlockSpec((tm, tk), lambda i, k: (i, k)),
                  pl.BlockSpec((tk, d), lambda i, k: (k, 0)), pl.BlockSpec((tk, d), lambda i, k: (k, 0))],
        out_specs=[pl.BlockSpec((tm, tk), lambda i, k: (i, k)), pl.BlockSpec((tm, d), lambda i, k: (i, 0))],
        out_shape=[jax.ShapeDtypeStruct((s, f), BF16), jax.ShapeDtypeStruct((s, d), F32)],
        compiler_params=_params(("parallel", "arbitrary")),
    )(dyb, z, w_down, w_up_t)


def _mlp_bwd_dw(z, dz, dyb, hn2, tm, tk):
    s, d = dyb.shape
    f = z.shape[1]

    def body(z_ref, dz_ref, dy_ref, hn_ref, gd_ref, gu_ref):
        i = pl.program_id(1)

        @pl.when(i == 0)
        def _():
            gd_ref[...] = jnp.zeros_like(gd_ref)
            gu_ref[...] = jnp.zeros_like(gu_ref)

        zf = z_ref[...].astype(F32)
        gd_ref[...] += _dot((zf * zf).astype(BF16), dy_ref[...], TN)
        gu_ref[...] += _dot(dz_ref[...], hn_ref[...], TN)

    return pl.pallas_call(
        body, name="mlp_bwd_dw", grid=(f // tk, s // tm),
        in_specs=[pl.BlockSpec((tm, tk), lambda k, i: (i, k)), pl.BlockSpec((tm, tk), lambda k, i: (i, k)),
                  pl.BlockSpec((tm, d), lambda k, i: (i, 0)), pl.BlockSpec((tm, d), lambda k, i: (i, 0))],
        out_specs=[pl.BlockSpec((tk, d), lambda k, i: (k, 0)), pl.BlockSpec((tk, d), lambda k, i: (k, 0))],
        out_shape=[jax.ShapeDtypeStruct((f, d), F32), jax.ShapeDtypeStruct((f, d), F32)],
        compiler_params=_params(("parallel", "arbitrary")),
    )(z, dz, dyb, hn2)


def _mix_out_bwd(dhn2, x2, dx3, mix, ymix, w_out, mod6, g_mlp, tm, rider=None):
    s, d = x2.shape

    def body(dh_ref, x2_ref, dx3_ref, mix_ref, y_ref, w_ref, mod_ref, g_ref, dx2_ref, dym_ref, gw_ref, st_ref):
        i = pl.program_id(0)

        @pl.when(i == 0)
        def _():
            st_ref[...] = jnp.zeros_like(st_ref)
            gw_ref[...] = jnp.zeros_like(gw_ref)

        dh = dh_ref[...]
        xhat, rstd = _rms(x2_ref[...])
        dn = dh * (1.0 + mod_ref[4:5, :])
        dx2 = dx3_ref[...] + _rms_bwd(dn * g_ref[...], xhat, rstd)
        dx2_ref[...] = dx2
        st_ref[0:1, :] += _colsum(dh)
        st_ref[1:2, :] += _colsum(dh * (xhat * g_ref[...]))
        st_ref[2:3, :] += _colsum(dn * xhat)
        st_ref[3:4, :] += _colsum(dx2 * mix_ref[...])
        dmix = (mod_ref[2:3, :] * dx2).astype(BF16)
        dym_ref[...] = _dot(dmix, w_ref[...], NT)
        gw_ref[...] += _dot(y_ref[...], dmix, TN)

    tile = pl.BlockSpec((tm, d), lambda i: (i, 0))
    return _call(
        body, "mix_out_bwd", (s // tm,),
        [tile, tile, tile, tile, tile, _full(w_out.shape), _full(mod6.shape), _full(g_mlp.shape)],
        [tile, tile, _full((d, d)), _full((SUBLANES, d))],
        [jax.ShapeDtypeStruct((s, d), F32), jax.ShapeDtypeStruct((s, d), F32),
         jax.ShapeDtypeStruct((d, d), F32), jax.ShapeDtypeStruct((SUBLANES, d), F32)],
        [dhn2, x2, dx3, mix, ymix, w_out, mod6, g_mlp], rider=rider)


def _mixer_bwd(proj, dymix, h_all, conv_sc, conv_lru, conv_b, wa_bd, wx_bd, ba, bx, lam, width, rider=None):
    s, din = proj.shape
    t = min(MIX_ROWS, s)
    nt = s // t
    nblk = width // LANES
    hb = t // SUBLANES
    last8 = s // SUBLANES - 1

    def body(proj_ref, projp_ref, projn_ref, dy_ref, dyn_ref, h_ref, hp_ref,
             wsc_ref, wlru_ref, blru_ref, wa_ref, wx_ref, ba_ref, bx_ref, lam_ref,
             dproj_ref, small_ref, gwa_ref, gwx_ref, an_ref, gn_ref, dun_ref):
        i = pl.program_id(0)

        @pl.when(i == 0)
        def _():
            small_ref[...] = jnp.zeros_like(small_ref)
            gwa_ref[...] = jnp.zeros_like(gwa_ref)
            gwx_ref[...] = jnp.zeros_like(gwx_ref)
            an_ref[...] = jnp.zeros_like(an_ref)
            gn_ref[...] = jnp.zeros_like(gn_ref)
            dun_ref[...] = jnp.zeros_like(dun_ref)

        has_prev = i < nt - 1
        has_next = i > 0
        for j in range(nblk):
            lo = j * LANES
            ls = slice(lo, lo + LANES)

            def col(p, ref=proj_ref):
                return ref[:, p * width + lo:p * width + lo + LANES]

            def prev(p):
                return jnp.where(has_prev, col(p, projp_ref), 0.0)

            def nxt(p):
                return jnp.where(has_next, col(p, projn_ref), 0.0)

            def add_row(r, v):
                small_ref[r:r + 1, ls] += _colsum(v)

            sc_b, sc_c, sc_x = col(0), col(1), col(2)
            p = sc_c * sc_x
            q, p1, p2 = _conv3(p, prev(1) * prev(2), wsc_ref, lo)
            dys = dy_ref[:, ls]
            dproj_ref[:, ls] = (dys * q).astype(BF16)
            dq = dys * sc_b
            dqn = jnp.where(has_next, dyn_ref[:, ls], 0.0) * nxt(0)
            dp = (wsc_ref[2:3, ls] * dq + wsc_ref[1:2, ls] * _shift_up(dq, 1, dqn)) + wsc_ref[0:1, ls] * _shift_up(dq, 2, dqn)
            dproj_ref[:, width + lo:width + lo + LANES] = (dp * sc_x).astype(BF16)
            dproj_ref[:, 2 * width + lo:2 * width + lo + LANES] = (dp * sc_c).astype(BF16)
            add_row(0, dq * p2)
            add_row(1, dq * p1)
            add_row(2, dq * p)

            xv = col(4)
            u, x1, x2, x3 = _conv4(xv, prev(4), wlru_ref, blru_ref, lo)
            lam_v = lam_ref[:, ls]
            sp = _softplus(-lam_v)
            wa, wx = wa_ref[j], wx_ref[j]
            ub, r, ig, a, mult = _lru_gates(u, wa, wx, ba_ref[:, ls], bx_ref[:, ls], sp)
            iu = ig * u
            h = h_ref[:, ls]
            hm1 = _shift_down(h, 1, jnp.where(has_prev, hp_ref[:, ls], 0.0))
            lyv = col(3)
            gel, th = _gelu(lyv)
            dyl = dy_ref[:, width + lo:width + lo + LANES]
            dproj_ref[:, 3 * width + lo:3 * width + lo + LANES] = (dyl * h * _dgelu(lyv, th)).astype(BF16)
            a_next = jnp.broadcast_to(an_ref[0:1, ls], (SUBLANES, LANES))
            g = _scan_rev(_shift_up(a, 1, a_next), dyl * gel, gn_ref[0:1, ls])
            an_ref[0:1, ls] = a[0:1, :]
            gn_ref[0:1, ls] = g[0:1, :]
            da = g * hm1
            dmult = g * iu
            diu = g * mult
            dlog_a = da * a - dmult * ((a * a) / mult)
            dpre_a = (dlog_a * (-RG_C * sp)) * (r * (1.0 - r))
            dpre_x = (diu * u) * (ig * (1.0 - ig))
            dab, dxb = dpre_a.astype(BF16), dpre_x.astype(BF16)
            du = diu * ig + _dot(dab, wa, NT) + _dot(dxb, wx, NT)
            gwa_ref[j] += _dot(ub, dab, TN)
            gwx_ref[j] += _dot(ub, dxb, TN)
            dun = dun_ref[:, ls]
            dun_ref[:, ls] = du[0:SUBLANES, :]
            dlx = (((wlru_ref[3:4, ls] * du + wlru_ref[2:3, ls] * _shift_up(du, 1, dun))
                    + wlru_ref[1:2, ls] * _shift_up(du, 2, dun)) + wlru_ref[0:1, ls] * _shift_up(du, 3, dun))
            dproj_ref[:, 4 * width + lo:4 * width + lo + LANES] = dlx.astype(BF16)
            add_row(3, du * x3)
            add_row(4, du * x2)
            add_row(5, du * x1)
            add_row(6, du * xv)
            add_row(7, du)
            add_row(8, dpre_a)
            add_row(9, dpre_x)
            add_row(10, (dlog_a * (RG_C * r)) * jax.nn.sigmoid(-lam_v))

    small = [conv_sc, conv_lru, conv_b, wa_bd, wx_bd, ba, bx, lam]
    rev = lambda i: nt - 1 - i
    return _call(
        body, "mixer_bwd", (nt,),
        [pl.BlockSpec((t, din), lambda i: (rev(i), 0)),
         pl.BlockSpec((SUBLANES, din), lambda i: (jnp.maximum(rev(i) * hb - 1, 0), 0)),
         pl.BlockSpec((SUBLANES, din), lambda i: (jnp.minimum((rev(i) + 1) * hb, last8), 0)),
         pl.BlockSpec((t, 2 * width), lambda i: (rev(i), 0)),
         pl.BlockSpec((SUBLANES, 2 * width), lambda i: (jnp.minimum((rev(i) + 1) * hb, last8), 0)),
         pl.BlockSpec((t, width), lambda i: (rev(i), 0)),
         pl.BlockSpec((SUBLANES, width), lambda i: (jnp.maximum(rev(i) * hb - 1, 0), 0))]
        + [_full(a.shape) for a in small],
        [pl.BlockSpec((t, din), lambda i: (rev(i), 0)), _full((2 * SUBLANES, width)),
         _full(wa_bd.shape), _full(wx_bd.shape)],
        [jax.ShapeDtypeStruct((s, din), BF16), jax.ShapeDtypeStruct((2 * SUBLANES, width), F32),
         jax.ShapeDtypeStruct(wa_bd.shape, F32), jax.ShapeDtypeStruct(wx_bd.shape, F32)],
        [proj, proj, proj, dymix, dymix, h_all, h_all, *small],
        scratch=[pltpu.VMEM((SUBLANES, width), F32), pltpu.VMEM((SUBLANES, width), F32),
                 pltpu.VMEM((SUBLANES, width), F32)], rider=rider)


def _mix_in_bwd_dx(dproj, x2d, dx2, w_in_t, mod6, g_mix, tm, rider=None):
    s, d = x2d.shape
    din = dproj.shape[1]

    def body(dp_ref, x_ref, dx2_ref, w_ref, mod_ref, g_ref, gx_ref, st_ref):
        i = pl.program_id(0)

        @pl.when(i == 0)
        def _():
            st_ref[...] = jnp.zeros_like(st_ref)

        dh = _dot(dp_ref[...], w_ref[...], NN)
        xhat, rstd = _rms(x_ref[...])
        dn = dh * (1.0 + mod_ref[1:2, :])
        gx_ref[...] = dx2_ref[...] + _rms_bwd(dn * g_ref[...], xhat, rstd)
        st_ref[0:1, :] += _colsum(dh)
        st_ref[1:2, :] += _colsum(dh * (xhat * g_ref[...]))
        st_ref[2:3, :] += _colsum(dn * xhat)

    tile = pl.BlockSpec((tm, d), lambda i: (i, 0))
    return _call(
        body, "mix_in_bwd_dx", (s // tm,),
        [pl.BlockSpec((tm, din), lambda i: (i, 0)), tile, tile, _full(w_in_t.shape), _full(mod6.shape),
         _full(g_mix.shape)],
        [tile, _full((SUBLANES, d))],
        [jax.ShapeDtypeStruct((s, d), F32), jax.ShapeDtypeStruct((SUBLANES, d), F32)],
        [dproj, x2d, dx2, w_in_t, mod6, g_mix], rider=rider)


def _mix_in_bwd_dw(dproj, hn1, tm, tn, rider=None):
    s, d = hn1.shape
    din = dproj.shape[1]

    def body(dp_ref, hn_ref, gw_ref):
        i = pl.program_id(1)

        @pl.when(i == 0)
        def _():
            gw_ref[...] = jnp.zeros_like(gw_ref)

        gw_ref[...] += _dot(dp_ref[...], hn_ref[...], TN)

    return _call(
        body, "mix_in_bwd_dw", (din // tn, s // tm),
        [pl.BlockSpec((tm, tn), lambda p, i: (i, p)), pl.BlockSpec((tm, d), lambda p, i: (i, 0))],
        [pl.BlockSpec((tn, d), lambda p, i: (p, 0))],
        [jax.ShapeDtypeStruct((din, d), F32)],
        [dproj, hn1], rider=rider)


def _adamw(w, g, m, v):
    m = ADAM_B1 * m + (1.0 - ADAM_B1) * g
    v = ADAM_B2 * v + (1.0 - ADAM_B2) * (g * g)
    m_hat = m / (1.0 - ADAM_B1 ** ADAM_STEP)
    v_hat = v / (1.0 - ADAM_B2 ** ADAM_STEP)
    delta = -ADAM_LR * (m_hat / (jnp.sqrt(v_hat) + ADAM_EPS) + ADAM_WD * w)
    return delta, m, v


def _pair_sum(g4, h4, core_chip, tr, name):
    _, _, r, n = g4.shape

    def body(sc_ref, g_ref, h_ref, sb_ref, own_ref):
        q = pl.program_id(1)
        ssum = g_ref[...] + h_ref[...]
        sb_ref[...] = ssum.astype(BF16)

        @pl.when(q == sc_ref[1])
        def _():
            own_ref[...] = ssum

    grid_spec = pltpu.PrefetchScalarGridSpec(
        num_scalar_prefetch=1, grid=(r // tr, 4),
        in_specs=[pl.BlockSpec((None, None, tr, n), lambda i, q, sc: (q, sc[0], i, 0)),
                  pl.BlockSpec((None, tr, n), lambda i, q, sc: (q, i, 0))],
        out_specs=[pl.BlockSpec((None, tr, n), lambda i, q, sc: (q, i, 0)),
                   pl.BlockSpec((tr, n), lambda i, q, sc: (i, 0))])
    return pl.pallas_call(
        body, name=name, grid_spec=grid_spec,
        out_shape=[jax.ShapeDtypeStruct((4, r, n), BF16), jax.ShapeDtypeStruct((r, n), F32)],
        compiler_params=_params(("parallel", "arbitrary")),
    )(core_chip, g4, h4)


def _sum4(own, parts, tr, name):
    r, n = own.shape

    def body(o_ref, p_ref, out_ref):
        acc = o_ref[...]
        for k in range(3):
            acc = acc + p_ref[k].astype(F32)
        out_ref[...] = acc

    return pl.pallas_call(
        body, name=name, grid=(r // tr,),
        in_specs=[pl.BlockSpec((tr, n), lambda i: (i, 0)), pl.BlockSpec((3, tr, n), lambda i: (0, i, 0))],
        out_specs=pl.BlockSpec((tr, n), lambda i: (i, 0)),
        out_shape=jax.ShapeDtypeStruct((r, n), F32),
        compiler_params=_params(("parallel",)),
    )(own, parts)


def _sum8(parts, tr, name):
    _, rows, n = parts.shape

    def body(p_ref, o_ref):
        acc = p_ref[0]
        for k in range(1, N_DEV):
            acc = acc + p_ref[k]
        o_ref[...] = acc

    return pl.pallas_call(
        body, name=name, grid=(rows // tr,),
        in_specs=[pl.BlockSpec((N_DEV, tr, n), lambda i: (0, i, 0))],
        out_specs=pl.BlockSpec((tr, n), lambda i: (i, 0)),
        out_shape=jax.ShapeDtypeStruct((rows, n), F32),
        compiler_params=_params(("parallel",)),
    )(parts)


def _adam_rows(w, g, m, v, tr, name):
    rows, n = w.shape

    def body(w_ref, g_ref, m_ref, v_ref, d_ref, nm_ref, nv_ref):
        d_ref[...], nm_ref[...], nv_ref[...] = _adamw(w_ref[...], g_ref[...], m_ref[...], v_ref[...])

    tile = pl.BlockSpec((tr, n), lambda i: (i, 0))
    return pl.pallas_call(
        body, name=name, grid=(rows // tr,),
        in_specs=[tile] * 4, out_specs=[tile] * 3,
        out_shape=[jax.ShapeDtypeStruct((rows, n), F32)] * 3,
        compiler_params=_params(("parallel",)),
    )(w, g, m, v)


def _ada_bwd_adam(cact_t, dmod_cols, w, m, v, tr):
    rows, n = w.shape

    def body(c_ref, d_ref, w_ref, m_ref, v_ref, g_ref, dl_ref, nm_ref, nv_ref):
        def term(b):
            return c_ref[b].astype(BF16).astype(F32) * d_ref[b:b + 1, :].astype(BF16).astype(F32)

        g = term(0)
        for b in range(1, N_DEV):
            g = g + term(b)
        g_ref[...] = g
        dl_ref[...], nm_ref[...], nv_ref[...] = _adamw(w_ref[...], g, m_ref[...], v_ref[...])

    tile = pl.BlockSpec((tr, n), lambda i: (i, 0))
    return pl.pallas_call(
        body, name="ada_bwd_adam", grid=(rows // tr,),
        in_specs=[pl.BlockSpec((N_DEV, tr, 1), lambda i: (0, i, 0)), _full(dmod_cols.shape), tile, tile, tile],
        out_specs=[tile] * 4,
        out_shape=[jax.ShapeDtypeStruct((rows, n), F32)] * 4,
        compiler_params=_params(("parallel",)),
    )(cact_t, dmod_cols, w, m, v)


def _adam_small(ws, gs, ms, vs):
    n = len(ws)

    def body(*refs):
        w_r, g_r, m_r, v_r = refs[:n], refs[n:2 * n], refs[2 * n:3 * n], refs[3 * n:4 * n]
        d_r, nm_r, nv_r = refs[4 * n:5 * n], refs[5 * n:6 * n], refs[6 * n:7 * n]
        for k in range(n):
            d_r[k][...], nm_r[k][...], nv_r[k][...] = _adamw(w_r[k][...], g_r[k][...], m_r[k][...], v_r[k][...])

    shapes = [jax.ShapeDtypeStruct(w.shape, F32) for w in ws]
    outs = pl.pallas_call(
        body, name="adam_small", out_shape=shapes * 3, compiler_params=_params(),
    )(*ws, *gs, *ms, *vs)
    return outs[:n], outs[n:2 * n], outs[2 * n:]


def _block_diag(w):
    h, hd, _ = w.shape
    per = LANES // hd
    eye = jnp.eye(per, dtype=w.dtype)
    w5 = w.reshape(h // per, per, hd, 1, hd) * eye[None, :, None, :, None]
    return w5.reshape(h // per, LANES, LANES)


def _block_diag_grad(g, h, hd):
    per = LANES // hd
    g5 = g.reshape(h // per, per, hd, per, hd)
    return jnp.stack([g5[:, a, :, a, :] for a in range(per)], axis=1).reshape(h, hd, hd)


def kernel(x, c, w_ada, b_ada, g_mix, w_in, conv_w_sc, conv_w_lru, conv_b_lru, w_rg_a, b_rg_a, w_rg_x, b_rg_x, lru_lambda, w_out, g_mlp, w_up, w_down, g_final, loss_target, m_w_ada, m_b_ada, m_g_mix, m_w_in, m_conv_w_sc, m_conv_w_lru, m_conv_b_lru, m_w_rg_a, m_b_rg_a, m_w_rg_x, m_b_rg_x, m_lru_lambda, m_w_out, m_g_mlp, m_w_up, m_w_down, m_g_final, v_w_ada, v_b_ada, v_g_mix, v_w_in, v_conv_w_sc, v_conv_w_lru, v_conv_b_lru, v_w_rg_a, v_b_rg_a, v_w_rg_x, v_b_rg_x, v_lru_lambda, v_w_out, v_g_mlp, v_w_up, v_w_down, v_g_final):
    s, d = x.shape[1], x.shape[2]
    width = conv_b_lru.shape[1]
    heads, hd = w_rg_a.shape[1], w_rg_a.shape[2]
    f = w_down.shape[1] * N_DEV
    n_ada = w_ada.shape[2]
    csh = conv_w_sc.shape[2]
    me = 4 * lax.axis_index("x") + 2 * lax.axis_index("y") + lax.axis_index("c")
    tm = min(512, s)
    tm_mlp = min(1024, s)
    tk = 512

    x2d = x[0]
    tgt = loss_target[0]

    pay = jnp.zeros((SUBLANES, d), F32)
    pay = pay.at[0:1, :].set(c)
    pay = pay.at[1:4, 0:csh].set(conv_w_sc[0])
    pay = pay.at[4:8, 0:csh].set(conv_w_lru[0])
    w_in_t_sh = w_in[0].T.astype(BF16)
    w_up_t_sh = w_up[0].T.astype(BF16)
    w_out_sh = w_out[0].astype(BF16)
    w_down_sh = w_down[0].astype(BF16)
    pay_all, w_in_t = _gather2("gather_in", [pay, w_in_t_sh])
    w_in_t = w_in_t.reshape(-1, d)
    c_all = pay_all[:, 0, :]
    conv_sc = pay_all[:, 1:4, 0:csh].transpose(1, 0, 2).reshape(3, width)
    conv_lru = pay_all[:, 4:8, 0:csh].transpose(1, 0, 2).reshape(4, width)

    b_ada_sh = lax.dynamic_slice(b_ada, (0, me * n_ada), (1, n_ada))
    mod_cols, c_act = _ada_fwd(c_all, w_ada[0], b_ada_sh)
    (mod_rows,) = _exchange("scatter_mod", [], [mod_cols.reshape(N_DEV, 1, n_ada)])
    mod6 = jnp.zeros((SUBLANES, d), F32).at[0:6, :].set(mod_rows.reshape(6, d))

    wa_bd = _block_diag(w_rg_a[0]).astype(BF16)
    wx_bd = _block_diag(w_rg_x[0]).astype(BF16)
    ba = b_rg_a.reshape(1, width)
    bx = b_rg_x.reshape(1, width)
    g_fin = g_final.reshape(1, d)

    (hn1, proj), (w_out_g, w_up_g) = _mix_in_fwd(
        x2d, mod6, g_mix, w_in_t, tm, rider=_ride_gather_ici([w_out_sh, w_up_t_sh]))
    (ymix, h_all), (w_out_g, w_up_g, w_down_g) = _mixer_fwd(
        proj, conv_sc, conv_lru, conv_b_lru, wa_bd, wx_bd, ba, bx, lru_lambda, width,
        rider=_merge_riders(_ride_gather_d2d([w_out_g, w_up_g]), _ride_gather_ici([w_down_sh])))
    w_out_b = w_out_g.reshape(-1, d)
    (mix, x2, hn2), (w_down_g,) = _mix_out_fwd(ymix, x2d, w_out_b, mod6, g_mlp, tm,
                                               rider=_ride_gather_d2d([w_down_g]))
    w_up_t = w_up_g.reshape(-1, d)
    w_down_b = w_down_g.reshape(-1, d)
    z, dx3, dyb, st_fin = _mlp_fwd_loss(hn2, w_up_t, w_down_b, x2, tgt, mod6, g_fin, tm, 4 * tk)

    core_chip = jnp.stack([lax.axis_index("c"), 2 * lax.axis_index("x") + lax.axis_index("y")]).astype(jnp.int32)
    dz, dhn2 = _mlp_bwd_dx(dyb, z, w_down_b, w_up_t, tm, 4 * tk)
    g_down, g_up_t = _mlp_bwd_dw(z, dz, dyb, hn2, tm_mlp, 2 * tk)
    g_up4, g_down4 = g_up_t.reshape(4, 2, -1, d), g_down.reshape(4, 2, -1, d)
    (dx2, dymix, g_out, st_out), (h_up, h_down) = _mix_out_bwd(
        dhn2, x2, dx3, mix, ymix, w_out_b, mod6, g_mlp, tm, rider=_ride_pair_swap([g_up4, g_down4]))
    sb_up, own_up = _pair_sum(g_up4, h_up, core_chip, 256, "pair_sum_w_up")
    sb_down, own_down = _pair_sum(g_down4, h_down, core_chip, 256, "pair_sum_w_down")
    g_out4 = g_out.reshape(4, 2, -1, d)
    (dproj, g_small, g_wa, g_wx), (p_up, p_down, h_out) = _mixer_bwd(
        proj, dymix, h_all, conv_sc, conv_lru, conv_b_lru, wa_bd, wx_bd, ba, bx, lru_lambda, width,
        rider=_merge_riders(_ride_chip_exchange([sb_up, sb_down]), _ride_pair_swap([g_out4])))
    sb_out, own_out = _pair_sum(g_out4, h_out, core_chip, g_out4.shape[2], "pair_sum_w_out")
    (grad_x, st_in), _ = _mix_in_bwd_dx(dproj, x2d, dx2, w_in_t, mod6, g_mix, tm)

    small = jnp.concatenate([
        st_in[0:2], st_out[3:4], st_out[0:2], st_fin[1:2],
        st_in[2:3], st_out[2:3], st_fin[0:1],
        jnp.concatenate([g_small[7:8], g_small[10:11]], axis=1),
        jnp.concatenate([g_small[8:9], g_small[9:10]], axis=1),
        jnp.concatenate([jnp.concatenate([g_small[0:3], jnp.zeros((1, width), F32)], axis=0), g_small[3:7]], axis=1),
        st_fin[2:3],
        _block_diag_grad(g_wa, heads, hd).reshape(-1, d),
        _block_diag_grad(g_wx, heads, hd).reshape(-1, d),
    ], axis=0)

    (g_in_t,), (p_out, small_all) = _mix_in_bwd_dw(
        dproj, hn1, tm_mlp, 512, rider=_merge_riders(_ride_chip_exchange([sb_out]), _ride_gather_direct([small])))
    g_in4 = g_in_t.reshape(4, 2, -1, d)
    (h_in,) = _comm("swap_w_in", _ride_pair_swap([g_in4]))
    sb_in, own_in = _pair_sum(g_in4, h_in, core_chip, g_in4.shape[2], "pair_sum_w_in")
    (p_in,) = _comm("exchange_w_in", _ride_chip_exchange([sb_in]))

    gs_in = _sum4(own_in, p_in, own_in.shape[0], "sum_w_in").T
    gs_up = _sum4(own_up, p_up, 256, "sum_w_up").T
    gs_out = _sum4(own_out, p_out, own_out.shape[0], "sum_w_out")
    gs_down = _sum4(own_down, p_down, 256, "sum_w_down")
    ad_in = _adam_rows(w_in[0], gs_in, m_w_in[0], v_w_in[0], 256, "adam_w_in")
    ad_up = _adam_rows(w_up[0], gs_up, m_w_up[0], v_w_up[0], 256, "adam_w_up")
    ad_out = _adam_rows(w_out[0], gs_out, m_w_out[0], v_w_out[0], w_out.shape[1], "adam_w_out")
    ad_down = _adam_rows(w_down[0], gs_down, m_w_down[0], v_w_down[0], 256, "adam_w_down")

    gsum = _sum8(small_all, SMALL_ROWS, "sum_small")
    loss = (0.5 / d) * jnp.sum(gsum[15])
    dmod_cols = lax.dynamic_slice(small_all[:, 0:6, :].reshape(N_DEV, 6 * d), (0, me * n_ada), (N_DEV, n_ada))
    g_ada, d_ada, nm_ada, nv_ada = _ada_bwd_adam(c_act[:, :, None], dmod_cols, w_ada[0], m_w_ada[0], v_w_ada[0], 256)

    g_conv = lax.dynamic_slice(gsum[11:15, 0:width], (0, me * csh), (4, csh))
    g_conv_l = lax.dynamic_slice(gsum[11:15, width:2 * width], (0, me * csh), (4, csh))
    small_g = [
        gsum[0:6].reshape(1, 6 * d),
        gsum[6:7],
        g_conv[0:3].reshape(1, 3, csh),
        g_conv_l.reshape(1, 4, csh),
        gsum[9:10, 0:width],
        gsum[16:48].reshape(1, heads, hd, hd),
        gsum[10:11, 0:width].reshape(1, heads, hd),
        gsum[48:80].reshape(1, heads, hd, hd),
        gsum[10:11, width:].reshape(1, heads, hd),
        gsum[9:10, width:],
        gsum[7:8],
        gsum[8],
    ]
    small_w = [b_ada, g_mix, conv_w_sc, conv_w_lru, conv_b_lru, w_rg_a, b_rg_a, w_rg_x, b_rg_x, lru_lambda, g_mlp, g_final]
    small_m = [m_b_ada, m_g_mix, m_conv_w_sc, m_conv_w_lru, m_conv_b_lru, m_w_rg_a, m_b_rg_a, m_w_rg_x, m_b_rg_x,
               m_lru_lambda, m_g_mlp, m_g_final]
    small_v = [v_b_ada, v_g_mix, v_conv_w_sc, v_conv_w_lru, v_conv_b_lru, v_w_rg_a, v_b_rg_a, v_w_rg_x, v_b_rg_x,
               v_lru_lambda, v_g_mlp, v_g_final]
    sd, snm, snv = _adam_small(small_w, small_g, small_m, small_v)

    def order(ada, w_in_, w_out_, w_up_, w_down_, sm):
        return [ada[None], sm[0], sm[1], w_in_[None], sm[2], sm[3], sm[4], sm[5], sm[6], sm[7], sm[8], sm[9],
                w_out_[None], sm[10], w_up_[None], w_down_[None], sm[11]]

    grads = order(g_ada, gs_in, gs_out, gs_up, gs_down, small_g)
    deltas = order(d_ada, ad_in[0], ad_out[0], ad_up[0], ad_down[0], sd)
    new_m = order(nm_ada, ad_in[1], ad_out[1], ad_up[1], ad_down[1], snm)
    new_v = order(nv_ada, ad_in[2], ad_out[2], ad_up[2], ad_down[2], snv)
    return (loss, grad_x[None], *grads, *deltas, *new_m, *new_v)
```

```python
import functools

import jax
import jax.numpy as jnp
from jax import lax
from jax.experimental import pallas as pl
from jax.experimental.pallas import tpu as pltpu

F32 = jnp.float32
BF16 = jnp.bfloat16
N_DEV = 8
EPS = 1e-6
RG_C = 8.0
GELU_K0 = 0.7978845608028654
GELU_K1 = 0.044715
ADAM_LR = 0.001
ADAM_B1 = 0.9
ADAM_B2 = 0.999
ADAM_EPS = 1e-08
ADAM_WD = 0.01
ADAM_STEP = 10
LANES = 128
SUBLANES = 8
VMEM_LIMIT = 52 * 1024 * 1024
MIX_ROWS = 256
SMALL_ROWS = 80

MESH = pl.DeviceIdType.MESH
ANY = pl.BlockSpec(memory_space=pl.ANY)
NN = ((1,), (0,))
NT = ((1,), (1,))
TN = ((0,), (0,))


def _dot(a, b, dims):
    return lax.dot_general(a, b, (dims, ((), ())), preferred_element_type=F32)


def _params(sem=None):
    return pltpu.CompilerParams(dimension_semantics=sem, vmem_limit_bytes=VMEM_LIMIT)


def _full(shape):
    nd = len(shape)
    return pl.BlockSpec(shape, lambda *_: (0,) * nd)


def _exchange(name, gathers, scatters):
    n_g = len(gathers)
    arrs = list(gathers) + list(scatters)
    n = len(arrs)
    out_shape = [jax.ShapeDtypeStruct((N_DEV,) + a.shape, a.dtype) for a in gathers]
    out_shape += [jax.ShapeDtypeStruct(a.shape, a.dtype) for a in scatters]

    def body(*refs):
        ins, outs = refs[:n], refs[n:2 * n]
        send_sems, recv_sems, local_sems = refs[2 * n:]
        x, y, c = lax.axis_index("x"), lax.axis_index("y"), lax.axis_index("c")
        me = 4 * x + 2 * y + c

        def src(a, dev):
            return ins[a] if a < n_g else ins[a].at[dev]

        def peer_of(k):
            px = 1 - x if (k >> 2) & 1 else x
            py = 1 - y if (k >> 1) & 1 else y
            pc = 1 - c if k & 1 else c
            return (px, py, pc), 4 * px + 2 * py + pc

        local = [pltpu.make_async_copy(src(a, me), outs[a].at[me], local_sems.at[a]) for a in range(n)]
        for cp in local:
            cp.start()
        sends = []
        for k in range(1, N_DEV):
            peer, pidx = peer_of(k)
            for a in range(n):
                cp = pltpu.make_async_remote_copy(
                    src_ref=src(a, pidx), dst_ref=outs[a].at[me],
                    send_sem=send_sems.at[a * (N_DEV - 1) + k - 1], recv_sem=recv_sems.at[a * (N_DEV - 1) + k - 1],
                    device_id=peer, device_id_type=MESH)
                cp.start()
                sends.append(cp)
        for k in range(1, N_DEV):
            peer, pidx = peer_of(k)
            for a in range(n):
                pltpu.make_async_remote_copy(
                    src_ref=src(a, pidx), dst_ref=outs[a].at[pidx],
                    send_sem=send_sems.at[a * (N_DEV - 1) + k - 1], recv_sem=recv_sems.at[a * (N_DEV - 1) + k - 1],
                    device_id=peer, device_id_type=MESH).wait_recv()
        for cp in sends:
            cp.wait_send()
        for cp in local:
            cp.wait()

    return pl.pallas_call(
        body, name=name, out_shape=out_shape,
        in_specs=[ANY] * n, out_specs=[ANY] * n,
        scratch_shapes=[pltpu.SemaphoreType.DMA((n * (N_DEV - 1),)),
                        pltpu.SemaphoreType.DMA((n * (N_DEV - 1),)),
                        pltpu.SemaphoreType.DMA((n,))],
    )(*arrs)


def _gather2(name, arrs):
    n = len(arrs)
    per = 7
    out_shape = [jax.ShapeDtypeStruct((N_DEV,) + a.shape, a.dtype) for a in arrs]

    def body(*refs):
        ins, outs = refs[:n], refs[n:2 * n]
        send_sems, recv_sems, local_sems = refs[2 * n:]
        x, y, c = lax.axis_index("x"), lax.axis_index("y"), lax.axis_index("c")
        sib = (x, y, 1 - c)
        chips = [(1 - x, y), (x, 1 - y), (1 - x, 1 - y)]

        def slot(a, px, py, pc):
            return outs[a].at[4 * px + 2 * py + pc]

        def copy(a, k, block, to, src=None):
            return pltpu.make_async_remote_copy(
                src_ref=slot(a, *block) if src is None else src, dst_ref=slot(a, *block),
                send_sem=send_sems.at[a * per + k], recv_sem=recv_sems.at[a * per + k],
                device_id=to, device_id_type=MESH)

        local = [pltpu.make_async_copy(ins[a], slot(a, x, y, c), local_sems.at[a]) for a in range(n)]
        for cp in local:
            cp.start()
        first = []
        for a in range(n):
            first += [copy(a, 1 + j, (x, y, c), (*chip, c), src=ins[a]) for j, chip in enumerate(chips)]
        for a in range(n):
            first.append(copy(a, 0, (x, y, c), sib, src=ins[a]))
        for cp in first:
            cp.start()
        passed = []
        for a in range(n):
            for j, chip in enumerate(chips):
                copy(a, 1 + j, (*chip, c), (x, y, c)).wait_recv()
                cp = copy(a, 4 + j, (*chip, c), sib)
                cp.start()
                passed.append(cp)
        for a in range(n):
            copy(a, 0, sib, (x, y, c)).wait_recv()
            for j, chip in enumerate(chips):
                copy(a, 4 + j, (*chip, 1 - c), (x, y, c)).wait_recv()
        for cp in first + passed:
            cp.wait_send()
        for cp in local:
            cp.wait()

    return pl.pallas_call(
        body, name=name, out_shape=out_shape,
        in_specs=[ANY] * n, out_specs=[ANY] * n,
        scratch_shapes=[pltpu.SemaphoreType.DMA((n * per,)), pltpu.SemaphoreType.DMA((n * per,)),
                        pltpu.SemaphoreType.DMA((n,))],
    )(*arrs)


def _pair_swap(name, arrs):
    n = len(arrs)
    out_shape = [jax.ShapeDtypeStruct((4,) + a.shape[2:], a.dtype) for a in arrs]

    def body(*refs):
        ins, outs = refs[:n], refs[n:2 * n]
        send_sems, recv_sems = refs[2 * n:]
        x, y, c = lax.axis_index("x"), lax.axis_index("y"), lax.axis_index("c")

        def copy(a, q):
            return pltpu.make_async_remote_copy(
                src_ref=ins[a].at[q, 1 - c], dst_ref=outs[a].at[q],
                send_sem=send_sems.at[a * 4 + q], recv_sem=recv_sems.at[a * 4 + q],
                device_id=(x, y, 1 - c), device_id_type=MESH)

        cps = [copy(a, q) for a in range(n) for q in range(4)]
        for cp in cps:
            cp.start()
        for cp in cps:
            cp.wait_recv()
        for cp in cps:
            cp.wait_send()

    return pl.pallas_call(
        body, name=name, out_shape=out_shape,
        in_specs=[ANY] * n, out_specs=[ANY] * n,
        scratch_shapes=[pltpu.SemaphoreType.DMA((n * 4,)), pltpu.SemaphoreType.DMA((n * 4,))],
    )(*arrs)


def _chip_exchange(name, arrs):
    n = len(arrs)
    out_shape = [jax.ShapeDtypeStruct((3,) + a.shape[1:], a.dtype) for a in arrs]

    def body(*refs):
        ins, outs = refs[:n], refs[n:2 * n]
        send_sems, recv_sems = refs[2 * n:]
        x, y, c = lax.axis_index("x"), lax.axis_index("y"), lax.axis_index("c")

        def copy(a, k):
            px = 1 - x if (k >> 1) & 1 else x
            py = 1 - y if k & 1 else y
            return pltpu.make_async_remote_copy(
                src_ref=ins[a].at[2 * px + py], dst_ref=outs[a].at[k - 1],
                send_sem=send_sems.at[a * 3 + k - 1], recv_sem=recv_sems.at[a * 3 + k - 1],
                device_id=(px, py, c), device_id_type=MESH)

        cps = [copy(a, k) for a in range(n) for k in (1, 2, 3)]
        for cp in cps:
            cp.start()
        for cp in cps:
            cp.wait_recv()
        for cp in cps:
            cp.wait_send()

    return pl.pallas_call(
        body, name=name, out_shape=out_shape,
        in_specs=[ANY] * n, out_specs=[ANY] * n,
        scratch_shapes=[pltpu.SemaphoreType.DMA((n * 3,)), pltpu.SemaphoreType.DMA((n * 3,))],
    )(*arrs)


class _Rider:
    def __init__(self, arrays, out_shapes, n_sems, build, aliases=None):
        self.arrays, self.out_shapes, self.n_sems, self.build = list(arrays), list(out_shapes), n_sems, build
        self.aliases = dict(aliases or {})


def _merge_riders(r1, r2):
    n1i, n1o, n1s = len(r1.arrays), len(r1.out_shapes), r1.n_sems

    def build(ins, outs, send_sems, recv_sems):
        a = r1.build(ins[:n1i], outs[:n1o], send_sems.at[pl.ds(0, n1s)], recv_sems.at[pl.ds(0, n1s)])
        b = r2.build(ins[n1i:], outs[n1o:], send_sems.at[pl.ds(n1s, r2.n_sems)], recv_sems.at[pl.ds(n1s, r2.n_sems)])
        return tuple(p + q for p, q in zip(a, b))

    aliases = dict(r1.aliases)
    aliases.update({k + n1i: v + n1o for k, v in r2.aliases.items()})
    return _Rider(r1.arrays + r2.arrays, r1.out_shapes + r2.out_shapes, n1s + r2.n_sems, build, aliases)


def _place():
    x, y, c = lax.axis_index("x"), lax.axis_index("y"), lax.axis_index("c")
    chips = [(1 - x, y), (x, 1 - y), (1 - x, 1 - y)]
    return x, y, c, chips


def _ride_gather_ici(arrs):
    n = len(arrs)

    def build(ins, outs, send_sems, recv_sems):
        x, y, c, chips = _place()
        peers = [(*chip, c) for chip in chips] + [(x, y, 1 - c)]
        me = 4 * x + 2 * y + c
        local = [pltpu.make_async_copy(ins[a], outs[a].at[me], send_sems.at[a * 5 + 4]) for a in range(n)]
        sends, recvs = [], []
        for a in range(n):
            for j, (px, py, pc) in enumerate(peers):
                sends.append(pltpu.make_async_remote_copy(
                    src_ref=ins[a], dst_ref=outs[a].at[me], send_sem=send_sems.at[a * 5 + j],
                    recv_sem=recv_sems.at[a * 5 + j], device_id=(px, py, pc), device_id_type=MESH))
                recvs.append(pltpu.make_async_remote_copy(
                    src_ref=ins[a], dst_ref=outs[a].at[4 * px + 2 * py + pc], send_sem=send_sems.at[a * 5 + j],
                    recv_sem=recv_sems.at[a * 5 + j], device_id=(px, py, pc), device_id_type=MESH))
        return local, sends, recvs

    shapes = [jax.ShapeDtypeStruct((N_DEV,) + a.shape, a.dtype) for a in arrs]
    return _Rider(arrs, shapes, n * 5, build)


def _ride_gather_direct(arrs):
    n = len(arrs)

    def build(ins, outs, send_sems, recv_sems):
        x, y, c, _ = _place()
        me = 4 * x + 2 * y + c
        local = [pltpu.make_async_copy(ins[a], outs[a].at[me], send_sems.at[a * N_DEV + 7]) for a in range(n)]
        sends, recvs = [], []
        for a in range(n):
            for k in range(1, N_DEV):
                px = 1 - x if (k >> 2) & 1 else x
                py = 1 - y if (k >> 1) & 1 else y
                pc = 1 - c if k & 1 else c
                sem = a * N_DEV + k - 1
                sends.append(pltpu.make_async_remote_copy(
                    src_ref=ins[a], dst_ref=outs[a].at[me], send_sem=send_sems.at[sem], recv_sem=recv_sems.at[sem],
                    device_id=(px, py, pc), device_id_type=MESH))
                recvs.append(pltpu.make_async_remote_copy(
                    src_ref=ins[a], dst_ref=outs[a].at[4 * px + 2 * py + pc], send_sem=send_sems.at[sem],
                    recv_sem=recv_sems.at[sem], device_id=(px, py, pc), device_id_type=MESH))
        return local, sends, recvs

    shapes = [jax.ShapeDtypeStruct((N_DEV,) + a.shape, a.dtype) for a in arrs]
    return _Rider(arrs, shapes, n * N_DEV, build)


def _ride_gather_d2d(gathered):
    n = len(gathered)

    def build(ins, outs, send_sems, recv_sems):
        x, y, c, chips = _place()
        sends, recvs = [], []
        for a in range(n):
            for j, (px, py) in enumerate(chips):
                mine = outs[a].at[4 * px + 2 * py + c]
                theirs = outs[a].at[4 * px + 2 * py + 1 - c]
                sends.append(pltpu.make_async_remote_copy(
                    src_ref=mine, dst_ref=mine, send_sem=send_sems.at[a * 3 + j], recv_sem=recv_sems.at[a * 3 + j],
                    device_id=(x, y, 1 - c), device_id_type=MESH))
                recvs.append(pltpu.make_async_remote_copy(
                    src_ref=mine, dst_ref=theirs, send_sem=send_sems.at[a * 3 + j], recv_sem=recv_sems.at[a * 3 + j],
                    device_id=(x, y, 1 - c), device_id_type=MESH))
        return [], sends, recvs

    shapes = [jax.ShapeDtypeStruct(a.shape, a.dtype) for a in gathered]
    return _Rider(gathered, shapes, n * 3, build, aliases={a: a for a in range(n)})


def _ride_pair_swap(arrs):
    n = len(arrs)

    def build(ins, outs, send_sems, recv_sems):
        x, y, c, _ = _place()
        cps = [pltpu.make_async_remote_copy(
            src_ref=ins[a].at[q, 1 - c], dst_ref=outs[a].at[q], send_sem=send_sems.at[a * 4 + q],
            recv_sem=recv_sems.at[a * 4 + q], device_id=(x, y, 1 - c), device_id_type=MESH)
            for a in range(n) for q in range(4)]
        return [], cps, cps

    shapes = [jax.ShapeDtypeStruct((4,) + a.shape[2:], a.dtype) for a in arrs]
    return _Rider(arrs, shapes, n * 4, build)


def _ride_chip_exchange(arrs):
    n = len(arrs)

    def build(ins, outs, send_sems, recv_sems):
        x, y, c, _ = _place()
        cps = []
        for a in range(n):
            for k in (1, 2, 3):
                px = 1 - x if (k >> 1) & 1 else x
                py = 1 - y if k & 1 else y
                cps.append(pltpu.make_async_remote_copy(
                    src_ref=ins[a].at[2 * px + py], dst_ref=outs[a].at[k - 1], send_sem=send_sems.at[a * 3 + k - 1],
                    recv_sem=recv_sems.at[a * 3 + k - 1], device_id=(px, py, c), device_id_type=MESH))
        return [], cps, cps

    shapes = [jax.ShapeDtypeStruct((3,) + a.shape[1:], a.dtype) for a in arrs]
    return _Rider(arrs, shapes, n * 3, build)


def _call(body, name, grid, in_specs, out_specs, out_shape, args, scratch=(), rider=None):
    n_in, n_out, n_scr = len(in_specs), len(out_specs), len(scratch)
    sem = ("arbitrary",) * len(grid)
    if rider is None:
        outs = pl.pallas_call(
            body, name=name, grid=grid, in_specs=in_specs, out_specs=out_specs, out_shape=out_shape,
            scratch_shapes=list(scratch), compiler_params=_params(sem))(*args)
        return outs, []
    ri, ro = len(rider.arrays), len(rider.out_shapes)

    def riding(*refs):
        ins, r_ins = refs[:n_in], refs[n_in:n_in + ri]
        outs = refs[n_in + ri:n_in + ri + n_out]
        r_outs = refs[n_in + ri + n_out:n_in + ri + n_out + ro]
        scr = refs[n_in + ri + n_out + ro:n_in + ri + n_out + ro + n_scr]
        send_sems, recv_sems = refs[-2:]
        first = functools.reduce(jnp.logical_and, [pl.program_id(k) == 0 for k in range(len(grid))])
        last = functools.reduce(jnp.logical_and, [pl.program_id(k) == grid[k] - 1 for k in range(len(grid))])

        @pl.when(first)
        def _():
            local, sends, _ = rider.build(r_ins, r_outs, send_sems, recv_sems)
            for cp in local + sends:
                cp.start()

        body(*ins, *outs, *scr)

        @pl.when(last)
        def _():
            local, sends, recvs = rider.build(r_ins, r_outs, send_sems, recv_sems)
            for cp in recvs:
                cp.wait_recv()
            for cp in sends:
                cp.wait_send()
            for cp in local:
                cp.wait()

    outs = pl.pallas_call(
        riding, name=name, grid=grid,
        in_specs=list(in_specs) + [ANY] * ri, out_specs=list(out_specs) + [ANY] * ro,
        out_shape=list(out_shape) + rider.out_shapes,
        scratch_shapes=list(scratch) + [pltpu.SemaphoreType.DMA((rider.n_sems,)), pltpu.SemaphoreType.DMA((rider.n_sems,))],
        input_output_aliases={n_in + k: n_out + v for k, v in rider.aliases.items()},
        compiler_params=_params(sem))(*args, *rider.arrays)
    return outs[:n_out], outs[n_out:]


def _comm(name, rider):
    def body(dummy_ref, out_ref):
        out_ref[...] = dummy_ref[...]

    dummy = jnp.zeros((SUBLANES, LANES), F32)
    spec = pl.BlockSpec((SUBLANES, LANES), lambda i: (0, 0))
    _, r_outs = _call(body, name, (1,), [spec], [spec], [jax.ShapeDtypeStruct(dummy.shape, F32)], [dummy], rider=rider)
    return r_outs


def _ada_fwd(c_all, w_ada_sh, b_ada_sh):
    nb, d = c_all.shape
    ncol = w_ada_sh.shape[1]

    def body(c_ref, w_ref, b_ref, mod_ref, cact_ref):
        cc = c_ref[...]
        ca = cc * jax.nn.sigmoid(cc)
        cact_ref[...] = ca
        mod_ref[...] = _dot(ca.astype(BF16), w_ref[...].astype(BF16), NN) + b_ref[...]

    return pl.pallas_call(
        body, name="ada_fwd",
        out_shape=[jax.ShapeDtypeStruct((nb, ncol), F32), jax.ShapeDtypeStruct((nb, d), F32)],
        compiler_params=_params(),
    )(c_all, w_ada_sh, b_ada_sh)


def _rms(xv):
    rstd = lax.rsqrt(jnp.mean(xv * xv, axis=-1, keepdims=True) + EPS)
    return xv * rstd, rstd


def _rms_bwd(dxhat, xhat, rstd):
    return rstd * (dxhat - xhat * jnp.mean(dxhat * xhat, axis=-1, keepdims=True))


def _colsum(v):
    return jnp.sum(v, axis=0, keepdims=True)


def _expm1(v):
    series = v * (1.0 + v * (0.5 + v * (1.0 / 6.0 + v * (1.0 / 24.0 + v * (1.0 / 120.0 + v * (1.0 / 720.0))))))
    return jnp.where(jnp.abs(v) < 0.3, series, jnp.exp(v) - 1.0)


def _softplus(v):
    return jnp.maximum(v, 0.0) + jnp.log1p(jnp.exp(-jnp.abs(v)))


def _gelu(v):
    t = jnp.tanh(GELU_K0 * (v + GELU_K1 * v * v * v))
    return 0.5 * v * (1.0 + t), t


def _dgelu(v, t):
    return 0.5 * (1.0 + t) + 0.5 * v * (1.0 - t * t) * GELU_K0 * (1.0 + 3.0 * GELU_K1 * v * v)


def _shift_down(v, k, prev8):
    r = pltpu.roll(v, k, 0)
    pr = pltpu.roll(prev8, k, 0)
    row8 = lax.broadcasted_iota(jnp.int32, prev8.shape, 0)
    top = jnp.where(row8 < k, pr, r[0:SUBLANES])
    return jnp.concatenate([top, r[SUBLANES:]], axis=0)


def _shift_up(v, k, next8):
    t = v.shape[0]
    r = pltpu.roll(v, t - k, 0)
    nr = pltpu.roll(next8, SUBLANES - k, 0)
    row8 = lax.broadcasted_iota(jnp.int32, next8.shape, 0)
    bot = jnp.where(row8 >= SUBLANES - k, nr, r[t - SUBLANES:t])
    return jnp.concatenate([r[:t - SUBLANES], bot], axis=0)


def _scan_fwd(a, b, h0):
    t = a.shape[0]
    row = lax.broadcasted_iota(jnp.int32, a.shape, 0)
    s = 1
    while s < min(t, SUBLANES):
        a_sh = pltpu.roll(a, s, 0)
        b_sh = pltpu.roll(b, s, 0)
        m = row >= s
        b = jnp.where(m, a * b_sh + b, b)
        a = jnp.where(m, a * a_sh, a)
        s *= 2
    while s < t:
        b = jnp.concatenate([b[:s], a[s:] * b[:t - s] + b[s:]], axis=0)
        a = jnp.concatenate([a[:s], a[s:] * a[:t - s]], axis=0)
        s *= 2
    return b + a * h0


def _scan_rev(m, b, g_next):
    t = m.shape[0]
    row = lax.broadcasted_iota(jnp.int32, m.shape, 0)
    s = 1
    while s < min(t, SUBLANES):
        m_sh = pltpu.roll(m, t - s, 0)
        b_sh = pltpu.roll(b, t - s, 0)
        msk = row < t - s
        b = jnp.where(msk, m * b_sh + b, b)
        m = jnp.where(msk, m * m_sh, m)
        s *= 2
    while s < t:
        b = jnp.concatenate([m[:t - s] * b[s:] + b[:t - s], b[t - s:]], axis=0)
        m = jnp.concatenate([m[:t - s] * m[s:], m[t - s:]], axis=0)
        s *= 2
    return b + m * g_next


def _lru_gates(u, wa, wx, ba, bx, sp):
    ub = u.astype(BF16)
    r = jax.nn.sigmoid(_dot(ub, wa, NN) + ba)
    i = jax.nn.sigmoid(_dot(ub, wx, NN) + bx)
    log_a = (-RG_C * r) * sp
    a = jnp.exp(log_a)
    mult = jnp.sqrt(-_expm1(2.0 * log_a))
    return ub, r, i, a, mult


def _conv3(p, pp, w_ref, lo):
    p1 = _shift_down(p, 1, pp)
    p2 = _shift_down(p, 2, pp)
    q = (w_ref[0:1, lo:lo + LANES] * p2 + w_ref[1:2, lo:lo + LANES] * p1) + w_ref[2:3, lo:lo + LANES] * p
    return q, p1, p2


def _conv4(xv, xp, w_ref, b_ref, lo):
    x1 = _shift_down(xv, 1, xp)
    x2 = _shift_down(xv, 2, xp)
    x3 = _shift_down(xv, 3, xp)
    u = (((w_ref[0:1, lo:lo + LANES] * x3 + w_ref[1:2, lo:lo + LANES] * x2) + w_ref[2:3, lo:lo + LANES] * x1)
         + w_ref[3:4, lo:lo + LANES] * xv) + b_ref[:, lo:lo + LANES]
    return u, x1, x2, x3


def _mix_in_fwd(x2d, mod6, g_mix, w_in_t, tm, rider=None):
    s, d = x2d.shape
    din = w_in_t.shape[0]

    def body(x_ref, mod_ref, g_ref, w_ref, hn_ref, proj_ref):
        xhat, _ = _rms(x_ref[...])
        hn = ((xhat * g_ref[...]) * (1.0 + mod_ref[1:2, :]) + mod_ref[0:1, :]).astype(BF16)
        hn_ref[...] = hn
        proj_ref[...] = _dot(hn, w_ref[...], NT)

    return _call(
        body, "mix_in_fwd", (s // tm,),
        [pl.BlockSpec((tm, d), lambda i: (i, 0)), _full(mod6.shape), _full(g_mix.shape), _full(w_in_t.shape)],
        [pl.BlockSpec((tm, d), lambda i: (i, 0)), pl.BlockSpec((tm, din), lambda i: (i, 0))],
        [jax.ShapeDtypeStruct((s, d), BF16), jax.ShapeDtypeStruct((s, din), F32)],
        [x2d, mod6, g_mix, w_in_t], rider=rider)


def _mixer_fwd(proj, conv_sc, conv_lru, conv_b, wa_bd, wx_bd, ba, bx, lam, width, rider=None):
    s, din = proj.shape
    t = min(MIX_ROWS, s)
    nblk = width // LANES
    hb = t // SUBLANES

    def body(proj_ref, projp_ref, wsc_ref, wlru_ref, blru_ref, wa_ref, wx_ref, ba_ref, bx_ref, lam_ref,
             ymix_ref, h_ref, hc_ref):
        i = pl.program_id(0)

        @pl.when(i == 0)
        def _():
            hc_ref[...] = jnp.zeros_like(hc_ref)

        has_prev = i > 0
        for j in range(nblk):
            lo = j * LANES

            def col(p, ref=proj_ref):
                return ref[:, p * width + lo:p * width + lo + LANES]

            def prev(p):
                return jnp.where(has_prev, col(p, projp_ref), 0.0)

            p = col(1) * col(2)
            q, _, _ = _conv3(p, prev(1) * prev(2), wsc_ref, lo)
            ymix_ref[:, lo:lo + LANES] = (col(0) * q).astype(BF16)

            u, _, _, _ = _conv4(col(4), prev(4), wlru_ref, blru_ref, lo)
            sp = _softplus(-lam_ref[:, lo:lo + LANES])
            _, r, ig, a, mult = _lru_gates(u, wa_ref[j], wx_ref[j], ba_ref[:, lo:lo + LANES], bx_ref[:, lo:lo + LANES], sp)
            h = _scan_fwd(a, mult * (ig * u), hc_ref[0:1, lo:lo + LANES])
            h_ref[:, lo:lo + LANES] = h
            hc_ref[0:1, lo:lo + LANES] = h[t - 1:t, :]
            gel, _ = _gelu(col(3))
            ymix_ref[:, width + lo:width + lo + LANES] = (gel * h).astype(BF16)

    small = [conv_sc, conv_lru, conv_b, wa_bd, wx_bd, ba, bx, lam]
    return _call(
        body, "mixer_fwd", (s // t,),
        [pl.BlockSpec((t, din), lambda i: (i, 0)),
         pl.BlockSpec((SUBLANES, din), lambda i: (jnp.maximum(i * hb - 1, 0), 0))]
        + [_full(a.shape) for a in small],
        [pl.BlockSpec((t, 2 * width), lambda i: (i, 0)), pl.BlockSpec((t, width), lambda i: (i, 0))],
        [jax.ShapeDtypeStruct((s, 2 * width), BF16), jax.ShapeDtypeStruct((s, width), F32)],
        [proj, proj, *small], scratch=[pltpu.VMEM((SUBLANES, width), F32)], rider=rider)


def _mix_out_fwd(ymix, x2d, w_out, mod6, g_mlp, tm, rider=None):
    s, d = x2d.shape

    def body(y_ref, x_ref, w_ref, mod_ref, g_ref, mix_ref, x2_ref, hn_ref):
        mix = _dot(y_ref[...], w_ref[...], NN)
        mix_ref[...] = mix
        x2 = x_ref[...] + mod_ref[2:3, :] * mix
        x2_ref[...] = x2
        xhat, _ = _rms(x2)
        hn_ref[...] = ((xhat * g_ref[...]) * (1.0 + mod_ref[4:5, :]) + mod_ref[3:4, :]).astype(BF16)

    tile = pl.BlockSpec((tm, d), lambda i: (i, 0))
    return _call(
        body, "mix_out_fwd", (s // tm,),
        [tile, tile, _full(w_out.shape), _full(mod6.shape), _full(g_mlp.shape)],
        [tile, tile, tile],
        [jax.ShapeDtypeStruct((s, d), F32), jax.ShapeDtypeStruct((s, d), F32), jax.ShapeDtypeStruct((s, d), BF16)],
        [ymix, x2d, w_out, mod6, g_mlp], rider=rider)


def _mlp_fwd_loss(hn2, w_up_t, w_down, x2, target, mod6, g_final, tm, tk):
    s, d = hn2.shape
    f = w_up_t.shape[0]
    nk = f // tk

    def body(hn_ref, wu_ref, wd_ref, x2_ref, t_ref, mod_ref, g_ref, z_ref, dx3_ref, dyb_ref, st_ref, y_ref):
        i, k = pl.program_id(0), pl.program_id(1)

        @pl.when(jnp.logical_and(i == 0, k == 0))
        def _():
            st_ref[...] = jnp.zeros_like(st_ref)

        z = jnp.maximum(_dot(hn_ref[...], wu_ref[...], NT), 0.0)
        z_ref[...] = z.astype(BF16)
        part = _dot((z * z).astype(BF16), wd_ref[...], NN)

        @pl.when(k == 0)
        def _():
            y_ref[...] = part

        @pl.when(k > 0)
        def _():
            y_ref[...] += part

        @pl.when(k == nk - 1)
        def _():
            gate = mod_ref[5:6, :]
            yv = y_ref[...]
            xhat, rstd = _rms(x2_ref[...] + gate * yv)
            diff = xhat * g_ref[...] - t_ref[...]
            dyo = diff * (1.0 / d)
            dx3 = _rms_bwd(dyo * g_ref[...], xhat, rstd)
            dx3_ref[...] = dx3
            dyb_ref[...] = (gate * dx3).astype(BF16)
            st_ref[0:1, :] += _colsum(dyo * xhat)
            st_ref[1:2, :] += _colsum(dx3 * yv)
            st_ref[2:3, :] += _colsum(diff * diff)

    tile = pl.BlockSpec((tm, d), lambda i, k: (i, 0))
    wblk = pl.BlockSpec((tk, d), lambda i, k: (k, 0))
    return pl.pallas_call(
        body, name="mlp_fwd_loss", grid=(s // tm, nk),
        in_specs=[tile, wblk, wblk, tile, tile, _full(mod6.shape), _full(g_final.shape)],
        out_specs=[pl.BlockSpec((tm, tk), lambda i, k: (i, k)), tile, tile, _full((SUBLANES, d))],
        out_shape=[jax.ShapeDtypeStruct((s, f), BF16), jax.ShapeDtypeStruct((s, d), F32),
                   jax.ShapeDtypeStruct((s, d), BF16), jax.ShapeDtypeStruct((SUBLANES, d), F32)],
        scratch_shapes=[pltpu.VMEM((tm, d), F32)],
        compiler_params=_params(("arbitrary", "arbitrary")),
    )(hn2, w_up_t, w_down, x2, target, mod6, g_final)


def _mlp_bwd_dx(dyb, z, w_down, w_up_t, tm, tk):
    s, d = dyb.shape
    f = z.shape[1]

    def body(dy_ref, z_ref, wd_ref, wu_ref, dz_ref, dh_ref):
        k = pl.program_id(1)
        dz = ((2.0 * z_ref[...].astype(F32)) * _dot(dy_ref[...], wd_ref[...], NT)).astype(BF16)
        dz_ref[...] = dz
        part = _dot(dz, wu_ref[...], NN)

        @pl.when(k == 0)
        def _():
            dh_ref[...] = part

        @pl.when(k > 0)
        def _():
            dh_ref[...] += part

    return pl.pallas_call(
        body, name="mlp_bwd_dx", grid=(s // tm, f // tk),
        in_specs=[pl.BlockSpec((tm, d), lambda i, k: (i, 0)), pl.BlockSpec((tm, tk), lambda i, k: (i, k)),
                  pl.BlockSpec((tk, d), lambda i, k: (k, 0)), pl.BlockSpec((tk, d), lambda i, k: (k, 0))],
        out_specs=[pl.BlockSpec((tm, tk), lambda i, k: (i, k)), pl.BlockSpec((tm, d), lambda i, k: (i, 0))],
        out_shape=[jax.ShapeDtypeStruct((s, f), BF16), jax.ShapeDtypeStruct((s, d), F32)],
        compiler_params=_params(("parallel", "arbitrary")),
    )(dyb, z, w_down, w_up_t)


def _mlp_bwd_dw(z, dz, dyb, hn2, tm, tk):
    s, d = dyb.shape
    f = z.shape[1]

    def body(z_ref, dz_ref, dy_ref, hn_ref, gd_ref, gu_ref):
        i = pl.program_id(1)

        @pl.when(i == 0)
        def _():
            gd_ref[...] = jnp.zeros_like(gd_ref)
            gu_ref[...] = jnp.zeros_like(gu_ref)

        zf = z_ref[...].astype(F32)
        gd_ref[...] += _dot((zf * zf).astype(BF16), dy_ref[...], TN)
        gu_ref[...] += _dot(dz_ref[...], hn_ref[...], TN)

    return pl.pallas_call(
        body, name="mlp_bwd_dw", grid=(f // tk, s // tm),
        in_specs=[pl.BlockSpec((tm, tk), lambda k, i: (i, k)), pl.BlockSpec((tm, tk), lambda k, i: (i, k)),
                  pl.BlockSpec((tm, d), lambda k, i: (i, 0)), pl.BlockSpec((tm, d), lambda k, i: (i, 0))],
        out_specs=[pl.BlockSpec((tk, d), lambda k, i: (k, 0)), pl.BlockSpec((tk, d), lambda k, i: (k, 0))],
        out_shape=[jax.ShapeDtypeStruct((f, d), F32), jax.ShapeDtypeStruct((f, d), F32)],
        compiler_params=_params(("parallel", "arbitrary")),
    )(z, dz, dyb, hn2)


def _mix_out_bwd(dhn2, x2, dx3, mix, ymix, w_out, mod6, g_mlp, tm, rider=None):
    s, d = x2.shape

    def body(dh_ref, x2_ref, dx3_ref, mix_ref, y_ref, w_ref, mod_ref, g_ref, dx2_ref, dym_ref, gw_ref, st_ref):
        i = pl.program_id(0)

        @pl.when(i == 0)
        def _():
            st_ref[...] = jnp.zeros_like(st_ref)
            gw_ref[...] = jnp.zeros_like(gw_ref)

        dh = dh_ref[...]
        xhat, rstd = _rms(x2_ref[...])
        dn = dh * (1.0 + mod_ref[4:5, :])
        dx2 = dx3_ref[...] + _rms_bwd(dn * g_ref[...], xhat, rstd)
        dx2_ref[...] = dx2
        st_ref[0:1, :] += _colsum(dh)
        st_ref[1:2, :] += _colsum(dh * (xhat * g_ref[...]))
        st_ref[2:3, :] += _colsum(dn * xhat)
        st_ref[3:4, :] += _colsum(dx2 * mix_ref[...])
        dmix = (mod_ref[2:3, :] * dx2).astype(BF16)
        dym_ref[...] = _dot(dmix, w_ref[...], NT)
        gw_ref[...] += _dot(y_ref[...], dmix, TN)

    tile = pl.BlockSpec((tm, d), lambda i: (i, 0))
    return _call(
        body, "mix_out_bwd", (s // tm,),
        [tile, tile, tile, tile, tile, _full(w_out.shape), _full(mod6.shape), _full(g_mlp.shape)],
        [tile, tile, _full((d, d)), _full((SUBLANES, d))],
        [jax.ShapeDtypeStruct((s, d), F32), jax.ShapeDtypeStruct((s, d), F32),
         jax.ShapeDtypeStruct((d, d), F32), jax.ShapeDtypeStruct((SUBLANES, d), F32)],
        [dhn2, x2, dx3, mix, ymix, w_out, mod6, g_mlp], rider=rider)


def _mixer_bwd(proj, dymix, h_all, conv_sc, conv_lru, conv_b, wa_bd, wx_bd, ba, bx, lam, width, rider=None):
    s, din = proj.shape
    t = min(MIX_ROWS, s)
    nt = s // t
    nblk = width // LANES
    hb = t // SUBLANES
    last8 = s // SUBLANES - 1

    def body(proj_ref, projp_ref, projn_ref, dy_ref, dyn_ref, h_ref, hp_ref,
             wsc_ref, wlru_ref, blru_ref, wa_ref, wx_ref, ba_ref, bx_ref, lam_ref,
             dproj_ref, small_ref, gwa_ref, gwx_ref, an_ref, gn_ref, dun_ref):
        i = pl.program_id(0)

        @pl.when(i == 0)
        def _():
            small_ref[...] = jnp.zeros_like(small_ref)
            gwa_ref[...] = jnp.zeros_like(gwa_ref)
            gwx_ref[...] = jnp.zeros_like(gwx_ref)
            an_ref[...] = jnp.zeros_like(an_ref)
            gn_ref[...] = jnp.zeros_like(gn_ref)
            dun_ref[...] = jnp.zeros_like(dun_ref)

        has_prev = i < nt - 1
        has_next = i > 0
        for j in range(nblk):
            lo = j * LANES
            ls = slice(lo, lo + LANES)

            def col(p, ref=proj_ref):
                return ref[:, p * width + lo:p * width + lo + LANES]

            def prev(p):
                return jnp.where(has_prev, col(p, projp_ref), 0.0)

            def nxt(p):
                return jnp.where(has_next, col(p, projn_ref), 0.0)

            def add_row(r, v):
                small_ref[r:r + 1, ls] += _colsum(v)

            sc_b, sc_c, sc_x = col(0), col(1), col(2)
            p = sc_c * sc_x
            q, p1, p2 = _conv3(p, prev(1) * prev(2), wsc_ref, lo)
            dys = dy_ref[:, ls]
            dproj_ref[:, ls] = (dys * q).astype(BF16)
            dq = dys * sc_b
            dqn = jnp.where(has_next, dyn_ref[:, ls], 0.0) * nxt(0)
            dp = (wsc_ref[2:3, ls] * dq + wsc_ref[1:2, ls] * _shift_up(dq, 1, dqn)) + wsc_ref[0:1, ls] * _shift_up(dq, 2, dqn)
            dproj_ref[:, width + lo:width + lo + LANES] = (dp * sc_x).astype(BF16)
            dproj_ref[:, 2 * width + lo:2 * width + lo + LANES] = (dp * sc_c).astype(BF16)
            add_row(0, dq * p2)
            add_row(1, dq * p1)
            add_row(2, dq * p)

            xv = col(4)
            u, x1, x2, x3 = _conv4(xv, prev(4), wlru_ref, blru_ref, lo)
            lam_v = lam_ref[:, ls]
            sp = _softplus(-lam_v)
            wa, wx = wa_ref[j], wx_ref[j]
            ub, r, ig, a, mult = _lru_gates(u, wa, wx, ba_ref[:, ls], bx_ref[:, ls], sp)
            iu = ig * u
            h = h_ref[:, ls]
            hm1 = _shift_down(h, 1, jnp.where(has_prev, hp_ref[:, ls], 0.0))
            lyv = col(3)
            gel, th = _gelu(lyv)
            dyl = dy_ref[:, width + lo:width + lo + LANES]
            dproj_ref[:, 3 * width + lo:3 * width + lo + LANES] = (dyl * h * _dgelu(lyv, th)).astype(BF16)
            a_next = jnp.broadcast_to(an_ref[0:1, ls], (SUBLANES, LANES))
            g = _scan_rev(_shift_up(a, 1, a_next), dyl * gel, gn_ref[0:1, ls])
            an_ref[0:1, ls] = a[0:1, :]
            gn_ref[0:1, ls] = g[0:1, :]
            da = g * hm1
            dmult = g * iu
            diu = g * mult
            dlog_a = da * a - dmult * ((a * a) / mult)
            dpre_a = (dlog_a * (-RG_C * sp)) * (r * (1.0 - r))
            dpre_x = (diu * u) * (ig * (1.0 - ig))
            dab, dxb = dpre_a.astype(BF16), dpre_x.astype(BF16)
            du = diu * ig + _dot(dab, wa, NT) + _dot(dxb, wx, NT)
            gwa_ref[j] += _dot(ub, dab, TN)
            gwx_ref[j] += _dot(ub, dxb, TN)
            dun = dun_ref[:, ls]
            dun_ref[:, ls] = du[0:SUBLANES, :]
            dlx = (((wlru_ref[3:4, ls] * du + wlru_ref[2:3, ls] * _shift_up(du, 1, dun))
                    + wlru_ref[1:2, ls] * _shift_up(du, 2, dun)) + wlru_ref[0:1, ls] * _shift_up(du, 3, dun))
            dproj_ref[:, 4 * width + lo:4 * width + lo + LANES] = dlx.astype(BF16)
            add_row(3, du * x3)
            add_row(4, du * x2)
            add_row(5, du * x1)
            add_row(6, du * xv)
            add_row(7, du)
            add_row(8, dpre_a)
            add_row(9, dpre_x)
            add_row(10, (dlog_a * (RG_C * r)) * jax.nn.sigmoid(-lam_v))

    small = [conv_sc, conv_lru, conv_b, wa_bd, wx_bd, ba, bx, lam]
    rev = lambda i: nt - 1 - i
    return _call(
        body, "mixer_bwd", (nt,),
        [pl.BlockSpec((t, din), lambda i: (rev(i), 0)),
         pl.BlockSpec((SUBLANES, din), lambda i: (jnp.maximum(rev(i) * hb - 1, 0), 0)),
         pl.BlockSpec((SUBLANES, din), lambda i: (jnp.minimum((rev(i) + 1) * hb, last8), 0)),
         pl.BlockSpec((t, 2 * width), lambda i: (rev(i), 0)),
         pl.BlockSpec((SUBLANES, 2 * width), lambda i: (jnp.minimum((rev(i) + 1) * hb, last8), 0)),
         pl.BlockSpec((t, width), lambda i: (rev(i), 0)),
         pl.BlockSpec((SUBLANES, width), lambda i: (jnp.maximum(rev(i) * hb - 1, 0), 0))]
        + [_full(a.shape) for a in small],
        [pl.BlockSpec((t, din), lambda i: (rev(i), 0)), _full((2 * SUBLANES, width)),
         _full(wa_bd.shape), _full(wx_bd.shape)],
        [jax.ShapeDtypeStruct((s, din), BF16), jax.ShapeDtypeStruct((2 * SUBLANES, width), F32),
         jax.ShapeDtypeStruct(wa_bd.shape, F32), jax.ShapeDtypeStruct(wx_bd.shape, F32)],
        [proj, proj, proj, dymix, dymix, h_all, h_all, *small],
        scratch=[pltpu.VMEM((SUBLANES, width), F32), pltpu.VMEM((SUBLANES, width), F32),
                 pltpu.VMEM((SUBLANES, width), F32)], rider=rider)


def _mix_in_bwd_dx(dproj, x2d, dx2, w_in_t, mod6, g_mix, tm, rider=None):
    s, d = x2d.shape
    din = dproj.shape[1]

    def body(dp_ref, x_ref, dx2_ref, w_ref, mod_ref, g_ref, gx_ref, st_ref):
        i = pl.program_id(0)

        @pl.when(i == 0)
        def _():
            st_ref[...] = jnp.zeros_like(st_ref)

        dh = _dot(dp_ref[...], w_ref[...], NN)
        xhat, rstd = _rms(x_ref[...])
        dn = dh * (1.0 + mod_ref[1:2, :])
        gx_ref[...] = dx2_ref[...] + _rms_bwd(dn * g_ref[...], xhat, rstd)
        st_ref[0:1, :] += _colsum(dh)
        st_ref[1:2, :] += _colsum(dh * (xhat * g_ref[...]))
        st_ref[2:3, :] += _colsum(dn * xhat)

    tile = pl.BlockSpec((tm, d), lambda i: (i, 0))
    return _call(
        body, "mix_in_bwd_dx", (s // tm,),
        [pl.BlockSpec((tm, din), lambda i: (i, 0)), tile, tile, _full(w_in_t.shape), _full(mod6.shape),
         _full(g_mix.shape)],
        [tile, _full((SUBLANES, d))],
        [jax.ShapeDtypeStruct((s, d), F32), jax.ShapeDtypeStruct((SUBLANES, d), F32)],
        [dproj, x2d, dx2, w_in_t, mod6, g_mix], rider=rider)


def _mix_in_bwd_dw(dproj, hn1, tm, tn, rider=None):
    s, d = hn1.shape
    din = dproj.shape[1]

    def body(dp_ref, hn_ref, gw_ref):
        i = pl.program_id(1)

        @pl.when(i == 0)
        def _():
            gw_ref[...] = jnp.zeros_like(gw_ref)

        gw_ref[...] += _dot(dp_ref[...], hn_ref[...], TN)

    return _call(
        body, "mix_in_bwd_dw", (din // tn, s // tm),
        [pl.BlockSpec((tm, tn), lambda p, i: (i, p)), pl.BlockSpec((tm, d), lambda p, i: (i, 0))],
        [pl.BlockSpec((tn, d), lambda p, i: (p, 0))],
        [jax.ShapeDtypeStruct((din, d), F32)],
        [dproj, hn1], rider=rider)


def _adamw(w, g, m, v):
    m = ADAM_B1 * m + (1.0 - ADAM_B1) * g
    v = ADAM_B2 * v + (1.0 - ADAM_B2) * (g * g)
    m_hat = m / (1.0 - ADAM_B1 ** ADAM_STEP)
    v_hat = v / (1.0 - ADAM_B2 ** ADAM_STEP)
    delta = -ADAM_LR * (m_hat / (jnp.sqrt(v_hat) + ADAM_EPS) + ADAM_WD * w)
    return delta, m, v


def _pair_sum(g4, h4, core_chip, tr, name):
    _, _, r, n = g4.shape

    def body(sc_ref, g_ref, h_ref, sb_ref, own_ref):
        q = pl.program_id(1)
        ssum = g_ref[...] + h_ref[...]
        sb_ref[...] = ssum.astype(BF16)

        @pl.when(q == sc_ref[1])
        def _():
            own_ref[...] = ssum

    grid_spec = pltpu.PrefetchScalarGridSpec(
        num_scalar_prefetch=1, grid=(r // tr, 4),
        in_specs=[pl.BlockSpec((None, None, tr, n), lambda i, q, sc: (q, sc[0], i, 0)),
                  pl.BlockSpec((None, tr, n), lambda i, q, sc: (q, i, 0))],
        out_specs=[pl.BlockSpec((None, tr, n), lambda i, q, sc: (q, i, 0)),
                   pl.BlockSpec((tr, n), lambda i, q, sc: (i, 0))])
    return pl.pallas_call(
        body, name=name, grid_spec=grid_spec,
        out_shape=[jax.ShapeDtypeStruct((4, r, n), BF16), jax.ShapeDtypeStruct((r, n), F32)],
        compiler_params=_params(("parallel", "arbitrary")),
    )(core_chip, g4, h4)


def _sum4(own, parts, tr, name):
    r, n = own.shape

    def body(o_ref, p_ref, out_ref):
        acc = o_ref[...]
        for k in range(3):
            acc = acc + p_ref[k].astype(F32)
        out_ref[...] = acc

    return pl.pallas_call(
        body, name=name, grid=(r // tr,),
        in_specs=[pl.BlockSpec((tr, n), lambda i: (i, 0)), pl.BlockSpec((3, tr, n), lambda i: (0, i, 0))],
        out_specs=pl.BlockSpec((tr, n), lambda i: (i, 0)),
        out_shape=jax.ShapeDtypeStruct((r, n), F32),
        compiler_params=_params(("parallel",)),
    )(own, parts)


def _sum8(parts, tr, name):
    _, rows, n = parts.shape

    def body(p_ref, o_ref):
        acc = p_ref[0]
        for k in range(1, N_DEV):
            acc = acc + p_ref[k]
        o_ref[...] = acc

    return pl.pallas_call(
        body, name=name, grid=(rows // tr,),
        in_specs=[pl.BlockSpec((N_DEV, tr, n), lambda i: (0, i, 0))],
        out_specs=pl.BlockSpec((tr, n), lambda i: (i, 0)),
        out_shape=jax.ShapeDtypeStruct((rows, n), F32),
        compiler_params=_params(("parallel",)),
    )(parts)


def _adam_rows(w, g, m, v, tr, name):
    rows, n = w.shape

    def body(w_ref, g_ref, m_ref, v_ref, d_ref, nm_ref, nv_ref):
        d_ref[...], nm_ref[...], nv_ref[...] = _adamw(w_ref[...], g_ref[...], m_ref[...], v_ref[...])

    tile = pl.BlockSpec((tr, n), lambda i: (i, 0))
    return pl.pallas_call(
        body, name=name, grid=(rows // tr,),
        in_specs=[tile] * 4, out_specs=[tile] * 3,
        out_shape=[jax.ShapeDtypeStruct((rows, n), F32)] * 3,
        compiler_params=_params(("parallel",)),
    )(w, g, m, v)


def _ada_bwd_adam(cact_t, dmod_cols, w, m, v, tr):
    rows, n = w.shape

    def body(c_ref, d_ref, w_ref, m_ref, v_ref, g_ref, dl_ref, nm_ref, nv_ref):
        def term(b):
            return c_ref[b].astype(BF16).astype(F32) * d_ref[b:b + 1, :].astype(BF16).astype(F32)

        g = term(0)
        for b in range(1, N_DEV):
            g = g + term(b)
        g_ref[...] = g
        dl_ref[...], nm_ref[...], nv_ref[...] = _adamw(w_ref[...], g, m_ref[...], v_ref[...])

    tile = pl.BlockSpec((tr, n), lambda i: (i, 0))
    return pl.pallas_call(
        body, name="ada_bwd_adam", grid=(rows // tr,),
        in_specs=[pl.BlockSpec((N_DEV, tr, 1), lambda i: (0, i, 0)), _full(dmod_cols.shape), tile, tile, tile],
        out_specs=[tile] * 4,
        out_shape=[jax.ShapeDtypeStruct((rows, n), F32)] * 4,
        compiler_params=_params(("parallel",)),
    )(cact_t, dmod_cols, w, m, v)


def _adam_small(ws, gs, ms, vs):
    n = len(ws)

    def body(*refs):
        w_r, g_r, m_r, v_r = refs[:n], refs[n:2 * n], refs[2 * n:3 * n], refs[3 * n:4 * n]
        d_r, nm_r, nv_r = refs[4 * n:5 * n], refs[5 * n:6 * n], refs[6 * n:7 * n]
        for k in range(n):
            d_r[k][...], nm_r[k][...], nv_r[k][...] = _adamw(w_r[k][...], g_r[k][...], m_r[k][...], v_r[k][...])

    shapes = [jax.ShapeDtypeStruct(w.shape, F32) for w in ws]
    outs = pl.pallas_call(
        body, name="adam_small", out_shape=shapes * 3, compiler_params=_params(),
    )(*ws, *gs, *ms, *vs)
    return outs[:n], outs[n:2 * n], outs[2 * n:]


def _block_diag(w):
    h, hd, _ = w.shape
    per = LANES // hd
    eye = jnp.eye(per, dtype=w.dtype)
    w5 = w.reshape(h // per, per, hd, 1, hd) * eye[None, :, None, :, None]
    return w5.reshape(h // per, LANES, LANES)


def _block_diag_grad(g, h, hd):
    per = LANES // hd
    g5 = g.reshape(h // per, per, hd, per, hd)
    return jnp.stack([g5[:, a, :, a, :] for a in range(per)], axis=1).reshape(h, hd, hd)


def kernel(x, c, w_ada, b_ada, g_mix, w_in, conv_w_sc, conv_w_lru, conv_b_lru, w_rg_a, b_rg_a, w_rg_x, b_rg_x, lru_lambda, w_out, g_mlp, w_up, w_down, g_final, loss_target, m_w_ada, m_b_ada, m_g_mix, m_w_in, m_conv_w_sc, m_conv_w_lru, m_conv_b_lru, m_w_rg_a, m_b_rg_a, m_w_rg_x, m_b_rg_x, m_lru_lambda, m_w_out, m_g_mlp, m_w_up, m_w_down, m_g_final, v_w_ada, v_b_ada, v_g_mix, v_w_in, v_conv_w_sc, v_conv_w_lru, v_conv_b_lru, v_w_rg_a, v_b_rg_a, v_w_rg_x, v_b_rg_x, v_lru_lambda, v_w_out, v_g_mlp, v_w_up, v_w_down, v_g_final):
    s, d = x.shape[1], x.shape[2]
    width = conv_b_lru.shape[1]
    heads, hd = w_rg_a.shape[1], w_rg_a.shape[2]
    f = w_down.shape[1] * N_DEV
    n_ada = w_ada.shape[2]
    csh = conv_w_sc.shape[2]
    me = 4 * lax.axis_index("x") + 2 * lax.axis_index("y") + lax.axis_index("c")
    tm = min(512, s)
    tm_mlp = min(1024, s)
    tk = 512

    x2d = x[0]
    tgt = loss_target[0]

    pay = jnp.zeros((SUBLANES, d), F32)
    pay = pay.at[0:1, :].set(c)
    pay = pay.at[1:4, 0:csh].set(conv_w_sc[0])
    pay = pay.at[4:8, 0:csh].set(conv_w_lru[0])
    w_in_t_sh = w_in[0].T.astype(BF16)
    w_up_t_sh = w_up[0].T.astype(BF16)
    w_out_sh = w_out[0].astype(BF16)
    w_down_sh = w_down[0].astype(BF16)
    pay_all, w_in_t = _gather2("gather_in", [pay, w_in_t_sh])
    w_in_t = w_in_t.reshape(-1, d)
    c_all = pay_all[:, 0, :]
    conv_sc = pay_all[:, 1:4, 0:csh].transpose(1, 0, 2).reshape(3, width)
    conv_lru = pay_all[:, 4:8, 0:csh].transpose(1, 0, 2).reshape(4, width)

    b_ada_sh = lax.dynamic_slice(b_ada, (0, me * n_ada), (1, n_ada))
    mod_cols, c_act = _ada_fwd(c_all, w_ada[0], b_ada_sh)
    (mod_rows,) = _exchange("scatter_mod", [], [mod_cols.reshape(N_DEV, 1, n_ada)])
    mod6 = jnp.zeros((SUBLANES, d), F32).at[0:6, :].set(mod_rows.reshape(6, d))

    wa_bd = _block_diag(w_rg_a[0]).astype(BF16)
    wx_bd = _block_diag(w_rg_x[0]).astype(BF16)
    ba = b_rg_a.reshape(1, width)
    bx = b_rg_x.reshape(1, width)
    g_fin = g_final.reshape(1, d)

    (hn1, proj), (w_out_g,) = _mix_in_fwd(
        x2d, mod6, g_mix, w_in_t, tm, rider=_ride_gather_ici([w_out_sh]))
    (ymix, h_all), (w_out_g, w_up_g) = _mixer_fwd(
        proj, conv_sc, conv_lru, conv_b_lru, wa_bd, wx_bd, ba, bx, lru_lambda, width,
        rider=_merge_riders(_ride_gather_d2d([w_out_g]), _ride_gather_ici([w_up_t_sh])))
    w_out_b = w_out_g.reshape(-1, d)
    (mix, x2, hn2), (w_up_g, w_down_g) = _mix_out_fwd(
        ymix, x2d, w_out_b, mod6, g_mlp, tm,
        rider=_merge_riders(_ride_gather_d2d([w_up_g]), _ride_gather_ici([w_down_sh])))
    (w_down_g,) = _comm("forward_w_down", _ride_gather_d2d([w_down_g]))
    w_up_t = w_up_g.reshape(-1, d)
    w_down_b = w_down_g.reshape(-1, d)
    z, dx3, dyb, st_fin = _mlp_fwd_loss(hn2, w_up_t, w_down_b, x2, tgt, mod6, g_fin, tm, 4 * tk)

    core_chip = jnp.stack([lax.axis_index("c"), 2 * lax.axis_index("x") + lax.axis_index("y")]).astype(jnp.int32)
    dz, dhn2 = _mlp_bwd_dx(dyb, z, w_down_b, w_up_t, tm, 4 * tk)
    g_down, g_up_t = _mlp_bwd_dw(z, dz, dyb, hn2, tm_mlp, 2 * tk)
    g_up4, g_down4 = g_up_t.reshape(4, 2, -1, d), g_down.reshape(4, 2, -1, d)
    (dx2, dymix, g_out, st_out), (h_up, h_down) = _mix_out_bwd(
        dhn2, x2, dx3, mix, ymix, w_out_b, mod6, g_mlp, tm, rider=_ride_pair_swap([g_up4, g_down4]))
    sb_up, own_up = _pair_sum(g_up4, h_up, core_chip, 256, "pair_sum_w_up")
    sb_down, own_down = _pair_sum(g_down4, h_down, core_chip, 256, "pair_sum_w_down")
    g_out4 = g_out.reshape(4, 2, -1, d)
    (dproj, g_small, g_wa, g_wx), (p_up, p_down, h_out) = _mixer_bwd(
        proj, dymix, h_all, conv_sc, conv_lru, conv_b_lru, wa_bd, wx_bd, ba, bx, lru_lambda, width,
        rider=_merge_riders(_ride_chip_exchange([sb_up, sb_down]), _ride_pair_swap([g_out4])))
    sb_out, own_out = _pair_sum(g_out4, h_out, core_chip, g_out4.shape[2], "pair_sum_w_out")
    (grad_x, st_in), _ = _mix_in_bwd_dx(dproj, x2d, dx2, w_in_t, mod6, g_mix, tm)

    small = jnp.concatenate([
        st_in[0:2], st_out[3:4], st_out[0:2], st_fin[1:2],
        st_in[2:3], st_out[2:3], st_fin[0:1],
        jnp.concatenate([g_small[7:8], g_small[10:11]], axis=1),
        jnp.concatenate([g_small[8:9], g_small[9:10]], axis=1),
        jnp.concatenate([jnp.concatenate([g_small[0:3], jnp.zeros((1, width), F32)], axis=0), g_small[3:7]], axis=1),
        st_fin[2:3],
        _block_diag_grad(g_wa, heads, hd).reshape(-1, d),
        _block_diag_grad(g_wx, heads, hd).reshape(-1, d),
    ], axis=0)

    (g_in_t,), (p_out, small_all) = _mix_in_bwd_dw(
        dproj, hn1, tm_mlp, 512, rider=_merge_riders(_ride_chip_exchange([sb_out]), _ride_gather_direct([small])))
    g_in4 = g_in_t.reshape(4, 2, -1, d)
    (h_in,) = _comm("swap_w_in", _ride_pair_swap([g_in4]))
    sb_in, own_in = _pair_sum(g_in4, h_in, core_chip, g_in4.shape[2], "pair_sum_w_in")
    (p_in,) = _comm("exchange_w_in", _ride_chip_exchange([sb_in]))

    gs_in = _sum4(own_in, p_in, own_in.shape[0], "sum_w_in").T
    gs_up = _sum4(own_up, p_up, 256, "sum_w_up").T
    gs_out = _sum4(own_out, p_out, own_out.shape[0], "sum_w_out")
    gs_down = _sum4(own_down, p_down, 256, "sum_w_down")
    ad_in = _adam_rows(w_in[0], gs_in, m_w_in[0], v_w_in[0], 256, "adam_w_in")
    ad_up = _adam_rows(w_up[0], gs_up, m_w_up[0], v_w_up[0], 256, "adam_w_up")
    ad_out = _adam_rows(w_out[0], gs_out, m_w_out[0], v_w_out[0], w_out.shape[1], "adam_w_out")
    ad_down = _adam_rows(w_down[0], gs_down, m_w_down[0], v_w_down[0], 256, "adam_w_down")

    gsum = _sum8(small_all, SMALL_ROWS, "sum_small")
    loss = (0.5 / d) * jnp.sum(gsum[15])
    dmod_cols = lax.dynamic_slice(small_all[:, 0:6, :].reshape(N_DEV, 6 * d), (0, me * n_ada), (N_DEV, n_ada))
    g_ada, d_ada, nm_ada, nv_ada = _ada_bwd_adam(c_act[:, :, None], dmod_cols, w_ada[0], m_w_ada[0], v_w_ada[0], 256)

    g_conv = lax.dynamic_slice(gsum[11:15, 0:width], (0, me * csh), (4, csh))
    g_conv_l = lax.dynamic_slice(gsum[11:15, width:2 * width], (0, me * csh), (4, csh))
    small_g = [
        gsum[0:6].reshape(1, 6 * d),
        gsum[6:7],
        g_conv[0:3].reshape(1, 3, csh),
        g_conv_l.reshape(1, 4, csh),
        gsum[9:10, 0:width],
        gsum[16:48].reshape(1, heads, hd, hd),
        gsum[10:11, 0:width].reshape(1, heads, hd),
        gsum[48:80].reshape(1, heads, hd, hd),
        gsum[10:11, width:].reshape(1, heads, hd),
        gsum[9:10, width:],
        gsum[7:8],
        gsum[8],
    ]
    small_w = [b_ada, g_mix, conv_w_sc, conv_w_lru, conv_b_lru, w_rg_a, b_rg_a, w_rg_x, b_rg_x, lru_lambda, g_mlp, g_final]
    small_m = [m_b_ada, m_g_mix, m_conv_w_sc, m_conv_w_lru, m_conv_b_lru, m_w_rg_a, m_b_rg_a, m_w_rg_x, m_b_rg_x,
               m_lru_lambda, m_g_mlp, m_g_final]
    small_v = [v_b_ada, v_g_mix, v_conv_w_sc, v_conv_w_lru, v_conv_b_lru, v_w_rg_a, v_b_rg_a, v_w_rg_x, v_b_rg_x,
               v_lru_lambda, v_g_mlp, v_g_final]
    sd, snm, snv = _adam_small(small_w, small_g, small_m, small_v)

    def order(ada, w_in_, w_out_, w_up_, w_down_, sm):
        return [ada[None], sm[0], sm[1], w_in_[None], sm[2], sm[3], sm[4], sm[5], sm[6], sm[7], sm[8], sm[9],
                w_out_[None], sm[10], w_up_[None], w_down_[None], sm[11]]

    grads = order(g_ada, gs_in, gs_out, gs_up, gs_down, small_g)
    deltas = order(d_ada, ad_in[0], ad_out[0], ad_up[0], ad_down[0], sd)
    new_m = order(nm_ada, ad_in[1], ad_out[1], ad_up[1], ad_down[1], snm)
    new_v = order(nv_ada, ad_in[2], ad_out[2], ad_up[2], ad_down[2], snv)
    return (loss, grad_x[None], *grads, *deltas, *new_m, *new_v)
```

```python
import functools

import jax
import jax.numpy as jnp
from jax import lax
from jax.experimental import pallas as pl
from jax.experimental.pallas import tpu as pltpu
from jax.experimental.pallas import tpu_sc as plsc

F32 = jnp.float32
BF16 = jnp.bfloat16
N_DEV = 8
EPS = 1e-6
RG_C = 8.0
GELU_K0 = 0.7978845608028654
GELU_K1 = 0.044715
ADAM_LR = 0.001
ADAM_B1 = 0.9
ADAM_B2 = 0.999
ADAM_EPS = 1e-08
ADAM_WD = 0.01
ADAM_STEP = 10
LANES = 128
SUBLANES = 8
VMEM_LIMIT = 52 * 1024 * 1024
MIX_ROWS = 256
SMALL_ROWS = 80

MESH = pl.DeviceIdType.MESH
ANY = pl.BlockSpec(memory_space=pl.ANY)
NN = ((1,), (0,))
NT = ((1,), (1,))
TN = ((0,), (0,))


def _dot(a, b, dims):
    return lax.dot_general(a, b, (dims, ((), ())), preferred_element_type=F32)


def _params(sem=None):
    return pltpu.CompilerParams(dimension_semantics=sem, vmem_limit_bytes=VMEM_LIMIT)


def _full(shape):
    nd = len(shape)
    return pl.BlockSpec(shape, lambda *_: (0,) * nd)


def _exchange(name, gathers, scatters):
    n_g = len(gathers)
    arrs = list(gathers) + list(scatters)
    n = len(arrs)
    out_shape = [jax.ShapeDtypeStruct((N_DEV,) + a.shape, a.dtype) for a in gathers]
    out_shape += [jax.ShapeDtypeStruct(a.shape, a.dtype) for a in scatters]

    def body(*refs):
        ins, outs = refs[:n], refs[n:2 * n]
        send_sems, recv_sems, local_sems = refs[2 * n:]
        x, y, c = lax.axis_index("x"), lax.axis_index("y"), lax.axis_index("c")
        me = 4 * x + 2 * y + c

        def src(a, dev):
            return ins[a] if a < n_g else ins[a].at[dev]

        def peer_of(k):
            px = 1 - x if (k >> 2) & 1 else x
            py = 1 - y if (k >> 1) & 1 else y
            pc = 1 - c if k & 1 else c
            return (px, py, pc), 4 * px + 2 * py + pc

        local = [pltpu.make_async_copy(src(a, me), outs[a].at[me], local_sems.at[a]) for a in range(n)]
        for cp in local:
            cp.start()
        sends = []
        for k in range(1, N_DEV):
            peer, pidx = peer_of(k)
            for a in range(n):
                cp = pltpu.make_async_remote_copy(
                    src_ref=src(a, pidx), dst_ref=outs[a].at[me],
                    send_sem=send_sems.at[a * (N_DEV - 1) + k - 1], recv_sem=recv_sems.at[a * (N_DEV - 1) + k - 1],
                    device_id=peer, device_id_type=MESH)
                cp.start()
                sends.append(cp)
        for k in range(1, N_DEV):
            peer, pidx = peer_of(k)
            for a in range(n):
                pltpu.make_async_remote_copy(
                    src_ref=src(a, pidx), dst_ref=outs[a].at[pidx],
                    send_sem=send_sems.at[a * (N_DEV - 1) + k - 1], recv_sem=recv_sems.at[a * (N_DEV - 1) + k - 1],
                    device_id=peer, device_id_type=MESH).wait_recv()
        for cp in sends:
            cp.wait_send()
        for cp in local:
            cp.wait()

    return pl.pallas_call(
        body, name=name, out_shape=out_shape,
        in_specs=[ANY] * n, out_specs=[ANY] * n,
        scratch_shapes=[pltpu.SemaphoreType.DMA((n * (N_DEV - 1),)),
                        pltpu.SemaphoreType.DMA((n * (N_DEV - 1),)),
                        pltpu.SemaphoreType.DMA((n,))],
    )(*arrs)


def _gather2(name, arrs):
    n = len(arrs)
    per = 7
    out_shape = [jax.ShapeDtypeStruct((N_DEV,) + a.shape, a.dtype) for a in arrs]

    def body(*refs):
        ins, outs = refs[:n], refs[n:2 * n]
        send_sems, recv_sems, local_sems = refs[2 * n:]
        x, y, c = lax.axis_index("x"), lax.axis_index("y"), lax.axis_index("c")
        sib = (x, y, 1 - c)
        chips = [(1 - x, y), (x, 1 - y), (1 - x, 1 - y)]

        def slot(a, px, py, pc):
            return outs[a].at[4 * px + 2 * py + pc]

        def copy(a, k, block, to, src=None):
            return pltpu.make_async_remote_copy(
                src_ref=slot(a, *block) if src is None else src, dst_ref=slot(a, *block),
                send_sem=send_sems.at[a * per + k], recv_sem=recv_sems.at[a * per + k],
                device_id=to, device_id_type=MESH)

        local = [pltpu.make_async_copy(ins[a], slot(a, x, y, c), local_sems.at[a]) for a in range(n)]
        for cp in local:
            cp.start()
        first = []
        for a in range(n):
            first += [copy(a, 1 + j, (x, y, c), (*chip, c), src=ins[a]) for j, chip in enumerate(chips)]
        for a in range(n):
            first.append(copy(a, 0, (x, y, c), sib, src=ins[a]))
        for cp in first:
            cp.start()
        passed = []
        for a in range(n):
            for j, chip in enumerate(chips):
                copy(a, 1 + j, (*chip, c), (x, y, c)).wait_recv()
                cp = copy(a, 4 + j, (*chip, c), sib)
                cp.start()
                passed.append(cp)
        for a in range(n):
            copy(a, 0, sib, (x, y, c)).wait_recv()
            for j, chip in enumerate(chips):
                copy(a, 4 + j, (*chip, 1 - c), (x, y, c)).wait_recv()
        for cp in first + passed:
            cp.wait_send()
        for cp in local:
            cp.wait()

    return pl.pallas_call(
        body, name=name, out_shape=out_shape,
        in_specs=[ANY] * n, out_specs=[ANY] * n,
        scratch_shapes=[pltpu.SemaphoreType.DMA((n * per,)), pltpu.SemaphoreType.DMA((n * per,)),
                        pltpu.SemaphoreType.DMA((n,))],
    )(*arrs)


def _seq_gather2(name, collective_id, arrs):
    n = len(arrs)
    per = 7

    def body(*refs):
        ins, outs = refs[:n], refs[n:2 * n]
        send_sems, recv_sems, local_sems = refs[2 * n:]
        x, y, c = lax.axis_index("x"), lax.axis_index("y"), lax.axis_index("c")
        sib = (x, y, 1 - c)
        chips = [(1 - x, y), (x, 1 - y), (1 - x, 1 - y)]
        barrier = pltpu.get_barrier_semaphore()
        for peer in [sib] + [(*chip, c) for chip in chips]:
            pl.semaphore_signal(barrier, inc=1, device_id=peer, device_id_type=MESH)
        pl.semaphore_wait(barrier, 4)

        def slot(a, px, py, pc):
            return outs[a].at[4 * px + 2 * py + pc]

        def copy(a, k, block, to, src=None):
            return pltpu.make_async_remote_copy(
                src_ref=slot(a, *block) if src is None else src, dst_ref=slot(a, *block),
                send_sem=send_sems.at[a * per + k], recv_sem=recv_sems.at[a * per + k],
                device_id=to, device_id_type=MESH)

        local = [pltpu.make_async_copy(ins[a], slot(a, x, y, c), local_sems.at[a]) for a in range(n)]
        for cp in local:
            cp.start()
        first = []
        for a in range(n):
            first += [copy(a, 1 + j, (x, y, c), (*chip, c), src=ins[a]) for j, chip in enumerate(chips)]
        for a in range(n):
            first.append(copy(a, 0, (x, y, c), sib, src=ins[a]))
        for cp in first:
            cp.start()
        passed = []
        for a in range(n):
            for j, chip in enumerate(chips):
                copy(a, 1 + j, (*chip, c), (x, y, c)).wait_recv()
                cp = copy(a, 4 + j, (*chip, c), sib)
                cp.start()
                passed.append(cp)
        for a in range(n):
            copy(a, 0, sib, (x, y, c)).wait_recv()
            for j, chip in enumerate(chips):
                copy(a, 4 + j, (*chip, 1 - c), (x, y, c)).wait_recv()
        for cp in first + passed:
            cp.wait_send()
        for cp in local:
            cp.wait()

    return pl.kernel(
        body, out_type=[jax.ShapeDtypeStruct((N_DEV,) + a.shape, a.dtype) for a in arrs],
        mesh=plsc.ScalarSubcoreMesh(axis_name="seq", num_cores=1),
        scratch_types=[pltpu.SemaphoreType.DMA((n * per,)), pltpu.SemaphoreType.DMA((n * per,)),
                       pltpu.SemaphoreType.DMA((n,))],
        compiler_params=pltpu.CompilerParams(collective_id=collective_id), name=name,
    )(*arrs)


def _pair_swap(name, arrs):
    n = len(arrs)
    out_shape = [jax.ShapeDtypeStruct((4,) + a.shape[2:], a.dtype) for a in arrs]

    def body(*refs):
        ins, outs = refs[:n], refs[n:2 * n]
        send_sems, recv_sems = refs[2 * n:]
        x, y, c = lax.axis_index("x"), lax.axis_index("y"), lax.axis_index("c")

        def copy(a, q):
            return pltpu.make_async_remote_copy(
                src_ref=ins[a].at[q, 1 - c], dst_ref=outs[a].at[q],
                send_sem=send_sems.at[a * 4 + q], recv_sem=recv_sems.at[a * 4 + q],
                device_id=(x, y, 1 - c), device_id_type=MESH)

        cps = [copy(a, q) for a in range(n) for q in range(4)]
        for cp in cps:
            cp.start()
        for cp in cps:
            cp.wait_recv()
        for cp in cps:
            cp.wait_send()

    return pl.pallas_call(
        body, name=name, out_shape=out_shape,
        in_specs=[ANY] * n, out_specs=[ANY] * n,
        scratch_shapes=[pltpu.SemaphoreType.DMA((n * 4,)), pltpu.SemaphoreType.DMA((n * 4,))],
    )(*arrs)


def _chip_exchange(name, arrs):
    n = len(arrs)
    out_shape = [jax.ShapeDtypeStruct((3,) + a.shape[1:], a.dtype) for a in arrs]

    def body(*refs):
        ins, outs = refs[:n], refs[n:2 * n]
        send_sems, recv_sems = refs[2 * n:]
        x, y, c = lax.axis_index("x"), lax.axis_index("y"), lax.axis_index("c")

        def copy(a, k):
            px = 1 - x if (k >> 1) & 1 else x
            py = 1 - y if k & 1 else y
            return pltpu.make_async_remote_copy(
                src_ref=ins[a].at[2 * px + py], dst_ref=outs[a].at[k - 1],
                send_sem=send_sems.at[a * 3 + k - 1], recv_sem=recv_sems.at[a * 3 + k - 1],
                device_id=(px, py, c), device_id_type=MESH)

        cps = [copy(a, k) for a in range(n) for k in (1, 2, 3)]
        for cp in cps:
            cp.start()
        for cp in cps:
            cp.wait_recv()
        for cp in cps:
            cp.wait_send()

    return pl.pallas_call(
        body, name=name, out_shape=out_shape,
        in_specs=[ANY] * n, out_specs=[ANY] * n,
        scratch_shapes=[pltpu.SemaphoreType.DMA((n * 3,)), pltpu.SemaphoreType.DMA((n * 3,))],
    )(*arrs)


class _Rider:
    def __init__(self, arrays, out_shapes, n_sems, build, aliases=None):
        self.arrays, self.out_shapes, self.n_sems, self.build = list(arrays), list(out_shapes), n_sems, build
        self.aliases = dict(aliases or {})


def _merge_riders(r1, r2):
    n1i, n1o, n1s = len(r1.arrays), len(r1.out_shapes), r1.n_sems

    def build(ins, outs, send_sems, recv_sems):
        a = r1.build(ins[:n1i], outs[:n1o], send_sems.at[pl.ds(0, n1s)], recv_sems.at[pl.ds(0, n1s)])
        b = r2.build(ins[n1i:], outs[n1o:], send_sems.at[pl.ds(n1s, r2.n_sems)], recv_sems.at[pl.ds(n1s, r2.n_sems)])
        return tuple(p + q for p, q in zip(a, b))

    aliases = dict(r1.aliases)
    aliases.update({k + n1i: v + n1o for k, v in r2.aliases.items()})
    return _Rider(r1.arrays + r2.arrays, r1.out_shapes + r2.out_shapes, n1s + r2.n_sems, build, aliases)


def _place():
    x, y, c = lax.axis_index("x"), lax.axis_index("y"), lax.axis_index("c")
    chips = [(1 - x, y), (x, 1 - y), (1 - x, 1 - y)]
    return x, y, c, chips


def _ride_gather_ici(arrs):
    n = len(arrs)

    def build(ins, outs, send_sems, recv_sems):
        x, y, c, chips = _place()
        peers = [(*chip, c) for chip in chips] + [(x, y, 1 - c)]
        me = 4 * x + 2 * y + c
        local = [pltpu.make_async_copy(ins[a], outs[a].at[me], send_sems.at[a * 5 + 4]) for a in range(n)]
        sends, recvs = [], []
        for a in range(n):
            for j, (px, py, pc) in enumerate(peers):
                sends.append(pltpu.make_async_remote_copy(
                    src_ref=ins[a], dst_ref=outs[a].at[me], send_sem=send_sems.at[a * 5 + j],
                    recv_sem=recv_sems.at[a * 5 + j], device_id=(px, py, pc), device_id_type=MESH))
                recvs.append(pltpu.make_async_remote_copy(
                    src_ref=ins[a], dst_ref=outs[a].at[4 * px + 2 * py + pc], send_sem=send_sems.at[a * 5 + j],
                    recv_sem=recv_sems.at[a * 5 + j], device_id=(px, py, pc), device_id_type=MESH))
        return local, sends, recvs

    shapes = [jax.ShapeDtypeStruct((N_DEV,) + a.shape, a.dtype) for a in arrs]
    return _Rider(arrs, shapes, n * 5, build)


def _ride_gather_direct(arrs):
    n = len(arrs)

    def build(ins, outs, send_sems, recv_sems):
        x, y, c, _ = _place()
        me = 4 * x + 2 * y + c
        local = [pltpu.make_async_copy(ins[a], outs[a].at[me], send_sems.at[a * N_DEV + 7]) for a in range(n)]
        sends, recvs = [], []
        for a in range(n):
            for k in range(1, N_DEV):
                px = 1 - x if (k >> 2) & 1 else x
                py = 1 - y if (k >> 1) & 1 else y
                pc = 1 - c if k & 1 else c
                sem = a * N_DEV + k - 1
                sends.append(pltpu.make_async_remote_copy(
                    src_ref=ins[a], dst_ref=outs[a].at[me], send_sem=send_sems.at[sem], recv_sem=recv_sems.at[sem],
                    device_id=(px, py, pc), device_id_type=MESH))
                recvs.append(pltpu.make_async_remote_copy(
                    src_ref=ins[a], dst_ref=outs[a].at[4 * px + 2 * py + pc], send_sem=send_sems.at[sem],
                    recv_sem=recv_sems.at[sem], device_id=(px, py, pc), device_id_type=MESH))
        return local, sends, recvs

    shapes = [jax.ShapeDtypeStruct((N_DEV,) + a.shape, a.dtype) for a in arrs]
    return _Rider(arrs, shapes, n * N_DEV, build)


def _ride_gather_d2d(gathered):
    n = len(gathered)

    def build(ins, outs, send_sems, recv_sems):
        x, y, c, chips = _place()
        sends, recvs = [], []
        for a in range(n):
            for j, (px, py) in enumerate(chips):
                mine = outs[a].at[4 * px + 2 * py + c]
                theirs = outs[a].at[4 * px + 2 * py + 1 - c]
                sends.append(pltpu.make_async_remote_copy(
                    src_ref=mine, dst_ref=mine, send_sem=send_sems.at[a * 3 + j], recv_sem=recv_sems.at[a * 3 + j],
                    device_id=(x, y, 1 - c), device_id_type=MESH))
                recvs.append(pltpu.make_async_remote_copy(
                    src_ref=mine, dst_ref=theirs, send_sem=send_sems.at[a * 3 + j], recv_sem=recv_sems.at[a * 3 + j],
                    device_id=(x, y, 1 - c), device_id_type=MESH))
        return [], sends, recvs

    shapes = [jax.ShapeDtypeStruct(a.shape, a.dtype) for a in gathered]
    return _Rider(gathered, shapes, n * 3, build, aliases={a: a for a in range(n)})


def _ride_pair_swap(arrs):
    n = len(arrs)

    def build(ins, outs, send_sems, recv_sems):
        x, y, c, _ = _place()
        cps = [pltpu.make_async_remote_copy(
            src_ref=ins[a].at[q, 1 - c], dst_ref=outs[a].at[q], send_sem=send_sems.at[a * 4 + q],
            recv_sem=recv_sems.at[a * 4 + q], device_id=(x, y, 1 - c), device_id_type=MESH)
            for a in range(n) for q in range(4)]
        return [], cps, cps

    shapes = [jax.ShapeDtypeStruct((4,) + a.shape[2:], a.dtype) for a in arrs]
    return _Rider(arrs, shapes, n * 4, build)


def _ride_chip_exchange(arrs):
    n = len(arrs)

    def build(ins, outs, send_sems, recv_sems):
        x, y, c, _ = _place()
        cps = []
        for a in range(n):
            for k in (1, 2, 3):
                px = 1 - x if (k >> 1) & 1 else x
                py = 1 - y if k & 1 else y
                cps.append(pltpu.make_async_remote_copy(
                    src_ref=ins[a].at[2 * px + py], dst_ref=outs[a].at[k - 1], send_sem=send_sems.at[a * 3 + k - 1],
                    recv_sem=recv_sems.at[a * 3 + k - 1], device_id=(px, py, c), device_id_type=MESH))
        return [], cps, cps

    shapes = [jax.ShapeDtypeStruct((3,) + a.shape[1:], a.dtype) for a in arrs]
    return _Rider(arrs, shapes, n * 3, build)


def _call(body, name, grid, in_specs, out_specs, out_shape, args, scratch=(), rider=None):
    n_in, n_out, n_scr = len(in_specs), len(out_specs), len(scratch)
    sem = ("arbitrary",) * len(grid)
    if rider is None:
        outs = pl.pallas_call(
            body, name=name, grid=grid, in_specs=in_specs, out_specs=out_specs, out_shape=out_shape,
            scratch_shapes=list(scratch), compiler_params=_params(sem))(*args)
        return outs, []
    ri, ro = len(rider.arrays), len(rider.out_shapes)

    def riding(*refs):
        ins, r_ins = refs[:n_in], refs[n_in:n_in + ri]
        outs = refs[n_in + ri:n_in + ri + n_out]
        r_outs = refs[n_in + ri + n_out:n_in + ri + n_out + ro]
        scr = refs[n_in + ri + n_out + ro:n_in + ri + n_out + ro + n_scr]
        send_sems, recv_sems = refs[-2:]
        first = functools.reduce(jnp.logical_and, [pl.program_id(k) == 0 for k in range(len(grid))])
        last = functools.reduce(jnp.logical_and, [pl.program_id(k) == grid[k] - 1 for k in range(len(grid))])

        @pl.when(first)
        def _():
            local, sends, _ = rider.build(r_ins, r_outs, send_sems, recv_sems)
            for cp in local + sends:
                cp.start()

        body(*ins, *outs, *scr)

        @pl.when(last)
        def _():
            local, sends, recvs = rider.build(r_ins, r_outs, send_sems, recv_sems)
            for cp in recvs:
                cp.wait_recv()
            for cp in sends:
                cp.wait_send()
            for cp in local:
                cp.wait()

    outs = pl.pallas_call(
        riding, name=name, grid=grid,
        in_specs=list(in_specs) + [ANY] * ri, out_specs=list(out_specs) + [ANY] * ro,
        out_shape=list(out_shape) + rider.out_shapes,
        scratch_shapes=list(scratch) + [pltpu.SemaphoreType.DMA((rider.n_sems,)), pltpu.SemaphoreType.DMA((rider.n_sems,))],
        input_output_aliases={n_in + k: n_out + v for k, v in rider.aliases.items()},
        compiler_params=_params(sem))(*args, *rider.arrays)
    return outs[:n_out], outs[n_out:]


def _comm(name, rider):
    def body(dummy_ref, out_ref):
        out_ref[...] = dummy_ref[...]

    dummy = jnp.zeros((SUBLANES, LANES), F32)
    spec = pl.BlockSpec((SUBLANES, LANES), lambda i: (0, 0))
    _, r_outs = _call(body, name, (1,), [spec], [spec], [jax.ShapeDtypeStruct(dummy.shape, F32)], [dummy], rider=rider)
    return r_outs


def _ada_fwd(c_all, w_ada_sh, b_ada_sh):
    nb, d = c_all.shape
    ncol = w_ada_sh.shape[1]

    def body(c_ref, w_ref, b_ref, mod_ref, cact_ref):
        cc = c_ref[...]
        ca = cc * jax.nn.sigmoid(cc)
        cact_ref[...] = ca
        mod_ref[...] = _dot(ca.astype(BF16), w_ref[...].astype(BF16), NN) + b_ref[...]

    return pl.pallas_call(
        body, name="ada_fwd",
        out_shape=[jax.ShapeDtypeStruct((nb, ncol), F32), jax.ShapeDtypeStruct((nb, d), F32)],
        compiler_params=_params(),
    )(c_all, w_ada_sh, b_ada_sh)


def _rms(xv):
    rstd = lax.rsqrt(jnp.mean(xv * xv, axis=-1, keepdims=True) + EPS)
    return xv * rstd, rstd


def _rms_bwd(dxhat, xhat, rstd):
    return rstd * (dxhat - xhat * jnp.mean(dxhat * xhat, axis=-1, keepdims=True))


def _colsum(v):
    return jnp.sum(v, axis=0, keepdims=True)


def _expm1(v):
    series = v * (1.0 + v * (0.5 + v * (1.0 / 6.0 + v * (1.0 / 24.0 + v * (1.0 / 120.0 + v * (1.0 / 720.0))))))
    return jnp.where(jnp.abs(v) < 0.3, series, jnp.exp(v) - 1.0)


def _softplus(v):
    return jnp.maximum(v, 0.0) + jnp.log1p(jnp.exp(-jnp.abs(v)))


def _gelu(v):
    t = jnp.tanh(GELU_K0 * (v + GELU_K1 * v * v * v))
    return 0.5 * v * (1.0 + t), t


def _dgelu(v, t):
    return 0.5 * (1.0 + t) + 0.5 * v * (1.0 - t * t) * GELU_K0 * (1.0 + 3.0 * GELU_K1 * v * v)


def _shift_down(v, k, prev8):
    r = pltpu.roll(v, k, 0)
    pr = pltpu.roll(prev8, k, 0)
    row8 = lax.broadcasted_iota(jnp.int32, prev8.shape, 0)
    top = jnp.where(row8 < k, pr, r[0:SUBLANES])
    return jnp.concatenate([top, r[SUBLANES:]], axis=0)


def _shift_up(v, k, next8):
    t = v.shape[0]
    r = pltpu.roll(v, t - k, 0)
    nr = pltpu.roll(next8, SUBLANES - k, 0)
    row8 = lax.broadcasted_iota(jnp.int32, next8.shape, 0)
    bot = jnp.where(row8 >= SUBLANES - k, nr, r[t - SUBLANES:t])
    return jnp.concatenate([r[:t - SUBLANES], bot], axis=0)


def _scan_fwd(a, b, h0):
    t = a.shape[0]
    row = lax.broadcasted_iota(jnp.int32, a.shape, 0)
    s = 1
    while s < min(t, SUBLANES):
        a_sh = pltpu.roll(a, s, 0)
        b_sh = pltpu.roll(b, s, 0)
        m = row >= s
        b = jnp.where(m, a * b_sh + b, b)
        a = jnp.where(m, a * a_sh, a)
        s *= 2
    while s < t:
        b = jnp.concatenate([b[:s], a[s:] * b[:t - s] + b[s:]], axis=0)
        a = jnp.concatenate([a[:s], a[s:] * a[:t - s]], axis=0)
        s *= 2
    return b + a * h0


def _scan_rev(m, b, g_next):
    t = m.shape[0]
    row = lax.broadcasted_iota(jnp.int32, m.shape, 0)
    s = 1
    while s < min(t, SUBLANES):
        m_sh = pltpu.roll(m, t - s, 0)
        b_sh = pltpu.roll(b, t - s, 0)
        msk = row < t - s
        b = jnp.where(msk, m * b_sh + b, b)
        m = jnp.where(msk, m * m_sh, m)
        s *= 2
    while s < t:
        b = jnp.concatenate([m[:t - s] * b[s:] + b[:t - s], b[t - s:]], axis=0)
        m = jnp.concatenate([m[:t - s] * m[s:], m[t - s:]], axis=0)
        s *= 2
    return b + m * g_next


def _lru_gates(u, wa, wx, ba, bx, sp):
    ub = u.astype(BF16)
    r = jax.nn.sigmoid(_dot(ub, wa, NN) + ba)
    i = jax.nn.sigmoid(_dot(ub, wx, NN) + bx)
    log_a = (-RG_C * r) * sp
    a = jnp.exp(log_a)
    mult = jnp.sqrt(-_expm1(2.0 * log_a))
    return ub, r, i, a, mult


def _conv3(p, pp, w_ref, lo):
    p1 = _shift_down(p, 1, pp)
    p2 = _shift_down(p, 2, pp)
    q = (w_ref[0:1, lo:lo + LANES] * p2 + w_ref[1:2, lo:lo + LANES] * p1) + w_ref[2:3, lo:lo + LANES] * p
    return q, p1, p2


def _conv4(xv, xp, w_ref, b_ref, lo):
    x1 = _shift_down(xv, 1, xp)
    x2 = _shift_down(xv, 2, xp)
    x3 = _shift_down(xv, 3, xp)
    u = (((w_ref[0:1, lo:lo + LANES] * x3 + w_ref[1:2, lo:lo + LANES] * x2) + w_ref[2:3, lo:lo + LANES] * x1)
         + w_ref[3:4, lo:lo + LANES] * xv) + b_ref[:, lo:lo + LANES]
    return u, x1, x2, x3


def _mix_in_fwd(x2d, mod6, g_mix, w_in_t, tm, rider=None):
    s, d = x2d.shape
    din = w_in_t.shape[0]

    def body(x_ref, mod_ref, g_ref, w_ref, hn_ref, proj_ref):
        xhat, _ = _rms(x_ref[...])
        hn = ((xhat * g_ref[...]) * (1.0 + mod_ref[1:2, :]) + mod_ref[0:1, :]).astype(BF16)
        hn_ref[...] = hn
        proj_ref[...] = _dot(hn, w_ref[...], NT)

    return _call(
        body, "mix_in_fwd", (s // tm,),
        [pl.BlockSpec((tm, d), lambda i: (i, 0)), _full(mod6.shape), _full(g_mix.shape), _full(w_in_t.shape)],
        [pl.BlockSpec((tm, d), lambda i: (i, 0)), pl.BlockSpec((tm, din), lambda i: (i, 0))],
        [jax.ShapeDtypeStruct((s, d), BF16), jax.ShapeDtypeStruct((s, din), F32)],
        [x2d, mod6, g_mix, w_in_t], rider=rider)


def _mixer_fwd(proj, conv_sc, conv_lru, conv_b, wa_bd, wx_bd, ba, bx, lam, width, rider=None):
    s, din = proj.shape
    t = min(MIX_ROWS, s)
    nblk = width // LANES
    hb = t // SUBLANES

    def body(proj_ref, projp_ref, wsc_ref, wlru_ref, blru_ref, wa_ref, wx_ref, ba_ref, bx_ref, lam_ref,
             ymix_ref, h_ref, hc_ref):
        i = pl.program_id(0)

        @pl.when(i == 0)
        def _():
            hc_ref[...] = jnp.zeros_like(hc_ref)

        has_prev = i > 0
        for j in range(nblk):
            lo = j * LANES

            def col(p, ref=proj_ref):
                return ref[:, p * width + lo:p * width + lo + LANES]

            def prev(p):
                return jnp.where(has_prev, col(p, projp_ref), 0.0)

            p = col(1) * col(2)
            q, _, _ = _conv3(p, prev(1) * prev(2), wsc_ref, lo)
            ymix_ref[:, lo:lo + LANES] = (col(0) * q).astype(BF16)

            u, _, _, _ = _conv4(col(4), prev(4), wlru_ref, blru_ref, lo)
            sp = _softplus(-lam_ref[:, lo:lo + LANES])
            _, r, ig, a, mult = _lru_gates(u, wa_ref[j], wx_ref[j], ba_ref[:, lo:lo + LANES], bx_ref[:, lo:lo + LANES], sp)
            h = _scan_fwd(a, mult * (ig * u), hc_ref[0:1, lo:lo + LANES])
            h_ref[:, lo:lo + LANES] = h
            hc_ref[0:1, lo:lo + LANES] = h[t - 1:t, :]
            gel, _ = _gelu(col(3))
            ymix_ref[:, width + lo:width + lo + LANES] = (gel * h).astype(BF16)

    small = [conv_sc, conv_lru, conv_b, wa_bd, wx_bd, ba, bx, lam]
    return _call(
        body, "mixer_fwd", (s // t,),
        [pl.BlockSpec((t, din), lambda i: (i, 0)),
         pl.BlockSpec((SUBLANES, din), lambda i: (jnp.maximum(i * hb - 1, 0), 0))]
        + [_full(a.shape) for a in small],
        [pl.BlockSpec((t, 2 * width), lambda i: (i, 0)), pl.BlockSpec((t, width), lambda i: (i, 0))],
        [jax.ShapeDtypeStruct((s, 2 * width), BF16), jax.ShapeDtypeStruct((s, width), F32)],
        [proj, proj, *small], scratch=[pltpu.VMEM((SUBLANES, width), F32)], rider=rider)


def _mix_out_fwd(ymix, x2d, w_out, mod6, g_mlp, tm, rider=None):
    s, d = x2d.shape

    def body(y_ref, x_ref, w_ref, mod_ref, g_ref, mix_ref, x2_ref, hn_ref):
        mix = _dot(y_ref[...], w_ref[...], NN)
        mix_ref[...] = mix
        x2 = x_ref[...] + mod_ref[2:3, :] * mix
        x2_ref[...] = x2
        xhat, _ = _rms(x2)
        hn_ref[...] = ((xhat * g_ref[...]) * (1.0 + mod_ref[4:5, :]) + mod_ref[3:4, :]).astype(BF16)

    tile = pl.BlockSpec((tm, d), lambda i: (i, 0))
    return _call(
        body, "mix_out_fwd", (s // tm,),
        [tile, tile, _full(w_out.shape), _full(mod6.shape), _full(g_mlp.shape)],
        [tile, tile, tile],
        [jax.ShapeDtypeStruct((s, d), F32), jax.ShapeDtypeStruct((s, d), F32), jax.ShapeDtypeStruct((s, d), BF16)],
        [ymix, x2d, w_out, mod6, g_mlp], rider=rider)


def _mlp_fwd_loss(hn2, w_up_t, w_down, x2, target, mod6, g_final, tm, tk):
    s, d = hn2.shape
    f = w_up_t.shape[0]
    nk = f // tk

    def body(hn_ref, wu_ref, wd_ref, x2_ref, t_ref, mod_ref, g_ref, z_ref, dx3_ref, dyb_ref, st_ref, y_ref):
        i, k = pl.program_id(0), pl.program_id(1)

        @pl.when(jnp.logical_and(i == 0, k == 0))
        def _():
            st_ref[...] = jnp.zeros_like(st_ref)

        z = jnp.maximum(_dot(hn_ref[...], wu_ref[...], NT), 0.0)
        z_ref[...] = z.astype(BF16)
        part = _dot((z * z).astype(BF16), wd_ref[...], NN)

        @pl.when(k == 0)
        def _():
            y_ref[...] = part

        @pl.when(k > 0)
        def _():
            y_ref[...] += part

        @pl.when(k == nk - 1)
        def _():
            gate = mod_ref[5:6, :]
            yv = y_ref[...]
            xhat, rstd = _rms(x2_ref[...] + gate * yv)
            diff = xhat * g_ref[...] - t_ref[...]
            dyo = diff * (1.0 / d)
            dx3 = _rms_bwd(dyo * g_ref[...], xhat, rstd)
            dx3_ref[...] = dx3
            dyb_ref[...] = (gate * dx3).astype(BF16)
            st_ref[0:1, :] += _colsum(dyo * xhat)
            st_ref[1:2, :] += _colsum(dx3 * yv)
            st_ref[2:3, :] += _colsum(diff * diff)

    tile = pl.BlockSpec((tm, d), lambda i, k: (i, 0))
    wblk = pl.BlockSpec((tk, d), lambda i, k: (k, 0))
    return pl.pallas_call(
        body, name="mlp_fwd_loss", grid=(s // tm, nk),
        in_specs=[tile, wblk, wblk, tile, tile, _full(mod6.shape), _full(g_final.shape)],
        out_specs=[pl.BlockSpec((tm, tk), lambda i, k: (i, k)), tile, tile, _full((SUBLANES, d))],
        out_shape=[jax.ShapeDtypeStruct((s, f), BF16), jax.ShapeDtypeStruct((s, d), F32),
                   jax.ShapeDtypeStruct((s, d), BF16), jax.ShapeDtypeStruct((SUBLANES, d), F32)],
        scratch_shapes=[pltpu.VMEM((tm, d), F32)],
        compiler_params=_params(("arbitrary", "arbitrary")),
    )(hn2, w_up_t, w_down, x2, target, mod6, g_final)


def _mlp_bwd_dx(dyb, z, w_down, w_up_t, tm, tk):
    s, d = dyb.shape
    f = z.shape[1]

    def body(dy_ref, z_ref, wd_ref, wu_ref, dz_ref, dh_ref):
        k = pl.program_id(1)
        dz = ((2.0 * z_ref[...].astype(F32)) * _dot(dy_ref[...], wd_ref[...], NT)).astype(BF16)
        dz_ref[...] = dz
        part = _dot(dz, wu_ref[...], NN)

        @pl.when(k == 0)
        def _():
            dh_ref[...] = part

        @pl.when(k > 0)
        def _():
            dh_ref[...] += part

    return pl.pallas_call(
        body, name="mlp_bwd_dx", grid=(s // tm, f // tk),
        in_specs=[pl.BlockSpec((tm, d), lambda i, k: (i, 0)), pl.BlockSpec((tm, tk), lambda i, k: (i, k)),
                  pl.BlockSpec((tk, d), lambda i, k: (k, 0)), pl.BlockSpec((tk, d), lambda i, k: (k, 0))],
        out_specs=[pl.BlockSpec((tm, tk), lambda i, k: (i, k)), pl.BlockSpec((tm, d), lambda i, k: (i, 0))],
        out_shape=[jax.ShapeDtypeStruct((s, f), BF16), jax.ShapeDtypeStruct((s, d), F32)],
        compiler_params=_params(("parallel", "arbitrary")),
    )(dyb, z, w_down, w_up_t)


def _mlp_bwd_dw(z, dz, dyb, hn2, tm, tk):
    s, d = dyb.shape
    f = z.shape[1]

    def body(z_ref, dz_ref, dy_ref, hn_ref, gd_ref, gu_ref):
        i = pl.program_id(1)

        @pl.when(i == 0)
        def _():
            gd_ref[...] = jnp.zeros_like(gd_ref)
            gu_ref[...] = jnp.zeros_like(gu_ref)

        zf = z_ref[...].astype(F32)
        gd_ref[...] += _dot((zf * zf).astype(BF16), dy_ref[...], TN)
        gu_ref[...] += _dot(dz_ref[...], hn_ref[...], TN)

    return pl.pallas_call(
        body, name="mlp_bwd_dw", grid=(f // tk, s // tm),
        in_specs=[pl.BlockSpec((tm, tk), lambda k, i: (i, k)), pl.BlockSpec((tm, tk), lambda k, i: (i, k)),
                  pl.BlockSpec((tm, d), lambda k, i: (i, 0)), pl.BlockSpec((tm, d), lambda k, i: (i, 0))],
        out_specs=[pl.BlockSpec((tk, d), lambda k, i: (k, 0)), pl.BlockSpec((tk, d), lambda k, i: (k, 0))],
        out_shape=[jax.ShapeDtypeStruct((f, d), F32), jax.ShapeDtypeStruct((f, d), F32)],
        compiler_params=_params(("parallel", "arbitrary")),
    )(z, dz, dyb, hn2)


def _mix_out_bwd(dhn2, x2, dx3, mix, ymix, w_out, mod6, g_mlp, tm, rider=None):
    s, d = x2.shape

    def body(dh_ref, x2_ref, dx3_ref, mix_ref, y_ref, w_ref, mod_ref, g_ref, dx2_ref, dym_ref, gw_ref, st_ref):
        i = pl.program_id(0)

        @pl.when(i == 0)
        def _():
            st_ref[...] = jnp.zeros_like(st_ref)
            gw_ref[...] = jnp.zeros_like(gw_ref)

        dh = dh_ref[...]
        xhat, rstd = _rms(x2_ref[...])
        dn = dh * (1.0 + mod_ref[4:5, :])
        dx2 = dx3_ref[...] + _rms_bwd(dn * g_ref[...], xhat, rstd)
        dx2_ref[...] = dx2
        st_ref[0:1, :] += _colsum(dh)
        st_ref[1:2, :] += _colsum(dh * (xhat * g_ref[...]))
        st_ref[2:3, :] += _colsum(dn * xhat)
        st_ref[3:4, :] += _colsum(dx2 * mix_ref[...])
        dmix = (mod_ref[2:3, :] * dx2).astype(BF16)
        dym_ref[...] = _dot(dmix, w_ref[...], NT)
        gw_ref[...] += _dot(y_ref[...], dmix, TN)

    tile = pl.BlockSpec((tm, d), lambda i: (i, 0))
    return _call(
        body, "mix_out_bwd", (s // tm,),
        [tile, tile, tile, tile, tile, _full(w_out.shape), _full(mod6.shape), _full(g_mlp.shape)],
        [tile, tile, _full((d, d)), _full((SUBLANES, d))],
        [jax.ShapeDtypeStruct((s, d), F32), jax.ShapeDtypeStruct((s, d), F32),
         jax.ShapeDtypeStruct((d, d), F32), jax.ShapeDtypeStruct((SUBLANES, d), F32)],
        [dhn2, x2, dx3, mix, ymix, w_out, mod6, g_mlp], rider=rider)


def _mixer_bwd(proj, dymix, h_all, conv_sc, conv_lru, conv_b, wa_bd, wx_bd, ba, bx, lam, width, rider=None):
    s, din = proj.shape
    t = min(MIX_ROWS, s)
    nt = s // t
    nblk = width // LANES
    hb = t // SUBLANES
    last8 = s // SUBLANES - 1

    def body(proj_ref, projp_ref, projn_ref, dy_ref, dyn_ref, h_ref, hp_ref,
             wsc_ref, wlru_ref, blru_ref, wa_ref, wx_ref, ba_ref, bx_ref, lam_ref,
             dproj_ref, small_ref, gwa_ref, gwx_ref, an_ref, gn_ref, dun_ref):
        i = pl.program_id(0)

        @pl.when(i == 0)
        def _():
            small_ref[...] = jnp.zeros_like(small_ref)
            gwa_ref[...] = jnp.zeros_like(gwa_ref)
            gwx_ref[...] = jnp.zeros_like(gwx_ref)
            an_ref[...] = jnp.zeros_like(an_ref)
            gn_ref[...] = jnp.zeros_like(gn_ref)
            dun_ref[...] = jnp.zeros_like(dun_ref)

        has_prev = i < nt - 1
        has_next = i > 0
        for j in range(nblk):
            lo = j * LANES
            ls = slice(lo, lo + LANES)

            def col(p, ref=proj_ref):
                return ref[:, p * width + lo:p * width + lo + LANES]

            def prev(p):
                return jnp.where(has_prev, col(p, projp_ref), 0.0)

            def nxt(p):
                return jnp.where(has_next, col(p, projn_ref), 0.0)

            def add_row(r, v):
                small_ref[r:r + 1, ls] += _colsum(v)

            sc_b, sc_c, sc_x = col(0), col(1), col(2)
            p = sc_c * sc_x
            q, p1, p2 = _conv3(p, prev(1) * prev(2), wsc_ref, lo)
            dys = dy_ref[:, ls]
            dproj_ref[:, ls] = (dys * q).astype(BF16)
            dq = dys * sc_b
            dqn = jnp.where(has_next, dyn_ref[:, ls], 0.0) * nxt(0)
            dp = (wsc_ref[2:3, ls] * dq + wsc_ref[1:2, ls] * _shift_up(dq, 1, dqn)) + wsc_ref[0:1, ls] * _shift_up(dq, 2, dqn)
            dproj_ref[:, width + lo:width + lo + LANES] = (dp * sc_x).astype(BF16)
            dproj_ref[:, 2 * width + lo:2 * width + lo + LANES] = (dp * sc_c).astype(BF16)
            add_row(0, dq * p2)
            add_row(1, dq * p1)
            add_row(2, dq * p)

            xv = col(4)
            u, x1, x2, x3 = _conv4(xv, prev(4), wlru_ref, blru_ref, lo)
            lam_v = lam_ref[:, ls]
            sp = _softplus(-lam_v)
            wa, wx = wa_ref[j], wx_ref[j]
            ub, r, ig, a, mult = _lru_gates(u, wa, wx, ba_ref[:, ls], bx_ref[:, ls], sp)
            iu = ig * u
            h = h_ref[:, ls]
            hm1 = _shift_down(h, 1, jnp.where(has_prev, hp_ref[:, ls], 0.0))
            lyv = col(3)
            gel, th = _gelu(lyv)
            dyl = dy_ref[:, width + lo:width + lo + LANES]
            dproj_ref[:, 3 * width + lo:3 * width + lo + LANES] = (dyl * h * _dgelu(lyv, th)).astype(BF16)
            a_next = jnp.broadcast_to(an_ref[0:1, ls], (SUBLANES, LANES))
            g = _scan_rev(_shift_up(a, 1, a_next), dyl * gel, gn_ref[0:1, ls])
            an_ref[0:1, ls] = a[0:1, :]
            gn_ref[0:1, ls] = g[0:1, :]
            da = g * hm1
            dmult = g * iu
            diu = g * mult
            dlog_a = da * a - dmult * ((a * a) / mult)
            dpre_a = (dlog_a * (-RG_C * sp)) * (r * (1.0 - r))
            dpre_x = (diu * u) * (ig * (1.0 - ig))
            dab, dxb = dpre_a.astype(BF16), dpre_x.astype(BF16)
            du = diu * ig + _dot(dab, wa, NT) + _dot(dxb, wx, NT)
            gwa_ref[j] += _dot(ub, dab, TN)
            gwx_ref[j] += _dot(ub, dxb, TN)
            dun = dun_ref[:, ls]
            dun_ref[:, ls] = du[0:SUBLANES, :]
            dlx = (((wlru_ref[3:4, ls] * du + wlru_ref[2:3, ls] * _shift_up(du, 1, dun))
                    + wlru_ref[1:2, ls] * _shift_up(du, 2, dun)) + wlru_ref[0:1, ls] * _shift_up(du, 3, dun))
            dproj_ref[:, 4 * width + lo:4 * width + lo + LANES] = dlx.astype(BF16)
            add_row(3, du * x3)
            add_row(4, du * x2)
            add_row(5, du * x1)
            add_row(6, du * xv)
            add_row(7, du)
            add_row(8, dpre_a)
            add_row(9, dpre_x)
            add_row(10, (dlog_a * (RG_C * r)) * jax.nn.sigmoid(-lam_v))

    small = [conv_sc, conv_lru, conv_b, wa_bd, wx_bd, ba, bx, lam]
    rev = lambda i: nt - 1 - i
    return _call(
        body, "mixer_bwd", (nt,),
        [pl.BlockSpec((t, din), lambda i: (rev(i), 0)),
         pl.BlockSpec((SUBLANES, din), lambda i: (jnp.maximum(rev(i) * hb - 1, 0), 0)),
         pl.BlockSpec((SUBLANES, din), lambda i: (jnp.minimum((rev(i) + 1) * hb, last8), 0)),
         pl.BlockSpec((t, 2 * width), lambda i: (rev(i), 0)),
         pl.BlockSpec((SUBLANES, 2 * width), lambda i: (jnp.minimum((rev(i) + 1) * hb, last8), 0)),
         pl.BlockSpec((t, width), lambda i: (rev(i), 0)),
         pl.BlockSpec((SUBLANES, width), lambda i: (jnp.maximum(rev(i) * hb - 1, 0), 0))]
        + [_full(a.shape) for a in small],
        [pl.BlockSpec((t, din), lambda i: (rev(i), 0)), _full((2 * SUBLANES, width)),
         _full(wa_bd.shape), _full(wx_bd.shape)],
        [jax.ShapeDtypeStruct((s, din), BF16), jax.ShapeDtypeStruct((2 * SUBLANES, width), F32),
         jax.ShapeDtypeStruct(wa_bd.shape, F32), jax.ShapeDtypeStruct(wx_bd.shape, F32)],
        [proj, proj, proj, dymix, dymix, h_all, h_all, *small],
        scratch=[pltpu.VMEM((SUBLANES, width), F32), pltpu.VMEM((SUBLANES, width), F32),
                 pltpu.VMEM((SUBLANES, width), F32)], rider=rider)


def _mix_in_bwd_dx(dproj, x2d, dx2, w_in_t, mod6, g_mix, tm, rider=None):
    s, d = x2d.shape
    din = dproj.shape[1]

    def body(dp_ref, x_ref, dx2_ref, w_ref, mod_ref, g_ref, gx_ref, st_ref):
        i = pl.program_id(0)

        @pl.when(i == 0)
        def _():
            st_ref[...] = jnp.zeros_like(st_ref)

        dh = _dot(dp_ref[...], w_ref[...], NN)
        xhat, rstd = _rms(x_ref[...])
        dn = dh * (1.0 + mod_ref[1:2, :])
        gx_ref[...] = dx2_ref[...] + _rms_bwd(dn * g_ref[...], xhat, rstd)
        st_ref[0:1, :] += _colsum(dh)
        st_ref[1:2, :] += _colsum(dh * (xhat * g_ref[...]))
        st_ref[2:3, :] += _colsum(dn * xhat)

    tile = pl.BlockSpec((tm, d), lambda i: (i, 0))
    return _call(
        body, "mix_in_bwd_dx", (s // tm,),
        [pl.BlockSpec((tm, din), lambda i: (i, 0)), tile, tile, _full(w_in_t.shape), _full(mod6.shape),
         _full(g_mix.shape)],
        [tile, _full((SUBLANES, d))],
        [jax.ShapeDtypeStruct((s, d), F32), jax.ShapeDtypeStruct((SUBLANES, d), F32)],
        [dproj, x2d, dx2, w_in_t, mod6, g_mix], rider=rider)


def _mix_in_bwd_dw(dproj, hn1, tm, tn, rider=None):
    s, d = hn1.shape
    din = dproj.shape[1]

    def body(dp_ref, hn_ref, gw_ref):
        i = pl.program_id(1)

        @pl.when(i == 0)
        def _():
            gw_ref[...] = jnp.zeros_like(gw_ref)

        gw_ref[...] += _dot(dp_ref[...], hn_ref[...], TN)

    return _call(
        body, "mix_in_bwd_dw", (din // tn, s // tm),
        [pl.BlockSpec((tm, tn), lambda p, i: (i, p)), pl.BlockSpec((tm, d), lambda p, i: (i, 0))],
        [pl.BlockSpec((tn, d), lambda p, i: (p, 0))],
        [jax.ShapeDtypeStruct((din, d), F32)],
        [dproj, hn1], rider=rider)


def _adamw(w, g, m, v):
    m = ADAM_B1 * m + (1.0 - ADAM_B1) * g
    v = ADAM_B2 * v + (1.0 - ADAM_B2) * (g * g)
    m_hat = m / (1.0 - ADAM_B1 ** ADAM_STEP)
    v_hat = v / (1.0 - ADAM_B2 ** ADAM_STEP)
    delta = -ADAM_LR * (m_hat / (jnp.sqrt(v_hat) + ADAM_EPS) + ADAM_WD * w)
    return delta, m, v


def _pair_sum(g4, h4, core_chip, tr, name):
    _, _, r, n = g4.shape

    def body(sc_ref, g_ref, h_ref, sb_ref, own_ref):
        q = pl.program_id(1)
        ssum = g_ref[...] + h_ref[...]
        sb_ref[...] = ssum.astype(BF16)

        @pl.when(q == sc_ref[1])
        def _():
            own_ref[...] = ssum

    grid_spec = pltpu.PrefetchScalarGridSpec(
        num_scalar_prefetch=1, grid=(r // tr, 4),
        in_specs=[pl.BlockSpec((None, None, tr, n), lambda i, q, sc: (q, sc[0], i, 0)),
                  pl.BlockSpec((None, tr, n), lambda i, q, sc: (q, i, 0))],
        out_specs=[pl.BlockSpec((None, tr, n), lambda i, q, sc: (q, i, 0)),
                   pl.BlockSpec((tr, n), lambda i, q, sc: (i, 0))])
    return pl.pallas_call(
        body, name=name, grid_spec=grid_spec,
        out_shape=[jax.ShapeDtypeStruct((4, r, n), BF16), jax.ShapeDtypeStruct((r, n), F32)],
        compiler_params=_params(("parallel", "arbitrary")),
    )(core_chip, g4, h4)


def _sum4(own, parts, tr, name):
    r, n = own.shape

    def body(o_ref, p_ref, out_ref):
        acc = o_ref[...]
        for k in range(3):
            acc = acc + p_ref[k].astype(F32)
        out_ref[...] = acc

    return pl.pallas_call(
        body, name=name, grid=(r // tr,),
        in_specs=[pl.BlockSpec((tr, n), lambda i: (i, 0)), pl.BlockSpec((3, tr, n), lambda i: (0, i, 0))],
        out_specs=pl.BlockSpec((tr, n), lambda i: (i, 0)),
        out_shape=jax.ShapeDtypeStruct((r, n), F32),
        compiler_params=_params(("parallel",)),
    )(own, parts)


def _sum8(parts, tr, name):
    _, rows, n = parts.shape

    def body(p_ref, o_ref):
        acc = p_ref[0]
        for k in range(1, N_DEV):
            acc = acc + p_ref[k]
        o_ref[...] = acc

    return pl.pallas_call(
        body, name=name, grid=(rows // tr,),
        in_specs=[pl.BlockSpec((N_DEV, tr, n), lambda i: (0, i, 0))],
        out_specs=pl.BlockSpec((tr, n), lambda i: (i, 0)),
        out_shape=jax.ShapeDtypeStruct((rows, n), F32),
        compiler_params=_params(("parallel",)),
    )(parts)


def _adam_rows(w, g, m, v, tr, name):
    rows, n = w.shape

    def body(w_ref, g_ref, m_ref, v_ref, d_ref, nm_ref, nv_ref):
        d_ref[...], nm_ref[...], nv_ref[...] = _adamw(w_ref[...], g_ref[...], m_ref[...], v_ref[...])

    tile = pl.BlockSpec((tr, n), lambda i: (i, 0))
    return pl.pallas_call(
        body, name=name, grid=(rows // tr,),
        in_specs=[tile] * 4, out_specs=[tile] * 3,
        out_shape=[jax.ShapeDtypeStruct((rows, n), F32)] * 3,
        compiler_params=_params(("parallel",)),
    )(w, g, m, v)


def _ada_bwd_adam(cact_t, dmod_cols, w, m, v, tr):
    rows, n = w.shape

    def body(c_ref, d_ref, w_ref, m_ref, v_ref, g_ref, dl_ref, nm_ref, nv_ref):
        def term(b):
            return c_ref[b].astype(BF16).astype(F32) * d_ref[b:b + 1, :].astype(BF16).astype(F32)

        g = term(0)
        for b in range(1, N_DEV):
            g = g + term(b)
        g_ref[...] = g
        dl_ref[...], nm_ref[...], nv_ref[...] = _adamw(w_ref[...], g, m_ref[...], v_ref[...])

    tile = pl.BlockSpec((tr, n), lambda i: (i, 0))
    return pl.pallas_call(
        body, name="ada_bwd_adam", grid=(rows // tr,),
        in_specs=[pl.BlockSpec((N_DEV, tr, 1), lambda i: (0, i, 0)), _full(dmod_cols.shape), tile, tile, tile],
        out_specs=[tile] * 4,
        out_shape=[jax.ShapeDtypeStruct((rows, n), F32)] * 4,
        compiler_params=_params(("parallel",)),
    )(cact_t, dmod_cols, w, m, v)


def _adam_small(ws, gs, ms, vs):
    n = len(ws)

    def body(*refs):
        w_r, g_r, m_r, v_r = refs[:n], refs[n:2 * n], refs[2 * n:3 * n], refs[3 * n:4 * n]
        d_r, nm_r, nv_r = refs[4 * n:5 * n], refs[5 * n:6 * n], refs[6 * n:7 * n]
        for k in range(n):
            d_r[k][...], nm_r[k][...], nv_r[k][...] = _adamw(w_r[k][...], g_r[k][...], m_r[k][...], v_r[k][...])

    shapes = [jax.ShapeDtypeStruct(w.shape, F32) for w in ws]
    outs = pl.pallas_call(
        body, name="adam_small", out_shape=shapes * 3, compiler_params=_params(),
    )(*ws, *gs, *ms, *vs)
    return outs[:n], outs[n:2 * n], outs[2 * n:]


def _block_diag(w):
    h, hd, _ = w.shape
    per = LANES // hd
    eye = jnp.eye(per, dtype=w.dtype)
    w5 = w.reshape(h // per, per, hd, 1, hd) * eye[None, :, None, :, None]
    return w5.reshape(h // per, LANES, LANES)


def _block_diag_grad(g, h, hd):
    per = LANES // hd
    g5 = g.reshape(h // per, per, hd, per, hd)
    return jnp.stack([g5[:, a, :, a, :] for a in range(per)], axis=1).reshape(h, hd, hd)


def kernel(x, c, w_ada, b_ada, g_mix, w_in, conv_w_sc, conv_w_lru, conv_b_lru, w_rg_a, b_rg_a, w_rg_x, b_rg_x, lru_lambda, w_out, g_mlp, w_up, w_down, g_final, loss_target, m_w_ada, m_b_ada, m_g_mix, m_w_in, m_conv_w_sc, m_conv_w_lru, m_conv_b_lru, m_w_rg_a, m_b_rg_a, m_w_rg_x, m_b_rg_x, m_lru_lambda, m_w_out, m_g_mlp, m_w_up, m_w_down, m_g_final, v_w_ada, v_b_ada, v_g_mix, v_w_in, v_conv_w_sc, v_conv_w_lru, v_conv_b_lru, v_w_rg_a, v_b_rg_a, v_w_rg_x, v_b_rg_x, v_lru_lambda, v_w_out, v_g_mlp, v_w_up, v_w_down, v_g_final):
    s, d = x.shape[1], x.shape[2]
    width = conv_b_lru.shape[1]
    heads, hd = w_rg_a.shape[1], w_rg_a.shape[2]
    f = w_down.shape[1] * N_DEV
    n_ada = w_ada.shape[2]
    csh = conv_w_sc.shape[2]
    me = 4 * lax.axis_index("x") + 2 * lax.axis_index("y") + lax.axis_index("c")
    tm = min(512, s)
    tm_mlp = min(1024, s)
    tk = 512

    x2d = x[0]
    tgt = loss_target[0]

    pay = jnp.zeros((SUBLANES, d), F32)
    pay = pay.at[0:1, :].set(c)
    pay = pay.at[1:4, 0:csh].set(conv_w_sc[0])
    pay = pay.at[4:8, 0:csh].set(conv_w_lru[0])
    w_in_t_sh = w_in[0].T.astype(BF16)
    w_up_t_sh = w_up[0].T.astype(BF16)
    w_out_sh = w_out[0].astype(BF16)
    w_down_sh = w_down[0].astype(BF16)
    pay_all, w_in_t = _gather2("gather_in", [pay, w_in_t_sh])
    w_up_g, w_down_g = _seq_gather2("gather_mlp_weights", 1, [w_up_t_sh, w_down_sh])
    w_in_t = w_in_t.reshape(-1, d)
    c_all = pay_all[:, 0, :]
    conv_sc = pay_all[:, 1:4, 0:csh].transpose(1, 0, 2).reshape(3, width)
    conv_lru = pay_all[:, 4:8, 0:csh].transpose(1, 0, 2).reshape(4, width)

    b_ada_sh = lax.dynamic_slice(b_ada, (0, me * n_ada), (1, n_ada))
    mod_cols, c_act = _ada_fwd(c_all, w_ada[0], b_ada_sh)
    (mod_rows,) = _exchange("scatter_mod", [], [mod_cols.reshape(N_DEV, 1, n_ada)])
    mod6 = jnp.zeros((SUBLANES, d), F32).at[0:6, :].set(mod_rows.reshape(6, d))

    wa_bd = _block_diag(w_rg_a[0]).astype(BF16)
    wx_bd = _block_diag(w_rg_x[0]).astype(BF16)
    ba = b_rg_a.reshape(1, width)
    bx = b_rg_x.reshape(1, width)
    g_fin = g_final.reshape(1, d)

    (hn1, proj), (w_out_g,) = _mix_in_fwd(
        x2d, mod6, g_mix, w_in_t, tm, rider=_ride_gather_ici([w_out_sh]))
    (ymix, h_all), (w_out_g,) = _mixer_fwd(
        proj, conv_sc, conv_lru, conv_b_lru, wa_bd, wx_bd, ba, bx, lru_lambda, width,
        rider=_ride_gather_d2d([w_out_g]))
    w_out_b = w_out_g.reshape(-1, d)
    (mix, x2, hn2), _ = _mix_out_fwd(ymix, x2d, w_out_b, mod6, g_mlp, tm)
    w_up_t = w_up_g.reshape(-1, d)
    w_down_b = w_down_g.reshape(-1, d)
    z, dx3, dyb, st_fin = _mlp_fwd_loss(hn2, w_up_t, w_down_b, x2, tgt, mod6, g_fin, tm, 4 * tk)

    core_chip = jnp.stack([lax.axis_index("c"), 2 * lax.axis_index("x") + lax.axis_index("y")]).astype(jnp.int32)
    dz, dhn2 = _mlp_bwd_dx(dyb, z, w_down_b, w_up_t, tm, 4 * tk)
    g_down, g_up_t = _mlp_bwd_dw(z, dz, dyb, hn2, tm_mlp, 2 * tk)
    g_up4, g_down4 = g_up_t.reshape(4, 2, -1, d), g_down.reshape(4, 2, -1, d)
    (dx2, dymix, g_out, st_out), (h_up, h_down) = _mix_out_bwd(
        dhn2, x2, dx3, mix, ymix, w_out_b, mod6, g_mlp, tm, rider=_ride_pair_swap([g_up4, g_down4]))
    sb_up, own_up = _pair_sum(g_up4, h_up, core_chip, 256, "pair_sum_w_up")
    sb_down, own_down = _pair_sum(g_down4, h_down, core_chip, 256, "pair_sum_w_down")
    g_out4 = g_out.reshape(4, 2, -1, d)
    (dproj, g_small, g_wa, g_wx), (p_up, p_down, h_out) = _mixer_bwd(
        proj, dymix, h_all, conv_sc, conv_lru, conv_b_lru, wa_bd, wx_bd, ba, bx, lru_lambda, width,
        rider=_merge_riders(_ride_chip_exchange([sb_up, sb_down]), _ride_pair_swap([g_out4])))
    sb_out, own_out = _pair_sum(g_out4, h_out, core_chip, g_out4.shape[2], "pair_sum_w_out")
    (grad_x, st_in), _ = _mix_in_bwd_dx(dproj, x2d, dx2, w_in_t, mod6, g_mix, tm)

    small = jnp.concatenate([
        st_in[0:2], st_out[3:4], st_out[0:2], st_fin[1:2],
        st_in[2:3], st_out[2:3], st_fin[0:1],
        jnp.concatenate([g_small[7:8], g_small[10:11]], axis=1),
        jnp.concatenate([g_small[8:9], g_small[9:10]], axis=1),
        jnp.concatenate([jnp.concatenate([g_small[0:3], jnp.zeros((1, width), F32)], axis=0), g_small[3:7]], axis=1),
        st_fin[2:3],
        _block_diag_grad(g_wa, heads, hd).reshape(-1, d),
        _block_diag_grad(g_wx, heads, hd).reshape(-1, d),
    ], axis=0)

    (g_in_t,), (p_out, small_all) = _mix_in_bwd_dw(
        dproj, hn1, tm_mlp, 512, rider=_merge_riders(_ride_chip_exchange([sb_out]), _ride_gather_direct([small])))
    g_in4 = g_in_t.reshape(4, 2, -1, d)
    (h_in,) = _comm("swap_w_in", _ride_pair_swap([g_in4]))
    sb_in, own_in = _pair_sum(g_in4, h_in, core_chip, g_in4.shape[2], "pair_sum_w_in")
    (p_in,) = _comm("exchange_w_in", _ride_chip_exchange([sb_in]))

    gs_in = _sum4(own_in, p_in, own_in.shape[0], "sum_w_in").T
    gs_up = _sum4(own_up, p_up, 256, "sum_w_up").T
    gs_out = _sum4(own_out, p_out, own_out.shape[0], "sum_w_out")
    gs_down = _sum4(own_down, p_down, 256, "sum_w_down")
    ad_in = _adam_rows(w_in[0], gs_in, m_w_in[0], v_w_in[0], 256, "adam_w_in")
    ad_up = _adam_rows(w_up[0], gs_up, m_w_up[0], v_w_up[0], 256, "adam_w_up")
    ad_out = _adam_rows(w_out[0], gs_out, m_w_out[0], v_w_out[0], w_out.shape[1], "adam_w_out")
    ad_down = _adam_rows(w_down[0], gs_down, m_w_down[0], v_w_down[0], 256, "adam_w_down")

    gsum = _sum8(small_all, SMALL_ROWS, "sum_small")
    loss = (0.5 / d) * jnp.sum(gsum[15])
    dmod_cols = lax.dynamic_slice(small_all[:, 0:6, :].reshape(N_DEV, 6 * d), (0, me * n_ada), (N_DEV, n_ada))
    g_ada, d_ada, nm_ada, nv_ada = _ada_bwd_adam(c_act[:, :, None], dmod_cols, w_ada[0], m_w_ada[0], v_w_ada[0], 256)

    g_conv = lax.dynamic_slice(gsum[11:15, 0:width], (0, me * csh), (4, csh))
    g_conv_l = lax.dynamic_slice(gsum[11:15, width:2 * width], (0, me * csh), (4, csh))
    small_g = [
        gsum[0:6].reshape(1, 6 * d),
        gsum[6:7],
        g_conv[0:3].reshape(1, 3, csh),
        g_conv_l.reshape(1, 4, csh),
        gsum[9:10, 0:width],
        gsum[16:48].reshape(1, heads, hd, hd),
        gsum[10:11, 0:width].reshape(1, heads, hd),
        gsum[48:80].reshape(1, heads, hd, hd),
        gsum[10:11, width:].reshape(1, heads, hd),
        gsum[9:10, width:],
        gsum[7:8],
        gsum[8],
    ]
    small_w = [b_ada, g_mix, conv_w_sc, conv_w_lru, conv_b_lru, w_rg_a, b_rg_a, w_rg_x, b_rg_x, lru_lambda, g_mlp, g_final]
    small_m = [m_b_ada, m_g_mix, m_conv_w_sc, m_conv_w_lru, m_conv_b_lru, m_w_rg_a, m_b_rg_a, m_w_rg_x, m_b_rg_x,
               m_lru_lambda, m_g_mlp, m_g_final]
    small_v = [v_b_ada, v_g_mix, v_conv_w_sc, v_conv_w_lru, v_conv_b_lru, v_w_rg_a, v_b_rg_a, v_w_rg_x, v_b_rg_x,
               v_lru_lambda, v_g_mlp, v_g_final]
    sd, snm, snv = _adam_small(small_w, small_g, small_m, small_v)

    def order(ada, w_in_, w_out_, w_up_, w_down_, sm):
        return [ada[None], sm[0], sm[1], w_in_[None], sm[2], sm[3], sm[4], sm[5], sm[6], sm[7], sm[8], sm[9],
                w_out_[None], sm[10], w_up_[None], w_down_[None], sm[11]]

    grads = order(g_ada, gs_in, gs_out, gs_up, gs_down, small_g)
    deltas = order(d_ada, ad_in[0], ad_out[0], ad_up[0], ad_down[0], sd)
    new_m = order(nm_ada, ad_in[1], ad_out[1], ad_up[1], ad_down[1], snm)
    new_v = order(nv_ada, ad_in[2], ad_out[2], ad_up[2], ad_down[2], snv)
    return (loss, grad_x[None], *grads, *deltas, *new_m, *new_v)
```

```python
import functools

import jax
import jax.numpy as jnp
from jax import lax
from jax.experimental import pallas as pl
from jax.experimental.pallas import tpu as pltpu
from jax.experimental.pallas import tpu_sc as plsc

F32 = jnp.float32
BF16 = jnp.bfloat16
N_DEV = 8
EPS = 1e-6
RG_C = 8.0
GELU_K0 = 0.7978845608028654
GELU_K1 = 0.044715
ADAM_LR = 0.001
ADAM_B1 = 0.9
ADAM_B2 = 0.999
ADAM_EPS = 1e-08
ADAM_WD = 0.01
ADAM_STEP = 10
LANES = 128
SUBLANES = 8
VMEM_LIMIT = 52 * 1024 * 1024
MIX_ROWS = 256
SMALL_ROWS = 80

MESH = pl.DeviceIdType.MESH
ANY = pl.BlockSpec(memory_space=pl.ANY)
NN = ((1,), (0,))
NT = ((1,), (1,))
TN = ((0,), (0,))


def _dot(a, b, dims):
    return lax.dot_general(a, b, (dims, ((), ())), preferred_element_type=F32)


def _params(sem=None):
    return pltpu.CompilerParams(dimension_semantics=sem, vmem_limit_bytes=VMEM_LIMIT)


def _full(shape):
    nd = len(shape)
    return pl.BlockSpec(shape, lambda *_: (0,) * nd)


def _exchange(name, gathers, scatters):
    n_g = len(gathers)
    arrs = list(gathers) + list(scatters)
    n = len(arrs)
    out_shape = [jax.ShapeDtypeStruct((N_DEV,) + a.shape, a.dtype) for a in gathers]
    out_shape += [jax.ShapeDtypeStruct(a.shape, a.dtype) for a in scatters]

    def body(*refs):
        ins, outs = refs[:n], refs[n:2 * n]
        send_sems, recv_sems, local_sems = refs[2 * n:]
        x, y, c = lax.axis_index("x"), lax.axis_index("y"), lax.axis_index("c")
        me = 4 * x + 2 * y + c

        def src(a, dev):
            return ins[a] if a < n_g else ins[a].at[dev]

        def peer_of(k):
            px = 1 - x if (k >> 2) & 1 else x
            py = 1 - y if (k >> 1) & 1 else y
            pc = 1 - c if k & 1 else c
            return (px, py, pc), 4 * px + 2 * py + pc

        local = [pltpu.make_async_copy(src(a, me), outs[a].at[me], local_sems.at[a]) for a in range(n)]
        for cp in local:
            cp.start()
        sends = []
        for k in range(1, N_DEV):
            peer, pidx = peer_of(k)
            for a in range(n):
                cp = pltpu.make_async_remote_copy(
                    src_ref=src(a, pidx), dst_ref=outs[a].at[me],
                    send_sem=send_sems.at[a * (N_DEV - 1) + k - 1], recv_sem=recv_sems.at[a * (N_DEV - 1) + k - 1],
                    device_id=peer, device_id_type=MESH)
                cp.start()
                sends.append(cp)
        for k in range(1, N_DEV):
            peer, pidx = peer_of(k)
            for a in range(n):
                pltpu.make_async_remote_copy(
                    src_ref=src(a, pidx), dst_ref=outs[a].at[pidx],
                    send_sem=send_sems.at[a * (N_DEV - 1) + k - 1], recv_sem=recv_sems.at[a * (N_DEV - 1) + k - 1],
                    device_id=peer, device_id_type=MESH).wait_recv()
        for cp in sends:
            cp.wait_send()
        for cp in local:
            cp.wait()

    return pl.pallas_call(
        body, name=name, out_shape=out_shape,
        in_specs=[ANY] * n, out_specs=[ANY] * n,
        scratch_shapes=[pltpu.SemaphoreType.DMA((n * (N_DEV - 1),)),
                        pltpu.SemaphoreType.DMA((n * (N_DEV - 1),)),
                        pltpu.SemaphoreType.DMA((n,))],
    )(*arrs)


def _gather2(name, arrs):
    n = len(arrs)
    per = 7
    out_shape = [jax.ShapeDtypeStruct((N_DEV,) + a.shape, a.dtype) for a in arrs]

    def body(*refs):
        ins, outs = refs[:n], refs[n:2 * n]
        send_sems, recv_sems, local_sems = refs[2 * n:]
        x, y, c = lax.axis_index("x"), lax.axis_index("y"), lax.axis_index("c")
        sib = (x, y, 1 - c)
        chips = [(1 - x, y), (x, 1 - y), (1 - x, 1 - y)]

        def slot(a, px, py, pc):
            return outs[a].at[4 * px + 2 * py + pc]

        def copy(a, k, block, to, src=None):
            return pltpu.make_async_remote_copy(
                src_ref=slot(a, *block) if src is None else src, dst_ref=slot(a, *block),
                send_sem=send_sems.at[a * per + k], recv_sem=recv_sems.at[a * per + k],
                device_id=to, device_id_type=MESH)

        local = [pltpu.make_async_copy(ins[a], slot(a, x, y, c), local_sems.at[a]) for a in range(n)]
        for cp in local:
            cp.start()
        first = []
        for a in range(n):
            first += [copy(a, 1 + j, (x, y, c), (*chip, c), src=ins[a]) for j, chip in enumerate(chips)]
        for a in range(n):
            first.append(copy(a, 0, (x, y, c), sib, src=ins[a]))
        for cp in first:
            cp.start()
        passed = []
        for a in range(n):
            for j, chip in enumerate(chips):
                copy(a, 1 + j, (*chip, c), (x, y, c)).wait_recv()
                cp = copy(a, 4 + j, (*chip, c), sib)
                cp.start()
                passed.append(cp)
        for a in range(n):
            copy(a, 0, sib, (x, y, c)).wait_recv()
            for j, chip in enumerate(chips):
                copy(a, 4 + j, (*chip, 1 - c), (x, y, c)).wait_recv()
        for cp in first + passed:
            cp.wait_send()
        for cp in local:
            cp.wait()

    return pl.pallas_call(
        body, name=name, out_shape=out_shape,
        in_specs=[ANY] * n, out_specs=[ANY] * n,
        scratch_shapes=[pltpu.SemaphoreType.DMA((n * per,)), pltpu.SemaphoreType.DMA((n * per,)),
                        pltpu.SemaphoreType.DMA((n,))],
    )(*arrs)


def _seq_gather2(name, collective_id, arrs):
    n = len(arrs)
    per = 7

    def body(*refs):
        ins, outs = refs[:n], refs[n:2 * n]
        send_sems, recv_sems, local_sems = refs[2 * n:]
        x, y, c = lax.axis_index("x"), lax.axis_index("y"), lax.axis_index("c")
        sib = (x, y, 1 - c)
        chips = [(1 - x, y), (x, 1 - y), (1 - x, 1 - y)]
        barrier = pltpu.get_barrier_semaphore()
        for peer in [sib] + [(*chip, c) for chip in chips]:
            pl.semaphore_signal(barrier, inc=1, device_id=peer, device_id_type=MESH)
        pl.semaphore_wait(barrier, 4)

        def slot(a, px, py, pc):
            return outs[a].at[4 * px + 2 * py + pc]

        def copy(a, k, block, to, src=None):
            return pltpu.make_async_remote_copy(
                src_ref=slot(a, *block) if src is None else src, dst_ref=slot(a, *block),
                send_sem=send_sems.at[a * per + k], recv_sem=recv_sems.at[a * per + k],
                device_id=to, device_id_type=MESH)

        local = [pltpu.make_async_copy(ins[a], slot(a, x, y, c), local_sems.at[a]) for a in range(n)]
        for cp in local:
            cp.start()
        first = []
        for a in range(n):
            first += [copy(a, 1 + j, (x, y, c), (*chip, c), src=ins[a]) for j, chip in enumerate(chips)]
        for a in range(n):
            first.append(copy(a, 0, (x, y, c), sib, src=ins[a]))
        for cp in first:
            cp.start()
        passed = []
        for a in range(n):
            for j, chip in enumerate(chips):
                copy(a, 1 + j, (*chip, c), (x, y, c)).wait_recv()
                cp = copy(a, 4 + j, (*chip, c), sib)
                cp.start()
                passed.append(cp)
        for a in range(n):
            copy(a, 0, sib, (x, y, c)).wait_recv()
            for j, chip in enumerate(chips):
                copy(a, 4 + j, (*chip, 1 - c), (x, y, c)).wait_recv()
        for cp in first + passed:
            cp.wait_send()
        for cp in local:
            cp.wait()

    return pl.kernel(
        body, out_type=[jax.ShapeDtypeStruct((N_DEV,) + a.shape, a.dtype) for a in arrs],
        mesh=plsc.ScalarSubcoreMesh(axis_name="seq", num_cores=1),
        scratch_types=[pltpu.SemaphoreType.DMA((n * per,)), pltpu.SemaphoreType.DMA((n * per,)),
                       pltpu.SemaphoreType.DMA((n,))],
        compiler_params=pltpu.CompilerParams(collective_id=collective_id), name=name,
    )(*arrs)


def _pair_swap(name, arrs):
    n = len(arrs)
    out_shape = [jax.ShapeDtypeStruct((4,) + a.shape[2:], a.dtype) for a in arrs]

    def body(*refs):
        ins, outs = refs[:n], refs[n:2 * n]
        send_sems, recv_sems = refs[2 * n:]
        x, y, c = lax.axis_index("x"), lax.axis_index("y"), lax.axis_index("c")

        def copy(a, q):
            return pltpu.make_async_remote_copy(
                src_ref=ins[a].at[q, 1 - c], dst_ref=outs[a].at[q],
                send_sem=send_sems.at[a * 4 + q], recv_sem=recv_sems.at[a * 4 + q],
                device_id=(x, y, 1 - c), device_id_type=MESH)

        cps = [copy(a, q) for a in range(n) for q in range(4)]
        for cp in cps:
            cp.start()
        for cp in cps:
            cp.wait_recv()
        for cp in cps:
            cp.wait_send()

    return pl.pallas_call(
        body, name=name, out_shape=out_shape,
        in_specs=[ANY] * n, out_specs=[ANY] * n,
        scratch_shapes=[pltpu.SemaphoreType.DMA((n * 4,)), pltpu.SemaphoreType.DMA((n * 4,))],
    )(*arrs)


def _chip_exchange(name, arrs):
    n = len(arrs)
    out_shape = [jax.ShapeDtypeStruct((3,) + a.shape[1:], a.dtype) for a in arrs]

    def body(*refs):
        ins, outs = refs[:n], refs[n:2 * n]
        send_sems, recv_sems = refs[2 * n:]
        x, y, c = lax.axis_index("x"), lax.axis_index("y"), lax.axis_index("c")

        def copy(a, k):
            px = 1 - x if (k >> 1) & 1 else x
            py = 1 - y if k & 1 else y
            return pltpu.make_async_remote_copy(
                src_ref=ins[a].at[2 * px + py], dst_ref=outs[a].at[k - 1],
                send_sem=send_sems.at[a * 3 + k - 1], recv_sem=recv_sems.at[a * 3 + k - 1],
                device_id=(px, py, c), device_id_type=MESH)

        cps = [copy(a, k) for a in range(n) for k in (1, 2, 3)]
        for cp in cps:
            cp.start()
        for cp in cps:
            cp.wait_recv()
        for cp in cps:
            cp.wait_send()

    return pl.pallas_call(
        body, name=name, out_shape=out_shape,
        in_specs=[ANY] * n, out_specs=[ANY] * n,
        scratch_shapes=[pltpu.SemaphoreType.DMA((n * 3,)), pltpu.SemaphoreType.DMA((n * 3,))],
    )(*arrs)


class _Rider:
    def __init__(self, arrays, out_shapes, n_sems, build, aliases=None):
        self.arrays, self.out_shapes, self.n_sems, self.build = list(arrays), list(out_shapes), n_sems, build
        self.aliases = dict(aliases or {})


def _merge_riders(r1, r2):
    n1i, n1o, n1s = len(r1.arrays), len(r1.out_shapes), r1.n_sems

    def build(ins, outs, send_sems, recv_sems):
        a = r1.build(ins[:n1i], outs[:n1o], send_sems.at[pl.ds(0, n1s)], recv_sems.at[pl.ds(0, n1s)])
        b = r2.build(ins[n1i:], outs[n1o:], send_sems.at[pl.ds(n1s, r2.n_sems)], recv_sems.at[pl.ds(n1s, r2.n_sems)])
        return tuple(p + q for p, q in zip(a, b))

    aliases = dict(r1.aliases)
    aliases.update({k + n1i: v + n1o for k, v in r2.aliases.items()})
    return _Rider(r1.arrays + r2.arrays, r1.out_shapes + r2.out_shapes, n1s + r2.n_sems, build, aliases)


def _place():
    x, y, c = lax.axis_index("x"), lax.axis_index("y"), lax.axis_index("c")
    chips = [(1 - x, y), (x, 1 - y), (1 - x, 1 - y)]
    return x, y, c, chips


def _ride_gather_ici(arrs):
    n = len(arrs)

    def build(ins, outs, send_sems, recv_sems):
        x, y, c, chips = _place()
        peers = [(*chip, c) for chip in chips] + [(x, y, 1 - c)]
        me = 4 * x + 2 * y + c
        local = [pltpu.make_async_copy(ins[a], outs[a].at[me], send_sems.at[a * 5 + 4]) for a in range(n)]
        sends, recvs = [], []
        for a in range(n):
            for j, (px, py, pc) in enumerate(peers):
                sends.append(pltpu.make_async_remote_copy(
                    src_ref=ins[a], dst_ref=outs[a].at[me], send_sem=send_sems.at[a * 5 + j],
                    recv_sem=recv_sems.at[a * 5 + j], device_id=(px, py, pc), device_id_type=MESH))
                recvs.append(pltpu.make_async_remote_copy(
                    src_ref=ins[a], dst_ref=outs[a].at[4 * px + 2 * py + pc], send_sem=send_sems.at[a * 5 + j],
                    recv_sem=recv_sems.at[a * 5 + j], device_id=(px, py, pc), device_id_type=MESH))
        return local, sends, recvs

    shapes = [jax.ShapeDtypeStruct((N_DEV,) + a.shape, a.dtype) for a in arrs]
    return _Rider(arrs, shapes, n * 5, build)


def _ride_gather_direct(arrs):
    n = len(arrs)

    def build(ins, outs, send_sems, recv_sems):
        x, y, c, _ = _place()
        me = 4 * x + 2 * y + c
        local = [pltpu.make_async_copy(ins[a], outs[a].at[me], send_sems.at[a * N_DEV + 7]) for a in range(n)]
        sends, recvs = [], []
        for a in range(n):
            for k in range(1, N_DEV):
                px = 1 - x if (k >> 2) & 1 else x
                py = 1 - y if (k >> 1) & 1 else y
                pc = 1 - c if k & 1 else c
                sem = a * N_DEV + k - 1
                sends.append(pltpu.make_async_remote_copy(
                    src_ref=ins[a], dst_ref=outs[a].at[me], send_sem=send_sems.at[sem], recv_sem=recv_sems.at[sem],
                    device_id=(px, py, pc), device_id_type=MESH))
                recvs.append(pltpu.make_async_remote_copy(
                    src_ref=ins[a], dst_ref=outs[a].at[4 * px + 2 * py + pc], send_sem=send_sems.at[sem],
                    recv_sem=recv_sems.at[sem], device_id=(px, py, pc), device_id_type=MESH))
        return local, sends, recvs

    shapes = [jax.ShapeDtypeStruct((N_DEV,) + a.shape, a.dtype) for a in arrs]
    return _Rider(arrs, shapes, n * N_DEV, build)


def _ride_gather_d2d(gathered):
    n = len(gathered)

    def build(ins, outs, send_sems, recv_sems):
        x, y, c, chips = _place()
        sends, recvs = [], []
        for a in range(n):
            for j, (px, py) in enumerate(chips):
                mine = outs[a].at[4 * px + 2 * py + c]
                theirs = outs[a].at[4 * px + 2 * py + 1 - c]
                sends.append(pltpu.make_async_remote_copy(
                    src_ref=mine, dst_ref=mine, send_sem=send_sems.at[a * 3 + j], recv_sem=recv_sems.at[a * 3 + j],
                    device_id=(x, y, 1 - c), device_id_type=MESH))
                recvs.append(pltpu.make_async_remote_copy(
                    src_ref=mine, dst_ref=theirs, send_sem=send_sems.at[a * 3 + j], recv_sem=recv_sems.at[a * 3 + j],
                    device_id=(x, y, 1 - c), device_id_type=MESH))
        return [], sends, recvs

    shapes = [jax.ShapeDtypeStruct(a.shape, a.dtype) for a in gathered]
    return _Rider(gathered, shapes, n * 3, build, aliases={a: a for a in range(n)})


def _ride_pair_swap(arrs):
    n = len(arrs)

    def build(ins, outs, send_sems, recv_sems):
        x, y, c, _ = _place()
        cps = [pltpu.make_async_remote_copy(
            src_ref=ins[a].at[q, 1 - c], dst_ref=outs[a].at[q], send_sem=send_sems.at[a * 4 + q],
            recv_sem=recv_sems.at[a * 4 + q], device_id=(x, y, 1 - c), device_id_type=MESH)
            for a in range(n) for q in range(4)]
        return [], cps, cps

    shapes = [jax.ShapeDtypeStruct((4,) + a.shape[2:], a.dtype) for a in arrs]
    return _Rider(arrs, shapes, n * 4, build)


def _ride_chip_exchange(arrs):
    n = len(arrs)

    def build(ins, outs, send_sems, recv_sems):
        x, y, c, _ = _place()
        cps = []
        for a in range(n):
            for k in (1, 2, 3):
                px = 1 - x if (k >> 1) & 1 else x
                py = 1 - y if k & 1 else y
                cps.append(pltpu.make_async_remote_copy(
                    src_ref=ins[a].at[2 * px + py], dst_ref=outs[a].at[k - 1], send_sem=send_sems.at[a * 3 + k - 1],
                    recv_sem=recv_sems.at[a * 3 + k - 1], device_id=(px, py, c), device_id_type=MESH))
        return [], cps, cps

    shapes = [jax.ShapeDtypeStruct((3,) + a.shape[1:], a.dtype) for a in arrs]
    return _Rider(arrs, shapes, n * 3, build)


def _call(body, name, grid, in_specs, out_specs, out_shape, args, scratch=(), rider=None):
    n_in, n_out, n_scr = len(in_specs), len(out_specs), len(scratch)
    sem = ("arbitrary",) * len(grid)
    if rider is None:
        outs = pl.pallas_call(
            body, name=name, grid=grid, in_specs=in_specs, out_specs=out_specs, out_shape=out_shape,
            scratch_shapes=list(scratch), compiler_params=_params(sem))(*args)
        return outs, []
    ri, ro = len(rider.arrays), len(rider.out_shapes)

    def riding(*refs):
        ins, r_ins = refs[:n_in], refs[n_in:n_in + ri]
        outs = refs[n_in + ri:n_in + ri + n_out]
        r_outs = refs[n_in + ri + n_out:n_in + ri + n_out + ro]
        scr = refs[n_in + ri + n_out + ro:n_in + ri + n_out + ro + n_scr]
        send_sems, recv_sems = refs[-2:]
        first = functools.reduce(jnp.logical_and, [pl.program_id(k) == 0 for k in range(len(grid))])
        last = functools.reduce(jnp.logical_and, [pl.program_id(k) == grid[k] - 1 for k in range(len(grid))])

        @pl.when(first)
        def _():
            local, sends, _ = rider.build(r_ins, r_outs, send_sems, recv_sems)
            for cp in local + sends:
                cp.start()

        body(*ins, *outs, *scr)

        @pl.when(last)
        def _():
            local, sends, recvs = rider.build(r_ins, r_outs, send_sems, recv_sems)
            for cp in recvs:
                cp.wait_recv()
            for cp in sends:
                cp.wait_send()
            for cp in local:
                cp.wait()

    outs = pl.pallas_call(
        riding, name=name, grid=grid,
        in_specs=list(in_specs) + [ANY] * ri, out_specs=list(out_specs) + [ANY] * ro,
        out_shape=list(out_shape) + rider.out_shapes,
        scratch_shapes=list(scratch) + [pltpu.SemaphoreType.DMA((rider.n_sems,)), pltpu.SemaphoreType.DMA((rider.n_sems,))],
        input_output_aliases={n_in + k: n_out + v for k, v in rider.aliases.items()},
        compiler_params=_params(sem))(*args, *rider.arrays)
    return outs[:n_out], outs[n_out:]


def _comm(name, rider):
    def body(dummy_ref, out_ref):
        out_ref[...] = dummy_ref[...]

    dummy = jnp.zeros((SUBLANES, LANES), F32)
    spec = pl.BlockSpec((SUBLANES, LANES), lambda i: (0, 0))
    _, r_outs = _call(body, name, (1,), [spec], [spec], [jax.ShapeDtypeStruct(dummy.shape, F32)], [dummy], rider=rider)
    return r_outs


def _ada_fwd(c_all, w_ada_sh, b_ada_sh):
    nb, d = c_all.shape
    ncol = w_ada_sh.shape[1]

    def body(c_ref, w_ref, b_ref, mod_ref, cact_ref):
        cc = c_ref[...]
        ca = cc * jax.nn.sigmoid(cc)
        cact_ref[...] = ca
        mod_ref[...] = _dot(ca.astype(BF16), w_ref[...].astype(BF16), NN) + b_ref[...]

    return pl.pallas_call(
        body, name="ada_fwd",
        out_shape=[jax.ShapeDtypeStruct((nb, ncol), F32), jax.ShapeDtypeStruct((nb, d), F32)],
        compiler_params=_params(),
    )(c_all, w_ada_sh, b_ada_sh)


def _rms(xv):
    rstd = lax.rsqrt(jnp.mean(xv * xv, axis=-1, keepdims=True) + EPS)
    return xv * rstd, rstd


def _rms_bwd(dxhat, xhat, rstd):
    return rstd * (dxhat - xhat * jnp.mean(dxhat * xhat, axis=-1, keepdims=True))


def _colsum(v):
    return jnp.sum(v, axis=0, keepdims=True)


def _expm1(v):
    series = v * (1.0 + v * (0.5 + v * (1.0 / 6.0 + v * (1.0 / 24.0 + v * (1.0 / 120.0 + v * (1.0 / 720.0))))))
    return jnp.where(jnp.abs(v) < 0.3, series, jnp.exp(v) - 1.0)


def _softplus(v):
    return jnp.maximum(v, 0.0) + jnp.log1p(jnp.exp(-jnp.abs(v)))


def _gelu(v):
    t = jnp.tanh(GELU_K0 * (v + GELU_K1 * v * v * v))
    return 0.5 * v * (1.0 + t), t


def _dgelu(v, t):
    return 0.5 * (1.0 + t) + 0.5 * v * (1.0 - t * t) * GELU_K0 * (1.0 + 3.0 * GELU_K1 * v * v)


def _shift_down(v, k, prev8):
    r = pltpu.roll(v, k, 0)
    pr = pltpu.roll(prev8, k, 0)
    row8 = lax.broadcasted_iota(jnp.int32, prev8.shape, 0)
    top = jnp.where(row8 < k, pr, r[0:SUBLANES])
    return jnp.concatenate([top, r[SUBLANES:]], axis=0)


def _shift_up(v, k, next8):
    t = v.shape[0]
    r = pltpu.roll(v, t - k, 0)
    nr = pltpu.roll(next8, SUBLANES - k, 0)
    row8 = lax.broadcasted_iota(jnp.int32, next8.shape, 0)
    bot = jnp.where(row8 >= SUBLANES - k, nr, r[t - SUBLANES:t])
    return jnp.concatenate([r[:t - SUBLANES], bot], axis=0)


def _scan_fwd(a, b, h0):
    t = a.shape[0]
    row = lax.broadcasted_iota(jnp.int32, a.shape, 0)
    s = 1
    while s < min(t, SUBLANES):
        a_sh = pltpu.roll(a, s, 0)
        b_sh = pltpu.roll(b, s, 0)
        m = row >= s
        b = jnp.where(m, a * b_sh + b, b)
        a = jnp.where(m, a * a_sh, a)
        s *= 2
    while s < t:
        b = jnp.concatenate([b[:s], a[s:] * b[:t - s] + b[s:]], axis=0)
        a = jnp.concatenate([a[:s], a[s:] * a[:t - s]], axis=0)
        s *= 2
    return b + a * h0


def _scan_rev(m, b, g_next):
    t = m.shape[0]
    row = lax.broadcasted_iota(jnp.int32, m.shape, 0)
    s = 1
    while s < min(t, SUBLANES):
        m_sh = pltpu.roll(m, t - s, 0)
        b_sh = pltpu.roll(b, t - s, 0)
        msk = row < t - s
        b = jnp.where(msk, m * b_sh + b, b)
        m = jnp.where(msk, m * m_sh, m)
        s *= 2
    while s < t:
        b = jnp.concatenate([m[:t - s] * b[s:] + b[:t - s], b[t - s:]], axis=0)
        m = jnp.concatenate([m[:t - s] * m[s:], m[t - s:]], axis=0)
        s *= 2
    return b + m * g_next


def _lru_gates(u, wa, wx, ba, bx, sp):
    ub = u.astype(BF16)
    r = jax.nn.sigmoid(_dot(ub, wa, NN) + ba)
    i = jax.nn.sigmoid(_dot(ub, wx, NN) + bx)
    log_a = (-RG_C * r) * sp
    a = jnp.exp(log_a)
    mult = jnp.sqrt(-_expm1(2.0 * log_a))
    return ub, r, i, a, mult


def _conv3(p, pp, w_ref, lo):
    p1 = _shift_down(p, 1, pp)
    p2 = _shift_down(p, 2, pp)
    q = (w_ref[0:1, lo:lo + LANES] * p2 + w_ref[1:2, lo:lo + LANES] * p1) + w_ref[2:3, lo:lo + LANES] * p
    return q, p1, p2


def _conv4(xv, xp, w_ref, b_ref, lo):
    x1 = _shift_down(xv, 1, xp)
    x2 = _shift_down(xv, 2, xp)
    x3 = _shift_down(xv, 3, xp)
    u = (((w_ref[0:1, lo:lo + LANES] * x3 + w_ref[1:2, lo:lo + LANES] * x2) + w_ref[2:3, lo:lo + LANES] * x1)
         + w_ref[3:4, lo:lo + LANES] * xv) + b_ref[:, lo:lo + LANES]
    return u, x1, x2, x3


def _mix_in_fwd(x2d, mod6, g_mix, w_in_t, tm, rider=None):
    s, d = x2d.shape
    din = w_in_t.shape[0]

    def body(x_ref, mod_ref, g_ref, w_ref, hn_ref, proj_ref):
        xhat, _ = _rms(x_ref[...])
        hn = ((xhat * g_ref[...]) * (1.0 + mod_ref[1:2, :]) + mod_ref[0:1, :]).astype(BF16)
        hn_ref[...] = hn
        proj_ref[...] = _dot(hn, w_ref[...], NT)

    return _call(
        body, "mix_in_fwd", (s // tm,),
        [pl.BlockSpec((tm, d), lambda i: (i, 0)), _full(mod6.shape), _full(g_mix.shape), _full(w_in_t.shape)],
        [pl.BlockSpec((tm, d), lambda i: (i, 0)), pl.BlockSpec((tm, din), lambda i: (i, 0))],
        [jax.ShapeDtypeStruct((s, d), BF16), jax.ShapeDtypeStruct((s, din), F32)],
        [x2d, mod6, g_mix, w_in_t], rider=rider)


def _mixer_fwd(proj, conv_sc, conv_lru, conv_b, wa_bd, wx_bd, ba, bx, lam, width, rider=None):
    s, din = proj.shape
    t = min(MIX_ROWS, s)
    nblk = width // LANES
    hb = t // SUBLANES

    def body(proj_ref, projp_ref, wsc_ref, wlru_ref, blru_ref, wa_ref, wx_ref, ba_ref, bx_ref, lam_ref,
             ymix_ref, h_ref, hc_ref):
        i = pl.program_id(0)

        @pl.when(i == 0)
        def _():
            hc_ref[...] = jnp.zeros_like(hc_ref)

        has_prev = i > 0
        for j in range(nblk):
            lo = j * LANES

            def col(p, ref=proj_ref):
                return ref[:, p * width + lo:p * width + lo + LANES]

            def prev(p):
                return jnp.where(has_prev, col(p, projp_ref), 0.0)

            p = col(1) * col(2)
            q, _, _ = _conv3(p, prev(1) * prev(2), wsc_ref, lo)
            ymix_ref[:, lo:lo + LANES] = (col(0) * q).astype(BF16)

            u, _, _, _ = _conv4(col(4), prev(4), wlru_ref, blru_ref, lo)
            sp = _softplus(-lam_ref[:, lo:lo + LANES])
            _, r, ig, a, mult = _lru_gates(u, wa_ref[j], wx_ref[j], ba_ref[:, lo:lo + LANES], bx_ref[:, lo:lo + LANES], sp)
            h = _scan_fwd(a, mult * (ig * u), hc_ref[0:1, lo:lo + LANES])
            h_ref[:, lo:lo + LANES] = h
            hc_ref[0:1, lo:lo + LANES] = h[t - 1:t, :]
            gel, _ = _gelu(col(3))
            ymix_ref[:, width + lo:width + lo + LANES] = (gel * h).astype(BF16)

    small = [conv_sc, conv_lru, conv_b, wa_bd, wx_bd, ba, bx, lam]
    return _call(
        body, "mixer_fwd", (s // t,),
        [pl.BlockSpec((t, din), lambda i: (i, 0)),
         pl.BlockSpec((SUBLANES, din), lambda i: (jnp.maximum(i * hb - 1, 0), 0))]
        + [_full(a.shape) for a in small],
        [pl.BlockSpec((t, 2 * width), lambda i: (i, 0)), pl.BlockSpec((t, width), lambda i: (i, 0))],
        [jax.ShapeDtypeStruct((s, 2 * width), BF16), jax.ShapeDtypeStruct((s, width), F32)],
        [proj, proj, *small], scratch=[pltpu.VMEM((SUBLANES, width), F32)], rider=rider)


def _mix_out_fwd(ymix, x2d, w_out, mod6, g_mlp, tm, rider=None):
    s, d = x2d.shape

    def body(y_ref, x_ref, w_ref, mod_ref, g_ref, mix_ref, x2_ref, hn_ref):
        mix = _dot(y_ref[...], w_ref[...], NN)
        mix_ref[...] = mix
        x2 = x_ref[...] + mod_ref[2:3, :] * mix
        x2_ref[...] = x2
        xhat, _ = _rms(x2)
        hn_ref[...] = ((xhat * g_ref[...]) * (1.0 + mod_ref[4:5, :]) + mod_ref[3:4, :]).astype(BF16)

    tile = pl.BlockSpec((tm, d), lambda i: (i, 0))
    return _call(
        body, "mix_out_fwd", (s // tm,),
        [tile, tile, _full(w_out.shape), _full(mod6.shape), _full(g_mlp.shape)],
        [tile, tile, tile],
        [jax.ShapeDtypeStruct((s, d), F32), jax.ShapeDtypeStruct((s, d), F32), jax.ShapeDtypeStruct((s, d), BF16)],
        [ymix, x2d, w_out, mod6, g_mlp], rider=rider)


def _mlp_fwd_loss(hn2, w_up_t, w_down, x2, target, mod6, g_final, tm, tk):
    s, d = hn2.shape
    f = w_up_t.shape[0]
    nk = f // tk

    def body(hn_ref, wu_ref, wd_ref, x2_ref, t_ref, mod_ref, g_ref, z_ref, dx3_ref, dyb_ref, st_ref, y_ref):
        i, k = pl.program_id(0), pl.program_id(1)

        @pl.when(jnp.logical_and(i == 0, k == 0))
        def _():
            st_ref[...] = jnp.zeros_like(st_ref)

        z = jnp.maximum(_dot(hn_ref[...], wu_ref[...], NT), 0.0)
        z_ref[...] = z.astype(BF16)
        part = _dot((z * z).astype(BF16), wd_ref[...], NN)

        @pl.when(k == 0)
        def _():
            y_ref[...] = part

        @pl.when(k > 0)
        def _():
            y_ref[...] += part

        @pl.when(k == nk - 1)
        def _():
            gate = mod_ref[5:6, :]
            yv = y_ref[...]
            xhat, rstd = _rms(x2_ref[...] + gate * yv)
            diff = xhat * g_ref[...] - t_ref[...]
            dyo = diff * (1.0 / d)
            dx3 = _rms_bwd(dyo * g_ref[...], xhat, rstd)
            dx3_ref[...] = dx3
            dyb_ref[...] = (gate * dx3).astype(BF16)
            st_ref[0:1, :] += _colsum(dyo * xhat)
            st_ref[1:2, :] += _colsum(dx3 * yv)
            st_ref[2:3, :] += _colsum(diff * diff)

    tile = pl.BlockSpec((tm, d), lambda i, k: (i, 0))
    wblk = pl.BlockSpec((tk, d), lambda i, k: (k, 0))
    return pl.pallas_call(
        body, name="mlp_fwd_loss", grid=(s // tm, nk),
        in_specs=[tile, wblk, wblk, tile, tile, _full(mod6.shape), _full(g_final.shape)],
        out_specs=[pl.BlockSpec((tm, tk), lambda i, k: (i, k)), tile, tile, _full((SUBLANES, d))],
        out_shape=[jax.ShapeDtypeStruct((s, f), BF16), jax.ShapeDtypeStruct((s, d), F32),
                   jax.ShapeDtypeStruct((s, d), BF16), jax.ShapeDtypeStruct((SUBLANES, d), F32)],
        scratch_shapes=[pltpu.VMEM((tm, d), F32)],
        compiler_params=_params(("arbitrary", "arbitrary")),
    )(hn2, w_up_t, w_down, x2, target, mod6, g_final)


def _mlp_bwd_dx(dyb, z, w_down, w_up_t, tm, tk):
    s, d = dyb.shape
    f = z.shape[1]

    def body(dy_ref, z_ref, wd_ref, wu_ref, dz_ref, dh_ref):
        k = pl.program_id(1)
        dz = ((2.0 * z_ref[...].astype(F32)) * _dot(dy_ref[...], wd_ref[...], NT)).astype(BF16)
        dz_ref[...] = dz
        part = _dot(dz, wu_ref[...], NN)

        @pl.when(k == 0)
        def _():
            dh_ref[...] = part

        @pl.when(k > 0)
        def _():
            dh_ref[...] += part

    return pl.pallas_call(
        body, name="mlp_bwd_dx", grid=(s // tm, f // tk),
        in_specs=[pl.BlockSpec((tm, d), lambda i, k: (i, 0)), pl.BlockSpec((tm, tk), lambda i, k: (i, k)),
                  pl.BlockSpec((tk, d), lambda i, k: (k, 0)), pl.BlockSpec((tk, d), lambda i, k: (k, 0))],
        out_specs=[pl.BlockSpec((tm, tk), lambda i, k: (i, k)), pl.BlockSpec((tm, d), lambda i, k: (i, 0))],
        out_shape=[jax.ShapeDtypeStruct((s, f), BF16), jax.ShapeDtypeStruct((s, d), F32)],
        compiler_params=_params(("parallel", "arbitrary")),
    )(dyb, z, w_down, w_up_t)


def _mlp_bwd_dw(z, dz, dyb, hn2, tm, tk):
    s, d = dyb.shape
    f = z.shape[1]

    def body(z_ref, dz_ref, dy_ref, hn_ref, gd_ref, gu_ref):
        i = pl.program_id(1)

        @pl.when(i == 0)
        def _():
            gd_ref[...] = jnp.zeros_like(gd_ref)
            gu_ref[...] = jnp.zeros_like(gu_ref)

        zf = z_ref[...].astype(F32)
        gd_ref[...] += _dot((zf * zf).astype(BF16), dy_ref[...], TN)
        gu_ref[...] += _dot(dz_ref[...], hn_ref[...], TN)

    return pl.pallas_call(
        body, name="mlp_bwd_dw", grid=(f // tk, s // tm),
        in_specs=[pl.BlockSpec((tm, tk), lambda k, i: (i, k)), pl.BlockSpec((tm, tk), lambda k, i: (i, k)),
                  pl.BlockSpec((tm, d), lambda k, i: (i, 0)), pl.BlockSpec((tm, d), lambda k, i: (i, 0))],
        out_specs=[pl.BlockSpec((tk, d), lambda k, i: (k, 0)), pl.BlockSpec((tk, d), lambda k, i: (k, 0))],
        out_shape=[jax.ShapeDtypeStruct((f, d), F32), jax.ShapeDtypeStruct((f, d), F32)],
        compiler_params=_params(("parallel", "arbitrary")),
    )(z, dz, dyb, hn2)


def _mix_out_bwd(dhn2, x2, dx3, mix, ymix, w_out, mod6, g_mlp, tm, rider=None):
    s, d = x2.shape

    def body(dh_ref, x2_ref, dx3_ref, mix_ref, y_ref, w_ref, mod_ref, g_ref, dx2_ref, dym_ref, gw_ref, st_ref):
        i = pl.program_id(0)

        @pl.when(i == 0)
        def _():
            st_ref[...] = jnp.zeros_like(st_ref)
            gw_ref[...] = jnp.zeros_like(gw_ref)

        dh = dh_ref[...]
        xhat, rstd = _rms(x2_ref[...])
        dn = dh * (1.0 + mod_ref[4:5, :])
        dx2 = dx3_ref[...] + _rms_bwd(dn * g_ref[...], xhat, rstd)
        dx2_ref[...] = dx2
        st_ref[0:1, :] += _colsum(dh)
        st_ref[1:2, :] += _colsum(dh * (xhat * g_ref[...]))
        st_ref[2:3, :] += _colsum(dn * xhat)
        st_ref[3:4, :] += _colsum(dx2 * mix_ref[...])
        dmix = (mod_ref[2:3, :] * dx2).astype(BF16)
        dym_ref[...] = _dot(dmix, w_ref[...], NT)
        gw_ref[...] += _dot(y_ref[...], dmix, TN)

    tile = pl.BlockSpec((tm, d), lambda i: (i, 0))
    return _call(
        body, "mix_out_bwd", (s // tm,),
        [tile, tile, tile, tile, tile, _full(w_out.shape), _full(mod6.shape), _full(g_mlp.shape)],
        [tile, tile, _full((d, d)), _full((SUBLANES, d))],
        [jax.ShapeDtypeStruct((s, d), F32), jax.ShapeDtypeStruct((s, d), F32),
         jax.ShapeDtypeStruct((d, d), F32), jax.ShapeDtypeStruct((SUBLANES, d), F32)],
        [dhn2, x2, dx3, mix, ymix, w_out, mod6, g_mlp], rider=rider)


def _mixer_bwd(proj, dymix, h_all, conv_sc, conv_lru, conv_b, wa_bd, wx_bd, ba, bx, lam, width, rider=None):
    s, din = proj.shape
    t = min(MIX_ROWS, s)
    nt = s // t
    nblk = width // LANES
    hb = t // SUBLANES
    last8 = s // SUBLANES - 1

    def body(proj_ref, projp_ref, projn_ref, dy_ref, dyn_ref, h_ref, hp_ref,
             wsc_ref, wlru_ref, blru_ref, wa_ref, wx_ref, ba_ref, bx_ref, lam_ref,
             dproj_ref, small_ref, gwa_ref, gwx_ref, an_ref, gn_ref, dun_ref):
        i = pl.program_id(0)

        @pl.when(i == 0)
        def _():
            small_ref[...] = jnp.zeros_like(small_ref)
            gwa_ref[...] = jnp.zeros_like(gwa_ref)
            gwx_ref[...] = jnp.zeros_like(gwx_ref)
            an_ref[...] = jnp.zeros_like(an_ref)
            gn_ref[...] = jnp.zeros_like(gn_ref)
            dun_ref[...] = jnp.zeros_like(dun_ref)

        has_prev = i < nt - 1
        has_next = i > 0
        for j in range(nblk):
            lo = j * LANES
            ls = slice(lo, lo + LANES)

            def col(p, ref=proj_ref):
                return ref[:, p * width + lo:p * width + lo + LANES]

            def prev(p):
                return jnp.where(has_prev, col(p, projp_ref), 0.0)

            def nxt(p):
                return jnp.where(has_next, col(p, projn_ref), 0.0)

            def add_row(r, v):
                small_ref[r:r + 1, ls] += _colsum(v)

            sc_b, sc_c, sc_x = col(0), col(1), col(2)
            p = sc_c * sc_x
            q, p1, p2 = _conv3(p, prev(1) * prev(2), wsc_ref, lo)
            dys = dy_ref[:, ls]
            dproj_ref[:, ls] = (dys * q).astype(BF16)
            dq = dys * sc_b
            dqn = jnp.where(has_next, dyn_ref[:, ls], 0.0) * nxt(0)
            dp = (wsc_ref[2:3, ls] * dq + wsc_ref[1:2, ls] * _shift_up(dq, 1, dqn)) + wsc_ref[0:1, ls] * _shift_up(dq, 2, dqn)
            dproj_ref[:, width + lo:width + lo + LANES] = (dp * sc_x).astype(BF16)
            dproj_ref[:, 2 * width + lo:2 * width + lo + LANES] = (dp * sc_c).astype(BF16)
            add_row(0, dq * p2)
            add_row(1, dq * p1)
            add_row(2, dq * p)

            xv = col(4)
            u, x1, x2, x3 = _conv4(xv, prev(4), wlru_ref, blru_ref, lo)
            lam_v = lam_ref[:, ls]
            sp = _softplus(-lam_v)
            wa, wx = wa_ref[j], wx_ref[j]
            ub, r, ig, a, mult = _lru_gates(u, wa, wx, ba_ref[:, ls], bx_ref[:, ls], sp)
            iu = ig * u
            h = h_ref[:, ls]
            hm1 = _shift_down(h, 1, jnp.where(has_prev, hp_ref[:, ls], 0.0))
            lyv = col(3)
            gel, th = _gelu(lyv)
            dyl = dy_ref[:, width + lo:width + lo + LANES]
            dproj_ref[:, 3 * width + lo:3 * width + lo + LANES] = (dyl * h * _dgelu(lyv, th)).astype(BF16)
            a_next = jnp.broadcast_to(an_ref[0:1, ls], (SUBLANES, LANES))
            g = _scan_rev(_shift_up(a, 1, a_next), dyl * gel, gn_ref[0:1, ls])
            an_ref[0:1, ls] = a[0:1, :]
            gn_ref[0:1, ls] = g[0:1, :]
            da = g * hm1
            dmult = g * iu
            diu = g * mult
            dlog_a = da * a - dmult * ((a * a) / mult)
            dpre_a = (dlog_a * (-RG_C * sp)) * (r * (1.0 - r))
            dpre_x = (diu * u) * (ig * (1.0 - ig))
            dab, dxb = dpre_a.astype(BF16), dpre_x.astype(BF16)
            du = diu * ig + _dot(dab, wa, NT) + _dot(dxb, wx, NT)
            gwa_ref[j] += _dot(ub, dab, TN)
            gwx_ref[j] += _dot(ub, dxb, TN)
            dun = dun_ref[:, ls]
            dun_ref[:, ls] = du[0:SUBLANES, :]
            dlx = (((wlru_ref[3:4, ls] * du + wlru_ref[2:3, ls] * _shift_up(du, 1, dun))
                    + wlru_ref[1:2, ls] * _shift_up(du, 2, dun)) + wlru_ref[0:1, ls] * _shift_up(du, 3, dun))
            dproj_ref[:, 4 * width + lo:4 * width + lo + LANES] = dlx.astype(BF16)
            add_row(3, du * x3)
            add_row(4, du * x2)
            add_row(5, du * x1)
            add_row(6, du * xv)
            add_row(7, du)
            add_row(8, dpre_a)
            add_row(9, dpre_x)
            add_row(10, (dlog_a * (RG_C * r)) * jax.nn.sigmoid(-lam_v))

    small = [conv_sc, conv_lru, conv_b, wa_bd, wx_bd, ba, bx, lam]
    rev = lambda i: nt - 1 - i
    return _call(
        body, "mixer_bwd", (nt,),
        [pl.BlockSpec((t, din), lambda i: (rev(i), 0)),
         pl.BlockSpec((SUBLANES, din), lambda i: (jnp.maximum(rev(i) * hb - 1, 0), 0)),
         pl.BlockSpec((SUBLANES, din), lambda i: (jnp.minimum((rev(i) + 1) * hb, last8), 0)),
         pl.BlockSpec((t, 2 * width), lambda i: (rev(i), 0)),
         pl.BlockSpec((SUBLANES, 2 * width), lambda i: (jnp.minimum((rev(i) + 1) * hb, last8), 0)),
         pl.BlockSpec((t, width), lambda i: (rev(i), 0)),
         pl.BlockSpec((SUBLANES, width), lambda i: (jnp.maximum(rev(i) * hb - 1, 0), 0))]
        + [_full(a.shape) for a in small],
        [pl.BlockSpec((t, din), lambda i: (rev(i), 0)), _full((2 * SUBLANES, width)),
         _full(wa_bd.shape), _full(wx_bd.shape)],
        [jax.ShapeDtypeStruct((s, din), BF16), jax.ShapeDtypeStruct((2 * SUBLANES, width), F32),
         jax.ShapeDtypeStruct(wa_bd.shape, F32), jax.ShapeDtypeStruct(wx_bd.shape, F32)],
        [proj, proj, proj, dymix, dymix, h_all, h_all, *small],
        scratch=[pltpu.VMEM((SUBLANES, width), F32), pltpu.VMEM((SUBLANES, width), F32),
                 pltpu.VMEM((SUBLANES, width), F32)], rider=rider)


def _mix_in_bwd_dx(dproj, x2d, dx2, w_in_t, mod6, g_mix, tm, rider=None):
    s, d = x2d.shape
    din = dproj.shape[1]

    def body(dp_ref, x_ref, dx2_ref, w_ref, mod_ref, g_ref, gx_ref, st_ref):
        i = pl.program_id(0)

        @pl.when(i == 0)
        def _():
            st_ref[...] = jnp.zeros_like(st_ref)

        dh = _dot(dp_ref[...], w_ref[...], NN)
        xhat, rstd = _rms(x_ref[...])
        dn = dh * (1.0 + mod_ref[1:2, :])
        gx_ref[...] = dx2_ref[...] + _rms_bwd(dn * g_ref[...], xhat, rstd)
        st_ref[0:1, :] += _colsum(dh)
        st_ref[1:2, :] += _colsum(dh * (xhat * g_ref[...]))
        st_ref[2:3, :] += _colsum(dn * xhat)

    tile = pl.BlockSpec((tm, d), lambda i: (i, 0))
    return _call(
        body, "mix_in_bwd_dx", (s // tm,),
        [pl.BlockSpec((tm, din), lambda i: (i, 0)), tile, tile, _full(w_in_t.shape), _full(mod6.shape),
         _full(g_mix.shape)],
        [tile, _full((SUBLANES, d))],
        [jax.ShapeDtypeStruct((s, d), F32), jax.ShapeDtypeStruct((SUBLANES, d), F32)],
        [dproj, x2d, dx2, w_in_t, mod6, g_mix], rider=rider)


def _mix_in_bwd_dw(dproj, hn1, tm, tn, rider=None):
    s, d = hn1.shape
    din = dproj.shape[1]

    def body(dp_ref, hn_ref, gw_ref):
        i = pl.program_id(1)

        @pl.when(i == 0)
        def _():
            gw_ref[...] = jnp.zeros_like(gw_ref)

        gw_ref[...] += _dot(dp_ref[...], hn_ref[...], TN)

    return _call(
        body, "mix_in_bwd_dw", (din // tn, s // tm),
        [pl.BlockSpec((tm, tn), lambda p, i: (i, p)), pl.BlockSpec((tm, d), lambda p, i: (i, 0))],
        [pl.BlockSpec((tn, d), lambda p, i: (p, 0))],
        [jax.ShapeDtypeStruct((din, d), F32)],
        [dproj, hn1], rider=rider)


def _adamw(w, g, m, v):
    m = ADAM_B1 * m + (1.0 - ADAM_B1) * g
    v = ADAM_B2 * v + (1.0 - ADAM_B2) * (g * g)
    m_hat = m / (1.0 - ADAM_B1 ** ADAM_STEP)
    v_hat = v / (1.0 - ADAM_B2 ** ADAM_STEP)
    delta = -ADAM_LR * (m_hat / (jnp.sqrt(v_hat) + ADAM_EPS) + ADAM_WD * w)
    return delta, m, v


def _pair_sum(g4, h4, core_chip, tr, name):
    _, _, r, n = g4.shape

    def body(sc_ref, g_ref, h_ref, sb_ref, own_ref):
        q = pl.program_id(1)
        ssum = g_ref[...] + h_ref[...]
        sb_ref[...] = ssum.astype(BF16)

        @pl.when(q == sc_ref[1])
        def _():
            own_ref[...] = ssum

    grid_spec = pltpu.PrefetchScalarGridSpec(
        num_scalar_prefetch=1, grid=(r // tr, 4),
        in_specs=[pl.BlockSpec((None, None, tr, n), lambda i, q, sc: (q, sc[0], i, 0)),
                  pl.BlockSpec((None, tr, n), lambda i, q, sc: (q, i, 0))],
        out_specs=[pl.BlockSpec((None, tr, n), lambda i, q, sc: (q, i, 0)),
                   pl.BlockSpec((tr, n), lambda i, q, sc: (i, 0))])
    return pl.pallas_call(
        body, name=name, grid_spec=grid_spec,
        out_shape=[jax.ShapeDtypeStruct((4, r, n), BF16), jax.ShapeDtypeStruct((r, n), F32)],
        compiler_params=_params(("parallel", "arbitrary")),
    )(core_chip, g4, h4)


def _sum4(own, parts, tr, name):
    r, n = own.shape

    def body(o_ref, p_ref, out_ref):
        acc = o_ref[...]
        for k in range(3):
            acc = acc + p_ref[k].astype(F32)
        out_ref[...] = acc

    return pl.pallas_call(
        body, name=name, grid=(r // tr,),
        in_specs=[pl.BlockSpec((tr, n), lambda i: (i, 0)), pl.BlockSpec((3, tr, n), lambda i: (0, i, 0))],
        out_specs=pl.BlockSpec((tr, n), lambda i: (i, 0)),
        out_shape=jax.ShapeDtypeStruct((r, n), F32),
        compiler_params=_params(("parallel",)),
    )(own, parts)


def _sum8(parts, tr, name):
    _, rows, n = parts.shape

    def body(p_ref, o_ref):
        acc = p_ref[0]
        for k in range(1, N_DEV):
            acc = acc + p_ref[k]
        o_ref[...] = acc

    return pl.pallas_call(
        body, name=name, grid=(rows // tr,),
        in_specs=[pl.BlockSpec((N_DEV, tr, n), lambda i: (0, i, 0))],
        out_specs=pl.BlockSpec((tr, n), lambda i: (i, 0)),
        out_shape=jax.ShapeDtypeStruct((rows, n), F32),
        compiler_params=_params(("parallel",)),
    )(parts)


def _adam_rows(w, g, m, v, tr, name):
    rows, n = w.shape

    def body(w_ref, g_ref, m_ref, v_ref, d_ref, nm_ref, nv_ref):
        d_ref[...], nm_ref[...], nv_ref[...] = _adamw(w_ref[...], g_ref[...], m_ref[...], v_ref[...])

    tile = pl.BlockSpec((tr, n), lambda i: (i, 0))
    return pl.pallas_call(
        body, name=name, grid=(rows // tr,),
        in_specs=[tile] * 4, out_specs=[tile] * 3,
        out_shape=[jax.ShapeDtypeStruct((rows, n), F32)] * 3,
        compiler_params=_params(("parallel",)),
    )(w, g, m, v)


def _ada_bwd_adam(cact_t, dmod_cols, w, m, v, tr):
    rows, n = w.shape

    def body(c_ref, d_ref, w_ref, m_ref, v_ref, g_ref, dl_ref, nm_ref, nv_ref):
        def term(b):
            return c_ref[b].astype(BF16).astype(F32) * d_ref[b:b + 1, :].astype(BF16).astype(F32)

        g = term(0)
        for b in range(1, N_DEV):
            g = g + term(b)
        g_ref[...] = g
        dl_ref[...], nm_ref[...], nv_ref[...] = _adamw(w_ref[...], g, m_ref[...], v_ref[...])

    tile = pl.BlockSpec((tr, n), lambda i: (i, 0))
    return pl.pallas_call(
        body, name="ada_bwd_adam", grid=(rows // tr,),
        in_specs=[pl.BlockSpec((N_DEV, tr, 1), lambda i: (0, i, 0)), _full(dmod_cols.shape), tile, tile, tile],
        out_specs=[tile] * 4,
        out_shape=[jax.ShapeDtypeStruct((rows, n), F32)] * 4,
        compiler_params=_params(("parallel",)),
    )(cact_t, dmod_cols, w, m, v)


def _adam_small(ws, gs, ms, vs):
    n = len(ws)

    def body(*refs):
        w_r, g_r, m_r, v_r = refs[:n], refs[n:2 * n], refs[2 * n:3 * n], refs[3 * n:4 * n]
        d_r, nm_r, nv_r = refs[4 * n:5 * n], refs[5 * n:6 * n], refs[6 * n:7 * n]
        for k in range(n):
            d_r[k][...], nm_r[k][...], nv_r[k][...] = _adamw(w_r[k][...], g_r[k][...], m_r[k][...], v_r[k][...])

    shapes = [jax.ShapeDtypeStruct(w.shape, F32) for w in ws]
    outs = pl.pallas_call(
        body, name="adam_small", out_shape=shapes * 3, compiler_params=_params(),
    )(*ws, *gs, *ms, *vs)
    return outs[:n], outs[n:2 * n], outs[2 * n:]


def _block_diag(w):
    h, hd, _ = w.shape
    per = LANES // hd
    eye = jnp.eye(per, dtype=w.dtype)
    w5 = w.reshape(h // per, per, hd, 1, hd) * eye[None, :, None, :, None]
    return w5.reshape(h // per, LANES, LANES)


def _block_diag_grad(g, h, hd):
    per = LANES // hd
    g5 = g.reshape(h // per, per, hd, per, hd)
    return jnp.stack([g5[:, a, :, a, :] for a in range(per)], axis=1).reshape(h, hd, hd)


def kernel(x, c, w_ada, b_ada, g_mix, w_in, conv_w_sc, conv_w_lru, conv_b_lru, w_rg_a, b_rg_a, w_rg_x, b_rg_x, lru_lambda, w_out, g_mlp, w_up, w_down, g_final, loss_target, m_w_ada, m_b_ada, m_g_mix, m_w_in, m_conv_w_sc, m_conv_w_lru, m_conv_b_lru, m_w_rg_a, m_b_rg_a, m_w_rg_x, m_b_rg_x, m_lru_lambda, m_w_out, m_g_mlp, m_w_up, m_w_down, m_g_final, v_w_ada, v_b_ada, v_g_mix, v_w_in, v_conv_w_sc, v_conv_w_lru, v_conv_b_lru, v_w_rg_a, v_b_rg_a, v_w_rg_x, v_b_rg_x, v_lru_lambda, v_w_out, v_g_mlp, v_w_up, v_w_down, v_g_final):
    s, d = x.shape[1], x.shape[2]
    width = conv_b_lru.shape[1]
    heads, hd = w_rg_a.shape[1], w_rg_a.shape[2]
    f = w_down.shape[1] * N_DEV
    n_ada = w_ada.shape[2]
    csh = conv_w_sc.shape[2]
    me = 4 * lax.axis_index("x") + 2 * lax.axis_index("y") + lax.axis_index("c")
    tm = min(512, s)
    tm_mlp = min(1024, s)
    tk = 512

    x2d = x[0]
    tgt = loss_target[0]

    pay = jnp.zeros((SUBLANES, d), F32)
    pay = pay.at[0:1, :].set(c)
    pay = pay.at[1:4, 0:csh].set(conv_w_sc[0])
    pay = pay.at[4:8, 0:csh].set(conv_w_lru[0])
    w_in_t_sh = w_in[0].T.astype(BF16)
    w_up_t_sh = w_up[0].T.astype(BF16)
    w_out_sh = w_out[0].astype(BF16)
    w_down_sh = w_down[0].astype(BF16)
    pay_all, w_in_t = _gather2("gather_in", [pay, w_in_t_sh])
    pay_all, w_up_t_sh, w_down_sh = lax.optimization_barrier((pay_all, w_up_t_sh, w_down_sh))
    w_up_g, w_down_g = _seq_gather2("gather_mlp_weights", 1, [w_up_t_sh, w_down_sh])
    w_in_t = w_in_t.reshape(-1, d)
    c_all = pay_all[:, 0, :]
    conv_sc = pay_all[:, 1:4, 0:csh].transpose(1, 0, 2).reshape(3, width)
    conv_lru = pay_all[:, 4:8, 0:csh].transpose(1, 0, 2).reshape(4, width)

    b_ada_sh = lax.dynamic_slice(b_ada, (0, me * n_ada), (1, n_ada))
    mod_cols, c_act = _ada_fwd(c_all, w_ada[0], b_ada_sh)
    (mod_rows,) = _exchange("scatter_mod", [], [mod_cols.reshape(N_DEV, 1, n_ada)])
    mod6 = jnp.zeros((SUBLANES, d), F32).at[0:6, :].set(mod_rows.reshape(6, d))

    wa_bd = _block_diag(w_rg_a[0]).astype(BF16)
    wx_bd = _block_diag(w_rg_x[0]).astype(BF16)
    ba = b_rg_a.reshape(1, width)
    bx = b_rg_x.reshape(1, width)
    g_fin = g_final.reshape(1, d)

    (hn1, proj), (w_out_g,) = _mix_in_fwd(
        x2d, mod6, g_mix, w_in_t, tm, rider=_ride_gather_ici([w_out_sh]))
    (ymix, h_all), (w_out_g,) = _mixer_fwd(
        proj, conv_sc, conv_lru, conv_b_lru, wa_bd, wx_bd, ba, bx, lru_lambda, width,
        rider=_ride_gather_d2d([w_out_g]))
    w_out_b = w_out_g.reshape(-1, d)
    (mix, x2, hn2), _ = _mix_out_fwd(ymix, x2d, w_out_b, mod6, g_mlp, tm)
    w_up_t = w_up_g.reshape(-1, d)
    w_down_b = w_down_g.reshape(-1, d)
    z, dx3, dyb, st_fin = _mlp_fwd_loss(hn2, w_up_t, w_down_b, x2, tgt, mod6, g_fin, tm, 4 * tk)

    core_chip = jnp.stack([lax.axis_index("c"), 2 * lax.axis_index("x") + lax.axis_index("y")]).astype(jnp.int32)
    dz, dhn2 = _mlp_bwd_dx(dyb, z, w_down_b, w_up_t, tm, 4 * tk)
    g_down, g_up_t = _mlp_bwd_dw(z, dz, dyb, hn2, tm_mlp, 2 * tk)
    g_up4, g_down4 = g_up_t.reshape(4, 2, -1, d), g_down.reshape(4, 2, -1, d)
    (dx2, dymix, g_out, st_out), (h_up, h_down) = _mix_out_bwd(
        dhn2, x2, dx3, mix, ymix, w_out_b, mod6, g_mlp, tm, rider=_ride_pair_swap([g_up4, g_down4]))
    sb_up, own_up = _pair_sum(g_up4, h_up, core_chip, 256, "pair_sum_w_up")
    sb_down, own_down = _pair_sum(g_down4, h_down, core_chip, 256, "pair_sum_w_down")
    g_out4 = g_out.reshape(4, 2, -1, d)
    (dproj, g_small, g_wa, g_wx), (p_up, p_down, h_out) = _mixer_bwd(
        proj, dymix, h_all, conv_sc, conv_lru, conv_b_lru, wa_bd, wx_bd, ba, bx, lru_lambda, width,
        rider=_merge_riders(_ride_chip_exchange([sb_up, sb_down]), _ride_pair_swap([g_out4])))
    sb_out, own_out = _pair_sum(g_out4, h_out, core_chip, g_out4.shape[2], "pair_sum_w_out")
    (grad_x, st_in), _ = _mix_in_bwd_dx(dproj, x2d, dx2, w_in_t, mod6, g_mix, tm)

    small = jnp.concatenate([
        st_in[0:2], st_out[3:4], st_out[0:2], st_fin[1:2],
        st_in[2:3], st_out[2:3], st_fin[0:1],
        jnp.concatenate([g_small[7:8], g_small[10:11]], axis=1),
        jnp.concatenate([g_small[8:9], g_small[9:10]], axis=1),
        jnp.concatenate([jnp.concatenate([g_small[0:3], jnp.zeros((1, width), F32)], axis=0), g_small[3:7]], axis=1),
        st_fin[2:3],
        _block_diag_grad(g_wa, heads, hd).reshape(-1, d),
        _block_diag_grad(g_wx, heads, hd).reshape(-1, d),
    ], axis=0)

    (g_in_t,), (p_out, small_all) = _mix_in_bwd_dw(
        dproj, hn1, tm_mlp, 512, rider=_merge_riders(_ride_chip_exchange([sb_out]), _ride_gather_direct([small])))
    g_in4 = g_in_t.reshape(4, 2, -1, d)
    (h_in,) = _comm("swap_w_in", _ride_pair_swap([g_in4]))
    sb_in, own_in = _pair_sum(g_in4, h_in, core_chip, g_in4.shape[2], "pair_sum_w_in")
    (p_in,) = _comm("exchange_w_in", _ride_chip_exchange([sb_in]))

    gs_in = _sum4(own_in, p_in, own_in.shape[0], "sum_w_in").T
    gs_up = _sum4(own_up, p_up, 256, "sum_w_up").T
    gs_out = _sum4(own_out, p_out, own_out.shape[0], "sum_w_out")
    gs_down = _sum4(own_down, p_down, 256, "sum_w_down")
    ad_in = _adam_rows(w_in[0], gs_in, m_w_in[0], v_w_in[0], 256, "adam_w_in")
    ad_up = _adam_rows(w_up[0], gs_up, m_w_up[0], v_w_up[0], 256, "adam_w_up")
    ad_out = _adam_rows(w_out[0], gs_out, m_w_out[0], v_w_out[0], w_out.shape[1], "adam_w_out")
    ad_down = _adam_rows(w_down[0], gs_down, m_w_down[0], v_w_down[0], 256, "adam_w_down")

    gsum = _sum8(small_all, SMALL_ROWS, "sum_small")
    loss = (0.5 / d) * jnp.sum(gsum[15])
    dmod_cols = lax.dynamic_slice(small_all[:, 0:6, :].reshape(N_DEV, 6 * d), (0, me * n_ada), (N_DEV, n_ada))
    g_ada, d_ada, nm_ada, nv_ada = _ada_bwd_adam(c_act[:, :, None], dmod_cols, w_ada[0], m_w_ada[0], v_w_ada[0], 256)

    g_conv = lax.dynamic_slice(gsum[11:15, 0:width], (0, me * csh), (4, csh))
    g_conv_l = lax.dynamic_slice(gsum[11:15, width:2 * width], (0, me * csh), (4, csh))
    small_g = [
        gsum[0:6].reshape(1, 6 * d),
        gsum[6:7],
        g_conv[0:3].reshape(1, 3, csh),
        g_conv_l.reshape(1, 4, csh),
        gsum[9:10, 0:width],
        gsum[16:48].reshape(1, heads, hd, hd),
        gsum[10:11, 0:width].reshape(1, heads, hd),
        gsum[48:80].reshape(1, heads, hd, hd),
        gsum[10:11, width:].reshape(1, heads, hd),
        gsum[9:10, width:],
        gsum[7:8],
        gsum[8],
    ]
    small_w = [b_ada, g_mix, conv_w_sc, conv_w_lru, conv_b_lru, w_rg_a, b_rg_a, w_rg_x, b_rg_x, lru_lambda, g_mlp, g_final]
    small_m = [m_b_ada, m_g_mix, m_conv_w_sc, m_conv_w_lru, m_conv_b_lru, m_w_rg_a, m_b_rg_a, m_w_rg_x, m_b_rg_x,
               m_lru_lambda, m_g_mlp, m_g_final]
    small_v = [v_b_ada, v_g_mix, v_conv_w_sc, v_conv_w_lru, v_conv_b_lru, v_w_rg_a, v_b_rg_a, v_w_rg_x, v_b_rg_x,
               v_lru_lambda, v_g_mlp, v_g_final]
    sd, snm, snv = _adam_small(small_w, small_g, small_m, small_v)

    def order(ada, w_in_, w_out_, w_up_, w_down_, sm):
        return [ada[None], sm[0], sm[1], w_in_[None], sm[2], sm[3], sm[4], sm[5], sm[6], sm[7], sm[8], sm[9],
                w_out_[None], sm[10], w_up_[None], w_down_[None], sm[11]]

    grads = order(g_ada, gs_in, gs_out, gs_up, gs_down, small_g)
    deltas = order(d_ada, ad_in[0], ad_out[0], ad_up[0], ad_down[0], sd)
    new_m = order(nm_ada, ad_in[1], ad_out[1], ad_up[1], ad_down[1], snm)
    new_v = order(nv_ada, ad_in[2], ad_out[2], ad_up[2], ad_down[2], snv)
    return (loss, grad_x[None], *grads, *deltas, *new_m, *new_v)
```

```python
import functools

import jax
import jax.numpy as jnp
from jax import lax
from jax.experimental import pallas as pl
from jax.experimental.pallas import tpu as pltpu
from jax.experimental.pallas import tpu_sc as plsc

F32 = jnp.float32
BF16 = jnp.bfloat16
N_DEV = 8
EPS = 1e-6
RG_C = 8.0
GELU_K0 = 0.7978845608028654
GELU_K1 = 0.044715
ADAM_LR = 0.001
ADAM_B1 = 0.9
ADAM_B2 = 0.999
ADAM_EPS = 1e-08
ADAM_WD = 0.01
ADAM_STEP = 10
LANES = 128
SUBLANES = 8
VMEM_LIMIT = 52 * 1024 * 1024
MIX_ROWS = 256
SMALL_ROWS = 80

MESH = pl.DeviceIdType.MESH
ANY = pl.BlockSpec(memory_space=pl.ANY)
NN = ((1,), (0,))
NT = ((1,), (1,))
TN = ((0,), (0,))


def _dot(a, b, dims):
    return lax.dot_general(a, b, (dims, ((), ())), preferred_element_type=F32)


def _params(sem=None):
    return pltpu.CompilerParams(dimension_semantics=sem, vmem_limit_bytes=VMEM_LIMIT)


def _full(shape):
    nd = len(shape)
    return pl.BlockSpec(shape, lambda *_: (0,) * nd)


def _exchange(name, gathers, scatters):
    n_g = len(gathers)
    arrs = list(gathers) + list(scatters)
    n = len(arrs)
    out_shape = [jax.ShapeDtypeStruct((N_DEV,) + a.shape, a.dtype) for a in gathers]
    out_shape += [jax.ShapeDtypeStruct(a.shape, a.dtype) for a in scatters]

    def body(*refs):
        ins, outs = refs[:n], refs[n:2 * n]
        send_sems, recv_sems, local_sems = refs[2 * n:]
        x, y, c = lax.axis_index("x"), lax.axis_index("y"), lax.axis_index("c")
        me = 4 * x + 2 * y + c

        def src(a, dev):
            return ins[a] if a < n_g else ins[a].at[dev]

        def peer_of(k):
            px = 1 - x if (k >> 2) & 1 else x
            py = 1 - y if (k >> 1) & 1 else y
            pc = 1 - c if k & 1 else c
            return (px, py, pc), 4 * px + 2 * py + pc

        local = [pltpu.make_async_copy(src(a, me), outs[a].at[me], local_sems.at[a]) for a in range(n)]
        for cp in local:
            cp.start()
        sends = []
        for k in range(1, N_DEV):
            peer, pidx = peer_of(k)
            for a in range(n):
                cp = pltpu.make_async_remote_copy(
                    src_ref=src(a, pidx), dst_ref=outs[a].at[me],
                    send_sem=send_sems.at[a * (N_DEV - 1) + k - 1], recv_sem=recv_sems.at[a * (N_DEV - 1) + k - 1],
                    device_id=peer, device_id_type=MESH)
                cp.start()
                sends.append(cp)
        for k in range(1, N_DEV):
            peer, pidx = peer_of(k)
            for a in range(n):
                pltpu.make_async_remote_copy(
                    src_ref=src(a, pidx), dst_ref=outs[a].at[pidx],
                    send_sem=send_sems.at[a * (N_DEV - 1) + k - 1], recv_sem=recv_sems.at[a * (N_DEV - 1) + k - 1],
                    device_id=peer, device_id_type=MESH).wait_recv()
        for cp in sends:
            cp.wait_send()
        for cp in local:
            cp.wait()

    return pl.pallas_call(
        body, name=name, out_shape=out_shape,
        in_specs=[ANY] * n, out_specs=[ANY] * n,
        scratch_shapes=[pltpu.SemaphoreType.DMA((n * (N_DEV - 1),)),
                        pltpu.SemaphoreType.DMA((n * (N_DEV - 1),)),
                        pltpu.SemaphoreType.DMA((n,))],
    )(*arrs)


def _gather2(name, arrs):
    n = len(arrs)
    per = 7
    out_shape = [jax.ShapeDtypeStruct((N_DEV,) + a.shape, a.dtype) for a in arrs]

    def body(*refs):
        ins, outs = refs[:n], refs[n:2 * n]
        send_sems, recv_sems, local_sems = refs[2 * n:]
        x, y, c = lax.axis_index("x"), lax.axis_index("y"), lax.axis_index("c")
        sib = (x, y, 1 - c)
        chips = [(1 - x, y), (x, 1 - y), (1 - x, 1 - y)]

        def slot(a, px, py, pc):
            return outs[a].at[4 * px + 2 * py + pc]

        def copy(a, k, block, to, src=None):
            return pltpu.make_async_remote_copy(
                src_ref=slot(a, *block) if src is None else src, dst_ref=slot(a, *block),
                send_sem=send_sems.at[a * per + k], recv_sem=recv_sems.at[a * per + k],
                device_id=to, device_id_type=MESH)

        local = [pltpu.make_async_copy(ins[a], slot(a, x, y, c), local_sems.at[a]) for a in range(n)]
        for cp in local:
            cp.start()
        first = []
        for a in range(n):
            first += [copy(a, 1 + j, (x, y, c), (*chip, c), src=ins[a]) for j, chip in enumerate(chips)]
        for a in range(n):
            first.append(copy(a, 0, (x, y, c), sib, src=ins[a]))
        for cp in first:
            cp.start()
        passed = []
        for a in range(n):
            for j, chip in enumerate(chips):
                copy(a, 1 + j, (*chip, c), (x, y, c)).wait_recv()
                cp = copy(a, 4 + j, (*chip, c), sib)
                cp.start()
                passed.append(cp)
        for a in range(n):
            copy(a, 0, sib, (x, y, c)).wait_recv()
            for j, chip in enumerate(chips):
                copy(a, 4 + j, (*chip, 1 - c), (x, y, c)).wait_recv()
        for cp in first + passed:
            cp.wait_send()
        for cp in local:
            cp.wait()

    return pl.pallas_call(
        body, name=name, out_shape=out_shape,
        in_specs=[ANY] * n, out_specs=[ANY] * n,
        scratch_shapes=[pltpu.SemaphoreType.DMA((n * per,)), pltpu.SemaphoreType.DMA((n * per,)),
                        pltpu.SemaphoreType.DMA((n,))],
    )(*arrs)


def _seq_gather2(name, collective_id, arrs):
    n = len(arrs)
    per = 7

    def body(*refs):
        ins, outs = refs[:n], refs[n:2 * n]
        send_sems, recv_sems, local_sems = refs[2 * n:]
        x, y, c = lax.axis_index("x"), lax.axis_index("y"), lax.axis_index("c")
        sib = (x, y, 1 - c)
        chips = [(1 - x, y), (x, 1 - y), (1 - x, 1 - y)]
        barrier = pltpu.get_barrier_semaphore()
        for peer in [sib] + [(*chip, c) for chip in chips]:
            pl.semaphore_signal(barrier, inc=1, device_id=peer, device_id_type=MESH)
        pl.semaphore_wait(barrier, 4)

        def slot(a, px, py, pc):
            return outs[a].at[4 * px + 2 * py + pc]

        def copy(a, k, block, to, src=None):
            return pltpu.make_async_remote_copy(
                src_ref=slot(a, *block) if src is None else src, dst_ref=slot(a, *block),
                send_sem=send_sems.at[a * per + k], recv_sem=recv_sems.at[a * per + k],
                device_id=to, device_id_type=MESH)

        local = [pltpu.make_async_copy(ins[a], slot(a, x, y, c), local_sems.at[a]) for a in range(n)]
        for cp in local:
            cp.start()
        first = []
        for a in range(n):
            first += [copy(a, 1 + j, (x, y, c), (*chip, c), src=ins[a]) for j, chip in enumerate(chips)]
        for a in range(n):
            first.append(copy(a, 0, (x, y, c), sib, src=ins[a]))
        for cp in first:
            cp.start()
        passed = []
        for a in range(n):
            for j, chip in enumerate(chips):
                copy(a, 1 + j, (*chip, c), (x, y, c)).wait_recv()
                cp = copy(a, 4 + j, (*chip, c), sib)
                cp.start()
                passed.append(cp)
        for a in range(n):
            copy(a, 0, sib, (x, y, c)).wait_recv()
            for j, chip in enumerate(chips):
                copy(a, 4 + j, (*chip, 1 - c), (x, y, c)).wait_recv()
        for cp in first + passed:
            cp.wait_send()
        for cp in local:
            cp.wait()

    return pl.kernel(
        body, out_type=[jax.ShapeDtypeStruct((N_DEV,) + a.shape, a.dtype) for a in arrs],
        mesh=plsc.ScalarSubcoreMesh(axis_name="seq", num_cores=1),
        scratch_types=[pltpu.SemaphoreType.DMA((n * per,)), pltpu.SemaphoreType.DMA((n * per,)),
                       pltpu.SemaphoreType.DMA((n,))],
        compiler_params=pltpu.CompilerParams(collective_id=collective_id), name=name,
    )(*arrs)


def _pair_swap(name, arrs):
    n = len(arrs)
    out_shape = [jax.ShapeDtypeStruct((4,) + a.shape[2:], a.dtype) for a in arrs]

    def body(*refs):
        ins, outs = refs[:n], refs[n:2 * n]
        send_sems, recv_sems = refs[2 * n:]
        x, y, c = lax.axis_index("x"), lax.axis_index("y"), lax.axis_index("c")

        def copy(a, q):
            return pltpu.make_async_remote_copy(
                src_ref=ins[a].at[q, 1 - c], dst_ref=outs[a].at[q],
                send_sem=send_sems.at[a * 4 + q], recv_sem=recv_sems.at[a * 4 + q],
                device_id=(x, y, 1 - c), device_id_type=MESH)

        cps = [copy(a, q) for a in range(n) for q in range(4)]
        for cp in cps:
            cp.start()
        for cp in cps:
            cp.wait_recv()
        for cp in cps:
            cp.wait_send()

    return pl.pallas_call(
        body, name=name, out_shape=out_shape,
        in_specs=[ANY] * n, out_specs=[ANY] * n,
        scratch_shapes=[pltpu.SemaphoreType.DMA((n * 4,)), pltpu.SemaphoreType.DMA((n * 4,))],
    )(*arrs)


def _chip_exchange(name, arrs):
    n = len(arrs)
    out_shape = [jax.ShapeDtypeStruct((3,) + a.shape[1:], a.dtype) for a in arrs]

    def body(*refs):
        ins, outs = refs[:n], refs[n:2 * n]
        send_sems, recv_sems = refs[2 * n:]
        x, y, c = lax.axis_index("x"), lax.axis_index("y"), lax.axis_index("c")

        def copy(a, k):
            px = 1 - x if (k >> 1) & 1 else x
            py = 1 - y if k & 1 else y
            return pltpu.make_async_remote_copy(
                src_ref=ins[a].at[2 * px + py], dst_ref=outs[a].at[k - 1],
                send_sem=send_sems.at[a * 3 + k - 1], recv_sem=recv_sems.at[a * 3 + k - 1],
                device_id=(px, py, c), device_id_type=MESH)

        cps = [copy(a, k) for a in range(n) for k in (1, 2, 3)]
        for cp in cps:
            cp.start()
        for cp in cps:
            cp.wait_recv()
        for cp in cps:
            cp.wait_send()

    return pl.pallas_call(
        body, name=name, out_shape=out_shape,
        in_specs=[ANY] * n, out_specs=[ANY] * n,
        scratch_shapes=[pltpu.SemaphoreType.DMA((n * 3,)), pltpu.SemaphoreType.DMA((n * 3,))],
    )(*arrs)


class _Rider:
    def __init__(self, arrays, out_shapes, n_sems, build, aliases=None):
        self.arrays, self.out_shapes, self.n_sems, self.build = list(arrays), list(out_shapes), n_sems, build
        self.aliases = dict(aliases or {})


def _merge_riders(r1, r2):
    n1i, n1o, n1s = len(r1.arrays), len(r1.out_shapes), r1.n_sems

    def build(ins, outs, send_sems, recv_sems):
        a = r1.build(ins[:n1i], outs[:n1o], send_sems.at[pl.ds(0, n1s)], recv_sems.at[pl.ds(0, n1s)])
        b = r2.build(ins[n1i:], outs[n1o:], send_sems.at[pl.ds(n1s, r2.n_sems)], recv_sems.at[pl.ds(n1s, r2.n_sems)])
        return tuple(p + q for p, q in zip(a, b))

    aliases = dict(r1.aliases)
    aliases.update({k + n1i: v + n1o for k, v in r2.aliases.items()})
    return _Rider(r1.arrays + r2.arrays, r1.out_shapes + r2.out_shapes, n1s + r2.n_sems, build, aliases)


def _place():
    x, y, c = lax.axis_index("x"), lax.axis_index("y"), lax.axis_index("c")
    chips = [(1 - x, y), (x, 1 - y), (1 - x, 1 - y)]
    return x, y, c, chips


def _ride_gather_ici(arrs):
    n = len(arrs)

    def build(ins, outs, send_sems, recv_sems):
        x, y, c, chips = _place()
        peers = [(*chip, c) for chip in chips] + [(x, y, 1 - c)]
        me = 4 * x + 2 * y + c
        local = [pltpu.make_async_copy(ins[a], outs[a].at[me], send_sems.at[a * 5 + 4]) for a in range(n)]
        sends, recvs = [], []
        for a in range(n):
            for j, (px, py, pc) in enumerate(peers):
                sends.append(pltpu.make_async_remote_copy(
                    src_ref=ins[a], dst_ref=outs[a].at[me], send_sem=send_sems.at[a * 5 + j],
                    recv_sem=recv_sems.at[a * 5 + j], device_id=(px, py, pc), device_id_type=MESH))
                recvs.append(pltpu.make_async_remote_copy(
                    src_ref=ins[a], dst_ref=outs[a].at[4 * px + 2 * py + pc], send_sem=send_sems.at[a * 5 + j],
                    recv_sem=recv_sems.at[a * 5 + j], device_id=(px, py, pc), device_id_type=MESH))
        return local, sends, recvs

    shapes = [jax.ShapeDtypeStruct((N_DEV,) + a.shape, a.dtype) for a in arrs]
    return _Rider(arrs, shapes, n * 5, build)


def _ride_gather_direct(arrs):
    n = len(arrs)

    def build(ins, outs, send_sems, recv_sems):
        x, y, c, _ = _place()
        me = 4 * x + 2 * y + c
        local = [pltpu.make_async_copy(ins[a], outs[a].at[me], send_sems.at[a * N_DEV + 7]) for a in range(n)]
        sends, recvs = [], []
        for a in range(n):
            for k in range(1, N_DEV):
                px = 1 - x if (k >> 2) & 1 else x
                py = 1 - y if (k >> 1) & 1 else y
                pc = 1 - c if k & 1 else c
                sem = a * N_DEV + k - 1
                sends.append(pltpu.make_async_remote_copy(
                    src_ref=ins[a], dst_ref=outs[a].at[me], send_sem=send_sems.at[sem], recv_sem=recv_sems.at[sem],
                    device_id=(px, py, pc), device_id_type=MESH))
                recvs.append(pltpu.make_async_remote_copy(
                    src_ref=ins[a], dst_ref=outs[a].at[4 * px + 2 * py + pc], send_sem=send_sems.at[sem],
                    recv_sem=recv_sems.at[sem], device_id=(px, py, pc), device_id_type=MESH))
        return local, sends, recvs

    shapes = [jax.ShapeDtypeStruct((N_DEV,) + a.shape, a.dtype) for a in arrs]
    return _Rider(arrs, shapes, n * N_DEV, build)


def _ride_gather_d2d(gathered):
    n = len(gathered)

    def build(ins, outs, send_sems, recv_sems):
        x, y, c, chips = _place()
        sends, recvs = [], []
        for a in range(n):
            for j, (px, py) in enumerate(chips):
                mine = outs[a].at[4 * px + 2 * py + c]
                theirs = outs[a].at[4 * px + 2 * py + 1 - c]
                sends.append(pltpu.make_async_remote_copy(
                    src_ref=mine, dst_ref=mine, send_sem=send_sems.at[a * 3 + j], recv_sem=recv_sems.at[a * 3 + j],
                    device_id=(x, y, 1 - c), device_id_type=MESH))
                recvs.append(pltpu.make_async_remote_copy(
                    src_ref=mine, dst_ref=theirs, send_sem=send_sems.at[a * 3 + j], recv_sem=recv_sems.at[a * 3 + j],
                    device_id=(x, y, 1 - c), device_id_type=MESH))
        return [], sends, recvs

    shapes = [jax.ShapeDtypeStruct(a.shape, a.dtype) for a in gathered]
    return _Rider(gathered, shapes, n * 3, build, aliases={a: a for a in range(n)})


def _ride_pair_swap(arrs):
    n = len(arrs)

    def build(ins, outs, send_sems, recv_sems):
        x, y, c, _ = _place()
        cps = [pltpu.make_async_remote_copy(
            src_ref=ins[a].at[q, 1 - c], dst_ref=outs[a].at[q], send_sem=send_sems.at[a * 4 + q],
            recv_sem=recv_sems.at[a * 4 + q], device_id=(x, y, 1 - c), device_id_type=MESH)
            for a in range(n) for q in range(4)]
        return [], cps, cps

    shapes = [jax.ShapeDtypeStruct((4,) + a.shape[2:], a.dtype) for a in arrs]
    return _Rider(arrs, shapes, n * 4, build)


def _ride_chip_exchange(arrs):
    n = len(arrs)

    def build(ins, outs, send_sems, recv_sems):
        x, y, c, _ = _place()
        cps = []
        for a in range(n):
            for k in (1, 2, 3):
                px = 1 - x if (k >> 1) & 1 else x
                py = 1 - y if k & 1 else y
                cps.append(pltpu.make_async_remote_copy(
                    src_ref=ins[a].at[2 * px + py], dst_ref=outs[a].at[k - 1], send_sem=send_sems.at[a * 3 + k - 1],
                    recv_sem=recv_sems.at[a * 3 + k - 1], device_id=(px, py, c), device_id_type=MESH))
        return [], cps, cps

    shapes = [jax.ShapeDtypeStruct((3,) + a.shape[1:], a.dtype) for a in arrs]
    return _Rider(arrs, shapes, n * 3, build)


def _call(body, name, grid, in_specs, out_specs, out_shape, args, scratch=(), rider=None):
    n_in, n_out, n_scr = len(in_specs), len(out_specs), len(scratch)
    sem = ("arbitrary",) * len(grid)
    if rider is None:
        outs = pl.pallas_call(
            body, name=name, grid=grid, in_specs=in_specs, out_specs=out_specs, out_shape=out_shape,
            scratch_shapes=list(scratch), compiler_params=_params(sem))(*args)
        return outs, []
    ri, ro = len(rider.arrays), len(rider.out_shapes)

    def riding(*refs):
        ins, r_ins = refs[:n_in], refs[n_in:n_in + ri]
        outs = refs[n_in + ri:n_in + ri + n_out]
        r_outs = refs[n_in + ri + n_out:n_in + ri + n_out + ro]
        scr = refs[n_in + ri + n_out + ro:n_in + ri + n_out + ro + n_scr]
        send_sems, recv_sems = refs[-2:]
        first = functools.reduce(jnp.logical_and, [pl.program_id(k) == 0 for k in range(len(grid))])
        last = functools.reduce(jnp.logical_and, [pl.program_id(k) == grid[k] - 1 for k in range(len(grid))])

        @pl.when(first)
        def _():
            local, sends, _ = rider.build(r_ins, r_outs, send_sems, recv_sems)
            for cp in local + sends:
                cp.start()

        body(*ins, *outs, *scr)

        @pl.when(last)
        def _():
            local, sends, recvs = rider.build(r_ins, r_outs, send_sems, recv_sems)
            for cp in recvs:
                cp.wait_recv()
            for cp in sends:
                cp.wait_send()
            for cp in local:
                cp.wait()

    outs = pl.pallas_call(
        riding, name=name, grid=grid,
        in_specs=list(in_specs) + [ANY] * ri, out_specs=list(out_specs) + [ANY] * ro,
        out_shape=list(out_shape) + rider.out_shapes,
        scratch_shapes=list(scratch) + [pltpu.SemaphoreType.DMA((rider.n_sems,)), pltpu.SemaphoreType.DMA((rider.n_sems,))],
        input_output_aliases={n_in + k: n_out + v for k, v in rider.aliases.items()},
        compiler_params=_params(sem))(*args, *rider.arrays)
    return outs[:n_out], outs[n_out:]


def _comm(name, rider):
    def body(dummy_ref, out_ref):
        out_ref[...] = dummy_ref[...]

    dummy = jnp.zeros((SUBLANES, LANES), F32)
    spec = pl.BlockSpec((SUBLANES, LANES), lambda i: (0, 0))
    _, r_outs = _call(body, name, (1,), [spec], [spec], [jax.ShapeDtypeStruct(dummy.shape, F32)], [dummy], rider=rider)
    return r_outs


def _ada_fwd(c_all, w_ada_sh, b_ada_sh):
    nb, d = c_all.shape
    ncol = w_ada_sh.shape[1]

    def body(c_ref, w_ref, b_ref, mod_ref, cact_ref):
        cc = c_ref[...]
        ca = cc * jax.nn.sigmoid(cc)
        cact_ref[...] = ca
        mod_ref[...] = _dot(ca.astype(BF16), w_ref[...].astype(BF16), NN) + b_ref[...]

    return pl.pallas_call(
        body, name="ada_fwd",
        out_shape=[jax.ShapeDtypeStruct((nb, ncol), F32), jax.ShapeDtypeStruct((nb, d), F32)],
        compiler_params=_params(),
    )(c_all, w_ada_sh, b_ada_sh)


def _rms(xv):
    rstd = lax.rsqrt(jnp.mean(xv * xv, axis=-1, keepdims=True) + EPS)
    return xv * rstd, rstd


def _rms_bwd(dxhat, xhat, rstd):
    return rstd * (dxhat - xhat * jnp.mean(dxhat * xhat, axis=-1, keepdims=True))


def _colsum(v):
    return jnp.sum(v, axis=0, keepdims=True)


def _expm1(v):
    series = v * (1.0 + v * (0.5 + v * (1.0 / 6.0 + v * (1.0 / 24.0 + v * (1.0 / 120.0 + v * (1.0 / 720.0))))))
    return jnp.where(jnp.abs(v) < 0.3, series, jnp.exp(v) - 1.0)


def _softplus(v):
    return jnp.maximum(v, 0.0) + jnp.log1p(jnp.exp(-jnp.abs(v)))


def _gelu(v):
    t = jnp.tanh(GELU_K0 * (v + GELU_K1 * v * v * v))
    return 0.5 * v * (1.0 + t), t


def _dgelu(v, t):
    return 0.5 * (1.0 + t) + 0.5 * v * (1.0 - t * t) * GELU_K0 * (1.0 + 3.0 * GELU_K1 * v * v)


def _shift_down(v, k, prev8):
    r = pltpu.roll(v, k, 0)
    pr = pltpu.roll(prev8, k, 0)
    row8 = lax.broadcasted_iota(jnp.int32, prev8.shape, 0)
    top = jnp.where(row8 < k, pr, r[0:SUBLANES])
    return jnp.concatenate([top, r[SUBLANES:]], axis=0)


def _shift_up(v, k, next8):
    t = v.shape[0]
    r = pltpu.roll(v, t - k, 0)
    nr = pltpu.roll(next8, SUBLANES - k, 0)
    row8 = lax.broadcasted_iota(jnp.int32, next8.shape, 0)
    bot = jnp.where(row8 >= SUBLANES - k, nr, r[t - SUBLANES:t])
    return jnp.concatenate([r[:t - SUBLANES], bot], axis=0)


def _scan_fwd(a, b, h0):
    t = a.shape[0]
    row = lax.broadcasted_iota(jnp.int32, a.shape, 0)
    s = 1
    while s < min(t, SUBLANES):
        a_sh = pltpu.roll(a, s, 0)
        b_sh = pltpu.roll(b, s, 0)
        m = row >= s
        b = jnp.where(m, a * b_sh + b, b)
        a = jnp.where(m, a * a_sh, a)
        s *= 2
    while s < t:
        b = jnp.concatenate([b[:s], a[s:] * b[:t - s] + b[s:]], axis=0)
        a = jnp.concatenate([a[:s], a[s:] * a[:t - s]], axis=0)
        s *= 2
    return b + a * h0


def _scan_rev(m, b, g_next):
    t = m.shape[0]
    row = lax.broadcasted_iota(jnp.int32, m.shape, 0)
    s = 1
    while s < min(t, SUBLANES):
        m_sh = pltpu.roll(m, t - s, 0)
        b_sh = pltpu.roll(b, t - s, 0)
        msk = row < t - s
        b = jnp.where(msk, m * b_sh + b, b)
        m = jnp.where(msk, m * m_sh, m)
        s *= 2
    while s < t:
        b = jnp.concatenate([m[:t - s] * b[s:] + b[:t - s], b[t - s:]], axis=0)
        m = jnp.concatenate([m[:t - s] * m[s:], m[t - s:]], axis=0)
        s *= 2
    return b + m * g_next


def _lru_gates(u, wa, wx, ba, bx, sp):
    ub = u.astype(BF16)
    r = jax.nn.sigmoid(_dot(ub, wa, NN) + ba)
    i = jax.nn.sigmoid(_dot(ub, wx, NN) + bx)
    log_a = (-RG_C * r) * sp
    a = jnp.exp(log_a)
    mult = jnp.sqrt(-_expm1(2.0 * log_a))
    return ub, r, i, a, mult


def _conv3(p, pp, w_ref, lo):
    p1 = _shift_down(p, 1, pp)
    p2 = _shift_down(p, 2, pp)
    q = (w_ref[0:1, lo:lo + LANES] * p2 + w_ref[1:2, lo:lo + LANES] * p1) + w_ref[2:3, lo:lo + LANES] * p
    return q, p1, p2


def _conv4(xv, xp, w_ref, b_ref, lo):
    x1 = _shift_down(xv, 1, xp)
    x2 = _shift_down(xv, 2, xp)
    x3 = _shift_down(xv, 3, xp)
    u = (((w_ref[0:1, lo:lo + LANES] * x3 + w_ref[1:2, lo:lo + LANES] * x2) + w_ref[2:3, lo:lo + LANES] * x1)
         + w_ref[3:4, lo:lo + LANES] * xv) + b_ref[:, lo:lo + LANES]
    return u, x1, x2, x3


def _mix_in_fwd(x2d, mod6, g_mix, w_in_t, tm, rider=None):
    s, d = x2d.shape
    din = w_in_t.shape[0]

    def body(x_ref, mod_ref, g_ref, w_ref, hn_ref, proj_ref):
        xhat, _ = _rms(x_ref[...])
        hn = ((xhat * g_ref[...]) * (1.0 + mod_ref[1:2, :]) + mod_ref[0:1, :]).astype(BF16)
        hn_ref[...] = hn
        proj_ref[...] = _dot(hn, w_ref[...], NT)

    return _call(
        body, "mix_in_fwd", (s // tm,),
        [pl.BlockSpec((tm, d), lambda i: (i, 0)), _full(mod6.shape), _full(g_mix.shape), _full(w_in_t.shape)],
        [pl.BlockSpec((tm, d), lambda i: (i, 0)), pl.BlockSpec((tm, din), lambda i: (i, 0))],
        [jax.ShapeDtypeStruct((s, d), BF16), jax.ShapeDtypeStruct((s, din), F32)],
        [x2d, mod6, g_mix, w_in_t], rider=rider)


def _mixer_fwd(proj, conv_sc, conv_lru, conv_b, wa_bd, wx_bd, ba, bx, lam, width, rider=None):
    s, din = proj.shape
    t = min(MIX_ROWS, s)
    nblk = width // LANES
    hb = t // SUBLANES

    def body(proj_ref, projp_ref, wsc_ref, wlru_ref, blru_ref, wa_ref, wx_ref, ba_ref, bx_ref, lam_ref,
             ymix_ref, h_ref, hc_ref):
        i = pl.program_id(0)

        @pl.when(i == 0)
        def _():
            hc_ref[...] = jnp.zeros_like(hc_ref)

        has_prev = i > 0
        for j in range(nblk):
            lo = j * LANES

            def col(p, ref=proj_ref):
                return ref[:, p * width + lo:p * width + lo + LANES]

            def prev(p):
                return jnp.where(has_prev, col(p, projp_ref), 0.0)

            p = col(1) * col(2)
            q, _, _ = _conv3(p, prev(1) * prev(2), wsc_ref, lo)
            ymix_ref[:, lo:lo + LANES] = (col(0) * q).astype(BF16)

            u, _, _, _ = _conv4(col(4), prev(4), wlru_ref, blru_ref, lo)
            sp = _softplus(-lam_ref[:, lo:lo + LANES])
            _, r, ig, a, mult = _lru_gates(u, wa_ref[j], wx_ref[j], ba_ref[:, lo:lo + LANES], bx_ref[:, lo:lo + LANES], sp)
            h = _scan_fwd(a, mult * (ig * u), hc_ref[0:1, lo:lo + LANES])
            h_ref[:, lo:lo + LANES] = h
            hc_ref[0:1, lo:lo + LANES] = h[t - 1:t, :]
            gel, _ = _gelu(col(3))
            ymix_ref[:, width + lo:width + lo + LANES] = (gel * h).astype(BF16)

    small = [conv_sc, conv_lru, conv_b, wa_bd, wx_bd, ba, bx, lam]
    return _call(
        body, "mixer_fwd", (s // t,),
        [pl.BlockSpec((t, din), lambda i: (i, 0)),
         pl.BlockSpec((SUBLANES, din), lambda i: (jnp.maximum(i * hb - 1, 0), 0))]
        + [_full(a.shape) for a in small],
        [pl.BlockSpec((t, 2 * width), lambda i: (i, 0)), pl.BlockSpec((t, width), lambda i: (i, 0))],
        [jax.ShapeDtypeStruct((s, 2 * width), BF16), jax.ShapeDtypeStruct((s, width), F32)],
        [proj, proj, *small], scratch=[pltpu.VMEM((SUBLANES, width), F32)], rider=rider)


def _mix_out_fwd(ymix, x2d, w_out, mod6, g_mlp, tm, rider=None):
    s, d = x2d.shape

    def body(y_ref, x_ref, w_ref, mod_ref, g_ref, mix_ref, x2_ref, hn_ref):
        mix = _dot(y_ref[...], w_ref[...], NN)
        mix_ref[...] = mix
        x2 = x_ref[...] + mod_ref[2:3, :] * mix
        x2_ref[...] = x2
        xhat, _ = _rms(x2)
        hn_ref[...] = ((xhat * g_ref[...]) * (1.0 + mod_ref[4:5, :]) + mod_ref[3:4, :]).astype(BF16)

    tile = pl.BlockSpec((tm, d), lambda i: (i, 0))
    return _call(
        body, "mix_out_fwd", (s // tm,),
        [tile, tile, _full(w_out.shape), _full(mod6.shape), _full(g_mlp.shape)],
        [tile, tile, tile],
        [jax.ShapeDtypeStruct((s, d), F32), jax.ShapeDtypeStruct((s, d), F32), jax.ShapeDtypeStruct((s, d), BF16)],
        [ymix, x2d, w_out, mod6, g_mlp], rider=rider)


def _mlp_fwd_loss(hn2, w_up_t, w_down, x2, target, mod6, g_final, tm, tk):
    s, d = hn2.shape
    f = w_up_t.shape[0]
    nk = f // tk

    def body(hn_ref, wu_ref, wd_ref, x2_ref, t_ref, mod_ref, g_ref, z_ref, dx3_ref, dyb_ref, st_ref, y_ref):
        i, k = pl.program_id(0), pl.program_id(1)

        @pl.when(jnp.logical_and(i == 0, k == 0))
        def _():
            st_ref[...] = jnp.zeros_like(st_ref)

        z = jnp.maximum(_dot(hn_ref[...], wu_ref[...], NT), 0.0)
        z_ref[...] = z.astype(BF16)
        part = _dot((z * z).astype(BF16), wd_ref[...], NN)

        @pl.when(k == 0)
        def _():
            y_ref[...] = part

        @pl.when(k > 0)
        def _():
            y_ref[...] += part

        @pl.when(k == nk - 1)
        def _():
            gate = mod_ref[5:6, :]
            yv = y_ref[...]
            xhat, rstd = _rms(x2_ref[...] + gate * yv)
            diff = xhat * g_ref[...] - t_ref[...]
            dyo = diff * (1.0 / d)
            dx3 = _rms_bwd(dyo * g_ref[...], xhat, rstd)
            dx3_ref[...] = dx3
            dyb_ref[...] = (gate * dx3).astype(BF16)
            st_ref[0:1, :] += _colsum(dyo * xhat)
            st_ref[1:2, :] += _colsum(dx3 * yv)
            st_ref[2:3, :] += _colsum(diff * diff)

    tile = pl.BlockSpec((tm, d), lambda i, k: (i, 0))
    wblk = pl.BlockSpec((tk, d), lambda i, k: (k, 0))
    return pl.pallas_call(
        body, name="mlp_fwd_loss", grid=(s // tm, nk),
        in_specs=[tile, wblk, wblk, tile, tile, _full(mod6.shape), _full(g_final.shape)],
        out_specs=[pl.BlockSpec((tm, tk), lambda i, k: (i, k)), tile, tile, _full((SUBLANES, d))],
        out_shape=[jax.ShapeDtypeStruct((s, f), BF16), jax.ShapeDtypeStruct((s, d), F32),
                   jax.ShapeDtypeStruct((s, d), BF16), jax.ShapeDtypeStruct((SUBLANES, d), F32)],
        scratch_shapes=[pltpu.VMEM((tm, d), F32)],
        compiler_params=_params(("arbitrary", "arbitrary")),
    )(hn2, w_up_t, w_down, x2, target, mod6, g_final)


def _mlp_bwd_dx(dyb, z, w_down, w_up_t, tm, tk):
    s, d = dyb.shape
    f = z.shape[1]

    def body(dy_ref, z_ref, wd_ref, wu_ref, dz_ref, dh_ref):
        k = pl.program_id(1)
        dz = ((2.0 * z_ref[...].astype(F32)) * _dot(dy_ref[...], wd_ref[...], NT)).astype(BF16)
        dz_ref[...] = dz
        part = _dot(dz, wu_ref[...], NN)

        @pl.when(k == 0)
        def _():
            dh_ref[...] = part

        @pl.when(k > 0)
        def _():
            dh_ref[...] += part

    return pl.pallas_call(
        body, name="mlp_bwd_dx", grid=(s // tm, f // tk),
        in_specs=[pl.BlockSpec((tm, d), lambda i, k: (i, 0)), pl.BlockSpec((tm, tk), lambda i, k: (i, k)),
                  pl.BlockSpec((tk, d), lambda i, k: (k, 0)), pl.BlockSpec((tk, d), lambda i, k: (k, 0))],
        out_specs=[pl.BlockSpec((tm, tk), lambda i, k: (i, k)), pl.BlockSpec((tm, d), lambda i, k: (i, 0))],
        out_shape=[jax.ShapeDtypeStruct((s, f), BF16), jax.ShapeDtypeStruct((s, d), F32)],
        compiler_params=_params(("parallel", "arbitrary")),
    )(dyb, z, w_down, w_up_t)


def _mlp_bwd_dw(z, dz, dyb, hn2, tm, tk):
    s, d = dyb.shape
    f = z.shape[1]

    def body(z_ref, dz_ref, dy_ref, hn_ref, gd_ref, gu_ref):
        i = pl.program_id(1)

        @pl.when(i == 0)
        def _():
            gd_ref[...] = jnp.zeros_like(gd_ref)
            gu_ref[...] = jnp.zeros_like(gu_ref)

        zf = z_ref[...].astype(F32)
        gd_ref[...] += _dot((zf * zf).astype(BF16), dy_ref[...], TN)
        gu_ref[...] += _dot(dz_ref[...], hn_ref[...], TN)

    return pl.pallas_call(
        body, name="mlp_bwd_dw", grid=(f // tk, s // tm),
        in_specs=[pl.BlockSpec((tm, tk), lambda k, i: (i, k)), pl.BlockSpec((tm, tk), lambda k, i: (i, k)),
                  pl.BlockSpec((tm, d), lambda k, i: (i, 0)), pl.BlockSpec((tm, d), lambda k, i: (i, 0))],
        out_specs=[pl.BlockSpec((tk, d), lambda k, i: (k, 0)), pl.BlockSpec((tk, d), lambda k, i: (k, 0))],
        out_shape=[jax.ShapeDtypeStruct((f, d), F32), jax.ShapeDtypeStruct((f, d), F32)],
        compiler_params=_params(("parallel", "arbitrary")),
    )(z, dz, dyb, hn2)


def _mix_out_bwd(dhn2, x2, dx3, mix, ymix, w_out, mod6, g_mlp, tm, rider=None):
    s, d = x2.shape

    def body(dh_ref, x2_ref, dx3_ref, mix_ref, y_ref, w_ref, mod_ref, g_ref, dx2_ref, dym_ref, gw_ref, st_ref):
        i = pl.program_id(0)

        @pl.when(i == 0)
        def _():
            st_ref[...] = jnp.zeros_like(st_ref)
            gw_ref[...] = jnp.zeros_like(gw_ref)

        dh = dh_ref[...]
        xhat, rstd = _rms(x2_ref[...])
        dn = dh * (1.0 + mod_ref[4:5, :])
        dx2 = dx3_ref[...] + _rms_bwd(dn * g_ref[...], xhat, rstd)
        dx2_ref[...] = dx2
        st_ref[0:1, :] += _colsum(dh)
        st_ref[1:2, :] += _colsum(dh * (xhat * g_ref[...]))
        st_ref[2:3, :] += _colsum(dn * xhat)
        st_ref[3:4, :] += _colsum(dx2 * mix_ref[...])
        dmix = (mod_ref[2:3, :] * dx2).astype(BF16)
        dym_ref[...] = _dot(dmix, w_ref[...], NT)
        gw_ref[...] += _dot(y_ref[...], dmix, TN)

    tile = pl.BlockSpec((tm, d), lambda i: (i, 0))
    return _call(
        body, "mix_out_bwd", (s // tm,),
        [tile, tile, tile, tile, tile, _full(w_out.shape), _full(mod6.shape), _full(g_mlp.shape)],
        [tile, tile, _full((d, d)), _full((SUBLANES, d))],
        [jax.ShapeDtypeStruct((s, d), F32), jax.ShapeDtypeStruct((s, d), F32),
         jax.ShapeDtypeStruct((d, d), F32), jax.ShapeDtypeStruct((SUBLANES, d), F32)],
        [dhn2, x2, dx3, mix, ymix, w_out, mod6, g_mlp], rider=rider)


def _mixer_bwd(proj, dymix, h_all, conv_sc, conv_lru, conv_b, wa_bd, wx_bd, ba, bx, lam, width, rider=None):
    s, din = proj.shape
    t = min(MIX_ROWS, s)
    nt = s // t
    nblk = width // LANES
    hb = t // SUBLANES
    last8 = s // SUBLANES - 1

    def body(proj_ref, projp_ref, projn_ref, dy_ref, dyn_ref, h_ref, hp_ref,
             wsc_ref, wlru_ref, blru_ref, wa_ref, wx_ref, ba_ref, bx_ref, lam_ref,
             dproj_ref, small_ref, gwa_ref, gwx_ref, an_ref, gn_ref, dun_ref):
        i = pl.program_id(0)

        @pl.when(i == 0)
        def _():
            small_ref[...] = jnp.zeros_like(small_ref)
            gwa_ref[...] = jnp.zeros_like(gwa_ref)
            gwx_ref[...] = jnp.zeros_like(gwx_ref)
            an_ref[...] = jnp.zeros_like(an_ref)
            gn_ref[...] = jnp.zeros_like(gn_ref)
            dun_ref[...] = jnp.zeros_like(dun_ref)

        has_prev = i < nt - 1
        has_next = i > 0
        for j in range(nblk):
            lo = j * LANES
            ls = slice(lo, lo + LANES)

            def col(p, ref=proj_ref):
                return ref[:, p * width + lo:p * width + lo + LANES]

            def prev(p):
                return jnp.where(has_prev, col(p, projp_ref), 0.0)

            def nxt(p):
                return jnp.where(has_next, col(p, projn_ref), 0.0)

            def add_row(r, v):
                small_ref[r:r + 1, ls] += _colsum(v)

            sc_b, sc_c, sc_x = col(0), col(1), col(2)
            p = sc_c * sc_x
            q, p1, p2 = _conv3(p, prev(1) * prev(2), wsc_ref, lo)
            dys = dy_ref[:, ls]
            dproj_ref[:, ls] = (dys * q).astype(BF16)
            dq = dys * sc_b
            dqn = jnp.where(has_next, dyn_ref[:, ls], 0.0) * nxt(0)
            dp = (wsc_ref[2:3, ls] * dq + wsc_ref[1:2, ls] * _shift_up(dq, 1, dqn)) + wsc_ref[0:1, ls] * _shift_up(dq, 2, dqn)
            dproj_ref[:, width + lo:width + lo + LANES] = (dp * sc_x).astype(BF16)
            dproj_ref[:, 2 * width + lo:2 * width + lo + LANES] = (dp * sc_c).astype(BF16)
            add_row(0, dq * p2)
            add_row(1, dq * p1)
            add_row(2, dq * p)

            xv = col(4)
            u, x1, x2, x3 = _conv4(xv, prev(4), wlru_ref, blru_ref, lo)
            lam_v = lam_ref[:, ls]
            sp = _softplus(-lam_v)
            wa, wx = wa_ref[j], wx_ref[j]
            ub, r, ig, a, mult = _lru_gates(u, wa, wx, ba_ref[:, ls], bx_ref[:, ls], sp)
            iu = ig * u
            h = h_ref[:, ls]
            hm1 = _shift_down(h, 1, jnp.where(has_prev, hp_ref[:, ls], 0.0))
            lyv = col(3)
            gel, th = _gelu(lyv)
            dyl = dy_ref[:, width + lo:width + lo + LANES]
            dproj_ref[:, 3 * width + lo:3 * width + lo + LANES] = (dyl * h * _dgelu(lyv, th)).astype(BF16)
            a_next = jnp.broadcast_to(an_ref[0:1, ls], (SUBLANES, LANES))
            g = _scan_rev(_shift_up(a, 1, a_next), dyl * gel, gn_ref[0:1, ls])
            an_ref[0:1, ls] = a[0:1, :]
            gn_ref[0:1, ls] = g[0:1, :]
            da = g * hm1
            dmult = g * iu
            diu = g * mult
            dlog_a = da * a - dmult * ((a * a) / mult)
            dpre_a = (dlog_a * (-RG_C * sp)) * (r * (1.0 - r))
            dpre_x = (diu * u) * (ig * (1.0 - ig))
            dab, dxb = dpre_a.astype(BF16), dpre_x.astype(BF16)
            du = diu * ig + _dot(dab, wa, NT) + _dot(dxb, wx, NT)
            gwa_ref[j] += _dot(ub, dab, TN)
            gwx_ref[j] += _dot(ub, dxb, TN)
            dun = dun_ref[:, ls]
            dun_ref[:, ls] = du[0:SUBLANES, :]
            dlx = (((wlru_ref[3:4, ls] * du + wlru_ref[2:3, ls] * _shift_up(du, 1, dun))
                    + wlru_ref[1:2, ls] * _shift_up(du, 2, dun)) + wlru_ref[0:1, ls] * _shift_up(du, 3, dun))
            dproj_ref[:, 4 * width + lo:4 * width + lo + LANES] = dlx.astype(BF16)
            add_row(3, du * x3)
            add_row(4, du * x2)
            add_row(5, du * x1)
            add_row(6, du * xv)
            add_row(7, du)
            add_row(8, dpre_a)
            add_row(9, dpre_x)
            add_row(10, (dlog_a * (RG_C * r)) * jax.nn.sigmoid(-lam_v))

    small = [conv_sc, conv_lru, conv_b, wa_bd, wx_bd, ba, bx, lam]
    rev = lambda i: nt - 1 - i
    return _call(
        body, "mixer_bwd", (nt,),
        [pl.BlockSpec((t, din), lambda i: (rev(i), 0)),
         pl.BlockSpec((SUBLANES, din), lambda i: (jnp.maximum(rev(i) * hb - 1, 0), 0)),
         pl.BlockSpec((SUBLANES, din), lambda i: (jnp.minimum((rev(i) + 1) * hb, last8), 0)),
         pl.BlockSpec((t, 2 * width), lambda i: (rev(i), 0)),
         pl.BlockSpec((SUBLANES, 2 * width), lambda i: (jnp.minimum((rev(i) + 1) * hb, last8), 0)),
         pl.BlockSpec((t, width), lambda i: (rev(i), 0)),
         pl.BlockSpec((SUBLANES, width), lambda i: (jnp.maximum(rev(i) * hb - 1, 0), 0))]
        + [_full(a.shape) for a in small],
        [pl.BlockSpec((t, din), lambda i: (rev(i), 0)), _full((2 * SUBLANES, width)),
         _full(wa_bd.shape), _full(wx_bd.shape)],
        [jax.ShapeDtypeStruct((s, din), BF16), jax.ShapeDtypeStruct((2 * SUBLANES, width), F32),
         jax.ShapeDtypeStruct(wa_bd.shape, F32), jax.ShapeDtypeStruct(wx_bd.shape, F32)],
        [proj, proj, proj, dymix, dymix, h_all, h_all, *small],
        scratch=[pltpu.VMEM((SUBLANES, width), F32), pltpu.VMEM((SUBLANES, width), F32),
                 pltpu.VMEM((SUBLANES, width), F32)], rider=rider)


def _mix_in_bwd_dx(dproj, x2d, dx2, w_in_t, mod6, g_mix, tm, rider=None):
    s, d = x2d.shape
    din = dproj.shape[1]

    def body(dp_ref, x_ref, dx2_ref, w_ref, mod_ref, g_ref, gx_ref, st_ref):
        i = pl.program_id(0)

        @pl.when(i == 0)
        def _():
            st_ref[...] = jnp.zeros_like(st_ref)

        dh = _dot(dp_ref[...], w_ref[...], NN)
        xhat, rstd = _rms(x_ref[...])
        dn = dh * (1.0 + mod_ref[1:2, :])
        gx_ref[...] = dx2_ref[...] + _rms_bwd(dn * g_ref[...], xhat, rstd)
        st_ref[0:1, :] += _colsum(dh)
        st_ref[1:2, :] += _colsum(dh * (xhat * g_ref[...]))
        st_ref[2:3, :] += _colsum(dn * xhat)

    tile = pl.BlockSpec((tm, d), lambda i: (i, 0))
    return _call(
        body, "mix_in_bwd_dx", (s // tm,),
        [pl.BlockSpec((tm, din), lambda i: (i, 0)), tile, tile, _full(w_in_t.shape), _full(mod6.shape),
         _full(g_mix.shape)],
        [tile, _full((SUBLANES, d))],
        [jax.ShapeDtypeStruct((s, d), F32), jax.ShapeDtypeStruct((SUBLANES, d), F32)],
        [dproj, x2d, dx2, w_in_t, mod6, g_mix], rider=rider)


def _mix_in_bwd_dw(dproj, hn1, tm, tn, rider=None):
    s, d = hn1.shape
    din = dproj.shape[1]

    def body(dp_ref, hn_ref, gw_ref):
        i = pl.program_id(1)

        @pl.when(i == 0)
        def _():
            gw_ref[...] = jnp.zeros_like(gw_ref)

        gw_ref[...] += _dot(dp_ref[...], hn_ref[...], TN)

    return _call(
        body, "mix_in_bwd_dw", (din // tn, s // tm),
        [pl.BlockSpec((tm, tn), lambda p, i: (i, p)), pl.BlockSpec((tm, d), lambda p, i: (i, 0))],
        [pl.BlockSpec((tn, d), lambda p, i: (p, 0))],
        [jax.ShapeDtypeStruct((din, d), F32)],
        [dproj, hn1], rider=rider)


def _adamw(w, g, m, v):
    m = ADAM_B1 * m + (1.0 - ADAM_B1) * g
    v = ADAM_B2 * v + (1.0 - ADAM_B2) * (g * g)
    m_hat = m / (1.0 - ADAM_B1 ** ADAM_STEP)
    v_hat = v / (1.0 - ADAM_B2 ** ADAM_STEP)
    delta = -ADAM_LR * (m_hat / (jnp.sqrt(v_hat) + ADAM_EPS) + ADAM_WD * w)
    return delta, m, v


def _pair_sum(g4, h4, core_chip, tr, name):
    _, _, r, n = g4.shape

    def body(sc_ref, g_ref, h_ref, sb_ref, own_ref):
        q = pl.program_id(1)
        ssum = g_ref[...] + h_ref[...]
        sb_ref[...] = ssum.astype(BF16)

        @pl.when(q == sc_ref[1])
        def _():
            own_ref[...] = ssum

    grid_spec = pltpu.PrefetchScalarGridSpec(
        num_scalar_prefetch=1, grid=(r // tr, 4),
        in_specs=[pl.BlockSpec((None, None, tr, n), lambda i, q, sc: (q, sc[0], i, 0)),
                  pl.BlockSpec((None, tr, n), lambda i, q, sc: (q, i, 0))],
        out_specs=[pl.BlockSpec((None, tr, n), lambda i, q, sc: (q, i, 0)),
                   pl.BlockSpec((tr, n), lambda i, q, sc: (i, 0))])
    return pl.pallas_call(
        body, name=name, grid_spec=grid_spec,
        out_shape=[jax.ShapeDtypeStruct((4, r, n), BF16), jax.ShapeDtypeStruct((r, n), F32)],
        compiler_params=_params(("parallel", "arbitrary")),
    )(core_chip, g4, h4)


def _sum4(own, parts, tr, name):
    r, n = own.shape

    def body(o_ref, p_ref, out_ref):
        acc = o_ref[...]
        for k in range(3):
            acc = acc + p_ref[k].astype(F32)
        out_ref[...] = acc

    return pl.pallas_call(
        body, name=name, grid=(r // tr,),
        in_specs=[pl.BlockSpec((tr, n), lambda i: (i, 0)), pl.BlockSpec((3, tr, n), lambda i: (0, i, 0))],
        out_specs=pl.BlockSpec((tr, n), lambda i: (i, 0)),
        out_shape=jax.ShapeDtypeStruct((r, n), F32),
        compiler_params=_params(("parallel",)),
    )(own, parts)


def _sum8(parts, tr, name):
    _, rows, n = parts.shape

    def body(p_ref, o_ref):
        acc = p_ref[0]
        for k in range(1, N_DEV):
            acc = acc + p_ref[k]
        o_ref[...] = acc

    return pl.pallas_call(
        body, name=name, grid=(rows // tr,),
        in_specs=[pl.BlockSpec((N_DEV, tr, n), lambda i: (0, i, 0))],
        out_specs=pl.BlockSpec((tr, n), lambda i: (i, 0)),
        out_shape=jax.ShapeDtypeStruct((rows, n), F32),
        compiler_params=_params(("parallel",)),
    )(parts)


def _adam_rows(w, g, m, v, tr, name):
    rows, n = w.shape

    def body(w_ref, g_ref, m_ref, v_ref, d_ref, nm_ref, nv_ref):
        d_ref[...], nm_ref[...], nv_ref[...] = _adamw(w_ref[...], g_ref[...], m_ref[...], v_ref[...])

    tile = pl.BlockSpec((tr, n), lambda i: (i, 0))
    return pl.pallas_call(
        body, name=name, grid=(rows // tr,),
        in_specs=[tile] * 4, out_specs=[tile] * 3,
        out_shape=[jax.ShapeDtypeStruct((rows, n), F32)] * 3,
        compiler_params=_params(("parallel",)),
    )(w, g, m, v)


def _ada_bwd_adam(cact_t, dmod_cols, w, m, v, tr):
    rows, n = w.shape

    def body(c_ref, d_ref, w_ref, m_ref, v_ref, g_ref, dl_ref, nm_ref, nv_ref):
        def term(b):
            return c_ref[b].astype(BF16).astype(F32) * d_ref[b:b + 1, :].astype(BF16).astype(F32)

        g = term(0)
        for b in range(1, N_DEV):
            g = g + term(b)
        g_ref[...] = g
        dl_ref[...], nm_ref[...], nv_ref[...] = _adamw(w_ref[...], g, m_ref[...], v_ref[...])

    tile = pl.BlockSpec((tr, n), lambda i: (i, 0))
    return pl.pallas_call(
        body, name="ada_bwd_adam", grid=(rows // tr,),
        in_specs=[pl.BlockSpec((N_DEV, tr, 1), lambda i: (0, i, 0)), _full(dmod_cols.shape), tile, tile, tile],
        out_specs=[tile] * 4,
        out_shape=[jax.ShapeDtypeStruct((rows, n), F32)] * 4,
        compiler_params=_params(("parallel",)),
    )(cact_t, dmod_cols, w, m, v)


def _adam_small(ws, gs, ms, vs):
    n = len(ws)

    def body(*refs):
        w_r, g_r, m_r, v_r = refs[:n], refs[n:2 * n], refs[2 * n:3 * n], refs[3 * n:4 * n]
        d_r, nm_r, nv_r = refs[4 * n:5 * n], refs[5 * n:6 * n], refs[6 * n:7 * n]
        for k in range(n):
            d_r[k][...], nm_r[k][...], nv_r[k][...] = _adamw(w_r[k][...], g_r[k][...], m_r[k][...], v_r[k][...])

    shapes = [jax.ShapeDtypeStruct(w.shape, F32) for w in ws]
    outs = pl.pallas_call(
        body, name="adam_small", out_shape=shapes * 3, compiler_params=_params(),
    )(*ws, *gs, *ms, *vs)
    return outs[:n], outs[n:2 * n], outs[2 * n:]


def _block_diag(w):
    h, hd, _ = w.shape
    per = LANES // hd
    eye = jnp.eye(per, dtype=w.dtype)
    w5 = w.reshape(h // per, per, hd, 1, hd) * eye[None, :, None, :, None]
    return w5.reshape(h // per, LANES, LANES)


def _block_diag_grad(g, h, hd):
    per = LANES // hd
    g5 = g.reshape(h // per, per, hd, per, hd)
    return jnp.stack([g5[:, a, :, a, :] for a in range(per)], axis=1).reshape(h, hd, hd)


def kernel(x, c, w_ada, b_ada, g_mix, w_in, conv_w_sc, conv_w_lru, conv_b_lru, w_rg_a, b_rg_a, w_rg_x, b_rg_x, lru_lambda, w_out, g_mlp, w_up, w_down, g_final, loss_target, m_w_ada, m_b_ada, m_g_mix, m_w_in, m_conv_w_sc, m_conv_w_lru, m_conv_b_lru, m_w_rg_a, m_b_rg_a, m_w_rg_x, m_b_rg_x, m_lru_lambda, m_w_out, m_g_mlp, m_w_up, m_w_down, m_g_final, v_w_ada, v_b_ada, v_g_mix, v_w_in, v_conv_w_sc, v_conv_w_lru, v_conv_b_lru, v_w_rg_a, v_b_rg_a, v_w_rg_x, v_b_rg_x, v_lru_lambda, v_w_out, v_g_mlp, v_w_up, v_w_down, v_g_final):
    s, d = x.shape[1], x.shape[2]
    width = conv_b_lru.shape[1]
    heads, hd = w_rg_a.shape[1], w_rg_a.shape[2]
    f = w_down.shape[1] * N_DEV
    n_ada = w_ada.shape[2]
    csh = conv_w_sc.shape[2]
    me = 4 * lax.axis_index("x") + 2 * lax.axis_index("y") + lax.axis_index("c")
    tm = min(512, s)
    tm_mlp = min(1024, s)
    tk = 512

    x2d = x[0]
    tgt = loss_target[0]

    pay = jnp.zeros((SUBLANES, d), F32)
    pay = pay.at[0:1, :].set(c)
    pay = pay.at[1:4, 0:csh].set(conv_w_sc[0])
    pay = pay.at[4:8, 0:csh].set(conv_w_lru[0])
    w_in_t_sh = w_in[0].T.astype(BF16)
    w_up_t_sh = w_up[0].T.astype(BF16)
    w_out_sh = w_out[0].astype(BF16)
    w_down_sh = w_down[0].astype(BF16)
    pay_all, w_in_t = _gather2("gather_in", [pay, w_in_t_sh])
    w_in_t = w_in_t.reshape(-1, d)
    c_all = pay_all[:, 0, :]
    conv_sc = pay_all[:, 1:4, 0:csh].transpose(1, 0, 2).reshape(3, width)
    conv_lru = pay_all[:, 4:8, 0:csh].transpose(1, 0, 2).reshape(4, width)

    b_ada_sh = lax.dynamic_slice(b_ada, (0, me * n_ada), (1, n_ada))
    mod_cols, c_act = _ada_fwd(c_all, w_ada[0], b_ada_sh)
    (mod_rows,) = _exchange("scatter_mod", [], [mod_cols.reshape(N_DEV, 1, n_ada)])
    mod_rows, w_out_sh, w_up_t_sh, w_down_sh = lax.optimization_barrier((mod_rows, w_out_sh, w_up_t_sh, w_down_sh))
    (w_out_g,) = _seq_gather2("gather_w_out", 1, [w_out_sh])
    w_up_g, w_down_g = _seq_gather2("gather_mlp_weights", 2, [w_up_t_sh, w_down_sh])
    mod6 = jnp.zeros((SUBLANES, d), F32).at[0:6, :].set(mod_rows.reshape(6, d))

    wa_bd = _block_diag(w_rg_a[0]).astype(BF16)
    wx_bd = _block_diag(w_rg_x[0]).astype(BF16)
    ba = b_rg_a.reshape(1, width)
    bx = b_rg_x.reshape(1, width)
    g_fin = g_final.reshape(1, d)

    (hn1, proj), _ = _mix_in_fwd(x2d, mod6, g_mix, w_in_t, tm)
    (ymix, h_all), _ = _mixer_fwd(proj, conv_sc, conv_lru, conv_b_lru, wa_bd, wx_bd, ba, bx, lru_lambda, width)
    w_out_b = w_out_g.reshape(-1, d)
    (mix, x2, hn2), _ = _mix_out_fwd(ymix, x2d, w_out_b, mod6, g_mlp, tm)
    w_up_t = w_up_g.reshape(-1, d)
    w_down_b = w_down_g.reshape(-1, d)
    z, dx3, dyb, st_fin = _mlp_fwd_loss(hn2, w_up_t, w_down_b, x2, tgt, mod6, g_fin, tm, 4 * tk)

    core_chip = jnp.stack([lax.axis_index("c"), 2 * lax.axis_index("x") + lax.axis_index("y")]).astype(jnp.int32)
    dz, dhn2 = _mlp_bwd_dx(dyb, z, w_down_b, w_up_t, tm, 4 * tk)
    g_down, g_up_t = _mlp_bwd_dw(z, dz, dyb, hn2, tm_mlp, 2 * tk)
    g_up4, g_down4 = g_up_t.reshape(4, 2, -1, d), g_down.reshape(4, 2, -1, d)
    (dx2, dymix, g_out, st_out), (h_up, h_down) = _mix_out_bwd(
        dhn2, x2, dx3, mix, ymix, w_out_b, mod6, g_mlp, tm, rider=_ride_pair_swap([g_up4, g_down4]))
    sb_up, own_up = _pair_sum(g_up4, h_up, core_chip, 256, "pair_sum_w_up")
    sb_down, own_down = _pair_sum(g_down4, h_down, core_chip, 256, "pair_sum_w_down")
    g_out4 = g_out.reshape(4, 2, -1, d)
    (dproj, g_small, g_wa, g_wx), (p_up, p_down, h_out) = _mixer_bwd(
        proj, dymix, h_all, conv_sc, conv_lru, conv_b_lru, wa_bd, wx_bd, ba, bx, lru_lambda, width,
        rider=_merge_riders(_ride_chip_exchange([sb_up, sb_down]), _ride_pair_swap([g_out4])))
    sb_out, own_out = _pair_sum(g_out4, h_out, core_chip, g_out4.shape[2], "pair_sum_w_out")
    (grad_x, st_in), _ = _mix_in_bwd_dx(dproj, x2d, dx2, w_in_t, mod6, g_mix, tm)

    small = jnp.concatenate([
        st_in[0:2], st_out[3:4], st_out[0:2], st_fin[1:2],
        st_in[2:3], st_out[2:3], st_fin[0:1],
        jnp.concatenate([g_small[7:8], g_small[10:11]], axis=1),
        jnp.concatenate([g_small[8:9], g_small[9:10]], axis=1),
        jnp.concatenate([jnp.concatenate([g_small[0:3], jnp.zeros((1, width), F32)], axis=0), g_small[3:7]], axis=1),
        st_fin[2:3],
        _block_diag_grad(g_wa, heads, hd).reshape(-1, d),
        _block_diag_grad(g_wx, heads, hd).reshape(-1, d),
    ], axis=0)

    (g_in_t,), (p_out, small_all) = _mix_in_bwd_dw(
        dproj, hn1, tm_mlp, 512, rider=_merge_riders(_ride_chip_exchange([sb_out]), _ride_gather_direct([small])))
    g_in4 = g_in_t.reshape(4, 2, -1, d)
    (h_in,) = _comm("swap_w_in", _ride_pair_swap([g_in4]))
    sb_in, own_in = _pair_sum(g_in4, h_in, core_chip, g_in4.shape[2], "pair_sum_w_in")
    (p_in,) = _comm("exchange_w_in", _ride_chip_exchange([sb_in]))

    gs_in = _sum4(own_in, p_in, own_in.shape[0], "sum_w_in").T
    gs_up = _sum4(own_up, p_up, 256, "sum_w_up").T
    gs_out = _sum4(own_out, p_out, own_out.shape[0], "sum_w_out")
    gs_down = _sum4(own_down, p_down, 256, "sum_w_down")
    ad_in = _adam_rows(w_in[0], gs_in, m_w_in[0], v_w_in[0], 256, "adam_w_in")
    ad_up = _adam_rows(w_up[0], gs_up, m_w_up[0], v_w_up[0], 256, "adam_w_up")
    ad_out = _adam_rows(w_out[0], gs_out, m_w_out[0], v_w_out[0], w_out.shape[1], "adam_w_out")
    ad_down = _adam_rows(w_down[0], gs_down, m_w_down[0], v_w_down[0], 256, "adam_w_down")

    gsum = _sum8(small_all, SMALL_ROWS, "sum_small")
    loss = (0.5 / d) * jnp.sum(gsum[15])
    dmod_cols = lax.dynamic_slice(small_all[:, 0:6, :].reshape(N_DEV, 6 * d), (0, me * n_ada), (N_DEV, n_ada))
    g_ada, d_ada, nm_ada, nv_ada = _ada_bwd_adam(c_act[:, :, None], dmod_cols, w_ada[0], m_w_ada[0], v_w_ada[0], 256)

    g_conv = lax.dynamic_slice(gsum[11:15, 0:width], (0, me * csh), (4, csh))
    g_conv_l = lax.dynamic_slice(gsum[11:15, width:2 * width], (0, me * csh), (4, csh))
    small_g = [
        gsum[0:6].reshape(1, 6 * d),
        gsum[6:7],
        g_conv[0:3].reshape(1, 3, csh),
        g_conv_l.reshape(1, 4, csh),
        gsum[9:10, 0:width],
        gsum[16:48].reshape(1, heads, hd, hd),
        gsum[10:11, 0:width].reshape(1, heads, hd),
        gsum[48:80].reshape(1, heads, hd, hd),
        gsum[10:11, width:].reshape(1, heads, hd),
        gsum[9:10, width:],
        gsum[7:8],
        gsum[8],
    ]
    small_w = [b_ada, g_mix, conv_w_sc, conv_w_lru, conv_b_lru, w_rg_a, b_rg_a, w_rg_x, b_rg_x, lru_lambda, g_mlp, g_final]
    small_m = [m_b_ada, m_g_mix, m_conv_w_sc, m_conv_w_lru, m_conv_b_lru, m_w_rg_a, m_b_rg_a, m_w_rg_x, m_b_rg_x,
               m_lru_lambda, m_g_mlp, m_g_final]
    small_v = [v_b_ada, v_g_mix, v_conv_w_sc, v_conv_w_lru, v_conv_b_lru, v_w_rg_a, v_b_rg_a, v_w_rg_x, v_b_rg_x,
               v_lru_lambda, v_g_mlp, v_g_final]
    sd, snm, snv = _adam_small(small_w, small_g, small_m, small_v)

    def order(ada, w_in_, w_out_, w_up_, w_down_, sm):
        return [ada[None], sm[0], sm[1], w_in_[None], sm[2], sm[3], sm[4], sm[5], sm[6], sm[7], sm[8], sm[9],
                w_out_[None], sm[10], w_up_[None], w_down_[None], sm[11]]

    grads = order(g_ada, gs_in, gs_out, gs_up, gs_down, small_g)
    deltas = order(d_ada, ad_in[0], ad_out[0], ad_up[0], ad_down[0], sd)
    new_m = order(nm_ada, ad_in[1], ad_out[1], ad_up[1], ad_down[1], snm)
    new_v = order(nv_ada, ad_in[2], ad_out[2], ad_up[2], ad_down[2], snv)
    return (loss, grad_x[None], *grads, *deltas, *new_m, *new_v)
```

```python
import functools

import jax
import jax.numpy as jnp
from jax import lax
from jax.experimental import pallas as pl
from jax.experimental.pallas import tpu as pltpu
from jax.experimental.pallas import tpu_sc as plsc

F32 = jnp.float32
BF16 = jnp.bfloat16
N_DEV = 8
EPS = 1e-6
RG_C = 8.0
GELU_K0 = 0.7978845608028654
GELU_K1 = 0.044715
ADAM_LR = 0.001
ADAM_B1 = 0.9
ADAM_B2 = 0.999
ADAM_EPS = 1e-08
ADAM_WD = 0.01
ADAM_STEP = 10
LANES = 128
SUBLANES = 8
VMEM_LIMIT = 52 * 1024 * 1024
MIX_ROWS = 256
SMALL_ROWS = 80

MESH = pl.DeviceIdType.MESH
ANY = pl.BlockSpec(memory_space=pl.ANY)
NN = ((1,), (0,))
NT = ((1,), (1,))
TN = ((0,), (0,))


def _dot(a, b, dims):
    return lax.dot_general(a, b, (dims, ((), ())), preferred_element_type=F32)


def _params(sem=None):
    return pltpu.CompilerParams(dimension_semantics=sem, vmem_limit_bytes=VMEM_LIMIT)


def _full(shape):
    nd = len(shape)
    return pl.BlockSpec(shape, lambda *_: (0,) * nd)


def _exchange(name, gathers, scatters):
    n_g = len(gathers)
    arrs = list(gathers) + list(scatters)
    n = len(arrs)
    out_shape = [jax.ShapeDtypeStruct((N_DEV,) + a.shape, a.dtype) for a in gathers]
    out_shape += [jax.ShapeDtypeStruct(a.shape, a.dtype) for a in scatters]

    def body(*refs):
        ins, outs = refs[:n], refs[n:2 * n]
        send_sems, recv_sems, local_sems = refs[2 * n:]
        x, y, c = lax.axis_index("x"), lax.axis_index("y"), lax.axis_index("c")
        me = 4 * x + 2 * y + c

        def src(a, dev):
            return ins[a] if a < n_g else ins[a].at[dev]

        def peer_of(k):
            px = 1 - x if (k >> 2) & 1 else x
            py = 1 - y if (k >> 1) & 1 else y
            pc = 1 - c if k & 1 else c
            return (px, py, pc), 4 * px + 2 * py + pc

        local = [pltpu.make_async_copy(src(a, me), outs[a].at[me], local_sems.at[a]) for a in range(n)]
        for cp in local:
            cp.start()
        sends = []
        for k in range(1, N_DEV):
            peer, pidx = peer_of(k)
            for a in range(n):
                cp = pltpu.make_async_remote_copy(
                    src_ref=src(a, pidx), dst_ref=outs[a].at[me],
                    send_sem=send_sems.at[a * (N_DEV - 1) + k - 1], recv_sem=recv_sems.at[a * (N_DEV - 1) + k - 1],
                    device_id=peer, device_id_type=MESH)
                cp.start()
                sends.append(cp)
        for k in range(1, N_DEV):
            peer, pidx = peer_of(k)
            for a in range(n):
                pltpu.make_async_remote_copy(
                    src_ref=src(a, pidx), dst_ref=outs[a].at[pidx],
                    send_sem=send_sems.at[a * (N_DEV - 1) + k - 1], recv_sem=recv_sems.at[a * (N_DEV - 1) + k - 1],
                    device_id=peer, device_id_type=MESH).wait_recv()
        for cp in sends:
            cp.wait_send()
        for cp in local:
            cp.wait()

    return pl.pallas_call(
        body, name=name, out_shape=out_shape,
        in_specs=[ANY] * n, out_specs=[ANY] * n,
        scratch_shapes=[pltpu.SemaphoreType.DMA((n * (N_DEV - 1),)),
                        pltpu.SemaphoreType.DMA((n * (N_DEV - 1),)),
                        pltpu.SemaphoreType.DMA((n,))],
    )(*arrs)


def _gather2(name, arrs):
    n = len(arrs)
    per = 7
    out_shape = [jax.ShapeDtypeStruct((N_DEV,) + a.shape, a.dtype) for a in arrs]

    def body(*refs):
        ins, outs = refs[:n], refs[n:2 * n]
        send_sems, recv_sems, local_sems = refs[2 * n:]
        x, y, c = lax.axis_index("x"), lax.axis_index("y"), lax.axis_index("c")
        sib = (x, y, 1 - c)
        chips = [(1 - x, y), (x, 1 - y), (1 - x, 1 - y)]

        def slot(a, px, py, pc):
            return outs[a].at[4 * px + 2 * py + pc]

        def copy(a, k, block, to, src=None):
            return pltpu.make_async_remote_copy(
                src_ref=slot(a, *block) if src is None else src, dst_ref=slot(a, *block),
                send_sem=send_sems.at[a * per + k], recv_sem=recv_sems.at[a * per + k],
                device_id=to, device_id_type=MESH)

        local = [pltpu.make_async_copy(ins[a], slot(a, x, y, c), local_sems.at[a]) for a in range(n)]
        for cp in local:
            cp.start()
        first = []
        for a in range(n):
            first += [copy(a, 1 + j, (x, y, c), (*chip, c), src=ins[a]) for j, chip in enumerate(chips)]
        for a in range(n):
            first.append(copy(a, 0, (x, y, c), sib, src=ins[a]))
        for cp in first:
            cp.start()
        passed = []
        for a in range(n):
            for j, chip in enumerate(chips):
                copy(a, 1 + j, (*chip, c), (x, y, c)).wait_recv()
                cp = copy(a, 4 + j, (*chip, c), sib)
                cp.start()
                passed.append(cp)
        for a in range(n):
            copy(a, 0, sib, (x, y, c)).wait_recv()
            for j, chip in enumerate(chips):
                copy(a, 4 + j, (*chip, 1 - c), (x, y, c)).wait_recv()
        for cp in first + passed:
            cp.wait_send()
        for cp in local:
            cp.wait()

    return pl.pallas_call(
        body, name=name, out_shape=out_shape,
        in_specs=[ANY] * n, out_specs=[ANY] * n,
        scratch_shapes=[pltpu.SemaphoreType.DMA((n * per,)), pltpu.SemaphoreType.DMA((n * per,)),
                        pltpu.SemaphoreType.DMA((n,))],
    )(*arrs)


def _seq_gather2(name, collective_id, arrs):
    n = len(arrs)
    per = 7

    def body(*refs):
        ins, outs = refs[:n], refs[n:2 * n]
        send_sems, recv_sems, local_sems = refs[2 * n:]
        x, y, c = lax.axis_index("x"), lax.axis_index("y"), lax.axis_index("c")
        sib = (x, y, 1 - c)
        chips = [(1 - x, y), (x, 1 - y), (1 - x, 1 - y)]
        barrier = pltpu.get_barrier_semaphore()
        for peer in [sib] + [(*chip, c) for chip in chips]:
            pl.semaphore_signal(barrier, inc=1, device_id=peer, device_id_type=MESH)
        pl.semaphore_wait(barrier, 4)

        def slot(a, px, py, pc):
            return outs[a].at[4 * px + 2 * py + pc]

        def copy(a, k, block, to, src=None):
            return pltpu.make_async_remote_copy(
                src_ref=slot(a, *block) if src is None else src, dst_ref=slot(a, *block),
                send_sem=send_sems.at[a * per + k], recv_sem=recv_sems.at[a * per + k],
                device_id=to, device_id_type=MESH)

        local = [pltpu.make_async_copy(ins[a], slot(a, x, y, c), local_sems.at[a]) for a in range(n)]
        for cp in local:
            cp.start()
        first = []
        for a in range(n):
            first += [copy(a, 1 + j, (x, y, c), (*chip, c), src=ins[a]) for j, chip in enumerate(chips)]
        for a in range(n):
            first.append(copy(a, 0, (x, y, c), sib, src=ins[a]))
        for cp in first:
            cp.start()
        passed = []
        for a in range(n):
            for j, chip in enumerate(chips):
                copy(a, 1 + j, (*chip, c), (x, y, c)).wait_recv()
                cp = copy(a, 4 + j, (*chip, c), sib)
                cp.start()
                passed.append(cp)
        for a in range(n):
            copy(a, 0, sib, (x, y, c)).wait_recv()
            for j, chip in enumerate(chips):
                copy(a, 4 + j, (*chip, 1 - c), (x, y, c)).wait_recv()
        for cp in first + passed:
            cp.wait_send()
        for cp in local:
            cp.wait()

    return pl.kernel(
        body, out_type=[jax.ShapeDtypeStruct((N_DEV,) + a.shape, a.dtype) for a in arrs],
        mesh=plsc.ScalarSubcoreMesh(axis_name="seq", num_cores=1),
        scratch_types=[pltpu.SemaphoreType.DMA((n * per,)), pltpu.SemaphoreType.DMA((n * per,)),
                       pltpu.SemaphoreType.DMA((n,))],
        compiler_params=pltpu.CompilerParams(collective_id=collective_id), name=name,
    )(*arrs)


def _seq_chip_exchange(name, collective_id, arrs):
    n = len(arrs)

    def body(*refs):
        ins, outs = refs[:n], refs[n:2 * n]
        send_sems, recv_sems = refs[2 * n:]
        x, y, c = lax.axis_index("x"), lax.axis_index("y"), lax.axis_index("c")

        def peer(k):
            return (1 - x if (k >> 1) & 1 else x), (1 - y if k & 1 else y)

        barrier = pltpu.get_barrier_semaphore()
        for k in (1, 2, 3):
            pl.semaphore_signal(barrier, inc=1, device_id=(*peer(k), c), device_id_type=MESH)
        pl.semaphore_wait(barrier, 3)

        def copy(a, k):
            px, py = peer(k)
            return pltpu.make_async_remote_copy(
                src_ref=ins[a].at[2 * px + py], dst_ref=outs[a].at[k - 1],
                send_sem=send_sems.at[a * 3 + k - 1], recv_sem=recv_sems.at[a * 3 + k - 1],
                device_id=(px, py, c), device_id_type=MESH)

        cps = [copy(a, k) for a in range(n) for k in (1, 2, 3)]
        for cp in cps:
            cp.start()
        for cp in cps:
            cp.wait_recv()
        for cp in cps:
            cp.wait_send()

    return pl.kernel(
        body, out_type=[jax.ShapeDtypeStruct((3,) + a.shape[1:], a.dtype) for a in arrs],
        mesh=plsc.ScalarSubcoreMesh(axis_name="seq", num_cores=1),
        scratch_types=[pltpu.SemaphoreType.DMA((n * 3,)), pltpu.SemaphoreType.DMA((n * 3,))],
        compiler_params=pltpu.CompilerParams(collective_id=collective_id), name=name,
    )(*arrs)


def _pair_swap(name, arrs):
    n = len(arrs)
    out_shape = [jax.ShapeDtypeStruct((4,) + a.shape[2:], a.dtype) for a in arrs]

    def body(*refs):
        ins, outs = refs[:n], refs[n:2 * n]
        send_sems, recv_sems = refs[2 * n:]
        x, y, c = lax.axis_index("x"), lax.axis_index("y"), lax.axis_index("c")

        def copy(a, q):
            return pltpu.make_async_remote_copy(
                src_ref=ins[a].at[q, 1 - c], dst_ref=outs[a].at[q],
                send_sem=send_sems.at[a * 4 + q], recv_sem=recv_sems.at[a * 4 + q],
                device_id=(x, y, 1 - c), device_id_type=MESH)

        cps = [copy(a, q) for a in range(n) for q in range(4)]
        for cp in cps:
            cp.start()
        for cp in cps:
            cp.wait_recv()
        for cp in cps:
            cp.wait_send()

    return pl.pallas_call(
        body, name=name, out_shape=out_shape,
        in_specs=[ANY] * n, out_specs=[ANY] * n,
        scratch_shapes=[pltpu.SemaphoreType.DMA((n * 4,)), pltpu.SemaphoreType.DMA((n * 4,))],
    )(*arrs)


def _chip_exchange(name, arrs):
    n = len(arrs)
    out_shape = [jax.ShapeDtypeStruct((3,) + a.shape[1:], a.dtype) for a in arrs]

    def body(*refs):
        ins, outs = refs[:n], refs[n:2 * n]
        send_sems, recv_sems = refs[2 * n:]
        x, y, c = lax.axis_index("x"), lax.axis_index("y"), lax.axis_index("c")

        def copy(a, k):
            px = 1 - x if (k >> 1) & 1 else x
            py = 1 - y if k & 1 else y
            return pltpu.make_async_remote_copy(
                src_ref=ins[a].at[2 * px + py], dst_ref=outs[a].at[k - 1],
                send_sem=send_sems.at[a * 3 + k - 1], recv_sem=recv_sems.at[a * 3 + k - 1],
                device_id=(px, py, c), device_id_type=MESH)

        cps = [copy(a, k) for a in range(n) for k in (1, 2, 3)]
        for cp in cps:
            cp.start()
        for cp in cps:
            cp.wait_recv()
        for cp in cps:
            cp.wait_send()

    return pl.pallas_call(
        body, name=name, out_shape=out_shape,
        in_specs=[ANY] * n, out_specs=[ANY] * n,
        scratch_shapes=[pltpu.SemaphoreType.DMA((n * 3,)), pltpu.SemaphoreType.DMA((n * 3,))],
    )(*arrs)


class _Rider:
    def __init__(self, arrays, out_shapes, n_sems, build, aliases=None):
        self.arrays, self.out_shapes, self.n_sems, self.build = list(arrays), list(out_shapes), n_sems, build
        self.aliases = dict(aliases or {})


def _merge_riders(r1, r2):
    n1i, n1o, n1s = len(r1.arrays), len(r1.out_shapes), r1.n_sems

    def build(ins, outs, send_sems, recv_sems):
        a = r1.build(ins[:n1i], outs[:n1o], send_sems.at[pl.ds(0, n1s)], recv_sems.at[pl.ds(0, n1s)])
        b = r2.build(ins[n1i:], outs[n1o:], send_sems.at[pl.ds(n1s, r2.n_sems)], recv_sems.at[pl.ds(n1s, r2.n_sems)])
        return tuple(p + q for p, q in zip(a, b))

    aliases = dict(r1.aliases)
    aliases.update({k + n1i: v + n1o for k, v in r2.aliases.items()})
    return _Rider(r1.arrays + r2.arrays, r1.out_shapes + r2.out_shapes, n1s + r2.n_sems, build, aliases)


def _place():
    x, y, c = lax.axis_index("x"), lax.axis_index("y"), lax.axis_index("c")
    chips = [(1 - x, y), (x, 1 - y), (1 - x, 1 - y)]
    return x, y, c, chips


def _ride_gather_ici(arrs):
    n = len(arrs)

    def build(ins, outs, send_sems, recv_sems):
        x, y, c, chips = _place()
        peers = [(*chip, c) for chip in chips] + [(x, y, 1 - c)]
        me = 4 * x + 2 * y + c
        local = [pltpu.make_async_copy(ins[a], outs[a].at[me], send_sems.at[a * 5 + 4]) for a in range(n)]
        sends, recvs = [], []
        for a in range(n):
            for j, (px, py, pc) in enumerate(peers):
                sends.append(pltpu.make_async_remote_copy(
                    src_ref=ins[a], dst_ref=outs[a].at[me], send_sem=send_sems.at[a * 5 + j],
                    recv_sem=recv_sems.at[a * 5 + j], device_id=(px, py, pc), device_id_type=MESH))
                recvs.append(pltpu.make_async_remote_copy(
                    src_ref=ins[a], dst_ref=outs[a].at[4 * px + 2 * py + pc], send_sem=send_sems.at[a * 5 + j],
                    recv_sem=recv_sems.at[a * 5 + j], device_id=(px, py, pc), device_id_type=MESH))
        return local, sends, recvs

    shapes = [jax.ShapeDtypeStruct((N_DEV,) + a.shape, a.dtype) for a in arrs]
    return _Rider(arrs, shapes, n * 5, build)


def _ride_gather_direct(arrs):
    n = len(arrs)

    def build(ins, outs, send_sems, recv_sems):
        x, y, c, _ = _place()
        me = 4 * x + 2 * y + c
        local = [pltpu.make_async_copy(ins[a], outs[a].at[me], send_sems.at[a * N_DEV + 7]) for a in range(n)]
        sends, recvs = [], []
        for a in range(n):
            for k in range(1, N_DEV):
                px = 1 - x if (k >> 2) & 1 else x
                py = 1 - y if (k >> 1) & 1 else y
                pc = 1 - c if k & 1 else c
                sem = a * N_DEV + k - 1
                sends.append(pltpu.make_async_remote_copy(
                    src_ref=ins[a], dst_ref=outs[a].at[me], send_sem=send_sems.at[sem], recv_sem=recv_sems.at[sem],
                    device_id=(px, py, pc), device_id_type=MESH))
                recvs.append(pltpu.make_async_remote_copy(
                    src_ref=ins[a], dst_ref=outs[a].at[4 * px + 2 * py + pc], send_sem=send_sems.at[sem],
                    recv_sem=recv_sems.at[sem], device_id=(px, py, pc), device_id_type=MESH))
        return local, sends, recvs

    shapes = [jax.ShapeDtypeStruct((N_DEV,) + a.shape, a.dtype) for a in arrs]
    return _Rider(arrs, shapes, n * N_DEV, build)


def _ride_gather_d2d(gathered):
    n = len(gathered)

    def build(ins, outs, send_sems, recv_sems):
        x, y, c, chips = _place()
        sends, recvs = [], []
        for a in range(n):
            for j, (px, py) in enumerate(chips):
                mine = outs[a].at[4 * px + 2 * py + c]
                theirs = outs[a].at[4 * px + 2 * py + 1 - c]
                sends.append(pltpu.make_async_remote_copy(
                    src_ref=mine, dst_ref=mine, send_sem=send_sems.at[a * 3 + j], recv_sem=recv_sems.at[a * 3 + j],
                    device_id=(x, y, 1 - c), device_id_type=MESH))
                recvs.append(pltpu.make_async_remote_copy(
                    src_ref=mine, dst_ref=theirs, send_sem=send_sems.at[a * 3 + j], recv_sem=recv_sems.at[a * 3 + j],
                    device_id=(x, y, 1 - c), device_id_type=MESH))
        return [], sends, recvs

    shapes = [jax.ShapeDtypeStruct(a.shape, a.dtype) for a in gathered]
    return _Rider(gathered, shapes, n * 3, build, aliases={a: a for a in range(n)})


def _ride_pair_swap(arrs):
    n = len(arrs)

    def build(ins, outs, send_sems, recv_sems):
        x, y, c, _ = _place()
        cps = [pltpu.make_async_remote_copy(
            src_ref=ins[a].at[q, 1 - c], dst_ref=outs[a].at[q], send_sem=send_sems.at[a * 4 + q],
            recv_sem=recv_sems.at[a * 4 + q], device_id=(x, y, 1 - c), device_id_type=MESH)
            for a in range(n) for q in range(4)]
        return [], cps, cps

    shapes = [jax.ShapeDtypeStruct((4,) + a.shape[2:], a.dtype) for a in arrs]
    return _Rider(arrs, shapes, n * 4, build)


def _ride_chip_exchange(arrs):
    n = len(arrs)

    def build(ins, outs, send_sems, recv_sems):
        x, y, c, _ = _place()
        cps = []
        for a in range(n):
            for k in (1, 2, 3):
                px = 1 - x if (k >> 1) & 1 else x
                py = 1 - y if k & 1 else y
                cps.append(pltpu.make_async_remote_copy(
                    src_ref=ins[a].at[2 * px + py], dst_ref=outs[a].at[k - 1], send_sem=send_sems.at[a * 3 + k - 1],
                    recv_sem=recv_sems.at[a * 3 + k - 1], device_id=(px, py, c), device_id_type=MESH))
        return [], cps, cps

    shapes = [jax.ShapeDtypeStruct((3,) + a.shape[1:], a.dtype) for a in arrs]
    return _Rider(arrs, shapes, n * 3, build)


def _call(body, name, grid, in_specs, out_specs, out_shape, args, scratch=(), rider=None):
    n_in, n_out, n_scr = len(in_specs), len(out_specs), len(scratch)
    sem = ("arbitrary",) * len(grid)
    if rider is None:
        outs = pl.pallas_call(
            body, name=name, grid=grid, in_specs=in_specs, out_specs=out_specs, out_shape=out_shape,
            scratch_shapes=list(scratch), compiler_params=_params(sem))(*args)
        return outs, []
    ri, ro = len(rider.arrays), len(rider.out_shapes)

    def riding(*refs):
        ins, r_ins = refs[:n_in], refs[n_in:n_in + ri]
        outs = refs[n_in + ri:n_in + ri + n_out]
        r_outs = refs[n_in + ri + n_out:n_in + ri + n_out + ro]
        scr = refs[n_in + ri + n_out + ro:n_in + ri + n_out + ro + n_scr]
        send_sems, recv_sems = refs[-2:]
        first = functools.reduce(jnp.logical_and, [pl.program_id(k) == 0 for k in range(len(grid))])
        last = functools.reduce(jnp.logical_and, [pl.program_id(k) == grid[k] - 1 for k in range(len(grid))])

        @pl.when(first)
        def _():
            local, sends, _ = rider.build(r_ins, r_outs, send_sems, recv_sems)
            for cp in local + sends:
                cp.start()

        body(*ins, *outs, *scr)

        @pl.when(last)
        def _():
            local, sends, recvs = rider.build(r_ins, r_outs, send_sems, recv_sems)
            for cp in recvs:
                cp.wait_recv()
            for cp in sends:
                cp.wait_send()
            for cp in local:
                cp.wait()

    outs = pl.pallas_call(
        riding, name=name, grid=grid,
        in_specs=list(in_specs) + [ANY] * ri, out_specs=list(out_specs) + [ANY] * ro,
        out_shape=list(out_shape) + rider.out_shapes,
        scratch_shapes=list(scratch) + [pltpu.SemaphoreType.DMA((rider.n_sems,)), pltpu.SemaphoreType.DMA((rider.n_sems,))],
        input_output_aliases={n_in + k: n_out + v for k, v in rider.aliases.items()},
        compiler_params=_params(sem))(*args, *rider.arrays)
    return outs[:n_out], outs[n_out:]


def _comm(name, rider):
    def body(dummy_ref, out_ref):
        out_ref[...] = dummy_ref[...]

    dummy = jnp.zeros((SUBLANES, LANES), F32)
    spec = pl.BlockSpec((SUBLANES, LANES), lambda i: (0, 0))
    _, r_outs = _call(body, name, (1,), [spec], [spec], [jax.ShapeDtypeStruct(dummy.shape, F32)], [dummy], rider=rider)
    return r_outs


def _ada_fwd(c_all, w_ada_sh, b_ada_sh):
    nb, d = c_all.shape
    ncol = w_ada_sh.shape[1]

    def body(c_ref, w_ref, b_ref, mod_ref, cact_ref):
        cc = c_ref[...]
        ca = cc * jax.nn.sigmoid(cc)
        cact_ref[...] = ca
        mod_ref[...] = _dot(ca.astype(BF16), w_ref[...].astype(BF16), NN) + b_ref[...]

    return pl.pallas_call(
        body, name="ada_fwd",
        out_shape=[jax.ShapeDtypeStruct((nb, ncol), F32), jax.ShapeDtypeStruct((nb, d), F32)],
        compiler_params=_params(),
    )(c_all, w_ada_sh, b_ada_sh)


def _rms(xv):
    rstd = lax.rsqrt(jnp.mean(xv * xv, axis=-1, keepdims=True) + EPS)
    return xv * rstd, rstd


def _rms_bwd(dxhat, xhat, rstd):
    return rstd * (dxhat - xhat * jnp.mean(dxhat * xhat, axis=-1, keepdims=True))


def _colsum(v):
    return jnp.sum(v, axis=0, keepdims=True)


def _expm1(v):
    series = v * (1.0 + v * (0.5 + v * (1.0 / 6.0 + v * (1.0 / 24.0 + v * (1.0 / 120.0 + v * (1.0 / 720.0))))))
    return jnp.where(jnp.abs(v) < 0.3, series, jnp.exp(v) - 1.0)


def _softplus(v):
    return jnp.maximum(v, 0.0) + jnp.log1p(jnp.exp(-jnp.abs(v)))


def _gelu(v):
    t = jnp.tanh(GELU_K0 * (v + GELU_K1 * v * v * v))
    return 0.5 * v * (1.0 + t), t


def _dgelu(v, t):
    return 0.5 * (1.0 + t) + 0.5 * v * (1.0 - t * t) * GELU_K0 * (1.0 + 3.0 * GELU_K1 * v * v)


def _shift_down(v, k, prev8):
    r = pltpu.roll(v, k, 0)
    pr = pltpu.roll(prev8, k, 0)
    row8 = lax.broadcasted_iota(jnp.int32, prev8.shape, 0)
    top = jnp.where(row8 < k, pr, r[0:SUBLANES])
    return jnp.concatenate([top, r[SUBLANES:]], axis=0)


def _shift_up(v, k, next8):
    t = v.shape[0]
    r = pltpu.roll(v, t - k, 0)
    nr = pltpu.roll(next8, SUBLANES - k, 0)
    row8 = lax.broadcasted_iota(jnp.int32, next8.shape, 0)
    bot = jnp.where(row8 >= SUBLANES - k, nr, r[t - SUBLANES:t])
    return jnp.concatenate([r[:t - SUBLANES], bot], axis=0)


def _scan_fwd(a, b, h0):
    t = a.shape[0]
    row = lax.broadcasted_iota(jnp.int32, a.shape, 0)
    s = 1
    while s < min(t, SUBLANES):
        a_sh = pltpu.roll(a, s, 0)
        b_sh = pltpu.roll(b, s, 0)
        m = row >= s
        b = jnp.where(m, a * b_sh + b, b)
        a = jnp.where(m, a * a_sh, a)
        s *= 2
    while s < t:
        b = jnp.concatenate([b[:s], a[s:] * b[:t - s] + b[s:]], axis=0)
        a = jnp.concatenate([a[:s], a[s:] * a[:t - s]], axis=0)
        s *= 2
    return b + a * h0


def _scan_rev(m, b, g_next):
    t = m.shape[0]
    row = lax.broadcasted_iota(jnp.int32, m.shape, 0)
    s = 1
    while s < min(t, SUBLANES):
        m_sh = pltpu.roll(m, t - s, 0)
        b_sh = pltpu.roll(b, t - s, 0)
        msk = row < t - s
        b = jnp.where(msk, m * b_sh + b, b)
        m = jnp.where(msk, m * m_sh, m)
        s *= 2
    while s < t:
        b = jnp.concatenate([m[:t - s] * b[s:] + b[:t - s], b[t - s:]], axis=0)
        m = jnp.concatenate([m[:t - s] * m[s:], m[t - s:]], axis=0)
        s *= 2
    return b + m * g_next


def _lru_gates(u, wa, wx, ba, bx, sp):
    ub = u.astype(BF16)
    r = jax.nn.sigmoid(_dot(ub, wa, NN) + ba)
    i = jax.nn.sigmoid(_dot(ub, wx, NN) + bx)
    log_a = (-RG_C * r) * sp
    a = jnp.exp(log_a)
    mult = jnp.sqrt(-_expm1(2.0 * log_a))
    return ub, r, i, a, mult


def _conv3(p, pp, w_ref, lo):
    p1 = _shift_down(p, 1, pp)
    p2 = _shift_down(p, 2, pp)
    q = (w_ref[0:1, lo:lo + LANES] * p2 + w_ref[1:2, lo:lo + LANES] * p1) + w_ref[2:3, lo:lo + LANES] * p
    return q, p1, p2


def _conv4(xv, xp, w_ref, b_ref, lo):
    x1 = _shift_down(xv, 1, xp)
    x2 = _shift_down(xv, 2, xp)
    x3 = _shift_down(xv, 3, xp)
    u = (((w_ref[0:1, lo:lo + LANES] * x3 + w_ref[1:2, lo:lo + LANES] * x2) + w_ref[2:3, lo:lo + LANES] * x1)
         + w_ref[3:4, lo:lo + LANES] * xv) + b_ref[:, lo:lo + LANES]
    return u, x1, x2, x3


def _mix_in_fwd(x2d, mod6, g_mix, w_in_t, tm, rider=None):
    s, d = x2d.shape
    din = w_in_t.shape[0]

    def body(x_ref, mod_ref, g_ref, w_ref, hn_ref, proj_ref):
        xhat, _ = _rms(x_ref[...])
        hn = ((xhat * g_ref[...]) * (1.0 + mod_ref[1:2, :]) + mod_ref[0:1, :]).astype(BF16)
        hn_ref[...] = hn
        proj_ref[...] = _dot(hn, w_ref[...], NT)

    return _call(
        body, "mix_in_fwd", (s // tm,),
        [pl.BlockSpec((tm, d), lambda i: (i, 0)), _full(mod6.shape), _full(g_mix.shape), _full(w_in_t.shape)],
        [pl.BlockSpec((tm, d), lambda i: (i, 0)), pl.BlockSpec((tm, din), lambda i: (i, 0))],
        [jax.ShapeDtypeStruct((s, d), BF16), jax.ShapeDtypeStruct((s, din), F32)],
        [x2d, mod6, g_mix, w_in_t], rider=rider)


def _mixer_fwd(proj, conv_sc, conv_lru, conv_b, wa_bd, wx_bd, ba, bx, lam, width, rider=None):
    s, din = proj.shape
    t = min(MIX_ROWS, s)
    nblk = width // LANES
    hb = t // SUBLANES

    def body(proj_ref, projp_ref, wsc_ref, wlru_ref, blru_ref, wa_ref, wx_ref, ba_ref, bx_ref, lam_ref,
             ymix_ref, h_ref, hc_ref):
        i = pl.program_id(0)

        @pl.when(i == 0)
        def _():
            hc_ref[...] = jnp.zeros_like(hc_ref)

        has_prev = i > 0
        for j in range(nblk):
            lo = j * LANES

            def col(p, ref=proj_ref):
                return ref[:, p * width + lo:p * width + lo + LANES]

            def prev(p):
                return jnp.where(has_prev, col(p, projp_ref), 0.0)

            p = col(1) * col(2)
            q, _, _ = _conv3(p, prev(1) * prev(2), wsc_ref, lo)
            ymix_ref[:, lo:lo + LANES] = (col(0) * q).astype(BF16)

            u, _, _, _ = _conv4(col(4), prev(4), wlru_ref, blru_ref, lo)
            sp = _softplus(-lam_ref[:, lo:lo + LANES])
            _, r, ig, a, mult = _lru_gates(u, wa_ref[j], wx_ref[j], ba_ref[:, lo:lo + LANES], bx_ref[:, lo:lo + LANES], sp)
            h = _scan_fwd(a, mult * (ig * u), hc_ref[0:1, lo:lo + LANES])
            h_ref[:, lo:lo + LANES] = h
            hc_ref[0:1, lo:lo + LANES] = h[t - 1:t, :]
            gel, _ = _gelu(col(3))
            ymix_ref[:, width + lo:width + lo + LANES] = (gel * h).astype(BF16)

    small = [conv_sc, conv_lru, conv_b, wa_bd, wx_bd, ba, bx, lam]
    return _call(
        body, "mixer_fwd", (s // t,),
        [pl.BlockSpec((t, din), lambda i: (i, 0)),
         pl.BlockSpec((SUBLANES, din), lambda i: (jnp.maximum(i * hb - 1, 0), 0))]
        + [_full(a.shape) for a in small],
        [pl.BlockSpec((t, 2 * width), lambda i: (i, 0)), pl.BlockSpec((t, width), lambda i: (i, 0))],
        [jax.ShapeDtypeStruct((s, 2 * width), BF16), jax.ShapeDtypeStruct((s, width), F32)],
        [proj, proj, *small], scratch=[pltpu.VMEM((SUBLANES, width), F32)], rider=rider)


def _mix_out_fwd(ymix, x2d, w_out, mod6, g_mlp, tm, rider=None):
    s, d = x2d.shape

    def body(y_ref, x_ref, w_ref, mod_ref, g_ref, mix_ref, x2_ref, hn_ref):
        mix = _dot(y_ref[...], w_ref[...], NN)
        mix_ref[...] = mix
        x2 = x_ref[...] + mod_ref[2:3, :] * mix
        x2_ref[...] = x2
        xhat, _ = _rms(x2)
        hn_ref[...] = ((xhat * g_ref[...]) * (1.0 + mod_ref[4:5, :]) + mod_ref[3:4, :]).astype(BF16)

    tile = pl.BlockSpec((tm, d), lambda i: (i, 0))
    return _call(
        body, "mix_out_fwd", (s // tm,),
        [tile, tile, _full(w_out.shape), _full(mod6.shape), _full(g_mlp.shape)],
        [tile, tile, tile],
        [jax.ShapeDtypeStruct((s, d), F32), jax.ShapeDtypeStruct((s, d), F32), jax.ShapeDtypeStruct((s, d), BF16)],
        [ymix, x2d, w_out, mod6, g_mlp], rider=rider)


def _mlp_fwd_loss(hn2, w_up_t, w_down, x2, target, mod6, g_final, tm, tk):
    s, d = hn2.shape
    f = w_up_t.shape[0]
    nk = f // tk

    def body(hn_ref, wu_ref, wd_ref, x2_ref, t_ref, mod_ref, g_ref, z_ref, dx3_ref, dyb_ref, st_ref, y_ref):
        i, k = pl.program_id(0), pl.program_id(1)

        @pl.when(jnp.logical_and(i == 0, k == 0))
        def _():
            st_ref[...] = jnp.zeros_like(st_ref)

        z = jnp.maximum(_dot(hn_ref[...], wu_ref[...], NT), 0.0)
        z_ref[...] = z.astype(BF16)
        part = _dot((z * z).astype(BF16), wd_ref[...], NN)

        @pl.when(k == 0)
        def _():
            y_ref[...] = part

        @pl.when(k > 0)
        def _():
            y_ref[...] += part

        @pl.when(k == nk - 1)
        def _():
            gate = mod_ref[5:6, :]
            yv = y_ref[...]
            xhat, rstd = _rms(x2_ref[...] + gate * yv)
            diff = xhat * g_ref[...] - t_ref[...]
            dyo = diff * (1.0 / d)
            dx3 = _rms_bwd(dyo * g_ref[...], xhat, rstd)
            dx3_ref[...] = dx3
            dyb_ref[...] = (gate * dx3).astype(BF16)
            st_ref[0:1, :] += _colsum(dyo * xhat)
            st_ref[1:2, :] += _colsum(dx3 * yv)
            st_ref[2:3, :] += _colsum(diff * diff)

    tile = pl.BlockSpec((tm, d), lambda i, k: (i, 0))
    wblk = pl.BlockSpec((tk, d), lambda i, k: (k, 0))
    return pl.pallas_call(
        body, name="mlp_fwd_loss", grid=(s // tm, nk),
        in_specs=[tile, wblk, wblk, tile, tile, _full(mod6.shape), _full(g_final.shape)],
        out_specs=[pl.BlockSpec((tm, tk), lambda i, k: (i, k)), tile, tile, _full((SUBLANES, d))],
        out_shape=[jax.ShapeDtypeStruct((s, f), BF16), jax.ShapeDtypeStruct((s, d), F32),
                   jax.ShapeDtypeStruct((s, d), BF16), jax.ShapeDtypeStruct((SUBLANES, d), F32)],
        scratch_shapes=[pltpu.VMEM((tm, d), F32)],
        compiler_params=_params(("arbitrary", "arbitrary")),
    )(hn2, w_up_t, w_down, x2, target, mod6, g_final)


def _mlp_bwd_dx(dyb, z, w_down, w_up_t, tm, tk):
    s, d = dyb.shape
    f = z.shape[1]

    def body(dy_ref, z_ref, wd_ref, wu_ref, dz_ref, dh_ref):
        k = pl.program_id(1)
        dz = ((2.0 * z_ref[...].astype(F32)) * _dot(dy_ref[...], wd_ref[...], NT)).astype(BF16)
        dz_ref[...] = dz
        part = _dot(dz, wu_ref[...], NN)

        @pl.when(k == 0)
        def _():
            dh_ref[...] = part

        @pl.when(k > 0)
        def _():
            dh_ref[...] += part

    return pl.pallas_call(
        body, name="mlp_bwd_dx", grid=(s // tm, f // tk),
        in_specs=[pl.BlockSpec((tm, d), lambda i, k: (i, 0)), pl.BlockSpec((tm, tk), lambda i, k: (i, k)),
                  pl.BlockSpec((tk, d), lambda i, k: (k, 0)), pl.BlockSpec((tk, d), lambda i, k: (k, 0))],
        out_specs=[pl.BlockSpec((tm, tk), lambda i, k: (i, k)), pl.BlockSpec((tm, d), lambda i, k: (i, 0))],
        out_shape=[jax.ShapeDtypeStruct((s, f), BF16), jax.ShapeDtypeStruct((s, d), F32)],
        compiler_params=_params(("parallel", "arbitrary")),
    )(dyb, z, w_down, w_up_t)


def _mlp_bwd_dw(z, dz, dyb, hn2, tm, tk):
    s, d = dyb.shape
    f = z.shape[1]

    def body(z_ref, dz_ref, dy_ref, hn_ref, gd_ref, gu_ref):
        i = pl.program_id(1)

        @pl.when(i == 0)
        def _():
            gd_ref[...] = jnp.zeros_like(gd_ref)
            gu_ref[...] = jnp.zeros_like(gu_ref)

        zf = z_ref[...].astype(F32)
        gd_ref[...] += _dot((zf * zf).astype(BF16), dy_ref[...], TN)
        gu_ref[...] += _dot(dz_ref[...], hn_ref[...], TN)

    return pl.pallas_call(
        body, name="mlp_bwd_dw", grid=(f // tk, s // tm),
        in_specs=[pl.BlockSpec((tm, tk), lambda k, i: (i, k)), pl.BlockSpec((tm, tk), lambda k, i: (i, k)),
                  pl.BlockSpec((tm, d), lambda k, i: (i, 0)), pl.BlockSpec((tm, d), lambda k, i: (i, 0))],
        out_specs=[pl.BlockSpec((tk, d), lambda k, i: (k, 0)), pl.BlockSpec((tk, d), lambda k, i: (k, 0))],
        out_shape=[jax.ShapeDtypeStruct((f, d), F32), jax.ShapeDtypeStruct((f, d), F32)],
        compiler_params=_params(("parallel", "arbitrary")),
    )(z, dz, dyb, hn2)


def _mix_out_bwd(dhn2, x2, dx3, mix, ymix, w_out, mod6, g_mlp, tm, rider=None):
    s, d = x2.shape

    def body(dh_ref, x2_ref, dx3_ref, mix_ref, y_ref, w_ref, mod_ref, g_ref, dx2_ref, dym_ref, gw_ref, st_ref):
        i = pl.program_id(0)

        @pl.when(i == 0)
        def _():
            st_ref[...] = jnp.zeros_like(st_ref)
            gw_ref[...] = jnp.zeros_like(gw_ref)

        dh = dh_ref[...]
        xhat, rstd = _rms(x2_ref[...])
        dn = dh * (1.0 + mod_ref[4:5, :])
        dx2 = dx3_ref[...] + _rms_bwd(dn * g_ref[...], xhat, rstd)
        dx2_ref[...] = dx2
        st_ref[0:1, :] += _colsum(dh)
        st_ref[1:2, :] += _colsum(dh * (xhat * g_ref[...]))
        st_ref[2:3, :] += _colsum(dn * xhat)
        st_ref[3:4, :] += _colsum(dx2 * mix_ref[...])
        dmix = (mod_ref[2:3, :] * dx2).astype(BF16)
        dym_ref[...] = _dot(dmix, w_ref[...], NT)
        gw_ref[...] += _dot(y_ref[...], dmix, TN)

    tile = pl.BlockSpec((tm, d), lambda i: (i, 0))
    return _call(
        body, "mix_out_bwd", (s // tm,),
        [tile, tile, tile, tile, tile, _full(w_out.shape), _full(mod6.shape), _full(g_mlp.shape)],
        [tile, tile, _full((d, d)), _full((SUBLANES, d))],
        [jax.ShapeDtypeStruct((s, d), F32), jax.ShapeDtypeStruct((s, d), F32),
         jax.ShapeDtypeStruct((d, d), F32), jax.ShapeDtypeStruct((SUBLANES, d), F32)],
        [dhn2, x2, dx3, mix, ymix, w_out, mod6, g_mlp], rider=rider)


def _mixer_bwd(proj, dymix, h_all, conv_sc, conv_lru, conv_b, wa_bd, wx_bd, ba, bx, lam, width, rider=None):
    s, din = proj.shape
    t = min(MIX_ROWS, s)
    nt = s // t
    nblk = width // LANES
    hb = t // SUBLANES
    last8 = s // SUBLANES - 1

    def body(proj_ref, projp_ref, projn_ref, dy_ref, dyn_ref, h_ref, hp_ref,
             wsc_ref, wlru_ref, blru_ref, wa_ref, wx_ref, ba_ref, bx_ref, lam_ref,
             dproj_ref, small_ref, gwa_ref, gwx_ref, an_ref, gn_ref, dun_ref):
        i = pl.program_id(0)

        @pl.when(i == 0)
        def _():
            small_ref[...] = jnp.zeros_like(small_ref)
            gwa_ref[...] = jnp.zeros_like(gwa_ref)
            gwx_ref[...] = jnp.zeros_like(gwx_ref)
            an_ref[...] = jnp.zeros_like(an_ref)
            gn_ref[...] = jnp.zeros_like(gn_ref)
            dun_ref[...] = jnp.zeros_like(dun_ref)

        has_prev = i < nt - 1
        has_next = i > 0
        for j in range(nblk):
            lo = j * LANES
            ls = slice(lo, lo + LANES)

            def col(p, ref=proj_ref):
                return ref[:, p * width + lo:p * width + lo + LANES]

            def prev(p):
                return jnp.where(has_prev, col(p, projp_ref), 0.0)

            def nxt(p):
                return jnp.where(has_next, col(p, projn_ref), 0.0)

            def add_row(r, v):
                small_ref[r:r + 1, ls] += _colsum(v)

            sc_b, sc_c, sc_x = col(0), col(1), col(2)
            p = sc_c * sc_x
            q, p1, p2 = _conv3(p, prev(1) * prev(2), wsc_ref, lo)
            dys = dy_ref[:, ls]
            dproj_ref[:, ls] = (dys * q).astype(BF16)
            dq = dys * sc_b
            dqn = jnp.where(has_next, dyn_ref[:, ls], 0.0) * nxt(0)
            dp = (wsc_ref[2:3, ls] * dq + wsc_ref[1:2, ls] * _shift_up(dq, 1, dqn)) + wsc_ref[0:1, ls] * _shift_up(dq, 2, dqn)
            dproj_ref[:, width + lo:width + lo + LANES] = (dp * sc_x).astype(BF16)
            dproj_ref[:, 2 * width + lo:2 * width + lo + LANES] = (dp * sc_c).astype(BF16)
            add_row(0, dq * p2)
            add_row(1, dq * p1)
            add_row(2, dq * p)

            xv = col(4)
            u, x1, x2, x3 = _conv4(xv, prev(4), wlru_ref, blru_ref, lo)
            lam_v = lam_ref[:, ls]
            sp = _softplus(-lam_v)
            wa, wx = wa_ref[j], wx_ref[j]
            ub, r, ig, a, mult = _lru_gates(u, wa, wx, ba_ref[:, ls], bx_ref[:, ls], sp)
            iu = ig * u
            h = h_ref[:, ls]
            hm1 = _shift_down(h, 1, jnp.where(has_prev, hp_ref[:, ls], 0.0))
            lyv = col(3)
            gel, th = _gelu(lyv)
            dyl = dy_ref[:, width + lo:width + lo + LANES]
            dproj_ref[:, 3 * width + lo:3 * width + lo + LANES] = (dyl * h * _dgelu(lyv, th)).astype(BF16)
            a_next = jnp.broadcast_to(an_ref[0:1, ls], (SUBLANES, LANES))
            g = _scan_rev(_shift_up(a, 1, a_next), dyl * gel, gn_ref[0:1, ls])
            an_ref[0:1, ls] = a[0:1, :]
            gn_ref[0:1, ls] = g[0:1, :]
            da = g * hm1
            dmult = g * iu
            diu = g * mult
            dlog_a = da * a - dmult * ((a * a) / mult)
            dpre_a = (dlog_a * (-RG_C * sp)) * (r * (1.0 - r))
            dpre_x = (diu * u) * (ig * (1.0 - ig))
            dab, dxb = dpre_a.astype(BF16), dpre_x.astype(BF16)
            du = diu * ig + _dot(dab, wa, NT) + _dot(dxb, wx, NT)
            gwa_ref[j] += _dot(ub, dab, TN)
            gwx_ref[j] += _dot(ub, dxb, TN)
            dun = dun_ref[:, ls]
            dun_ref[:, ls] = du[0:SUBLANES, :]
            dlx = (((wlru_ref[3:4, ls] * du + wlru_ref[2:3, ls] * _shift_up(du, 1, dun))
                    + wlru_ref[1:2, ls] * _shift_up(du, 2, dun)) + wlru_ref[0:1, ls] * _shift_up(du, 3, dun))
            dproj_ref[:, 4 * width + lo:4 * width + lo + LANES] = dlx.astype(BF16)
            add_row(3, du * x3)
            add_row(4, du * x2)
            add_row(5, du * x1)
            add_row(6, du * xv)
            add_row(7, du)
            add_row(8, dpre_a)
            add_row(9, dpre_x)
            add_row(10, (dlog_a * (RG_C * r)) * jax.nn.sigmoid(-lam_v))

    small = [conv_sc, conv_lru, conv_b, wa_bd, wx_bd, ba, bx, lam]
    rev = lambda i: nt - 1 - i
    return _call(
        body, "mixer_bwd", (nt,),
        [pl.BlockSpec((t, din), lambda i: (rev(i), 0)),
         pl.BlockSpec((SUBLANES, din), lambda i: (jnp.maximum(rev(i) * hb - 1, 0), 0)),
         pl.BlockSpec((SUBLANES, din), lambda i: (jnp.minimum((rev(i) + 1) * hb, last8), 0)),
         pl.BlockSpec((t, 2 * width), lambda i: (rev(i), 0)),
         pl.BlockSpec((SUBLANES, 2 * width), lambda i: (jnp.minimum((rev(i) + 1) * hb, last8), 0)),
         pl.BlockSpec((t, width), lambda i: (rev(i), 0)),
         pl.BlockSpec((SUBLANES, width), lambda i: (jnp.maximum(rev(i) * hb - 1, 0), 0))]
        + [_full(a.shape) for a in small],
        [pl.BlockSpec((t, din), lambda i: (rev(i), 0)), _full((2 * SUBLANES, width)),
         _full(wa_bd.shape), _full(wx_bd.shape)],
        [jax.ShapeDtypeStruct((s, din), BF16), jax.ShapeDtypeStruct((2 * SUBLANES, width), F32),
         jax.ShapeDtypeStruct(wa_bd.shape, F32), jax.ShapeDtypeStruct(wx_bd.shape, F32)],
        [proj, proj, proj, dymix, dymix, h_all, h_all, *small],
        scratch=[pltpu.VMEM((SUBLANES, width), F32), pltpu.VMEM((SUBLANES, width), F32),
                 pltpu.VMEM((SUBLANES, width), F32)], rider=rider)


def _mix_in_bwd_dx(dproj, x2d, dx2, w_in_t, mod6, g_mix, tm, rider=None):
    s, d = x2d.shape
    din = dproj.shape[1]

    def body(dp_ref, x_ref, dx2_ref, w_ref, mod_ref, g_ref, gx_ref, st_ref):
        i = pl.program_id(0)

        @pl.when(i == 0)
        def _():
            st_ref[...] = jnp.zeros_like(st_ref)

        dh = _dot(dp_ref[...], w_ref[...], NN)
        xhat, rstd = _rms(x_ref[...])
        dn = dh * (1.0 + mod_ref[1:2, :])
        gx_ref[...] = dx2_ref[...] + _rms_bwd(dn * g_ref[...], xhat, rstd)
        st_ref[0:1, :] += _colsum(dh)
        st_ref[1:2, :] += _colsum(dh * (xhat * g_ref[...]))
        st_ref[2:3, :] += _colsum(dn * xhat)

    tile = pl.BlockSpec((tm, d), lambda i: (i, 0))
    return _call(
        body, "mix_in_bwd_dx", (s // tm,),
        [pl.BlockSpec((tm, din), lambda i: (i, 0)), tile, tile, _full(w_in_t.shape), _full(mod6.shape),
         _full(g_mix.shape)],
        [tile, _full((SUBLANES, d))],
        [jax.ShapeDtypeStruct((s, d), F32), jax.ShapeDtypeStruct((SUBLANES, d), F32)],
        [dproj, x2d, dx2, w_in_t, mod6, g_mix], rider=rider)


def _mix_in_bwd_dw(dproj, hn1, tm, tn, rider=None):
    s, d = hn1.shape
    din = dproj.shape[1]

    def body(dp_ref, hn_ref, gw_ref):
        i = pl.program_id(1)

        @pl.when(i == 0)
        def _():
            gw_ref[...] = jnp.zeros_like(gw_ref)

        gw_ref[...] += _dot(dp_ref[...], hn_ref[...], TN)

    return _call(
        body, "mix_in_bwd_dw", (din // tn, s // tm),
        [pl.BlockSpec((tm, tn), lambda p, i: (i, p)), pl.BlockSpec((tm, d), lambda p, i: (i, 0))],
        [pl.BlockSpec((tn, d), lambda p, i: (p, 0))],
        [jax.ShapeDtypeStruct((din, d), F32)],
        [dproj, hn1], rider=rider)


def _adamw(w, g, m, v):
    m = ADAM_B1 * m + (1.0 - ADAM_B1) * g
    v = ADAM_B2 * v + (1.0 - ADAM_B2) * (g * g)
    m_hat = m / (1.0 - ADAM_B1 ** ADAM_STEP)
    v_hat = v / (1.0 - ADAM_B2 ** ADAM_STEP)
    delta = -ADAM_LR * (m_hat / (jnp.sqrt(v_hat) + ADAM_EPS) + ADAM_WD * w)
    return delta, m, v


def _pair_sum(g4, h4, core_chip, tr, name):
    _, _, r, n = g4.shape

    def body(sc_ref, g_ref, h_ref, sb_ref, own_ref):
        q = pl.program_id(1)
        ssum = g_ref[...] + h_ref[...]
        sb_ref[...] = ssum.astype(BF16)

        @pl.when(q == sc_ref[1])
        def _():
            own_ref[...] = ssum

    grid_spec = pltpu.PrefetchScalarGridSpec(
        num_scalar_prefetch=1, grid=(r // tr, 4),
        in_specs=[pl.BlockSpec((None, None, tr, n), lambda i, q, sc: (q, sc[0], i, 0)),
                  pl.BlockSpec((None, tr, n), lambda i, q, sc: (q, i, 0))],
        out_specs=[pl.BlockSpec((None, tr, n), lambda i, q, sc: (q, i, 0)),
                   pl.BlockSpec((tr, n), lambda i, q, sc: (i, 0))])
    return pl.pallas_call(
        body, name=name, grid_spec=grid_spec,
        out_shape=[jax.ShapeDtypeStruct((4, r, n), BF16), jax.ShapeDtypeStruct((r, n), F32)],
        compiler_params=_params(("parallel", "arbitrary")),
    )(core_chip, g4, h4)


def _sum4(own, parts, tr, name):
    r, n = own.shape

    def body(o_ref, p_ref, out_ref):
        acc = o_ref[...]
        for k in range(3):
            acc = acc + p_ref[k].astype(F32)
        out_ref[...] = acc

    return pl.pallas_call(
        body, name=name, grid=(r // tr,),
        in_specs=[pl.BlockSpec((tr, n), lambda i: (i, 0)), pl.BlockSpec((3, tr, n), lambda i: (0, i, 0))],
        out_specs=pl.BlockSpec((tr, n), lambda i: (i, 0)),
        out_shape=jax.ShapeDtypeStruct((r, n), F32),
        compiler_params=_params(("parallel",)),
    )(own, parts)


def _sum8(parts, tr, name):
    _, rows, n = parts.shape

    def body(p_ref, o_ref):
        acc = p_ref[0]
        for k in range(1, N_DEV):
            acc = acc + p_ref[k]
        o_ref[...] = acc

    return pl.pallas_call(
        body, name=name, grid=(rows // tr,),
        in_specs=[pl.BlockSpec((N_DEV, tr, n), lambda i: (0, i, 0))],
        out_specs=pl.BlockSpec((tr, n), lambda i: (i, 0)),
        out_shape=jax.ShapeDtypeStruct((rows, n), F32),
        compiler_params=_params(("parallel",)),
    )(parts)


def _adam_rows(w, g, m, v, tr, name):
    rows, n = w.shape

    def body(w_ref, g_ref, m_ref, v_ref, d_ref, nm_ref, nv_ref):
        d_ref[...], nm_ref[...], nv_ref[...] = _adamw(w_ref[...], g_ref[...], m_ref[...], v_ref[...])

    tile = pl.BlockSpec((tr, n), lambda i: (i, 0))
    return pl.pallas_call(
        body, name=name, grid=(rows // tr,),
        in_specs=[tile] * 4, out_specs=[tile] * 3,
        out_shape=[jax.ShapeDtypeStruct((rows, n), F32)] * 3,
        compiler_params=_params(("parallel",)),
    )(w, g, m, v)


def _ada_bwd_adam(cact_t, dmod_cols, w, m, v, tr):
    rows, n = w.shape

    def body(c_ref, d_ref, w_ref, m_ref, v_ref, g_ref, dl_ref, nm_ref, nv_ref):
        def term(b):
            return c_ref[b].astype(BF16).astype(F32) * d_ref[b:b + 1, :].astype(BF16).astype(F32)

        g = term(0)
        for b in range(1, N_DEV):
            g = g + term(b)
        g_ref[...] = g
        dl_ref[...], nm_ref[...], nv_ref[...] = _adamw(w_ref[...], g, m_ref[...], v_ref[...])

    tile = pl.BlockSpec((tr, n), lambda i: (i, 0))
    return pl.pallas_call(
        body, name="ada_bwd_adam", grid=(rows // tr,),
        in_specs=[pl.BlockSpec((N_DEV, tr, 1), lambda i: (0, i, 0)), _full(dmod_cols.shape), tile, tile, tile],
        out_specs=[tile] * 4,
        out_shape=[jax.ShapeDtypeStruct((rows, n), F32)] * 4,
        compiler_params=_params(("parallel",)),
    )(cact_t, dmod_cols, w, m, v)


def _adam_small(ws, gs, ms, vs):
    n = len(ws)

    def body(*refs):
        w_r, g_r, m_r, v_r = refs[:n], refs[n:2 * n], refs[2 * n:3 * n], refs[3 * n:4 * n]
        d_r, nm_r, nv_r = refs[4 * n:5 * n], refs[5 * n:6 * n], refs[6 * n:7 * n]
        for k in range(n):
            d_r[k][...], nm_r[k][...], nv_r[k][...] = _adamw(w_r[k][...], g_r[k][...], m_r[k][...], v_r[k][...])

    shapes = [jax.ShapeDtypeStruct(w.shape, F32) for w in ws]
    outs = pl.pallas_call(
        body, name="adam_small", out_shape=shapes * 3, compiler_params=_params(),
    )(*ws, *gs, *ms, *vs)
    return outs[:n], outs[n:2 * n], outs[2 * n:]


def _block_diag(w):
    h, hd, _ = w.shape
    per = LANES // hd
    eye = jnp.eye(per, dtype=w.dtype)
    w5 = w.reshape(h // per, per, hd, 1, hd) * eye[None, :, None, :, None]
    return w5.reshape(h // per, LANES, LANES)


def _block_diag_grad(g, h, hd):
    per = LANES // hd
    g5 = g.reshape(h // per, per, hd, per, hd)
    return jnp.stack([g5[:, a, :, a, :] for a in range(per)], axis=1).reshape(h, hd, hd)


def kernel(x, c, w_ada, b_ada, g_mix, w_in, conv_w_sc, conv_w_lru, conv_b_lru, w_rg_a, b_rg_a, w_rg_x, b_rg_x, lru_lambda, w_out, g_mlp, w_up, w_down, g_final, loss_target, m_w_ada, m_b_ada, m_g_mix, m_w_in, m_conv_w_sc, m_conv_w_lru, m_conv_b_lru, m_w_rg_a, m_b_rg_a, m_w_rg_x, m_b_rg_x, m_lru_lambda, m_w_out, m_g_mlp, m_w_up, m_w_down, m_g_final, v_w_ada, v_b_ada, v_g_mix, v_w_in, v_conv_w_sc, v_conv_w_lru, v_conv_b_lru, v_w_rg_a, v_b_rg_a, v_w_rg_x, v_b_rg_x, v_lru_lambda, v_w_out, v_g_mlp, v_w_up, v_w_down, v_g_final):
    s, d = x.shape[1], x.shape[2]
    width = conv_b_lru.shape[1]
    heads, hd = w_rg_a.shape[1], w_rg_a.shape[2]
    f = w_down.shape[1] * N_DEV
    n_ada = w_ada.shape[2]
    csh = conv_w_sc.shape[2]
    me = 4 * lax.axis_index("x") + 2 * lax.axis_index("y") + lax.axis_index("c")
    tm = min(512, s)
    tm_mlp = min(1024, s)
    tk = 512

    x2d = x[0]
    tgt = loss_target[0]

    pay = jnp.zeros((SUBLANES, d), F32)
    pay = pay.at[0:1, :].set(c)
    pay = pay.at[1:4, 0:csh].set(conv_w_sc[0])
    pay = pay.at[4:8, 0:csh].set(conv_w_lru[0])
    w_in_t_sh = w_in[0].T.astype(BF16)
    w_up_t_sh = w_up[0].T.astype(BF16)
    w_out_sh = w_out[0].astype(BF16)
    w_down_sh = w_down[0].astype(BF16)
    pay_all, w_in_t = _gather2("gather_in", [pay, w_in_t_sh])
    w_in_t = w_in_t.reshape(-1, d)
    c_all = pay_all[:, 0, :]
    conv_sc = pay_all[:, 1:4, 0:csh].transpose(1, 0, 2).reshape(3, width)
    conv_lru = pay_all[:, 4:8, 0:csh].transpose(1, 0, 2).reshape(4, width)

    b_ada_sh = lax.dynamic_slice(b_ada, (0, me * n_ada), (1, n_ada))
    mod_cols, c_act = _ada_fwd(c_all, w_ada[0], b_ada_sh)
    (mod_rows,) = _exchange("scatter_mod", [], [mod_cols.reshape(N_DEV, 1, n_ada)])
    mod_rows, w_out_sh, w_up_t_sh, w_down_sh = lax.optimization_barrier((mod_rows, w_out_sh, w_up_t_sh, w_down_sh))
    (w_out_g,) = _seq_gather2("gather_w_out", 1, [w_out_sh])
    w_up_g, w_down_g = _seq_gather2("gather_mlp_weights", 2, [w_up_t_sh, w_down_sh])
    mod6 = jnp.zeros((SUBLANES, d), F32).at[0:6, :].set(mod_rows.reshape(6, d))

    wa_bd = _block_diag(w_rg_a[0]).astype(BF16)
    wx_bd = _block_diag(w_rg_x[0]).astype(BF16)
    ba = b_rg_a.reshape(1, width)
    bx = b_rg_x.reshape(1, width)
    g_fin = g_final.reshape(1, d)

    (hn1, proj), _ = _mix_in_fwd(x2d, mod6, g_mix, w_in_t, tm)
    (ymix, h_all), _ = _mixer_fwd(proj, conv_sc, conv_lru, conv_b_lru, wa_bd, wx_bd, ba, bx, lru_lambda, width)
    w_out_b = w_out_g.reshape(-1, d)
    (mix, x2, hn2), _ = _mix_out_fwd(ymix, x2d, w_out_b, mod6, g_mlp, tm)
    w_up_t = w_up_g.reshape(-1, d)
    w_down_b = w_down_g.reshape(-1, d)
    z, dx3, dyb, st_fin = _mlp_fwd_loss(hn2, w_up_t, w_down_b, x2, tgt, mod6, g_fin, tm, 4 * tk)

    core_chip = jnp.stack([lax.axis_index("c"), 2 * lax.axis_index("x") + lax.axis_index("y")]).astype(jnp.int32)
    dz, dhn2 = _mlp_bwd_dx(dyb, z, w_down_b, w_up_t, tm, 4 * tk)
    g_down, g_up_t = _mlp_bwd_dw(z, dz, dyb, hn2, tm_mlp, 2 * tk)
    g_up4, g_down4 = g_up_t.reshape(4, 2, -1, d), g_down.reshape(4, 2, -1, d)
    (dx2, dymix, g_out, st_out), (h_up, h_down) = _mix_out_bwd(
        dhn2, x2, dx3, mix, ymix, w_out_b, mod6, g_mlp, tm, rider=_ride_pair_swap([g_up4, g_down4]))
    sb_up, own_up = _pair_sum(g_up4, h_up, core_chip, 256, "pair_sum_w_up")
    sb_down, own_down = _pair_sum(g_down4, h_down, core_chip, 256, "pair_sum_w_down")
    g_out4 = g_out.reshape(4, 2, -1, d)
    p_up, p_down = _seq_chip_exchange("exchange_mlp_grads", 3, [sb_up, sb_down])
    (dproj, g_small, g_wa, g_wx), (h_out,) = _mixer_bwd(
        proj, dymix, h_all, conv_sc, conv_lru, conv_b_lru, wa_bd, wx_bd, ba, bx, lru_lambda, width,
        rider=_ride_pair_swap([g_out4]))
    sb_out, own_out = _pair_sum(g_out4, h_out, core_chip, g_out4.shape[2], "pair_sum_w_out")
    (p_out,) = _seq_chip_exchange("exchange_w_out_grad", 4, [sb_out])
    (grad_x, st_in), _ = _mix_in_bwd_dx(dproj, x2d, dx2, w_in_t, mod6, g_mix, tm)

    small = jnp.concatenate([
        st_in[0:2], st_out[3:4], st_out[0:2], st_fin[1:2],
        st_in[2:3], st_out[2:3], st_fin[0:1],
        jnp.concatenate([g_small[7:8], g_small[10:11]], axis=1),
        jnp.concatenate([g_small[8:9], g_small[9:10]], axis=1),
        jnp.concatenate([jnp.concatenate([g_small[0:3], jnp.zeros((1, width), F32)], axis=0), g_small[3:7]], axis=1),
        st_fin[2:3],
        _block_diag_grad(g_wa, heads, hd).reshape(-1, d),
        _block_diag_grad(g_wx, heads, hd).reshape(-1, d),
    ], axis=0)

    (small_all,) = _seq_gather2("gather_small_grads", 5, [small])
    (g_in_t,), _ = _mix_in_bwd_dw(dproj, hn1, tm_mlp, 512)
    g_in4 = g_in_t.reshape(4, 2, -1, d)
    (h_in,) = _comm("swap_w_in", _ride_pair_swap([g_in4]))
    sb_in, own_in = _pair_sum(g_in4, h_in, core_chip, g_in4.shape[2], "pair_sum_w_in")
    (p_in,) = _seq_chip_exchange("exchange_w_in_grad", 6, [sb_in])

    gs_in = _sum4(own_in, p_in, own_in.shape[0], "sum_w_in").T
    gs_up = _sum4(own_up, p_up, 256, "sum_w_up").T
    gs_out = _sum4(own_out, p_out, own_out.shape[0], "sum_w_out")
    gs_down = _sum4(own_down, p_down, 256, "sum_w_down")
    ad_in = _adam_rows(w_in[0], gs_in, m_w_in[0], v_w_in[0], 256, "adam_w_in")
    ad_up = _adam_rows(w_up[0], gs_up, m_w_up[0], v_w_up[0], 256, "adam_w_up")
    ad_out = _adam_rows(w_out[0], gs_out, m_w_out[0], v_w_out[0], w_out.shape[1], "adam_w_out")
    ad_down = _adam_rows(w_down[0], gs_down, m_w_down[0], v_w_down[0], 256, "adam_w_down")

    gsum = _sum8(small_all, SMALL_ROWS, "sum_small")
    loss = (0.5 / d) * jnp.sum(gsum[15])
    dmod_cols = lax.dynamic_slice(small_all[:, 0:6, :].reshape(N_DEV, 6 * d), (0, me * n_ada), (N_DEV, n_ada))
    g_ada, d_ada, nm_ada, nv_ada = _ada_bwd_adam(c_act[:, :, None], dmod_cols, w_ada[0], m_w_ada[0], v_w_ada[0], 256)

    g_conv = lax.dynamic_slice(gsum[11:15, 0:width], (0, me * csh), (4, csh))
    g_conv_l = lax.dynamic_slice(gsum[11:15, width:2 * width], (0, me * csh), (4, csh))
    small_g = [
        gsum[0:6].reshape(1, 6 * d),
        gsum[6:7],
        g_conv[0:3].reshape(1, 3, csh),
        g_conv_l.reshape(1, 4, csh),
        gsum[9:10, 0:width],
        gsum[16:48].reshape(1, heads, hd, hd),
        gsum[10:11, 0:width].reshape(1, heads, hd),
        gsum[48:80].reshape(1, heads, hd, hd),
        gsum[10:11, width:].reshape(1, heads, hd),
        gsum[9:10, width:],
        gsum[7:8],
        gsum[8],
    ]
    small_w = [b_ada, g_mix, conv_w_sc, conv_w_lru, conv_b_lru, w_rg_a, b_rg_a, w_rg_x, b_rg_x, lru_lambda, g_mlp, g_final]
    small_m = [m_b_ada, m_g_mix, m_conv_w_sc, m_conv_w_lru, m_conv_b_lru, m_w_rg_a, m_b_rg_a, m_w_rg_x, m_b_rg_x,
               m_lru_lambda, m_g_mlp, m_g_final]
    small_v = [v_b_ada, v_g_mix, v_conv_w_sc, v_conv_w_lru, v_conv_b_lru, v_w_rg_a, v_b_rg_a, v_w_rg_x, v_b_rg_x,
               v_lru_lambda, v_g_mlp, v_g_final]
    sd, snm, snv = _adam_small(small_w, small_g, small_m, small_v)

    def order(ada, w_in_, w_out_, w_up_, w_down_, sm):
        return [ada[None], sm[0], sm[1], w_in_[None], sm[2], sm[3], sm[4], sm[5], sm[6], sm[7], sm[8], sm[9],
                w_out_[None], sm[10], w_up_[None], w_down_[None], sm[11]]

    grads = order(g_ada, gs_in, gs_out, gs_up, gs_down, small_g)
    deltas = order(d_ada, ad_in[0], ad_out[0], ad_up[0], ad_down[0], sd)
    new_m = order(nm_ada, ad_in[1], ad_out[1], ad_up[1], ad_down[1], snm)
    new_v = order(nv_ada, ad_in[2], ad_out[2], ad_up[2], ad_down[2], snv)
    return (loss, grad_x[None], *grads, *deltas, *new_m, *new_v)
```

```python
import functools

import jax
import jax.numpy as jnp
from jax import lax
from jax.experimental import pallas as pl
from jax.experimental.pallas import tpu as pltpu
from jax.experimental.pallas import tpu_sc as plsc

F32 = jnp.float32
BF16 = jnp.bfloat16
N_DEV = 8
EPS = 1e-6
RG_C = 8.0
GELU_K0 = 0.7978845608028654
GELU_K1 = 0.044715
ADAM_LR = 0.001
ADAM_B1 = 0.9
ADAM_B2 = 0.999
ADAM_EPS = 1e-08
ADAM_WD = 0.01
ADAM_STEP = 10
LANES = 128
SUBLANES = 8
VMEM_LIMIT = 52 * 1024 * 1024
MIX_ROWS = 256
SMALL_ROWS = 80

MESH = pl.DeviceIdType.MESH
ANY = pl.BlockSpec(memory_space=pl.ANY)
NN = ((1,), (0,))
NT = ((1,), (1,))
TN = ((0,), (0,))


def _dot(a, b, dims):
    return lax.dot_general(a, b, (dims, ((), ())), preferred_element_type=F32)


def _params(sem=None):
    return pltpu.CompilerParams(dimension_semantics=sem, vmem_limit_bytes=VMEM_LIMIT)


def _full(shape):
    nd = len(shape)
    return pl.BlockSpec(shape, lambda *_: (0,) * nd)


def _exchange(name, gathers, scatters):
    n_g = len(gathers)
    arrs = list(gathers) + list(scatters)
    n = len(arrs)
    out_shape = [jax.ShapeDtypeStruct((N_DEV,) + a.shape, a.dtype) for a in gathers]
    out_shape += [jax.ShapeDtypeStruct(a.shape, a.dtype) for a in scatters]

    def body(*refs):
        ins, outs = refs[:n], refs[n:2 * n]
        send_sems, recv_sems, local_sems = refs[2 * n:]
        x, y, c = lax.axis_index("x"), lax.axis_index("y"), lax.axis_index("c")
        me = 4 * x + 2 * y + c

        def src(a, dev):
            return ins[a] if a < n_g else ins[a].at[dev]

        def peer_of(k):
            px = 1 - x if (k >> 2) & 1 else x
            py = 1 - y if (k >> 1) & 1 else y
            pc = 1 - c if k & 1 else c
            return (px, py, pc), 4 * px + 2 * py + pc

        local = [pltpu.make_async_copy(src(a, me), outs[a].at[me], local_sems.at[a]) for a in range(n)]
        for cp in local:
            cp.start()
        sends = []
        for k in range(1, N_DEV):
            peer, pidx = peer_of(k)
            for a in range(n):
                cp = pltpu.make_async_remote_copy(
                    src_ref=src(a, pidx), dst_ref=outs[a].at[me],
                    send_sem=send_sems.at[a * (N_DEV - 1) + k - 1], recv_sem=recv_sems.at[a * (N_DEV - 1) + k - 1],
                    device_id=peer, device_id_type=MESH)
                cp.start()
                sends.append(cp)
        for k in range(1, N_DEV):
            peer, pidx = peer_of(k)
            for a in range(n):
                pltpu.make_async_remote_copy(
                    src_ref=src(a, pidx), dst_ref=outs[a].at[pidx],
                    send_sem=send_sems.at[a * (N_DEV - 1) + k - 1], recv_sem=recv_sems.at[a * (N_DEV - 1) + k - 1],
                    device_id=peer, device_id_type=MESH).wait_recv()
        for cp in sends:
            cp.wait_send()
        for cp in local:
            cp.wait()

    return pl.pallas_call(
        body, name=name, out_shape=out_shape,
        in_specs=[ANY] * n, out_specs=[ANY] * n,
        scratch_shapes=[pltpu.SemaphoreType.DMA((n * (N_DEV - 1),)),
                        pltpu.SemaphoreType.DMA((n * (N_DEV - 1),)),
                        pltpu.SemaphoreType.DMA((n,))],
    )(*arrs)


def _gather2(name, arrs):
    n = len(arrs)
    per = 7
    out_shape = [jax.ShapeDtypeStruct((N_DEV,) + a.shape, a.dtype) for a in arrs]

    def body(*refs):
        ins, outs = refs[:n], refs[n:2 * n]
        send_sems, recv_sems, local_sems = refs[2 * n:]
        x, y, c = lax.axis_index("x"), lax.axis_index("y"), lax.axis_index("c")
        sib = (x, y, 1 - c)
        chips = [(1 - x, y), (x, 1 - y), (1 - x, 1 - y)]

        def slot(a, px, py, pc):
            return outs[a].at[4 * px + 2 * py + pc]

        def copy(a, k, block, to, src=None):
            return pltpu.make_async_remote_copy(
                src_ref=slot(a, *block) if src is None else src, dst_ref=slot(a, *block),
                send_sem=send_sems.at[a * per + k], recv_sem=recv_sems.at[a * per + k],
                device_id=to, device_id_type=MESH)

        local = [pltpu.make_async_copy(ins[a], slot(a, x, y, c), local_sems.at[a]) for a in range(n)]
        for cp in local:
            cp.start()
        first = []
        for a in range(n):
            first += [copy(a, 1 + j, (x, y, c), (*chip, c), src=ins[a]) for j, chip in enumerate(chips)]
        for a in range(n):
            first.append(copy(a, 0, (x, y, c), sib, src=ins[a]))
        for cp in first:
            cp.start()
        passed = []
        for a in range(n):
            for j, chip in enumerate(chips):
                copy(a, 1 + j, (*chip, c), (x, y, c)).wait_recv()
                cp = copy(a, 4 + j, (*chip, c), sib)
                cp.start()
                passed.append(cp)
        for a in range(n):
            copy(a, 0, sib, (x, y, c)).wait_recv()
            for j, chip in enumerate(chips):
                copy(a, 4 + j, (*chip, 1 - c), (x, y, c)).wait_recv()
        for cp in first + passed:
            cp.wait_send()
        for cp in local:
            cp.wait()

    return pl.pallas_call(
        body, name=name, out_shape=out_shape,
        in_specs=[ANY] * n, out_specs=[ANY] * n,
        scratch_shapes=[pltpu.SemaphoreType.DMA((n * per,)), pltpu.SemaphoreType.DMA((n * per,)),
                        pltpu.SemaphoreType.DMA((n,))],
    )(*arrs)


def _seq_gather2(name, collective_id, arrs):
    n = len(arrs)
    per = 7

    def body(*refs):
        ins, outs = refs[:n], refs[n:2 * n]
        send_sems, recv_sems, local_sems = refs[2 * n:]
        x, y, c = lax.axis_index("x"), lax.axis_index("y"), lax.axis_index("c")
        sib = (x, y, 1 - c)
        chips = [(1 - x, y), (x, 1 - y), (1 - x, 1 - y)]
        barrier = pltpu.get_barrier_semaphore()
        for peer in [sib] + [(*chip, c) for chip in chips]:
            pl.semaphore_signal(barrier, inc=1, device_id=peer, device_id_type=MESH)
        pl.semaphore_wait(barrier, 4)

        def slot(a, px, py, pc):
            return outs[a].at[4 * px + 2 * py + pc]

        def copy(a, k, block, to, src=None):
            return pltpu.make_async_remote_copy(
                src_ref=slot(a, *block) if src is None else src, dst_ref=slot(a, *block),
                send_sem=send_sems.at[a * per + k], recv_sem=recv_sems.at[a * per + k],
                device_id=to, device_id_type=MESH)

        local = [pltpu.make_async_copy(ins[a], slot(a, x, y, c), local_sems.at[a]) for a in range(n)]
        for cp in local:
            cp.start()
        first = []
        for a in range(n):
            first += [copy(a, 1 + j, (x, y, c), (*chip, c), src=ins[a]) for j, chip in enumerate(chips)]
        for a in range(n):
            first.append(copy(a, 0, (x, y, c), sib, src=ins[a]))
        for cp in first:
            cp.start()
        passed = []
        for a in range(n):
            for j, chip in enumerate(chips):
                copy(a, 1 + j, (*chip, c), (x, y, c)).wait_recv()
                cp = copy(a, 4 + j, (*chip, c), sib)
                cp.start()
                passed.append(cp)
        for a in range(n):
            copy(a, 0, sib, (x, y, c)).wait_recv()
            for j, chip in enumerate(chips):
                copy(a, 4 + j, (*chip, 1 - c), (x, y, c)).wait_recv()
        for cp in first + passed:
            cp.wait_send()
        for cp in local:
            cp.wait()

    return pl.kernel(
        body, out_type=[jax.ShapeDtypeStruct((N_DEV,) + a.shape, a.dtype) for a in arrs],
        mesh=plsc.ScalarSubcoreMesh(axis_name="seq", num_cores=1),
        scratch_types=[pltpu.SemaphoreType.DMA((n * per,)), pltpu.SemaphoreType.DMA((n * per,)),
                       pltpu.SemaphoreType.DMA((n,))],
        compiler_params=pltpu.CompilerParams(collective_id=collective_id), name=name,
    )(*arrs)


def _seq_chip_exchange(name, collective_id, arrs):
    n = len(arrs)

    def body(*refs):
        ins, outs = refs[:n], refs[n:2 * n]
        send_sems, recv_sems = refs[2 * n:]
        x, y, c = lax.axis_index("x"), lax.axis_index("y"), lax.axis_index("c")

        def peer(k):
            return (1 - x if (k >> 1) & 1 else x), (1 - y if k & 1 else y)

        barrier = pltpu.get_barrier_semaphore()
        for k in (1, 2, 3):
            pl.semaphore_signal(barrier, inc=1, device_id=(*peer(k), c), device_id_type=MESH)
        pl.semaphore_wait(barrier, 3)

        def copy(a, k):
            px, py = peer(k)
            return pltpu.make_async_remote_copy(
                src_ref=ins[a].at[2 * px + py], dst_ref=outs[a].at[k - 1],
                send_sem=send_sems.at[a * 3 + k - 1], recv_sem=recv_sems.at[a * 3 + k - 1],
                device_id=(px, py, c), device_id_type=MESH)

        cps = [copy(a, k) for a in range(n) for k in (1, 2, 3)]
        for cp in cps:
            cp.start()
        for cp in cps:
            cp.wait_recv()
        for cp in cps:
            cp.wait_send()

    return pl.kernel(
        body, out_type=[jax.ShapeDtypeStruct((3,) + a.shape[1:], a.dtype) for a in arrs],
        mesh=plsc.ScalarSubcoreMesh(axis_name="seq", num_cores=1),
        scratch_types=[pltpu.SemaphoreType.DMA((n * 3,)), pltpu.SemaphoreType.DMA((n * 3,))],
        compiler_params=pltpu.CompilerParams(collective_id=collective_id), name=name,
    )(*arrs)


def _pair_swap(name, arrs):
    n = len(arrs)
    out_shape = [jax.ShapeDtypeStruct((4,) + a.shape[2:], a.dtype) for a in arrs]

    def body(*refs):
        ins, outs = refs[:n], refs[n:2 * n]
        send_sems, recv_sems = refs[2 * n:]
        x, y, c = lax.axis_index("x"), lax.axis_index("y"), lax.axis_index("c")

        def copy(a, q):
            return pltpu.make_async_remote_copy(
                src_ref=ins[a].at[q, 1 - c], dst_ref=outs[a].at[q],
                send_sem=send_sems.at[a * 4 + q], recv_sem=recv_sems.at[a * 4 + q],
                device_id=(x, y, 1 - c), device_id_type=MESH)

        cps = [copy(a, q) for a in range(n) for q in range(4)]
        for cp in cps:
            cp.start()
        for cp in cps:
            cp.wait_recv()
        for cp in cps:
            cp.wait_send()

    return pl.pallas_call(
        body, name=name, out_shape=out_shape,
        in_specs=[ANY] * n, out_specs=[ANY] * n,
        scratch_shapes=[pltpu.SemaphoreType.DMA((n * 4,)), pltpu.SemaphoreType.DMA((n * 4,))],
    )(*arrs)


def _chip_exchange(name, arrs):
    n = len(arrs)
    out_shape = [jax.ShapeDtypeStruct((3,) + a.shape[1:], a.dtype) for a in arrs]

    def body(*refs):
        ins, outs = refs[:n], refs[n:2 * n]
        send_sems, recv_sems = refs[2 * n:]
        x, y, c = lax.axis_index("x"), lax.axis_index("y"), lax.axis_index("c")

        def copy(a, k):
            px = 1 - x if (k >> 1) & 1 else x
            py = 1 - y if k & 1 else y
            return pltpu.make_async_remote_copy(
                src_ref=ins[a].at[2 * px + py], dst_ref=outs[a].at[k - 1],
                send_sem=send_sems.at[a * 3 + k - 1], recv_sem=recv_sems.at[a * 3 + k - 1],
                device_id=(px, py, c), device_id_type=MESH)

        cps = [copy(a, k) for a in range(n) for k in (1, 2, 3)]
        for cp in cps:
            cp.start()
        for cp in cps:
            cp.wait_recv()
        for cp in cps:
            cp.wait_send()

    return pl.pallas_call(
        body, name=name, out_shape=out_shape,
        in_specs=[ANY] * n, out_specs=[ANY] * n,
        scratch_shapes=[pltpu.SemaphoreType.DMA((n * 3,)), pltpu.SemaphoreType.DMA((n * 3,))],
    )(*arrs)


class _Rider:
    def __init__(self, arrays, out_shapes, n_sems, build, aliases=None):
        self.arrays, self.out_shapes, self.n_sems, self.build = list(arrays), list(out_shapes), n_sems, build
        self.aliases = dict(aliases or {})


def _merge_riders(r1, r2):
    n1i, n1o, n1s = len(r1.arrays), len(r1.out_shapes), r1.n_sems

    def build(ins, outs, send_sems, recv_sems):
        a = r1.build(ins[:n1i], outs[:n1o], send_sems.at[pl.ds(0, n1s)], recv_sems.at[pl.ds(0, n1s)])
        b = r2.build(ins[n1i:], outs[n1o:], send_sems.at[pl.ds(n1s, r2.n_sems)], recv_sems.at[pl.ds(n1s, r2.n_sems)])
        return tuple(p + q for p, q in zip(a, b))

    aliases = dict(r1.aliases)
    aliases.update({k + n1i: v + n1o for k, v in r2.aliases.items()})
    return _Rider(r1.arrays + r2.arrays, r1.out_shapes + r2.out_shapes, n1s + r2.n_sems, build, aliases)


def _place():
    x, y, c = lax.axis_index("x"), lax.axis_index("y"), lax.axis_index("c")
    chips = [(1 - x, y), (x, 1 - y), (1 - x, 1 - y)]
    return x, y, c, chips


def _ride_gather_ici(arrs):
    n = len(arrs)

    def build(ins, outs, send_sems, recv_sems):
        x, y, c, chips = _place()
        peers = [(*chip, c) for chip in chips] + [(x, y, 1 - c)]
        me = 4 * x + 2 * y + c
        local = [pltpu.make_async_copy(ins[a], outs[a].at[me], send_sems.at[a * 5 + 4]) for a in range(n)]
        sends, recvs = [], []
        for a in range(n):
            for j, (px, py, pc) in enumerate(peers):
                sends.append(pltpu.make_async_remote_copy(
                    src_ref=ins[a], dst_ref=outs[a].at[me], send_sem=send_sems.at[a * 5 + j],
                    recv_sem=recv_sems.at[a * 5 + j], device_id=(px, py, pc), device_id_type=MESH))
                recvs.append(pltpu.make_async_remote_copy(
                    src_ref=ins[a], dst_ref=outs[a].at[4 * px + 2 * py + pc], send_sem=send_sems.at[a * 5 + j],
                    recv_sem=recv_sems.at[a * 5 + j], device_id=(px, py, pc), device_id_type=MESH))
        return local, sends, recvs

    shapes = [jax.ShapeDtypeStruct((N_DEV,) + a.shape, a.dtype) for a in arrs]
    return _Rider(arrs, shapes, n * 5, build)


def _ride_gather_direct(arrs):
    n = len(arrs)

    def build(ins, outs, send_sems, recv_sems):
        x, y, c, _ = _place()
        me = 4 * x + 2 * y + c
        local = [pltpu.make_async_copy(ins[a], outs[a].at[me], send_sems.at[a * N_DEV + 7]) for a in range(n)]
        sends, recvs = [], []
        for a in range(n):
            for k in range(1, N_DEV):
                px = 1 - x if (k >> 2) & 1 else x
                py = 1 - y if (k >> 1) & 1 else y
                pc = 1 - c if k & 1 else c
                sem = a * N_DEV + k - 1
                sends.append(pltpu.make_async_remote_copy(
                    src_ref=ins[a], dst_ref=outs[a].at[me], send_sem=send_sems.at[sem], recv_sem=recv_sems.at[sem],
                    device_id=(px, py, pc), device_id_type=MESH))
                recvs.append(pltpu.make_async_remote_copy(
                    src_ref=ins[a], dst_ref=outs[a].at[4 * px + 2 * py + pc], send_sem=send_sems.at[sem],
                    recv_sem=recv_sems.at[sem], device_id=(px, py, pc), device_id_type=MESH))
        return local, sends, recvs

    shapes = [jax.ShapeDtypeStruct((N_DEV,) + a.shape, a.dtype) for a in arrs]
    return _Rider(arrs, shapes, n * N_DEV, build)


def _ride_gather_d2d(gathered):
    n = len(gathered)

    def build(ins, outs, send_sems, recv_sems):
        x, y, c, chips = _place()
        sends, recvs = [], []
        for a in range(n):
            for j, (px, py) in enumerate(chips):
                mine = outs[a].at[4 * px + 2 * py + c]
                theirs = outs[a].at[4 * px + 2 * py + 1 - c]
                sends.append(pltpu.make_async_remote_copy(
                    src_ref=mine, dst_ref=mine, send_sem=send_sems.at[a * 3 + j], recv_sem=recv_sems.at[a * 3 + j],
                    device_id=(x, y, 1 - c), device_id_type=MESH))
                recvs.append(pltpu.make_async_remote_copy(
                    src_ref=mine, dst_ref=theirs, send_sem=send_sems.at[a * 3 + j], recv_sem=recv_sems.at[a * 3 + j],
                    device_id=(x, y, 1 - c), device_id_type=MESH))
        return [], sends, recvs

    shapes = [jax.ShapeDtypeStruct(a.shape, a.dtype) for a in gathered]
    return _Rider(gathered, shapes, n * 3, build, aliases={a: a for a in range(n)})


def _ride_pair_swap(arrs):
    n = len(arrs)

    def build(ins, outs, send_sems, recv_sems):
        x, y, c, _ = _place()
        cps = [pltpu.make_async_remote_copy(
            src_ref=ins[a].at[q, 1 - c], dst_ref=outs[a].at[q], send_sem=send_sems.at[a * 4 + q],
            recv_sem=recv_sems.at[a * 4 + q], device_id=(x, y, 1 - c), device_id_type=MESH)
            for a in range(n) for q in range(4)]
        return [], cps, cps

    shapes = [jax.ShapeDtypeStruct((4,) + a.shape[2:], a.dtype) for a in arrs]
    return _Rider(arrs, shapes, n * 4, build)


def _ride_chip_exchange(arrs):
    n = len(arrs)

    def build(ins, outs, send_sems, recv_sems):
        x, y, c, _ = _place()
        cps = []
        for a in range(n):
            for k in (1, 2, 3):
                px = 1 - x if (k >> 1) & 1 else x
                py = 1 - y if k & 1 else y
                cps.append(pltpu.make_async_remote_copy(
                    src_ref=ins[a].at[2 * px + py], dst_ref=outs[a].at[k - 1], send_sem=send_sems.at[a * 3 + k - 1],
                    recv_sem=recv_sems.at[a * 3 + k - 1], device_id=(px, py, c), device_id_type=MESH))
        return [], cps, cps

    shapes = [jax.ShapeDtypeStruct((3,) + a.shape[1:], a.dtype) for a in arrs]
    return _Rider(arrs, shapes, n * 3, build)


def _call(body, name, grid, in_specs, out_specs, out_shape, args, scratch=(), rider=None):
    n_in, n_out, n_scr = len(in_specs), len(out_specs), len(scratch)
    sem = ("arbitrary",) * len(grid)
    if rider is None:
        outs = pl.pallas_call(
            body, name=name, grid=grid, in_specs=in_specs, out_specs=out_specs, out_shape=out_shape,
            scratch_shapes=list(scratch), compiler_params=_params(sem))(*args)
        return outs, []
    ri, ro = len(rider.arrays), len(rider.out_shapes)

    def riding(*refs):
        ins, r_ins = refs[:n_in], refs[n_in:n_in + ri]
        outs = refs[n_in + ri:n_in + ri + n_out]
        r_outs = refs[n_in + ri + n_out:n_in + ri + n_out + ro]
        scr = refs[n_in + ri + n_out + ro:n_in + ri + n_out + ro + n_scr]
        send_sems, recv_sems = refs[-2:]
        first = functools.reduce(jnp.logical_and, [pl.program_id(k) == 0 for k in range(len(grid))])
        last = functools.reduce(jnp.logical_and, [pl.program_id(k) == grid[k] - 1 for k in range(len(grid))])

        @pl.when(first)
        def _():
            local, sends, _ = rider.build(r_ins, r_outs, send_sems, recv_sems)
            for cp in local + sends:
                cp.start()

        body(*ins, *outs, *scr)

        @pl.when(last)
        def _():
            local, sends, recvs = rider.build(r_ins, r_outs, send_sems, recv_sems)
            for cp in recvs:
                cp.wait_recv()
            for cp in sends:
                cp.wait_send()
            for cp in local:
                cp.wait()

    outs = pl.pallas_call(
        riding, name=name, grid=grid,
        in_specs=list(in_specs) + [ANY] * ri, out_specs=list(out_specs) + [ANY] * ro,
        out_shape=list(out_shape) + rider.out_shapes,
        scratch_shapes=list(scratch) + [pltpu.SemaphoreType.DMA((rider.n_sems,)), pltpu.SemaphoreType.DMA((rider.n_sems,))],
        input_output_aliases={n_in + k: n_out + v for k, v in rider.aliases.items()},
        compiler_params=_params(sem))(*args, *rider.arrays)
    return outs[:n_out], outs[n_out:]


def _comm(name, rider):
    def body(dummy_ref, out_ref):
        out_ref[...] = dummy_ref[...]

    dummy = jnp.zeros((SUBLANES, LANES), F32)
    spec = pl.BlockSpec((SUBLANES, LANES), lambda i: (0, 0))
    _, r_outs = _call(body, name, (1,), [spec], [spec], [jax.ShapeDtypeStruct(dummy.shape, F32)], [dummy], rider=rider)
    return r_outs


def _ada_fwd(c_all, w_ada_sh, b_ada_sh):
    nb, d = c_all.shape
    ncol = w_ada_sh.shape[1]

    def body(c_ref, w_ref, b_ref, mod_ref, cact_ref):
        cc = c_ref[...]
        ca = cc * jax.nn.sigmoid(cc)
        cact_ref[...] = ca
        mod_ref[...] = _dot(ca.astype(BF16), w_ref[...].astype(BF16), NN) + b_ref[...]

    return pl.pallas_call(
        body, name="ada_fwd",
        out_shape=[jax.ShapeDtypeStruct((nb, ncol), F32), jax.ShapeDtypeStruct((nb, d), F32)],
        compiler_params=_params(),
    )(c_all, w_ada_sh, b_ada_sh)


def _rms(xv):
    rstd = lax.rsqrt(jnp.mean(xv * xv, axis=-1, keepdims=True) + EPS)
    return xv * rstd, rstd


def _rms_bwd(dxhat, xhat, rstd):
    return rstd * (dxhat - xhat * jnp.mean(dxhat * xhat, axis=-1, keepdims=True))


def _colsum(v):
    return jnp.sum(v, axis=0, keepdims=True)


def _expm1(v):
    series = v * (1.0 + v * (0.5 + v * (1.0 / 6.0 + v * (1.0 / 24.0 + v * (1.0 / 120.0 + v * (1.0 / 720.0))))))
    return jnp.where(jnp.abs(v) < 0.3, series, jnp.exp(v) - 1.0)


def _softplus(v):
    return jnp.maximum(v, 0.0) + jnp.log1p(jnp.exp(-jnp.abs(v)))


def _gelu(v):
    t = jnp.tanh(GELU_K0 * (v + GELU_K1 * v * v * v))
    return 0.5 * v * (1.0 + t), t


def _dgelu(v, t):
    return 0.5 * (1.0 + t) + 0.5 * v * (1.0 - t * t) * GELU_K0 * (1.0 + 3.0 * GELU_K1 * v * v)


def _shift_down(v, k, prev8):
    r = pltpu.roll(v, k, 0)
    pr = pltpu.roll(prev8, k, 0)
    row8 = lax.broadcasted_iota(jnp.int32, prev8.shape, 0)
    top = jnp.where(row8 < k, pr, r[0:SUBLANES])
    return jnp.concatenate([top, r[SUBLANES:]], axis=0)


def _shift_up(v, k, next8):
    t = v.shape[0]
    r = pltpu.roll(v, t - k, 0)
    nr = pltpu.roll(next8, SUBLANES - k, 0)
    row8 = lax.broadcasted_iota(jnp.int32, next8.shape, 0)
    bot = jnp.where(row8 >= SUBLANES - k, nr, r[t - SUBLANES:t])
    return jnp.concatenate([r[:t - SUBLANES], bot], axis=0)


def _scan_fwd(a, b, h0):
    t = a.shape[0]
    row = lax.broadcasted_iota(jnp.int32, a.shape, 0)
    s = 1
    while s < min(t, SUBLANES):
        a_sh = pltpu.roll(a, s, 0)
        b_sh = pltpu.roll(b, s, 0)
        m = row >= s
        b = jnp.where(m, a * b_sh + b, b)
        a = jnp.where(m, a * a_sh, a)
        s *= 2
    while s < t:
        b = jnp.concatenate([b[:s], a[s:] * b[:t - s] + b[s:]], axis=0)
        a = jnp.concatenate([a[:s], a[s:] * a[:t - s]], axis=0)
        s *= 2
    return b + a * h0


def _scan_rev(m, b, g_next):
    t = m.shape[0]
    row = lax.broadcasted_iota(jnp.int32, m.shape, 0)
    s = 1
    while s < min(t, SUBLANES):
        m_sh = pltpu.roll(m, t - s, 0)
        b_sh = pltpu.roll(b, t - s, 0)
        msk = row < t - s
        b = jnp.where(msk, m * b_sh + b, b)
        m = jnp.where(msk, m * m_sh, m)
        s *= 2
    while s < t:
        b = jnp.concatenate([m[:t - s] * b[s:] + b[:t - s], b[t - s:]], axis=0)
        m = jnp.concatenate([m[:t - s] * m[s:], m[t - s:]], axis=0)
        s *= 2
    return b + m * g_next


def _lru_gates(u, wa, wx, ba, bx, sp):
    ub = u.astype(BF16)
    r = jax.nn.sigmoid(_dot(ub, wa, NN) + ba)
    i = jax.nn.sigmoid(_dot(ub, wx, NN) + bx)
    log_a = (-RG_C * r) * sp
    a = jnp.exp(log_a)
    mult = jnp.sqrt(-_expm1(2.0 * log_a))
    return ub, r, i, a, mult


def _conv3(p, pp, w_ref, lo):
    p1 = _shift_down(p, 1, pp)
    p2 = _shift_down(p, 2, pp)
    q = (w_ref[0:1, lo:lo + LANES] * p2 + w_ref[1:2, lo:lo + LANES] * p1) + w_ref[2:3, lo:lo + LANES] * p
    return q, p1, p2


def _conv4(xv, xp, w_ref, b_ref, lo):
    x1 = _shift_down(xv, 1, xp)
    x2 = _shift_down(xv, 2, xp)
    x3 = _shift_down(xv, 3, xp)
    u = (((w_ref[0:1, lo:lo + LANES] * x3 + w_ref[1:2, lo:lo + LANES] * x2) + w_ref[2:3, lo:lo + LANES] * x1)
         + w_ref[3:4, lo:lo + LANES] * xv) + b_ref[:, lo:lo + LANES]
    return u, x1, x2, x3


def _mix_in_fwd(x2d, mod6, g_mix, w_in_t, tm, rider=None):
    s, d = x2d.shape
    din = w_in_t.shape[0]

    def body(x_ref, mod_ref, g_ref, w_ref, hn_ref, proj_ref):
        xhat, _ = _rms(x_ref[...])
        hn = ((xhat * g_ref[...]) * (1.0 + mod_ref[1:2, :]) + mod_ref[0:1, :]).astype(BF16)
        hn_ref[...] = hn
        proj_ref[...] = _dot(hn, w_ref[...], NT)

    return _call(
        body, "mix_in_fwd", (s // tm,),
        [pl.BlockSpec((tm, d), lambda i: (i, 0)), _full(mod6.shape), _full(g_mix.shape), _full(w_in_t.shape)],
        [pl.BlockSpec((tm, d), lambda i: (i, 0)), pl.BlockSpec((tm, din), lambda i: (i, 0))],
        [jax.ShapeDtypeStruct((s, d), BF16), jax.ShapeDtypeStruct((s, din), F32)],
        [x2d, mod6, g_mix, w_in_t], rider=rider)


def _mixer_fwd(proj, conv_sc, conv_lru, conv_b, wa_bd, wx_bd, ba, bx, lam, width, rider=None):
    s, din = proj.shape
    t = min(MIX_ROWS, s)
    nblk = width // LANES
    hb = t // SUBLANES

    def body(proj_ref, projp_ref, wsc_ref, wlru_ref, blru_ref, wa_ref, wx_ref, ba_ref, bx_ref, lam_ref,
             ymix_ref, h_ref, hc_ref):
        i = pl.program_id(0)

        @pl.when(i == 0)
        def _():
            hc_ref[...] = jnp.zeros_like(hc_ref)

        has_prev = i > 0
        for j in range(nblk):
            lo = j * LANES

            def col(p, ref=proj_ref):
                return ref[:, p * width + lo:p * width + lo + LANES]

            def prev(p):
                return jnp.where(has_prev, col(p, projp_ref), 0.0)

            p = col(1) * col(2)
            q, _, _ = _conv3(p, prev(1) * prev(2), wsc_ref, lo)
            ymix_ref[:, lo:lo + LANES] = (col(0) * q).astype(BF16)

            u, _, _, _ = _conv4(col(4), prev(4), wlru_ref, blru_ref, lo)
            sp = _softplus(-lam_ref[:, lo:lo + LANES])
            _, r, ig, a, mult = _lru_gates(u, wa_ref[j], wx_ref[j], ba_ref[:, lo:lo + LANES], bx_ref[:, lo:lo + LANES], sp)
            h = _scan_fwd(a, mult * (ig * u), hc_ref[0:1, lo:lo + LANES])
            h_ref[:, lo:lo + LANES] = h
            hc_ref[0:1, lo:lo + LANES] = h[t - 1:t, :]
            gel, _ = _gelu(col(3))
            ymix_ref[:, width + lo:width + lo + LANES] = (gel * h).astype(BF16)

    small = [conv_sc, conv_lru, conv_b, wa_bd, wx_bd, ba, bx, lam]
    return _call(
        body, "mixer_fwd", (s // t,),
        [pl.BlockSpec((t, din), lambda i: (i, 0)),
         pl.BlockSpec((SUBLANES, din), lambda i: (jnp.maximum(i * hb - 1, 0), 0))]
        + [_full(a.shape) for a in small],
        [pl.BlockSpec((t, 2 * width), lambda i: (i, 0)), pl.BlockSpec((t, width), lambda i: (i, 0))],
        [jax.ShapeDtypeStruct((s, 2 * width), BF16), jax.ShapeDtypeStruct((s, width), F32)],
        [proj, proj, *small], scratch=[pltpu.VMEM((SUBLANES, width), F32)], rider=rider)


def _mix_out_fwd(ymix, x2d, w_out, mod6, g_mlp, tm, rider=None):
    s, d = x2d.shape

    def body(y_ref, x_ref, w_ref, mod_ref, g_ref, mix_ref, x2_ref, hn_ref):
        mix = _dot(y_ref[...], w_ref[...], NN)
        mix_ref[...] = mix
        x2 = x_ref[...] + mod_ref[2:3, :] * mix
        x2_ref[...] = x2
        xhat, _ = _rms(x2)
        hn_ref[...] = ((xhat * g_ref[...]) * (1.0 + mod_ref[4:5, :]) + mod_ref[3:4, :]).astype(BF16)

    tile = pl.BlockSpec((tm, d), lambda i: (i, 0))
    return _call(
        body, "mix_out_fwd", (s // tm,),
        [tile, tile, _full(w_out.shape), _full(mod6.shape), _full(g_mlp.shape)],
        [tile, tile, tile],
        [jax.ShapeDtypeStruct((s, d), F32), jax.ShapeDtypeStruct((s, d), F32), jax.ShapeDtypeStruct((s, d), BF16)],
        [ymix, x2d, w_out, mod6, g_mlp], rider=rider)


def _mlp_fwd_loss(hn2, w_up_t, w_down, x2, target, mod6, g_final, tm, tk):
    s, d = hn2.shape
    f = w_up_t.shape[0]
    nk = f // tk

    def body(hn_ref, wu_ref, wd_ref, x2_ref, t_ref, mod_ref, g_ref, z_ref, dx3_ref, dyb_ref, st_ref, y_ref):
        i, k = pl.program_id(0), pl.program_id(1)

        @pl.when(jnp.logical_and(i == 0, k == 0))
        def _():
            st_ref[...] = jnp.zeros_like(st_ref)

        z = jnp.maximum(_dot(hn_ref[...], wu_ref[...], NT), 0.0)
        z_ref[...] = z.astype(BF16)
        part = _dot((z * z).astype(BF16), wd_ref[...], NN)

        @pl.when(k == 0)
        def _():
            y_ref[...] = part

        @pl.when(k > 0)
        def _():
            y_ref[...] += part

        @pl.when(k == nk - 1)
        def _():
            gate = mod_ref[5:6, :]
            yv = y_ref[...]
            xhat, rstd = _rms(x2_ref[...] + gate * yv)
            diff = xhat * g_ref[...] - t_ref[...]
            dyo = diff * (1.0 / d)
            dx3 = _rms_bwd(dyo * g_ref[...], xhat, rstd)
            dx3_ref[...] = dx3
            dyb_ref[...] = (gate * dx3).astype(BF16)
            st_ref[0:1, :] += _colsum(dyo * xhat)
            st_ref[1:2, :] += _colsum(dx3 * yv)
            st_ref[2:3, :] += _colsum(diff * diff)

    tile = pl.BlockSpec((tm, d), lambda i, k: (i, 0))
    wblk = pl.BlockSpec((tk, d), lambda i, k: (k, 0))
    return pl.pallas_call(
        body, name="mlp_fwd_loss", grid=(s // tm, nk),
        in_specs=[tile, wblk, wblk, tile, tile, _full(mod6.shape), _full(g_final.shape)],
        out_specs=[pl.BlockSpec((tm, tk), lambda i, k: (i, k)), tile, tile, _full((SUBLANES, d))],
        out_shape=[jax.ShapeDtypeStruct((s, f), BF16), jax.ShapeDtypeStruct((s, d), F32),
                   jax.ShapeDtypeStruct((s, d), BF16), jax.ShapeDtypeStruct((SUBLANES, d), F32)],
        scratch_shapes=[pltpu.VMEM((tm, d), F32)],
        compiler_params=_params(("arbitrary", "arbitrary")),
    )(hn2, w_up_t, w_down, x2, target, mod6, g_final)


def _mlp_bwd_dx(dyb, z, w_down, w_up_t, tm, tk):
    s, d = dyb.shape
    f = z.shape[1]

    def body(dy_ref, z_ref, wd_ref, wu_ref, dz_ref, dh_ref):
        k = pl.program_id(1)
        dz = ((2.0 * z_ref[...].astype(F32)) * _dot(dy_ref[...], wd_ref[...], NT)).astype(BF16)
        dz_ref[...] = dz
        part = _dot(dz, wu_ref[...], NN)

        @pl.when(k == 0)
        def _():
            dh_ref[...] = part

        @pl.when(k > 0)
        def _():
            dh_ref[...] += part

    return pl.pallas_call(
        body, name="mlp_bwd_dx", grid=(s // tm, f // tk),
        in_specs=[pl.BlockSpec((tm, d), lambda i, k: (i, 0)), pl.BlockSpec((tm, tk), lambda i, k: (i, k)),
                  pl.BlockSpec((tk, d), lambda i, k: (k, 0)), pl.BlockSpec((tk, d), lambda i, k: (k, 0))],
        out_specs=[pl.BlockSpec((tm, tk), lambda i, k: (i, k)), pl.BlockSpec((tm, d), lambda i, k: (i, 0))],
        out_shape=[jax.ShapeDtypeStruct((s, f), BF16), jax.ShapeDtypeStruct((s, d), F32)],
        compiler_params=_params(("parallel", "arbitrary")),
    )(dyb, z, w_down, w_up_t)


def _mlp_bwd_dw(z, dz, dyb, hn2, tm, tk):
    s, d = dyb.shape
    f = z.shape[1]

    def body(z_ref, dz_ref, dy_ref, hn_ref, gd_ref, gu_ref):
        i = pl.program_id(1)

        @pl.when(i == 0)
        def _():
            gd_ref[...] = jnp.zeros_like(gd_ref)
            gu_ref[...] = jnp.zeros_like(gu_ref)

        zf = z_ref[...].astype(F32)
        gd_ref[...] += _dot((zf * zf).astype(BF16), dy_ref[...], TN)
        gu_ref[...] += _dot(dz_ref[...], hn_ref[...], TN)

    return pl.pallas_call(
        body, name="mlp_bwd_dw", grid=(f // tk, s // tm),
        in_specs=[pl.BlockSpec((tm, tk), lambda k, i: (i, k)), pl.BlockSpec((tm, tk), lambda k, i: (i, k)),
                  pl.BlockSpec((tm, d), lambda k, i: (i, 0)), pl.BlockSpec((tm, d), lambda k, i: (i, 0))],
        out_specs=[pl.BlockSpec((tk, d), lambda k, i: (k, 0)), pl.BlockSpec((tk, d), lambda k, i: (k, 0))],
        out_shape=[jax.ShapeDtypeStruct((f, d), F32), jax.ShapeDtypeStruct((f, d), F32)],
        compiler_params=_params(("parallel", "arbitrary")),
    )(z, dz, dyb, hn2)


def _mix_out_bwd(dhn2, x2, dx3, mix, ymix, w_out, mod6, g_mlp, tm, rider=None):
    s, d = x2.shape

    def body(dh_ref, x2_ref, dx3_ref, mix_ref, y_ref, w_ref, mod_ref, g_ref, dx2_ref, dym_ref, gw_ref, st_ref):
        i = pl.program_id(0)

        @pl.when(i == 0)
        def _():
            st_ref[...] = jnp.zeros_like(st_ref)
            gw_ref[...] = jnp.zeros_like(gw_ref)

        dh = dh_ref[...]
        xhat, rstd = _rms(x2_ref[...])
        dn = dh * (1.0 + mod_ref[4:5, :])
        dx2 = dx3_ref[...] + _rms_bwd(dn * g_ref[...], xhat, rstd)
        dx2_ref[...] = dx2
        st_ref[0:1, :] += _colsum(dh)
        st_ref[1:2, :] += _colsum(dh * (xhat * g_ref[...]))
        st_ref[2:3, :] += _colsum(dn * xhat)
        st_ref[3:4, :] += _colsum(dx2 * mix_ref[...])
        dmix = (mod_ref[2:3, :] * dx2).astype(BF16)
        dym_ref[...] = _dot(dmix, w_ref[...], NT)
        gw_ref[...] += _dot(y_ref[...], dmix, TN)

    tile = pl.BlockSpec((tm, d), lambda i: (i, 0))
    return _call(
        body, "mix_out_bwd", (s // tm,),
        [tile, tile, tile, tile, tile, _full(w_out.shape), _full(mod6.shape), _full(g_mlp.shape)],
        [tile, tile, _full((d, d)), _full((SUBLANES, d))],
        [jax.ShapeDtypeStruct((s, d), F32), jax.ShapeDtypeStruct((s, d), F32),
         jax.ShapeDtypeStruct((d, d), F32), jax.ShapeDtypeStruct((SUBLANES, d), F32)],
        [dhn2, x2, dx3, mix, ymix, w_out, mod6, g_mlp], rider=rider)


def _mixer_bwd(proj, dymix, h_all, conv_sc, conv_lru, conv_b, wa_bd, wx_bd, ba, bx, lam, width, rider=None):
    s, din = proj.shape
    t = min(MIX_ROWS, s)
    nt = s // t
    nblk = width // LANES
    hb = t // SUBLANES
    last8 = s // SUBLANES - 1

    def body(proj_ref, projp_ref, projn_ref, dy_ref, dyn_ref, h_ref, hp_ref,
             wsc_ref, wlru_ref, blru_ref, wa_ref, wx_ref, ba_ref, bx_ref, lam_ref,
             dproj_ref, small_ref, gwa_ref, gwx_ref, an_ref, gn_ref, dun_ref):
        i = pl.program_id(0)

        @pl.when(i == 0)
        def _():
            small_ref[...] = jnp.zeros_like(small_ref)
            gwa_ref[...] = jnp.zeros_like(gwa_ref)
            gwx_ref[...] = jnp.zeros_like(gwx_ref)
            an_ref[...] = jnp.zeros_like(an_ref)
            gn_ref[...] = jnp.zeros_like(gn_ref)
            dun_ref[...] = jnp.zeros_like(dun_ref)

        has_prev = i < nt - 1
        has_next = i > 0
        for j in range(nblk):
            lo = j * LANES
            ls = slice(lo, lo + LANES)

            def col(p, ref=proj_ref):
                return ref[:, p * width + lo:p * width + lo + LANES]

            def prev(p):
                return jnp.where(has_prev, col(p, projp_ref), 0.0)

            def nxt(p):
                return jnp.where(has_next, col(p, projn_ref), 0.0)

            def add_row(r, v):
                small_ref[r:r + 1, ls] += _colsum(v)

            sc_b, sc_c, sc_x = col(0), col(1), col(2)
            p = sc_c * sc_x
            q, p1, p2 = _conv3(p, prev(1) * prev(2), wsc_ref, lo)
            dys = dy_ref[:, ls]
            dproj_ref[:, ls] = (dys * q).astype(BF16)
            dq = dys * sc_b
            dqn = jnp.where(has_next, dyn_ref[:, ls], 0.0) * nxt(0)
            dp = (wsc_ref[2:3, ls] * dq + wsc_ref[1:2, ls] * _shift_up(dq, 1, dqn)) + wsc_ref[0:1, ls] * _shift_up(dq, 2, dqn)
            dproj_ref[:, width + lo:width + lo + LANES] = (dp * sc_x).astype(BF16)
            dproj_ref[:, 2 * width + lo:2 * width + lo + LANES] = (dp * sc_c).astype(BF16)
            add_row(0, dq * p2)
            add_row(1, dq * p1)
            add_row(2, dq * p)

            xv = col(4)
            u, x1, x2, x3 = _conv4(xv, prev(4), wlru_ref, blru_ref, lo)
            lam_v = lam_ref[:, ls]
            sp = _softplus(-lam_v)
            wa, wx = wa_ref[j], wx_ref[j]
            ub, r, ig, a, mult = _lru_gates(u, wa, wx, ba_ref[:, ls], bx_ref[:, ls], sp)
            iu = ig * u
            h = h_ref[:, ls]
            hm1 = _shift_down(h, 1, jnp.where(has_prev, hp_ref[:, ls], 0.0))
            lyv = col(3)
            gel, th = _gelu(lyv)
            dyl = dy_ref[:, width + lo:width + lo + LANES]
            dproj_ref[:, 3 * width + lo:3 * width + lo + LANES] = (dyl * h * _dgelu(lyv, th)).astype(BF16)
            a_next = jnp.broadcast_to(an_ref[0:1, ls], (SUBLANES, LANES))
            g = _scan_rev(_shift_up(a, 1, a_next), dyl * gel, gn_ref[0:1, ls])
            an_ref[0:1, ls] = a[0:1, :]
            gn_ref[0:1, ls] = g[0:1, :]
            da = g * hm1
            dmult = g * iu
            diu = g * mult
            dlog_a = da * a - dmult * ((a * a) / mult)
            dpre_a = (dlog_a * (-RG_C * sp)) * (r * (1.0 - r))
            dpre_x = (diu * u) * (ig * (1.0 - ig))
            dab, dxb = dpre_a.astype(BF16), dpre_x.astype(BF16)
            du = diu * ig + _dot(dab, wa, NT) + _dot(dxb, wx, NT)
            gwa_ref[j] += _dot(ub, dab, TN)
            gwx_ref[j] += _dot(ub, dxb, TN)
            dun = dun_ref[:, ls]
            dun_ref[:, ls] = du[0:SUBLANES, :]
            dlx = (((wlru_ref[3:4, ls] * du + wlru_ref[2:3, ls] * _shift_up(du, 1, dun))
                    + wlru_ref[1:2, ls] * _shift_up(du, 2, dun)) + wlru_ref[0:1, ls] * _shift_up(du, 3, dun))
            dproj_ref[:, 4 * width + lo:4 * width + lo + LANES] = dlx.astype(BF16)
            add_row(3, du * x3)
            add_row(4, du * x2)
            add_row(5, du * x1)
            add_row(6, du * xv)
            add_row(7, du)
            add_row(8, dpre_a)
            add_row(9, dpre_x)
            add_row(10, (dlog_a * (RG_C * r)) * jax.nn.sigmoid(-lam_v))

    small = [conv_sc, conv_lru, conv_b, wa_bd, wx_bd, ba, bx, lam]
    rev = lambda i: nt - 1 - i
    return _call(
        body, "mixer_bwd", (nt,),
        [pl.BlockSpec((t, din), lambda i: (rev(i), 0)),
         pl.BlockSpec((SUBLANES, din), lambda i: (jnp.maximum(rev(i) * hb - 1, 0), 0)),
         pl.BlockSpec((SUBLANES, din), lambda i: (jnp.minimum((rev(i) + 1) * hb, last8), 0)),
         pl.BlockSpec((t, 2 * width), lambda i: (rev(i), 0)),
         pl.BlockSpec((SUBLANES, 2 * width), lambda i: (jnp.minimum((rev(i) + 1) * hb, last8), 0)),
         pl.BlockSpec((t, width), lambda i: (rev(i), 0)),
         pl.BlockSpec((SUBLANES, width), lambda i: (jnp.maximum(rev(i) * hb - 1, 0), 0))]
        + [_full(a.shape) for a in small],
        [pl.BlockSpec((t, din), lambda i: (rev(i), 0)), _full((2 * SUBLANES, width)),
         _full(wa_bd.shape), _full(wx_bd.shape)],
        [jax.ShapeDtypeStruct((s, din), BF16), jax.ShapeDtypeStruct((2 * SUBLANES, width), F32),
         jax.ShapeDtypeStruct(wa_bd.shape, F32), jax.ShapeDtypeStruct(wx_bd.shape, F32)],
        [proj, proj, proj, dymix, dymix, h_all, h_all, *small],
        scratch=[pltpu.VMEM((SUBLANES, width), F32), pltpu.VMEM((SUBLANES, width), F32),
                 pltpu.VMEM((SUBLANES, width), F32)], rider=rider)


def _mix_in_bwd_dx(dproj, x2d, dx2, w_in_t, mod6, g_mix, tm, rider=None):
    s, d = x2d.shape
    din = dproj.shape[1]

    def body(dp_ref, x_ref, dx2_ref, w_ref, mod_ref, g_ref, gx_ref, st_ref):
        i = pl.program_id(0)

        @pl.when(i == 0)
        def _():
            st_ref[...] = jnp.zeros_like(st_ref)

        dh = _dot(dp_ref[...], w_ref[...], NN)
        xhat, rstd = _rms(x_ref[...])
        dn = dh * (1.0 + mod_ref[1:2, :])
        gx_ref[...] = dx2_ref[...] + _rms_bwd(dn * g_ref[...], xhat, rstd)
        st_ref[0:1, :] += _colsum(dh)
        st_ref[1:2, :] += _colsum(dh * (xhat * g_ref[...]))
        st_ref[2:3, :] += _colsum(dn * xhat)

    tile = pl.BlockSpec((tm, d), lambda i: (i, 0))
    return _call(
        body, "mix_in_bwd_dx", (s // tm,),
        [pl.BlockSpec((tm, din), lambda i: (i, 0)), tile, tile, _full(w_in_t.shape), _full(mod6.shape),
         _full(g_mix.shape)],
        [tile, _full((SUBLANES, d))],
        [jax.ShapeDtypeStruct((s, d), F32), jax.ShapeDtypeStruct((SUBLANES, d), F32)],
        [dproj, x2d, dx2, w_in_t, mod6, g_mix], rider=rider)


def _mix_in_bwd_dw(dproj, hn1, tm, tn, rider=None):
    s, d = hn1.shape
    din = dproj.shape[1]

    def body(dp_ref, hn_ref, gw_ref):
        i = pl.program_id(1)

        @pl.when(i == 0)
        def _():
            gw_ref[...] = jnp.zeros_like(gw_ref)

        gw_ref[...] += _dot(dp_ref[...], hn_ref[...], TN)

    return _call(
        body, "mix_in_bwd_dw", (din // tn, s // tm),
        [pl.BlockSpec((tm, tn), lambda p, i: (i, p)), pl.BlockSpec((tm, d), lambda p, i: (i, 0))],
        [pl.BlockSpec((tn, d), lambda p, i: (p, 0))],
        [jax.ShapeDtypeStruct((din, d), F32)],
        [dproj, hn1], rider=rider)


def _adamw(w, g, m, v):
    m = ADAM_B1 * m + (1.0 - ADAM_B1) * g
    v = ADAM_B2 * v + (1.0 - ADAM_B2) * (g * g)
    m_hat = m / (1.0 - ADAM_B1 ** ADAM_STEP)
    v_hat = v / (1.0 - ADAM_B2 ** ADAM_STEP)
    delta = -ADAM_LR * (m_hat / (jnp.sqrt(v_hat) + ADAM_EPS) + ADAM_WD * w)
    return delta, m, v


def _pair_sum(g4, h4, core_chip, tr, name):
    _, _, r, n = g4.shape

    def body(sc_ref, g_ref, h_ref, sb_ref, own_ref):
        q = pl.program_id(1)
        ssum = g_ref[...] + h_ref[...]
        sb_ref[...] = ssum.astype(BF16)

        @pl.when(q == sc_ref[1])
        def _():
            own_ref[...] = ssum

    grid_spec = pltpu.PrefetchScalarGridSpec(
        num_scalar_prefetch=1, grid=(r // tr, 4),
        in_specs=[pl.BlockSpec((None, None, tr, n), lambda i, q, sc: (q, sc[0], i, 0)),
                  pl.BlockSpec((None, tr, n), lambda i, q, sc: (q, i, 0))],
        out_specs=[pl.BlockSpec((None, tr, n), lambda i, q, sc: (q, i, 0)),
                   pl.BlockSpec((tr, n), lambda i, q, sc: (i, 0))])
    return pl.pallas_call(
        body, name=name, grid_spec=grid_spec,
        out_shape=[jax.ShapeDtypeStruct((4, r, n), BF16), jax.ShapeDtypeStruct((r, n), F32)],
        compiler_params=_params(("parallel", "arbitrary")),
    )(core_chip, g4, h4)


def _sum4(own, parts, tr, name):
    r, n = own.shape

    def body(o_ref, p_ref, out_ref):
        acc = o_ref[...]
        for k in range(3):
            acc = acc + p_ref[k].astype(F32)
        out_ref[...] = acc

    return pl.pallas_call(
        body, name=name, grid=(r // tr,),
        in_specs=[pl.BlockSpec((tr, n), lambda i: (i, 0)), pl.BlockSpec((3, tr, n), lambda i: (0, i, 0))],
        out_specs=pl.BlockSpec((tr, n), lambda i: (i, 0)),
        out_shape=jax.ShapeDtypeStruct((r, n), F32),
        compiler_params=_params(("parallel",)),
    )(own, parts)


def _sum8(parts, tr, name):
    _, rows, n = parts.shape

    def body(p_ref, o_ref):
        acc = p_ref[0]
        for k in range(1, N_DEV):
            acc = acc + p_ref[k]
        o_ref[...] = acc

    return pl.pallas_call(
        body, name=name, grid=(rows // tr,),
        in_specs=[pl.BlockSpec((N_DEV, tr, n), lambda i: (0, i, 0))],
        out_specs=pl.BlockSpec((tr, n), lambda i: (i, 0)),
        out_shape=jax.ShapeDtypeStruct((rows, n), F32),
        compiler_params=_params(("parallel",)),
    )(parts)


def _adam_rows(w, g, m, v, tr, name):
    rows, n = w.shape

    def body(w_ref, g_ref, m_ref, v_ref, d_ref, nm_ref, nv_ref):
        d_ref[...], nm_ref[...], nv_ref[...] = _adamw(w_ref[...], g_ref[...], m_ref[...], v_ref[...])

    tile = pl.BlockSpec((tr, n), lambda i: (i, 0))
    return pl.pallas_call(
        body, name=name, grid=(rows // tr,),
        in_specs=[tile] * 4, out_specs=[tile] * 3,
        out_shape=[jax.ShapeDtypeStruct((rows, n), F32)] * 3,
        compiler_params=_params(("parallel",)),
    )(w, g, m, v)


def _ada_bwd_adam(cact_t, dmod_cols, w, m, v, tr):
    rows, n = w.shape

    def body(c_ref, d_ref, w_ref, m_ref, v_ref, g_ref, dl_ref, nm_ref, nv_ref):
        def term(b):
            return c_ref[b].astype(BF16).astype(F32) * d_ref[b:b + 1, :].astype(BF16).astype(F32)

        g = term(0)
        for b in range(1, N_DEV):
            g = g + term(b)
        g_ref[...] = g
        dl_ref[...], nm_ref[...], nv_ref[...] = _adamw(w_ref[...], g, m_ref[...], v_ref[...])

    tile = pl.BlockSpec((tr, n), lambda i: (i, 0))
    return pl.pallas_call(
        body, name="ada_bwd_adam", grid=(rows // tr,),
        in_specs=[pl.BlockSpec((N_DEV, tr, 1), lambda i: (0, i, 0)), _full(dmod_cols.shape), tile, tile, tile],
        out_specs=[tile] * 4,
        out_shape=[jax.ShapeDtypeStruct((rows, n), F32)] * 4,
        compiler_params=_params(("parallel",)),
    )(cact_t, dmod_cols, w, m, v)


def _adam_small(ws, gs, ms, vs):
    n = len(ws)

    def body(*refs):
        w_r, g_r, m_r, v_r = refs[:n], refs[n:2 * n], refs[2 * n:3 * n], refs[3 * n:4 * n]
        d_r, nm_r, nv_r = refs[4 * n:5 * n], refs[5 * n:6 * n], refs[6 * n:7 * n]
        for k in range(n):
            d_r[k][...], nm_r[k][...], nv_r[k][...] = _adamw(w_r[k][...], g_r[k][...], m_r[k][...], v_r[k][...])

    shapes = [jax.ShapeDtypeStruct(w.shape, F32) for w in ws]
    outs = pl.pallas_call(
        body, name="adam_small", out_shape=shapes * 3, compiler_params=_params(),
    )(*ws, *gs, *ms, *vs)
    return outs[:n], outs[n:2 * n], outs[2 * n:]


def _block_diag(w):
    h, hd, _ = w.shape
    per = LANES // hd
    eye = jnp.eye(per, dtype=w.dtype)
    w5 = w.reshape(h // per, per, hd, 1, hd) * eye[None, :, None, :, None]
    return w5.reshape(h // per, LANES, LANES)


def _block_diag_grad(g, h, hd):
    per = LANES // hd
    g5 = g.reshape(h // per, per, hd, per, hd)
    return jnp.stack([g5[:, a, :, a, :] for a in range(per)], axis=1).reshape(h, hd, hd)


def kernel(x, c, w_ada, b_ada, g_mix, w_in, conv_w_sc, conv_w_lru, conv_b_lru, w_rg_a, b_rg_a, w_rg_x, b_rg_x, lru_lambda, w_out, g_mlp, w_up, w_down, g_final, loss_target, m_w_ada, m_b_ada, m_g_mix, m_w_in, m_conv_w_sc, m_conv_w_lru, m_conv_b_lru, m_w_rg_a, m_b_rg_a, m_w_rg_x, m_b_rg_x, m_lru_lambda, m_w_out, m_g_mlp, m_w_up, m_w_down, m_g_final, v_w_ada, v_b_ada, v_g_mix, v_w_in, v_conv_w_sc, v_conv_w_lru, v_conv_b_lru, v_w_rg_a, v_b_rg_a, v_w_rg_x, v_b_rg_x, v_lru_lambda, v_w_out, v_g_mlp, v_w_up, v_w_down, v_g_final):
    s, d = x.shape[1], x.shape[2]
    width = conv_b_lru.shape[1]
    heads, hd = w_rg_a.shape[1], w_rg_a.shape[2]
    f = w_down.shape[1] * N_DEV
    n_ada = w_ada.shape[2]
    csh = conv_w_sc.shape[2]
    me = 4 * lax.axis_index("x") + 2 * lax.axis_index("y") + lax.axis_index("c")
    tm = min(512, s)
    tm_mlp = min(1024, s)
    tk = 512

    x2d = x[0]
    tgt = loss_target[0]

    pay = jnp.zeros((SUBLANES, d), F32)
    pay = pay.at[0:1, :].set(c)
    pay = pay.at[1:4, 0:csh].set(conv_w_sc[0])
    pay = pay.at[4:8, 0:csh].set(conv_w_lru[0])
    w_in_t_sh = w_in[0].T.astype(BF16)
    w_up_t_sh = w_up[0].T.astype(BF16)
    w_out_sh = w_out[0].astype(BF16)
    w_down_sh = w_down[0].astype(BF16)
    pay_all, w_in_t = _gather2("gather_in", [pay, w_in_t_sh])
    w_in_t = w_in_t.reshape(-1, d)
    c_all = pay_all[:, 0, :]
    conv_sc = pay_all[:, 1:4, 0:csh].transpose(1, 0, 2).reshape(3, width)
    conv_lru = pay_all[:, 4:8, 0:csh].transpose(1, 0, 2).reshape(4, width)

    b_ada_sh = lax.dynamic_slice(b_ada, (0, me * n_ada), (1, n_ada))
    mod_cols, c_act = _ada_fwd(c_all, w_ada[0], b_ada_sh)
    (mod_rows,) = _exchange("scatter_mod", [], [mod_cols.reshape(N_DEV, 1, n_ada)])
    mod_rows, w_out_sh, w_up_t_sh, w_down_sh = lax.optimization_barrier((mod_rows, w_out_sh, w_up_t_sh, w_down_sh))
    (w_out_g,) = _seq_gather2("gather_w_out", 1, [w_out_sh])
    w_up_g, w_down_g = _seq_gather2("gather_mlp_weights", 2, [w_up_t_sh, w_down_sh])
    mod6 = jnp.zeros((SUBLANES, d), F32).at[0:6, :].set(mod_rows.reshape(6, d))

    wa_bd = _block_diag(w_rg_a[0]).astype(BF16)
    wx_bd = _block_diag(w_rg_x[0]).astype(BF16)
    ba = b_rg_a.reshape(1, width)
    bx = b_rg_x.reshape(1, width)
    g_fin = g_final.reshape(1, d)

    (hn1, proj), _ = _mix_in_fwd(x2d, mod6, g_mix, w_in_t, tm)
    (ymix, h_all), _ = _mixer_fwd(proj, conv_sc, conv_lru, conv_b_lru, wa_bd, wx_bd, ba, bx, lru_lambda, width)
    w_out_b = w_out_g.reshape(-1, d)
    (mix, x2, hn2), _ = _mix_out_fwd(ymix, x2d, w_out_b, mod6, g_mlp, tm)
    w_up_t = w_up_g.reshape(-1, d)
    w_down_b = w_down_g.reshape(-1, d)
    z, dx3, dyb, st_fin = _mlp_fwd_loss(hn2, w_up_t, w_down_b, x2, tgt, mod6, g_fin, tm, 4 * tk)

    core_chip = jnp.stack([lax.axis_index("c"), 2 * lax.axis_index("x") + lax.axis_index("y")]).astype(jnp.int32)
    dz, dhn2 = _mlp_bwd_dx(dyb, z, w_down_b, w_up_t, tm, 4 * tk)
    g_down, g_up_t = _mlp_bwd_dw(z, dz, dyb, hn2, tm_mlp, 2 * tk)
    g_up4, g_down4 = g_up_t.reshape(4, 2, -1, d), g_down.reshape(4, 2, -1, d)
    (dx2, dymix, g_out, st_out), (h_up, h_down) = _mix_out_bwd(
        dhn2, x2, dx3, mix, ymix, w_out_b, mod6, g_mlp, tm, rider=_ride_pair_swap([g_up4, g_down4]))
    sb_up, own_up = _pair_sum(g_up4, h_up, core_chip, 256, "pair_sum_w_up")
    sb_down, own_down = _pair_sum(g_down4, h_down, core_chip, 256, "pair_sum_w_down")
    g_out4 = g_out.reshape(4, 2, -1, d)
    p_up, p_down = _seq_chip_exchange("exchange_mlp_grads", 3, [sb_up, sb_down])
    (dproj, g_small, g_wa, g_wx), (h_out,) = _mixer_bwd(
        proj, dymix, h_all, conv_sc, conv_lru, conv_b_lru, wa_bd, wx_bd, ba, bx, lru_lambda, width,
        rider=_ride_pair_swap([g_out4]))
    sb_out, own_out = _pair_sum(g_out4, h_out, core_chip, g_out4.shape[2], "pair_sum_w_out")
    (p_out,) = _seq_chip_exchange("exchange_w_out_grad", 4, [sb_out])
    (grad_x, st_in), _ = _mix_in_bwd_dx(dproj, x2d, dx2, w_in_t, mod6, g_mix, tm)

    small = jnp.concatenate([
        st_in[0:2], st_out[3:4], st_out[0:2], st_fin[1:2],
        st_in[2:3], st_out[2:3], st_fin[0:1],
        jnp.concatenate([g_small[7:8], g_small[10:11]], axis=1),
        jnp.concatenate([g_small[8:9], g_small[9:10]], axis=1),
        jnp.concatenate([jnp.concatenate([g_small[0:3], jnp.zeros((1, width), F32)], axis=0), g_small[3:7]], axis=1),
        st_fin[2:3],
        _block_diag_grad(g_wa, heads, hd).reshape(-1, d),
        _block_diag_grad(g_wx, heads, hd).reshape(-1, d),
    ], axis=0)

    (small_all,) = _seq_gather2("gather_small_grads", 5, [small])
    (g_in_t,), _ = _mix_in_bwd_dw(dproj, hn1, tm_mlp, 512)
    g_in4 = g_in_t.reshape(4, 2, -1, d)
    (h_in,) = _comm("swap_w_in", _ride_pair_swap([g_in4]))
    sb_in, own_in = _pair_sum(g_in4, h_in, core_chip, g_in4.shape[2], "pair_sum_w_in")
    (p_in,) = _seq_chip_exchange("exchange_w_in_grad", 6, [sb_in])
    p_up, p_down, p_out, small_all, sb_in = lax.optimization_barrier((p_up, p_down, p_out, small_all, sb_in))

    gs_up_t = _sum4(own_up, p_up, 256, "sum_w_up")
    gs_up = gs_up_t.T
    gs_out = _sum4(own_out, p_out, own_out.shape[0], "sum_w_out")
    gs_down = _sum4(own_down, p_down, 256, "sum_w_down")
    ad_up = _adam_rows(w_up[0], gs_up, m_w_up[0], v_w_up[0], 256, "adam_w_up")
    ad_out = _adam_rows(w_out[0], gs_out, m_w_out[0], v_w_out[0], w_out.shape[1], "adam_w_out")
    ad_down = _adam_rows(w_down[0], gs_down, m_w_down[0], v_w_down[0], 256, "adam_w_down")

    gsum = _sum8(small_all, SMALL_ROWS, "sum_small")
    p_in, gsum, gs_up_t = lax.optimization_barrier((p_in, gsum, gs_up_t))
    gs_in = _sum4(own_in, p_in, own_in.shape[0], "sum_w_in").T
    ad_in = _adam_rows(w_in[0], gs_in, m_w_in[0], v_w_in[0], 256, "adam_w_in")
    loss = (0.5 / d) * jnp.sum(gsum[15])
    dmod_cols = lax.dynamic_slice(small_all[:, 0:6, :].reshape(N_DEV, 6 * d), (0, me * n_ada), (N_DEV, n_ada))
    g_ada, d_ada, nm_ada, nv_ada = _ada_bwd_adam(c_act[:, :, None], dmod_cols, w_ada[0], m_w_ada[0], v_w_ada[0], 256)

    g_conv = lax.dynamic_slice(gsum[11:15, 0:width], (0, me * csh), (4, csh))
    g_conv_l = lax.dynamic_slice(gsum[11:15, width:2 * width], (0, me * csh), (4, csh))
    small_g = [
        gsum[0:6].reshape(1, 6 * d),
        gsum[6:7],
        g_conv[0:3].reshape(1, 3, csh),
        g_conv_l.reshape(1, 4, csh),
        gsum[9:10, 0:width],
        gsum[16:48].reshape(1, heads, hd, hd),
        gsum[10:11, 0:width].reshape(1, heads, hd),
        gsum[48:80].reshape(1, heads, hd, hd),
        gsum[10:11, width:].reshape(1, heads, hd),
        gsum[9:10, width:],
        gsum[7:8],
        gsum[8],
    ]
    small_w = [b_ada, g_mix, conv_w_sc, conv_w_lru, conv_b_lru, w_rg_a, b_rg_a, w_rg_x, b_rg_x, lru_lambda, g_mlp, g_final]
    small_m = [m_b_ada, m_g_mix, m_conv_w_sc, m_conv_w_lru, m_conv_b_lru, m_w_rg_a, m_b_rg_a, m_w_rg_x, m_b_rg_x,
               m_lru_lambda, m_g_mlp, m_g_final]
    small_v = [v_b_ada, v_g_mix, v_conv_w_sc, v_conv_w_lru, v_conv_b_lru, v_w_rg_a, v_b_rg_a, v_w_rg_x, v_b_rg_x,
               v_lru_lambda, v_g_mlp, v_g_final]
    sd, snm, snv = _adam_small(small_w, small_g, small_m, small_v)

    def order(ada, w_in_, w_out_, w_up_, w_down_, sm):
        return [ada[None], sm[0], sm[1], w_in_[None], sm[2], sm[3], sm[4], sm[5], sm[6], sm[7], sm[8], sm[9],
                w_out_[None], sm[10], w_up_[None], w_down_[None], sm[11]]

    grads = order(g_ada, gs_in, gs_out, gs_up, gs_down, small_g)
    deltas = order(d_ada, ad_in[0], ad_out[0], ad_up[0], ad_down[0], sd)
    new_m = order(nm_ada, ad_in[1], ad_out[1], ad_up[1], ad_down[1], snm)
    new_v = order(nv_ada, ad_in[2], ad_out[2], ad_up[2], ad_down[2], snv)
    return (loss, grad_x[None], *grads, *deltas, *new_m, *new_v)
```

```python
import functools

import jax
import jax.numpy as jnp
from jax import lax
from jax.experimental import pallas as pl
from jax.experimental.pallas import tpu as pltpu
from jax.experimental.pallas import tpu_sc as plsc

F32 = jnp.float32
BF16 = jnp.bfloat16
N_DEV = 8
EPS = 1e-6
RG_C = 8.0
GELU_K0 = 0.7978845608028654
GELU_K1 = 0.044715
ADAM_LR = 0.001
ADAM_B1 = 0.9
ADAM_B2 = 0.999
ADAM_EPS = 1e-08
ADAM_WD = 0.01
ADAM_STEP = 10
LANES = 128
SUBLANES = 8
VMEM_LIMIT = 52 * 1024 * 1024
MIX_ROWS = 256
SMALL_ROWS = 80

MESH = pl.DeviceIdType.MESH
ANY = pl.BlockSpec(memory_space=pl.ANY)
NN = ((1,), (0,))
NT = ((1,), (1,))
TN = ((0,), (0,))


def _dot(a, b, dims):
    return lax.dot_general(a, b, (dims, ((), ())), preferred_element_type=F32)


def _params(sem=None):
    return pltpu.CompilerParams(dimension_semantics=sem, vmem_limit_bytes=VMEM_LIMIT)


def _full(shape):
    nd = len(shape)
    return pl.BlockSpec(shape, lambda *_: (0,) * nd)


def _exchange(name, gathers, scatters):
    n_g = len(gathers)
    arrs = list(gathers) + list(scatters)
    n = len(arrs)
    out_shape = [jax.ShapeDtypeStruct((N_DEV,) + a.shape, a.dtype) for a in gathers]
    out_shape += [jax.ShapeDtypeStruct(a.shape, a.dtype) for a in scatters]

    def body(*refs):
        ins, outs = refs[:n], refs[n:2 * n]
        send_sems, recv_sems, local_sems = refs[2 * n:]
        x, y, c = lax.axis_index("x"), lax.axis_index("y"), lax.axis_index("c")
        me = 4 * x + 2 * y + c

        def src(a, dev):
            return ins[a] if a < n_g else ins[a].at[dev]

        def peer_of(k):
            px = 1 - x if (k >> 2) & 1 else x
            py = 1 - y if (k >> 1) & 1 else y
            pc = 1 - c if k & 1 else c
            return (px, py, pc), 4 * px + 2 * py + pc

        local = [pltpu.make_async_copy(src(a, me), outs[a].at[me], local_sems.at[a]) for a in range(n)]
        for cp in local:
            cp.start()
        sends = []
        for k in range(1, N_DEV):
            peer, pidx = peer_of(k)
            for a in range(n):
                cp = pltpu.make_async_remote_copy(
                    src_ref=src(a, pidx), dst_ref=outs[a].at[me],
                    send_sem=send_sems.at[a * (N_DEV - 1) + k - 1], recv_sem=recv_sems.at[a * (N_DEV - 1) + k - 1],
                    device_id=peer, device_id_type=MESH)
                cp.start()
                sends.append(cp)
        for k in range(1, N_DEV):
            peer, pidx = peer_of(k)
            for a in range(n):
                pltpu.make_async_remote_copy(
                    src_ref=src(a, pidx), dst_ref=outs[a].at[pidx],
                    send_sem=send_sems.at[a * (N_DEV - 1) + k - 1], recv_sem=recv_sems.at[a * (N_DEV - 1) + k - 1],
                    device_id=peer, device_id_type=MESH).wait_recv()
        for cp in sends:
            cp.wait_send()
        for cp in local:
            cp.wait()

    return pl.pallas_call(
        body, name=name, out_shape=out_shape,
        in_specs=[ANY] * n, out_specs=[ANY] * n,
        scratch_shapes=[pltpu.SemaphoreType.DMA((n * (N_DEV - 1),)),
                        pltpu.SemaphoreType.DMA((n * (N_DEV - 1),)),
                        pltpu.SemaphoreType.DMA((n,))],
    )(*arrs)


def _gather2(name, arrs):
    n = len(arrs)
    per = 7
    out_shape = [jax.ShapeDtypeStruct((N_DEV,) + a.shape, a.dtype) for a in arrs]

    def body(*refs):
        ins, outs = refs[:n], refs[n:2 * n]
        send_sems, recv_sems, local_sems = refs[2 * n:]
        x, y, c = lax.axis_index("x"), lax.axis_index("y"), lax.axis_index("c")
        sib = (x, y, 1 - c)
        chips = [(1 - x, y), (x, 1 - y), (1 - x, 1 - y)]

        def slot(a, px, py, pc):
            return outs[a].at[4 * px + 2 * py + pc]

        def copy(a, k, block, to, src=None):
            return pltpu.make_async_remote_copy(
                src_ref=slot(a, *block) if src is None else src, dst_ref=slot(a, *block),
                send_sem=send_sems.at[a * per + k], recv_sem=recv_sems.at[a * per + k],
                device_id=to, device_id_type=MESH)

        local = [pltpu.make_async_copy(ins[a], slot(a, x, y, c), local_sems.at[a]) for a in range(n)]
        for cp in local:
            cp.start()
        first = []
        for a in range(n):
            first += [copy(a, 1 + j, (x, y, c), (*chip, c), src=ins[a]) for j, chip in enumerate(chips)]
        for a in range(n):
            first.append(copy(a, 0, (x, y, c), sib, src=ins[a]))
        for cp in first:
            cp.start()
        passed = []
        for a in range(n):
            for j, chip in enumerate(chips):
                copy(a, 1 + j, (*chip, c), (x, y, c)).wait_recv()
                cp = copy(a, 4 + j, (*chip, c), sib)
                cp.start()
                passed.append(cp)
        for a in range(n):
            copy(a, 0, sib, (x, y, c)).wait_recv()
            for j, chip in enumerate(chips):
                copy(a, 4 + j, (*chip, 1 - c), (x, y, c)).wait_recv()
        for cp in first + passed:
            cp.wait_send()
        for cp in local:
            cp.wait()

    return pl.pallas_call(
        body, name=name, out_shape=out_shape,
        in_specs=[ANY] * n, out_specs=[ANY] * n,
        scratch_shapes=[pltpu.SemaphoreType.DMA((n * per,)), pltpu.SemaphoreType.DMA((n * per,)),
                        pltpu.SemaphoreType.DMA((n,))],
    )(*arrs)


def _seq_gather2(name, collective_id, arrs):
    n = len(arrs)
    per = 7

    def body(*refs):
        ins, outs = refs[:n], refs[n:2 * n]
        send_sems, recv_sems, local_sems = refs[2 * n:]
        x, y, c = lax.axis_index("x"), lax.axis_index("y"), lax.axis_index("c")
        sib = (x, y, 1 - c)
        chips = [(1 - x, y), (x, 1 - y), (1 - x, 1 - y)]
        barrier = pltpu.get_barrier_semaphore()
        for peer in [sib] + [(*chip, c) for chip in chips]:
            pl.semaphore_signal(barrier, inc=1, device_id=peer, device_id_type=MESH)
        pl.semaphore_wait(barrier, 4)

        def slot(a, px, py, pc):
            return outs[a].at[4 * px + 2 * py + pc]

        def copy(a, k, block, to, src=None):
            return pltpu.make_async_remote_copy(
                src_ref=slot(a, *block) if src is None else src, dst_ref=slot(a, *block),
                send_sem=send_sems.at[a * per + k], recv_sem=recv_sems.at[a * per + k],
                device_id=to, device_id_type=MESH)

        local = [pltpu.make_async_copy(ins[a], slot(a, x, y, c), local_sems.at[a]) for a in range(n)]
        for cp in local:
            cp.start()
        first = []
        for a in range(n):
            first += [copy(a, 1 + j, (x, y, c), (*chip, c), src=ins[a]) for j, chip in enumerate(chips)]
        for a in range(n):
            first.append(copy(a, 0, (x, y, c), sib, src=ins[a]))
        for cp in first:
            cp.start()
        passed = []
        for a in range(n):
            for j, chip in enumerate(chips):
                copy(a, 1 + j, (*chip, c), (x, y, c)).wait_recv()
                cp = copy(a, 4 + j, (*chip, c), sib)
                cp.start()
                passed.append(cp)
        for a in range(n):
            copy(a, 0, sib, (x, y, c)).wait_recv()
            for j, chip in enumerate(chips):
                copy(a, 4 + j, (*chip, 1 - c), (x, y, c)).wait_recv()
        for cp in first + passed:
            cp.wait_send()
        for cp in local:
            cp.wait()

    return pl.kernel(
        body, out_type=[jax.ShapeDtypeStruct((N_DEV,) + a.shape, a.dtype) for a in arrs],
        mesh=plsc.ScalarSubcoreMesh(axis_name="seq", num_cores=1),
        scratch_types=[pltpu.SemaphoreType.DMA((n * per,)), pltpu.SemaphoreType.DMA((n * per,)),
                       pltpu.SemaphoreType.DMA((n,))],
        compiler_params=pltpu.CompilerParams(collective_id=collective_id), name=name,
    )(*arrs)


def _seq_chip_exchange(name, collective_id, arrs):
    n = len(arrs)

    def body(*refs):
        ins, outs = refs[:n], refs[n:2 * n]
        send_sems, recv_sems = refs[2 * n:]
        x, y, c = lax.axis_index("x"), lax.axis_index("y"), lax.axis_index("c")

        def peer(k):
            return (1 - x if (k >> 1) & 1 else x), (1 - y if k & 1 else y)

        barrier = pltpu.get_barrier_semaphore()
        for k in (1, 2, 3):
            pl.semaphore_signal(barrier, inc=1, device_id=(*peer(k), c), device_id_type=MESH)
        pl.semaphore_wait(barrier, 3)

        def copy(a, k):
            px, py = peer(k)
            return pltpu.make_async_remote_copy(
                src_ref=ins[a].at[2 * px + py], dst_ref=outs[a].at[k - 1],
                send_sem=send_sems.at[a * 3 + k - 1], recv_sem=recv_sems.at[a * 3 + k - 1],
                device_id=(px, py, c), device_id_type=MESH)

        cps = [copy(a, k) for a in range(n) for k in (1, 2, 3)]
        for cp in cps:
            cp.start()
        for cp in cps:
            cp.wait_recv()
        for cp in cps:
            cp.wait_send()

    return pl.kernel(
        body, out_type=[jax.ShapeDtypeStruct((3,) + a.shape[1:], a.dtype) for a in arrs],
        mesh=plsc.ScalarSubcoreMesh(axis_name="seq", num_cores=1),
        scratch_types=[pltpu.SemaphoreType.DMA((n * 3,)), pltpu.SemaphoreType.DMA((n * 3,))],
        compiler_params=pltpu.CompilerParams(collective_id=collective_id), name=name,
    )(*arrs)


def _seq_pair_swap(name, collective_id, arrs):
    n = len(arrs)

    def body(*refs):
        ins, outs = refs[:n], refs[n:2 * n]
        send_sems, recv_sems = refs[2 * n:]
        x, y, c = lax.axis_index("x"), lax.axis_index("y"), lax.axis_index("c")
        barrier = pltpu.get_barrier_semaphore()
        pl.semaphore_signal(barrier, inc=1, device_id=(x, y, 1 - c), device_id_type=MESH)
        pl.semaphore_wait(barrier, 1)

        def copy(a, q):
            return pltpu.make_async_remote_copy(
                src_ref=ins[a].at[q, 1 - c], dst_ref=outs[a].at[q],
                send_sem=send_sems.at[a * 4 + q], recv_sem=recv_sems.at[a * 4 + q],
                device_id=(x, y, 1 - c), device_id_type=MESH)

        cps = [copy(a, q) for a in range(n) for q in range(4)]
        for cp in cps:
            cp.start()
        for cp in cps:
            cp.wait_recv()
        for cp in cps:
            cp.wait_send()

    return pl.kernel(
        body, out_type=[jax.ShapeDtypeStruct((4,) + a.shape[2:], a.dtype) for a in arrs],
        mesh=plsc.ScalarSubcoreMesh(axis_name="seq", num_cores=1),
        scratch_types=[pltpu.SemaphoreType.DMA((n * 4,)), pltpu.SemaphoreType.DMA((n * 4,))],
        compiler_params=pltpu.CompilerParams(collective_id=collective_id), name=name,
    )(*arrs)


def _pair_swap(name, arrs):
    n = len(arrs)
    out_shape = [jax.ShapeDtypeStruct((4,) + a.shape[2:], a.dtype) for a in arrs]

    def body(*refs):
        ins, outs = refs[:n], refs[n:2 * n]
        send_sems, recv_sems = refs[2 * n:]
        x, y, c = lax.axis_index("x"), lax.axis_index("y"), lax.axis_index("c")

        def copy(a, q):
            return pltpu.make_async_remote_copy(
                src_ref=ins[a].at[q, 1 - c], dst_ref=outs[a].at[q],
                send_sem=send_sems.at[a * 4 + q], recv_sem=recv_sems.at[a * 4 + q],
                device_id=(x, y, 1 - c), device_id_type=MESH)

        cps = [copy(a, q) for a in range(n) for q in range(4)]
        for cp in cps:
            cp.start()
        for cp in cps:
            cp.wait_recv()
        for cp in cps:
            cp.wait_send()

    return pl.pallas_call(
        body, name=name, out_shape=out_shape,
        in_specs=[ANY] * n, out_specs=[ANY] * n,
        scratch_shapes=[pltpu.SemaphoreType.DMA((n * 4,)), pltpu.SemaphoreType.DMA((n * 4,))],
    )(*arrs)


def _chip_exchange(name, arrs):
    n = len(arrs)
    out_shape = [jax.ShapeDtypeStruct((3,) + a.shape[1:], a.dtype) for a in arrs]

    def body(*refs):
        ins, outs = refs[:n], refs[n:2 * n]
        send_sems, recv_sems = refs[2 * n:]
        x, y, c = lax.axis_index("x"), lax.axis_index("y"), lax.axis_index("c")

        def copy(a, k):
            px = 1 - x if (k >> 1) & 1 else x
            py = 1 - y if k & 1 else y
            return pltpu.make_async_remote_copy(
                src_ref=ins[a].at[2 * px + py], dst_ref=outs[a].at[k - 1],
                send_sem=send_sems.at[a * 3 + k - 1], recv_sem=recv_sems.at[a * 3 + k - 1],
                device_id=(px, py, c), device_id_type=MESH)

        cps = [copy(a, k) for a in range(n) for k in (1, 2, 3)]
        for cp in cps:
            cp.start()
        for cp in cps:
            cp.wait_recv()
        for cp in cps:
            cp.wait_send()

    return pl.pallas_call(
        body, name=name, out_shape=out_shape,
        in_specs=[ANY] * n, out_specs=[ANY] * n,
        scratch_shapes=[pltpu.SemaphoreType.DMA((n * 3,)), pltpu.SemaphoreType.DMA((n * 3,))],
    )(*arrs)


class _Rider:
    def __init__(self, arrays, out_shapes, n_sems, build, aliases=None):
        self.arrays, self.out_shapes, self.n_sems, self.build = list(arrays), list(out_shapes), n_sems, build
        self.aliases = dict(aliases or {})


def _merge_riders(r1, r2):
    n1i, n1o, n1s = len(r1.arrays), len(r1.out_shapes), r1.n_sems

    def build(ins, outs, send_sems, recv_sems):
        a = r1.build(ins[:n1i], outs[:n1o], send_sems.at[pl.ds(0, n1s)], recv_sems.at[pl.ds(0, n1s)])
        b = r2.build(ins[n1i:], outs[n1o:], send_sems.at[pl.ds(n1s, r2.n_sems)], recv_sems.at[pl.ds(n1s, r2.n_sems)])
        return tuple(p + q for p, q in zip(a, b))

    aliases = dict(r1.aliases)
    aliases.update({k + n1i: v + n1o for k, v in r2.aliases.items()})
    return _Rider(r1.arrays + r2.arrays, r1.out_shapes + r2.out_shapes, n1s + r2.n_sems, build, aliases)


def _place():
    x, y, c = lax.axis_index("x"), lax.axis_index("y"), lax.axis_index("c")
    chips = [(1 - x, y), (x, 1 - y), (1 - x, 1 - y)]
    return x, y, c, chips


def _ride_gather_ici(arrs):
    n = len(arrs)

    def build(ins, outs, send_sems, recv_sems):
        x, y, c, chips = _place()
        peers = [(*chip, c) for chip in chips] + [(x, y, 1 - c)]
        me = 4 * x + 2 * y + c
        local = [pltpu.make_async_copy(ins[a], outs[a].at[me], send_sems.at[a * 5 + 4]) for a in range(n)]
        sends, recvs = [], []
        for a in range(n):
            for j, (px, py, pc) in enumerate(peers):
                sends.append(pltpu.make_async_remote_copy(
                    src_ref=ins[a], dst_ref=outs[a].at[me], send_sem=send_sems.at[a * 5 + j],
                    recv_sem=recv_sems.at[a * 5 + j], device_id=(px, py, pc), device_id_type=MESH))
                recvs.append(pltpu.make_async_remote_copy(
                    src_ref=ins[a], dst_ref=outs[a].at[4 * px + 2 * py + pc], send_sem=send_sems.at[a * 5 + j],
                    recv_sem=recv_sems.at[a * 5 + j], device_id=(px, py, pc), device_id_type=MESH))
        return local, sends, recvs

    shapes = [jax.ShapeDtypeStruct((N_DEV,) + a.shape, a.dtype) for a in arrs]
    return _Rider(arrs, shapes, n * 5, build)


def _ride_gather_direct(arrs):
    n = len(arrs)

    def build(ins, outs, send_sems, recv_sems):
        x, y, c, _ = _place()
        me = 4 * x + 2 * y + c
        local = [pltpu.make_async_copy(ins[a], outs[a].at[me], send_sems.at[a * N_DEV + 7]) for a in range(n)]
        sends, recvs = [], []
        for a in range(n):
            for k in range(1, N_DEV):
                px = 1 - x if (k >> 2) & 1 else x
                py = 1 - y if (k >> 1) & 1 else y
                pc = 1 - c if k & 1 else c
                sem = a * N_DEV + k - 1
                sends.append(pltpu.make_async_remote_copy(
                    src_ref=ins[a], dst_ref=outs[a].at[me], send_sem=send_sems.at[sem], recv_sem=recv_sems.at[sem],
                    device_id=(px, py, pc), device_id_type=MESH))
                recvs.append(pltpu.make_async_remote_copy(
                    src_ref=ins[a], dst_ref=outs[a].at[4 * px + 2 * py + pc], send_sem=send_sems.at[sem],
                    recv_sem=recv_sems.at[sem], device_id=(px, py, pc), device_id_type=MESH))
        return local, sends, recvs

    shapes = [jax.ShapeDtypeStruct((N_DEV,) + a.shape, a.dtype) for a in arrs]
    return _Rider(arrs, shapes, n * N_DEV, build)


def _ride_gather_d2d(gathered):
    n = len(gathered)

    def build(ins, outs, send_sems, recv_sems):
        x, y, c, chips = _place()
        sends, recvs = [], []
        for a in range(n):
            for j, (px, py) in enumerate(chips):
                mine = outs[a].at[4 * px + 2 * py + c]
                theirs = outs[a].at[4 * px + 2 * py + 1 - c]
                sends.append(pltpu.make_async_remote_copy(
                    src_ref=mine, dst_ref=mine, send_sem=send_sems.at[a * 3 + j], recv_sem=recv_sems.at[a * 3 + j],
                    device_id=(x, y, 1 - c), device_id_type=MESH))
                recvs.append(pltpu.make_async_remote_copy(
                    src_ref=mine, dst_ref=theirs, send_sem=send_sems.at[a * 3 + j], recv_sem=recv_sems.at[a * 3 + j],
                    device_id=(x, y, 1 - c), device_id_type=MESH))
        return [], sends, recvs

    shapes = [jax.ShapeDtypeStruct(a.shape, a.dtype) for a in gathered]
    return _Rider(gathered, shapes, n * 3, build, aliases={a: a for a in range(n)})


def _ride_pair_swap(arrs):
    n = len(arrs)

    def build(ins, outs, send_sems, recv_sems):
        x, y, c, _ = _place()
        cps = [pltpu.make_async_remote_copy(
            src_ref=ins[a].at[q, 1 - c], dst_ref=outs[a].at[q], send_sem=send_sems.at[a * 4 + q],
            recv_sem=recv_sems.at[a * 4 + q], device_id=(x, y, 1 - c), device_id_type=MESH)
            for a in range(n) for q in range(4)]
        return [], cps, cps

    shapes = [jax.ShapeDtypeStruct((4,) + a.shape[2:], a.dtype) for a in arrs]
    return _Rider(arrs, shapes, n * 4, build)


def _ride_chip_exchange(arrs):
    n = len(arrs)

    def build(ins, outs, send_sems, recv_sems):
        x, y, c, _ = _place()
        cps = []
        for a in range(n):
            for k in (1, 2, 3):
                px = 1 - x if (k >> 1) & 1 else x
                py = 1 - y if k & 1 else y
                cps.append(pltpu.make_async_remote_copy(
                    src_ref=ins[a].at[2 * px + py], dst_ref=outs[a].at[k - 1], send_sem=send_sems.at[a * 3 + k - 1],
                    recv_sem=recv_sems.at[a * 3 + k - 1], device_id=(px, py, c), device_id_type=MESH))
        return [], cps, cps

    shapes = [jax.ShapeDtypeStruct((3,) + a.shape[1:], a.dtype) for a in arrs]
    return _Rider(arrs, shapes, n * 3, build)


def _call(body, name, grid, in_specs, out_specs, out_shape, args, scratch=(), rider=None):
    n_in, n_out, n_scr = len(in_specs), len(out_specs), len(scratch)
    sem = ("arbitrary",) * len(grid)
    if rider is None:
        outs = pl.pallas_call(
            body, name=name, grid=grid, in_specs=in_specs, out_specs=out_specs, out_shape=out_shape,
            scratch_shapes=list(scratch), compiler_params=_params(sem))(*args)
        return outs, []
    ri, ro = len(rider.arrays), len(rider.out_shapes)

    def riding(*refs):
        ins, r_ins = refs[:n_in], refs[n_in:n_in + ri]
        outs = refs[n_in + ri:n_in + ri + n_out]
        r_outs = refs[n_in + ri + n_out:n_in + ri + n_out + ro]
        scr = refs[n_in + ri + n_out + ro:n_in + ri + n_out + ro + n_scr]
        send_sems, recv_sems = refs[-2:]
        first = functools.reduce(jnp.logical_and, [pl.program_id(k) == 0 for k in range(len(grid))])
        last = functools.reduce(jnp.logical_and, [pl.program_id(k) == grid[k] - 1 for k in range(len(grid))])

        @pl.when(first)
        def _():
            local, sends, _ = rider.build(r_ins, r_outs, send_sems, recv_sems)
            for cp in local + sends:
                cp.start()

        body(*ins, *outs, *scr)

        @pl.when(last)
        def _():
            local, sends, recvs = rider.build(r_ins, r_outs, send_sems, recv_sems)
            for cp in recvs:
                cp.wait_recv()
            for cp in sends:
                cp.wait_send()
            for cp in local:
                cp.wait()

    outs = pl.pallas_call(
        riding, name=name, grid=grid,
        in_specs=list(in_specs) + [ANY] * ri, out_specs=list(out_specs) + [ANY] * ro,
        out_shape=list(out_shape) + rider.out_shapes,
        scratch_shapes=list(scratch) + [pltpu.SemaphoreType.DMA((rider.n_sems,)), pltpu.SemaphoreType.DMA((rider.n_sems,))],
        input_output_aliases={n_in + k: n_out + v for k, v in rider.aliases.items()},
        compiler_params=_params(sem))(*args, *rider.arrays)
    return outs[:n_out], outs[n_out:]


def _comm(name, rider):
    def body(dummy_ref, out_ref):
        out_ref[...] = dummy_ref[...]

    dummy = jnp.zeros((SUBLANES, LANES), F32)
    spec = pl.BlockSpec((SUBLANES, LANES), lambda i: (0, 0))
    _, r_outs = _call(body, name, (1,), [spec], [spec], [jax.ShapeDtypeStruct(dummy.shape, F32)], [dummy], rider=rider)
    return r_outs


def _ada_fwd(c_all, w_ada_sh, b_ada_sh):
    nb, d = c_all.shape
    ncol = w_ada_sh.shape[1]

    def body(c_ref, w_ref, b_ref, mod_ref, cact_ref):
        cc = c_ref[...]
        ca = cc * jax.nn.sigmoid(cc)
        cact_ref[...] = ca
        mod_ref[...] = _dot(ca.astype(BF16), w_ref[...].astype(BF16), NN) + b_ref[...]

    return pl.pallas_call(
        body, name="ada_fwd",
        out_shape=[jax.ShapeDtypeStruct((nb, ncol), F32), jax.ShapeDtypeStruct((nb, d), F32)],
        compiler_params=_params(),
    )(c_all, w_ada_sh, b_ada_sh)


def _rms(xv):
    rstd = lax.rsqrt(jnp.mean(xv * xv, axis=-1, keepdims=True) + EPS)
    return xv * rstd, rstd


def _rms_bwd(dxhat, xhat, rstd):
    return rstd * (dxhat - xhat * jnp.mean(dxhat * xhat, axis=-1, keepdims=True))


def _colsum(v):
    return jnp.sum(v, axis=0, keepdims=True)


def _expm1(v):
    series = v * (1.0 + v * (0.5 + v * (1.0 / 6.0 + v * (1.0 / 24.0 + v * (1.0 / 120.0 + v * (1.0 / 720.0))))))
    return jnp.where(jnp.abs(v) < 0.3, series, jnp.exp(v) - 1.0)


def _softplus(v):
    return jnp.maximum(v, 0.0) + jnp.log1p(jnp.exp(-jnp.abs(v)))


def _gelu(v):
    t = jnp.tanh(GELU_K0 * (v + GELU_K1 * v * v * v))
    return 0.5 * v * (1.0 + t), t


def _dgelu(v, t):
    return 0.5 * (1.0 + t) + 0.5 * v * (1.0 - t * t) * GELU_K0 * (1.0 + 3.0 * GELU_K1 * v * v)


def _shift_down(v, k, prev8):
    r = pltpu.roll(v, k, 0)
    pr = pltpu.roll(prev8, k, 0)
    row8 = lax.broadcasted_iota(jnp.int32, prev8.shape, 0)
    top = jnp.where(row8 < k, pr, r[0:SUBLANES])
    return jnp.concatenate([top, r[SUBLANES:]], axis=0)


def _shift_up(v, k, next8):
    t = v.shape[0]
    r = pltpu.roll(v, t - k, 0)
    nr = pltpu.roll(next8, SUBLANES - k, 0)
    row8 = lax.broadcasted_iota(jnp.int32, next8.shape, 0)
    bot = jnp.where(row8 >= SUBLANES - k, nr, r[t - SUBLANES:t])
    return jnp.concatenate([r[:t - SUBLANES], bot], axis=0)


def _scan_fwd(a, b, h0):
    t = a.shape[0]
    row = lax.broadcasted_iota(jnp.int32, a.shape, 0)
    s = 1
    while s < min(t, SUBLANES):
        a_sh = pltpu.roll(a, s, 0)
        b_sh = pltpu.roll(b, s, 0)
        m = row >= s
        b = jnp.where(m, a * b_sh + b, b)
        a = jnp.where(m, a * a_sh, a)
        s *= 2
    while s < t:
        b = jnp.concatenate([b[:s], a[s:] * b[:t - s] + b[s:]], axis=0)
        a = jnp.concatenate([a[:s], a[s:] * a[:t - s]], axis=0)
        s *= 2
    return b + a * h0


def _scan_rev(m, b, g_next):
    t = m.shape[0]
    row = lax.broadcasted_iota(jnp.int32, m.shape, 0)
    s = 1
    while s < min(t, SUBLANES):
        m_sh = pltpu.roll(m, t - s, 0)
        b_sh = pltpu.roll(b, t - s, 0)
        msk = row < t - s
        b = jnp.where(msk, m * b_sh + b, b)
        m = jnp.where(msk, m * m_sh, m)
        s *= 2
    while s < t:
        b = jnp.concatenate([m[:t - s] * b[s:] + b[:t - s], b[t - s:]], axis=0)
        m = jnp.concatenate([m[:t - s] * m[s:], m[t - s:]], axis=0)
        s *= 2
    return b + m * g_next


def _lru_gates(u, wa, wx, ba, bx, sp):
    ub = u.astype(BF16)
    r = jax.nn.sigmoid(_dot(ub, wa, NN) + ba)
    i = jax.nn.sigmoid(_dot(ub, wx, NN) + bx)
    log_a = (-RG_C * r) * sp
    a = jnp.exp(log_a)
    mult = jnp.sqrt(-_expm1(2.0 * log_a))
    return ub, r, i, a, mult


def _conv3(p, pp, w_ref, lo):
    p1 = _shift_down(p, 1, pp)
    p2 = _shift_down(p, 2, pp)
    q = (w_ref[0:1, lo:lo + LANES] * p2 + w_ref[1:2, lo:lo + LANES] * p1) + w_ref[2:3, lo:lo + LANES] * p
    return q, p1, p2


def _conv4(xv, xp, w_ref, b_ref, lo):
    x1 = _shift_down(xv, 1, xp)
    x2 = _shift_down(xv, 2, xp)
    x3 = _shift_down(xv, 3, xp)
    u = (((w_ref[0:1, lo:lo + LANES] * x3 + w_ref[1:2, lo:lo + LANES] * x2) + w_ref[2:3, lo:lo + LANES] * x1)
         + w_ref[3:4, lo:lo + LANES] * xv) + b_ref[:, lo:lo + LANES]
    return u, x1, x2, x3


def _mix_in_fwd(x2d, mod6, g_mix, w_in_t, tm, rider=None):
    s, d = x2d.shape
    din = w_in_t.shape[0]

    def body(x_ref, mod_ref, g_ref, w_ref, hn_ref, proj_ref):
        xhat, _ = _rms(x_ref[...])
        hn = ((xhat * g_ref[...]) * (1.0 + mod_ref[1:2, :]) + mod_ref[0:1, :]).astype(BF16)
        hn_ref[...] = hn
        proj_ref[...] = _dot(hn, w_ref[...], NT)

    return _call(
        body, "mix_in_fwd", (s // tm,),
        [pl.BlockSpec((tm, d), lambda i: (i, 0)), _full(mod6.shape), _full(g_mix.shape), _full(w_in_t.shape)],
        [pl.BlockSpec((tm, d), lambda i: (i, 0)), pl.BlockSpec((tm, din), lambda i: (i, 0))],
        [jax.ShapeDtypeStruct((s, d), BF16), jax.ShapeDtypeStruct((s, din), F32)],
        [x2d, mod6, g_mix, w_in_t], rider=rider)


def _mixer_fwd(proj, conv_sc, conv_lru, conv_b, wa_bd, wx_bd, ba, bx, lam, width, rider=None):
    s, din = proj.shape
    t = min(MIX_ROWS, s)
    nblk = width // LANES
    hb = t // SUBLANES

    def body(proj_ref, projp_ref, wsc_ref, wlru_ref, blru_ref, wa_ref, wx_ref, ba_ref, bx_ref, lam_ref,
             ymix_ref, h_ref, hc_ref):
        i = pl.program_id(0)

        @pl.when(i == 0)
        def _():
            hc_ref[...] = jnp.zeros_like(hc_ref)

        has_prev = i > 0
        for j in range(nblk):
            lo = j * LANES

            def col(p, ref=proj_ref):
                return ref[:, p * width + lo:p * width + lo + LANES]

            def prev(p):
                return jnp.where(has_prev, col(p, projp_ref), 0.0)

            p = col(1) * col(2)
            q, _, _ = _conv3(p, prev(1) * prev(2), wsc_ref, lo)
            ymix_ref[:, lo:lo + LANES] = (col(0) * q).astype(BF16)

            u, _, _, _ = _conv4(col(4), prev(4), wlru_ref, blru_ref, lo)
            sp = _softplus(-lam_ref[:, lo:lo + LANES])
            _, r, ig, a, mult = _lru_gates(u, wa_ref[j], wx_ref[j], ba_ref[:, lo:lo + LANES], bx_ref[:, lo:lo + LANES], sp)
            h = _scan_fwd(a, mult * (ig * u), hc_ref[0:1, lo:lo + LANES])
            h_ref[:, lo:lo + LANES] = h
            hc_ref[0:1, lo:lo + LANES] = h[t - 1:t, :]
            gel, _ = _gelu(col(3))
            ymix_ref[:, width + lo:width + lo + LANES] = (gel * h).astype(BF16)

    small = [conv_sc, conv_lru, conv_b, wa_bd, wx_bd, ba, bx, lam]
    return _call(
        body, "mixer_fwd", (s // t,),
        [pl.BlockSpec((t, din), lambda i: (i, 0)),
         pl.BlockSpec((SUBLANES, din), lambda i: (jnp.maximum(i * hb - 1, 0), 0))]
        + [_full(a.shape) for a in small],
        [pl.BlockSpec((t, 2 * width), lambda i: (i, 0)), pl.BlockSpec((t, width), lambda i: (i, 0))],
        [jax.ShapeDtypeStruct((s, 2 * width), BF16), jax.ShapeDtypeStruct((s, width), F32)],
        [proj, proj, *small], scratch=[pltpu.VMEM((SUBLANES, width), F32)], rider=rider)


def _mix_out_fwd(ymix, x2d, w_out, mod6, g_mlp, tm, rider=None):
    s, d = x2d.shape

    def body(y_ref, x_ref, w_ref, mod_ref, g_ref, mix_ref, x2_ref, hn_ref):
        mix = _dot(y_ref[...], w_ref[...], NN)
        mix_ref[...] = mix
        x2 = x_ref[...] + mod_ref[2:3, :] * mix
        x2_ref[...] = x2
        xhat, _ = _rms(x2)
        hn_ref[...] = ((xhat * g_ref[...]) * (1.0 + mod_ref[4:5, :]) + mod_ref[3:4, :]).astype(BF16)

    tile = pl.BlockSpec((tm, d), lambda i: (i, 0))
    return _call(
        body, "mix_out_fwd", (s // tm,),
        [tile, tile, _full(w_out.shape), _full(mod6.shape), _full(g_mlp.shape)],
        [tile, tile, tile],
        [jax.ShapeDtypeStruct((s, d), F32), jax.ShapeDtypeStruct((s, d), F32), jax.ShapeDtypeStruct((s, d), BF16)],
        [ymix, x2d, w_out, mod6, g_mlp], rider=rider)


def _mlp_fwd_loss(hn2, w_up_t, w_down, x2, target, mod6, g_final, tm, tk):
    s, d = hn2.shape
    f = w_up_t.shape[0]
    nk = f // tk

    def body(hn_ref, wu_ref, wd_ref, x2_ref, t_ref, mod_ref, g_ref, z_ref, dx3_ref, dyb_ref, st_ref, y_ref):
        i, k = pl.program_id(0), pl.program_id(1)

        @pl.when(jnp.logical_and(i == 0, k == 0))
        def _():
            st_ref[...] = jnp.zeros_like(st_ref)

        z = jnp.maximum(_dot(hn_ref[...], wu_ref[...], NT), 0.0)
        z_ref[...] = z.astype(BF16)
        part = _dot((z * z).astype(BF16), wd_ref[...], NN)

        @pl.when(k == 0)
        def _():
            y_ref[...] = part

        @pl.when(k > 0)
        def _():
            y_ref[...] += part

        @pl.when(k == nk - 1)
        def _():
            gate = mod_ref[5:6, :]
            yv = y_ref[...]
            xhat, rstd = _rms(x2_ref[...] + gate * yv)
            diff = xhat * g_ref[...] - t_ref[...]
            dyo = diff * (1.0 / d)
            dx3 = _rms_bwd(dyo * g_ref[...], xhat, rstd)
            dx3_ref[...] = dx3
            dyb_ref[...] = (gate * dx3).astype(BF16)
            st_ref[0:1, :] += _colsum(dyo * xhat)
            st_ref[1:2, :] += _colsum(dx3 * yv)
            st_ref[2:3, :] += _colsum(diff * diff)

    tile = pl.BlockSpec((tm, d), lambda i, k: (i, 0))
    wblk = pl.BlockSpec((tk, d), lambda i, k: (k, 0))
    return pl.pallas_call(
        body, name="mlp_fwd_loss", grid=(s // tm, nk),
        in_specs=[tile, wblk, wblk, tile, tile, _full(mod6.shape), _full(g_final.shape)],
        out_specs=[pl.BlockSpec((tm, tk), lambda i, k: (i, k)), tile, tile, _full((SUBLANES, d))],
        out_shape=[jax.ShapeDtypeStruct((s, f), BF16), jax.ShapeDtypeStruct((s, d), F32),
                   jax.ShapeDtypeStruct((s, d), BF16), jax.ShapeDtypeStruct((SUBLANES, d), F32)],
        scratch_shapes=[pltpu.VMEM((tm, d), F32)],
        compiler_params=_params(("arbitrary", "arbitrary")),
    )(hn2, w_up_t, w_down, x2, target, mod6, g_final)


def _mlp_bwd_dx(dyb, z, w_down, w_up_t, tm, tk):
    s, d = dyb.shape
    f = z.shape[1]

    def body(dy_ref, z_ref, wd_ref, wu_ref, dz_ref, dh_ref):
        k = pl.program_id(1)
        dz = ((2.0 * z_ref[...].astype(F32)) * _dot(dy_ref[...], wd_ref[...], NT)).astype(BF16)
        dz_ref[...] = dz
        part = _dot(dz, wu_ref[...], NN)

        @pl.when(k == 0)
        def _():
            dh_ref[...] = part

        @pl.when(k > 0)
        def _():
            dh_ref[...] += part

    return pl.pallas_call(
        body, name="mlp_bwd_dx", grid=(s // tm, f // tk),
        in_specs=[pl.BlockSpec((tm, d), lambda i, k: (i, 0)), pl.BlockSpec((tm, tk), lambda i, k: (i, k)),
                  pl.BlockSpec((tk, d), lambda i, k: (k, 0)), pl.BlockSpec((tk, d), lambda i, k: (k, 0))],
        out_specs=[pl.BlockSpec((tm, tk), lambda i, k: (i, k)), pl.BlockSpec((tm, d), lambda i, k: (i, 0))],
        out_shape=[jax.ShapeDtypeStruct((s, f), BF16), jax.ShapeDtypeStruct((s, d), F32)],
        compiler_params=_params(("parallel", "arbitrary")),
    )(dyb, z, w_down, w_up_t)


def _mlp_bwd_dw(z, dz, dyb, hn2, tm, tk):
    s, d = dyb.shape
    f = z.shape[1]

    def body(z_ref, dz_ref, dy_ref, hn_ref, gd_ref, gu_ref):
        i = pl.program_id(1)

        @pl.when(i == 0)
        def _():
            gd_ref[...] = jnp.zeros_like(gd_ref)
            gu_ref[...] = jnp.zeros_like(gu_ref)

        zf = z_ref[...].astype(F32)
        gd_ref[...] += _dot((zf * zf).astype(BF16), dy_ref[...], TN)
        gu_ref[...] += _dot(dz_ref[...], hn_ref[...], TN)

    return pl.pallas_call(
        body, name="mlp_bwd_dw", grid=(f // tk, s // tm),
        in_specs=[pl.BlockSpec((tm, tk), lambda k, i: (i, k)), pl.BlockSpec((tm, tk), lambda k, i: (i, k)),
                  pl.BlockSpec((tm, d), lambda k, i: (i, 0)), pl.BlockSpec((tm, d), lambda k, i: (i, 0))],
        out_specs=[pl.BlockSpec((tk, d), lambda k, i: (k, 0)), pl.BlockSpec((tk, d), lambda k, i: (k, 0))],
        out_shape=[jax.ShapeDtypeStruct((f, d), F32), jax.ShapeDtypeStruct((f, d), F32)],
        compiler_params=_params(("parallel", "arbitrary")),
    )(z, dz, dyb, hn2)


def _mix_out_bwd(dhn2, x2, dx3, mix, ymix, w_out, mod6, g_mlp, tm, rider=None):
    s, d = x2.shape

    def body(dh_ref, x2_ref, dx3_ref, mix_ref, y_ref, w_ref, mod_ref, g_ref, dx2_ref, dym_ref, gw_ref, st_ref):
        i = pl.program_id(0)

        @pl.when(i == 0)
        def _():
            st_ref[...] = jnp.zeros_like(st_ref)
            gw_ref[...] = jnp.zeros_like(gw_ref)

        dh = dh_ref[...]
        xhat, rstd = _rms(x2_ref[...])
        dn = dh * (1.0 + mod_ref[4:5, :])
        dx2 = dx3_ref[...] + _rms_bwd(dn * g_ref[...], xhat, rstd)
        dx2_ref[...] = dx2
        st_ref[0:1, :] += _colsum(dh)
        st_ref[1:2, :] += _colsum(dh * (xhat * g_ref[...]))
        st_ref[2:3, :] += _colsum(dn * xhat)
        st_ref[3:4, :] += _colsum(dx2 * mix_ref[...])
        dmix = (mod_ref[2:3, :] * dx2).astype(BF16)
        dym_ref[...] = _dot(dmix, w_ref[...], NT)
        gw_ref[...] += _dot(y_ref[...], dmix, TN)

    tile = pl.BlockSpec((tm, d), lambda i: (i, 0))
    return _call(
        body, "mix_out_bwd", (s // tm,),
        [tile, tile, tile, tile, tile, _full(w_out.shape), _full(mod6.shape), _full(g_mlp.shape)],
        [tile, tile, _full((d, d)), _full((SUBLANES, d))],
        [jax.ShapeDtypeStruct((s, d), F32), jax.ShapeDtypeStruct((s, d), F32),
         jax.ShapeDtypeStruct((d, d), F32), jax.ShapeDtypeStruct((SUBLANES, d), F32)],
        [dhn2, x2, dx3, mix, ymix, w_out, mod6, g_mlp], rider=rider)


def _mixer_bwd(proj, dymix, h_all, conv_sc, conv_lru, conv_b, wa_bd, wx_bd, ba, bx, lam, width, rider=None):
    s, din = proj.shape
    t = min(MIX_ROWS, s)
    nt = s // t
    nblk = width // LANES
    hb = t // SUBLANES
    last8 = s // SUBLANES - 1

    def body(proj_ref, projp_ref, projn_ref, dy_ref, dyn_ref, h_ref, hp_ref,
             wsc_ref, wlru_ref, blru_ref, wa_ref, wx_ref, ba_ref, bx_ref, lam_ref,
             dproj_ref, small_ref, gwa_ref, gwx_ref, an_ref, gn_ref, dun_ref):
        i = pl.program_id(0)

        @pl.when(i == 0)
        def _():
            small_ref[...] = jnp.zeros_like(small_ref)
            gwa_ref[...] = jnp.zeros_like(gwa_ref)
            gwx_ref[...] = jnp.zeros_like(gwx_ref)
            an_ref[...] = jnp.zeros_like(an_ref)
            gn_ref[...] = jnp.zeros_like(gn_ref)
            dun_ref[...] = jnp.zeros_like(dun_ref)

        has_prev = i < nt - 1
        has_next = i > 0
        for j in range(nblk):
            lo = j * LANES
            ls = slice(lo, lo + LANES)

            def col(p, ref=proj_ref):
                return ref[:, p * width + lo:p * width + lo + LANES]

            def prev(p):
                return jnp.where(has_prev, col(p, projp_ref), 0.0)

            def nxt(p):
                return jnp.where(has_next, col(p, projn_ref), 0.0)

            def add_row(r, v):
                small_ref[r:r + 1, ls] += _colsum(v)

            sc_b, sc_c, sc_x = col(0), col(1), col(2)
            p = sc_c * sc_x
            q, p1, p2 = _conv3(p, prev(1) * prev(2), wsc_ref, lo)
            dys = dy_ref[:, ls]
            dproj_ref[:, ls] = (dys * q).astype(BF16)
            dq = dys * sc_b
            dqn = jnp.where(has_next, dyn_ref[:, ls], 0.0) * nxt(0)
            dp = (wsc_ref[2:3, ls] * dq + wsc_ref[1:2, ls] * _shift_up(dq, 1, dqn)) + wsc_ref[0:1, ls] * _shift_up(dq, 2, dqn)
            dproj_ref[:, width + lo:width + lo + LANES] = (dp * sc_x).astype(BF16)
            dproj_ref[:, 2 * width + lo:2 * width + lo + LANES] = (dp * sc_c).astype(BF16)
            add_row(0, dq * p2)
            add_row(1, dq * p1)
            add_row(2, dq * p)

            xv = col(4)
            u, x1, x2, x3 = _conv4(xv, prev(4), wlru_ref, blru_ref, lo)
            lam_v = lam_ref[:, ls]
            sp = _softplus(-lam_v)
            wa, wx = wa_ref[j], wx_ref[j]
            ub, r, ig, a, mult = _lru_gates(u, wa, wx, ba_ref[:, ls], bx_ref[:, ls], sp)
            iu = ig * u
            h = h_ref[:, ls]
            hm1 = _shift_down(h, 1, jnp.where(has_prev, hp_ref[:, ls], 0.0))
            lyv = col(3)
            gel, th = _gelu(lyv)
            dyl = dy_ref[:, width + lo:width + lo + LANES]
            dproj_ref[:, 3 * width + lo:3 * width + lo + LANES] = (dyl * h * _dgelu(lyv, th)).astype(BF16)
            a_next = jnp.broadcast_to(an_ref[0:1, ls], (SUBLANES, LANES))
            g = _scan_rev(_shift_up(a, 1, a_next), dyl * gel, gn_ref[0:1, ls])
            an_ref[0:1, ls] = a[0:1, :]
            gn_ref[0:1, ls] = g[0:1, :]
            da = g * hm1
            dmult = g * iu
            diu = g * mult
            dlog_a = da * a - dmult * ((a * a) / mult)
            dpre_a = (dlog_a * (-RG_C * sp)) * (r * (1.0 - r))
            dpre_x = (diu * u) * (ig * (1.0 - ig))
            dab, dxb = dpre_a.astype(BF16), dpre_x.astype(BF16)
            du = diu * ig + _dot(dab, wa, NT) + _dot(dxb, wx, NT)
            gwa_ref[j] += _dot(ub, dab, TN)
            gwx_ref[j] += _dot(ub, dxb, TN)
            dun = dun_ref[:, ls]
            dun_ref[:, ls] = du[0:SUBLANES, :]
            dlx = (((wlru_ref[3:4, ls] * du + wlru_ref[2:3, ls] * _shift_up(du, 1, dun))
                    + wlru_ref[1:2, ls] * _shift_up(du, 2, dun)) + wlru_ref[0:1, ls] * _shift_up(du, 3, dun))
            dproj_ref[:, 4 * width + lo:4 * width + lo + LANES] = dlx.astype(BF16)
            add_row(3, du * x3)
            add_row(4, du * x2)
            add_row(5, du * x1)
            add_row(6, du * xv)
            add_row(7, du)
            add_row(8, dpre_a)
            add_row(9, dpre_x)
            add_row(10, (dlog_a * (RG_C * r)) * jax.nn.sigmoid(-lam_v))

    small = [conv_sc, conv_lru, conv_b, wa_bd, wx_bd, ba, bx, lam]
    rev = lambda i: nt - 1 - i
    return _call(
        body, "mixer_bwd", (nt,),
        [pl.BlockSpec((t, din), lambda i: (rev(i), 0)),
         pl.BlockSpec((SUBLANES, din), lambda i: (jnp.maximum(rev(i) * hb - 1, 0), 0)),
         pl.BlockSpec((SUBLANES, din), lambda i: (jnp.minimum((rev(i) + 1) * hb, last8), 0)),
         pl.BlockSpec((t, 2 * width), lambda i: (rev(i), 0)),
         pl.BlockSpec((SUBLANES, 2 * width), lambda i: (jnp.minimum((rev(i) + 1) * hb, last8), 0)),
         pl.BlockSpec((t, width), lambda i: (rev(i), 0)),
         pl.BlockSpec((SUBLANES, width), lambda i: (jnp.maximum(rev(i) * hb - 1, 0), 0))]
        + [_full(a.shape) for a in small],
        [pl.BlockSpec((t, din), lambda i: (rev(i), 0)), _full((2 * SUBLANES, width)),
         _full(wa_bd.shape), _full(wx_bd.shape)],
        [jax.ShapeDtypeStruct((s, din), BF16), jax.ShapeDtypeStruct((2 * SUBLANES, width), F32),
         jax.ShapeDtypeStruct(wa_bd.shape, F32), jax.ShapeDtypeStruct(wx_bd.shape, F32)],
        [proj, proj, proj, dymix, dymix, h_all, h_all, *small],
        scratch=[pltpu.VMEM((SUBLANES, width), F32), pltpu.VMEM((SUBLANES, width), F32),
                 pltpu.VMEM((SUBLANES, width), F32)], rider=rider)


def _mix_in_bwd_dx(dproj, x2d, dx2, w_in_t, mod6, g_mix, tm, rider=None):
    s, d = x2d.shape
    din = dproj.shape[1]

    def body(dp_ref, x_ref, dx2_ref, w_ref, mod_ref, g_ref, gx_ref, st_ref):
        i = pl.program_id(0)

        @pl.when(i == 0)
        def _():
            st_ref[...] = jnp.zeros_like(st_ref)

        dh = _dot(dp_ref[...], w_ref[...], NN)
        xhat, rstd = _rms(x_ref[...])
        dn = dh * (1.0 + mod_ref[1:2, :])
        gx_ref[...] = dx2_ref[...] + _rms_bwd(dn * g_ref[...], xhat, rstd)
        st_ref[0:1, :] += _colsum(dh)
        st_ref[1:2, :] += _colsum(dh * (xhat * g_ref[...]))
        st_ref[2:3, :] += _colsum(dn * xhat)

    tile = pl.BlockSpec((tm, d), lambda i: (i, 0))
    return _call(
        body, "mix_in_bwd_dx", (s // tm,),
        [pl.BlockSpec((tm, din), lambda i: (i, 0)), tile, tile, _full(w_in_t.shape), _full(mod6.shape),
         _full(g_mix.shape)],
        [tile, _full((SUBLANES, d))],
        [jax.ShapeDtypeStruct((s, d), F32), jax.ShapeDtypeStruct((SUBLANES, d), F32)],
        [dproj, x2d, dx2, w_in_t, mod6, g_mix], rider=rider)


def _mix_in_bwd_dw(dproj, hn1, tm, tn, rider=None):
    s, d = hn1.shape
    din = dproj.shape[1]

    def body(dp_ref, hn_ref, gw_ref):
        i = pl.program_id(1)

        @pl.when(i == 0)
        def _():
            gw_ref[...] = jnp.zeros_like(gw_ref)

        gw_ref[...] += _dot(dp_ref[...], hn_ref[...], TN)

    return _call(
        body, "mix_in_bwd_dw", (din // tn, s // tm),
        [pl.BlockSpec((tm, tn), lambda p, i: (i, p)), pl.BlockSpec((tm, d), lambda p, i: (i, 0))],
        [pl.BlockSpec((tn, d), lambda p, i: (p, 0))],
        [jax.ShapeDtypeStruct((din, d), F32)],
        [dproj, hn1], rider=rider)


def _adamw(w, g, m, v):
    m = ADAM_B1 * m + (1.0 - ADAM_B1) * g
    v = ADAM_B2 * v + (1.0 - ADAM_B2) * (g * g)
    m_hat = m / (1.0 - ADAM_B1 ** ADAM_STEP)
    v_hat = v / (1.0 - ADAM_B2 ** ADAM_STEP)
    delta = -ADAM_LR * (m_hat / (jnp.sqrt(v_hat) + ADAM_EPS) + ADAM_WD * w)
    return delta, m, v


def _pair_sum(g4, h4, core_chip, tr, name):
    _, _, r, n = g4.shape

    def body(sc_ref, g_ref, h_ref, sb_ref, own_ref):
        q = pl.program_id(1)
        ssum = g_ref[...] + h_ref[...]
        sb_ref[...] = ssum.astype(BF16)

        @pl.when(q == sc_ref[1])
        def _():
            own_ref[...] = ssum

    grid_spec = pltpu.PrefetchScalarGridSpec(
        num_scalar_prefetch=1, grid=(r // tr, 4),
        in_specs=[pl.BlockSpec((None, None, tr, n), lambda i, q, sc: (q, sc[0], i, 0)),
                  pl.BlockSpec((None, tr, n), lambda i, q, sc: (q, i, 0))],
        out_specs=[pl.BlockSpec((None, tr, n), lambda i, q, sc: (q, i, 0)),
                   pl.BlockSpec((tr, n), lambda i, q, sc: (i, 0))])
    return pl.pallas_call(
        body, name=name, grid_spec=grid_spec,
        out_shape=[jax.ShapeDtypeStruct((4, r, n), BF16), jax.ShapeDtypeStruct((r, n), F32)],
        compiler_params=_params(("parallel", "arbitrary")),
    )(core_chip, g4, h4)


def _sum4(own, parts, tr, name):
    r, n = own.shape

    def body(o_ref, p_ref, out_ref):
        acc = o_ref[...]
        for k in range(3):
            acc = acc + p_ref[k].astype(F32)
        out_ref[...] = acc

    return pl.pallas_call(
        body, name=name, grid=(r // tr,),
        in_specs=[pl.BlockSpec((tr, n), lambda i: (i, 0)), pl.BlockSpec((3, tr, n), lambda i: (0, i, 0))],
        out_specs=pl.BlockSpec((tr, n), lambda i: (i, 0)),
        out_shape=jax.ShapeDtypeStruct((r, n), F32),
        compiler_params=_params(("parallel",)),
    )(own, parts)


def _sum8(parts, tr, name):
    _, rows, n = parts.shape

    def body(p_ref, o_ref):
        acc = p_ref[0]
        for k in range(1, N_DEV):
            acc = acc + p_ref[k]
        o_ref[...] = acc

    return pl.pallas_call(
        body, name=name, grid=(rows // tr,),
        in_specs=[pl.BlockSpec((N_DEV, tr, n), lambda i: (0, i, 0))],
        out_specs=pl.BlockSpec((tr, n), lambda i: (i, 0)),
        out_shape=jax.ShapeDtypeStruct((rows, n), F32),
        compiler_params=_params(("parallel",)),
    )(parts)


def _adam_rows(w, g, m, v, tr, name):
    rows, n = w.shape

    def body(w_ref, g_ref, m_ref, v_ref, d_ref, nm_ref, nv_ref):
        d_ref[...], nm_ref[...], nv_ref[...] = _adamw(w_ref[...], g_ref[...], m_ref[...], v_ref[...])

    tile = pl.BlockSpec((tr, n), lambda i: (i, 0))
    return pl.pallas_call(
        body, name=name, grid=(rows // tr,),
        in_specs=[tile] * 4, out_specs=[tile] * 3,
        out_shape=[jax.ShapeDtypeStruct((rows, n), F32)] * 3,
        compiler_params=_params(("parallel",)),
    )(w, g, m, v)


def _ada_bwd_adam(cact_t, dmod_cols, w, m, v, tr):
    rows, n = w.shape

    def body(c_ref, d_ref, w_ref, m_ref, v_ref, g_ref, dl_ref, nm_ref, nv_ref):
        def term(b):
            return c_ref[b].astype(BF16).astype(F32) * d_ref[b:b + 1, :].astype(BF16).astype(F32)

        g = term(0)
        for b in range(1, N_DEV):
            g = g + term(b)
        g_ref[...] = g
        dl_ref[...], nm_ref[...], nv_ref[...] = _adamw(w_ref[...], g, m_ref[...], v_ref[...])

    tile = pl.BlockSpec((tr, n), lambda i: (i, 0))
    return pl.pallas_call(
        body, name="ada_bwd_adam", grid=(rows // tr,),
        in_specs=[pl.BlockSpec((N_DEV, tr, 1), lambda i: (0, i, 0)), _full(dmod_cols.shape), tile, tile, tile],
        out_specs=[tile] * 4,
        out_shape=[jax.ShapeDtypeStruct((rows, n), F32)] * 4,
        compiler_params=_params(("parallel",)),
    )(cact_t, dmod_cols, w, m, v)


def _adam_small(ws, gs, ms, vs):
    n = len(ws)

    def body(*refs):
        w_r, g_r, m_r, v_r = refs[:n], refs[n:2 * n], refs[2 * n:3 * n], refs[3 * n:4 * n]
        d_r, nm_r, nv_r = refs[4 * n:5 * n], refs[5 * n:6 * n], refs[6 * n:7 * n]
        for k in range(n):
            d_r[k][...], nm_r[k][...], nv_r[k][...] = _adamw(w_r[k][...], g_r[k][...], m_r[k][...], v_r[k][...])

    shapes = [jax.ShapeDtypeStruct(w.shape, F32) for w in ws]
    outs = pl.pallas_call(
        body, name="adam_small", out_shape=shapes * 3, compiler_params=_params(),
    )(*ws, *gs, *ms, *vs)
    return outs[:n], outs[n:2 * n], outs[2 * n:]


def _block_diag(w):
    h, hd, _ = w.shape
    per = LANES // hd
    eye = jnp.eye(per, dtype=w.dtype)
    w5 = w.reshape(h // per, per, hd, 1, hd) * eye[None, :, None, :, None]
    return w5.reshape(h // per, LANES, LANES)


def _block_diag_grad(g, h, hd):
    per = LANES // hd
    g5 = g.reshape(h // per, per, hd, per, hd)
    return jnp.stack([g5[:, a, :, a, :] for a in range(per)], axis=1).reshape(h, hd, hd)


def kernel(x, c, w_ada, b_ada, g_mix, w_in, conv_w_sc, conv_w_lru, conv_b_lru, w_rg_a, b_rg_a, w_rg_x, b_rg_x, lru_lambda, w_out, g_mlp, w_up, w_down, g_final, loss_target, m_w_ada, m_b_ada, m_g_mix, m_w_in, m_conv_w_sc, m_conv_w_lru, m_conv_b_lru, m_w_rg_a, m_b_rg_a, m_w_rg_x, m_b_rg_x, m_lru_lambda, m_w_out, m_g_mlp, m_w_up, m_w_down, m_g_final, v_w_ada, v_b_ada, v_g_mix, v_w_in, v_conv_w_sc, v_conv_w_lru, v_conv_b_lru, v_w_rg_a, v_b_rg_a, v_w_rg_x, v_b_rg_x, v_lru_lambda, v_w_out, v_g_mlp, v_w_up, v_w_down, v_g_final):
    s, d = x.shape[1], x.shape[2]
    width = conv_b_lru.shape[1]
    heads, hd = w_rg_a.shape[1], w_rg_a.shape[2]
    f = w_down.shape[1] * N_DEV
    n_ada = w_ada.shape[2]
    csh = conv_w_sc.shape[2]
    me = 4 * lax.axis_index("x") + 2 * lax.axis_index("y") + lax.axis_index("c")
    tm = min(512, s)
    tm_mlp = min(1024, s)
    tk = 512

    x2d = x[0]
    tgt = loss_target[0]

    pay = jnp.zeros((SUBLANES, d), F32)
    pay = pay.at[0:1, :].set(c)
    pay = pay.at[1:4, 0:csh].set(conv_w_sc[0])
    pay = pay.at[4:8, 0:csh].set(conv_w_lru[0])
    w_in_t_sh = w_in[0].T.astype(BF16)
    w_up_t_sh = w_up[0].T.astype(BF16)
    w_out_sh = w_out[0].astype(BF16)
    w_down_sh = w_down[0].astype(BF16)
    pay_all, w_in_t = _gather2("gather_in", [pay, w_in_t_sh])
    w_in_t = w_in_t.reshape(-1, d)
    c_all = pay_all[:, 0, :]
    conv_sc = pay_all[:, 1:4, 0:csh].transpose(1, 0, 2).reshape(3, width)
    conv_lru = pay_all[:, 4:8, 0:csh].transpose(1, 0, 2).reshape(4, width)

    b_ada_sh = lax.dynamic_slice(b_ada, (0, me * n_ada), (1, n_ada))
    mod_cols, c_act = _ada_fwd(c_all, w_ada[0], b_ada_sh)
    (mod_rows,) = _exchange("scatter_mod", [], [mod_cols.reshape(N_DEV, 1, n_ada)])
    mod_rows, w_out_sh, w_up_t_sh, w_down_sh = lax.optimization_barrier((mod_rows, w_out_sh, w_up_t_sh, w_down_sh))
    (w_out_g,) = _seq_gather2("gather_w_out", 1, [w_out_sh])
    w_up_g, w_down_g = _seq_gather2("gather_mlp_weights", 2, [w_up_t_sh, w_down_sh])
    mod6 = jnp.zeros((SUBLANES, d), F32).at[0:6, :].set(mod_rows.reshape(6, d))

    wa_bd = _block_diag(w_rg_a[0]).astype(BF16)
    wx_bd = _block_diag(w_rg_x[0]).astype(BF16)
    ba = b_rg_a.reshape(1, width)
    bx = b_rg_x.reshape(1, width)
    g_fin = g_final.reshape(1, d)

    (hn1, proj), _ = _mix_in_fwd(x2d, mod6, g_mix, w_in_t, tm)
    (ymix, h_all), _ = _mixer_fwd(proj, conv_sc, conv_lru, conv_b_lru, wa_bd, wx_bd, ba, bx, lru_lambda, width)
    w_out_b = w_out_g.reshape(-1, d)
    (mix, x2, hn2), _ = _mix_out_fwd(ymix, x2d, w_out_b, mod6, g_mlp, tm)
    w_up_t = w_up_g.reshape(-1, d)
    w_down_b = w_down_g.reshape(-1, d)
    z, dx3, dyb, st_fin = _mlp_fwd_loss(hn2, w_up_t, w_down_b, x2, tgt, mod6, g_fin, tm, 4 * tk)

    core_chip = jnp.stack([lax.axis_index("c"), 2 * lax.axis_index("x") + lax.axis_index("y")]).astype(jnp.int32)
    dz, dhn2 = _mlp_bwd_dx(dyb, z, w_down_b, w_up_t, tm, 4 * tk)
    g_down, g_up_t = _mlp_bwd_dw(z, dz, dyb, hn2, tm_mlp, 2 * tk)
    g_up4, g_down4 = g_up_t.reshape(4, 2, -1, d), g_down.reshape(4, 2, -1, d)
    h_up, h_down = _seq_pair_swap("swap_mlp_grads", 7, [g_up4, g_down4])
    (dx2, dymix, g_out, st_out), _ = _mix_out_bwd(dhn2, x2, dx3, mix, ymix, w_out_b, mod6, g_mlp, tm)
    h_up, h_down, g_out = lax.optimization_barrier((h_up, h_down, g_out))
    sb_up, own_up = _pair_sum(g_up4, h_up, core_chip, 256, "pair_sum_w_up")
    sb_down, own_down = _pair_sum(g_down4, h_down, core_chip, 256, "pair_sum_w_down")
    g_out4 = g_out.reshape(4, 2, -1, d)
    (h_out,) = _seq_pair_swap("swap_w_out_grad", 8, [g_out4])
    p_up, p_down = _seq_chip_exchange("exchange_mlp_grads", 3, [sb_up, sb_down])
    (dproj, g_small, g_wa, g_wx), _ = _mixer_bwd(
        proj, dymix, h_all, conv_sc, conv_lru, conv_b_lru, wa_bd, wx_bd, ba, bx, lru_lambda, width)
    h_out, dproj = lax.optimization_barrier((h_out, dproj))
    sb_out, own_out = _pair_sum(g_out4, h_out, core_chip, g_out4.shape[2], "pair_sum_w_out")
    (p_out,) = _seq_chip_exchange("exchange_w_out_grad", 4, [sb_out])
    (grad_x, st_in), _ = _mix_in_bwd_dx(dproj, x2d, dx2, w_in_t, mod6, g_mix, tm)

    small = jnp.concatenate([
        st_in[0:2], st_out[3:4], st_out[0:2], st_fin[1:2],
        st_in[2:3], st_out[2:3], st_fin[0:1],
        jnp.concatenate([g_small[7:8], g_small[10:11]], axis=1),
        jnp.concatenate([g_small[8:9], g_small[9:10]], axis=1),
        jnp.concatenate([jnp.concatenate([g_small[0:3], jnp.zeros((1, width), F32)], axis=0), g_small[3:7]], axis=1),
        st_fin[2:3],
        _block_diag_grad(g_wa, heads, hd).reshape(-1, d),
        _block_diag_grad(g_wx, heads, hd).reshape(-1, d),
    ], axis=0)

    (small_all,) = _seq_gather2("gather_small_grads", 5, [small])
    (g_in_t,), _ = _mix_in_bwd_dw(dproj, hn1, tm_mlp, 512)
    g_in4 = g_in_t.reshape(4, 2, -1, d)
    (h_in,) = _seq_pair_swap("swap_w_in_grad", 9, [g_in4])
    p_up, p_down, p_out, small_all, g_in_t = lax.optimization_barrier((p_up, p_down, p_out, small_all, g_in_t))

    gs_up_t = _sum4(own_up, p_up, 256, "sum_w_up")
    gs_up = gs_up_t.T
    gs_out = _sum4(own_out, p_out, own_out.shape[0], "sum_w_out")
    gs_down = _sum4(own_down, p_down, 256, "sum_w_down")
    ad_up = _adam_rows(w_up[0], gs_up, m_w_up[0], v_w_up[0], 256, "adam_w_up")
    h_in, gs_out, gs_down, ad_up = lax.optimization_barrier((h_in, gs_out, gs_down, ad_up))
    sb_in, own_in = _pair_sum(g_in4, h_in, core_chip, g_in4.shape[2], "pair_sum_w_in")
    (p_in,) = _seq_chip_exchange("exchange_w_in_grad", 6, [sb_in])
    ad_out = _adam_rows(w_out[0], gs_out, m_w_out[0], v_w_out[0], w_out.shape[1], "adam_w_out")
    ad_down = _adam_rows(w_down[0], gs_down, m_w_down[0], v_w_down[0], 256, "adam_w_down")

    gsum = _sum8(small_all, SMALL_ROWS, "sum_small")
    loss = (0.5 / d) * jnp.sum(gsum[15])
    dmod_cols = lax.dynamic_slice(small_all[:, 0:6, :].reshape(N_DEV, 6 * d), (0, me * n_ada), (N_DEV, n_ada))
    g_ada, d_ada, nm_ada, nv_ada = _ada_bwd_adam(c_act[:, :, None], dmod_cols, w_ada[0], m_w_ada[0], v_w_ada[0], 256)

    g_conv = lax.dynamic_slice(gsum[11:15, 0:width], (0, me * csh), (4, csh))
    g_conv_l = lax.dynamic_slice(gsum[11:15, width:2 * width], (0, me * csh), (4, csh))
    small_g = [
        gsum[0:6].reshape(1, 6 * d),
        gsum[6:7],
        g_conv[0:3].reshape(1, 3, csh),
        g_conv_l.reshape(1, 4, csh),
        gsum[9:10, 0:width],
        gsum[16:48].reshape(1, heads, hd, hd),
        gsum[10:11, 0:width].reshape(1, heads, hd),
        gsum[48:80].reshape(1, heads, hd, hd),
        gsum[10:11, width:].reshape(1, heads, hd),
        gsum[9:10, width:],
        gsum[7:8],
        gsum[8],
    ]
    small_w = [b_ada, g_mix, conv_w_sc, conv_w_lru, conv_b_lru, w_rg_a, b_rg_a, w_rg_x, b_rg_x, lru_lambda, g_mlp, g_final]
    small_m = [m_b_ada, m_g_mix, m_conv_w_sc, m_conv_w_lru, m_conv_b_lru, m_w_rg_a, m_b_rg_a, m_w_rg_x, m_b_rg_x,
               m_lru_lambda, m_g_mlp, m_g_final]
    small_v = [v_b_ada, v_g_mix, v_conv_w_sc, v_conv_w_lru, v_conv_b_lru, v_w_rg_a, v_b_rg_a, v_w_rg_x, v_b_rg_x,
               v_lru_lambda, v_g_mlp, v_g_final]
    sd, snm, snv = _adam_small(small_w, small_g, small_m, small_v)
    p_in, ad_out, ad_down, (g_ada, d_ada, nm_ada, nv_ada), sd = lax.optimization_barrier(
        (p_in, ad_out, ad_down, (g_ada, d_ada, nm_ada, nv_ada), sd))
    gs_in = _sum4(own_in, p_in, own_in.shape[0], "sum_w_in").T
    ad_in = _adam_rows(w_in[0], gs_in, m_w_in[0], v_w_in[0], 256, "adam_w_in")

    def order(ada, w_in_, w_out_, w_up_, w_down_, sm):
        return [ada[None], sm[0], sm[1], w_in_[None], sm[2], sm[3], sm[4], sm[5], sm[6], sm[7], sm[8], sm[9],
                w_out_[None], sm[10], w_up_[None], w_down_[None], sm[11]]

    grads = order(g_ada, gs_in, gs_out, gs_up, gs_down, small_g)
    deltas = order(d_ada, ad_in[0], ad_out[0], ad_up[0], ad_down[0], sd)
    new_m = order(nm_ada, ad_in[1], ad_out[1], ad_up[1], ad_down[1], snm)
    new_v = order(nv_ada, ad_in[2], ad_out[2], ad_up[2], ad_down[2], snv)
    return (loss, grad_x[None], *grads, *deltas, *new_m, *new_v)
```

```python
import functools

import jax
import jax.numpy as jnp
from jax import lax
from jax.experimental import pallas as pl
from jax.experimental.pallas import tpu as pltpu
from jax.experimental.pallas import tpu_sc as plsc

F32 = jnp.float32
BF16 = jnp.bfloat16
N_DEV = 8
EPS = 1e-6
RG_C = 8.0
GELU_K0 = 0.7978845608028654
GELU_K1 = 0.044715
ADAM_LR = 0.001
ADAM_B1 = 0.9
ADAM_B2 = 0.999
ADAM_EPS = 1e-08
ADAM_WD = 0.01
ADAM_STEP = 10
LANES = 128
SUBLANES = 8
VMEM_LIMIT = 52 * 1024 * 1024
MIX_ROWS = 256
SMALL_ROWS = 80

MESH = pl.DeviceIdType.MESH
ANY = pl.BlockSpec(memory_space=pl.ANY)
NN = ((1,), (0,))
NT = ((1,), (1,))
TN = ((0,), (0,))


def _dot(a, b, dims):
    return lax.dot_general(a, b, (dims, ((), ())), preferred_element_type=F32)


def _params(sem=None):
    return pltpu.CompilerParams(dimension_semantics=sem, vmem_limit_bytes=VMEM_LIMIT)


def _full(shape):
    nd = len(shape)
    return pl.BlockSpec(shape, lambda *_: (0,) * nd)


def _exchange(name, gathers, scatters):
    n_g = len(gathers)
    arrs = list(gathers) + list(scatters)
    n = len(arrs)
    out_shape = [jax.ShapeDtypeStruct((N_DEV,) + a.shape, a.dtype) for a in gathers]
    out_shape += [jax.ShapeDtypeStruct(a.shape, a.dtype) for a in scatters]

    def body(*refs):
        ins, outs = refs[:n], refs[n:2 * n]
        send_sems, recv_sems, local_sems = refs[2 * n:]
        x, y, c = lax.axis_index("x"), lax.axis_index("y"), lax.axis_index("c")
        me = 4 * x + 2 * y + c

        def src(a, dev):
            return ins[a] if a < n_g else ins[a].at[dev]

        def peer_of(k):
            px = 1 - x if (k >> 2) & 1 else x
            py = 1 - y if (k >> 1) & 1 else y
            pc = 1 - c if k & 1 else c
            return (px, py, pc), 4 * px + 2 * py + pc

        local = [pltpu.make_async_copy(src(a, me), outs[a].at[me], local_sems.at[a]) for a in range(n)]
        for cp in local:
            cp.start()
        sends = []
        for k in range(1, N_DEV):
            peer, pidx = peer_of(k)
            for a in range(n):
                cp = pltpu.make_async_remote_copy(
                    src_ref=src(a, pidx), dst_ref=outs[a].at[me],
                    send_sem=send_sems.at[a * (N_DEV - 1) + k - 1], recv_sem=recv_sems.at[a * (N_DEV - 1) + k - 1],
                    device_id=peer, device_id_type=MESH)
                cp.start()
                sends.append(cp)
        for k in range(1, N_DEV):
            peer, pidx = peer_of(k)
            for a in range(n):
                pltpu.make_async_remote_copy(
                    src_ref=src(a, pidx), dst_ref=outs[a].at[pidx],
                    send_sem=send_sems.at[a * (N_DEV - 1) + k - 1], recv_sem=recv_sems.at[a * (N_DEV - 1) + k - 1],
                    device_id=peer, device_id_type=MESH).wait_recv()
        for cp in sends:
            cp.wait_send()
        for cp in local:
            cp.wait()

    return pl.pallas_call(
        body, name=name, out_shape=out_shape,
        in_specs=[ANY] * n, out_specs=[ANY] * n,
        scratch_shapes=[pltpu.SemaphoreType.DMA((n * (N_DEV - 1),)),
                        pltpu.SemaphoreType.DMA((n * (N_DEV - 1),)),
                        pltpu.SemaphoreType.DMA((n,))],
    )(*arrs)


def _gather2(name, arrs):
    n = len(arrs)
    per = 7
    out_shape = [jax.ShapeDtypeStruct((N_DEV,) + a.shape, a.dtype) for a in arrs]

    def body(*refs):
        ins, outs = refs[:n], refs[n:2 * n]
        send_sems, recv_sems, local_sems = refs[2 * n:]
        x, y, c = lax.axis_index("x"), lax.axis_index("y"), lax.axis_index("c")
        sib = (x, y, 1 - c)
        chips = [(1 - x, y), (x, 1 - y), (1 - x, 1 - y)]

        def slot(a, px, py, pc):
            return outs[a].at[4 * px + 2 * py + pc]

        def copy(a, k, block, to, src=None):
            return pltpu.make_async_remote_copy(
                src_ref=slot(a, *block) if src is None else src, dst_ref=slot(a, *block),
                send_sem=send_sems.at[a * per + k], recv_sem=recv_sems.at[a * per + k],
                device_id=to, device_id_type=MESH)

        local = [pltpu.make_async_copy(ins[a], slot(a, x, y, c), local_sems.at[a]) for a in range(n)]
        for cp in local:
            cp.start()
        first = []
        for a in range(n):
            first += [copy(a, 1 + j, (x, y, c), (*chip, c), src=ins[a]) for j, chip in enumerate(chips)]
        for a in range(n):
            first.append(copy(a, 0, (x, y, c), sib, src=ins[a]))
        for cp in first:
            cp.start()
        passed = []
        for a in range(n):
            for j, chip in enumerate(chips):
                copy(a, 1 + j, (*chip, c), (x, y, c)).wait_recv()
                cp = copy(a, 4 + j, (*chip, c), sib)
                cp.start()
                passed.append(cp)
        for a in range(n):
            copy(a, 0, sib, (x, y, c)).wait_recv()
            for j, chip in enumerate(chips):
                copy(a, 4 + j, (*chip, 1 - c), (x, y, c)).wait_recv()
        for cp in first + passed:
            cp.wait_send()
        for cp in local:
            cp.wait()

    return pl.pallas_call(
        body, name=name, out_shape=out_shape,
        in_specs=[ANY] * n, out_specs=[ANY] * n,
        scratch_shapes=[pltpu.SemaphoreType.DMA((n * per,)), pltpu.SemaphoreType.DMA((n * per,)),
                        pltpu.SemaphoreType.DMA((n,))],
    )(*arrs)


def _seq_gather2(name, collective_id, arrs):
    n = len(arrs)
    per = 7

    def body(*refs):
        ins, outs = refs[:n], refs[n:2 * n]
        send_sems, recv_sems, local_sems = refs[2 * n:]
        x, y, c = lax.axis_index("x"), lax.axis_index("y"), lax.axis_index("c")
        sib = (x, y, 1 - c)
        chips = [(1 - x, y), (x, 1 - y), (1 - x, 1 - y)]
        barrier = pltpu.get_barrier_semaphore()
        for peer in [sib] + [(*chip, c) for chip in chips]:
            pl.semaphore_signal(barrier, inc=1, device_id=peer, device_id_type=MESH)
        pl.semaphore_wait(barrier, 4)

        def slot(a, px, py, pc):
            return outs[a].at[4 * px + 2 * py + pc]

        def copy(a, k, block, to, src=None):
            return pltpu.make_async_remote_copy(
                src_ref=slot(a, *block) if src is None else src, dst_ref=slot(a, *block),
                send_sem=send_sems.at[a * per + k], recv_sem=recv_sems.at[a * per + k],
                device_id=to, device_id_type=MESH)

        local = [pltpu.make_async_copy(ins[a], slot(a, x, y, c), local_sems.at[a]) for a in range(n)]
        for cp in local:
            cp.start()
        first = []
        for a in range(n):
            first += [copy(a, 1 + j, (x, y, c), (*chip, c), src=ins[a]) for j, chip in enumerate(chips)]
        for a in range(n):
            first.append(copy(a, 0, (x, y, c), sib, src=ins[a]))
        for cp in first:
            cp.start()
        passed = []
        for a in range(n):
            for j, chip in enumerate(chips):
                copy(a, 1 + j, (*chip, c), (x, y, c)).wait_recv()
                cp = copy(a, 4 + j, (*chip, c), sib)
                cp.start()
                passed.append(cp)
        for a in range(n):
            copy(a, 0, sib, (x, y, c)).wait_recv()
            for j, chip in enumerate(chips):
                copy(a, 4 + j, (*chip, 1 - c), (x, y, c)).wait_recv()
        for cp in first + passed:
            cp.wait_send()
        for cp in local:
            cp.wait()

    return pl.kernel(
        body, out_type=[jax.ShapeDtypeStruct((N_DEV,) + a.shape, a.dtype) for a in arrs],
        mesh=plsc.ScalarSubcoreMesh(axis_name="seq", num_cores=1),
        scratch_types=[pltpu.SemaphoreType.DMA((n * per,)), pltpu.SemaphoreType.DMA((n * per,)),
                       pltpu.SemaphoreType.DMA((n,))],
        compiler_params=pltpu.CompilerParams(collective_id=collective_id), name=name,
    )(*arrs)


def _seq_chip_exchange(name, collective_id, arrs):
    n = len(arrs)

    def body(*refs):
        ins, outs = refs[:n], refs[n:2 * n]
        send_sems, recv_sems = refs[2 * n:]
        x, y, c = lax.axis_index("x"), lax.axis_index("y"), lax.axis_index("c")

        def peer(k):
            return (1 - x if (k >> 1) & 1 else x), (1 - y if k & 1 else y)

        barrier = pltpu.get_barrier_semaphore()
        for k in (1, 2, 3):
            pl.semaphore_signal(barrier, inc=1, device_id=(*peer(k), c), device_id_type=MESH)
        pl.semaphore_wait(barrier, 3)

        def copy(a, k):
            px, py = peer(k)
            return pltpu.make_async_remote_copy(
                src_ref=ins[a].at[2 * px + py], dst_ref=outs[a].at[k - 1],
                send_sem=send_sems.at[a * 3 + k - 1], recv_sem=recv_sems.at[a * 3 + k - 1],
                device_id=(px, py, c), device_id_type=MESH)

        cps = [copy(a, k) for a in range(n) for k in (1, 2, 3)]
        for cp in cps:
            cp.start()
        for cp in cps:
            cp.wait_recv()
        for cp in cps:
            cp.wait_send()

    return pl.kernel(
        body, out_type=[jax.ShapeDtypeStruct((3,) + a.shape[1:], a.dtype) for a in arrs],
        mesh=plsc.ScalarSubcoreMesh(axis_name="seq", num_cores=1),
        scratch_types=[pltpu.SemaphoreType.DMA((n * 3,)), pltpu.SemaphoreType.DMA((n * 3,))],
        compiler_params=pltpu.CompilerParams(collective_id=collective_id), name=name,
    )(*arrs)


def _seq_pair_swap(name, collective_id, arrs):
    n = len(arrs)

    def body(*refs):
        ins, outs = refs[:n], refs[n:2 * n]
        send_sems, recv_sems = refs[2 * n:]
        x, y, c = lax.axis_index("x"), lax.axis_index("y"), lax.axis_index("c")
        barrier = pltpu.get_barrier_semaphore()
        pl.semaphore_signal(barrier, inc=1, device_id=(x, y, 1 - c), device_id_type=MESH)
        pl.semaphore_wait(barrier, 1)

        def copy(a, q):
            return pltpu.make_async_remote_copy(
                src_ref=ins[a].at[q, 1 - c], dst_ref=outs[a].at[q],
                send_sem=send_sems.at[a * 4 + q], recv_sem=recv_sems.at[a * 4 + q],
                device_id=(x, y, 1 - c), device_id_type=MESH)

        cps = [copy(a, q) for a in range(n) for q in range(4)]
        for cp in cps:
            cp.start()
        for cp in cps:
            cp.wait_recv()
        for cp in cps:
            cp.wait_send()

    return pl.kernel(
        body, out_type=[jax.ShapeDtypeStruct((4,) + a.shape[2:], a.dtype) for a in arrs],
        mesh=plsc.ScalarSubcoreMesh(axis_name="seq", num_cores=1),
        scratch_types=[pltpu.SemaphoreType.DMA((n * 4,)), pltpu.SemaphoreType.DMA((n * 4,))],
        compiler_params=pltpu.CompilerParams(collective_id=collective_id), name=name,
    )(*arrs)


def _pair_swap(name, arrs):
    n = len(arrs)
    out_shape = [jax.ShapeDtypeStruct((4,) + a.shape[2:], a.dtype) for a in arrs]

    def body(*refs):
        ins, outs = refs[:n], refs[n:2 * n]
        send_sems, recv_sems = refs[2 * n:]
        x, y, c = lax.axis_index("x"), lax.axis_index("y"), lax.axis_index("c")

        def copy(a, q):
            return pltpu.make_async_remote_copy(
                src_ref=ins[a].at[q, 1 - c], dst_ref=outs[a].at[q],
                send_sem=send_sems.at[a * 4 + q], recv_sem=recv_sems.at[a * 4 + q],
                device_id=(x, y, 1 - c), device_id_type=MESH)

        cps = [copy(a, q) for a in range(n) for q in range(4)]
        for cp in cps:
            cp.start()
        for cp in cps:
            cp.wait_recv()
        for cp in cps:
            cp.wait_send()

    return pl.pallas_call(
        body, name=name, out_shape=out_shape,
        in_specs=[ANY] * n, out_specs=[ANY] * n,
        scratch_shapes=[pltpu.SemaphoreType.DMA((n * 4,)), pltpu.SemaphoreType.DMA((n * 4,))],
    )(*arrs)


def _chip_exchange(name, arrs):
    n = len(arrs)
    out_shape = [jax.ShapeDtypeStruct((3,) + a.shape[1:], a.dtype) for a in arrs]

    def body(*refs):
        ins, outs = refs[:n], refs[n:2 * n]
        send_sems, recv_sems = refs[2 * n:]
        x, y, c = lax.axis_index("x"), lax.axis_index("y"), lax.axis_index("c")

        def copy(a, k):
            px = 1 - x if (k >> 1) & 1 else x
            py = 1 - y if k & 1 else y
            return pltpu.make_async_remote_copy(
                src_ref=ins[a].at[2 * px + py], dst_ref=outs[a].at[k - 1],
                send_sem=send_sems.at[a * 3 + k - 1], recv_sem=recv_sems.at[a * 3 + k - 1],
                device_id=(px, py, c), device_id_type=MESH)

        cps = [copy(a, k) for a in range(n) for k in (1, 2, 3)]
        for cp in cps:
            cp.start()
        for cp in cps:
            cp.wait_recv()
        for cp in cps:
            cp.wait_send()

    return pl.pallas_call(
        body, name=name, out_shape=out_shape,
        in_specs=[ANY] * n, out_specs=[ANY] * n,
        scratch_shapes=[pltpu.SemaphoreType.DMA((n * 3,)), pltpu.SemaphoreType.DMA((n * 3,))],
    )(*arrs)


class _Rider:
    def __init__(self, arrays, out_shapes, n_sems, build, aliases=None):
        self.arrays, self.out_shapes, self.n_sems, self.build = list(arrays), list(out_shapes), n_sems, build
        self.aliases = dict(aliases or {})


def _merge_riders(r1, r2):
    n1i, n1o, n1s = len(r1.arrays), len(r1.out_shapes), r1.n_sems

    def build(ins, outs, send_sems, recv_sems):
        a = r1.build(ins[:n1i], outs[:n1o], send_sems.at[pl.ds(0, n1s)], recv_sems.at[pl.ds(0, n1s)])
        b = r2.build(ins[n1i:], outs[n1o:], send_sems.at[pl.ds(n1s, r2.n_sems)], recv_sems.at[pl.ds(n1s, r2.n_sems)])
        return tuple(p + q for p, q in zip(a, b))

    aliases = dict(r1.aliases)
    aliases.update({k + n1i: v + n1o for k, v in r2.aliases.items()})
    return _Rider(r1.arrays + r2.arrays, r1.out_shapes + r2.out_shapes, n1s + r2.n_sems, build, aliases)


def _place():
    x, y, c = lax.axis_index("x"), lax.axis_index("y"), lax.axis_index("c")
    chips = [(1 - x, y), (x, 1 - y), (1 - x, 1 - y)]
    return x, y, c, chips


def _ride_gather_ici(arrs):
    n = len(arrs)

    def build(ins, outs, send_sems, recv_sems):
        x, y, c, chips = _place()
        peers = [(*chip, c) for chip in chips] + [(x, y, 1 - c)]
        me = 4 * x + 2 * y + c
        local = [pltpu.make_async_copy(ins[a], outs[a].at[me], send_sems.at[a * 5 + 4]) for a in range(n)]
        sends, recvs = [], []
        for a in range(n):
            for j, (px, py, pc) in enumerate(peers):
                sends.append(pltpu.make_async_remote_copy(
                    src_ref=ins[a], dst_ref=outs[a].at[me], send_sem=send_sems.at[a * 5 + j],
                    recv_sem=recv_sems.at[a * 5 + j], device_id=(px, py, pc), device_id_type=MESH))
                recvs.append(pltpu.make_async_remote_copy(
                    src_ref=ins[a], dst_ref=outs[a].at[4 * px + 2 * py + pc], send_sem=send_sems.at[a * 5 + j],
                    recv_sem=recv_sems.at[a * 5 + j], device_id=(px, py, pc), device_id_type=MESH))
        return local, sends, recvs

    shapes = [jax.ShapeDtypeStruct((N_DEV,) + a.shape, a.dtype) for a in arrs]
    return _Rider(arrs, shapes, n * 5, build)


def _ride_gather_direct(arrs):
    n = len(arrs)

    def build(ins, outs, send_sems, recv_sems):
        x, y, c, _ = _place()
        me = 4 * x + 2 * y + c
        local = [pltpu.make_async_copy(ins[a], outs[a].at[me], send_sems.at[a * N_DEV + 7]) for a in range(n)]
        sends, recvs = [], []
        for a in range(n):
            for k in range(1, N_DEV):
                px = 1 - x if (k >> 2) & 1 else x
                py = 1 - y if (k >> 1) & 1 else y
                pc = 1 - c if k & 1 else c
                sem = a * N_DEV + k - 1
                sends.append(pltpu.make_async_remote_copy(
                    src_ref=ins[a], dst_ref=outs[a].at[me], send_sem=send_sems.at[sem], recv_sem=recv_sems.at[sem],
                    device_id=(px, py, pc), device_id_type=MESH))
                recvs.append(pltpu.make_async_remote_copy(
                    src_ref=ins[a], dst_ref=outs[a].at[4 * px + 2 * py + pc], send_sem=send_sems.at[sem],
                    recv_sem=recv_sems.at[sem], device_id=(px, py, pc), device_id_type=MESH))
        return local, sends, recvs

    shapes = [jax.ShapeDtypeStruct((N_DEV,) + a.shape, a.dtype) for a in arrs]
    return _Rider(arrs, shapes, n * N_DEV, build)


def _ride_gather_d2d(gathered):
    n = len(gathered)

    def build(ins, outs, send_sems, recv_sems):
        x, y, c, chips = _place()
        sends, recvs = [], []
        for a in range(n):
            for j, (px, py) in enumerate(chips):
                mine = outs[a].at[4 * px + 2 * py + c]
                theirs = outs[a].at[4 * px + 2 * py + 1 - c]
                sends.append(pltpu.make_async_remote_copy(
                    src_ref=mine, dst_ref=mine, send_sem=send_sems.at[a * 3 + j], recv_sem=recv_sems.at[a * 3 + j],
                    device_id=(x, y, 1 - c), device_id_type=MESH))
                recvs.append(pltpu.make_async_remote_copy(
                    src_ref=mine, dst_ref=theirs, send_sem=send_sems.at[a * 3 + j], recv_sem=recv_sems.at[a * 3 + j],
                    device_id=(x, y, 1 - c), device_id_type=MESH))
        return [], sends, recvs

    shapes = [jax.ShapeDtypeStruct(a.shape, a.dtype) for a in gathered]
    return _Rider(gathered, shapes, n * 3, build, aliases={a: a for a in range(n)})


def _ride_pair_swap(arrs):
    n = len(arrs)

    def build(ins, outs, send_sems, recv_sems):
        x, y, c, _ = _place()
        cps = [pltpu.make_async_remote_copy(
            src_ref=ins[a].at[q, 1 - c], dst_ref=outs[a].at[q], send_sem=send_sems.at[a * 4 + q],
            recv_sem=recv_sems.at[a * 4 + q], device_id=(x, y, 1 - c), device_id_type=MESH)
            for a in range(n) for q in range(4)]
        return [], cps, cps

    shapes = [jax.ShapeDtypeStruct((4,) + a.shape[2:], a.dtype) for a in arrs]
    return _Rider(arrs, shapes, n * 4, build)


def _ride_chip_exchange(arrs):
    n = len(arrs)

    def build(ins, outs, send_sems, recv_sems):
        x, y, c, _ = _place()
        cps = []
        for a in range(n):
            for k in (1, 2, 3):
                px = 1 - x if (k >> 1) & 1 else x
                py = 1 - y if k & 1 else y
                cps.append(pltpu.make_async_remote_copy(
                    src_ref=ins[a].at[2 * px + py], dst_ref=outs[a].at[k - 1], send_sem=send_sems.at[a * 3 + k - 1],
                    recv_sem=recv_sems.at[a * 3 + k - 1], device_id=(px, py, c), device_id_type=MESH))
        return [], cps, cps

    shapes = [jax.ShapeDtypeStruct((3,) + a.shape[1:], a.dtype) for a in arrs]
    return _Rider(arrs, shapes, n * 3, build)


def _call(body, name, grid, in_specs, out_specs, out_shape, args, scratch=(), rider=None):
    n_in, n_out, n_scr = len(in_specs), len(out_specs), len(scratch)
    sem = ("arbitrary",) * len(grid)
    if rider is None:
        outs = pl.pallas_call(
            body, name=name, grid=grid, in_specs=in_specs, out_specs=out_specs, out_shape=out_shape,
            scratch_shapes=list(scratch), compiler_params=_params(sem))(*args)
        return outs, []
    ri, ro = len(rider.arrays), len(rider.out_shapes)

    def riding(*refs):
        ins, r_ins = refs[:n_in], refs[n_in:n_in + ri]
        outs = refs[n_in + ri:n_in + ri + n_out]
        r_outs = refs[n_in + ri + n_out:n_in + ri + n_out + ro]
        scr = refs[n_in + ri + n_out + ro:n_in + ri + n_out + ro + n_scr]
        send_sems, recv_sems = refs[-2:]
        first = functools.reduce(jnp.logical_and, [pl.program_id(k) == 0 for k in range(len(grid))])
        last = functools.reduce(jnp.logical_and, [pl.program_id(k) == grid[k] - 1 for k in range(len(grid))])

        @pl.when(first)
        def _():
            local, sends, _ = rider.build(r_ins, r_outs, send_sems, recv_sems)
            for cp in local + sends:
                cp.start()

        body(*ins, *outs, *scr)

        @pl.when(last)
        def _():
            local, sends, recvs = rider.build(r_ins, r_outs, send_sems, recv_sems)
            for cp in recvs:
                cp.wait_recv()
            for cp in sends:
                cp.wait_send()
            for cp in local:
                cp.wait()

    outs = pl.pallas_call(
        riding, name=name, grid=grid,
        in_specs=list(in_specs) + [ANY] * ri, out_specs=list(out_specs) + [ANY] * ro,
        out_shape=list(out_shape) + rider.out_shapes,
        scratch_shapes=list(scratch) + [pltpu.SemaphoreType.DMA((rider.n_sems,)), pltpu.SemaphoreType.DMA((rider.n_sems,))],
        input_output_aliases={n_in + k: n_out + v for k, v in rider.aliases.items()},
        compiler_params=_params(sem))(*args, *rider.arrays)
    return outs[:n_out], outs[n_out:]


def _comm(name, rider):
    def body(dummy_ref, out_ref):
        out_ref[...] = dummy_ref[...]

    dummy = jnp.zeros((SUBLANES, LANES), F32)
    spec = pl.BlockSpec((SUBLANES, LANES), lambda i: (0, 0))
    _, r_outs = _call(body, name, (1,), [spec], [spec], [jax.ShapeDtypeStruct(dummy.shape, F32)], [dummy], rider=rider)
    return r_outs


def _ada_fwd(c_all, w_ada_sh, b_ada_sh):
    nb, d = c_all.shape
    ncol = w_ada_sh.shape[1]

    def body(c_ref, w_ref, b_ref, mod_ref, cact_ref):
        cc = c_ref[...]
        ca = cc * jax.nn.sigmoid(cc)
        cact_ref[...] = ca
        mod_ref[...] = _dot(ca.astype(BF16), w_ref[...].astype(BF16), NN) + b_ref[...]

    return pl.pallas_call(
        body, name="ada_fwd",
        out_shape=[jax.ShapeDtypeStruct((nb, ncol), F32), jax.ShapeDtypeStruct((nb, d), F32)],
        compiler_params=_params(),
    )(c_all, w_ada_sh, b_ada_sh)


def _rms(xv):
    rstd = lax.rsqrt(jnp.mean(xv * xv, axis=-1, keepdims=True) + EPS)
    return xv * rstd, rstd


def _rms_bwd(dxhat, xhat, rstd):
    return rstd * (dxhat - xhat * jnp.mean(dxhat * xhat, axis=-1, keepdims=True))


def _colsum(v):
    return jnp.sum(v, axis=0, keepdims=True)


def _expm1(v, ev):
    series = v * (1.0 + v * (0.5 + v * (1.0 / 6.0 + v * (1.0 / 24.0 + v * (1.0 / 120.0)))))
    return jnp.where(jnp.abs(v) < 0.2, series, ev - 1.0)


def _softplus(v):
    return jnp.maximum(v, 0.0) + jnp.log1p(jnp.exp(-jnp.abs(v)))


def _gelu(v):
    t = jnp.tanh(v * (GELU_K0 + (GELU_K0 * GELU_K1) * (v * v)))
    return 0.5 * v * (1.0 + t), t


def _dgelu(v, t):
    return 0.5 * ((1.0 + t) + (v * (1.0 - t * t)) * (GELU_K0 + (3.0 * GELU_K0 * GELU_K1) * (v * v)))


def _shift_down(v, k, prev8):
    r = pltpu.roll(v, k, 0)
    pr = pltpu.roll(prev8, k, 0)
    row8 = lax.broadcasted_iota(jnp.int32, prev8.shape, 0)
    top = jnp.where(row8 < k, pr, r[0:SUBLANES])
    return jnp.concatenate([top, r[SUBLANES:]], axis=0)


def _shift_up(v, k, next8):
    t = v.shape[0]
    r = pltpu.roll(v, t - k, 0)
    nr = pltpu.roll(next8, SUBLANES - k, 0)
    row8 = lax.broadcasted_iota(jnp.int32, next8.shape, 0)
    bot = jnp.where(row8 >= SUBLANES - k, nr, r[t - SUBLANES:t])
    return jnp.concatenate([r[:t - SUBLANES], bot], axis=0)


def _scan_fwd(a, b, h0):
    t = a.shape[0]
    row = lax.broadcasted_iota(jnp.int32, a.shape, 0)
    s = 1
    while s < min(t, SUBLANES):
        a_sh = pltpu.roll(a, s, 0)
        b_sh = pltpu.roll(b, s, 0)
        m = row >= s
        b = jnp.where(m, a * b_sh + b, b)
        a = jnp.where(m, a * a_sh, a)
        s *= 2
    while s < t:
        b = jnp.concatenate([b[:s], a[s:] * b[:t - s] + b[s:]], axis=0)
        a = jnp.concatenate([a[:s], a[s:] * a[:t - s]], axis=0)
        s *= 2
    return b + a * h0


def _scan_rev(m, b, g_next):
    t = m.shape[0]
    row = lax.broadcasted_iota(jnp.int32, m.shape, 0)
    s = 1
    while s < min(t, SUBLANES):
        m_sh = pltpu.roll(m, t - s, 0)
        b_sh = pltpu.roll(b, t - s, 0)
        msk = row < t - s
        b = jnp.where(msk, m * b_sh + b, b)
        m = jnp.where(msk, m * m_sh, m)
        s *= 2
    while s < t:
        b = jnp.concatenate([m[:t - s] * b[s:] + b[:t - s], b[t - s:]], axis=0)
        m = jnp.concatenate([m[:t - s] * m[s:], m[t - s:]], axis=0)
        s *= 2
    return b + m * g_next


def _lru_gates(u, wa, wx, ba, bx, sp):
    ub = u.astype(BF16)
    r = jax.nn.sigmoid(_dot(ub, wa, NN) + ba)
    i = jax.nn.sigmoid(_dot(ub, wx, NN) + bx)
    log_a = (-RG_C * r) * sp
    a = jnp.exp(log_a)
    mult = jnp.sqrt(-_expm1(log_a, a) * (a + 1.0))
    return ub, r, i, a, mult


def _conv3(p, pp, w_ref, lo):
    p1 = _shift_down(p, 1, pp)
    p2 = _shift_down(p, 2, pp)
    q = (w_ref[0:1, lo:lo + LANES] * p2 + w_ref[1:2, lo:lo + LANES] * p1) + w_ref[2:3, lo:lo + LANES] * p
    return q, p1, p2


def _conv4(xv, xp, w_ref, b_ref, lo):
    x1 = _shift_down(xv, 1, xp)
    x2 = _shift_down(xv, 2, xp)
    x3 = _shift_down(xv, 3, xp)
    u = (((w_ref[0:1, lo:lo + LANES] * x3 + w_ref[1:2, lo:lo + LANES] * x2) + w_ref[2:3, lo:lo + LANES] * x1)
         + w_ref[3:4, lo:lo + LANES] * xv) + b_ref[:, lo:lo + LANES]
    return u, x1, x2, x3


def _mix_in_fwd(x2d, mod6, g_mix, w_in_t, tm, rider=None):
    s, d = x2d.shape
    din = w_in_t.shape[0]

    def body(x_ref, mod_ref, g_ref, w_ref, hn_ref, proj_ref):
        xhat, _ = _rms(x_ref[...])
        hn = ((xhat * g_ref[...]) * (1.0 + mod_ref[1:2, :]) + mod_ref[0:1, :]).astype(BF16)
        hn_ref[...] = hn
        proj_ref[...] = _dot(hn, w_ref[...], NT)

    return _call(
        body, "mix_in_fwd", (s // tm,),
        [pl.BlockSpec((tm, d), lambda i: (i, 0)), _full(mod6.shape), _full(g_mix.shape), _full(w_in_t.shape)],
        [pl.BlockSpec((tm, d), lambda i: (i, 0)), pl.BlockSpec((tm, din), lambda i: (i, 0))],
        [jax.ShapeDtypeStruct((s, d), BF16), jax.ShapeDtypeStruct((s, din), F32)],
        [x2d, mod6, g_mix, w_in_t], rider=rider)


def _mixer_fwd(proj, conv_sc, conv_lru, conv_b, wa_bd, wx_bd, ba, bx, lam, width, rider=None):
    s, din = proj.shape
    t = min(MIX_ROWS, s)
    nblk = width // LANES
    hb = t // SUBLANES

    def body(proj_ref, projp_ref, wsc_ref, wlru_ref, blru_ref, wa_ref, wx_ref, ba_ref, bx_ref, lam_ref,
             ymix_ref, h_ref, hc_ref):
        i = pl.program_id(0)

        @pl.when(i == 0)
        def _():
            hc_ref[...] = jnp.zeros_like(hc_ref)

        has_prev = i > 0
        for j in range(nblk):
            lo = j * LANES

            def col(p, ref=proj_ref):
                return ref[:, p * width + lo:p * width + lo + LANES]

            def prev(p):
                return jnp.where(has_prev, col(p, projp_ref), 0.0)

            p = col(1) * col(2)
            q, _, _ = _conv3(p, prev(1) * prev(2), wsc_ref, lo)
            ymix_ref[:, lo:lo + LANES] = (col(0) * q).astype(BF16)

            u, _, _, _ = _conv4(col(4), prev(4), wlru_ref, blru_ref, lo)
            sp = _softplus(-lam_ref[:, lo:lo + LANES])
            _, r, ig, a, mult = _lru_gates(u, wa_ref[j], wx_ref[j], ba_ref[:, lo:lo + LANES], bx_ref[:, lo:lo + LANES], sp)
            h = _scan_fwd(a, mult * (ig * u), hc_ref[0:1, lo:lo + LANES])
            h_ref[:, lo:lo + LANES] = h
            hc_ref[0:1, lo:lo + LANES] = h[t - 1:t, :]
            gel, _ = _gelu(col(3))
            ymix_ref[:, width + lo:width + lo + LANES] = (gel * h).astype(BF16)

    small = [conv_sc, conv_lru, conv_b, wa_bd, wx_bd, ba, bx, lam]
    return _call(
        body, "mixer_fwd", (s // t,),
        [pl.BlockSpec((t, din), lambda i: (i, 0)),
         pl.BlockSpec((SUBLANES, din), lambda i: (jnp.maximum(i * hb - 1, 0), 0))]
        + [_full(a.shape) for a in small],
        [pl.BlockSpec((t, 2 * width), lambda i: (i, 0)), pl.BlockSpec((t, width), lambda i: (i, 0))],
        [jax.ShapeDtypeStruct((s, 2 * width), BF16), jax.ShapeDtypeStruct((s, width), F32)],
        [proj, proj, *small], scratch=[pltpu.VMEM((SUBLANES, width), F32)], rider=rider)


def _mix_out_fwd(ymix, x2d, w_out, mod6, g_mlp, tm, rider=None):
    s, d = x2d.shape

    def body(y_ref, x_ref, w_ref, mod_ref, g_ref, mix_ref, x2_ref, hn_ref):
        mix = _dot(y_ref[...], w_ref[...], NN)
        mix_ref[...] = mix
        x2 = x_ref[...] + mod_ref[2:3, :] * mix
        x2_ref[...] = x2
        xhat, _ = _rms(x2)
        hn_ref[...] = ((xhat * g_ref[...]) * (1.0 + mod_ref[4:5, :]) + mod_ref[3:4, :]).astype(BF16)

    tile = pl.BlockSpec((tm, d), lambda i: (i, 0))
    return _call(
        body, "mix_out_fwd", (s // tm,),
        [tile, tile, _full(w_out.shape), _full(mod6.shape), _full(g_mlp.shape)],
        [tile, tile, tile],
        [jax.ShapeDtypeStruct((s, d), F32), jax.ShapeDtypeStruct((s, d), F32), jax.ShapeDtypeStruct((s, d), BF16)],
        [ymix, x2d, w_out, mod6, g_mlp], rider=rider)


def _mlp_fwd_loss(hn2, w_up_t, w_down, x2, target, mod6, g_final, tm, tk):
    s, d = hn2.shape
    f = w_up_t.shape[0]
    nk = f // tk

    def body(hn_ref, wu_ref, wd_ref, x2_ref, t_ref, mod_ref, g_ref, z_ref, dx3_ref, dyb_ref, st_ref, y_ref):
        i, k = pl.program_id(0), pl.program_id(1)

        @pl.when(jnp.logical_and(i == 0, k == 0))
        def _():
            st_ref[...] = jnp.zeros_like(st_ref)

        z = jnp.maximum(_dot(hn_ref[...], wu_ref[...], NT), 0.0)
        z_ref[...] = z.astype(BF16)
        part = _dot((z * z).astype(BF16), wd_ref[...], NN)

        @pl.when(k == 0)
        def _():
            y_ref[...] = part

        @pl.when(k > 0)
        def _():
            y_ref[...] += part

        @pl.when(k == nk - 1)
        def _():
            gate = mod_ref[5:6, :]
            yv = y_ref[...]
            xhat, rstd = _rms(x2_ref[...] + gate * yv)
            diff = xhat * g_ref[...] - t_ref[...]
            dyo = diff * (1.0 / d)
            dx3 = _rms_bwd(dyo * g_ref[...], xhat, rstd)
            dx3_ref[...] = dx3
            dyb_ref[...] = (gate * dx3).astype(BF16)
            st_ref[0:1, :] += _colsum(dyo * xhat)
            st_ref[1:2, :] += _colsum(dx3 * yv)
            st_ref[2:3, :] += _colsum(diff * diff)

    tile = pl.BlockSpec((tm, d), lambda i, k: (i, 0))
    wblk = pl.BlockSpec((tk, d), lambda i, k: (k, 0))
    return pl.pallas_call(
        body, name="mlp_fwd_loss", grid=(s // tm, nk),
        in_specs=[tile, wblk, wblk, tile, tile, _full(mod6.shape), _full(g_final.shape)],
        out_specs=[pl.BlockSpec((tm, tk), lambda i, k: (i, k)), tile, tile, _full((SUBLANES, d))],
        out_shape=[jax.ShapeDtypeStruct((s, f), BF16), jax.ShapeDtypeStruct((s, d), F32),
                   jax.ShapeDtypeStruct((s, d), BF16), jax.ShapeDtypeStruct((SUBLANES, d), F32)],
        scratch_shapes=[pltpu.VMEM((tm, d), F32)],
        compiler_params=_params(("arbitrary", "arbitrary")),
    )(hn2, w_up_t, w_down, x2, target, mod6, g_final)


def _mlp_bwd_dx(dyb, z, w_down, w_up_t, tm, tk):
    s, d = dyb.shape
    f = z.shape[1]

    def body(dy_ref, z_ref, wd_ref, wu_ref, dz_ref, dh_ref):
        k = pl.program_id(1)
        dz = ((2.0 * z_ref[...].astype(F32)) * _dot(dy_ref[...], wd_ref[...], NT)).astype(BF16)
        dz_ref[...] = dz
        part = _dot(dz, wu_ref[...], NN)

        @pl.when(k == 0)
        def _():
            dh_ref[...] = part

        @pl.when(k > 0)
        def _():
            dh_ref[...] += part

    return pl.pallas_call(
        body, name="mlp_bwd_dx", grid=(s // tm, f // tk),
        in_specs=[pl.BlockSpec((tm, d), lambda i, k: (i, 0)), pl.BlockSpec((tm, tk), lambda i, k: (i, k)),
                  pl.BlockSpec((tk, d), lambda i, k: (k, 0)), pl.BlockSpec((tk, d), lambda i, k: (k, 0))],
        out_specs=[pl.BlockSpec((tm, tk), lambda i, k: (i, k)), pl.BlockSpec((tm, d), lambda i, k: (i, 0))],
        out_shape=[jax.ShapeDtypeStruct((s, f), BF16), jax.ShapeDtypeStruct((s, d), F32)],
        compiler_params=_params(("parallel", "arbitrary")),
    )(dyb, z, w_down, w_up_t)


def _mlp_bwd_dw(z, dz, dyb, hn2, tm, tk):
    s, d = dyb.shape
    f = z.shape[1]

    def body(z_ref, dz_ref, dy_ref, hn_ref, gd_ref, gu_ref):
        i = pl.program_id(1)

        @pl.when(i == 0)
        def _():
            gd_ref[...] = jnp.zeros_like(gd_ref)
            gu_ref[...] = jnp.zeros_like(gu_ref)

        zf = z_ref[...].astype(F32)
        gd_ref[...] += _dot((zf * zf).astype(BF16), dy_ref[...], TN)
        gu_ref[...] += _dot(dz_ref[...], hn_ref[...], TN)

    return pl.pallas_call(
        body, name="mlp_bwd_dw", grid=(f // tk, s // tm),
        in_specs=[pl.BlockSpec((tm, tk), lambda k, i: (i, k)), pl.BlockSpec((tm, tk), lambda k, i: (i, k)),
                  pl.BlockSpec((tm, d), lambda k, i: (i, 0)), pl.BlockSpec((tm, d), lambda k, i: (i, 0))],
        out_specs=[pl.BlockSpec((tk, d), lambda k, i: (k, 0)), pl.BlockSpec((tk, d), lambda k, i: (k, 0))],
        out_shape=[jax.ShapeDtypeStruct((f, d), F32), jax.ShapeDtypeStruct((f, d), F32)],
        compiler_params=_params(("parallel", "arbitrary")),
    )(z, dz, dyb, hn2)


def _mix_out_bwd(dhn2, x2, dx3, mix, ymix, w_out, mod6, g_mlp, tm, rider=None):
    s, d = x2.shape

    def body(dh_ref, x2_ref, dx3_ref, mix_ref, y_ref, w_ref, mod_ref, g_ref, dx2_ref, dym_ref, gw_ref, st_ref):
        i = pl.program_id(0)

        @pl.when(i == 0)
        def _():
            st_ref[...] = jnp.zeros_like(st_ref)
            gw_ref[...] = jnp.zeros_like(gw_ref)

        dh = dh_ref[...]
        xhat, rstd = _rms(x2_ref[...])
        dn = dh * (1.0 + mod_ref[4:5, :])
        dx2 = dx3_ref[...] + _rms_bwd(dn * g_ref[...], xhat, rstd)
        dx2_ref[...] = dx2
        st_ref[0:1, :] += _colsum(dh)
        st_ref[1:2, :] += _colsum(dh * (xhat * g_ref[...]))
        st_ref[2:3, :] += _colsum(dn * xhat)
        st_ref[3:4, :] += _colsum(dx2 * mix_ref[...])
        dmix = (mod_ref[2:3, :] * dx2).astype(BF16)
        dym_ref[...] = _dot(dmix, w_ref[...], NT)
        gw_ref[...] += _dot(y_ref[...], dmix, TN)

    tile = pl.BlockSpec((tm, d), lambda i: (i, 0))
    return _call(
        body, "mix_out_bwd", (s // tm,),
        [tile, tile, tile, tile, tile, _full(w_out.shape), _full(mod6.shape), _full(g_mlp.shape)],
        [tile, tile, _full((d, d)), _full((SUBLANES, d))],
        [jax.ShapeDtypeStruct((s, d), F32), jax.ShapeDtypeStruct((s, d), F32),
         jax.ShapeDtypeStruct((d, d), F32), jax.ShapeDtypeStruct((SUBLANES, d), F32)],
        [dhn2, x2, dx3, mix, ymix, w_out, mod6, g_mlp], rider=rider)


def _mixer_bwd(proj, dymix, h_all, conv_sc, conv_lru, conv_b, wa_bd, wx_bd, ba, bx, lam, width, rider=None):
    s, din = proj.shape
    t = min(MIX_ROWS, s)
    nt = s // t
    nblk = width // LANES
    hb = t // SUBLANES
    last8 = s // SUBLANES - 1

    def body(proj_ref, projp_ref, projn_ref, dy_ref, dyn_ref, h_ref, hp_ref,
             wsc_ref, wlru_ref, blru_ref, wa_ref, wx_ref, ba_ref, bx_ref, lam_ref,
             dproj_ref, small_ref, gwa_ref, gwx_ref, an_ref, gn_ref, dun_ref):
        i = pl.program_id(0)

        @pl.when(i == 0)
        def _():
            small_ref[...] = jnp.zeros_like(small_ref)
            gwa_ref[...] = jnp.zeros_like(gwa_ref)
            gwx_ref[...] = jnp.zeros_like(gwx_ref)
            an_ref[...] = jnp.zeros_like(an_ref)
            gn_ref[...] = jnp.zeros_like(gn_ref)
            dun_ref[...] = jnp.zeros_like(dun_ref)

        has_prev = i < nt - 1
        has_next = i > 0
        for j in range(nblk):
            lo = j * LANES
            ls = slice(lo, lo + LANES)

            def col(p, ref=proj_ref):
                return ref[:, p * width + lo:p * width + lo + LANES]

            def prev(p):
                return jnp.where(has_prev, col(p, projp_ref), 0.0)

            def nxt(p):
                return jnp.where(has_next, col(p, projn_ref), 0.0)

            def add_row(r, v):
                small_ref[r:r + 1, ls] += _colsum(v)

            sc_b, sc_c, sc_x = col(0), col(1), col(2)
            p = sc_c * sc_x
            q, p1, p2 = _conv3(p, prev(1) * prev(2), wsc_ref, lo)
            dys = dy_ref[:, ls]
            dproj_ref[:, ls] = (dys * q).astype(BF16)
            dq = dys * sc_b
            dqn = jnp.where(has_next, dyn_ref[:, ls], 0.0) * nxt(0)
            dp = (wsc_ref[2:3, ls] * dq + wsc_ref[1:2, ls] * _shift_up(dq, 1, dqn)) + wsc_ref[0:1, ls] * _shift_up(dq, 2, dqn)
            dproj_ref[:, width + lo:width + lo + LANES] = (dp * sc_x).astype(BF16)
            dproj_ref[:, 2 * width + lo:2 * width + lo + LANES] = (dp * sc_c).astype(BF16)
            add_row(0, dq * p2)
            add_row(1, dq * p1)
            add_row(2, dq * p)

            xv = col(4)
            u, x1, x2, x3 = _conv4(xv, prev(4), wlru_ref, blru_ref, lo)
            lam_v = lam_ref[:, ls]
            sp = _softplus(-lam_v)
            wa, wx = wa_ref[j], wx_ref[j]
            ub, r, ig, a, mult = _lru_gates(u, wa, wx, ba_ref[:, ls], bx_ref[:, ls], sp)
            iu = ig * u
            h = h_ref[:, ls]
            hm1 = _shift_down(h, 1, jnp.where(has_prev, hp_ref[:, ls], 0.0))
            lyv = col(3)
            gel, th = _gelu(lyv)
            dyl = dy_ref[:, width + lo:width + lo + LANES]
            dproj_ref[:, 3 * width + lo:3 * width + lo + LANES] = (dyl * h * _dgelu(lyv, th)).astype(BF16)
            a_next = jnp.broadcast_to(an_ref[0:1, ls], (SUBLANES, LANES))
            g = _scan_rev(_shift_up(a, 1, a_next), dyl * gel, gn_ref[0:1, ls])
            an_ref[0:1, ls] = a[0:1, :]
            gn_ref[0:1, ls] = g[0:1, :]
            da = g * hm1
            dmult = g * iu
            diu = g * mult
            dlog_a = da * a - dmult * ((a * a) / mult)
            dpre_a = (dlog_a * (-RG_C * sp)) * (r * (1.0 - r))
            dpre_x = (diu * u) * (ig * (1.0 - ig))
            dab, dxb = dpre_a.astype(BF16), dpre_x.astype(BF16)
            du = diu * ig + _dot(dab, wa, NT) + _dot(dxb, wx, NT)
            gwa_ref[j] += _dot(ub, dab, TN)
            gwx_ref[j] += _dot(ub, dxb, TN)
            dun = dun_ref[:, ls]
            dun_ref[:, ls] = du[0:SUBLANES, :]
            dlx = (((wlru_ref[3:4, ls] * du + wlru_ref[2:3, ls] * _shift_up(du, 1, dun))
                    + wlru_ref[1:2, ls] * _shift_up(du, 2, dun)) + wlru_ref[0:1, ls] * _shift_up(du, 3, dun))
            dproj_ref[:, 4 * width + lo:4 * width + lo + LANES] = dlx.astype(BF16)
            add_row(3, du * x3)
            add_row(4, du * x2)
            add_row(5, du * x1)
            add_row(6, du * xv)
            add_row(7, du)
            add_row(8, dpre_a)
            add_row(9, dpre_x)
            add_row(10, (dlog_a * (RG_C * r)) * jax.nn.sigmoid(-lam_v))

    small = [conv_sc, conv_lru, conv_b, wa_bd, wx_bd, ba, bx, lam]
    rev = lambda i: nt - 1 - i
    return _call(
        body, "mixer_bwd", (nt,),
        [pl.BlockSpec((t, din), lambda i: (rev(i), 0)),
         pl.BlockSpec((SUBLANES, din), lambda i: (jnp.maximum(rev(i) * hb - 1, 0), 0)),
         pl.BlockSpec((SUBLANES, din), lambda i: (jnp.minimum((rev(i) + 1) * hb, last8), 0)),
         pl.BlockSpec((t, 2 * width), lambda i: (rev(i), 0)),
         pl.BlockSpec((SUBLANES, 2 * width), lambda i: (jnp.minimum((rev(i) + 1) * hb, last8), 0)),
         pl.BlockSpec((t, width), lambda i: (rev(i), 0)),
         pl.BlockSpec((SUBLANES, width), lambda i: (jnp.maximum(rev(i) * hb - 1, 0), 0))]
        + [_full(a.shape) for a in small],
        [pl.BlockSpec((t, din), lambda i: (rev(i), 0)), _full((2 * SUBLANES, width)),
         _full(wa_bd.shape), _full(wx_bd.shape)],
        [jax.ShapeDtypeStruct((s, din), BF16), jax.ShapeDtypeStruct((2 * SUBLANES, width), F32),
         jax.ShapeDtypeStruct(wa_bd.shape, F32), jax.ShapeDtypeStruct(wx_bd.shape, F32)],
        [proj, proj, proj, dymix, dymix, h_all, h_all, *small],
        scratch=[pltpu.VMEM((SUBLANES, width), F32), pltpu.VMEM((SUBLANES, width), F32),
                 pltpu.VMEM((SUBLANES, width), F32)], rider=rider)


def _mix_in_bwd_dx(dproj, x2d, dx2, w_in_t, mod6, g_mix, tm, rider=None):
    s, d = x2d.shape
    din = dproj.shape[1]

    def body(dp_ref, x_ref, dx2_ref, w_ref, mod_ref, g_ref, gx_ref, st_ref):
        i = pl.program_id(0)

        @pl.when(i == 0)
        def _():
            st_ref[...] = jnp.zeros_like(st_ref)

        dh = _dot(dp_ref[...], w_ref[...], NN)
        xhat, rstd = _rms(x_ref[...])
        dn = dh * (1.0 + mod_ref[1:2, :])
        gx_ref[...] = dx2_ref[...] + _rms_bwd(dn * g_ref[...], xhat, rstd)
        st_ref[0:1, :] += _colsum(dh)
        st_ref[1:2, :] += _colsum(dh * (xhat * g_ref[...]))
        st_ref[2:3, :] += _colsum(dn * xhat)

    tile = pl.BlockSpec((tm, d), lambda i: (i, 0))
    return _call(
        body, "mix_in_bwd_dx", (s // tm,),
        [pl.BlockSpec((tm, din), lambda i: (i, 0)), tile, tile, _full(w_in_t.shape), _full(mod6.shape),
         _full(g_mix.shape)],
        [tile, _full((SUBLANES, d))],
        [jax.ShapeDtypeStruct((s, d), F32), jax.ShapeDtypeStruct((SUBLANES, d), F32)],
        [dproj, x2d, dx2, w_in_t, mod6, g_mix], rider=rider)


def _mix_in_bwd_dw(dproj, hn1, tm, tn, rider=None):
    s, d = hn1.shape
    din = dproj.shape[1]

    def body(dp_ref, hn_ref, gw_ref):
        i = pl.program_id(1)

        @pl.when(i == 0)
        def _():
            gw_ref[...] = jnp.zeros_like(gw_ref)

        gw_ref[...] += _dot(dp_ref[...], hn_ref[...], TN)

    return _call(
        body, "mix_in_bwd_dw", (din // tn, s // tm),
        [pl.BlockSpec((tm, tn), lambda p, i: (i, p)), pl.BlockSpec((tm, d), lambda p, i: (i, 0))],
        [pl.BlockSpec((tn, d), lambda p, i: (p, 0))],
        [jax.ShapeDtypeStruct((din, d), F32)],
        [dproj, hn1], rider=rider)


def _adamw(w, g, m, v):
    m = ADAM_B1 * m + (1.0 - ADAM_B1) * g
    v = ADAM_B2 * v + (1.0 - ADAM_B2) * (g * g)
    m_hat = m / (1.0 - ADAM_B1 ** ADAM_STEP)
    v_hat = v / (1.0 - ADAM_B2 ** ADAM_STEP)
    delta = -ADAM_LR * (m_hat / (jnp.sqrt(v_hat) + ADAM_EPS) + ADAM_WD * w)
    return delta, m, v


def _pair_sum(g4, h4, core_chip, tr, name):
    _, _, r, n = g4.shape

    def body(sc_ref, g_ref, h_ref, sb_ref, own_ref):
        q = pl.program_id(1)
        ssum = g_ref[...] + h_ref[...]
        sb_ref[...] = ssum.astype(BF16)

        @pl.when(q == sc_ref[1])
        def _():
            own_ref[...] = ssum

    grid_spec = pltpu.PrefetchScalarGridSpec(
        num_scalar_prefetch=1, grid=(r // tr, 4),
        in_specs=[pl.BlockSpec((None, None, tr, n), lambda i, q, sc: (q, sc[0], i, 0)),
                  pl.BlockSpec((None, tr, n), lambda i, q, sc: (q, i, 0))],
        out_specs=[pl.BlockSpec((None, tr, n), lambda i, q, sc: (q, i, 0)),
                   pl.BlockSpec((tr, n), lambda i, q, sc: (i, 0))])
    return pl.pallas_call(
        body, name=name, grid_spec=grid_spec,
        out_shape=[jax.ShapeDtypeStruct((4, r, n), BF16), jax.ShapeDtypeStruct((r, n), F32)],
        compiler_params=_params(("parallel", "arbitrary")),
    )(core_chip, g4, h4)


def _sum4(own, parts, tr, name):
    r, n = own.shape

    def body(o_ref, p_ref, out_ref):
        acc = o_ref[...]
        for k in range(3):
            acc = acc + p_ref[k].astype(F32)
        out_ref[...] = acc

    return pl.pallas_call(
        body, name=name, grid=(r // tr,),
        in_specs=[pl.BlockSpec((tr, n), lambda i: (i, 0)), pl.BlockSpec((3, tr, n), lambda i: (0, i, 0))],
        out_specs=pl.BlockSpec((tr, n), lambda i: (i, 0)),
        out_shape=jax.ShapeDtypeStruct((r, n), F32),
        compiler_params=_params(("parallel",)),
    )(own, parts)


def _sum4_adam(own, parts, w, m, v, tr, name, transposed):
    r, n = own.shape
    rows, cols = w.shape

    def body(o_ref, p_ref, w_ref, m_ref, v_ref, g_ref, d_ref, nm_ref, nv_ref):
        g = o_ref[...]
        for k in range(3):
            g = g + p_ref[k].astype(F32)
        if transposed:
            g = g.T
        g_ref[...] = g
        d_ref[...], nm_ref[...], nv_ref[...] = _adamw(w_ref[...], g, m_ref[...], v_ref[...])

    if transposed:
        g_specs = [pl.BlockSpec((r, tr), lambda i: (0, i)), pl.BlockSpec((3, r, tr), lambda i: (0, 0, i))]
    else:
        g_specs = [pl.BlockSpec((tr, n), lambda i: (i, 0)), pl.BlockSpec((3, tr, n), lambda i: (0, i, 0))]
    tile = pl.BlockSpec((tr, cols), lambda i: (i, 0))
    return pl.pallas_call(
        body, name=name, grid=(rows // tr,),
        in_specs=g_specs + [tile] * 3, out_specs=[tile] * 4,
        out_shape=[jax.ShapeDtypeStruct((rows, cols), F32)] * 4,
        compiler_params=_params(("parallel",)),
    )(own, parts, w, m, v)


def _sum8(parts, tr, name):
    _, rows, n = parts.shape

    def body(p_ref, o_ref):
        acc = p_ref[0]
        for k in range(1, N_DEV):
            acc = acc + p_ref[k]
        o_ref[...] = acc

    return pl.pallas_call(
        body, name=name, grid=(rows // tr,),
        in_specs=[pl.BlockSpec((N_DEV, tr, n), lambda i: (0, i, 0))],
        out_specs=pl.BlockSpec((tr, n), lambda i: (i, 0)),
        out_shape=jax.ShapeDtypeStruct((rows, n), F32),
        compiler_params=_params(("parallel",)),
    )(parts)


def _adam_rows(w, g, m, v, tr, name):
    rows, n = w.shape

    def body(w_ref, g_ref, m_ref, v_ref, d_ref, nm_ref, nv_ref):
        d_ref[...], nm_ref[...], nv_ref[...] = _adamw(w_ref[...], g_ref[...], m_ref[...], v_ref[...])

    tile = pl.BlockSpec((tr, n), lambda i: (i, 0))
    return pl.pallas_call(
        body, name=name, grid=(rows // tr,),
        in_specs=[tile] * 4, out_specs=[tile] * 3,
        out_shape=[jax.ShapeDtypeStruct((rows, n), F32)] * 3,
        compiler_params=_params(("parallel",)),
    )(w, g, m, v)


def _ada_bwd_adam(cact_t, dmod_cols, w, m, v, tr):
    rows, n = w.shape

    def body(c_ref, d_ref, w_ref, m_ref, v_ref, g_ref, dl_ref, nm_ref, nv_ref):
        def term(b):
            return c_ref[b].astype(BF16).astype(F32) * d_ref[b:b + 1, :].astype(BF16).astype(F32)

        g = term(0)
        for b in range(1, N_DEV):
            g = g + term(b)
        g_ref[...] = g
        dl_ref[...], nm_ref[...], nv_ref[...] = _adamw(w_ref[...], g, m_ref[...], v_ref[...])

    tile = pl.BlockSpec((tr, n), lambda i: (i, 0))
    return pl.pallas_call(
        body, name="ada_bwd_adam", grid=(rows // tr,),
        in_specs=[pl.BlockSpec((N_DEV, tr, 1), lambda i: (0, i, 0)), _full(dmod_cols.shape), tile, tile, tile],
        out_specs=[tile] * 4,
        out_shape=[jax.ShapeDtypeStruct((rows, n), F32)] * 4,
        compiler_params=_params(("parallel",)),
    )(cact_t, dmod_cols, w, m, v)


def _adam_small(ws, gs, ms, vs):
    n = len(ws)

    def body(*refs):
        w_r, g_r, m_r, v_r = refs[:n], refs[n:2 * n], refs[2 * n:3 * n], refs[3 * n:4 * n]
        d_r, nm_r, nv_r = refs[4 * n:5 * n], refs[5 * n:6 * n], refs[6 * n:7 * n]
        for k in range(n):
            d_r[k][...], nm_r[k][...], nv_r[k][...] = _adamw(w_r[k][...], g_r[k][...], m_r[k][...], v_r[k][...])

    shapes = [jax.ShapeDtypeStruct(w.shape, F32) for w in ws]
    outs = pl.pallas_call(
        body, name="adam_small", out_shape=shapes * 3, compiler_params=_params(),
    )(*ws, *gs, *ms, *vs)
    return outs[:n], outs[n:2 * n], outs[2 * n:]


def _block_diag(w):
    h, hd, _ = w.shape
    per = LANES // hd
    eye = jnp.eye(per, dtype=w.dtype)
    w5 = w.reshape(h // per, per, hd, 1, hd) * eye[None, :, None, :, None]
    return w5.reshape(h // per, LANES, LANES)


def _block_diag_grad(g, h, hd):
    per = LANES // hd
    g5 = g.reshape(h // per, per, hd, per, hd)
    return jnp.stack([g5[:, a, :, a, :] for a in range(per)], axis=1).reshape(h, hd, hd)


def kernel(x, c, w_ada, b_ada, g_mix, w_in, conv_w_sc, conv_w_lru, conv_b_lru, w_rg_a, b_rg_a, w_rg_x, b_rg_x, lru_lambda, w_out, g_mlp, w_up, w_down, g_final, loss_target, m_w_ada, m_b_ada, m_g_mix, m_w_in, m_conv_w_sc, m_conv_w_lru, m_conv_b_lru, m_w_rg_a, m_b_rg_a, m_w_rg_x, m_b_rg_x, m_lru_lambda, m_w_out, m_g_mlp, m_w_up, m_w_down, m_g_final, v_w_ada, v_b_ada, v_g_mix, v_w_in, v_conv_w_sc, v_conv_w_lru, v_conv_b_lru, v_w_rg_a, v_b_rg_a, v_w_rg_x, v_b_rg_x, v_lru_lambda, v_w_out, v_g_mlp, v_w_up, v_w_down, v_g_final):
    s, d = x.shape[1], x.shape[2]
    width = conv_b_lru.shape[1]
    heads, hd = w_rg_a.shape[1], w_rg_a.shape[2]
    f = w_down.shape[1] * N_DEV
    n_ada = w_ada.shape[2]
    csh = conv_w_sc.shape[2]
    me = 4 * lax.axis_index("x") + 2 * lax.axis_index("y") + lax.axis_index("c")
    tm = min(512, s)
    tm_mlp = min(1024, s)
    tk = 512

    x2d = x[0]
    tgt = loss_target[0]

    pay = jnp.zeros((SUBLANES, d), F32)
    pay = pay.at[0:1, :].set(c)
    pay = pay.at[1:4, 0:csh].set(conv_w_sc[0])
    pay = pay.at[4:8, 0:csh].set(conv_w_lru[0])
    w_in_t_sh = w_in[0].T.astype(BF16)
    w_up_t_sh = w_up[0].T.astype(BF16)
    w_out_sh = w_out[0].astype(BF16)
    w_down_sh = w_down[0].astype(BF16)
    pay_all, w_in_t = _gather2("gather_in", [pay, w_in_t_sh])
    w_in_t = w_in_t.reshape(-1, d)
    c_all = pay_all[:, 0, :]
    conv_sc = pay_all[:, 1:4, 0:csh].transpose(1, 0, 2).reshape(3, width)
    conv_lru = pay_all[:, 4:8, 0:csh].transpose(1, 0, 2).reshape(4, width)

    b_ada_sh = lax.dynamic_slice(b_ada, (0, me * n_ada), (1, n_ada))
    mod_cols, c_act = _ada_fwd(c_all, w_ada[0], b_ada_sh)
    (mod_rows,) = _exchange("scatter_mod", [], [mod_cols.reshape(N_DEV, 1, n_ada)])
    mod_rows, w_out_sh, w_up_t_sh, w_down_sh = lax.optimization_barrier((mod_rows, w_out_sh, w_up_t_sh, w_down_sh))
    (w_out_g,) = _seq_gather2("gather_w_out", 1, [w_out_sh])
    w_up_g, w_down_g = _seq_gather2("gather_mlp_weights", 2, [w_up_t_sh, w_down_sh])
    mod6 = jnp.zeros((SUBLANES, d), F32).at[0:6, :].set(mod_rows.reshape(6, d))

    wa_bd = _block_diag(w_rg_a[0]).astype(BF16)
    wx_bd = _block_diag(w_rg_x[0]).astype(BF16)
    ba = b_rg_a.reshape(1, width)
    bx = b_rg_x.reshape(1, width)
    g_fin = g_final.reshape(1, d)

    (hn1, proj), _ = _mix_in_fwd(x2d, mod6, g_mix, w_in_t, tm)
    (ymix, h_all), _ = _mixer_fwd(proj, conv_sc, conv_lru, conv_b_lru, wa_bd, wx_bd, ba, bx, lru_lambda, width)
    w_out_b = w_out_g.reshape(-1, d)
    (mix, x2, hn2), _ = _mix_out_fwd(ymix, x2d, w_out_b, mod6, g_mlp, tm)
    w_up_t = w_up_g.reshape(-1, d)
    w_down_b = w_down_g.reshape(-1, d)
    z, dx3, dyb, st_fin = _mlp_fwd_loss(hn2, w_up_t, w_down_b, x2, tgt, mod6, g_fin, tm, 4 * tk)

    core_chip = jnp.stack([lax.axis_index("c"), 2 * lax.axis_index("x") + lax.axis_index("y")]).astype(jnp.int32)
    dz, dhn2 = _mlp_bwd_dx(dyb, z, w_down_b, w_up_t, tm_mlp, 2 * tk)
    g_down, g_up_t = _mlp_bwd_dw(z, dz, dyb, hn2, tm_mlp, 2 * tk)
    g_up4, g_down4 = g_up_t.reshape(4, 2, -1, d), g_down.reshape(4, 2, -1, d)
    h_up, h_down = _seq_pair_swap("swap_mlp_grads", 7, [g_up4, g_down4])
    (dx2, dymix, g_out, st_out), _ = _mix_out_bwd(dhn2, x2, dx3, mix, ymix, w_out_b, mod6, g_mlp, tm)
    h_up, h_down, g_out = lax.optimization_barrier((h_up, h_down, g_out))
    sb_up, own_up = _pair_sum(g_up4, h_up, core_chip, 256, "pair_sum_w_up")
    sb_down, own_down = _pair_sum(g_down4, h_down, core_chip, 256, "pair_sum_w_down")
    g_out4 = g_out.reshape(4, 2, -1, d)
    (h_out,) = _seq_pair_swap("swap_w_out_grad", 8, [g_out4])
    p_up, p_down = _seq_chip_exchange("exchange_mlp_grads", 3, [sb_up, sb_down])
    (dproj, g_small, g_wa, g_wx), _ = _mixer_bwd(
        proj, dymix, h_all, conv_sc, conv_lru, conv_b_lru, wa_bd, wx_bd, ba, bx, lru_lambda, width)
    h_out, dproj = lax.optimization_barrier((h_out, dproj))
    sb_out, own_out = _pair_sum(g_out4, h_out, core_chip, g_out4.shape[2], "pair_sum_w_out")
    (p_out,) = _seq_chip_exchange("exchange_w_out_grad", 4, [sb_out])
    (grad_x, st_in), _ = _mix_in_bwd_dx(dproj, x2d, dx2, w_in_t, mod6, g_mix, tm)

    small = jnp.concatenate([
        st_in[0:2], st_out[3:4], st_out[0:2], st_fin[1:2],
        st_in[2:3], st_out[2:3], st_fin[0:1],
        jnp.concatenate([g_small[7:8], g_small[10:11]], axis=1),
        jnp.concatenate([g_small[8:9], g_small[9:10]], axis=1),
        jnp.concatenate([jnp.concatenate([g_small[0:3], jnp.zeros((1, width), F32)], axis=0), g_small[3:7]], axis=1),
        st_fin[2:3],
        _block_diag_grad(g_wa, heads, hd).reshape(-1, d),
        _block_diag_grad(g_wx, heads, hd).reshape(-1, d),
    ], axis=0)

    (small_all,) = _seq_gather2("gather_small_grads", 5, [small])
    (g_in_t,), _ = _mix_in_bwd_dw(dproj, hn1, tm_mlp, 512)
    g_in4 = g_in_t.reshape(4, 2, -1, d)
    (h_in,) = _seq_pair_swap("swap_w_in_grad", 9, [g_in4])
    p_up, p_down, p_out, small_all, g_in_t = lax.optimization_barrier((p_up, p_down, p_out, small_all, g_in_t))

    ad_up = _sum4_adam(own_up, p_up, w_up[0], m_w_up[0], v_w_up[0], 256, "adam_w_up", True)
    h_in, ad_up = lax.optimization_barrier((h_in, ad_up))
    sb_in, own_in = _pair_sum(g_in4, h_in, core_chip, g_in4.shape[2], "pair_sum_w_in")
    (p_in,) = _seq_chip_exchange("exchange_w_in_grad", 6, [sb_in])
    ad_out = _sum4_adam(own_out, p_out, w_out[0], m_w_out[0], v_w_out[0], w_out.shape[1], "adam_w_out", False)
    ad_down = _sum4_adam(own_down, p_down, w_down[0], m_w_down[0], v_w_down[0], 256, "adam_w_down", False)

    gsum = _sum8(small_all, SMALL_ROWS, "sum_small")
    loss = (0.5 / d) * jnp.sum(gsum[15])
    dmod_cols = lax.dynamic_slice(small_all[:, 0:6, :].reshape(N_DEV, 6 * d), (0, me * n_ada), (N_DEV, n_ada))
    g_ada, d_ada, nm_ada, nv_ada = _ada_bwd_adam(c_act[:, :, None], dmod_cols, w_ada[0], m_w_ada[0], v_w_ada[0], 256)

    g_conv = lax.dynamic_slice(gsum[11:15, 0:width], (0, me * csh), (4, csh))
    g_conv_l = lax.dynamic_slice(gsum[11:15, width:2 * width], (0, me * csh), (4, csh))
    small_g = [
        gsum[0:6].reshape(1, 6 * d),
        gsum[6:7],
        g_conv[0:3].reshape(1, 3, csh),
        g_conv_l.reshape(1, 4, csh),
        gsum[9:10, 0:width],
        gsum[16:48].reshape(1, heads, hd, hd),
        gsum[10:11, 0:width].reshape(1, heads, hd),
        gsum[48:80].reshape(1, heads, hd, hd),
        gsum[10:11, width:].reshape(1, heads, hd),
        gsum[9:10, width:],
        gsum[7:8],
        gsum[8],
    ]
    small_w = [b_ada, g_mix, conv_w_sc, conv_w_lru, conv_b_lru, w_rg_a, b_rg_a, w_rg_x, b_rg_x, lru_lambda, g_mlp, g_final]
    small_m = [m_b_ada, m_g_mix, m_conv_w_sc, m_conv_w_lru, m_conv_b_lru, m_w_rg_a, m_b_rg_a, m_w_rg_x, m_b_rg_x,
               m_lru_lambda, m_g_mlp, m_g_final]
    small_v = [v_b_ada, v_g_mix, v_conv_w_sc, v_conv_w_lru, v_conv_b_lru, v_w_rg_a, v_b_rg_a, v_w_rg_x, v_b_rg_x,
               v_lru_lambda, v_g_mlp, v_g_final]
    sd, snm, snv = _adam_small(small_w, small_g, small_m, small_v)
    p_in, ad_out, ad_down, (g_ada, d_ada, nm_ada, nv_ada), sd = lax.optimization_barrier(
        (p_in, ad_out, ad_down, (g_ada, d_ada, nm_ada, nv_ada), sd))
    ad_in = _sum4_adam(own_in, p_in, w_in[0], m_w_in[0], v_w_in[0], 256, "adam_w_in", True)

    def order(ada, w_in_, w_out_, w_up_, w_down_, sm):
        return [ada[None], sm[0], sm[1], w_in_[None], sm[2], sm[3], sm[4], sm[5], sm[6], sm[7], sm[8], sm[9],
                w_out_[None], sm[10], w_up_[None], w_down_[None], sm[11]]

    grads = order(g_ada, ad_in[0], ad_out[0], ad_up[0], ad_down[0], small_g)
    deltas = order(d_ada, ad_in[1], ad_out[1], ad_up[1], ad_down[1], sd)
    new_m = order(nm_ada, ad_in[2], ad_out[2], ad_up[2], ad_down[2], snm)
    new_v = order(nv_ada, ad_in[3], ad_out[3], ad_up[3], ad_down[3], snv)
    return (loss, grad_x[None], *grads, *deltas, *new_m, *new_v)
```

```python
import functools

import jax
import jax.numpy as jnp
from jax import lax
from jax.experimental import pallas as pl
from jax.experimental.pallas import tpu as pltpu
from jax.experimental.pallas import tpu_sc as plsc

F32 = jnp.float32
BF16 = jnp.bfloat16
N_DEV = 8
EPS = 1e-6
RG_C = 8.0
GELU_K0 = 0.7978845608028654
GELU_K1 = 0.044715
ADAM_LR = 0.001
ADAM_B1 = 0.9
ADAM_B2 = 0.999
ADAM_EPS = 1e-08
ADAM_WD = 0.01
ADAM_STEP = 10
LANES = 128
SUBLANES = 8
VMEM_LIMIT = 52 * 1024 * 1024
MIX_ROWS = 256
SMALL_ROWS = 80

MESH = pl.DeviceIdType.MESH
ANY = pl.BlockSpec(memory_space=pl.ANY)
NN = ((1,), (0,))
NT = ((1,), (1,))
TN = ((0,), (0,))


def _dot(a, b, dims):
    return lax.dot_general(a, b, (dims, ((), ())), preferred_element_type=F32)


def _params(sem=None):
    return pltpu.CompilerParams(dimension_semantics=sem, vmem_limit_bytes=VMEM_LIMIT)


def _full(shape):
    nd = len(shape)
    return pl.BlockSpec(shape, lambda *_: (0,) * nd)


def _exchange(name, gathers, scatters):
    n_g = len(gathers)
    arrs = list(gathers) + list(scatters)
    n = len(arrs)
    out_shape = [jax.ShapeDtypeStruct((N_DEV,) + a.shape, a.dtype) for a in gathers]
    out_shape += [jax.ShapeDtypeStruct(a.shape, a.dtype) for a in scatters]

    def body(*refs):
        ins, outs = refs[:n], refs[n:2 * n]
        send_sems, recv_sems, local_sems = refs[2 * n:]
        x, y, c = lax.axis_index("x"), lax.axis_index("y"), lax.axis_index("c")
        me = 4 * x + 2 * y + c

        def src(a, dev):
            return ins[a] if a < n_g else ins[a].at[dev]

        def peer_of(k):
            px = 1 - x if (k >> 2) & 1 else x
            py = 1 - y if (k >> 1) & 1 else y
            pc = 1 - c if k & 1 else c
            return (px, py, pc), 4 * px + 2 * py + pc

        local = [pltpu.make_async_copy(src(a, me), outs[a].at[me], local_sems.at[a]) for a in range(n)]
        for cp in local:
            cp.start()
        sends = []
        for k in range(1, N_DEV):
            peer, pidx = peer_of(k)
            for a in range(n):
                cp = pltpu.make_async_remote_copy(
                    src_ref=src(a, pidx), dst_ref=outs[a].at[me],
                    send_sem=send_sems.at[a * (N_DEV - 1) + k - 1], recv_sem=recv_sems.at[a * (N_DEV - 1) + k - 1],
                    device_id=peer, device_id_type=MESH)
                cp.start()
                sends.append(cp)
        for k in range(1, N_DEV):
            peer, pidx = peer_of(k)
            for a in range(n):
                pltpu.make_async_remote_copy(
                    src_ref=src(a, pidx), dst_ref=outs[a].at[pidx],
                    send_sem=send_sems.at[a * (N_DEV - 1) + k - 1], recv_sem=recv_sems.at[a * (N_DEV - 1) + k - 1],
                    device_id=peer, device_id_type=MESH).wait_recv()
        for cp in sends:
            cp.wait_send()
        for cp in local:
            cp.wait()

    return pl.pallas_call(
        body, name=name, out_shape=out_shape,
        in_specs=[ANY] * n, out_specs=[ANY] * n,
        scratch_shapes=[pltpu.SemaphoreType.DMA((n * (N_DEV - 1),)),
                        pltpu.SemaphoreType.DMA((n * (N_DEV - 1),)),
                        pltpu.SemaphoreType.DMA((n,))],
    )(*arrs)


def _gather2(name, arrs):
    n = len(arrs)
    per = 7
    out_shape = [jax.ShapeDtypeStruct((N_DEV,) + a.shape, a.dtype) for a in arrs]

    def body(*refs):
        ins, outs = refs[:n], refs[n:2 * n]
        send_sems, recv_sems, local_sems = refs[2 * n:]
        x, y, c = lax.axis_index("x"), lax.axis_index("y"), lax.axis_index("c")
        sib = (x, y, 1 - c)
        chips = [(1 - x, y), (x, 1 - y), (1 - x, 1 - y)]

        def slot(a, px, py, pc):
            return outs[a].at[4 * px + 2 * py + pc]

        def copy(a, k, block, to, src=None):
            return pltpu.make_async_remote_copy(
                src_ref=slot(a, *block) if src is None else src, dst_ref=slot(a, *block),
                send_sem=send_sems.at[a * per + k], recv_sem=recv_sems.at[a * per + k],
                device_id=to, device_id_type=MESH)

        local = [pltpu.make_async_copy(ins[a], slot(a, x, y, c), local_sems.at[a]) for a in range(n)]
        for cp in local:
            cp.start()
        first = []
        for a in range(n):
            first += [copy(a, 1 + j, (x, y, c), (*chip, c), src=ins[a]) for j, chip in enumerate(chips)]
        for a in range(n):
            first.append(copy(a, 0, (x, y, c), sib, src=ins[a]))
        for cp in first:
            cp.start()
        passed = []
        for a in range(n):
            for j, chip in enumerate(chips):
                copy(a, 1 + j, (*chip, c), (x, y, c)).wait_recv()
                cp = copy(a, 4 + j, (*chip, c), sib)
                cp.start()
                passed.append(cp)
        for a in range(n):
            copy(a, 0, sib, (x, y, c)).wait_recv()
            for j, chip in enumerate(chips):
                copy(a, 4 + j, (*chip, 1 - c), (x, y, c)).wait_recv()
        for cp in first + passed:
            cp.wait_send()
        for cp in local:
            cp.wait()

    return pl.pallas_call(
        body, name=name, out_shape=out_shape,
        in_specs=[ANY] * n, out_specs=[ANY] * n,
        scratch_shapes=[pltpu.SemaphoreType.DMA((n * per,)), pltpu.SemaphoreType.DMA((n * per,)),
                        pltpu.SemaphoreType.DMA((n,))],
    )(*arrs)


def _seq_gather2(name, collective_id, arrs):
    n = len(arrs)
    per = 7

    def body(*refs):
        ins, outs = refs[:n], refs[n:2 * n]
        send_sems, recv_sems, local_sems = refs[2 * n:]
        x, y, c = lax.axis_index("x"), lax.axis_index("y"), lax.axis_index("c")
        sib = (x, y, 1 - c)
        chips = [(1 - x, y), (x, 1 - y), (1 - x, 1 - y)]
        barrier = pltpu.get_barrier_semaphore()
        for peer in [sib] + [(*chip, c) for chip in chips]:
            pl.semaphore_signal(barrier, inc=1, device_id=peer, device_id_type=MESH)
        pl.semaphore_wait(barrier, 4)

        def slot(a, px, py, pc):
            return outs[a].at[4 * px + 2 * py + pc]

        def copy(a, k, block, to, src=None):
            return pltpu.make_async_remote_copy(
                src_ref=slot(a, *block) if src is None else src, dst_ref=slot(a, *block),
                send_sem=send_sems.at[a * per + k], recv_sem=recv_sems.at[a * per + k],
                device_id=to, device_id_type=MESH)

        local = [pltpu.make_async_copy(ins[a], slot(a, x, y, c), local_sems.at[a]) for a in range(n)]
        for cp in local:
            cp.start()
        first = []
        for a in range(n):
            first += [copy(a, 1 + j, (x, y, c), (*chip, c), src=ins[a]) for j, chip in enumerate(chips)]
        for a in range(n):
            first.append(copy(a, 0, (x, y, c), sib, src=ins[a]))
        for cp in first:
            cp.start()
        passed = []
        for a in range(n):
            for j, chip in enumerate(chips):
                copy(a, 1 + j, (*chip, c), (x, y, c)).wait_recv()
                cp = copy(a, 4 + j, (*chip, c), sib)
                cp.start()
                passed.append(cp)
        for a in range(n):
            copy(a, 0, sib, (x, y, c)).wait_recv()
            for j, chip in enumerate(chips):
                copy(a, 4 + j, (*chip, 1 - c), (x, y, c)).wait_recv()
        for cp in first + passed:
            cp.wait_send()
        for cp in local:
            cp.wait()

    return pl.kernel(
        body, out_type=[jax.ShapeDtypeStruct((N_DEV,) + a.shape, a.dtype) for a in arrs],
        mesh=plsc.ScalarSubcoreMesh(axis_name="seq", num_cores=1),
        scratch_types=[pltpu.SemaphoreType.DMA((n * per,)), pltpu.SemaphoreType.DMA((n * per,)),
                       pltpu.SemaphoreType.DMA((n,))],
        compiler_params=pltpu.CompilerParams(collective_id=collective_id), name=name,
    )(*arrs)


def _seq_chip_exchange(name, collective_id, arrs):
    n = len(arrs)

    def body(*refs):
        ins, outs = refs[:n], refs[n:2 * n]
        send_sems, recv_sems = refs[2 * n:]
        x, y, c = lax.axis_index("x"), lax.axis_index("y"), lax.axis_index("c")

        def peer(k):
            return (1 - x if (k >> 1) & 1 else x), (1 - y if k & 1 else y)

        barrier = pltpu.get_barrier_semaphore()
        for k in (1, 2, 3):
            pl.semaphore_signal(barrier, inc=1, device_id=(*peer(k), c), device_id_type=MESH)
        pl.semaphore_wait(barrier, 3)

        def copy(a, k):
            px, py = peer(k)
            return pltpu.make_async_remote_copy(
                src_ref=ins[a].at[2 * px + py], dst_ref=outs[a].at[k - 1],
                send_sem=send_sems.at[a * 3 + k - 1], recv_sem=recv_sems.at[a * 3 + k - 1],
                device_id=(px, py, c), device_id_type=MESH)

        cps = [copy(a, k) for a in range(n) for k in (1, 2, 3)]
        for cp in cps:
            cp.start()
        for cp in cps:
            cp.wait_recv()
        for cp in cps:
            cp.wait_send()

    return pl.kernel(
        body, out_type=[jax.ShapeDtypeStruct((3,) + a.shape[1:], a.dtype) for a in arrs],
        mesh=plsc.ScalarSubcoreMesh(axis_name="seq", num_cores=1),
        scratch_types=[pltpu.SemaphoreType.DMA((n * 3,)), pltpu.SemaphoreType.DMA((n * 3,))],
        compiler_params=pltpu.CompilerParams(collective_id=collective_id), name=name,
    )(*arrs)


def _seq_pair_swap(name, collective_id, arrs):
    n = len(arrs)

    def body(*refs):
        ins, outs = refs[:n], refs[n:2 * n]
        send_sems, recv_sems = refs[2 * n:]
        x, y, c = lax.axis_index("x"), lax.axis_index("y"), lax.axis_index("c")
        barrier = pltpu.get_barrier_semaphore()
        pl.semaphore_signal(barrier, inc=1, device_id=(x, y, 1 - c), device_id_type=MESH)
        pl.semaphore_wait(barrier, 1)

        def copy(a, q):
            return pltpu.make_async_remote_copy(
                src_ref=ins[a].at[q, 1 - c], dst_ref=outs[a].at[q],
                send_sem=send_sems.at[a * 4 + q], recv_sem=recv_sems.at[a * 4 + q],
                device_id=(x, y, 1 - c), device_id_type=MESH)

        cps = [copy(a, q) for a in range(n) for q in range(4)]
        for cp in cps:
            cp.start()
        for cp in cps:
            cp.wait_recv()
        for cp in cps:
            cp.wait_send()

    return pl.kernel(
        body, out_type=[jax.ShapeDtypeStruct((4,) + a.shape[2:], a.dtype) for a in arrs],
        mesh=plsc.ScalarSubcoreMesh(axis_name="seq", num_cores=1),
        scratch_types=[pltpu.SemaphoreType.DMA((n * 4,)), pltpu.SemaphoreType.DMA((n * 4,))],
        compiler_params=pltpu.CompilerParams(collective_id=collective_id), name=name,
    )(*arrs)


def _pair_swap(name, arrs):
    n = len(arrs)
    out_shape = [jax.ShapeDtypeStruct((4,) + a.shape[2:], a.dtype) for a in arrs]

    def body(*refs):
        ins, outs = refs[:n], refs[n:2 * n]
        send_sems, recv_sems = refs[2 * n:]
        x, y, c = lax.axis_index("x"), lax.axis_index("y"), lax.axis_index("c")

        def copy(a, q):
            return pltpu.make_async_remote_copy(
                src_ref=ins[a].at[q, 1 - c], dst_ref=outs[a].at[q],
                send_sem=send_sems.at[a * 4 + q], recv_sem=recv_sems.at[a * 4 + q],
                device_id=(x, y, 1 - c), device_id_type=MESH)

        cps = [copy(a, q) for a in range(n) for q in range(4)]
        for cp in cps:
            cp.start()
        for cp in cps:
            cp.wait_recv()
        for cp in cps:
            cp.wait_send()

    return pl.pallas_call(
        body, name=name, out_shape=out_shape,
        in_specs=[ANY] * n, out_specs=[ANY] * n,
        scratch_shapes=[pltpu.SemaphoreType.DMA((n * 4,)), pltpu.SemaphoreType.DMA((n * 4,))],
    )(*arrs)


def _chip_exchange(name, arrs):
    n = len(arrs)
    out_shape = [jax.ShapeDtypeStruct((3,) + a.shape[1:], a.dtype) for a in arrs]

    def body(*refs):
        ins, outs = refs[:n], refs[n:2 * n]
        send_sems, recv_sems = refs[2 * n:]
        x, y, c = lax.axis_index("x"), lax.axis_index("y"), lax.axis_index("c")

        def copy(a, k):
            px = 1 - x if (k >> 1) & 1 else x
            py = 1 - y if k & 1 else y
            return pltpu.make_async_remote_copy(
                src_ref=ins[a].at[2 * px + py], dst_ref=outs[a].at[k - 1],
                send_sem=send_sems.at[a * 3 + k - 1], recv_sem=recv_sems.at[a * 3 + k - 1],
                device_id=(px, py, c), device_id_type=MESH)

        cps = [copy(a, k) for a in range(n) for k in (1, 2, 3)]
        for cp in cps:
            cp.start()
        for cp in cps:
            cp.wait_recv()
        for cp in cps:
            cp.wait_send()

    return pl.pallas_call(
        body, name=name, out_shape=out_shape,
        in_specs=[ANY] * n, out_specs=[ANY] * n,
        scratch_shapes=[pltpu.SemaphoreType.DMA((n * 3,)), pltpu.SemaphoreType.DMA((n * 3,))],
    )(*arrs)


class _Rider:
    def __init__(self, arrays, out_shapes, n_sems, build, aliases=None):
        self.arrays, self.out_shapes, self.n_sems, self.build = list(arrays), list(out_shapes), n_sems, build
        self.aliases = dict(aliases or {})


def _merge_riders(r1, r2):
    n1i, n1o, n1s = len(r1.arrays), len(r1.out_shapes), r1.n_sems

    def build(ins, outs, send_sems, recv_sems):
        a = r1.build(ins[:n1i], outs[:n1o], send_sems.at[pl.ds(0, n1s)], recv_sems.at[pl.ds(0, n1s)])
        b = r2.build(ins[n1i:], outs[n1o:], send_sems.at[pl.ds(n1s, r2.n_sems)], recv_sems.at[pl.ds(n1s, r2.n_sems)])
        return tuple(p + q for p, q in zip(a, b))

    aliases = dict(r1.aliases)
    aliases.update({k + n1i: v + n1o for k, v in r2.aliases.items()})
    return _Rider(r1.arrays + r2.arrays, r1.out_shapes + r2.out_shapes, n1s + r2.n_sems, build, aliases)


def _place():
    x, y, c = lax.axis_index("x"), lax.axis_index("y"), lax.axis_index("c")
    chips = [(1 - x, y), (x, 1 - y), (1 - x, 1 - y)]
    return x, y, c, chips


def _ride_gather_ici(arrs):
    n = len(arrs)

    def build(ins, outs, send_sems, recv_sems):
        x, y, c, chips = _place()
        peers = [(*chip, c) for chip in chips] + [(x, y, 1 - c)]
        me = 4 * x + 2 * y + c
        local = [pltpu.make_async_copy(ins[a], outs[a].at[me], send_sems.at[a * 5 + 4]) for a in range(n)]
        sends, recvs = [], []
        for a in range(n):
            for j, (px, py, pc) in enumerate(peers):
                sends.append(pltpu.make_async_remote_copy(
                    src_ref=ins[a], dst_ref=outs[a].at[me], send_sem=send_sems.at[a * 5 + j],
                    recv_sem=recv_sems.at[a * 5 + j], device_id=(px, py, pc), device_id_type=MESH))
                recvs.append(pltpu.make_async_remote_copy(
                    src_ref=ins[a], dst_ref=outs[a].at[4 * px + 2 * py + pc], send_sem=send_sems.at[a * 5 + j],
                    recv_sem=recv_sems.at[a * 5 + j], device_id=(px, py, pc), device_id_type=MESH))
        return local, sends, recvs

    shapes = [jax.ShapeDtypeStruct((N_DEV,) + a.shape, a.dtype) for a in arrs]
    return _Rider(arrs, shapes, n * 5, build)


def _ride_gather_direct(arrs):
    n = len(arrs)

    def build(ins, outs, send_sems, recv_sems):
        x, y, c, _ = _place()
        me = 4 * x + 2 * y + c
        local = [pltpu.make_async_copy(ins[a], outs[a].at[me], send_sems.at[a * N_DEV + 7]) for a in range(n)]
        sends, recvs = [], []
        for a in range(n):
            for k in range(1, N_DEV):
                px = 1 - x if (k >> 2) & 1 else x
                py = 1 - y if (k >> 1) & 1 else y
                pc = 1 - c if k & 1 else c
                sem = a * N_DEV + k - 1
                sends.append(pltpu.make_async_remote_copy(
                    src_ref=ins[a], dst_ref=outs[a].at[me], send_sem=send_sems.at[sem], recv_sem=recv_sems.at[sem],
                    device_id=(px, py, pc), device_id_type=MESH))
                recvs.append(pltpu.make_async_remote_copy(
                    src_ref=ins[a], dst_ref=outs[a].at[4 * px + 2 * py + pc], send_sem=send_sems.at[sem],
                    recv_sem=recv_sems.at[sem], device_id=(px, py, pc), device_id_type=MESH))
        return local, sends, recvs

    shapes = [jax.ShapeDtypeStruct((N_DEV,) + a.shape, a.dtype) for a in arrs]
    return _Rider(arrs, shapes, n * N_DEV, build)


def _ride_gather_d2d(gathered):
    n = len(gathered)

    def build(ins, outs, send_sems, recv_sems):
        x, y, c, chips = _place()
        sends, recvs = [], []
        for a in range(n):
            for j, (px, py) in enumerate(chips):
                mine = outs[a].at[4 * px + 2 * py + c]
                theirs = outs[a].at[4 * px + 2 * py + 1 - c]
                sends.append(pltpu.make_async_remote_copy(
                    src_ref=mine, dst_ref=mine, send_sem=send_sems.at[a * 3 + j], recv_sem=recv_sems.at[a * 3 + j],
                    device_id=(x, y, 1 - c), device_id_type=MESH))
                recvs.append(pltpu.make_async_remote_copy(
                    src_ref=mine, dst_ref=theirs, send_sem=send_sems.at[a * 3 + j], recv_sem=recv_sems.at[a * 3 + j],
                    device_id=(x, y, 1 - c), device_id_type=MESH))
        return [], sends, recvs

    shapes = [jax.ShapeDtypeStruct(a.shape, a.dtype) for a in gathered]
    return _Rider(gathered, shapes, n * 3, build, aliases={a: a for a in range(n)})


def _ride_pair_swap(arrs):
    n = len(arrs)

    def build(ins, outs, send_sems, recv_sems):
        x, y, c, _ = _place()
        cps = [pltpu.make_async_remote_copy(
            src_ref=ins[a].at[q, 1 - c], dst_ref=outs[a].at[q], send_sem=send_sems.at[a * 4 + q],
            recv_sem=recv_sems.at[a * 4 + q], device_id=(x, y, 1 - c), device_id_type=MESH)
            for a in range(n) for q in range(4)]
        return [], cps, cps

    shapes = [jax.ShapeDtypeStruct((4,) + a.shape[2:], a.dtype) for a in arrs]
    return _Rider(arrs, shapes, n * 4, build)


def _ride_chip_exchange(arrs):
    n = len(arrs)

    def build(ins, outs, send_sems, recv_sems):
        x, y, c, _ = _place()
        cps = []
        for a in range(n):
            for k in (1, 2, 3):
                px = 1 - x if (k >> 1) & 1 else x
                py = 1 - y if k & 1 else y
                cps.append(pltpu.make_async_remote_copy(
                    src_ref=ins[a].at[2 * px + py], dst_ref=outs[a].at[k - 1], send_sem=send_sems.at[a * 3 + k - 1],
                    recv_sem=recv_sems.at[a * 3 + k - 1], device_id=(px, py, c), device_id_type=MESH))
        return [], cps, cps

    shapes = [jax.ShapeDtypeStruct((3,) + a.shape[1:], a.dtype) for a in arrs]
    return _Rider(arrs, shapes, n * 3, build)


def _call(body, name, grid, in_specs, out_specs, out_shape, args, scratch=(), rider=None):
    n_in, n_out, n_scr = len(in_specs), len(out_specs), len(scratch)
    sem = ("arbitrary",) * len(grid)
    if rider is None:
        outs = pl.pallas_call(
            body, name=name, grid=grid, in_specs=in_specs, out_specs=out_specs, out_shape=out_shape,
            scratch_shapes=list(scratch), compiler_params=_params(sem))(*args)
        return outs, []
    ri, ro = len(rider.arrays), len(rider.out_shapes)

    def riding(*refs):
        ins, r_ins = refs[:n_in], refs[n_in:n_in + ri]
        outs = refs[n_in + ri:n_in + ri + n_out]
        r_outs = refs[n_in + ri + n_out:n_in + ri + n_out + ro]
        scr = refs[n_in + ri + n_out + ro:n_in + ri + n_out + ro + n_scr]
        send_sems, recv_sems = refs[-2:]
        first = functools.reduce(jnp.logical_and, [pl.program_id(k) == 0 for k in range(len(grid))])
        last = functools.reduce(jnp.logical_and, [pl.program_id(k) == grid[k] - 1 for k in range(len(grid))])

        @pl.when(first)
        def _():
            local, sends, _ = rider.build(r_ins, r_outs, send_sems, recv_sems)
            for cp in local + sends:
                cp.start()

        body(*ins, *outs, *scr)

        @pl.when(last)
        def _():
            local, sends, recvs = rider.build(r_ins, r_outs, send_sems, recv_sems)
            for cp in recvs:
                cp.wait_recv()
            for cp in sends:
                cp.wait_send()
            for cp in local:
                cp.wait()

    outs = pl.pallas_call(
        riding, name=name, grid=grid,
        in_specs=list(in_specs) + [ANY] * ri, out_specs=list(out_specs) + [ANY] * ro,
        out_shape=list(out_shape) + rider.out_shapes,
        scratch_shapes=list(scratch) + [pltpu.SemaphoreType.DMA((rider.n_sems,)), pltpu.SemaphoreType.DMA((rider.n_sems,))],
        input_output_aliases={n_in + k: n_out + v for k, v in rider.aliases.items()},
        compiler_params=_params(sem))(*args, *rider.arrays)
    return outs[:n_out], outs[n_out:]


def _comm(name, rider):
    def body(dummy_ref, out_ref):
        out_ref[...] = dummy_ref[...]

    dummy = jnp.zeros((SUBLANES, LANES), F32)
    spec = pl.BlockSpec((SUBLANES, LANES), lambda i: (0, 0))
    _, r_outs = _call(body, name, (1,), [spec], [spec], [jax.ShapeDtypeStruct(dummy.shape, F32)], [dummy], rider=rider)
    return r_outs


def _ada_fwd(c_all, w_ada_sh, b_ada_sh):
    nb, d = c_all.shape
    ncol = w_ada_sh.shape[1]

    def body(c_ref, w_ref, b_ref, mod_ref, cact_ref):
        cc = c_ref[...]
        ca = cc * jax.nn.sigmoid(cc)
        cact_ref[...] = ca
        mod_ref[...] = _dot(ca.astype(BF16), w_ref[...].astype(BF16), NN) + b_ref[...]

    return pl.pallas_call(
        body, name="ada_fwd",
        out_shape=[jax.ShapeDtypeStruct((nb, ncol), F32), jax.ShapeDtypeStruct((nb, d), F32)],
        compiler_params=_params(),
    )(c_all, w_ada_sh, b_ada_sh)


def _rms(xv):
    rstd = lax.rsqrt(jnp.mean(xv * xv, axis=-1, keepdims=True) + EPS)
    return xv * rstd, rstd


def _rms_bwd(dxhat, xhat, rstd):
    return rstd * (dxhat - xhat * jnp.mean(dxhat * xhat, axis=-1, keepdims=True))


def _colsum(v):
    return jnp.sum(v, axis=0, keepdims=True)


def _expm1(v, ev):
    series = v * (1.0 + v * (0.5 + v * (1.0 / 6.0 + v * (1.0 / 24.0 + v * (1.0 / 120.0)))))
    return jnp.where(jnp.abs(v) < 0.2, series, ev - 1.0)


def _softplus(v):
    return jnp.maximum(v, 0.0) + jnp.log1p(jnp.exp(-jnp.abs(v)))


def _gelu(v):
    t = jnp.tanh(v * (GELU_K0 + (GELU_K0 * GELU_K1) * (v * v)))
    return 0.5 * v * (1.0 + t), t


def _dgelu(v, t):
    return 0.5 * ((1.0 + t) + (v * (1.0 - t * t)) * (GELU_K0 + (3.0 * GELU_K0 * GELU_K1) * (v * v)))


def _shift_down(v, k, prev8):
    r = pltpu.roll(v, k, 0)
    pr = pltpu.roll(prev8, k, 0)
    row8 = lax.broadcasted_iota(jnp.int32, prev8.shape, 0)
    top = jnp.where(row8 < k, pr, r[0:SUBLANES])
    return jnp.concatenate([top, r[SUBLANES:]], axis=0)


def _shift_up(v, k, next8):
    t = v.shape[0]
    r = pltpu.roll(v, t - k, 0)
    nr = pltpu.roll(next8, SUBLANES - k, 0)
    row8 = lax.broadcasted_iota(jnp.int32, next8.shape, 0)
    bot = jnp.where(row8 >= SUBLANES - k, nr, r[t - SUBLANES:t])
    return jnp.concatenate([r[:t - SUBLANES], bot], axis=0)


def _scan_fwd(a, b, h0):
    t = a.shape[0]
    row = lax.broadcasted_iota(jnp.int32, a.shape, 0)
    s = 1
    while s < min(t, SUBLANES):
        a_sh = pltpu.roll(a, s, 0)
        b_sh = pltpu.roll(b, s, 0)
        m = row >= s
        b = jnp.where(m, a * b_sh + b, b)
        a = jnp.where(m, a * a_sh, a)
        s *= 2
    while s < t:
        b = jnp.concatenate([b[:s], a[s:] * b[:t - s] + b[s:]], axis=0)
        a = jnp.concatenate([a[:s], a[s:] * a[:t - s]], axis=0)
        s *= 2
    return b + a * h0


def _scan_rev(m, b, g_next):
    t = m.shape[0]
    row = lax.broadcasted_iota(jnp.int32, m.shape, 0)
    s = 1
    while s < min(t, SUBLANES):
        m_sh = pltpu.roll(m, t - s, 0)
        b_sh = pltpu.roll(b, t - s, 0)
        msk = row < t - s
        b = jnp.where(msk, m * b_sh + b, b)
        m = jnp.where(msk, m * m_sh, m)
        s *= 2
    while s < t:
        b = jnp.concatenate([m[:t - s] * b[s:] + b[:t - s], b[t - s:]], axis=0)
        m = jnp.concatenate([m[:t - s] * m[s:], m[t - s:]], axis=0)
        s *= 2
    return b + m * g_next


def _lru_gates(u, wa, wx, ba, bx, sp):
    ub = u.astype(BF16)
    r = jax.nn.sigmoid(_dot(ub, wa, NN) + ba)
    i = jax.nn.sigmoid(_dot(ub, wx, NN) + bx)
    log_a = (-RG_C * r) * sp
    a = jnp.exp(log_a)
    mult = jnp.sqrt(-_expm1(log_a, a) * (a + 1.0))
    return ub, r, i, a, mult


def _conv3(p, pp, w_ref, lo):
    p1 = _shift_down(p, 1, pp)
    p2 = _shift_down(p, 2, pp)
    q = (w_ref[0:1, lo:lo + LANES] * p2 + w_ref[1:2, lo:lo + LANES] * p1) + w_ref[2:3, lo:lo + LANES] * p
    return q, p1, p2


def _conv4(xv, xp, w_ref, b_ref, lo):
    x1 = _shift_down(xv, 1, xp)
    x2 = _shift_down(xv, 2, xp)
    x3 = _shift_down(xv, 3, xp)
    u = (((w_ref[0:1, lo:lo + LANES] * x3 + w_ref[1:2, lo:lo + LANES] * x2) + w_ref[2:3, lo:lo + LANES] * x1)
         + w_ref[3:4, lo:lo + LANES] * xv) + b_ref[:, lo:lo + LANES]
    return u, x1, x2, x3


def _mix_in_fwd(x2d, mod6, g_mix, w_in_t, tm, rider=None):
    s, d = x2d.shape
    din = w_in_t.shape[0]

    def body(x_ref, mod_ref, g_ref, w_ref, hn_ref, proj_ref):
        xhat, _ = _rms(x_ref[...])
        hn = ((xhat * g_ref[...]) * (1.0 + mod_ref[1:2, :]) + mod_ref[0:1, :]).astype(BF16)
        hn_ref[...] = hn
        proj_ref[...] = _dot(hn, w_ref[...], NT)

    return _call(
        body, "mix_in_fwd", (s // tm,),
        [pl.BlockSpec((tm, d), lambda i: (i, 0)), _full(mod6.shape), _full(g_mix.shape), _full(w_in_t.shape)],
        [pl.BlockSpec((tm, d), lambda i: (i, 0)), pl.BlockSpec((tm, din), lambda i: (i, 0))],
        [jax.ShapeDtypeStruct((s, d), BF16), jax.ShapeDtypeStruct((s, din), F32)],
        [x2d, mod6, g_mix, w_in_t], rider=rider)


def _mixer_fwd(proj, conv_sc, conv_lru, conv_b, wa_bd, wx_bd, ba, bx, lam, width, rider=None):
    s, din = proj.shape
    t = min(MIX_ROWS, s)
    nblk = width // LANES
    hb = t // SUBLANES

    def body(proj_ref, projp_ref, wsc_ref, wlru_ref, blru_ref, wa_ref, wx_ref, ba_ref, bx_ref, lam_ref,
             ymix_ref, h_ref, hc_ref):
        i = pl.program_id(0)

        @pl.when(i == 0)
        def _():
            hc_ref[...] = jnp.zeros_like(hc_ref)

        has_prev = i > 0
        for j in range(nblk):
            lo = j * LANES

            def col(p, ref=proj_ref):
                return ref[:, p * width + lo:p * width + lo + LANES]

            def prev(p):
                return jnp.where(has_prev, col(p, projp_ref), 0.0)

            p = col(1) * col(2)
            q, _, _ = _conv3(p, prev(1) * prev(2), wsc_ref, lo)
            ymix_ref[:, lo:lo + LANES] = (col(0) * q).astype(BF16)

            u, _, _, _ = _conv4(col(4), prev(4), wlru_ref, blru_ref, lo)
            sp = _softplus(-lam_ref[:, lo:lo + LANES])
            _, r, ig, a, mult = _lru_gates(u, wa_ref[j], wx_ref[j], ba_ref[:, lo:lo + LANES], bx_ref[:, lo:lo + LANES], sp)
            h = _scan_fwd(a, mult * (ig * u), hc_ref[0:1, lo:lo + LANES])
            h_ref[:, lo:lo + LANES] = h
            hc_ref[0:1, lo:lo + LANES] = h[t - 1:t, :]
            gel, _ = _gelu(col(3))
            ymix_ref[:, width + lo:width + lo + LANES] = (gel * h).astype(BF16)

    small = [conv_sc, conv_lru, conv_b, wa_bd, wx_bd, ba, bx, lam]
    return _call(
        body, "mixer_fwd", (s // t,),
        [pl.BlockSpec((t, din), lambda i: (i, 0)),
         pl.BlockSpec((SUBLANES, din), lambda i: (jnp.maximum(i * hb - 1, 0), 0))]
        + [_full(a.shape) for a in small],
        [pl.BlockSpec((t, 2 * width), lambda i: (i, 0)), pl.BlockSpec((t, width), lambda i: (i, 0))],
        [jax.ShapeDtypeStruct((s, 2 * width), BF16), jax.ShapeDtypeStruct((s, width), F32)],
        [proj, proj, *small], scratch=[pltpu.VMEM((SUBLANES, width), F32)], rider=rider)


def _mix_out_fwd(ymix, x2d, w_out, mod6, g_mlp, tm, rider=None):
    s, d = x2d.shape

    def body(y_ref, x_ref, w_ref, mod_ref, g_ref, mix_ref, x2_ref, hn_ref):
        mix = _dot(y_ref[...], w_ref[...], NN)
        mix_ref[...] = mix
        x2 = x_ref[...] + mod_ref[2:3, :] * mix
        x2_ref[...] = x2
        xhat, _ = _rms(x2)
        hn_ref[...] = ((xhat * g_ref[...]) * (1.0 + mod_ref[4:5, :]) + mod_ref[3:4, :]).astype(BF16)

    tile = pl.BlockSpec((tm, d), lambda i: (i, 0))
    return _call(
        body, "mix_out_fwd", (s // tm,),
        [tile, tile, _full(w_out.shape), _full(mod6.shape), _full(g_mlp.shape)],
        [tile, tile, tile],
        [jax.ShapeDtypeStruct((s, d), F32), jax.ShapeDtypeStruct((s, d), F32), jax.ShapeDtypeStruct((s, d), BF16)],
        [ymix, x2d, w_out, mod6, g_mlp], rider=rider)


def _mlp_fwd_loss(hn2, w_up_t, w_down, x2, target, mod6, g_final, tm, tk):
    s, d = hn2.shape
    f = w_up_t.shape[0]
    nk = f // tk

    def body(hn_ref, wu_ref, wd_ref, x2_ref, t_ref, mod_ref, g_ref, z_ref, dx3_ref, dyb_ref, st_ref, y_ref):
        i, k = pl.program_id(0), pl.program_id(1)

        @pl.when(jnp.logical_and(i == 0, k == 0))
        def _():
            st_ref[...] = jnp.zeros_like(st_ref)

        z = jnp.maximum(_dot(hn_ref[...], wu_ref[...], NT), 0.0)
        z_ref[...] = z.astype(BF16)
        part = _dot((z * z).astype(BF16), wd_ref[...], NN)

        @pl.when(k == 0)
        def _():
            y_ref[...] = part

        @pl.when(k > 0)
        def _():
            y_ref[...] += part

        @pl.when(k == nk - 1)
        def _():
            gate = mod_ref[5:6, :]
            yv = y_ref[...]
            xhat, rstd = _rms(x2_ref[...] + gate * yv)
            diff = xhat * g_ref[...] - t_ref[...]
            dyo = diff * (1.0 / d)
            dx3 = _rms_bwd(dyo * g_ref[...], xhat, rstd)
            dx3_ref[...] = dx3
            dyb_ref[...] = (gate * dx3).astype(BF16)
            st_ref[0:1, :] += _colsum(dyo * xhat)
            st_ref[1:2, :] += _colsum(dx3 * yv)
            st_ref[2:3, :] += _colsum(diff * diff)

    tile = pl.BlockSpec((tm, d), lambda i, k: (i, 0))
    wblk = pl.BlockSpec((tk, d), lambda i, k: (k, 0))
    return pl.pallas_call(
        body, name="mlp_fwd_loss", grid=(s // tm, nk),
        in_specs=[tile, wblk, wblk, tile, tile, _full(mod6.shape), _full(g_final.shape)],
        out_specs=[pl.BlockSpec((tm, tk), lambda i, k: (i, k)), tile, tile, _full((SUBLANES, d))],
        out_shape=[jax.ShapeDtypeStruct((s, f), BF16), jax.ShapeDtypeStruct((s, d), F32),
                   jax.ShapeDtypeStruct((s, d), BF16), jax.ShapeDtypeStruct((SUBLANES, d), F32)],
        scratch_shapes=[pltpu.VMEM((tm, d), F32)],
        compiler_params=_params(("arbitrary", "arbitrary")),
    )(hn2, w_up_t, w_down, x2, target, mod6, g_final)


def _mlp_bwd_dx(dyb, z, w_down, w_up_t, tm, tk):
    s, d = dyb.shape
    f = z.shape[1]

    def body(dy_ref, z_ref, wd_ref, wu_ref, dz_ref, dh_ref):
        k = pl.program_id(1)
        dz = ((2.0 * z_ref[...].astype(F32)) * _dot(dy_ref[...], wd_ref[...], NT)).astype(BF16)
        dz_ref[...] = dz
        part = _dot(dz, wu_ref[...], NN)

        @pl.when(k == 0)
        def _():
            dh_ref[...] = part

        @pl.when(k > 0)
        def _():
            dh_ref[...] += part

    return pl.pallas_call(
        body, name="mlp_bwd_dx", grid=(s // tm, f // tk),
        in_specs=[pl.BlockSpec((tm, d), lambda i, k: (i, 0)), pl.BlockSpec((tm, tk), lambda i, k: (i, k)),
                  pl.BlockSpec((tk, d), lambda i, k: (k, 0)), pl.BlockSpec((tk, d), lambda i, k: (k, 0))],
        out_specs=[pl.BlockSpec((tm, tk), lambda i, k: (i, k)), pl.BlockSpec((tm, d), lambda i, k: (i, 0))],
        out_shape=[jax.ShapeDtypeStruct((s, f), BF16), jax.ShapeDtypeStruct((s, d), F32)],
        compiler_params=_params(("parallel", "arbitrary")),
    )(dyb, z, w_down, w_up_t)


def _mlp_bwd_dw(z, dz, dyb, hn2, tm, tk):
    s, d = dyb.shape
    f = z.shape[1]

    def body(z_ref, dz_ref, dy_ref, hn_ref, gd_ref, gu_ref):
        i = pl.program_id(1)

        @pl.when(i == 0)
        def _():
            gd_ref[...] = jnp.zeros_like(gd_ref)
            gu_ref[...] = jnp.zeros_like(gu_ref)

        zf = z_ref[...].astype(F32)
        gd_ref[...] += _dot((zf * zf).astype(BF16), dy_ref[...], TN)
        gu_ref[...] += _dot(dz_ref[...], hn_ref[...], TN)

    return pl.pallas_call(
        body, name="mlp_bwd_dw", grid=(f // tk, s // tm),
        in_specs=[pl.BlockSpec((tm, tk), lambda k, i: (i, k)), pl.BlockSpec((tm, tk), lambda k, i: (i, k)),
                  pl.BlockSpec((tm, d), lambda k, i: (i, 0)), pl.BlockSpec((tm, d), lambda k, i: (i, 0))],
        out_specs=[pl.BlockSpec((tk, d), lambda k, i: (k, 0)), pl.BlockSpec((tk, d), lambda k, i: (k, 0))],
        out_shape=[jax.ShapeDtypeStruct((f, d), F32), jax.ShapeDtypeStruct((f, d), F32)],
        compiler_params=_params(("parallel", "arbitrary")),
    )(z, dz, dyb, hn2)


def _mix_out_bwd(dhn2, x2, dx3, mix, ymix, w_out, mod6, g_mlp, tm, rider=None):
    s, d = x2.shape

    def body(dh_ref, x2_ref, dx3_ref, mix_ref, y_ref, w_ref, mod_ref, g_ref, dx2_ref, dym_ref, gw_ref, st_ref):
        i = pl.program_id(0)

        @pl.when(i == 0)
        def _():
            st_ref[...] = jnp.zeros_like(st_ref)
            gw_ref[...] = jnp.zeros_like(gw_ref)

        dh = dh_ref[...]
        xhat, rstd = _rms(x2_ref[...])
        dn = dh * (1.0 + mod_ref[4:5, :])
        dx2 = dx3_ref[...] + _rms_bwd(dn * g_ref[...], xhat, rstd)
        dx2_ref[...] = dx2
        st_ref[0:1, :] += _colsum(dh)
        st_ref[1:2, :] += _colsum(dh * (xhat * g_ref[...]))
        st_ref[2:3, :] += _colsum(dn * xhat)
        st_ref[3:4, :] += _colsum(dx2 * mix_ref[...])
        dmix = (mod_ref[2:3, :] * dx2).astype(BF16)
        dym_ref[...] = _dot(dmix, w_ref[...], NT)
        gw_ref[...] += _dot(y_ref[...], dmix, TN)

    tile = pl.BlockSpec((tm, d), lambda i: (i, 0))
    return _call(
        body, "mix_out_bwd", (s // tm,),
        [tile, tile, tile, tile, tile, _full(w_out.shape), _full(mod6.shape), _full(g_mlp.shape)],
        [tile, tile, _full((d, d)), _full((SUBLANES, d))],
        [jax.ShapeDtypeStruct((s, d), F32), jax.ShapeDtypeStruct((s, d), F32),
         jax.ShapeDtypeStruct((d, d), F32), jax.ShapeDtypeStruct((SUBLANES, d), F32)],
        [dhn2, x2, dx3, mix, ymix, w_out, mod6, g_mlp], rider=rider)


def _mixer_bwd(proj, dymix, h_all, conv_sc, conv_lru, conv_b, wa_bd, wx_bd, ba, bx, lam, width, rider=None):
    s, din = proj.shape
    t = min(MIX_ROWS, s)
    nt = s // t
    nblk = width // LANES
    hb = t // SUBLANES
    last8 = s // SUBLANES - 1

    def body(proj_ref, projp_ref, projn_ref, dy_ref, dyn_ref, h_ref, hp_ref,
             wsc_ref, wlru_ref, blru_ref, wa_ref, wx_ref, ba_ref, bx_ref, lam_ref,
             dproj_ref, small_ref, gwa_ref, gwx_ref, an_ref, gn_ref, dun_ref):
        i = pl.program_id(0)

        @pl.when(i == 0)
        def _():
            small_ref[...] = jnp.zeros_like(small_ref)
            gwa_ref[...] = jnp.zeros_like(gwa_ref)
            gwx_ref[...] = jnp.zeros_like(gwx_ref)
            an_ref[...] = jnp.zeros_like(an_ref)
            gn_ref[...] = jnp.zeros_like(gn_ref)
            dun_ref[...] = jnp.zeros_like(dun_ref)

        has_prev = i < nt - 1
        has_next = i > 0
        for j in range(nblk):
            lo = j * LANES
            ls = slice(lo, lo + LANES)

            def col(p, ref=proj_ref):
                return ref[:, p * width + lo:p * width + lo + LANES]

            def prev(p):
                return jnp.where(has_prev, col(p, projp_ref), 0.0)

            def nxt(p):
                return jnp.where(has_next, col(p, projn_ref), 0.0)

            def add_row(r, v):
                small_ref[r:r + 1, ls] += _colsum(v)

            sc_b, sc_c, sc_x = col(0), col(1), col(2)
            p = sc_c * sc_x
            q, p1, p2 = _conv3(p, prev(1) * prev(2), wsc_ref, lo)
            dys = dy_ref[:, ls]
            dproj_ref[:, ls] = (dys * q).astype(BF16)
            dq = dys * sc_b
            dqn = jnp.where(has_next, dyn_ref[:, ls], 0.0) * nxt(0)
            dp = (wsc_ref[2:3, ls] * dq + wsc_ref[1:2, ls] * _shift_up(dq, 1, dqn)) + wsc_ref[0:1, ls] * _shift_up(dq, 2, dqn)
            dproj_ref[:, width + lo:width + lo + LANES] = (dp * sc_x).astype(BF16)
            dproj_ref[:, 2 * width + lo:2 * width + lo + LANES] = (dp * sc_c).astype(BF16)
            add_row(0, dq * p2)
            add_row(1, dq * p1)
            add_row(2, dq * p)

            xv = col(4)
            u, x1, x2, x3 = _conv4(xv, prev(4), wlru_ref, blru_ref, lo)
            lam_v = lam_ref[:, ls]
            sp = _softplus(-lam_v)
            wa, wx = wa_ref[j], wx_ref[j]
            ub, r, ig, a, mult = _lru_gates(u, wa, wx, ba_ref[:, ls], bx_ref[:, ls], sp)
            iu = ig * u
            h = h_ref[:, ls]
            hm1 = _shift_down(h, 1, jnp.where(has_prev, hp_ref[:, ls], 0.0))
            lyv = col(3)
            gel, th = _gelu(lyv)
            dyl = dy_ref[:, width + lo:width + lo + LANES]
            dproj_ref[:, 3 * width + lo:3 * width + lo + LANES] = (dyl * h * _dgelu(lyv, th)).astype(BF16)
            a_next = jnp.broadcast_to(an_ref[0:1, ls], (SUBLANES, LANES))
            g = _scan_rev(_shift_up(a, 1, a_next), dyl * gel, gn_ref[0:1, ls])
            an_ref[0:1, ls] = a[0:1, :]
            gn_ref[0:1, ls] = g[0:1, :]
            da = g * hm1
            dmult = g * iu
            diu = g * mult
            dlog_a = da * a - dmult * ((a * a) / mult)
            dpre_a = (dlog_a * (-RG_C * sp)) * (r * (1.0 - r))
            dpre_x = (diu * u) * (ig * (1.0 - ig))
            dab, dxb = dpre_a.astype(BF16), dpre_x.astype(BF16)
            du = diu * ig + _dot(dab, wa, NT) + _dot(dxb, wx, NT)
            gwa_ref[j] += _dot(ub, dab, TN)
            gwx_ref[j] += _dot(ub, dxb, TN)
            dun = dun_ref[:, ls]
            dun_ref[:, ls] = du[0:SUBLANES, :]
            dlx = (((wlru_ref[3:4, ls] * du + wlru_ref[2:3, ls] * _shift_up(du, 1, dun))
                    + wlru_ref[1:2, ls] * _shift_up(du, 2, dun)) + wlru_ref[0:1, ls] * _shift_up(du, 3, dun))
            dproj_ref[:, 4 * width + lo:4 * width + lo + LANES] = dlx.astype(BF16)
            add_row(3, du * x3)
            add_row(4, du * x2)
            add_row(5, du * x1)
            add_row(6, du * xv)
            add_row(7, du)
            add_row(8, dpre_a)
            add_row(9, dpre_x)
            add_row(10, (dlog_a * (RG_C * r)) * jax.nn.sigmoid(-lam_v))

    small = [conv_sc, conv_lru, conv_b, wa_bd, wx_bd, ba, bx, lam]
    rev = lambda i: nt - 1 - i
    return _call(
        body, "mixer_bwd", (nt,),
        [pl.BlockSpec((t, din), lambda i: (rev(i), 0)),
         pl.BlockSpec((SUBLANES, din), lambda i: (jnp.maximum(rev(i) * hb - 1, 0), 0)),
         pl.BlockSpec((SUBLANES, din), lambda i: (jnp.minimum((rev(i) + 1) * hb, last8), 0)),
         pl.BlockSpec((t, 2 * width), lambda i: (rev(i), 0)),
         pl.BlockSpec((SUBLANES, 2 * width), lambda i: (jnp.minimum((rev(i) + 1) * hb, last8), 0)),
         pl.BlockSpec((t, width), lambda i: (rev(i), 0)),
         pl.BlockSpec((SUBLANES, width), lambda i: (jnp.maximum(rev(i) * hb - 1, 0), 0))]
        + [_full(a.shape) for a in small],
        [pl.BlockSpec((t, din), lambda i: (rev(i), 0)), _full((2 * SUBLANES, width)),
         _full(wa_bd.shape), _full(wx_bd.shape)],
        [jax.ShapeDtypeStruct((s, din), BF16), jax.ShapeDtypeStruct((2 * SUBLANES, width), F32),
         jax.ShapeDtypeStruct(wa_bd.shape, F32), jax.ShapeDtypeStruct(wx_bd.shape, F32)],
        [proj, proj, proj, dymix, dymix, h_all, h_all, *small],
        scratch=[pltpu.VMEM((SUBLANES, width), F32), pltpu.VMEM((SUBLANES, width), F32),
                 pltpu.VMEM((SUBLANES, width), F32)], rider=rider)


def _mix_in_bwd_dx(dproj, x2d, dx2, w_in_t, mod6, g_mix, tm, rider=None):
    s, d = x2d.shape
    din = dproj.shape[1]

    def body(dp_ref, x_ref, dx2_ref, w_ref, mod_ref, g_ref, gx_ref, st_ref):
        i = pl.program_id(0)

        @pl.when(i == 0)
        def _():
            st_ref[...] = jnp.zeros_like(st_ref)

        dh = _dot(dp_ref[...], w_ref[...], NN)
        xhat, rstd = _rms(x_ref[...])
        dn = dh * (1.0 + mod_ref[1:2, :])
        gx_ref[...] = dx2_ref[...] + _rms_bwd(dn * g_ref[...], xhat, rstd)
        st_ref[0:1, :] += _colsum(dh)
        st_ref[1:2, :] += _colsum(dh * (xhat * g_ref[...]))
        st_ref[2:3, :] += _colsum(dn * xhat)

    tile = pl.BlockSpec((tm, d), lambda i: (i, 0))
    return _call(
        body, "mix_in_bwd_dx", (s // tm,),
        [pl.BlockSpec((tm, din), lambda i: (i, 0)), tile, tile, _full(w_in_t.shape), _full(mod6.shape),
         _full(g_mix.shape)],
        [tile, _full((SUBLANES, d))],
        [jax.ShapeDtypeStruct((s, d), F32), jax.ShapeDtypeStruct((SUBLANES, d), F32)],
        [dproj, x2d, dx2, w_in_t, mod6, g_mix], rider=rider)


def _mix_in_bwd_dw(dproj, hn1, tm, tn, rider=None):
    s, d = hn1.shape
    din = dproj.shape[1]

    def body(dp_ref, hn_ref, gw_ref):
        i = pl.program_id(1)

        @pl.when(i == 0)
        def _():
            gw_ref[...] = jnp.zeros_like(gw_ref)

        gw_ref[...] += _dot(dp_ref[...], hn_ref[...], TN)

    return _call(
        body, "mix_in_bwd_dw", (din // tn, s // tm),
        [pl.BlockSpec((tm, tn), lambda p, i: (i, p)), pl.BlockSpec((tm, d), lambda p, i: (i, 0))],
        [pl.BlockSpec((tn, d), lambda p, i: (p, 0))],
        [jax.ShapeDtypeStruct((din, d), F32)],
        [dproj, hn1], rider=rider)


def _adamw(w, g, m, v):
    m = ADAM_B1 * m + (1.0 - ADAM_B1) * g
    v = ADAM_B2 * v + (1.0 - ADAM_B2) * (g * g)
    m_hat = m / (1.0 - ADAM_B1 ** ADAM_STEP)
    v_hat = v / (1.0 - ADAM_B2 ** ADAM_STEP)
    delta = -ADAM_LR * (m_hat / (jnp.sqrt(v_hat) + ADAM_EPS) + ADAM_WD * w)
    return delta, m, v


def _pair_sum(g4, h4, core_chip, tr, name):
    _, _, r, n = g4.shape

    def body(sc_ref, g_ref, h_ref, sb_ref, own_ref):
        q = pl.program_id(1)
        ssum = g_ref[...] + h_ref[...]
        sb_ref[...] = ssum.astype(BF16)

        @pl.when(q == sc_ref[1])
        def _():
            own_ref[...] = ssum

    grid_spec = pltpu.PrefetchScalarGridSpec(
        num_scalar_prefetch=1, grid=(r // tr, 4),
        in_specs=[pl.BlockSpec((None, None, tr, n), lambda i, q, sc: (q, sc[0], i, 0)),
                  pl.BlockSpec((None, tr, n), lambda i, q, sc: (q, i, 0))],
        out_specs=[pl.BlockSpec((None, tr, n), lambda i, q, sc: (q, i, 0)),
                   pl.BlockSpec((tr, n), lambda i, q, sc: (i, 0))])
    return pl.pallas_call(
        body, name=name, grid_spec=grid_spec,
        out_shape=[jax.ShapeDtypeStruct((4, r, n), BF16), jax.ShapeDtypeStruct((r, n), F32)],
        compiler_params=_params(("parallel", "arbitrary")),
    )(core_chip, g4, h4)


def _sum4(own, parts, tr, name):
    r, n = own.shape

    def body(o_ref, p_ref, out_ref):
        acc = o_ref[...]
        for k in range(3):
            acc = acc + p_ref[k].astype(F32)
        out_ref[...] = acc

    return pl.pallas_call(
        body, name=name, grid=(r // tr,),
        in_specs=[pl.BlockSpec((tr, n), lambda i: (i, 0)), pl.BlockSpec((3, tr, n), lambda i: (0, i, 0))],
        out_specs=pl.BlockSpec((tr, n), lambda i: (i, 0)),
        out_shape=jax.ShapeDtypeStruct((r, n), F32),
        compiler_params=_params(("parallel",)),
    )(own, parts)


def _sum4_adam(own, parts, w, m, v, tr, name, transposed):
    r, n = own.shape
    rows, cols = w.shape

    def body(o_ref, p_ref, w_ref, m_ref, v_ref, g_ref, d_ref, nm_ref, nv_ref):
        g = o_ref[...]
        for k in range(3):
            g = g + p_ref[k].astype(F32)
        if transposed:
            g = g.T
        g_ref[...] = g
        d_ref[...], nm_ref[...], nv_ref[...] = _adamw(w_ref[...], g, m_ref[...], v_ref[...])

    if transposed:
        g_specs = [pl.BlockSpec((r, tr), lambda i: (0, i)), pl.BlockSpec((3, r, tr), lambda i: (0, 0, i))]
    else:
        g_specs = [pl.BlockSpec((tr, n), lambda i: (i, 0)), pl.BlockSpec((3, tr, n), lambda i: (0, i, 0))]
    tile = pl.BlockSpec((tr, cols), lambda i: (i, 0))
    return pl.pallas_call(
        body, name=name, grid=(rows // tr,),
        in_specs=g_specs + [tile] * 3, out_specs=[tile] * 4,
        out_shape=[jax.ShapeDtypeStruct((rows, cols), F32)] * 4,
        compiler_params=_params(("parallel",)),
    )(own, parts, w, m, v)


def _sum8(parts, tr, name):
    _, rows, n = parts.shape

    def body(p_ref, o_ref):
        acc = p_ref[0]
        for k in range(1, N_DEV):
            acc = acc + p_ref[k]
        o_ref[...] = acc

    return pl.pallas_call(
        body, name=name, grid=(rows // tr,),
        in_specs=[pl.BlockSpec((N_DEV, tr, n), lambda i: (0, i, 0))],
        out_specs=pl.BlockSpec((tr, n), lambda i: (i, 0)),
        out_shape=jax.ShapeDtypeStruct((rows, n), F32),
        compiler_params=_params(("parallel",)),
    )(parts)


def _adam_rows(w, g, m, v, tr, name):
    rows, n = w.shape

    def body(w_ref, g_ref, m_ref, v_ref, d_ref, nm_ref, nv_ref):
        d_ref[...], nm_ref[...], nv_ref[...] = _adamw(w_ref[...], g_ref[...], m_ref[...], v_ref[...])

    tile = pl.BlockSpec((tr, n), lambda i: (i, 0))
    return pl.pallas_call(
        body, name=name, grid=(rows // tr,),
        in_specs=[tile] * 4, out_specs=[tile] * 3,
        out_shape=[jax.ShapeDtypeStruct((rows, n), F32)] * 3,
        compiler_params=_params(("parallel",)),
    )(w, g, m, v)


def _ada_bwd_adam(cact_t, dmod_cols, w, m, v, tr):
    rows, n = w.shape

    def body(c_ref, d_ref, w_ref, m_ref, v_ref, g_ref, dl_ref, nm_ref, nv_ref):
        def term(b):
            return c_ref[b].astype(BF16).astype(F32) * d_ref[b:b + 1, :].astype(BF16).astype(F32)

        g = term(0)
        for b in range(1, N_DEV):
            g = g + term(b)
        g_ref[...] = g
        dl_ref[...], nm_ref[...], nv_ref[...] = _adamw(w_ref[...], g, m_ref[...], v_ref[...])

    tile = pl.BlockSpec((tr, n), lambda i: (i, 0))
    return pl.pallas_call(
        body, name="ada_bwd_adam", grid=(rows // tr,),
        in_specs=[pl.BlockSpec((N_DEV, tr, 1), lambda i: (0, i, 0)), _full(dmod_cols.shape), tile, tile, tile],
        out_specs=[tile] * 4,
        out_shape=[jax.ShapeDtypeStruct((rows, n), F32)] * 4,
        compiler_params=_params(("parallel",)),
    )(cact_t, dmod_cols, w, m, v)


def _adam_small(ws, gs, ms, vs):
    n = len(ws)

    def body(*refs):
        w_r, g_r, m_r, v_r = refs[:n], refs[n:2 * n], refs[2 * n:3 * n], refs[3 * n:4 * n]
        d_r, nm_r, nv_r = refs[4 * n:5 * n], refs[5 * n:6 * n], refs[6 * n:7 * n]
        for k in range(n):
            d_r[k][...], nm_r[k][...], nv_r[k][...] = _adamw(w_r[k][...], g_r[k][...], m_r[k][...], v_r[k][...])

    shapes = [jax.ShapeDtypeStruct(w.shape, F32) for w in ws]
    outs = pl.pallas_call(
        body, name="adam_small", out_shape=shapes * 3, compiler_params=_params(),
    )(*ws, *gs, *ms, *vs)
    return outs[:n], outs[n:2 * n], outs[2 * n:]


def _block_diag(w):
    h, hd, _ = w.shape
    per = LANES // hd
    eye = jnp.eye(per, dtype=w.dtype)
    w5 = w.reshape(h // per, per, hd, 1, hd) * eye[None, :, None, :, None]
    return w5.reshape(h // per, LANES, LANES)


def _block_diag_grad(g, h, hd):
    per = LANES // hd
    g5 = g.reshape(h // per, per, hd, per, hd)
    return jnp.stack([g5[:, a, :, a, :] for a in range(per)], axis=1).reshape(h, hd, hd)


def kernel(x, c, w_ada, b_ada, g_mix, w_in, conv_w_sc, conv_w_lru, conv_b_lru, w_rg_a, b_rg_a, w_rg_x, b_rg_x, lru_lambda, w_out, g_mlp, w_up, w_down, g_final, loss_target, m_w_ada, m_b_ada, m_g_mix, m_w_in, m_conv_w_sc, m_conv_w_lru, m_conv_b_lru, m_w_rg_a, m_b_rg_a, m_w_rg_x, m_b_rg_x, m_lru_lambda, m_w_out, m_g_mlp, m_w_up, m_w_down, m_g_final, v_w_ada, v_b_ada, v_g_mix, v_w_in, v_conv_w_sc, v_conv_w_lru, v_conv_b_lru, v_w_rg_a, v_b_rg_a, v_w_rg_x, v_b_rg_x, v_lru_lambda, v_w_out, v_g_mlp, v_w_up, v_w_down, v_g_final):
    s, d = x.shape[1], x.shape[2]
    width = conv_b_lru.shape[1]
    heads, hd = w_rg_a.shape[1], w_rg_a.shape[2]
    f = w_down.shape[1] * N_DEV
    n_ada = w_ada.shape[2]
    csh = conv_w_sc.shape[2]
    me = 4 * lax.axis_index("x") + 2 * lax.axis_index("y") + lax.axis_index("c")
    tm = min(512, s)
    tm_mlp = min(1024, s)
    tk = 512

    x2d = x[0]
    tgt = loss_target[0]

    pay = jnp.zeros((SUBLANES, d), F32)
    pay = pay.at[0:1, :].set(c)
    pay = pay.at[1:4, 0:csh].set(conv_w_sc[0])
    pay = pay.at[4:8, 0:csh].set(conv_w_lru[0])
    w_in_t_sh = w_in[0].T.astype(BF16)
    w_up_t_sh = w_up[0].T.astype(BF16)
    w_out_sh = w_out[0].astype(BF16)
    w_down_sh = w_down[0].astype(BF16)
    pay_all, w_in_t = _gather2("gather_in", [pay, w_in_t_sh])
    w_in_t = w_in_t.reshape(-1, d)
    c_all = pay_all[:, 0, :]
    conv_sc = pay_all[:, 1:4, 0:csh].transpose(1, 0, 2).reshape(3, width)
    conv_lru = pay_all[:, 4:8, 0:csh].transpose(1, 0, 2).reshape(4, width)

    b_ada_sh = lax.dynamic_slice(b_ada, (0, me * n_ada), (1, n_ada))
    mod_cols, c_act = _ada_fwd(c_all, w_ada[0], b_ada_sh)
    (mod_rows,) = _exchange("scatter_mod", [], [mod_cols.reshape(N_DEV, 1, n_ada)])
    mod_rows, w_out_sh, w_up_t_sh, w_down_sh = lax.optimization_barrier((mod_rows, w_out_sh, w_up_t_sh, w_down_sh))
    (w_out_g,) = _seq_gather2("gather_w_out", 1, [w_out_sh])
    w_up_g, w_down_g = _seq_gather2("gather_mlp_weights", 2, [w_up_t_sh, w_down_sh])
    mod6 = jnp.zeros((SUBLANES, d), F32).at[0:6, :].set(mod_rows.reshape(6, d))

    wa_bd = _block_diag(w_rg_a[0]).astype(BF16)
    wx_bd = _block_diag(w_rg_x[0]).astype(BF16)
    ba = b_rg_a.reshape(1, width)
    bx = b_rg_x.reshape(1, width)
    g_fin = g_final.reshape(1, d)

    (hn1, proj), _ = _mix_in_fwd(x2d, mod6, g_mix, w_in_t, tm)
    (ymix, h_all), _ = _mixer_fwd(proj, conv_sc, conv_lru, conv_b_lru, wa_bd, wx_bd, ba, bx, lru_lambda, width)
    w_out_b = w_out_g.reshape(-1, d)
    (mix, x2, hn2), _ = _mix_out_fwd(ymix, x2d, w_out_b, mod6, g_mlp, tm)
    w_up_t = w_up_g.reshape(-1, d)
    w_down_b = w_down_g.reshape(-1, d)
    z, dx3, dyb, st_fin = _mlp_fwd_loss(hn2, w_up_t, w_down_b, x2, tgt, mod6, g_fin, tm, 4 * tk)

    core_chip = jnp.stack([lax.axis_index("c"), 2 * lax.axis_index("x") + lax.axis_index("y")]).astype(jnp.int32)
    dz, dhn2 = _mlp_bwd_dx(dyb, z, w_down_b, w_up_t, tm_mlp, 2 * tk)
    g_down, g_up_t = _mlp_bwd_dw(z, dz, dyb, hn2, tm_mlp, 2 * tk)
    g_up4, g_down4 = g_up_t.reshape(4, 2, -1, d), g_down.reshape(4, 2, -1, d)
    h_up, h_down = _seq_pair_swap("swap_mlp_grads", 7, [g_up4, g_down4])
    (dx2, dymix, g_out, st_out), _ = _mix_out_bwd(dhn2, x2, dx3, mix, ymix, w_out_b, mod6, g_mlp, tm)
    h_up, h_down, g_out = lax.optimization_barrier((h_up, h_down, g_out))
    sb_up, own_up = _pair_sum(g_up4, h_up, core_chip, 256, "pair_sum_w_up")
    sb_down, own_down = _pair_sum(g_down4, h_down, core_chip, 256, "pair_sum_w_down")
    g_out4 = g_out.reshape(4, 2, -1, d)
    (h_out,) = _seq_pair_swap("swap_w_out_grad", 8, [g_out4])
    p_up, p_down = _seq_chip_exchange("exchange_mlp_grads", 3, [sb_up, sb_down])
    (dproj, g_small, g_wa, g_wx), _ = _mixer_bwd(
        proj, dymix, h_all, conv_sc, conv_lru, conv_b_lru, wa_bd, wx_bd, ba, bx, lru_lambda, width)
    h_out, dproj = lax.optimization_barrier((h_out, dproj))
    sb_out, own_out = _pair_sum(g_out4, h_out, core_chip, g_out4.shape[2], "pair_sum_w_out")
    (p_out,) = _seq_chip_exchange("exchange_w_out_grad", 4, [sb_out])
    (grad_x, st_in), _ = _mix_in_bwd_dx(dproj, x2d, dx2, w_in_t, mod6, g_mix, tm)

    small = jnp.concatenate([
        st_in[0:2], st_out[3:4], st_out[0:2], st_fin[1:2],
        st_in[2:3], st_out[2:3], st_fin[0:1],
        jnp.concatenate([g_small[7:8], g_small[10:11]], axis=1),
        jnp.concatenate([g_small[8:9], g_small[9:10]], axis=1),
        jnp.concatenate([jnp.concatenate([g_small[0:3], jnp.zeros((1, width), F32)], axis=0), g_small[3:7]], axis=1),
        st_fin[2:3],
        _block_diag_grad(g_wa, heads, hd).reshape(-1, d),
        _block_diag_grad(g_wx, heads, hd).reshape(-1, d),
    ], axis=0)

    (small_all,) = _seq_gather2("gather_small_grads", 5, [small])
    (g_in_t,), _ = _mix_in_bwd_dw(dproj, hn1, min(2048, s), dproj.shape[1] // 2)
    g_in4 = g_in_t.reshape(4, 2, -1, d)
    (h_in,) = _seq_pair_swap("swap_w_in_grad", 9, [g_in4])
    p_up, p_down, p_out, small_all, g_in_t = lax.optimization_barrier((p_up, p_down, p_out, small_all, g_in_t))

    ad_up = _sum4_adam(own_up, p_up, w_up[0], m_w_up[0], v_w_up[0], 256, "adam_w_up", True)
    h_in, ad_up = lax.optimization_barrier((h_in, ad_up))
    sb_in, own_in = _pair_sum(g_in4, h_in, core_chip, g_in4.shape[2], "pair_sum_w_in")
    (p_in,) = _seq_chip_exchange("exchange_w_in_grad", 6, [sb_in])
    ad_out = _sum4_adam(own_out, p_out, w_out[0], m_w_out[0], v_w_out[0], w_out.shape[1], "adam_w_out", False)
    ad_down = _sum4_adam(own_down, p_down, w_down[0], m_w_down[0], v_w_down[0], 256, "adam_w_down", False)

    gsum = _sum8(small_all, SMALL_ROWS, "sum_small")
    loss = (0.5 / d) * jnp.sum(gsum[15])
    dmod_cols = lax.dynamic_slice(small_all[:, 0:6, :].reshape(N_DEV, 6 * d), (0, me * n_ada), (N_DEV, n_ada))
    g_ada, d_ada, nm_ada, nv_ada = _ada_bwd_adam(c_act[:, :, None], dmod_cols, w_ada[0], m_w_ada[0], v_w_ada[0], 256)

    g_conv = lax.dynamic_slice(gsum[11:15, 0:width], (0, me * csh), (4, csh))
    g_conv_l = lax.dynamic_slice(gsum[11:15, width:2 * width], (0, me * csh), (4, csh))
    small_g = [
        gsum[0:6].reshape(1, 6 * d),
        gsum[6:7],
        g_conv[0:3].reshape(1, 3, csh),
        g_conv_l.reshape(1, 4, csh),
        gsum[9:10, 0:width],
        gsum[16:48].reshape(1, heads, hd, hd),
        gsum[10:11, 0:width].reshape(1, heads, hd),
        gsum[48:80].reshape(1, heads, hd, hd),
        gsum[10:11, width:].reshape(1, heads, hd),
        gsum[9:10, width:],
        gsum[7:8],
        gsum[8],
    ]
    small_w = [b_ada, g_mix, conv_w_sc, conv_w_lru, conv_b_lru, w_rg_a, b_rg_a, w_rg_x, b_rg_x, lru_lambda, g_mlp, g_final]
    small_m = [m_b_ada, m_g_mix, m_conv_w_sc, m_conv_w_lru, m_conv_b_lru, m_w_rg_a, m_b_rg_a, m_w_rg_x, m_b_rg_x,
               m_lru_lambda, m_g_mlp, m_g_final]
    small_v = [v_b_ada, v_g_mix, v_conv_w_sc, v_conv_w_lru, v_conv_b_lru, v_w_rg_a, v_b_rg_a, v_w_rg_x, v_b_rg_x,
               v_lru_lambda, v_g_mlp, v_g_final]
    sd, snm, snv = _adam_small(small_w, small_g, small_m, small_v)
    p_in, ad_out, ad_down, (g_ada, d_ada, nm_ada, nv_ada), sd = lax.optimization_barrier(
        (p_in, ad_out, ad_down, (g_ada, d_ada, nm_ada, nv_ada), sd))
    ad_in = _sum4_adam(own_in, p_in, w_in[0].T, m_w_in[0].T, v_w_in[0].T, own_in.shape[0], "adam_w_in", False)
    ad_in = [a.T for a in ad_in]

    def order(ada, w_in_, w_out_, w_up_, w_down_, sm):
        return [ada[None], sm[0], sm[1], w_in_[None], sm[2], sm[3], sm[4], sm[5], sm[6], sm[7], sm[8], sm[9],
                w_out_[None], sm[10], w_up_[None], w_down_[None], sm[11]]

    grads = order(g_ada, ad_in[0], ad_out[0], ad_up[0], ad_down[0], small_g)
    deltas = order(d_ada, ad_in[1], ad_out[1], ad_up[1], ad_down[1], sd)
    new_m = order(nm_ada, ad_in[2], ad_out[2], ad_up[2], ad_down[2], snm)
    new_v = order(nv_ada, ad_in[3], ad_out[3], ad_up[3], ad_down[3], snv)
    return (loss, grad_x[None], *grads, *deltas, *new_m, *new_v)
```

```python
import functools

import jax
import jax.numpy as jnp
from jax import lax
from jax.experimental import pallas as pl
from jax.experimental.pallas import tpu as pltpu
from jax.experimental.pallas import tpu_sc as plsc

F32 = jnp.float32
BF16 = jnp.bfloat16
N_DEV = 8
EPS = 1e-6
RG_C = 8.0
GELU_K0 = 0.7978845608028654
GELU_K1 = 0.044715
ADAM_LR = 0.001
ADAM_B1 = 0.9
ADAM_B2 = 0.999
ADAM_EPS = 1e-08
ADAM_WD = 0.01
ADAM_STEP = 10
LANES = 128
SUBLANES = 8
VMEM_LIMIT = 52 * 1024 * 1024
MIX_ROWS = 256
SMALL_ROWS = 80

MESH = pl.DeviceIdType.MESH
ANY = pl.BlockSpec(memory_space=pl.ANY)
NN = ((1,), (0,))
NT = ((1,), (1,))
TN = ((0,), (0,))


def _dot(a, b, dims):
    return lax.dot_general(a, b, (dims, ((), ())), preferred_element_type=F32)


def _params(sem=None):
    return pltpu.CompilerParams(dimension_semantics=sem, vmem_limit_bytes=VMEM_LIMIT)


def _full(shape):
    nd = len(shape)
    return pl.BlockSpec(shape, lambda *_: (0,) * nd)


def _exchange(name, gathers, scatters):
    n_g = len(gathers)
    arrs = list(gathers) + list(scatters)
    n = len(arrs)
    out_shape = [jax.ShapeDtypeStruct((N_DEV,) + a.shape, a.dtype) for a in gathers]
    out_shape += [jax.ShapeDtypeStruct(a.shape, a.dtype) for a in scatters]

    def body(*refs):
        ins, outs = refs[:n], refs[n:2 * n]
        send_sems, recv_sems, local_sems = refs[2 * n:]
        x, y, c = lax.axis_index("x"), lax.axis_index("y"), lax.axis_index("c")
        me = 4 * x + 2 * y + c

        def src(a, dev):
            return ins[a] if a < n_g else ins[a].at[dev]

        def peer_of(k):
            px = 1 - x if (k >> 2) & 1 else x
            py = 1 - y if (k >> 1) & 1 else y
            pc = 1 - c if k & 1 else c
            return (px, py, pc), 4 * px + 2 * py + pc

        local = [pltpu.make_async_copy(src(a, me), outs[a].at[me], local_sems.at[a]) for a in range(n)]
        for cp in local:
            cp.start()
        sends = []
        for k in range(1, N_DEV):
            peer, pidx = peer_of(k)
            for a in range(n):
                cp = pltpu.make_async_remote_copy(
                    src_ref=src(a, pidx), dst_ref=outs[a].at[me],
                    send_sem=send_sems.at[a * (N_DEV - 1) + k - 1], recv_sem=recv_sems.at[a * (N_DEV - 1) + k - 1],
                    device_id=peer, device_id_type=MESH)
                cp.start()
                sends.append(cp)
        for k in range(1, N_DEV):
            peer, pidx = peer_of(k)
            for a in range(n):
                pltpu.make_async_remote_copy(
                    src_ref=src(a, pidx), dst_ref=outs[a].at[pidx],
                    send_sem=send_sems.at[a * (N_DEV - 1) + k - 1], recv_sem=recv_sems.at[a * (N_DEV - 1) + k - 1],
                    device_id=peer, device_id_type=MESH).wait_recv()
        for cp in sends:
            cp.wait_send()
        for cp in local:
            cp.wait()

    return pl.pallas_call(
        body, name=name, out_shape=out_shape,
        in_specs=[ANY] * n, out_specs=[ANY] * n,
        scratch_shapes=[pltpu.SemaphoreType.DMA((n * (N_DEV - 1),)),
                        pltpu.SemaphoreType.DMA((n * (N_DEV - 1),)),
                        pltpu.SemaphoreType.DMA((n,))],
    )(*arrs)


def _gather2(name, arrs):
    n = len(arrs)
    per = 7
    out_shape = [jax.ShapeDtypeStruct((N_DEV,) + a.shape, a.dtype) for a in arrs]

    def body(*refs):
        ins, outs = refs[:n], refs[n:2 * n]
        send_sems, recv_sems, local_sems = refs[2 * n:]
        x, y, c = lax.axis_index("x"), lax.axis_index("y"), lax.axis_index("c")
        sib = (x, y, 1 - c)
        chips = [(1 - x, y), (x, 1 - y), (1 - x, 1 - y)]

        def slot(a, px, py, pc):
            return outs[a].at[4 * px + 2 * py + pc]

        def copy(a, k, block, to, src=None):
            return pltpu.make_async_remote_copy(
                src_ref=slot(a, *block) if src is None else src, dst_ref=slot(a, *block),
                send_sem=send_sems.at[a * per + k], recv_sem=recv_sems.at[a * per + k],
                device_id=to, device_id_type=MESH)

        local = [pltpu.make_async_copy(ins[a], slot(a, x, y, c), local_sems.at[a]) for a in range(n)]
        for cp in local:
            cp.start()
        first = []
        for a in range(n):
            first += [copy(a, 1 + j, (x, y, c), (*chip, c), src=ins[a]) for j, chip in enumerate(chips)]
        for a in range(n):
            first.append(copy(a, 0, (x, y, c), sib, src=ins[a]))
        for cp in first:
            cp.start()
        passed = []
        for a in range(n):
            for j, chip in enumerate(chips):
                copy(a, 1 + j, (*chip, c), (x, y, c)).wait_recv()
                cp = copy(a, 4 + j, (*chip, c), sib)
                cp.start()
                passed.append(cp)
        for a in range(n):
            copy(a, 0, sib, (x, y, c)).wait_recv()
            for j, chip in enumerate(chips):
                copy(a, 4 + j, (*chip, 1 - c), (x, y, c)).wait_recv()
        for cp in first + passed:
            cp.wait_send()
        for cp in local:
            cp.wait()

    return pl.pallas_call(
        body, name=name, out_shape=out_shape,
        in_specs=[ANY] * n, out_specs=[ANY] * n,
        scratch_shapes=[pltpu.SemaphoreType.DMA((n * per,)), pltpu.SemaphoreType.DMA((n * per,)),
                        pltpu.SemaphoreType.DMA((n,))],
    )(*arrs)


def _seq_gather2(name, collective_id, arrs):
    n = len(arrs)
    per = 7

    def body(*refs):
        ins, outs = refs[:n], refs[n:2 * n]
        send_sems, recv_sems, local_sems = refs[2 * n:]
        x, y, c = lax.axis_index("x"), lax.axis_index("y"), lax.axis_index("c")
        sib = (x, y, 1 - c)
        chips = [(1 - x, y), (x, 1 - y), (1 - x, 1 - y)]
        barrier = pltpu.get_barrier_semaphore()
        for peer in [sib] + [(*chip, c) for chip in chips]:
            pl.semaphore_signal(barrier, inc=1, device_id=peer, device_id_type=MESH)
        pl.semaphore_wait(barrier, 4)

        def slot(a, px, py, pc):
            return outs[a].at[4 * px + 2 * py + pc]

        def copy(a, k, block, to, src=None):
            return pltpu.make_async_remote_copy(
                src_ref=slot(a, *block) if src is None else src, dst_ref=slot(a, *block),
                send_sem=send_sems.at[a * per + k], recv_sem=recv_sems.at[a * per + k],
                device_id=to, device_id_type=MESH)

        local = [pltpu.make_async_copy(ins[a], slot(a, x, y, c), local_sems.at[a]) for a in range(n)]
        for cp in local:
            cp.start()
        first = []
        for a in range(n):
            first += [copy(a, 1 + j, (x, y, c), (*chip, c), src=ins[a]) for j, chip in enumerate(chips)]
        for a in range(n):
            first.append(copy(a, 0, (x, y, c), sib, src=ins[a]))
        for cp in first:
            cp.start()
        passed = []
        for a in range(n):
            for j, chip in enumerate(chips):
                copy(a, 1 + j, (*chip, c), (x, y, c)).wait_recv()
                cp = copy(a, 4 + j, (*chip, c), sib)
                cp.start()
                passed.append(cp)
        for a in range(n):
            copy(a, 0, sib, (x, y, c)).wait_recv()
            for j, chip in enumerate(chips):
                copy(a, 4 + j, (*chip, 1 - c), (x, y, c)).wait_recv()
        for cp in first + passed:
            cp.wait_send()
        for cp in local:
            cp.wait()

    return pl.kernel(
        body, out_type=[jax.ShapeDtypeStruct((N_DEV,) + a.shape, a.dtype) for a in arrs],
        mesh=plsc.ScalarSubcoreMesh(axis_name="seq", num_cores=1),
        scratch_types=[pltpu.SemaphoreType.DMA((n * per,)), pltpu.SemaphoreType.DMA((n * per,)),
                       pltpu.SemaphoreType.DMA((n,))],
        compiler_params=pltpu.CompilerParams(collective_id=collective_id), name=name,
    )(*arrs)


def _seq_chip_exchange(name, collective_id, arrs):
    n = len(arrs)

    def body(*refs):
        ins, outs = refs[:n], refs[n:2 * n]
        send_sems, recv_sems = refs[2 * n:]
        x, y, c = lax.axis_index("x"), lax.axis_index("y"), lax.axis_index("c")

        def peer(k):
            return (1 - x if (k >> 1) & 1 else x), (1 - y if k & 1 else y)

        barrier = pltpu.get_barrier_semaphore()
        for k in (1, 2, 3):
            pl.semaphore_signal(barrier, inc=1, device_id=(*peer(k), c), device_id_type=MESH)
        pl.semaphore_wait(barrier, 3)

        def copy(a, k):
            px, py = peer(k)
            return pltpu.make_async_remote_copy(
                src_ref=ins[a].at[2 * px + py], dst_ref=outs[a].at[k - 1],
                send_sem=send_sems.at[a * 3 + k - 1], recv_sem=recv_sems.at[a * 3 + k - 1],
                device_id=(px, py, c), device_id_type=MESH)

        cps = [copy(a, k) for a in range(n) for k in (1, 2, 3)]
        for cp in cps:
            cp.start()
        for cp in cps:
            cp.wait_recv()
        for cp in cps:
            cp.wait_send()

    return pl.kernel(
        body, out_type=[jax.ShapeDtypeStruct((3,) + a.shape[1:], a.dtype) for a in arrs],
        mesh=plsc.ScalarSubcoreMesh(axis_name="seq", num_cores=1),
        scratch_types=[pltpu.SemaphoreType.DMA((n * 3,)), pltpu.SemaphoreType.DMA((n * 3,))],
        compiler_params=pltpu.CompilerParams(collective_id=collective_id), name=name,
    )(*arrs)


def _seq_pair_swap(name, collective_id, arrs):
    n = len(arrs)

    def body(*refs):
        ins, outs = refs[:n], refs[n:2 * n]
        send_sems, recv_sems = refs[2 * n:]
        x, y, c = lax.axis_index("x"), lax.axis_index("y"), lax.axis_index("c")
        barrier = pltpu.get_barrier_semaphore()
        pl.semaphore_signal(barrier, inc=1, device_id=(x, y, 1 - c), device_id_type=MESH)
        pl.semaphore_wait(barrier, 1)

        def copy(a, q):
            return pltpu.make_async_remote_copy(
                src_ref=ins[a].at[q, 1 - c], dst_ref=outs[a].at[q],
                send_sem=send_sems.at[a * 4 + q], recv_sem=recv_sems.at[a * 4 + q],
                device_id=(x, y, 1 - c), device_id_type=MESH)

        cps = [copy(a, q) for a in range(n) for q in range(4)]
        for cp in cps:
            cp.start()
        for cp in cps:
            cp.wait_recv()
        for cp in cps:
            cp.wait_send()

    return pl.kernel(
        body, out_type=[jax.ShapeDtypeStruct((4,) + a.shape[2:], a.dtype) for a in arrs],
        mesh=plsc.ScalarSubcoreMesh(axis_name="seq", num_cores=1),
        scratch_types=[pltpu.SemaphoreType.DMA((n * 4,)), pltpu.SemaphoreType.DMA((n * 4,))],
        compiler_params=pltpu.CompilerParams(collective_id=collective_id), name=name,
    )(*arrs)


def _pair_swap(name, arrs):
    n = len(arrs)
    out_shape = [jax.ShapeDtypeStruct((4,) + a.shape[2:], a.dtype) for a in arrs]

    def body(*refs):
        ins, outs = refs[:n], refs[n:2 * n]
        send_sems, recv_sems = refs[2 * n:]
        x, y, c = lax.axis_index("x"), lax.axis_index("y"), lax.axis_index("c")

        def copy(a, q):
            return pltpu.make_async_remote_copy(
                src_ref=ins[a].at[q, 1 - c], dst_ref=outs[a].at[q],
                send_sem=send_sems.at[a * 4 + q], recv_sem=recv_sems.at[a * 4 + q],
                device_id=(x, y, 1 - c), device_id_type=MESH)

        cps = [copy(a, q) for a in range(n) for q in range(4)]
        for cp in cps:
            cp.start()
        for cp in cps:
            cp.wait_recv()
        for cp in cps:
            cp.wait_send()

    return pl.pallas_call(
        body, name=name, out_shape=out_shape,
        in_specs=[ANY] * n, out_specs=[ANY] * n,
        scratch_shapes=[pltpu.SemaphoreType.DMA((n * 4,)), pltpu.SemaphoreType.DMA((n * 4,))],
    )(*arrs)


def _chip_exchange(name, arrs):
    n = len(arrs)
    out_shape = [jax.ShapeDtypeStruct((3,) + a.shape[1:], a.dtype) for a in arrs]

    def body(*refs):
        ins, outs = refs[:n], refs[n:2 * n]
        send_sems, recv_sems = refs[2 * n:]
        x, y, c = lax.axis_index("x"), lax.axis_index("y"), lax.axis_index("c")

        def copy(a, k):
            px = 1 - x if (k >> 1) & 1 else x
            py = 1 - y if k & 1 else y
            return pltpu.make_async_remote_copy(
                src_ref=ins[a].at[2 * px + py], dst_ref=outs[a].at[k - 1],
                send_sem=send_sems.at[a * 3 + k - 1], recv_sem=recv_sems.at[a * 3 + k - 1],
                device_id=(px, py, c), device_id_type=MESH)

        cps = [copy(a, k) for a in range(n) for k in (1, 2, 3)]
        for cp in cps:
            cp.start()
        for cp in cps:
            cp.wait_recv()
        for cp in cps:
            cp.wait_send()

    return pl.pallas_call(
        body, name=name, out_shape=out_shape,
        in_specs=[ANY] * n, out_specs=[ANY] * n,
        scratch_shapes=[pltpu.SemaphoreType.DMA((n * 3,)), pltpu.SemaphoreType.DMA((n * 3,))],
    )(*arrs)


class _Rider:
    def __init__(self, arrays, out_shapes, n_sems, build, aliases=None):
        self.arrays, self.out_shapes, self.n_sems, self.build = list(arrays), list(out_shapes), n_sems, build
        self.aliases = dict(aliases or {})


def _merge_riders(r1, r2):
    n1i, n1o, n1s = len(r1.arrays), len(r1.out_shapes), r1.n_sems

    def build(ins, outs, send_sems, recv_sems):
        a = r1.build(ins[:n1i], outs[:n1o], send_sems.at[pl.ds(0, n1s)], recv_sems.at[pl.ds(0, n1s)])
        b = r2.build(ins[n1i:], outs[n1o:], send_sems.at[pl.ds(n1s, r2.n_sems)], recv_sems.at[pl.ds(n1s, r2.n_sems)])
        return tuple(p + q for p, q in zip(a, b))

    aliases = dict(r1.aliases)
    aliases.update({k + n1i: v + n1o for k, v in r2.aliases.items()})
    return _Rider(r1.arrays + r2.arrays, r1.out_shapes + r2.out_shapes, n1s + r2.n_sems, build, aliases)


def _place():
    x, y, c = lax.axis_index("x"), lax.axis_index("y"), lax.axis_index("c")
    chips = [(1 - x, y), (x, 1 - y), (1 - x, 1 - y)]
    return x, y, c, chips


def _ride_gather_ici(arrs):
    n = len(arrs)

    def build(ins, outs, send_sems, recv_sems):
        x, y, c, chips = _place()
        peers = [(*chip, c) for chip in chips] + [(x, y, 1 - c)]
        me = 4 * x + 2 * y + c
        local = [pltpu.make_async_copy(ins[a], outs[a].at[me], send_sems.at[a * 5 + 4]) for a in range(n)]
        sends, recvs = [], []
        for a in range(n):
            for j, (px, py, pc) in enumerate(peers):
                sends.append(pltpu.make_async_remote_copy(
                    src_ref=ins[a], dst_ref=outs[a].at[me], send_sem=send_sems.at[a * 5 + j],
                    recv_sem=recv_sems.at[a * 5 + j], device_id=(px, py, pc), device_id_type=MESH))
                recvs.append(pltpu.make_async_remote_copy(
                    src_ref=ins[a], dst_ref=outs[a].at[4 * px + 2 * py + pc], send_sem=send_sems.at[a * 5 + j],
                    recv_sem=recv_sems.at[a * 5 + j], device_id=(px, py, pc), device_id_type=MESH))
        return local, sends, recvs

    shapes = [jax.ShapeDtypeStruct((N_DEV,) + a.shape, a.dtype) for a in arrs]
    return _Rider(arrs, shapes, n * 5, build)


def _ride_gather_direct(arrs):
    n = len(arrs)

    def build(ins, outs, send_sems, recv_sems):
        x, y, c, _ = _place()
        me = 4 * x + 2 * y + c
        local = [pltpu.make_async_copy(ins[a], outs[a].at[me], send_sems.at[a * N_DEV + 7]) for a in range(n)]
        sends, recvs = [], []
        for a in range(n):
            for k in range(1, N_DEV):
                px = 1 - x if (k >> 2) & 1 else x
                py = 1 - y if (k >> 1) & 1 else y
                pc = 1 - c if k & 1 else c
                sem = a * N_DEV + k - 1
                sends.append(pltpu.make_async_remote_copy(
                    src_ref=ins[a], dst_ref=outs[a].at[me], send_sem=send_sems.at[sem], recv_sem=recv_sems.at[sem],
                    device_id=(px, py, pc), device_id_type=MESH))
                recvs.append(pltpu.make_async_remote_copy(
                    src_ref=ins[a], dst_ref=outs[a].at[4 * px + 2 * py + pc], send_sem=send_sems.at[sem],
                    recv_sem=recv_sems.at[sem], device_id=(px, py, pc), device_id_type=MESH))
        return local, sends, recvs

    shapes = [jax.ShapeDtypeStruct((N_DEV,) + a.shape, a.dtype) for a in arrs]
    return _Rider(arrs, shapes, n * N_DEV, build)


def _ride_gather_d2d(gathered):
    n = len(gathered)

    def build(ins, outs, send_sems, recv_sems):
        x, y, c, chips = _place()
        sends, recvs = [], []
        for a in range(n):
            for j, (px, py) in enumerate(chips):
                mine = outs[a].at[4 * px + 2 * py + c]
                theirs = outs[a].at[4 * px + 2 * py + 1 - c]
                sends.append(pltpu.make_async_remote_copy(
                    src_ref=mine, dst_ref=mine, send_sem=send_sems.at[a * 3 + j], recv_sem=recv_sems.at[a * 3 + j],
                    device_id=(x, y, 1 - c), device_id_type=MESH))
                recvs.append(pltpu.make_async_remote_copy(
                    src_ref=mine, dst_ref=theirs, send_sem=send_sems.at[a * 3 + j], recv_sem=recv_sems.at[a * 3 + j],
                    device_id=(x, y, 1 - c), device_id_type=MESH))
        return [], sends, recvs

    shapes = [jax.ShapeDtypeStruct(a.shape, a.dtype) for a in gathered]
    return _Rider(gathered, shapes, n * 3, build, aliases={a: a for a in range(n)})


def _ride_pair_swap(arrs):
    n = len(arrs)

    def build(ins, outs, send_sems, recv_sems):
        x, y, c, _ = _place()
        cps = [pltpu.make_async_remote_copy(
            src_ref=ins[a].at[q, 1 - c], dst_ref=outs[a].at[q], send_sem=send_sems.at[a * 4 + q],
            recv_sem=recv_sems.at[a * 4 + q], device_id=(x, y, 1 - c), device_id_type=MESH)
            for a in range(n) for q in range(4)]
        return [], cps, cps

    shapes = [jax.ShapeDtypeStruct((4,) + a.shape[2:], a.dtype) for a in arrs]
    return _Rider(arrs, shapes, n * 4, build)


def _ride_chip_exchange(arrs):
    n = len(arrs)

    def build(ins, outs, send_sems, recv_sems):
        x, y, c, _ = _place()
        cps = []
        for a in range(n):
            for k in (1, 2, 3):
                px = 1 - x if (k >> 1) & 1 else x
                py = 1 - y if k & 1 else y
                cps.append(pltpu.make_async_remote_copy(
                    src_ref=ins[a].at[2 * px + py], dst_ref=outs[a].at[k - 1], send_sem=send_sems.at[a * 3 + k - 1],
                    recv_sem=recv_sems.at[a * 3 + k - 1], device_id=(px, py, c), device_id_type=MESH))
        return [], cps, cps

    shapes = [jax.ShapeDtypeStruct((3,) + a.shape[1:], a.dtype) for a in arrs]
    return _Rider(arrs, shapes, n * 3, build)


def _call(body, name, grid, in_specs, out_specs, out_shape, args, scratch=(), rider=None):
    n_in, n_out, n_scr = len(in_specs), len(out_specs), len(scratch)
    sem = ("arbitrary",) * len(grid)
    if rider is None:
        outs = pl.pallas_call(
            body, name=name, grid=grid, in_specs=in_specs, out_specs=out_specs, out_shape=out_shape,
            scratch_shapes=list(scratch), compiler_params=_params(sem))(*args)
        return outs, []
    ri, ro = len(rider.arrays), len(rider.out_shapes)

    def riding(*refs):
        ins, r_ins = refs[:n_in], refs[n_in:n_in + ri]
        outs = refs[n_in + ri:n_in + ri + n_out]
        r_outs = refs[n_in + ri + n_out:n_in + ri + n_out + ro]
        scr = refs[n_in + ri + n_out + ro:n_in + ri + n_out + ro + n_scr]
        send_sems, recv_sems = refs[-2:]
        first = functools.reduce(jnp.logical_and, [pl.program_id(k) == 0 for k in range(len(grid))])
        last = functools.reduce(jnp.logical_and, [pl.program_id(k) == grid[k] - 1 for k in range(len(grid))])

        @pl.when(first)
        def _():
            local, sends, _ = rider.build(r_ins, r_outs, send_sems, recv_sems)
            for cp in local + sends:
                cp.start()

        body(*ins, *outs, *scr)

        @pl.when(last)
        def _():
            local, sends, recvs = rider.build(r_ins, r_outs, send_sems, recv_sems)
            for cp in recvs:
                cp.wait_recv()
            for cp in sends:
                cp.wait_send()
            for cp in local:
                cp.wait()

    outs = pl.pallas_call(
        riding, name=name, grid=grid,
        in_specs=list(in_specs) + [ANY] * ri, out_specs=list(out_specs) + [ANY] * ro,
        out_shape=list(out_shape) + rider.out_shapes,
        scratch_shapes=list(scratch) + [pltpu.SemaphoreType.DMA((rider.n_sems,)), pltpu.SemaphoreType.DMA((rider.n_sems,))],
        input_output_aliases={n_in + k: n_out + v for k, v in rider.aliases.items()},
        compiler_params=_params(sem))(*args, *rider.arrays)
    return outs[:n_out], outs[n_out:]


def _comm(name, rider):
    def body(dummy_ref, out_ref):
        out_ref[...] = dummy_ref[...]

    dummy = jnp.zeros((SUBLANES, LANES), F32)
    spec = pl.BlockSpec((SUBLANES, LANES), lambda i: (0, 0))
    _, r_outs = _call(body, name, (1,), [spec], [spec], [jax.ShapeDtypeStruct(dummy.shape, F32)], [dummy], rider=rider)
    return r_outs


def _ada_fwd(c_all, w_ada_sh, b_ada_sh):
    nb, d = c_all.shape
    ncol = w_ada_sh.shape[1]

    def body(c_ref, w_ref, b_ref, mod_ref, cact_ref):
        cc = c_ref[...]
        ca = cc * jax.nn.sigmoid(cc)
        cact_ref[...] = ca
        mod_ref[...] = _dot(ca.astype(BF16), w_ref[...].astype(BF16), NN) + b_ref[...]

    return pl.pallas_call(
        body, name="ada_fwd",
        out_shape=[jax.ShapeDtypeStruct((nb, ncol), F32), jax.ShapeDtypeStruct((nb, d), F32)],
        compiler_params=_params(),
    )(c_all, w_ada_sh, b_ada_sh)


def _rms(xv):
    rstd = lax.rsqrt(jnp.mean(xv * xv, axis=-1, keepdims=True) + EPS)
    return xv * rstd, rstd


def _rms_bwd(dxhat, xhat, rstd):
    return rstd * (dxhat - xhat * jnp.mean(dxhat * xhat, axis=-1, keepdims=True))


def _colsum(v):
    return jnp.sum(v, axis=0, keepdims=True)


def _expm1(v, ev):
    series = v * (1.0 + v * (0.5 + v * (1.0 / 6.0 + v * (1.0 / 24.0 + v * (1.0 / 120.0)))))
    return jnp.where(jnp.abs(v) < 0.2, series, ev - 1.0)


def _softplus(v):
    return jnp.maximum(v, 0.0) + jnp.log1p(jnp.exp(-jnp.abs(v)))


def _gelu(v):
    t = jnp.tanh(v * (GELU_K0 + (GELU_K0 * GELU_K1) * (v * v)))
    return 0.5 * v * (1.0 + t), t


def _dgelu(v, t):
    return 0.5 * ((1.0 + t) + (v * (1.0 - t * t)) * (GELU_K0 + (3.0 * GELU_K0 * GELU_K1) * (v * v)))


def _shift_down(v, k, prev8):
    r = pltpu.roll(v, k, 0)
    pr = pltpu.roll(prev8, k, 0)
    row8 = lax.broadcasted_iota(jnp.int32, prev8.shape, 0)
    top = jnp.where(row8 < k, pr, r[0:SUBLANES])
    return jnp.concatenate([top, r[SUBLANES:]], axis=0)


def _shift_up(v, k, next8):
    t = v.shape[0]
    r = pltpu.roll(v, t - k, 0)
    nr = pltpu.roll(next8, SUBLANES - k, 0)
    row8 = lax.broadcasted_iota(jnp.int32, next8.shape, 0)
    bot = jnp.where(row8 >= SUBLANES - k, nr, r[t - SUBLANES:t])
    return jnp.concatenate([r[:t - SUBLANES], bot], axis=0)


def _scan_fwd(a, b, h0, stage_a, stage_b):
    t = a.shape[0]
    stage_a[0:SUBLANES, :] = jnp.ones((SUBLANES, a.shape[1]), F32)
    stage_b[0:SUBLANES, :] = jnp.zeros((SUBLANES, a.shape[1]), F32)
    s = 1
    while s < min(t, SUBLANES):
        stage_a[SUBLANES:SUBLANES + t, :] = a
        stage_b[SUBLANES:SUBLANES + t, :] = b
        b = a * stage_b[SUBLANES - s:SUBLANES - s + t, :] + b
        a = a * stage_a[SUBLANES - s:SUBLANES - s + t, :]
        s *= 2
    while s < t:
        b = jnp.concatenate([b[:s], a[s:] * b[:t - s] + b[s:]], axis=0)
        a = jnp.concatenate([a[:s], a[s:] * a[:t - s]], axis=0)
        s *= 2
    return b + a * h0


def _scan_rev(m, b, g_next, stage_m, stage_b):
    t = m.shape[0]
    stage_m[SUBLANES + t:2 * SUBLANES + t, :] = jnp.ones((SUBLANES, m.shape[1]), F32)
    stage_b[SUBLANES + t:2 * SUBLANES + t, :] = jnp.zeros((SUBLANES, m.shape[1]), F32)
    s = 1
    while s < min(t, SUBLANES):
        stage_m[SUBLANES:SUBLANES + t, :] = m
        stage_b[SUBLANES:SUBLANES + t, :] = b
        b = m * stage_b[SUBLANES + s:SUBLANES + s + t, :] + b
        m = m * stage_m[SUBLANES + s:SUBLANES + s + t, :]
        s *= 2
    while s < t:
        b = jnp.concatenate([m[:t - s] * b[s:] + b[:t - s], b[t - s:]], axis=0)
        m = jnp.concatenate([m[:t - s] * m[s:], m[t - s:]], axis=0)
        s *= 2
    return b + m * g_next


def _lru_gates(u, wa, wx, ba, bx, sp):
    ub = u.astype(BF16)
    r = jax.nn.sigmoid(_dot(ub, wa, NN) + ba)
    i = jax.nn.sigmoid(_dot(ub, wx, NN) + bx)
    log_a = (-RG_C * r) * sp
    a = jnp.exp(log_a)
    mult = jnp.sqrt(-_expm1(log_a, a) * (a + 1.0))
    return ub, r, i, a, mult


def _staged_shifts(stage, v, prev8, next8, downs, ups):
    t = v.shape[0]
    if prev8 is not None:
        stage[0:SUBLANES, :] = prev8
    stage[SUBLANES:SUBLANES + t, :] = v
    if next8 is not None:
        stage[SUBLANES + t:2 * SUBLANES + t, :] = next8
    return ([stage[SUBLANES - k:SUBLANES - k + t, :] for k in downs],
            [stage[SUBLANES + k:SUBLANES + k + t, :] for k in ups])


def _conv3(p, pp, w_ref, lo, stage=None):
    if stage is None:
        p1 = _shift_down(p, 1, pp)
        p2 = _shift_down(p, 2, pp)
    else:
        (p1, p2), _ = _staged_shifts(stage, p, pp, None, (1, 2), ())
    q = (w_ref[0:1, lo:lo + LANES] * p2 + w_ref[1:2, lo:lo + LANES] * p1) + w_ref[2:3, lo:lo + LANES] * p
    return q, p1, p2


def _conv4(xv, xp, w_ref, b_ref, lo, stage=None):
    if stage is None:
        x1 = _shift_down(xv, 1, xp)
        x2 = _shift_down(xv, 2, xp)
        x3 = _shift_down(xv, 3, xp)
    else:
        (x1, x2, x3), _ = _staged_shifts(stage, xv, xp, None, (1, 2, 3), ())
    u = (((w_ref[0:1, lo:lo + LANES] * x3 + w_ref[1:2, lo:lo + LANES] * x2) + w_ref[2:3, lo:lo + LANES] * x1)
         + w_ref[3:4, lo:lo + LANES] * xv) + b_ref[:, lo:lo + LANES]
    return u, x1, x2, x3


def _mix_in_fwd(x2d, mod6, g_mix, w_in_t, tm, rider=None):
    s, d = x2d.shape
    din = w_in_t.shape[0]

    def body(x_ref, mod_ref, g_ref, w_ref, hn_ref, proj_ref):
        xhat, _ = _rms(x_ref[...])
        hn = ((xhat * g_ref[...]) * (1.0 + mod_ref[1:2, :]) + mod_ref[0:1, :]).astype(BF16)
        hn_ref[...] = hn
        proj_ref[...] = _dot(hn, w_ref[...], NT)

    return _call(
        body, "mix_in_fwd", (s // tm,),
        [pl.BlockSpec((tm, d), lambda i: (i, 0)), _full(mod6.shape), _full(g_mix.shape), _full(w_in_t.shape)],
        [pl.BlockSpec((tm, d), lambda i: (i, 0)), pl.BlockSpec((tm, din), lambda i: (i, 0))],
        [jax.ShapeDtypeStruct((s, d), BF16), jax.ShapeDtypeStruct((s, din), F32)],
        [x2d, mod6, g_mix, w_in_t], rider=rider)


def _mixer_fwd(proj, conv_sc, conv_lru, conv_b, wa_bd, wx_bd, ba, bx, lam, width, rider=None):
    s, din = proj.shape
    t = min(MIX_ROWS, s)
    nblk = width // LANES
    hb = t // SUBLANES

    def body(proj_ref, projp_ref, wsc_ref, wlru_ref, blru_ref, wa_ref, wx_ref, ba_ref, bx_ref, lam_ref,
             ymix_ref, h_ref, hc_ref, stage_ref):
        i = pl.program_id(0)

        @pl.when(i == 0)
        def _():
            hc_ref[...] = jnp.zeros_like(hc_ref)

        has_prev = i > 0
        for j in range(nblk):
            lo = j * LANES

            def col(p, ref=proj_ref):
                return ref[:, p * width + lo:p * width + lo + LANES]

            def prev(p):
                return jnp.where(has_prev, col(p, projp_ref), 0.0)

            p = col(1) * col(2)
            q, _, _ = _conv3(p, prev(1) * prev(2), wsc_ref, lo, stage_ref.at[0])
            ymix_ref[:, lo:lo + LANES] = (col(0) * q).astype(BF16)

            u, _, _, _ = _conv4(col(4), prev(4), wlru_ref, blru_ref, lo, stage_ref.at[1])
            sp = _softplus(-lam_ref[:, lo:lo + LANES])
            _, r, ig, a, mult = _lru_gates(u, wa_ref[j], wx_ref[j], ba_ref[:, lo:lo + LANES], bx_ref[:, lo:lo + LANES], sp)
            h = _scan_fwd(a, mult * (ig * u), hc_ref[0:1, lo:lo + LANES], stage_ref.at[2], stage_ref.at[3])
            h_ref[:, lo:lo + LANES] = h
            hc_ref[0:1, lo:lo + LANES] = h[t - 1:t, :]
            gel, _ = _gelu(col(3))
            ymix_ref[:, width + lo:width + lo + LANES] = (gel * h).astype(BF16)

    small = [conv_sc, conv_lru, conv_b, wa_bd, wx_bd, ba, bx, lam]
    return _call(
        body, "mixer_fwd", (s // t,),
        [pl.BlockSpec((t, din), lambda i: (i, 0)),
         pl.BlockSpec((SUBLANES, din), lambda i: (jnp.maximum(i * hb - 1, 0), 0))]
        + [_full(a.shape) for a in small],
        [pl.BlockSpec((t, 2 * width), lambda i: (i, 0)), pl.BlockSpec((t, width), lambda i: (i, 0))],
        [jax.ShapeDtypeStruct((s, 2 * width), BF16), jax.ShapeDtypeStruct((s, width), F32)],
        [proj, proj, *small],
        scratch=[pltpu.VMEM((SUBLANES, width), F32), pltpu.VMEM((4, t + 2 * SUBLANES, LANES), F32)], rider=rider)


def _mix_out_fwd(ymix, x2d, w_out, mod6, g_mlp, tm, rider=None):
    s, d = x2d.shape

    def body(y_ref, x_ref, w_ref, mod_ref, g_ref, mix_ref, x2_ref, hn_ref):
        mix = _dot(y_ref[...], w_ref[...], NN)
        mix_ref[...] = mix
        x2 = x_ref[...] + mod_ref[2:3, :] * mix
        x2_ref[...] = x2
        xhat, _ = _rms(x2)
        hn_ref[...] = ((xhat * g_ref[...]) * (1.0 + mod_ref[4:5, :]) + mod_ref[3:4, :]).astype(BF16)

    tile = pl.BlockSpec((tm, d), lambda i: (i, 0))
    return _call(
        body, "mix_out_fwd", (s // tm,),
        [tile, tile, _full(w_out.shape), _full(mod6.shape), _full(g_mlp.shape)],
        [tile, tile, tile],
        [jax.ShapeDtypeStruct((s, d), F32), jax.ShapeDtypeStruct((s, d), F32), jax.ShapeDtypeStruct((s, d), BF16)],
        [ymix, x2d, w_out, mod6, g_mlp], rider=rider)


def _mlp_fwd_loss(hn2, w_up_t, w_down, x2, target, mod6, g_final, tm, tk):
    s, d = hn2.shape
    f = w_up_t.shape[0]
    nk = f // tk

    def body(hn_ref, wu_ref, wd_ref, x2_ref, t_ref, mod_ref, g_ref, z_ref, dx3_ref, dyb_ref, st_ref, y_ref):
        i, k = pl.program_id(0), pl.program_id(1)

        @pl.when(jnp.logical_and(i == 0, k == 0))
        def _():
            st_ref[...] = jnp.zeros_like(st_ref)

        z = jnp.maximum(_dot(hn_ref[...], wu_ref[...], NT), 0.0)
        z_ref[...] = z.astype(BF16)
        part = _dot((z * z).astype(BF16), wd_ref[...], NN)

        @pl.when(k == 0)
        def _():
            y_ref[...] = part

        @pl.when(k > 0)
        def _():
            y_ref[...] += part

        @pl.when(k == nk - 1)
        def _():
            gate = mod_ref[5:6, :]
            yv = y_ref[...]
            xhat, rstd = _rms(x2_ref[...] + gate * yv)
            diff = xhat * g_ref[...] - t_ref[...]
            dyo = diff * (1.0 / d)
            dx3 = _rms_bwd(dyo * g_ref[...], xhat, rstd)
            dx3_ref[...] = dx3
            dyb_ref[...] = (gate * dx3).astype(BF16)
            st_ref[0:1, :] += _colsum(dyo * xhat)
            st_ref[1:2, :] += _colsum(dx3 * yv)
            st_ref[2:3, :] += _colsum(diff * diff)

    tile = pl.BlockSpec((tm, d), lambda i, k: (i, 0))
    wblk = pl.BlockSpec((tk, d), lambda i, k: (k, 0))
    return pl.pallas_call(
        body, name="mlp_fwd_loss", grid=(s // tm, nk),
        in_specs=[tile, wblk, wblk, tile, tile, _full(mod6.shape), _full(g_final.shape)],
        out_specs=[pl.BlockSpec((tm, tk), lambda i, k: (i, k)), tile, tile, _full((SUBLANES, d))],
        out_shape=[jax.ShapeDtypeStruct((s, f), BF16), jax.ShapeDtypeStruct((s, d), F32),
                   jax.ShapeDtypeStruct((s, d), BF16), jax.ShapeDtypeStruct((SUBLANES, d), F32)],
        scratch_shapes=[pltpu.VMEM((tm, d), F32)],
        compiler_params=_params(("arbitrary", "arbitrary")),
    )(hn2, w_up_t, w_down, x2, target, mod6, g_final)


def _mlp_bwd_dx(dyb, z, w_down, w_up_t, tm, tk):
    s, d = dyb.shape
    f = z.shape[1]

    def body(dy_ref, z_ref, wd_ref, wu_ref, dz_ref, dh_ref):
        k = pl.program_id(1)
        dz = ((2.0 * z_ref[...].astype(F32)) * _dot(dy_ref[...], wd_ref[...], NT)).astype(BF16)
        dz_ref[...] = dz
        part = _dot(dz, wu_ref[...], NN)

        @pl.when(k == 0)
        def _():
            dh_ref[...] = part

        @pl.when(k > 0)
        def _():
            dh_ref[...] += part

    return pl.pallas_call(
        body, name="mlp_bwd_dx", grid=(s // tm, f // tk),
        in_specs=[pl.BlockSpec((tm, d), lambda i, k: (i, 0)), pl.BlockSpec((tm, tk), lambda i, k: (i, k)),
                  pl.BlockSpec((tk, d), lambda i, k: (k, 0)), pl.BlockSpec((tk, d), lambda i, k: (k, 0))],
        out_specs=[pl.BlockSpec((tm, tk), lambda i, k: (i, k)), pl.BlockSpec((tm, d), lambda i, k: (i, 0))],
        out_shape=[jax.ShapeDtypeStruct((s, f), BF16), jax.ShapeDtypeStruct((s, d), F32)],
        compiler_params=_params(("parallel", "arbitrary")),
    )(dyb, z, w_down, w_up_t)


def _mlp_bwd_dw(z, dz, dyb, hn2, tm, tk):
    s, d = dyb.shape
    f = z.shape[1]

    def body(z_ref, dz_ref, dy_ref, hn_ref, gd_ref, gu_ref):
        i = pl.program_id(1)

        @pl.when(i == 0)
        def _():
            gd_ref[...] = jnp.zeros_like(gd_ref)
            gu_ref[...] = jnp.zeros_like(gu_ref)

        zf = z_ref[...].astype(F32)
        gd_ref[...] += _dot((zf * zf).astype(BF16), dy_ref[...], TN)
        gu_ref[...] += _dot(dz_ref[...], hn_ref[...], TN)

    return pl.pallas_call(
        body, name="mlp_bwd_dw", grid=(f // tk, s // tm),
        in_specs=[pl.BlockSpec((tm, tk), lambda k, i: (i, k)), pl.BlockSpec((tm, tk), lambda k, i: (i, k)),
                  pl.BlockSpec((tm, d), lambda k, i: (i, 0)), pl.BlockSpec((tm, d), lambda k, i: (i, 0))],
        out_specs=[pl.BlockSpec((tk, d), lambda k, i: (k, 0)), pl.BlockSpec((tk, d), lambda k, i: (k, 0))],
        out_shape=[jax.ShapeDtypeStruct((f, d), F32), jax.ShapeDtypeStruct((f, d), F32)],
        compiler_params=_params(("parallel", "arbitrary")),
    )(z, dz, dyb, hn2)


def _mix_out_bwd(dhn2, x2, dx3, mix, ymix, w_out, mod6, g_mlp, tm, rider=None):
    s, d = x2.shape

    def body(dh_ref, x2_ref, dx3_ref, mix_ref, y_ref, w_ref, mod_ref, g_ref, dx2_ref, dym_ref, gw_ref, st_ref):
        i = pl.program_id(0)

        @pl.when(i == 0)
        def _():
            st_ref[...] = jnp.zeros_like(st_ref)
            gw_ref[...] = jnp.zeros_like(gw_ref)

        dh = dh_ref[...]
        xhat, rstd = _rms(x2_ref[...])
        dn = dh * (1.0 + mod_ref[4:5, :])
        dx2 = dx3_ref[...] + _rms_bwd(dn * g_ref[...], xhat, rstd)
        dx2_ref[...] = dx2
        st_ref[0:1, :] += _colsum(dh)
        st_ref[1:2, :] += _colsum(dh * (xhat * g_ref[...]))
        st_ref[2:3, :] += _colsum(dn * xhat)
        st_ref[3:4, :] += _colsum(dx2 * mix_ref[...])
        dmix = (mod_ref[2:3, :] * dx2).astype(BF16)
        dym_ref[...] = _dot(dmix, w_ref[...], NT)
        gw_ref[...] += _dot(y_ref[...], dmix, TN)

    tile = pl.BlockSpec((tm, d), lambda i: (i, 0))
    return _call(
        body, "mix_out_bwd", (s // tm,),
        [tile, tile, tile, tile, tile, _full(w_out.shape), _full(mod6.shape), _full(g_mlp.shape)],
        [tile, tile, _full((d, d)), _full((SUBLANES, d))],
        [jax.ShapeDtypeStruct((s, d), F32), jax.ShapeDtypeStruct((s, d), F32),
         jax.ShapeDtypeStruct((d, d), F32), jax.ShapeDtypeStruct((SUBLANES, d), F32)],
        [dhn2, x2, dx3, mix, ymix, w_out, mod6, g_mlp], rider=rider)


def _mixer_bwd(proj, dymix, h_all, conv_sc, conv_lru, conv_b, wa_bd, wx_bd, ba, bx, lam, width, rider=None):
    s, din = proj.shape
    t = min(MIX_ROWS, s)
    nt = s // t
    nblk = width // LANES
    hb = t // SUBLANES
    last8 = s // SUBLANES - 1

    def body(proj_ref, projp_ref, projn_ref, dy_ref, dyn_ref, h_ref, hp_ref,
             wsc_ref, wlru_ref, blru_ref, wa_ref, wx_ref, ba_ref, bx_ref, lam_ref,
             dproj_ref, small_ref, gwa_ref, gwx_ref, an_ref, gn_ref, dun_ref, stage_ref):
        i = pl.program_id(0)

        @pl.when(i == 0)
        def _():
            small_ref[...] = jnp.zeros_like(small_ref)
            gwa_ref[...] = jnp.zeros_like(gwa_ref)
            gwx_ref[...] = jnp.zeros_like(gwx_ref)
            an_ref[...] = jnp.zeros_like(an_ref)
            gn_ref[...] = jnp.zeros_like(gn_ref)
            dun_ref[...] = jnp.zeros_like(dun_ref)

        has_prev = i < nt - 1
        has_next = i > 0
        for j in range(nblk):
            lo = j * LANES
            ls = slice(lo, lo + LANES)

            def col(p, ref=proj_ref):
                return ref[:, p * width + lo:p * width + lo + LANES]

            def prev(p):
                return jnp.where(has_prev, col(p, projp_ref), 0.0)

            def nxt(p):
                return jnp.where(has_next, col(p, projn_ref), 0.0)

            def add_row(r, v):
                small_ref[r:r + 1, ls] += _colsum(v)

            sc_b, sc_c, sc_x = col(0), col(1), col(2)
            p = sc_c * sc_x
            q, p1, p2 = _conv3(p, prev(1) * prev(2), wsc_ref, lo, stage_ref.at[0])
            dys = dy_ref[:, ls]
            dproj_ref[:, ls] = (dys * q).astype(BF16)
            dq = dys * sc_b
            dqn = jnp.where(has_next, dyn_ref[:, ls], 0.0) * nxt(0)
            _, (dq1, dq2) = _staged_shifts(stage_ref.at[1], dq, None, dqn, (), (1, 2))
            dp = (wsc_ref[2:3, ls] * dq + wsc_ref[1:2, ls] * dq1) + wsc_ref[0:1, ls] * dq2
            dproj_ref[:, width + lo:width + lo + LANES] = (dp * sc_x).astype(BF16)
            dproj_ref[:, 2 * width + lo:2 * width + lo + LANES] = (dp * sc_c).astype(BF16)
            add_row(0, dq * p2)
            add_row(1, dq * p1)
            add_row(2, dq * p)

            xv = col(4)
            u, x1, x2, x3 = _conv4(xv, prev(4), wlru_ref, blru_ref, lo, stage_ref.at[2])
            lam_v = lam_ref[:, ls]
            sp = _softplus(-lam_v)
            wa, wx = wa_ref[j], wx_ref[j]
            ub, r, ig, a, mult = _lru_gates(u, wa, wx, ba_ref[:, ls], bx_ref[:, ls], sp)
            iu = ig * u
            h = h_ref[:, ls]
            (hm1,), _ = _staged_shifts(stage_ref.at[3], h, jnp.where(has_prev, hp_ref[:, ls], 0.0), None, (1,), ())
            lyv = col(3)
            gel, th = _gelu(lyv)
            dyl = dy_ref[:, width + lo:width + lo + LANES]
            dproj_ref[:, 3 * width + lo:3 * width + lo + LANES] = (dyl * h * _dgelu(lyv, th)).astype(BF16)
            a_next = jnp.broadcast_to(an_ref[0:1, ls], (SUBLANES, LANES))
            _, (a_up,) = _staged_shifts(stage_ref.at[4], a, None, a_next, (), (1,))
            g = _scan_rev(a_up, dyl * gel, gn_ref[0:1, ls], stage_ref.at[5], stage_ref.at[6])
            an_ref[0:1, ls] = a[0:1, :]
            gn_ref[0:1, ls] = g[0:1, :]
            da = g * hm1
            dmult = g * iu
            diu = g * mult
            dlog_a = da * a - dmult * ((a * a) / mult)
            dpre_a = (dlog_a * (-RG_C * sp)) * (r * (1.0 - r))
            dpre_x = (diu * u) * (ig * (1.0 - ig))
            dab, dxb = dpre_a.astype(BF16), dpre_x.astype(BF16)
            du = diu * ig + _dot(dab, wa, NT) + _dot(dxb, wx, NT)
            gwa_ref[j] += _dot(ub, dab, TN)
            gwx_ref[j] += _dot(ub, dxb, TN)
            dun = dun_ref[:, ls]
            dun_ref[:, ls] = du[0:SUBLANES, :]
            _, (du1, du2, du3) = _staged_shifts(stage_ref.at[7], du, None, dun, (), (1, 2, 3))
            dlx = (((wlru_ref[3:4, ls] * du + wlru_ref[2:3, ls] * du1) + wlru_ref[1:2, ls] * du2)
                   + wlru_ref[0:1, ls] * du3)
            dproj_ref[:, 4 * width + lo:4 * width + lo + LANES] = dlx.astype(BF16)
            add_row(3, du * x3)
            add_row(4, du * x2)
            add_row(5, du * x1)
            add_row(6, du * xv)
            add_row(7, du)
            add_row(8, dpre_a)
            add_row(9, dpre_x)
            add_row(10, (dlog_a * (RG_C * r)) * jax.nn.sigmoid(-lam_v))

    small = [conv_sc, conv_lru, conv_b, wa_bd, wx_bd, ba, bx, lam]
    rev = lambda i: nt - 1 - i
    return _call(
        body, "mixer_bwd", (nt,),
        [pl.BlockSpec((t, din), lambda i: (rev(i), 0)),
         pl.BlockSpec((SUBLANES, din), lambda i: (jnp.maximum(rev(i) * hb - 1, 0), 0)),
         pl.BlockSpec((SUBLANES, din), lambda i: (jnp.minimum((rev(i) + 1) * hb, last8), 0)),
         pl.BlockSpec((t, 2 * width), lambda i: (rev(i), 0)),
         pl.BlockSpec((SUBLANES, 2 * width), lambda i: (jnp.minimum((rev(i) + 1) * hb, last8), 0)),
         pl.BlockSpec((t, width), lambda i: (rev(i), 0)),
         pl.BlockSpec((SUBLANES, width), lambda i: (jnp.maximum(rev(i) * hb - 1, 0), 0))]
        + [_full(a.shape) for a in small],
        [pl.BlockSpec((t, din), lambda i: (rev(i), 0)), _full((2 * SUBLANES, width)),
         _full(wa_bd.shape), _full(wx_bd.shape)],
        [jax.ShapeDtypeStruct((s, din), BF16), jax.ShapeDtypeStruct((2 * SUBLANES, width), F32),
         jax.ShapeDtypeStruct(wa_bd.shape, F32), jax.ShapeDtypeStruct(wx_bd.shape, F32)],
        [proj, proj, proj, dymix, dymix, h_all, h_all, *small],
        scratch=[pltpu.VMEM((SUBLANES, width), F32), pltpu.VMEM((SUBLANES, width), F32),
                 pltpu.VMEM((SUBLANES, width), F32), pltpu.VMEM((8, t + 2 * SUBLANES, LANES), F32)], rider=rider)


def _mix_in_bwd_dx(dproj, x2d, dx2, w_in_t, mod6, g_mix, tm, rider=None):
    s, d = x2d.shape
    din = dproj.shape[1]

    def body(dp_ref, x_ref, dx2_ref, w_ref, mod_ref, g_ref, gx_ref, st_ref):
        i = pl.program_id(0)

        @pl.when(i == 0)
        def _():
            st_ref[...] = jnp.zeros_like(st_ref)

        dh = _dot(dp_ref[...], w_ref[...], NN)
        xhat, rstd = _rms(x_ref[...])
        dn = dh * (1.0 + mod_ref[1:2, :])
        gx_ref[...] = dx2_ref[...] + _rms_bwd(dn * g_ref[...], xhat, rstd)
        st_ref[0:1, :] += _colsum(dh)
        st_ref[1:2, :] += _colsum(dh * (xhat * g_ref[...]))
        st_ref[2:3, :] += _colsum(dn * xhat)

    tile = pl.BlockSpec((tm, d), lambda i: (i, 0))
    return _call(
        body, "mix_in_bwd_dx", (s // tm,),
        [pl.BlockSpec((tm, din), lambda i: (i, 0)), tile, tile, _full(w_in_t.shape), _full(mod6.shape),
         _full(g_mix.shape)],
        [tile, _full((SUBLANES, d))],
        [jax.ShapeDtypeStruct((s, d), F32), jax.ShapeDtypeStruct((SUBLANES, d), F32)],
        [dproj, x2d, dx2, w_in_t, mod6, g_mix], rider=rider)


def _mix_in_bwd_dw(dproj, hn1, tm, tn, rider=None):
    s, d = hn1.shape
    din = dproj.shape[1]

    def body(dp_ref, hn_ref, gw_ref):
        i = pl.program_id(1)

        @pl.when(i == 0)
        def _():
            gw_ref[...] = jnp.zeros_like(gw_ref)

        gw_ref[...] += _dot(dp_ref[...], hn_ref[...], TN)

    return _call(
        body, "mix_in_bwd_dw", (din // tn, s // tm),
        [pl.BlockSpec((tm, tn), lambda p, i: (i, p)), pl.BlockSpec((tm, d), lambda p, i: (i, 0))],
        [pl.BlockSpec((tn, d), lambda p, i: (p, 0))],
        [jax.ShapeDtypeStruct((din, d), F32)],
        [dproj, hn1], rider=rider)


def _adamw(w, g, m, v):
    m = ADAM_B1 * m + (1.0 - ADAM_B1) * g
    v = ADAM_B2 * v + (1.0 - ADAM_B2) * (g * g)
    m_hat = m / (1.0 - ADAM_B1 ** ADAM_STEP)
    v_hat = v / (1.0 - ADAM_B2 ** ADAM_STEP)
    delta = -ADAM_LR * (m_hat / (jnp.sqrt(v_hat) + ADAM_EPS) + ADAM_WD * w)
    return delta, m, v


def _pair_sum(g4, h4, core_chip, tr, name):
    _, _, r, n = g4.shape

    def body(sc_ref, g_ref, h_ref, sb_ref, own_ref):
        q = pl.program_id(1)
        ssum = g_ref[...] + h_ref[...]
        sb_ref[...] = ssum.astype(BF16)

        @pl.when(q == sc_ref[1])
        def _():
            own_ref[...] = ssum

    grid_spec = pltpu.PrefetchScalarGridSpec(
        num_scalar_prefetch=1, grid=(r // tr, 4),
        in_specs=[pl.BlockSpec((None, None, tr, n), lambda i, q, sc: (q, sc[0], i, 0)),
                  pl.BlockSpec((None, tr, n), lambda i, q, sc: (q, i, 0))],
        out_specs=[pl.BlockSpec((None, tr, n), lambda i, q, sc: (q, i, 0)),
                   pl.BlockSpec((tr, n), lambda i, q, sc: (i, 0))])
    return pl.pallas_call(
        body, name=name, grid_spec=grid_spec,
        out_shape=[jax.ShapeDtypeStruct((4, r, n), BF16), jax.ShapeDtypeStruct((r, n), F32)],
        compiler_params=_params(("parallel", "arbitrary")),
    )(core_chip, g4, h4)


def _sum4(own, parts, tr, name):
    r, n = own.shape

    def body(o_ref, p_ref, out_ref):
        acc = o_ref[...]
        for k in range(3):
            acc = acc + p_ref[k].astype(F32)
        out_ref[...] = acc

    return pl.pallas_call(
        body, name=name, grid=(r // tr,),
        in_specs=[pl.BlockSpec((tr, n), lambda i: (i, 0)), pl.BlockSpec((3, tr, n), lambda i: (0, i, 0))],
        out_specs=pl.BlockSpec((tr, n), lambda i: (i, 0)),
        out_shape=jax.ShapeDtypeStruct((r, n), F32),
        compiler_params=_params(("parallel",)),
    )(own, parts)


def _sum4_adam(own, parts, w, m, v, tr, name, transposed):
    r, n = own.shape
    rows, cols = w.shape

    def body(o_ref, p_ref, w_ref, m_ref, v_ref, g_ref, d_ref, nm_ref, nv_ref):
        g = o_ref[...]
        for k in range(3):
            g = g + p_ref[k].astype(F32)
        if transposed:
            g = g.T
        g_ref[...] = g
        d_ref[...], nm_ref[...], nv_ref[...] = _adamw(w_ref[...], g, m_ref[...], v_ref[...])

    if transposed:
        g_specs = [pl.BlockSpec((r, tr), lambda i: (0, i)), pl.BlockSpec((3, r, tr), lambda i: (0, 0, i))]
    else:
        g_specs = [pl.BlockSpec((tr, n), lambda i: (i, 0)), pl.BlockSpec((3, tr, n), lambda i: (0, i, 0))]
    tile = pl.BlockSpec((tr, cols), lambda i: (i, 0))
    return pl.pallas_call(
        body, name=name, grid=(rows // tr,),
        in_specs=g_specs + [tile] * 3, out_specs=[tile] * 4,
        out_shape=[jax.ShapeDtypeStruct((rows, cols), F32)] * 4,
        compiler_params=_params(("parallel",)),
    )(own, parts, w, m, v)


def _sum8(parts, tr, name):
    _, rows, n = parts.shape

    def body(p_ref, o_ref):
        acc = p_ref[0]
        for k in range(1, N_DEV):
            acc = acc + p_ref[k]
        o_ref[...] = acc

    return pl.pallas_call(
        body, name=name, grid=(rows // tr,),
        in_specs=[pl.BlockSpec((N_DEV, tr, n), lambda i: (0, i, 0))],
        out_specs=pl.BlockSpec((tr, n), lambda i: (i, 0)),
        out_shape=jax.ShapeDtypeStruct((rows, n), F32),
        compiler_params=_params(("parallel",)),
    )(parts)


def _adam_rows(w, g, m, v, tr, name):
    rows, n = w.shape

    def body(w_ref, g_ref, m_ref, v_ref, d_ref, nm_ref, nv_ref):
        d_ref[...], nm_ref[...], nv_ref[...] = _adamw(w_ref[...], g_ref[...], m_ref[...], v_ref[...])

    tile = pl.BlockSpec((tr, n), lambda i: (i, 0))
    return pl.pallas_call(
        body, name=name, grid=(rows // tr,),
        in_specs=[tile] * 4, out_specs=[tile] * 3,
        out_shape=[jax.ShapeDtypeStruct((rows, n), F32)] * 3,
        compiler_params=_params(("parallel",)),
    )(w, g, m, v)


def _ada_bwd_adam(cact_t, dmod_cols, w, m, v, tr):
    rows, n = w.shape

    def body(c_ref, d_ref, w_ref, m_ref, v_ref, g_ref, dl_ref, nm_ref, nv_ref):
        def term(b):
            return c_ref[b].astype(BF16).astype(F32) * d_ref[b:b + 1, :].astype(BF16).astype(F32)

        g = term(0)
        for b in range(1, N_DEV):
            g = g + term(b)
        g_ref[...] = g
        dl_ref[...], nm_ref[...], nv_ref[...] = _adamw(w_ref[...], g, m_ref[...], v_ref[...])

    tile = pl.BlockSpec((tr, n), lambda i: (i, 0))
    return pl.pallas_call(
        body, name="ada_bwd_adam", grid=(rows // tr,),
        in_specs=[pl.BlockSpec((N_DEV, tr, 1), lambda i: (0, i, 0)), _full(dmod_cols.shape), tile, tile, tile],
        out_specs=[tile] * 4,
        out_shape=[jax.ShapeDtypeStruct((rows, n), F32)] * 4,
        compiler_params=_params(("parallel",)),
    )(cact_t, dmod_cols, w, m, v)


def _adam_small(ws, gs, ms, vs):
    n = len(ws)

    def body(*refs):
        w_r, g_r, m_r, v_r = refs[:n], refs[n:2 * n], refs[2 * n:3 * n], refs[3 * n:4 * n]
        d_r, nm_r, nv_r = refs[4 * n:5 * n], refs[5 * n:6 * n], refs[6 * n:7 * n]
        for k in range(n):
            d_r[k][...], nm_r[k][...], nv_r[k][...] = _adamw(w_r[k][...], g_r[k][...], m_r[k][...], v_r[k][...])

    shapes = [jax.ShapeDtypeStruct(w.shape, F32) for w in ws]
    outs = pl.pallas_call(
        body, name="adam_small", out_shape=shapes * 3, compiler_params=_params(),
    )(*ws, *gs, *ms, *vs)
    return outs[:n], outs[n:2 * n], outs[2 * n:]


def _block_diag(w):
    h, hd, _ = w.shape
    per = LANES // hd
    eye = jnp.eye(per, dtype=w.dtype)
    w5 = w.reshape(h // per, per, hd, 1, hd) * eye[None, :, None, :, None]
    return w5.reshape(h // per, LANES, LANES)


def _block_diag_grad(g, h, hd):
    per = LANES // hd
    g5 = g.reshape(h // per, per, hd, per, hd)
    return jnp.stack([g5[:, a, :, a, :] for a in range(per)], axis=1).reshape(h, hd, hd)


def kernel(x, c, w_ada, b_ada, g_mix, w_in, conv_w_sc, conv_w_lru, conv_b_lru, w_rg_a, b_rg_a, w_rg_x, b_rg_x, lru_lambda, w_out, g_mlp, w_up, w_down, g_final, loss_target, m_w_ada, m_b_ada, m_g_mix, m_w_in, m_conv_w_sc, m_conv_w_lru, m_conv_b_lru, m_w_rg_a, m_b_rg_a, m_w_rg_x, m_b_rg_x, m_lru_lambda, m_w_out, m_g_mlp, m_w_up, m_w_down, m_g_final, v_w_ada, v_b_ada, v_g_mix, v_w_in, v_conv_w_sc, v_conv_w_lru, v_conv_b_lru, v_w_rg_a, v_b_rg_a, v_w_rg_x, v_b_rg_x, v_lru_lambda, v_w_out, v_g_mlp, v_w_up, v_w_down, v_g_final):
    s, d = x.shape[1], x.shape[2]
    width = conv_b_lru.shape[1]
    heads, hd = w_rg_a.shape[1], w_rg_a.shape[2]
    f = w_down.shape[1] * N_DEV
    n_ada = w_ada.shape[2]
    csh = conv_w_sc.shape[2]
    me = 4 * lax.axis_index("x") + 2 * lax.axis_index("y") + lax.axis_index("c")
    tm = min(512, s)
    tm_mlp = min(1024, s)
    tk = 512

    x2d = x[0]
    tgt = loss_target[0]

    pay = jnp.zeros((SUBLANES, d), F32)
    pay = pay.at[0:1, :].set(c)
    pay = pay.at[1:4, 0:csh].set(conv_w_sc[0])
    pay = pay.at[4:8, 0:csh].set(conv_w_lru[0])
    w_in_t_sh = w_in[0].T.astype(BF16)
    w_up_t_sh = w_up[0].T.astype(BF16)
    w_out_sh = w_out[0].astype(BF16)
    w_down_sh = w_down[0].astype(BF16)
    pay_all, w_in_t = _gather2("gather_in", [pay, w_in_t_sh])
    w_in_t = w_in_t.reshape(-1, d)
    c_all = pay_all[:, 0, :]
    conv_sc = pay_all[:, 1:4, 0:csh].transpose(1, 0, 2).reshape(3, width)
    conv_lru = pay_all[:, 4:8, 0:csh].transpose(1, 0, 2).reshape(4, width)

    b_ada_sh = lax.dynamic_slice(b_ada, (0, me * n_ada), (1, n_ada))
    mod_cols, c_act = _ada_fwd(c_all, w_ada[0], b_ada_sh)
    (mod_rows,) = _exchange("scatter_mod", [], [mod_cols.reshape(N_DEV, 1, n_ada)])
    mod_rows, w_out_sh, w_up_t_sh, w_down_sh = lax.optimization_barrier((mod_rows, w_out_sh, w_up_t_sh, w_down_sh))
    (w_out_g,) = _seq_gather2("gather_w_out", 1, [w_out_sh])
    w_up_g, w_down_g = _seq_gather2("gather_mlp_weights", 2, [w_up_t_sh, w_down_sh])
    mod6 = jnp.zeros((SUBLANES, d), F32).at[0:6, :].set(mod_rows.reshape(6, d))

    wa_bd = _block_diag(w_rg_a[0]).astype(BF16)
    wx_bd = _block_diag(w_rg_x[0]).astype(BF16)
    ba = b_rg_a.reshape(1, width)
    bx = b_rg_x.reshape(1, width)
    g_fin = g_final.reshape(1, d)

    (hn1, proj), _ = _mix_in_fwd(x2d, mod6, g_mix, w_in_t, tm)
    (ymix, h_all), _ = _mixer_fwd(proj, conv_sc, conv_lru, conv_b_lru, wa_bd, wx_bd, ba, bx, lru_lambda, width)
    w_out_b = w_out_g.reshape(-1, d)
    (mix, x2, hn2), _ = _mix_out_fwd(ymix, x2d, w_out_b, mod6, g_mlp, tm)
    w_up_t = w_up_g.reshape(-1, d)
    w_down_b = w_down_g.reshape(-1, d)
    z, dx3, dyb, st_fin = _mlp_fwd_loss(hn2, w_up_t, w_down_b, x2, tgt, mod6, g_fin, tm, 4 * tk)

    core_chip = jnp.stack([lax.axis_index("c"), 2 * lax.axis_index("x") + lax.axis_index("y")]).astype(jnp.int32)
    dz, dhn2 = _mlp_bwd_dx(dyb, z, w_down_b, w_up_t, tm_mlp, 2 * tk)
    g_down, g_up_t = _mlp_bwd_dw(z, dz, dyb, hn2, tm_mlp, 2 * tk)
    g_up4, g_down4 = g_up_t.reshape(4, 2, -1, d), g_down.reshape(4, 2, -1, d)
    h_up, h_down = _seq_pair_swap("swap_mlp_grads", 7, [g_up4, g_down4])
    (dx2, dymix, g_out, st_out), _ = _mix_out_bwd(dhn2, x2, dx3, mix, ymix, w_out_b, mod6, g_mlp, tm)
    h_up, h_down, g_out = lax.optimization_barrier((h_up, h_down, g_out))
    sb_up, own_up = _pair_sum(g_up4, h_up, core_chip, 256, "pair_sum_w_up")
    sb_down, own_down = _pair_sum(g_down4, h_down, core_chip, 256, "pair_sum_w_down")
    g_out4 = g_out.reshape(4, 2, -1, d)
    (h_out,) = _seq_pair_swap("swap_w_out_grad", 8, [g_out4])
    p_up, p_down = _seq_chip_exchange("exchange_mlp_grads", 3, [sb_up, sb_down])
    (dproj, g_small, g_wa, g_wx), _ = _mixer_bwd(
        proj, dymix, h_all, conv_sc, conv_lru, conv_b_lru, wa_bd, wx_bd, ba, bx, lru_lambda, width)
    h_out, dproj = lax.optimization_barrier((h_out, dproj))
    sb_out, own_out = _pair_sum(g_out4, h_out, core_chip, g_out4.shape[2], "pair_sum_w_out")
    (p_out,) = _seq_chip_exchange("exchange_w_out_grad", 4, [sb_out])
    (grad_x, st_in), _ = _mix_in_bwd_dx(dproj, x2d, dx2, w_in_t, mod6, g_mix, tm)

    small = jnp.concatenate([
        st_in[0:2], st_out[3:4], st_out[0:2], st_fin[1:2],
        st_in[2:3], st_out[2:3], st_fin[0:1],
        jnp.concatenate([g_small[7:8], g_small[10:11]], axis=1),
        jnp.concatenate([g_small[8:9], g_small[9:10]], axis=1),
        jnp.concatenate([jnp.concatenate([g_small[0:3], jnp.zeros((1, width), F32)], axis=0), g_small[3:7]], axis=1),
        st_fin[2:3],
        _block_diag_grad(g_wa, heads, hd).reshape(-1, d),
        _block_diag_grad(g_wx, heads, hd).reshape(-1, d),
    ], axis=0)

    (small_all,) = _seq_gather2("gather_small_grads", 5, [small])
    (g_in_t,), _ = _mix_in_bwd_dw(dproj, hn1, min(2048, s), dproj.shape[1] // 2)
    g_in4 = g_in_t.reshape(4, 2, -1, d)
    (h_in,) = _seq_pair_swap("swap_w_in_grad", 9, [g_in4])
    p_up, p_down, p_out, small_all, g_in_t = lax.optimization_barrier((p_up, p_down, p_out, small_all, g_in_t))

    ad_up = _sum4_adam(own_up, p_up, w_up[0], m_w_up[0], v_w_up[0], 256, "adam_w_up", True)
    h_in, ad_up = lax.optimization_barrier((h_in, ad_up))
    sb_in, own_in = _pair_sum(g_in4, h_in, core_chip, g_in4.shape[2], "pair_sum_w_in")
    (p_in,) = _seq_chip_exchange("exchange_w_in_grad", 6, [sb_in])
    ad_out = _sum4_adam(own_out, p_out, w_out[0], m_w_out[0], v_w_out[0], w_out.shape[1], "adam_w_out", False)
    ad_down = _sum4_adam(own_down, p_down, w_down[0], m_w_down[0], v_w_down[0], 256, "adam_w_down", False)

    gsum = _sum8(small_all, SMALL_ROWS, "sum_small")
    loss = (0.5 / d) * jnp.sum(gsum[15])
    dmod_cols = lax.dynamic_slice(small_all[:, 0:6, :].reshape(N_DEV, 6 * d), (0, me * n_ada), (N_DEV, n_ada))
    g_ada, d_ada, nm_ada, nv_ada = _ada_bwd_adam(c_act[:, :, None], dmod_cols, w_ada[0], m_w_ada[0], v_w_ada[0], 256)

    g_conv = lax.dynamic_slice(gsum[11:15, 0:width], (0, me * csh), (4, csh))
    g_conv_l = lax.dynamic_slice(gsum[11:15, width:2 * width], (0, me * csh), (4, csh))
    small_g = [
        gsum[0:6].reshape(1, 6 * d),
        gsum[6:7],
        g_conv[0:3].reshape(1, 3, csh),
        g_conv_l.reshape(1, 4, csh),
        gsum[9:10, 0:width],
        gsum[16:48].reshape(1, heads, hd, hd),
        gsum[10:11, 0:width].reshape(1, heads, hd),
        gsum[48:80].reshape(1, heads, hd, hd),
        gsum[10:11, width:].reshape(1, heads, hd),
        gsum[9:10, width:],
        gsum[7:8],
        gsum[8],
    ]
    small_w = [b_ada, g_mix, conv_w_sc, conv_w_lru, conv_b_lru, w_rg_a, b_rg_a, w_rg_x, b_rg_x, lru_lambda, g_mlp, g_final]
    small_m = [m_b_ada, m_g_mix, m_conv_w_sc, m_conv_w_lru, m_conv_b_lru, m_w_rg_a, m_b_rg_a, m_w_rg_x, m_b_rg_x,
               m_lru_lambda, m_g_mlp, m_g_final]
    small_v = [v_b_ada, v_g_mix, v_conv_w_sc, v_conv_w_lru, v_conv_b_lru, v_w_rg_a, v_b_rg_a, v_w_rg_x, v_b_rg_x,
               v_lru_lambda, v_g_mlp, v_g_final]
    sd, snm, snv = _adam_small(small_w, small_g, small_m, small_v)
    p_in, ad_out, ad_down, (g_ada, d_ada, nm_ada, nv_ada), sd = lax.optimization_barrier(
        (p_in, ad_out, ad_down, (g_ada, d_ada, nm_ada, nv_ada), sd))
    ad_in = _sum4_adam(own_in, p_in, w_in[0].T, m_w_in[0].T, v_w_in[0].T, own_in.shape[0], "adam_w_in", False)
    ad_in = [a.T for a in ad_in]

    def order(ada, w_in_, w_out_, w_up_, w_down_, sm):
        return [ada[None], sm[0], sm[1], w_in_[None], sm[2], sm[3], sm[4], sm[5], sm[6], sm[7], sm[8], sm[9],
                w_out_[None], sm[10], w_up_[None], w_down_[None], sm[11]]

    grads = order(g_ada, ad_in[0], ad_out[0], ad_up[0], ad_down[0], small_g)
    deltas = order(d_ada, ad_in[1], ad_out[1], ad_up[1], ad_down[1], sd)
    new_m = order(nm_ada, ad_in[2], ad_out[2], ad_up[2], ad_down[2], snm)
    new_v = order(nv_ada, ad_in[3], ad_out[3], ad_up[3], ad_down[3], snv)
    return (loss, grad_x[None], *grads, *deltas, *new_m, *new_v)
```

```python
import functools

import jax
import jax.numpy as jnp
from jax import lax
from jax.experimental import pallas as pl
from jax.experimental.pallas import tpu as pltpu
from jax.experimental.pallas import tpu_sc as plsc

F32 = jnp.float32
BF16 = jnp.bfloat16
N_DEV = 8
EPS = 1e-6
RG_C = 8.0
GELU_K0 = 0.7978845608028654
GELU_K1 = 0.044715
ADAM_LR = 0.001
ADAM_B1 = 0.9
ADAM_B2 = 0.999
ADAM_EPS = 1e-08
ADAM_WD = 0.01
ADAM_STEP = 10
LANES = 128
SUBLANES = 8
VMEM_LIMIT = 52 * 1024 * 1024
MIX_ROWS = 256
SMALL_ROWS = 80

MESH = pl.DeviceIdType.MESH
ANY = pl.BlockSpec(memory_space=pl.ANY)
NN = ((1,), (0,))
NT = ((1,), (1,))
TN = ((0,), (0,))


def _dot(a, b, dims):
    return lax.dot_general(a, b, (dims, ((), ())), preferred_element_type=F32)


def _params(sem=None):
    return pltpu.CompilerParams(dimension_semantics=sem, vmem_limit_bytes=VMEM_LIMIT)


def _full(shape):
    nd = len(shape)
    return pl.BlockSpec(shape, lambda *_: (0,) * nd)


def _exchange(name, gathers, scatters):
    n_g = len(gathers)
    arrs = list(gathers) + list(scatters)
    n = len(arrs)
    out_shape = [jax.ShapeDtypeStruct((N_DEV,) + a.shape, a.dtype) for a in gathers]
    out_shape += [jax.ShapeDtypeStruct(a.shape, a.dtype) for a in scatters]

    def body(*refs):
        ins, outs = refs[:n], refs[n:2 * n]
        send_sems, recv_sems, local_sems = refs[2 * n:]
        x, y, c = lax.axis_index("x"), lax.axis_index("y"), lax.axis_index("c")
        me = 4 * x + 2 * y + c

        def src(a, dev):
            return ins[a] if a < n_g else ins[a].at[dev]

        def peer_of(k):
            px = 1 - x if (k >> 2) & 1 else x
            py = 1 - y if (k >> 1) & 1 else y
            pc = 1 - c if k & 1 else c
            return (px, py, pc), 4 * px + 2 * py + pc

        local = [pltpu.make_async_copy(src(a, me), outs[a].at[me], local_sems.at[a]) for a in range(n)]
        for cp in local:
            cp.start()
        sends = []
        for k in range(1, N_DEV):
            peer, pidx = peer_of(k)
            for a in range(n):
                cp = pltpu.make_async_remote_copy(
                    src_ref=src(a, pidx), dst_ref=outs[a].at[me],
                    send_sem=send_sems.at[a * (N_DEV - 1) + k - 1], recv_sem=recv_sems.at[a * (N_DEV - 1) + k - 1],
                    device_id=peer, device_id_type=MESH)
                cp.start()
                sends.append(cp)
        for k in range(1, N_DEV):
            peer, pidx = peer_of(k)
            for a in range(n):
                pltpu.make_async_remote_copy(
                    src_ref=src(a, pidx), dst_ref=outs[a].at[pidx],
                    send_sem=send_sems.at[a * (N_DEV - 1) + k - 1], recv_sem=recv_sems.at[a * (N_DEV - 1) + k - 1],
                    device_id=peer, device_id_type=MESH).wait_recv()
        for cp in sends:
            cp.wait_send()
        for cp in local:
            cp.wait()

    return pl.pallas_call(
        body, name=name, out_shape=out_shape,
        in_specs=[ANY] * n, out_specs=[ANY] * n,
        scratch_shapes=[pltpu.SemaphoreType.DMA((n * (N_DEV - 1),)),
                        pltpu.SemaphoreType.DMA((n * (N_DEV - 1),)),
                        pltpu.SemaphoreType.DMA((n,))],
    )(*arrs)


def _gather2(name, arrs):
    n = len(arrs)
    per = 7
    out_shape = [jax.ShapeDtypeStruct((N_DEV,) + a.shape, a.dtype) for a in arrs]

    def body(*refs):
        ins, outs = refs[:n], refs[n:2 * n]
        send_sems, recv_sems, local_sems = refs[2 * n:]
        x, y, c = lax.axis_index("x"), lax.axis_index("y"), lax.axis_index("c")
        sib = (x, y, 1 - c)
        chips = [(1 - x, y), (x, 1 - y), (1 - x, 1 - y)]

        def slot(a, px, py, pc):
            return outs[a].at[4 * px + 2 * py + pc]

        def copy(a, k, block, to, src=None):
            return pltpu.make_async_remote_copy(
                src_ref=slot(a, *block) if src is None else src, dst_ref=slot(a, *block),
                send_sem=send_sems.at[a * per + k], recv_sem=recv_sems.at[a * per + k],
                device_id=to, device_id_type=MESH)

        local = [pltpu.make_async_copy(ins[a], slot(a, x, y, c), local_sems.at[a]) for a in range(n)]
        for cp in local:
            cp.start()
        first = []
        for a in range(n):
            first += [copy(a, 1 + j, (x, y, c), (*chip, c), src=ins[a]) for j, chip in enumerate(chips)]
        for a in range(n):
            first.append(copy(a, 0, (x, y, c), sib, src=ins[a]))
        for cp in first:
            cp.start()
        passed = []
        for a in range(n):
            for j, chip in enumerate(chips):
                copy(a, 1 + j, (*chip, c), (x, y, c)).wait_recv()
                cp = copy(a, 4 + j, (*chip, c), sib)
                cp.start()
                passed.append(cp)
        for a in range(n):
            copy(a, 0, sib, (x, y, c)).wait_recv()
            for j, chip in enumerate(chips):
                copy(a, 4 + j, (*chip, 1 - c), (x, y, c)).wait_recv()
        for cp in first + passed:
            cp.wait_send()
        for cp in local:
            cp.wait()

    return pl.pallas_call(
        body, name=name, out_shape=out_shape,
        in_specs=[ANY] * n, out_specs=[ANY] * n,
        scratch_shapes=[pltpu.SemaphoreType.DMA((n * per,)), pltpu.SemaphoreType.DMA((n * per,)),
                        pltpu.SemaphoreType.DMA((n,))],
    )(*arrs)


def _seq_gather2(name, collective_id, arrs):
    n = len(arrs)
    per = 7

    def body(*refs):
        ins, outs = refs[:n], refs[n:2 * n]
        send_sems, recv_sems, local_sems = refs[2 * n:]
        x, y, c = lax.axis_index("x"), lax.axis_index("y"), lax.axis_index("c")
        sib = (x, y, 1 - c)
        chips = [(1 - x, y), (x, 1 - y), (1 - x, 1 - y)]
        barrier = pltpu.get_barrier_semaphore()
        for peer in [sib] + [(*chip, c) for chip in chips]:
            pl.semaphore_signal(barrier, inc=1, device_id=peer, device_id_type=MESH)
        pl.semaphore_wait(barrier, 4)

        def slot(a, px, py, pc):
            return outs[a].at[4 * px + 2 * py + pc]

        def copy(a, k, block, to, src=None):
            return pltpu.make_async_remote_copy(
                src_ref=slot(a, *block) if src is None else src, dst_ref=slot(a, *block),
                send_sem=send_sems.at[a * per + k], recv_sem=recv_sems.at[a * per + k],
                device_id=to, device_id_type=MESH)

        local = [pltpu.make_async_copy(ins[a], slot(a, x, y, c), local_sems.at[a]) for a in range(n)]
        for cp in local:
            cp.start()
        first = []
        for a in range(n):
            first += [copy(a, 1 + j, (x, y, c), (*chip, c), src=ins[a]) for j, chip in enumerate(chips)]
        for a in range(n):
            first.append(copy(a, 0, (x, y, c), sib, src=ins[a]))
        for cp in first:
            cp.start()
        passed = []
        for a in range(n):
            for j, chip in enumerate(chips):
                copy(a, 1 + j, (*chip, c), (x, y, c)).wait_recv()
                cp = copy(a, 4 + j, (*chip, c), sib)
                cp.start()
                passed.append(cp)
        for a in range(n):
            copy(a, 0, sib, (x, y, c)).wait_recv()
            for j, chip in enumerate(chips):
                copy(a, 4 + j, (*chip, 1 - c), (x, y, c)).wait_recv()
        for cp in first + passed:
            cp.wait_send()
        for cp in local:
            cp.wait()

    return pl.kernel(
        body, out_type=[jax.ShapeDtypeStruct((N_DEV,) + a.shape, a.dtype) for a in arrs],
        mesh=plsc.ScalarSubcoreMesh(axis_name="seq", num_cores=1),
        scratch_types=[pltpu.SemaphoreType.DMA((n * per,)), pltpu.SemaphoreType.DMA((n * per,)),
                       pltpu.SemaphoreType.DMA((n,))],
        compiler_params=pltpu.CompilerParams(collective_id=collective_id), name=name,
    )(*arrs)


def _seq_chip_exchange(name, collective_id, arrs):
    n = len(arrs)

    def body(*refs):
        ins, outs = refs[:n], refs[n:2 * n]
        send_sems, recv_sems = refs[2 * n:]
        x, y, c = lax.axis_index("x"), lax.axis_index("y"), lax.axis_index("c")

        def peer(k):
            return (1 - x if (k >> 1) & 1 else x), (1 - y if k & 1 else y)

        barrier = pltpu.get_barrier_semaphore()
        for k in (1, 2, 3):
            pl.semaphore_signal(barrier, inc=1, device_id=(*peer(k), c), device_id_type=MESH)
        pl.semaphore_wait(barrier, 3)

        def copy(a, k):
            px, py = peer(k)
            return pltpu.make_async_remote_copy(
                src_ref=ins[a].at[2 * px + py], dst_ref=outs[a].at[k - 1],
                send_sem=send_sems.at[a * 3 + k - 1], recv_sem=recv_sems.at[a * 3 + k - 1],
                device_id=(px, py, c), device_id_type=MESH)

        cps = [copy(a, k) for a in range(n) for k in (1, 2, 3)]
        for cp in cps:
            cp.start()
        for cp in cps:
            cp.wait_recv()
        for cp in cps:
            cp.wait_send()

    return pl.kernel(
        body, out_type=[jax.ShapeDtypeStruct((3,) + a.shape[1:], a.dtype) for a in arrs],
        mesh=plsc.ScalarSubcoreMesh(axis_name="seq", num_cores=1),
        scratch_types=[pltpu.SemaphoreType.DMA((n * 3,)), pltpu.SemaphoreType.DMA((n * 3,))],
        compiler_params=pltpu.CompilerParams(collective_id=collective_id), name=name,
    )(*arrs)


def _seq_pair_swap(name, collective_id, arrs):
    n = len(arrs)

    def body(*refs):
        ins, outs = refs[:n], refs[n:2 * n]
        send_sems, recv_sems = refs[2 * n:]
        x, y, c = lax.axis_index("x"), lax.axis_index("y"), lax.axis_index("c")
        barrier = pltpu.get_barrier_semaphore()
        pl.semaphore_signal(barrier, inc=1, device_id=(x, y, 1 - c), device_id_type=MESH)
        pl.semaphore_wait(barrier, 1)

        def copy(a, q):
            return pltpu.make_async_remote_copy(
                src_ref=ins[a].at[q, 1 - c], dst_ref=outs[a].at[q],
                send_sem=send_sems.at[a * 4 + q], recv_sem=recv_sems.at[a * 4 + q],
                device_id=(x, y, 1 - c), device_id_type=MESH)

        cps = [copy(a, q) for a in range(n) for q in range(4)]
        for cp in cps:
            cp.start()
        for cp in cps:
            cp.wait_recv()
        for cp in cps:
            cp.wait_send()

    return pl.kernel(
        body, out_type=[jax.ShapeDtypeStruct((4,) + a.shape[2:], a.dtype) for a in arrs],
        mesh=plsc.ScalarSubcoreMesh(axis_name="seq", num_cores=1),
        scratch_types=[pltpu.SemaphoreType.DMA((n * 4,)), pltpu.SemaphoreType.DMA((n * 4,))],
        compiler_params=pltpu.CompilerParams(collective_id=collective_id), name=name,
    )(*arrs)


def _pair_swap(name, arrs):
    n = len(arrs)
    out_shape = [jax.ShapeDtypeStruct((4,) + a.shape[2:], a.dtype) for a in arrs]

    def body(*refs):
        ins, outs = refs[:n], refs[n:2 * n]
        send_sems, recv_sems = refs[2 * n:]
        x, y, c = lax.axis_index("x"), lax.axis_index("y"), lax.axis_index("c")

        def copy(a, q):
            return pltpu.make_async_remote_copy(
                src_ref=ins[a].at[q, 1 - c], dst_ref=outs[a].at[q],
                send_sem=send_sems.at[a * 4 + q], recv_sem=recv_sems.at[a * 4 + q],
                device_id=(x, y, 1 - c), device_id_type=MESH)

        cps = [copy(a, q) for a in range(n) for q in range(4)]
        for cp in cps:
            cp.start()
        for cp in cps:
            cp.wait_recv()
        for cp in cps:
            cp.wait_send()

    return pl.pallas_call(
        body, name=name, out_shape=out_shape,
        in_specs=[ANY] * n, out_specs=[ANY] * n,
        scratch_shapes=[pltpu.SemaphoreType.DMA((n * 4,)), pltpu.SemaphoreType.DMA((n * 4,))],
    )(*arrs)


def _chip_exchange(name, arrs):
    n = len(arrs)
    out_shape = [jax.ShapeDtypeStruct((3,) + a.shape[1:], a.dtype) for a in arrs]

    def body(*refs):
        ins, outs = refs[:n], refs[n:2 * n]
        send_sems, recv_sems = refs[2 * n:]
        x, y, c = lax.axis_index("x"), lax.axis_index("y"), lax.axis_index("c")

        def copy(a, k):
            px = 1 - x if (k >> 1) & 1 else x
            py = 1 - y if k & 1 else y
            return pltpu.make_async_remote_copy(
                src_ref=ins[a].at[2 * px + py], dst_ref=outs[a].at[k - 1],
                send_sem=send_sems.at[a * 3 + k - 1], recv_sem=recv_sems.at[a * 3 + k - 1],
                device_id=(px, py, c), device_id_type=MESH)

        cps = [copy(a, k) for a in range(n) for k in (1, 2, 3)]
        for cp in cps:
            cp.start()
        for cp in cps:
            cp.wait_recv()
        for cp in cps:
            cp.wait_send()

    return pl.pallas_call(
        body, name=name, out_shape=out_shape,
        in_specs=[ANY] * n, out_specs=[ANY] * n,
        scratch_shapes=[pltpu.SemaphoreType.DMA((n * 3,)), pltpu.SemaphoreType.DMA((n * 3,))],
    )(*arrs)


class _Rider:
    def __init__(self, arrays, out_shapes, n_sems, build, aliases=None):
        self.arrays, self.out_shapes, self.n_sems, self.build = list(arrays), list(out_shapes), n_sems, build
        self.aliases = dict(aliases or {})


def _merge_riders(r1, r2):
    n1i, n1o, n1s = len(r1.arrays), len(r1.out_shapes), r1.n_sems

    def build(ins, outs, send_sems, recv_sems):
        a = r1.build(ins[:n1i], outs[:n1o], send_sems.at[pl.ds(0, n1s)], recv_sems.at[pl.ds(0, n1s)])
        b = r2.build(ins[n1i:], outs[n1o:], send_sems.at[pl.ds(n1s, r2.n_sems)], recv_sems.at[pl.ds(n1s, r2.n_sems)])
        return tuple(p + q for p, q in zip(a, b))

    aliases = dict(r1.aliases)
    aliases.update({k + n1i: v + n1o for k, v in r2.aliases.items()})
    return _Rider(r1.arrays + r2.arrays, r1.out_shapes + r2.out_shapes, n1s + r2.n_sems, build, aliases)


def _place():
    x, y, c = lax.axis_index("x"), lax.axis_index("y"), lax.axis_index("c")
    chips = [(1 - x, y), (x, 1 - y), (1 - x, 1 - y)]
    return x, y, c, chips


def _ride_gather_ici(arrs):
    n = len(arrs)

    def build(ins, outs, send_sems, recv_sems):
        x, y, c, chips = _place()
        peers = [(*chip, c) for chip in chips] + [(x, y, 1 - c)]
        me = 4 * x + 2 * y + c
        local = [pltpu.make_async_copy(ins[a], outs[a].at[me], send_sems.at[a * 5 + 4]) for a in range(n)]
        sends, recvs = [], []
        for a in range(n):
            for j, (px, py, pc) in enumerate(peers):
                sends.append(pltpu.make_async_remote_copy(
                    src_ref=ins[a], dst_ref=outs[a].at[me], send_sem=send_sems.at[a * 5 + j],
                    recv_sem=recv_sems.at[a * 5 + j], device_id=(px, py, pc), device_id_type=MESH))
                recvs.append(pltpu.make_async_remote_copy(
                    src_ref=ins[a], dst_ref=outs[a].at[4 * px + 2 * py + pc], send_sem=send_sems.at[a * 5 + j],
                    recv_sem=recv_sems.at[a * 5 + j], device_id=(px, py, pc), device_id_type=MESH))
        return local, sends, recvs

    shapes = [jax.ShapeDtypeStruct((N_DEV,) + a.shape, a.dtype) for a in arrs]
    return _Rider(arrs, shapes, n * 5, build)


def _ride_gather_direct(arrs):
    n = len(arrs)

    def build(ins, outs, send_sems, recv_sems):
        x, y, c, _ = _place()
        me = 4 * x + 2 * y + c
        local = [pltpu.make_async_copy(ins[a], outs[a].at[me], send_sems.at[a * N_DEV + 7]) for a in range(n)]
        sends, recvs = [], []
        for a in range(n):
            for k in range(1, N_DEV):
                px = 1 - x if (k >> 2) & 1 else x
                py = 1 - y if (k >> 1) & 1 else y
                pc = 1 - c if k & 1 else c
                sem = a * N_DEV + k - 1
                sends.append(pltpu.make_async_remote_copy(
                    src_ref=ins[a], dst_ref=outs[a].at[me], send_sem=send_sems.at[sem], recv_sem=recv_sems.at[sem],
                    device_id=(px, py, pc), device_id_type=MESH))
                recvs.append(pltpu.make_async_remote_copy(
                    src_ref=ins[a], dst_ref=outs[a].at[4 * px + 2 * py + pc], send_sem=send_sems.at[sem],
                    recv_sem=recv_sems.at[sem], device_id=(px, py, pc), device_id_type=MESH))
        return local, sends, recvs

    shapes = [jax.ShapeDtypeStruct((N_DEV,) + a.shape, a.dtype) for a in arrs]
    return _Rider(arrs, shapes, n * N_DEV, build)


def _ride_gather_d2d(gathered):
    n = len(gathered)

    def build(ins, outs, send_sems, recv_sems):
        x, y, c, chips = _place()
        sends, recvs = [], []
        for a in range(n):
            for j, (px, py) in enumerate(chips):
                mine = outs[a].at[4 * px + 2 * py + c]
                theirs = outs[a].at[4 * px + 2 * py + 1 - c]
                sends.append(pltpu.make_async_remote_copy(
                    src_ref=mine, dst_ref=mine, send_sem=send_sems.at[a * 3 + j], recv_sem=recv_sems.at[a * 3 + j],
                    device_id=(x, y, 1 - c), device_id_type=MESH))
                recvs.append(pltpu.make_async_remote_copy(
                    src_ref=mine, dst_ref=theirs, send_sem=send_sems.at[a * 3 + j], recv_sem=recv_sems.at[a * 3 + j],
                    device_id=(x, y, 1 - c), device_id_type=MESH))
        return [], sends, recvs

    shapes = [jax.ShapeDtypeStruct(a.shape, a.dtype) for a in gathered]
    return _Rider(gathered, shapes, n * 3, build, aliases={a: a for a in range(n)})


def _ride_pair_swap(arrs):
    n = len(arrs)

    def build(ins, outs, send_sems, recv_sems):
        x, y, c, _ = _place()
        cps = [pltpu.make_async_remote_copy(
            src_ref=ins[a].at[q, 1 - c], dst_ref=outs[a].at[q], send_sem=send_sems.at[a * 4 + q],
            recv_sem=recv_sems.at[a * 4 + q], device_id=(x, y, 1 - c), device_id_type=MESH)
            for a in range(n) for q in range(4)]
        return [], cps, cps

    shapes = [jax.ShapeDtypeStruct((4,) + a.shape[2:], a.dtype) for a in arrs]
    return _Rider(arrs, shapes, n * 4, build)


def _ride_chip_exchange(arrs):
    n = len(arrs)

    def build(ins, outs, send_sems, recv_sems):
        x, y, c, _ = _place()
        cps = []
        for a in range(n):
            for k in (1, 2, 3):
                px = 1 - x if (k >> 1) & 1 else x
                py = 1 - y if k & 1 else y
                cps.append(pltpu.make_async_remote_copy(
                    src_ref=ins[a].at[2 * px + py], dst_ref=outs[a].at[k - 1], send_sem=send_sems.at[a * 3 + k - 1],
                    recv_sem=recv_sems.at[a * 3 + k - 1], device_id=(px, py, c), device_id_type=MESH))
        return [], cps, cps

    shapes = [jax.ShapeDtypeStruct((3,) + a.shape[1:], a.dtype) for a in arrs]
    return _Rider(arrs, shapes, n * 3, build)


def _call(body, name, grid, in_specs, out_specs, out_shape, args, scratch=(), rider=None):
    n_in, n_out, n_scr = len(in_specs), len(out_specs), len(scratch)
    sem = ("arbitrary",) * len(grid)
    if rider is None:
        outs = pl.pallas_call(
            body, name=name, grid=grid, in_specs=in_specs, out_specs=out_specs, out_shape=out_shape,
            scratch_shapes=list(scratch), compiler_params=_params(sem))(*args)
        return outs, []
    ri, ro = len(rider.arrays), len(rider.out_shapes)

    def riding(*refs):
        ins, r_ins = refs[:n_in], refs[n_in:n_in + ri]
        outs = refs[n_in + ri:n_in + ri + n_out]
        r_outs = refs[n_in + ri + n_out:n_in + ri + n_out + ro]
        scr = refs[n_in + ri + n_out + ro:n_in + ri + n_out + ro + n_scr]
        send_sems, recv_sems = refs[-2:]
        first = functools.reduce(jnp.logical_and, [pl.program_id(k) == 0 for k in range(len(grid))])
        last = functools.reduce(jnp.logical_and, [pl.program_id(k) == grid[k] - 1 for k in range(len(grid))])

        @pl.when(first)
        def _():
            local, sends, _ = rider.build(r_ins, r_outs, send_sems, recv_sems)
            for cp in local + sends:
                cp.start()

        body(*ins, *outs, *scr)

        @pl.when(last)
        def _():
            local, sends, recvs = rider.build(r_ins, r_outs, send_sems, recv_sems)
            for cp in recvs:
                cp.wait_recv()
            for cp in sends:
                cp.wait_send()
            for cp in local:
                cp.wait()

    outs = pl.pallas_call(
        riding, name=name, grid=grid,
        in_specs=list(in_specs) + [ANY] * ri, out_specs=list(out_specs) + [ANY] * ro,
        out_shape=list(out_shape) + rider.out_shapes,
        scratch_shapes=list(scratch) + [pltpu.SemaphoreType.DMA((rider.n_sems,)), pltpu.SemaphoreType.DMA((rider.n_sems,))],
        input_output_aliases={n_in + k: n_out + v for k, v in rider.aliases.items()},
        compiler_params=_params(sem))(*args, *rider.arrays)
    return outs[:n_out], outs[n_out:]


def _comm(name, rider):
    def body(dummy_ref, out_ref):
        out_ref[...] = dummy_ref[...]

    dummy = jnp.zeros((SUBLANES, LANES), F32)
    spec = pl.BlockSpec((SUBLANES, LANES), lambda i: (0, 0))
    _, r_outs = _call(body, name, (1,), [spec], [spec], [jax.ShapeDtypeStruct(dummy.shape, F32)], [dummy], rider=rider)
    return r_outs


def _ada_fwd(c_all, w_ada_sh, b_ada_sh):
    nb, d = c_all.shape
    ncol = w_ada_sh.shape[1]

    def body(c_ref, w_ref, b_ref, mod_ref, cact_ref):
        cc = c_ref[...]
        ca = cc * jax.nn.sigmoid(cc)
        cact_ref[...] = ca
        mod_ref[...] = _dot(ca.astype(BF16), w_ref[...].astype(BF16), NN) + b_ref[...]

    return pl.pallas_call(
        body, name="ada_fwd",
        out_shape=[jax.ShapeDtypeStruct((nb, ncol), F32), jax.ShapeDtypeStruct((nb, d), F32)],
        compiler_params=_params(),
    )(c_all, w_ada_sh, b_ada_sh)


def _rms(xv):
    rstd = lax.rsqrt(jnp.mean(xv * xv, axis=-1, keepdims=True) + EPS)
    return xv * rstd, rstd


def _rms_bwd(dxhat, xhat, rstd):
    return rstd * (dxhat - xhat * jnp.mean(dxhat * xhat, axis=-1, keepdims=True))


def _colsum(v):
    return jnp.sum(v, axis=0, keepdims=True)


def _expm1(v, ev):
    series = v * (1.0 + v * (0.5 + v * (1.0 / 6.0 + v * (1.0 / 24.0 + v * (1.0 / 120.0)))))
    return jnp.where(jnp.abs(v) < 0.2, series, ev - 1.0)


def _softplus(v):
    return jnp.maximum(v, 0.0) + jnp.log1p(jnp.exp(-jnp.abs(v)))


def _gelu(v):
    t = jnp.tanh(v * (GELU_K0 + (GELU_K0 * GELU_K1) * (v * v)))
    return 0.5 * v * (1.0 + t), t


def _dgelu(v, t):
    return 0.5 * ((1.0 + t) + (v * (1.0 - t * t)) * (GELU_K0 + (3.0 * GELU_K0 * GELU_K1) * (v * v)))


def _shift_down(v, k, prev8):
    r = pltpu.roll(v, k, 0)
    pr = pltpu.roll(prev8, k, 0)
    row8 = lax.broadcasted_iota(jnp.int32, prev8.shape, 0)
    top = jnp.where(row8 < k, pr, r[0:SUBLANES])
    return jnp.concatenate([top, r[SUBLANES:]], axis=0)


def _shift_up(v, k, next8):
    t = v.shape[0]
    r = pltpu.roll(v, t - k, 0)
    nr = pltpu.roll(next8, SUBLANES - k, 0)
    row8 = lax.broadcasted_iota(jnp.int32, next8.shape, 0)
    bot = jnp.where(row8 >= SUBLANES - k, nr, r[t - SUBLANES:t])
    return jnp.concatenate([r[:t - SUBLANES], bot], axis=0)


def _scan_fwd(a, b, h0, stage_a, stage_b):
    t = a.shape[0]
    stage_a[0:SUBLANES, :] = jnp.ones((SUBLANES, a.shape[1]), F32)
    stage_b[0:SUBLANES, :] = jnp.zeros((SUBLANES, a.shape[1]), F32)
    s = 1
    while s < min(t, SUBLANES):
        stage_a[SUBLANES:SUBLANES + t, :] = a
        stage_b[SUBLANES:SUBLANES + t, :] = b
        b = a * stage_b[SUBLANES - s:SUBLANES - s + t, :] + b
        a = a * stage_a[SUBLANES - s:SUBLANES - s + t, :]
        s *= 2
    while s < t:
        b = jnp.concatenate([b[:s], a[s:] * b[:t - s] + b[s:]], axis=0)
        a = jnp.concatenate([a[:s], a[s:] * a[:t - s]], axis=0)
        s *= 2
    return b + a * h0


def _scan_rev(m, b, g_next, stage_m, stage_b):
    t = m.shape[0]
    stage_m[SUBLANES + t:2 * SUBLANES + t, :] = jnp.ones((SUBLANES, m.shape[1]), F32)
    stage_b[SUBLANES + t:2 * SUBLANES + t, :] = jnp.zeros((SUBLANES, m.shape[1]), F32)
    s = 1
    while s < min(t, SUBLANES):
        stage_m[SUBLANES:SUBLANES + t, :] = m
        stage_b[SUBLANES:SUBLANES + t, :] = b
        b = m * stage_b[SUBLANES + s:SUBLANES + s + t, :] + b
        m = m * stage_m[SUBLANES + s:SUBLANES + s + t, :]
        s *= 2
    while s < t:
        b = jnp.concatenate([m[:t - s] * b[s:] + b[:t - s], b[t - s:]], axis=0)
        m = jnp.concatenate([m[:t - s] * m[s:], m[t - s:]], axis=0)
        s *= 2
    return b + m * g_next


def _lru_gates(u, wa, wx, ba, bx, sp):
    ub = u.astype(BF16)
    r = jax.nn.sigmoid(_dot(ub, wa, NN) + ba)
    i = jax.nn.sigmoid(_dot(ub, wx, NN) + bx)
    log_a = (-RG_C * r) * sp
    a = jnp.exp(log_a)
    mult = jnp.sqrt(-_expm1(log_a, a) * (a + 1.0))
    return ub, r, i, a, mult


def _staged_shifts(stage, v, prev8, next8, downs, ups):
    t = v.shape[0]
    if prev8 is not None:
        stage[0:SUBLANES, :] = prev8
    stage[SUBLANES:SUBLANES + t, :] = v
    if next8 is not None:
        stage[SUBLANES + t:2 * SUBLANES + t, :] = next8
    return ([stage[SUBLANES - k:SUBLANES - k + t, :] for k in downs],
            [stage[SUBLANES + k:SUBLANES + k + t, :] for k in ups])


def _conv3(p, pp, w_ref, lo, stage=None):
    if stage is None:
        p1 = _shift_down(p, 1, pp)
        p2 = _shift_down(p, 2, pp)
    else:
        (p1, p2), _ = _staged_shifts(stage, p, pp, None, (1, 2), ())
    q = (w_ref[0:1, lo:lo + LANES] * p2 + w_ref[1:2, lo:lo + LANES] * p1) + w_ref[2:3, lo:lo + LANES] * p
    return q, p1, p2


def _conv4(xv, xp, w_ref, b_ref, lo, stage=None):
    if stage is None:
        x1 = _shift_down(xv, 1, xp)
        x2 = _shift_down(xv, 2, xp)
        x3 = _shift_down(xv, 3, xp)
    else:
        (x1, x2, x3), _ = _staged_shifts(stage, xv, xp, None, (1, 2, 3), ())
    u = (((w_ref[0:1, lo:lo + LANES] * x3 + w_ref[1:2, lo:lo + LANES] * x2) + w_ref[2:3, lo:lo + LANES] * x1)
         + w_ref[3:4, lo:lo + LANES] * xv) + b_ref[:, lo:lo + LANES]
    return u, x1, x2, x3


def _mix_in_fwd(x2d, mod6, g_mix, w_in_t, tm, rider=None):
    s, d = x2d.shape
    din = w_in_t.shape[0]

    def body(x_ref, mod_ref, g_ref, w_ref, hn_ref, proj_ref):
        xhat, _ = _rms(x_ref[...])
        hn = ((xhat * g_ref[...]) * (1.0 + mod_ref[1:2, :]) + mod_ref[0:1, :]).astype(BF16)
        hn_ref[...] = hn
        proj_ref[...] = _dot(hn, w_ref[...], NT)

    return _call(
        body, "mix_in_fwd", (s // tm,),
        [pl.BlockSpec((tm, d), lambda i: (i, 0)), _full(mod6.shape), _full(g_mix.shape), _full(w_in_t.shape)],
        [pl.BlockSpec((tm, d), lambda i: (i, 0)), pl.BlockSpec((tm, din), lambda i: (i, 0))],
        [jax.ShapeDtypeStruct((s, d), BF16), jax.ShapeDtypeStruct((s, din), F32)],
        [x2d, mod6, g_mix, w_in_t], rider=rider)


def _mix_in_mixer_fwd(x2d, mod6, g_mix, w_in_t, conv_sc, conv_lru, conv_b, wa_bd, wx_bd, ba, bx, lam, width, tm):
    s, d = x2d.shape
    din = w_in_t.shape[0]
    nt = s // tm
    sub = min(MIX_ROWS, tm)
    nblk = width // LANES

    def body(x_ref, mod_ref, g_ref, w_ref, wsc_ref, wlru_ref, blru_ref, wa_ref, wx_ref, ba_ref, bx_ref, lam_ref,
             hn_ref, proj_ref, ymix_ref, h_ref, buf_ref, halo_ref, hc_ref, stage_ref):
        i = pl.program_id(0)

        @pl.when(i == 0)
        def _():
            buf_ref[1] = jnp.zeros((tm, din), F32)
            halo_ref[...] = jnp.zeros_like(halo_ref)

        @pl.when(i <= 1)
        def _():
            hc_ref[...] = jnp.zeros_like(hc_ref)

        def step(dst, src):
            xhat, _ = _rms(x_ref[...])
            hn = ((xhat * g_ref[...]) * (1.0 + mod_ref[1:2, :]) + mod_ref[0:1, :]).astype(BF16)
            hn_ref[...] = hn
            n_mix = (tm // sub) * nblk
            n_chunk = din // width

            def project(k):
                res = _dot(hn_ref[...], w_ref[k * width:(k + 1) * width, :], NT)
                proj_ref[:, k * width:(k + 1) * width] = res
                dst[:, k * width:(k + 1) * width] = res

            done = 0
            for half in range(tm // sub):
                r0 = half * sub
                rows = slice(r0, r0 + sub)
                for j in range(nblk):
                    lo = j * LANES
                    while done < n_chunk and done * n_mix <= (half * nblk + j) * n_chunk:
                        project(done)
                        done += 1

                    def col(p):
                        return src[rows, p * width + lo:p * width + lo + LANES]

                    def prev(p):
                        c0 = p * width + lo
                        if half == 0:
                            return halo_ref[:, c0:c0 + LANES]
                        return src[r0 - SUBLANES:r0, c0:c0 + LANES]

                    pp = col(1) * col(2)
                    q, _, _ = _conv3(pp, prev(1) * prev(2), wsc_ref, lo, stage_ref.at[0])
                    ymix_ref[rows, lo:lo + LANES] = (col(0) * q).astype(BF16)

                    u, _, _, _ = _conv4(col(4), prev(4), wlru_ref, blru_ref, lo, stage_ref.at[1])
                    sp = _softplus(-lam_ref[:, lo:lo + LANES])
                    _, r, ig, a, mult = _lru_gates(u, wa_ref[j], wx_ref[j], ba_ref[:, lo:lo + LANES],
                                                   bx_ref[:, lo:lo + LANES], sp)
                    h = _scan_fwd(a, mult * (ig * u), hc_ref[0:1, lo:lo + LANES], stage_ref.at[2], stage_ref.at[3])
                    h_ref[rows, lo:lo + LANES] = h
                    hc_ref[0:1, lo:lo + LANES] = h[sub - 1:sub, :]
                    gel, _ = _gelu(col(3))
                    ymix_ref[rows, width + lo:width + lo + LANES] = (gel * h).astype(BF16)
            while done < n_chunk:
                project(done)
                done += 1
            halo_ref[...] = src[tm - SUBLANES:tm, :]

        @pl.when(i % 2 == 0)
        def _():
            step(buf_ref.at[0], buf_ref.at[1])

        @pl.when(i % 2 == 1)
        def _():
            step(buf_ref.at[1], buf_ref.at[0])

    small = [conv_sc, conv_lru, conv_b, wa_bd, wx_bd, ba, bx, lam]
    cur = lambda i: (jnp.minimum(i, nt - 1), 0)
    last = lambda i: (jnp.maximum(i - 1, 0), 0)
    outs, _ = _call(
        body, "mix_in_mixer_fwd", (nt + 1,),
        [pl.BlockSpec((tm, d), cur), _full(mod6.shape), _full(g_mix.shape), _full(w_in_t.shape)]
        + [_full(a.shape) for a in small],
        [pl.BlockSpec((tm, d), cur), pl.BlockSpec((tm, din), cur),
         pl.BlockSpec((tm, 2 * width), last), pl.BlockSpec((tm, width), last)],
        [jax.ShapeDtypeStruct((s, d), BF16), jax.ShapeDtypeStruct((s, din), F32),
         jax.ShapeDtypeStruct((s, 2 * width), BF16), jax.ShapeDtypeStruct((s, width), F32)],
        [x2d, mod6, g_mix, w_in_t, *small],
        scratch=[pltpu.VMEM((2, tm, din), F32), pltpu.VMEM((SUBLANES, din), F32), pltpu.VMEM((SUBLANES, width), F32),
                 pltpu.VMEM((4, sub + 2 * SUBLANES, LANES), F32)])
    return outs


def _mixer_fwd(proj, conv_sc, conv_lru, conv_b, wa_bd, wx_bd, ba, bx, lam, width, rider=None):
    s, din = proj.shape
    t = min(MIX_ROWS, s)
    nblk = width // LANES
    hb = t // SUBLANES

    def body(proj_ref, projp_ref, wsc_ref, wlru_ref, blru_ref, wa_ref, wx_ref, ba_ref, bx_ref, lam_ref,
             ymix_ref, h_ref, hc_ref, stage_ref):
        i = pl.program_id(0)

        @pl.when(i == 0)
        def _():
            hc_ref[...] = jnp.zeros_like(hc_ref)

        has_prev = i > 0
        for j in range(nblk):
            lo = j * LANES

            def col(p, ref=proj_ref):
                return ref[:, p * width + lo:p * width + lo + LANES]

            def prev(p):
                return jnp.where(has_prev, col(p, projp_ref), 0.0)

            p = col(1) * col(2)
            q, _, _ = _conv3(p, prev(1) * prev(2), wsc_ref, lo, stage_ref.at[0])
            ymix_ref[:, lo:lo + LANES] = (col(0) * q).astype(BF16)

            u, _, _, _ = _conv4(col(4), prev(4), wlru_ref, blru_ref, lo, stage_ref.at[1])
            sp = _softplus(-lam_ref[:, lo:lo + LANES])
            _, r, ig, a, mult = _lru_gates(u, wa_ref[j], wx_ref[j], ba_ref[:, lo:lo + LANES], bx_ref[:, lo:lo + LANES], sp)
            h = _scan_fwd(a, mult * (ig * u), hc_ref[0:1, lo:lo + LANES], stage_ref.at[2], stage_ref.at[3])
            h_ref[:, lo:lo + LANES] = h
            hc_ref[0:1, lo:lo + LANES] = h[t - 1:t, :]
            gel, _ = _gelu(col(3))
            ymix_ref[:, width + lo:width + lo + LANES] = (gel * h).astype(BF16)

    small = [conv_sc, conv_lru, conv_b, wa_bd, wx_bd, ba, bx, lam]
    return _call(
        body, "mixer_fwd", (s // t,),
        [pl.BlockSpec((t, din), lambda i: (i, 0)),
         pl.BlockSpec((SUBLANES, din), lambda i: (jnp.maximum(i * hb - 1, 0), 0))]
        + [_full(a.shape) for a in small],
        [pl.BlockSpec((t, 2 * width), lambda i: (i, 0)), pl.BlockSpec((t, width), lambda i: (i, 0))],
        [jax.ShapeDtypeStruct((s, 2 * width), BF16), jax.ShapeDtypeStruct((s, width), F32)],
        [proj, proj, *small],
        scratch=[pltpu.VMEM((SUBLANES, width), F32), pltpu.VMEM((4, t + 2 * SUBLANES, LANES), F32)], rider=rider)


def _mix_out_fwd(ymix, x2d, w_out, mod6, g_mlp, tm, rider=None):
    s, d = x2d.shape

    def body(y_ref, x_ref, w_ref, mod_ref, g_ref, mix_ref, x2_ref, hn_ref):
        mix = _dot(y_ref[...], w_ref[...], NN)
        mix_ref[...] = mix
        x2 = x_ref[...] + mod_ref[2:3, :] * mix
        x2_ref[...] = x2
        xhat, _ = _rms(x2)
        hn_ref[...] = ((xhat * g_ref[...]) * (1.0 + mod_ref[4:5, :]) + mod_ref[3:4, :]).astype(BF16)

    tile = pl.BlockSpec((tm, d), lambda i: (i, 0))
    return _call(
        body, "mix_out_fwd", (s // tm,),
        [tile, tile, _full(w_out.shape), _full(mod6.shape), _full(g_mlp.shape)],
        [tile, tile, tile],
        [jax.ShapeDtypeStruct((s, d), F32), jax.ShapeDtypeStruct((s, d), F32), jax.ShapeDtypeStruct((s, d), BF16)],
        [ymix, x2d, w_out, mod6, g_mlp], rider=rider)


def _mlp_fwd_loss(hn2, w_up_t, w_down, x2, target, mod6, g_final, tm, tk):
    s, d = hn2.shape
    f = w_up_t.shape[0]
    nk = f // tk

    def body(hn_ref, wu_ref, wd_ref, x2_ref, t_ref, mod_ref, g_ref, z_ref, dx3_ref, dyb_ref, st_ref, y_ref):
        i, k = pl.program_id(0), pl.program_id(1)

        @pl.when(jnp.logical_and(i == 0, k == 0))
        def _():
            st_ref[...] = jnp.zeros_like(st_ref)

        z = jnp.maximum(_dot(hn_ref[...], wu_ref[...], NT), 0.0)
        z_ref[...] = z.astype(BF16)
        part = _dot((z * z).astype(BF16), wd_ref[...], NN)

        @pl.when(k == 0)
        def _():
            y_ref[...] = part

        @pl.when(k > 0)
        def _():
            y_ref[...] += part

        @pl.when(k == nk - 1)
        def _():
            gate = mod_ref[5:6, :]
            yv = y_ref[...]
            xhat, rstd = _rms(x2_ref[...] + gate * yv)
            diff = xhat * g_ref[...] - t_ref[...]
            dyo = diff * (1.0 / d)
            dx3 = _rms_bwd(dyo * g_ref[...], xhat, rstd)
            dx3_ref[...] = dx3
            dyb_ref[...] = (gate * dx3).astype(BF16)
            st_ref[0:1, :] += _colsum(dyo * xhat)
            st_ref[1:2, :] += _colsum(dx3 * yv)
            st_ref[2:3, :] += _colsum(diff * diff)

    tile = pl.BlockSpec((tm, d), lambda i, k: (i, 0))
    wblk = pl.BlockSpec((tk, d), lambda i, k: (k, 0))
    return pl.pallas_call(
        body, name="mlp_fwd_loss", grid=(s // tm, nk),
        in_specs=[tile, wblk, wblk, tile, tile, _full(mod6.shape), _full(g_final.shape)],
        out_specs=[pl.BlockSpec((tm, tk), lambda i, k: (i, k)), tile, tile, _full((SUBLANES, d))],
        out_shape=[jax.ShapeDtypeStruct((s, f), BF16), jax.ShapeDtypeStruct((s, d), F32),
                   jax.ShapeDtypeStruct((s, d), BF16), jax.ShapeDtypeStruct((SUBLANES, d), F32)],
        scratch_shapes=[pltpu.VMEM((tm, d), F32)],
        compiler_params=_params(("arbitrary", "arbitrary")),
    )(hn2, w_up_t, w_down, x2, target, mod6, g_final)


def _mlp_bwd_dx(dyb, z, w_down, w_up_t, tm, tk):
    s, d = dyb.shape
    f = z.shape[1]

    def body(dy_ref, z_ref, wd_ref, wu_ref, dz_ref, dh_ref):
        k = pl.program_id(1)
        dz = ((2.0 * z_ref[...].astype(F32)) * _dot(dy_ref[...], wd_ref[...], NT)).astype(BF16)
        dz_ref[...] = dz
        part = _dot(dz, wu_ref[...], NN)

        @pl.when(k == 0)
        def _():
            dh_ref[...] = part

        @pl.when(k > 0)
        def _():
            dh_ref[...] += part

    return pl.pallas_call(
        body, name="mlp_bwd_dx", grid=(s // tm, f // tk),
        in_specs=[pl.BlockSpec((tm, d), lambda i, k: (i, 0)), pl.BlockSpec((tm, tk), lambda i, k: (i, k)),
                  pl.BlockSpec((tk, d), lambda i, k: (k, 0)), pl.BlockSpec((tk, d), lambda i, k: (k, 0))],
        out_specs=[pl.BlockSpec((tm, tk), lambda i, k: (i, k)), pl.BlockSpec((tm, d), lambda i, k: (i, 0))],
        out_shape=[jax.ShapeDtypeStruct((s, f), BF16), jax.ShapeDtypeStruct((s, d), F32)],
        compiler_params=_params(("parallel", "arbitrary")),
    )(dyb, z, w_down, w_up_t)


def _mlp_bwd_dw(z, dz, dyb, hn2, tm, tk):
    s, d = dyb.shape
    f = z.shape[1]

    def body(z_ref, dz_ref, dy_ref, hn_ref, gd_ref, gu_ref):
        i = pl.program_id(1)

        @pl.when(i == 0)
        def _():
            gd_ref[...] = jnp.zeros_like(gd_ref)
            gu_ref[...] = jnp.zeros_like(gu_ref)

        zf = z_ref[...].astype(F32)
        gd_ref[...] += _dot((zf * zf).astype(BF16), dy_ref[...], TN)
        gu_ref[...] += _dot(dz_ref[...], hn_ref[...], TN)

    return pl.pallas_call(
        body, name="mlp_bwd_dw", grid=(f // tk, s // tm),
        in_specs=[pl.BlockSpec((tm, tk), lambda k, i: (i, k)), pl.BlockSpec((tm, tk), lambda k, i: (i, k)),
                  pl.BlockSpec((tm, d), lambda k, i: (i, 0)), pl.BlockSpec((tm, d), lambda k, i: (i, 0))],
        out_specs=[pl.BlockSpec((tk, d), lambda k, i: (k, 0)), pl.BlockSpec((tk, d), lambda k, i: (k, 0))],
        out_shape=[jax.ShapeDtypeStruct((f, d), F32), jax.ShapeDtypeStruct((f, d), F32)],
        compiler_params=_params(("parallel", "arbitrary")),
    )(z, dz, dyb, hn2)


def _mix_out_bwd(dhn2, x2, dx3, mix, ymix, w_out, mod6, g_mlp, tm, rider=None):
    s, d = x2.shape

    def body(dh_ref, x2_ref, dx3_ref, mix_ref, y_ref, w_ref, mod_ref, g_ref, dx2_ref, dym_ref, gw_ref, st_ref):
        i = pl.program_id(0)

        @pl.when(i == 0)
        def _():
            st_ref[...] = jnp.zeros_like(st_ref)
            gw_ref[...] = jnp.zeros_like(gw_ref)

        dh = dh_ref[...]
        xhat, rstd = _rms(x2_ref[...])
        dn = dh * (1.0 + mod_ref[4:5, :])
        dx2 = dx3_ref[...] + _rms_bwd(dn * g_ref[...], xhat, rstd)
        dx2_ref[...] = dx2
        st_ref[0:1, :] += _colsum(dh)
        st_ref[1:2, :] += _colsum(dh * (xhat * g_ref[...]))
        st_ref[2:3, :] += _colsum(dn * xhat)
        st_ref[3:4, :] += _colsum(dx2 * mix_ref[...])
        dmix = (mod_ref[2:3, :] * dx2).astype(BF16)
        dym_ref[...] = _dot(dmix, w_ref[...], NT)
        gw_ref[...] += _dot(y_ref[...], dmix, TN)

    tile = pl.BlockSpec((tm, d), lambda i: (i, 0))
    return _call(
        body, "mix_out_bwd", (s // tm,),
        [tile, tile, tile, tile, tile, _full(w_out.shape), _full(mod6.shape), _full(g_mlp.shape)],
        [tile, tile, _full((d, d)), _full((SUBLANES, d))],
        [jax.ShapeDtypeStruct((s, d), F32), jax.ShapeDtypeStruct((s, d), F32),
         jax.ShapeDtypeStruct((d, d), F32), jax.ShapeDtypeStruct((SUBLANES, d), F32)],
        [dhn2, x2, dx3, mix, ymix, w_out, mod6, g_mlp], rider=rider)


def _mixer_bwd(proj, dymix, h_all, conv_sc, conv_lru, conv_b, wa_bd, wx_bd, ba, bx, lam, width, rider=None):
    s, din = proj.shape
    t = min(MIX_ROWS, s)
    nt = s // t
    nblk = width // LANES
    hb = t // SUBLANES
    last8 = s // SUBLANES - 1

    def body(proj_ref, projp_ref, projn_ref, dy_ref, dyn_ref, h_ref, hp_ref,
             wsc_ref, wlru_ref, blru_ref, wa_ref, wx_ref, ba_ref, bx_ref, lam_ref,
             dproj_ref, small_ref, gwa_ref, gwx_ref, an_ref, gn_ref, dun_ref, stage_ref):
        i = pl.program_id(0)

        @pl.when(i == 0)
        def _():
            small_ref[...] = jnp.zeros_like(small_ref)
            gwa_ref[...] = jnp.zeros_like(gwa_ref)
            gwx_ref[...] = jnp.zeros_like(gwx_ref)
            an_ref[...] = jnp.zeros_like(an_ref)
            gn_ref[...] = jnp.zeros_like(gn_ref)
            dun_ref[...] = jnp.zeros_like(dun_ref)

        has_prev = i < nt - 1
        has_next = i > 0
        for j in range(nblk):
            lo = j * LANES
            ls = slice(lo, lo + LANES)

            def col(p, ref=proj_ref):
                return ref[:, p * width + lo:p * width + lo + LANES]

            def prev(p):
                return jnp.where(has_prev, col(p, projp_ref), 0.0)

            def nxt(p):
                return jnp.where(has_next, col(p, projn_ref), 0.0)

            def add_row(r, v):
                small_ref[r:r + 1, ls] += _colsum(v)

            sc_b, sc_c, sc_x = col(0), col(1), col(2)
            p = sc_c * sc_x
            q, p1, p2 = _conv3(p, prev(1) * prev(2), wsc_ref, lo, stage_ref.at[0])
            dys = dy_ref[:, ls]
            dproj_ref[:, ls] = (dys * q).astype(BF16)
            dq = dys * sc_b
            dqn = jnp.where(has_next, dyn_ref[:, ls], 0.0) * nxt(0)
            _, (dq1, dq2) = _staged_shifts(stage_ref.at[1], dq, None, dqn, (), (1, 2))
            dp = (wsc_ref[2:3, ls] * dq + wsc_ref[1:2, ls] * dq1) + wsc_ref[0:1, ls] * dq2
            dproj_ref[:, width + lo:width + lo + LANES] = (dp * sc_x).astype(BF16)
            dproj_ref[:, 2 * width + lo:2 * width + lo + LANES] = (dp * sc_c).astype(BF16)
            add_row(0, dq * p2)
            add_row(1, dq * p1)
            add_row(2, dq * p)

            xv = col(4)
            u, x1, x2, x3 = _conv4(xv, prev(4), wlru_ref, blru_ref, lo, stage_ref.at[2])
            lam_v = lam_ref[:, ls]
            sp = _softplus(-lam_v)
            wa, wx = wa_ref[j], wx_ref[j]
            ub, r, ig, a, mult = _lru_gates(u, wa, wx, ba_ref[:, ls], bx_ref[:, ls], sp)
            iu = ig * u
            h = h_ref[:, ls]
            (hm1,), _ = _staged_shifts(stage_ref.at[3], h, jnp.where(has_prev, hp_ref[:, ls], 0.0), None, (1,), ())
            lyv = col(3)
            gel, th = _gelu(lyv)
            dyl = dy_ref[:, width + lo:width + lo + LANES]
            dproj_ref[:, 3 * width + lo:3 * width + lo + LANES] = (dyl * h * _dgelu(lyv, th)).astype(BF16)
            a_next = jnp.broadcast_to(an_ref[0:1, ls], (SUBLANES, LANES))
            _, (a_up,) = _staged_shifts(stage_ref.at[4], a, None, a_next, (), (1,))
            g = _scan_rev(a_up, dyl * gel, gn_ref[0:1, ls], stage_ref.at[5], stage_ref.at[6])
            an_ref[0:1, ls] = a[0:1, :]
            gn_ref[0:1, ls] = g[0:1, :]
            da = g * hm1
            dmult = g * iu
            diu = g * mult
            dlog_a = da * a - dmult * ((a * a) / mult)
            dpre_a = (dlog_a * (-RG_C * sp)) * (r * (1.0 - r))
            dpre_x = (diu * u) * (ig * (1.0 - ig))
            dab, dxb = dpre_a.astype(BF16), dpre_x.astype(BF16)
            du = diu * ig + _dot(dab, wa, NT) + _dot(dxb, wx, NT)
            gwa_ref[j] += _dot(ub, dab, TN)
            gwx_ref[j] += _dot(ub, dxb, TN)
            dun = dun_ref[:, ls]
            dun_ref[:, ls] = du[0:SUBLANES, :]
            _, (du1, du2, du3) = _staged_shifts(stage_ref.at[7], du, None, dun, (), (1, 2, 3))
            dlx = (((wlru_ref[3:4, ls] * du + wlru_ref[2:3, ls] * du1) + wlru_ref[1:2, ls] * du2)
                   + wlru_ref[0:1, ls] * du3)
            dproj_ref[:, 4 * width + lo:4 * width + lo + LANES] = dlx.astype(BF16)
            add_row(3, du * x3)
            add_row(4, du * x2)
            add_row(5, du * x1)
            add_row(6, du * xv)
            add_row(7, du)
            add_row(8, dpre_a)
            add_row(9, dpre_x)
            add_row(10, (dlog_a * (RG_C * r)) * jax.nn.sigmoid(-lam_v))

    small = [conv_sc, conv_lru, conv_b, wa_bd, wx_bd, ba, bx, lam]
    rev = lambda i: nt - 1 - i
    return _call(
        body, "mixer_bwd", (nt,),
        [pl.BlockSpec((t, din), lambda i: (rev(i), 0)),
         pl.BlockSpec((SUBLANES, din), lambda i: (jnp.maximum(rev(i) * hb - 1, 0), 0)),
         pl.BlockSpec((SUBLANES, din), lambda i: (jnp.minimum((rev(i) + 1) * hb, last8), 0)),
         pl.BlockSpec((t, 2 * width), lambda i: (rev(i), 0)),
         pl.BlockSpec((SUBLANES, 2 * width), lambda i: (jnp.minimum((rev(i) + 1) * hb, last8), 0)),
         pl.BlockSpec((t, width), lambda i: (rev(i), 0)),
         pl.BlockSpec((SUBLANES, width), lambda i: (jnp.maximum(rev(i) * hb - 1, 0), 0))]
        + [_full(a.shape) for a in small],
        [pl.BlockSpec((t, din), lambda i: (rev(i), 0)), _full((2 * SUBLANES, width)),
         _full(wa_bd.shape), _full(wx_bd.shape)],
        [jax.ShapeDtypeStruct((s, din), BF16), jax.ShapeDtypeStruct((2 * SUBLANES, width), F32),
         jax.ShapeDtypeStruct(wa_bd.shape, F32), jax.ShapeDtypeStruct(wx_bd.shape, F32)],
        [proj, proj, proj, dymix, dymix, h_all, h_all, *small],
        scratch=[pltpu.VMEM((SUBLANES, width), F32), pltpu.VMEM((SUBLANES, width), F32),
                 pltpu.VMEM((SUBLANES, width), F32), pltpu.VMEM((8, t + 2 * SUBLANES, LANES), F32)], rider=rider)


def _mix_in_bwd_dx(dproj, x2d, dx2, w_in_t, mod6, g_mix, tm, rider=None):
    s, d = x2d.shape
    din = dproj.shape[1]

    def body(dp_ref, x_ref, dx2_ref, w_ref, mod_ref, g_ref, gx_ref, st_ref):
        i = pl.program_id(0)

        @pl.when(i == 0)
        def _():
            st_ref[...] = jnp.zeros_like(st_ref)

        dh = _dot(dp_ref[...], w_ref[...], NN)
        xhat, rstd = _rms(x_ref[...])
        dn = dh * (1.0 + mod_ref[1:2, :])
        gx_ref[...] = dx2_ref[...] + _rms_bwd(dn * g_ref[...], xhat, rstd)
        st_ref[0:1, :] += _colsum(dh)
        st_ref[1:2, :] += _colsum(dh * (xhat * g_ref[...]))
        st_ref[2:3, :] += _colsum(dn * xhat)

    tile = pl.BlockSpec((tm, d), lambda i: (i, 0))
    return _call(
        body, "mix_in_bwd_dx", (s // tm,),
        [pl.BlockSpec((tm, din), lambda i: (i, 0)), tile, tile, _full(w_in_t.shape), _full(mod6.shape),
         _full(g_mix.shape)],
        [tile, _full((SUBLANES, d))],
        [jax.ShapeDtypeStruct((s, d), F32), jax.ShapeDtypeStruct((SUBLANES, d), F32)],
        [dproj, x2d, dx2, w_in_t, mod6, g_mix], rider=rider)


def _mix_in_bwd_dw(dproj, hn1, tm, tn, rider=None):
    s, d = hn1.shape
    din = dproj.shape[1]

    def body(dp_ref, hn_ref, gw_ref):
        i = pl.program_id(1)

        @pl.when(i == 0)
        def _():
            gw_ref[...] = jnp.zeros_like(gw_ref)

        gw_ref[...] += _dot(dp_ref[...], hn_ref[...], TN)

    return _call(
        body, "mix_in_bwd_dw", (din // tn, s // tm),
        [pl.BlockSpec((tm, tn), lambda p, i: (i, p)), pl.BlockSpec((tm, d), lambda p, i: (i, 0))],
        [pl.BlockSpec((tn, d), lambda p, i: (p, 0))],
        [jax.ShapeDtypeStruct((din, d), F32)],
        [dproj, hn1], rider=rider)


def _adamw(w, g, m, v):
    m = ADAM_B1 * m + (1.0 - ADAM_B1) * g
    v = ADAM_B2 * v + (1.0 - ADAM_B2) * (g * g)
    m_hat = m / (1.0 - ADAM_B1 ** ADAM_STEP)
    v_hat = v / (1.0 - ADAM_B2 ** ADAM_STEP)
    delta = -ADAM_LR * (m_hat / (jnp.sqrt(v_hat) + ADAM_EPS) + ADAM_WD * w)
    return delta, m, v


def _pair_sum(g4, h4, core_chip, tr, name):
    _, _, r, n = g4.shape

    def body(sc_ref, g_ref, h_ref, sb_ref, own_ref):
        q = pl.program_id(1)
        ssum = g_ref[...] + h_ref[...]
        sb_ref[...] = ssum.astype(BF16)

        @pl.when(q == sc_ref[1])
        def _():
            own_ref[...] = ssum

    grid_spec = pltpu.PrefetchScalarGridSpec(
        num_scalar_prefetch=1, grid=(r // tr, 4),
        in_specs=[pl.BlockSpec((None, None, tr, n), lambda i, q, sc: (q, sc[0], i, 0)),
                  pl.BlockSpec((None, tr, n), lambda i, q, sc: (q, i, 0))],
        out_specs=[pl.BlockSpec((None, tr, n), lambda i, q, sc: (q, i, 0)),
                   pl.BlockSpec((tr, n), lambda i, q, sc: (i, 0))])
    return pl.pallas_call(
        body, name=name, grid_spec=grid_spec,
        out_shape=[jax.ShapeDtypeStruct((4, r, n), BF16), jax.ShapeDtypeStruct((r, n), F32)],
        compiler_params=_params(("parallel", "arbitrary")),
    )(core_chip, g4, h4)


def _sum4(own, parts, tr, name):
    r, n = own.shape

    def body(o_ref, p_ref, out_ref):
        acc = o_ref[...]
        for k in range(3):
            acc = acc + p_ref[k].astype(F32)
        out_ref[...] = acc

    return pl.pallas_call(
        body, name=name, grid=(r // tr,),
        in_specs=[pl.BlockSpec((tr, n), lambda i: (i, 0)), pl.BlockSpec((3, tr, n), lambda i: (0, i, 0))],
        out_specs=pl.BlockSpec((tr, n), lambda i: (i, 0)),
        out_shape=jax.ShapeDtypeStruct((r, n), F32),
        compiler_params=_params(("parallel",)),
    )(own, parts)


def _sum4_adam(own, parts, w, m, v, tr, name, transposed):
    r, n = own.shape
    rows, cols = w.shape

    def body(o_ref, p_ref, w_ref, m_ref, v_ref, g_ref, d_ref, nm_ref, nv_ref):
        g = o_ref[...]
        for k in range(3):
            g = g + p_ref[k].astype(F32)
        if transposed:
            g = g.T
        g_ref[...] = g
        d_ref[...], nm_ref[...], nv_ref[...] = _adamw(w_ref[...], g, m_ref[...], v_ref[...])

    if transposed:
        g_specs = [pl.BlockSpec((r, tr), lambda i: (0, i)), pl.BlockSpec((3, r, tr), lambda i: (0, 0, i))]
    else:
        g_specs = [pl.BlockSpec((tr, n), lambda i: (i, 0)), pl.BlockSpec((3, tr, n), lambda i: (0, i, 0))]
    tile = pl.BlockSpec((tr, cols), lambda i: (i, 0))
    return pl.pallas_call(
        body, name=name, grid=(rows // tr,),
        in_specs=g_specs + [tile] * 3, out_specs=[tile] * 4,
        out_shape=[jax.ShapeDtypeStruct((rows, cols), F32)] * 4,
        compiler_params=_params(("parallel",)),
    )(own, parts, w, m, v)


def _sum8(parts, tr, name):
    _, rows, n = parts.shape

    def body(p_ref, o_ref):
        acc = p_ref[0]
        for k in range(1, N_DEV):
            acc = acc + p_ref[k]
        o_ref[...] = acc

    return pl.pallas_call(
        body, name=name, grid=(rows // tr,),
        in_specs=[pl.BlockSpec((N_DEV, tr, n), lambda i: (0, i, 0))],
        out_specs=pl.BlockSpec((tr, n), lambda i: (i, 0)),
        out_shape=jax.ShapeDtypeStruct((rows, n), F32),
        compiler_params=_params(("parallel",)),
    )(parts)


def _adam_rows(w, g, m, v, tr, name):
    rows, n = w.shape

    def body(w_ref, g_ref, m_ref, v_ref, d_ref, nm_ref, nv_ref):
        d_ref[...], nm_ref[...], nv_ref[...] = _adamw(w_ref[...], g_ref[...], m_ref[...], v_ref[...])

    tile = pl.BlockSpec((tr, n), lambda i: (i, 0))
    return pl.pallas_call(
        body, name=name, grid=(rows // tr,),
        in_specs=[tile] * 4, out_specs=[tile] * 3,
        out_shape=[jax.ShapeDtypeStruct((rows, n), F32)] * 3,
        compiler_params=_params(("parallel",)),
    )(w, g, m, v)


def _ada_bwd_adam(cact_t, dmod_cols, w, m, v, tr):
    rows, n = w.shape

    def body(c_ref, d_ref, w_ref, m_ref, v_ref, g_ref, dl_ref, nm_ref, nv_ref):
        def term(b):
            return c_ref[b].astype(BF16).astype(F32) * d_ref[b:b + 1, :].astype(BF16).astype(F32)

        g = term(0)
        for b in range(1, N_DEV):
            g = g + term(b)
        g_ref[...] = g
        dl_ref[...], nm_ref[...], nv_ref[...] = _adamw(w_ref[...], g, m_ref[...], v_ref[...])

    tile = pl.BlockSpec((tr, n), lambda i: (i, 0))
    return pl.pallas_call(
        body, name="ada_bwd_adam", grid=(rows // tr,),
        in_specs=[pl.BlockSpec((N_DEV, tr, 1), lambda i: (0, i, 0)), _full(dmod_cols.shape), tile, tile, tile],
        out_specs=[tile] * 4,
        out_shape=[jax.ShapeDtypeStruct((rows, n), F32)] * 4,
        compiler_params=_params(("parallel",)),
    )(cact_t, dmod_cols, w, m, v)


def _adam_small(ws, gs, ms, vs):
    n = len(ws)

    def body(*refs):
        w_r, g_r, m_r, v_r = refs[:n], refs[n:2 * n], refs[2 * n:3 * n], refs[3 * n:4 * n]
        d_r, nm_r, nv_r = refs[4 * n:5 * n], refs[5 * n:6 * n], refs[6 * n:7 * n]
        for k in range(n):
            d_r[k][...], nm_r[k][...], nv_r[k][...] = _adamw(w_r[k][...], g_r[k][...], m_r[k][...], v_r[k][...])

    shapes = [jax.ShapeDtypeStruct(w.shape, F32) for w in ws]
    outs = pl.pallas_call(
        body, name="adam_small", out_shape=shapes * 3, compiler_params=_params(),
    )(*ws, *gs, *ms, *vs)
    return outs[:n], outs[n:2 * n], outs[2 * n:]


def _block_diag(w):
    h, hd, _ = w.shape
    per = LANES // hd
    eye = jnp.eye(per, dtype=w.dtype)
    w5 = w.reshape(h // per, per, hd, 1, hd) * eye[None, :, None, :, None]
    return w5.reshape(h // per, LANES, LANES)


def _block_diag_grad(g, h, hd):
    per = LANES // hd
    g5 = g.reshape(h // per, per, hd, per, hd)
    return jnp.stack([g5[:, a, :, a, :] for a in range(per)], axis=1).reshape(h, hd, hd)


def kernel(x, c, w_ada, b_ada, g_mix, w_in, conv_w_sc, conv_w_lru, conv_b_lru, w_rg_a, b_rg_a, w_rg_x, b_rg_x, lru_lambda, w_out, g_mlp, w_up, w_down, g_final, loss_target, m_w_ada, m_b_ada, m_g_mix, m_w_in, m_conv_w_sc, m_conv_w_lru, m_conv_b_lru, m_w_rg_a, m_b_rg_a, m_w_rg_x, m_b_rg_x, m_lru_lambda, m_w_out, m_g_mlp, m_w_up, m_w_down, m_g_final, v_w_ada, v_b_ada, v_g_mix, v_w_in, v_conv_w_sc, v_conv_w_lru, v_conv_b_lru, v_w_rg_a, v_b_rg_a, v_w_rg_x, v_b_rg_x, v_lru_lambda, v_w_out, v_g_mlp, v_w_up, v_w_down, v_g_final):
    s, d = x.shape[1], x.shape[2]
    width = conv_b_lru.shape[1]
    heads, hd = w_rg_a.shape[1], w_rg_a.shape[2]
    f = w_down.shape[1] * N_DEV
    n_ada = w_ada.shape[2]
    csh = conv_w_sc.shape[2]
    me = 4 * lax.axis_index("x") + 2 * lax.axis_index("y") + lax.axis_index("c")
    tm = min(512, s)
    tm_mlp = min(1024, s)
    tk = 512

    x2d = x[0]
    tgt = loss_target[0]

    pay = jnp.zeros((SUBLANES, d), F32)
    pay = pay.at[0:1, :].set(c)
    pay = pay.at[1:4, 0:csh].set(conv_w_sc[0])
    pay = pay.at[4:8, 0:csh].set(conv_w_lru[0])
    w_in_t_sh = w_in[0].T.astype(BF16)
    w_up_t_sh = w_up[0].T.astype(BF16)
    w_out_sh = w_out[0].astype(BF16)
    w_down_sh = w_down[0].astype(BF16)
    pay_all, w_in_t = _gather2("gather_in", [pay, w_in_t_sh])
    w_in_t = w_in_t.reshape(-1, d)
    c_all = pay_all[:, 0, :]
    conv_sc = pay_all[:, 1:4, 0:csh].transpose(1, 0, 2).reshape(3, width)
    conv_lru = pay_all[:, 4:8, 0:csh].transpose(1, 0, 2).reshape(4, width)

    b_ada_sh = lax.dynamic_slice(b_ada, (0, me * n_ada), (1, n_ada))
    mod_cols, c_act = _ada_fwd(c_all, w_ada[0], b_ada_sh)
    (mod_rows,) = _exchange("scatter_mod", [], [mod_cols.reshape(N_DEV, 1, n_ada)])
    mod_rows, w_out_sh, w_up_t_sh, w_down_sh = lax.optimization_barrier((mod_rows, w_out_sh, w_up_t_sh, w_down_sh))
    (w_out_g,) = _seq_gather2("gather_w_out", 1, [w_out_sh])
    w_up_g, w_down_g = _seq_gather2("gather_mlp_weights", 2, [w_up_t_sh, w_down_sh])
    mod6 = jnp.zeros((SUBLANES, d), F32).at[0:6, :].set(mod_rows.reshape(6, d))

    wa_bd = _block_diag(w_rg_a[0]).astype(BF16)
    wx_bd = _block_diag(w_rg_x[0]).astype(BF16)
    ba = b_rg_a.reshape(1, width)
    bx = b_rg_x.reshape(1, width)
    g_fin = g_final.reshape(1, d)

    hn1, proj, ymix, h_all = _mix_in_mixer_fwd(x2d, mod6, g_mix, w_in_t, conv_sc, conv_lru, conv_b_lru,
                                               wa_bd, wx_bd, ba, bx, lru_lambda, width, tm)
    w_out_b = w_out_g.reshape(-1, d)
    (mix, x2, hn2), _ = _mix_out_fwd(ymix, x2d, w_out_b, mod6, g_mlp, tm)
    w_up_t = w_up_g.reshape(-1, d)
    w_down_b = w_down_g.reshape(-1, d)
    z, dx3, dyb, st_fin = _mlp_fwd_loss(hn2, w_up_t, w_down_b, x2, tgt, mod6, g_fin, tm, 4 * tk)

    core_chip = jnp.stack([lax.axis_index("c"), 2 * lax.axis_index("x") + lax.axis_index("y")]).astype(jnp.int32)
    dz, dhn2 = _mlp_bwd_dx(dyb, z, w_down_b, w_up_t, tm_mlp, 2 * tk)
    g_down, g_up_t = _mlp_bwd_dw(z, dz, dyb, hn2, tm_mlp, 2 * tk)
    g_up4, g_down4 = g_up_t.reshape(4, 2, -1, d), g_down.reshape(4, 2, -1, d)
    h_up, h_down = _seq_pair_swap("swap_mlp_grads", 7, [g_up4, g_down4])
    (dx2, dymix, g_out, st_out), _ = _mix_out_bwd(dhn2, x2, dx3, mix, ymix, w_out_b, mod6, g_mlp, tm)
    h_up, h_down, g_out = lax.optimization_barrier((h_up, h_down, g_out))
    sb_up, own_up = _pair_sum(g_up4, h_up, core_chip, 256, "pair_sum_w_up")
    sb_down, own_down = _pair_sum(g_down4, h_down, core_chip, 256, "pair_sum_w_down")
    g_out4 = g_out.reshape(4, 2, -1, d)
    (h_out,) = _seq_pair_swap("swap_w_out_grad", 8, [g_out4])
    p_up, p_down = _seq_chip_exchange("exchange_mlp_grads", 3, [sb_up, sb_down])
    (dproj, g_small, g_wa, g_wx), _ = _mixer_bwd(
        proj, dymix, h_all, conv_sc, conv_lru, conv_b_lru, wa_bd, wx_bd, ba, bx, lru_lambda, width)
    h_out, dproj = lax.optimization_barrier((h_out, dproj))
    sb_out, own_out = _pair_sum(g_out4, h_out, core_chip, g_out4.shape[2], "pair_sum_w_out")
    (p_out,) = _seq_chip_exchange("exchange_w_out_grad", 4, [sb_out])
    (grad_x, st_in), _ = _mix_in_bwd_dx(dproj, x2d, dx2, w_in_t, mod6, g_mix, tm)

    small = jnp.concatenate([
        st_in[0:2], st_out[3:4], st_out[0:2], st_fin[1:2],
        st_in[2:3], st_out[2:3], st_fin[0:1],
        jnp.concatenate([g_small[7:8], g_small[10:11]], axis=1),
        jnp.concatenate([g_small[8:9], g_small[9:10]], axis=1),
        jnp.concatenate([jnp.concatenate([g_small[0:3], jnp.zeros((1, width), F32)], axis=0), g_small[3:7]], axis=1),
        st_fin[2:3],
        _block_diag_grad(g_wa, heads, hd).reshape(-1, d),
        _block_diag_grad(g_wx, heads, hd).reshape(-1, d),
    ], axis=0)

    (small_all,) = _seq_gather2("gather_small_grads", 5, [small])
    (g_in_t,), _ = _mix_in_bwd_dw(dproj, hn1, min(2048, s), dproj.shape[1] // 2)
    g_in4 = g_in_t.reshape(4, 2, -1, d)
    (h_in,) = _seq_pair_swap("swap_w_in_grad", 9, [g_in4])
    p_up, p_down, p_out, small_all, g_in_t = lax.optimization_barrier((p_up, p_down, p_out, small_all, g_in_t))

    ad_up = _sum4_adam(own_up, p_up, w_up[0], m_w_up[0], v_w_up[0], 256, "adam_w_up", True)
    h_in, ad_up = lax.optimization_barrier((h_in, ad_up))
    sb_in, own_in = _pair_sum(g_in4, h_in, core_chip, g_in4.shape[2], "pair_sum_w_in")
    (p_in,) = _seq_chip_exchange("exchange_w_in_grad", 6, [sb_in])
    ad_out = _sum4_adam(own_out, p_out, w_out[0], m_w_out[0], v_w_out[0], w_out.shape[1], "adam_w_out", False)
    ad_down = _sum4_adam(own_down, p_down, w_down[0], m_w_down[0], v_w_down[0], 256, "adam_w_down", False)

    gsum = _sum8(small_all, SMALL_ROWS, "sum_small")
    loss = (0.5 / d) * jnp.sum(gsum[15])
    dmod_cols = lax.dynamic_slice(small_all[:, 0:6, :].reshape(N_DEV, 6 * d), (0, me * n_ada), (N_DEV, n_ada))
    g_ada, d_ada, nm_ada, nv_ada = _ada_bwd_adam(c_act[:, :, None], dmod_cols, w_ada[0], m_w_ada[0], v_w_ada[0], 256)

    g_conv = lax.dynamic_slice(gsum[11:15, 0:width], (0, me * csh), (4, csh))
    g_conv_l = lax.dynamic_slice(gsum[11:15, width:2 * width], (0, me * csh), (4, csh))
    small_g = [
        gsum[0:6].reshape(1, 6 * d),
        gsum[6:7],
        g_conv[0:3].reshape(1, 3, csh),
        g_conv_l.reshape(1, 4, csh),
        gsum[9:10, 0:width],
        gsum[16:48].reshape(1, heads, hd, hd),
        gsum[10:11, 0:width].reshape(1, heads, hd),
        gsum[48:80].reshape(1, heads, hd, hd),
        gsum[10:11, width:].reshape(1, heads, hd),
        gsum[9:10, width:],
        gsum[7:8],
        gsum[8],
    ]
    small_w = [b_ada, g_mix, conv_w_sc, conv_w_lru, conv_b_lru, w_rg_a, b_rg_a, w_rg_x, b_rg_x, lru_lambda, g_mlp, g_final]
    small_m = [m_b_ada, m_g_mix, m_conv_w_sc, m_conv_w_lru, m_conv_b_lru, m_w_rg_a, m_b_rg_a, m_w_rg_x, m_b_rg_x,
               m_lru_lambda, m_g_mlp, m_g_final]
    small_v = [v_b_ada, v_g_mix, v_conv_w_sc, v_conv_w_lru, v_conv_b_lru, v_w_rg_a, v_b_rg_a, v_w_rg_x, v_b_rg_x,
               v_lru_lambda, v_g_mlp, v_g_final]
    sd, snm, snv = _adam_small(small_w, small_g, small_m, small_v)
    p_in, ad_out, ad_down, (g_ada, d_ada, nm_ada, nv_ada), sd = lax.optimization_barrier(
        (p_in, ad_out, ad_down, (g_ada, d_ada, nm_ada, nv_ada), sd))
    ad_in = _sum4_adam(own_in, p_in, w_in[0].T, m_w_in[0].T, v_w_in[0].T, own_in.shape[0], "adam_w_in", False)
    ad_in = [a.T for a in ad_in]

    def order(ada, w_in_, w_out_, w_up_, w_down_, sm):
        return [ada[None], sm[0], sm[1], w_in_[None], sm[2], sm[3], sm[4], sm[5], sm[6], sm[7], sm[8], sm[9],
                w_out_[None], sm[10], w_up_[None], w_down_[None], sm[11]]

    grads = order(g_ada, ad_in[0], ad_out[0], ad_up[0], ad_down[0], small_g)
    deltas = order(d_ada, ad_in[1], ad_out[1], ad_up[1], ad_down[1], sd)
    new_m = order(nm_ada, ad_in[2], ad_out[2], ad_up[2], ad_down[2], snm)
    new_v = order(nv_ada, ad_in[3], ad_out[3], ad_up[3], ad_down[3], snv)
    return (loss, grad_x[None], *grads, *deltas, *new_m, *new_v)
```

```python
import functools

import jax
import jax.numpy as jnp
from jax import lax
from jax.experimental import pallas as pl
from jax.experimental.pallas import tpu as pltpu
from jax.experimental.pallas import tpu_sc as plsc

F32 = jnp.float32
BF16 = jnp.bfloat16
N_DEV = 8
EPS = 1e-6
RG_C = 8.0
GELU_K0 = 0.7978845608028654
GELU_K1 = 0.044715
ADAM_LR = 0.001
ADAM_B1 = 0.9
ADAM_B2 = 0.999
ADAM_EPS = 1e-08
ADAM_WD = 0.01
ADAM_STEP = 10
LANES = 128
SUBLANES = 8
VMEM_LIMIT = 52 * 1024 * 1024
VMEM_LIMIT_BIG = 58 * 1024 * 1024
MIX_ROWS = 256
SMALL_ROWS = 80

MESH = pl.DeviceIdType.MESH
ANY = pl.BlockSpec(memory_space=pl.ANY)
NN = ((1,), (0,))
NT = ((1,), (1,))
TN = ((0,), (0,))


def _dot(a, b, dims):
    return lax.dot_general(a, b, (dims, ((), ())), preferred_element_type=F32)


def _params(sem=None):
    return pltpu.CompilerParams(dimension_semantics=sem, vmem_limit_bytes=VMEM_LIMIT)


def _full(shape):
    nd = len(shape)
    return pl.BlockSpec(shape, lambda *_: (0,) * nd)


def _exchange(name, gathers, scatters):
    n_g = len(gathers)
    arrs = list(gathers) + list(scatters)
    n = len(arrs)
    out_shape = [jax.ShapeDtypeStruct((N_DEV,) + a.shape, a.dtype) for a in gathers]
    out_shape += [jax.ShapeDtypeStruct(a.shape, a.dtype) for a in scatters]

    def body(*refs):
        ins, outs = refs[:n], refs[n:2 * n]
        send_sems, recv_sems, local_sems = refs[2 * n:]
        x, y, c = lax.axis_index("x"), lax.axis_index("y"), lax.axis_index("c")
        me = 4 * x + 2 * y + c

        def src(a, dev):
            return ins[a] if a < n_g else ins[a].at[dev]

        def peer_of(k):
            px = 1 - x if (k >> 2) & 1 else x
            py = 1 - y if (k >> 1) & 1 else y
            pc = 1 - c if k & 1 else c
            return (px, py, pc), 4 * px + 2 * py + pc

        local = [pltpu.make_async_copy(src(a, me), outs[a].at[me], local_sems.at[a]) for a in range(n)]
        for cp in local:
            cp.start()
        sends = []
        for k in range(1, N_DEV):
            peer, pidx = peer_of(k)
            for a in range(n):
                cp = pltpu.make_async_remote_copy(
                    src_ref=src(a, pidx), dst_ref=outs[a].at[me],
                    send_sem=send_sems.at[a * (N_DEV - 1) + k - 1], recv_sem=recv_sems.at[a * (N_DEV - 1) + k - 1],
                    device_id=peer, device_id_type=MESH)
                cp.start()
                sends.append(cp)
        for k in range(1, N_DEV):
            peer, pidx = peer_of(k)
            for a in range(n):
                pltpu.make_async_remote_copy(
                    src_ref=src(a, pidx), dst_ref=outs[a].at[pidx],
                    send_sem=send_sems.at[a * (N_DEV - 1) + k - 1], recv_sem=recv_sems.at[a * (N_DEV - 1) + k - 1],
                    device_id=peer, device_id_type=MESH).wait_recv()
        for cp in sends:
            cp.wait_send()
        for cp in local:
            cp.wait()

    return pl.pallas_call(
        body, name=name, out_shape=out_shape,
        in_specs=[ANY] * n, out_specs=[ANY] * n,
        scratch_shapes=[pltpu.SemaphoreType.DMA((n * (N_DEV - 1),)),
                        pltpu.SemaphoreType.DMA((n * (N_DEV - 1),)),
                        pltpu.SemaphoreType.DMA((n,))],
    )(*arrs)


def _gather2(name, arrs):
    n = len(arrs)
    per = 7
    out_shape = [jax.ShapeDtypeStruct((N_DEV,) + a.shape, a.dtype) for a in arrs]

    def body(*refs):
        ins, outs = refs[:n], refs[n:2 * n]
        send_sems, recv_sems, local_sems = refs[2 * n:]
        x, y, c = lax.axis_index("x"), lax.axis_index("y"), lax.axis_index("c")
        sib = (x, y, 1 - c)
        chips = [(1 - x, y), (x, 1 - y), (1 - x, 1 - y)]

        def slot(a, px, py, pc):
            return outs[a].at[4 * px + 2 * py + pc]

        def copy(a, k, block, to, src=None):
            return pltpu.make_async_remote_copy(
                src_ref=slot(a, *block) if src is None else src, dst_ref=slot(a, *block),
                send_sem=send_sems.at[a * per + k], recv_sem=recv_sems.at[a * per + k],
                device_id=to, device_id_type=MESH)

        local = [pltpu.make_async_copy(ins[a], slot(a, x, y, c), local_sems.at[a]) for a in range(n)]
        for cp in local:
            cp.start()
        first = []
        for a in range(n):
            first += [copy(a, 1 + j, (x, y, c), (*chip, c), src=ins[a]) for j, chip in enumerate(chips)]
        for a in range(n):
            first.append(copy(a, 0, (x, y, c), sib, src=ins[a]))
        for cp in first:
            cp.start()
        passed = []
        for a in range(n):
            for j, chip in enumerate(chips):
                copy(a, 1 + j, (*chip, c), (x, y, c)).wait_recv()
                cp = copy(a, 4 + j, (*chip, c), sib)
                cp.start()
                passed.append(cp)
        for a in range(n):
            copy(a, 0, sib, (x, y, c)).wait_recv()
            for j, chip in enumerate(chips):
                copy(a, 4 + j, (*chip, 1 - c), (x, y, c)).wait_recv()
        for cp in first + passed:
            cp.wait_send()
        for cp in local:
            cp.wait()

    return pl.pallas_call(
        body, name=name, out_shape=out_shape,
        in_specs=[ANY] * n, out_specs=[ANY] * n,
        scratch_shapes=[pltpu.SemaphoreType.DMA((n * per,)), pltpu.SemaphoreType.DMA((n * per,)),
                        pltpu.SemaphoreType.DMA((n,))],
    )(*arrs)


def _seq_gather2(name, collective_id, arrs):
    n = len(arrs)
    per = 7

    def body(*refs):
        ins, outs = refs[:n], refs[n:2 * n]
        send_sems, recv_sems, local_sems = refs[2 * n:]
        x, y, c = lax.axis_index("x"), lax.axis_index("y"), lax.axis_index("c")
        sib = (x, y, 1 - c)
        chips = [(1 - x, y), (x, 1 - y), (1 - x, 1 - y)]
        barrier = pltpu.get_barrier_semaphore()
        for peer in [sib] + [(*chip, c) for chip in chips]:
            pl.semaphore_signal(barrier, inc=1, device_id=peer, device_id_type=MESH)
        pl.semaphore_wait(barrier, 4)

        def slot(a, px, py, pc):
            return outs[a].at[4 * px + 2 * py + pc]

        def copy(a, k, block, to, src=None):
            return pltpu.make_async_remote_copy(
                src_ref=slot(a, *block) if src is None else src, dst_ref=slot(a, *block),
                send_sem=send_sems.at[a * per + k], recv_sem=recv_sems.at[a * per + k],
                device_id=to, device_id_type=MESH)

        local = [pltpu.make_async_copy(ins[a], slot(a, x, y, c), local_sems.at[a]) for a in range(n)]
        for cp in local:
            cp.start()
        first = []
        for a in range(n):
            first += [copy(a, 1 + j, (x, y, c), (*chip, c), src=ins[a]) for j, chip in enumerate(chips)]
        for a in range(n):
            first.append(copy(a, 0, (x, y, c), sib, src=ins[a]))
        for cp in first:
            cp.start()
        passed = []
        for a in range(n):
            for j, chip in enumerate(chips):
                copy(a, 1 + j, (*chip, c), (x, y, c)).wait_recv()
                cp = copy(a, 4 + j, (*chip, c), sib)
                cp.start()
                passed.append(cp)
        for a in range(n):
            copy(a, 0, sib, (x, y, c)).wait_recv()
            for j, chip in enumerate(chips):
                copy(a, 4 + j, (*chip, 1 - c), (x, y, c)).wait_recv()
        for cp in first + passed:
            cp.wait_send()
        for cp in local:
            cp.wait()

    return pl.kernel(
        body, out_type=[jax.ShapeDtypeStruct((N_DEV,) + a.shape, a.dtype) for a in arrs],
        mesh=plsc.ScalarSubcoreMesh(axis_name="seq", num_cores=1),
        scratch_types=[pltpu.SemaphoreType.DMA((n * per,)), pltpu.SemaphoreType.DMA((n * per,)),
                       pltpu.SemaphoreType.DMA((n,))],
        compiler_params=pltpu.CompilerParams(collective_id=collective_id), name=name,
    )(*arrs)


def _seq_chip_exchange(name, collective_id, arrs):
    n = len(arrs)

    def body(*refs):
        ins, outs = refs[:n], refs[n:2 * n]
        send_sems, recv_sems = refs[2 * n:]
        x, y, c = lax.axis_index("x"), lax.axis_index("y"), lax.axis_index("c")

        def peer(k):
            return (1 - x if (k >> 1) & 1 else x), (1 - y if k & 1 else y)

        barrier = pltpu.get_barrier_semaphore()
        for k in (1, 2, 3):
            pl.semaphore_signal(barrier, inc=1, device_id=(*peer(k), c), device_id_type=MESH)
        pl.semaphore_wait(barrier, 3)

        def copy(a, k):
            px, py = peer(k)
            return pltpu.make_async_remote_copy(
                src_ref=ins[a].at[2 * px + py], dst_ref=outs[a].at[k - 1],
                send_sem=send_sems.at[a * 3 + k - 1], recv_sem=recv_sems.at[a * 3 + k - 1],
                device_id=(px, py, c), device_id_type=MESH)

        cps = [copy(a, k) for a in range(n) for k in (1, 2, 3)]
        for cp in cps:
            cp.start()
        for cp in cps:
            cp.wait_recv()
        for cp in cps:
            cp.wait_send()

    return pl.kernel(
        body, out_type=[jax.ShapeDtypeStruct((3,) + a.shape[1:], a.dtype) for a in arrs],
        mesh=plsc.ScalarSubcoreMesh(axis_name="seq", num_cores=1),
        scratch_types=[pltpu.SemaphoreType.DMA((n * 3,)), pltpu.SemaphoreType.DMA((n * 3,))],
        compiler_params=pltpu.CompilerParams(collective_id=collective_id), name=name,
    )(*arrs)


def _seq_pair_swap(name, collective_id, arrs):
    n = len(arrs)

    def body(*refs):
        ins, outs = refs[:n], refs[n:2 * n]
        send_sems, recv_sems = refs[2 * n:]
        x, y, c = lax.axis_index("x"), lax.axis_index("y"), lax.axis_index("c")
        barrier = pltpu.get_barrier_semaphore()
        pl.semaphore_signal(barrier, inc=1, device_id=(x, y, 1 - c), device_id_type=MESH)
        pl.semaphore_wait(barrier, 1)

        def copy(a, q):
            return pltpu.make_async_remote_copy(
                src_ref=ins[a].at[q, 1 - c], dst_ref=outs[a].at[q],
                send_sem=send_sems.at[a * 4 + q], recv_sem=recv_sems.at[a * 4 + q],
                device_id=(x, y, 1 - c), device_id_type=MESH)

        cps = [copy(a, q) for a in range(n) for q in range(4)]
        for cp in cps:
            cp.start()
        for cp in cps:
            cp.wait_recv()
        for cp in cps:
            cp.wait_send()

    return pl.kernel(
        body, out_type=[jax.ShapeDtypeStruct((4,) + a.shape[2:], a.dtype) for a in arrs],
        mesh=plsc.ScalarSubcoreMesh(axis_name="seq", num_cores=1),
        scratch_types=[pltpu.SemaphoreType.DMA((n * 4,)), pltpu.SemaphoreType.DMA((n * 4,))],
        compiler_params=pltpu.CompilerParams(collective_id=collective_id), name=name,
    )(*arrs)


def _pair_swap(name, arrs):
    n = len(arrs)
    out_shape = [jax.ShapeDtypeStruct((4,) + a.shape[2:], a.dtype) for a in arrs]

    def body(*refs):
        ins, outs = refs[:n], refs[n:2 * n]
        send_sems, recv_sems = refs[2 * n:]
        x, y, c = lax.axis_index("x"), lax.axis_index("y"), lax.axis_index("c")

        def copy(a, q):
            return pltpu.make_async_remote_copy(
                src_ref=ins[a].at[q, 1 - c], dst_ref=outs[a].at[q],
                send_sem=send_sems.at[a * 4 + q], recv_sem=recv_sems.at[a * 4 + q],
                device_id=(x, y, 1 - c), device_id_type=MESH)

        cps = [copy(a, q) for a in range(n) for q in range(4)]
        for cp in cps:
            cp.start()
        for cp in cps:
            cp.wait_recv()
        for cp in cps:
            cp.wait_send()

    return pl.pallas_call(
        body, name=name, out_shape=out_shape,
        in_specs=[ANY] * n, out_specs=[ANY] * n,
        scratch_shapes=[pltpu.SemaphoreType.DMA((n * 4,)), pltpu.SemaphoreType.DMA((n * 4,))],
    )(*arrs)


def _chip_exchange(name, arrs):
    n = len(arrs)
    out_shape = [jax.ShapeDtypeStruct((3,) + a.shape[1:], a.dtype) for a in arrs]

    def body(*refs):
        ins, outs = refs[:n], refs[n:2 * n]
        send_sems, recv_sems = refs[2 * n:]
        x, y, c = lax.axis_index("x"), lax.axis_index("y"), lax.axis_index("c")

        def copy(a, k):
            px = 1 - x if (k >> 1) & 1 else x
            py = 1 - y if k & 1 else y
            return pltpu.make_async_remote_copy(
                src_ref=ins[a].at[2 * px + py], dst_ref=outs[a].at[k - 1],
                send_sem=send_sems.at[a * 3 + k - 1], recv_sem=recv_sems.at[a * 3 + k - 1],
                device_id=(px, py, c), device_id_type=MESH)

        cps = [copy(a, k) for a in range(n) for k in (1, 2, 3)]
        for cp in cps:
            cp.start()
        for cp in cps:
            cp.wait_recv()
        for cp in cps:
            cp.wait_send()

    return pl.pallas_call(
        body, name=name, out_shape=out_shape,
        in_specs=[ANY] * n, out_specs=[ANY] * n,
        scratch_shapes=[pltpu.SemaphoreType.DMA((n * 3,)), pltpu.SemaphoreType.DMA((n * 3,))],
    )(*arrs)


class _Rider:
    def __init__(self, arrays, out_shapes, n_sems, build, aliases=None):
        self.arrays, self.out_shapes, self.n_sems, self.build = list(arrays), list(out_shapes), n_sems, build
        self.aliases = dict(aliases or {})


def _merge_riders(r1, r2):
    n1i, n1o, n1s = len(r1.arrays), len(r1.out_shapes), r1.n_sems

    def build(ins, outs, send_sems, recv_sems):
        a = r1.build(ins[:n1i], outs[:n1o], send_sems.at[pl.ds(0, n1s)], recv_sems.at[pl.ds(0, n1s)])
        b = r2.build(ins[n1i:], outs[n1o:], send_sems.at[pl.ds(n1s, r2.n_sems)], recv_sems.at[pl.ds(n1s, r2.n_sems)])
        return tuple(p + q for p, q in zip(a, b))

    aliases = dict(r1.aliases)
    aliases.update({k + n1i: v + n1o for k, v in r2.aliases.items()})
    return _Rider(r1.arrays + r2.arrays, r1.out_shapes + r2.out_shapes, n1s + r2.n_sems, build, aliases)


def _place():
    x, y, c = lax.axis_index("x"), lax.axis_index("y"), lax.axis_index("c")
    chips = [(1 - x, y), (x, 1 - y), (1 - x, 1 - y)]
    return x, y, c, chips


def _ride_gather_ici(arrs):
    n = len(arrs)

    def build(ins, outs, send_sems, recv_sems):
        x, y, c, chips = _place()
        peers = [(*chip, c) for chip in chips] + [(x, y, 1 - c)]
        me = 4 * x + 2 * y + c
        local = [pltpu.make_async_copy(ins[a], outs[a].at[me], send_sems.at[a * 5 + 4]) for a in range(n)]
        sends, recvs = [], []
        for a in range(n):
            for j, (px, py, pc) in enumerate(peers):
                sends.append(pltpu.make_async_remote_copy(
                    src_ref=ins[a], dst_ref=outs[a].at[me], send_sem=send_sems.at[a * 5 + j],
                    recv_sem=recv_sems.at[a * 5 + j], device_id=(px, py, pc), device_id_type=MESH))
                recvs.append(pltpu.make_async_remote_copy(
                    src_ref=ins[a], dst_ref=outs[a].at[4 * px + 2 * py + pc], send_sem=send_sems.at[a * 5 + j],
                    recv_sem=recv_sems.at[a * 5 + j], device_id=(px, py, pc), device_id_type=MESH))
        return local, sends, recvs

    shapes = [jax.ShapeDtypeStruct((N_DEV,) + a.shape, a.dtype) for a in arrs]
    return _Rider(arrs, shapes, n * 5, build)


def _ride_gather_direct(arrs):
    n = len(arrs)

    def build(ins, outs, send_sems, recv_sems):
        x, y, c, _ = _place()
        me = 4 * x + 2 * y + c
        local = [pltpu.make_async_copy(ins[a], outs[a].at[me], send_sems.at[a * N_DEV + 7]) for a in range(n)]
        sends, recvs = [], []
        for a in range(n):
            for k in range(1, N_DEV):
                px = 1 - x if (k >> 2) & 1 else x
                py = 1 - y if (k >> 1) & 1 else y
                pc = 1 - c if k & 1 else c
                sem = a * N_DEV + k - 1
                sends.append(pltpu.make_async_remote_copy(
                    src_ref=ins[a], dst_ref=outs[a].at[me], send_sem=send_sems.at[sem], recv_sem=recv_sems.at[sem],
                    device_id=(px, py, pc), device_id_type=MESH))
                recvs.append(pltpu.make_async_remote_copy(
                    src_ref=ins[a], dst_ref=outs[a].at[4 * px + 2 * py + pc], send_sem=send_sems.at[sem],
                    recv_sem=recv_sems.at[sem], device_id=(px, py, pc), device_id_type=MESH))
        return local, sends, recvs

    shapes = [jax.ShapeDtypeStruct((N_DEV,) + a.shape, a.dtype) for a in arrs]
    return _Rider(arrs, shapes, n * N_DEV, build)


def _ride_gather_d2d(gathered):
    n = len(gathered)

    def build(ins, outs, send_sems, recv_sems):
        x, y, c, chips = _place()
        sends, recvs = [], []
        for a in range(n):
            for j, (px, py) in enumerate(chips):
                mine = outs[a].at[4 * px + 2 * py + c]
                theirs = outs[a].at[4 * px + 2 * py + 1 - c]
                sends.append(pltpu.make_async_remote_copy(
                    src_ref=mine, dst_ref=mine, send_sem=send_sems.at[a * 3 + j], recv_sem=recv_sems.at[a * 3 + j],
                    device_id=(x, y, 1 - c), device_id_type=MESH))
                recvs.append(pltpu.make_async_remote_copy(
                    src_ref=mine, dst_ref=theirs, send_sem=send_sems.at[a * 3 + j], recv_sem=recv_sems.at[a * 3 + j],
                    device_id=(x, y, 1 - c), device_id_type=MESH))
        return [], sends, recvs

    shapes = [jax.ShapeDtypeStruct(a.shape, a.dtype) for a in gathered]
    return _Rider(gathered, shapes, n * 3, build, aliases={a: a for a in range(n)})


def _ride_pair_swap(arrs):
    n = len(arrs)

    def build(ins, outs, send_sems, recv_sems):
        x, y, c, _ = _place()
        cps = [pltpu.make_async_remote_copy(
            src_ref=ins[a].at[q, 1 - c], dst_ref=outs[a].at[q], send_sem=send_sems.at[a * 4 + q],
            recv_sem=recv_sems.at[a * 4 + q], device_id=(x, y, 1 - c), device_id_type=MESH)
            for a in range(n) for q in range(4)]
        return [], cps, cps

    shapes = [jax.ShapeDtypeStruct((4,) + a.shape[2:], a.dtype) for a in arrs]
    return _Rider(arrs, shapes, n * 4, build)


def _ride_chip_exchange(arrs):
    n = len(arrs)

    def build(ins, outs, send_sems, recv_sems):
        x, y, c, _ = _place()
        cps = []
        for a in range(n):
            for k in (1, 2, 3):
                px = 1 - x if (k >> 1) & 1 else x
                py = 1 - y if k & 1 else y
                cps.append(pltpu.make_async_remote_copy(
                    src_ref=ins[a].at[2 * px + py], dst_ref=outs[a].at[k - 1], send_sem=send_sems.at[a * 3 + k - 1],
                    recv_sem=recv_sems.at[a * 3 + k - 1], device_id=(px, py, c), device_id_type=MESH))
        return [], cps, cps

    shapes = [jax.ShapeDtypeStruct((3,) + a.shape[1:], a.dtype) for a in arrs]
    return _Rider(arrs, shapes, n * 3, build)


def _call(body, name, grid, in_specs, out_specs, out_shape, args, scratch=(), rider=None):
    n_in, n_out, n_scr = len(in_specs), len(out_specs), len(scratch)
    sem = ("arbitrary",) * len(grid)
    if rider is None:
        outs = pl.pallas_call(
            body, name=name, grid=grid, in_specs=in_specs, out_specs=out_specs, out_shape=out_shape,
            scratch_shapes=list(scratch), compiler_params=_params(sem))(*args)
        return outs, []
    ri, ro = len(rider.arrays), len(rider.out_shapes)

    def riding(*refs):
        ins, r_ins = refs[:n_in], refs[n_in:n_in + ri]
        outs = refs[n_in + ri:n_in + ri + n_out]
        r_outs = refs[n_in + ri + n_out:n_in + ri + n_out + ro]
        scr = refs[n_in + ri + n_out + ro:n_in + ri + n_out + ro + n_scr]
        send_sems, recv_sems = refs[-2:]
        first = functools.reduce(jnp.logical_and, [pl.program_id(k) == 0 for k in range(len(grid))])
        last = functools.reduce(jnp.logical_and, [pl.program_id(k) == grid[k] - 1 for k in range(len(grid))])

        @pl.when(first)
        def _():
            local, sends, _ = rider.build(r_ins, r_outs, send_sems, recv_sems)
            for cp in local + sends:
                cp.start()

        body(*ins, *outs, *scr)

        @pl.when(last)
        def _():
            local, sends, recvs = rider.build(r_ins, r_outs, send_sems, recv_sems)
            for cp in recvs:
                cp.wait_recv()
            for cp in sends:
                cp.wait_send()
            for cp in local:
                cp.wait()

    outs = pl.pallas_call(
        riding, name=name, grid=grid,
        in_specs=list(in_specs) + [ANY] * ri, out_specs=list(out_specs) + [ANY] * ro,
        out_shape=list(out_shape) + rider.out_shapes,
        scratch_shapes=list(scratch) + [pltpu.SemaphoreType.DMA((rider.n_sems,)), pltpu.SemaphoreType.DMA((rider.n_sems,))],
        input_output_aliases={n_in + k: n_out + v for k, v in rider.aliases.items()},
        compiler_params=_params(sem))(*args, *rider.arrays)
    return outs[:n_out], outs[n_out:]


def _comm(name, rider):
    def body(dummy_ref, out_ref):
        out_ref[...] = dummy_ref[...]

    dummy = jnp.zeros((SUBLANES, LANES), F32)
    spec = pl.BlockSpec((SUBLANES, LANES), lambda i: (0, 0))
    _, r_outs = _call(body, name, (1,), [spec], [spec], [jax.ShapeDtypeStruct(dummy.shape, F32)], [dummy], rider=rider)
    return r_outs


def _ada_fwd(c_all, w_ada_sh, b_ada_sh):
    nb, d = c_all.shape
    ncol = w_ada_sh.shape[1]

    def body(c_ref, w_ref, b_ref, mod_ref, cact_ref):
        cc = c_ref[...]
        ca = cc * jax.nn.sigmoid(cc)
        cact_ref[...] = ca
        mod_ref[...] = _dot(ca.astype(BF16), w_ref[...].astype(BF16), NN) + b_ref[...]

    return pl.pallas_call(
        body, name="ada_fwd",
        out_shape=[jax.ShapeDtypeStruct((nb, ncol), F32), jax.ShapeDtypeStruct((nb, d), F32)],
        compiler_params=_params(),
    )(c_all, w_ada_sh, b_ada_sh)


def _rms(xv):
    rstd = lax.rsqrt(jnp.mean(xv * xv, axis=-1, keepdims=True) + EPS)
    return xv * rstd, rstd


def _rms_bwd(dxhat, xhat, rstd):
    return rstd * (dxhat - xhat * jnp.mean(dxhat * xhat, axis=-1, keepdims=True))


def _colsum(v):
    return jnp.sum(v, axis=0, keepdims=True)


def _expm1(v, ev):
    series = v * (1.0 + v * (0.5 + v * (1.0 / 6.0 + v * (1.0 / 24.0 + v * (1.0 / 120.0)))))
    return jnp.where(jnp.abs(v) < 0.2, series, ev - 1.0)


def _softplus(v):
    return jnp.maximum(v, 0.0) + jnp.log1p(jnp.exp(-jnp.abs(v)))


def _gelu(v):
    t = jnp.tanh(v * (GELU_K0 + (GELU_K0 * GELU_K1) * (v * v)))
    return 0.5 * v * (1.0 + t), t


def _dgelu(v, t):
    return 0.5 * ((1.0 + t) + (v * (1.0 - t * t)) * (GELU_K0 + (3.0 * GELU_K0 * GELU_K1) * (v * v)))


def _shift_down(v, k, prev8):
    r = pltpu.roll(v, k, 0)
    pr = pltpu.roll(prev8, k, 0)
    row8 = lax.broadcasted_iota(jnp.int32, prev8.shape, 0)
    top = jnp.where(row8 < k, pr, r[0:SUBLANES])
    return jnp.concatenate([top, r[SUBLANES:]], axis=0)


def _shift_up(v, k, next8):
    t = v.shape[0]
    r = pltpu.roll(v, t - k, 0)
    nr = pltpu.roll(next8, SUBLANES - k, 0)
    row8 = lax.broadcasted_iota(jnp.int32, next8.shape, 0)
    bot = jnp.where(row8 >= SUBLANES - k, nr, r[t - SUBLANES:t])
    return jnp.concatenate([r[:t - SUBLANES], bot], axis=0)


def _scan_tile(a, b, x0, st, k0, reverse):
    t = a.shape[0]
    off = SUBLANES
    stage_a, stage_b = st.at[k0], st.at[k0 + 1]
    halo = slice(off + t, off + t + SUBLANES) if reverse else slice(0, SUBLANES)
    stage_a[halo, :] = jnp.ones((SUBLANES, a.shape[1]), F32)
    stage_b[halo, :] = jnp.zeros((SUBLANES, a.shape[1]), F32)
    s = 1
    while s < min(t, SUBLANES):
        stage_a[off:off + t, :] = a
        stage_b[off:off + t, :] = b
        at = off + s if reverse else off - s
        b = a * stage_b[at:at + t, :] + b
        a = a * stage_a[at:at + t, :]
        s *= 2
    while s < t:
        if reverse:
            b = jnp.concatenate([a[:t - s] * b[s:] + b[:t - s], b[t - s:]], axis=0)
            a = jnp.concatenate([a[:t - s] * a[s:], a[t - s:]], axis=0)
        else:
            b = jnp.concatenate([b[:s], a[s:] * b[:t - s] + b[s:]], axis=0)
            a = jnp.concatenate([a[:s], a[s:] * a[:t - s]], axis=0)
        s *= 2
    x = b + a * x0
    return x, (x[0:SUBLANES, :] if reverse else x[t - SUBLANES:t, :])


def _lru_gates(u, wa, wx, ba, bx, sp):
    ub = u.astype(BF16)
    r = jax.nn.sigmoid(_dot(ub, wa, NN) + ba)
    i = jax.nn.sigmoid(_dot(ub, wx, NN) + bx)
    log_a = (-RG_C * r) * sp
    a = jnp.exp(log_a)
    mult = jnp.sqrt(-_expm1(log_a, a) * (a + 1.0))
    return ub, r, i, a, mult


def _staged_shifts(stage, v, prev8, next8, downs, ups):
    t = v.shape[0]
    if prev8 is not None:
        stage[0:SUBLANES, :] = prev8
    stage[SUBLANES:SUBLANES + t, :] = v
    if next8 is not None:
        stage[SUBLANES + t:2 * SUBLANES + t, :] = next8
    return ([stage[SUBLANES - k:SUBLANES - k + t, :] for k in downs],
            [stage[SUBLANES + k:SUBLANES + k + t, :] for k in ups])


def _conv3(p, pp, w_ref, lo, stage=None):
    if stage is None:
        p1 = _shift_down(p, 1, pp)
        p2 = _shift_down(p, 2, pp)
    else:
        (p1, p2), _ = _staged_shifts(stage, p, pp, None, (1, 2), ())
    q = (w_ref[0:1, lo:lo + LANES] * p2 + w_ref[1:2, lo:lo + LANES] * p1) + w_ref[2:3, lo:lo + LANES] * p
    return q, p1, p2


def _conv4(xv, xp, w_ref, b_ref, lo, stage=None):
    if stage is None:
        x1 = _shift_down(xv, 1, xp)
        x2 = _shift_down(xv, 2, xp)
        x3 = _shift_down(xv, 3, xp)
    else:
        (x1, x2, x3), _ = _staged_shifts(stage, xv, xp, None, (1, 2, 3), ())
    u = (((w_ref[0:1, lo:lo + LANES] * x3 + w_ref[1:2, lo:lo + LANES] * x2) + w_ref[2:3, lo:lo + LANES] * x1)
         + w_ref[3:4, lo:lo + LANES] * xv) + b_ref[:, lo:lo + LANES]
    return u, x1, x2, x3


def _mix_in_fwd(x2d, mod6, g_mix, w_in_t, tm, rider=None):
    s, d = x2d.shape
    din = w_in_t.shape[0]

    def body(x_ref, mod_ref, g_ref, w_ref, hn_ref, proj_ref):
        xhat, _ = _rms(x_ref[...])
        hn = ((xhat * g_ref[...]) * (1.0 + mod_ref[1:2, :]) + mod_ref[0:1, :]).astype(BF16)
        hn_ref[...] = hn
        proj_ref[...] = _dot(hn, w_ref[...], NT)

    return _call(
        body, "mix_in_fwd", (s // tm,),
        [pl.BlockSpec((tm, d), lambda i: (i, 0)), _full(mod6.shape), _full(g_mix.shape), _full(w_in_t.shape)],
        [pl.BlockSpec((tm, d), lambda i: (i, 0)), pl.BlockSpec((tm, din), lambda i: (i, 0))],
        [jax.ShapeDtypeStruct((s, d), BF16), jax.ShapeDtypeStruct((s, din), F32)],
        [x2d, mod6, g_mix, w_in_t], rider=rider)


def _mix_in_mixer_fwd(x2d, mod6, g_mix, w_in_t, conv_sc, conv_lru, conv_b, wa_bd, wx_bd, ba, bx, lam, width, tm):
    s, d = x2d.shape
    din = w_in_t.shape[0]
    nt = s // tm
    sub = min(MIX_ROWS, tm)
    nblk = width // LANES

    def body(x_ref, mod_ref, g_ref, w_ref, wsc_ref, wlru_ref, blru_ref, wa_ref, wx_ref, ba_ref, bx_ref, lam_ref,
             hn_ref, proj_ref, ymix_ref, h_ref, buf_ref, halo_ref, hc_ref, stage_ref):
        i = pl.program_id(0)

        @pl.when(i == 0)
        def _():
            buf_ref[1] = jnp.zeros((tm, din), F32)
            halo_ref[...] = jnp.zeros_like(halo_ref)

        @pl.when(i <= 1)
        def _():
            hc_ref[...] = jnp.zeros_like(hc_ref)

        def step(dst, src):
            xhat, _ = _rms(x_ref[...])
            hn = ((xhat * g_ref[...]) * (1.0 + mod_ref[1:2, :]) + mod_ref[0:1, :]).astype(BF16)
            hn_ref[...] = hn
            n_mix = (tm // sub) * nblk
            n_chunk = din // width

            def project(k):
                res = _dot(hn_ref[...], w_ref[k * width:(k + 1) * width, :], NT)
                proj_ref[:, k * width:(k + 1) * width] = res
                dst[:, k * width:(k + 1) * width] = res

            done = 0
            for half in range(tm // sub):
                r0 = half * sub
                rows = slice(r0, r0 + sub)
                for j in range(nblk):
                    lo = j * LANES
                    while done < n_chunk and done * n_mix <= (half * nblk + j) * n_chunk:
                        project(done)
                        done += 1

                    def col(p):
                        return src[rows, p * width + lo:p * width + lo + LANES]

                    def prev(p):
                        c0 = p * width + lo
                        if half == 0:
                            return halo_ref[:, c0:c0 + LANES]
                        return src[r0 - SUBLANES:r0, c0:c0 + LANES]

                    pp = col(1) * col(2)
                    q, _, _ = _conv3(pp, prev(1) * prev(2), wsc_ref, lo, stage_ref.at[0])
                    ymix_ref[rows, lo:lo + LANES] = (col(0) * q).astype(BF16)

                    u, _, _, _ = _conv4(col(4), prev(4), wlru_ref, blru_ref, lo, stage_ref.at[1])
                    sp = _softplus(-lam_ref[:, lo:lo + LANES])
                    _, r, ig, a, mult = _lru_gates(u, wa_ref[j], wx_ref[j], ba_ref[:, lo:lo + LANES],
                                                   bx_ref[:, lo:lo + LANES], sp)
                    h, ends = _scan_tile(a, mult * (ig * u), hc_ref[0:1, lo:lo + LANES], stage_ref, 2, False)
                    h_ref[rows, lo:lo + LANES] = h
                    hc_ref[0:1, lo:lo + LANES] = ends[SUBLANES - 1:SUBLANES, :]
                    gel, _ = _gelu(col(3))
                    ymix_ref[rows, width + lo:width + lo + LANES] = (gel * h).astype(BF16)
            while done < n_chunk:
                project(done)
                done += 1
            halo_ref[...] = src[tm - SUBLANES:tm, :]

        @pl.when(i % 2 == 0)
        def _():
            step(buf_ref.at[0], buf_ref.at[1])

        @pl.when(i % 2 == 1)
        def _():
            step(buf_ref.at[1], buf_ref.at[0])

    small = [conv_sc, conv_lru, conv_b, wa_bd, wx_bd, ba, bx, lam]
    cur = lambda i: (jnp.minimum(i, nt - 1), 0)
    last = lambda i: (jnp.maximum(i - 1, 0), 0)
    outs, _ = _call(
        body, "mix_in_mixer_fwd", (nt + 1,),
        [pl.BlockSpec((tm, d), cur), _full(mod6.shape), _full(g_mix.shape), _full(w_in_t.shape)]
        + [_full(a.shape) for a in small],
        [pl.BlockSpec((tm, d), cur), pl.BlockSpec((tm, din), cur),
         pl.BlockSpec((tm, 2 * width), last), pl.BlockSpec((tm, width), last)],
        [jax.ShapeDtypeStruct((s, d), BF16), jax.ShapeDtypeStruct((s, din), F32),
         jax.ShapeDtypeStruct((s, 2 * width), BF16), jax.ShapeDtypeStruct((s, width), F32)],
        [x2d, mod6, g_mix, w_in_t, *small],
        scratch=[pltpu.VMEM((2, tm, din), F32), pltpu.VMEM((SUBLANES, din), F32), pltpu.VMEM((SUBLANES, width), F32),
                 pltpu.VMEM((4, sub + 2 * SUBLANES, LANES), F32)])
    return outs


def _mixer_fwd(proj, conv_sc, conv_lru, conv_b, wa_bd, wx_bd, ba, bx, lam, width, rider=None):
    s, din = proj.shape
    t = min(MIX_ROWS, s)
    nblk = width // LANES
    hb = t // SUBLANES

    def body(proj_ref, projp_ref, wsc_ref, wlru_ref, blru_ref, wa_ref, wx_ref, ba_ref, bx_ref, lam_ref,
             ymix_ref, h_ref, hc_ref, stage_ref):
        i = pl.program_id(0)

        @pl.when(i == 0)
        def _():
            hc_ref[...] = jnp.zeros_like(hc_ref)

        has_prev = i > 0
        for j in range(nblk):
            lo = j * LANES

            def col(p, ref=proj_ref):
                return ref[:, p * width + lo:p * width + lo + LANES]

            def prev(p):
                return jnp.where(has_prev, col(p, projp_ref), 0.0)

            p = col(1) * col(2)
            q, _, _ = _conv3(p, prev(1) * prev(2), wsc_ref, lo, stage_ref.at[0])
            ymix_ref[:, lo:lo + LANES] = (col(0) * q).astype(BF16)

            u, _, _, _ = _conv4(col(4), prev(4), wlru_ref, blru_ref, lo, stage_ref.at[1])
            sp = _softplus(-lam_ref[:, lo:lo + LANES])
            _, r, ig, a, mult = _lru_gates(u, wa_ref[j], wx_ref[j], ba_ref[:, lo:lo + LANES], bx_ref[:, lo:lo + LANES], sp)
            h, ends = _scan_tile(a, mult * (ig * u), hc_ref[0:1, lo:lo + LANES], stage_ref, 2, False)
            h_ref[:, lo:lo + LANES] = h
            hc_ref[0:1, lo:lo + LANES] = ends[SUBLANES - 1:SUBLANES, :]
            gel, _ = _gelu(col(3))
            ymix_ref[:, width + lo:width + lo + LANES] = (gel * h).astype(BF16)

    small = [conv_sc, conv_lru, conv_b, wa_bd, wx_bd, ba, bx, lam]
    return _call(
        body, "mixer_fwd", (s // t,),
        [pl.BlockSpec((t, din), lambda i: (i, 0)),
         pl.BlockSpec((SUBLANES, din), lambda i: (jnp.maximum(i * hb - 1, 0), 0))]
        + [_full(a.shape) for a in small],
        [pl.BlockSpec((t, 2 * width), lambda i: (i, 0)), pl.BlockSpec((t, width), lambda i: (i, 0))],
        [jax.ShapeDtypeStruct((s, 2 * width), BF16), jax.ShapeDtypeStruct((s, width), F32)],
        [proj, proj, *small],
        scratch=[pltpu.VMEM((SUBLANES, width), F32), pltpu.VMEM((7, t + 2 * SUBLANES, LANES), F32)], rider=rider)


def _mix_out_fwd(ymix, x2d, w_out, mod6, g_mlp, tm, rider=None):
    s, d = x2d.shape

    def body(y_ref, x_ref, w_ref, mod_ref, g_ref, mix_ref, x2_ref, hn_ref):
        mix = _dot(y_ref[...], w_ref[...], NN)
        mix_ref[...] = mix.astype(BF16)
        x2 = x_ref[...] + mod_ref[2:3, :] * mix
        x2_ref[...] = x2
        xhat, _ = _rms(x2)
        hn_ref[...] = ((xhat * g_ref[...]) * (1.0 + mod_ref[4:5, :]) + mod_ref[3:4, :]).astype(BF16)

    tile = pl.BlockSpec((tm, d), lambda i: (i, 0))
    return _call(
        body, "mix_out_fwd", (s // tm,),
        [tile, tile, _full(w_out.shape), _full(mod6.shape), _full(g_mlp.shape)],
        [tile, tile, tile],
        [jax.ShapeDtypeStruct((s, d), BF16), jax.ShapeDtypeStruct((s, d), F32), jax.ShapeDtypeStruct((s, d), BF16)],
        [ymix, x2d, w_out, mod6, g_mlp], rider=rider)


def _mlp_fwd_loss(hn2, w_up_t, w_down, x2, target, mod6, g_final, tm, tk):
    s, d = hn2.shape
    f = w_up_t.shape[0]
    nk = f // tk

    def body(hn_ref, wu_ref, wd_ref, x2_hbm, t_hbm, mod_ref, g_ref, z_ref, dx3_ref, dyb_ref, st_ref,
             y_ref, x2_ref, t_ref, sems):
        i, k = pl.program_id(0), pl.program_id(1)

        def fetch():
            rows = pl.ds(pl.multiple_of(i * tm, tm), tm)
            return (pltpu.make_async_copy(x2_hbm.at[rows, :], x2_ref, sems.at[0]),
                    pltpu.make_async_copy(t_hbm.at[rows, :], t_ref, sems.at[1]))

        @pl.when(jnp.logical_and(i == 0, k == 0))
        def _():
            st_ref[...] = jnp.zeros_like(st_ref)

        @pl.when(k == 0)
        def _():
            for cp in fetch():
                cp.start()

        z = jnp.maximum(_dot(hn_ref[...], wu_ref[...], NT), 0.0)
        z_ref[...] = z.astype(BF16)
        part = _dot((z * z).astype(BF16), wd_ref[...], NN)

        @pl.when(k == 0)
        def _():
            y_ref[...] = part

        @pl.when(k > 0)
        def _():
            y_ref[...] += part

        @pl.when(k == nk - 1)
        def _():
            for cp in fetch():
                cp.wait()
            gate = mod_ref[5:6, :]
            yv = y_ref[...]
            xhat, rstd = _rms(x2_ref[...] + gate * yv)
            diff = xhat * g_ref[...] - t_ref[...]
            dyo = diff * (1.0 / d)
            dx3 = _rms_bwd(dyo * g_ref[...], xhat, rstd)
            dx3_ref[...] = dx3
            dyb_ref[...] = (gate * dx3).astype(BF16)
            st_ref[0:1, :] += _colsum(dyo * xhat)
            st_ref[1:2, :] += _colsum(dx3 * yv)
            st_ref[2:3, :] += _colsum(diff * diff)

    tile = pl.BlockSpec((tm, d), lambda i, k: (i, 0))
    wblk = pl.BlockSpec((tk, d), lambda i, k: (k, 0))
    return pl.pallas_call(
        body, name="mlp_fwd_loss", grid=(s // tm, nk),
        in_specs=[tile, wblk, wblk, ANY, ANY, _full(mod6.shape), _full(g_final.shape)],
        out_specs=[pl.BlockSpec((tm, tk), lambda i, k: (i, k)), tile, tile, _full((SUBLANES, d))],
        out_shape=[jax.ShapeDtypeStruct((s, f), BF16), jax.ShapeDtypeStruct((s, d), F32),
                   jax.ShapeDtypeStruct((s, d), BF16), jax.ShapeDtypeStruct((SUBLANES, d), F32)],
        scratch_shapes=[pltpu.VMEM((tm, d), F32), pltpu.VMEM((tm, d), F32), pltpu.VMEM((tm, d), F32),
                        pltpu.SemaphoreType.DMA((2,))],
        compiler_params=pltpu.CompilerParams(dimension_semantics=("arbitrary", "arbitrary"),
                                             vmem_limit_bytes=VMEM_LIMIT_BIG),
    )(hn2, w_up_t, w_down, x2, target, mod6, g_final)


def _mlp_bwd_dx(dyb, z, w_down, w_up_t, tm, tk):
    s, d = dyb.shape
    f = z.shape[1]

    def body(dy_ref, z_ref, wd_ref, wu_ref, dz_ref, dh_ref):
        k = pl.program_id(1)
        dz = ((2.0 * z_ref[...].astype(F32)) * _dot(dy_ref[...], wd_ref[...], NT)).astype(BF16)
        dz_ref[...] = dz
        part = _dot(dz, wu_ref[...], NN)

        @pl.when(k == 0)
        def _():
            dh_ref[...] = part

        @pl.when(k > 0)
        def _():
            dh_ref[...] += part

    return pl.pallas_call(
        body, name="mlp_bwd_dx", grid=(s // tm, f // tk),
        in_specs=[pl.BlockSpec((tm, d), lambda i, k: (i, 0)), pl.BlockSpec((tm, tk), lambda i, k: (i, k)),
                  pl.BlockSpec((tk, d), lambda i, k: (k, 0)), pl.BlockSpec((tk, d), lambda i, k: (k, 0))],
        out_specs=[pl.BlockSpec((tm, tk), lambda i, k: (i, k)), pl.BlockSpec((tm, d), lambda i, k: (i, 0))],
        out_shape=[jax.ShapeDtypeStruct((s, f), BF16), jax.ShapeDtypeStruct((s, d), F32)],
        compiler_params=_params(("parallel", "arbitrary")),
    )(dyb, z, w_down, w_up_t)


def _mlp_bwd_dw(z, dz, dyb, hn2, tm, tk):
    s, d = dyb.shape
    f = z.shape[1]

    def body(z_ref, dz_ref, dy_ref, hn_ref, gd_ref, gu_ref):
        i = pl.program_id(1)

        @pl.when(i == 0)
        def _():
            gd_ref[...] = jnp.zeros_like(gd_ref)
            gu_ref[...] = jnp.zeros_like(gu_ref)

        zf = z_ref[...].astype(F32)
        gd_ref[...] += _dot((zf * zf).astype(BF16), dy_ref[...], TN)
        gu_ref[...] += _dot(dz_ref[...], hn_ref[...], TN)

    return pl.pallas_call(
        body, name="mlp_bwd_dw", grid=(f // tk, s // tm),
        in_specs=[pl.BlockSpec((tm, tk), lambda k, i: (i, k)), pl.BlockSpec((tm, tk), lambda k, i: (i, k)),
                  pl.BlockSpec((tm, d), lambda k, i: (i, 0)), pl.BlockSpec((tm, d), lambda k, i: (i, 0))],
        out_specs=[pl.BlockSpec((tk, d), lambda k, i: (k, 0)), pl.BlockSpec((tk, d), lambda k, i: (k, 0))],
        out_shape=[jax.ShapeDtypeStruct((f, d), F32), jax.ShapeDtypeStruct((f, d), F32)],
        compiler_params=_params(("parallel", "arbitrary")),
    )(z, dz, dyb, hn2)


def _mix_out_bwd(dhn2, x2, dx3, mix, ymix, w_out, mod6, g_mlp, tm, rider=None):
    s, d = x2.shape

    def body(dh_ref, x2_ref, dx3_ref, mix_ref, y_ref, w_ref, mod_ref, g_ref, dx2_ref, dym_ref, gw_ref, st_ref):
        i = pl.program_id(0)

        @pl.when(i == 0)
        def _():
            st_ref[...] = jnp.zeros_like(st_ref)
            gw_ref[...] = jnp.zeros_like(gw_ref)

        dh = dh_ref[...]
        xhat, rstd = _rms(x2_ref[...])
        dn = dh * (1.0 + mod_ref[4:5, :])
        dx2 = dx3_ref[...] + _rms_bwd(dn * g_ref[...], xhat, rstd)
        dx2_ref[...] = dx2
        st_ref[0:1, :] += _colsum(dh)
        st_ref[1:2, :] += _colsum(dh * (xhat * g_ref[...]))
        st_ref[2:3, :] += _colsum(dn * xhat)
        st_ref[3:4, :] += _colsum(dx2 * mix_ref[...].astype(F32))
        dmix = (mod_ref[2:3, :] * dx2).astype(BF16)
        dym_ref[...] = _dot(dmix, w_ref[...], NT)
        gw_ref[...] += _dot(y_ref[...], dmix, TN)

    tile = pl.BlockSpec((tm, d), lambda i: (i, 0))
    return _call(
        body, "mix_out_bwd", (s // tm,),
        [tile, tile, tile, tile, tile, _full(w_out.shape), _full(mod6.shape), _full(g_mlp.shape)],
        [tile, tile, _full((d, d)), _full((SUBLANES, d))],
        [jax.ShapeDtypeStruct((s, d), F32), jax.ShapeDtypeStruct((s, d), F32),
         jax.ShapeDtypeStruct((d, d), F32), jax.ShapeDtypeStruct((SUBLANES, d), F32)],
        [dhn2, x2, dx3, mix, ymix, w_out, mod6, g_mlp], rider=rider)


def _mixer_bwd(proj, dymix, h_all, conv_sc, conv_lru, conv_b, wa_bd, wx_bd, ba, bx, lam, width, rider=None):
    s, din = proj.shape
    t = min(MIX_ROWS, s)
    nt = s // t
    nblk = width // LANES
    hb = t // SUBLANES
    last8 = s // SUBLANES - 1

    def body(proj_ref, projp_ref, projn_ref, dy_ref, dyn_ref, h_ref, hp_ref,
             wsc_ref, wlru_ref, blru_ref, wa_ref, wx_ref, ba_ref, bx_ref, lam_ref,
             dproj_ref, small_ref, gwa_ref, gwx_ref, an_ref, gn_ref, dun_ref, stage_ref):
        i = pl.program_id(0)

        @pl.when(i == 0)
        def _():
            small_ref[...] = jnp.zeros_like(small_ref)
            gwa_ref[...] = jnp.zeros_like(gwa_ref)
            gwx_ref[...] = jnp.zeros_like(gwx_ref)
            an_ref[...] = jnp.zeros_like(an_ref)
            gn_ref[...] = jnp.zeros_like(gn_ref)
            dun_ref[...] = jnp.zeros_like(dun_ref)

        has_prev = i < nt - 1
        has_next = i > 0
        for j in range(nblk):
            lo = j * LANES
            ls = slice(lo, lo + LANES)

            def col(p, ref=proj_ref):
                return ref[:, p * width + lo:p * width + lo + LANES]

            def prev(p):
                return jnp.where(has_prev, col(p, projp_ref), 0.0)

            def nxt(p):
                return jnp.where(has_next, col(p, projn_ref), 0.0)

            def add_row(r, v):
                small_ref[r:r + 1, ls] += _colsum(v)

            sc_b, sc_c, sc_x = col(0), col(1), col(2)
            p = sc_c * sc_x
            q, p1, p2 = _conv3(p, prev(1) * prev(2), wsc_ref, lo, stage_ref.at[0])
            dys = dy_ref[:, ls]
            dproj_ref[:, ls] = (dys * q).astype(BF16)
            dq = dys * sc_b
            dqn = jnp.where(has_next, dyn_ref[:, ls], 0.0) * nxt(0)
            _, (dq1, dq2) = _staged_shifts(stage_ref.at[1], dq, None, dqn, (), (1, 2))
            dp = (wsc_ref[2:3, ls] * dq + wsc_ref[1:2, ls] * dq1) + wsc_ref[0:1, ls] * dq2
            dproj_ref[:, width + lo:width + lo + LANES] = (dp * sc_x).astype(BF16)
            dproj_ref[:, 2 * width + lo:2 * width + lo + LANES] = (dp * sc_c).astype(BF16)
            add_row(0, dq * p2)
            add_row(1, dq * p1)
            add_row(2, dq * p)

            xv = col(4)
            u, x1, x2, x3 = _conv4(xv, prev(4), wlru_ref, blru_ref, lo, stage_ref.at[2])
            lam_v = lam_ref[:, ls]
            sp = _softplus(-lam_v)
            wa, wx = wa_ref[j], wx_ref[j]
            ub, r, ig, a, mult = _lru_gates(u, wa, wx, ba_ref[:, ls], bx_ref[:, ls], sp)
            iu = ig * u
            h = h_ref[:, ls]
            (hm1,), _ = _staged_shifts(stage_ref.at[3], h, jnp.where(has_prev, hp_ref[:, ls], 0.0), None, (1,), ())
            lyv = col(3)
            gel, th = _gelu(lyv)
            dyl = dy_ref[:, width + lo:width + lo + LANES]
            dproj_ref[:, 3 * width + lo:3 * width + lo + LANES] = (dyl * h * _dgelu(lyv, th)).astype(BF16)
            a_next = jnp.broadcast_to(an_ref[0:1, ls], (SUBLANES, LANES))
            _, (a_up,) = _staged_shifts(stage_ref.at[4], a, None, a_next, (), (1,))
            g, _ = _scan_tile(a_up, dyl * gel, gn_ref[0:1, ls], stage_ref, 5, True)
            an_ref[0:1, ls] = a[0:1, :]
            gn_ref[0:1, ls] = g[0:1, :]
            da = g * hm1
            dmult = g * iu
            diu = g * mult
            dlog_a = da * a - dmult * ((a * a) / mult)
            dpre_a = (dlog_a * (-RG_C * sp)) * (r * (1.0 - r))
            dpre_x = (diu * u) * (ig * (1.0 - ig))
            dab, dxb = dpre_a.astype(BF16), dpre_x.astype(BF16)
            du = diu * ig + _dot(dab, wa, NT) + _dot(dxb, wx, NT)
            gwa_ref[j] += _dot(ub, dab, TN)
            gwx_ref[j] += _dot(ub, dxb, TN)
            dun = dun_ref[:, ls]
            dun_ref[:, ls] = du[0:SUBLANES, :]
            _, (du1, du2, du3) = _staged_shifts(stage_ref.at[7], du, None, dun, (), (1, 2, 3))
            dlx = (((wlru_ref[3:4, ls] * du + wlru_ref[2:3, ls] * du1) + wlru_ref[1:2, ls] * du2)
                   + wlru_ref[0:1, ls] * du3)
            dproj_ref[:, 4 * width + lo:4 * width + lo + LANES] = dlx.astype(BF16)
            add_row(3, du * x3)
            add_row(4, du * x2)
            add_row(5, du * x1)
            add_row(6, du * xv)
            add_row(7, du)
            add_row(8, dpre_a)
            add_row(9, dpre_x)
            add_row(10, (dlog_a * (RG_C * r)) * jax.nn.sigmoid(-lam_v))

    small = [conv_sc, conv_lru, conv_b, wa_bd, wx_bd, ba, bx, lam]
    rev = lambda i: nt - 1 - i
    return _call(
        body, "mixer_bwd", (nt,),
        [pl.BlockSpec((t, din), lambda i: (rev(i), 0)),
         pl.BlockSpec((SUBLANES, din), lambda i: (jnp.maximum(rev(i) * hb - 1, 0), 0)),
         pl.BlockSpec((SUBLANES, din), lambda i: (jnp.minimum((rev(i) + 1) * hb, last8), 0)),
         pl.BlockSpec((t, 2 * width), lambda i: (rev(i), 0)),
         pl.BlockSpec((SUBLANES, 2 * width), lambda i: (jnp.minimum((rev(i) + 1) * hb, last8), 0)),
         pl.BlockSpec((t, width), lambda i: (rev(i), 0)),
         pl.BlockSpec((SUBLANES, width), lambda i: (jnp.maximum(rev(i) * hb - 1, 0), 0))]
        + [_full(a.shape) for a in small],
        [pl.BlockSpec((t, din), lambda i: (rev(i), 0)), _full((2 * SUBLANES, width)),
         _full(wa_bd.shape), _full(wx_bd.shape)],
        [jax.ShapeDtypeStruct((s, din), BF16), jax.ShapeDtypeStruct((2 * SUBLANES, width), F32),
         jax.ShapeDtypeStruct(wa_bd.shape, F32), jax.ShapeDtypeStruct(wx_bd.shape, F32)],
        [proj, proj, proj, dymix, dymix, h_all, h_all, *small],
        scratch=[pltpu.VMEM((SUBLANES, width), F32), pltpu.VMEM((SUBLANES, width), F32),
                 pltpu.VMEM((SUBLANES, width), F32), pltpu.VMEM((8, t + 2 * SUBLANES, LANES), F32)], rider=rider)


def _mix_in_bwd_dx(dproj, x2d, dx2, w_in_t, mod6, g_mix, tm, rider=None):
    s, d = x2d.shape
    din = dproj.shape[1]

    def body(dp_ref, x_ref, dx2_ref, w_ref, mod_ref, g_ref, gx_ref, st_ref):
        i = pl.program_id(0)

        @pl.when(i == 0)
        def _():
            st_ref[...] = jnp.zeros_like(st_ref)

        dh = _dot(dp_ref[...], w_ref[...], NN)
        xhat, rstd = _rms(x_ref[...])
        dn = dh * (1.0 + mod_ref[1:2, :])
        gx_ref[...] = dx2_ref[...] + _rms_bwd(dn * g_ref[...], xhat, rstd)
        st_ref[0:1, :] += _colsum(dh)
        st_ref[1:2, :] += _colsum(dh * (xhat * g_ref[...]))
        st_ref[2:3, :] += _colsum(dn * xhat)

    tile = pl.BlockSpec((tm, d), lambda i: (i, 0))
    return _call(
        body, "mix_in_bwd_dx", (s // tm,),
        [pl.BlockSpec((tm, din), lambda i: (i, 0)), tile, tile, _full(w_in_t.shape), _full(mod6.shape),
         _full(g_mix.shape)],
        [tile, _full((SUBLANES, d))],
        [jax.ShapeDtypeStruct((s, d), F32), jax.ShapeDtypeStruct((SUBLANES, d), F32)],
        [dproj, x2d, dx2, w_in_t, mod6, g_mix], rider=rider)


def _mix_in_bwd_dw(dproj, hn1, tm, tn, rider=None):
    s, d = hn1.shape
    din = dproj.shape[1]

    def body(dp_ref, hn_ref, gw_ref):
        i = pl.program_id(1)

        @pl.when(i == 0)
        def _():
            gw_ref[...] = jnp.zeros_like(gw_ref)

        gw_ref[...] += _dot(dp_ref[...], hn_ref[...], TN)

    return _call(
        body, "mix_in_bwd_dw", (din // tn, s // tm),
        [pl.BlockSpec((tm, tn), lambda p, i: (i, p)), pl.BlockSpec((tm, d), lambda p, i: (i, 0))],
        [pl.BlockSpec((tn, d), lambda p, i: (p, 0))],
        [jax.ShapeDtypeStruct((din, d), F32)],
        [dproj, hn1], rider=rider)


def _adamw(w, g, m, v):
    m = ADAM_B1 * m + (1.0 - ADAM_B1) * g
    v = ADAM_B2 * v + (1.0 - ADAM_B2) * (g * g)
    m_hat = m / (1.0 - ADAM_B1 ** ADAM_STEP)
    v_hat = v / (1.0 - ADAM_B2 ** ADAM_STEP)
    delta = -ADAM_LR * (m_hat / (jnp.sqrt(v_hat) + ADAM_EPS) + ADAM_WD * w)
    return delta, m, v


def _pair_sum(g4, h4, core_chip, tr, name):
    _, _, r, n = g4.shape

    def body(sc_ref, g_ref, h_ref, sb_ref, own_ref):
        q = pl.program_id(1)
        ssum = g_ref[...] + h_ref[...]
        sb_ref[...] = ssum.astype(BF16)

        @pl.when(q == sc_ref[1])
        def _():
            own_ref[...] = ssum

    grid_spec = pltpu.PrefetchScalarGridSpec(
        num_scalar_prefetch=1, grid=(r // tr, 4),
        in_specs=[pl.BlockSpec((None, None, tr, n), lambda i, q, sc: (q, sc[0], i, 0)),
                  pl.BlockSpec((None, tr, n), lambda i, q, sc: (q, i, 0))],
        out_specs=[pl.BlockSpec((None, tr, n), lambda i, q, sc: (q, i, 0)),
                   pl.BlockSpec((tr, n), lambda i, q, sc: (i, 0))])
    return pl.pallas_call(
        body, name=name, grid_spec=grid_spec,
        out_shape=[jax.ShapeDtypeStruct((4, r, n), BF16), jax.ShapeDtypeStruct((r, n), F32)],
        compiler_params=_params(("parallel", "arbitrary")),
    )(core_chip, g4, h4)


def _sum4(own, parts, tr, name):
    r, n = own.shape

    def body(o_ref, p_ref, out_ref):
        acc = o_ref[...]
        for k in range(3):
            acc = acc + p_ref[k].astype(F32)
        out_ref[...] = acc

    return pl.pallas_call(
        body, name=name, grid=(r // tr,),
        in_specs=[pl.BlockSpec((tr, n), lambda i: (i, 0)), pl.BlockSpec((3, tr, n), lambda i: (0, i, 0))],
        out_specs=pl.BlockSpec((tr, n), lambda i: (i, 0)),
        out_shape=jax.ShapeDtypeStruct((r, n), F32),
        compiler_params=_params(("parallel",)),
    )(own, parts)


def _sum4_adam(own, parts, w, m, v, tr, name, transposed):
    r, n = own.shape
    rows, cols = w.shape

    def body(o_ref, p_ref, w_ref, m_ref, v_ref, g_ref, d_ref, nm_ref, nv_ref):
        g = o_ref[...]
        for k in range(3):
            g = g + p_ref[k].astype(F32)
        if transposed:
            g = g.T
        g_ref[...] = g
        d_ref[...], nm_ref[...], nv_ref[...] = _adamw(w_ref[...], g, m_ref[...], v_ref[...])

    if transposed:
        g_specs = [pl.BlockSpec((r, tr), lambda i: (0, i)), pl.BlockSpec((3, r, tr), lambda i: (0, 0, i))]
    else:
        g_specs = [pl.BlockSpec((tr, n), lambda i: (i, 0)), pl.BlockSpec((3, tr, n), lambda i: (0, i, 0))]
    tile = pl.BlockSpec((tr, cols), lambda i: (i, 0))
    return pl.pallas_call(
        body, name=name, grid=(rows // tr,),
        in_specs=g_specs + [tile] * 3, out_specs=[tile] * 4,
        out_shape=[jax.ShapeDtypeStruct((rows, cols), F32)] * 4,
        compiler_params=_params(("parallel",)),
    )(own, parts, w, m, v)


def _sum8(parts, tr, name):
    _, rows, n = parts.shape

    def body(p_ref, o_ref):
        acc = p_ref[0]
        for k in range(1, N_DEV):
            acc = acc + p_ref[k]
        o_ref[...] = acc

    return pl.pallas_call(
        body, name=name, grid=(rows // tr,),
        in_specs=[pl.BlockSpec((N_DEV, tr, n), lambda i: (0, i, 0))],
        out_specs=pl.BlockSpec((tr, n), lambda i: (i, 0)),
        out_shape=jax.ShapeDtypeStruct((rows, n), F32),
        compiler_params=_params(("parallel",)),
    )(parts)


def _adam_rows(w, g, m, v, tr, name):
    rows, n = w.shape

    def body(w_ref, g_ref, m_ref, v_ref, d_ref, nm_ref, nv_ref):
        d_ref[...], nm_ref[...], nv_ref[...] = _adamw(w_ref[...], g_ref[...], m_ref[...], v_ref[...])

    tile = pl.BlockSpec((tr, n), lambda i: (i, 0))
    return pl.pallas_call(
        body, name=name, grid=(rows // tr,),
        in_specs=[tile] * 4, out_specs=[tile] * 3,
        out_shape=[jax.ShapeDtypeStruct((rows, n), F32)] * 3,
        compiler_params=_params(("parallel",)),
    )(w, g, m, v)


def _ada_bwd_adam(cact_t, dmod_cols, w, m, v, tr):
    rows, n = w.shape

    def body(c_ref, d_ref, w_ref, m_ref, v_ref, g_ref, dl_ref, nm_ref, nv_ref):
        def term(b):
            return c_ref[b].astype(BF16).astype(F32) * d_ref[b:b + 1, :].astype(BF16).astype(F32)

        g = term(0)
        for b in range(1, N_DEV):
            g = g + term(b)
        g_ref[...] = g
        dl_ref[...], nm_ref[...], nv_ref[...] = _adamw(w_ref[...], g, m_ref[...], v_ref[...])

    tile = pl.BlockSpec((tr, n), lambda i: (i, 0))
    return pl.pallas_call(
        body, name="ada_bwd_adam", grid=(rows // tr,),
        in_specs=[pl.BlockSpec((N_DEV, tr, 1), lambda i: (0, i, 0)), _full(dmod_cols.shape), tile, tile, tile],
        out_specs=[tile] * 4,
        out_shape=[jax.ShapeDtypeStruct((rows, n), F32)] * 4,
        compiler_params=_params(("parallel",)),
    )(cact_t, dmod_cols, w, m, v)


def _adam_small(ws, gs, ms, vs):
    n = len(ws)

    def body(*refs):
        w_r, g_r, m_r, v_r = refs[:n], refs[n:2 * n], refs[2 * n:3 * n], refs[3 * n:4 * n]
        d_r, nm_r, nv_r = refs[4 * n:5 * n], refs[5 * n:6 * n], refs[6 * n:7 * n]
        for k in range(n):
            d_r[k][...], nm_r[k][...], nv_r[k][...] = _adamw(w_r[k][...], g_r[k][...], m_r[k][...], v_r[k][...])

    shapes = [jax.ShapeDtypeStruct(w.shape, F32) for w in ws]
    outs = pl.pallas_call(
        body, name="adam_small", out_shape=shapes * 3, compiler_params=_params(),
    )(*ws, *gs, *ms, *vs)
    return outs[:n], outs[n:2 * n], outs[2 * n:]


def _block_diag(w):
    h, hd, _ = w.shape
    per = LANES // hd
    eye = jnp.eye(per, dtype=w.dtype)
    w5 = w.reshape(h // per, per, hd, 1, hd) * eye[None, :, None, :, None]
    return w5.reshape(h // per, LANES, LANES)


def _block_diag_grad(g, h, hd):
    per = LANES // hd
    g5 = g.reshape(h // per, per, hd, per, hd)
    return jnp.stack([g5[:, a, :, a, :] for a in range(per)], axis=1).reshape(h, hd, hd)


def kernel(x, c, w_ada, b_ada, g_mix, w_in, conv_w_sc, conv_w_lru, conv_b_lru, w_rg_a, b_rg_a, w_rg_x, b_rg_x, lru_lambda, w_out, g_mlp, w_up, w_down, g_final, loss_target, m_w_ada, m_b_ada, m_g_mix, m_w_in, m_conv_w_sc, m_conv_w_lru, m_conv_b_lru, m_w_rg_a, m_b_rg_a, m_w_rg_x, m_b_rg_x, m_lru_lambda, m_w_out, m_g_mlp, m_w_up, m_w_down, m_g_final, v_w_ada, v_b_ada, v_g_mix, v_w_in, v_conv_w_sc, v_conv_w_lru, v_conv_b_lru, v_w_rg_a, v_b_rg_a, v_w_rg_x, v_b_rg_x, v_lru_lambda, v_w_out, v_g_mlp, v_w_up, v_w_down, v_g_final):
    s, d = x.shape[1], x.shape[2]
    width = conv_b_lru.shape[1]
    heads, hd = w_rg_a.shape[1], w_rg_a.shape[2]
    f = w_down.shape[1] * N_DEV
    n_ada = w_ada.shape[2]
    csh = conv_w_sc.shape[2]
    me = 4 * lax.axis_index("x") + 2 * lax.axis_index("y") + lax.axis_index("c")
    tm = min(512, s)
    tm_mlp = min(1024, s)
    tk = 512

    x2d = x[0]
    tgt = loss_target[0]

    pay = jnp.zeros((SUBLANES, d), F32)
    pay = pay.at[0:1, :].set(c)
    pay = pay.at[1:4, 0:csh].set(conv_w_sc[0])
    pay = pay.at[4:8, 0:csh].set(conv_w_lru[0])
    w_in_t_sh = w_in[0].T.astype(BF16)
    w_up_t_sh = w_up[0].T.astype(BF16)
    w_out_sh = w_out[0].astype(BF16)
    w_down_sh = w_down[0].astype(BF16)
    pay_all, w_in_t = _gather2("gather_in", [pay, w_in_t_sh])
    w_in_t = w_in_t.reshape(-1, d)
    c_all = pay_all[:, 0, :]
    conv_sc = pay_all[:, 1:4, 0:csh].transpose(1, 0, 2).reshape(3, width)
    conv_lru = pay_all[:, 4:8, 0:csh].transpose(1, 0, 2).reshape(4, width)

    b_ada_sh = lax.dynamic_slice(b_ada, (0, me * n_ada), (1, n_ada))
    mod_cols, c_act = _ada_fwd(c_all, w_ada[0], b_ada_sh)
    (mod_rows,) = _exchange("scatter_mod", [], [mod_cols.reshape(N_DEV, 1, n_ada)])
    mod_rows, w_out_sh, w_up_t_sh, w_down_sh = lax.optimization_barrier((mod_rows, w_out_sh, w_up_t_sh, w_down_sh))
    (w_out_g,) = _seq_gather2("gather_w_out", 1, [w_out_sh])
    w_up_g, w_down_g = _seq_gather2("gather_mlp_weights", 2, [w_up_t_sh, w_down_sh])
    mod6 = jnp.zeros((SUBLANES, d), F32).at[0:6, :].set(mod_rows.reshape(6, d))

    wa_bd = _block_diag(w_rg_a[0]).astype(BF16)
    wx_bd = _block_diag(w_rg_x[0]).astype(BF16)
    ba = b_rg_a.reshape(1, width)
    bx = b_rg_x.reshape(1, width)
    g_fin = g_final.reshape(1, d)

    hn1, proj, ymix, h_all = _mix_in_mixer_fwd(x2d, mod6, g_mix, w_in_t, conv_sc, conv_lru, conv_b_lru,
                                               wa_bd, wx_bd, ba, bx, lru_lambda, width, tm)
    w_out_b = w_out_g.reshape(-1, d)
    (mix, x2, hn2), _ = _mix_out_fwd(ymix, x2d, w_out_b, mod6, g_mlp, tm)
    w_up_t = w_up_g.reshape(-1, d)
    w_down_b = w_down_g.reshape(-1, d)
    z, dx3, dyb, st_fin = _mlp_fwd_loss(hn2, w_up_t, w_down_b, x2, tgt, mod6, g_fin, tm_mlp, 2 * tk)

    core_chip = jnp.stack([lax.axis_index("c"), 2 * lax.axis_index("x") + lax.axis_index("y")]).astype(jnp.int32)
    dz, dhn2 = _mlp_bwd_dx(dyb, z, w_down_b, w_up_t, tm_mlp, 2 * tk)
    g_down, g_up_t = _mlp_bwd_dw(z, dz, dyb, hn2, tm_mlp, 2 * tk)
    g_up4, g_down4 = g_up_t.reshape(4, 2, -1, d), g_down.reshape(4, 2, -1, d)
    h_up, h_down = _seq_pair_swap("swap_mlp_grads", 7, [g_up4, g_down4])
    (dx2, dymix, g_out, st_out), _ = _mix_out_bwd(dhn2, x2, dx3, mix, ymix, w_out_b, mod6, g_mlp, tm)
    h_up, h_down, g_out = lax.optimization_barrier((h_up, h_down, g_out))
    sb_up, own_up = _pair_sum(g_up4, h_up, core_chip, 256, "pair_sum_w_up")
    sb_down, own_down = _pair_sum(g_down4, h_down, core_chip, 256, "pair_sum_w_down")
    g_out4 = g_out.reshape(4, 2, -1, d)
    (h_out,) = _seq_pair_swap("swap_w_out_grad", 8, [g_out4])
    p_up, p_down = _seq_chip_exchange("exchange_mlp_grads", 3, [sb_up, sb_down])
    (dproj, g_small, g_wa, g_wx), _ = _mixer_bwd(
        proj, dymix, h_all, conv_sc, conv_lru, conv_b_lru, wa_bd, wx_bd, ba, bx, lru_lambda, width)
    h_out, dproj = lax.optimization_barrier((h_out, dproj))
    sb_out, own_out = _pair_sum(g_out4, h_out, core_chip, g_out4.shape[2], "pair_sum_w_out")
    (p_out,) = _seq_chip_exchange("exchange_w_out_grad", 4, [sb_out])
    (grad_x, st_in), _ = _mix_in_bwd_dx(dproj, x2d, dx2, w_in_t, mod6, g_mix, tm)

    small = jnp.concatenate([
        st_in[0:2], st_out[3:4], st_out[0:2], st_fin[1:2],
        st_in[2:3], st_out[2:3], st_fin[0:1],
        jnp.concatenate([g_small[7:8], g_small[10:11]], axis=1),
        jnp.concatenate([g_small[8:9], g_small[9:10]], axis=1),
        jnp.concatenate([jnp.concatenate([g_small[0:3], jnp.zeros((1, width), F32)], axis=0), g_small[3:7]], axis=1),
        st_fin[2:3],
        _block_diag_grad(g_wa, heads, hd).reshape(-1, d),
        _block_diag_grad(g_wx, heads, hd).reshape(-1, d),
    ], axis=0)

    (small_all,) = _seq_gather2("gather_small_grads", 5, [small])
    (g_in_t,), _ = _mix_in_bwd_dw(dproj, hn1, min(2048, s), dproj.shape[1] // 2)
    g_in4 = g_in_t.reshape(4, 2, -1, d)
    (h_in,) = _seq_pair_swap("swap_w_in_grad", 9, [g_in4])
    p_up, p_down, p_out, small_all, g_in_t = lax.optimization_barrier((p_up, p_down, p_out, small_all, g_in_t))

    ad_up = _sum4_adam(own_up, p_up, w_up[0], m_w_up[0], v_w_up[0], 256, "adam_w_up", True)
    h_in, ad_up = lax.optimization_barrier((h_in, ad_up))
    sb_in, own_in = _pair_sum(g_in4, h_in, core_chip, g_in4.shape[2], "pair_sum_w_in")
    (p_in,) = _seq_chip_exchange("exchange_w_in_grad", 6, [sb_in])
    ad_out = _sum4_adam(own_out, p_out, w_out[0], m_w_out[0], v_w_out[0], w_out.shape[1], "adam_w_out", False)
    ad_down = _sum4_adam(own_down, p_down, w_down[0], m_w_down[0], v_w_down[0], 256, "adam_w_down", False)

    gsum = _sum8(small_all, SMALL_ROWS, "sum_small")
    loss = (0.5 / d) * jnp.sum(gsum[15])
    dmod_cols = lax.dynamic_slice(small_all[:, 0:6, :].reshape(N_DEV, 6 * d), (0, me * n_ada), (N_DEV, n_ada))
    g_ada, d_ada, nm_ada, nv_ada = _ada_bwd_adam(c_act[:, :, None], dmod_cols, w_ada[0], m_w_ada[0], v_w_ada[0], 256)

    g_conv = lax.dynamic_slice(gsum[11:15, 0:width], (0, me * csh), (4, csh))
    g_conv_l = lax.dynamic_slice(gsum[11:15, width:2 * width], (0, me * csh), (4, csh))
    small_g = [
        gsum[0:6].reshape(1, 6 * d),
        gsum[6:7],
        g_conv[0:3].reshape(1, 3, csh),
        g_conv_l.reshape(1, 4, csh),
        gsum[9:10, 0:width],
        gsum[16:48].reshape(1, heads, hd, hd),
        gsum[10:11, 0:width].reshape(1, heads, hd),
        gsum[48:80].reshape(1, heads, hd, hd),
        gsum[10:11, width:].reshape(1, heads, hd),
        gsum[9:10, width:],
        gsum[7:8],
        gsum[8],
    ]
    small_w = [b_ada, g_mix, conv_w_sc, conv_w_lru, conv_b_lru, w_rg_a, b_rg_a, w_rg_x, b_rg_x, lru_lambda, g_mlp, g_final]
    small_m = [m_b_ada, m_g_mix, m_conv_w_sc, m_conv_w_lru, m_conv_b_lru, m_w_rg_a, m_b_rg_a, m_w_rg_x, m_b_rg_x,
               m_lru_lambda, m_g_mlp, m_g_final]
    small_v = [v_b_ada, v_g_mix, v_conv_w_sc, v_conv_w_lru, v_conv_b_lru, v_w_rg_a, v_b_rg_a, v_w_rg_x, v_b_rg_x,
               v_lru_lambda, v_g_mlp, v_g_final]
    sd, snm, snv = _adam_small(small_w, small_g, small_m, small_v)
    p_in, ad_out, ad_down, (g_ada, d_ada, nm_ada, nv_ada), sd = lax.optimization_barrier(
        (p_in, ad_out, ad_down, (g_ada, d_ada, nm_ada, nv_ada), sd))
    ad_in = _sum4_adam(own_in, p_in, w_in[0].T, m_w_in[0].T, v_w_in[0].T, own_in.shape[0], "adam_w_in", False)
    ad_in = [a.T for a in ad_in]

    def order(ada, w_in_, w_out_, w_up_, w_down_, sm):
        return [ada[None], sm[0], sm[1], w_in_[None], sm[2], sm[3], sm[4], sm[5], sm[6], sm[7], sm[8], sm[9],
                w_out_[None], sm[10], w_up_[None], w_down_[None], sm[11]]

    grads = order(g_ada, ad_in[0], ad_out[0], ad_up[0], ad_down[0], small_g)
    deltas = order(d_ada, ad_in[1], ad_out[1], ad_up[1], ad_down[1], sd)
    new_m = order(nm_ada, ad_in[2], ad_out[2], ad_up[2], ad_down[2], snm)
    new_v = order(nv_ada, ad_in[3], ad_out[3], ad_up[3], ad_down[3], snv)
    return (loss, grad_x[None], *grads, *deltas, *new_m, *new_v)
```

```python
import jax
import jax.numpy as jnp
from jax import lax
from jax.experimental import pallas as pl
from jax.experimental.pallas import tpu as pltpu
from jax.experimental.pallas import tpu_sc as plsc

F32 = jnp.float32
BF16 = jnp.bfloat16
N_DEV = 8
EPS = 1e-6
RG_C = 8.0
GELU_K0 = 0.7978845608028654
GELU_K1 = 0.044715
ADAM_LR = 0.001
ADAM_B1 = 0.9
ADAM_B2 = 0.999
ADAM_EPS = 1e-08
ADAM_WD = 0.01
ADAM_STEP = 10
LANES = 128
SUBLANES = 8
VMEM_LIMIT = 52 * 1024 * 1024
VMEM_LIMIT_BIG = 58 * 1024 * 1024
MIX_ROWS = 256
SMALL_ROWS = 80

MESH = pl.DeviceIdType.MESH
ANY = pl.BlockSpec(memory_space=pl.ANY)
NN = ((1,), (0,))
NT = ((1,), (1,))
TN = ((0,), (0,))


def _dot(a, b, dims):
    return lax.dot_general(a, b, (dims, ((), ())), preferred_element_type=F32)


def _params(sem=None):
    return pltpu.CompilerParams(dimension_semantics=sem, vmem_limit_bytes=VMEM_LIMIT)


def _full(shape):
    nd = len(shape)
    return pl.BlockSpec(shape, lambda *_: (0,) * nd)


def _exchange(name, gathers, scatters):
    n_g = len(gathers)
    arrs = list(gathers) + list(scatters)
    n = len(arrs)
    out_shape = [jax.ShapeDtypeStruct((N_DEV,) + a.shape, a.dtype) for a in gathers]
    out_shape += [jax.ShapeDtypeStruct(a.shape, a.dtype) for a in scatters]

    def body(*refs):
        ins, outs = refs[:n], refs[n:2 * n]
        send_sems, recv_sems, local_sems = refs[2 * n:]
        x, y, c = lax.axis_index("x"), lax.axis_index("y"), lax.axis_index("c")
        me = 4 * x + 2 * y + c

        def src(a, dev):
            return ins[a] if a < n_g else ins[a].at[dev]

        def peer_of(k):
            px = 1 - x if (k >> 2) & 1 else x
            py = 1 - y if (k >> 1) & 1 else y
            pc = 1 - c if k & 1 else c
            return (px, py, pc), 4 * px + 2 * py + pc

        local = [pltpu.make_async_copy(src(a, me), outs[a].at[me], local_sems.at[a]) for a in range(n)]
        for cp in local:
            cp.start()
        sends = []
        for k in range(1, N_DEV):
            peer, pidx = peer_of(k)
            for a in range(n):
                cp = pltpu.make_async_remote_copy(
                    src_ref=src(a, pidx), dst_ref=outs[a].at[me],
                    send_sem=send_sems.at[a * (N_DEV - 1) + k - 1], recv_sem=recv_sems.at[a * (N_DEV - 1) + k - 1],
                    device_id=peer, device_id_type=MESH)
                cp.start()
                sends.append(cp)
        for k in range(1, N_DEV):
            peer, pidx = peer_of(k)
            for a in range(n):
                pltpu.make_async_remote_copy(
                    src_ref=src(a, pidx), dst_ref=outs[a].at[pidx],
                    send_sem=send_sems.at[a * (N_DEV - 1) + k - 1], recv_sem=recv_sems.at[a * (N_DEV - 1) + k - 1],
                    device_id=peer, device_id_type=MESH).wait_recv()
        for cp in sends:
            cp.wait_send()
        for cp in local:
            cp.wait()

    return pl.pallas_call(
        body, name=name, out_shape=out_shape,
        in_specs=[ANY] * n, out_specs=[ANY] * n,
        scratch_shapes=[pltpu.SemaphoreType.DMA((n * (N_DEV - 1),)),
                        pltpu.SemaphoreType.DMA((n * (N_DEV - 1),)),
                        pltpu.SemaphoreType.DMA((n,))],
    )(*arrs)


def _gather2(name, arrs):
    n = len(arrs)
    per = 7
    out_shape = [jax.ShapeDtypeStruct((N_DEV,) + a.shape, a.dtype) for a in arrs]

    def body(*refs):
        ins, outs = refs[:n], refs[n:2 * n]
        send_sems, recv_sems, local_sems = refs[2 * n:]
        x, y, c = lax.axis_index("x"), lax.axis_index("y"), lax.axis_index("c")
        sib = (x, y, 1 - c)
        chips = [(1 - x, y), (x, 1 - y), (1 - x, 1 - y)]

        def slot(a, px, py, pc):
            return outs[a].at[4 * px + 2 * py + pc]

        def copy(a, k, block, to, src=None):
            return pltpu.make_async_remote_copy(
                src_ref=slot(a, *block) if src is None else src, dst_ref=slot(a, *block),
                send_sem=send_sems.at[a * per + k], recv_sem=recv_sems.at[a * per + k],
                device_id=to, device_id_type=MESH)

        local = [pltpu.make_async_copy(ins[a], slot(a, x, y, c), local_sems.at[a]) for a in range(n)]
        for cp in local:
            cp.start()
        first = []
        for a in range(n):
            first += [copy(a, 1 + j, (x, y, c), (*chip, c), src=ins[a]) for j, chip in enumerate(chips)]
        for a in range(n):
            first.append(copy(a, 0, (x, y, c), sib, src=ins[a]))
        for cp in first:
            cp.start()
        passed = []
        for a in range(n):
            for j, chip in enumerate(chips):
                copy(a, 1 + j, (*chip, c), (x, y, c)).wait_recv()
                cp = copy(a, 4 + j, (*chip, c), sib)
                cp.start()
                passed.append(cp)
        for a in range(n):
            copy(a, 0, sib, (x, y, c)).wait_recv()
            for j, chip in enumerate(chips):
                copy(a, 4 + j, (*chip, 1 - c), (x, y, c)).wait_recv()
        for cp in first + passed:
            cp.wait_send()
        for cp in local:
            cp.wait()

    return pl.pallas_call(
        body, name=name, out_shape=out_shape,
        in_specs=[ANY] * n, out_specs=[ANY] * n,
        scratch_shapes=[pltpu.SemaphoreType.DMA((n * per,)), pltpu.SemaphoreType.DMA((n * per,)),
                        pltpu.SemaphoreType.DMA((n,))],
    )(*arrs)


def _seq_gather2(name, collective_id, arrs):
    n = len(arrs)
    per = 7

    def body(*refs):
        ins, outs = refs[:n], refs[n:2 * n]
        send_sems, recv_sems, local_sems = refs[2 * n:]
        x, y, c = lax.axis_index("x"), lax.axis_index("y"), lax.axis_index("c")
        sib = (x, y, 1 - c)
        chips = [(1 - x, y), (x, 1 - y), (1 - x, 1 - y)]
        barrier = pltpu.get_barrier_semaphore()
        for peer in [sib] + [(*chip, c) for chip in chips]:
            pl.semaphore_signal(barrier, inc=1, device_id=peer, device_id_type=MESH)
        pl.semaphore_wait(barrier, 4)

        def slot(a, px, py, pc):
            return outs[a].at[4 * px + 2 * py + pc]

        def copy(a, k, block, to, src=None):
            return pltpu.make_async_remote_copy(
                src_ref=slot(a, *block) if src is None else src, dst_ref=slot(a, *block),
                send_sem=send_sems.at[a * per + k], recv_sem=recv_sems.at[a * per + k],
                device_id=to, device_id_type=MESH)

        local = [pltpu.make_async_copy(ins[a], slot(a, x, y, c), local_sems.at[a]) for a in range(n)]
        for cp in local:
            cp.start()
        first = []
        for a in range(n):
            first += [copy(a, 1 + j, (x, y, c), (*chip, c), src=ins[a]) for j, chip in enumerate(chips)]
        for a in range(n):
            first.append(copy(a, 0, (x, y, c), sib, src=ins[a]))
        for cp in first:
            cp.start()
        passed = []
        for a in range(n):
            for j, chip in enumerate(chips):
                copy(a, 1 + j, (*chip, c), (x, y, c)).wait_recv()
                cp = copy(a, 4 + j, (*chip, c), sib)
                cp.start()
                passed.append(cp)
        for a in range(n):
            copy(a, 0, sib, (x, y, c)).wait_recv()
            for j, chip in enumerate(chips):
                copy(a, 4 + j, (*chip, 1 - c), (x, y, c)).wait_recv()
        for cp in first + passed:
            cp.wait_send()
        for cp in local:
            cp.wait()

    return pl.kernel(
        body, out_type=[jax.ShapeDtypeStruct((N_DEV,) + a.shape, a.dtype) for a in arrs],
        mesh=plsc.ScalarSubcoreMesh(axis_name="seq", num_cores=1),
        scratch_types=[pltpu.SemaphoreType.DMA((n * per,)), pltpu.SemaphoreType.DMA((n * per,)),
                       pltpu.SemaphoreType.DMA((n,))],
        compiler_params=pltpu.CompilerParams(collective_id=collective_id), name=name,
    )(*arrs)


def _seq_chip_exchange(name, collective_id, arrs):
    n = len(arrs)

    def body(*refs):
        ins, outs = refs[:n], refs[n:2 * n]
        send_sems, recv_sems = refs[2 * n:]
        x, y, c = lax.axis_index("x"), lax.axis_index("y"), lax.axis_index("c")

        def peer(k):
            return (1 - x if (k >> 1) & 1 else x), (1 - y if k & 1 else y)

        barrier = pltpu.get_barrier_semaphore()
        for k in (1, 2, 3):
            pl.semaphore_signal(barrier, inc=1, device_id=(*peer(k), c), device_id_type=MESH)
        pl.semaphore_wait(barrier, 3)

        def copy(a, k):
            px, py = peer(k)
            return pltpu.make_async_remote_copy(
                src_ref=ins[a].at[2 * px + py], dst_ref=outs[a].at[k - 1],
                send_sem=send_sems.at[a * 3 + k - 1], recv_sem=recv_sems.at[a * 3 + k - 1],
                device_id=(px, py, c), device_id_type=MESH)

        cps = [copy(a, k) for a in range(n) for k in (1, 2, 3)]
        for cp in cps:
            cp.start()
        for cp in cps:
            cp.wait_recv()
        for cp in cps:
            cp.wait_send()

    return pl.kernel(
        body, out_type=[jax.ShapeDtypeStruct((3,) + a.shape[1:], a.dtype) for a in arrs],
        mesh=plsc.ScalarSubcoreMesh(axis_name="seq", num_cores=1),
        scratch_types=[pltpu.SemaphoreType.DMA((n * 3,)), pltpu.SemaphoreType.DMA((n * 3,))],
        compiler_params=pltpu.CompilerParams(collective_id=collective_id), name=name,
    )(*arrs)


def _seq_pair_swap(name, collective_id, arrs):
    n = len(arrs)

    def body(*refs):
        ins, outs = refs[:n], refs[n:2 * n]
        send_sems, recv_sems = refs[2 * n:]
        x, y, c = lax.axis_index("x"), lax.axis_index("y"), lax.axis_index("c")
        barrier = pltpu.get_barrier_semaphore()
        pl.semaphore_signal(barrier, inc=1, device_id=(x, y, 1 - c), device_id_type=MESH)
        pl.semaphore_wait(barrier, 1)

        def copy(a, q):
            return pltpu.make_async_remote_copy(
                src_ref=ins[a].at[q, 1 - c], dst_ref=outs[a].at[q],
                send_sem=send_sems.at[a * 4 + q], recv_sem=recv_sems.at[a * 4 + q],
                device_id=(x, y, 1 - c), device_id_type=MESH)

        cps = [copy(a, q) for a in range(n) for q in range(4)]
        for cp in cps:
            cp.start()
        for cp in cps:
            cp.wait_recv()
        for cp in cps:
            cp.wait_send()

    return pl.kernel(
        body, out_type=[jax.ShapeDtypeStruct((4,) + a.shape[2:], a.dtype) for a in arrs],
        mesh=plsc.ScalarSubcoreMesh(axis_name="seq", num_cores=1),
        scratch_types=[pltpu.SemaphoreType.DMA((n * 4,)), pltpu.SemaphoreType.DMA((n * 4,))],
        compiler_params=pltpu.CompilerParams(collective_id=collective_id), name=name,
    )(*arrs)


def _call(body, name, grid, in_specs, out_specs, out_shape, args, scratch=()):
    return pl.pallas_call(
        body, name=name, grid=grid, in_specs=in_specs, out_specs=out_specs, out_shape=out_shape,
        scratch_shapes=list(scratch), compiler_params=_params(("arbitrary",) * len(grid)))(*args)


def _ada_fwd(c_all, w_ada_sh, b_ada_sh):
    nb, d = c_all.shape
    ncol = w_ada_sh.shape[1]

    def body(c_ref, w_ref, b_ref, mod_ref, cact_ref):
        cc = c_ref[...]
        ca = cc * jax.nn.sigmoid(cc)
        cact_ref[...] = ca
        mod_ref[...] = _dot(ca.astype(BF16), w_ref[...].astype(BF16), NN) + b_ref[...]

    return pl.pallas_call(
        body, name="ada_fwd",
        out_shape=[jax.ShapeDtypeStruct((nb, ncol), F32), jax.ShapeDtypeStruct((nb, d), F32)],
        compiler_params=_params(),
    )(c_all, w_ada_sh, b_ada_sh)


def _rms(xv):
    rstd = lax.rsqrt(jnp.mean(xv * xv, axis=-1, keepdims=True) + EPS)
    return xv * rstd, rstd


def _rms_bwd(dxhat, xhat, rstd):
    return rstd * (dxhat - xhat * jnp.mean(dxhat * xhat, axis=-1, keepdims=True))


def _colsum(v):
    return jnp.sum(v, axis=0, keepdims=True)


def _expm1(v, ev):
    series = v * (1.0 + v * (0.5 + v * (1.0 / 6.0 + v * (1.0 / 24.0 + v * (1.0 / 120.0)))))
    return jnp.where(jnp.abs(v) < 0.2, series, ev - 1.0)


def _softplus(v):
    return jnp.maximum(v, 0.0) + jnp.log1p(jnp.exp(-jnp.abs(v)))


def _gelu(v):
    t = jnp.tanh(v * (GELU_K0 + (GELU_K0 * GELU_K1) * (v * v)))
    return 0.5 * v * (1.0 + t), t


def _dgelu(v, t):
    return 0.5 * ((1.0 + t) + (v * (1.0 - t * t)) * (GELU_K0 + (3.0 * GELU_K0 * GELU_K1) * (v * v)))


def _scan_tile(a, b, x0, st, k0, reverse):
    t = a.shape[0]
    off = SUBLANES
    stage_a, stage_b = st.at[k0], st.at[k0 + 1]
    halo = slice(off + t, off + t + SUBLANES) if reverse else slice(0, SUBLANES)
    stage_a[halo, :] = jnp.ones((SUBLANES, a.shape[1]), F32)
    stage_b[halo, :] = jnp.zeros((SUBLANES, a.shape[1]), F32)
    s = 1
    while s < min(t, SUBLANES):
        stage_a[off:off + t, :] = a
        stage_b[off:off + t, :] = b
        at = off + s if reverse else off - s
        b = a * stage_b[at:at + t, :] + b
        a = a * stage_a[at:at + t, :]
        s *= 2
    while s < t:
        if reverse:
            b = jnp.concatenate([a[:t - s] * b[s:] + b[:t - s], b[t - s:]], axis=0)
            a = jnp.concatenate([a[:t - s] * a[s:], a[t - s:]], axis=0)
        else:
            b = jnp.concatenate([b[:s], a[s:] * b[:t - s] + b[s:]], axis=0)
            a = jnp.concatenate([a[:s], a[s:] * a[:t - s]], axis=0)
        s *= 2
    x = b + a * x0
    return x, (x[0:SUBLANES, :] if reverse else x[t - SUBLANES:t, :])


def _lru_gates(u, wa, wx, ba, bx, sp):
    ub = u.astype(BF16)
    r = jax.nn.sigmoid(_dot(ub, wa, NN) + ba)
    i = jax.nn.sigmoid(_dot(ub, wx, NN) + bx)
    log_a = (-RG_C * r) * sp
    a = jnp.exp(log_a)
    mult = jnp.sqrt(-_expm1(log_a, a) * (a + 1.0))
    return ub, r, i, a, mult


def _staged_shifts(stage, v, prev8, next8, downs, ups):
    t = v.shape[0]
    if prev8 is not None:
        stage[0:SUBLANES, :] = prev8
    stage[SUBLANES:SUBLANES + t, :] = v
    if next8 is not None:
        stage[SUBLANES + t:2 * SUBLANES + t, :] = next8
    return ([stage[SUBLANES - k:SUBLANES - k + t, :] for k in downs],
            [stage[SUBLANES + k:SUBLANES + k + t, :] for k in ups])


def _conv3(p, pp, w_ref, lo, stage):
    (p1, p2), _ = _staged_shifts(stage, p, pp, None, (1, 2), ())
    q = (w_ref[0:1, lo:lo + LANES] * p2 + w_ref[1:2, lo:lo + LANES] * p1) + w_ref[2:3, lo:lo + LANES] * p
    return q, p1, p2


def _conv4(xv, xp, w_ref, b_ref, lo, stage):
    (x1, x2, x3), _ = _staged_shifts(stage, xv, xp, None, (1, 2, 3), ())
    u = (((w_ref[0:1, lo:lo + LANES] * x3 + w_ref[1:2, lo:lo + LANES] * x2) + w_ref[2:3, lo:lo + LANES] * x1)
         + w_ref[3:4, lo:lo + LANES] * xv) + b_ref[:, lo:lo + LANES]
    return u, x1, x2, x3


def _mix_in_mixer_fwd(x2d, mod6, g_mix, w_in_t, conv_sc, conv_lru, conv_b, wa_bd, wx_bd, ba, bx, lam, width, tm):
    s, d = x2d.shape
    din = w_in_t.shape[0]
    nt = s // tm
    sub = min(MIX_ROWS, tm)
    nblk = width // LANES

    def body(x_ref, mod_ref, g_ref, w_ref, wsc_ref, wlru_ref, blru_ref, wa_ref, wx_ref, ba_ref, bx_ref, lam_ref,
             hn_ref, proj_ref, ymix_ref, h_ref, buf_ref, halo_ref, hc_ref, stage_ref):
        i = pl.program_id(0)

        @pl.when(i == 0)
        def _():
            buf_ref[1] = jnp.zeros((tm, din), F32)
            halo_ref[...] = jnp.zeros_like(halo_ref)

        @pl.when(i <= 1)
        def _():
            hc_ref[...] = jnp.zeros_like(hc_ref)

        def step(dst, src):
            xhat, _ = _rms(x_ref[...])
            hn = ((xhat * g_ref[...]) * (1.0 + mod_ref[1:2, :]) + mod_ref[0:1, :]).astype(BF16)
            hn_ref[...] = hn
            n_mix = (tm // sub) * nblk
            n_chunk = din // width

            def project(k):
                res = _dot(hn_ref[...], w_ref[k * width:(k + 1) * width, :], NT)
                proj_ref[:, k * width:(k + 1) * width] = res
                dst[:, k * width:(k + 1) * width] = res

            done = 0
            for half in range(tm // sub):
                r0 = half * sub
                rows = slice(r0, r0 + sub)
                for j in range(nblk):
                    lo = j * LANES
                    while done < n_chunk and done * n_mix <= (half * nblk + j) * n_chunk:
                        project(done)
                        done += 1

                    def col(p):
                        return src[rows, p * width + lo:p * width + lo + LANES]

                    def prev(p):
                        c0 = p * width + lo
                        if half == 0:
                            return halo_ref[:, c0:c0 + LANES]
                        return src[r0 - SUBLANES:r0, c0:c0 + LANES]

                    pp = col(1) * col(2)
                    q, _, _ = _conv3(pp, prev(1) * prev(2), wsc_ref, lo, stage_ref.at[0])
                    ymix_ref[rows, lo:lo + LANES] = (col(0) * q).astype(BF16)

                    u, _, _, _ = _conv4(col(4), prev(4), wlru_ref, blru_ref, lo, stage_ref.at[1])
                    sp = _softplus(-lam_ref[:, lo:lo + LANES])
                    _, r, ig, a, mult = _lru_gates(u, wa_ref[j], wx_ref[j], ba_ref[:, lo:lo + LANES],
                                                   bx_ref[:, lo:lo + LANES], sp)
                    h, ends = _scan_tile(a, mult * (ig * u), hc_ref[0:1, lo:lo + LANES], stage_ref, 2, False)
                    h_ref[rows, lo:lo + LANES] = h
                    hc_ref[0:1, lo:lo + LANES] = ends[SUBLANES - 1:SUBLANES, :]
                    gel, _ = _gelu(col(3))
                    ymix_ref[rows, width + lo:width + lo + LANES] = (gel * h).astype(BF16)
            while done < n_chunk:
                project(done)
                done += 1
            halo_ref[...] = src[tm - SUBLANES:tm, :]

        @pl.when(i % 2 == 0)
        def _():
            step(buf_ref.at[0], buf_ref.at[1])

        @pl.when(i % 2 == 1)
        def _():
            step(buf_ref.at[1], buf_ref.at[0])

    small = [conv_sc, conv_lru, conv_b, wa_bd, wx_bd, ba, bx, lam]
    cur = lambda i: (jnp.minimum(i, nt - 1), 0)
    last = lambda i: (jnp.maximum(i - 1, 0), 0)
    outs = _call(
        body, "mix_in_mixer_fwd", (nt + 1,),
        [pl.BlockSpec((tm, d), cur), _full(mod6.shape), _full(g_mix.shape), _full(w_in_t.shape)]
        + [_full(a.shape) for a in small],
        [pl.BlockSpec((tm, d), cur), pl.BlockSpec((tm, din), cur),
         pl.BlockSpec((tm, 2 * width), last), pl.BlockSpec((tm, width), last)],
        [jax.ShapeDtypeStruct((s, d), BF16), jax.ShapeDtypeStruct((s, din), F32),
         jax.ShapeDtypeStruct((s, 2 * width), BF16), jax.ShapeDtypeStruct((s, width), F32)],
        [x2d, mod6, g_mix, w_in_t, *small],
        scratch=[pltpu.VMEM((2, tm, din), F32), pltpu.VMEM((SUBLANES, din), F32), pltpu.VMEM((SUBLANES, width), F32),
                 pltpu.VMEM((4, sub + 2 * SUBLANES, LANES), F32)])
    return outs


def _mix_out_fwd(ymix, x2d, w_out, mod6, g_mlp, tm):
    s, d = x2d.shape

    def body(y_ref, x_ref, w_ref, mod_ref, g_ref, mix_ref, x2_ref, hn_ref):
        mix = _dot(y_ref[...], w_ref[...], NN)
        mix_ref[...] = mix.astype(BF16)
        x2 = x_ref[...] + mod_ref[2:3, :] * mix
        x2_ref[...] = x2
        xhat, _ = _rms(x2)
        hn_ref[...] = ((xhat * g_ref[...]) * (1.0 + mod_ref[4:5, :]) + mod_ref[3:4, :]).astype(BF16)

    tile = pl.BlockSpec((tm, d), lambda i: (i, 0))
    return _call(
        body, "mix_out_fwd", (s // tm,),
        [tile, tile, _full(w_out.shape), _full(mod6.shape), _full(g_mlp.shape)],
        [tile, tile, tile],
        [jax.ShapeDtypeStruct((s, d), BF16), jax.ShapeDtypeStruct((s, d), F32), jax.ShapeDtypeStruct((s, d), BF16)],
        [ymix, x2d, w_out, mod6, g_mlp])


def _mlp_fwd_loss(hn2, w_up_t, w_down, x2, target, mod6, g_final, tm, tk):
    s, d = hn2.shape
    f = w_up_t.shape[0]
    nk = f // tk

    def body(hn_ref, wu_ref, wd_ref, x2_hbm, t_hbm, mod_ref, g_ref, z_ref, dx3_ref, dyb_ref, st_ref,
             y_ref, x2_ref, t_ref, sems):
        i, k = pl.program_id(0), pl.program_id(1)

        def fetch():
            rows = pl.ds(pl.multiple_of(i * tm, tm), tm)
            return (pltpu.make_async_copy(x2_hbm.at[rows, :], x2_ref, sems.at[0]),
                    pltpu.make_async_copy(t_hbm.at[rows, :], t_ref, sems.at[1]))

        @pl.when(jnp.logical_and(i == 0, k == 0))
        def _():
            st_ref[...] = jnp.zeros_like(st_ref)

        @pl.when(k == 0)
        def _():
            for cp in fetch():
                cp.start()

        z = jnp.maximum(_dot(hn_ref[...], wu_ref[...], NT), 0.0)
        z_ref[...] = z.astype(BF16)
        part = _dot((z * z).astype(BF16), wd_ref[...], NN)

        @pl.when(k == 0)
        def _():
            y_ref[...] = part

        @pl.when(k > 0)
        def _():
            y_ref[...] += part

        @pl.when(k == nk - 1)
        def _():
            for cp in fetch():
                cp.wait()
            gate = mod_ref[5:6, :]
            yv = y_ref[...]
            xhat, rstd = _rms(x2_ref[...] + gate * yv)
            diff = xhat * g_ref[...] - t_ref[...]
            dyo = diff * (1.0 / d)
            dx3 = _rms_bwd(dyo * g_ref[...], xhat, rstd)
            dx3_ref[...] = dx3
            dyb_ref[...] = (gate * dx3).astype(BF16)
            st_ref[0:1, :] += _colsum(dyo * xhat)
            st_ref[1:2, :] += _colsum(dx3 * yv)
            st_ref[2:3, :] += _colsum(diff * diff)

    tile = pl.BlockSpec((tm, d), lambda i, k: (i, 0))
    wblk = pl.BlockSpec((tk, d), lambda i, k: (k, 0))
    return pl.pallas_call(
        body, name="mlp_fwd_loss", grid=(s // tm, nk),
        in_specs=[tile, wblk, wblk, ANY, ANY, _full(mod6.shape), _full(g_final.shape)],
        out_specs=[pl.BlockSpec((tm, tk), lambda i, k: (i, k)), tile, tile, _full((SUBLANES, d))],
        out_shape=[jax.ShapeDtypeStruct((s, f), BF16), jax.ShapeDtypeStruct((s, d), F32),
                   jax.ShapeDtypeStruct((s, d), BF16), jax.ShapeDtypeStruct((SUBLANES, d), F32)],
        scratch_shapes=[pltpu.VMEM((tm, d), F32), pltpu.VMEM((tm, d), F32), pltpu.VMEM((tm, d), F32),
                        pltpu.SemaphoreType.DMA((2,))],
        compiler_params=pltpu.CompilerParams(dimension_semantics=("arbitrary", "arbitrary"),
                                             vmem_limit_bytes=VMEM_LIMIT_BIG),
    )(hn2, w_up_t, w_down, x2, target, mod6, g_final)


def _mlp_bwd_dx(dyb, z, w_down, w_up_t, tm, tk):
    s, d = dyb.shape
    f = z.shape[1]

    def body(dy_ref, z_ref, wd_ref, wu_ref, dz_ref, dh_ref):
        k = pl.program_id(1)
        dz = ((2.0 * z_ref[...].astype(F32)) * _dot(dy_ref[...], wd_ref[...], NT)).astype(BF16)
        dz_ref[...] = dz
        part = _dot(dz, wu_ref[...], NN)

        @pl.when(k == 0)
        def _():
            dh_ref[...] = part

        @pl.when(k > 0)
        def _():
            dh_ref[...] += part

    return pl.pallas_call(
        body, name="mlp_bwd_dx", grid=(s // tm, f // tk),
        in_specs=[pl.BlockSpec((tm, d), lambda i, k: (i, 0)), pl.BlockSpec((tm, tk), lambda i, k: (i, k)),
                  pl.BlockSpec((tk, d), lambda i, k: (k, 0)), pl.BlockSpec((tk, d), lambda i, k: (k, 0))],
        out_specs=[pl.BlockSpec((tm, tk), lambda i, k: (i, k)), pl.BlockSpec((tm, d), lambda i, k: (i, 0))],
        out_shape=[jax.ShapeDtypeStruct((s, f), BF16), jax.ShapeDtypeStruct((s, d), F32)],
        compiler_params=_params(("parallel", "arbitrary")),
    )(dyb, z, w_down, w_up_t)


def _mlp_bwd_dw(z, dz, dyb, hn2, tm, tk):
    s, d = dyb.shape
    f = z.shape[1]

    def body(z_ref, dz_ref, dy_ref, hn_ref, gd_ref, gu_ref):
        i = pl.program_id(1)

        @pl.when(i == 0)
        def _():
            gd_ref[...] = jnp.zeros_like(gd_ref)
            gu_ref[...] = jnp.zeros_like(gu_ref)

        zf = z_ref[...].astype(F32)
        gd_ref[...] += _dot((zf * zf).astype(BF16), dy_ref[...], TN)
        gu_ref[...] += _dot(dz_ref[...], hn_ref[...], TN)

    return pl.pallas_call(
        body, name="mlp_bwd_dw", grid=(f // tk, s // tm),
        in_specs=[pl.BlockSpec((tm, tk), lambda k, i: (i, k)), pl.BlockSpec((tm, tk), lambda k, i: (i, k)),
                  pl.BlockSpec((tm, d), lambda k, i: (i, 0)), pl.BlockSpec((tm, d), lambda k, i: (i, 0))],
        out_specs=[pl.BlockSpec((tk, d), lambda k, i: (k, 0)), pl.BlockSpec((tk, d), lambda k, i: (k, 0))],
        out_shape=[jax.ShapeDtypeStruct((f, d), F32), jax.ShapeDtypeStruct((f, d), F32)],
        compiler_params=_params(("parallel", "arbitrary")),
    )(z, dz, dyb, hn2)


def _mix_out_bwd(dhn2, x2, dx3, mix, ymix, w_out, mod6, g_mlp, tm):
    s, d = x2.shape

    def body(dh_ref, x2_ref, dx3_ref, mix_ref, y_ref, w_ref, mod_ref, g_ref, dx2_ref, dym_ref, gw_ref, st_ref):
        i = pl.program_id(0)

        @pl.when(i == 0)
        def _():
            st_ref[...] = jnp.zeros_like(st_ref)
            gw_ref[...] = jnp.zeros_like(gw_ref)

        dh = dh_ref[...]
        xhat, rstd = _rms(x2_ref[...])
        dn = dh * (1.0 + mod_ref[4:5, :])
        dx2 = dx3_ref[...] + _rms_bwd(dn * g_ref[...], xhat, rstd)
        dx2_ref[...] = dx2
        st_ref[0:1, :] += _colsum(dh)
        st_ref[1:2, :] += _colsum(dh * (xhat * g_ref[...]))
        st_ref[2:3, :] += _colsum(dn * xhat)
        st_ref[3:4, :] += _colsum(dx2 * mix_ref[...].astype(F32))
        dmix = (mod_ref[2:3, :] * dx2).astype(BF16)
        dym_ref[...] = _dot(dmix, w_ref[...], NT)
        gw_ref[...] += _dot(y_ref[...], dmix, TN)

    tile = pl.BlockSpec((tm, d), lambda i: (i, 0))
    return _call(
        body, "mix_out_bwd", (s // tm,),
        [tile, tile, tile, tile, tile, _full(w_out.shape), _full(mod6.shape), _full(g_mlp.shape)],
        [tile, tile, _full((d, d)), _full((SUBLANES, d))],
        [jax.ShapeDtypeStruct((s, d), F32), jax.ShapeDtypeStruct((s, d), F32),
         jax.ShapeDtypeStruct((d, d), F32), jax.ShapeDtypeStruct((SUBLANES, d), F32)],
        [dhn2, x2, dx3, mix, ymix, w_out, mod6, g_mlp])


def _mixer_bwd(proj, dymix, h_all, conv_sc, conv_lru, conv_b, wa_bd, wx_bd, ba, bx, lam, width):
    s, din = proj.shape
    t = min(MIX_ROWS, s)
    nt = s // t
    nblk = width // LANES
    hb = t // SUBLANES
    last8 = s // SUBLANES - 1

    def body(proj_ref, projp_ref, projn_ref, dy_ref, dyn_ref, h_ref, hp_ref,
             wsc_ref, wlru_ref, blru_ref, wa_ref, wx_ref, ba_ref, bx_ref, lam_ref,
             dproj_ref, small_ref, gwa_ref, gwx_ref, an_ref, gn_ref, dun_ref, stage_ref):
        i = pl.program_id(0)

        @pl.when(i == 0)
        def _():
            small_ref[...] = jnp.zeros_like(small_ref)
            gwa_ref[...] = jnp.zeros_like(gwa_ref)
            gwx_ref[...] = jnp.zeros_like(gwx_ref)
            an_ref[...] = jnp.zeros_like(an_ref)
            gn_ref[...] = jnp.zeros_like(gn_ref)
            dun_ref[...] = jnp.zeros_like(dun_ref)

        has_prev = i < nt - 1
        has_next = i > 0
        for j in range(nblk):
            lo = j * LANES
            ls = slice(lo, lo + LANES)

            def col(p, ref=proj_ref):
                return ref[:, p * width + lo:p * width + lo + LANES]

            def prev(p):
                return jnp.where(has_prev, col(p, projp_ref), 0.0)

            def nxt(p):
                return jnp.where(has_next, col(p, projn_ref), 0.0)

            def add_row(r, v):
                small_ref[r:r + 1, ls] += _colsum(v)

            sc_b, sc_c, sc_x = col(0), col(1), col(2)
            p = sc_c * sc_x
            q, p1, p2 = _conv3(p, prev(1) * prev(2), wsc_ref, lo, stage_ref.at[0])
            dys = dy_ref[:, ls]
            dproj_ref[:, ls] = (dys * q).astype(BF16)
            dq = dys * sc_b
            dqn = jnp.where(has_next, dyn_ref[:, ls], 0.0) * nxt(0)
            _, (dq1, dq2) = _staged_shifts(stage_ref.at[1], dq, None, dqn, (), (1, 2))
            dp = (wsc_ref[2:3, ls] * dq + wsc_ref[1:2, ls] * dq1) + wsc_ref[0:1, ls] * dq2
            dproj_ref[:, width + lo:width + lo + LANES] = (dp * sc_x).astype(BF16)
            dproj_ref[:, 2 * width + lo:2 * width + lo + LANES] = (dp * sc_c).astype(BF16)
            add_row(0, dq * p2)
            add_row(1, dq * p1)
            add_row(2, dq * p)

            xv = col(4)
            u, x1, x2, x3 = _conv4(xv, prev(4), wlru_ref, blru_ref, lo, stage_ref.at[2])
            lam_v = lam_ref[:, ls]
            sp = _softplus(-lam_v)
            wa, wx = wa_ref[j], wx_ref[j]
            ub, r, ig, a, mult = _lru_gates(u, wa, wx, ba_ref[:, ls], bx_ref[:, ls], sp)
            iu = ig * u
            h = h_ref[:, ls]
            (hm1,), _ = _staged_shifts(stage_ref.at[3], h, jnp.where(has_prev, hp_ref[:, ls], 0.0), None, (1,), ())
            lyv = col(3)
            gel, th = _gelu(lyv)
            dyl = dy_ref[:, width + lo:width + lo + LANES]
            dproj_ref[:, 3 * width + lo:3 * width + lo + LANES] = (dyl * h * _dgelu(lyv, th)).astype(BF16)
            a_next = jnp.broadcast_to(an_ref[0:1, ls], (SUBLANES, LANES))
            _, (a_up,) = _staged_shifts(stage_ref.at[4], a, None, a_next, (), (1,))
            g, _ = _scan_tile(a_up, dyl * gel, gn_ref[0:1, ls], stage_ref, 5, True)
            an_ref[0:1, ls] = a[0:1, :]
            gn_ref[0:1, ls] = g[0:1, :]
            da = g * hm1
            dmult = g * iu
            diu = g * mult
            dlog_a = da * a - dmult * ((a * a) / mult)
            dpre_a = (dlog_a * (-RG_C * sp)) * (r * (1.0 - r))
            dpre_x = (diu * u) * (ig * (1.0 - ig))
            dab, dxb = dpre_a.astype(BF16), dpre_x.astype(BF16)
            du = diu * ig + _dot(dab, wa, NT) + _dot(dxb, wx, NT)
            gwa_ref[j] += _dot(ub, dab, TN)
            gwx_ref[j] += _dot(ub, dxb, TN)
            dun = dun_ref[:, ls]
            dun_ref[:, ls] = du[0:SUBLANES, :]
            _, (du1, du2, du3) = _staged_shifts(stage_ref.at[7], du, None, dun, (), (1, 2, 3))
            dlx = (((wlru_ref[3:4, ls] * du + wlru_ref[2:3, ls] * du1) + wlru_ref[1:2, ls] * du2)
                   + wlru_ref[0:1, ls] * du3)
            dproj_ref[:, 4 * width + lo:4 * width + lo + LANES] = dlx.astype(BF16)
            add_row(3, du * x3)
            add_row(4, du * x2)
            add_row(5, du * x1)
            add_row(6, du * xv)
            add_row(7, du)
            add_row(8, dpre_a)
            add_row(9, dpre_x)
            add_row(10, (dlog_a * (RG_C * r)) * jax.nn.sigmoid(-lam_v))

    small = [conv_sc, conv_lru, conv_b, wa_bd, wx_bd, ba, bx, lam]
    rev = lambda i: nt - 1 - i
    return _call(
        body, "mixer_bwd", (nt,),
        [pl.BlockSpec((t, din), lambda i: (rev(i), 0)),
         pl.BlockSpec((SUBLANES, din), lambda i: (jnp.maximum(rev(i) * hb - 1, 0), 0)),
         pl.BlockSpec((SUBLANES, din), lambda i: (jnp.minimum((rev(i) + 1) * hb, last8), 0)),
         pl.BlockSpec((t, 2 * width), lambda i: (rev(i), 0)),
         pl.BlockSpec((SUBLANES, 2 * width), lambda i: (jnp.minimum((rev(i) + 1) * hb, last8), 0)),
         pl.BlockSpec((t, width), lambda i: (rev(i), 0)),
         pl.BlockSpec((SUBLANES, width), lambda i: (jnp.maximum(rev(i) * hb - 1, 0), 0))]
        + [_full(a.shape) for a in small],
        [pl.BlockSpec((t, din), lambda i: (rev(i), 0)), _full((2 * SUBLANES, width)),
         _full(wa_bd.shape), _full(wx_bd.shape)],
        [jax.ShapeDtypeStruct((s, din), BF16), jax.ShapeDtypeStruct((2 * SUBLANES, width), F32),
         jax.ShapeDtypeStruct(wa_bd.shape, F32), jax.ShapeDtypeStruct(wx_bd.shape, F32)],
        [proj, proj, proj, dymix, dymix, h_all, h_all, *small],
        scratch=[pltpu.VMEM((SUBLANES, width), F32), pltpu.VMEM((SUBLANES, width), F32),
                 pltpu.VMEM((SUBLANES, width), F32), pltpu.VMEM((8, t + 2 * SUBLANES, LANES), F32)])


def _mix_in_bwd_dx(dproj, x2d, dx2, w_in_t, mod6, g_mix, tm):
    s, d = x2d.shape
    din = dproj.shape[1]

    def body(dp_ref, x_ref, dx2_ref, w_ref, mod_ref, g_ref, gx_ref, st_ref):
        i = pl.program_id(0)

        @pl.when(i == 0)
        def _():
            st_ref[...] = jnp.zeros_like(st_ref)

        dh = _dot(dp_ref[...], w_ref[...], NN)
        xhat, rstd = _rms(x_ref[...])
        dn = dh * (1.0 + mod_ref[1:2, :])
        gx_ref[...] = dx2_ref[...] + _rms_bwd(dn * g_ref[...], xhat, rstd)
        st_ref[0:1, :] += _colsum(dh)
        st_ref[1:2, :] += _colsum(dh * (xhat * g_ref[...]))
        st_ref[2:3, :] += _colsum(dn * xhat)

    tile = pl.BlockSpec((tm, d), lambda i: (i, 0))
    return _call(
        body, "mix_in_bwd_dx", (s // tm,),
        [pl.BlockSpec((tm, din), lambda i: (i, 0)), tile, tile, _full(w_in_t.shape), _full(mod6.shape),
         _full(g_mix.shape)],
        [tile, _full((SUBLANES, d))],
        [jax.ShapeDtypeStruct((s, d), F32), jax.ShapeDtypeStruct((SUBLANES, d), F32)],
        [dproj, x2d, dx2, w_in_t, mod6, g_mix])


def _mix_in_bwd_dw(dproj, hn1, tm, tn):
    s, d = hn1.shape
    din = dproj.shape[1]

    def body(dp_ref, hn_ref, gw_ref):
        i = pl.program_id(1)

        @pl.when(i == 0)
        def _():
            gw_ref[...] = jnp.zeros_like(gw_ref)

        gw_ref[...] += _dot(dp_ref[...], hn_ref[...], TN)

    return _call(
        body, "mix_in_bwd_dw", (din // tn, s // tm),
        [pl.BlockSpec((tm, tn), lambda p, i: (i, p)), pl.BlockSpec((tm, d), lambda p, i: (i, 0))],
        [pl.BlockSpec((tn, d), lambda p, i: (p, 0))],
        [jax.ShapeDtypeStruct((din, d), F32)],
        [dproj, hn1])


def _adamw(w, g, m, v):
    m = ADAM_B1 * m + (1.0 - ADAM_B1) * g
    v = ADAM_B2 * v + (1.0 - ADAM_B2) * (g * g)
    m_hat = m / (1.0 - ADAM_B1 ** ADAM_STEP)
    v_hat = v / (1.0 - ADAM_B2 ** ADAM_STEP)
    delta = -ADAM_LR * (m_hat / (jnp.sqrt(v_hat) + ADAM_EPS) + ADAM_WD * w)
    return delta, m, v


def _pair_sum(g4s, h4s, core_chip, tr, name):
    na = len(g4s)
    _, _, r, n = g4s[0].shape

    def body(sc_ref, *refs):
        q = pl.program_id(1)
        for a in range(na):
            g_ref, h_ref = refs[2 * a], refs[2 * a + 1]
            sb_ref, own_ref = refs[2 * na + 2 * a], refs[2 * na + 2 * a + 1]
            ssum = g_ref[...] + h_ref[...]
            sb_ref[...] = ssum.astype(BF16)

            @pl.when(q == sc_ref[1])
            def _():
                own_ref[...] = ssum

    grid_spec = pltpu.PrefetchScalarGridSpec(
        num_scalar_prefetch=1, grid=(r // tr, 4),
        in_specs=[pl.BlockSpec((None, None, tr, n), lambda i, q, sc: (q, sc[0], i, 0)),
                  pl.BlockSpec((None, tr, n), lambda i, q, sc: (q, i, 0))] * na,
        out_specs=[pl.BlockSpec((None, tr, n), lambda i, q, sc: (q, i, 0)),
                   pl.BlockSpec((tr, n), lambda i, q, sc: (i, 0))] * na)
    outs = pl.pallas_call(
        body, name=name, grid_spec=grid_spec,
        out_shape=[jax.ShapeDtypeStruct((4, r, n), BF16), jax.ShapeDtypeStruct((r, n), F32)] * na,
        compiler_params=_params(("parallel", "arbitrary")),
    )(core_chip, *[x for pair in zip(g4s, h4s) for x in pair])
    return [(outs[2 * a], outs[2 * a + 1]) for a in range(na)]


def _sum4_adam(own, parts, w, m, v, tr, name, transposed):
    r, n = own.shape
    rows, cols = w.shape

    def body(o_ref, p_ref, w_ref, m_ref, v_ref, g_ref, d_ref, nm_ref, nv_ref):
        g = o_ref[...]
        for k in range(3):
            g = g + p_ref[k].astype(F32)
        if transposed:
            g = g.T
        g_ref[...] = g
        d_ref[...], nm_ref[...], nv_ref[...] = _adamw(w_ref[...], g, m_ref[...], v_ref[...])

    if transposed:
        g_specs = [pl.BlockSpec((r, tr), lambda i: (0, i)), pl.BlockSpec((3, r, tr), lambda i: (0, 0, i))]
    else:
        g_specs = [pl.BlockSpec((tr, n), lambda i: (i, 0)), pl.BlockSpec((3, tr, n), lambda i: (0, i, 0))]
    tile = pl.BlockSpec((tr, cols), lambda i: (i, 0))
    return pl.pallas_call(
        body, name=name, grid=(rows // tr,),
        in_specs=g_specs + [tile] * 3, out_specs=[tile] * 4,
        out_shape=[jax.ShapeDtypeStruct((rows, cols), F32)] * 4,
        compiler_params=_params(("parallel",)),
    )(own, parts, w, m, v)


def _sum8(parts, tr, name):
    _, rows, n = parts.shape

    def body(p_ref, o_ref):
        acc = p_ref[0]
        for k in range(1, N_DEV):
            acc = acc + p_ref[k]
        o_ref[...] = acc

    return pl.pallas_call(
        body, name=name, grid=(rows // tr,),
        in_specs=[pl.BlockSpec((N_DEV, tr, n), lambda i: (0, i, 0))],
        out_specs=pl.BlockSpec((tr, n), lambda i: (i, 0)),
        out_shape=jax.ShapeDtypeStruct((rows, n), F32),
        compiler_params=_params(("parallel",)),
    )(parts)


def _ada_bwd_adam(cact_t, dmod_cols, w, m, v, tr):
    rows, n = w.shape

    def body(c_ref, d_ref, w_ref, m_ref, v_ref, g_ref, dl_ref, nm_ref, nv_ref):
        def term(b):
            return c_ref[b].astype(BF16).astype(F32) * d_ref[b:b + 1, :].astype(BF16).astype(F32)

        g = term(0)
        for b in range(1, N_DEV):
            g = g + term(b)
        g_ref[...] = g
        dl_ref[...], nm_ref[...], nv_ref[...] = _adamw(w_ref[...], g, m_ref[...], v_ref[...])

    tile = pl.BlockSpec((tr, n), lambda i: (i, 0))
    return pl.pallas_call(
        body, name="ada_bwd_adam", grid=(rows // tr,),
        in_specs=[pl.BlockSpec((N_DEV, tr, 1), lambda i: (0, i, 0)), _full(dmod_cols.shape), tile, tile, tile],
        out_specs=[tile] * 4,
        out_shape=[jax.ShapeDtypeStruct((rows, n), F32)] * 4,
        compiler_params=_params(("parallel",)),
    )(cact_t, dmod_cols, w, m, v)


def _adam_small(ws, gs, ms, vs):
    n = len(ws)

    def body(*refs):
        w_r, g_r, m_r, v_r = refs[:n], refs[n:2 * n], refs[2 * n:3 * n], refs[3 * n:4 * n]
        d_r, nm_r, nv_r = refs[4 * n:5 * n], refs[5 * n:6 * n], refs[6 * n:7 * n]
        for k in range(n):
            d_r[k][...], nm_r[k][...], nv_r[k][...] = _adamw(w_r[k][...], g_r[k][...], m_r[k][...], v_r[k][...])

    shapes = [jax.ShapeDtypeStruct(w.shape, F32) for w in ws]
    outs = pl.pallas_call(
        body, name="adam_small", out_shape=shapes * 3, compiler_params=_params(),
    )(*ws, *gs, *ms, *vs)
    return outs[:n], outs[n:2 * n], outs[2 * n:]


def _block_diag(w):
    h, hd, _ = w.shape
    per = LANES // hd
    eye = jnp.eye(per, dtype=w.dtype)
    w5 = w.reshape(h // per, per, hd, 1, hd) * eye[None, :, None, :, None]
    return w5.reshape(h // per, LANES, LANES)


def _block_diag_grad(g, h, hd):
    per = LANES // hd
    g5 = g.reshape(h // per, per, hd, per, hd)
    return jnp.stack([g5[:, a, :, a, :] for a in range(per)], axis=1).reshape(h, hd, hd)


def kernel(x, c, w_ada, b_ada, g_mix, w_in, conv_w_sc, conv_w_lru, conv_b_lru, w_rg_a, b_rg_a, w_rg_x, b_rg_x, lru_lambda, w_out, g_mlp, w_up, w_down, g_final, loss_target, m_w_ada, m_b_ada, m_g_mix, m_w_in, m_conv_w_sc, m_conv_w_lru, m_conv_b_lru, m_w_rg_a, m_b_rg_a, m_w_rg_x, m_b_rg_x, m_lru_lambda, m_w_out, m_g_mlp, m_w_up, m_w_down, m_g_final, v_w_ada, v_b_ada, v_g_mix, v_w_in, v_conv_w_sc, v_conv_w_lru, v_conv_b_lru, v_w_rg_a, v_b_rg_a, v_w_rg_x, v_b_rg_x, v_lru_lambda, v_w_out, v_g_mlp, v_w_up, v_w_down, v_g_final):
    s, d = x.shape[1], x.shape[2]
    width = conv_b_lru.shape[1]
    heads, hd = w_rg_a.shape[1], w_rg_a.shape[2]
    f = w_down.shape[1] * N_DEV
    n_ada = w_ada.shape[2]
    csh = conv_w_sc.shape[2]
    me = 4 * lax.axis_index("x") + 2 * lax.axis_index("y") + lax.axis_index("c")
    tm = min(512, s)
    tm_mlp = min(1024, s)
    tk = 512

    x2d = x[0]
    tgt = loss_target[0]

    pay = jnp.zeros((SUBLANES, d), F32)
    pay = pay.at[0:1, :].set(c)
    pay = pay.at[1:4, 0:csh].set(conv_w_sc[0])
    pay = pay.at[4:8, 0:csh].set(conv_w_lru[0])
    w_in_t_sh = w_in[0].T.astype(BF16)
    w_up_t_sh = w_up[0].T.astype(BF16)
    w_out_sh = w_out[0].astype(BF16)
    w_down_sh = w_down[0].astype(BF16)
    pay_all, w_in_t = _gather2("gather_in", [pay, w_in_t_sh])
    w_in_t = w_in_t.reshape(-1, d)
    c_all = pay_all[:, 0, :]
    conv_sc = pay_all[:, 1:4, 0:csh].transpose(1, 0, 2).reshape(3, width)
    conv_lru = pay_all[:, 4:8, 0:csh].transpose(1, 0, 2).reshape(4, width)

    b_ada_sh = lax.dynamic_slice(b_ada, (0, me * n_ada), (1, n_ada))
    mod_cols, c_act = _ada_fwd(c_all, w_ada[0], b_ada_sh)
    (mod_rows,) = _exchange("scatter_mod", [], [mod_cols.reshape(N_DEV, 1, n_ada)])
    mod_rows, w_out_sh, w_up_t_sh, w_down_sh = lax.optimization_barrier((mod_rows, w_out_sh, w_up_t_sh, w_down_sh))
    (w_out_g,) = _seq_gather2("gather_w_out", 1, [w_out_sh])
    w_up_g, w_down_g = _seq_gather2("gather_mlp_weights", 2, [w_up_t_sh, w_down_sh])
    mod6 = jnp.zeros((SUBLANES, d), F32).at[0:6, :].set(mod_rows.reshape(6, d))

    wa_bd = _block_diag(w_rg_a[0]).astype(BF16)
    wx_bd = _block_diag(w_rg_x[0]).astype(BF16)
    ba = b_rg_a.reshape(1, width)
    bx = b_rg_x.reshape(1, width)
    g_fin = g_final.reshape(1, d)

    hn1, proj, ymix, h_all = _mix_in_mixer_fwd(x2d, mod6, g_mix, w_in_t, conv_sc, conv_lru, conv_b_lru,
                                               wa_bd, wx_bd, ba, bx, lru_lambda, width, tm)
    w_out_b = w_out_g.reshape(-1, d)
    mix, x2, hn2 = _mix_out_fwd(ymix, x2d, w_out_b, mod6, g_mlp, tm_mlp)
    w_up_t = w_up_g.reshape(-1, d)
    w_down_b = w_down_g.reshape(-1, d)
    z, dx3, dyb, st_fin = _mlp_fwd_loss(hn2, w_up_t, w_down_b, x2, tgt, mod6, g_fin, tm_mlp, 2 * tk)

    core_chip = jnp.stack([lax.axis_index("c"), 2 * lax.axis_index("x") + lax.axis_index("y")]).astype(jnp.int32)
    dz, dhn2 = _mlp_bwd_dx(dyb, z, w_down_b, w_up_t, tm_mlp, 2 * tk)
    g_down, g_up_t = _mlp_bwd_dw(z, dz, dyb, hn2, tm_mlp, 2 * tk)
    g_up4, g_down4 = g_up_t.reshape(4, 2, -1, d), g_down.reshape(4, 2, -1, d)
    h_up, h_down = _seq_pair_swap("swap_mlp_grads", 7, [g_up4, g_down4])
    dx2, dymix, g_out, st_out = _mix_out_bwd(dhn2, x2, dx3, mix, ymix, w_out_b, mod6, g_mlp, tm)
    h_up, h_down, g_out = lax.optimization_barrier((h_up, h_down, g_out))
    (sb_up, own_up), (sb_down, own_down) = _pair_sum([g_up4, g_down4], [h_up, h_down], core_chip, 256, "pair_sum_mlp")
    g_out4 = g_out.reshape(4, 2, -1, d)
    (h_out,) = _seq_pair_swap("swap_w_out_grad", 8, [g_out4])
    p_up, p_down = _seq_chip_exchange("exchange_mlp_grads", 3, [sb_up, sb_down])
    dproj, g_small, g_wa, g_wx = _mixer_bwd(
        proj, dymix, h_all, conv_sc, conv_lru, conv_b_lru, wa_bd, wx_bd, ba, bx, lru_lambda, width)
    h_out, dproj = lax.optimization_barrier((h_out, dproj))
    ((sb_out, own_out),) = _pair_sum([g_out4], [h_out], core_chip, g_out4.shape[2], "pair_sum_w_out")
    (p_out,) = _seq_chip_exchange("exchange_w_out_grad", 4, [sb_out])
    grad_x, st_in = _mix_in_bwd_dx(dproj, x2d, dx2, w_in_t, mod6, g_mix, tm)

    small = jnp.concatenate([
        st_in[0:2], st_out[3:4], st_out[0:2], st_fin[1:2],
        st_in[2:3], st_out[2:3], st_fin[0:1],
        jnp.concatenate([g_small[7:8], g_small[10:11]], axis=1),
        jnp.concatenate([g_small[8:9], g_small[9:10]], axis=1),
        jnp.concatenate([jnp.concatenate([g_small[0:3], jnp.zeros((1, width), F32)], axis=0), g_small[3:7]], axis=1),
        st_fin[2:3],
        _block_diag_grad(g_wa, heads, hd).reshape(-1, d),
        _block_diag_grad(g_wx, heads, hd).reshape(-1, d),
    ], axis=0)

    (small_all,) = _seq_gather2("gather_small_grads", 5, [small])
    g_in_t, = _mix_in_bwd_dw(dproj, hn1, min(2048, s), dproj.shape[1] // 2)
    g_in4 = g_in_t.reshape(4, 2, -1, d)
    (h_in,) = _seq_pair_swap("swap_w_in_grad", 9, [g_in4])
    p_up, p_down, p_out, small_all, g_in_t = lax.optimization_barrier((p_up, p_down, p_out, small_all, g_in_t))

    ad_up = _sum4_adam(own_up, p_up, w_up[0], m_w_up[0], v_w_up[0], 256, "adam_w_up", True)
    h_in, ad_up = lax.optimization_barrier((h_in, ad_up))
    ((sb_in, own_in),) = _pair_sum([g_in4], [h_in], core_chip, g_in4.shape[2], "pair_sum_w_in")
    (p_in,) = _seq_chip_exchange("exchange_w_in_grad", 6, [sb_in])
    ad_out = _sum4_adam(own_out, p_out, w_out[0], m_w_out[0], v_w_out[0], w_out.shape[1], "adam_w_out", False)
    ad_down = _sum4_adam(own_down, p_down, w_down[0], m_w_down[0], v_w_down[0], 256, "adam_w_down", False)

    gsum = _sum8(small_all, SMALL_ROWS, "sum_small")
    loss = (0.5 / d) * jnp.sum(gsum[15])
    dmod_cols = lax.dynamic_slice(small_all[:, 0:6, :].reshape(N_DEV, 6 * d), (0, me * n_ada), (N_DEV, n_ada))
    g_ada, d_ada, nm_ada, nv_ada = _ada_bwd_adam(c_act[:, :, None], dmod_cols, w_ada[0], m_w_ada[0], v_w_ada[0], 256)

    g_conv = lax.dynamic_slice(gsum[11:15, 0:width], (0, me * csh), (4, csh))
    g_conv_l = lax.dynamic_slice(gsum[11:15, width:2 * width], (0, me * csh), (4, csh))
    small_g = [
        gsum[0:6].reshape(1, 6 * d),
        gsum[6:7],
        g_conv[0:3].reshape(1, 3, csh),
        g_conv_l.reshape(1, 4, csh),
        gsum[9:10, 0:width],
        gsum[16:48].reshape(1, heads, hd, hd),
        gsum[10:11, 0:width].reshape(1, heads, hd),
        gsum[48:80].reshape(1, heads, hd, hd),
        gsum[10:11, width:].reshape(1, heads, hd),
        gsum[9:10, width:],
        gsum[7:8],
        gsum[8],
    ]
    small_w = [b_ada, g_mix, conv_w_sc, conv_w_lru, conv_b_lru, w_rg_a, b_rg_a, w_rg_x, b_rg_x, lru_lambda, g_mlp, g_final]
    small_m = [m_b_ada, m_g_mix, m_conv_w_sc, m_conv_w_lru, m_conv_b_lru, m_w_rg_a, m_b_rg_a, m_w_rg_x, m_b_rg_x,
               m_lru_lambda, m_g_mlp, m_g_final]
    small_v = [v_b_ada, v_g_mix, v_conv_w_sc, v_conv_w_lru, v_conv_b_lru, v_w_rg_a, v_b_rg_a, v_w_rg_x, v_b_rg_x,
               v_lru_lambda, v_g_mlp, v_g_final]
    sd, snm, snv = _adam_small(small_w, small_g, small_m, small_v)
    p_in, ad_out, ad_down, (g_ada, d_ada, nm_ada, nv_ada), sd = lax.optimization_barrier(
        (p_in, ad_out, ad_down, (g_ada, d_ada, nm_ada, nv_ada), sd))
    ad_in = _sum4_adam(own_in, p_in, w_in[0].T, m_w_in[0].T, v_w_in[0].T, own_in.shape[0], "adam_w_in", False)
    ad_in = [a.T for a in ad_in]

    def order(ada, w_in_, w_out_, w_up_, w_down_, sm):
        return [ada[None], sm[0], sm[1], w_in_[None], sm[2], sm[3], sm[4], sm[5], sm[6], sm[7], sm[8], sm[9],
                w_out_[None], sm[10], w_up_[None], w_down_[None], sm[11]]

    grads = order(g_ada, ad_in[0], ad_out[0], ad_up[0], ad_down[0], small_g)
    deltas = order(d_ada, ad_in[1], ad_out[1], ad_up[1], ad_down[1], sd)
    new_m = order(nm_ada, ad_in[2], ad_out[2], ad_up[2], ad_down[2], snm)
    new_v = order(nv_ada, ad_in[3], ad_out[3], ad_up[3], ad_down[3], snv)
    return (loss, grad_x[None], *grads, *deltas, *new_m, *new_v)
```

```python
import jax
import jax.numpy as jnp
from jax import lax
from jax.experimental import pallas as pl
from jax.experimental.pallas import tpu as pltpu
from jax.experimental.pallas import tpu_sc as plsc

F32 = jnp.float32
BF16 = jnp.bfloat16
N_DEV = 8
EPS = 1e-6
RG_C = 8.0
GELU_K0 = 0.7978845608028654
GELU_K1 = 0.044715
ADAM_LR = 0.001
ADAM_B1 = 0.9
ADAM_B2 = 0.999
ADAM_EPS = 1e-08
ADAM_WD = 0.01
ADAM_STEP = 10
LANES = 128
SUBLANES = 8
VMEM_LIMIT = 52 * 1024 * 1024
VMEM_LIMIT_BIG = 58 * 1024 * 1024
MIX_ROWS = 256
SMALL_ROWS = 80

MESH = pl.DeviceIdType.MESH
ANY = pl.BlockSpec(memory_space=pl.ANY)
NN = ((1,), (0,))
NT = ((1,), (1,))
TN = ((0,), (0,))


def _dot(a, b, dims):
    return lax.dot_general(a, b, (dims, ((), ())), preferred_element_type=F32)


def _params(sem=None):
    return pltpu.CompilerParams(dimension_semantics=sem, vmem_limit_bytes=VMEM_LIMIT)


def _full(shape):
    nd = len(shape)
    return pl.BlockSpec(shape, lambda *_: (0,) * nd)


def _exchange(name, gathers, scatters):
    n_g = len(gathers)
    arrs = list(gathers) + list(scatters)
    n = len(arrs)
    out_shape = [jax.ShapeDtypeStruct((N_DEV,) + a.shape, a.dtype) for a in gathers]
    out_shape += [jax.ShapeDtypeStruct(a.shape, a.dtype) for a in scatters]

    def body(*refs):
        ins, outs = refs[:n], refs[n:2 * n]
        send_sems, recv_sems, local_sems = refs[2 * n:]
        x, y, c = lax.axis_index("x"), lax.axis_index("y"), lax.axis_index("c")
        me = 4 * x + 2 * y + c

        def src(a, dev):
            return ins[a] if a < n_g else ins[a].at[dev]

        def peer_of(k):
            px = 1 - x if (k >> 2) & 1 else x
            py = 1 - y if (k >> 1) & 1 else y
            pc = 1 - c if k & 1 else c
            return (px, py, pc), 4 * px + 2 * py + pc

        local = [pltpu.make_async_copy(src(a, me), outs[a].at[me], local_sems.at[a]) for a in range(n)]
        for cp in local:
            cp.start()
        sends = []
        for k in range(1, N_DEV):
            peer, pidx = peer_of(k)
            for a in range(n):
                cp = pltpu.make_async_remote_copy(
                    src_ref=src(a, pidx), dst_ref=outs[a].at[me],
                    send_sem=send_sems.at[a * (N_DEV - 1) + k - 1], recv_sem=recv_sems.at[a * (N_DEV - 1) + k - 1],
                    device_id=peer, device_id_type=MESH)
                cp.start()
                sends.append(cp)
        for k in range(1, N_DEV):
            peer, pidx = peer_of(k)
            for a in range(n):
                pltpu.make_async_remote_copy(
                    src_ref=src(a, pidx), dst_ref=outs[a].at[pidx],
                    send_sem=send_sems.at[a * (N_DEV - 1) + k - 1], recv_sem=recv_sems.at[a * (N_DEV - 1) + k - 1],
                    device_id=peer, device_id_type=MESH).wait_recv()
        for cp in sends:
            cp.wait_send()
        for cp in local:
            cp.wait()

    return pl.pallas_call(
        body, name=name, out_shape=out_shape,
        in_specs=[ANY] * n, out_specs=[ANY] * n,
        scratch_shapes=[pltpu.SemaphoreType.DMA((n * (N_DEV - 1),)),
                        pltpu.SemaphoreType.DMA((n * (N_DEV - 1),)),
                        pltpu.SemaphoreType.DMA((n,))],
    )(*arrs)


def _gather2(name, arrs):
    n = len(arrs)
    per = 7
    out_shape = [jax.ShapeDtypeStruct((N_DEV,) + a.shape, a.dtype) for a in arrs]

    def body(*refs):
        ins, outs = refs[:n], refs[n:2 * n]
        send_sems, recv_sems, local_sems = refs[2 * n:]
        x, y, c = lax.axis_index("x"), lax.axis_index("y"), lax.axis_index("c")
        sib = (x, y, 1 - c)
        chips = [(1 - x, y), (x, 1 - y), (1 - x, 1 - y)]

        def slot(a, px, py, pc):
            return outs[a].at[4 * px + 2 * py + pc]

        def copy(a, k, block, to, src=None):
            return pltpu.make_async_remote_copy(
                src_ref=slot(a, *block) if src is None else src, dst_ref=slot(a, *block),
                send_sem=send_sems.at[a * per + k], recv_sem=recv_sems.at[a * per + k],
                device_id=to, device_id_type=MESH)

        local = [pltpu.make_async_copy(ins[a], slot(a, x, y, c), local_sems.at[a]) for a in range(n)]
        for cp in local:
            cp.start()
        first = []
        for a in range(n):
            first += [copy(a, 1 + j, (x, y, c), (*chip, c), src=ins[a]) for j, chip in enumerate(chips)]
        for a in range(n):
            first.append(copy(a, 0, (x, y, c), sib, src=ins[a]))
        for cp in first:
            cp.start()
        passed = []
        for a in range(n):
            for j, chip in enumerate(chips):
                copy(a, 1 + j, (*chip, c), (x, y, c)).wait_recv()
                cp = copy(a, 4 + j, (*chip, c), sib)
                cp.start()
                passed.append(cp)
        for a in range(n):
            copy(a, 0, sib, (x, y, c)).wait_recv()
            for j, chip in enumerate(chips):
                copy(a, 4 + j, (*chip, 1 - c), (x, y, c)).wait_recv()
        for cp in first + passed:
            cp.wait_send()
        for cp in local:
            cp.wait()

    return pl.pallas_call(
        body, name=name, out_shape=out_shape,
        in_specs=[ANY] * n, out_specs=[ANY] * n,
        scratch_shapes=[pltpu.SemaphoreType.DMA((n * per,)), pltpu.SemaphoreType.DMA((n * per,)),
                        pltpu.SemaphoreType.DMA((n,))],
    )(*arrs)


def _seq_gather2(name, collective_id, arrs):
    n = len(arrs)
    per = 7

    def body(*refs):
        ins, outs = refs[:n], refs[n:2 * n]
        send_sems, recv_sems, local_sems = refs[2 * n:]
        x, y, c = lax.axis_index("x"), lax.axis_index("y"), lax.axis_index("c")
        sib = (x, y, 1 - c)
        chips = [(1 - x, y), (x, 1 - y), (1 - x, 1 - y)]
        barrier = pltpu.get_barrier_semaphore()
        for peer in [sib] + [(*chip, c) for chip in chips]:
            pl.semaphore_signal(barrier, inc=1, device_id=peer, device_id_type=MESH)
        pl.semaphore_wait(barrier, 4)

        def slot(a, px, py, pc):
            return outs[a].at[4 * px + 2 * py + pc]

        def copy(a, k, block, to, src=None):
            return pltpu.make_async_remote_copy(
                src_ref=slot(a, *block) if src is None else src, dst_ref=slot(a, *block),
                send_sem=send_sems.at[a * per + k], recv_sem=recv_sems.at[a * per + k],
                device_id=to, device_id_type=MESH)

        local = [pltpu.make_async_copy(ins[a], slot(a, x, y, c), local_sems.at[a]) for a in range(n)]
        for cp in local:
            cp.start()
        first = []
        for a in range(n):
            first += [copy(a, 1 + j, (x, y, c), (*chip, c), src=ins[a]) for j, chip in enumerate(chips)]
        for a in range(n):
            first.append(copy(a, 0, (x, y, c), sib, src=ins[a]))
        for cp in first:
            cp.start()
        passed = []
        for a in range(n):
            for j, chip in enumerate(chips):
                copy(a, 1 + j, (*chip, c), (x, y, c)).wait_recv()
                cp = copy(a, 4 + j, (*chip, c), sib)
                cp.start()
                passed.append(cp)
        for a in range(n):
            copy(a, 0, sib, (x, y, c)).wait_recv()
            for j, chip in enumerate(chips):
                copy(a, 4 + j, (*chip, 1 - c), (x, y, c)).wait_recv()
        for cp in first + passed:
            cp.wait_send()
        for cp in local:
            cp.wait()

    return pl.kernel(
        body, out_type=[jax.ShapeDtypeStruct((N_DEV,) + a.shape, a.dtype) for a in arrs],
        mesh=plsc.ScalarSubcoreMesh(axis_name="seq", num_cores=1),
        scratch_types=[pltpu.SemaphoreType.DMA((n * per,)), pltpu.SemaphoreType.DMA((n * per,)),
                       pltpu.SemaphoreType.DMA((n,))],
        compiler_params=pltpu.CompilerParams(collective_id=collective_id), name=name,
    )(*arrs)


def _seq_chip_exchange(name, collective_id, arrs):
    n = len(arrs)

    def body(*refs):
        ins, outs = refs[:n], refs[n:2 * n]
        send_sems, recv_sems = refs[2 * n:]
        x, y, c = lax.axis_index("x"), lax.axis_index("y"), lax.axis_index("c")

        def peer(k):
            return (1 - x if (k >> 1) & 1 else x), (1 - y if k & 1 else y)

        barrier = pltpu.get_barrier_semaphore()
        for k in (1, 2, 3):
            pl.semaphore_signal(barrier, inc=1, device_id=(*peer(k), c), device_id_type=MESH)
        pl.semaphore_wait(barrier, 3)

        def copy(a, k):
            px, py = peer(k)
            return pltpu.make_async_remote_copy(
                src_ref=ins[a].at[2 * px + py], dst_ref=outs[a].at[k - 1],
                send_sem=send_sems.at[a * 3 + k - 1], recv_sem=recv_sems.at[a * 3 + k - 1],
                device_id=(px, py, c), device_id_type=MESH)

        cps = [copy(a, k) for a in range(n) for k in (1, 2, 3)]
        for cp in cps:
            cp.start()
        for cp in cps:
            cp.wait_recv()
        for cp in cps:
            cp.wait_send()

    return pl.kernel(
        body, out_type=[jax.ShapeDtypeStruct((3,) + a.shape[1:], a.dtype) for a in arrs],
        mesh=plsc.ScalarSubcoreMesh(axis_name="seq", num_cores=1),
        scratch_types=[pltpu.SemaphoreType.DMA((n * 3,)), pltpu.SemaphoreType.DMA((n * 3,))],
        compiler_params=pltpu.CompilerParams(collective_id=collective_id), name=name,
    )(*arrs)


def _seq_pair_swap(name, collective_id, arrs):
    n = len(arrs)

    def body(*refs):
        ins, outs = refs[:n], refs[n:2 * n]
        send_sems, recv_sems = refs[2 * n:]
        x, y, c = lax.axis_index("x"), lax.axis_index("y"), lax.axis_index("c")
        barrier = pltpu.get_barrier_semaphore()
        pl.semaphore_signal(barrier, inc=1, device_id=(x, y, 1 - c), device_id_type=MESH)
        pl.semaphore_wait(barrier, 1)

        def copy(a, q):
            return pltpu.make_async_remote_copy(
                src_ref=ins[a].at[q, 1 - c], dst_ref=outs[a].at[q],
                send_sem=send_sems.at[a * 4 + q], recv_sem=recv_sems.at[a * 4 + q],
                device_id=(x, y, 1 - c), device_id_type=MESH)

        cps = [copy(a, q) for a in range(n) for q in range(4)]
        for cp in cps:
            cp.start()
        for cp in cps:
            cp.wait_recv()
        for cp in cps:
            cp.wait_send()

    return pl.kernel(
        body, out_type=[jax.ShapeDtypeStruct((4,) + a.shape[2:], a.dtype) for a in arrs],
        mesh=plsc.ScalarSubcoreMesh(axis_name="seq", num_cores=1),
        scratch_types=[pltpu.SemaphoreType.DMA((n * 4,)), pltpu.SemaphoreType.DMA((n * 4,))],
        compiler_params=pltpu.CompilerParams(collective_id=collective_id), name=name,
    )(*arrs)


def _call(body, name, grid, in_specs, out_specs, out_shape, args, scratch=()):
    return pl.pallas_call(
        body, name=name, grid=grid, in_specs=in_specs, out_specs=out_specs, out_shape=out_shape,
        scratch_shapes=list(scratch), compiler_params=_params(("arbitrary",) * len(grid)))(*args)


def _ada_fwd(c_all, w_ada_sh, b_ada_sh):
    nb, d = c_all.shape
    ncol = w_ada_sh.shape[1]

    def body(c_ref, w_ref, b_ref, mod_ref, cact_ref):
        cc = c_ref[...]
        ca = cc * jax.nn.sigmoid(cc)
        cact_ref[...] = ca
        mod_ref[...] = _dot(ca.astype(BF16), w_ref[...].astype(BF16), NN) + b_ref[...]

    return pl.pallas_call(
        body, name="ada_fwd",
        out_shape=[jax.ShapeDtypeStruct((nb, ncol), F32), jax.ShapeDtypeStruct((nb, d), F32)],
        compiler_params=_params(),
    )(c_all, w_ada_sh, b_ada_sh)


def _rms(xv):
    rstd = lax.rsqrt(jnp.mean(xv * xv, axis=-1, keepdims=True) + EPS)
    return xv * rstd, rstd


def _rms_bwd(dxhat, xhat, rstd):
    return rstd * (dxhat - xhat * jnp.mean(dxhat * xhat, axis=-1, keepdims=True))


def _colsum(v):
    return jnp.sum(v, axis=0, keepdims=True)


def _expm1(v, ev):
    series = v * (1.0 + v * (0.5 + v * (1.0 / 6.0 + v * (1.0 / 24.0 + v * (1.0 / 120.0)))))
    return jnp.where(jnp.abs(v) < 0.2, series, ev - 1.0)


def _softplus(v):
    return jnp.maximum(v, 0.0) + jnp.log1p(jnp.exp(-jnp.abs(v)))


def _gelu(v):
    t = jnp.tanh(v * (GELU_K0 + (GELU_K0 * GELU_K1) * (v * v)))
    return 0.5 * v * (1.0 + t), t


def _dgelu(v, t):
    return 0.5 * ((1.0 + t) + (v * (1.0 - t * t)) * (GELU_K0 + (3.0 * GELU_K0 * GELU_K1) * (v * v)))


def _scan_tile(a, b, x0, st, k0, reverse):
    t = a.shape[0]
    off = SUBLANES
    stage_a, stage_b = st.at[k0], st.at[k0 + 1]
    halo = slice(off + t, off + t + SUBLANES) if reverse else slice(0, SUBLANES)
    stage_a[halo, :] = jnp.ones((SUBLANES, a.shape[1]), F32)
    stage_b[halo, :] = jnp.zeros((SUBLANES, a.shape[1]), F32)
    s = 1
    while s < min(t, SUBLANES):
        stage_a[off:off + t, :] = a
        stage_b[off:off + t, :] = b
        at = off + s if reverse else off - s
        b = a * stage_b[at:at + t, :] + b
        a = a * stage_a[at:at + t, :]
        s *= 2
    while s < t:
        if reverse:
            b = jnp.concatenate([a[:t - s] * b[s:] + b[:t - s], b[t - s:]], axis=0)
            a = jnp.concatenate([a[:t - s] * a[s:], a[t - s:]], axis=0)
        else:
            b = jnp.concatenate([b[:s], a[s:] * b[:t - s] + b[s:]], axis=0)
            a = jnp.concatenate([a[:s], a[s:] * a[:t - s]], axis=0)
        s *= 2
    x = b + a * x0
    return x, (x[0:SUBLANES, :] if reverse else x[t - SUBLANES:t, :])


def _lru_gates(u, wa, wx, ba, bx, sp):
    ub = u.astype(BF16)
    r = jax.nn.sigmoid(_dot(ub, wa, NN) + ba)
    i = jax.nn.sigmoid(_dot(ub, wx, NN) + bx)
    log_a = (-RG_C * r) * sp
    a = jnp.exp(log_a)
    mult = jnp.sqrt(-_expm1(log_a, a) * (a + 1.0))
    return ub, r, i, a, mult


def _staged_shifts(stage, v, prev8, next8, downs, ups):
    t = v.shape[0]
    if prev8 is not None:
        stage[0:SUBLANES, :] = prev8
    stage[SUBLANES:SUBLANES + t, :] = v
    if next8 is not None:
        stage[SUBLANES + t:2 * SUBLANES + t, :] = next8
    return ([stage[SUBLANES - k:SUBLANES - k + t, :] for k in downs],
            [stage[SUBLANES + k:SUBLANES + k + t, :] for k in ups])


def _conv3(p, pp, w_ref, lo, stage):
    (p1, p2), _ = _staged_shifts(stage, p, pp, None, (1, 2), ())
    q = (w_ref[0:1, lo:lo + LANES] * p2 + w_ref[1:2, lo:lo + LANES] * p1) + w_ref[2:3, lo:lo + LANES] * p
    return q, p1, p2


def _conv4(xv, xp, w_ref, b_ref, lo, stage):
    (x1, x2, x3), _ = _staged_shifts(stage, xv, xp, None, (1, 2, 3), ())
    u = (((w_ref[0:1, lo:lo + LANES] * x3 + w_ref[1:2, lo:lo + LANES] * x2) + w_ref[2:3, lo:lo + LANES] * x1)
         + w_ref[3:4, lo:lo + LANES] * xv) + b_ref[:, lo:lo + LANES]
    return u, x1, x2, x3


def _mix_in_mixer_fwd(x2d, mod6, g_mix, w_in_t, conv_sc, conv_lru, conv_b, wa_bd, wx_bd, ba, bx, lam, width, tm):
    s, d = x2d.shape
    din = w_in_t.shape[0]
    nt = s // tm
    sub = min(MIX_ROWS, tm)
    nblk = width // LANES

    def body(x_ref, mod_ref, g_ref, w_ref, wsc_ref, wlru_ref, blru_ref, wa_ref, wx_ref, ba_ref, bx_ref, lam_ref,
             hn_ref, proj_ref, ymix_ref, h_ref, buf_ref, halo_ref, hc_ref, stage_ref):
        i = pl.program_id(0)

        @pl.when(i == 0)
        def _():
            buf_ref[1] = jnp.zeros((tm, din), F32)
            halo_ref[...] = jnp.zeros_like(halo_ref)

        @pl.when(i <= 1)
        def _():
            hc_ref[...] = jnp.zeros_like(hc_ref)

        def step(dst, src):
            xhat, _ = _rms(x_ref[...])
            hn = ((xhat * g_ref[...]) * (1.0 + mod_ref[1:2, :]) + mod_ref[0:1, :]).astype(BF16)
            hn_ref[...] = hn
            n_mix = (tm // sub) * nblk
            n_chunk = din // width

            def project(k):
                res = _dot(hn_ref[...], w_ref[k * width:(k + 1) * width, :], NT)
                proj_ref[:, k * width:(k + 1) * width] = res
                dst[:, k * width:(k + 1) * width] = res

            done = 0
            for half in range(tm // sub):
                r0 = half * sub
                rows = slice(r0, r0 + sub)
                for j in range(nblk):
                    lo = j * LANES
                    while done < n_chunk and done * n_mix <= (half * nblk + j) * n_chunk:
                        project(done)
                        done += 1

                    def col(p):
                        return src[rows, p * width + lo:p * width + lo + LANES]

                    def prev(p):
                        c0 = p * width + lo
                        if half == 0:
                            return halo_ref[:, c0:c0 + LANES]
                        return src[r0 - SUBLANES:r0, c0:c0 + LANES]

                    pp = col(1) * col(2)
                    q, _, _ = _conv3(pp, prev(1) * prev(2), wsc_ref, lo, stage_ref.at[0])
                    ymix_ref[rows, lo:lo + LANES] = (col(0) * q).astype(BF16)

                    u, _, _, _ = _conv4(col(4), prev(4), wlru_ref, blru_ref, lo, stage_ref.at[1])
                    sp = _softplus(-lam_ref[:, lo:lo + LANES])
                    _, r, ig, a, mult = _lru_gates(u, wa_ref[j], wx_ref[j], ba_ref[:, lo:lo + LANES],
                                                   bx_ref[:, lo:lo + LANES], sp)
                    h, ends = _scan_tile(a, mult * (ig * u), hc_ref[0:1, lo:lo + LANES], stage_ref, 2, False)
                    h_ref[rows, lo:lo + LANES] = h
                    hc_ref[0:1, lo:lo + LANES] = ends[SUBLANES - 1:SUBLANES, :]
                    gel, _ = _gelu(col(3))
                    ymix_ref[rows, width + lo:width + lo + LANES] = (gel * h).astype(BF16)
            while done < n_chunk:
                project(done)
                done += 1
            halo_ref[...] = src[tm - SUBLANES:tm, :]

        @pl.when(i % 2 == 0)
        def _():
            step(buf_ref.at[0], buf_ref.at[1])

        @pl.when(i % 2 == 1)
        def _():
            step(buf_ref.at[1], buf_ref.at[0])

    small = [conv_sc, conv_lru, conv_b, wa_bd, wx_bd, ba, bx, lam]
    cur = lambda i: (jnp.minimum(i, nt - 1), 0)
    last = lambda i: (jnp.maximum(i - 1, 0), 0)
    outs = _call(
        body, "mix_in_mixer_fwd", (nt + 1,),
        [pl.BlockSpec((tm, d), cur), _full(mod6.shape), _full(g_mix.shape), _full(w_in_t.shape)]
        + [_full(a.shape) for a in small],
        [pl.BlockSpec((tm, d), cur), pl.BlockSpec((tm, din), cur),
         pl.BlockSpec((tm, 2 * width), last), pl.BlockSpec((tm, width), last)],
        [jax.ShapeDtypeStruct((s, d), BF16), jax.ShapeDtypeStruct((s, din), F32),
         jax.ShapeDtypeStruct((s, 2 * width), BF16), jax.ShapeDtypeStruct((s, width), F32)],
        [x2d, mod6, g_mix, w_in_t, *small],
        scratch=[pltpu.VMEM((2, tm, din), F32), pltpu.VMEM((SUBLANES, din), F32), pltpu.VMEM((SUBLANES, width), F32),
                 pltpu.VMEM((4, sub + 2 * SUBLANES, LANES), F32)])
    return outs


def _mix_out_fwd(ymix, x2d, w_out, mod6, g_mlp, tm):
    s, d = x2d.shape

    def body(y_ref, x_ref, w_ref, mod_ref, g_ref, mix_ref, x2_ref, hn_ref):
        mix = _dot(y_ref[...], w_ref[...], NN)
        mix_ref[...] = mix.astype(BF16)
        x2 = x_ref[...] + mod_ref[2:3, :] * mix
        x2_ref[...] = x2
        xhat, _ = _rms(x2)
        hn_ref[...] = ((xhat * g_ref[...]) * (1.0 + mod_ref[4:5, :]) + mod_ref[3:4, :]).astype(BF16)

    tile = pl.BlockSpec((tm, d), lambda i: (i, 0))
    return _call(
        body, "mix_out_fwd", (s // tm,),
        [tile, tile, _full(w_out.shape), _full(mod6.shape), _full(g_mlp.shape)],
        [tile, tile, tile],
        [jax.ShapeDtypeStruct((s, d), BF16), jax.ShapeDtypeStruct((s, d), F32), jax.ShapeDtypeStruct((s, d), BF16)],
        [ymix, x2d, w_out, mod6, g_mlp])


def _mlp_fwd_loss(hn2, w_up_t, w_down, x2, target, mod6, g_final, tm, tk):
    s, d = hn2.shape
    f = w_up_t.shape[0]
    nk = f // tk

    def body(hn_ref, wu_ref, wd_ref, x2_hbm, t_hbm, mod_ref, g_ref, z_ref, dx3_ref, dyb_ref, st_ref,
             y_ref, x2_ref, t_ref, sems):
        i, k = pl.program_id(0), pl.program_id(1)

        def fetch():
            rows = pl.ds(pl.multiple_of(i * tm, tm), tm)
            return (pltpu.make_async_copy(x2_hbm.at[rows, :], x2_ref, sems.at[0]),
                    pltpu.make_async_copy(t_hbm.at[rows, :], t_ref, sems.at[1]))

        @pl.when(jnp.logical_and(i == 0, k == 0))
        def _():
            st_ref[...] = jnp.zeros_like(st_ref)

        @pl.when(k == 0)
        def _():
            for cp in fetch():
                cp.start()
            y_ref[...] = jnp.zeros_like(y_ref)

        z = jnp.maximum(_dot(hn_ref[...], wu_ref[...], NT), 0.0)
        z_ref[...] = z.astype(BF16)
        y_ref[...] += _dot((z * z).astype(BF16), wd_ref[...], NN)

        @pl.when(k == nk - 1)
        def _():
            for cp in fetch():
                cp.wait()
            gate = mod_ref[5:6, :]
            yv = y_ref[...]
            xhat, rstd = _rms(x2_ref[...] + gate * yv)
            diff = xhat * g_ref[...] - t_ref[...]
            dyo = diff * (1.0 / d)
            dx3 = _rms_bwd(dyo * g_ref[...], xhat, rstd)
            dx3_ref[...] = dx3
            dyb_ref[...] = (gate * dx3).astype(BF16)
            st_ref[0:1, :] += _colsum(dyo * xhat)
            st_ref[1:2, :] += _colsum(dx3 * yv)
            st_ref[2:3, :] += _colsum(diff * diff)

    tile = pl.BlockSpec((tm, d), lambda i, k: (i, 0))
    wblk = pl.BlockSpec((tk, d), lambda i, k: (k, 0))
    return pl.pallas_call(
        body, name="mlp_fwd_loss", grid=(s // tm, nk),
        in_specs=[tile, wblk, wblk, ANY, ANY, _full(mod6.shape), _full(g_final.shape)],
        out_specs=[pl.BlockSpec((tm, tk), lambda i, k: (i, k)), tile, tile, _full((SUBLANES, d))],
        out_shape=[jax.ShapeDtypeStruct((s, f), BF16), jax.ShapeDtypeStruct((s, d), F32),
                   jax.ShapeDtypeStruct((s, d), BF16), jax.ShapeDtypeStruct((SUBLANES, d), F32)],
        scratch_shapes=[pltpu.VMEM((tm, d), F32), pltpu.VMEM((tm, d), F32), pltpu.VMEM((tm, d), F32),
                        pltpu.SemaphoreType.DMA((2,))],
        compiler_params=pltpu.CompilerParams(dimension_semantics=("arbitrary", "arbitrary"),
                                             vmem_limit_bytes=VMEM_LIMIT_BIG),
    )(hn2, w_up_t, w_down, x2, target, mod6, g_final)


def _mlp_bwd_dx(dyb, z, w_down, w_up_t, tm, tk):
    s, d = dyb.shape
    f = z.shape[1]

    def body(dy_ref, z_ref, wd_ref, wu_ref, dz_ref, dh_ref):
        k = pl.program_id(1)

        @pl.when(k == 0)
        def _():
            dh_ref[...] = jnp.zeros_like(dh_ref)

        dz = ((2.0 * z_ref[...].astype(F32)) * _dot(dy_ref[...], wd_ref[...], NT)).astype(BF16)
        dz_ref[...] = dz
        dh_ref[...] += _dot(dz, wu_ref[...], NN)

    return pl.pallas_call(
        body, name="mlp_bwd_dx", grid=(s // tm, f // tk),
        in_specs=[pl.BlockSpec((tm, d), lambda i, k: (i, 0)), pl.BlockSpec((tm, tk), lambda i, k: (i, k)),
                  pl.BlockSpec((tk, d), lambda i, k: (k, 0)), pl.BlockSpec((tk, d), lambda i, k: (k, 0))],
        out_specs=[pl.BlockSpec((tm, tk), lambda i, k: (i, k)), pl.BlockSpec((tm, d), lambda i, k: (i, 0))],
        out_shape=[jax.ShapeDtypeStruct((s, f), BF16), jax.ShapeDtypeStruct((s, d), F32)],
        compiler_params=_params(("parallel", "arbitrary")),
    )(dyb, z, w_down, w_up_t)


def _mlp_bwd_dw(z, dz, dyb, hn2, tm, tk):
    s, d = dyb.shape
    f = z.shape[1]

    def body(z_ref, dz_ref, dy_ref, hn_ref, gd_ref, gu_ref):
        i = pl.program_id(1)

        @pl.when(i == 0)
        def _():
            gd_ref[...] = jnp.zeros_like(gd_ref)
            gu_ref[...] = jnp.zeros_like(gu_ref)

        zf = z_ref[...].astype(F32)
        gd_ref[...] += _dot((zf * zf).astype(BF16), dy_ref[...], TN)
        gu_ref[...] += _dot(dz_ref[...], hn_ref[...], TN)

    return pl.pallas_call(
        body, name="mlp_bwd_dw", grid=(f // tk, s // tm),
        in_specs=[pl.BlockSpec((tm, tk), lambda k, i: (i, k)), pl.BlockSpec((tm, tk), lambda k, i: (i, k)),
                  pl.BlockSpec((tm, d), lambda k, i: (i, 0)), pl.BlockSpec((tm, d), lambda k, i: (i, 0))],
        out_specs=[pl.BlockSpec((tk, d), lambda k, i: (k, 0)), pl.BlockSpec((tk, d), lambda k, i: (k, 0))],
        out_shape=[jax.ShapeDtypeStruct((f, d), F32), jax.ShapeDtypeStruct((f, d), F32)],
        compiler_params=_params(("parallel", "arbitrary")),
    )(z, dz, dyb, hn2)


def _mix_out_bwd(dhn2, x2, dx3, mix, ymix, w_out, mod6, g_mlp, tm):
    s, d = x2.shape

    def body(dh_ref, x2_ref, dx3_ref, mix_ref, y_ref, w_ref, mod_ref, g_ref, dx2_ref, dym_ref, gw_ref, st_ref):
        i = pl.program_id(0)

        @pl.when(i == 0)
        def _():
            st_ref[...] = jnp.zeros_like(st_ref)
            gw_ref[...] = jnp.zeros_like(gw_ref)

        dh = dh_ref[...]
        xhat, rstd = _rms(x2_ref[...])
        dn = dh * (1.0 + mod_ref[4:5, :])
        dx2 = dx3_ref[...] + _rms_bwd(dn * g_ref[...], xhat, rstd)
        dx2_ref[...] = dx2
        st_ref[0:1, :] += _colsum(dh)
        st_ref[1:2, :] += _colsum(dh * (xhat * g_ref[...]))
        st_ref[2:3, :] += _colsum(dn * xhat)
        st_ref[3:4, :] += _colsum(dx2 * mix_ref[...].astype(F32))
        dmix = (mod_ref[2:3, :] * dx2).astype(BF16)
        dym_ref[...] = _dot(dmix, w_ref[...], NT)
        gw_ref[...] += _dot(y_ref[...], dmix, TN)

    tile = pl.BlockSpec((tm, d), lambda i: (i, 0))
    return _call(
        body, "mix_out_bwd", (s // tm,),
        [tile, tile, tile, tile, tile, _full(w_out.shape), _full(mod6.shape), _full(g_mlp.shape)],
        [tile, tile, _full((d, d)), _full((SUBLANES, d))],
        [jax.ShapeDtypeStruct((s, d), F32), jax.ShapeDtypeStruct((s, d), F32),
         jax.ShapeDtypeStruct((d, d), F32), jax.ShapeDtypeStruct((SUBLANES, d), F32)],
        [dhn2, x2, dx3, mix, ymix, w_out, mod6, g_mlp])


def _mixer_bwd(proj, dymix, h_all, conv_sc, conv_lru, conv_b, wa_bd, wx_bd, ba, bx, lam, width):
    s, din = proj.shape
    t = min(MIX_ROWS, s)
    nt = s // t
    nblk = width // LANES
    hb = t // SUBLANES
    last8 = s // SUBLANES - 1

    def body(proj_ref, projp_ref, projn_ref, dy_ref, dyn_ref, h_ref, hp_ref,
             wsc_ref, wlru_ref, blru_ref, wa_ref, wx_ref, ba_ref, bx_ref, lam_ref,
             dproj_ref, small_ref, gwa_ref, gwx_ref, an_ref, gn_ref, dun_ref, stage_ref):
        i = pl.program_id(0)

        @pl.when(i == 0)
        def _():
            small_ref[...] = jnp.zeros_like(small_ref)
            gwa_ref[...] = jnp.zeros_like(gwa_ref)
            gwx_ref[...] = jnp.zeros_like(gwx_ref)
            an_ref[...] = jnp.zeros_like(an_ref)
            gn_ref[...] = jnp.zeros_like(gn_ref)
            dun_ref[...] = jnp.zeros_like(dun_ref)

        has_prev = i < nt - 1
        has_next = i > 0
        for j in range(nblk):
            lo = j * LANES
            ls = slice(lo, lo + LANES)

            def col(p, ref=proj_ref):
                return ref[:, p * width + lo:p * width + lo + LANES]

            def prev(p):
                return jnp.where(has_prev, col(p, projp_ref), 0.0)

            def nxt(p):
                return jnp.where(has_next, col(p, projn_ref), 0.0)

            def add_row(r, v):
                small_ref[r:r + 1, ls] += _colsum(v)

            sc_b, sc_c, sc_x = col(0), col(1), col(2)
            p = sc_c * sc_x
            q, p1, p2 = _conv3(p, prev(1) * prev(2), wsc_ref, lo, stage_ref.at[0])
            dys = dy_ref[:, ls]
            dproj_ref[:, ls] = (dys * q).astype(BF16)
            dq = dys * sc_b
            dqn = jnp.where(has_next, dyn_ref[:, ls], 0.0) * nxt(0)
            _, (dq1, dq2) = _staged_shifts(stage_ref.at[1], dq, None, dqn, (), (1, 2))
            dp = (wsc_ref[2:3, ls] * dq + wsc_ref[1:2, ls] * dq1) + wsc_ref[0:1, ls] * dq2
            dproj_ref[:, width + lo:width + lo + LANES] = (dp * sc_x).astype(BF16)
            dproj_ref[:, 2 * width + lo:2 * width + lo + LANES] = (dp * sc_c).astype(BF16)
            add_row(0, dq * p2)
            add_row(1, dq * p1)
            add_row(2, dq * p)

            xv = col(4)
            u, x1, x2, x3 = _conv4(xv, prev(4), wlru_ref, blru_ref, lo, stage_ref.at[2])
            lam_v = lam_ref[:, ls]
            sp = _softplus(-lam_v)
            wa, wx = wa_ref[j], wx_ref[j]
            ub, r, ig, a, mult = _lru_gates(u, wa, wx, ba_ref[:, ls], bx_ref[:, ls], sp)
            iu = ig * u
            h = h_ref[:, ls]
            (hm1,), _ = _staged_shifts(stage_ref.at[3], h, jnp.where(has_prev, hp_ref[:, ls], 0.0), None, (1,), ())
            lyv = col(3)
            gel, th = _gelu(lyv)
            dyl = dy_ref[:, width + lo:width + lo + LANES]
            dproj_ref[:, 3 * width + lo:3 * width + lo + LANES] = (dyl * h * _dgelu(lyv, th)).astype(BF16)
            a_next = jnp.broadcast_to(an_ref[0:1, ls], (SUBLANES, LANES))
            _, (a_up,) = _staged_shifts(stage_ref.at[4], a, None, a_next, (), (1,))
            g, _ = _scan_tile(a_up, dyl * gel, gn_ref[0:1, ls], stage_ref, 5, True)
            an_ref[0:1, ls] = a[0:1, :]
            gn_ref[0:1, ls] = g[0:1, :]
            da = g * hm1
            dmult = g * iu
            diu = g * mult
            dlog_a = da * a - dmult * ((a * a) / mult)
            dpre_a = (dlog_a * (-RG_C * sp)) * (r * (1.0 - r))
            dpre_x = (diu * u) * (ig * (1.0 - ig))
            dab, dxb = dpre_a.astype(BF16), dpre_x.astype(BF16)
            du = diu * ig + _dot(dab, wa, NT) + _dot(dxb, wx, NT)
            gwa_ref[j] += _dot(ub, dab, TN)
            gwx_ref[j] += _dot(ub, dxb, TN)
            dun = dun_ref[:, ls]
            dun_ref[:, ls] = du[0:SUBLANES, :]
            _, (du1, du2, du3) = _staged_shifts(stage_ref.at[7], du, None, dun, (), (1, 2, 3))
            dlx = (((wlru_ref[3:4, ls] * du + wlru_ref[2:3, ls] * du1) + wlru_ref[1:2, ls] * du2)
                   + wlru_ref[0:1, ls] * du3)
            dproj_ref[:, 4 * width + lo:4 * width + lo + LANES] = dlx.astype(BF16)
            add_row(3, du * x3)
            add_row(4, du * x2)
            add_row(5, du * x1)
            add_row(6, du * xv)
            add_row(7, du)
            add_row(8, dpre_a)
            add_row(9, dpre_x)
            add_row(10, (dlog_a * (RG_C * r)) * jax.nn.sigmoid(-lam_v))

    small = [conv_sc, conv_lru, conv_b, wa_bd, wx_bd, ba, bx, lam]
    rev = lambda i: nt - 1 - i
    return _call(
        body, "mixer_bwd", (nt,),
        [pl.BlockSpec((t, din), lambda i: (rev(i), 0)),
         pl.BlockSpec((SUBLANES, din), lambda i: (jnp.maximum(rev(i) * hb - 1, 0), 0)),
         pl.BlockSpec((SUBLANES, din), lambda i: (jnp.minimum((rev(i) + 1) * hb, last8), 0)),
         pl.BlockSpec((t, 2 * width), lambda i: (rev(i), 0)),
         pl.BlockSpec((SUBLANES, 2 * width), lambda i: (jnp.minimum((rev(i) + 1) * hb, last8), 0)),
         pl.BlockSpec((t, width), lambda i: (rev(i), 0)),
         pl.BlockSpec((SUBLANES, width), lambda i: (jnp.maximum(rev(i) * hb - 1, 0), 0))]
        + [_full(a.shape) for a in small],
        [pl.BlockSpec((t, din), lambda i: (rev(i), 0)), _full((2 * SUBLANES, width)),
         _full(wa_bd.shape), _full(wx_bd.shape)],
        [jax.ShapeDtypeStruct((s, din), BF16), jax.ShapeDtypeStruct((2 * SUBLANES, width), F32),
         jax.ShapeDtypeStruct(wa_bd.shape, F32), jax.ShapeDtypeStruct(wx_bd.shape, F32)],
        [proj, proj, proj, dymix, dymix, h_all, h_all, *small],
        scratch=[pltpu.VMEM((SUBLANES, width), F32), pltpu.VMEM((SUBLANES, width), F32),
                 pltpu.VMEM((SUBLANES, width), F32), pltpu.VMEM((8, t + 2 * SUBLANES, LANES), F32)])


def _mix_in_bwd_dx(dproj, x2d, dx2, w_in_t, mod6, g_mix, tm):
    s, d = x2d.shape
    din = dproj.shape[1]

    def body(dp_ref, x_ref, dx2_ref, w_ref, mod_ref, g_ref, gx_ref, st_ref):
        i = pl.program_id(0)

        @pl.when(i == 0)
        def _():
            st_ref[...] = jnp.zeros_like(st_ref)

        dh = _dot(dp_ref[...], w_ref[...], NN)
        xhat, rstd = _rms(x_ref[...])
        dn = dh * (1.0 + mod_ref[1:2, :])
        gx_ref[...] = dx2_ref[...] + _rms_bwd(dn * g_ref[...], xhat, rstd)
        st_ref[0:1, :] += _colsum(dh)
        st_ref[1:2, :] += _colsum(dh * (xhat * g_ref[...]))
        st_ref[2:3, :] += _colsum(dn * xhat)

    tile = pl.BlockSpec((tm, d), lambda i: (i, 0))
    return _call(
        body, "mix_in_bwd_dx", (s // tm,),
        [pl.BlockSpec((tm, din), lambda i: (i, 0)), tile, tile, _full(w_in_t.shape), _full(mod6.shape),
         _full(g_mix.shape)],
        [tile, _full((SUBLANES, d))],
        [jax.ShapeDtypeStruct((s, d), F32), jax.ShapeDtypeStruct((SUBLANES, d), F32)],
        [dproj, x2d, dx2, w_in_t, mod6, g_mix])


def _mix_in_bwd_dw(dproj, hn1, tm, tn):
    s, d = hn1.shape
    din = dproj.shape[1]

    def body(dp_ref, hn_ref, gw_ref):
        i = pl.program_id(1)

        @pl.when(i == 0)
        def _():
            gw_ref[...] = jnp.zeros_like(gw_ref)

        gw_ref[...] += _dot(dp_ref[...], hn_ref[...], TN)

    return _call(
        body, "mix_in_bwd_dw", (din // tn, s // tm),
        [pl.BlockSpec((tm, tn), lambda p, i: (i, p)), pl.BlockSpec((tm, d), lambda p, i: (i, 0))],
        [pl.BlockSpec((tn, d), lambda p, i: (p, 0))],
        [jax.ShapeDtypeStruct((din, d), F32)],
        [dproj, hn1])


def _adamw(w, g, m, v):
    m = ADAM_B1 * m + (1.0 - ADAM_B1) * g
    v = ADAM_B2 * v + (1.0 - ADAM_B2) * (g * g)
    m_hat = m / (1.0 - ADAM_B1 ** ADAM_STEP)
    v_hat = v / (1.0 - ADAM_B2 ** ADAM_STEP)
    delta = -ADAM_LR * (m_hat / (jnp.sqrt(v_hat) + ADAM_EPS) + ADAM_WD * w)
    return delta, m, v


def _pair_sum(g4s, h4s, core_chip, tr, name):
    na = len(g4s)
    _, _, r, n = g4s[0].shape

    def body(sc_ref, *refs):
        q = pl.program_id(1)
        for a in range(na):
            g_ref, h_ref = refs[2 * a], refs[2 * a + 1]
            sb_ref, own_ref = refs[2 * na + 2 * a], refs[2 * na + 2 * a + 1]
            ssum = g_ref[...] + h_ref[...]
            sb_ref[...] = ssum.astype(BF16)

            @pl.when(q == sc_ref[1])
            def _():
                own_ref[...] = ssum

    grid_spec = pltpu.PrefetchScalarGridSpec(
        num_scalar_prefetch=1, grid=(r // tr, 4),
        in_specs=[pl.BlockSpec((None, None, tr, n), lambda i, q, sc: (q, sc[0], i, 0)),
                  pl.BlockSpec((None, tr, n), lambda i, q, sc: (q, i, 0))] * na,
        out_specs=[pl.BlockSpec((None, tr, n), lambda i, q, sc: (q, i, 0)),
                   pl.BlockSpec((tr, n), lambda i, q, sc: (i, 0))] * na)
    outs = pl.pallas_call(
        body, name=name, grid_spec=grid_spec,
        out_shape=[jax.ShapeDtypeStruct((4, r, n), BF16), jax.ShapeDtypeStruct((r, n), F32)] * na,
        compiler_params=_params(("parallel", "arbitrary")),
    )(core_chip, *[x for pair in zip(g4s, h4s) for x in pair])
    return [(outs[2 * a], outs[2 * a + 1]) for a in range(na)]


def _sum4_adam(own, parts, w, m, v, tr, name, transposed):
    r, n = own.shape
    rows, cols = w.shape

    def body(o_ref, p_ref, w_ref, m_ref, v_ref, g_ref, d_ref, nm_ref, nv_ref):
        g = o_ref[...]
        for k in range(3):
            g = g + p_ref[k].astype(F32)
        if transposed:
            g = g.T
        g_ref[...] = g
        d_ref[...], nm_ref[...], nv_ref[...] = _adamw(w_ref[...], g, m_ref[...], v_ref[...])

    if transposed:
        g_specs = [pl.BlockSpec((r, tr), lambda i: (0, i)), pl.BlockSpec((3, r, tr), lambda i: (0, 0, i))]
    else:
        g_specs = [pl.BlockSpec((tr, n), lambda i: (i, 0)), pl.BlockSpec((3, tr, n), lambda i: (0, i, 0))]
    tile = pl.BlockSpec((tr, cols), lambda i: (i, 0))
    return pl.pallas_call(
        body, name=name, grid=(rows // tr,),
        in_specs=g_specs + [tile] * 3, out_specs=[tile] * 4,
        out_shape=[jax.ShapeDtypeStruct((rows, cols), F32)] * 4,
        compiler_params=_params(("parallel",)),
    )(own, parts, w, m, v)


def _sum8(parts, tr, name):
    _, rows, n = parts.shape

    def body(p_ref, o_ref):
        acc = p_ref[0]
        for k in range(1, N_DEV):
            acc = acc + p_ref[k]
        o_ref[...] = acc

    return pl.pallas_call(
        body, name=name, grid=(rows // tr,),
        in_specs=[pl.BlockSpec((N_DEV, tr, n), lambda i: (0, i, 0))],
        out_specs=pl.BlockSpec((tr, n), lambda i: (i, 0)),
        out_shape=jax.ShapeDtypeStruct((rows, n), F32),
        compiler_params=_params(("parallel",)),
    )(parts)


def _ada_bwd_adam(cact_t, dmod_cols, w, m, v, tr):
    rows, n = w.shape

    def body(c_ref, d_ref, w_ref, m_ref, v_ref, g_ref, dl_ref, nm_ref, nv_ref):
        def term(b):
            return c_ref[b].astype(BF16).astype(F32) * d_ref[b:b + 1, :].astype(BF16).astype(F32)

        g = term(0)
        for b in range(1, N_DEV):
            g = g + term(b)
        g_ref[...] = g
        dl_ref[...], nm_ref[...], nv_ref[...] = _adamw(w_ref[...], g, m_ref[...], v_ref[...])

    tile = pl.BlockSpec((tr, n), lambda i: (i, 0))
    return pl.pallas_call(
        body, name="ada_bwd_adam", grid=(rows // tr,),
        in_specs=[pl.BlockSpec((N_DEV, tr, 1), lambda i: (0, i, 0)), _full(dmod_cols.shape), tile, tile, tile],
        out_specs=[tile] * 4,
        out_shape=[jax.ShapeDtypeStruct((rows, n), F32)] * 4,
        compiler_params=_params(("parallel",)),
    )(cact_t, dmod_cols, w, m, v)


def _adam_small(ws, gs, ms, vs):
    n = len(ws)

    def body(*refs):
        w_r, g_r, m_r, v_r = refs[:n], refs[n:2 * n], refs[2 * n:3 * n], refs[3 * n:4 * n]
        d_r, nm_r, nv_r = refs[4 * n:5 * n], refs[5 * n:6 * n], refs[6 * n:7 * n]
        for k in range(n):
            d_r[k][...], nm_r[k][...], nv_r[k][...] = _adamw(w_r[k][...], g_r[k][...], m_r[k][...], v_r[k][...])

    shapes = [jax.ShapeDtypeStruct(w.shape, F32) for w in ws]
    outs = pl.pallas_call(
        body, name="adam_small", out_shape=shapes * 3, compiler_params=_params(),
    )(*ws, *gs, *ms, *vs)
    return outs[:n], outs[n:2 * n], outs[2 * n:]


def _block_diag(w):
    h, hd, _ = w.shape
    per = LANES // hd
    eye = jnp.eye(per, dtype=w.dtype)
    w5 = w.reshape(h // per, per, hd, 1, hd) * eye[None, :, None, :, None]
    return w5.reshape(h // per, LANES, LANES)


def _block_diag_grad(g, h, hd):
    per = LANES // hd
    g5 = g.reshape(h // per, per, hd, per, hd)
    return jnp.stack([g5[:, a, :, a, :] for a in range(per)], axis=1).reshape(h, hd, hd)


def kernel(x, c, w_ada, b_ada, g_mix, w_in, conv_w_sc, conv_w_lru, conv_b_lru, w_rg_a, b_rg_a, w_rg_x, b_rg_x, lru_lambda, w_out, g_mlp, w_up, w_down, g_final, loss_target, m_w_ada, m_b_ada, m_g_mix, m_w_in, m_conv_w_sc, m_conv_w_lru, m_conv_b_lru, m_w_rg_a, m_b_rg_a, m_w_rg_x, m_b_rg_x, m_lru_lambda, m_w_out, m_g_mlp, m_w_up, m_w_down, m_g_final, v_w_ada, v_b_ada, v_g_mix, v_w_in, v_conv_w_sc, v_conv_w_lru, v_conv_b_lru, v_w_rg_a, v_b_rg_a, v_w_rg_x, v_b_rg_x, v_lru_lambda, v_w_out, v_g_mlp, v_w_up, v_w_down, v_g_final):
    s, d = x.shape[1], x.shape[2]
    width = conv_b_lru.shape[1]
    heads, hd = w_rg_a.shape[1], w_rg_a.shape[2]
    f = w_down.shape[1] * N_DEV
    n_ada = w_ada.shape[2]
    csh = conv_w_sc.shape[2]
    me = 4 * lax.axis_index("x") + 2 * lax.axis_index("y") + lax.axis_index("c")
    tm = min(512, s)
    tm_mlp = min(1024, s)
    tk = 512

    x2d = x[0]
    tgt = loss_target[0]

    pay = jnp.zeros((SUBLANES, d), F32)
    pay = pay.at[0:1, :].set(c)
    pay = pay.at[1:4, 0:csh].set(conv_w_sc[0])
    pay = pay.at[4:8, 0:csh].set(conv_w_lru[0])
    w_in_t_sh = w_in[0].T.astype(BF16)
    w_up_t_sh = w_up[0].T.astype(BF16)
    w_out_sh = w_out[0].astype(BF16)
    w_down_sh = w_down[0].astype(BF16)
    pay_all, w_in_t = _gather2("gather_in", [pay, w_in_t_sh])
    w_in_t = w_in_t.reshape(-1, d)
    c_all = pay_all[:, 0, :]
    conv_sc = pay_all[:, 1:4, 0:csh].transpose(1, 0, 2).reshape(3, width)
    conv_lru = pay_all[:, 4:8, 0:csh].transpose(1, 0, 2).reshape(4, width)

    b_ada_sh = lax.dynamic_slice(b_ada, (0, me * n_ada), (1, n_ada))
    mod_cols, c_act = _ada_fwd(c_all, w_ada[0], b_ada_sh)
    (mod_rows,) = _exchange("scatter_mod", [], [mod_cols.reshape(N_DEV, 1, n_ada)])
    mod_rows, w_out_sh, w_up_t_sh, w_down_sh = lax.optimization_barrier((mod_rows, w_out_sh, w_up_t_sh, w_down_sh))
    (w_out_g,) = _seq_gather2("gather_w_out", 1, [w_out_sh])
    w_up_g, w_down_g = _seq_gather2("gather_mlp_weights", 2, [w_up_t_sh, w_down_sh])
    mod6 = jnp.zeros((SUBLANES, d), F32).at[0:6, :].set(mod_rows.reshape(6, d))

    wa_bd = _block_diag(w_rg_a[0]).astype(BF16)
    wx_bd = _block_diag(w_rg_x[0]).astype(BF16)
    ba = b_rg_a.reshape(1, width)
    bx = b_rg_x.reshape(1, width)
    g_fin = g_final.reshape(1, d)

    hn1, proj, ymix, h_all = _mix_in_mixer_fwd(x2d, mod6, g_mix, w_in_t, conv_sc, conv_lru, conv_b_lru,
                                               wa_bd, wx_bd, ba, bx, lru_lambda, width, tm)
    w_out_b = w_out_g.reshape(-1, d)
    mix, x2, hn2 = _mix_out_fwd(ymix, x2d, w_out_b, mod6, g_mlp, tm_mlp)
    w_up_t = w_up_g.reshape(-1, d)
    w_down_b = w_down_g.reshape(-1, d)
    z, dx3, dyb, st_fin = _mlp_fwd_loss(hn2, w_up_t, w_down_b, x2, tgt, mod6, g_fin, tm_mlp, 2 * tk)

    core_chip = jnp.stack([lax.axis_index("c"), 2 * lax.axis_index("x") + lax.axis_index("y")]).astype(jnp.int32)
    dz, dhn2 = _mlp_bwd_dx(dyb, z, w_down_b, w_up_t, tm_mlp, 2 * tk)
    g_down, g_up_t = _mlp_bwd_dw(z, dz, dyb, hn2, tm_mlp, 2 * tk)
    g_up4, g_down4 = g_up_t.reshape(4, 2, -1, d), g_down.reshape(4, 2, -1, d)
    h_up, h_down = _seq_pair_swap("swap_mlp_grads", 7, [g_up4, g_down4])
    dx2, dymix, g_out, st_out = _mix_out_bwd(dhn2, x2, dx3, mix, ymix, w_out_b, mod6, g_mlp, tm)
    h_up, h_down, g_out = lax.optimization_barrier((h_up, h_down, g_out))
    (sb_up, own_up), (sb_down, own_down) = _pair_sum([g_up4, g_down4], [h_up, h_down], core_chip, 256, "pair_sum_mlp")
    g_out4 = g_out.reshape(4, 2, -1, d)
    (h_out,) = _seq_pair_swap("swap_w_out_grad", 8, [g_out4])
    p_up, p_down = _seq_chip_exchange("exchange_mlp_grads", 3, [sb_up, sb_down])
    dproj, g_small, g_wa, g_wx = _mixer_bwd(
        proj, dymix, h_all, conv_sc, conv_lru, conv_b_lru, wa_bd, wx_bd, ba, bx, lru_lambda, width)
    h_out, dproj = lax.optimization_barrier((h_out, dproj))
    ((sb_out, own_out),) = _pair_sum([g_out4], [h_out], core_chip, g_out4.shape[2], "pair_sum_w_out")
    (p_out,) = _seq_chip_exchange("exchange_w_out_grad", 4, [sb_out])
    grad_x, st_in = _mix_in_bwd_dx(dproj, x2d, dx2, w_in_t, mod6, g_mix, tm)

    small = jnp.concatenate([
        st_in[0:2], st_out[3:4], st_out[0:2], st_fin[1:2],
        st_in[2:3], st_out[2:3], st_fin[0:1],
        jnp.concatenate([g_small[7:8], g_small[10:11]], axis=1),
        jnp.concatenate([g_small[8:9], g_small[9:10]], axis=1),
        jnp.concatenate([jnp.concatenate([g_small[0:3], jnp.zeros((1, width), F32)], axis=0), g_small[3:7]], axis=1),
        st_fin[2:3],
        _block_diag_grad(g_wa, heads, hd).reshape(-1, d),
        _block_diag_grad(g_wx, heads, hd).reshape(-1, d),
    ], axis=0)

    (small_all,) = _seq_gather2("gather_small_grads", 5, [small])
    g_in_t, = _mix_in_bwd_dw(dproj, hn1, min(2048, s), dproj.shape[1] // 2)
    g_in4 = g_in_t.reshape(4, 2, -1, d)
    (h_in,) = _seq_pair_swap("swap_w_in_grad", 9, [g_in4])
    p_up, p_down, p_out, small_all, g_in_t = lax.optimization_barrier((p_up, p_down, p_out, small_all, g_in_t))

    ad_up = _sum4_adam(own_up, p_up, w_up[0], m_w_up[0], v_w_up[0], 256, "adam_w_up", True)
    h_in, ad_up = lax.optimization_barrier((h_in, ad_up))
    ((sb_in, own_in),) = _pair_sum([g_in4], [h_in], core_chip, g_in4.shape[2], "pair_sum_w_in")
    (p_in,) = _seq_chip_exchange("exchange_w_in_grad", 6, [sb_in])
    ad_out = _sum4_adam(own_out, p_out, w_out[0], m_w_out[0], v_w_out[0], w_out.shape[1], "adam_w_out", False)
    ad_down = _sum4_adam(own_down, p_down, w_down[0], m_w_down[0], v_w_down[0], 256, "adam_w_down", False)

    gsum = _sum8(small_all, SMALL_ROWS, "sum_small")
    loss = (0.5 / d) * jnp.sum(gsum[15])
    dmod_cols = lax.dynamic_slice(small_all[:, 0:6, :].reshape(N_DEV, 6 * d), (0, me * n_ada), (N_DEV, n_ada))
    g_ada, d_ada, nm_ada, nv_ada = _ada_bwd_adam(c_act[:, :, None], dmod_cols, w_ada[0], m_w_ada[0], v_w_ada[0], 256)

    g_conv = lax.dynamic_slice(gsum[11:15, 0:width], (0, me * csh), (4, csh))
    g_conv_l = lax.dynamic_slice(gsum[11:15, width:2 * width], (0, me * csh), (4, csh))
    small_g = [
        gsum[0:6].reshape(1, 6 * d),
        gsum[6:7],
        g_conv[0:3].reshape(1, 3, csh),
        g_conv_l.reshape(1, 4, csh),
        gsum[9:10, 0:width],
        gsum[16:48].reshape(1, heads, hd, hd),
        gsum[10:11, 0:width].reshape(1, heads, hd),
        gsum[48:80].reshape(1, heads, hd, hd),
        gsum[10:11, width:].reshape(1, heads, hd),
        gsum[9:10, width:],
        gsum[7:8],
        gsum[8],
    ]
    small_w = [b_ada, g_mix, conv_w_sc, conv_w_lru, conv_b_lru, w_rg_a, b_rg_a, w_rg_x, b_rg_x, lru_lambda, g_mlp, g_final]
    small_m = [m_b_ada, m_g_mix, m_conv_w_sc, m_conv_w_lru, m_conv_b_lru, m_w_rg_a, m_b_rg_a, m_w_rg_x, m_b_rg_x,
               m_lru_lambda, m_g_mlp, m_g_final]
    small_v = [v_b_ada, v_g_mix, v_conv_w_sc, v_conv_w_lru, v_conv_b_lru, v_w_rg_a, v_b_rg_a, v_w_rg_x, v_b_rg_x,
               v_lru_lambda, v_g_mlp, v_g_final]
    sd, snm, snv = _adam_small(small_w, small_g, small_m, small_v)
    p_in, ad_out, ad_down, (g_ada, d_ada, nm_ada, nv_ada), sd = lax.optimization_barrier(
        (p_in, ad_out, ad_down, (g_ada, d_ada, nm_ada, nv_ada), sd))
    ad_in = _sum4_adam(own_in, p_in, w_in[0].T, m_w_in[0].T, v_w_in[0].T, own_in.shape[0], "adam_w_in", False)
    ad_in = [a.T for a in ad_in]

    def order(ada, w_in_, w_out_, w_up_, w_down_, sm):
        return [ada[None], sm[0], sm[1], w_in_[None], sm[2], sm[3], sm[4], sm[5], sm[6], sm[7], sm[8], sm[9],
                w_out_[None], sm[10], w_up_[None], w_down_[None], sm[11]]

    grads = order(g_ada, ad_in[0], ad_out[0], ad_up[0], ad_down[0], small_g)
    deltas = order(d_ada, ad_in[1], ad_out[1], ad_up[1], ad_down[1], sd)
    new_m = order(nm_ada, ad_in[2], ad_out[2], ad_up[2], ad_down[2], snm)
    new_v = order(nv_ada, ad_in[3], ad_out[3], ad_up[3], ad_down[3], snv)
    return (loss, grad_x[None], *grads, *deltas, *new_m, *new_v)
```

```python
import jax
import jax.numpy as jnp
from jax import lax
from jax.experimental import pallas as pl
from jax.experimental.pallas import tpu as pltpu
from jax.experimental.pallas import tpu_sc as plsc

F32 = jnp.float32
BF16 = jnp.bfloat16
N_DEV = 8
EPS = 1e-6
RG_C = 8.0
GELU_K0 = 0.7978845608028654
GELU_K1 = 0.044715
ADAM_LR = 0.001
ADAM_B1 = 0.9
ADAM_B2 = 0.999
ADAM_EPS = 1e-08
ADAM_WD = 0.01
ADAM_STEP = 10
LANES = 128
SUBLANES = 8
VMEM_LIMIT = 52 * 1024 * 1024
VMEM_LIMIT_BIG = 58 * 1024 * 1024
MIX_ROWS = 256
SMALL_ROWS = 80

MESH = pl.DeviceIdType.MESH
ANY = pl.BlockSpec(memory_space=pl.ANY)
NN = ((1,), (0,))
NT = ((1,), (1,))
TN = ((0,), (0,))


def _dot(a, b, dims):
    return lax.dot_general(a, b, (dims, ((), ())), preferred_element_type=F32)


def _params(sem=None):
    return pltpu.CompilerParams(dimension_semantics=sem, vmem_limit_bytes=VMEM_LIMIT)


def _full(shape):
    nd = len(shape)
    return pl.BlockSpec(shape, lambda *_: (0,) * nd)


def _exchange(name, gathers, scatters):
    n_g = len(gathers)
    arrs = list(gathers) + list(scatters)
    n = len(arrs)
    out_shape = [jax.ShapeDtypeStruct((N_DEV,) + a.shape, a.dtype) for a in gathers]
    out_shape += [jax.ShapeDtypeStruct(a.shape, a.dtype) for a in scatters]

    def body(*refs):
        ins, outs = refs[:n], refs[n:2 * n]
        send_sems, recv_sems, local_sems = refs[2 * n:]
        x, y, c = lax.axis_index("x"), lax.axis_index("y"), lax.axis_index("c")
        me = 4 * x + 2 * y + c

        def src(a, dev):
            return ins[a] if a < n_g else ins[a].at[dev]

        def peer_of(k):
            px = 1 - x if (k >> 2) & 1 else x
            py = 1 - y if (k >> 1) & 1 else y
            pc = 1 - c if k & 1 else c
            return (px, py, pc), 4 * px + 2 * py + pc

        local = [pltpu.make_async_copy(src(a, me), outs[a].at[me], local_sems.at[a]) for a in range(n)]
        for cp in local:
            cp.start()
        sends = []
        for k in range(1, N_DEV):
            peer, pidx = peer_of(k)
            for a in range(n):
                cp = pltpu.make_async_remote_copy(
                    src_ref=src(a, pidx), dst_ref=outs[a].at[me],
                    send_sem=send_sems.at[a * (N_DEV - 1) + k - 1], recv_sem=recv_sems.at[a * (N_DEV - 1) + k - 1],
                    device_id=peer, device_id_type=MESH)
                cp.start()
                sends.append(cp)
        for k in range(1, N_DEV):
            peer, pidx = peer_of(k)
            for a in range(n):
                pltpu.make_async_remote_copy(
                    src_ref=src(a, pidx), dst_ref=outs[a].at[pidx],
                    send_sem=send_sems.at[a * (N_DEV - 1) + k - 1], recv_sem=recv_sems.at[a * (N_DEV - 1) + k - 1],
                    device_id=peer, device_id_type=MESH).wait_recv()
        for cp in sends:
            cp.wait_send()
        for cp in local:
            cp.wait()

    return pl.pallas_call(
        body, name=name, out_shape=out_shape,
        in_specs=[ANY] * n, out_specs=[ANY] * n,
        scratch_shapes=[pltpu.SemaphoreType.DMA((n * (N_DEV - 1),)),
                        pltpu.SemaphoreType.DMA((n * (N_DEV - 1),)),
                        pltpu.SemaphoreType.DMA((n,))],
    )(*arrs)


def _gather2(name, arrs):
    n = len(arrs)
    per = 7
    out_shape = [jax.ShapeDtypeStruct((N_DEV,) + a.shape, a.dtype) for a in arrs]

    def body(*refs):
        ins, outs = refs[:n], refs[n:2 * n]
        send_sems, recv_sems, local_sems = refs[2 * n:]
        x, y, c = lax.axis_index("x"), lax.axis_index("y"), lax.axis_index("c")
        sib = (x, y, 1 - c)
        chips = [(1 - x, y), (x, 1 - y), (1 - x, 1 - y)]

        def slot(a, px, py, pc):
            return outs[a].at[4 * px + 2 * py + pc]

        def copy(a, k, block, to, src=None):
            return pltpu.make_async_remote_copy(
                src_ref=slot(a, *block) if src is None else src, dst_ref=slot(a, *block),
                send_sem=send_sems.at[a * per + k], recv_sem=recv_sems.at[a * per + k],
                device_id=to, device_id_type=MESH)

        local = [pltpu.make_async_copy(ins[a], slot(a, x, y, c), local_sems.at[a]) for a in range(n)]
        for cp in local:
            cp.start()
        first = []
        for a in range(n):
            first += [copy(a, 1 + j, (x, y, c), (*chip, c), src=ins[a]) for j, chip in enumerate(chips)]
        for a in range(n):
            first.append(copy(a, 0, (x, y, c), sib, src=ins[a]))
        for cp in first:
            cp.start()
        passed = []
        for a in range(n):
            for j, chip in enumerate(chips):
                copy(a, 1 + j, (*chip, c), (x, y, c)).wait_recv()
                cp = copy(a, 4 + j, (*chip, c), sib)
                cp.start()
                passed.append(cp)
        for a in range(n):
            copy(a, 0, sib, (x, y, c)).wait_recv()
            for j, chip in enumerate(chips):
                copy(a, 4 + j, (*chip, 1 - c), (x, y, c)).wait_recv()
        for cp in first + passed:
            cp.wait_send()
        for cp in local:
            cp.wait()

    return pl.pallas_call(
        body, name=name, out_shape=out_shape,
        in_specs=[ANY] * n, out_specs=[ANY] * n,
        scratch_shapes=[pltpu.SemaphoreType.DMA((n * per,)), pltpu.SemaphoreType.DMA((n * per,)),
                        pltpu.SemaphoreType.DMA((n,))],
    )(*arrs)


def _seq_gather2(name, collective_id, arrs):
    n = len(arrs)
    per = 7

    def body(*refs):
        ins, outs = refs[:n], refs[n:2 * n]
        send_sems, recv_sems, local_sems = refs[2 * n:]
        x, y, c = lax.axis_index("x"), lax.axis_index("y"), lax.axis_index("c")
        sib = (x, y, 1 - c)
        chips = [(1 - x, y), (x, 1 - y), (1 - x, 1 - y)]
        barrier = pltpu.get_barrier_semaphore()
        for peer in [sib] + [(*chip, c) for chip in chips]:
            pl.semaphore_signal(barrier, inc=1, device_id=peer, device_id_type=MESH)
        pl.semaphore_wait(barrier, 4)

        def slot(a, px, py, pc):
            return outs[a].at[4 * px + 2 * py + pc]

        def copy(a, k, block, to, src=None):
            return pltpu.make_async_remote_copy(
                src_ref=slot(a, *block) if src is None else src, dst_ref=slot(a, *block),
                send_sem=send_sems.at[a * per + k], recv_sem=recv_sems.at[a * per + k],
                device_id=to, device_id_type=MESH)

        local = [pltpu.make_async_copy(ins[a], slot(a, x, y, c), local_sems.at[a]) for a in range(n)]
        for cp in local:
            cp.start()
        first = []
        for a in range(n):
            first += [copy(a, 1 + j, (x, y, c), (*chip, c), src=ins[a]) for j, chip in enumerate(chips)]
        for a in range(n):
            first.append(copy(a, 0, (x, y, c), sib, src=ins[a]))
        for cp in first:
            cp.start()
        passed = []
        for a in range(n):
            for j, chip in enumerate(chips):
                copy(a, 1 + j, (*chip, c), (x, y, c)).wait_recv()
                cp = copy(a, 4 + j, (*chip, c), sib)
                cp.start()
                passed.append(cp)
        for a in range(n):
            copy(a, 0, sib, (x, y, c)).wait_recv()
            for j, chip in enumerate(chips):
                copy(a, 4 + j, (*chip, 1 - c), (x, y, c)).wait_recv()
        for cp in first + passed:
            cp.wait_send()
        for cp in local:
            cp.wait()

    return pl.kernel(
        body, out_type=[jax.ShapeDtypeStruct((N_DEV,) + a.shape, a.dtype) for a in arrs],
        mesh=plsc.ScalarSubcoreMesh(axis_name="seq", num_cores=1),
        scratch_types=[pltpu.SemaphoreType.DMA((n * per,)), pltpu.SemaphoreType.DMA((n * per,)),
                       pltpu.SemaphoreType.DMA((n,))],
        compiler_params=pltpu.CompilerParams(collective_id=collective_id), name=name,
    )(*arrs)


def _seq_chip_exchange(name, collective_id, arrs):
    n = len(arrs)

    def body(*refs):
        ins, outs = refs[:n], refs[n:2 * n]
        send_sems, recv_sems = refs[2 * n:]
        x, y, c = lax.axis_index("x"), lax.axis_index("y"), lax.axis_index("c")

        def peer(k):
            return (1 - x if (k >> 1) & 1 else x), (1 - y if k & 1 else y)

        barrier = pltpu.get_barrier_semaphore()
        for k in (1, 2, 3):
            pl.semaphore_signal(barrier, inc=1, device_id=(*peer(k), c), device_id_type=MESH)
        pl.semaphore_wait(barrier, 3)

        def copy(a, k):
            px, py = peer(k)
            return pltpu.make_async_remote_copy(
                src_ref=ins[a].at[2 * px + py], dst_ref=outs[a].at[k - 1],
                send_sem=send_sems.at[a * 3 + k - 1], recv_sem=recv_sems.at[a * 3 + k - 1],
                device_id=(px, py, c), device_id_type=MESH)

        cps = [copy(a, k) for a in range(n) for k in (1, 2, 3)]
        for cp in cps:
            cp.start()
        for cp in cps:
            cp.wait_recv()
        for cp in cps:
            cp.wait_send()

    return pl.kernel(
        body, out_type=[jax.ShapeDtypeStruct((3,) + a.shape[1:], a.dtype) for a in arrs],
        mesh=plsc.ScalarSubcoreMesh(axis_name="seq", num_cores=1),
        scratch_types=[pltpu.SemaphoreType.DMA((n * 3,)), pltpu.SemaphoreType.DMA((n * 3,))],
        compiler_params=pltpu.CompilerParams(collective_id=collective_id), name=name,
    )(*arrs)


def _seq_pair_swap(name, collective_id, arrs):
    n = len(arrs)

    def body(*refs):
        ins, outs = refs[:n], refs[n:2 * n]
        send_sems, recv_sems = refs[2 * n:]
        x, y, c = lax.axis_index("x"), lax.axis_index("y"), lax.axis_index("c")
        barrier = pltpu.get_barrier_semaphore()
        pl.semaphore_signal(barrier, inc=1, device_id=(x, y, 1 - c), device_id_type=MESH)
        pl.semaphore_wait(barrier, 1)

        def copy(a, q):
            return pltpu.make_async_remote_copy(
                src_ref=ins[a].at[q, 1 - c], dst_ref=outs[a].at[q],
                send_sem=send_sems.at[a * 4 + q], recv_sem=recv_sems.at[a * 4 + q],
                device_id=(x, y, 1 - c), device_id_type=MESH)

        cps = [copy(a, q) for a in range(n) for q in range(4)]
        for cp in cps:
            cp.start()
        for cp in cps:
            cp.wait_recv()
        for cp in cps:
            cp.wait_send()

    return pl.kernel(
        body, out_type=[jax.ShapeDtypeStruct((4,) + a.shape[2:], a.dtype) for a in arrs],
        mesh=plsc.ScalarSubcoreMesh(axis_name="seq", num_cores=1),
        scratch_types=[pltpu.SemaphoreType.DMA((n * 4,)), pltpu.SemaphoreType.DMA((n * 4,))],
        compiler_params=pltpu.CompilerParams(collective_id=collective_id), name=name,
    )(*arrs)


def _call(body, name, grid, in_specs, out_specs, out_shape, args, scratch=()):
    return pl.pallas_call(
        body, name=name, grid=grid, in_specs=in_specs, out_specs=out_specs, out_shape=out_shape,
        scratch_shapes=list(scratch), compiler_params=_params(("arbitrary",) * len(grid)))(*args)


def _ada_fwd(c_all, w_ada_sh, b_ada_sh):
    nb, d = c_all.shape
    ncol = w_ada_sh.shape[1]

    def body(c_ref, w_ref, b_ref, mod_ref, cact_ref):
        cc = c_ref[...]
        ca = cc * jax.nn.sigmoid(cc)
        cact_ref[...] = ca
        mod_ref[...] = _dot(ca.astype(BF16), w_ref[...].astype(BF16), NN) + b_ref[...]

    return pl.pallas_call(
        body, name="ada_fwd",
        out_shape=[jax.ShapeDtypeStruct((nb, ncol), F32), jax.ShapeDtypeStruct((nb, d), F32)],
        compiler_params=_params(),
    )(c_all, w_ada_sh, b_ada_sh)


def _rms(xv):
    rstd = lax.rsqrt(jnp.mean(xv * xv, axis=-1, keepdims=True) + EPS)
    return xv * rstd, rstd


def _rms_bwd(dxhat, xhat, rstd):
    return rstd * (dxhat - xhat * jnp.mean(dxhat * xhat, axis=-1, keepdims=True))


def _colsum(v):
    return jnp.sum(v, axis=0, keepdims=True)


def _expm1(v, ev):
    series = v * (1.0 + v * (0.5 + v * (1.0 / 6.0 + v * (1.0 / 24.0 + v * (1.0 / 120.0)))))
    return jnp.where(jnp.abs(v) < 0.2, series, ev - 1.0)


def _softplus(v):
    return jnp.maximum(v, 0.0) + jnp.log1p(jnp.exp(-jnp.abs(v)))


def _gelu(v):
    t = jnp.tanh(v * (GELU_K0 + (GELU_K0 * GELU_K1) * (v * v)))
    return 0.5 * v * (1.0 + t), t


def _dgelu(v, t):
    return 0.5 * ((1.0 + t) + (v * (1.0 - t * t)) * (GELU_K0 + (3.0 * GELU_K0 * GELU_K1) * (v * v)))


def _scan_tile(a, b, x0, st, k0, reverse):
    t = a.shape[0]
    off = SUBLANES
    stage_a, stage_b = st.at[k0], st.at[k0 + 1]
    halo = slice(off + t, off + t + SUBLANES) if reverse else slice(0, SUBLANES)
    stage_a[halo, :] = jnp.ones((SUBLANES, a.shape[1]), F32)
    stage_b[halo, :] = jnp.zeros((SUBLANES, a.shape[1]), F32)
    s = 1
    while s < min(t, SUBLANES):
        stage_a[off:off + t, :] = a
        stage_b[off:off + t, :] = b
        at = off + s if reverse else off - s
        b = a * stage_b[at:at + t, :] + b
        a = a * stage_a[at:at + t, :]
        s *= 2
    while s < t:
        if reverse:
            b = jnp.concatenate([a[:t - s] * b[s:] + b[:t - s], b[t - s:]], axis=0)
            a = jnp.concatenate([a[:t - s] * a[s:], a[t - s:]], axis=0)
        else:
            b = jnp.concatenate([b[:s], a[s:] * b[:t - s] + b[s:]], axis=0)
            a = jnp.concatenate([a[:s], a[s:] * a[:t - s]], axis=0)
        s *= 2
    x = b + a * x0
    return x, (x[0:SUBLANES, :] if reverse else x[t - SUBLANES:t, :])


def _lru_gates(u, wa, wx, ba, bx, sp):
    ub = u.astype(BF16)
    r = jax.nn.sigmoid(_dot(ub, wa, NN) + ba)
    i = jax.nn.sigmoid(_dot(ub, wx, NN) + bx)
    log_a = (-RG_C * r) * sp
    a = jnp.exp(log_a)
    mult = jnp.sqrt(-_expm1(log_a, a) * (a + 1.0))
    return ub, r, i, a, mult


def _staged_shifts(stage, v, prev8, next8, downs, ups):
    t = v.shape[0]
    if prev8 is not None:
        stage[0:SUBLANES, :] = prev8
    stage[SUBLANES:SUBLANES + t, :] = v
    if next8 is not None:
        stage[SUBLANES + t:2 * SUBLANES + t, :] = next8
    return ([stage[SUBLANES - k:SUBLANES - k + t, :] for k in downs],
            [stage[SUBLANES + k:SUBLANES + k + t, :] for k in ups])


def _conv3(p, pp, w_ref, lo, stage):
    (p1, p2), _ = _staged_shifts(stage, p, pp, None, (1, 2), ())
    q = (w_ref[0:1, lo:lo + LANES] * p2 + w_ref[1:2, lo:lo + LANES] * p1) + w_ref[2:3, lo:lo + LANES] * p
    return q, p1, p2


def _conv4(xv, xp, w_ref, b_ref, lo, stage):
    (x1, x2, x3), _ = _staged_shifts(stage, xv, xp, None, (1, 2, 3), ())
    u = (((w_ref[0:1, lo:lo + LANES] * x3 + w_ref[1:2, lo:lo + LANES] * x2) + w_ref[2:3, lo:lo + LANES] * x1)
         + w_ref[3:4, lo:lo + LANES] * xv) + b_ref[:, lo:lo + LANES]
    return u, x1, x2, x3


def _mix_in_mixer_fwd(x2d, mod6, g_mix, w_in_t, conv_sc, conv_lru, conv_b, wa_bd, wx_bd, ba, bx, lam, width, tm):
    s, d = x2d.shape
    din = w_in_t.shape[0]
    nt = s // tm
    sub = min(MIX_ROWS, tm)
    nblk = width // LANES

    def body(x_ref, mod_ref, g_ref, w_ref, wsc_ref, wlru_ref, blru_ref, wa_ref, wx_ref, ba_ref, bx_ref, lam_ref,
             hn_ref, proj_ref, ymix_ref, h_ref, buf_ref, halo_ref, hc_ref, stage_ref):
        i = pl.program_id(0)

        @pl.when(i == 0)
        def _():
            buf_ref[1] = jnp.zeros((tm, din), F32)
            halo_ref[...] = jnp.zeros_like(halo_ref)

        @pl.when(i <= 1)
        def _():
            hc_ref[...] = jnp.zeros_like(hc_ref)

        def step(dst, src):
            xhat, _ = _rms(x_ref[...])
            hn = ((xhat * g_ref[...]) * (1.0 + mod_ref[1:2, :]) + mod_ref[0:1, :]).astype(BF16)
            hn_ref[...] = hn
            n_mix = (tm // sub) * nblk
            n_chunk = din // width

            def project(k):
                res = _dot(hn_ref[...], w_ref[k * width:(k + 1) * width, :], NT)
                proj_ref[:, k * width:(k + 1) * width] = res
                dst[:, k * width:(k + 1) * width] = res

            done = 0
            for half in range(tm // sub):
                r0 = half * sub
                rows = slice(r0, r0 + sub)
                for j in range(nblk):
                    lo = j * LANES
                    while done < n_chunk and done * n_mix <= (half * nblk + j) * n_chunk:
                        project(done)
                        done += 1

                    def col(p):
                        return src[rows, p * width + lo:p * width + lo + LANES]

                    def prev(p):
                        c0 = p * width + lo
                        if half == 0:
                            return halo_ref[:, c0:c0 + LANES]
                        return src[r0 - SUBLANES:r0, c0:c0 + LANES]

                    pp = col(1) * col(2)
                    q, _, _ = _conv3(pp, prev(1) * prev(2), wsc_ref, lo, stage_ref.at[0])
                    ymix_ref[rows, lo:lo + LANES] = (col(0) * q).astype(BF16)

                    u, _, _, _ = _conv4(col(4), prev(4), wlru_ref, blru_ref, lo, stage_ref.at[1])
                    sp = _softplus(-lam_ref[:, lo:lo + LANES])
                    _, r, ig, a, mult = _lru_gates(u, wa_ref[j], wx_ref[j], ba_ref[:, lo:lo + LANES],
                                                   bx_ref[:, lo:lo + LANES], sp)
                    h, ends = _scan_tile(a, mult * (ig * u), hc_ref[0:1, lo:lo + LANES], stage_ref, 2, False)
                    h_ref[rows, lo:lo + LANES] = h
                    hc_ref[0:1, lo:lo + LANES] = ends[SUBLANES - 1:SUBLANES, :]
                    gel, _ = _gelu(col(3))
                    ymix_ref[rows, width + lo:width + lo + LANES] = (gel * h).astype(BF16)
            while done < n_chunk:
                project(done)
                done += 1
            halo_ref[...] = src[tm - SUBLANES:tm, :]

        @pl.when(i % 2 == 0)
        def _():
            step(buf_ref.at[0], buf_ref.at[1])

        @pl.when(i % 2 == 1)
        def _():
            step(buf_ref.at[1], buf_ref.at[0])

    small = [conv_sc, conv_lru, conv_b, wa_bd, wx_bd, ba, bx, lam]
    cur = lambda i: (jnp.minimum(i, nt - 1), 0)
    last = lambda i: (jnp.maximum(i - 1, 0), 0)
    outs = _call(
        body, "mix_in_mixer_fwd", (nt + 1,),
        [pl.BlockSpec((tm, d), cur), _full(mod6.shape), _full(g_mix.shape), _full(w_in_t.shape)]
        + [_full(a.shape) for a in small],
        [pl.BlockSpec((tm, d), cur), pl.BlockSpec((tm, din), cur),
         pl.BlockSpec((tm, 2 * width), last), pl.BlockSpec((tm, width), last)],
        [jax.ShapeDtypeStruct((s, d), BF16), jax.ShapeDtypeStruct((s, din), F32),
         jax.ShapeDtypeStruct((s, 2 * width), BF16), jax.ShapeDtypeStruct((s, width), F32)],
        [x2d, mod6, g_mix, w_in_t, *small],
        scratch=[pltpu.VMEM((2, tm, din), F32), pltpu.VMEM((SUBLANES, din), F32), pltpu.VMEM((SUBLANES, width), F32),
                 pltpu.VMEM((4, sub + 2 * SUBLANES, LANES), F32)])
    return outs


def _mix_out_fwd(ymix, x2d, w_out, mod6, g_mlp, tm):
    s, d = x2d.shape

    def body(y_ref, x_ref, w_ref, mod_ref, g_ref, mix_ref, x2_ref, hn_ref):
        mix = _dot(y_ref[...], w_ref[...], NN)
        mix_ref[...] = mix.astype(BF16)
        x2 = x_ref[...] + mod_ref[2:3, :] * mix
        x2_ref[...] = x2
        xhat, _ = _rms(x2)
        hn_ref[...] = ((xhat * g_ref[...]) * (1.0 + mod_ref[4:5, :]) + mod_ref[3:4, :]).astype(BF16)

    tile = pl.BlockSpec((tm, d), lambda i: (i, 0))
    return _call(
        body, "mix_out_fwd", (s // tm,),
        [tile, tile, _full(w_out.shape), _full(mod6.shape), _full(g_mlp.shape)],
        [tile, tile, tile],
        [jax.ShapeDtypeStruct((s, d), BF16), jax.ShapeDtypeStruct((s, d), F32), jax.ShapeDtypeStruct((s, d), BF16)],
        [ymix, x2d, w_out, mod6, g_mlp])


def _mlp_fwd_loss(hn2, w_up_t, w_down, x2, target, mod6, g_final, tm, tk):
    s, d = hn2.shape
    f = w_up_t.shape[0]
    nk = f // tk

    def body(hn_ref, wu_ref, wd_ref, x2_hbm, t_hbm, mod_ref, g_ref, z_ref, dx3_ref, dyb_ref, st_ref,
             y_ref, x2_ref, t_ref, sems):
        i, k = pl.program_id(0), pl.program_id(1)

        def fetch():
            rows = pl.ds(pl.multiple_of(i * tm, tm), tm)
            return (pltpu.make_async_copy(x2_hbm.at[rows, :], x2_ref, sems.at[0]),
                    pltpu.make_async_copy(t_hbm.at[rows, :], t_ref, sems.at[1]))

        @pl.when(jnp.logical_and(i == 0, k == 0))
        def _():
            st_ref[...] = jnp.zeros_like(st_ref)

        @pl.when(k == 0)
        def _():
            for cp in fetch():
                cp.start()
            y_ref[...] = jnp.zeros_like(y_ref)

        z = jnp.maximum(_dot(hn_ref[...], wu_ref[...], NT), 0.0)
        z_ref[...] = z.astype(BF16)
        y_ref[...] += _dot((z * z).astype(BF16), wd_ref[...], NN)

        @pl.when(k == nk - 1)
        def _():
            for cp in fetch():
                cp.wait()
            gate = mod_ref[5:6, :]
            yv = y_ref[...]
            xhat, rstd = _rms(x2_ref[...] + gate * yv)
            diff = xhat * g_ref[...] - t_ref[...]
            dyo = diff * (1.0 / d)
            dx3 = _rms_bwd(dyo * g_ref[...], xhat, rstd)
            dx3_ref[...] = dx3
            dyb_ref[...] = (gate * dx3).astype(BF16)
            st_ref[0:1, :] += _colsum(dyo * xhat)
            st_ref[1:2, :] += _colsum(dx3 * yv)
            st_ref[2:3, :] += _colsum(diff * diff)

    tile = pl.BlockSpec((tm, d), lambda i, k: (i, 0))
    wblk = pl.BlockSpec((tk, d), lambda i, k: (k, 0))
    return pl.pallas_call(
        body, name="mlp_fwd_loss", grid=(s // tm, nk),
        in_specs=[tile, wblk, wblk, ANY, ANY, _full(mod6.shape), _full(g_final.shape)],
        out_specs=[pl.BlockSpec((tm, tk), lambda i, k: (i, k)), tile, tile, _full((SUBLANES, d))],
        out_shape=[jax.ShapeDtypeStruct((s, f), BF16), jax.ShapeDtypeStruct((s, d), F32),
                   jax.ShapeDtypeStruct((s, d), BF16), jax.ShapeDtypeStruct((SUBLANES, d), F32)],
        scratch_shapes=[pltpu.VMEM((tm, d), F32), pltpu.VMEM((tm, d), F32), pltpu.VMEM((tm, d), F32),
                        pltpu.SemaphoreType.DMA((2,))],
        compiler_params=pltpu.CompilerParams(dimension_semantics=("arbitrary", "arbitrary"),
                                             vmem_limit_bytes=VMEM_LIMIT_BIG),
    )(hn2, w_up_t, w_down, x2, target, mod6, g_final)


def _mlp_bwd_dx(dyb, z, w_down, w_up_t, tm, tk):
    s, d = dyb.shape
    f = z.shape[1]

    nk = f // tk

    def body(dy_ref, z_ref, wd_ref, wu_ref, dz_ref, dh_ref, acc_ref):
        k = pl.program_id(1)

        @pl.when(k == 0)
        def _():
            acc_ref[...] = jnp.zeros_like(acc_ref)

        dz = ((2.0 * z_ref[...].astype(F32)) * _dot(dy_ref[...], wd_ref[...], NT)).astype(BF16)
        dz_ref[...] = dz
        acc_ref[...] += _dot(dz, wu_ref[...], NN)

        @pl.when(k == nk - 1)
        def _():
            dh_ref[...] = acc_ref[...].astype(BF16)

    return pl.pallas_call(
        body, name="mlp_bwd_dx", grid=(s // tm, nk),
        in_specs=[pl.BlockSpec((tm, d), lambda i, k: (i, 0)), pl.BlockSpec((tm, tk), lambda i, k: (i, k)),
                  pl.BlockSpec((tk, d), lambda i, k: (k, 0)), pl.BlockSpec((tk, d), lambda i, k: (k, 0))],
        out_specs=[pl.BlockSpec((tm, tk), lambda i, k: (i, k)), pl.BlockSpec((tm, d), lambda i, k: (i, 0))],
        out_shape=[jax.ShapeDtypeStruct((s, f), BF16), jax.ShapeDtypeStruct((s, d), BF16)],
        scratch_shapes=[pltpu.VMEM((tm, d), F32)],
        compiler_params=_params(("parallel", "arbitrary")),
    )(dyb, z, w_down, w_up_t)


def _mlp_bwd_dw(z, dz, dyb, hn2, tm, tk):
    s, d = dyb.shape
    f = z.shape[1]

    def body(z_ref, dz_ref, dy_ref, hn_ref, gd_ref, gu_ref):
        i = pl.program_id(1)

        @pl.when(i == 0)
        def _():
            gd_ref[...] = jnp.zeros_like(gd_ref)
            gu_ref[...] = jnp.zeros_like(gu_ref)

        zf = z_ref[...].astype(F32)
        gd_ref[...] += _dot((zf * zf).astype(BF16), dy_ref[...], TN)
        gu_ref[...] += _dot(dz_ref[...], hn_ref[...], TN)

    return pl.pallas_call(
        body, name="mlp_bwd_dw", grid=(f // tk, s // tm),
        in_specs=[pl.BlockSpec((tm, tk), lambda k, i: (i, k)), pl.BlockSpec((tm, tk), lambda k, i: (i, k)),
                  pl.BlockSpec((tm, d), lambda k, i: (i, 0)), pl.BlockSpec((tm, d), lambda k, i: (i, 0))],
        out_specs=[pl.BlockSpec((tk, d), lambda k, i: (k, 0)), pl.BlockSpec((tk, d), lambda k, i: (k, 0))],
        out_shape=[jax.ShapeDtypeStruct((f, d), F32), jax.ShapeDtypeStruct((f, d), F32)],
        compiler_params=_params(("parallel", "arbitrary")),
    )(z, dz, dyb, hn2)


def _mix_out_bwd(dhn2, x2, dx3, mix, ymix, w_out, mod6, g_mlp, tm):
    s, d = x2.shape

    def body(dh_ref, x2_ref, dx3_ref, mix_ref, y_ref, w_ref, mod_ref, g_ref, dx2_ref, dym_ref, gw_ref, st_ref):
        i = pl.program_id(0)

        @pl.when(i == 0)
        def _():
            st_ref[...] = jnp.zeros_like(st_ref)
            gw_ref[...] = jnp.zeros_like(gw_ref)

        dh = dh_ref[...].astype(F32)
        xhat, rstd = _rms(x2_ref[...])
        dn = dh * (1.0 + mod_ref[4:5, :])
        dx2 = dx3_ref[...] + _rms_bwd(dn * g_ref[...], xhat, rstd)
        dx2_ref[...] = dx2
        st_ref[0:1, :] += _colsum(dh)
        st_ref[1:2, :] += _colsum(dh * (xhat * g_ref[...]))
        st_ref[2:3, :] += _colsum(dn * xhat)
        st_ref[3:4, :] += _colsum(dx2 * mix_ref[...].astype(F32))
        dmix = (mod_ref[2:3, :] * dx2).astype(BF16)
        dym_ref[...] = _dot(dmix, w_ref[...], NT)
        gw_ref[...] += _dot(y_ref[...], dmix, TN)

    tile = pl.BlockSpec((tm, d), lambda i: (i, 0))
    return _call(
        body, "mix_out_bwd", (s // tm,),
        [tile, tile, tile, tile, tile, _full(w_out.shape), _full(mod6.shape), _full(g_mlp.shape)],
        [tile, tile, _full((d, d)), _full((SUBLANES, d))],
        [jax.ShapeDtypeStruct((s, d), F32), jax.ShapeDtypeStruct((s, d), F32),
         jax.ShapeDtypeStruct((d, d), F32), jax.ShapeDtypeStruct((SUBLANES, d), F32)],
        [dhn2, x2, dx3, mix, ymix, w_out, mod6, g_mlp])


def _mixer_bwd(proj, dymix, h_all, conv_sc, conv_lru, conv_b, wa_bd, wx_bd, ba, bx, lam, width):
    s, din = proj.shape
    t = min(MIX_ROWS, s)
    nt = s // t
    nblk = width // LANES
    hb = t // SUBLANES
    last8 = s // SUBLANES - 1

    def body(proj_ref, projp_ref, projn_ref, dy_ref, dyn_ref, h_ref, hp_ref,
             wsc_ref, wlru_ref, blru_ref, wa_ref, wx_ref, ba_ref, bx_ref, lam_ref,
             dproj_ref, small_ref, gwa_ref, gwx_ref, an_ref, gn_ref, dun_ref, stage_ref):
        i = pl.program_id(0)

        @pl.when(i == 0)
        def _():
            small_ref[...] = jnp.zeros_like(small_ref)
            gwa_ref[...] = jnp.zeros_like(gwa_ref)
            gwx_ref[...] = jnp.zeros_like(gwx_ref)
            an_ref[...] = jnp.zeros_like(an_ref)
            gn_ref[...] = jnp.zeros_like(gn_ref)
            dun_ref[...] = jnp.zeros_like(dun_ref)

        has_prev = i < nt - 1
        has_next = i > 0
        for j in range(nblk):
            lo = j * LANES
            ls = slice(lo, lo + LANES)

            def col(p, ref=proj_ref):
                return ref[:, p * width + lo:p * width + lo + LANES]

            def prev(p):
                return jnp.where(has_prev, col(p, projp_ref), 0.0)

            def nxt(p):
                return jnp.where(has_next, col(p, projn_ref), 0.0)

            def add_row(r, v):
                small_ref[r:r + 1, ls] += _colsum(v)

            sc_b, sc_c, sc_x = col(0), col(1), col(2)
            p = sc_c * sc_x
            q, p1, p2 = _conv3(p, prev(1) * prev(2), wsc_ref, lo, stage_ref.at[0])
            dys = dy_ref[:, ls]
            dproj_ref[:, ls] = (dys * q).astype(BF16)
            dq = dys * sc_b
            dqn = jnp.where(has_next, dyn_ref[:, ls], 0.0) * nxt(0)
            _, (dq1, dq2) = _staged_shifts(stage_ref.at[1], dq, None, dqn, (), (1, 2))
            dp = (wsc_ref[2:3, ls] * dq + wsc_ref[1:2, ls] * dq1) + wsc_ref[0:1, ls] * dq2
            dproj_ref[:, width + lo:width + lo + LANES] = (dp * sc_x).astype(BF16)
            dproj_ref[:, 2 * width + lo:2 * width + lo + LANES] = (dp * sc_c).astype(BF16)
            add_row(0, dq * p2)
            add_row(1, dq * p1)
            add_row(2, dq * p)

            xv = col(4)
            u, x1, x2, x3 = _conv4(xv, prev(4), wlru_ref, blru_ref, lo, stage_ref.at[2])
            lam_v = lam_ref[:, ls]
            sp = _softplus(-lam_v)
            wa, wx = wa_ref[j], wx_ref[j]
            ub, r, ig, a, mult = _lru_gates(u, wa, wx, ba_ref[:, ls], bx_ref[:, ls], sp)
            iu = ig * u
            h = h_ref[:, ls]
            (hm1,), _ = _staged_shifts(stage_ref.at[3], h, jnp.where(has_prev, hp_ref[:, ls], 0.0), None, (1,), ())
            lyv = col(3)
            gel, th = _gelu(lyv)
            dyl = dy_ref[:, width + lo:width + lo + LANES]
            dproj_ref[:, 3 * width + lo:3 * width + lo + LANES] = (dyl * h * _dgelu(lyv, th)).astype(BF16)
            a_next = jnp.broadcast_to(an_ref[0:1, ls], (SUBLANES, LANES))
            _, (a_up,) = _staged_shifts(stage_ref.at[4], a, None, a_next, (), (1,))
            g, _ = _scan_tile(a_up, dyl * gel, gn_ref[0:1, ls], stage_ref, 5, True)
            an_ref[0:1, ls] = a[0:1, :]
            gn_ref[0:1, ls] = g[0:1, :]
            da = g * hm1
            dmult = g * iu
            diu = g * mult
            dlog_a = da * a - dmult * ((a * a) / mult)
            dpre_a = (dlog_a * (-RG_C * sp)) * (r * (1.0 - r))
            dpre_x = (diu * u) * (ig * (1.0 - ig))
            dab, dxb = dpre_a.astype(BF16), dpre_x.astype(BF16)
            du = diu * ig + _dot(dab, wa, NT) + _dot(dxb, wx, NT)
            gwa_ref[j] += _dot(ub, dab, TN)
            gwx_ref[j] += _dot(ub, dxb, TN)
            dun = dun_ref[:, ls]
            dun_ref[:, ls] = du[0:SUBLANES, :]
            _, (du1, du2, du3) = _staged_shifts(stage_ref.at[7], du, None, dun, (), (1, 2, 3))
            dlx = (((wlru_ref[3:4, ls] * du + wlru_ref[2:3, ls] * du1) + wlru_ref[1:2, ls] * du2)
                   + wlru_ref[0:1, ls] * du3)
            dproj_ref[:, 4 * width + lo:4 * width + lo + LANES] = dlx.astype(BF16)
            add_row(3, du * x3)
            add_row(4, du * x2)
            add_row(5, du * x1)
            add_row(6, du * xv)
            add_row(7, du)
            add_row(8, dpre_a)
            add_row(9, dpre_x)
            add_row(10, (dlog_a * (RG_C * r)) * jax.nn.sigmoid(-lam_v))

    small = [conv_sc, conv_lru, conv_b, wa_bd, wx_bd, ba, bx, lam]
    rev = lambda i: nt - 1 - i
    return _call(
        body, "mixer_bwd", (nt,),
        [pl.BlockSpec((t, din), lambda i: (rev(i), 0)),
         pl.BlockSpec((SUBLANES, din), lambda i: (jnp.maximum(rev(i) * hb - 1, 0), 0)),
         pl.BlockSpec((SUBLANES, din), lambda i: (jnp.minimum((rev(i) + 1) * hb, last8), 0)),
         pl.BlockSpec((t, 2 * width), lambda i: (rev(i), 0)),
         pl.BlockSpec((SUBLANES, 2 * width), lambda i: (jnp.minimum((rev(i) + 1) * hb, last8), 0)),
         pl.BlockSpec((t, width), lambda i: (rev(i), 0)),
         pl.BlockSpec((SUBLANES, width), lambda i: (jnp.maximum(rev(i) * hb - 1, 0), 0))]
        + [_full(a.shape) for a in small],
        [pl.BlockSpec((t, din), lambda i: (rev(i), 0)), _full((2 * SUBLANES, width)),
         _full(wa_bd.shape), _full(wx_bd.shape)],
        [jax.ShapeDtypeStruct((s, din), BF16), jax.ShapeDtypeStruct((2 * SUBLANES, width), F32),
         jax.ShapeDtypeStruct(wa_bd.shape, F32), jax.ShapeDtypeStruct(wx_bd.shape, F32)],
        [proj, proj, proj, dymix, dymix, h_all, h_all, *small],
        scratch=[pltpu.VMEM((SUBLANES, width), F32), pltpu.VMEM((SUBLANES, width), F32),
                 pltpu.VMEM((SUBLANES, width), F32), pltpu.VMEM((8, t + 2 * SUBLANES, LANES), F32)])


def _mix_in_bwd_dx(dproj, x2d, dx2, w_in_t, mod6, g_mix, tm):
    s, d = x2d.shape
    din = dproj.shape[1]

    def body(dp_ref, x_ref, dx2_ref, w_ref, mod_ref, g_ref, gx_ref, st_ref):
        i = pl.program_id(0)

        @pl.when(i == 0)
        def _():
            st_ref[...] = jnp.zeros_like(st_ref)

        dh = _dot(dp_ref[...], w_ref[...], NN)
        xhat, rstd = _rms(x_ref[...])
        dn = dh * (1.0 + mod_ref[1:2, :])
        gx_ref[...] = dx2_ref[...] + _rms_bwd(dn * g_ref[...], xhat, rstd)
        st_ref[0:1, :] += _colsum(dh)
        st_ref[1:2, :] += _colsum(dh * (xhat * g_ref[...]))
        st_ref[2:3, :] += _colsum(dn * xhat)

    tile = pl.BlockSpec((tm, d), lambda i: (i, 0))
    return _call(
        body, "mix_in_bwd_dx", (s // tm,),
        [pl.BlockSpec((tm, din), lambda i: (i, 0)), tile, tile, _full(w_in_t.shape), _full(mod6.shape),
         _full(g_mix.shape)],
        [tile, _full((SUBLANES, d))],
        [jax.ShapeDtypeStruct((s, d), F32), jax.ShapeDtypeStruct((SUBLANES, d), F32)],
        [dproj, x2d, dx2, w_in_t, mod6, g_mix])


def _mix_in_bwd_dw(dproj, hn1, tm, tn):
    s, d = hn1.shape
    din = dproj.shape[1]

    def body(dp_ref, hn_ref, gw_ref):
        i = pl.program_id(1)

        @pl.when(i == 0)
        def _():
            gw_ref[...] = jnp.zeros_like(gw_ref)

        gw_ref[...] += _dot(dp_ref[...], hn_ref[...], TN)

    return _call(
        body, "mix_in_bwd_dw", (din // tn, s // tm),
        [pl.BlockSpec((tm, tn), lambda p, i: (i, p)), pl.BlockSpec((tm, d), lambda p, i: (i, 0))],
        [pl.BlockSpec((tn, d), lambda p, i: (p, 0))],
        [jax.ShapeDtypeStruct((din, d), F32)],
        [dproj, hn1])


def _adamw(w, g, m, v):
    m = ADAM_B1 * m + (1.0 - ADAM_B1) * g
    v = ADAM_B2 * v + (1.0 - ADAM_B2) * (g * g)
    m_hat = m / (1.0 - ADAM_B1 ** ADAM_STEP)
    v_hat = v / (1.0 - ADAM_B2 ** ADAM_STEP)
    delta = -ADAM_LR * (m_hat / (jnp.sqrt(v_hat) + ADAM_EPS) + ADAM_WD * w)
    return delta, m, v


def _pair_sum(g4s, h4s, core_chip, tr, name):
    na = len(g4s)
    _, _, r, n = g4s[0].shape

    def body(sc_ref, *refs):
        q = pl.program_id(1)
        for a in range(na):
            g_ref, h_ref = refs[2 * a], refs[2 * a + 1]
            sb_ref, own_ref = refs[2 * na + 2 * a], refs[2 * na + 2 * a + 1]
            ssum = g_ref[...] + h_ref[...]
            sb_ref[...] = ssum.astype(BF16)

            @pl.when(q == sc_ref[1])
            def _():
                own_ref[...] = ssum

    grid_spec = pltpu.PrefetchScalarGridSpec(
        num_scalar_prefetch=1, grid=(r // tr, 4),
        in_specs=[pl.BlockSpec((None, None, tr, n), lambda i, q, sc: (q, sc[0], i, 0)),
                  pl.BlockSpec((None, tr, n), lambda i, q, sc: (q, i, 0))] * na,
        out_specs=[pl.BlockSpec((None, tr, n), lambda i, q, sc: (q, i, 0)),
                   pl.BlockSpec((tr, n), lambda i, q, sc: (i, 0))] * na)
    outs = pl.pallas_call(
        body, name=name, grid_spec=grid_spec,
        out_shape=[jax.ShapeDtypeStruct((4, r, n), BF16), jax.ShapeDtypeStruct((r, n), F32)] * na,
        compiler_params=_params(("parallel", "arbitrary")),
    )(core_chip, *[x for pair in zip(g4s, h4s) for x in pair])
    return [(outs[2 * a], outs[2 * a + 1]) for a in range(na)]


def _sum4_adam(own, parts, w, m, v, tr, name, transposed):
    r, n = own.shape
    rows, cols = w.shape

    def body(o_ref, p_ref, w_ref, m_ref, v_ref, g_ref, d_ref, nm_ref, nv_ref):
        g = o_ref[...]
        for k in range(3):
            g = g + p_ref[k].astype(F32)
        if transposed:
            g = g.T
        g_ref[...] = g
        d_ref[...], nm_ref[...], nv_ref[...] = _adamw(w_ref[...], g, m_ref[...], v_ref[...])

    if transposed:
        g_specs = [pl.BlockSpec((r, tr), lambda i: (0, i)), pl.BlockSpec((3, r, tr), lambda i: (0, 0, i))]
    else:
        g_specs = [pl.BlockSpec((tr, n), lambda i: (i, 0)), pl.BlockSpec((3, tr, n), lambda i: (0, i, 0))]
    tile = pl.BlockSpec((tr, cols), lambda i: (i, 0))
    return pl.pallas_call(
        body, name=name, grid=(rows // tr,),
        in_specs=g_specs + [tile] * 3, out_specs=[tile] * 4,
        out_shape=[jax.ShapeDtypeStruct((rows, cols), F32)] * 4,
        compiler_params=_params(("parallel",)),
    )(own, parts, w, m, v)


def _sum8(parts, tr, name):
    _, rows, n = parts.shape

    def body(p_ref, o_ref):
        acc = p_ref[0]
        for k in range(1, N_DEV):
            acc = acc + p_ref[k]
        o_ref[...] = acc

    return pl.pallas_call(
        body, name=name, grid=(rows // tr,),
        in_specs=[pl.BlockSpec((N_DEV, tr, n), lambda i: (0, i, 0))],
        out_specs=pl.BlockSpec((tr, n), lambda i: (i, 0)),
        out_shape=jax.ShapeDtypeStruct((rows, n), F32),
        compiler_params=_params(("parallel",)),
    )(parts)


def _ada_bwd_adam(cact_t, dmod_cols, w, m, v, tr):
    rows, n = w.shape

    def body(c_ref, d_ref, w_ref, m_ref, v_ref, g_ref, dl_ref, nm_ref, nv_ref):
        def term(b):
            return c_ref[b].astype(BF16).astype(F32) * d_ref[b:b + 1, :].astype(BF16).astype(F32)

        g = term(0)
        for b in range(1, N_DEV):
            g = g + term(b)
        g_ref[...] = g
        dl_ref[...], nm_ref[...], nv_ref[...] = _adamw(w_ref[...], g, m_ref[...], v_ref[...])

    tile = pl.BlockSpec((tr, n), lambda i: (i, 0))
    return pl.pallas_call(
        body, name="ada_bwd_adam", grid=(rows // tr,),
        in_specs=[pl.BlockSpec((N_DEV, tr, 1), lambda i: (0, i, 0)), _full(dmod_cols.shape), tile, tile, tile],
        out_specs=[tile] * 4,
        out_shape=[jax.ShapeDtypeStruct((rows, n), F32)] * 4,
        compiler_params=_params(("parallel",)),
    )(cact_t, dmod_cols, w, m, v)


def _adam_small(ws, gs, ms, vs):
    n = len(ws)

    def body(*refs):
        w_r, g_r, m_r, v_r = refs[:n], refs[n:2 * n], refs[2 * n:3 * n], refs[3 * n:4 * n]
        d_r, nm_r, nv_r = refs[4 * n:5 * n], refs[5 * n:6 * n], refs[6 * n:7 * n]
        for k in range(n):
            d_r[k][...], nm_r[k][...], nv_r[k][...] = _adamw(w_r[k][...], g_r[k][...], m_r[k][...], v_r[k][...])

    shapes = [jax.ShapeDtypeStruct(w.shape, F32) for w in ws]
    outs = pl.pallas_call(
        body, name="adam_small", out_shape=shapes * 3, compiler_params=_params(),
    )(*ws, *gs, *ms, *vs)
    return outs[:n], outs[n:2 * n], outs[2 * n:]


def _block_diag(w):
    h, hd, _ = w.shape
    per = LANES // hd
    eye = jnp.eye(per, dtype=w.dtype)
    w5 = w.reshape(h // per, per, hd, 1, hd) * eye[None, :, None, :, None]
    return w5.reshape(h // per, LANES, LANES)


def _block_diag_grad(g, h, hd):
    per = LANES // hd
    g5 = g.reshape(h // per, per, hd, per, hd)
    return jnp.stack([g5[:, a, :, a, :] for a in range(per)], axis=1).reshape(h, hd, hd)


def kernel(x, c, w_ada, b_ada, g_mix, w_in, conv_w_sc, conv_w_lru, conv_b_lru, w_rg_a, b_rg_a, w_rg_x, b_rg_x, lru_lambda, w_out, g_mlp, w_up, w_down, g_final, loss_target, m_w_ada, m_b_ada, m_g_mix, m_w_in, m_conv_w_sc, m_conv_w_lru, m_conv_b_lru, m_w_rg_a, m_b_rg_a, m_w_rg_x, m_b_rg_x, m_lru_lambda, m_w_out, m_g_mlp, m_w_up, m_w_down, m_g_final, v_w_ada, v_b_ada, v_g_mix, v_w_in, v_conv_w_sc, v_conv_w_lru, v_conv_b_lru, v_w_rg_a, v_b_rg_a, v_w_rg_x, v_b_rg_x, v_lru_lambda, v_w_out, v_g_mlp, v_w_up, v_w_down, v_g_final):
    s, d = x.shape[1], x.shape[2]
    width = conv_b_lru.shape[1]
    heads, hd = w_rg_a.shape[1], w_rg_a.shape[2]
    f = w_down.shape[1] * N_DEV
    n_ada = w_ada.shape[2]
    csh = conv_w_sc.shape[2]
    me = 4 * lax.axis_index("x") + 2 * lax.axis_index("y") + lax.axis_index("c")
    tm = min(512, s)
    tm_mlp = min(1024, s)
    tk = 512

    x2d = x[0]
    tgt = loss_target[0]

    pay = jnp.zeros((SUBLANES, d), F32)
    pay = pay.at[0:1, :].set(c)
    pay = pay.at[1:4, 0:csh].set(conv_w_sc[0])
    pay = pay.at[4:8, 0:csh].set(conv_w_lru[0])
    w_in_t_sh = w_in[0].T.astype(BF16)
    w_up_t_sh = w_up[0].T.astype(BF16)
    w_out_sh = w_out[0].astype(BF16)
    w_down_sh = w_down[0].astype(BF16)
    pay_all, w_in_t = _gather2("gather_in", [pay, w_in_t_sh])
    w_in_t = w_in_t.reshape(-1, d)
    c_all = pay_all[:, 0, :]
    conv_sc = pay_all[:, 1:4, 0:csh].transpose(1, 0, 2).reshape(3, width)
    conv_lru = pay_all[:, 4:8, 0:csh].transpose(1, 0, 2).reshape(4, width)

    b_ada_sh = lax.dynamic_slice(b_ada, (0, me * n_ada), (1, n_ada))
    mod_cols, c_act = _ada_fwd(c_all, w_ada[0], b_ada_sh)
    (mod_rows,) = _exchange("scatter_mod", [], [mod_cols.reshape(N_DEV, 1, n_ada)])
    mod_rows, w_out_sh, w_up_t_sh, w_down_sh = lax.optimization_barrier((mod_rows, w_out_sh, w_up_t_sh, w_down_sh))
    (w_out_g,) = _seq_gather2("gather_w_out", 1, [w_out_sh])
    w_up_g, w_down_g = _seq_gather2("gather_mlp_weights", 2, [w_up_t_sh, w_down_sh])
    mod6 = jnp.zeros((SUBLANES, d), F32).at[0:6, :].set(mod_rows.reshape(6, d))

    wa_bd = _block_diag(w_rg_a[0]).astype(BF16)
    wx_bd = _block_diag(w_rg_x[0]).astype(BF16)
    ba = b_rg_a.reshape(1, width)
    bx = b_rg_x.reshape(1, width)
    g_fin = g_final.reshape(1, d)

    hn1, proj, ymix, h_all = _mix_in_mixer_fwd(x2d, mod6, g_mix, w_in_t, conv_sc, conv_lru, conv_b_lru,
                                               wa_bd, wx_bd, ba, bx, lru_lambda, width, tm)
    w_out_b = w_out_g.reshape(-1, d)
    mix, x2, hn2 = _mix_out_fwd(ymix, x2d, w_out_b, mod6, g_mlp, tm_mlp)
    w_up_t = w_up_g.reshape(-1, d)
    w_down_b = w_down_g.reshape(-1, d)
    z, dx3, dyb, st_fin = _mlp_fwd_loss(hn2, w_up_t, w_down_b, x2, tgt, mod6, g_fin, tm_mlp, 2 * tk)

    core_chip = jnp.stack([lax.axis_index("c"), 2 * lax.axis_index("x") + lax.axis_index("y")]).astype(jnp.int32)
    dz, dhn2 = _mlp_bwd_dx(dyb, z, w_down_b, w_up_t, tm_mlp, 2 * tk)
    g_down, g_up_t = _mlp_bwd_dw(z, dz, dyb, hn2, tm_mlp, 2 * tk)
    g_up4, g_down4 = g_up_t.reshape(4, 2, -1, d), g_down.reshape(4, 2, -1, d)
    h_up, h_down = _seq_pair_swap("swap_mlp_grads", 7, [g_up4, g_down4])
    dx2, dymix, g_out, st_out = _mix_out_bwd(dhn2, x2, dx3, mix, ymix, w_out_b, mod6, g_mlp, tm)
    h_up, h_down, g_out = lax.optimization_barrier((h_up, h_down, g_out))
    (sb_up, own_up), (sb_down, own_down) = _pair_sum([g_up4, g_down4], [h_up, h_down], core_chip, 256, "pair_sum_mlp")
    g_out4 = g_out.reshape(4, 2, -1, d)
    (h_out,) = _seq_pair_swap("swap_w_out_grad", 8, [g_out4])
    p_up, p_down = _seq_chip_exchange("exchange_mlp_grads", 3, [sb_up, sb_down])
    dproj, g_small, g_wa, g_wx = _mixer_bwd(
        proj, dymix, h_all, conv_sc, conv_lru, conv_b_lru, wa_bd, wx_bd, ba, bx, lru_lambda, width)
    h_out, dproj = lax.optimization_barrier((h_out, dproj))
    ((sb_out, own_out),) = _pair_sum([g_out4], [h_out], core_chip, g_out4.shape[2], "pair_sum_w_out")
    (p_out,) = _seq_chip_exchange("exchange_w_out_grad", 4, [sb_out])
    grad_x, st_in = _mix_in_bwd_dx(dproj, x2d, dx2, w_in_t, mod6, g_mix, tm)

    small = jnp.concatenate([
        st_in[0:2], st_out[3:4], st_out[0:2], st_fin[1:2],
        st_in[2:3], st_out[2:3], st_fin[0:1],
        jnp.concatenate([g_small[7:8], g_small[10:11]], axis=1),
        jnp.concatenate([g_small[8:9], g_small[9:10]], axis=1),
        jnp.concatenate([jnp.concatenate([g_small[0:3], jnp.zeros((1, width), F32)], axis=0), g_small[3:7]], axis=1),
        st_fin[2:3],
        _block_diag_grad(g_wa, heads, hd).reshape(-1, d),
        _block_diag_grad(g_wx, heads, hd).reshape(-1, d),
    ], axis=0)

    (small_all,) = _seq_gather2("gather_small_grads", 5, [small])
    g_in_t, = _mix_in_bwd_dw(dproj, hn1, min(2048, s), dproj.shape[1] // 2)
    g_in4 = g_in_t.reshape(4, 2, -1, d)
    (h_in,) = _seq_pair_swap("swap_w_in_grad", 9, [g_in4])
    p_up, p_down, p_out, small_all, g_in_t = lax.optimization_barrier((p_up, p_down, p_out, small_all, g_in_t))

    ad_up = _sum4_adam(own_up, p_up, w_up[0], m_w_up[0], v_w_up[0], 256, "adam_w_up", True)
    h_in, ad_up = lax.optimization_barrier((h_in, ad_up))
    ((sb_in, own_in),) = _pair_sum([g_in4], [h_in], core_chip, g_in4.shape[2], "pair_sum_w_in")
    (p_in,) = _seq_chip_exchange("exchange_w_in_grad", 6, [sb_in])
    ad_out = _sum4_adam(own_out, p_out, w_out[0], m_w_out[0], v_w_out[0], w_out.shape[1], "adam_w_out", False)
    ad_down = _sum4_adam(own_down, p_down, w_down[0], m_w_down[0], v_w_down[0], 256, "adam_w_down", False)

    gsum = _sum8(small_all, SMALL_ROWS, "sum_small")
    loss = (0.5 / d) * jnp.sum(gsum[15])
    dmod_cols = lax.dynamic_slice(small_all[:, 0:6, :].reshape(N_DEV, 6 * d), (0, me * n_ada), (N_DEV, n_ada))
    g_ada, d_ada, nm_ada, nv_ada = _ada_bwd_adam(c_act[:, :, None], dmod_cols, w_ada[0], m_w_ada[0], v_w_ada[0], 256)

    g_conv = lax.dynamic_slice(gsum[11:15, 0:width], (0, me * csh), (4, csh))
    g_conv_l = lax.dynamic_slice(gsum[11:15, width:2 * width], (0, me * csh), (4, csh))
    small_g = [
        gsum[0:6].reshape(1, 6 * d),
        gsum[6:7],
        g_conv[0:3].reshape(1, 3, csh),
        g_conv_l.reshape(1, 4, csh),
        gsum[9:10, 0:width],
        gsum[16:48].reshape(1, heads, hd, hd),
        gsum[10:11, 0:width].reshape(1, heads, hd),
        gsum[48:80].reshape(1, heads, hd, hd),
        gsum[10:11, width:].reshape(1, heads, hd),
        gsum[9:10, width:],
        gsum[7:8],
        gsum[8],
    ]
    small_w = [b_ada, g_mix, conv_w_sc, conv_w_lru, conv_b_lru, w_rg_a, b_rg_a, w_rg_x, b_rg_x, lru_lambda, g_mlp, g_final]
    small_m = [m_b_ada, m_g_mix, m_conv_w_sc, m_conv_w_lru, m_conv_b_lru, m_w_rg_a, m_b_rg_a, m_w_rg_x, m_b_rg_x,
               m_lru_lambda, m_g_mlp, m_g_final]
    small_v = [v_b_ada, v_g_mix, v_conv_w_sc, v_conv_w_lru, v_conv_b_lru, v_w_rg_a, v_b_rg_a, v_w_rg_x, v_b_rg_x,
               v_lru_lambda, v_g_mlp, v_g_final]
    sd, snm, snv = _adam_small(small_w, small_g, small_m, small_v)
    p_in, ad_out, ad_down, (g_ada, d_ada, nm_ada, nv_ada), sd = lax.optimization_barrier(
        (p_in, ad_out, ad_down, (g_ada, d_ada, nm_ada, nv_ada), sd))
    ad_in = _sum4_adam(own_in, p_in, w_in[0].T, m_w_in[0].T, v_w_in[0].T, own_in.shape[0], "adam_w_in", False)
    ad_in = [a.T for a in ad_in]

    def order(ada, w_in_, w_out_, w_up_, w_down_, sm):
        return [ada[None], sm[0], sm[1], w_in_[None], sm[2], sm[3], sm[4], sm[5], sm[6], sm[7], sm[8], sm[9],
                w_out_[None], sm[10], w_up_[None], w_down_[None], sm[11]]

    grads = order(g_ada, ad_in[0], ad_out[0], ad_up[0], ad_down[0], small_g)
    deltas = order(d_ada, ad_in[1], ad_out[1], ad_up[1], ad_down[1], sd)
    new_m = order(nm_ada, ad_in[2], ad_out[2], ad_up[2], ad_down[2], snm)
    new_v = order(nv_ada, ad_in[3], ad_out[3], ad_up[3], ad_down[3], snv)
    return (loss, grad_x[None], *grads, *deltas, *new_m, *new_v)
```

```python
import jax
import jax.numpy as jnp
from jax import lax
from jax.experimental import pallas as pl
from jax.experimental.pallas import tpu as pltpu
from jax.experimental.pallas import tpu_sc as plsc

F32 = jnp.float32
BF16 = jnp.bfloat16
N_DEV = 8
EPS = 1e-6
RG_C = 8.0
GELU_K0 = 0.7978845608028654
GELU_K1 = 0.044715
ADAM_LR = 0.001
ADAM_B1 = 0.9
ADAM_B2 = 0.999
ADAM_EPS = 1e-08
ADAM_WD = 0.01
ADAM_STEP = 10
LANES = 128
SUBLANES = 8
VMEM_LIMIT = 52 * 1024 * 1024
VMEM_LIMIT_BIG = 58 * 1024 * 1024
MIX_ROWS = 256
SMALL_ROWS = 80

MESH = pl.DeviceIdType.MESH
ANY = pl.BlockSpec(memory_space=pl.ANY)
NN = ((1,), (0,))
NT = ((1,), (1,))
TN = ((0,), (0,))


def _dot(a, b, dims):
    return lax.dot_general(a, b, (dims, ((), ())), preferred_element_type=F32)


def _params(sem=None):
    return pltpu.CompilerParams(dimension_semantics=sem, vmem_limit_bytes=VMEM_LIMIT)


def _full(shape):
    nd = len(shape)
    return pl.BlockSpec(shape, lambda *_: (0,) * nd)


def _exchange(name, gathers, scatters):
    n_g = len(gathers)
    arrs = list(gathers) + list(scatters)
    n = len(arrs)
    out_shape = [jax.ShapeDtypeStruct((N_DEV,) + a.shape, a.dtype) for a in gathers]
    out_shape += [jax.ShapeDtypeStruct(a.shape, a.dtype) for a in scatters]

    def body(*refs):
        ins, outs = refs[:n], refs[n:2 * n]
        send_sems, recv_sems, local_sems = refs[2 * n:]
        x, y, c = lax.axis_index("x"), lax.axis_index("y"), lax.axis_index("c")
        me = 4 * x + 2 * y + c

        def src(a, dev):
            return ins[a] if a < n_g else ins[a].at[dev]

        def peer_of(k):
            px = 1 - x if (k >> 2) & 1 else x
            py = 1 - y if (k >> 1) & 1 else y
            pc = 1 - c if k & 1 else c
            return (px, py, pc), 4 * px + 2 * py + pc

        local = [pltpu.make_async_copy(src(a, me), outs[a].at[me], local_sems.at[a]) for a in range(n)]
        for cp in local:
            cp.start()
        sends = []
        for k in range(1, N_DEV):
            peer, pidx = peer_of(k)
            for a in range(n):
                cp = pltpu.make_async_remote_copy(
                    src_ref=src(a, pidx), dst_ref=outs[a].at[me],
                    send_sem=send_sems.at[a * (N_DEV - 1) + k - 1], recv_sem=recv_sems.at[a * (N_DEV - 1) + k - 1],
                    device_id=peer, device_id_type=MESH)
                cp.start()
                sends.append(cp)
        for k in range(1, N_DEV):
            peer, pidx = peer_of(k)
            for a in range(n):
                pltpu.make_async_remote_copy(
                    src_ref=src(a, pidx), dst_ref=outs[a].at[pidx],
                    send_sem=send_sems.at[a * (N_DEV - 1) + k - 1], recv_sem=recv_sems.at[a * (N_DEV - 1) + k - 1],
                    device_id=peer, device_id_type=MESH).wait_recv()
        for cp in sends:
            cp.wait_send()
        for cp in local:
            cp.wait()

    return pl.pallas_call(
        body, name=name, out_shape=out_shape,
        in_specs=[ANY] * n, out_specs=[ANY] * n,
        scratch_shapes=[pltpu.SemaphoreType.DMA((n * (N_DEV - 1),)),
                        pltpu.SemaphoreType.DMA((n * (N_DEV - 1),)),
                        pltpu.SemaphoreType.DMA((n,))],
    )(*arrs)


def _gather2(name, arrs):
    n = len(arrs)
    per = 7
    out_shape = [jax.ShapeDtypeStruct((N_DEV,) + a.shape, a.dtype) for a in arrs]

    def body(*refs):
        ins, outs = refs[:n], refs[n:2 * n]
        send_sems, recv_sems, local_sems = refs[2 * n:]
        x, y, c = lax.axis_index("x"), lax.axis_index("y"), lax.axis_index("c")
        sib = (x, y, 1 - c)
        chips = [(1 - x, y), (x, 1 - y), (1 - x, 1 - y)]

        def slot(a, px, py, pc):
            return outs[a].at[4 * px + 2 * py + pc]

        def copy(a, k, block, to, src=None):
            return pltpu.make_async_remote_copy(
                src_ref=slot(a, *block) if src is None else src, dst_ref=slot(a, *block),
                send_sem=send_sems.at[a * per + k], recv_sem=recv_sems.at[a * per + k],
                device_id=to, device_id_type=MESH)

        local = [pltpu.make_async_copy(ins[a], slot(a, x, y, c), local_sems.at[a]) for a in range(n)]
        for cp in local:
            cp.start()
        first = []
        for a in range(n):
            first += [copy(a, 1 + j, (x, y, c), (*chip, c), src=ins[a]) for j, chip in enumerate(chips)]
        for a in range(n):
            first.append(copy(a, 0, (x, y, c), sib, src=ins[a]))
        for cp in first:
            cp.start()
        passed = []
        for a in range(n):
            for j, chip in enumerate(chips):
                copy(a, 1 + j, (*chip, c), (x, y, c)).wait_recv()
                cp = copy(a, 4 + j, (*chip, c), sib)
                cp.start()
                passed.append(cp)
        for a in range(n):
            copy(a, 0, sib, (x, y, c)).wait_recv()
            for j, chip in enumerate(chips):
                copy(a, 4 + j, (*chip, 1 - c), (x, y, c)).wait_recv()
        for cp in first + passed:
            cp.wait_send()
        for cp in local:
            cp.wait()

    return pl.pallas_call(
        body, name=name, out_shape=out_shape,
        in_specs=[ANY] * n, out_specs=[ANY] * n,
        scratch_shapes=[pltpu.SemaphoreType.DMA((n * per,)), pltpu.SemaphoreType.DMA((n * per,)),
                        pltpu.SemaphoreType.DMA((n,))],
    )(*arrs)


def _seq_gather2(name, collective_id, arrs):
    n = len(arrs)
    per = 7

    def body(*refs):
        ins, outs = refs[:n], refs[n:2 * n]
        send_sems, recv_sems, local_sems = refs[2 * n:]
        x, y, c = lax.axis_index("x"), lax.axis_index("y"), lax.axis_index("c")
        sib = (x, y, 1 - c)
        chips = [(1 - x, y), (x, 1 - y), (1 - x, 1 - y)]
        barrier = pltpu.get_barrier_semaphore()
        for peer in [sib] + [(*chip, c) for chip in chips]:
            pl.semaphore_signal(barrier, inc=1, device_id=peer, device_id_type=MESH)
        pl.semaphore_wait(barrier, 4)

        def slot(a, px, py, pc):
            return outs[a].at[4 * px + 2 * py + pc]

        def copy(a, k, block, to, src=None):
            return pltpu.make_async_remote_copy(
                src_ref=slot(a, *block) if src is None else src, dst_ref=slot(a, *block),
                send_sem=send_sems.at[a * per + k], recv_sem=recv_sems.at[a * per + k],
                device_id=to, device_id_type=MESH)

        local = [pltpu.make_async_copy(ins[a], slot(a, x, y, c), local_sems.at[a]) for a in range(n)]
        for cp in local:
            cp.start()
        first = []
        for a in range(n):
            first += [copy(a, 1 + j, (x, y, c), (*chip, c), src=ins[a]) for j, chip in enumerate(chips)]
        for a in range(n):
            first.append(copy(a, 0, (x, y, c), sib, src=ins[a]))
        for cp in first:
            cp.start()
        passed = []
        for a in range(n):
            for j, chip in enumerate(chips):
                copy(a, 1 + j, (*chip, c), (x, y, c)).wait_recv()
                cp = copy(a, 4 + j, (*chip, c), sib)
                cp.start()
                passed.append(cp)
        for a in range(n):
            copy(a, 0, sib, (x, y, c)).wait_recv()
            for j, chip in enumerate(chips):
                copy(a, 4 + j, (*chip, 1 - c), (x, y, c)).wait_recv()
        for cp in first + passed:
            cp.wait_send()
        for cp in local:
            cp.wait()

    return pl.kernel(
        body, out_type=[jax.ShapeDtypeStruct((N_DEV,) + a.shape, a.dtype) for a in arrs],
        mesh=plsc.ScalarSubcoreMesh(axis_name="seq", num_cores=1),
        scratch_types=[pltpu.SemaphoreType.DMA((n * per,)), pltpu.SemaphoreType.DMA((n * per,)),
                       pltpu.SemaphoreType.DMA((n,))],
        compiler_params=pltpu.CompilerParams(collective_id=collective_id), name=name,
    )(*arrs)


def _seq_chip_exchange(name, collective_id, arrs):
    n = len(arrs)

    def body(*refs):
        ins, outs = refs[:n], refs[n:2 * n]
        send_sems, recv_sems = refs[2 * n:]
        x, y, c = lax.axis_index("x"), lax.axis_index("y"), lax.axis_index("c")

        def peer(k):
            return (1 - x if (k >> 1) & 1 else x), (1 - y if k & 1 else y)

        barrier = pltpu.get_barrier_semaphore()
        for k in (1, 2, 3):
            pl.semaphore_signal(barrier, inc=1, device_id=(*peer(k), c), device_id_type=MESH)
        pl.semaphore_wait(barrier, 3)

        def copy(a, k):
            px, py = peer(k)
            return pltpu.make_async_remote_copy(
                src_ref=ins[a].at[2 * px + py], dst_ref=outs[a].at[k - 1],
                send_sem=send_sems.at[a * 3 + k - 1], recv_sem=recv_sems.at[a * 3 + k - 1],
                device_id=(px, py, c), device_id_type=MESH)

        cps = [copy(a, k) for a in range(n) for k in (1, 2, 3)]
        for cp in cps:
            cp.start()
        for cp in cps:
            cp.wait_recv()
        for cp in cps:
            cp.wait_send()

    return pl.kernel(
        body, out_type=[jax.ShapeDtypeStruct((3,) + a.shape[1:], a.dtype) for a in arrs],
        mesh=plsc.ScalarSubcoreMesh(axis_name="seq", num_cores=1),
        scratch_types=[pltpu.SemaphoreType.DMA((n * 3,)), pltpu.SemaphoreType.DMA((n * 3,))],
        compiler_params=pltpu.CompilerParams(collective_id=collective_id), name=name,
    )(*arrs)


def _seq_pair_swap(name, collective_id, arrs):
    n = len(arrs)

    def body(*refs):
        ins, outs = refs[:n], refs[n:2 * n]
        send_sems, recv_sems = refs[2 * n:]
        x, y, c = lax.axis_index("x"), lax.axis_index("y"), lax.axis_index("c")
        barrier = pltpu.get_barrier_semaphore()
        pl.semaphore_signal(barrier, inc=1, device_id=(x, y, 1 - c), device_id_type=MESH)
        pl.semaphore_wait(barrier, 1)

        def copy(a, q):
            return pltpu.make_async_remote_copy(
                src_ref=ins[a].at[q, 1 - c], dst_ref=outs[a].at[q],
                send_sem=send_sems.at[a * 4 + q], recv_sem=recv_sems.at[a * 4 + q],
                device_id=(x, y, 1 - c), device_id_type=MESH)

        cps = [copy(a, q) for a in range(n) for q in range(4)]
        for cp in cps:
            cp.start()
        for cp in cps:
            cp.wait_recv()
        for cp in cps:
            cp.wait_send()

    return pl.kernel(
        body, out_type=[jax.ShapeDtypeStruct((4,) + a.shape[2:], a.dtype) for a in arrs],
        mesh=plsc.ScalarSubcoreMesh(axis_name="seq", num_cores=1),
        scratch_types=[pltpu.SemaphoreType.DMA((n * 4,)), pltpu.SemaphoreType.DMA((n * 4,))],
        compiler_params=pltpu.CompilerParams(collective_id=collective_id), name=name,
    )(*arrs)


def _call(body, name, grid, in_specs, out_specs, out_shape, args, scratch=()):
    return pl.pallas_call(
        body, name=name, grid=grid, in_specs=in_specs, out_specs=out_specs, out_shape=out_shape,
        scratch_shapes=list(scratch), compiler_params=_params(("arbitrary",) * len(grid)))(*args)


def _ada_fwd(c_all, w_ada_sh, b_ada_sh):
    nb, d = c_all.shape
    ncol = w_ada_sh.shape[1]

    def body(c_ref, w_ref, b_ref, mod_ref, cact_ref):
        cc = c_ref[...]
        ca = cc * jax.nn.sigmoid(cc)
        cact_ref[...] = ca
        mod_ref[...] = _dot(ca.astype(BF16), w_ref[...].astype(BF16), NN) + b_ref[...]

    return pl.pallas_call(
        body, name="ada_fwd",
        out_shape=[jax.ShapeDtypeStruct((nb, ncol), F32), jax.ShapeDtypeStruct((nb, d), F32)],
        compiler_params=_params(),
    )(c_all, w_ada_sh, b_ada_sh)


def _rms(xv):
    rstd = lax.rsqrt(jnp.mean(xv * xv, axis=-1, keepdims=True) + EPS)
    return xv * rstd, rstd


def _rms_bwd(dxhat, xhat, rstd):
    return rstd * (dxhat - xhat * jnp.mean(dxhat * xhat, axis=-1, keepdims=True))


def _colsum(v):
    return jnp.sum(v, axis=0, keepdims=True)


def _expm1(v, ev):
    series = v * (1.0 + v * (0.5 + v * (1.0 / 6.0 + v * (1.0 / 24.0 + v * (1.0 / 120.0)))))
    return jnp.where(jnp.abs(v) < 0.2, series, ev - 1.0)


def _softplus(v):
    return jnp.maximum(v, 0.0) + jnp.log1p(jnp.exp(-jnp.abs(v)))


def _gelu(v):
    t = jnp.tanh(v * (GELU_K0 + (GELU_K0 * GELU_K1) * (v * v)))
    return 0.5 * v * (1.0 + t), t


def _dgelu(v, t):
    return 0.5 * ((1.0 + t) + (v * (1.0 - t * t)) * (GELU_K0 + (3.0 * GELU_K0 * GELU_K1) * (v * v)))


def _scan_tile(a, b, x0, st, k0, reverse):
    t = a.shape[0]
    off = SUBLANES
    stage_a, stage_b = st.at[k0], st.at[k0 + 1]
    halo = slice(off + t, off + t + SUBLANES) if reverse else slice(0, SUBLANES)
    stage_a[halo, :] = jnp.ones((SUBLANES, a.shape[1]), F32)
    stage_b[halo, :] = jnp.zeros((SUBLANES, a.shape[1]), F32)
    s = 1
    while s < min(t, SUBLANES):
        stage_a[off:off + t, :] = a
        stage_b[off:off + t, :] = b
        at = off + s if reverse else off - s
        b = a * stage_b[at:at + t, :] + b
        a = a * stage_a[at:at + t, :]
        s *= 2
    while s < t:
        if reverse:
            b = jnp.concatenate([a[:t - s] * b[s:] + b[:t - s], b[t - s:]], axis=0)
            a = jnp.concatenate([a[:t - s] * a[s:], a[t - s:]], axis=0)
        else:
            b = jnp.concatenate([b[:s], a[s:] * b[:t - s] + b[s:]], axis=0)
            a = jnp.concatenate([a[:s], a[s:] * a[:t - s]], axis=0)
        s *= 2
    x = b + a * x0
    return x, (x[0:SUBLANES, :] if reverse else x[t - SUBLANES:t, :])


def _lru_gates(u, wa, wx, ba, bx, sp):
    ub = u.astype(BF16)
    r = jax.nn.sigmoid(_dot(ub, wa, NN) + ba)
    i = jax.nn.sigmoid(_dot(ub, wx, NN) + bx)
    log_a = (-RG_C * r) * sp
    a = jnp.exp(log_a)
    mult = jnp.sqrt(-_expm1(log_a, a) * (a + 1.0))
    return ub, r, i, a, mult


def _staged_shifts(stage, v, prev8, next8, downs, ups):
    t = v.shape[0]
    if prev8 is not None:
        stage[0:SUBLANES, :] = prev8
    stage[SUBLANES:SUBLANES + t, :] = v
    if next8 is not None:
        stage[SUBLANES + t:2 * SUBLANES + t, :] = next8
    return ([stage[SUBLANES - k:SUBLANES - k + t, :] for k in downs],
            [stage[SUBLANES + k:SUBLANES + k + t, :] for k in ups])


def _conv3(p, pp, w_ref, lo, stage):
    (p1, p2), _ = _staged_shifts(stage, p, pp, None, (1, 2), ())
    q = (w_ref[0:1, lo:lo + LANES] * p2 + w_ref[1:2, lo:lo + LANES] * p1) + w_ref[2:3, lo:lo + LANES] * p
    return q, p1, p2


def _conv4(xv, xp, w_ref, b_ref, lo, stage):
    (x1, x2, x3), _ = _staged_shifts(stage, xv, xp, None, (1, 2, 3), ())
    u = (((w_ref[0:1, lo:lo + LANES] * x3 + w_ref[1:2, lo:lo + LANES] * x2) + w_ref[2:3, lo:lo + LANES] * x1)
         + w_ref[3:4, lo:lo + LANES] * xv) + b_ref[:, lo:lo + LANES]
    return u, x1, x2, x3


def _mix_in_mixer_fwd(x2d, mod6, g_mix, w_in_t, conv_sc, conv_lru, conv_b, wa_bd, wx_bd, ba, bx, lam, width, tm):
    s, d = x2d.shape
    din = w_in_t.shape[0]
    nt = s // tm
    sub = min(MIX_ROWS, tm)
    nblk = width // LANES

    def body(x_ref, mod_ref, g_ref, w_ref, wsc_ref, wlru_ref, blru_ref, wa_ref, wx_ref, ba_ref, bx_ref, lam_ref,
             hn_ref, proj_ref, ymix_ref, h_ref, buf_ref, halo_ref, hc_ref, stage_ref):
        i = pl.program_id(0)

        @pl.when(i == 0)
        def _():
            buf_ref[1] = jnp.zeros((tm, din), F32)
            halo_ref[...] = jnp.zeros_like(halo_ref)

        @pl.when(i <= 1)
        def _():
            hc_ref[...] = jnp.zeros_like(hc_ref)

        def step(dst, src):
            xhat, _ = _rms(x_ref[...])
            hn = ((xhat * g_ref[...]) * (1.0 + mod_ref[1:2, :]) + mod_ref[0:1, :]).astype(BF16)
            hn_ref[...] = hn
            n_mix = (tm // sub) * nblk
            n_chunk = din // width

            def project(k):
                res = _dot(hn_ref[...], w_ref[k * width:(k + 1) * width, :], NT)
                proj_ref[:, k * width:(k + 1) * width] = res
                dst[:, k * width:(k + 1) * width] = res

            done = 0
            for half in range(tm // sub):
                r0 = half * sub
                rows = slice(r0, r0 + sub)
                for j in range(nblk):
                    lo = j * LANES
                    while done < n_chunk and done * n_mix <= (half * nblk + j) * n_chunk:
                        project(done)
                        done += 1

                    def col(p):
                        return src[rows, p * width + lo:p * width + lo + LANES]

                    def prev(p):
                        c0 = p * width + lo
                        if half == 0:
                            return halo_ref[:, c0:c0 + LANES]
                        return src[r0 - SUBLANES:r0, c0:c0 + LANES]

                    pp = col(1) * col(2)
                    q, _, _ = _conv3(pp, prev(1) * prev(2), wsc_ref, lo, stage_ref.at[0])
                    ymix_ref[rows, lo:lo + LANES] = (col(0) * q).astype(BF16)

                    u, _, _, _ = _conv4(col(4), prev(4), wlru_ref, blru_ref, lo, stage_ref.at[1])
                    sp = _softplus(-lam_ref[:, lo:lo + LANES])
                    _, r, ig, a, mult = _lru_gates(u, wa_ref[j], wx_ref[j], ba_ref[:, lo:lo + LANES],
                                                   bx_ref[:, lo:lo + LANES], sp)
                    h, ends = _scan_tile(a, mult * (ig * u), hc_ref[0:1, lo:lo + LANES], stage_ref, 2, False)
                    h_ref[rows, lo:lo + LANES] = h
                    hc_ref[0:1, lo:lo + LANES] = ends[SUBLANES - 1:SUBLANES, :]
                    gel, _ = _gelu(col(3))
                    ymix_ref[rows, width + lo:width + lo + LANES] = (gel * h).astype(BF16)
            while done < n_chunk:
                project(done)
                done += 1
            halo_ref[...] = src[tm - SUBLANES:tm, :]

        @pl.when(i % 2 == 0)
        def _():
            step(buf_ref.at[0], buf_ref.at[1])

        @pl.when(i % 2 == 1)
        def _():
            step(buf_ref.at[1], buf_ref.at[0])

    small = [conv_sc, conv_lru, conv_b, wa_bd, wx_bd, ba, bx, lam]
    cur = lambda i: (jnp.minimum(i, nt - 1), 0)
    last = lambda i: (jnp.maximum(i - 1, 0), 0)
    outs = _call(
        body, "mix_in_mixer_fwd", (nt + 1,),
        [pl.BlockSpec((tm, d), cur), _full(mod6.shape), _full(g_mix.shape), _full(w_in_t.shape)]
        + [_full(a.shape) for a in small],
        [pl.BlockSpec((tm, d), cur), pl.BlockSpec((tm, din), cur),
         pl.BlockSpec((tm, 2 * width), last), pl.BlockSpec((tm, width), last)],
        [jax.ShapeDtypeStruct((s, d), BF16), jax.ShapeDtypeStruct((s, din), F32),
         jax.ShapeDtypeStruct((s, 2 * width), BF16), jax.ShapeDtypeStruct((s, width), F32)],
        [x2d, mod6, g_mix, w_in_t, *small],
        scratch=[pltpu.VMEM((2, tm, din), F32), pltpu.VMEM((SUBLANES, din), F32), pltpu.VMEM((SUBLANES, width), F32),
                 pltpu.VMEM((4, sub + 2 * SUBLANES, LANES), F32)])
    return outs


def _mix_out_fwd(ymix, x2d, w_out, mod6, g_mlp, tm):
    s, d = x2d.shape

    def body(y_ref, x_ref, w_ref, mod_ref, g_ref, mix_ref, x2_ref, hn_ref):
        mix = _dot(y_ref[...], w_ref[...], NN)
        mix_ref[...] = mix.astype(BF16)
        x2 = x_ref[...] + mod_ref[2:3, :] * mix
        x2_ref[...] = x2
        xhat, _ = _rms(x2)
        hn_ref[...] = ((xhat * g_ref[...]) * (1.0 + mod_ref[4:5, :]) + mod_ref[3:4, :]).astype(BF16)

    tile = pl.BlockSpec((tm, d), lambda i: (i, 0))
    return _call(
        body, "mix_out_fwd", (s // tm,),
        [tile, tile, _full(w_out.shape), _full(mod6.shape), _full(g_mlp.shape)],
        [tile, tile, tile],
        [jax.ShapeDtypeStruct((s, d), BF16), jax.ShapeDtypeStruct((s, d), F32), jax.ShapeDtypeStruct((s, d), BF16)],
        [ymix, x2d, w_out, mod6, g_mlp])


def _mlp_fwd_loss(hn2, w_up_t, w_down, x2, target, mod6, g_final, tm, tk):
    s, d = hn2.shape
    f = w_up_t.shape[0]
    nk = f // tk

    def body(hn_ref, wu_ref, wd_ref, x2_hbm, t_hbm, mod_ref, g_ref, z_ref, dx3_ref, dyb_ref, st_ref,
             y_ref, x2_ref, t_ref, sems):
        i, k = pl.program_id(0), pl.program_id(1)

        def fetch():
            rows = pl.ds(pl.multiple_of(i * tm, tm), tm)
            return (pltpu.make_async_copy(x2_hbm.at[rows, :], x2_ref, sems.at[0]),
                    pltpu.make_async_copy(t_hbm.at[rows, :], t_ref, sems.at[1]))

        @pl.when(jnp.logical_and(i == 0, k == 0))
        def _():
            st_ref[...] = jnp.zeros_like(st_ref)

        @pl.when(k == 0)
        def _():
            for cp in fetch():
                cp.start()
            y_ref[...] = jnp.zeros_like(y_ref)

        z = jnp.maximum(_dot(hn_ref[...], wu_ref[...], NT), 0.0)
        z_ref[...] = z.astype(BF16)
        y_ref[...] += _dot((z * z).astype(BF16), wd_ref[...], NN)

        @pl.when(k == nk - 1)
        def _():
            for cp in fetch():
                cp.wait()
            gate = mod_ref[5:6, :]
            yv = y_ref[...]
            xhat, rstd = _rms(x2_ref[...] + gate * yv)
            diff = xhat * g_ref[...] - t_ref[...]
            dyo = diff * (1.0 / d)
            dx3 = _rms_bwd(dyo * g_ref[...], xhat, rstd)
            dx3_ref[...] = dx3
            dyb_ref[...] = (gate * dx3).astype(BF16)
            st_ref[0:1, :] += _colsum(dyo * xhat)
            st_ref[1:2, :] += _colsum(dx3 * yv)
            st_ref[2:3, :] += _colsum(diff * diff)

    tile = pl.BlockSpec((tm, d), lambda i, k: (i, 0))
    wblk = pl.BlockSpec((tk, d), lambda i, k: (k, 0))
    return pl.pallas_call(
        body, name="mlp_fwd_loss", grid=(s // tm, nk),
        in_specs=[tile, wblk, wblk, ANY, ANY, _full(mod6.shape), _full(g_final.shape)],
        out_specs=[pl.BlockSpec((tm, tk), lambda i, k: (i, k)), tile, tile, _full((SUBLANES, d))],
        out_shape=[jax.ShapeDtypeStruct((s, f), BF16), jax.ShapeDtypeStruct((s, d), F32),
                   jax.ShapeDtypeStruct((s, d), BF16), jax.ShapeDtypeStruct((SUBLANES, d), F32)],
        scratch_shapes=[pltpu.VMEM((tm, d), F32), pltpu.VMEM((tm, d), F32), pltpu.VMEM((tm, d), F32),
                        pltpu.SemaphoreType.DMA((2,))],
        compiler_params=pltpu.CompilerParams(dimension_semantics=("arbitrary", "arbitrary"),
                                             vmem_limit_bytes=VMEM_LIMIT_BIG),
    )(hn2, w_up_t, w_down, x2, target, mod6, g_final)


def _mlp_bwd_dx(dyb, z, w_down, w_up_t, tm, tk):
    s, d = dyb.shape
    f = z.shape[1]

    nk = f // tk

    def body(dy_ref, z_ref, wd_ref, wu_ref, dz_ref, dh_ref, acc_ref):
        k = pl.program_id(1)

        @pl.when(k == 0)
        def _():
            acc_ref[...] = jnp.zeros_like(acc_ref)

        dz = ((2.0 * z_ref[...].astype(F32)) * _dot(dy_ref[...], wd_ref[...], NT)).astype(BF16)
        dz_ref[...] = dz
        acc_ref[...] += _dot(dz, wu_ref[...], NN)

        @pl.when(k == nk - 1)
        def _():
            dh_ref[...] = acc_ref[...].astype(BF16)

    return pl.pallas_call(
        body, name="mlp_bwd_dx", grid=(s // tm, nk),
        in_specs=[pl.BlockSpec((tm, d), lambda i, k: (i, 0)), pl.BlockSpec((tm, tk), lambda i, k: (i, k)),
                  pl.BlockSpec((tk, d), lambda i, k: (k, 0)), pl.BlockSpec((tk, d), lambda i, k: (k, 0))],
        out_specs=[pl.BlockSpec((tm, tk), lambda i, k: (i, k)), pl.BlockSpec((tm, d), lambda i, k: (i, 0))],
        out_shape=[jax.ShapeDtypeStruct((s, f), BF16), jax.ShapeDtypeStruct((s, d), BF16)],
        scratch_shapes=[pltpu.VMEM((tm, d), F32)],
        compiler_params=_params(("parallel", "arbitrary")),
    )(dyb, z, w_down, w_up_t)


def _mlp_bwd_dw(z, dz, dyb, hn2, tm, tk):
    s, d = dyb.shape
    f = z.shape[1]

    def body(z_ref, dz_ref, dy_ref, hn_ref, gd_ref, gu_ref):
        i = pl.program_id(1)

        @pl.when(i == 0)
        def _():
            gd_ref[...] = jnp.zeros_like(gd_ref)
            gu_ref[...] = jnp.zeros_like(gu_ref)

        zf = z_ref[...].astype(F32)
        gd_ref[...] += _dot((zf * zf).astype(BF16), dy_ref[...], TN)
        gu_ref[...] += _dot(dz_ref[...], hn_ref[...], TN)

    return pl.pallas_call(
        body, name="mlp_bwd_dw", grid=(f // tk, s // tm),
        in_specs=[pl.BlockSpec((tm, tk), lambda k, i: (i, k)), pl.BlockSpec((tm, tk), lambda k, i: (i, k)),
                  pl.BlockSpec((tm, d), lambda k, i: (i, 0)), pl.BlockSpec((tm, d), lambda k, i: (i, 0))],
        out_specs=[pl.BlockSpec((tk, d), lambda k, i: (k, 0)), pl.BlockSpec((tk, d), lambda k, i: (k, 0))],
        out_shape=[jax.ShapeDtypeStruct((f, d), F32), jax.ShapeDtypeStruct((f, d), F32)],
        compiler_params=_params(("parallel", "arbitrary")),
    )(z, dz, dyb, hn2)


def _mix_out_bwd(dhn2, x2, dx3, mix, ymix, w_out, mod6, g_mlp, tm):
    s, d = x2.shape

    def body(dh_ref, x2_ref, dx3_ref, mix_ref, y_ref, w_ref, mod_ref, g_ref, dx2_ref, dym_ref, gw_ref, st_ref):
        i = pl.program_id(0)

        @pl.when(i == 0)
        def _():
            st_ref[...] = jnp.zeros_like(st_ref)
            gw_ref[...] = jnp.zeros_like(gw_ref)

        dh = dh_ref[...].astype(F32)
        xhat, rstd = _rms(x2_ref[...])
        dn = dh * (1.0 + mod_ref[4:5, :])
        dx2 = dx3_ref[...] + _rms_bwd(dn * g_ref[...], xhat, rstd)
        dx2_ref[...] = dx2
        st_ref[0:1, :] += _colsum(dh)
        st_ref[1:2, :] += _colsum(dh * (xhat * g_ref[...]))
        st_ref[2:3, :] += _colsum(dn * xhat)
        st_ref[3:4, :] += _colsum(dx2 * mix_ref[...].astype(F32))
        dmix = (mod_ref[2:3, :] * dx2).astype(BF16)
        dym_ref[...] = _dot(dmix, w_ref[...], NT)
        gw_ref[...] += _dot(y_ref[...], dmix, TN)

    tile = pl.BlockSpec((tm, d), lambda i: (i, 0))
    return _call(
        body, "mix_out_bwd", (s // tm,),
        [tile, tile, tile, tile, tile, _full(w_out.shape), _full(mod6.shape), _full(g_mlp.shape)],
        [tile, tile, _full((d, d)), _full((SUBLANES, d))],
        [jax.ShapeDtypeStruct((s, d), F32), jax.ShapeDtypeStruct((s, d), F32),
         jax.ShapeDtypeStruct((d, d), F32), jax.ShapeDtypeStruct((SUBLANES, d), F32)],
        [dhn2, x2, dx3, mix, ymix, w_out, mod6, g_mlp])


def _mixer_bwd(proj, dymix, h_all, conv_sc, conv_lru, conv_b, wa_bd, wx_bd, ba, bx, lam, width):
    s, din = proj.shape
    t = min(MIX_ROWS, s)
    nt = s // t
    nblk = width // LANES
    hb = t // SUBLANES
    last8 = s // SUBLANES - 1

    def body(proj_ref, projp_ref, projn_ref, dy_ref, dyn_ref, h_ref, hp_ref,
             wsc_ref, wlru_ref, blru_ref, wa_ref, wx_ref, ba_ref, bx_ref, lam_ref,
             dproj_ref, small_ref, gwa_ref, gwx_ref, an_ref, gn_ref, dun_ref, stage_ref):
        i = pl.program_id(0)

        @pl.when(i == 0)
        def _():
            small_ref[...] = jnp.zeros_like(small_ref)
            gwa_ref[...] = jnp.zeros_like(gwa_ref)
            gwx_ref[...] = jnp.zeros_like(gwx_ref)
            an_ref[...] = jnp.zeros_like(an_ref)
            gn_ref[...] = jnp.zeros_like(gn_ref)
            dun_ref[...] = jnp.zeros_like(dun_ref)

        has_prev = i < nt - 1
        has_next = i > 0
        for j in range(nblk):
            lo = j * LANES
            ls = slice(lo, lo + LANES)

            def col(p, ref=proj_ref):
                return ref[:, p * width + lo:p * width + lo + LANES]

            def prev(p):
                return jnp.where(has_prev, col(p, projp_ref), 0.0)

            def nxt(p):
                return jnp.where(has_next, col(p, projn_ref), 0.0)

            def add_row(r, v):
                small_ref[r:r + 1, ls] += _colsum(v)

            sc_b, sc_c, sc_x = col(0), col(1), col(2)
            p = sc_c * sc_x
            q, p1, p2 = _conv3(p, prev(1) * prev(2), wsc_ref, lo, stage_ref.at[0])
            dys = dy_ref[:, ls]
            dproj_ref[:, ls] = (dys * q).astype(BF16)
            dq = dys * sc_b
            dqn = jnp.where(has_next, dyn_ref[:, ls], 0.0) * nxt(0)
            _, (dq1, dq2) = _staged_shifts(stage_ref.at[1], dq, None, dqn, (), (1, 2))
            dp = (wsc_ref[2:3, ls] * dq + wsc_ref[1:2, ls] * dq1) + wsc_ref[0:1, ls] * dq2
            dproj_ref[:, width + lo:width + lo + LANES] = (dp * sc_x).astype(BF16)
            dproj_ref[:, 2 * width + lo:2 * width + lo + LANES] = (dp * sc_c).astype(BF16)
            add_row(0, dq * p2)
            add_row(1, dq * p1)
            add_row(2, dq * p)

            xv = col(4)
            u, x1, x2, x3 = _conv4(xv, prev(4), wlru_ref, blru_ref, lo, stage_ref.at[2])
            lam_v = lam_ref[:, ls]
            sp = _softplus(-lam_v)
            wa, wx = wa_ref[j], wx_ref[j]
            ub, r, ig, a, mult = _lru_gates(u, wa, wx, ba_ref[:, ls], bx_ref[:, ls], sp)
            iu = ig * u
            h = h_ref[:, ls]
            (hm1,), _ = _staged_shifts(stage_ref.at[3], h, jnp.where(has_prev, hp_ref[:, ls], 0.0), None, (1,), ())
            lyv = col(3)
            gel, th = _gelu(lyv)
            dyl = dy_ref[:, width + lo:width + lo + LANES]
            dproj_ref[:, 3 * width + lo:3 * width + lo + LANES] = (dyl * h * _dgelu(lyv, th)).astype(BF16)
            a_next = jnp.broadcast_to(an_ref[0:1, ls], (SUBLANES, LANES))
            _, (a_up,) = _staged_shifts(stage_ref.at[4], a, None, a_next, (), (1,))
            g, _ = _scan_tile(a_up, dyl * gel, gn_ref[0:1, ls], stage_ref, 5, True)
            an_ref[0:1, ls] = a[0:1, :]
            gn_ref[0:1, ls] = g[0:1, :]
            da = g * hm1
            dmult = g * iu
            diu = g * mult
            dlog_a = da * a - dmult * ((a * a) / mult)
            dpre_a = (dlog_a * (-RG_C * sp)) * (r * (1.0 - r))
            dpre_x = (diu * u) * (ig * (1.0 - ig))
            dab, dxb = dpre_a.astype(BF16), dpre_x.astype(BF16)
            du = diu * ig + _dot(dab, wa, NT) + _dot(dxb, wx, NT)
            gwa_ref[j] += _dot(ub, dab, TN)
            gwx_ref[j] += _dot(ub, dxb, TN)
            dun = dun_ref[:, ls]
            dun_ref[:, ls] = du[0:SUBLANES, :]
            _, (du1, du2, du3) = _staged_shifts(stage_ref.at[7], du, None, dun, (), (1, 2, 3))
            dlx = (((wlru_ref[3:4, ls] * du + wlru_ref[2:3, ls] * du1) + wlru_ref[1:2, ls] * du2)
                   + wlru_ref[0:1, ls] * du3)
            dproj_ref[:, 4 * width + lo:4 * width + lo + LANES] = dlx.astype(BF16)
            add_row(3, du * x3)
            add_row(4, du * x2)
            add_row(5, du * x1)
            add_row(6, du * xv)
            add_row(7, du)
            add_row(8, dpre_a)
            add_row(9, dpre_x)
            add_row(10, (dlog_a * (RG_C * r)) * jax.nn.sigmoid(-lam_v))

    small = [conv_sc, conv_lru, conv_b, wa_bd, wx_bd, ba, bx, lam]
    rev = lambda i: nt - 1 - i
    return _call(
        body, "mixer_bwd", (nt,),
        [pl.BlockSpec((t, din), lambda i: (rev(i), 0)),
         pl.BlockSpec((SUBLANES, din), lambda i: (jnp.maximum(rev(i) * hb - 1, 0), 0)),
         pl.BlockSpec((SUBLANES, din), lambda i: (jnp.minimum((rev(i) + 1) * hb, last8), 0)),
         pl.BlockSpec((t, 2 * width), lambda i: (rev(i), 0)),
         pl.BlockSpec((SUBLANES, 2 * width), lambda i: (jnp.minimum((rev(i) + 1) * hb, last8), 0)),
         pl.BlockSpec((t, width), lambda i: (rev(i), 0)),
         pl.BlockSpec((SUBLANES, width), lambda i: (jnp.maximum(rev(i) * hb - 1, 0), 0))]
        + [_full(a.shape) for a in small],
        [pl.BlockSpec((t, din), lambda i: (rev(i), 0)), _full((2 * SUBLANES, width)),
         _full(wa_bd.shape), _full(wx_bd.shape)],
        [jax.ShapeDtypeStruct((s, din), BF16), jax.ShapeDtypeStruct((2 * SUBLANES, width), F32),
         jax.ShapeDtypeStruct(wa_bd.shape, F32), jax.ShapeDtypeStruct(wx_bd.shape, F32)],
        [proj, proj, proj, dymix, dymix, h_all, h_all, *small],
        scratch=[pltpu.VMEM((SUBLANES, width), F32), pltpu.VMEM((SUBLANES, width), F32),
                 pltpu.VMEM((SUBLANES, width), F32), pltpu.VMEM((8, t + 2 * SUBLANES, LANES), F32)])


def _mix_in_bwd_dx(dproj, x2d, dx2, w_in_t, mod6, g_mix, tm):
    s, d = x2d.shape
    din = dproj.shape[1]

    def body(dp_ref, x_ref, dx2_ref, w_ref, mod_ref, g_ref, gx_ref, st_ref):
        i = pl.program_id(0)

        @pl.when(i == 0)
        def _():
            st_ref[...] = jnp.zeros_like(st_ref)

        dh = _dot(dp_ref[...], w_ref[...], NN)
        xhat, rstd = _rms(x_ref[...])
        dn = dh * (1.0 + mod_ref[1:2, :])
        gx_ref[...] = dx2_ref[...] + _rms_bwd(dn * g_ref[...], xhat, rstd)
        st_ref[0:1, :] += _colsum(dh)
        st_ref[1:2, :] += _colsum(dh * (xhat * g_ref[...]))
        st_ref[2:3, :] += _colsum(dn * xhat)

    tile = pl.BlockSpec((tm, d), lambda i: (i, 0))
    return _call(
        body, "mix_in_bwd_dx", (s // tm,),
        [pl.BlockSpec((tm, din), lambda i: (i, 0)), tile, tile, _full(w_in_t.shape), _full(mod6.shape),
         _full(g_mix.shape)],
        [tile, _full((SUBLANES, d))],
        [jax.ShapeDtypeStruct((s, d), F32), jax.ShapeDtypeStruct((SUBLANES, d), F32)],
        [dproj, x2d, dx2, w_in_t, mod6, g_mix])


def _mix_in_bwd_dw(dproj, hn1, tm, tn):
    s, d = hn1.shape
    din = dproj.shape[1]

    def body(dp_ref, hn_ref, gw_ref):
        i = pl.program_id(1)

        @pl.when(i == 0)
        def _():
            gw_ref[...] = jnp.zeros_like(gw_ref)

        gw_ref[...] += _dot(dp_ref[...], hn_ref[...], TN)

    return _call(
        body, "mix_in_bwd_dw", (din // tn, s // tm),
        [pl.BlockSpec((tm, tn), lambda p, i: (i, p)), pl.BlockSpec((tm, d), lambda p, i: (i, 0))],
        [pl.BlockSpec((tn, d), lambda p, i: (p, 0))],
        [jax.ShapeDtypeStruct((din, d), F32)],
        [dproj, hn1])


def _adamw(w, g, m, v):
    m = ADAM_B1 * m + (1.0 - ADAM_B1) * g
    v = ADAM_B2 * v + (1.0 - ADAM_B2) * (g * g)
    m_hat = m / (1.0 - ADAM_B1 ** ADAM_STEP)
    v_hat = v / (1.0 - ADAM_B2 ** ADAM_STEP)
    delta = -ADAM_LR * (m_hat / (jnp.sqrt(v_hat) + ADAM_EPS) + ADAM_WD * w)
    return delta, m, v


def _pair_sum(g4s, h4s, core_chip, tr, name):
    na = len(g4s)
    _, _, r, n = g4s[0].shape

    def body(sc_ref, *refs):
        q = pl.program_id(1)
        for a in range(na):
            g_ref, h_ref = refs[2 * a], refs[2 * a + 1]
            sb_ref, own_ref = refs[2 * na + 2 * a], refs[2 * na + 2 * a + 1]
            ssum = g_ref[...] + h_ref[...]
            sb_ref[...] = ssum.astype(BF16)

            @pl.when(q == sc_ref[1])
            def _():
                own_ref[...] = ssum

    grid_spec = pltpu.PrefetchScalarGridSpec(
        num_scalar_prefetch=1, grid=(r // tr, 4),
        in_specs=[pl.BlockSpec((None, None, tr, n), lambda i, q, sc: (q, sc[0], i, 0)),
                  pl.BlockSpec((None, tr, n), lambda i, q, sc: (q, i, 0))] * na,
        out_specs=[pl.BlockSpec((None, tr, n), lambda i, q, sc: (q, i, 0)),
                   pl.BlockSpec((tr, n), lambda i, q, sc: (i, 0))] * na)
    outs = pl.pallas_call(
        body, name=name, grid_spec=grid_spec,
        out_shape=[jax.ShapeDtypeStruct((4, r, n), BF16), jax.ShapeDtypeStruct((r, n), F32)] * na,
        compiler_params=_params(("parallel", "arbitrary")),
    )(core_chip, *[x for pair in zip(g4s, h4s) for x in pair])
    return [(outs[2 * a], outs[2 * a + 1]) for a in range(na)]


def _sum4_adam(own, parts, w, m, v, tr, name, transposed):
    r, n = own.shape
    rows, cols = w.shape

    def body(o_ref, p_ref, w_ref, m_ref, v_ref, g_ref, d_ref, nm_ref, nv_ref):
        g = o_ref[...]
        for k in range(3):
            g = g + p_ref[k].astype(F32)
        if transposed:
            g = g.T
        g_ref[...] = g
        d_ref[...], nm_ref[...], nv_ref[...] = _adamw(w_ref[...], g, m_ref[...], v_ref[...])

    if transposed:
        g_specs = [pl.BlockSpec((r, tr), lambda i: (0, i)), pl.BlockSpec((3, r, tr), lambda i: (0, 0, i))]
    else:
        g_specs = [pl.BlockSpec((tr, n), lambda i: (i, 0)), pl.BlockSpec((3, tr, n), lambda i: (0, i, 0))]
    tile = pl.BlockSpec((tr, cols), lambda i: (i, 0))
    return pl.pallas_call(
        body, name=name, grid=(rows // tr,),
        in_specs=g_specs + [tile] * 3, out_specs=[tile] * 4,
        out_shape=[jax.ShapeDtypeStruct((rows, cols), F32)] * 4,
        compiler_params=_params(("parallel",)),
    )(own, parts, w, m, v)


def _sum8(parts, tr, name):
    _, rows, n = parts.shape

    def body(p_ref, o_ref):
        acc = p_ref[0]
        for k in range(1, N_DEV):
            acc = acc + p_ref[k]
        o_ref[...] = acc

    return pl.pallas_call(
        body, name=name, grid=(rows // tr,),
        in_specs=[pl.BlockSpec((N_DEV, tr, n), lambda i: (0, i, 0))],
        out_specs=pl.BlockSpec((tr, n), lambda i: (i, 0)),
        out_shape=jax.ShapeDtypeStruct((rows, n), F32),
        compiler_params=_params(("parallel",)),
    )(parts)


def _ada_bwd_adam(cact_t, dmod_cols, w, m, v, tr):
    rows, n = w.shape

    def body(c_ref, d_ref, w_ref, m_ref, v_ref, g_ref, dl_ref, nm_ref, nv_ref):
        def term(b):
            return c_ref[b].astype(BF16).astype(F32) * d_ref[b:b + 1, :].astype(BF16).astype(F32)

        g = term(0)
        for b in range(1, N_DEV):
            g = g + term(b)
        g_ref[...] = g
        dl_ref[...], nm_ref[...], nv_ref[...] = _adamw(w_ref[...], g, m_ref[...], v_ref[...])

    tile = pl.BlockSpec((tr, n), lambda i: (i, 0))
    return pl.pallas_call(
        body, name="ada_bwd_adam", grid=(rows // tr,),
        in_specs=[pl.BlockSpec((N_DEV, tr, 1), lambda i: (0, i, 0)), _full(dmod_cols.shape), tile, tile, tile],
        out_specs=[tile] * 4,
        out_shape=[jax.ShapeDtypeStruct((rows, n), F32)] * 4,
        compiler_params=_params(("parallel",)),
    )(cact_t, dmod_cols, w, m, v)


def _adam_small(ws, gs, ms, vs):
    n = len(ws)

    def body(*refs):
        w_r, g_r, m_r, v_r = refs[:n], refs[n:2 * n], refs[2 * n:3 * n], refs[3 * n:4 * n]
        d_r, nm_r, nv_r = refs[4 * n:5 * n], refs[5 * n:6 * n], refs[6 * n:7 * n]
        for k in range(n):
            d_r[k][...], nm_r[k][...], nv_r[k][...] = _adamw(w_r[k][...], g_r[k][...], m_r[k][...], v_r[k][...])

    shapes = [jax.ShapeDtypeStruct(w.shape, F32) for w in ws]
    outs = pl.pallas_call(
        body, name="adam_small", out_shape=shapes * 3, compiler_params=_params(),
    )(*ws, *gs, *ms, *vs)
    return outs[:n], outs[n:2 * n], outs[2 * n:]


def _block_diag(w):
    h, hd, _ = w.shape
    per = LANES // hd
    eye = jnp.eye(per, dtype=w.dtype)
    w5 = w.reshape(h // per, per, hd, 1, hd) * eye[None, :, None, :, None]
    return w5.reshape(h // per, LANES, LANES)


def _block_diag_grad(g, h, hd):
    per = LANES // hd
    g5 = g.reshape(h // per, per, hd, per, hd)
    return jnp.stack([g5[:, a, :, a, :] for a in range(per)], axis=1).reshape(h, hd, hd)


def kernel(x, c, w_ada, b_ada, g_mix, w_in, conv_w_sc, conv_w_lru, conv_b_lru, w_rg_a, b_rg_a, w_rg_x, b_rg_x, lru_lambda, w_out, g_mlp, w_up, w_down, g_final, loss_target, m_w_ada, m_b_ada, m_g_mix, m_w_in, m_conv_w_sc, m_conv_w_lru, m_conv_b_lru, m_w_rg_a, m_b_rg_a, m_w_rg_x, m_b_rg_x, m_lru_lambda, m_w_out, m_g_mlp, m_w_up, m_w_down, m_g_final, v_w_ada, v_b_ada, v_g_mix, v_w_in, v_conv_w_sc, v_conv_w_lru, v_conv_b_lru, v_w_rg_a, v_b_rg_a, v_w_rg_x, v_b_rg_x, v_lru_lambda, v_w_out, v_g_mlp, v_w_up, v_w_down, v_g_final):
    s, d = x.shape[1], x.shape[2]
    width = conv_b_lru.shape[1]
    heads, hd = w_rg_a.shape[1], w_rg_a.shape[2]
    f = w_down.shape[1] * N_DEV
    n_ada = w_ada.shape[2]
    csh = conv_w_sc.shape[2]
    me = 4 * lax.axis_index("x") + 2 * lax.axis_index("y") + lax.axis_index("c")
    tm = min(512, s)
    tm_mlp = min(1024, s)
    tk = 512

    x2d = x[0]
    tgt = loss_target[0]

    pay = jnp.zeros((SUBLANES, d), F32)
    pay = pay.at[0:1, :].set(c)
    pay = pay.at[1:4, 0:csh].set(conv_w_sc[0])
    pay = pay.at[4:8, 0:csh].set(conv_w_lru[0])
    w_in_t_sh = w_in[0].T.astype(BF16)
    w_up_t_sh = w_up[0].T.astype(BF16)
    w_out_sh = w_out[0].astype(BF16)
    w_down_sh = w_down[0].astype(BF16)
    pay_all, w_in_t = _gather2("gather_in", [pay, w_in_t_sh])
    w_in_t = w_in_t.reshape(-1, d)
    c_all = pay_all[:, 0, :]
    conv_sc = pay_all[:, 1:4, 0:csh].transpose(1, 0, 2).reshape(3, width)
    conv_lru = pay_all[:, 4:8, 0:csh].transpose(1, 0, 2).reshape(4, width)

    b_ada_sh = lax.dynamic_slice(b_ada, (0, me * n_ada), (1, n_ada))
    mod_cols, c_act = _ada_fwd(c_all, w_ada[0], b_ada_sh)
    (mod_rows,) = _exchange("scatter_mod", [], [mod_cols.reshape(N_DEV, 1, n_ada)])
    mod_rows, w_out_sh, w_up_t_sh, w_down_sh = lax.optimization_barrier((mod_rows, w_out_sh, w_up_t_sh, w_down_sh))
    (w_out_g,) = _seq_gather2("gather_w_out", 1, [w_out_sh])
    w_up_g, w_down_g = _seq_gather2("gather_mlp_weights", 2, [w_up_t_sh, w_down_sh])
    mod6 = jnp.zeros((SUBLANES, d), F32).at[0:6, :].set(mod_rows.reshape(6, d))

    wa_bd = _block_diag(w_rg_a[0]).astype(BF16)
    wx_bd = _block_diag(w_rg_x[0]).astype(BF16)
    ba = b_rg_a.reshape(1, width)
    bx = b_rg_x.reshape(1, width)
    g_fin = g_final.reshape(1, d)

    hn1, proj, ymix, h_all = _mix_in_mixer_fwd(x2d, mod6, g_mix, w_in_t, conv_sc, conv_lru, conv_b_lru,
                                               wa_bd, wx_bd, ba, bx, lru_lambda, width, min(MIX_ROWS, s))
    w_out_b = w_out_g.reshape(-1, d)
    mix, x2, hn2 = _mix_out_fwd(ymix, x2d, w_out_b, mod6, g_mlp, tm_mlp)
    w_up_t = w_up_g.reshape(-1, d)
    w_down_b = w_down_g.reshape(-1, d)
    z, dx3, dyb, st_fin = _mlp_fwd_loss(hn2, w_up_t, w_down_b, x2, tgt, mod6, g_fin, tm_mlp, 2 * tk)

    core_chip = jnp.stack([lax.axis_index("c"), 2 * lax.axis_index("x") + lax.axis_index("y")]).astype(jnp.int32)
    dz, dhn2 = _mlp_bwd_dx(dyb, z, w_down_b, w_up_t, tm_mlp, 2 * tk)
    g_down, g_up_t = _mlp_bwd_dw(z, dz, dyb, hn2, tm_mlp, 2 * tk)
    g_up4, g_down4 = g_up_t.reshape(4, 2, -1, d), g_down.reshape(4, 2, -1, d)
    h_up, h_down = _seq_pair_swap("swap_mlp_grads", 7, [g_up4, g_down4])
    dx2, dymix, g_out, st_out = _mix_out_bwd(dhn2, x2, dx3, mix, ymix, w_out_b, mod6, g_mlp, tm)
    h_up, h_down, g_out = lax.optimization_barrier((h_up, h_down, g_out))
    (sb_up, own_up), (sb_down, own_down) = _pair_sum([g_up4, g_down4], [h_up, h_down], core_chip, 256, "pair_sum_mlp")
    g_out4 = g_out.reshape(4, 2, -1, d)
    (h_out,) = _seq_pair_swap("swap_w_out_grad", 8, [g_out4])
    p_up, p_down = _seq_chip_exchange("exchange_mlp_grads", 3, [sb_up, sb_down])
    dproj, g_small, g_wa, g_wx = _mixer_bwd(
        proj, dymix, h_all, conv_sc, conv_lru, conv_b_lru, wa_bd, wx_bd, ba, bx, lru_lambda, width)
    h_out, dproj = lax.optimization_barrier((h_out, dproj))
    ((sb_out, own_out),) = _pair_sum([g_out4], [h_out], core_chip, g_out4.shape[2], "pair_sum_w_out")
    (p_out,) = _seq_chip_exchange("exchange_w_out_grad", 4, [sb_out])
    grad_x, st_in = _mix_in_bwd_dx(dproj, x2d, dx2, w_in_t, mod6, g_mix, tm)

    small = jnp.concatenate([
        st_in[0:2], st_out[3:4], st_out[0:2], st_fin[1:2],
        st_in[2:3], st_out[2:3], st_fin[0:1],
        jnp.concatenate([g_small[7:8], g_small[10:11]], axis=1),
        jnp.concatenate([g_small[8:9], g_small[9:10]], axis=1),
        jnp.concatenate([jnp.concatenate([g_small[0:3], jnp.zeros((1, width), F32)], axis=0), g_small[3:7]], axis=1),
        st_fin[2:3],
        _block_diag_grad(g_wa, heads, hd).reshape(-1, d),
        _block_diag_grad(g_wx, heads, hd).reshape(-1, d),
    ], axis=0)

    (small_all,) = _seq_gather2("gather_small_grads", 5, [small])
    g_in_t, = _mix_in_bwd_dw(dproj, hn1, min(2048, s), dproj.shape[1] // 2)
    g_in4 = g_in_t.reshape(4, 2, -1, d)
    (h_in,) = _seq_pair_swap("swap_w_in_grad", 9, [g_in4])
    p_up, p_down, p_out, small_all, g_in_t = lax.optimization_barrier((p_up, p_down, p_out, small_all, g_in_t))

    ad_up = _sum4_adam(own_up, p_up, w_up[0], m_w_up[0], v_w_up[0], 256, "adam_w_up", True)
    h_in, ad_up = lax.optimization_barrier((h_in, ad_up))
    ((sb_in, own_in),) = _pair_sum([g_in4], [h_in], core_chip, g_in4.shape[2], "pair_sum_w_in")
    (p_in,) = _seq_chip_exchange("exchange_w_in_grad", 6, [sb_in])
    ad_out = _sum4_adam(own_out, p_out, w_out[0], m_w_out[0], v_w_out[0], w_out.shape[1], "adam_w_out", False)
    ad_down = _sum4_adam(own_down, p_down, w_down[0], m_w_down[0], v_w_down[0], 256, "adam_w_down", False)

    gsum = _sum8(small_all, SMALL_ROWS, "sum_small")
    loss = (0.5 / d) * jnp.sum(gsum[15])
    dmod_cols = lax.dynamic_slice(small_all[:, 0:6, :].reshape(N_DEV, 6 * d), (0, me * n_ada), (N_DEV, n_ada))
    g_ada, d_ada, nm_ada, nv_ada = _ada_bwd_adam(c_act[:, :, None], dmod_cols, w_ada[0], m_w_ada[0], v_w_ada[0], 256)

    g_conv = lax.dynamic_slice(gsum[11:15, 0:width], (0, me * csh), (4, csh))
    g_conv_l = lax.dynamic_slice(gsum[11:15, width:2 * width], (0, me * csh), (4, csh))
    small_g = [
        gsum[0:6].reshape(1, 6 * d),
        gsum[6:7],
        g_conv[0:3].reshape(1, 3, csh),
        g_conv_l.reshape(1, 4, csh),
        gsum[9:10, 0:width],
        gsum[16:48].reshape(1, heads, hd, hd),
        gsum[10:11, 0:width].reshape(1, heads, hd),
        gsum[48:80].reshape(1, heads, hd, hd),
        gsum[10:11, width:].reshape(1, heads, hd),
        gsum[9:10, width:],
        gsum[7:8],
        gsum[8],
    ]
    small_w = [b_ada, g_mix, conv_w_sc, conv_w_lru, conv_b_lru, w_rg_a, b_rg_a, w_rg_x, b_rg_x, lru_lambda, g_mlp, g_final]
    small_m = [m_b_ada, m_g_mix, m_conv_w_sc, m_conv_w_lru, m_conv_b_lru, m_w_rg_a, m_b_rg_a, m_w_rg_x, m_b_rg_x,
               m_lru_lambda, m_g_mlp, m_g_final]
    small_v = [v_b_ada, v_g_mix, v_conv_w_sc, v_conv_w_lru, v_conv_b_lru, v_w_rg_a, v_b_rg_a, v_w_rg_x, v_b_rg_x,
               v_lru_lambda, v_g_mlp, v_g_final]
    sd, snm, snv = _adam_small(small_w, small_g, small_m, small_v)
    p_in, ad_out, ad_down, (g_ada, d_ada, nm_ada, nv_ada), sd = lax.optimization_barrier(
        (p_in, ad_out, ad_down, (g_ada, d_ada, nm_ada, nv_ada), sd))
    ad_in = _sum4_adam(own_in, p_in, w_in[0].T, m_w_in[0].T, v_w_in[0].T, own_in.shape[0], "adam_w_in", False)
    ad_in = [a.T for a in ad_in]

    def order(ada, w_in_, w_out_, w_up_, w_down_, sm):
        return [ada[None], sm[0], sm[1], w_in_[None], sm[2], sm[3], sm[4], sm[5], sm[6], sm[7], sm[8], sm[9],
                w_out_[None], sm[10], w_up_[None], w_down_[None], sm[11]]

    grads = order(g_ada, ad_in[0], ad_out[0], ad_up[0], ad_down[0], small_g)
    deltas = order(d_ada, ad_in[1], ad_out[1], ad_up[1], ad_down[1], sd)
    new_m = order(nm_ada, ad_in[2], ad_out[2], ad_up[2], ad_down[2], snm)
    new_v = order(nv_ada, ad_in[3], ad_out[3], ad_up[3], ad_down[3], snv)
    return (loss, grad_x[None], *grads, *deltas, *new_m, *new_v)
```

```python
import jax
import jax.numpy as jnp
from jax import lax
from jax.experimental import pallas as pl
from jax.experimental.pallas import tpu as pltpu
from jax.experimental.pallas import tpu_sc as plsc

F32 = jnp.float32
BF16 = jnp.bfloat16
N_DEV = 8
EPS = 1e-6
RG_C = 8.0
GELU_K0 = 0.7978845608028654
GELU_K1 = 0.044715
ADAM_LR = 0.001
ADAM_B1 = 0.9
ADAM_B2 = 0.999
ADAM_EPS = 1e-08
ADAM_WD = 0.01
ADAM_STEP = 10
LANES = 128
SUBLANES = 8
VMEM_LIMIT = 52 * 1024 * 1024
VMEM_LIMIT_BIG = 58 * 1024 * 1024
MIX_ROWS = 256
SMALL_ROWS = 80

MESH = pl.DeviceIdType.MESH
ANY = pl.BlockSpec(memory_space=pl.ANY)
NN = ((1,), (0,))
NT = ((1,), (1,))
TN = ((0,), (0,))


def _dot(a, b, dims):
    return lax.dot_general(a, b, (dims, ((), ())), preferred_element_type=F32)


def _params(sem=None):
    return pltpu.CompilerParams(dimension_semantics=sem, vmem_limit_bytes=VMEM_LIMIT)


def _full(shape):
    nd = len(shape)
    return pl.BlockSpec(shape, lambda *_: (0,) * nd)


def _exchange(name, gathers, scatters):
    n_g = len(gathers)
    arrs = list(gathers) + list(scatters)
    n = len(arrs)
    out_shape = [jax.ShapeDtypeStruct((N_DEV,) + a.shape, a.dtype) for a in gathers]
    out_shape += [jax.ShapeDtypeStruct(a.shape, a.dtype) for a in scatters]

    def body(*refs):
        ins, outs = refs[:n], refs[n:2 * n]
        send_sems, recv_sems, local_sems = refs[2 * n:]
        x, y, c = lax.axis_index("x"), lax.axis_index("y"), lax.axis_index("c")
        me = 4 * x + 2 * y + c

        def src(a, dev):
            return ins[a] if a < n_g else ins[a].at[dev]

        def peer_of(k):
            px = 1 - x if (k >> 2) & 1 else x
            py = 1 - y if (k >> 1) & 1 else y
            pc = 1 - c if k & 1 else c
            return (px, py, pc), 4 * px + 2 * py + pc

        local = [pltpu.make_async_copy(src(a, me), outs[a].at[me], local_sems.at[a]) for a in range(n)]
        for cp in local:
            cp.start()
        sends = []
        for k in range(1, N_DEV):
            peer, pidx = peer_of(k)
            for a in range(n):
                cp = pltpu.make_async_remote_copy(
                    src_ref=src(a, pidx), dst_ref=outs[a].at[me],
                    send_sem=send_sems.at[a * (N_DEV - 1) + k - 1], recv_sem=recv_sems.at[a * (N_DEV - 1) + k - 1],
                    device_id=peer, device_id_type=MESH)
                cp.start()
                sends.append(cp)
        for k in range(1, N_DEV):
            peer, pidx = peer_of(k)
            for a in range(n):
                pltpu.make_async_remote_copy(
                    src_ref=src(a, pidx), dst_ref=outs[a].at[pidx],
                    send_sem=send_sems.at[a * (N_DEV - 1) + k - 1], recv_sem=recv_sems.at[a * (N_DEV - 1) + k - 1],
                    device_id=peer, device_id_type=MESH).wait_recv()
        for cp in sends:
            cp.wait_send()
        for cp in local:
            cp.wait()

    return pl.pallas_call(
        body, name=name, out_shape=out_shape,
        in_specs=[ANY] * n, out_specs=[ANY] * n,
        scratch_shapes=[pltpu.SemaphoreType.DMA((n * (N_DEV - 1),)),
                        pltpu.SemaphoreType.DMA((n * (N_DEV - 1),)),
                        pltpu.SemaphoreType.DMA((n,))],
    )(*arrs)


def _gather2(name, arrs):
    n = len(arrs)
    per = 7
    out_shape = [jax.ShapeDtypeStruct((N_DEV,) + a.shape, a.dtype) for a in arrs]

    def body(*refs):
        ins, outs = refs[:n], refs[n:2 * n]
        send_sems, recv_sems, local_sems = refs[2 * n:]
        x, y, c = lax.axis_index("x"), lax.axis_index("y"), lax.axis_index("c")
        sib = (x, y, 1 - c)
        chips = [(1 - x, y), (x, 1 - y), (1 - x, 1 - y)]

        def slot(a, px, py, pc):
            return outs[a].at[4 * px + 2 * py + pc]

        def copy(a, k, block, to, src=None):
            return pltpu.make_async_remote_copy(
                src_ref=slot(a, *block) if src is None else src, dst_ref=slot(a, *block),
                send_sem=send_sems.at[a * per + k], recv_sem=recv_sems.at[a * per + k],
                device_id=to, device_id_type=MESH)

        local = [pltpu.make_async_copy(ins[a], slot(a, x, y, c), local_sems.at[a]) for a in range(n)]
        for cp in local:
            cp.start()
        first = []
        for a in range(n):
            first += [copy(a, 1 + j, (x, y, c), (*chip, c), src=ins[a]) for j, chip in enumerate(chips)]
        for a in range(n):
            first.append(copy(a, 0, (x, y, c), sib, src=ins[a]))
        for cp in first:
            cp.start()
        passed = []
        for a in range(n):
            for j, chip in enumerate(chips):
                copy(a, 1 + j, (*chip, c), (x, y, c)).wait_recv()
                cp = copy(a, 4 + j, (*chip, c), sib)
                cp.start()
                passed.append(cp)
        for a in range(n):
            copy(a, 0, sib, (x, y, c)).wait_recv()
            for j, chip in enumerate(chips):
                copy(a, 4 + j, (*chip, 1 - c), (x, y, c)).wait_recv()
        for cp in first + passed:
            cp.wait_send()
        for cp in local:
            cp.wait()

    return pl.pallas_call(
        body, name=name, out_shape=out_shape,
        in_specs=[ANY] * n, out_specs=[ANY] * n,
        scratch_shapes=[pltpu.SemaphoreType.DMA((n * per,)), pltpu.SemaphoreType.DMA((n * per,)),
                        pltpu.SemaphoreType.DMA((n,))],
    )(*arrs)


def _seq_gather2(name, collective_id, arrs):
    n = len(arrs)
    per = 7

    def body(*refs):
        ins, outs = refs[:n], refs[n:2 * n]
        send_sems, recv_sems, local_sems = refs[2 * n:]
        x, y, c = lax.axis_index("x"), lax.axis_index("y"), lax.axis_index("c")
        sib = (x, y, 1 - c)
        chips = [(1 - x, y), (x, 1 - y), (1 - x, 1 - y)]
        barrier = pltpu.get_barrier_semaphore()
        for peer in [sib] + [(*chip, c) for chip in chips]:
            pl.semaphore_signal(barrier, inc=1, device_id=peer, device_id_type=MESH)
        pl.semaphore_wait(barrier, 4)

        def slot(a, px, py, pc):
            return outs[a].at[4 * px + 2 * py + pc]

        def copy(a, k, block, to, src=None):
            return pltpu.make_async_remote_copy(
                src_ref=slot(a, *block) if src is None else src, dst_ref=slot(a, *block),
                send_sem=send_sems.at[a * per + k], recv_sem=recv_sems.at[a * per + k],
                device_id=to, device_id_type=MESH)

        local = [pltpu.make_async_copy(ins[a], slot(a, x, y, c), local_sems.at[a]) for a in range(n)]
        for cp in local:
            cp.start()
        first = []
        for a in range(n):
            first += [copy(a, 1 + j, (x, y, c), (*chip, c), src=ins[a]) for j, chip in enumerate(chips)]
        for a in range(n):
            first.append(copy(a, 0, (x, y, c), sib, src=ins[a]))
        for cp in first:
            cp.start()
        passed = []
        for a in range(n):
            for j, chip in enumerate(chips):
                copy(a, 1 + j, (*chip, c), (x, y, c)).wait_recv()
                cp = copy(a, 4 + j, (*chip, c), sib)
                cp.start()
                passed.append(cp)
        for a in range(n):
            copy(a, 0, sib, (x, y, c)).wait_recv()
            for j, chip in enumerate(chips):
                copy(a, 4 + j, (*chip, 1 - c), (x, y, c)).wait_recv()
        for cp in first + passed:
            cp.wait_send()
        for cp in local:
            cp.wait()

    return pl.kernel(
        body, out_type=[jax.ShapeDtypeStruct((N_DEV,) + a.shape, a.dtype) for a in arrs],
        mesh=plsc.ScalarSubcoreMesh(axis_name="seq", num_cores=1),
        scratch_types=[pltpu.SemaphoreType.DMA((n * per,)), pltpu.SemaphoreType.DMA((n * per,)),
                       pltpu.SemaphoreType.DMA((n,))],
        compiler_params=pltpu.CompilerParams(collective_id=collective_id), name=name,
    )(*arrs)


def _seq_chip_exchange(name, collective_id, arrs):
    n = len(arrs)

    def body(*refs):
        ins, outs = refs[:n], refs[n:2 * n]
        send_sems, recv_sems = refs[2 * n:]
        x, y, c = lax.axis_index("x"), lax.axis_index("y"), lax.axis_index("c")

        def peer(k):
            return (1 - x if (k >> 1) & 1 else x), (1 - y if k & 1 else y)

        barrier = pltpu.get_barrier_semaphore()
        for k in (1, 2, 3):
            pl.semaphore_signal(barrier, inc=1, device_id=(*peer(k), c), device_id_type=MESH)
        pl.semaphore_wait(barrier, 3)

        def copy(a, k):
            px, py = peer(k)
            return pltpu.make_async_remote_copy(
                src_ref=ins[a].at[2 * px + py], dst_ref=outs[a].at[k - 1],
                send_sem=send_sems.at[a * 3 + k - 1], recv_sem=recv_sems.at[a * 3 + k - 1],
                device_id=(px, py, c), device_id_type=MESH)

        cps = [copy(a, k) for a in range(n) for k in (1, 2, 3)]
        for cp in cps:
            cp.start()
        for cp in cps:
            cp.wait_recv()
        for cp in cps:
            cp.wait_send()

    return pl.kernel(
        body, out_type=[jax.ShapeDtypeStruct((3,) + a.shape[1:], a.dtype) for a in arrs],
        mesh=plsc.ScalarSubcoreMesh(axis_name="seq", num_cores=1),
        scratch_types=[pltpu.SemaphoreType.DMA((n * 3,)), pltpu.SemaphoreType.DMA((n * 3,))],
        compiler_params=pltpu.CompilerParams(collective_id=collective_id), name=name,
    )(*arrs)


def _seq_pair_swap(name, collective_id, arrs):
    n = len(arrs)

    def body(*refs):
        ins, outs = refs[:n], refs[n:2 * n]
        send_sems, recv_sems = refs[2 * n:]
        x, y, c = lax.axis_index("x"), lax.axis_index("y"), lax.axis_index("c")
        barrier = pltpu.get_barrier_semaphore()
        pl.semaphore_signal(barrier, inc=1, device_id=(x, y, 1 - c), device_id_type=MESH)
        pl.semaphore_wait(barrier, 1)

        def copy(a, q):
            return pltpu.make_async_remote_copy(
                src_ref=ins[a].at[q, 1 - c], dst_ref=outs[a].at[q],
                send_sem=send_sems.at[a * 4 + q], recv_sem=recv_sems.at[a * 4 + q],
                device_id=(x, y, 1 - c), device_id_type=MESH)

        cps = [copy(a, q) for a in range(n) for q in range(4)]
        for cp in cps:
            cp.start()
        for cp in cps:
            cp.wait_recv()
        for cp in cps:
            cp.wait_send()

    return pl.kernel(
        body, out_type=[jax.ShapeDtypeStruct((4,) + a.shape[2:], a.dtype) for a in arrs],
        mesh=plsc.ScalarSubcoreMesh(axis_name="seq", num_cores=1),
        scratch_types=[pltpu.SemaphoreType.DMA((n * 4,)), pltpu.SemaphoreType.DMA((n * 4,))],
        compiler_params=pltpu.CompilerParams(collective_id=collective_id), name=name,
    )(*arrs)


def _call(body, name, grid, in_specs, out_specs, out_shape, args, scratch=()):
    return pl.pallas_call(
        body, name=name, grid=grid, in_specs=in_specs, out_specs=out_specs, out_shape=out_shape,
        scratch_shapes=list(scratch), compiler_params=_params(("arbitrary",) * len(grid)))(*args)


def _ada_fwd(c_all, w_ada_sh, b_ada_sh):
    nb, d = c_all.shape
    ncol = w_ada_sh.shape[1]

    def body(c_ref, w_ref, b_ref, mod_ref, cact_ref):
        cc = c_ref[...]
        ca = cc * jax.nn.sigmoid(cc)
        cact_ref[...] = ca
        mod_ref[...] = _dot(ca.astype(BF16), w_ref[...].astype(BF16), NN) + b_ref[...]

    return pl.pallas_call(
        body, name="ada_fwd",
        out_shape=[jax.ShapeDtypeStruct((nb, ncol), F32), jax.ShapeDtypeStruct((nb, d), F32)],
        compiler_params=_params(),
    )(c_all, w_ada_sh, b_ada_sh)


def _rms(xv):
    rstd = lax.rsqrt(jnp.mean(xv * xv, axis=-1, keepdims=True) + EPS)
    return xv * rstd, rstd


def _rms_bwd(dxhat, xhat, rstd):
    return rstd * (dxhat - xhat * jnp.mean(dxhat * xhat, axis=-1, keepdims=True))


def _colsum(v):
    return jnp.sum(v, axis=0, keepdims=True)


def _expm1(v, ev):
    series = v * (1.0 + v * (0.5 + v * (1.0 / 6.0 + v * (1.0 / 24.0 + v * (1.0 / 120.0)))))
    return jnp.where(jnp.abs(v) < 0.2, series, ev - 1.0)


def _softplus(v):
    return jnp.maximum(v, 0.0) + jnp.log1p(jnp.exp(-jnp.abs(v)))


def _gelu(v):
    t = jnp.tanh(v * (GELU_K0 + (GELU_K0 * GELU_K1) * (v * v)))
    return 0.5 * v * (1.0 + t), t


def _dgelu(v, t):
    return 0.5 * ((1.0 + t) + (v * (1.0 - t * t)) * (GELU_K0 + (3.0 * GELU_K0 * GELU_K1) * (v * v)))


def _scan_tile(a, b, x0, st, k0, reverse):
    t = a.shape[0]
    off = SUBLANES
    stage_a, stage_b = st.at[k0], st.at[k0 + 1]
    halo = slice(off + t, off + t + SUBLANES) if reverse else slice(0, SUBLANES)
    stage_a[halo, :] = jnp.ones((SUBLANES, a.shape[1]), F32)
    stage_b[halo, :] = jnp.zeros((SUBLANES, a.shape[1]), F32)
    s = 1
    while s < min(t, SUBLANES):
        stage_a[off:off + t, :] = a
        stage_b[off:off + t, :] = b
        at = off + s if reverse else off - s
        b = a * stage_b[at:at + t, :] + b
        a = a * stage_a[at:at + t, :]
        s *= 2
    while s < t:
        if reverse:
            b = jnp.concatenate([a[:t - s] * b[s:] + b[:t - s], b[t - s:]], axis=0)
            a = jnp.concatenate([a[:t - s] * a[s:], a[t - s:]], axis=0)
        else:
            b = jnp.concatenate([b[:s], a[s:] * b[:t - s] + b[s:]], axis=0)
            a = jnp.concatenate([a[:s], a[s:] * a[:t - s]], axis=0)
        s *= 2
    x = b + a * x0
    return x, (x[0:SUBLANES, :] if reverse else x[t - SUBLANES:t, :])


def _lru_gates(u, wa, wx, ba, bx, sp):
    ub = u.astype(BF16)
    r = jax.nn.sigmoid(_dot(ub, wa, NN) + ba)
    i = jax.nn.sigmoid(_dot(ub, wx, NN) + bx)
    log_a = (-RG_C * r) * sp
    a = jnp.exp(log_a)
    mult = jnp.sqrt(-_expm1(log_a, a) * (a + 1.0))
    return ub, r, i, a, mult


def _staged_shifts(stage, v, prev8, next8, downs, ups):
    t = v.shape[0]
    if prev8 is not None:
        stage[0:SUBLANES, :] = prev8
    stage[SUBLANES:SUBLANES + t, :] = v
    if next8 is not None:
        stage[SUBLANES + t:2 * SUBLANES + t, :] = next8
    return ([stage[SUBLANES - k:SUBLANES - k + t, :] for k in downs],
            [stage[SUBLANES + k:SUBLANES + k + t, :] for k in ups])


def _conv3(p, pp, w_ref, lo, stage):
    (p1, p2), _ = _staged_shifts(stage, p, pp, None, (1, 2), ())
    q = (w_ref[0:1, lo:lo + LANES] * p2 + w_ref[1:2, lo:lo + LANES] * p1) + w_ref[2:3, lo:lo + LANES] * p
    return q, p1, p2


def _conv4(xv, xp, w_ref, b_ref, lo, stage):
    (x1, x2, x3), _ = _staged_shifts(stage, xv, xp, None, (1, 2, 3), ())
    u = (((w_ref[0:1, lo:lo + LANES] * x3 + w_ref[1:2, lo:lo + LANES] * x2) + w_ref[2:3, lo:lo + LANES] * x1)
         + w_ref[3:4, lo:lo + LANES] * xv) + b_ref[:, lo:lo + LANES]
    return u, x1, x2, x3


def _mix_in_mixer_fwd(x2d, mod6, g_mix, w_in_t, conv_sc, conv_lru, conv_b, wa_bd, wx_bd, ba, bx, lam, width, tm):
    s, d = x2d.shape
    din = w_in_t.shape[0]
    nt = s // tm
    sub = min(MIX_ROWS, tm)
    nblk = width // LANES

    def body(x_ref, mod_ref, g_ref, w_ref, wsc_ref, wlru_ref, blru_ref, wa_ref, wx_ref, ba_ref, bx_ref, lam_ref,
             hn_ref, proj_ref, ymix_ref, h_ref, buf_ref, halo_ref, hc_ref, stage_ref):
        i = pl.program_id(0)

        @pl.when(i == 0)
        def _():
            buf_ref[1] = jnp.zeros((tm, din), F32)
            halo_ref[...] = jnp.zeros_like(halo_ref)

        @pl.when(i <= 1)
        def _():
            hc_ref[...] = jnp.zeros_like(hc_ref)

        def step(dst, src):
            xhat, _ = _rms(x_ref[...])
            hn = ((xhat * g_ref[...]) * (1.0 + mod_ref[1:2, :]) + mod_ref[0:1, :]).astype(BF16)
            hn_ref[...] = hn
            n_mix = (tm // sub) * nblk
            n_chunk = din // width

            def project(k):
                res = _dot(hn_ref[...], w_ref[k * width:(k + 1) * width, :], NT)
                proj_ref[:, k * width:(k + 1) * width] = res.astype(BF16)
                dst[:, k * width:(k + 1) * width] = res

            done = 0
            for half in range(tm // sub):
                r0 = half * sub
                rows = slice(r0, r0 + sub)
                for j in range(nblk):
                    lo = j * LANES
                    while done < n_chunk and done * n_mix <= (half * nblk + j) * n_chunk:
                        project(done)
                        done += 1

                    def col(p):
                        return src[rows, p * width + lo:p * width + lo + LANES]

                    def prev(p):
                        c0 = p * width + lo
                        if half == 0:
                            return halo_ref[:, c0:c0 + LANES]
                        return src[r0 - SUBLANES:r0, c0:c0 + LANES]

                    pp = col(1) * col(2)
                    q, _, _ = _conv3(pp, prev(1) * prev(2), wsc_ref, lo, stage_ref.at[0])
                    ymix_ref[rows, lo:lo + LANES] = (col(0) * q).astype(BF16)

                    u, _, _, _ = _conv4(col(4), prev(4), wlru_ref, blru_ref, lo, stage_ref.at[1])
                    sp = _softplus(-lam_ref[:, lo:lo + LANES])
                    _, r, ig, a, mult = _lru_gates(u, wa_ref[j], wx_ref[j], ba_ref[:, lo:lo + LANES],
                                                   bx_ref[:, lo:lo + LANES], sp)
                    h, ends = _scan_tile(a, mult * (ig * u), hc_ref[0:1, lo:lo + LANES], stage_ref, 2, False)
                    h_ref[rows, lo:lo + LANES] = h
                    hc_ref[0:1, lo:lo + LANES] = ends[SUBLANES - 1:SUBLANES, :]
                    gel, _ = _gelu(col(3))
                    ymix_ref[rows, width + lo:width + lo + LANES] = (gel * h).astype(BF16)
            while done < n_chunk:
                project(done)
                done += 1
            halo_ref[...] = src[tm - SUBLANES:tm, :]

        @pl.when(i % 2 == 0)
        def _():
            step(buf_ref.at[0], buf_ref.at[1])

        @pl.when(i % 2 == 1)
        def _():
            step(buf_ref.at[1], buf_ref.at[0])

    small = [conv_sc, conv_lru, conv_b, wa_bd, wx_bd, ba, bx, lam]
    cur = lambda i: (jnp.minimum(i, nt - 1), 0)
    last = lambda i: (jnp.maximum(i - 1, 0), 0)
    outs = _call(
        body, "mix_in_mixer_fwd", (nt + 1,),
        [pl.BlockSpec((tm, d), cur), _full(mod6.shape), _full(g_mix.shape), _full(w_in_t.shape)]
        + [_full(a.shape) for a in small],
        [pl.BlockSpec((tm, d), cur), pl.BlockSpec((tm, din), cur),
         pl.BlockSpec((tm, 2 * width), last), pl.BlockSpec((tm, width), last)],
        [jax.ShapeDtypeStruct((s, d), BF16), jax.ShapeDtypeStruct((s, din), BF16),
         jax.ShapeDtypeStruct((s, 2 * width), BF16), jax.ShapeDtypeStruct((s, width), F32)],
        [x2d, mod6, g_mix, w_in_t, *small],
        scratch=[pltpu.VMEM((2, tm, din), F32), pltpu.VMEM((SUBLANES, din), F32), pltpu.VMEM((SUBLANES, width), F32),
                 pltpu.VMEM((4, sub + 2 * SUBLANES, LANES), F32)])
    return outs


def _mix_out_fwd(ymix, x2d, w_out, mod6, g_mlp, tm):
    s, d = x2d.shape

    def body(y_ref, x_ref, w_ref, mod_ref, g_ref, mix_ref, x2_ref, hn_ref):
        mix = _dot(y_ref[...], w_ref[...], NN)
        mix_ref[...] = mix.astype(BF16)
        x2 = x_ref[...] + mod_ref[2:3, :] * mix
        x2_ref[...] = x2
        xhat, _ = _rms(x2)
        hn_ref[...] = ((xhat * g_ref[...]) * (1.0 + mod_ref[4:5, :]) + mod_ref[3:4, :]).astype(BF16)

    tile = pl.BlockSpec((tm, d), lambda i: (i, 0))
    return _call(
        body, "mix_out_fwd", (s // tm,),
        [tile, tile, _full(w_out.shape), _full(mod6.shape), _full(g_mlp.shape)],
        [tile, tile, tile],
        [jax.ShapeDtypeStruct((s, d), BF16), jax.ShapeDtypeStruct((s, d), F32), jax.ShapeDtypeStruct((s, d), BF16)],
        [ymix, x2d, w_out, mod6, g_mlp])


def _mlp_fwd_loss(hn2, w_up_t, w_down, x2, target, mod6, g_final, tm, tk):
    s, d = hn2.shape
    f = w_up_t.shape[0]
    nk = f // tk

    def body(hn_ref, wu_ref, wd_ref, x2_hbm, t_hbm, mod_ref, g_ref, z_ref, dx3_ref, dyb_ref, st_ref,
             y_ref, x2_ref, t_ref, sems):
        i, k = pl.program_id(0), pl.program_id(1)

        def fetch():
            rows = pl.ds(pl.multiple_of(i * tm, tm), tm)
            return (pltpu.make_async_copy(x2_hbm.at[rows, :], x2_ref, sems.at[0]),
                    pltpu.make_async_copy(t_hbm.at[rows, :], t_ref, sems.at[1]))

        @pl.when(jnp.logical_and(i == 0, k == 0))
        def _():
            st_ref[...] = jnp.zeros_like(st_ref)

        @pl.when(k == 0)
        def _():
            for cp in fetch():
                cp.start()
            y_ref[...] = jnp.zeros_like(y_ref)

        z = jnp.maximum(_dot(hn_ref[...], wu_ref[...], NT), 0.0)
        z_ref[...] = z.astype(BF16)
        y_ref[...] += _dot((z * z).astype(BF16), wd_ref[...], NN)

        @pl.when(k == nk - 1)
        def _():
            for cp in fetch():
                cp.wait()
            gate = mod_ref[5:6, :]
            yv = y_ref[...]
            xhat, rstd = _rms(x2_ref[...] + gate * yv)
            diff = xhat * g_ref[...] - t_ref[...]
            dyo = diff * (1.0 / d)
            dx3 = _rms_bwd(dyo * g_ref[...], xhat, rstd)
            dx3_ref[...] = dx3
            dyb_ref[...] = (gate * dx3).astype(BF16)
            st_ref[0:1, :] += _colsum(dyo * xhat)
            st_ref[1:2, :] += _colsum(dx3 * yv)
            st_ref[2:3, :] += _colsum(diff * diff)

    tile = pl.BlockSpec((tm, d), lambda i, k: (i, 0))
    wblk = pl.BlockSpec((tk, d), lambda i, k: (k, 0))
    return pl.pallas_call(
        body, name="mlp_fwd_loss", grid=(s // tm, nk),
        in_specs=[tile, wblk, wblk, ANY, ANY, _full(mod6.shape), _full(g_final.shape)],
        out_specs=[pl.BlockSpec((tm, tk), lambda i, k: (i, k)), tile, tile, _full((SUBLANES, d))],
        out_shape=[jax.ShapeDtypeStruct((s, f), BF16), jax.ShapeDtypeStruct((s, d), F32),
                   jax.ShapeDtypeStruct((s, d), BF16), jax.ShapeDtypeStruct((SUBLANES, d), F32)],
        scratch_shapes=[pltpu.VMEM((tm, d), F32), pltpu.VMEM((tm, d), F32), pltpu.VMEM((tm, d), F32),
                        pltpu.SemaphoreType.DMA((2,))],
        compiler_params=pltpu.CompilerParams(dimension_semantics=("arbitrary", "arbitrary"),
                                             vmem_limit_bytes=VMEM_LIMIT_BIG),
    )(hn2, w_up_t, w_down, x2, target, mod6, g_final)


def _mlp_bwd_dx(dyb, z, w_down, w_up_t, tm, tk):
    s, d = dyb.shape
    f = z.shape[1]

    nk = f // tk

    def body(dy_ref, z_ref, wd_ref, wu_ref, dz_ref, dh_ref, acc_ref):
        k = pl.program_id(1)

        @pl.when(k == 0)
        def _():
            acc_ref[...] = jnp.zeros_like(acc_ref)

        dz = ((2.0 * z_ref[...].astype(F32)) * _dot(dy_ref[...], wd_ref[...], NT)).astype(BF16)
        dz_ref[...] = dz
        acc_ref[...] += _dot(dz, wu_ref[...], NN)

        @pl.when(k == nk - 1)
        def _():
            dh_ref[...] = acc_ref[...].astype(BF16)

    return pl.pallas_call(
        body, name="mlp_bwd_dx", grid=(s // tm, nk),
        in_specs=[pl.BlockSpec((tm, d), lambda i, k: (i, 0)), pl.BlockSpec((tm, tk), lambda i, k: (i, k)),
                  pl.BlockSpec((tk, d), lambda i, k: (k, 0)), pl.BlockSpec((tk, d), lambda i, k: (k, 0))],
        out_specs=[pl.BlockSpec((tm, tk), lambda i, k: (i, k)), pl.BlockSpec((tm, d), lambda i, k: (i, 0))],
        out_shape=[jax.ShapeDtypeStruct((s, f), BF16), jax.ShapeDtypeStruct((s, d), BF16)],
        scratch_shapes=[pltpu.VMEM((tm, d), F32)],
        compiler_params=_params(("parallel", "arbitrary")),
    )(dyb, z, w_down, w_up_t)


def _mlp_bwd_dw(z, dz, dyb, hn2, tm, tk):
    s, d = dyb.shape
    f = z.shape[1]

    def body(z_ref, dz_ref, dy_ref, hn_ref, gd_ref, gu_ref):
        i = pl.program_id(1)

        @pl.when(i == 0)
        def _():
            gd_ref[...] = jnp.zeros_like(gd_ref)
            gu_ref[...] = jnp.zeros_like(gu_ref)

        zf = z_ref[...].astype(F32)
        gd_ref[...] += _dot((zf * zf).astype(BF16), dy_ref[...], TN)
        gu_ref[...] += _dot(dz_ref[...], hn_ref[...], TN)

    return pl.pallas_call(
        body, name="mlp_bwd_dw", grid=(f // tk, s // tm),
        in_specs=[pl.BlockSpec((tm, tk), lambda k, i: (i, k)), pl.BlockSpec((tm, tk), lambda k, i: (i, k)),
                  pl.BlockSpec((tm, d), lambda k, i: (i, 0)), pl.BlockSpec((tm, d), lambda k, i: (i, 0))],
        out_specs=[pl.BlockSpec((tk, d), lambda k, i: (k, 0)), pl.BlockSpec((tk, d), lambda k, i: (k, 0))],
        out_shape=[jax.ShapeDtypeStruct((f, d), F32), jax.ShapeDtypeStruct((f, d), F32)],
        compiler_params=_params(("parallel", "arbitrary")),
    )(z, dz, dyb, hn2)


def _mix_out_bwd(dhn2, x2, dx3, mix, ymix, w_out, mod6, g_mlp, tm):
    s, d = x2.shape

    def body(dh_ref, x2_ref, dx3_ref, mix_ref, y_ref, w_ref, mod_ref, g_ref, dx2_ref, dym_ref, gw_ref, st_ref):
        i = pl.program_id(0)

        @pl.when(i == 0)
        def _():
            st_ref[...] = jnp.zeros_like(st_ref)
            gw_ref[...] = jnp.zeros_like(gw_ref)

        dh = dh_ref[...].astype(F32)
        xhat, rstd = _rms(x2_ref[...])
        dn = dh * (1.0 + mod_ref[4:5, :])
        dx2 = dx3_ref[...] + _rms_bwd(dn * g_ref[...], xhat, rstd)
        dx2_ref[...] = dx2
        st_ref[0:1, :] += _colsum(dh)
        st_ref[1:2, :] += _colsum(dh * (xhat * g_ref[...]))
        st_ref[2:3, :] += _colsum(dn * xhat)
        st_ref[3:4, :] += _colsum(dx2 * mix_ref[...].astype(F32))
        dmix = (mod_ref[2:3, :] * dx2).astype(BF16)
        dym_ref[...] = _dot(dmix, w_ref[...], NT)
        gw_ref[...] += _dot(y_ref[...], dmix, TN)

    tile = pl.BlockSpec((tm, d), lambda i: (i, 0))
    return _call(
        body, "mix_out_bwd", (s // tm,),
        [tile, tile, tile, tile, tile, _full(w_out.shape), _full(mod6.shape), _full(g_mlp.shape)],
        [tile, tile, _full((d, d)), _full((SUBLANES, d))],
        [jax.ShapeDtypeStruct((s, d), F32), jax.ShapeDtypeStruct((s, d), F32),
         jax.ShapeDtypeStruct((d, d), F32), jax.ShapeDtypeStruct((SUBLANES, d), F32)],
        [dhn2, x2, dx3, mix, ymix, w_out, mod6, g_mlp])


def _mixer_bwd(proj, dymix, h_all, conv_sc, conv_lru, conv_b, wa_bd, wx_bd, ba, bx, lam, width):
    s, din = proj.shape
    t = min(MIX_ROWS, s)
    nt = s // t
    nblk = width // LANES
    hb = t // SUBLANES
    last8 = s // SUBLANES - 1

    def body(proj_ref, projp_ref, projn_ref, dy_ref, dyn_ref, h_ref, hp_ref,
             wsc_ref, wlru_ref, blru_ref, wa_ref, wx_ref, ba_ref, bx_ref, lam_ref,
             dproj_ref, small_ref, gwa_ref, gwx_ref, an_ref, gn_ref, dun_ref, stage_ref):
        i = pl.program_id(0)

        @pl.when(i == 0)
        def _():
            small_ref[...] = jnp.zeros_like(small_ref)
            gwa_ref[...] = jnp.zeros_like(gwa_ref)
            gwx_ref[...] = jnp.zeros_like(gwx_ref)
            an_ref[...] = jnp.zeros_like(an_ref)
            gn_ref[...] = jnp.zeros_like(gn_ref)
            dun_ref[...] = jnp.zeros_like(dun_ref)

        has_prev = i < nt - 1
        has_next = i > 0
        for j in range(nblk):
            lo = j * LANES
            ls = slice(lo, lo + LANES)

            def col(p, ref=proj_ref):
                return ref[:, p * width + lo:p * width + lo + LANES].astype(F32)

            def prev(p):
                return jnp.where(has_prev, col(p, projp_ref)[SUBLANES:2 * SUBLANES], 0.0)

            def nxt(p):
                return jnp.where(has_next, col(p, projn_ref)[0:SUBLANES], 0.0)

            def add_row(r, v):
                small_ref[r:r + 1, ls] += _colsum(v)

            sc_b, sc_c, sc_x = col(0), col(1), col(2)
            p = sc_c * sc_x
            q, p1, p2 = _conv3(p, prev(1) * prev(2), wsc_ref, lo, stage_ref.at[0])
            dys = dy_ref[:, ls]
            dproj_ref[:, ls] = (dys * q).astype(BF16)
            dq = dys * sc_b
            dqn = jnp.where(has_next, dyn_ref[:, ls], 0.0) * nxt(0)
            _, (dq1, dq2) = _staged_shifts(stage_ref.at[1], dq, None, dqn, (), (1, 2))
            dp = (wsc_ref[2:3, ls] * dq + wsc_ref[1:2, ls] * dq1) + wsc_ref[0:1, ls] * dq2
            dproj_ref[:, width + lo:width + lo + LANES] = (dp * sc_x).astype(BF16)
            dproj_ref[:, 2 * width + lo:2 * width + lo + LANES] = (dp * sc_c).astype(BF16)
            add_row(0, dq * p2)
            add_row(1, dq * p1)
            add_row(2, dq * p)

            xv = col(4)
            u, x1, x2, x3 = _conv4(xv, prev(4), wlru_ref, blru_ref, lo, stage_ref.at[2])
            lam_v = lam_ref[:, ls]
            sp = _softplus(-lam_v)
            wa, wx = wa_ref[j], wx_ref[j]
            ub, r, ig, a, mult = _lru_gates(u, wa, wx, ba_ref[:, ls], bx_ref[:, ls], sp)
            iu = ig * u
            h = h_ref[:, ls]
            (hm1,), _ = _staged_shifts(stage_ref.at[3], h, jnp.where(has_prev, hp_ref[:, ls], 0.0), None, (1,), ())
            lyv = col(3)
            gel, th = _gelu(lyv)
            dyl = dy_ref[:, width + lo:width + lo + LANES]
            dproj_ref[:, 3 * width + lo:3 * width + lo + LANES] = (dyl * h * _dgelu(lyv, th)).astype(BF16)
            a_next = jnp.broadcast_to(an_ref[0:1, ls], (SUBLANES, LANES))
            _, (a_up,) = _staged_shifts(stage_ref.at[4], a, None, a_next, (), (1,))
            g, _ = _scan_tile(a_up, dyl * gel, gn_ref[0:1, ls], stage_ref, 5, True)
            an_ref[0:1, ls] = a[0:1, :]
            gn_ref[0:1, ls] = g[0:1, :]
            da = g * hm1
            dmult = g * iu
            diu = g * mult
            dlog_a = da * a - dmult * ((a * a) / mult)
            dpre_a = (dlog_a * (-RG_C * sp)) * (r * (1.0 - r))
            dpre_x = (diu * u) * (ig * (1.0 - ig))
            dab, dxb = dpre_a.astype(BF16), dpre_x.astype(BF16)
            du = diu * ig + _dot(dab, wa, NT) + _dot(dxb, wx, NT)
            gwa_ref[j] += _dot(ub, dab, TN)
            gwx_ref[j] += _dot(ub, dxb, TN)
            dun = dun_ref[:, ls]
            dun_ref[:, ls] = du[0:SUBLANES, :]
            _, (du1, du2, du3) = _staged_shifts(stage_ref.at[7], du, None, dun, (), (1, 2, 3))
            dlx = (((wlru_ref[3:4, ls] * du + wlru_ref[2:3, ls] * du1) + wlru_ref[1:2, ls] * du2)
                   + wlru_ref[0:1, ls] * du3)
            dproj_ref[:, 4 * width + lo:4 * width + lo + LANES] = dlx.astype(BF16)
            add_row(3, du * x3)
            add_row(4, du * x2)
            add_row(5, du * x1)
            add_row(6, du * xv)
            add_row(7, du)
            add_row(8, dpre_a)
            add_row(9, dpre_x)
            add_row(10, (dlog_a * (RG_C * r)) * jax.nn.sigmoid(-lam_v))

    small = [conv_sc, conv_lru, conv_b, wa_bd, wx_bd, ba, bx, lam]
    rev = lambda i: nt - 1 - i
    return _call(
        body, "mixer_bwd", (nt,),
        [pl.BlockSpec((t, din), lambda i: (rev(i), 0)),
         pl.BlockSpec((2 * SUBLANES, din), lambda i: (jnp.maximum(rev(i) * (hb // 2) - 1, 0), 0)),
         pl.BlockSpec((2 * SUBLANES, din), lambda i: (jnp.minimum((rev(i) + 1) * (hb // 2), last8 // 2), 0)),
         pl.BlockSpec((t, 2 * width), lambda i: (rev(i), 0)),
         pl.BlockSpec((SUBLANES, 2 * width), lambda i: (jnp.minimum((rev(i) + 1) * hb, last8), 0)),
         pl.BlockSpec((t, width), lambda i: (rev(i), 0)),
         pl.BlockSpec((SUBLANES, width), lambda i: (jnp.maximum(rev(i) * hb - 1, 0), 0))]
        + [_full(a.shape) for a in small],
        [pl.BlockSpec((t, din), lambda i: (rev(i), 0)), _full((2 * SUBLANES, width)),
         _full(wa_bd.shape), _full(wx_bd.shape)],
        [jax.ShapeDtypeStruct((s, din), BF16), jax.ShapeDtypeStruct((2 * SUBLANES, width), F32),
         jax.ShapeDtypeStruct(wa_bd.shape, F32), jax.ShapeDtypeStruct(wx_bd.shape, F32)],
        [proj, proj, proj, dymix, dymix, h_all, h_all, *small],
        scratch=[pltpu.VMEM((SUBLANES, width), F32), pltpu.VMEM((SUBLANES, width), F32),
                 pltpu.VMEM((SUBLANES, width), F32), pltpu.VMEM((8, t + 2 * SUBLANES, LANES), F32)])


def _mix_in_bwd_dx(dproj, x2d, dx2, w_in_t, mod6, g_mix, tm):
    s, d = x2d.shape
    din = dproj.shape[1]

    def body(dp_ref, x_ref, dx2_ref, w_ref, mod_ref, g_ref, gx_ref, st_ref):
        i = pl.program_id(0)

        @pl.when(i == 0)
        def _():
            st_ref[...] = jnp.zeros_like(st_ref)

        dh = _dot(dp_ref[...], w_ref[...], NN)
        xhat, rstd = _rms(x_ref[...])
        dn = dh * (1.0 + mod_ref[1:2, :])
        gx_ref[...] = dx2_ref[...] + _rms_bwd(dn * g_ref[...], xhat, rstd)
        st_ref[0:1, :] += _colsum(dh)
        st_ref[1:2, :] += _colsum(dh * (xhat * g_ref[...]))
        st_ref[2:3, :] += _colsum(dn * xhat)

    tile = pl.BlockSpec((tm, d), lambda i: (i, 0))
    return _call(
        body, "mix_in_bwd_dx", (s // tm,),
        [pl.BlockSpec((tm, din), lambda i: (i, 0)), tile, tile, _full(w_in_t.shape), _full(mod6.shape),
         _full(g_mix.shape)],
        [tile, _full((SUBLANES, d))],
        [jax.ShapeDtypeStruct((s, d), F32), jax.ShapeDtypeStruct((SUBLANES, d), F32)],
        [dproj, x2d, dx2, w_in_t, mod6, g_mix])


def _mix_in_bwd_dw(dproj, hn1, tm, tn):
    s, d = hn1.shape
    din = dproj.shape[1]

    def body(dp_ref, hn_ref, gw_ref):
        i = pl.program_id(1)

        @pl.when(i == 0)
        def _():
            gw_ref[...] = jnp.zeros_like(gw_ref)

        gw_ref[...] += _dot(dp_ref[...], hn_ref[...], TN)

    return _call(
        body, "mix_in_bwd_dw", (din // tn, s // tm),
        [pl.BlockSpec((tm, tn), lambda p, i: (i, p)), pl.BlockSpec((tm, d), lambda p, i: (i, 0))],
        [pl.BlockSpec((tn, d), lambda p, i: (p, 0))],
        [jax.ShapeDtypeStruct((din, d), F32)],
        [dproj, hn1])


def _adamw(w, g, m, v):
    m = ADAM_B1 * m + (1.0 - ADAM_B1) * g
    v = ADAM_B2 * v + (1.0 - ADAM_B2) * (g * g)
    m_hat = m / (1.0 - ADAM_B1 ** ADAM_STEP)
    v_hat = v / (1.0 - ADAM_B2 ** ADAM_STEP)
    delta = -ADAM_LR * (m_hat / (jnp.sqrt(v_hat) + ADAM_EPS) + ADAM_WD * w)
    return delta, m, v


def _pair_sum(g4s, h4s, core_chip, tr, name):
    na = len(g4s)
    _, _, r, n = g4s[0].shape

    def body(sc_ref, *refs):
        q = pl.program_id(1)
        for a in range(na):
            g_ref, h_ref = refs[2 * a], refs[2 * a + 1]
            sb_ref, own_ref = refs[2 * na + 2 * a], refs[2 * na + 2 * a + 1]
            ssum = g_ref[...] + h_ref[...]
            sb_ref[...] = ssum.astype(BF16)

            @pl.when(q == sc_ref[1])
            def _():
                own_ref[...] = ssum

    grid_spec = pltpu.PrefetchScalarGridSpec(
        num_scalar_prefetch=1, grid=(r // tr, 4),
        in_specs=[pl.BlockSpec((None, None, tr, n), lambda i, q, sc: (q, sc[0], i, 0)),
                  pl.BlockSpec((None, tr, n), lambda i, q, sc: (q, i, 0))] * na,
        out_specs=[pl.BlockSpec((None, tr, n), lambda i, q, sc: (q, i, 0)),
                   pl.BlockSpec((tr, n), lambda i, q, sc: (i, 0))] * na)
    outs = pl.pallas_call(
        body, name=name, grid_spec=grid_spec,
        out_shape=[jax.ShapeDtypeStruct((4, r, n), BF16), jax.ShapeDtypeStruct((r, n), F32)] * na,
        compiler_params=_params(("parallel", "arbitrary")),
    )(core_chip, *[x for pair in zip(g4s, h4s) for x in pair])
    return [(outs[2 * a], outs[2 * a + 1]) for a in range(na)]


def _sum4_adam(own, parts, w, m, v, tr, name, transposed):
    r, n = own.shape
    rows, cols = w.shape

    def body(o_ref, p_ref, w_ref, m_ref, v_ref, g_ref, d_ref, nm_ref, nv_ref):
        g = o_ref[...]
        for k in range(3):
            g = g + p_ref[k].astype(F32)
        if transposed:
            g = g.T
        g_ref[...] = g
        d_ref[...], nm_ref[...], nv_ref[...] = _adamw(w_ref[...], g, m_ref[...], v_ref[...])

    if transposed:
        g_specs = [pl.BlockSpec((r, tr), lambda i: (0, i)), pl.BlockSpec((3, r, tr), lambda i: (0, 0, i))]
    else:
        g_specs = [pl.BlockSpec((tr, n), lambda i: (i, 0)), pl.BlockSpec((3, tr, n), lambda i: (0, i, 0))]
    tile = pl.BlockSpec((tr, cols), lambda i: (i, 0))
    return pl.pallas_call(
        body, name=name, grid=(rows // tr,),
        in_specs=g_specs + [tile] * 3, out_specs=[tile] * 4,
        out_shape=[jax.ShapeDtypeStruct((rows, cols), F32)] * 4,
        compiler_params=_params(("parallel",)),
    )(own, parts, w, m, v)


def _sum8(parts, tr, name):
    _, rows, n = parts.shape

    def body(p_ref, o_ref):
        acc = p_ref[0]
        for k in range(1, N_DEV):
            acc = acc + p_ref[k]
        o_ref[...] = acc

    return pl.pallas_call(
        body, name=name, grid=(rows // tr,),
        in_specs=[pl.BlockSpec((N_DEV, tr, n), lambda i: (0, i, 0))],
        out_specs=pl.BlockSpec((tr, n), lambda i: (i, 0)),
        out_shape=jax.ShapeDtypeStruct((rows, n), F32),
        compiler_params=_params(("parallel",)),
    )(parts)


def _ada_bwd_adam(cact_t, dmod_cols, w, m, v, tr):
    rows, n = w.shape

    def body(c_ref, d_ref, w_ref, m_ref, v_ref, g_ref, dl_ref, nm_ref, nv_ref):
        def term(b):
            return c_ref[b].astype(BF16).astype(F32) * d_ref[b:b + 1, :].astype(BF16).astype(F32)

        g = term(0)
        for b in range(1, N_DEV):
            g = g + term(b)
        g_ref[...] = g
        dl_ref[...], nm_ref[...], nv_ref[...] = _adamw(w_ref[...], g, m_ref[...], v_ref[...])

    tile = pl.BlockSpec((tr, n), lambda i: (i, 0))
    return pl.pallas_call(
        body, name="ada_bwd_adam", grid=(rows // tr,),
        in_specs=[pl.BlockSpec((N_DEV, tr, 1), lambda i: (0, i, 0)), _full(dmod_cols.shape), tile, tile, tile],
        out_specs=[tile] * 4,
        out_shape=[jax.ShapeDtypeStruct((rows, n), F32)] * 4,
        compiler_params=_params(("parallel",)),
    )(cact_t, dmod_cols, w, m, v)


def _adam_small(ws, gs, ms, vs):
    n = len(ws)

    def body(*refs):
        w_r, g_r, m_r, v_r = refs[:n], refs[n:2 * n], refs[2 * n:3 * n], refs[3 * n:4 * n]
        d_r, nm_r, nv_r = refs[4 * n:5 * n], refs[5 * n:6 * n], refs[6 * n:7 * n]
        for k in range(n):
            d_r[k][...], nm_r[k][...], nv_r[k][...] = _adamw(w_r[k][...], g_r[k][...], m_r[k][...], v_r[k][...])

    shapes = [jax.ShapeDtypeStruct(w.shape, F32) for w in ws]
    outs = pl.pallas_call(
        body, name="adam_small", out_shape=shapes * 3, compiler_params=_params(),
    )(*ws, *gs, *ms, *vs)
    return outs[:n], outs[n:2 * n], outs[2 * n:]


def _block_diag(w):
    h, hd, _ = w.shape
    per = LANES // hd
    eye = jnp.eye(per, dtype=w.dtype)
    w5 = w.reshape(h // per, per, hd, 1, hd) * eye[None, :, None, :, None]
    return w5.reshape(h // per, LANES, LANES)


def _block_diag_grad(g, h, hd):
    per = LANES // hd
    g5 = g.reshape(h // per, per, hd, per, hd)
    return jnp.stack([g5[:, a, :, a, :] for a in range(per)], axis=1).reshape(h, hd, hd)


def kernel(x, c, w_ada, b_ada, g_mix, w_in, conv_w_sc, conv_w_lru, conv_b_lru, w_rg_a, b_rg_a, w_rg_x, b_rg_x, lru_lambda, w_out, g_mlp, w_up, w_down, g_final, loss_target, m_w_ada, m_b_ada, m_g_mix, m_w_in, m_conv_w_sc, m_conv_w_lru, m_conv_b_lru, m_w_rg_a, m_b_rg_a, m_w_rg_x, m_b_rg_x, m_lru_lambda, m_w_out, m_g_mlp, m_w_up, m_w_down, m_g_final, v_w_ada, v_b_ada, v_g_mix, v_w_in, v_conv_w_sc, v_conv_w_lru, v_conv_b_lru, v_w_rg_a, v_b_rg_a, v_w_rg_x, v_b_rg_x, v_lru_lambda, v_w_out, v_g_mlp, v_w_up, v_w_down, v_g_final):
    s, d = x.shape[1], x.shape[2]
    width = conv_b_lru.shape[1]
    heads, hd = w_rg_a.shape[1], w_rg_a.shape[2]
    f = w_down.shape[1] * N_DEV
    n_ada = w_ada.shape[2]
    csh = conv_w_sc.shape[2]
    me = 4 * lax.axis_index("x") + 2 * lax.axis_index("y") + lax.axis_index("c")
    tm = min(512, s)
    tm_mlp = min(1024, s)
    tk = 512

    x2d = x[0]
    tgt = loss_target[0]

    pay = jnp.zeros((SUBLANES, d), F32)
    pay = pay.at[0:1, :].set(c)
    pay = pay.at[1:4, 0:csh].set(conv_w_sc[0])
    pay = pay.at[4:8, 0:csh].set(conv_w_lru[0])
    w_in_t_sh = w_in[0].T.astype(BF16)
    w_up_t_sh = w_up[0].T.astype(BF16)
    w_out_sh = w_out[0].astype(BF16)
    w_down_sh = w_down[0].astype(BF16)
    pay_all, w_in_t = _gather2("gather_in", [pay, w_in_t_sh])
    w_in_t = w_in_t.reshape(-1, d)
    c_all = pay_all[:, 0, :]
    conv_sc = pay_all[:, 1:4, 0:csh].transpose(1, 0, 2).reshape(3, width)
    conv_lru = pay_all[:, 4:8, 0:csh].transpose(1, 0, 2).reshape(4, width)

    b_ada_sh = lax.dynamic_slice(b_ada, (0, me * n_ada), (1, n_ada))
    mod_cols, c_act = _ada_fwd(c_all, w_ada[0], b_ada_sh)
    (mod_rows,) = _exchange("scatter_mod", [], [mod_cols.reshape(N_DEV, 1, n_ada)])
    mod_rows, w_out_sh, w_up_t_sh, w_down_sh = lax.optimization_barrier((mod_rows, w_out_sh, w_up_t_sh, w_down_sh))
    (w_out_g,) = _seq_gather2("gather_w_out", 1, [w_out_sh])
    w_up_g, w_down_g = _seq_gather2("gather_mlp_weights", 2, [w_up_t_sh, w_down_sh])
    mod6 = jnp.zeros((SUBLANES, d), F32).at[0:6, :].set(mod_rows.reshape(6, d))

    wa_bd = _block_diag(w_rg_a[0]).astype(BF16)
    wx_bd = _block_diag(w_rg_x[0]).astype(BF16)
    ba = b_rg_a.reshape(1, width)
    bx = b_rg_x.reshape(1, width)
    g_fin = g_final.reshape(1, d)

    hn1, proj, ymix, h_all = _mix_in_mixer_fwd(x2d, mod6, g_mix, w_in_t, conv_sc, conv_lru, conv_b_lru,
                                               wa_bd, wx_bd, ba, bx, lru_lambda, width, tm)
    w_out_b = w_out_g.reshape(-1, d)
    mix, x2, hn2 = _mix_out_fwd(ymix, x2d, w_out_b, mod6, g_mlp, tm_mlp)
    w_up_t = w_up_g.reshape(-1, d)
    w_down_b = w_down_g.reshape(-1, d)
    z, dx3, dyb, st_fin = _mlp_fwd_loss(hn2, w_up_t, w_down_b, x2, tgt, mod6, g_fin, tm_mlp, 2 * tk)

    core_chip = jnp.stack([lax.axis_index("c"), 2 * lax.axis_index("x") + lax.axis_index("y")]).astype(jnp.int32)
    dz, dhn2 = _mlp_bwd_dx(dyb, z, w_down_b, w_up_t, tm_mlp, 2 * tk)
    g_down, g_up_t = _mlp_bwd_dw(z, dz, dyb, hn2, tm_mlp, 2 * tk)
    g_up4, g_down4 = g_up_t.reshape(4, 2, -1, d), g_down.reshape(4, 2, -1, d)
    h_up, h_down = _seq_pair_swap("swap_mlp_grads", 7, [g_up4, g_down4])
    dx2, dymix, g_out, st_out = _mix_out_bwd(dhn2, x2, dx3, mix, ymix, w_out_b, mod6, g_mlp, tm)
    h_up, h_down, g_out = lax.optimization_barrier((h_up, h_down, g_out))
    (sb_up, own_up), (sb_down, own_down) = _pair_sum([g_up4, g_down4], [h_up, h_down], core_chip, 256, "pair_sum_mlp")
    g_out4 = g_out.reshape(4, 2, -1, d)
    (h_out,) = _seq_pair_swap("swap_w_out_grad", 8, [g_out4])
    p_up, p_down = _seq_chip_exchange("exchange_mlp_grads", 3, [sb_up, sb_down])
    dproj, g_small, g_wa, g_wx = _mixer_bwd(
        proj, dymix, h_all, conv_sc, conv_lru, conv_b_lru, wa_bd, wx_bd, ba, bx, lru_lambda, width)
    h_out, dproj = lax.optimization_barrier((h_out, dproj))
    ((sb_out, own_out),) = _pair_sum([g_out4], [h_out], core_chip, g_out4.shape[2], "pair_sum_w_out")
    (p_out,) = _seq_chip_exchange("exchange_w_out_grad", 4, [sb_out])
    grad_x, st_in = _mix_in_bwd_dx(dproj, x2d, dx2, w_in_t, mod6, g_mix, tm)

    small = jnp.concatenate([
        st_in[0:2], st_out[3:4], st_out[0:2], st_fin[1:2],
        st_in[2:3], st_out[2:3], st_fin[0:1],
        jnp.concatenate([g_small[7:8], g_small[10:11]], axis=1),
        jnp.concatenate([g_small[8:9], g_small[9:10]], axis=1),
        jnp.concatenate([jnp.concatenate([g_small[0:3], jnp.zeros((1, width), F32)], axis=0), g_small[3:7]], axis=1),
        st_fin[2:3],
        _block_diag_grad(g_wa, heads, hd).reshape(-1, d),
        _block_diag_grad(g_wx, heads, hd).reshape(-1, d),
    ], axis=0)

    (small_all,) = _seq_gather2("gather_small_grads", 5, [small])
    g_in_t, = _mix_in_bwd_dw(dproj, hn1, min(2048, s), dproj.shape[1] // 2)
    g_in4 = g_in_t.reshape(4, 2, -1, d)
    (h_in,) = _seq_pair_swap("swap_w_in_grad", 9, [g_in4])
    p_up, p_down, p_out, small_all, g_in_t = lax.optimization_barrier((p_up, p_down, p_out, small_all, g_in_t))

    ad_up = _sum4_adam(own_up, p_up, w_up[0], m_w_up[0], v_w_up[0], 256, "adam_w_up", True)
    h_in, ad_up = lax.optimization_barrier((h_in, ad_up))
    ((sb_in, own_in),) = _pair_sum([g_in4], [h_in], core_chip, g_in4.shape[2], "pair_sum_w_in")
    (p_in,) = _seq_chip_exchange("exchange_w_in_grad", 6, [sb_in])
    ad_out = _sum4_adam(own_out, p_out, w_out[0], m_w_out[0], v_w_out[0], w_out.shape[1], "adam_w_out", False)
    ad_down = _sum4_adam(own_down, p_down, w_down[0], m_w_down[0], v_w_down[0], 256, "adam_w_down", False)

    gsum = _sum8(small_all, SMALL_ROWS, "sum_small")
    loss = (0.5 / d) * jnp.sum(gsum[15])
    dmod_cols = lax.dynamic_slice(small_all[:, 0:6, :].reshape(N_DEV, 6 * d), (0, me * n_ada), (N_DEV, n_ada))
    g_ada, d_ada, nm_ada, nv_ada = _ada_bwd_adam(c_act[:, :, None], dmod_cols, w_ada[0], m_w_ada[0], v_w_ada[0], 256)

    g_conv = lax.dynamic_slice(gsum[11:15, 0:width], (0, me * csh), (4, csh))
    g_conv_l = lax.dynamic_slice(gsum[11:15, width:2 * width], (0, me * csh), (4, csh))
    small_g = [
        gsum[0:6].reshape(1, 6 * d),
        gsum[6:7],
        g_conv[0:3].reshape(1, 3, csh),
        g_conv_l.reshape(1, 4, csh),
        gsum[9:10, 0:width],
        gsum[16:48].reshape(1, heads, hd, hd),
        gsum[10:11, 0:width].reshape(1, heads, hd),
        gsum[48:80].reshape(1, heads, hd, hd),
        gsum[10:11, width:].reshape(1, heads, hd),
        gsum[9:10, width:],
        gsum[7:8],
        gsum[8],
    ]
    small_w = [b_ada, g_mix, conv_w_sc, conv_w_lru, conv_b_lru, w_rg_a, b_rg_a, w_rg_x, b_rg_x, lru_lambda, g_mlp, g_final]
    small_m = [m_b_ada, m_g_mix, m_conv_w_sc, m_conv_w_lru, m_conv_b_lru, m_w_rg_a, m_b_rg_a, m_w_rg_x, m_b_rg_x,
               m_lru_lambda, m_g_mlp, m_g_final]
    small_v = [v_b_ada, v_g_mix, v_conv_w_sc, v_conv_w_lru, v_conv_b_lru, v_w_rg_a, v_b_rg_a, v_w_rg_x, v_b_rg_x,
               v_lru_lambda, v_g_mlp, v_g_final]
    sd, snm, snv = _adam_small(small_w, small_g, small_m, small_v)
    p_in, ad_out, ad_down, (g_ada, d_ada, nm_ada, nv_ada), sd = lax.optimization_barrier(
        (p_in, ad_out, ad_down, (g_ada, d_ada, nm_ada, nv_ada), sd))
    ad_in = _sum4_adam(own_in, p_in, w_in[0].T, m_w_in[0].T, v_w_in[0].T, own_in.shape[0], "adam_w_in", False)
    ad_in = [a.T for a in ad_in]

    def order(ada, w_in_, w_out_, w_up_, w_down_, sm):
        return [ada[None], sm[0], sm[1], w_in_[None], sm[2], sm[3], sm[4], sm[5], sm[6], sm[7], sm[8], sm[9],
                w_out_[None], sm[10], w_up_[None], w_down_[None], sm[11]]

    grads = order(g_ada, ad_in[0], ad_out[0], ad_up[0], ad_down[0], small_g)
    deltas = order(d_ada, ad_in[1], ad_out[1], ad_up[1], ad_down[1], sd)
    new_m = order(nm_ada, ad_in[2], ad_out[2], ad_up[2], ad_down[2], snm)
    new_v = order(nv_ada, ad_in[3], ad_out[3], ad_up[3], ad_down[3], snv)
    return (loss, grad_x[None], *grads, *deltas, *new_m, *new_v)
```

```python
import jax
import jax.numpy as jnp
from jax import lax
from jax.experimental import pallas as pl
from jax.experimental.pallas import tpu as pltpu
from jax.experimental.pallas import tpu_sc as plsc

F32 = jnp.float32
BF16 = jnp.bfloat16
N_DEV = 8
EPS = 1e-6
RG_C = 8.0
GELU_K0 = 0.7978845608028654
GELU_K1 = 0.044715
ADAM_LR = 0.001
ADAM_B1 = 0.9
ADAM_B2 = 0.999
ADAM_EPS = 1e-08
ADAM_WD = 0.01
ADAM_STEP = 10
LANES = 128
SUBLANES = 8
VMEM_LIMIT = 52 * 1024 * 1024
VMEM_LIMIT_BIG = 58 * 1024 * 1024
MIX_ROWS = 256
SMALL_ROWS = 80

MESH = pl.DeviceIdType.MESH
ANY = pl.BlockSpec(memory_space=pl.ANY)
NN = ((1,), (0,))
NT = ((1,), (1,))
TN = ((0,), (0,))


def _dot(a, b, dims):
    return lax.dot_general(a, b, (dims, ((), ())), preferred_element_type=F32)


def _params(sem=None):
    return pltpu.CompilerParams(dimension_semantics=sem, vmem_limit_bytes=VMEM_LIMIT)


def _full(shape):
    nd = len(shape)
    return pl.BlockSpec(shape, lambda *_: (0,) * nd)


def _exchange(name, gathers, scatters):
    n_g = len(gathers)
    arrs = list(gathers) + list(scatters)
    n = len(arrs)
    out_shape = [jax.ShapeDtypeStruct((N_DEV,) + a.shape, a.dtype) for a in gathers]
    out_shape += [jax.ShapeDtypeStruct(a.shape, a.dtype) for a in scatters]

    def body(*refs):
        ins, outs = refs[:n], refs[n:2 * n]
        send_sems, recv_sems, local_sems = refs[2 * n:]
        x, y, c = lax.axis_index("x"), lax.axis_index("y"), lax.axis_index("c")
        me = 4 * x + 2 * y + c

        def src(a, dev):
            return ins[a] if a < n_g else ins[a].at[dev]

        def peer_of(k):
            px = 1 - x if (k >> 2) & 1 else x
            py = 1 - y if (k >> 1) & 1 else y
            pc = 1 - c if k & 1 else c
            return (px, py, pc), 4 * px + 2 * py + pc

        local = [pltpu.make_async_copy(src(a, me), outs[a].at[me], local_sems.at[a]) for a in range(n)]
        for cp in local:
            cp.start()
        sends = []
        for k in range(1, N_DEV):
            peer, pidx = peer_of(k)
            for a in range(n):
                cp = pltpu.make_async_remote_copy(
                    src_ref=src(a, pidx), dst_ref=outs[a].at[me],
                    send_sem=send_sems.at[a * (N_DEV - 1) + k - 1], recv_sem=recv_sems.at[a * (N_DEV - 1) + k - 1],
                    device_id=peer, device_id_type=MESH)
                cp.start()
                sends.append(cp)
        for k in range(1, N_DEV):
            peer, pidx = peer_of(k)
            for a in range(n):
                pltpu.make_async_remote_copy(
                    src_ref=src(a, pidx), dst_ref=outs[a].at[pidx],
                    send_sem=send_sems.at[a * (N_DEV - 1) + k - 1], recv_sem=recv_sems.at[a * (N_DEV - 1) + k - 1],
                    device_id=peer, device_id_type=MESH).wait_recv()
        for cp in sends:
            cp.wait_send()
        for cp in local:
            cp.wait()

    return pl.pallas_call(
        body, name=name, out_shape=out_shape,
        in_specs=[ANY] * n, out_specs=[ANY] * n,
        scratch_shapes=[pltpu.SemaphoreType.DMA((n * (N_DEV - 1),)),
                        pltpu.SemaphoreType.DMA((n * (N_DEV - 1),)),
                        pltpu.SemaphoreType.DMA((n,))],
    )(*arrs)


GATHER_SEMS = 7


def _gather_copies(ins, outs, send_sems, recv_sems, local_sems, x, y, c):
    n = len(ins)
    per = GATHER_SEMS
    sib = (x, y, 1 - c)
    xn, yn, dg = (1 - x, y), (x, 1 - y), (1 - x, 1 - y)
    fx, fy = x + (1 - c) * (1 - 2 * x), y + c * (1 - 2 * y)
    tx, ty = x + c * (1 - 2 * x), y + (1 - c) * (1 - 2 * y)

    def slot(a, px, py, pc):
        return outs[a].at[4 * px + 2 * py + pc]

    def copy(a, k, block, to, src=None):
        return pltpu.make_async_remote_copy(
            src_ref=slot(a, *block) if src is None else src, dst_ref=slot(a, *block),
            send_sem=send_sems.at[a * per + k], recv_sem=recv_sems.at[a * per + k],
            device_id=to, device_id_type=MESH)

    local = [pltpu.make_async_copy(ins[a], slot(a, x, y, c), local_sems.at[a]) for a in range(n)]
    for cp in local:
        cp.start()
    started = []
    for a in range(n):
        started += [copy(a, 1, (x, y, c), (*xn, c), src=ins[a]), copy(a, 2, (x, y, c), (*yn, c), src=ins[a])]
    for a in range(n):
        started.append(copy(a, 0, (x, y, c), sib, src=ins[a]))
    for cp in started:
        cp.start()
    for a in range(n):
        copy(a, 1, (*xn, c), (x, y, c)).wait_recv()
        copy(a, 2, (*yn, c), (x, y, c)).wait_recv()
        later = [copy(a, 3, (fx, fy, c), (tx, ty, c)), copy(a, 4, (*xn, c), sib), copy(a, 5, (*yn, c), sib)]
        for cp in later:
            cp.start()
        started += later
    for a in range(n):
        copy(a, 3, (*dg, c), (x, y, c)).wait_recv()
        cp = copy(a, 6, (*dg, c), sib)
        cp.start()
        started.append(cp)
    for a in range(n):
        copy(a, 0, sib, (x, y, c)).wait_recv()
        for k, chip in ((4, xn), (5, yn), (6, dg)):
            copy(a, k, (*chip, 1 - c), (x, y, c)).wait_recv()
    for cp in started:
        cp.wait_send()
    for cp in local:
        cp.wait()


def _gather2(name, arrs):
    n = len(arrs)
    per = GATHER_SEMS
    out_shape = [jax.ShapeDtypeStruct((N_DEV,) + a.shape, a.dtype) for a in arrs]

    def body(*refs):
        ins, outs = refs[:n], refs[n:2 * n]
        send_sems, recv_sems, local_sems = refs[2 * n:]
        x, y, c = lax.axis_index("x"), lax.axis_index("y"), lax.axis_index("c")
        _gather_copies(ins, outs, send_sems, recv_sems, local_sems, x, y, c)

    return pl.pallas_call(
        body, name=name, out_shape=out_shape,
        in_specs=[ANY] * n, out_specs=[ANY] * n,
        scratch_shapes=[pltpu.SemaphoreType.DMA((n * per,)), pltpu.SemaphoreType.DMA((n * per,)),
                        pltpu.SemaphoreType.DMA((n,))],
    )(*arrs)


def _seq_gather2(name, collective_id, arrs):
    n = len(arrs)
    per = GATHER_SEMS

    def body(*refs):
        ins, outs = refs[:n], refs[n:2 * n]
        send_sems, recv_sems, local_sems = refs[2 * n:]
        x, y, c = lax.axis_index("x"), lax.axis_index("y"), lax.axis_index("c")
        barrier = pltpu.get_barrier_semaphore()
        for peer in [(x, y, 1 - c), (1 - x, y, c), (x, 1 - y, c)]:
            pl.semaphore_signal(barrier, inc=1, device_id=peer, device_id_type=MESH)
        pl.semaphore_wait(barrier, 3)
        _gather_copies(ins, outs, send_sems, recv_sems, local_sems, x, y, c)

    return pl.kernel(
        body, out_type=[jax.ShapeDtypeStruct((N_DEV,) + a.shape, a.dtype) for a in arrs],
        mesh=plsc.ScalarSubcoreMesh(axis_name="seq", num_cores=1),
        scratch_types=[pltpu.SemaphoreType.DMA((n * per,)), pltpu.SemaphoreType.DMA((n * per,)),
                       pltpu.SemaphoreType.DMA((n,))],
        compiler_params=pltpu.CompilerParams(collective_id=collective_id), name=name,
    )(*arrs)


def _seq_chip_exchange(name, collective_id, arrs):
    n = len(arrs)

    def body(*refs):
        ins, outs = refs[:n], refs[n:2 * n]
        send_sems, recv_sems = refs[2 * n:]
        x, y, c = lax.axis_index("x"), lax.axis_index("y"), lax.axis_index("c")

        def peer(k):
            return (1 - x if (k >> 1) & 1 else x), (1 - y if k & 1 else y)

        barrier = pltpu.get_barrier_semaphore()
        for k in (1, 2, 3):
            pl.semaphore_signal(barrier, inc=1, device_id=(*peer(k), c), device_id_type=MESH)
        pl.semaphore_wait(barrier, 3)

        def copy(a, k):
            px, py = peer(k)
            return pltpu.make_async_remote_copy(
                src_ref=ins[a].at[2 * px + py], dst_ref=outs[a].at[k - 1],
                send_sem=send_sems.at[a * 3 + k - 1], recv_sem=recv_sems.at[a * 3 + k - 1],
                device_id=(px, py, c), device_id_type=MESH)

        cps = [copy(a, k) for a in range(n) for k in (1, 2, 3)]
        for cp in cps:
            cp.start()
        for cp in cps:
            cp.wait_recv()
        for cp in cps:
            cp.wait_send()

    return pl.kernel(
        body, out_type=[jax.ShapeDtypeStruct((3,) + a.shape[1:], a.dtype) for a in arrs],
        mesh=plsc.ScalarSubcoreMesh(axis_name="seq", num_cores=1),
        scratch_types=[pltpu.SemaphoreType.DMA((n * 3,)), pltpu.SemaphoreType.DMA((n * 3,))],
        compiler_params=pltpu.CompilerParams(collective_id=collective_id), name=name,
    )(*arrs)


def _seq_pair_swap(name, collective_id, arrs):
    n = len(arrs)

    def body(*refs):
        ins, outs = refs[:n], refs[n:2 * n]
        send_sems, recv_sems = refs[2 * n:]
        x, y, c = lax.axis_index("x"), lax.axis_index("y"), lax.axis_index("c")
        barrier = pltpu.get_barrier_semaphore()
        pl.semaphore_signal(barrier, inc=1, device_id=(x, y, 1 - c), device_id_type=MESH)
        pl.semaphore_wait(barrier, 1)

        def copy(a, q):
            return pltpu.make_async_remote_copy(
                src_ref=ins[a].at[q, 1 - c], dst_ref=outs[a].at[q],
                send_sem=send_sems.at[a * 4 + q], recv_sem=recv_sems.at[a * 4 + q],
                device_id=(x, y, 1 - c), device_id_type=MESH)

        cps = [copy(a, q) for a in range(n) for q in range(4)]
        for cp in cps:
            cp.start()
        for cp in cps:
            cp.wait_recv()
        for cp in cps:
            cp.wait_send()

    return pl.kernel(
        body, out_type=[jax.ShapeDtypeStruct((4,) + a.shape[2:], a.dtype) for a in arrs],
        mesh=plsc.ScalarSubcoreMesh(axis_name="seq", num_cores=1),
        scratch_types=[pltpu.SemaphoreType.DMA((n * 4,)), pltpu.SemaphoreType.DMA((n * 4,))],
        compiler_params=pltpu.CompilerParams(collective_id=collective_id), name=name,
    )(*arrs)


def _call(body, name, grid, in_specs, out_specs, out_shape, args, scratch=()):
    return pl.pallas_call(
        body, name=name, grid=grid, in_specs=in_specs, out_specs=out_specs, out_shape=out_shape,
        scratch_shapes=list(scratch), compiler_params=_params(("arbitrary",) * len(grid)))(*args)


def _ada_fwd(c_all, w_ada_sh, b_ada_sh):
    nb, d = c_all.shape
    ncol = w_ada_sh.shape[1]

    def body(c_ref, w_ref, b_ref, mod_ref, cact_ref):
        cc = c_ref[...]
        ca = cc * jax.nn.sigmoid(cc)
        cact_ref[...] = ca
        mod_ref[...] = _dot(ca.astype(BF16), w_ref[...].astype(BF16), NN) + b_ref[...]

    return pl.pallas_call(
        body, name="ada_fwd",
        out_shape=[jax.ShapeDtypeStruct((nb, ncol), F32), jax.ShapeDtypeStruct((nb, d), F32)],
        compiler_params=_params(),
    )(c_all, w_ada_sh, b_ada_sh)


def _rms(xv):
    rstd = lax.rsqrt(jnp.mean(xv * xv, axis=-1, keepdims=True) + EPS)
    return xv * rstd, rstd


def _rms_bwd(dxhat, xhat, rstd):
    return rstd * (dxhat - xhat * jnp.mean(dxhat * xhat, axis=-1, keepdims=True))


def _colsum(v):
    return jnp.sum(v, axis=0, keepdims=True)


def _expm1(v, ev):
    series = v * (1.0 + v * (0.5 + v * (1.0 / 6.0 + v * (1.0 / 24.0 + v * (1.0 / 120.0)))))
    return jnp.where(jnp.abs(v) < 0.2, series, ev - 1.0)


def _softplus(v):
    return jnp.maximum(v, 0.0) + jnp.log1p(jnp.exp(-jnp.abs(v)))


def _gelu(v):
    t = jnp.tanh(v * (GELU_K0 + (GELU_K0 * GELU_K1) * (v * v)))
    return 0.5 * v * (1.0 + t), t


def _dgelu(v, t):
    return 0.5 * ((1.0 + t) + (v * (1.0 - t * t)) * (GELU_K0 + (3.0 * GELU_K0 * GELU_K1) * (v * v)))


def _scan_tile(a, b, x0, st, k0, reverse):
    t = a.shape[0]
    off = SUBLANES
    stage_a, stage_b = st.at[k0], st.at[k0 + 1]
    halo = slice(off + t, off + t + SUBLANES) if reverse else slice(0, SUBLANES)
    stage_a[halo, :] = jnp.ones((SUBLANES, a.shape[1]), F32)
    stage_b[halo, :] = jnp.zeros((SUBLANES, a.shape[1]), F32)
    s = 1
    while s < min(t, SUBLANES):
        stage_a[off:off + t, :] = a
        stage_b[off:off + t, :] = b
        at = off + s if reverse else off - s
        b = a * stage_b[at:at + t, :] + b
        a = a * stage_a[at:at + t, :]
        s *= 2
    while s < t:
        if reverse:
            b = jnp.concatenate([a[:t - s] * b[s:] + b[:t - s], b[t - s:]], axis=0)
            a = jnp.concatenate([a[:t - s] * a[s:], a[t - s:]], axis=0)
        else:
            b = jnp.concatenate([b[:s], a[s:] * b[:t - s] + b[s:]], axis=0)
            a = jnp.concatenate([a[:s], a[s:] * a[:t - s]], axis=0)
        s *= 2
    x = b + a * x0
    return x, (x[0:SUBLANES, :] if reverse else x[t - SUBLANES:t, :])


def _lru_gates(u, wa, wx, ba, bx, sp):
    ub = u.astype(BF16)
    r = jax.nn.sigmoid(_dot(ub, wa, NN) + ba)
    i = jax.nn.sigmoid(_dot(ub, wx, NN) + bx)
    log_a = (-RG_C * r) * sp
    a = jnp.exp(log_a)
    mult = jnp.sqrt(-_expm1(log_a, a) * (a + 1.0))
    return ub, r, i, a, mult


def _staged_shifts(stage, v, prev8, next8, downs, ups):
    t = v.shape[0]
    if prev8 is not None:
        stage[0:SUBLANES, :] = prev8
    stage[SUBLANES:SUBLANES + t, :] = v
    if next8 is not None:
        stage[SUBLANES + t:2 * SUBLANES + t, :] = next8
    return ([stage[SUBLANES - k:SUBLANES - k + t, :] for k in downs],
            [stage[SUBLANES + k:SUBLANES + k + t, :] for k in ups])


def _conv3(p, pp, w_ref, lo, stage):
    (p1, p2), _ = _staged_shifts(stage, p, pp, None, (1, 2), ())
    q = (w_ref[0:1, lo:lo + LANES] * p2 + w_ref[1:2, lo:lo + LANES] * p1) + w_ref[2:3, lo:lo + LANES] * p
    return q, p1, p2


def _conv4(xv, xp, w_ref, b_ref, lo, stage):
    (x1, x2, x3), _ = _staged_shifts(stage, xv, xp, None, (1, 2, 3), ())
    u = (((w_ref[0:1, lo:lo + LANES] * x3 + w_ref[1:2, lo:lo + LANES] * x2) + w_ref[2:3, lo:lo + LANES] * x1)
         + w_ref[3:4, lo:lo + LANES] * xv) + b_ref[:, lo:lo + LANES]
    return u, x1, x2, x3


def _mix_in_mixer_fwd(x2d, mod6, g_mix, w_in_t, conv_sc, conv_lru, conv_b, wa_bd, wx_bd, ba, bx, lam, width, tm):
    s, d = x2d.shape
    din = w_in_t.shape[0]
    nt = s // tm
    sub = min(MIX_ROWS, tm)
    nblk = width // LANES

    def body(x_ref, mod_ref, g_ref, w_ref, wsc_ref, wlru_ref, blru_ref, wa_ref, wx_ref, ba_ref, bx_ref, lam_ref,
             hn_ref, proj_ref, ymix_ref, h_ref, buf_ref, halo_ref, hc_ref, stage_ref):
        i = pl.program_id(0)

        @pl.when(i == 0)
        def _():
            buf_ref[1] = jnp.zeros((tm, din), F32)
            halo_ref[...] = jnp.zeros_like(halo_ref)

        @pl.when(i <= 1)
        def _():
            hc_ref[...] = jnp.zeros_like(hc_ref)

        def step(dst, src):
            xhat, _ = _rms(x_ref[...])
            hn = ((xhat * g_ref[...]) * (1.0 + mod_ref[1:2, :]) + mod_ref[0:1, :]).astype(BF16)
            hn_ref[...] = hn
            n_mix = (tm // sub) * nblk
            n_chunk = din // width

            def project(k):
                res = _dot(hn_ref[...], w_ref[k * width:(k + 1) * width, :], NT)
                proj_ref[:, k * width:(k + 1) * width] = res
                dst[:, k * width:(k + 1) * width] = res

            done = 0
            for half in range(tm // sub):
                r0 = half * sub
                rows = slice(r0, r0 + sub)
                for j in range(nblk):
                    lo = j * LANES
                    while done < n_chunk and done * n_mix <= (half * nblk + j) * n_chunk:
                        project(done)
                        done += 1

                    def col(p):
                        return src[rows, p * width + lo:p * width + lo + LANES]

                    def prev(p):
                        c0 = p * width + lo
                        if half == 0:
                            return halo_ref[:, c0:c0 + LANES]
                        return src[r0 - SUBLANES:r0, c0:c0 + LANES]

                    pp = col(1) * col(2)
                    q, _, _ = _conv3(pp, prev(1) * prev(2), wsc_ref, lo, stage_ref.at[0])
                    ymix_ref[rows, lo:lo + LANES] = (col(0) * q).astype(BF16)

                    u, _, _, _ = _conv4(col(4), prev(4), wlru_ref, blru_ref, lo, stage_ref.at[1])
                    sp = _softplus(-lam_ref[:, lo:lo + LANES])
                    _, r, ig, a, mult = _lru_gates(u, wa_ref[j], wx_ref[j], ba_ref[:, lo:lo + LANES],
                                                   bx_ref[:, lo:lo + LANES], sp)
                    h, ends = _scan_tile(a, mult * (ig * u), hc_ref[0:1, lo:lo + LANES], stage_ref, 2, False)
                    h_ref[rows, lo:lo + LANES] = h
                    hc_ref[0:1, lo:lo + LANES] = ends[SUBLANES - 1:SUBLANES, :]
                    gel, _ = _gelu(col(3))
                    ymix_ref[rows, width + lo:width + lo + LANES] = (gel * h).astype(BF16)
            while done < n_chunk:
                project(done)
                done += 1
            halo_ref[...] = src[tm - SUBLANES:tm, :]

        @pl.when(i % 2 == 0)
        def _():
            step(buf_ref.at[0], buf_ref.at[1])

        @pl.when(i % 2 == 1)
        def _():
            step(buf_ref.at[1], buf_ref.at[0])

    small = [conv_sc, conv_lru, conv_b, wa_bd, wx_bd, ba, bx, lam]
    cur = lambda i: (jnp.minimum(i, nt - 1), 0)
    last = lambda i: (jnp.maximum(i - 1, 0), 0)
    outs = _call(
        body, "mix_in_mixer_fwd", (nt + 1,),
        [pl.BlockSpec((tm, d), cur), _full(mod6.shape), _full(g_mix.shape), _full(w_in_t.shape)]
        + [_full(a.shape) for a in small],
        [pl.BlockSpec((tm, d), cur), pl.BlockSpec((tm, din), cur),
         pl.BlockSpec((tm, 2 * width), last), pl.BlockSpec((tm, width), last)],
        [jax.ShapeDtypeStruct((s, d), BF16), jax.ShapeDtypeStruct((s, din), F32),
         jax.ShapeDtypeStruct((s, 2 * width), BF16), jax.ShapeDtypeStruct((s, width), F32)],
        [x2d, mod6, g_mix, w_in_t, *small],
        scratch=[pltpu.VMEM((2, tm, din), F32), pltpu.VMEM((SUBLANES, din), F32), pltpu.VMEM((SUBLANES, width), F32),
                 pltpu.VMEM((4, sub + 2 * SUBLANES, LANES), F32)])
    return outs


def _mix_out_fwd(ymix, x2d, w_out, mod6, g_mlp, tm):
    s, d = x2d.shape

    def body(y_ref, x_ref, w_ref, mod_ref, g_ref, mix_ref, x2_ref, hn_ref):
        mix = _dot(y_ref[...], w_ref[...], NN)
        mix_ref[...] = mix.astype(BF16)
        x2 = x_ref[...] + mod_ref[2:3, :] * mix
        x2_ref[...] = x2
        xhat, _ = _rms(x2)
        hn_ref[...] = ((xhat * g_ref[...]) * (1.0 + mod_ref[4:5, :]) + mod_ref[3:4, :]).astype(BF16)

    tile = pl.BlockSpec((tm, d), lambda i: (i, 0))
    return _call(
        body, "mix_out_fwd", (s // tm,),
        [tile, tile, _full(w_out.shape), _full(mod6.shape), _full(g_mlp.shape)],
        [tile, tile, tile],
        [jax.ShapeDtypeStruct((s, d), BF16), jax.ShapeDtypeStruct((s, d), F32), jax.ShapeDtypeStruct((s, d), BF16)],
        [ymix, x2d, w_out, mod6, g_mlp])


def _mlp_fwd_loss(hn2, w_up_t, w_down, x2, target, mod6, g_final, tm, tk):
    s, d = hn2.shape
    f = w_up_t.shape[0]
    nk = f // tk

    def body(hn_ref, wu_ref, wd_ref, x2_hbm, t_hbm, mod_ref, g_ref, z_ref, dx3_ref, dyb_ref, st_ref,
             y_ref, x2_ref, t_ref, sems):
        i, k = pl.program_id(0), pl.program_id(1)

        def fetch():
            rows = pl.ds(pl.multiple_of(i * tm, tm), tm)
            return (pltpu.make_async_copy(x2_hbm.at[rows, :], x2_ref, sems.at[0]),
                    pltpu.make_async_copy(t_hbm.at[rows, :], t_ref, sems.at[1]))

        @pl.when(jnp.logical_and(i == 0, k == 0))
        def _():
            st_ref[...] = jnp.zeros_like(st_ref)

        @pl.when(k == 0)
        def _():
            for cp in fetch():
                cp.start()
            y_ref[...] = jnp.zeros_like(y_ref)

        z = jnp.maximum(_dot(hn_ref[...], wu_ref[...], NT), 0.0)
        z_ref[...] = z.astype(BF16)
        y_ref[...] += _dot((z * z).astype(BF16), wd_ref[...], NN)

        @pl.when(k == nk - 1)
        def _():
            for cp in fetch():
                cp.wait()
            gate = mod_ref[5:6, :]
            yv = y_ref[...]
            xhat, rstd = _rms(x2_ref[...] + gate * yv)
            diff = xhat * g_ref[...] - t_ref[...]
            dyo = diff * (1.0 / d)
            dx3 = _rms_bwd(dyo * g_ref[...], xhat, rstd)
            dx3_ref[...] = dx3
            dyb_ref[...] = (gate * dx3).astype(BF16)
            st_ref[0:1, :] += _colsum(dyo * xhat)
            st_ref[1:2, :] += _colsum(dx3 * yv)
            st_ref[2:3, :] += _colsum(diff * diff)

    tile = pl.BlockSpec((tm, d), lambda i, k: (i, 0))
    wblk = pl.BlockSpec((tk, d), lambda i, k: (k, 0))
    return pl.pallas_call(
        body, name="mlp_fwd_loss", grid=(s // tm, nk),
        in_specs=[tile, wblk, wblk, ANY, ANY, _full(mod6.shape), _full(g_final.shape)],
        out_specs=[pl.BlockSpec((tm, tk), lambda i, k: (i, k)), tile, tile, _full((SUBLANES, d))],
        out_shape=[jax.ShapeDtypeStruct((s, f), BF16), jax.ShapeDtypeStruct((s, d), F32),
                   jax.ShapeDtypeStruct((s, d), BF16), jax.ShapeDtypeStruct((SUBLANES, d), F32)],
        scratch_shapes=[pltpu.VMEM((tm, d), F32), pltpu.VMEM((tm, d), F32), pltpu.VMEM((tm, d), F32),
                        pltpu.SemaphoreType.DMA((2,))],
        compiler_params=pltpu.CompilerParams(dimension_semantics=("arbitrary", "arbitrary"),
                                             vmem_limit_bytes=VMEM_LIMIT_BIG),
    )(hn2, w_up_t, w_down, x2, target, mod6, g_final)


def _mlp_bwd_dx(dyb, z, w_down, w_up_t, tm, tk):
    s, d = dyb.shape
    f = z.shape[1]

    nk = f // tk

    def body(dy_ref, z_ref, wd_ref, wu_ref, dz_ref, dh_ref, acc_ref):
        k = pl.program_id(1)

        @pl.when(k == 0)
        def _():
            acc_ref[...] = jnp.zeros_like(acc_ref)

        dz = ((2.0 * z_ref[...].astype(F32)) * _dot(dy_ref[...], wd_ref[...], NT)).astype(BF16)
        dz_ref[...] = dz
        acc_ref[...] += _dot(dz, wu_ref[...], NN)

        @pl.when(k == nk - 1)
        def _():
            dh_ref[...] = acc_ref[...].astype(BF16)

    return pl.pallas_call(
        body, name="mlp_bwd_dx", grid=(s // tm, nk),
        in_specs=[pl.BlockSpec((tm, d), lambda i, k: (i, 0)), pl.BlockSpec((tm, tk), lambda i, k: (i, k)),
                  pl.BlockSpec((tk, d), lambda i, k: (k, 0)), pl.BlockSpec((tk, d), lambda i, k: (k, 0))],
        out_specs=[pl.BlockSpec((tm, tk), lambda i, k: (i, k)), pl.BlockSpec((tm, d), lambda i, k: (i, 0))],
        out_shape=[jax.ShapeDtypeStruct((s, f), BF16), jax.ShapeDtypeStruct((s, d), BF16)],
        scratch_shapes=[pltpu.VMEM((tm, d), F32)],
        compiler_params=_params(("parallel", "arbitrary")),
    )(dyb, z, w_down, w_up_t)


def _mlp_bwd_dw(z, dz, dyb, hn2, tm, tk):
    s, d = dyb.shape
    f = z.shape[1]

    def body(z_ref, dz_ref, dy_ref, hn_ref, gd_ref, gu_ref):
        i = pl.program_id(1)

        @pl.when(i == 0)
        def _():
            gd_ref[...] = jnp.zeros_like(gd_ref)
            gu_ref[...] = jnp.zeros_like(gu_ref)

        zf = z_ref[...].astype(F32)
        gd_ref[...] += _dot((zf * zf).astype(BF16), dy_ref[...], TN)
        gu_ref[...] += _dot(dz_ref[...], hn_ref[...], TN)

    return pl.pallas_call(
        body, name="mlp_bwd_dw", grid=(f // tk, s // tm),
        in_specs=[pl.BlockSpec((tm, tk), lambda k, i: (i, k)), pl.BlockSpec((tm, tk), lambda k, i: (i, k)),
                  pl.BlockSpec((tm, d), lambda k, i: (i, 0)), pl.BlockSpec((tm, d), lambda k, i: (i, 0))],
        out_specs=[pl.BlockSpec((tk, d), lambda k, i: (k, 0)), pl.BlockSpec((tk, d), lambda k, i: (k, 0))],
        out_shape=[jax.ShapeDtypeStruct((f, d), F32), jax.ShapeDtypeStruct((f, d), F32)],
        compiler_params=_params(("parallel", "arbitrary")),
    )(z, dz, dyb, hn2)


def _mix_out_bwd(dhn2, x2, dx3, mix, ymix, w_out, mod6, g_mlp, tm):
    s, d = x2.shape

    def body(dh_ref, x2_ref, dx3_ref, mix_ref, y_ref, w_ref, mod_ref, g_ref, dx2_ref, dym_ref, gw_ref, st_ref):
        i = pl.program_id(0)

        @pl.when(i == 0)
        def _():
            st_ref[...] = jnp.zeros_like(st_ref)
            gw_ref[...] = jnp.zeros_like(gw_ref)

        dh = dh_ref[...].astype(F32)
        xhat, rstd = _rms(x2_ref[...])
        dn = dh * (1.0 + mod_ref[4:5, :])
        dx2 = dx3_ref[...] + _rms_bwd(dn * g_ref[...], xhat, rstd)
        dx2_ref[...] = dx2
        st_ref[0:1, :] += _colsum(dh)
        st_ref[1:2, :] += _colsum(dh * (xhat * g_ref[...]))
        st_ref[2:3, :] += _colsum(dn * xhat)
        st_ref[3:4, :] += _colsum(dx2 * mix_ref[...].astype(F32))
        dmix = (mod_ref[2:3, :] * dx2).astype(BF16)
        dym_ref[...] = _dot(dmix, w_ref[...], NT)
        gw_ref[...] += _dot(y_ref[...], dmix, TN)

    tile = pl.BlockSpec((tm, d), lambda i: (i, 0))
    return _call(
        body, "mix_out_bwd", (s // tm,),
        [tile, tile, tile, tile, tile, _full(w_out.shape), _full(mod6.shape), _full(g_mlp.shape)],
        [tile, tile, _full((d, d)), _full((SUBLANES, d))],
        [jax.ShapeDtypeStruct((s, d), F32), jax.ShapeDtypeStruct((s, d), F32),
         jax.ShapeDtypeStruct((d, d), F32), jax.ShapeDtypeStruct((SUBLANES, d), F32)],
        [dhn2, x2, dx3, mix, ymix, w_out, mod6, g_mlp])


def _mixer_bwd(proj, dymix, h_all, conv_sc, conv_lru, conv_b, wa_bd, wx_bd, ba, bx, lam, width):
    s, din = proj.shape
    t = min(MIX_ROWS, s)
    nt = s // t
    nblk = width // LANES
    hb = t // SUBLANES
    last8 = s // SUBLANES - 1

    def body(proj_ref, projp_ref, projn_ref, dy_ref, dyn_ref, h_ref, hp_ref,
             wsc_ref, wlru_ref, blru_ref, wa_ref, wx_ref, ba_ref, bx_ref, lam_ref,
             dproj_ref, small_ref, gwa_ref, gwx_ref, an_ref, gn_ref, dun_ref, stage_ref):
        i = pl.program_id(0)

        @pl.when(i == 0)
        def _():
            small_ref[...] = jnp.zeros_like(small_ref)
            gwa_ref[...] = jnp.zeros_like(gwa_ref)
            gwx_ref[...] = jnp.zeros_like(gwx_ref)
            an_ref[...] = jnp.zeros_like(an_ref)
            gn_ref[...] = jnp.zeros_like(gn_ref)
            dun_ref[...] = jnp.zeros_like(dun_ref)

        has_prev = i < nt - 1
        has_next = i > 0
        for j in range(nblk):
            lo = j * LANES
            ls = slice(lo, lo + LANES)

            def col(p, ref=proj_ref):
                return ref[:, p * width + lo:p * width + lo + LANES]

            def prev(p):
                return jnp.where(has_prev, col(p, projp_ref), 0.0)

            def nxt(p):
                return jnp.where(has_next, col(p, projn_ref), 0.0)

            def add_row(r, v):
                small_ref[r:r + 1, ls] += _colsum(v)

            sc_b, sc_c, sc_x = col(0), col(1), col(2)
            p = sc_c * sc_x
            q, p1, p2 = _conv3(p, prev(1) * prev(2), wsc_ref, lo, stage_ref.at[0])
            dys = dy_ref[:, ls]
            dproj_ref[:, ls] = (dys * q).astype(BF16)
            dq = dys * sc_b
            dqn = jnp.where(has_next, dyn_ref[:, ls], 0.0) * nxt(0)
            _, (dq1, dq2) = _staged_shifts(stage_ref.at[1], dq, None, dqn, (), (1, 2))
            dp = (wsc_ref[2:3, ls] * dq + wsc_ref[1:2, ls] * dq1) + wsc_ref[0:1, ls] * dq2
            dproj_ref[:, width + lo:width + lo + LANES] = (dp * sc_x).astype(BF16)
            dproj_ref[:, 2 * width + lo:2 * width + lo + LANES] = (dp * sc_c).astype(BF16)
            add_row(0, dq * p2)
            add_row(1, dq * p1)
            add_row(2, dq * p)

            xv = col(4)
            u, x1, x2, x3 = _conv4(xv, prev(4), wlru_ref, blru_ref, lo, stage_ref.at[2])
            lam_v = lam_ref[:, ls]
            sp = _softplus(-lam_v)
            wa, wx = wa_ref[j], wx_ref[j]
            ub, r, ig, a, mult = _lru_gates(u, wa, wx, ba_ref[:, ls], bx_ref[:, ls], sp)
            iu = ig * u
            h = h_ref[:, ls]
            (hm1,), _ = _staged_shifts(stage_ref.at[3], h, jnp.where(has_prev, hp_ref[:, ls], 0.0), None, (1,), ())
            lyv = col(3)
            gel, th = _gelu(lyv)
            dyl = dy_ref[:, width + lo:width + lo + LANES]
            dproj_ref[:, 3 * width + lo:3 * width + lo + LANES] = (dyl * h * _dgelu(lyv, th)).astype(BF16)
            a_next = jnp.broadcast_to(an_ref[0:1, ls], (SUBLANES, LANES))
            _, (a_up,) = _staged_shifts(stage_ref.at[4], a, None, a_next, (), (1,))
            g, _ = _scan_tile(a_up, dyl * gel, gn_ref[0:1, ls], stage_ref, 5, True)
            an_ref[0:1, ls] = a[0:1, :]
            gn_ref[0:1, ls] = g[0:1, :]
            da = g * hm1
            dmult = g * iu
            diu = g * mult
            dlog_a = da * a - dmult * ((a * a) / mult)
            dpre_a = (dlog_a * (-RG_C * sp)) * (r * (1.0 - r))
            dpre_x = (diu * u) * (ig * (1.0 - ig))
            dab, dxb = dpre_a.astype(BF16), dpre_x.astype(BF16)
            du = diu * ig + _dot(dab, wa, NT) + _dot(dxb, wx, NT)
            gwa_ref[j] += _dot(ub, dab, TN)
            gwx_ref[j] += _dot(ub, dxb, TN)
            dun = dun_ref[:, ls]
            dun_ref[:, ls] = du[0:SUBLANES, :]
            _, (du1, du2, du3) = _staged_shifts(stage_ref.at[7], du, None, dun, (), (1, 2, 3))
            dlx = (((wlru_ref[3:4, ls] * du + wlru_ref[2:3, ls] * du1) + wlru_ref[1:2, ls] * du2)
                   + wlru_ref[0:1, ls] * du3)
            dproj_ref[:, 4 * width + lo:4 * width + lo + LANES] = dlx.astype(BF16)
            add_row(3, du * x3)
            add_row(4, du * x2)
            add_row(5, du * x1)
            add_row(6, du * xv)
            add_row(7, du)
            add_row(8, dpre_a)
            add_row(9, dpre_x)
            add_row(10, (dlog_a * (RG_C * r)) * jax.nn.sigmoid(-lam_v))

    small = [conv_sc, conv_lru, conv_b, wa_bd, wx_bd, ba, bx, lam]
    rev = lambda i: nt - 1 - i
    return _call(
        body, "mixer_bwd", (nt,),
        [pl.BlockSpec((t, din), lambda i: (rev(i), 0)),
         pl.BlockSpec((SUBLANES, din), lambda i: (jnp.maximum(rev(i) * hb - 1, 0), 0)),
         pl.BlockSpec((SUBLANES, din), lambda i: (jnp.minimum((rev(i) + 1) * hb, last8), 0)),
         pl.BlockSpec((t, 2 * width), lambda i: (rev(i), 0)),
         pl.BlockSpec((SUBLANES, 2 * width), lambda i: (jnp.minimum((rev(i) + 1) * hb, last8), 0)),
         pl.BlockSpec((t, width), lambda i: (rev(i), 0)),
         pl.BlockSpec((SUBLANES, width), lambda i: (jnp.maximum(rev(i) * hb - 1, 0), 0))]
        + [_full(a.shape) for a in small],
        [pl.BlockSpec((t, din), lambda i: (rev(i), 0)), _full((2 * SUBLANES, width)),
         _full(wa_bd.shape), _full(wx_bd.shape)],
        [jax.ShapeDtypeStruct((s, din), BF16), jax.ShapeDtypeStruct((2 * SUBLANES, width), F32),
         jax.ShapeDtypeStruct(wa_bd.shape, F32), jax.ShapeDtypeStruct(wx_bd.shape, F32)],
        [proj, proj, proj, dymix, dymix, h_all, h_all, *small],
        scratch=[pltpu.VMEM((SUBLANES, width), F32), pltpu.VMEM((SUBLANES, width), F32),
                 pltpu.VMEM((SUBLANES, width), F32), pltpu.VMEM((8, t + 2 * SUBLANES, LANES), F32)])


def _mix_in_bwd_dx(dproj, x2d, dx2, w_in_t, mod6, g_mix, tm):
    s, d = x2d.shape
    din = dproj.shape[1]

    def body(dp_ref, x_ref, dx2_ref, w_ref, mod_ref, g_ref, gx_ref, st_ref):
        i = pl.program_id(0)

        @pl.when(i == 0)
        def _():
            st_ref[...] = jnp.zeros_like(st_ref)

        dh = _dot(dp_ref[...], w_ref[...], NN)
        xhat, rstd = _rms(x_ref[...])
        dn = dh * (1.0 + mod_ref[1:2, :])
        gx_ref[...] = dx2_ref[...] + _rms_bwd(dn * g_ref[...], xhat, rstd)
        st_ref[0:1, :] += _colsum(dh)
        st_ref[1:2, :] += _colsum(dh * (xhat * g_ref[...]))
        st_ref[2:3, :] += _colsum(dn * xhat)

    tile = pl.BlockSpec((tm, d), lambda i: (i, 0))
    return _call(
        body, "mix_in_bwd_dx", (s // tm,),
        [pl.BlockSpec((tm, din), lambda i: (i, 0)), tile, tile, _full(w_in_t.shape), _full(mod6.shape),
         _full(g_mix.shape)],
        [tile, _full((SUBLANES, d))],
        [jax.ShapeDtypeStruct((s, d), F32), jax.ShapeDtypeStruct((SUBLANES, d), F32)],
        [dproj, x2d, dx2, w_in_t, mod6, g_mix])


def _mix_in_bwd_dw(dproj, hn1, tm, tn):
    s, d = hn1.shape
    din = dproj.shape[1]

    def body(dp_ref, hn_ref, gw_ref):
        i = pl.program_id(1)

        @pl.when(i == 0)
        def _():
            gw_ref[...] = jnp.zeros_like(gw_ref)

        gw_ref[...] += _dot(dp_ref[...], hn_ref[...], TN)

    return _call(
        body, "mix_in_bwd_dw", (din // tn, s // tm),
        [pl.BlockSpec((tm, tn), lambda p, i: (i, p)), pl.BlockSpec((tm, d), lambda p, i: (i, 0))],
        [pl.BlockSpec((tn, d), lambda p, i: (p, 0))],
        [jax.ShapeDtypeStruct((din, d), F32)],
        [dproj, hn1])


def _adamw(w, g, m, v):
    m = ADAM_B1 * m + (1.0 - ADAM_B1) * g
    v = ADAM_B2 * v + (1.0 - ADAM_B2) * (g * g)
    m_hat = m / (1.0 - ADAM_B1 ** ADAM_STEP)
    v_hat = v / (1.0 - ADAM_B2 ** ADAM_STEP)
    delta = -ADAM_LR * (m_hat / (jnp.sqrt(v_hat) + ADAM_EPS) + ADAM_WD * w)
    return delta, m, v


def _pair_sum(g4s, h4s, core_chip, tr, name):
    na = len(g4s)
    _, _, r, n = g4s[0].shape

    def body(sc_ref, *refs):
        q = pl.program_id(1)
        for a in range(na):
            g_ref, h_ref = refs[2 * a], refs[2 * a + 1]
            sb_ref, own_ref = refs[2 * na + 2 * a], refs[2 * na + 2 * a + 1]
            ssum = g_ref[...] + h_ref[...]
            sb_ref[...] = ssum.astype(BF16)

            @pl.when(q == sc_ref[1])
            def _():
                own_ref[...] = ssum

    grid_spec = pltpu.PrefetchScalarGridSpec(
        num_scalar_prefetch=1, grid=(r // tr, 4),
        in_specs=[pl.BlockSpec((None, None, tr, n), lambda i, q, sc: (q, sc[0], i, 0)),
                  pl.BlockSpec((None, tr, n), lambda i, q, sc: (q, i, 0))] * na,
        out_specs=[pl.BlockSpec((None, tr, n), lambda i, q, sc: (q, i, 0)),
                   pl.BlockSpec((tr, n), lambda i, q, sc: (i, 0))] * na)
    outs = pl.pallas_call(
        body, name=name, grid_spec=grid_spec,
        out_shape=[jax.ShapeDtypeStruct((4, r, n), BF16), jax.ShapeDtypeStruct((r, n), F32)] * na,
        compiler_params=_params(("parallel", "arbitrary")),
    )(core_chip, *[x for pair in zip(g4s, h4s) for x in pair])
    return [(outs[2 * a], outs[2 * a + 1]) for a in range(na)]


def _sum4_adam(own, parts, w, m, v, tr, name, transposed):
    r, n = own.shape
    rows, cols = w.shape

    def body(o_ref, p_ref, w_ref, m_ref, v_ref, g_ref, d_ref, nm_ref, nv_ref):
        g = o_ref[...]
        for k in range(3):
            g = g + p_ref[k].astype(F32)
        if transposed:
            g = g.T
        g_ref[...] = g
        d_ref[...], nm_ref[...], nv_ref[...] = _adamw(w_ref[...], g, m_ref[...], v_ref[...])

    if transposed:
        g_specs = [pl.BlockSpec((r, tr), lambda i: (0, i)), pl.BlockSpec((3, r, tr), lambda i: (0, 0, i))]
    else:
        g_specs = [pl.BlockSpec((tr, n), lambda i: (i, 0)), pl.BlockSpec((3, tr, n), lambda i: (0, i, 0))]
    tile = pl.BlockSpec((tr, cols), lambda i: (i, 0))
    return pl.pallas_call(
        body, name=name, grid=(rows // tr,),
        in_specs=g_specs + [tile] * 3, out_specs=[tile] * 4,
        out_shape=[jax.ShapeDtypeStruct((rows, cols), F32)] * 4,
        compiler_params=_params(("parallel",)),
    )(own, parts, w, m, v)


def _sum8(parts, tr, name):
    _, rows, n = parts.shape

    def body(p_ref, o_ref):
        acc = p_ref[0]
        for k in range(1, N_DEV):
            acc = acc + p_ref[k]
        o_ref[...] = acc

    return pl.pallas_call(
        body, name=name, grid=(rows // tr,),
        in_specs=[pl.BlockSpec((N_DEV, tr, n), lambda i: (0, i, 0))],
        out_specs=pl.BlockSpec((tr, n), lambda i: (i, 0)),
        out_shape=jax.ShapeDtypeStruct((rows, n), F32),
        compiler_params=_params(("parallel",)),
    )(parts)


def _ada_bwd_adam(cact_t, dmod_cols, w, m, v, tr):
    rows, n = w.shape

    def body(c_ref, d_ref, w_ref, m_ref, v_ref, g_ref, dl_ref, nm_ref, nv_ref):
        def term(b):
            return c_ref[b].astype(BF16).astype(F32) * d_ref[b:b + 1, :].astype(BF16).astype(F32)

        g = term(0)
        for b in range(1, N_DEV):
            g = g + term(b)
        g_ref[...] = g
        dl_ref[...], nm_ref[...], nv_ref[...] = _adamw(w_ref[...], g, m_ref[...], v_ref[...])

    tile = pl.BlockSpec((tr, n), lambda i: (i, 0))
    return pl.pallas_call(
        body, name="ada_bwd_adam", grid=(rows // tr,),
        in_specs=[pl.BlockSpec((N_DEV, tr, 1), lambda i: (0, i, 0)), _full(dmod_cols.shape), tile, tile, tile],
        out_specs=[tile] * 4,
        out_shape=[jax.ShapeDtypeStruct((rows, n), F32)] * 4,
        compiler_params=_params(("parallel",)),
    )(cact_t, dmod_cols, w, m, v)


def _adam_small(ws, gs, ms, vs):
    n = len(ws)

    def body(*refs):
        w_r, g_r, m_r, v_r = refs[:n], refs[n:2 * n], refs[2 * n:3 * n], refs[3 * n:4 * n]
        d_r, nm_r, nv_r = refs[4 * n:5 * n], refs[5 * n:6 * n], refs[6 * n:7 * n]
        for k in range(n):
            d_r[k][...], nm_r[k][...], nv_r[k][...] = _adamw(w_r[k][...], g_r[k][...], m_r[k][...], v_r[k][...])

    shapes = [jax.ShapeDtypeStruct(w.shape, F32) for w in ws]
    outs = pl.pallas_call(
        body, name="adam_small", out_shape=shapes * 3, compiler_params=_params(),
    )(*ws, *gs, *ms, *vs)
    return outs[:n], outs[n:2 * n], outs[2 * n:]


def _block_diag(w):
    h, hd, _ = w.shape
    per = LANES // hd
    eye = jnp.eye(per, dtype=w.dtype)
    w5 = w.reshape(h // per, per, hd, 1, hd) * eye[None, :, None, :, None]
    return w5.reshape(h // per, LANES, LANES)


def _block_diag_grad(g, h, hd):
    per = LANES // hd
    g5 = g.reshape(h // per, per, hd, per, hd)
    return jnp.stack([g5[:, a, :, a, :] for a in range(per)], axis=1).reshape(h, hd, hd)


def kernel(x, c, w_ada, b_ada, g_mix, w_in, conv_w_sc, conv_w_lru, conv_b_lru, w_rg_a, b_rg_a, w_rg_x, b_rg_x, lru_lambda, w_out, g_mlp, w_up, w_down, g_final, loss_target, m_w_ada, m_b_ada, m_g_mix, m_w_in, m_conv_w_sc, m_conv_w_lru, m_conv_b_lru, m_w_rg_a, m_b_rg_a, m_w_rg_x, m_b_rg_x, m_lru_lambda, m_w_out, m_g_mlp, m_w_up, m_w_down, m_g_final, v_w_ada, v_b_ada, v_g_mix, v_w_in, v_conv_w_sc, v_conv_w_lru, v_conv_b_lru, v_w_rg_a, v_b_rg_a, v_w_rg_x, v_b_rg_x, v_lru_lambda, v_w_out, v_g_mlp, v_w_up, v_w_down, v_g_final):
    s, d = x.shape[1], x.shape[2]
    width = conv_b_lru.shape[1]
    heads, hd = w_rg_a.shape[1], w_rg_a.shape[2]
    f = w_down.shape[1] * N_DEV
    n_ada = w_ada.shape[2]
    csh = conv_w_sc.shape[2]
    me = 4 * lax.axis_index("x") + 2 * lax.axis_index("y") + lax.axis_index("c")
    tm = min(512, s)
    tm_mlp = min(1024, s)
    tk = 512

    x2d = x[0]
    tgt = loss_target[0]

    pay = jnp.zeros((SUBLANES, d), F32)
    pay = pay.at[0:1, :].set(c)
    pay = pay.at[1:4, 0:csh].set(conv_w_sc[0])
    pay = pay.at[4:8, 0:csh].set(conv_w_lru[0])
    w_in_t_sh = w_in[0].T.astype(BF16)
    w_up_t_sh = w_up[0].T.astype(BF16)
    w_out_sh = w_out[0].astype(BF16)
    w_down_sh = w_down[0].astype(BF16)
    pay_all, w_in_t = _gather2("gather_in", [pay, w_in_t_sh])
    w_in_t = w_in_t.reshape(-1, d)
    c_all = pay_all[:, 0, :]
    conv_sc = pay_all[:, 1:4, 0:csh].transpose(1, 0, 2).reshape(3, width)
    conv_lru = pay_all[:, 4:8, 0:csh].transpose(1, 0, 2).reshape(4, width)

    b_ada_sh = lax.dynamic_slice(b_ada, (0, me * n_ada), (1, n_ada))
    mod_cols, c_act = _ada_fwd(c_all, w_ada[0], b_ada_sh)
    (mod_rows,) = _exchange("scatter_mod", [], [mod_cols.reshape(N_DEV, 1, n_ada)])
    mod_rows, w_out_sh, w_up_t_sh, w_down_sh = lax.optimization_barrier((mod_rows, w_out_sh, w_up_t_sh, w_down_sh))
    (w_out_g,) = _seq_gather2("gather_w_out", 1, [w_out_sh])
    w_up_g, w_down_g = _seq_gather2("gather_mlp_weights", 2, [w_up_t_sh, w_down_sh])
    mod6 = jnp.zeros((SUBLANES, d), F32).at[0:6, :].set(mod_rows.reshape(6, d))

    wa_bd = _block_diag(w_rg_a[0]).astype(BF16)
    wx_bd = _block_diag(w_rg_x[0]).astype(BF16)
    ba = b_rg_a.reshape(1, width)
    bx = b_rg_x.reshape(1, width)
    g_fin = g_final.reshape(1, d)

    hn1, proj, ymix, h_all = _mix_in_mixer_fwd(x2d, mod6, g_mix, w_in_t, conv_sc, conv_lru, conv_b_lru,
                                               wa_bd, wx_bd, ba, bx, lru_lambda, width, tm)
    w_out_b = w_out_g.reshape(-1, d)
    mix, x2, hn2 = _mix_out_fwd(ymix, x2d, w_out_b, mod6, g_mlp, tm_mlp)
    w_up_t = w_up_g.reshape(-1, d)
    w_down_b = w_down_g.reshape(-1, d)
    z, dx3, dyb, st_fin = _mlp_fwd_loss(hn2, w_up_t, w_down_b, x2, tgt, mod6, g_fin, tm_mlp, 2 * tk)

    core_chip = jnp.stack([lax.axis_index("c"), 2 * lax.axis_index("x") + lax.axis_index("y")]).astype(jnp.int32)
    dz, dhn2 = _mlp_bwd_dx(dyb, z, w_down_b, w_up_t, tm_mlp, 2 * tk)
    g_down, g_up_t = _mlp_bwd_dw(z, dz, dyb, hn2, tm_mlp, 2 * tk)
    g_up4, g_down4 = g_up_t.reshape(4, 2, -1, d), g_down.reshape(4, 2, -1, d)
    h_up, h_down = _seq_pair_swap("swap_mlp_grads", 7, [g_up4, g_down4])
    dx2, dymix, g_out, st_out = _mix_out_bwd(dhn2, x2, dx3, mix, ymix, w_out_b, mod6, g_mlp, tm)
    h_up, h_down, g_out = lax.optimization_barrier((h_up, h_down, g_out))
    (sb_up, own_up), (sb_down, own_down) = _pair_sum([g_up4, g_down4], [h_up, h_down], core_chip, 256, "pair_sum_mlp")
    g_out4 = g_out.reshape(4, 2, -1, d)
    (h_out,) = _seq_pair_swap("swap_w_out_grad", 8, [g_out4])
    p_up, p_down = _seq_chip_exchange("exchange_mlp_grads", 3, [sb_up, sb_down])
    dproj, g_small, g_wa, g_wx = _mixer_bwd(
        proj, dymix, h_all, conv_sc, conv_lru, conv_b_lru, wa_bd, wx_bd, ba, bx, lru_lambda, width)
    h_out, dproj = lax.optimization_barrier((h_out, dproj))
    ((sb_out, own_out),) = _pair_sum([g_out4], [h_out], core_chip, g_out4.shape[2], "pair_sum_w_out")
    (p_out,) = _seq_chip_exchange("exchange_w_out_grad", 4, [sb_out])
    grad_x, st_in = _mix_in_bwd_dx(dproj, x2d, dx2, w_in_t, mod6, g_mix, tm)

    small = jnp.concatenate([
        st_in[0:2], st_out[3:4], st_out[0:2], st_fin[1:2],
        st_in[2:3], st_out[2:3], st_fin[0:1],
        jnp.concatenate([g_small[7:8], g_small[10:11]], axis=1),
        jnp.concatenate([g_small[8:9], g_small[9:10]], axis=1),
        jnp.concatenate([jnp.concatenate([g_small[0:3], jnp.zeros((1, width), F32)], axis=0), g_small[3:7]], axis=1),
        st_fin[2:3],
        _block_diag_grad(g_wa, heads, hd).reshape(-1, d),
        _block_diag_grad(g_wx, heads, hd).reshape(-1, d),
    ], axis=0)

    (small_all,) = _seq_gather2("gather_small_grads", 5, [small])
    g_in_t, = _mix_in_bwd_dw(dproj, hn1, min(2048, s), dproj.shape[1] // 2)
    g_in4 = g_in_t.reshape(4, 2, -1, d)
    (h_in,) = _seq_pair_swap("swap_w_in_grad", 9, [g_in4])
    p_up, p_down, p_out, small_all, g_in_t = lax.optimization_barrier((p_up, p_down, p_out, small_all, g_in_t))

    ad_up = _sum4_adam(own_up, p_up, w_up[0], m_w_up[0], v_w_up[0], 256, "adam_w_up", True)
    h_in, ad_up = lax.optimization_barrier((h_in, ad_up))
    ((sb_in, own_in),) = _pair_sum([g_in4], [h_in], core_chip, g_in4.shape[2], "pair_sum_w_in")
    (p_in,) = _seq_chip_exchange("exchange_w_in_grad", 6, [sb_in])
    ad_out = _sum4_adam(own_out, p_out, w_out[0], m_w_out[0], v_w_out[0], w_out.shape[1], "adam_w_out", False)
    ad_down = _sum4_adam(own_down, p_down, w_down[0], m_w_down[0], v_w_down[0], 256, "adam_w_down", False)

    gsum = _sum8(small_all, SMALL_ROWS, "sum_small")
    loss = (0.5 / d) * jnp.sum(gsum[15])
    dmod_cols = lax.dynamic_slice(small_all[:, 0:6, :].reshape(N_DEV, 6 * d), (0, me * n_ada), (N_DEV, n_ada))
    g_ada, d_ada, nm_ada, nv_ada = _ada_bwd_adam(c_act[:, :, None], dmod_cols, w_ada[0], m_w_ada[0], v_w_ada[0], 256)

    g_conv = lax.dynamic_slice(gsum[11:15, 0:width], (0, me * csh), (4, csh))
    g_conv_l = lax.dynamic_slice(gsum[11:15, width:2 * width], (0, me * csh), (4, csh))
    small_g = [
        gsum[0:6].reshape(1, 6 * d),
        gsum[6:7],
        g_conv[0:3].reshape(1, 3, csh),
        g_conv_l.reshape(1, 4, csh),
        gsum[9:10, 0:width],
        gsum[16:48].reshape(1, heads, hd, hd),
        gsum[10:11, 0:width].reshape(1, heads, hd),
        gsum[48:80].reshape(1, heads, hd, hd),
        gsum[10:11, width:].reshape(1, heads, hd),
        gsum[9:10, width:],
        gsum[7:8],
        gsum[8],
    ]
    small_w = [b_ada, g_mix, conv_w_sc, conv_w_lru, conv_b_lru, w_rg_a, b_rg_a, w_rg_x, b_rg_x, lru_lambda, g_mlp, g_final]
    small_m = [m_b_ada, m_g_mix, m_conv_w_sc, m_conv_w_lru, m_conv_b_lru, m_w_rg_a, m_b_rg_a, m_w_rg_x, m_b_rg_x,
               m_lru_lambda, m_g_mlp, m_g_final]
    small_v = [v_b_ada, v_g_mix, v_conv_w_sc, v_conv_w_lru, v_conv_b_lru, v_w_rg_a, v_b_rg_a, v_w_rg_x, v_b_rg_x,
               v_lru_lambda, v_g_mlp, v_g_final]
    sd, snm, snv = _adam_small(small_w, small_g, small_m, small_v)
    p_in, ad_out, ad_down, (g_ada, d_ada, nm_ada, nv_ada), sd = lax.optimization_barrier(
        (p_in, ad_out, ad_down, (g_ada, d_ada, nm_ada, nv_ada), sd))
    ad_in = _sum4_adam(own_in, p_in, w_in[0].T, m_w_in[0].T, v_w_in[0].T, own_in.shape[0], "adam_w_in", False)
    ad_in = [a.T for a in ad_in]

    def order(ada, w_in_, w_out_, w_up_, w_down_, sm):
        return [ada[None], sm[0], sm[1], w_in_[None], sm[2], sm[3], sm[4], sm[5], sm[6], sm[7], sm[8], sm[9],
                w_out_[None], sm[10], w_up_[None], w_down_[None], sm[11]]

    grads = order(g_ada, ad_in[0], ad_out[0], ad_up[0], ad_down[0], small_g)
    deltas = order(d_ada, ad_in[1], ad_out[1], ad_up[1], ad_down[1], sd)
    new_m = order(nm_ada, ad_in[2], ad_out[2], ad_up[2], ad_down[2], snm)
    new_v = order(nv_ada, ad_in[3], ad_out[3], ad_up[3], ad_down[3], snv)
    return (loss, grad_x[None], *grads, *deltas, *new_m, *new_v)
```

```python
import jax
import jax.numpy as jnp
from jax import lax
from jax.experimental import pallas as pl
from jax.experimental.pallas import tpu as pltpu
from jax.experimental.pallas import tpu_sc as plsc

F32 = jnp.float32
BF16 = jnp.bfloat16
N_DEV = 8
EPS = 1e-6
RG_C = 8.0
GELU_K0 = 0.7978845608028654
GELU_K1 = 0.044715
ADAM_LR = 0.001
ADAM_B1 = 0.9
ADAM_B2 = 0.999
ADAM_EPS = 1e-08
ADAM_WD = 0.01
ADAM_STEP = 10
LANES = 128
SUBLANES = 8
VMEM_LIMIT = 52 * 1024 * 1024
VMEM_LIMIT_BIG = 58 * 1024 * 1024
MIX_ROWS = 256
SMALL_ROWS = 80

MESH = pl.DeviceIdType.MESH
ANY = pl.BlockSpec(memory_space=pl.ANY)
NN = ((1,), (0,))
NT = ((1,), (1,))
TN = ((0,), (0,))


def _dot(a, b, dims):
    return lax.dot_general(a, b, (dims, ((), ())), preferred_element_type=F32)


def _params(sem=None):
    return pltpu.CompilerParams(dimension_semantics=sem, vmem_limit_bytes=VMEM_LIMIT)


def _full(shape):
    nd = len(shape)
    return pl.BlockSpec(shape, lambda *_: (0,) * nd)


def _exchange(name, gathers, scatters):
    n_g = len(gathers)
    arrs = list(gathers) + list(scatters)
    n = len(arrs)
    out_shape = [jax.ShapeDtypeStruct((N_DEV,) + a.shape, a.dtype) for a in gathers]
    out_shape += [jax.ShapeDtypeStruct(a.shape, a.dtype) for a in scatters]

    def body(*refs):
        ins, outs = refs[:n], refs[n:2 * n]
        send_sems, recv_sems, local_sems = refs[2 * n:]
        x, y, c = lax.axis_index("x"), lax.axis_index("y"), lax.axis_index("c")
        me = 4 * x + 2 * y + c

        def src(a, dev):
            return ins[a] if a < n_g else ins[a].at[dev]

        def peer_of(k):
            px = 1 - x if (k >> 2) & 1 else x
            py = 1 - y if (k >> 1) & 1 else y
            pc = 1 - c if k & 1 else c
            return (px, py, pc), 4 * px + 2 * py + pc

        local = [pltpu.make_async_copy(src(a, me), outs[a].at[me], local_sems.at[a]) for a in range(n)]
        for cp in local:
            cp.start()
        sends = []
        for k in range(1, N_DEV):
            peer, pidx = peer_of(k)
            for a in range(n):
                cp = pltpu.make_async_remote_copy(
                    src_ref=src(a, pidx), dst_ref=outs[a].at[me],
                    send_sem=send_sems.at[a * (N_DEV - 1) + k - 1], recv_sem=recv_sems.at[a * (N_DEV - 1) + k - 1],
                    device_id=peer, device_id_type=MESH)
                cp.start()
                sends.append(cp)
        for k in range(1, N_DEV):
            peer, pidx = peer_of(k)
            for a in range(n):
                pltpu.make_async_remote_copy(
                    src_ref=src(a, pidx), dst_ref=outs[a].at[pidx],
                    send_sem=send_sems.at[a * (N_DEV - 1) + k - 1], recv_sem=recv_sems.at[a * (N_DEV - 1) + k - 1],
                    device_id=peer, device_id_type=MESH).wait_recv()
        for cp in sends:
            cp.wait_send()
        for cp in local:
            cp.wait()

    return pl.pallas_call(
        body, name=name, out_shape=out_shape,
        in_specs=[ANY] * n, out_specs=[ANY] * n,
        scratch_shapes=[pltpu.SemaphoreType.DMA((n * (N_DEV - 1),)),
                        pltpu.SemaphoreType.DMA((n * (N_DEV - 1),)),
                        pltpu.SemaphoreType.DMA((n,))],
    )(*arrs)


GATHER_SEMS = 7


def _gather_copies(ins, outs, send_sems, recv_sems, local_sems, x, y, c):
    n = len(ins)
    per = GATHER_SEMS
    sib = (x, y, 1 - c)
    xn, yn, dg = (1 - x, y), (x, 1 - y), (1 - x, 1 - y)
    fx, fy = x + (1 - c) * (1 - 2 * x), y + c * (1 - 2 * y)
    tx, ty = x + c * (1 - 2 * x), y + (1 - c) * (1 - 2 * y)

    def slot(a, px, py, pc):
        return outs[a].at[4 * px + 2 * py + pc]

    def copy(a, k, block, to, src=None):
        return pltpu.make_async_remote_copy(
            src_ref=slot(a, *block) if src is None else src, dst_ref=slot(a, *block),
            send_sem=send_sems.at[a * per + k], recv_sem=recv_sems.at[a * per + k],
            device_id=to, device_id_type=MESH)

    local = [pltpu.make_async_copy(ins[a], slot(a, x, y, c), local_sems.at[a]) for a in range(n)]
    for cp in local:
        cp.start()
    started = []
    for a in range(n):
        started += [copy(a, 1, (x, y, c), (*xn, c), src=ins[a]), copy(a, 2, (x, y, c), (*yn, c), src=ins[a])]
    for a in range(n):
        started.append(copy(a, 0, (x, y, c), sib, src=ins[a]))
    for cp in started:
        cp.start()
    for a in range(n):
        copy(a, 1, (*xn, c), (x, y, c)).wait_recv()
        copy(a, 2, (*yn, c), (x, y, c)).wait_recv()
        later = [copy(a, 3, (fx, fy, c), (tx, ty, c)), copy(a, 4, (*xn, c), sib), copy(a, 5, (*yn, c), sib)]
        for cp in later:
            cp.start()
        started += later
    for a in range(n):
        copy(a, 3, (*dg, c), (x, y, c)).wait_recv()
        cp = copy(a, 6, (*dg, c), sib)
        cp.start()
        started.append(cp)
    for a in range(n):
        copy(a, 0, sib, (x, y, c)).wait_recv()
        for k, chip in ((4, xn), (5, yn), (6, dg)):
            copy(a, k, (*chip, 1 - c), (x, y, c)).wait_recv()
    for cp in started:
        cp.wait_send()
    for cp in local:
        cp.wait()


def _gather2(name, arrs):
    n = len(arrs)
    per = GATHER_SEMS
    out_shape = [jax.ShapeDtypeStruct((N_DEV,) + a.shape, a.dtype) for a in arrs]

    def body(*refs):
        ins, outs = refs[:n], refs[n:2 * n]
        send_sems, recv_sems, local_sems = refs[2 * n:]
        x, y, c = lax.axis_index("x"), lax.axis_index("y"), lax.axis_index("c")
        _gather_copies(ins, outs, send_sems, recv_sems, local_sems, x, y, c)

    return pl.pallas_call(
        body, name=name, out_shape=out_shape,
        in_specs=[ANY] * n, out_specs=[ANY] * n,
        scratch_shapes=[pltpu.SemaphoreType.DMA((n * per,)), pltpu.SemaphoreType.DMA((n * per,)),
                        pltpu.SemaphoreType.DMA((n,))],
    )(*arrs)


def _seq_gather2(name, collective_id, arrs):
    n = len(arrs)
    per = GATHER_SEMS

    def body(*refs):
        ins, outs = refs[:n], refs[n:2 * n]
        send_sems, recv_sems, local_sems = refs[2 * n:]
        x, y, c = lax.axis_index("x"), lax.axis_index("y"), lax.axis_index("c")
        barrier = pltpu.get_barrier_semaphore()
        for peer in [(x, y, 1 - c), (1 - x, y, c), (x, 1 - y, c)]:
            pl.semaphore_signal(barrier, inc=1, device_id=peer, device_id_type=MESH)
        pl.semaphore_wait(barrier, 3)
        _gather_copies(ins, outs, send_sems, recv_sems, local_sems, x, y, c)

    return pl.kernel(
        body, out_type=[jax.ShapeDtypeStruct((N_DEV,) + a.shape, a.dtype) for a in arrs],
        mesh=plsc.ScalarSubcoreMesh(axis_name="seq", num_cores=1),
        scratch_types=[pltpu.SemaphoreType.DMA((n * per,)), pltpu.SemaphoreType.DMA((n * per,)),
                       pltpu.SemaphoreType.DMA((n,))],
        compiler_params=pltpu.CompilerParams(collective_id=collective_id), name=name,
    )(*arrs)


def _seq_chip_exchange(name, collective_id, arrs):
    n = len(arrs)

    def body(*refs):
        ins, outs = refs[:n], refs[n:2 * n]
        send_sems, recv_sems = refs[2 * n:]
        x, y, c = lax.axis_index("x"), lax.axis_index("y"), lax.axis_index("c")

        def peer(k):
            return (1 - x if (k >> 1) & 1 else x), (1 - y if k & 1 else y)

        barrier = pltpu.get_barrier_semaphore()
        for k in (1, 2, 3):
            pl.semaphore_signal(barrier, inc=1, device_id=(*peer(k), c), device_id_type=MESH)
        pl.semaphore_wait(barrier, 3)

        def copy(a, k):
            px, py = peer(k)
            return pltpu.make_async_remote_copy(
                src_ref=ins[a].at[2 * px + py], dst_ref=outs[a].at[k - 1],
                send_sem=send_sems.at[a * 3 + k - 1], recv_sem=recv_sems.at[a * 3 + k - 1],
                device_id=(px, py, c), device_id_type=MESH)

        cps = [copy(a, k) for a in range(n) for k in (1, 2, 3)]
        for cp in cps:
            cp.start()
        for cp in cps:
            cp.wait_recv()
        for cp in cps:
            cp.wait_send()

    return pl.kernel(
        body, out_type=[jax.ShapeDtypeStruct((3,) + a.shape[1:], a.dtype) for a in arrs],
        mesh=plsc.ScalarSubcoreMesh(axis_name="seq", num_cores=1),
        scratch_types=[pltpu.SemaphoreType.DMA((n * 3,)), pltpu.SemaphoreType.DMA((n * 3,))],
        compiler_params=pltpu.CompilerParams(collective_id=collective_id), name=name,
    )(*arrs)


def _seq_pair_swap(name, collective_id, arrs):
    n = len(arrs)

    def body(*refs):
        ins, outs = refs[:n], refs[n:2 * n]
        send_sems, recv_sems = refs[2 * n:]
        x, y, c = lax.axis_index("x"), lax.axis_index("y"), lax.axis_index("c")
        barrier = pltpu.get_barrier_semaphore()
        pl.semaphore_signal(barrier, inc=1, device_id=(x, y, 1 - c), device_id_type=MESH)
        pl.semaphore_wait(barrier, 1)

        def copy(a, q):
            return pltpu.make_async_remote_copy(
                src_ref=ins[a].at[q, 1 - c], dst_ref=outs[a].at[q],
                send_sem=send_sems.at[a * 4 + q], recv_sem=recv_sems.at[a * 4 + q],
                device_id=(x, y, 1 - c), device_id_type=MESH)

        cps = [copy(a, q) for a in range(n) for q in range(4)]
        for cp in cps:
            cp.start()
        for cp in cps:
            cp.wait_recv()
        for cp in cps:
            cp.wait_send()

    return pl.kernel(
        body, out_type=[jax.ShapeDtypeStruct((4,) + a.shape[2:], a.dtype) for a in arrs],
        mesh=plsc.ScalarSubcoreMesh(axis_name="seq", num_cores=1),
        scratch_types=[pltpu.SemaphoreType.DMA((n * 4,)), pltpu.SemaphoreType.DMA((n * 4,))],
        compiler_params=pltpu.CompilerParams(collective_id=collective_id), name=name,
    )(*arrs)


def _call(body, name, grid, in_specs, out_specs, out_shape, args, scratch=()):
    return pl.pallas_call(
        body, name=name, grid=grid, in_specs=in_specs, out_specs=out_specs, out_shape=out_shape,
        scratch_shapes=list(scratch), compiler_params=_params(("arbitrary",) * len(grid)))(*args)


def _ada_fwd(c_all, w_ada_sh, b_ada_sh):
    nb, d = c_all.shape
    ncol = w_ada_sh.shape[1]

    def body(c_ref, w_ref, b_ref, mod_ref, cact_ref):
        cc = c_ref[...]
        ca = cc * jax.nn.sigmoid(cc)
        cact_ref[...] = ca
        mod_ref[...] = _dot(ca.astype(BF16), w_ref[...].astype(BF16), NN) + b_ref[...]

    return pl.pallas_call(
        body, name="ada_fwd",
        out_shape=[jax.ShapeDtypeStruct((nb, ncol), F32), jax.ShapeDtypeStruct((nb, d), F32)],
        compiler_params=_params(),
    )(c_all, w_ada_sh, b_ada_sh)


def _rms(xv):
    rstd = lax.rsqrt(jnp.mean(xv * xv, axis=-1, keepdims=True) + EPS)
    return xv * rstd, rstd


def _rms_bwd(dxhat, xhat, rstd):
    return rstd * (dxhat - xhat * jnp.mean(dxhat * xhat, axis=-1, keepdims=True))


def _colsum(v):
    return jnp.sum(v, axis=0, keepdims=True)


def _expm1(v, ev):
    series = v * (1.0 + v * (0.5 + v * (1.0 / 6.0 + v * (1.0 / 24.0 + v * (1.0 / 120.0)))))
    return jnp.where(jnp.abs(v) < 0.2, series, ev - 1.0)


def _softplus(v):
    return jnp.maximum(v, 0.0) + jnp.log1p(jnp.exp(-jnp.abs(v)))


def _gelu(v):
    t = jnp.tanh(v * (GELU_K0 + (GELU_K0 * GELU_K1) * (v * v)))
    return 0.5 * v * (1.0 + t), t


def _dgelu(v, t):
    return 0.5 * ((1.0 + t) + (v * (1.0 - t * t)) * (GELU_K0 + (3.0 * GELU_K0 * GELU_K1) * (v * v)))


def _scan_tile(a, b, x0, st, k0, reverse):
    t = a.shape[0]
    off = SUBLANES
    stage_a, stage_b = st.at[k0], st.at[k0 + 1]
    halo = slice(off + t, off + t + SUBLANES) if reverse else slice(0, SUBLANES)
    stage_a[halo, :] = jnp.ones((SUBLANES, a.shape[1]), F32)
    stage_b[halo, :] = jnp.zeros((SUBLANES, a.shape[1]), F32)
    s = 1
    while s < min(t, SUBLANES):
        stage_a[off:off + t, :] = a
        stage_b[off:off + t, :] = b
        at = off + s if reverse else off - s
        b = a * stage_b[at:at + t, :] + b
        a = a * stage_a[at:at + t, :]
        s *= 2
    while s < t:
        if reverse:
            b = jnp.concatenate([a[:t - s] * b[s:] + b[:t - s], b[t - s:]], axis=0)
            a = jnp.concatenate([a[:t - s] * a[s:], a[t - s:]], axis=0)
        else:
            b = jnp.concatenate([b[:s], a[s:] * b[:t - s] + b[s:]], axis=0)
            a = jnp.concatenate([a[:s], a[s:] * a[:t - s]], axis=0)
        s *= 2
    x = b + a * x0
    return x, (x[0:SUBLANES, :] if reverse else x[t - SUBLANES:t, :])


def _lru_gates(u, wa, wx, ba, bx, sp):
    ub = u.astype(BF16)
    r = jax.nn.sigmoid(_dot(ub, wa, NN) + ba)
    i = jax.nn.sigmoid(_dot(ub, wx, NN) + bx)
    log_a = (-RG_C * r) * sp
    a = jnp.exp(log_a)
    mult = jnp.sqrt(-_expm1(log_a, a) * (a + 1.0))
    return ub, r, i, a, mult


def _staged_shifts(stage, v, prev8, next8, downs, ups):
    t = v.shape[0]
    if prev8 is not None:
        stage[0:SUBLANES, :] = prev8
    stage[SUBLANES:SUBLANES + t, :] = v
    if next8 is not None:
        stage[SUBLANES + t:2 * SUBLANES + t, :] = next8
    return ([stage[SUBLANES - k:SUBLANES - k + t, :] for k in downs],
            [stage[SUBLANES + k:SUBLANES + k + t, :] for k in ups])


def _conv3(p, pp, w_ref, lo, stage):
    (p1, p2), _ = _staged_shifts(stage, p, pp, None, (1, 2), ())
    q = (w_ref[0:1, lo:lo + LANES] * p2 + w_ref[1:2, lo:lo + LANES] * p1) + w_ref[2:3, lo:lo + LANES] * p
    return q, p1, p2


def _conv4(xv, xp, w_ref, b_ref, lo, stage):
    (x1, x2, x3), _ = _staged_shifts(stage, xv, xp, None, (1, 2, 3), ())
    u = (((w_ref[0:1, lo:lo + LANES] * x3 + w_ref[1:2, lo:lo + LANES] * x2) + w_ref[2:3, lo:lo + LANES] * x1)
         + w_ref[3:4, lo:lo + LANES] * xv) + b_ref[:, lo:lo + LANES]
    return u, x1, x2, x3


def _mix_in_mixer_fwd(x2d, mod6, g_mix, w_in_t, conv_sc, conv_lru, conv_b, wa_bd, wx_bd, ba, bx, lam, width, tm):
    s, d = x2d.shape
    din = w_in_t.shape[0]
    nt = s // tm
    sub = min(MIX_ROWS, tm)
    nblk = width // LANES

    def body(x_ref, mod_ref, g_ref, w_ref, wsc_ref, wlru_ref, blru_ref, wa_ref, wx_ref, ba_ref, bx_ref, lam_ref,
             hn_ref, proj_ref, ymix_ref, h_ref, buf_ref, halo_ref, hc_ref, stage_ref):
        i = pl.program_id(0)

        @pl.when(i == 0)
        def _():
            buf_ref[1] = jnp.zeros((tm, din), F32)
            halo_ref[...] = jnp.zeros_like(halo_ref)

        @pl.when(i <= 1)
        def _():
            hc_ref[...] = jnp.zeros_like(hc_ref)

        def step(dst, src):
            xhat, _ = _rms(x_ref[...])
            hn = ((xhat * g_ref[...]) * (1.0 + mod_ref[1:2, :]) + mod_ref[0:1, :]).astype(BF16)
            hn_ref[...] = hn
            n_mix = (tm // sub) * nblk
            n_chunk = din // width

            def project(k):
                res = _dot(hn_ref[...], w_ref[k * width:(k + 1) * width, :], NT)
                proj_ref[:, k * width:(k + 1) * width] = res
                dst[:, k * width:(k + 1) * width] = res

            done = 0
            for half in range(tm // sub):
                r0 = half * sub
                rows = slice(r0, r0 + sub)
                for j in range(nblk):
                    lo = j * LANES
                    while done < n_chunk and done * n_mix <= (half * nblk + j) * n_chunk:
                        project(done)
                        done += 1

                    def col(p):
                        return src[rows, p * width + lo:p * width + lo + LANES]

                    def prev(p):
                        c0 = p * width + lo
                        if half == 0:
                            return halo_ref[:, c0:c0 + LANES]
                        return src[r0 - SUBLANES:r0, c0:c0 + LANES]

                    pp = col(1) * col(2)
                    q, _, _ = _conv3(pp, prev(1) * prev(2), wsc_ref, lo, stage_ref.at[0])
                    ymix_ref[rows, lo:lo + LANES] = (col(0) * q).astype(BF16)

                    u, _, _, _ = _conv4(col(4), prev(4), wlru_ref, blru_ref, lo, stage_ref.at[1])
                    sp = _softplus(-lam_ref[:, lo:lo + LANES])
                    _, r, ig, a, mult = _lru_gates(u, wa_ref[j], wx_ref[j], ba_ref[:, lo:lo + LANES],
                                                   bx_ref[:, lo:lo + LANES], sp)
                    h, ends = _scan_tile(a, mult * (ig * u), hc_ref[0:1, lo:lo + LANES], stage_ref, 2, False)
                    h_ref[rows, lo:lo + LANES] = h
                    hc_ref[0:1, lo:lo + LANES] = ends[SUBLANES - 1:SUBLANES, :]
                    gel, _ = _gelu(col(3))
                    ymix_ref[rows, width + lo:width + lo + LANES] = (gel * h).astype(BF16)
            while done < n_chunk:
                project(done)
                done += 1
            halo_ref[...] = src[tm - SUBLANES:tm, :]

        @pl.when(i % 2 == 0)
        def _():
            step(buf_ref.at[0], buf_ref.at[1])

        @pl.when(i % 2 == 1)
        def _():
            step(buf_ref.at[1], buf_ref.at[0])

    small = [conv_sc, conv_lru, conv_b, wa_bd, wx_bd, ba, bx, lam]
    cur = lambda i: (jnp.minimum(i, nt - 1), 0)
    last = lambda i: (jnp.maximum(i - 1, 0), 0)
    outs = _call(
        body, "mix_in_mixer_fwd", (nt + 1,),
        [pl.BlockSpec((tm, d), cur), _full(mod6.shape), _full(g_mix.shape), _full(w_in_t.shape)]
        + [_full(a.shape) for a in small],
        [pl.BlockSpec((tm, d), cur), pl.BlockSpec((tm, din), cur),
         pl.BlockSpec((tm, 2 * width), last), pl.BlockSpec((tm, width), last)],
        [jax.ShapeDtypeStruct((s, d), BF16), jax.ShapeDtypeStruct((s, din), F32),
         jax.ShapeDtypeStruct((s, 2 * width), BF16), jax.ShapeDtypeStruct((s, width), F32)],
        [x2d, mod6, g_mix, w_in_t, *small],
        scratch=[pltpu.VMEM((2, tm, din), F32), pltpu.VMEM((SUBLANES, din), F32), pltpu.VMEM((SUBLANES, width), F32),
                 pltpu.VMEM((4, sub + 2 * SUBLANES, LANES), F32)])
    return outs


def _mix_out_fwd(ymix, x2d, w_out, mod6, g_mlp, tm):
    s, d = x2d.shape

    def body(y_ref, x_ref, w_ref, mod_ref, g_ref, mix_ref, x2_ref, hn_ref):
        mix = _dot(y_ref[...], w_ref[...], NN)
        mix_ref[...] = mix.astype(BF16)
        x2 = x_ref[...] + mod_ref[2:3, :] * mix
        x2_ref[...] = x2
        xhat, _ = _rms(x2)
        hn_ref[...] = ((xhat * g_ref[...]) * (1.0 + mod_ref[4:5, :]) + mod_ref[3:4, :]).astype(BF16)

    tile = pl.BlockSpec((tm, d), lambda i: (i, 0))
    return _call(
        body, "mix_out_fwd", (s // tm,),
        [tile, tile, _full(w_out.shape), _full(mod6.shape), _full(g_mlp.shape)],
        [tile, tile, tile],
        [jax.ShapeDtypeStruct((s, d), BF16), jax.ShapeDtypeStruct((s, d), F32), jax.ShapeDtypeStruct((s, d), BF16)],
        [ymix, x2d, w_out, mod6, g_mlp])


def _mlp_fwd_loss(hn2, w_up_t, w_down, x2, target, mod6, g_final, tm, tk):
    s, d = hn2.shape
    f = w_up_t.shape[0]
    nk = f // tk

    def body(hn_ref, wu_ref, wd_ref, x2_hbm, t_hbm, mod_ref, g_ref, z_ref, dx3_ref, dyb_ref, st_ref,
             y_ref, x2_ref, t_ref, sems):
        i, k = pl.program_id(0), pl.program_id(1)

        def fetch():
            rows = pl.ds(pl.multiple_of(i * tm, tm), tm)
            return (pltpu.make_async_copy(x2_hbm.at[rows, :], x2_ref, sems.at[0]),
                    pltpu.make_async_copy(t_hbm.at[rows, :], t_ref, sems.at[1]))

        @pl.when(jnp.logical_and(i == 0, k == 0))
        def _():
            st_ref[...] = jnp.zeros_like(st_ref)

        @pl.when(k == 0)
        def _():
            for cp in fetch():
                cp.start()
            y_ref[...] = jnp.zeros_like(y_ref)

        z = jnp.maximum(_dot(hn_ref[...], wu_ref[...], NT), 0.0)
        z_ref[...] = z.astype(BF16)
        y_ref[...] += _dot((z * z).astype(BF16), wd_ref[...], NN)

        @pl.when(k == nk - 1)
        def _():
            for cp in fetch():
                cp.wait()
            gate = mod_ref[5:6, :]
            yv = y_ref[...]
            xhat, rstd = _rms(x2_ref[...] + gate * yv)
            diff = xhat * g_ref[...] - t_ref[...]
            dyo = diff * (1.0 / d)
            dx3 = _rms_bwd(dyo * g_ref[...], xhat, rstd)
            dx3_ref[...] = dx3
            dyb_ref[...] = (gate * dx3).astype(BF16)
            st_ref[0:1, :] += _colsum(dyo * xhat)
            st_ref[1:2, :] += _colsum(dx3 * yv)
            st_ref[2:3, :] += _colsum(diff * diff)

    tile = pl.BlockSpec((tm, d), lambda i, k: (i, 0))
    wblk = pl.BlockSpec((tk, d), lambda i, k: (k, 0))
    return pl.pallas_call(
        body, name="mlp_fwd_loss", grid=(s // tm, nk),
        in_specs=[tile, wblk, wblk, ANY, ANY, _full(mod6.shape), _full(g_final.shape)],
        out_specs=[pl.BlockSpec((tm, tk), lambda i, k: (i, k)), tile, tile, _full((SUBLANES, d))],
        out_shape=[jax.ShapeDtypeStruct((s, f), BF16), jax.ShapeDtypeStruct((s, d), F32),
                   jax.ShapeDtypeStruct((s, d), BF16), jax.ShapeDtypeStruct((SUBLANES, d), F32)],
        scratch_shapes=[pltpu.VMEM((tm, d), F32), pltpu.VMEM((tm, d), F32), pltpu.VMEM((tm, d), F32),
                        pltpu.SemaphoreType.DMA((2,))],
        compiler_params=pltpu.CompilerParams(dimension_semantics=("arbitrary", "arbitrary"),
                                             vmem_limit_bytes=VMEM_LIMIT_BIG),
    )(hn2, w_up_t, w_down, x2, target, mod6, g_final)


def _mlp_bwd_dx(dyb, z, w_down, w_up_t, tm, tk):
    s, d = dyb.shape
    f = z.shape[1]

    nk = f // tk

    def body(dy_ref, z_ref, wd_ref, wu_ref, dz_ref, dh_ref, acc_ref):
        k = pl.program_id(1)

        @pl.when(k == 0)
        def _():
            acc_ref[...] = jnp.zeros_like(acc_ref)

        dz = ((2.0 * z_ref[...].astype(F32)) * _dot(dy_ref[...], wd_ref[...], NT)).astype(BF16)
        dz_ref[...] = dz
        acc_ref[...] += _dot(dz, wu_ref[...], NN)

        @pl.when(k == nk - 1)
        def _():
            dh_ref[...] = acc_ref[...].astype(BF16)

    return pl.pallas_call(
        body, name="mlp_bwd_dx", grid=(s // tm, nk),
        in_specs=[pl.BlockSpec((tm, d), lambda i, k: (i, 0)), pl.BlockSpec((tm, tk), lambda i, k: (i, k)),
                  pl.BlockSpec((tk, d), lambda i, k: (k, 0)), pl.BlockSpec((tk, d), lambda i, k: (k, 0))],
        out_specs=[pl.BlockSpec((tm, tk), lambda i, k: (i, k)), pl.BlockSpec((tm, d), lambda i, k: (i, 0))],
        out_shape=[jax.ShapeDtypeStruct((s, f), BF16), jax.ShapeDtypeStruct((s, d), BF16)],
        scratch_shapes=[pltpu.VMEM((tm, d), F32)],
        compiler_params=_params(("parallel", "arbitrary")),
    )(dyb, z, w_down, w_up_t)


def _mlp_bwd_dw(z, dz, dyb, hn2, tm, tk):
    s, d = dyb.shape
    f = z.shape[1]

    def body(z_ref, dz_ref, dy_ref, hn_ref, gd_ref, gu_ref):
        i = pl.program_id(1)

        @pl.when(i == 0)
        def _():
            gd_ref[...] = jnp.zeros_like(gd_ref)
            gu_ref[...] = jnp.zeros_like(gu_ref)

        zf = z_ref[...].astype(F32)
        gd_ref[...] += _dot((zf * zf).astype(BF16), dy_ref[...], TN)
        gu_ref[...] += _dot(dz_ref[...], hn_ref[...], TN)

    return pl.pallas_call(
        body, name="mlp_bwd_dw", grid=(f // tk, s // tm),
        in_specs=[pl.BlockSpec((tm, tk), lambda k, i: (i, k)), pl.BlockSpec((tm, tk), lambda k, i: (i, k)),
                  pl.BlockSpec((tm, d), lambda k, i: (i, 0)), pl.BlockSpec((tm, d), lambda k, i: (i, 0))],
        out_specs=[pl.BlockSpec((tk, d), lambda k, i: (k, 0)), pl.BlockSpec((tk, d), lambda k, i: (k, 0))],
        out_shape=[jax.ShapeDtypeStruct((f, d), F32), jax.ShapeDtypeStruct((f, d), F32)],
        compiler_params=_params(("parallel", "arbitrary")),
    )(z, dz, dyb, hn2)


def _mix_out_bwd(dhn2, x2, dx3, mix, ymix, w_out, mod6, g_mlp, tm):
    s, d = x2.shape

    def body(dh_ref, x2_ref, dx3_ref, mix_ref, y_ref, w_ref, mod_ref, g_ref, dx2_ref, dym_ref, gw_ref, st_ref):
        i = pl.program_id(0)

        @pl.when(i == 0)
        def _():
            st_ref[...] = jnp.zeros_like(st_ref)
            gw_ref[...] = jnp.zeros_like(gw_ref)

        dh = dh_ref[...].astype(F32)
        xhat, rstd = _rms(x2_ref[...])
        dn = dh * (1.0 + mod_ref[4:5, :])
        dx2 = dx3_ref[...] + _rms_bwd(dn * g_ref[...], xhat, rstd)
        dx2_ref[...] = dx2
        st_ref[0:1, :] += _colsum(dh)
        st_ref[1:2, :] += _colsum(dh * (xhat * g_ref[...]))
        st_ref[2:3, :] += _colsum(dn * xhat)
        st_ref[3:4, :] += _colsum(dx2 * mix_ref[...].astype(F32))
        dmix = (mod_ref[2:3, :] * dx2).astype(BF16)
        dym_ref[...] = _dot(dmix, w_ref[...], NT).astype(BF16)
        gw_ref[...] += _dot(y_ref[...], dmix, TN)

    tile = pl.BlockSpec((tm, d), lambda i: (i, 0))
    return _call(
        body, "mix_out_bwd", (s // tm,),
        [tile, tile, tile, tile, tile, _full(w_out.shape), _full(mod6.shape), _full(g_mlp.shape)],
        [tile, tile, _full((d, d)), _full((SUBLANES, d))],
        [jax.ShapeDtypeStruct((s, d), F32), jax.ShapeDtypeStruct((s, d), BF16),
         jax.ShapeDtypeStruct((d, d), F32), jax.ShapeDtypeStruct((SUBLANES, d), F32)],
        [dhn2, x2, dx3, mix, ymix, w_out, mod6, g_mlp])


def _mixer_bwd(proj, dymix, h_all, conv_sc, conv_lru, conv_b, wa_bd, wx_bd, ba, bx, lam, width):
    s, din = proj.shape
    t = min(MIX_ROWS, s)
    nt = s // t
    nblk = width // LANES
    hb = t // SUBLANES
    last8 = s // SUBLANES - 1

    def body(proj_ref, projp_ref, projn_ref, dy_ref, dyn_ref, h_ref, hp_ref,
             wsc_ref, wlru_ref, blru_ref, wa_ref, wx_ref, ba_ref, bx_ref, lam_ref,
             dproj_ref, small_ref, gwa_ref, gwx_ref, an_ref, gn_ref, dun_ref, stage_ref):
        i = pl.program_id(0)

        @pl.when(i == 0)
        def _():
            small_ref[...] = jnp.zeros_like(small_ref)
            gwa_ref[...] = jnp.zeros_like(gwa_ref)
            gwx_ref[...] = jnp.zeros_like(gwx_ref)
            an_ref[...] = jnp.zeros_like(an_ref)
            gn_ref[...] = jnp.zeros_like(gn_ref)
            dun_ref[...] = jnp.zeros_like(dun_ref)

        has_prev = i < nt - 1
        has_next = i > 0
        for j in range(nblk):
            lo = j * LANES
            ls = slice(lo, lo + LANES)

            def col(p, ref=proj_ref):
                return ref[:, p * width + lo:p * width + lo + LANES]

            def prev(p):
                return jnp.where(has_prev, col(p, projp_ref), 0.0)

            def nxt(p):
                return jnp.where(has_next, col(p, projn_ref), 0.0)

            def add_row(r, v):
                small_ref[r:r + 1, ls] += _colsum(v)

            sc_b, sc_c, sc_x = col(0), col(1), col(2)
            p = sc_c * sc_x
            q, p1, p2 = _conv3(p, prev(1) * prev(2), wsc_ref, lo, stage_ref.at[0])
            dys = dy_ref[:, ls].astype(F32)
            dproj_ref[:, ls] = (dys * q).astype(BF16)
            dq = dys * sc_b
            dqn = jnp.where(has_next, dyn_ref[:, ls].astype(F32)[0:SUBLANES], 0.0) * nxt(0)
            _, (dq1, dq2) = _staged_shifts(stage_ref.at[1], dq, None, dqn, (), (1, 2))
            dp = (wsc_ref[2:3, ls] * dq + wsc_ref[1:2, ls] * dq1) + wsc_ref[0:1, ls] * dq2
            dproj_ref[:, width + lo:width + lo + LANES] = (dp * sc_x).astype(BF16)
            dproj_ref[:, 2 * width + lo:2 * width + lo + LANES] = (dp * sc_c).astype(BF16)
            add_row(0, dq * p2)
            add_row(1, dq * p1)
            add_row(2, dq * p)

            xv = col(4)
            u, x1, x2, x3 = _conv4(xv, prev(4), wlru_ref, blru_ref, lo, stage_ref.at[2])
            lam_v = lam_ref[:, ls]
            sp = _softplus(-lam_v)
            wa, wx = wa_ref[j], wx_ref[j]
            ub, r, ig, a, mult = _lru_gates(u, wa, wx, ba_ref[:, ls], bx_ref[:, ls], sp)
            iu = ig * u
            h = h_ref[:, ls]
            (hm1,), _ = _staged_shifts(stage_ref.at[3], h, jnp.where(has_prev, hp_ref[:, ls], 0.0), None, (1,), ())
            lyv = col(3)
            gel, th = _gelu(lyv)
            dyl = dy_ref[:, width + lo:width + lo + LANES].astype(F32)
            dproj_ref[:, 3 * width + lo:3 * width + lo + LANES] = (dyl * h * _dgelu(lyv, th)).astype(BF16)
            a_next = jnp.broadcast_to(an_ref[0:1, ls], (SUBLANES, LANES))
            _, (a_up,) = _staged_shifts(stage_ref.at[4], a, None, a_next, (), (1,))
            g, _ = _scan_tile(a_up, dyl * gel, gn_ref[0:1, ls], stage_ref, 5, True)
            an_ref[0:1, ls] = a[0:1, :]
            gn_ref[0:1, ls] = g[0:1, :]
            da = g * hm1
            dmult = g * iu
            diu = g * mult
            dlog_a = da * a - dmult * ((a * a) / mult)
            dpre_a = (dlog_a * (-RG_C * sp)) * (r * (1.0 - r))
            dpre_x = (diu * u) * (ig * (1.0 - ig))
            dab, dxb = dpre_a.astype(BF16), dpre_x.astype(BF16)
            du = diu * ig + _dot(dab, wa, NT) + _dot(dxb, wx, NT)
            gwa_ref[j] += _dot(ub, dab, TN)
            gwx_ref[j] += _dot(ub, dxb, TN)
            dun = dun_ref[:, ls]
            dun_ref[:, ls] = du[0:SUBLANES, :]
            _, (du1, du2, du3) = _staged_shifts(stage_ref.at[7], du, None, dun, (), (1, 2, 3))
            dlx = (((wlru_ref[3:4, ls] * du + wlru_ref[2:3, ls] * du1) + wlru_ref[1:2, ls] * du2)
                   + wlru_ref[0:1, ls] * du3)
            dproj_ref[:, 4 * width + lo:4 * width + lo + LANES] = dlx.astype(BF16)
            add_row(3, du * x3)
            add_row(4, du * x2)
            add_row(5, du * x1)
            add_row(6, du * xv)
            add_row(7, du)
            add_row(8, dpre_a)
            add_row(9, dpre_x)
            add_row(10, (dlog_a * (RG_C * r)) * jax.nn.sigmoid(-lam_v))

    small = [conv_sc, conv_lru, conv_b, wa_bd, wx_bd, ba, bx, lam]
    rev = lambda i: nt - 1 - i
    return _call(
        body, "mixer_bwd", (nt,),
        [pl.BlockSpec((t, din), lambda i: (rev(i), 0)),
         pl.BlockSpec((SUBLANES, din), lambda i: (jnp.maximum(rev(i) * hb - 1, 0), 0)),
         pl.BlockSpec((SUBLANES, din), lambda i: (jnp.minimum((rev(i) + 1) * hb, last8), 0)),
         pl.BlockSpec((t, 2 * width), lambda i: (rev(i), 0)),
         pl.BlockSpec((2 * SUBLANES, 2 * width), lambda i: (jnp.minimum((rev(i) + 1) * (hb // 2), last8 // 2), 0)),
         pl.BlockSpec((t, width), lambda i: (rev(i), 0)),
         pl.BlockSpec((SUBLANES, width), lambda i: (jnp.maximum(rev(i) * hb - 1, 0), 0))]
        + [_full(a.shape) for a in small],
        [pl.BlockSpec((t, din), lambda i: (rev(i), 0)), _full((2 * SUBLANES, width)),
         _full(wa_bd.shape), _full(wx_bd.shape)],
        [jax.ShapeDtypeStruct((s, din), BF16), jax.ShapeDtypeStruct((2 * SUBLANES, width), F32),
         jax.ShapeDtypeStruct(wa_bd.shape, F32), jax.ShapeDtypeStruct(wx_bd.shape, F32)],
        [proj, proj, proj, dymix, dymix, h_all, h_all, *small],
        scratch=[pltpu.VMEM((SUBLANES, width), F32), pltpu.VMEM((SUBLANES, width), F32),
                 pltpu.VMEM((SUBLANES, width), F32), pltpu.VMEM((8, t + 2 * SUBLANES, LANES), F32)])


def _mix_in_bwd_dx(dproj, x2d, dx2, w_in_t, mod6, g_mix, tm):
    s, d = x2d.shape
    din = dproj.shape[1]

    def body(dp_ref, x_ref, dx2_ref, w_ref, mod_ref, g_ref, gx_ref, st_ref):
        i = pl.program_id(0)

        @pl.when(i == 0)
        def _():
            st_ref[...] = jnp.zeros_like(st_ref)

        dh = _dot(dp_ref[...], w_ref[...], NN)
        xhat, rstd = _rms(x_ref[...])
        dn = dh * (1.0 + mod_ref[1:2, :])
        gx_ref[...] = dx2_ref[...] + _rms_bwd(dn * g_ref[...], xhat, rstd)
        st_ref[0:1, :] += _colsum(dh)
        st_ref[1:2, :] += _colsum(dh * (xhat * g_ref[...]))
        st_ref[2:3, :] += _colsum(dn * xhat)

    tile = pl.BlockSpec((tm, d), lambda i: (i, 0))
    return _call(
        body, "mix_in_bwd_dx", (s // tm,),
        [pl.BlockSpec((tm, din), lambda i: (i, 0)), tile, tile, _full(w_in_t.shape), _full(mod6.shape),
         _full(g_mix.shape)],
        [tile, _full((SUBLANES, d))],
        [jax.ShapeDtypeStruct((s, d), F32), jax.ShapeDtypeStruct((SUBLANES, d), F32)],
        [dproj, x2d, dx2, w_in_t, mod6, g_mix])


def _mix_in_bwd_dw(dproj, hn1, tm, tn):
    s, d = hn1.shape
    din = dproj.shape[1]

    def body(dp_ref, hn_ref, gw_ref):
        i = pl.program_id(1)

        @pl.when(i == 0)
        def _():
            gw_ref[...] = jnp.zeros_like(gw_ref)

        gw_ref[...] += _dot(dp_ref[...], hn_ref[...], TN)

    return _call(
        body, "mix_in_bwd_dw", (din // tn, s // tm),
        [pl.BlockSpec((tm, tn), lambda p, i: (i, p)), pl.BlockSpec((tm, d), lambda p, i: (i, 0))],
        [pl.BlockSpec((tn, d), lambda p, i: (p, 0))],
        [jax.ShapeDtypeStruct((din, d), F32)],
        [dproj, hn1])


def _adamw(w, g, m, v):
    m = ADAM_B1 * m + (1.0 - ADAM_B1) * g
    v = ADAM_B2 * v + (1.0 - ADAM_B2) * (g * g)
    m_hat = m / (1.0 - ADAM_B1 ** ADAM_STEP)
    v_hat = v / (1.0 - ADAM_B2 ** ADAM_STEP)
    delta = -ADAM_LR * (m_hat / (jnp.sqrt(v_hat) + ADAM_EPS) + ADAM_WD * w)
    return delta, m, v


def _pair_sum(g4s, h4s, core_chip, tr, name):
    na = len(g4s)
    _, _, r, n = g4s[0].shape

    def body(sc_ref, *refs):
        q = pl.program_id(1)
        for a in range(na):
            g_ref, h_ref = refs[2 * a], refs[2 * a + 1]
            sb_ref, own_ref = refs[2 * na + 2 * a], refs[2 * na + 2 * a + 1]
            ssum = g_ref[...] + h_ref[...]
            sb_ref[...] = ssum.astype(BF16)

            @pl.when(q == sc_ref[1])
            def _():
                own_ref[...] = ssum

    grid_spec = pltpu.PrefetchScalarGridSpec(
        num_scalar_prefetch=1, grid=(r // tr, 4),
        in_specs=[pl.BlockSpec((None, None, tr, n), lambda i, q, sc: (q, sc[0], i, 0)),
                  pl.BlockSpec((None, tr, n), lambda i, q, sc: (q, i, 0))] * na,
        out_specs=[pl.BlockSpec((None, tr, n), lambda i, q, sc: (q, i, 0)),
                   pl.BlockSpec((tr, n), lambda i, q, sc: (i, 0))] * na)
    outs = pl.pallas_call(
        body, name=name, grid_spec=grid_spec,
        out_shape=[jax.ShapeDtypeStruct((4, r, n), BF16), jax.ShapeDtypeStruct((r, n), F32)] * na,
        compiler_params=_params(("parallel", "arbitrary")),
    )(core_chip, *[x for pair in zip(g4s, h4s) for x in pair])
    return [(outs[2 * a], outs[2 * a + 1]) for a in range(na)]


def _sum4_adam(own, parts, w, m, v, tr, name, transposed):
    r, n = own.shape
    rows, cols = w.shape

    def body(o_ref, p_ref, w_ref, m_ref, v_ref, g_ref, d_ref, nm_ref, nv_ref):
        g = o_ref[...]
        for k in range(3):
            g = g + p_ref[k].astype(F32)
        if transposed:
            g = g.T
        g_ref[...] = g
        d_ref[...], nm_ref[...], nv_ref[...] = _adamw(w_ref[...], g, m_ref[...], v_ref[...])

    if transposed:
        g_specs = [pl.BlockSpec((r, tr), lambda i: (0, i)), pl.BlockSpec((3, r, tr), lambda i: (0, 0, i))]
    else:
        g_specs = [pl.BlockSpec((tr, n), lambda i: (i, 0)), pl.BlockSpec((3, tr, n), lambda i: (0, i, 0))]
    tile = pl.BlockSpec((tr, cols), lambda i: (i, 0))
    return pl.pallas_call(
        body, name=name, grid=(rows // tr,),
        in_specs=g_specs + [tile] * 3, out_specs=[tile] * 4,
        out_shape=[jax.ShapeDtypeStruct((rows, cols), F32)] * 4,
        compiler_params=_params(("parallel",)),
    )(own, parts, w, m, v)


def _sum8(parts, tr, name):
    _, rows, n = parts.shape

    def body(p_ref, o_ref):
        acc = p_ref[0]
        for k in range(1, N_DEV):
            acc = acc + p_ref[k]
        o_ref[...] = acc

    return pl.pallas_call(
        body, name=name, grid=(rows // tr,),
        in_specs=[pl.BlockSpec((N_DEV, tr, n), lambda i: (0, i, 0))],
        out_specs=pl.BlockSpec((tr, n), lambda i: (i, 0)),
        out_shape=jax.ShapeDtypeStruct((rows, n), F32),
        compiler_params=_params(("parallel",)),
    )(parts)


def _ada_bwd_adam(cact_t, dmod_cols, w, m, v, tr):
    rows, n = w.shape

    def body(c_ref, d_ref, w_ref, m_ref, v_ref, g_ref, dl_ref, nm_ref, nv_ref):
        def term(b):
            return c_ref[b].astype(BF16).astype(F32) * d_ref[b:b + 1, :].astype(BF16).astype(F32)

        g = term(0)
        for b in range(1, N_DEV):
            g = g + term(b)
        g_ref[...] = g
        dl_ref[...], nm_ref[...], nv_ref[...] = _adamw(w_ref[...], g, m_ref[...], v_ref[...])

    tile = pl.BlockSpec((tr, n), lambda i: (i, 0))
    return pl.pallas_call(
        body, name="ada_bwd_adam", grid=(rows // tr,),
        in_specs=[pl.BlockSpec((N_DEV, tr, 1), lambda i: (0, i, 0)), _full(dmod_cols.shape), tile, tile, tile],
        out_specs=[tile] * 4,
        out_shape=[jax.ShapeDtypeStruct((rows, n), F32)] * 4,
        compiler_params=_params(("parallel",)),
    )(cact_t, dmod_cols, w, m, v)


def _adam_small(ws, gs, ms, vs):
    n = len(ws)

    def body(*refs):
        w_r, g_r, m_r, v_r = refs[:n], refs[n:2 * n], refs[2 * n:3 * n], refs[3 * n:4 * n]
        d_r, nm_r, nv_r = refs[4 * n:5 * n], refs[5 * n:6 * n], refs[6 * n:7 * n]
        for k in range(n):
            d_r[k][...], nm_r[k][...], nv_r[k][...] = _adamw(w_r[k][...], g_r[k][...], m_r[k][...], v_r[k][...])

    shapes = [jax.ShapeDtypeStruct(w.shape, F32) for w in ws]
    outs = pl.pallas_call(
        body, name="adam_small", out_shape=shapes * 3, compiler_params=_params(),
    )(*ws, *gs, *ms, *vs)
    return outs[:n], outs[n:2 * n], outs[2 * n:]


def _block_diag(w):
    h, hd, _ = w.shape
    per = LANES // hd
    eye = jnp.eye(per, dtype=w.dtype)
    w5 = w.reshape(h // per, per, hd, 1, hd) * eye[None, :, None, :, None]
    return w5.reshape(h // per, LANES, LANES)


def _block_diag_grad(g, h, hd):
    per = LANES // hd
    g5 = g.reshape(h // per, per, hd, per, hd)
    return jnp.stack([g5[:, a, :, a, :] for a in range(per)], axis=1).reshape(h, hd, hd)


def kernel(x, c, w_ada, b_ada, g_mix, w_in, conv_w_sc, conv_w_lru, conv_b_lru, w_rg_a, b_rg_a, w_rg_x, b_rg_x, lru_lambda, w_out, g_mlp, w_up, w_down, g_final, loss_target, m_w_ada, m_b_ada, m_g_mix, m_w_in, m_conv_w_sc, m_conv_w_lru, m_conv_b_lru, m_w_rg_a, m_b_rg_a, m_w_rg_x, m_b_rg_x, m_lru_lambda, m_w_out, m_g_mlp, m_w_up, m_w_down, m_g_final, v_w_ada, v_b_ada, v_g_mix, v_w_in, v_conv_w_sc, v_conv_w_lru, v_conv_b_lru, v_w_rg_a, v_b_rg_a, v_w_rg_x, v_b_rg_x, v_lru_lambda, v_w_out, v_g_mlp, v_w_up, v_w_down, v_g_final):
    s, d = x.shape[1], x.shape[2]
    width = conv_b_lru.shape[1]
    heads, hd = w_rg_a.shape[1], w_rg_a.shape[2]
    f = w_down.shape[1] * N_DEV
    n_ada = w_ada.shape[2]
    csh = conv_w_sc.shape[2]
    me = 4 * lax.axis_index("x") + 2 * lax.axis_index("y") + lax.axis_index("c")
    tm = min(512, s)
    tm_mlp = min(1024, s)
    tk = 512

    x2d = x[0]
    tgt = loss_target[0]

    pay = jnp.zeros((SUBLANES, d), F32)
    pay = pay.at[0:1, :].set(c)
    pay = pay.at[1:4, 0:csh].set(conv_w_sc[0])
    pay = pay.at[4:8, 0:csh].set(conv_w_lru[0])
    w_in_t_sh = w_in[0].T.astype(BF16)
    w_up_t_sh = w_up[0].T.astype(BF16)
    w_out_sh = w_out[0].astype(BF16)
    w_down_sh = w_down[0].astype(BF16)
    pay_all, w_in_t = _gather2("gather_in", [pay, w_in_t_sh])
    w_in_t = w_in_t.reshape(-1, d)
    c_all = pay_all[:, 0, :]
    conv_sc = pay_all[:, 1:4, 0:csh].transpose(1, 0, 2).reshape(3, width)
    conv_lru = pay_all[:, 4:8, 0:csh].transpose(1, 0, 2).reshape(4, width)

    b_ada_sh = lax.dynamic_slice(b_ada, (0, me * n_ada), (1, n_ada))
    mod_cols, c_act = _ada_fwd(c_all, w_ada[0], b_ada_sh)
    (mod_rows,) = _exchange("scatter_mod", [], [mod_cols.reshape(N_DEV, 1, n_ada)])
    mod_rows, w_out_sh, w_up_t_sh, w_down_sh = lax.optimization_barrier((mod_rows, w_out_sh, w_up_t_sh, w_down_sh))
    (w_out_g,) = _seq_gather2("gather_w_out", 1, [w_out_sh])
    w_up_g, w_down_g = _seq_gather2("gather_mlp_weights", 2, [w_up_t_sh, w_down_sh])
    mod6 = jnp.zeros((SUBLANES, d), F32).at[0:6, :].set(mod_rows.reshape(6, d))

    wa_bd = _block_diag(w_rg_a[0]).astype(BF16)
    wx_bd = _block_diag(w_rg_x[0]).astype(BF16)
    ba = b_rg_a.reshape(1, width)
    bx = b_rg_x.reshape(1, width)
    g_fin = g_final.reshape(1, d)

    hn1, proj, ymix, h_all = _mix_in_mixer_fwd(x2d, mod6, g_mix, w_in_t, conv_sc, conv_lru, conv_b_lru,
                                               wa_bd, wx_bd, ba, bx, lru_lambda, width, tm)
    w_out_b = w_out_g.reshape(-1, d)
    mix, x2, hn2 = _mix_out_fwd(ymix, x2d, w_out_b, mod6, g_mlp, tm_mlp)
    w_up_t = w_up_g.reshape(-1, d)
    w_down_b = w_down_g.reshape(-1, d)
    z, dx3, dyb, st_fin = _mlp_fwd_loss(hn2, w_up_t, w_down_b, x2, tgt, mod6, g_fin, tm_mlp, 2 * tk)

    core_chip = jnp.stack([lax.axis_index("c"), 2 * lax.axis_index("x") + lax.axis_index("y")]).astype(jnp.int32)
    dz, dhn2 = _mlp_bwd_dx(dyb, z, w_down_b, w_up_t, tm_mlp, 2 * tk)
    g_down, g_up_t = _mlp_bwd_dw(z, dz, dyb, hn2, tm_mlp, 2 * tk)
    g_up4, g_down4 = g_up_t.reshape(4, 2, -1, d), g_down.reshape(4, 2, -1, d)
    h_up, h_down = _seq_pair_swap("swap_mlp_grads", 7, [g_up4, g_down4])
    dx2, dymix, g_out, st_out = _mix_out_bwd(dhn2, x2, dx3, mix, ymix, w_out_b, mod6, g_mlp, tm)
    h_up, h_down, g_out = lax.optimization_barrier((h_up, h_down, g_out))
    (sb_up, own_up), (sb_down, own_down) = _pair_sum([g_up4, g_down4], [h_up, h_down], core_chip, g_up4.shape[2], "pair_sum_mlp")
    g_out4 = g_out.reshape(4, 2, -1, d)
    (h_out,) = _seq_pair_swap("swap_w_out_grad", 8, [g_out4])
    p_up, p_down = _seq_chip_exchange("exchange_mlp_grads", 3, [sb_up, sb_down])
    dproj, g_small, g_wa, g_wx = _mixer_bwd(
        proj, dymix, h_all, conv_sc, conv_lru, conv_b_lru, wa_bd, wx_bd, ba, bx, lru_lambda, width)
    h_out, dproj = lax.optimization_barrier((h_out, dproj))
    ((sb_out, own_out),) = _pair_sum([g_out4], [h_out], core_chip, g_out4.shape[2], "pair_sum_w_out")
    (p_out,) = _seq_chip_exchange("exchange_w_out_grad", 4, [sb_out])
    grad_x, st_in = _mix_in_bwd_dx(dproj, x2d, dx2, w_in_t, mod6, g_mix, tm)

    small = jnp.concatenate([
        st_in[0:2], st_out[3:4], st_out[0:2], st_fin[1:2],
        st_in[2:3], st_out[2:3], st_fin[0:1],
        jnp.concatenate([g_small[7:8], g_small[10:11]], axis=1),
        jnp.concatenate([g_small[8:9], g_small[9:10]], axis=1),
        jnp.concatenate([jnp.concatenate([g_small[0:3], jnp.zeros((1, width), F32)], axis=0), g_small[3:7]], axis=1),
        st_fin[2:3],
        _block_diag_grad(g_wa, heads, hd).reshape(-1, d),
        _block_diag_grad(g_wx, heads, hd).reshape(-1, d),
    ], axis=0)

    (small_all,) = _seq_gather2("gather_small_grads", 5, [small])
    g_in_t, = _mix_in_bwd_dw(dproj, hn1, min(2048, s), dproj.shape[1] // 2)
    g_in4 = g_in_t.reshape(4, 2, -1, d)
    (h_in,) = _seq_pair_swap("swap_w_in_grad", 9, [g_in4])
    p_up, p_down, p_out, small_all, g_in_t = lax.optimization_barrier((p_up, p_down, p_out, small_all, g_in_t))

    ad_up = _sum4_adam(own_up, p_up, w_up[0], m_w_up[0], v_w_up[0], 256, "adam_w_up", True)
    h_in, ad_up = lax.optimization_barrier((h_in, ad_up))
    ((sb_in, own_in),) = _pair_sum([g_in4], [h_in], core_chip, g_in4.shape[2], "pair_sum_w_in")
    (p_in,) = _seq_chip_exchange("exchange_w_in_grad", 6, [sb_in])
    ad_out = _sum4_adam(own_out, p_out, w_out[0], m_w_out[0], v_w_out[0], w_out.shape[1], "adam_w_out", False)
    ad_down = _sum4_adam(own_down, p_down, w_down[0], m_w_down[0], v_w_down[0], 256, "adam_w_down", False)

    gsum = _sum8(small_all, SMALL_ROWS, "sum_small")
    loss = (0.5 / d) * jnp.sum(gsum[15])
    dmod_cols = lax.dynamic_slice(small_all[:, 0:6, :].reshape(N_DEV, 6 * d), (0, me * n_ada), (N_DEV, n_ada))
    g_ada, d_ada, nm_ada, nv_ada = _ada_bwd_adam(c_act[:, :, None], dmod_cols, w_ada[0], m_w_ada[0], v_w_ada[0], 256)

    g_conv = lax.dynamic_slice(gsum[11:15, 0:width], (0, me * csh), (4, csh))
    g_conv_l = lax.dynamic_slice(gsum[11:15, width:2 * width], (0, me * csh), (4, csh))
    small_g = [
        gsum[0:6].reshape(1, 6 * d),
        gsum[6:7],
        g_conv[0:3].reshape(1, 3, csh),
        g_conv_l.reshape(1, 4, csh),
        gsum[9:10, 0:width],
        gsum[16:48].reshape(1, heads, hd, hd),
        gsum[10:11, 0:width].reshape(1, heads, hd),
        gsum[48:80].reshape(1, heads, hd, hd),
        gsum[10:11, width:].reshape(1, heads, hd),
        gsum[9:10, width:],
        gsum[7:8],
        gsum[8],
    ]
    small_w = [b_ada, g_mix, conv_w_sc, conv_w_lru, conv_b_lru, w_rg_a, b_rg_a, w_rg_x, b_rg_x, lru_lambda, g_mlp, g_final]
    small_m = [m_b_ada, m_g_mix, m_conv_w_sc, m_conv_w_lru, m_conv_b_lru, m_w_rg_a, m_b_rg_a, m_w_rg_x, m_b_rg_x,
               m_lru_lambda, m_g_mlp, m_g_final]
    small_v = [v_b_ada, v_g_mix, v_conv_w_sc, v_conv_w_lru, v_conv_b_lru, v_w_rg_a, v_b_rg_a, v_w_rg_x, v_b_rg_x,
               v_lru_lambda, v_g_mlp, v_g_final]
    sd, snm, snv = _adam_small(small_w, small_g, small_m, small_v)
    p_in, ad_out, ad_down, (g_ada, d_ada, nm_ada, nv_ada), sd = lax.optimization_barrier(
        (p_in, ad_out, ad_down, (g_ada, d_ada, nm_ada, nv_ada), sd))
    ad_in = _sum4_adam(own_in, p_in, w_in[0].T, m_w_in[0].T, v_w_in[0].T, own_in.shape[0], "adam_w_in", False)
    ad_in = [a.T for a in ad_in]

    def order(ada, w_in_, w_out_, w_up_, w_down_, sm):
        return [ada[None], sm[0], sm[1], w_in_[None], sm[2], sm[3], sm[4], sm[5], sm[6], sm[7], sm[8], sm[9],
                w_out_[None], sm[10], w_up_[None], w_down_[None], sm[11]]

    grads = order(g_ada, ad_in[0], ad_out[0], ad_up[0], ad_down[0], small_g)
    deltas = order(d_ada, ad_in[1], ad_out[1], ad_up[1], ad_down[1], sd)
    new_m = order(nm_ada, ad_in[2], ad_out[2], ad_up[2], ad_down[2], snm)
    new_v = order(nv_ada, ad_in[3], ad_out[3], ad_up[3], ad_down[3], snv)
    return (loss, grad_x[None], *grads, *deltas, *new_m, *new_v)
```

```python
import jax
import jax.numpy as jnp
from jax import lax
from jax.experimental import pallas as pl
from jax.experimental.pallas import tpu as pltpu
from jax.experimental.pallas import tpu_sc as plsc

F32 = jnp.float32
BF16 = jnp.bfloat16
N_DEV = 8
EPS = 1e-6
RG_C = 8.0
GELU_K0 = 0.7978845608028654
GELU_K1 = 0.044715
ADAM_LR = 0.001
ADAM_B1 = 0.9
ADAM_B2 = 0.999
ADAM_EPS = 1e-08
ADAM_WD = 0.01
ADAM_STEP = 10
LANES = 128
SUBLANES = 8
VMEM_LIMIT = 52 * 1024 * 1024
VMEM_LIMIT_BIG = 58 * 1024 * 1024
MIX_ROWS = 256
SMALL_ROWS = 80

MESH = pl.DeviceIdType.MESH
ANY = pl.BlockSpec(memory_space=pl.ANY)
NN = ((1,), (0,))
NT = ((1,), (1,))
TN = ((0,), (0,))


def _dot(a, b, dims):
    return lax.dot_general(a, b, (dims, ((), ())), preferred_element_type=F32)


def _params(sem=None):
    return pltpu.CompilerParams(dimension_semantics=sem, vmem_limit_bytes=VMEM_LIMIT)


def _full(shape):
    nd = len(shape)
    return pl.BlockSpec(shape, lambda *_: (0,) * nd)


def _exchange(name, gathers, scatters):
    n_g = len(gathers)
    arrs = list(gathers) + list(scatters)
    n = len(arrs)
    out_shape = [jax.ShapeDtypeStruct((N_DEV,) + a.shape, a.dtype) for a in gathers]
    out_shape += [jax.ShapeDtypeStruct(a.shape, a.dtype) for a in scatters]

    def body(*refs):
        ins, outs = refs[:n], refs[n:2 * n]
        send_sems, recv_sems, local_sems = refs[2 * n:]
        x, y, c = lax.axis_index("x"), lax.axis_index("y"), lax.axis_index("c")
        me = 4 * x + 2 * y + c

        def src(a, dev):
            return ins[a] if a < n_g else ins[a].at[dev]

        def peer_of(k):
            px = 1 - x if (k >> 2) & 1 else x
            py = 1 - y if (k >> 1) & 1 else y
            pc = 1 - c if k & 1 else c
            return (px, py, pc), 4 * px + 2 * py + pc

        local = [pltpu.make_async_copy(src(a, me), outs[a].at[me], local_sems.at[a]) for a in range(n)]
        for cp in local:
            cp.start()
        sends = []
        for k in range(1, N_DEV):
            peer, pidx = peer_of(k)
            for a in range(n):
                cp = pltpu.make_async_remote_copy(
                    src_ref=src(a, pidx), dst_ref=outs[a].at[me],
                    send_sem=send_sems.at[a * (N_DEV - 1) + k - 1], recv_sem=recv_sems.at[a * (N_DEV - 1) + k - 1],
                    device_id=peer, device_id_type=MESH)
                cp.start()
                sends.append(cp)
        for k in range(1, N_DEV):
            peer, pidx = peer_of(k)
            for a in range(n):
                pltpu.make_async_remote_copy(
                    src_ref=src(a, pidx), dst_ref=outs[a].at[pidx],
                    send_sem=send_sems.at[a * (N_DEV - 1) + k - 1], recv_sem=recv_sems.at[a * (N_DEV - 1) + k - 1],
                    device_id=peer, device_id_type=MESH).wait_recv()
        for cp in sends:
            cp.wait_send()
        for cp in local:
            cp.wait()

    return pl.pallas_call(
        body, name=name, out_shape=out_shape,
        in_specs=[ANY] * n, out_specs=[ANY] * n,
        scratch_shapes=[pltpu.SemaphoreType.DMA((n * (N_DEV - 1),)),
                        pltpu.SemaphoreType.DMA((n * (N_DEV - 1),)),
                        pltpu.SemaphoreType.DMA((n,))],
    )(*arrs)


GATHER_SEMS = 7


def _gather_copies(ins, outs, send_sems, recv_sems, local_sems, x, y, c):
    n = len(ins)
    per = GATHER_SEMS
    sib = (x, y, 1 - c)
    xn, yn, dg = (1 - x, y), (x, 1 - y), (1 - x, 1 - y)
    fx, fy = x + (1 - c) * (1 - 2 * x), y + c * (1 - 2 * y)
    tx, ty = x + c * (1 - 2 * x), y + (1 - c) * (1 - 2 * y)

    def slot(a, px, py, pc):
        return outs[a].at[4 * px + 2 * py + pc]

    def copy(a, k, block, to, src=None):
        return pltpu.make_async_remote_copy(
            src_ref=slot(a, *block) if src is None else src, dst_ref=slot(a, *block),
            send_sem=send_sems.at[a * per + k], recv_sem=recv_sems.at[a * per + k],
            device_id=to, device_id_type=MESH)

    local = [pltpu.make_async_copy(ins[a], slot(a, x, y, c), local_sems.at[a]) for a in range(n)]
    for cp in local:
        cp.start()
    started = []
    for a in range(n):
        started += [copy(a, 1, (x, y, c), (*xn, c), src=ins[a]), copy(a, 2, (x, y, c), (*yn, c), src=ins[a])]
    for a in range(n):
        started.append(copy(a, 0, (x, y, c), sib, src=ins[a]))
    for cp in started:
        cp.start()
    for a in range(n):
        copy(a, 1, (*xn, c), (x, y, c)).wait_recv()
        copy(a, 2, (*yn, c), (x, y, c)).wait_recv()
        later = [copy(a, 3, (fx, fy, c), (tx, ty, c)), copy(a, 4, (*xn, c), sib), copy(a, 5, (*yn, c), sib)]
        for cp in later:
            cp.start()
        started += later
    for a in range(n):
        copy(a, 3, (*dg, c), (x, y, c)).wait_recv()
        cp = copy(a, 6, (*dg, c), sib)
        cp.start()
        started.append(cp)
    for a in range(n):
        copy(a, 0, sib, (x, y, c)).wait_recv()
        for k, chip in ((4, xn), (5, yn), (6, dg)):
            copy(a, k, (*chip, 1 - c), (x, y, c)).wait_recv()
    for cp in started:
        cp.wait_send()
    for cp in local:
        cp.wait()


def _gather2(name, arrs):
    n = len(arrs)
    per = GATHER_SEMS
    out_shape = [jax.ShapeDtypeStruct((N_DEV,) + a.shape, a.dtype) for a in arrs]

    def body(*refs):
        ins, outs = refs[:n], refs[n:2 * n]
        send_sems, recv_sems, local_sems = refs[2 * n:]
        x, y, c = lax.axis_index("x"), lax.axis_index("y"), lax.axis_index("c")
        _gather_copies(ins, outs, send_sems, recv_sems, local_sems, x, y, c)

    return pl.pallas_call(
        body, name=name, out_shape=out_shape,
        in_specs=[ANY] * n, out_specs=[ANY] * n,
        scratch_shapes=[pltpu.SemaphoreType.DMA((n * per,)), pltpu.SemaphoreType.DMA((n * per,)),
                        pltpu.SemaphoreType.DMA((n,))],
    )(*arrs)


def _seq_gather2(name, collective_id, arrs):
    n = len(arrs)
    per = GATHER_SEMS

    def body(*refs):
        ins, outs = refs[:n], refs[n:2 * n]
        send_sems, recv_sems, local_sems = refs[2 * n:]
        x, y, c = lax.axis_index("x"), lax.axis_index("y"), lax.axis_index("c")
        barrier = pltpu.get_barrier_semaphore()
        for peer in [(x, y, 1 - c), (1 - x, y, c), (x, 1 - y, c)]:
            pl.semaphore_signal(barrier, inc=1, device_id=peer, device_id_type=MESH)
        pl.semaphore_wait(barrier, 3)
        _gather_copies(ins, outs, send_sems, recv_sems, local_sems, x, y, c)

    return pl.kernel(
        body, out_type=[jax.ShapeDtypeStruct((N_DEV,) + a.shape, a.dtype) for a in arrs],
        mesh=plsc.ScalarSubcoreMesh(axis_name="seq", num_cores=1),
        scratch_types=[pltpu.SemaphoreType.DMA((n * per,)), pltpu.SemaphoreType.DMA((n * per,)),
                       pltpu.SemaphoreType.DMA((n,))],
        compiler_params=pltpu.CompilerParams(collective_id=collective_id), name=name,
    )(*arrs)


def _seq_chip_exchange(name, collective_id, arrs):
    n = len(arrs)

    def body(*refs):
        ins, outs = refs[:n], refs[n:2 * n]
        send_sems, recv_sems = refs[2 * n:]
        x, y, c = lax.axis_index("x"), lax.axis_index("y"), lax.axis_index("c")

        def peer(k):
            return (1 - x if (k >> 1) & 1 else x), (1 - y if k & 1 else y)

        barrier = pltpu.get_barrier_semaphore()
        for k in (1, 2, 3):
            pl.semaphore_signal(barrier, inc=1, device_id=(*peer(k), c), device_id_type=MESH)
        pl.semaphore_wait(barrier, 3)

        def copy(a, k):
            px, py = peer(k)
            return pltpu.make_async_remote_copy(
                src_ref=ins[a].at[2 * px + py], dst_ref=outs[a].at[k - 1],
                send_sem=send_sems.at[a * 3 + k - 1], recv_sem=recv_sems.at[a * 3 + k - 1],
                device_id=(px, py, c), device_id_type=MESH)

        cps = [copy(a, k) for a in range(n) for k in (1, 2, 3)]
        for cp in cps:
            cp.start()
        for cp in cps:
            cp.wait_recv()
        for cp in cps:
            cp.wait_send()

    return pl.kernel(
        body, out_type=[jax.ShapeDtypeStruct((3,) + a.shape[1:], a.dtype) for a in arrs],
        mesh=plsc.ScalarSubcoreMesh(axis_name="seq", num_cores=1),
        scratch_types=[pltpu.SemaphoreType.DMA((n * 3,)), pltpu.SemaphoreType.DMA((n * 3,))],
        compiler_params=pltpu.CompilerParams(collective_id=collective_id), name=name,
    )(*arrs)


def _seq_pair_swap(name, collective_id, arrs):
    n = len(arrs)

    def body(*refs):
        ins, outs = refs[:n], refs[n:2 * n]
        send_sems, recv_sems = refs[2 * n:]
        x, y, c = lax.axis_index("x"), lax.axis_index("y"), lax.axis_index("c")
        barrier = pltpu.get_barrier_semaphore()
        pl.semaphore_signal(barrier, inc=1, device_id=(x, y, 1 - c), device_id_type=MESH)
        pl.semaphore_wait(barrier, 1)

        def copy(a, q):
            return pltpu.make_async_remote_copy(
                src_ref=ins[a].at[q, 1 - c], dst_ref=outs[a].at[q],
                send_sem=send_sems.at[a * 4 + q], recv_sem=recv_sems.at[a * 4 + q],
                device_id=(x, y, 1 - c), device_id_type=MESH)

        cps = [copy(a, q) for a in range(n) for q in range(4)]
        for cp in cps:
            cp.start()
        for cp in cps:
            cp.wait_recv()
        for cp in cps:
            cp.wait_send()

    return pl.kernel(
        body, out_type=[jax.ShapeDtypeStruct((4,) + a.shape[2:], a.dtype) for a in arrs],
        mesh=plsc.ScalarSubcoreMesh(axis_name="seq", num_cores=1),
        scratch_types=[pltpu.SemaphoreType.DMA((n * 4,)), pltpu.SemaphoreType.DMA((n * 4,))],
        compiler_params=pltpu.CompilerParams(collective_id=collective_id), name=name,
    )(*arrs)


def _call(body, name, grid, in_specs, out_specs, out_shape, args, scratch=()):
    return pl.pallas_call(
        body, name=name, grid=grid, in_specs=in_specs, out_specs=out_specs, out_shape=out_shape,
        scratch_shapes=list(scratch), compiler_params=_params(("arbitrary",) * len(grid)))(*args)


def _ada_fwd(c_all, w_ada_sh, b_ada_sh):
    nb, d = c_all.shape
    ncol = w_ada_sh.shape[1]

    def body(c_ref, w_ref, b_ref, mod_ref, cact_ref):
        cc = c_ref[...]
        ca = cc * jax.nn.sigmoid(cc)
        cact_ref[...] = ca
        mod_ref[...] = _dot(ca.astype(BF16), w_ref[...].astype(BF16), NN) + b_ref[...]

    return pl.pallas_call(
        body, name="ada_fwd",
        out_shape=[jax.ShapeDtypeStruct((nb, ncol), F32), jax.ShapeDtypeStruct((nb, d), F32)],
        compiler_params=_params(),
    )(c_all, w_ada_sh, b_ada_sh)


def _rms(xv):
    rstd = lax.rsqrt(jnp.mean(xv * xv, axis=-1, keepdims=True) + EPS)
    return xv * rstd, rstd


def _rms_bwd(dxhat, xhat, rstd):
    return rstd * (dxhat - xhat * jnp.mean(dxhat * xhat, axis=-1, keepdims=True))


def _colsum(v):
    return jnp.sum(v, axis=0, keepdims=True)


def _expm1(v, ev):
    series = v * (1.0 + v * (0.5 + v * (1.0 / 6.0 + v * (1.0 / 24.0 + v * (1.0 / 120.0)))))
    return jnp.where(jnp.abs(v) < 0.2, series, ev - 1.0)


def _softplus(v):
    return jnp.maximum(v, 0.0) + jnp.log1p(jnp.exp(-jnp.abs(v)))


def _gelu(v):
    t = jnp.tanh(v * (GELU_K0 + (GELU_K0 * GELU_K1) * (v * v)))
    return 0.5 * v * (1.0 + t), t


def _dgelu(v, t):
    return 0.5 * ((1.0 + t) + (v * (1.0 - t * t)) * (GELU_K0 + (3.0 * GELU_K0 * GELU_K1) * (v * v)))


def _scan_tile(a, b, x0, st, k0, reverse):
    t = a.shape[0]
    off = SUBLANES
    stage_a, stage_b = st.at[k0], st.at[k0 + 1]
    halo = slice(off + t, off + t + SUBLANES) if reverse else slice(0, SUBLANES)
    stage_a[halo, :] = jnp.ones((SUBLANES, a.shape[1]), F32)
    stage_b[halo, :] = jnp.zeros((SUBLANES, a.shape[1]), F32)
    s = 1
    while s < min(t, SUBLANES):
        stage_a[off:off + t, :] = a
        stage_b[off:off + t, :] = b
        at = off + s if reverse else off - s
        b = a * stage_b[at:at + t, :] + b
        a = a * stage_a[at:at + t, :]
        s *= 2
    while s < t:
        if reverse:
            b = jnp.concatenate([a[:t - s] * b[s:] + b[:t - s], b[t - s:]], axis=0)
            a = jnp.concatenate([a[:t - s] * a[s:], a[t - s:]], axis=0)
        else:
            b = jnp.concatenate([b[:s], a[s:] * b[:t - s] + b[s:]], axis=0)
            a = jnp.concatenate([a[:s], a[s:] * a[:t - s]], axis=0)
        s *= 2
    x = b + a * x0
    return x, (x[0:SUBLANES, :] if reverse else x[t - SUBLANES:t, :])


def _lru_gates(u, wa, wx, ba, bx, sp):
    ub = u.astype(BF16)
    r = jax.nn.sigmoid(_dot(ub, wa, NN) + ba)
    i = jax.nn.sigmoid(_dot(ub, wx, NN) + bx)
    log_a = (-RG_C * r) * sp
    a = jnp.exp(log_a)
    mult = jnp.sqrt(-_expm1(log_a, a) * (a + 1.0))
    return ub, r, i, a, mult


def _staged_shifts(stage, v, prev8, next8, downs, ups):
    t = v.shape[0]
    if prev8 is not None:
        stage[0:SUBLANES, :] = prev8
    stage[SUBLANES:SUBLANES + t, :] = v
    if next8 is not None:
        stage[SUBLANES + t:2 * SUBLANES + t, :] = next8
    return ([stage[SUBLANES - k:SUBLANES - k + t, :] for k in downs],
            [stage[SUBLANES + k:SUBLANES + k + t, :] for k in ups])


def _conv3(p, pp, w_ref, lo, stage):
    (p1, p2), _ = _staged_shifts(stage, p, pp, None, (1, 2), ())
    q = (w_ref[0:1, lo:lo + LANES] * p2 + w_ref[1:2, lo:lo + LANES] * p1) + w_ref[2:3, lo:lo + LANES] * p
    return q, p1, p2


def _conv4(xv, xp, w_ref, b_ref, lo, stage):
    (x1, x2, x3), _ = _staged_shifts(stage, xv, xp, None, (1, 2, 3), ())
    u = (((w_ref[0:1, lo:lo + LANES] * x3 + w_ref[1:2, lo:lo + LANES] * x2) + w_ref[2:3, lo:lo + LANES] * x1)
         + w_ref[3:4, lo:lo + LANES] * xv) + b_ref[:, lo:lo + LANES]
    return u, x1, x2, x3


def _mix_in_mixer_fwd(x2d, mod6, g_mix, w_in_t, conv_sc, conv_lru, conv_b, wa_bd, wx_bd, ba, bx, lam, width, tm):
    s, d = x2d.shape
    din = w_in_t.shape[0]
    nt = s // tm
    sub = min(MIX_ROWS, tm)
    nblk = width // LANES

    def body(x_ref, mod_ref, g_ref, w_ref, wsc_ref, wlru_ref, blru_ref, wa_ref, wx_ref, ba_ref, bx_ref, lam_ref,
             hn_ref, proj_ref, ymix_ref, h_ref, buf_ref, halo_ref, hc_ref, stage_ref):
        i = pl.program_id(0)

        @pl.when(i == 0)
        def _():
            buf_ref[1] = jnp.zeros((tm, din), F32)
            halo_ref[...] = jnp.zeros_like(halo_ref)

        @pl.when(i <= 1)
        def _():
            hc_ref[...] = jnp.zeros_like(hc_ref)

        def step(dst, src):
            xhat, _ = _rms(x_ref[...])
            hn = ((xhat * g_ref[...]) * (1.0 + mod_ref[1:2, :]) + mod_ref[0:1, :]).astype(BF16)
            hn_ref[...] = hn
            n_mix = (tm // sub) * nblk
            n_chunk = din // width

            def project(k):
                res = _dot(hn_ref[...], w_ref[k * width:(k + 1) * width, :], NT)
                proj_ref[:, k * width:(k + 1) * width] = res
                dst[:, k * width:(k + 1) * width] = res

            done = 0
            for half in range(tm // sub):
                r0 = half * sub
                rows = slice(r0, r0 + sub)
                for j in range(nblk):
                    lo = j * LANES
                    while done < n_chunk and done * n_mix <= (half * nblk + j) * n_chunk:
                        project(done)
                        done += 1

                    def col(p):
                        return src[rows, p * width + lo:p * width + lo + LANES]

                    def prev(p):
                        c0 = p * width + lo
                        if half == 0:
                            return halo_ref[:, c0:c0 + LANES]
                        return src[r0 - SUBLANES:r0, c0:c0 + LANES]

                    pp = col(1) * col(2)
                    q, _, _ = _conv3(pp, prev(1) * prev(2), wsc_ref, lo, stage_ref.at[0])
                    ymix_ref[rows, lo:lo + LANES] = (col(0) * q).astype(BF16)

                    u, _, _, _ = _conv4(col(4), prev(4), wlru_ref, blru_ref, lo, stage_ref.at[1])
                    sp = _softplus(-lam_ref[:, lo:lo + LANES])
                    _, r, ig, a, mult = _lru_gates(u, wa_ref[j], wx_ref[j], ba_ref[:, lo:lo + LANES],
                                                   bx_ref[:, lo:lo + LANES], sp)
                    h, ends = _scan_tile(a, mult * (ig * u), hc_ref[0:1, lo:lo + LANES], stage_ref, 2, False)
                    h_ref[rows, lo:lo + LANES] = h
                    hc_ref[0:1, lo:lo + LANES] = ends[SUBLANES - 1:SUBLANES, :]
                    gel, _ = _gelu(col(3))
                    ymix_ref[rows, width + lo:width + lo + LANES] = (gel * h).astype(BF16)
            while done < n_chunk:
                project(done)
                done += 1
            halo_ref[...] = src[tm - SUBLANES:tm, :]

        @pl.when(i % 2 == 0)
        def _():
            step(buf_ref.at[0], buf_ref.at[1])

        @pl.when(i % 2 == 1)
        def _():
            step(buf_ref.at[1], buf_ref.at[0])

    small = [conv_sc, conv_lru, conv_b, wa_bd, wx_bd, ba, bx, lam]
    cur = lambda i: (jnp.minimum(i, nt - 1), 0)
    last = lambda i: (jnp.maximum(i - 1, 0), 0)
    outs = _call(
        body, "mix_in_mixer_fwd", (nt + 1,),
        [pl.BlockSpec((tm, d), cur), _full(mod6.shape), _full(g_mix.shape), _full(w_in_t.shape)]
        + [_full(a.shape) for a in small],
        [pl.BlockSpec((tm, d), cur), pl.BlockSpec((tm, din), cur),
         pl.BlockSpec((tm, 2 * width), last), pl.BlockSpec((tm, width), last)],
        [jax.ShapeDtypeStruct((s, d), BF16), jax.ShapeDtypeStruct((s, din), F32),
         jax.ShapeDtypeStruct((s, 2 * width), BF16), jax.ShapeDtypeStruct((s, width), F32)],
        [x2d, mod6, g_mix, w_in_t, *small],
        scratch=[pltpu.VMEM((2, tm, din), F32), pltpu.VMEM((SUBLANES, din), F32), pltpu.VMEM((SUBLANES, width), F32),
                 pltpu.VMEM((4, sub + 2 * SUBLANES, LANES), F32)])
    return outs


def _mix_out_fwd(ymix, x2d, w_out, mod6, g_mlp, tm):
    s, d = x2d.shape

    def body(y_ref, x_ref, w_ref, mod_ref, g_ref, mix_ref, x2_ref, hn_ref):
        mix = _dot(y_ref[...], w_ref[...], NN)
        mix_ref[...] = mix.astype(BF16)
        x2 = x_ref[...] + mod_ref[2:3, :] * mix
        x2_ref[...] = x2
        xhat, _ = _rms(x2)
        hn_ref[...] = ((xhat * g_ref[...]) * (1.0 + mod_ref[4:5, :]) + mod_ref[3:4, :]).astype(BF16)

    tile = pl.BlockSpec((tm, d), lambda i: (i, 0))
    return _call(
        body, "mix_out_fwd", (s // tm,),
        [tile, tile, _full(w_out.shape), _full(mod6.shape), _full(g_mlp.shape)],
        [tile, tile, tile],
        [jax.ShapeDtypeStruct((s, d), BF16), jax.ShapeDtypeStruct((s, d), F32), jax.ShapeDtypeStruct((s, d), BF16)],
        [ymix, x2d, w_out, mod6, g_mlp])


def _mlp_fwd_loss(hn2, w_up_t, w_down, x2, target, mod6, g_final, tm, tk):
    s, d = hn2.shape
    f = w_up_t.shape[0]
    nk = f // tk

    def body(hn_ref, wu_ref, wd_ref, x2_hbm, t_hbm, mod_ref, g_ref, z_ref, dx3_ref, dyb_ref, st_ref,
             y_ref, x2_ref, t_ref, sems):
        i, k = pl.program_id(0), pl.program_id(1)

        def fetch():
            rows = pl.ds(pl.multiple_of(i * tm, tm), tm)
            return (pltpu.make_async_copy(x2_hbm.at[rows, :], x2_ref, sems.at[0]),
                    pltpu.make_async_copy(t_hbm.at[rows, :], t_ref, sems.at[1]))

        @pl.when(jnp.logical_and(i == 0, k == 0))
        def _():
            st_ref[...] = jnp.zeros_like(st_ref)

        @pl.when(k == 0)
        def _():
            for cp in fetch():
                cp.start()
            y_ref[...] = jnp.zeros_like(y_ref)

        z = jnp.maximum(_dot(hn_ref[...], wu_ref[...], NT), 0.0)
        z_ref[...] = z.astype(BF16)
        y_ref[...] += _dot((z * z).astype(BF16), wd_ref[...], NN)

        @pl.when(k == nk - 1)
        def _():
            for cp in fetch():
                cp.wait()
            gate = mod_ref[5:6, :]
            yv = y_ref[...]
            xhat, rstd = _rms(x2_ref[...] + gate * yv)
            diff = xhat * g_ref[...] - t_ref[...]
            dyo = diff * (1.0 / d)
            dx3 = _rms_bwd(dyo * g_ref[...], xhat, rstd)
            dx3_ref[...] = dx3.astype(BF16)
            dyb_ref[...] = (gate * dx3).astype(BF16)
            st_ref[0:1, :] += _colsum(dyo * xhat)
            st_ref[1:2, :] += _colsum(dx3 * yv)
            st_ref[2:3, :] += _colsum(diff * diff)

    tile = pl.BlockSpec((tm, d), lambda i, k: (i, 0))
    wblk = pl.BlockSpec((tk, d), lambda i, k: (k, 0))
    return pl.pallas_call(
        body, name="mlp_fwd_loss", grid=(s // tm, nk),
        in_specs=[tile, wblk, wblk, ANY, ANY, _full(mod6.shape), _full(g_final.shape)],
        out_specs=[pl.BlockSpec((tm, tk), lambda i, k: (i, k)), tile, tile, _full((SUBLANES, d))],
        out_shape=[jax.ShapeDtypeStruct((s, f), BF16), jax.ShapeDtypeStruct((s, d), BF16),
                   jax.ShapeDtypeStruct((s, d), BF16), jax.ShapeDtypeStruct((SUBLANES, d), F32)],
        scratch_shapes=[pltpu.VMEM((tm, d), F32), pltpu.VMEM((tm, d), F32), pltpu.VMEM((tm, d), F32),
                        pltpu.SemaphoreType.DMA((2,))],
        compiler_params=pltpu.CompilerParams(dimension_semantics=("arbitrary", "arbitrary"),
                                             vmem_limit_bytes=VMEM_LIMIT_BIG),
    )(hn2, w_up_t, w_down, x2, target, mod6, g_final)


def _mlp_bwd_dx(dyb, z, w_down, w_up_t, tm, tk):
    s, d = dyb.shape
    f = z.shape[1]

    nk = f // tk

    def body(dy_ref, z_ref, wd_ref, wu_ref, dz_ref, dh_ref, acc_ref):
        k = pl.program_id(1)

        @pl.when(k == 0)
        def _():
            acc_ref[...] = jnp.zeros_like(acc_ref)

        dz = ((2.0 * z_ref[...].astype(F32)) * _dot(dy_ref[...], wd_ref[...], NT)).astype(BF16)
        dz_ref[...] = dz
        acc_ref[...] += _dot(dz, wu_ref[...], NN)

        @pl.when(k == nk - 1)
        def _():
            dh_ref[...] = acc_ref[...].astype(BF16)

    return pl.pallas_call(
        body, name="mlp_bwd_dx", grid=(s // tm, nk),
        in_specs=[pl.BlockSpec((tm, d), lambda i, k: (i, 0)), pl.BlockSpec((tm, tk), lambda i, k: (i, k)),
                  pl.BlockSpec((tk, d), lambda i, k: (k, 0)), pl.BlockSpec((tk, d), lambda i, k: (k, 0))],
        out_specs=[pl.BlockSpec((tm, tk), lambda i, k: (i, k)), pl.BlockSpec((tm, d), lambda i, k: (i, 0))],
        out_shape=[jax.ShapeDtypeStruct((s, f), BF16), jax.ShapeDtypeStruct((s, d), BF16)],
        scratch_shapes=[pltpu.VMEM((tm, d), F32)],
        compiler_params=_params(("parallel", "arbitrary")),
    )(dyb, z, w_down, w_up_t)


def _mlp_bwd_dw(z, dz, dyb, hn2, tm, tk):
    s, d = dyb.shape
    f = z.shape[1]

    def body(z_ref, dz_ref, dy_ref, hn_ref, gd_ref, gu_ref):
        i = pl.program_id(1)

        @pl.when(i == 0)
        def _():
            gd_ref[...] = jnp.zeros_like(gd_ref)
            gu_ref[...] = jnp.zeros_like(gu_ref)

        zf = z_ref[...].astype(F32)
        gd_ref[...] += _dot((zf * zf).astype(BF16), dy_ref[...], TN)
        gu_ref[...] += _dot(dz_ref[...], hn_ref[...], TN)

    return pl.pallas_call(
        body, name="mlp_bwd_dw", grid=(f // tk, s // tm),
        in_specs=[pl.BlockSpec((tm, tk), lambda k, i: (i, k)), pl.BlockSpec((tm, tk), lambda k, i: (i, k)),
                  pl.BlockSpec((tm, d), lambda k, i: (i, 0)), pl.BlockSpec((tm, d), lambda k, i: (i, 0))],
        out_specs=[pl.BlockSpec((tk, d), lambda k, i: (k, 0)), pl.BlockSpec((tk, d), lambda k, i: (k, 0))],
        out_shape=[jax.ShapeDtypeStruct((f, d), F32), jax.ShapeDtypeStruct((f, d), F32)],
        compiler_params=_params(("parallel", "arbitrary")),
    )(z, dz, dyb, hn2)


def _mix_out_bwd(dhn2, x2, dx3, mix, ymix, w_out, mod6, g_mlp, tm):
    s, d = x2.shape

    def body(dh_ref, x2_ref, dx3_ref, mix_ref, y_ref, w_ref, mod_ref, g_ref, dx2_ref, dym_ref, gw_ref, st_ref):
        i = pl.program_id(0)

        @pl.when(i == 0)
        def _():
            st_ref[...] = jnp.zeros_like(st_ref)
            gw_ref[...] = jnp.zeros_like(gw_ref)

        dh = dh_ref[...].astype(F32)
        xhat, rstd = _rms(x2_ref[...])
        dn = dh * (1.0 + mod_ref[4:5, :])
        dx2 = dx3_ref[...].astype(F32) + _rms_bwd(dn * g_ref[...], xhat, rstd)
        dx2_ref[...] = dx2.astype(BF16)
        st_ref[0:1, :] += _colsum(dh)
        st_ref[1:2, :] += _colsum(dh * (xhat * g_ref[...]))
        st_ref[2:3, :] += _colsum(dn * xhat)
        st_ref[3:4, :] += _colsum(dx2 * mix_ref[...].astype(F32))
        dmix = (mod_ref[2:3, :] * dx2).astype(BF16)
        dym_ref[...] = _dot(dmix, w_ref[...], NT).astype(BF16)
        gw_ref[...] += _dot(y_ref[...], dmix, TN)

    tile = pl.BlockSpec((tm, d), lambda i: (i, 0))
    return _call(
        body, "mix_out_bwd", (s // tm,),
        [tile, tile, tile, tile, tile, _full(w_out.shape), _full(mod6.shape), _full(g_mlp.shape)],
        [tile, tile, _full((d, d)), _full((SUBLANES, d))],
        [jax.ShapeDtypeStruct((s, d), BF16), jax.ShapeDtypeStruct((s, d), BF16),
         jax.ShapeDtypeStruct((d, d), F32), jax.ShapeDtypeStruct((SUBLANES, d), F32)],
        [dhn2, x2, dx3, mix, ymix, w_out, mod6, g_mlp])


def _mixer_bwd(proj, dymix, h_all, conv_sc, conv_lru, conv_b, wa_bd, wx_bd, ba, bx, lam, width):
    s, din = proj.shape
    t = min(MIX_ROWS, s)
    nt = s // t
    nblk = width // LANES
    hb = t // SUBLANES
    last8 = s // SUBLANES - 1

    def body(proj_ref, projp_ref, projn_ref, dy_ref, dyn_ref, h_ref, hp_ref,
             wsc_ref, wlru_ref, blru_ref, wa_ref, wx_ref, ba_ref, bx_ref, lam_ref,
             dproj_ref, small_ref, gwa_ref, gwx_ref, an_ref, gn_ref, dun_ref, stage_ref):
        i = pl.program_id(0)

        @pl.when(i == 0)
        def _():
            small_ref[...] = jnp.zeros_like(small_ref)
            gwa_ref[...] = jnp.zeros_like(gwa_ref)
            gwx_ref[...] = jnp.zeros_like(gwx_ref)
            an_ref[...] = jnp.zeros_like(an_ref)
            gn_ref[...] = jnp.zeros_like(gn_ref)
            dun_ref[...] = jnp.zeros_like(dun_ref)

        has_prev = i < nt - 1
        has_next = i > 0
        for j in range(nblk):
            lo = j * LANES
            ls = slice(lo, lo + LANES)

            def col(p, ref=proj_ref):
                return ref[:, p * width + lo:p * width + lo + LANES]

            def prev(p):
                return jnp.where(has_prev, col(p, projp_ref), 0.0)

            def nxt(p):
                return jnp.where(has_next, col(p, projn_ref), 0.0)

            def add_row(r, v):
                small_ref[r:r + 1, ls] += _colsum(v)

            sc_b, sc_c, sc_x = col(0), col(1), col(2)
            p = sc_c * sc_x
            q, p1, p2 = _conv3(p, prev(1) * prev(2), wsc_ref, lo, stage_ref.at[0])
            dys = dy_ref[:, ls].astype(F32)
            dproj_ref[:, ls] = (dys * q).astype(BF16)
            dq = dys * sc_b
            dqn = jnp.where(has_next, dyn_ref[:, ls].astype(F32)[0:SUBLANES], 0.0) * nxt(0)
            _, (dq1, dq2) = _staged_shifts(stage_ref.at[1], dq, None, dqn, (), (1, 2))
            dp = (wsc_ref[2:3, ls] * dq + wsc_ref[1:2, ls] * dq1) + wsc_ref[0:1, ls] * dq2
            dproj_ref[:, width + lo:width + lo + LANES] = (dp * sc_x).astype(BF16)
            dproj_ref[:, 2 * width + lo:2 * width + lo + LANES] = (dp * sc_c).astype(BF16)
            add_row(0, dq * p2)
            add_row(1, dq * p1)
            add_row(2, dq * p)

            xv = col(4)
            u, x1, x2, x3 = _conv4(xv, prev(4), wlru_ref, blru_ref, lo, stage_ref.at[2])
            lam_v = lam_ref[:, ls]
            sp = _softplus(-lam_v)
            wa, wx = wa_ref[j], wx_ref[j]
            ub, r, ig, a, mult = _lru_gates(u, wa, wx, ba_ref[:, ls], bx_ref[:, ls], sp)
            iu = ig * u
            h = h_ref[:, ls]
            (hm1,), _ = _staged_shifts(stage_ref.at[3], h, jnp.where(has_prev, hp_ref[:, ls], 0.0), None, (1,), ())
            lyv = col(3)
            gel, th = _gelu(lyv)
            dyl = dy_ref[:, width + lo:width + lo + LANES].astype(F32)
            dproj_ref[:, 3 * width + lo:3 * width + lo + LANES] = (dyl * h * _dgelu(lyv, th)).astype(BF16)
            a_next = jnp.broadcast_to(an_ref[0:1, ls], (SUBLANES, LANES))
            _, (a_up,) = _staged_shifts(stage_ref.at[4], a, None, a_next, (), (1,))
            g, _ = _scan_tile(a_up, dyl * gel, gn_ref[0:1, ls], stage_ref, 5, True)
            an_ref[0:1, ls] = a[0:1, :]
            gn_ref[0:1, ls] = g[0:1, :]
            da = g * hm1
            dmult = g * iu
            diu = g * mult
            dlog_a = da * a - dmult * ((a * a) / mult)
            dpre_a = (dlog_a * (-RG_C * sp)) * (r * (1.0 - r))
            dpre_x = (diu * u) * (ig * (1.0 - ig))
            dab, dxb = dpre_a.astype(BF16), dpre_x.astype(BF16)
            du = diu * ig + _dot(dab, wa, NT) + _dot(dxb, wx, NT)
            gwa_ref[j] += _dot(ub, dab, TN)
            gwx_ref[j] += _dot(ub, dxb, TN)
            dun = dun_ref[:, ls]
            dun_ref[:, ls] = du[0:SUBLANES, :]
            _, (du1, du2, du3) = _staged_shifts(stage_ref.at[7], du, None, dun, (), (1, 2, 3))
            dlx = (((wlru_ref[3:4, ls] * du + wlru_ref[2:3, ls] * du1) + wlru_ref[1:2, ls] * du2)
                   + wlru_ref[0:1, ls] * du3)
            dproj_ref[:, 4 * width + lo:4 * width + lo + LANES] = dlx.astype(BF16)
            add_row(3, du * x3)
            add_row(4, du * x2)
            add_row(5, du * x1)
            add_row(6, du * xv)
            add_row(7, du)
            add_row(8, dpre_a)
            add_row(9, dpre_x)
            add_row(10, (dlog_a * (RG_C * r)) * jax.nn.sigmoid(-lam_v))

    small = [conv_sc, conv_lru, conv_b, wa_bd, wx_bd, ba, bx, lam]
    rev = lambda i: nt - 1 - i
    return _call(
        body, "mixer_bwd", (nt,),
        [pl.BlockSpec((t, din), lambda i: (rev(i), 0)),
         pl.BlockSpec((SUBLANES, din), lambda i: (jnp.maximum(rev(i) * hb - 1, 0), 0)),
         pl.BlockSpec((SUBLANES, din), lambda i: (jnp.minimum((rev(i) + 1) * hb, last8), 0)),
         pl.BlockSpec((t, 2 * width), lambda i: (rev(i), 0)),
         pl.BlockSpec((2 * SUBLANES, 2 * width), lambda i: (jnp.minimum((rev(i) + 1) * (hb // 2), last8 // 2), 0)),
         pl.BlockSpec((t, width), lambda i: (rev(i), 0)),
         pl.BlockSpec((SUBLANES, width), lambda i: (jnp.maximum(rev(i) * hb - 1, 0), 0))]
        + [_full(a.shape) for a in small],
        [pl.BlockSpec((t, din), lambda i: (rev(i), 0)), _full((2 * SUBLANES, width)),
         _full(wa_bd.shape), _full(wx_bd.shape)],
        [jax.ShapeDtypeStruct((s, din), BF16), jax.ShapeDtypeStruct((2 * SUBLANES, width), F32),
         jax.ShapeDtypeStruct(wa_bd.shape, F32), jax.ShapeDtypeStruct(wx_bd.shape, F32)],
        [proj, proj, proj, dymix, dymix, h_all, h_all, *small],
        scratch=[pltpu.VMEM((SUBLANES, width), F32), pltpu.VMEM((SUBLANES, width), F32),
                 pltpu.VMEM((SUBLANES, width), F32), pltpu.VMEM((8, t + 2 * SUBLANES, LANES), F32)])


def _mix_in_bwd_dx(dproj, x2d, dx2, w_in_t, mod6, g_mix, tm):
    s, d = x2d.shape
    din = dproj.shape[1]

    def body(dp_ref, x_ref, dx2_ref, w_ref, mod_ref, g_ref, gx_ref, st_ref):
        i = pl.program_id(0)

        @pl.when(i == 0)
        def _():
            st_ref[...] = jnp.zeros_like(st_ref)

        dh = _dot(dp_ref[...], w_ref[...], NN)
        xhat, rstd = _rms(x_ref[...])
        dn = dh * (1.0 + mod_ref[1:2, :])
        gx_ref[...] = dx2_ref[...].astype(F32) + _rms_bwd(dn * g_ref[...], xhat, rstd)
        st_ref[0:1, :] += _colsum(dh)
        st_ref[1:2, :] += _colsum(dh * (xhat * g_ref[...]))
        st_ref[2:3, :] += _colsum(dn * xhat)

    tile = pl.BlockSpec((tm, d), lambda i: (i, 0))
    return _call(
        body, "mix_in_bwd_dx", (s // tm,),
        [pl.BlockSpec((tm, din), lambda i: (i, 0)), tile, tile, _full(w_in_t.shape), _full(mod6.shape),
         _full(g_mix.shape)],
        [tile, _full((SUBLANES, d))],
        [jax.ShapeDtypeStruct((s, d), F32), jax.ShapeDtypeStruct((SUBLANES, d), F32)],
        [dproj, x2d, dx2, w_in_t, mod6, g_mix])


def _mix_in_bwd_dw(dproj, hn1, tm, tn):
    s, d = hn1.shape
    din = dproj.shape[1]

    def body(dp_ref, hn_ref, gw_ref):
        i = pl.program_id(1)

        @pl.when(i == 0)
        def _():
            gw_ref[...] = jnp.zeros_like(gw_ref)

        gw_ref[...] += _dot(dp_ref[...], hn_ref[...], TN)

    return _call(
        body, "mix_in_bwd_dw", (din // tn, s // tm),
        [pl.BlockSpec((tm, tn), lambda p, i: (i, p)), pl.BlockSpec((tm, d), lambda p, i: (i, 0))],
        [pl.BlockSpec((tn, d), lambda p, i: (p, 0))],
        [jax.ShapeDtypeStruct((din, d), F32)],
        [dproj, hn1])


def _adamw(w, g, m, v):
    m = ADAM_B1 * m + (1.0 - ADAM_B1) * g
    v = ADAM_B2 * v + (1.0 - ADAM_B2) * (g * g)
    m_hat = m / (1.0 - ADAM_B1 ** ADAM_STEP)
    v_hat = v / (1.0 - ADAM_B2 ** ADAM_STEP)
    delta = -ADAM_LR * (m_hat / (jnp.sqrt(v_hat) + ADAM_EPS) + ADAM_WD * w)
    return delta, m, v


def _pair_sum(g4s, h4s, core_chip, tr, name):
    na = len(g4s)
    _, _, r, n = g4s[0].shape

    def body(sc_ref, *refs):
        q = pl.program_id(1)
        for a in range(na):
            g_ref, h_ref = refs[2 * a], refs[2 * a + 1]
            sb_ref, own_ref = refs[2 * na + 2 * a], refs[2 * na + 2 * a + 1]
            ssum = g_ref[...] + h_ref[...]
            sb_ref[...] = ssum.astype(BF16)

            @pl.when(q == sc_ref[1])
            def _():
                own_ref[...] = ssum

    grid_spec = pltpu.PrefetchScalarGridSpec(
        num_scalar_prefetch=1, grid=(r // tr, 4),
        in_specs=[pl.BlockSpec((None, None, tr, n), lambda i, q, sc: (q, sc[0], i, 0)),
                  pl.BlockSpec((None, tr, n), lambda i, q, sc: (q, i, 0))] * na,
        out_specs=[pl.BlockSpec((None, tr, n), lambda i, q, sc: (q, i, 0)),
                   pl.BlockSpec((tr, n), lambda i, q, sc: (i, 0))] * na)
    outs = pl.pallas_call(
        body, name=name, grid_spec=grid_spec,
        out_shape=[jax.ShapeDtypeStruct((4, r, n), BF16), jax.ShapeDtypeStruct((r, n), F32)] * na,
        compiler_params=_params(("parallel", "arbitrary")),
    )(core_chip, *[x for pair in zip(g4s, h4s) for x in pair])
    return [(outs[2 * a], outs[2 * a + 1]) for a in range(na)]


def _sum4_adam(own, parts, w, m, v, tr, name, transposed):
    r, n = own.shape
    rows, cols = w.shape

    def body(o_ref, p_ref, w_ref, m_ref, v_ref, g_ref, d_ref, nm_ref, nv_ref):
        g = o_ref[...]
        for k in range(3):
            g = g + p_ref[k].astype(F32)
        if transposed:
            g = g.T
        g_ref[...] = g
        d_ref[...], nm_ref[...], nv_ref[...] = _adamw(w_ref[...], g, m_ref[...], v_ref[...])

    if transposed:
        g_specs = [pl.BlockSpec((r, tr), lambda i: (0, i)), pl.BlockSpec((3, r, tr), lambda i: (0, 0, i))]
    else:
        g_specs = [pl.BlockSpec((tr, n), lambda i: (i, 0)), pl.BlockSpec((3, tr, n), lambda i: (0, i, 0))]
    tile = pl.BlockSpec((tr, cols), lambda i: (i, 0))
    return pl.pallas_call(
        body, name=name, grid=(rows // tr,),
        in_specs=g_specs + [tile] * 3, out_specs=[tile] * 4,
        out_shape=[jax.ShapeDtypeStruct((rows, cols), F32)] * 4,
        compiler_params=_params(("parallel",)),
    )(own, parts, w, m, v)


def _sum8(parts, tr, name):
    _, rows, n = parts.shape

    def body(p_ref, o_ref):
        acc = p_ref[0]
        for k in range(1, N_DEV):
            acc = acc + p_ref[k]
        o_ref[...] = acc

    return pl.pallas_call(
        body, name=name, grid=(rows // tr,),
        in_specs=[pl.BlockSpec((N_DEV, tr, n), lambda i: (0, i, 0))],
        out_specs=pl.BlockSpec((tr, n), lambda i: (i, 0)),
        out_shape=jax.ShapeDtypeStruct((rows, n), F32),
        compiler_params=_params(("parallel",)),
    )(parts)


def _ada_bwd_adam(cact_t, dmod_cols, w, m, v, tr):
    rows, n = w.shape

    def body(c_ref, d_ref, w_ref, m_ref, v_ref, g_ref, dl_ref, nm_ref, nv_ref):
        def term(b):
            return c_ref[b].astype(BF16).astype(F32) * d_ref[b:b + 1, :].astype(BF16).astype(F32)

        g = term(0)
        for b in range(1, N_DEV):
            g = g + term(b)
        g_ref[...] = g
        dl_ref[...], nm_ref[...], nv_ref[...] = _adamw(w_ref[...], g, m_ref[...], v_ref[...])

    tile = pl.BlockSpec((tr, n), lambda i: (i, 0))
    return pl.pallas_call(
        body, name="ada_bwd_adam", grid=(rows // tr,),
        in_specs=[pl.BlockSpec((N_DEV, tr, 1), lambda i: (0, i, 0)), _full(dmod_cols.shape), tile, tile, tile],
        out_specs=[tile] * 4,
        out_shape=[jax.ShapeDtypeStruct((rows, n), F32)] * 4,
        compiler_params=_params(("parallel",)),
    )(cact_t, dmod_cols, w, m, v)


def _adam_small(ws, gs, ms, vs):
    n = len(ws)

    def body(*refs):
        w_r, g_r, m_r, v_r = refs[:n], refs[n:2 * n], refs[2 * n:3 * n], refs[3 * n:4 * n]
        d_r, nm_r, nv_r = refs[4 * n:5 * n], refs[5 * n:6 * n], refs[6 * n:7 * n]
        for k in range(n):
            d_r[k][...], nm_r[k][...], nv_r[k][...] = _adamw(w_r[k][...], g_r[k][...], m_r[k][...], v_r[k][...])

    shapes = [jax.ShapeDtypeStruct(w.shape, F32) for w in ws]
    outs = pl.pallas_call(
        body, name="adam_small", out_shape=shapes * 3, compiler_params=_params(),
    )(*ws, *gs, *ms, *vs)
    return outs[:n], outs[n:2 * n], outs[2 * n:]


def _block_diag(w):
    h, hd, _ = w.shape
    per = LANES // hd
    eye = jnp.eye(per, dtype=w.dtype)
    w5 = w.reshape(h // per, per, hd, 1, hd) * eye[None, :, None, :, None]
    return w5.reshape(h // per, LANES, LANES)


def _block_diag_grad(g, h, hd):
    per = LANES // hd
    g5 = g.reshape(h // per, per, hd, per, hd)
    return jnp.stack([g5[:, a, :, a, :] for a in range(per)], axis=1).reshape(h, hd, hd)


def kernel(x, c, w_ada, b_ada, g_mix, w_in, conv_w_sc, conv_w_lru, conv_b_lru, w_rg_a, b_rg_a, w_rg_x, b_rg_x, lru_lambda, w_out, g_mlp, w_up, w_down, g_final, loss_target, m_w_ada, m_b_ada, m_g_mix, m_w_in, m_conv_w_sc, m_conv_w_lru, m_conv_b_lru, m_w_rg_a, m_b_rg_a, m_w_rg_x, m_b_rg_x, m_lru_lambda, m_w_out, m_g_mlp, m_w_up, m_w_down, m_g_final, v_w_ada, v_b_ada, v_g_mix, v_w_in, v_conv_w_sc, v_conv_w_lru, v_conv_b_lru, v_w_rg_a, v_b_rg_a, v_w_rg_x, v_b_rg_x, v_lru_lambda, v_w_out, v_g_mlp, v_w_up, v_w_down, v_g_final):
    s, d = x.shape[1], x.shape[2]
    width = conv_b_lru.shape[1]
    heads, hd = w_rg_a.shape[1], w_rg_a.shape[2]
    f = w_down.shape[1] * N_DEV
    n_ada = w_ada.shape[2]
    csh = conv_w_sc.shape[2]
    me = 4 * lax.axis_index("x") + 2 * lax.axis_index("y") + lax.axis_index("c")
    tm = min(512, s)
    tm_mlp = min(1024, s)
    tk = 512

    x2d = x[0]
    tgt = loss_target[0]

    pay = jnp.zeros((SUBLANES, d), F32)
    pay = pay.at[0:1, :].set(c)
    pay = pay.at[1:4, 0:csh].set(conv_w_sc[0])
    pay = pay.at[4:8, 0:csh].set(conv_w_lru[0])
    w_in_t_sh = w_in[0].T.astype(BF16)
    w_up_t_sh = w_up[0].T.astype(BF16)
    w_out_sh = w_out[0].astype(BF16)
    w_down_sh = w_down[0].astype(BF16)
    pay_all, w_in_t = _gather2("gather_in", [pay, w_in_t_sh])
    w_in_t = w_in_t.reshape(-1, d)
    c_all = pay_all[:, 0, :]
    conv_sc = pay_all[:, 1:4, 0:csh].transpose(1, 0, 2).reshape(3, width)
    conv_lru = pay_all[:, 4:8, 0:csh].transpose(1, 0, 2).reshape(4, width)

    b_ada_sh = lax.dynamic_slice(b_ada, (0, me * n_ada), (1, n_ada))
    mod_cols, c_act = _ada_fwd(c_all, w_ada[0], b_ada_sh)
    (mod_rows,) = _exchange("scatter_mod", [], [mod_cols.reshape(N_DEV, 1, n_ada)])
    mod_rows, w_out_sh, w_up_t_sh, w_down_sh = lax.optimization_barrier((mod_rows, w_out_sh, w_up_t_sh, w_down_sh))
    (w_out_g,) = _seq_gather2("gather_w_out", 1, [w_out_sh])
    w_up_g, w_down_g = _seq_gather2("gather_mlp_weights", 2, [w_up_t_sh, w_down_sh])
    mod6 = jnp.zeros((SUBLANES, d), F32).at[0:6, :].set(mod_rows.reshape(6, d))

    wa_bd = _block_diag(w_rg_a[0]).astype(BF16)
    wx_bd = _block_diag(w_rg_x[0]).astype(BF16)
    ba = b_rg_a.reshape(1, width)
    bx = b_rg_x.reshape(1, width)
    g_fin = g_final.reshape(1, d)

    hn1, proj, ymix, h_all = _mix_in_mixer_fwd(x2d, mod6, g_mix, w_in_t, conv_sc, conv_lru, conv_b_lru,
                                               wa_bd, wx_bd, ba, bx, lru_lambda, width, tm)
    w_out_b = w_out_g.reshape(-1, d)
    mix, x2, hn2 = _mix_out_fwd(ymix, x2d, w_out_b, mod6, g_mlp, tm_mlp)
    w_up_t = w_up_g.reshape(-1, d)
    w_down_b = w_down_g.reshape(-1, d)
    z, dx3, dyb, st_fin = _mlp_fwd_loss(hn2, w_up_t, w_down_b, x2, tgt, mod6, g_fin, tm_mlp, 2 * tk)

    core_chip = jnp.stack([lax.axis_index("c"), 2 * lax.axis_index("x") + lax.axis_index("y")]).astype(jnp.int32)
    dz, dhn2 = _mlp_bwd_dx(dyb, z, w_down_b, w_up_t, tm_mlp, 2 * tk)
    g_down, g_up_t = _mlp_bwd_dw(z, dz, dyb, hn2, tm_mlp, 2 * tk)
    g_up4, g_down4 = g_up_t.reshape(4, 2, -1, d), g_down.reshape(4, 2, -1, d)
    h_up, h_down = _seq_pair_swap("swap_mlp_grads", 7, [g_up4, g_down4])
    dx2, dymix, g_out, st_out = _mix_out_bwd(dhn2, x2, dx3, mix, ymix, w_out_b, mod6, g_mlp, tm)
    h_up, h_down, g_out = lax.optimization_barrier((h_up, h_down, g_out))
    (sb_up, own_up), (sb_down, own_down) = _pair_sum([g_up4, g_down4], [h_up, h_down], core_chip, g_up4.shape[2], "pair_sum_mlp")
    g_out4 = g_out.reshape(4, 2, -1, d)
    (h_out,) = _seq_pair_swap("swap_w_out_grad", 8, [g_out4])
    p_up, p_down = _seq_chip_exchange("exchange_mlp_grads", 3, [sb_up, sb_down])
    dproj, g_small, g_wa, g_wx = _mixer_bwd(
        proj, dymix, h_all, conv_sc, conv_lru, conv_b_lru, wa_bd, wx_bd, ba, bx, lru_lambda, width)
    h_out, dproj = lax.optimization_barrier((h_out, dproj))
    ((sb_out, own_out),) = _pair_sum([g_out4], [h_out], core_chip, g_out4.shape[2], "pair_sum_w_out")
    (p_out,) = _seq_chip_exchange("exchange_w_out_grad", 4, [sb_out])
    grad_x, st_in = _mix_in_bwd_dx(dproj, x2d, dx2, w_in_t, mod6, g_mix, tm)

    small = jnp.concatenate([
        st_in[0:2], st_out[3:4], st_out[0:2], st_fin[1:2],
        st_in[2:3], st_out[2:3], st_fin[0:1],
        jnp.concatenate([g_small[7:8], g_small[10:11]], axis=1),
        jnp.concatenate([g_small[8:9], g_small[9:10]], axis=1),
        jnp.concatenate([jnp.concatenate([g_small[0:3], jnp.zeros((1, width), F32)], axis=0), g_small[3:7]], axis=1),
        st_fin[2:3],
        _block_diag_grad(g_wa, heads, hd).reshape(-1, d),
        _block_diag_grad(g_wx, heads, hd).reshape(-1, d),
    ], axis=0)

    (small_all,) = _seq_gather2("gather_small_grads", 5, [small])
    g_in_t, = _mix_in_bwd_dw(dproj, hn1, min(2048, s), dproj.shape[1] // 2)
    g_in4 = g_in_t.reshape(4, 2, -1, d)
    (h_in,) = _seq_pair_swap("swap_w_in_grad", 9, [g_in4])
    p_up, p_down, p_out, small_all, g_in_t = lax.optimization_barrier((p_up, p_down, p_out, small_all, g_in_t))

    ad_up = _sum4_adam(own_up, p_up, w_up[0], m_w_up[0], v_w_up[0], 256, "adam_w_up", True)
    h_in, ad_up = lax.optimization_barrier((h_in, ad_up))
    ((sb_in, own_in),) = _pair_sum([g_in4], [h_in], core_chip, g_in4.shape[2], "pair_sum_w_in")
    (p_in,) = _seq_chip_exchange("exchange_w_in_grad", 6, [sb_in])
    ad_out = _sum4_adam(own_out, p_out, w_out[0], m_w_out[0], v_w_out[0], w_out.shape[1], "adam_w_out", False)
    ad_down = _sum4_adam(own_down, p_down, w_down[0], m_w_down[0], v_w_down[0], 256, "adam_w_down", False)

    gsum = _sum8(small_all, SMALL_ROWS, "sum_small")
    loss = (0.5 / d) * jnp.sum(gsum[15])
    dmod_cols = lax.dynamic_slice(small_all[:, 0:6, :].reshape(N_DEV, 6 * d), (0, me * n_ada), (N_DEV, n_ada))
    g_ada, d_ada, nm_ada, nv_ada = _ada_bwd_adam(c_act[:, :, None], dmod_cols, w_ada[0], m_w_ada[0], v_w_ada[0], 256)

    g_conv = lax.dynamic_slice(gsum[11:15, 0:width], (0, me * csh), (4, csh))
    g_conv_l = lax.dynamic_slice(gsum[11:15, width:2 * width], (0, me * csh), (4, csh))
    small_g = [
        gsum[0:6].reshape(1, 6 * d),
        gsum[6:7],
        g_conv[0:3].reshape(1, 3, csh),
        g_conv_l.reshape(1, 4, csh),
        gsum[9:10, 0:width],
        gsum[16:48].reshape(1, heads, hd, hd),
        gsum[10:11, 0:width].reshape(1, heads, hd),
        gsum[48:80].reshape(1, heads, hd, hd),
        gsum[10:11, width:].reshape(1, heads, hd),
        gsum[9:10, width:],
        gsum[7:8],
        gsum[8],
    ]
    small_w = [b_ada, g_mix, conv_w_sc, conv_w_lru, conv_b_lru, w_rg_a, b_rg_a, w_rg_x, b_rg_x, lru_lambda, g_mlp, g_final]
    small_m = [m_b_ada, m_g_mix, m_conv_w_sc, m_conv_w_lru, m_conv_b_lru, m_w_rg_a, m_b_rg_a, m_w_rg_x, m_b_rg_x,
               m_lru_lambda, m_g_mlp, m_g_final]
    small_v = [v_b_ada, v_g_mix, v_conv_w_sc, v_conv_w_lru, v_conv_b_lru, v_w_rg_a, v_b_rg_a, v_w_rg_x, v_b_rg_x,
               v_lru_lambda, v_g_mlp, v_g_final]
    sd, snm, snv = _adam_small(small_w, small_g, small_m, small_v)
    p_in, ad_out, ad_down, (g_ada, d_ada, nm_ada, nv_ada), sd = lax.optimization_barrier(
        (p_in, ad_out, ad_down, (g_ada, d_ada, nm_ada, nv_ada), sd))
    ad_in = _sum4_adam(own_in, p_in, w_in[0].T, m_w_in[0].T, v_w_in[0].T, own_in.shape[0], "adam_w_in", False)
    ad_in = [a.T for a in ad_in]

    def order(ada, w_in_, w_out_, w_up_, w_down_, sm):
        return [ada[None], sm[0], sm[1], w_in_[None], sm[2], sm[3], sm[4], sm[5], sm[6], sm[7], sm[8], sm[9],
                w_out_[None], sm[10], w_up_[None], w_down_[None], sm[11]]

    grads = order(g_ada, ad_in[0], ad_out[0], ad_up[0], ad_down[0], small_g)
    deltas = order(d_ada, ad_in[1], ad_out[1], ad_up[1], ad_down[1], sd)
    new_m = order(nm_ada, ad_in[2], ad_out[2], ad_up[2], ad_down[2], snm)
    new_v = order(nv_ada, ad_in[3], ad_out[3], ad_up[3], ad_down[3], snv)
    return (loss, grad_x[None], *grads, *deltas, *new_m, *new_v)
```

```python
import jax
import jax.numpy as jnp
from jax import lax
from jax.experimental import pallas as pl
from jax.experimental.pallas import tpu as pltpu
from jax.experimental.pallas import tpu_sc as plsc

F32 = jnp.float32
BF16 = jnp.bfloat16
N_DEV = 8
EPS = 1e-6
RG_C = 8.0
GELU_K0 = 0.7978845608028654
GELU_K1 = 0.044715
ADAM_LR = 0.001
ADAM_B1 = 0.9
ADAM_B2 = 0.999
ADAM_EPS = 1e-08
ADAM_WD = 0.01
ADAM_STEP = 10
LANES = 128
SUBLANES = 8
VMEM_LIMIT = 52 * 1024 * 1024
VMEM_LIMIT_BIG = 58 * 1024 * 1024
MIX_ROWS = 256
SMALL_ROWS = 80

MESH = pl.DeviceIdType.MESH
ANY = pl.BlockSpec(memory_space=pl.ANY)
NN = ((1,), (0,))
NT = ((1,), (1,))
TN = ((0,), (0,))


def _dot(a, b, dims):
    return lax.dot_general(a, b, (dims, ((), ())), preferred_element_type=F32)


def _params(sem=None):
    return pltpu.CompilerParams(dimension_semantics=sem, vmem_limit_bytes=VMEM_LIMIT)


def _full(shape):
    nd = len(shape)
    return pl.BlockSpec(shape, lambda *_: (0,) * nd)


def _exchange(name, gathers, scatters):
    n_g = len(gathers)
    arrs = list(gathers) + list(scatters)
    n = len(arrs)
    out_shape = [jax.ShapeDtypeStruct((N_DEV,) + a.shape, a.dtype) for a in gathers]
    out_shape += [jax.ShapeDtypeStruct(a.shape, a.dtype) for a in scatters]

    def body(*refs):
        ins, outs = refs[:n], refs[n:2 * n]
        send_sems, recv_sems, local_sems = refs[2 * n:]
        x, y, c = lax.axis_index("x"), lax.axis_index("y"), lax.axis_index("c")
        me = 4 * x + 2 * y + c

        def src(a, dev):
            return ins[a] if a < n_g else ins[a].at[dev]

        def peer_of(k):
            px = 1 - x if (k >> 2) & 1 else x
            py = 1 - y if (k >> 1) & 1 else y
            pc = 1 - c if k & 1 else c
            return (px, py, pc), 4 * px + 2 * py + pc

        local = [pltpu.make_async_copy(src(a, me), outs[a].at[me], local_sems.at[a]) for a in range(n)]
        for cp in local:
            cp.start()
        sends = []
        for k in range(1, N_DEV):
            peer, pidx = peer_of(k)
            for a in range(n):
                cp = pltpu.make_async_remote_copy(
                    src_ref=src(a, pidx), dst_ref=outs[a].at[me],
                    send_sem=send_sems.at[a * (N_DEV - 1) + k - 1], recv_sem=recv_sems.at[a * (N_DEV - 1) + k - 1],
                    device_id=peer, device_id_type=MESH)
                cp.start()
                sends.append(cp)
        for k in range(1, N_DEV):
            peer, pidx = peer_of(k)
            for a in range(n):
                pltpu.make_async_remote_copy(
                    src_ref=src(a, pidx), dst_ref=outs[a].at[pidx],
                    send_sem=send_sems.at[a * (N_DEV - 1) + k - 1], recv_sem=recv_sems.at[a * (N_DEV - 1) + k - 1],
                    device_id=peer, device_id_type=MESH).wait_recv()
        for cp in sends:
            cp.wait_send()
        for cp in local:
            cp.wait()

    return pl.pallas_call(
        body, name=name, out_shape=out_shape,
        in_specs=[ANY] * n, out_specs=[ANY] * n,
        scratch_shapes=[pltpu.SemaphoreType.DMA((n * (N_DEV - 1),)),
                        pltpu.SemaphoreType.DMA((n * (N_DEV - 1),)),
                        pltpu.SemaphoreType.DMA((n,))],
    )(*arrs)


GATHER_SEMS = 7


def _gather_copies(ins, outs, send_sems, recv_sems, local_sems, x, y, c):
    n = len(ins)
    per = GATHER_SEMS
    sib = (x, y, 1 - c)
    xn, yn, dg = (1 - x, y), (x, 1 - y), (1 - x, 1 - y)
    fx, fy = x + (1 - c) * (1 - 2 * x), y + c * (1 - 2 * y)
    tx, ty = x + c * (1 - 2 * x), y + (1 - c) * (1 - 2 * y)

    def slot(a, px, py, pc):
        return outs[a].at[4 * px + 2 * py + pc]

    def copy(a, k, block, to, src=None):
        return pltpu.make_async_remote_copy(
            src_ref=slot(a, *block) if src is None else src, dst_ref=slot(a, *block),
            send_sem=send_sems.at[a * per + k], recv_sem=recv_sems.at[a * per + k],
            device_id=to, device_id_type=MESH)

    local = [pltpu.make_async_copy(ins[a], slot(a, x, y, c), local_sems.at[a]) for a in range(n)]
    for cp in local:
        cp.start()
    started = []
    for a in range(n):
        started += [copy(a, 1, (x, y, c), (*xn, c), src=ins[a]), copy(a, 2, (x, y, c), (*yn, c), src=ins[a])]
    for a in range(n):
        started.append(copy(a, 0, (x, y, c), sib, src=ins[a]))
    for cp in started:
        cp.start()
    for a in range(n):
        copy(a, 1, (*xn, c), (x, y, c)).wait_recv()
        copy(a, 2, (*yn, c), (x, y, c)).wait_recv()
        later = [copy(a, 3, (fx, fy, c), (tx, ty, c)), copy(a, 4, (*xn, c), sib), copy(a, 5, (*yn, c), sib)]
        for cp in later:
            cp.start()
        started += later
    for a in range(n):
        copy(a, 3, (*dg, c), (x, y, c)).wait_recv()
        cp = copy(a, 6, (*dg, c), sib)
        cp.start()
        started.append(cp)
    for a in range(n):
        copy(a, 0, sib, (x, y, c)).wait_recv()
        for k, chip in ((4, xn), (5, yn), (6, dg)):
            copy(a, k, (*chip, 1 - c), (x, y, c)).wait_recv()
    for cp in started:
        cp.wait_send()
    for cp in local:
        cp.wait()


def _gather2(name, arrs):
    n = len(arrs)
    per = GATHER_SEMS
    out_shape = [jax.ShapeDtypeStruct((N_DEV,) + a.shape, a.dtype) for a in arrs]

    def body(*refs):
        ins, outs = refs[:n], refs[n:2 * n]
        send_sems, recv_sems, local_sems = refs[2 * n:]
        x, y, c = lax.axis_index("x"), lax.axis_index("y"), lax.axis_index("c")
        _gather_copies(ins, outs, send_sems, recv_sems, local_sems, x, y, c)

    return pl.pallas_call(
        body, name=name, out_shape=out_shape,
        in_specs=[ANY] * n, out_specs=[ANY] * n,
        scratch_shapes=[pltpu.SemaphoreType.DMA((n * per,)), pltpu.SemaphoreType.DMA((n * per,)),
                        pltpu.SemaphoreType.DMA((n,))],
    )(*arrs)


def _seq_gather2(name, collective_id, arrs):
    n = len(arrs)
    per = GATHER_SEMS

    def body(*refs):
        ins, outs = refs[:n], refs[n:2 * n]
        send_sems, recv_sems, local_sems = refs[2 * n:]
        x, y, c = lax.axis_index("x"), lax.axis_index("y"), lax.axis_index("c")
        barrier = pltpu.get_barrier_semaphore()
        for peer in [(x, y, 1 - c), (1 - x, y, c), (x, 1 - y, c)]:
            pl.semaphore_signal(barrier, inc=1, device_id=peer, device_id_type=MESH)
        pl.semaphore_wait(barrier, 3)
        _gather_copies(ins, outs, send_sems, recv_sems, local_sems, x, y, c)

    return pl.kernel(
        body, out_type=[jax.ShapeDtypeStruct((N_DEV,) + a.shape, a.dtype) for a in arrs],
        mesh=plsc.ScalarSubcoreMesh(axis_name="seq", num_cores=1),
        scratch_types=[pltpu.SemaphoreType.DMA((n * per,)), pltpu.SemaphoreType.DMA((n * per,)),
                       pltpu.SemaphoreType.DMA((n,))],
        compiler_params=pltpu.CompilerParams(collective_id=collective_id), name=name,
    )(*arrs)


def _seq_chip_exchange(name, collective_id, arrs):
    n = len(arrs)

    def body(*refs):
        ins, outs = refs[:n], refs[n:2 * n]
        send_sems, recv_sems = refs[2 * n:]
        x, y, c = lax.axis_index("x"), lax.axis_index("y"), lax.axis_index("c")

        def peer(k):
            return (1 - x if (k >> 1) & 1 else x), (1 - y if k & 1 else y)

        barrier = pltpu.get_barrier_semaphore()
        for k in (1, 2, 3):
            pl.semaphore_signal(barrier, inc=1, device_id=(*peer(k), c), device_id_type=MESH)
        pl.semaphore_wait(barrier, 3)

        def copy(a, k):
            px, py = peer(k)
            return pltpu.make_async_remote_copy(
                src_ref=ins[a].at[2 * px + py], dst_ref=outs[a].at[k - 1],
                send_sem=send_sems.at[a * 3 + k - 1], recv_sem=recv_sems.at[a * 3 + k - 1],
                device_id=(px, py, c), device_id_type=MESH)

        cps = [copy(a, k) for a in range(n) for k in (1, 2, 3)]
        for cp in cps:
            cp.start()
        for cp in cps:
            cp.wait_recv()
        for cp in cps:
            cp.wait_send()

    return pl.kernel(
        body, out_type=[jax.ShapeDtypeStruct((3,) + a.shape[1:], a.dtype) for a in arrs],
        mesh=plsc.ScalarSubcoreMesh(axis_name="seq", num_cores=1),
        scratch_types=[pltpu.SemaphoreType.DMA((n * 3,)), pltpu.SemaphoreType.DMA((n * 3,))],
        compiler_params=pltpu.CompilerParams(collective_id=collective_id), name=name,
    )(*arrs)


def _seq_pair_swap(name, collective_id, arrs):
    n = len(arrs)

    def body(*refs):
        ins, outs = refs[:n], refs[n:2 * n]
        send_sems, recv_sems = refs[2 * n:]
        x, y, c = lax.axis_index("x"), lax.axis_index("y"), lax.axis_index("c")
        barrier = pltpu.get_barrier_semaphore()
        pl.semaphore_signal(barrier, inc=1, device_id=(x, y, 1 - c), device_id_type=MESH)
        pl.semaphore_wait(barrier, 1)

        def copy(a, q):
            return pltpu.make_async_remote_copy(
                src_ref=ins[a].at[q, 1 - c], dst_ref=outs[a].at[q],
                send_sem=send_sems.at[a * 4 + q], recv_sem=recv_sems.at[a * 4 + q],
                device_id=(x, y, 1 - c), device_id_type=MESH)

        cps = [copy(a, q) for a in range(n) for q in range(4)]
        for cp in cps:
            cp.start()
        for cp in cps:
            cp.wait_recv()
        for cp in cps:
            cp.wait_send()

    return pl.kernel(
        body, out_type=[jax.ShapeDtypeStruct((4,) + a.shape[2:], a.dtype) for a in arrs],
        mesh=plsc.ScalarSubcoreMesh(axis_name="seq", num_cores=1),
        scratch_types=[pltpu.SemaphoreType.DMA((n * 4,)), pltpu.SemaphoreType.DMA((n * 4,))],
        compiler_params=pltpu.CompilerParams(collective_id=collective_id), name=name,
    )(*arrs)


def _call(body, name, grid, in_specs, out_specs, out_shape, args, scratch=(), vmem=VMEM_LIMIT):
    return pl.pallas_call(
        body, name=name, grid=grid, in_specs=in_specs, out_specs=out_specs, out_shape=out_shape,
        scratch_shapes=list(scratch),
        compiler_params=pltpu.CompilerParams(dimension_semantics=("arbitrary",) * len(grid), vmem_limit_bytes=vmem),
    )(*args)


def _ada_fwd(c_all, w_ada_sh, b_ada_sh):
    nb, d = c_all.shape
    ncol = w_ada_sh.shape[1]

    def body(c_ref, w_ref, b_ref, mod_ref, cact_ref):
        cc = c_ref[...]
        ca = cc * jax.nn.sigmoid(cc)
        cact_ref[...] = ca
        mod_ref[...] = _dot(ca.astype(BF16), w_ref[...].astype(BF16), NN) + b_ref[...]

    return pl.pallas_call(
        body, name="ada_fwd",
        out_shape=[jax.ShapeDtypeStruct((nb, ncol), F32), jax.ShapeDtypeStruct((nb, d), F32)],
        compiler_params=_params(),
    )(c_all, w_ada_sh, b_ada_sh)


def _rms(xv):
    rstd = lax.rsqrt(jnp.mean(xv * xv, axis=-1, keepdims=True) + EPS)
    return xv * rstd, rstd


def _rms_bwd(dxhat, xhat, rstd):
    return rstd * (dxhat - xhat * jnp.mean(dxhat * xhat, axis=-1, keepdims=True))


def _colsum(v):
    return jnp.sum(v, axis=0, keepdims=True)


def _expm1(v, ev):
    series = v * (1.0 + v * (0.5 + v * (1.0 / 6.0 + v * (1.0 / 24.0 + v * (1.0 / 120.0)))))
    return jnp.where(jnp.abs(v) < 0.2, series, ev - 1.0)


def _softplus(v):
    return jnp.maximum(v, 0.0) + jnp.log1p(jnp.exp(-jnp.abs(v)))


def _gelu(v):
    t = jnp.tanh(v * (GELU_K0 + (GELU_K0 * GELU_K1) * (v * v)))
    return 0.5 * v * (1.0 + t), t


def _dgelu(v, t):
    return 0.5 * ((1.0 + t) + (v * (1.0 - t * t)) * (GELU_K0 + (3.0 * GELU_K0 * GELU_K1) * (v * v)))


def _scan_tile(a, b, x0, st, k0, reverse):
    t = a.shape[0]
    off = SUBLANES
    stage_a, stage_b = st.at[k0], st.at[k0 + 1]
    halo = slice(off + t, off + t + SUBLANES) if reverse else slice(0, SUBLANES)
    stage_a[halo, :] = jnp.ones((SUBLANES, a.shape[1]), F32)
    stage_b[halo, :] = jnp.zeros((SUBLANES, a.shape[1]), F32)
    s = 1
    while s < min(t, SUBLANES):
        stage_a[off:off + t, :] = a
        stage_b[off:off + t, :] = b
        at = off + s if reverse else off - s
        b = a * stage_b[at:at + t, :] + b
        a = a * stage_a[at:at + t, :]
        s *= 2
    while s < t:
        if reverse:
            b = jnp.concatenate([a[:t - s] * b[s:] + b[:t - s], b[t - s:]], axis=0)
            a = jnp.concatenate([a[:t - s] * a[s:], a[t - s:]], axis=0)
        else:
            b = jnp.concatenate([b[:s], a[s:] * b[:t - s] + b[s:]], axis=0)
            a = jnp.concatenate([a[:s], a[s:] * a[:t - s]], axis=0)
        s *= 2
    x = b + a * x0
    return x, (x[0:SUBLANES, :] if reverse else x[t - SUBLANES:t, :])


def _lru_gates(u, wa, wx, ba, bx, sp):
    ub = u.astype(BF16)
    r = jax.nn.sigmoid(_dot(ub, wa, NN) + ba)
    i = jax.nn.sigmoid(_dot(ub, wx, NN) + bx)
    log_a = (-RG_C * r) * sp
    a = jnp.exp(log_a)
    mult = jnp.sqrt(-_expm1(log_a, a) * (a + 1.0))
    return ub, r, i, a, mult


def _staged_shifts(stage, v, prev8, next8, downs, ups):
    t = v.shape[0]
    if prev8 is not None:
        stage[0:SUBLANES, :] = prev8
    stage[SUBLANES:SUBLANES + t, :] = v
    if next8 is not None:
        stage[SUBLANES + t:2 * SUBLANES + t, :] = next8
    return ([stage[SUBLANES - k:SUBLANES - k + t, :] for k in downs],
            [stage[SUBLANES + k:SUBLANES + k + t, :] for k in ups])


def _conv3(p, pp, w_ref, lo, stage):
    (p1, p2), _ = _staged_shifts(stage, p, pp, None, (1, 2), ())
    q = (w_ref[0:1, lo:lo + LANES] * p2 + w_ref[1:2, lo:lo + LANES] * p1) + w_ref[2:3, lo:lo + LANES] * p
    return q, p1, p2


def _conv4(xv, xp, w_ref, b_ref, lo, stage):
    (x1, x2, x3), _ = _staged_shifts(stage, xv, xp, None, (1, 2, 3), ())
    u = (((w_ref[0:1, lo:lo + LANES] * x3 + w_ref[1:2, lo:lo + LANES] * x2) + w_ref[2:3, lo:lo + LANES] * x1)
         + w_ref[3:4, lo:lo + LANES] * xv) + b_ref[:, lo:lo + LANES]
    return u, x1, x2, x3


def _mix_in_mixer_fwd(x2d, mod6, g_mix, w_in_t, conv_sc, conv_lru, conv_b, wa_bd, wx_bd, ba, bx, lam, width, tm):
    s, d = x2d.shape
    din = w_in_t.shape[0]
    nt = s // tm
    sub = min(MIX_ROWS, tm)
    nblk = width // LANES

    def body(x_ref, mod_ref, g_ref, w_ref, wsc_ref, wlru_ref, blru_ref, wa_ref, wx_ref, ba_ref, bx_ref, lam_ref,
             hn_ref, proj_ref, ymix_ref, h_ref, buf_ref, halo_ref, hc_ref, stage_ref):
        i = pl.program_id(0)

        @pl.when(i == 0)
        def _():
            buf_ref[1] = jnp.zeros((tm, din), F32)
            halo_ref[...] = jnp.zeros_like(halo_ref)

        @pl.when(i <= 1)
        def _():
            hc_ref[...] = jnp.zeros_like(hc_ref)

        def step(dst, src):
            xhat, _ = _rms(x_ref[...])
            hn = ((xhat * g_ref[...]) * (1.0 + mod_ref[1:2, :]) + mod_ref[0:1, :]).astype(BF16)
            hn_ref[...] = hn
            n_mix = (tm // sub) * nblk
            n_chunk = din // width

            def project(k):
                res = _dot(hn_ref[...], w_ref[k * width:(k + 1) * width, :], NT)
                proj_ref[:, k * width:(k + 1) * width] = res
                dst[:, k * width:(k + 1) * width] = res

            done = 0
            for half in range(tm // sub):
                r0 = half * sub
                rows = slice(r0, r0 + sub)
                for j in range(nblk):
                    lo = j * LANES
                    while done < n_chunk and done * n_mix <= (half * nblk + j) * n_chunk:
                        project(done)
                        done += 1

                    def col(p):
                        return src[rows, p * width + lo:p * width + lo + LANES]

                    def prev(p):
                        c0 = p * width + lo
                        if half == 0:
                            return halo_ref[:, c0:c0 + LANES]
                        return src[r0 - SUBLANES:r0, c0:c0 + LANES]

                    pp = col(1) * col(2)
                    q, _, _ = _conv3(pp, prev(1) * prev(2), wsc_ref, lo, stage_ref.at[0])
                    ymix_ref[rows, lo:lo + LANES] = (col(0) * q).astype(BF16)

                    u, _, _, _ = _conv4(col(4), prev(4), wlru_ref, blru_ref, lo, stage_ref.at[1])
                    sp = _softplus(-lam_ref[:, lo:lo + LANES])
                    _, r, ig, a, mult = _lru_gates(u, wa_ref[j], wx_ref[j], ba_ref[:, lo:lo + LANES],
                                                   bx_ref[:, lo:lo + LANES], sp)
                    h, ends = _scan_tile(a, mult * (ig * u), hc_ref[0:1, lo:lo + LANES], stage_ref, 2, False)
                    h_ref[rows, lo:lo + LANES] = h
                    hc_ref[0:1, lo:lo + LANES] = ends[SUBLANES - 1:SUBLANES, :]
                    gel, _ = _gelu(col(3))
                    ymix_ref[rows, width + lo:width + lo + LANES] = (gel * h).astype(BF16)
            while done < n_chunk:
                project(done)
                done += 1
            halo_ref[...] = src[tm - SUBLANES:tm, :]

        @pl.when(i % 2 == 0)
        def _():
            step(buf_ref.at[0], buf_ref.at[1])

        @pl.when(i % 2 == 1)
        def _():
            step(buf_ref.at[1], buf_ref.at[0])

    small = [conv_sc, conv_lru, conv_b, wa_bd, wx_bd, ba, bx, lam]
    cur = lambda i: (jnp.minimum(i, nt - 1), 0)
    last = lambda i: (jnp.maximum(i - 1, 0), 0)
    outs = _call(
        body, "mix_in_mixer_fwd", (nt + 1,),
        [pl.BlockSpec((tm, d), cur), _full(mod6.shape), _full(g_mix.shape), _full(w_in_t.shape)]
        + [_full(a.shape) for a in small],
        [pl.BlockSpec((tm, d), cur), pl.BlockSpec((tm, din), cur),
         pl.BlockSpec((tm, 2 * width), last), pl.BlockSpec((tm, width), last)],
        [jax.ShapeDtypeStruct((s, d), BF16), jax.ShapeDtypeStruct((s, din), F32),
         jax.ShapeDtypeStruct((s, 2 * width), BF16), jax.ShapeDtypeStruct((s, width), F32)],
        [x2d, mod6, g_mix, w_in_t, *small],
        scratch=[pltpu.VMEM((2, tm, din), F32), pltpu.VMEM((SUBLANES, din), F32), pltpu.VMEM((SUBLANES, width), F32),
                 pltpu.VMEM((4, sub + 2 * SUBLANES, LANES), F32)])
    return outs


def _mix_out_fwd(ymix, x2d, w_out, mod6, g_mlp, tm):
    s, d = x2d.shape

    def body(y_ref, x_ref, w_ref, mod_ref, g_ref, mix_ref, x2_ref, hn_ref):
        mix = _dot(y_ref[...], w_ref[...], NN)
        mix_ref[...] = mix.astype(BF16)
        x2 = x_ref[...] + mod_ref[2:3, :] * mix
        x2_ref[...] = x2
        xhat, _ = _rms(x2)
        hn_ref[...] = ((xhat * g_ref[...]) * (1.0 + mod_ref[4:5, :]) + mod_ref[3:4, :]).astype(BF16)

    tile = pl.BlockSpec((tm, d), lambda i: (i, 0))
    return _call(
        body, "mix_out_fwd", (s // tm,),
        [tile, tile, _full(w_out.shape), _full(mod6.shape), _full(g_mlp.shape)],
        [tile, tile, tile],
        [jax.ShapeDtypeStruct((s, d), BF16), jax.ShapeDtypeStruct((s, d), F32), jax.ShapeDtypeStruct((s, d), BF16)],
        [ymix, x2d, w_out, mod6, g_mlp])


def _mlp_fwd_loss(hn2, w_up_t, w_down, x2, target, mod6, g_final, tm, tk):
    s, d = hn2.shape
    f = w_up_t.shape[0]
    nk = f // tk

    def body(hn_ref, wu_ref, wd_ref, x2_hbm, t_hbm, mod_ref, g_ref, z_ref, dx3_ref, dyb_ref, st_ref,
             y_ref, x2_ref, t_ref, sems):
        i, k = pl.program_id(0), pl.program_id(1)

        def fetch():
            rows = pl.ds(pl.multiple_of(i * tm, tm), tm)
            return (pltpu.make_async_copy(x2_hbm.at[rows, :], x2_ref, sems.at[0]),
                    pltpu.make_async_copy(t_hbm.at[rows, :], t_ref, sems.at[1]))

        @pl.when(jnp.logical_and(i == 0, k == 0))
        def _():
            st_ref[...] = jnp.zeros_like(st_ref)

        @pl.when(k == 0)
        def _():
            for cp in fetch():
                cp.start()
            y_ref[...] = jnp.zeros_like(y_ref)

        z = jnp.maximum(_dot(hn_ref[...], wu_ref[...], NT), 0.0)
        z_ref[...] = z.astype(BF16)
        y_ref[...] += _dot((z * z).astype(BF16), wd_ref[...], NN)

        @pl.when(k == nk - 1)
        def _():
            for cp in fetch():
                cp.wait()
            gate = mod_ref[5:6, :]
            yv = y_ref[...]
            xhat, rstd = _rms(x2_ref[...] + gate * yv)
            diff = xhat * g_ref[...] - t_ref[...]
            dyo = diff * (1.0 / d)
            dx3 = _rms_bwd(dyo * g_ref[...], xhat, rstd)
            dx3_ref[...] = dx3.astype(BF16)
            dyb_ref[...] = (gate * dx3).astype(BF16)
            st_ref[0:1, :] += _colsum(dyo * xhat)
            st_ref[1:2, :] += _colsum(dx3 * yv)
            st_ref[2:3, :] += _colsum(diff * diff)

    tile = pl.BlockSpec((tm, d), lambda i, k: (i, 0))
    wblk = pl.BlockSpec((tk, d), lambda i, k: (k, 0))
    return pl.pallas_call(
        body, name="mlp_fwd_loss", grid=(s // tm, nk),
        in_specs=[tile, wblk, wblk, ANY, ANY, _full(mod6.shape), _full(g_final.shape)],
        out_specs=[pl.BlockSpec((tm, tk), lambda i, k: (i, k)), tile, tile, _full((SUBLANES, d))],
        out_shape=[jax.ShapeDtypeStruct((s, f), BF16), jax.ShapeDtypeStruct((s, d), BF16),
                   jax.ShapeDtypeStruct((s, d), BF16), jax.ShapeDtypeStruct((SUBLANES, d), F32)],
        scratch_shapes=[pltpu.VMEM((tm, d), F32), pltpu.VMEM((tm, d), F32), pltpu.VMEM((tm, d), F32),
                        pltpu.SemaphoreType.DMA((2,))],
        compiler_params=pltpu.CompilerParams(dimension_semantics=("arbitrary", "arbitrary"),
                                             vmem_limit_bytes=VMEM_LIMIT_BIG),
    )(hn2, w_up_t, w_down, x2, target, mod6, g_final)


def _mlp_bwd_dx(dyb, z, w_down, w_up_t, tm, tk):
    s, d = dyb.shape
    f = z.shape[1]

    nk = f // tk

    def body(dy_ref, z_ref, wd_ref, wu_ref, dz_ref, dh_ref, acc_ref):
        k = pl.program_id(1)

        @pl.when(k == 0)
        def _():
            acc_ref[...] = jnp.zeros_like(acc_ref)

        dz = ((2.0 * z_ref[...].astype(F32)) * _dot(dy_ref[...], wd_ref[...], NT)).astype(BF16)
        dz_ref[...] = dz
        acc_ref[...] += _dot(dz, wu_ref[...], NN)

        @pl.when(k == nk - 1)
        def _():
            dh_ref[...] = acc_ref[...].astype(BF16)

    return pl.pallas_call(
        body, name="mlp_bwd_dx", grid=(s // tm, nk),
        in_specs=[pl.BlockSpec((tm, d), lambda i, k: (i, 0)), pl.BlockSpec((tm, tk), lambda i, k: (i, k)),
                  pl.BlockSpec((tk, d), lambda i, k: (k, 0)), pl.BlockSpec((tk, d), lambda i, k: (k, 0))],
        out_specs=[pl.BlockSpec((tm, tk), lambda i, k: (i, k)), pl.BlockSpec((tm, d), lambda i, k: (i, 0))],
        out_shape=[jax.ShapeDtypeStruct((s, f), BF16), jax.ShapeDtypeStruct((s, d), BF16)],
        scratch_shapes=[pltpu.VMEM((tm, d), F32)],
        compiler_params=_params(("parallel", "arbitrary")),
    )(dyb, z, w_down, w_up_t)


def _mlp_bwd_dw(z, dz, dyb, hn2, tm, tk):
    s, d = dyb.shape
    f = z.shape[1]

    def body(z_ref, dz_ref, dy_ref, hn_ref, gd_ref, gu_ref):
        i = pl.program_id(1)

        @pl.when(i == 0)
        def _():
            gd_ref[...] = jnp.zeros_like(gd_ref)
            gu_ref[...] = jnp.zeros_like(gu_ref)

        zf = z_ref[...].astype(F32)
        gd_ref[...] += _dot((zf * zf).astype(BF16), dy_ref[...], TN)
        gu_ref[...] += _dot(dz_ref[...], hn_ref[...], TN)

    return pl.pallas_call(
        body, name="mlp_bwd_dw", grid=(f // tk, s // tm),
        in_specs=[pl.BlockSpec((tm, tk), lambda k, i: (i, k)), pl.BlockSpec((tm, tk), lambda k, i: (i, k)),
                  pl.BlockSpec((tm, d), lambda k, i: (i, 0)), pl.BlockSpec((tm, d), lambda k, i: (i, 0))],
        out_specs=[pl.BlockSpec((tk, d), lambda k, i: (k, 0)), pl.BlockSpec((tk, d), lambda k, i: (k, 0))],
        out_shape=[jax.ShapeDtypeStruct((f, d), F32), jax.ShapeDtypeStruct((f, d), F32)],
        compiler_params=_params(("parallel", "arbitrary")),
    )(z, dz, dyb, hn2)


def _mix_out_bwd(dhn2, x2, dx3, mix, ymix, w_out, mod6, g_mlp, tm):
    s, d = x2.shape

    def body(dh_ref, x2_ref, dx3_ref, mix_ref, y_ref, w_ref, mod_ref, g_ref, dx2_ref, dym_ref, gw_ref, st_ref):
        i = pl.program_id(0)

        @pl.when(i == 0)
        def _():
            st_ref[...] = jnp.zeros_like(st_ref)
            gw_ref[...] = jnp.zeros_like(gw_ref)

        dh = dh_ref[...].astype(F32)
        xhat, rstd = _rms(x2_ref[...])
        dn = dh * (1.0 + mod_ref[4:5, :])
        dx2 = dx3_ref[...].astype(F32) + _rms_bwd(dn * g_ref[...], xhat, rstd)
        dx2_ref[...] = dx2.astype(BF16)
        st_ref[0:1, :] += _colsum(dh)
        st_ref[1:2, :] += _colsum(dh * (xhat * g_ref[...]))
        st_ref[2:3, :] += _colsum(dn * xhat)
        st_ref[3:4, :] += _colsum(dx2 * mix_ref[...].astype(F32))
        dmix = (mod_ref[2:3, :] * dx2).astype(BF16)
        dym_ref[...] = _dot(dmix, w_ref[...], NT).astype(BF16)
        gw_ref[...] += _dot(y_ref[...], dmix, TN)

    tile = pl.BlockSpec((tm, d), lambda i: (i, 0))
    return _call(
        body, "mix_out_bwd", (s // tm,),
        [tile, tile, tile, tile, tile, _full(w_out.shape), _full(mod6.shape), _full(g_mlp.shape)],
        [tile, tile, _full((d, d)), _full((SUBLANES, d))],
        [jax.ShapeDtypeStruct((s, d), BF16), jax.ShapeDtypeStruct((s, d), BF16),
         jax.ShapeDtypeStruct((d, d), F32), jax.ShapeDtypeStruct((SUBLANES, d), F32)],
        [dhn2, x2, dx3, mix, ymix, w_out, mod6, g_mlp])


def _mixer_bwd(proj, dymix, h_all, conv_sc, conv_lru, conv_b, wa_bd, wx_bd, ba, bx, lam, width):
    s, din = proj.shape
    t = min(MIX_ROWS, s)
    nt = s // t
    nblk = width // LANES
    hb = t // SUBLANES
    last8 = s // SUBLANES - 1

    def body(proj_ref, projp_ref, projn_ref, dy_ref, dyn_ref, h_ref, hp_ref,
             wsc_ref, wlru_ref, blru_ref, wa_ref, wx_ref, ba_ref, bx_ref, lam_ref,
             dproj_ref, small_ref, gwa_ref, gwx_ref, an_ref, gn_ref, dun_ref, stage_ref):
        i = pl.program_id(0)

        @pl.when(i == 0)
        def _():
            small_ref[...] = jnp.zeros_like(small_ref)
            gwa_ref[...] = jnp.zeros_like(gwa_ref)
            gwx_ref[...] = jnp.zeros_like(gwx_ref)
            an_ref[...] = jnp.zeros_like(an_ref)
            gn_ref[...] = jnp.zeros_like(gn_ref)
            dun_ref[...] = jnp.zeros_like(dun_ref)

        has_prev = i < nt - 1
        has_next = i > 0
        for j in range(nblk):
            lo = j * LANES
            ls = slice(lo, lo + LANES)

            def col(p, ref=proj_ref):
                return ref[:, p * width + lo:p * width + lo + LANES]

            def prev(p):
                return jnp.where(has_prev, col(p, projp_ref), 0.0)

            def nxt(p):
                return jnp.where(has_next, col(p, projn_ref), 0.0)

            def add_row(r, v):
                small_ref[r:r + 1, ls] += _colsum(v)

            sc_b, sc_c, sc_x = col(0), col(1), col(2)
            p = sc_c * sc_x
            q, p1, p2 = _conv3(p, prev(1) * prev(2), wsc_ref, lo, stage_ref.at[0])
            dys = dy_ref[:, ls].astype(F32)
            dproj_ref[:, ls] = (dys * q).astype(BF16)
            dq = dys * sc_b
            dqn = jnp.where(has_next, dyn_ref[:, ls].astype(F32)[0:SUBLANES], 0.0) * nxt(0)
            _, (dq1, dq2) = _staged_shifts(stage_ref.at[1], dq, None, dqn, (), (1, 2))
            dp = (wsc_ref[2:3, ls] * dq + wsc_ref[1:2, ls] * dq1) + wsc_ref[0:1, ls] * dq2
            dproj_ref[:, width + lo:width + lo + LANES] = (dp * sc_x).astype(BF16)
            dproj_ref[:, 2 * width + lo:2 * width + lo + LANES] = (dp * sc_c).astype(BF16)
            add_row(0, dq * p2)
            add_row(1, dq * p1)
            add_row(2, dq * p)

            xv = col(4)
            u, x1, x2, x3 = _conv4(xv, prev(4), wlru_ref, blru_ref, lo, stage_ref.at[2])
            lam_v = lam_ref[:, ls]
            sp = _softplus(-lam_v)
            wa, wx = wa_ref[j], wx_ref[j]
            ub, r, ig, a, mult = _lru_gates(u, wa, wx, ba_ref[:, ls], bx_ref[:, ls], sp)
            iu = ig * u
            h = h_ref[:, ls]
            (hm1,), _ = _staged_shifts(stage_ref.at[3], h, jnp.where(has_prev, hp_ref[:, ls], 0.0), None, (1,), ())
            lyv = col(3)
            gel, th = _gelu(lyv)
            dyl = dy_ref[:, width + lo:width + lo + LANES].astype(F32)
            dproj_ref[:, 3 * width + lo:3 * width + lo + LANES] = (dyl * h * _dgelu(lyv, th)).astype(BF16)
            a_next = jnp.broadcast_to(an_ref[0:1, ls], (SUBLANES, LANES))
            _, (a_up,) = _staged_shifts(stage_ref.at[4], a, None, a_next, (), (1,))
            g, _ = _scan_tile(a_up, dyl * gel, gn_ref[0:1, ls], stage_ref, 5, True)
            an_ref[0:1, ls] = a[0:1, :]
            gn_ref[0:1, ls] = g[0:1, :]
            da = g * hm1
            dmult = g * iu
            diu = g * mult
            dlog_a = da * a - dmult * ((a * a) / mult)
            dpre_a = (dlog_a * (-RG_C * sp)) * (r * (1.0 - r))
            dpre_x = (diu * u) * (ig * (1.0 - ig))
            dab, dxb = dpre_a.astype(BF16), dpre_x.astype(BF16)
            du = diu * ig + _dot(dab, wa, NT) + _dot(dxb, wx, NT)
            gwa_ref[j] += _dot(ub, dab, TN)
            gwx_ref[j] += _dot(ub, dxb, TN)
            dun = dun_ref[:, ls]
            dun_ref[:, ls] = du[0:SUBLANES, :]
            _, (du1, du2, du3) = _staged_shifts(stage_ref.at[7], du, None, dun, (), (1, 2, 3))
            dlx = (((wlru_ref[3:4, ls] * du + wlru_ref[2:3, ls] * du1) + wlru_ref[1:2, ls] * du2)
                   + wlru_ref[0:1, ls] * du3)
            dproj_ref[:, 4 * width + lo:4 * width + lo + LANES] = dlx.astype(BF16)
            add_row(3, du * x3)
            add_row(4, du * x2)
            add_row(5, du * x1)
            add_row(6, du * xv)
            add_row(7, du)
            add_row(8, dpre_a)
            add_row(9, dpre_x)
            add_row(10, (dlog_a * (RG_C * r)) * jax.nn.sigmoid(-lam_v))

    small = [conv_sc, conv_lru, conv_b, wa_bd, wx_bd, ba, bx, lam]
    rev = lambda i: nt - 1 - i
    return _call(
        body, "mixer_bwd", (nt,),
        [pl.BlockSpec((t, din), lambda i: (rev(i), 0)),
         pl.BlockSpec((SUBLANES, din), lambda i: (jnp.maximum(rev(i) * hb - 1, 0), 0)),
         pl.BlockSpec((SUBLANES, din), lambda i: (jnp.minimum((rev(i) + 1) * hb, last8), 0)),
         pl.BlockSpec((t, 2 * width), lambda i: (rev(i), 0)),
         pl.BlockSpec((2 * SUBLANES, 2 * width), lambda i: (jnp.minimum((rev(i) + 1) * (hb // 2), last8 // 2), 0)),
         pl.BlockSpec((t, width), lambda i: (rev(i), 0)),
         pl.BlockSpec((SUBLANES, width), lambda i: (jnp.maximum(rev(i) * hb - 1, 0), 0))]
        + [_full(a.shape) for a in small],
        [pl.BlockSpec((t, din), lambda i: (rev(i), 0)), _full((2 * SUBLANES, width)),
         _full(wa_bd.shape), _full(wx_bd.shape)],
        [jax.ShapeDtypeStruct((s, din), BF16), jax.ShapeDtypeStruct((2 * SUBLANES, width), F32),
         jax.ShapeDtypeStruct(wa_bd.shape, F32), jax.ShapeDtypeStruct(wx_bd.shape, F32)],
        [proj, proj, proj, dymix, dymix, h_all, h_all, *small],
        scratch=[pltpu.VMEM((SUBLANES, width), F32), pltpu.VMEM((SUBLANES, width), F32),
                 pltpu.VMEM((SUBLANES, width), F32), pltpu.VMEM((8, t + 2 * SUBLANES, LANES), F32)])


def _mix_in_bwd_dx(dproj, x2d, dx2, w_in_t, mod6, g_mix, tm):
    s, d = x2d.shape
    din = dproj.shape[1]

    def body(dp_ref, x_ref, dx2_ref, w_ref, mod_ref, g_ref, gx_ref, st_ref):
        i = pl.program_id(0)

        @pl.when(i == 0)
        def _():
            st_ref[...] = jnp.zeros_like(st_ref)

        dh = _dot(dp_ref[...], w_ref[...], NN)
        xhat, rstd = _rms(x_ref[...])
        dn = dh * (1.0 + mod_ref[1:2, :])
        gx_ref[...] = dx2_ref[...].astype(F32) + _rms_bwd(dn * g_ref[...], xhat, rstd)
        st_ref[0:1, :] += _colsum(dh)
        st_ref[1:2, :] += _colsum(dh * (xhat * g_ref[...]))
        st_ref[2:3, :] += _colsum(dn * xhat)

    tile = pl.BlockSpec((tm, d), lambda i: (i, 0))
    return _call(
        body, "mix_in_bwd_dx", (s // tm,),
        [pl.BlockSpec((tm, din), lambda i: (i, 0)), tile, tile, _full(w_in_t.shape), _full(mod6.shape),
         _full(g_mix.shape)],
        [tile, _full((SUBLANES, d))],
        [jax.ShapeDtypeStruct((s, d), F32), jax.ShapeDtypeStruct((SUBLANES, d), F32)],
        [dproj, x2d, dx2, w_in_t, mod6, g_mix], vmem=VMEM_LIMIT_BIG)


def _mix_in_bwd_dw(dproj, hn1, tm, tn):
    s, d = hn1.shape
    din = dproj.shape[1]

    def body(dp_ref, hn_ref, gw_ref):
        i = pl.program_id(1)

        @pl.when(i == 0)
        def _():
            gw_ref[...] = jnp.zeros_like(gw_ref)

        gw_ref[...] += _dot(dp_ref[...], hn_ref[...], TN)

    return _call(
        body, "mix_in_bwd_dw", (din // tn, s // tm),
        [pl.BlockSpec((tm, tn), lambda p, i: (i, p)), pl.BlockSpec((tm, d), lambda p, i: (i, 0))],
        [pl.BlockSpec((tn, d), lambda p, i: (p, 0))],
        [jax.ShapeDtypeStruct((din, d), F32)],
        [dproj, hn1])


def _adamw(w, g, m, v):
    m = ADAM_B1 * m + (1.0 - ADAM_B1) * g
    v = ADAM_B2 * v + (1.0 - ADAM_B2) * (g * g)
    m_hat = m / (1.0 - ADAM_B1 ** ADAM_STEP)
    v_hat = v / (1.0 - ADAM_B2 ** ADAM_STEP)
    delta = -ADAM_LR * (m_hat / (jnp.sqrt(v_hat) + ADAM_EPS) + ADAM_WD * w)
    return delta, m, v


def _pair_sum(g4s, h4s, core_chip, tr, name):
    na = len(g4s)
    _, _, r, n = g4s[0].shape

    def body(sc_ref, *refs):
        q = pl.program_id(1)
        for a in range(na):
            g_ref, h_ref = refs[2 * a], refs[2 * a + 1]
            sb_ref, own_ref = refs[2 * na + 2 * a], refs[2 * na + 2 * a + 1]
            ssum = g_ref[...] + h_ref[...]
            sb_ref[...] = ssum.astype(BF16)

            @pl.when(q == sc_ref[1])
            def _():
                own_ref[...] = ssum

    grid_spec = pltpu.PrefetchScalarGridSpec(
        num_scalar_prefetch=1, grid=(r // tr, 4),
        in_specs=[pl.BlockSpec((None, None, tr, n), lambda i, q, sc: (q, sc[0], i, 0)),
                  pl.BlockSpec((None, tr, n), lambda i, q, sc: (q, i, 0))] * na,
        out_specs=[pl.BlockSpec((None, tr, n), lambda i, q, sc: (q, i, 0)),
                   pl.BlockSpec((tr, n), lambda i, q, sc: (i, 0))] * na)
    outs = pl.pallas_call(
        body, name=name, grid_spec=grid_spec,
        out_shape=[jax.ShapeDtypeStruct((4, r, n), BF16), jax.ShapeDtypeStruct((r, n), F32)] * na,
        compiler_params=_params(("parallel", "arbitrary")),
    )(core_chip, *[x for pair in zip(g4s, h4s) for x in pair])
    return [(outs[2 * a], outs[2 * a + 1]) for a in range(na)]


def _sum4_adam(own, parts, w, m, v, tr, name, transposed):
    r, n = own.shape
    rows, cols = w.shape

    def body(o_ref, p_ref, w_ref, m_ref, v_ref, g_ref, d_ref, nm_ref, nv_ref):
        g = o_ref[...]
        for k in range(3):
            g = g + p_ref[k].astype(F32)
        if transposed:
            g = g.T
        g_ref[...] = g
        d_ref[...], nm_ref[...], nv_ref[...] = _adamw(w_ref[...], g, m_ref[...], v_ref[...])

    if transposed:
        g_specs = [pl.BlockSpec((r, tr), lambda i: (0, i)), pl.BlockSpec((3, r, tr), lambda i: (0, 0, i))]
    else:
        g_specs = [pl.BlockSpec((tr, n), lambda i: (i, 0)), pl.BlockSpec((3, tr, n), lambda i: (0, i, 0))]
    tile = pl.BlockSpec((tr, cols), lambda i: (i, 0))
    return pl.pallas_call(
        body, name=name, grid=(rows // tr,),
        in_specs=g_specs + [tile] * 3, out_specs=[tile] * 4,
        out_shape=[jax.ShapeDtypeStruct((rows, cols), F32)] * 4,
        compiler_params=_params(("parallel",)),
    )(own, parts, w, m, v)


def _sum8(parts, tr, name):
    _, rows, n = parts.shape

    def body(p_ref, o_ref):
        acc = p_ref[0]
        for k in range(1, N_DEV):
            acc = acc + p_ref[k]
        o_ref[...] = acc

    return pl.pallas_call(
        body, name=name, grid=(rows // tr,),
        in_specs=[pl.BlockSpec((N_DEV, tr, n), lambda i: (0, i, 0))],
        out_specs=pl.BlockSpec((tr, n), lambda i: (i, 0)),
        out_shape=jax.ShapeDtypeStruct((rows, n), F32),
        compiler_params=_params(("parallel",)),
    )(parts)


def _ada_bwd_adam(cact_t, dmod_cols, w, m, v, tr):
    rows, n = w.shape

    def body(c_ref, d_ref, w_ref, m_ref, v_ref, g_ref, dl_ref, nm_ref, nv_ref):
        def term(b):
            return c_ref[b].astype(BF16).astype(F32) * d_ref[b:b + 1, :].astype(BF16).astype(F32)

        g = term(0)
        for b in range(1, N_DEV):
            g = g + term(b)
        g_ref[...] = g
        dl_ref[...], nm_ref[...], nv_ref[...] = _adamw(w_ref[...], g, m_ref[...], v_ref[...])

    tile = pl.BlockSpec((tr, n), lambda i: (i, 0))
    return pl.pallas_call(
        body, name="ada_bwd_adam", grid=(rows // tr,),
        in_specs=[pl.BlockSpec((N_DEV, tr, 1), lambda i: (0, i, 0)), _full(dmod_cols.shape), tile, tile, tile],
        out_specs=[tile] * 4,
        out_shape=[jax.ShapeDtypeStruct((rows, n), F32)] * 4,
        compiler_params=_params(("parallel",)),
    )(cact_t, dmod_cols, w, m, v)


def _adam_small(ws, gs, ms, vs):
    n = len(ws)

    def body(*refs):
        w_r, g_r, m_r, v_r = refs[:n], refs[n:2 * n], refs[2 * n:3 * n], refs[3 * n:4 * n]
        d_r, nm_r, nv_r = refs[4 * n:5 * n], refs[5 * n:6 * n], refs[6 * n:7 * n]
        for k in range(n):
            d_r[k][...], nm_r[k][...], nv_r[k][...] = _adamw(w_r[k][...], g_r[k][...], m_r[k][...], v_r[k][...])

    shapes = [jax.ShapeDtypeStruct(w.shape, F32) for w in ws]
    outs = pl.pallas_call(
        body, name="adam_small", out_shape=shapes * 3, compiler_params=_params(),
    )(*ws, *gs, *ms, *vs)
    return outs[:n], outs[n:2 * n], outs[2 * n:]


def _block_diag(w):
    h, hd, _ = w.shape
    per = LANES // hd
    eye = jnp.eye(per, dtype=w.dtype)
    w5 = w.reshape(h // per, per, hd, 1, hd) * eye[None, :, None, :, None]
    return w5.reshape(h // per, LANES, LANES)


def _block_diag_grad(g, h, hd):
    per = LANES // hd
    g5 = g.reshape(h // per, per, hd, per, hd)
    return jnp.stack([g5[:, a, :, a, :] for a in range(per)], axis=1).reshape(h, hd, hd)


def kernel(x, c, w_ada, b_ada, g_mix, w_in, conv_w_sc, conv_w_lru, conv_b_lru, w_rg_a, b_rg_a, w_rg_x, b_rg_x, lru_lambda, w_out, g_mlp, w_up, w_down, g_final, loss_target, m_w_ada, m_b_ada, m_g_mix, m_w_in, m_conv_w_sc, m_conv_w_lru, m_conv_b_lru, m_w_rg_a, m_b_rg_a, m_w_rg_x, m_b_rg_x, m_lru_lambda, m_w_out, m_g_mlp, m_w_up, m_w_down, m_g_final, v_w_ada, v_b_ada, v_g_mix, v_w_in, v_conv_w_sc, v_conv_w_lru, v_conv_b_lru, v_w_rg_a, v_b_rg_a, v_w_rg_x, v_b_rg_x, v_lru_lambda, v_w_out, v_g_mlp, v_w_up, v_w_down, v_g_final):
    s, d = x.shape[1], x.shape[2]
    width = conv_b_lru.shape[1]
    heads, hd = w_rg_a.shape[1], w_rg_a.shape[2]
    f = w_down.shape[1] * N_DEV
    n_ada = w_ada.shape[2]
    csh = conv_w_sc.shape[2]
    me = 4 * lax.axis_index("x") + 2 * lax.axis_index("y") + lax.axis_index("c")
    tm = min(512, s)
    tm_mlp = min(1024, s)
    tk = 512

    x2d = x[0]
    tgt = loss_target[0]

    pay = jnp.zeros((SUBLANES, d), F32)
    pay = pay.at[0:1, :].set(c)
    pay = pay.at[1:4, 0:csh].set(conv_w_sc[0])
    pay = pay.at[4:8, 0:csh].set(conv_w_lru[0])
    w_in_t_sh = w_in[0].T.astype(BF16)
    w_up_t_sh = w_up[0].T.astype(BF16)
    w_out_sh = w_out[0].astype(BF16)
    w_down_sh = w_down[0].astype(BF16)
    pay_all, w_in_t = _gather2("gather_in", [pay, w_in_t_sh])
    w_in_t = w_in_t.reshape(-1, d)
    c_all = pay_all[:, 0, :]
    conv_sc = pay_all[:, 1:4, 0:csh].transpose(1, 0, 2).reshape(3, width)
    conv_lru = pay_all[:, 4:8, 0:csh].transpose(1, 0, 2).reshape(4, width)

    b_ada_sh = lax.dynamic_slice(b_ada, (0, me * n_ada), (1, n_ada))
    mod_cols, c_act = _ada_fwd(c_all, w_ada[0], b_ada_sh)
    (mod_rows,) = _exchange("scatter_mod", [], [mod_cols.reshape(N_DEV, 1, n_ada)])
    mod_rows, w_out_sh, w_up_t_sh, w_down_sh = lax.optimization_barrier((mod_rows, w_out_sh, w_up_t_sh, w_down_sh))
    (w_out_g,) = _seq_gather2("gather_w_out", 1, [w_out_sh])
    w_up_g, w_down_g = _seq_gather2("gather_mlp_weights", 2, [w_up_t_sh, w_down_sh])
    mod6 = jnp.zeros((SUBLANES, d), F32).at[0:6, :].set(mod_rows.reshape(6, d))

    wa_bd = _block_diag(w_rg_a[0]).astype(BF16)
    wx_bd = _block_diag(w_rg_x[0]).astype(BF16)
    ba = b_rg_a.reshape(1, width)
    bx = b_rg_x.reshape(1, width)
    g_fin = g_final.reshape(1, d)

    hn1, proj, ymix, h_all = _mix_in_mixer_fwd(x2d, mod6, g_mix, w_in_t, conv_sc, conv_lru, conv_b_lru,
                                               wa_bd, wx_bd, ba, bx, lru_lambda, width, tm)
    w_out_b = w_out_g.reshape(-1, d)
    mix, x2, hn2 = _mix_out_fwd(ymix, x2d, w_out_b, mod6, g_mlp, tm_mlp)
    w_up_t = w_up_g.reshape(-1, d)
    w_down_b = w_down_g.reshape(-1, d)
    z, dx3, dyb, st_fin = _mlp_fwd_loss(hn2, w_up_t, w_down_b, x2, tgt, mod6, g_fin, tm_mlp, 2 * tk)

    core_chip = jnp.stack([lax.axis_index("c"), 2 * lax.axis_index("x") + lax.axis_index("y")]).astype(jnp.int32)
    dz, dhn2 = _mlp_bwd_dx(dyb, z, w_down_b, w_up_t, tm_mlp, 2 * tk)
    g_down, g_up_t = _mlp_bwd_dw(z, dz, dyb, hn2, tm_mlp, 2 * tk)
    g_up4, g_down4 = g_up_t.reshape(4, 2, -1, d), g_down.reshape(4, 2, -1, d)
    h_up, h_down = _seq_pair_swap("swap_mlp_grads", 7, [g_up4, g_down4])
    dx2, dymix, g_out, st_out = _mix_out_bwd(dhn2, x2, dx3, mix, ymix, w_out_b, mod6, g_mlp, tm)
    h_up, h_down, g_out = lax.optimization_barrier((h_up, h_down, g_out))
    (sb_up, own_up), (sb_down, own_down) = _pair_sum([g_up4, g_down4], [h_up, h_down], core_chip, g_up4.shape[2], "pair_sum_mlp")
    g_out4 = g_out.reshape(4, 2, -1, d)
    (h_out,) = _seq_pair_swap("swap_w_out_grad", 8, [g_out4])
    p_up, p_down = _seq_chip_exchange("exchange_mlp_grads", 3, [sb_up, sb_down])
    dproj, g_small, g_wa, g_wx = _mixer_bwd(
        proj, dymix, h_all, conv_sc, conv_lru, conv_b_lru, wa_bd, wx_bd, ba, bx, lru_lambda, width)
    h_out, dproj = lax.optimization_barrier((h_out, dproj))
    ((sb_out, own_out),) = _pair_sum([g_out4], [h_out], core_chip, g_out4.shape[2], "pair_sum_w_out")
    (p_out,) = _seq_chip_exchange("exchange_w_out_grad", 4, [sb_out])
    grad_x, st_in = _mix_in_bwd_dx(dproj, x2d, dx2, w_in_t, mod6, g_mix, tm_mlp)

    small = jnp.concatenate([
        st_in[0:2], st_out[3:4], st_out[0:2], st_fin[1:2],
        st_in[2:3], st_out[2:3], st_fin[0:1],
        jnp.concatenate([g_small[7:8], g_small[10:11]], axis=1),
        jnp.concatenate([g_small[8:9], g_small[9:10]], axis=1),
        jnp.concatenate([jnp.concatenate([g_small[0:3], jnp.zeros((1, width), F32)], axis=0), g_small[3:7]], axis=1),
        st_fin[2:3],
        _block_diag_grad(g_wa, heads, hd).reshape(-1, d),
        _block_diag_grad(g_wx, heads, hd).reshape(-1, d),
    ], axis=0)

    (small_all,) = _seq_gather2("gather_small_grads", 5, [small])
    g_in_t, = _mix_in_bwd_dw(dproj, hn1, min(2048, s), dproj.shape[1] // 2)
    g_in4 = g_in_t.reshape(4, 2, -1, d)
    (h_in,) = _seq_pair_swap("swap_w_in_grad", 9, [g_in4])
    p_up, p_down, p_out, small_all, g_in_t = lax.optimization_barrier((p_up, p_down, p_out, small_all, g_in_t))

    ad_up = _sum4_adam(own_up, p_up, w_up[0], m_w_up[0], v_w_up[0], 256, "adam_w_up", True)
    h_in, ad_up = lax.optimization_barrier((h_in, ad_up))
    ((sb_in, own_in),) = _pair_sum([g_in4], [h_in], core_chip, g_in4.shape[2], "pair_sum_w_in")
    (p_in,) = _seq_chip_exchange("exchange_w_in_grad", 6, [sb_in])
    ad_out = _sum4_adam(own_out, p_out, w_out[0], m_w_out[0], v_w_out[0], w_out.shape[1], "adam_w_out", False)
    ad_down = _sum4_adam(own_down, p_down, w_down[0], m_w_down[0], v_w_down[0], 256, "adam_w_down", False)

    gsum = _sum8(small_all, SMALL_ROWS, "sum_small")
    loss = (0.5 / d) * jnp.sum(gsum[15])
    dmod_cols = lax.dynamic_slice(small_all[:, 0:6, :].reshape(N_DEV, 6 * d), (0, me * n_ada), (N_DEV, n_ada))
    g_ada, d_ada, nm_ada, nv_ada = _ada_bwd_adam(c_act[:, :, None], dmod_cols, w_ada[0], m_w_ada[0], v_w_ada[0], 256)

    g_conv = lax.dynamic_slice(gsum[11:15, 0:width], (0, me * csh), (4, csh))
    g_conv_l = lax.dynamic_slice(gsum[11:15, width:2 * width], (0, me * csh), (4, csh))
    small_g = [
        gsum[0:6].reshape(1, 6 * d),
        gsum[6:7],
        g_conv[0:3].reshape(1, 3, csh),
        g_conv_l.reshape(1, 4, csh),
        gsum[9:10, 0:width],
        gsum[16:48].reshape(1, heads, hd, hd),
        gsum[10:11, 0:width].reshape(1, heads, hd),
        gsum[48:80].reshape(1, heads, hd, hd),
        gsum[10:11, width:].reshape(1, heads, hd),
        gsum[9:10, width:],
        gsum[7:8],
        gsum[8],
    ]
    small_w = [b_ada, g_mix, conv_w_sc, conv_w_lru, conv_b_lru, w_rg_a, b_rg_a, w_rg_x, b_rg_x, lru_lambda, g_mlp, g_final]
    small_m = [m_b_ada, m_g_mix, m_conv_w_sc, m_conv_w_lru, m_conv_b_lru, m_w_rg_a, m_b_rg_a, m_w_rg_x, m_b_rg_x,
               m_lru_lambda, m_g_mlp, m_g_final]
    small_v = [v_b_ada, v_g_mix, v_conv_w_sc, v_conv_w_lru, v_conv_b_lru, v_w_rg_a, v_b_rg_a, v_w_rg_x, v_b_rg_x,
               v_lru_lambda, v_g_mlp, v_g_final]
    sd, snm, snv = _adam_small(small_w, small_g, small_m, small_v)
    p_in, ad_out, ad_down, (g_ada, d_ada, nm_ada, nv_ada), sd = lax.optimization_barrier(
        (p_in, ad_out, ad_down, (g_ada, d_ada, nm_ada, nv_ada), sd))
    ad_in = _sum4_adam(own_in, p_in, w_in[0].T, m_w_in[0].T, v_w_in[0].T, own_in.shape[0], "adam_w_in", False)
    ad_in = [a.T for a in ad_in]

    def order(ada, w_in_, w_out_, w_up_, w_down_, sm):
        return [ada[None], sm[0], sm[1], w_in_[None], sm[2], sm[3], sm[4], sm[5], sm[6], sm[7], sm[8], sm[9],
                w_out_[None], sm[10], w_up_[None], w_down_[None], sm[11]]

    grads = order(g_ada, ad_in[0], ad_out[0], ad_up[0], ad_down[0], small_g)
    deltas = order(d_ada, ad_in[1], ad_out[1], ad_up[1], ad_down[1], sd)
    new_m = order(nm_ada, ad_in[2], ad_out[2], ad_up[2], ad_down[2], snm)
    new_v = order(nv_ada, ad_in[3], ad_out[3], ad_up[3], ad_down[3], snv)
    return (loss, grad_x[None], *grads, *deltas, *new_m, *new_v)
```

```python
import jax
import jax.numpy as jnp
from jax import lax
from jax.experimental import pallas as pl
from jax.experimental.pallas import tpu as pltpu
from jax.experimental.pallas import tpu_sc as plsc

F32 = jnp.float32
BF16 = jnp.bfloat16
N_DEV = 8
EPS = 1e-6
RG_C = 8.0
GELU_K0 = 0.7978845608028654
GELU_K1 = 0.044715
ADAM_LR = 0.001
ADAM_B1 = 0.9
ADAM_B2 = 0.999
ADAM_EPS = 1e-08
ADAM_WD = 0.01
ADAM_STEP = 10
LANES = 128
SUBLANES = 8
VMEM_LIMIT = 52 * 1024 * 1024
VMEM_LIMIT_BIG = 58 * 1024 * 1024
MIX_ROWS = 256
SMALL_ROWS = 80

MESH = pl.DeviceIdType.MESH
ANY = pl.BlockSpec(memory_space=pl.ANY)
NN = ((1,), (0,))
NT = ((1,), (1,))
TN = ((0,), (0,))


def _dot(a, b, dims):
    return lax.dot_general(a, b, (dims, ((), ())), preferred_element_type=F32)


def _params(sem=None):
    return pltpu.CompilerParams(dimension_semantics=sem, vmem_limit_bytes=VMEM_LIMIT)


def _full(shape):
    nd = len(shape)
    return pl.BlockSpec(shape, lambda *_: (0,) * nd)


def _exchange(name, gathers, scatters):
    n_g = len(gathers)
    arrs = list(gathers) + list(scatters)
    n = len(arrs)
    out_shape = [jax.ShapeDtypeStruct((N_DEV,) + a.shape, a.dtype) for a in gathers]
    out_shape += [jax.ShapeDtypeStruct(a.shape, a.dtype) for a in scatters]

    def body(*refs):
        ins, outs = refs[:n], refs[n:2 * n]
        send_sems, recv_sems, local_sems = refs[2 * n:]
        x, y, c = lax.axis_index("x"), lax.axis_index("y"), lax.axis_index("c")
        me = 4 * x + 2 * y + c

        def src(a, dev):
            return ins[a] if a < n_g else ins[a].at[dev]

        def peer_of(k):
            px = 1 - x if (k >> 2) & 1 else x
            py = 1 - y if (k >> 1) & 1 else y
            pc = 1 - c if k & 1 else c
            return (px, py, pc), 4 * px + 2 * py + pc

        local = [pltpu.make_async_copy(src(a, me), outs[a].at[me], local_sems.at[a]) for a in range(n)]
        for cp in local:
            cp.start()
        sends = []
        for k in range(1, N_DEV):
            peer, pidx = peer_of(k)
            for a in range(n):
                cp = pltpu.make_async_remote_copy(
                    src_ref=src(a, pidx), dst_ref=outs[a].at[me],
                    send_sem=send_sems.at[a * (N_DEV - 1) + k - 1], recv_sem=recv_sems.at[a * (N_DEV - 1) + k - 1],
                    device_id=peer, device_id_type=MESH)
                cp.start()
                sends.append(cp)
        for k in range(1, N_DEV):
            peer, pidx = peer_of(k)
            for a in range(n):
                pltpu.make_async_remote_copy(
                    src_ref=src(a, pidx), dst_ref=outs[a].at[pidx],
                    send_sem=send_sems.at[a * (N_DEV - 1) + k - 1], recv_sem=recv_sems.at[a * (N_DEV - 1) + k - 1],
                    device_id=peer, device_id_type=MESH).wait_recv()
        for cp in sends:
            cp.wait_send()
        for cp in local:
            cp.wait()

    return pl.pallas_call(
        body, name=name, out_shape=out_shape,
        in_specs=[ANY] * n, out_specs=[ANY] * n,
        scratch_shapes=[pltpu.SemaphoreType.DMA((n * (N_DEV - 1),)),
                        pltpu.SemaphoreType.DMA((n * (N_DEV - 1),)),
                        pltpu.SemaphoreType.DMA((n,))],
    )(*arrs)


GATHER_SEMS = 7


def _gather_copies(ins, outs, send_sems, recv_sems, local_sems, x, y, c):
    n = len(ins)
    per = GATHER_SEMS
    sib = (x, y, 1 - c)
    xn, yn, dg = (1 - x, y), (x, 1 - y), (1 - x, 1 - y)
    fx, fy = x + (1 - c) * (1 - 2 * x), y + c * (1 - 2 * y)
    tx, ty = x + c * (1 - 2 * x), y + (1 - c) * (1 - 2 * y)

    def slot(a, px, py, pc):
        return outs[a].at[4 * px + 2 * py + pc]

    def copy(a, k, block, to, src=None):
        return pltpu.make_async_remote_copy(
            src_ref=slot(a, *block) if src is None else src, dst_ref=slot(a, *block),
            send_sem=send_sems.at[a * per + k], recv_sem=recv_sems.at[a * per + k],
            device_id=to, device_id_type=MESH)

    local = [pltpu.make_async_copy(ins[a], slot(a, x, y, c), local_sems.at[a]) for a in range(n)]
    for cp in local:
        cp.start()
    started = []
    for a in range(n):
        started += [copy(a, 1, (x, y, c), (*xn, c), src=ins[a]), copy(a, 2, (x, y, c), (*yn, c), src=ins[a])]
    for a in range(n):
        started.append(copy(a, 0, (x, y, c), sib, src=ins[a]))
    for cp in started:
        cp.start()
    for a in range(n):
        copy(a, 1, (*xn, c), (x, y, c)).wait_recv()
        copy(a, 2, (*yn, c), (x, y, c)).wait_recv()
        later = [copy(a, 3, (fx, fy, c), (tx, ty, c)), copy(a, 4, (*xn, c), sib), copy(a, 5, (*yn, c), sib)]
        for cp in later:
            cp.start()
        started += later
    for a in range(n):
        copy(a, 3, (*dg, c), (x, y, c)).wait_recv()
        cp = copy(a, 6, (*dg, c), sib)
        cp.start()
        started.append(cp)
    for a in range(n):
        copy(a, 0, sib, (x, y, c)).wait_recv()
        for k, chip in ((4, xn), (5, yn), (6, dg)):
            copy(a, k, (*chip, 1 - c), (x, y, c)).wait_recv()
    for cp in started:
        cp.wait_send()
    for cp in local:
        cp.wait()


def _gather2(name, arrs):
    n = len(arrs)
    per = GATHER_SEMS
    out_shape = [jax.ShapeDtypeStruct((N_DEV,) + a.shape, a.dtype) for a in arrs]

    def body(*refs):
        ins, outs = refs[:n], refs[n:2 * n]
        send_sems, recv_sems, local_sems = refs[2 * n:]
        x, y, c = lax.axis_index("x"), lax.axis_index("y"), lax.axis_index("c")
        _gather_copies(ins, outs, send_sems, recv_sems, local_sems, x, y, c)

    return pl.pallas_call(
        body, name=name, out_shape=out_shape,
        in_specs=[ANY] * n, out_specs=[ANY] * n,
        scratch_shapes=[pltpu.SemaphoreType.DMA((n * per,)), pltpu.SemaphoreType.DMA((n * per,)),
                        pltpu.SemaphoreType.DMA((n,))],
    )(*arrs)


def _seq_gather2(name, collective_id, arrs):
    n = len(arrs)
    per = GATHER_SEMS

    def body(*refs):
        ins, outs = refs[:n], refs[n:2 * n]
        send_sems, recv_sems, local_sems = refs[2 * n:]
        x, y, c = lax.axis_index("x"), lax.axis_index("y"), lax.axis_index("c")
        barrier = pltpu.get_barrier_semaphore()
        for peer in [(x, y, 1 - c), (1 - x, y, c), (x, 1 - y, c)]:
            pl.semaphore_signal(barrier, inc=1, device_id=peer, device_id_type=MESH)
        pl.semaphore_wait(barrier, 3)
        _gather_copies(ins, outs, send_sems, recv_sems, local_sems, x, y, c)

    return pl.kernel(
        body, out_type=[jax.ShapeDtypeStruct((N_DEV,) + a.shape, a.dtype) for a in arrs],
        mesh=plsc.ScalarSubcoreMesh(axis_name="seq", num_cores=1),
        scratch_types=[pltpu.SemaphoreType.DMA((n * per,)), pltpu.SemaphoreType.DMA((n * per,)),
                       pltpu.SemaphoreType.DMA((n,))],
        compiler_params=pltpu.CompilerParams(collective_id=collective_id), name=name,
    )(*arrs)


def _seq_chip_exchange(name, collective_id, arrs):
    n = len(arrs)

    def body(*refs):
        ins, outs = refs[:n], refs[n:2 * n]
        send_sems, recv_sems = refs[2 * n:]
        x, y, c = lax.axis_index("x"), lax.axis_index("y"), lax.axis_index("c")

        def peer(k):
            return (1 - x if (k >> 1) & 1 else x), (1 - y if k & 1 else y)

        barrier = pltpu.get_barrier_semaphore()
        for k in (1, 2, 3):
            pl.semaphore_signal(barrier, inc=1, device_id=(*peer(k), c), device_id_type=MESH)
        pl.semaphore_wait(barrier, 3)

        def copy(a, k):
            px, py = peer(k)
            return pltpu.make_async_remote_copy(
                src_ref=ins[a].at[2 * px + py], dst_ref=outs[a].at[k - 1],
                send_sem=send_sems.at[a * 3 + k - 1], recv_sem=recv_sems.at[a * 3 + k - 1],
                device_id=(px, py, c), device_id_type=MESH)

        cps = [copy(a, k) for a in range(n) for k in (1, 2, 3)]
        for cp in cps:
            cp.start()
        for cp in cps:
            cp.wait_recv()
        for cp in cps:
            cp.wait_send()

    return pl.kernel(
        body, out_type=[jax.ShapeDtypeStruct((3,) + a.shape[1:], a.dtype) for a in arrs],
        mesh=plsc.ScalarSubcoreMesh(axis_name="seq", num_cores=1),
        scratch_types=[pltpu.SemaphoreType.DMA((n * 3,)), pltpu.SemaphoreType.DMA((n * 3,))],
        compiler_params=pltpu.CompilerParams(collective_id=collective_id), name=name,
    )(*arrs)


def _seq_pair_swap(name, collective_id, arrs):
    n = len(arrs)

    def body(*refs):
        ins, outs = refs[:n], refs[n:2 * n]
        send_sems, recv_sems = refs[2 * n:]
        x, y, c = lax.axis_index("x"), lax.axis_index("y"), lax.axis_index("c")
        barrier = pltpu.get_barrier_semaphore()
        pl.semaphore_signal(barrier, inc=1, device_id=(x, y, 1 - c), device_id_type=MESH)
        pl.semaphore_wait(barrier, 1)

        def copy(a, q):
            return pltpu.make_async_remote_copy(
                src_ref=ins[a].at[q, 1 - c], dst_ref=outs[a].at[q],
                send_sem=send_sems.at[a * 4 + q], recv_sem=recv_sems.at[a * 4 + q],
                device_id=(x, y, 1 - c), device_id_type=MESH)

        cps = [copy(a, q) for a in range(n) for q in range(4)]
        for cp in cps:
            cp.start()
        for cp in cps:
            cp.wait_recv()
        for cp in cps:
            cp.wait_send()

    return pl.kernel(
        body, out_type=[jax.ShapeDtypeStruct((4,) + a.shape[2:], a.dtype) for a in arrs],
        mesh=plsc.ScalarSubcoreMesh(axis_name="seq", num_cores=1),
        scratch_types=[pltpu.SemaphoreType.DMA((n * 4,)), pltpu.SemaphoreType.DMA((n * 4,))],
        compiler_params=pltpu.CompilerParams(collective_id=collective_id), name=name,
    )(*arrs)


def _call(body, name, grid, in_specs, out_specs, out_shape, args, scratch=(), vmem=VMEM_LIMIT):
    return pl.pallas_call(
        body, name=name, grid=grid, in_specs=in_specs, out_specs=out_specs, out_shape=out_shape,
        scratch_shapes=list(scratch),
        compiler_params=pltpu.CompilerParams(dimension_semantics=("arbitrary",) * len(grid), vmem_limit_bytes=vmem),
    )(*args)


def _ada_fwd(c_all, w_ada_sh, b_ada_sh):
    nb, d = c_all.shape
    ncol = w_ada_sh.shape[1]

    def body(c_ref, w_ref, b_ref, mod_ref, cact_ref):
        cc = c_ref[...]
        ca = cc * jax.nn.sigmoid(cc)
        cact_ref[...] = ca
        mod_ref[...] = _dot(ca.astype(BF16), w_ref[...].astype(BF16), NN) + b_ref[...]

    return pl.pallas_call(
        body, name="ada_fwd",
        out_shape=[jax.ShapeDtypeStruct((nb, ncol), F32), jax.ShapeDtypeStruct((nb, d), F32)],
        compiler_params=_params(),
    )(c_all, w_ada_sh, b_ada_sh)


def _rms(xv):
    rstd = lax.rsqrt(jnp.mean(xv * xv, axis=-1, keepdims=True) + EPS)
    return xv * rstd, rstd


def _rms_bwd(dxhat, xhat, rstd):
    return rstd * (dxhat - xhat * jnp.mean(dxhat * xhat, axis=-1, keepdims=True))


def _colsum(v):
    return jnp.sum(v, axis=0, keepdims=True)


def _expm1(v, ev):
    series = v * (1.0 + v * (0.5 + v * (1.0 / 6.0 + v * (1.0 / 24.0 + v * (1.0 / 120.0)))))
    return jnp.where(jnp.abs(v) < 0.2, series, ev - 1.0)


def _softplus(v):
    return jnp.maximum(v, 0.0) + jnp.log1p(jnp.exp(-jnp.abs(v)))


def _gelu(v):
    t = jnp.tanh(v * (GELU_K0 + (GELU_K0 * GELU_K1) * (v * v)))
    return 0.5 * v * (1.0 + t), t


def _dgelu(v, t):
    return 0.5 * ((1.0 + t) + (v * (1.0 - t * t)) * (GELU_K0 + (3.0 * GELU_K0 * GELU_K1) * (v * v)))


def _scan_tile(a, b, x0, st, k0, reverse):
    t = a.shape[0]
    off = SUBLANES
    stage_a, stage_b = st.at[k0], st.at[k0 + 1]
    halo = slice(off + t, off + t + SUBLANES) if reverse else slice(0, SUBLANES)
    stage_a[halo, :] = jnp.ones((SUBLANES, a.shape[1]), F32)
    stage_b[halo, :] = jnp.zeros((SUBLANES, a.shape[1]), F32)
    s = 1
    while s < min(t, SUBLANES):
        stage_a[off:off + t, :] = a
        stage_b[off:off + t, :] = b
        at = off + s if reverse else off - s
        b = a * stage_b[at:at + t, :] + b
        a = a * stage_a[at:at + t, :]
        s *= 2
    while s < t:
        if reverse:
            b = jnp.concatenate([a[:t - s] * b[s:] + b[:t - s], b[t - s:]], axis=0)
            a = jnp.concatenate([a[:t - s] * a[s:], a[t - s:]], axis=0)
        else:
            b = jnp.concatenate([b[:s], a[s:] * b[:t - s] + b[s:]], axis=0)
            a = jnp.concatenate([a[:s], a[s:] * a[:t - s]], axis=0)
        s *= 2
    x = b + a * x0
    return x, (x[0:SUBLANES, :] if reverse else x[t - SUBLANES:t, :])


def _lru_gates(u, wa, wx, ba, bx, sp):
    ub = u.astype(BF16)
    r = jax.nn.sigmoid(_dot(ub, wa, NN) + ba)
    i = jax.nn.sigmoid(_dot(ub, wx, NN) + bx)
    log_a = (-RG_C * r) * sp
    a = jnp.exp(log_a)
    mult = jnp.sqrt(-_expm1(log_a, a) * (a + 1.0))
    return ub, r, i, a, mult


def _staged_shifts(stage, v, prev8, next8, downs, ups):
    t = v.shape[0]
    if prev8 is not None:
        stage[0:SUBLANES, :] = prev8
    stage[SUBLANES:SUBLANES + t, :] = v
    if next8 is not None:
        stage[SUBLANES + t:2 * SUBLANES + t, :] = next8
    return ([stage[SUBLANES - k:SUBLANES - k + t, :] for k in downs],
            [stage[SUBLANES + k:SUBLANES + k + t, :] for k in ups])


def _conv3(p, pp, w_ref, lo, stage):
    (p1, p2), _ = _staged_shifts(stage, p, pp, None, (1, 2), ())
    q = (w_ref[0:1, lo:lo + LANES] * p2 + w_ref[1:2, lo:lo + LANES] * p1) + w_ref[2:3, lo:lo + LANES] * p
    return q, p1, p2


def _conv4(xv, xp, w_ref, b_ref, lo, stage):
    (x1, x2, x3), _ = _staged_shifts(stage, xv, xp, None, (1, 2, 3), ())
    u = (((w_ref[0:1, lo:lo + LANES] * x3 + w_ref[1:2, lo:lo + LANES] * x2) + w_ref[2:3, lo:lo + LANES] * x1)
         + w_ref[3:4, lo:lo + LANES] * xv) + b_ref[:, lo:lo + LANES]
    return u, x1, x2, x3


def _mix_in_mixer_fwd(x2d, mod6, g_mix, w_in_t, conv_sc, conv_lru, conv_b, wa_bd, wx_bd, ba, bx, lam, width, tm):
    s, d = x2d.shape
    din = w_in_t.shape[0]
    nt = s // tm
    sub = min(MIX_ROWS, tm)
    nblk = width // LANES

    def body(x_ref, mod_ref, g_ref, w_ref, wsc_ref, wlru_ref, blru_ref, wa_ref, wx_ref, ba_ref, bx_ref, lam_ref,
             hn_ref, proj_ref, ymix_ref, h_ref, buf_ref, halo_ref, hc_ref, stage_ref):
        i = pl.program_id(0)

        @pl.when(i == 0)
        def _():
            buf_ref[1] = jnp.zeros((tm, din), F32)
            halo_ref[...] = jnp.zeros_like(halo_ref)

        @pl.when(i <= 1)
        def _():
            hc_ref[...] = jnp.zeros_like(hc_ref)

        def step(dst, src):
            xhat, _ = _rms(x_ref[...])
            hn = ((xhat * g_ref[...]) * (1.0 + mod_ref[1:2, :]) + mod_ref[0:1, :]).astype(BF16)
            hn_ref[...] = hn
            n_mix = (tm // sub) * nblk
            n_chunk = din // width

            def project(k):
                res = _dot(hn_ref[...], w_ref[k * width:(k + 1) * width, :], NT)
                proj_ref[:, k * width:(k + 1) * width] = res
                dst[:, k * width:(k + 1) * width] = res

            done = 0
            for half in range(tm // sub):
                r0 = half * sub
                rows = slice(r0, r0 + sub)
                for j in range(nblk):
                    lo = j * LANES
                    while done < n_chunk and done * n_mix <= (half * nblk + j) * n_chunk:
                        project(done)
                        done += 1

                    def col(p):
                        return src[rows, p * width + lo:p * width + lo + LANES]

                    def prev(p):
                        c0 = p * width + lo
                        if half == 0:
                            return halo_ref[:, c0:c0 + LANES]
                        return src[r0 - SUBLANES:r0, c0:c0 + LANES]

                    pp = col(1) * col(2)
                    q, _, _ = _conv3(pp, prev(1) * prev(2), wsc_ref, lo, stage_ref.at[0])
                    ymix_ref[rows, lo:lo + LANES] = (col(0) * q).astype(BF16)

                    u, _, _, _ = _conv4(col(4), prev(4), wlru_ref, blru_ref, lo, stage_ref.at[1])
                    sp = _softplus(-lam_ref[:, lo:lo + LANES])
                    _, r, ig, a, mult = _lru_gates(u, wa_ref[j], wx_ref[j], ba_ref[:, lo:lo + LANES],
                                                   bx_ref[:, lo:lo + LANES], sp)
                    h, ends = _scan_tile(a, mult * (ig * u), hc_ref[0:1, lo:lo + LANES], stage_ref, 2, False)
                    h_ref[rows, lo:lo + LANES] = h
                    hc_ref[0:1, lo:lo + LANES] = ends[SUBLANES - 1:SUBLANES, :]
                    gel, _ = _gelu(col(3))
                    ymix_ref[rows, width + lo:width + lo + LANES] = (gel * h).astype(BF16)
            while done < n_chunk:
                project(done)
                done += 1
            halo_ref[...] = src[tm - SUBLANES:tm, :]

        @pl.when(i % 2 == 0)
        def _():
            step(buf_ref.at[0], buf_ref.at[1])

        @pl.when(i % 2 == 1)
        def _():
            step(buf_ref.at[1], buf_ref.at[0])

    small = [conv_sc, conv_lru, conv_b, wa_bd, wx_bd, ba, bx, lam]
    cur = lambda i: (jnp.minimum(i, nt - 1), 0)
    last = lambda i: (jnp.maximum(i - 1, 0), 0)
    outs = _call(
        body, "mix_in_mixer_fwd", (nt + 1,),
        [pl.BlockSpec((tm, d), cur), _full(mod6.shape), _full(g_mix.shape), _full(w_in_t.shape)]
        + [_full(a.shape) for a in small],
        [pl.BlockSpec((tm, d), cur), pl.BlockSpec((tm, din), cur),
         pl.BlockSpec((tm, 2 * width), last), pl.BlockSpec((tm, width), last)],
        [jax.ShapeDtypeStruct((s, d), BF16), jax.ShapeDtypeStruct((s, din), F32),
         jax.ShapeDtypeStruct((s, 2 * width), BF16), jax.ShapeDtypeStruct((s, width), F32)],
        [x2d, mod6, g_mix, w_in_t, *small],
        scratch=[pltpu.VMEM((2, tm, din), F32), pltpu.VMEM((SUBLANES, din), F32), pltpu.VMEM((SUBLANES, width), F32),
                 pltpu.VMEM((4, sub + 2 * SUBLANES, LANES), F32)])
    return outs


def _mix_out_fwd(ymix, x2d, w_out, mod6, g_mlp, tm):
    s, d = x2d.shape

    def body(y_ref, x_ref, w_ref, mod_ref, g_ref, mix_ref, x2_ref, hn_ref):
        mix = _dot(y_ref[...], w_ref[...], NN)
        mix_ref[...] = mix.astype(BF16)
        x2 = x_ref[...] + mod_ref[2:3, :] * mix
        x2_ref[...] = x2
        xhat, _ = _rms(x2)
        hn_ref[...] = ((xhat * g_ref[...]) * (1.0 + mod_ref[4:5, :]) + mod_ref[3:4, :]).astype(BF16)

    tile = pl.BlockSpec((tm, d), lambda i: (i, 0))
    return _call(
        body, "mix_out_fwd", (s // tm,),
        [tile, tile, _full(w_out.shape), _full(mod6.shape), _full(g_mlp.shape)],
        [tile, tile, tile],
        [jax.ShapeDtypeStruct((s, d), BF16), jax.ShapeDtypeStruct((s, d), F32), jax.ShapeDtypeStruct((s, d), BF16)],
        [ymix, x2d, w_out, mod6, g_mlp])


def _mlp_fwd_loss(hn2, w_up_t, w_down, x2, target, mod6, g_final, tm, tk):
    s, d = hn2.shape
    f = w_up_t.shape[0]
    nk = f // tk

    def body(hn_ref, wu_ref, wd_ref, x2_hbm, t_hbm, mod_ref, g_ref, z_ref, dx3_ref, dyb_ref, st_ref,
             y_ref, x2_ref, t_ref, sems):
        i, k = pl.program_id(0), pl.program_id(1)

        def fetch():
            rows = pl.ds(pl.multiple_of(i * tm, tm), tm)
            return (pltpu.make_async_copy(x2_hbm.at[rows, :], x2_ref, sems.at[0]),
                    pltpu.make_async_copy(t_hbm.at[rows, :], t_ref, sems.at[1]))

        @pl.when(jnp.logical_and(i == 0, k == 0))
        def _():
            st_ref[...] = jnp.zeros_like(st_ref)

        @pl.when(k == 0)
        def _():
            for cp in fetch():
                cp.start()
            y_ref[...] = jnp.zeros_like(y_ref)

        z = jnp.maximum(_dot(hn_ref[...], wu_ref[...], NT), 0.0)
        z_ref[...] = z.astype(BF16)
        y_ref[...] += _dot((z * z).astype(BF16), wd_ref[...], NN)

        @pl.when(k == nk - 1)
        def _():
            for cp in fetch():
                cp.wait()
            gate = mod_ref[5:6, :]
            yv = y_ref[...]
            xhat, rstd = _rms(x2_ref[...] + gate * yv)
            diff = xhat * g_ref[...] - t_ref[...]
            dyo = diff * (1.0 / d)
            dx3 = _rms_bwd(dyo * g_ref[...], xhat, rstd)
            dx3_ref[...] = dx3.astype(BF16)
            dyb_ref[...] = (gate * dx3).astype(BF16)
            st_ref[0:1, :] += _colsum(dyo * xhat)
            st_ref[1:2, :] += _colsum(dx3 * yv)
            st_ref[2:3, :] += _colsum(diff * diff)

    tile = pl.BlockSpec((tm, d), lambda i, k: (i, 0))
    wblk = pl.BlockSpec((tk, d), lambda i, k: (k, 0))
    return pl.pallas_call(
        body, name="mlp_fwd_loss", grid=(s // tm, nk),
        in_specs=[tile, wblk, wblk, ANY, ANY, _full(mod6.shape), _full(g_final.shape)],
        out_specs=[pl.BlockSpec((tm, tk), lambda i, k: (i, k)), tile, tile, _full((SUBLANES, d))],
        out_shape=[jax.ShapeDtypeStruct((s, f), BF16), jax.ShapeDtypeStruct((s, d), BF16),
                   jax.ShapeDtypeStruct((s, d), BF16), jax.ShapeDtypeStruct((SUBLANES, d), F32)],
        scratch_shapes=[pltpu.VMEM((tm, d), F32), pltpu.VMEM((tm, d), F32), pltpu.VMEM((tm, d), F32),
                        pltpu.SemaphoreType.DMA((2,))],
        compiler_params=pltpu.CompilerParams(dimension_semantics=("arbitrary", "arbitrary"),
                                             vmem_limit_bytes=VMEM_LIMIT_BIG),
    )(hn2, w_up_t, w_down, x2, target, mod6, g_final)


def _mlp_bwd_dx(dyb, z, w_down, w_up_t, tm, tk):
    s, d = dyb.shape
    f = z.shape[1]

    nk = f // tk

    def body(dy_ref, z_ref, wd_ref, wu_ref, dz_ref, dh_ref, acc_ref):
        k = pl.program_id(1)

        @pl.when(k == 0)
        def _():
            acc_ref[...] = jnp.zeros_like(acc_ref)

        dz = ((2.0 * z_ref[...].astype(F32)) * _dot(dy_ref[...], wd_ref[...], NT)).astype(BF16)
        dz_ref[...] = dz
        acc_ref[...] += _dot(dz, wu_ref[...], NN)

        @pl.when(k == nk - 1)
        def _():
            dh_ref[...] = acc_ref[...].astype(BF16)

    return pl.pallas_call(
        body, name="mlp_bwd_dx", grid=(s // tm, nk),
        in_specs=[pl.BlockSpec((tm, d), lambda i, k: (i, 0)), pl.BlockSpec((tm, tk), lambda i, k: (i, k)),
                  pl.BlockSpec((tk, d), lambda i, k: (k, 0)), pl.BlockSpec((tk, d), lambda i, k: (k, 0))],
        out_specs=[pl.BlockSpec((tm, tk), lambda i, k: (i, k)), pl.BlockSpec((tm, d), lambda i, k: (i, 0))],
        out_shape=[jax.ShapeDtypeStruct((s, f), BF16), jax.ShapeDtypeStruct((s, d), BF16)],
        scratch_shapes=[pltpu.VMEM((tm, d), F32)],
        compiler_params=_params(("parallel", "arbitrary")),
    )(dyb, z, w_down, w_up_t)


def _mlp_bwd_dw(z, dz, dyb, hn2, tm, tk):
    s, d = dyb.shape
    f = z.shape[1]

    def body(z_ref, dz_ref, dy_ref, hn_ref, gd_ref, gu_ref):
        i = pl.program_id(1)

        @pl.when(i == 0)
        def _():
            gd_ref[...] = jnp.zeros_like(gd_ref)
            gu_ref[...] = jnp.zeros_like(gu_ref)

        zf = z_ref[...].astype(F32)
        gd_ref[...] += _dot((zf * zf).astype(BF16), dy_ref[...], TN)
        gu_ref[...] += _dot(dz_ref[...], hn_ref[...], TN)

    return pl.pallas_call(
        body, name="mlp_bwd_dw", grid=(f // tk, s // tm),
        in_specs=[pl.BlockSpec((tm, tk), lambda k, i: (i, k)), pl.BlockSpec((tm, tk), lambda k, i: (i, k)),
                  pl.BlockSpec((tm, d), lambda k, i: (i, 0)), pl.BlockSpec((tm, d), lambda k, i: (i, 0))],
        out_specs=[pl.BlockSpec((tk, d), lambda k, i: (k, 0)), pl.BlockSpec((tk, d), lambda k, i: (k, 0))],
        out_shape=[jax.ShapeDtypeStruct((f, d), F32), jax.ShapeDtypeStruct((f, d), F32)],
        compiler_params=_params(("parallel", "arbitrary")),
    )(z, dz, dyb, hn2)


def _mix_out_bwd(dhn2, x2, dx3, mix, ymix, w_out, mod6, g_mlp, tm):
    s, d = x2.shape

    def body(dh_ref, x2_ref, dx3_ref, mix_ref, y_ref, w_ref, mod_ref, g_ref, dx2_ref, dym_ref, gw_ref, st_ref):
        i = pl.program_id(0)

        @pl.when(i == 0)
        def _():
            st_ref[...] = jnp.zeros_like(st_ref)
            gw_ref[...] = jnp.zeros_like(gw_ref)

        dh = dh_ref[...].astype(F32)
        xhat, rstd = _rms(x2_ref[...])
        dn = dh * (1.0 + mod_ref[4:5, :])
        dx2 = dx3_ref[...].astype(F32) + _rms_bwd(dn * g_ref[...], xhat, rstd)
        dx2_ref[...] = dx2.astype(BF16)
        st_ref[0:1, :] += _colsum(dh)
        st_ref[1:2, :] += _colsum(dh * (xhat * g_ref[...]))
        st_ref[2:3, :] += _colsum(dn * xhat)
        st_ref[3:4, :] += _colsum(dx2 * mix_ref[...].astype(F32))
        dmix = (mod_ref[2:3, :] * dx2).astype(BF16)
        dym_ref[...] = _dot(dmix, w_ref[...], NT).astype(BF16)
        gw_ref[...] += _dot(y_ref[...], dmix, TN)

    tile = pl.BlockSpec((tm, d), lambda i: (i, 0))
    return _call(
        body, "mix_out_bwd", (s // tm,),
        [tile, tile, tile, tile, tile, _full(w_out.shape), _full(mod6.shape), _full(g_mlp.shape)],
        [tile, tile, _full((d, d)), _full((SUBLANES, d))],
        [jax.ShapeDtypeStruct((s, d), BF16), jax.ShapeDtypeStruct((s, d), BF16),
         jax.ShapeDtypeStruct((d, d), F32), jax.ShapeDtypeStruct((SUBLANES, d), F32)],
        [dhn2, x2, dx3, mix, ymix, w_out, mod6, g_mlp])


def _mixer_bwd(proj, dymix, h_all, conv_sc, conv_lru, conv_b, wa_bd, wx_bd, ba, bx, lam, width):
    s, din = proj.shape
    t = min(MIX_ROWS, s)
    nt = s // t
    nblk = width // LANES
    hb = t // SUBLANES
    last8 = s // SUBLANES - 1

    def body(proj_ref, projp_ref, projn_ref, dy_ref, dyn_ref, h_ref, hp_ref,
             wsc_ref, wlru_ref, blru_ref, wa_ref, wx_ref, ba_ref, bx_ref, lam_ref,
             dproj_ref, small_ref, gwa_ref, gwx_ref, an_ref, gn_ref, dun_ref, stage_ref):
        i = pl.program_id(0)

        @pl.when(i == 0)
        def _():
            small_ref[...] = jnp.zeros_like(small_ref)
            gwa_ref[...] = jnp.zeros_like(gwa_ref)
            gwx_ref[...] = jnp.zeros_like(gwx_ref)
            an_ref[...] = jnp.zeros_like(an_ref)
            gn_ref[...] = jnp.zeros_like(gn_ref)
            dun_ref[...] = jnp.zeros_like(dun_ref)

        has_prev = i < nt - 1
        has_next = i > 0
        for j in range(nblk):
            lo = j * LANES
            ls = slice(lo, lo + LANES)

            def col(p, ref=proj_ref):
                return ref[:, p * width + lo:p * width + lo + LANES]

            def prev(p):
                return jnp.where(has_prev, col(p, projp_ref), 0.0)

            def nxt(p):
                return jnp.where(has_next, col(p, projn_ref), 0.0)

            def add_row(r, v):
                small_ref[r:r + 1, ls] += _colsum(v)

            sc_b, sc_c, sc_x = col(0), col(1), col(2)
            p = sc_c * sc_x
            q, p1, p2 = _conv3(p, prev(1) * prev(2), wsc_ref, lo, stage_ref.at[0])
            dys = dy_ref[:, ls].astype(F32)
            dproj_ref[:, ls] = (dys * q).astype(BF16)
            dq = dys * sc_b
            dqn = jnp.where(has_next, dyn_ref[:, ls].astype(F32)[0:SUBLANES], 0.0) * nxt(0)
            _, (dq1, dq2) = _staged_shifts(stage_ref.at[1], dq, None, dqn, (), (1, 2))
            dp = (wsc_ref[2:3, ls] * dq + wsc_ref[1:2, ls] * dq1) + wsc_ref[0:1, ls] * dq2
            dproj_ref[:, width + lo:width + lo + LANES] = (dp * sc_x).astype(BF16)
            dproj_ref[:, 2 * width + lo:2 * width + lo + LANES] = (dp * sc_c).astype(BF16)
            add_row(0, dq * p2)
            add_row(1, dq * p1)
            add_row(2, dq * p)

            xv = col(4)
            u, x1, x2, x3 = _conv4(xv, prev(4), wlru_ref, blru_ref, lo, stage_ref.at[2])
            lam_v = lam_ref[:, ls]
            sp = _softplus(-lam_v)
            wa, wx = wa_ref[j], wx_ref[j]
            ub, r, ig, a, mult = _lru_gates(u, wa, wx, ba_ref[:, ls], bx_ref[:, ls], sp)
            iu = ig * u
            h = h_ref[:, ls]
            (hm1,), _ = _staged_shifts(stage_ref.at[3], h, jnp.where(has_prev, hp_ref[:, ls], 0.0), None, (1,), ())
            lyv = col(3)
            gel, th = _gelu(lyv)
            dyl = dy_ref[:, width + lo:width + lo + LANES].astype(F32)
            dproj_ref[:, 3 * width + lo:3 * width + lo + LANES] = (dyl * h * _dgelu(lyv, th)).astype(BF16)
            a_next = jnp.broadcast_to(an_ref[0:1, ls], (SUBLANES, LANES))
            _, (a_up,) = _staged_shifts(stage_ref.at[4], a, None, a_next, (), (1,))
            g, _ = _scan_tile(a_up, dyl * gel, gn_ref[0:1, ls], stage_ref, 5, True)
            an_ref[0:1, ls] = a[0:1, :]
            gn_ref[0:1, ls] = g[0:1, :]
            da = g * hm1
            dmult = g * iu
            diu = g * mult
            dlog_a = da * a - dmult * ((a * a) / mult)
            dpre_a = (dlog_a * (-RG_C * sp)) * (r * (1.0 - r))
            dpre_x = (diu * u) * (ig * (1.0 - ig))
            dab, dxb = dpre_a.astype(BF16), dpre_x.astype(BF16)
            du = diu * ig + _dot(dab, wa, NT) + _dot(dxb, wx, NT)
            gwa_ref[j] += _dot(ub, dab, TN)
            gwx_ref[j] += _dot(ub, dxb, TN)
            dun = dun_ref[:, ls]
            dun_ref[:, ls] = du[0:SUBLANES, :]
            _, (du1, du2, du3) = _staged_shifts(stage_ref.at[7], du, None, dun, (), (1, 2, 3))
            dlx = (((wlru_ref[3:4, ls] * du + wlru_ref[2:3, ls] * du1) + wlru_ref[1:2, ls] * du2)
                   + wlru_ref[0:1, ls] * du3)
            dproj_ref[:, 4 * width + lo:4 * width + lo + LANES] = dlx.astype(BF16)
            add_row(3, du * x3)
            add_row(4, du * x2)
            add_row(5, du * x1)
            add_row(6, du * xv)
            add_row(7, du)
            add_row(8, dpre_a)
            add_row(9, dpre_x)
            add_row(10, (dlog_a * (RG_C * r)) * jax.nn.sigmoid(-lam_v))

    small = [conv_sc, conv_lru, conv_b, wa_bd, wx_bd, ba, bx, lam]
    rev = lambda i: nt - 1 - i
    return _call(
        body, "mixer_bwd", (nt,),
        [pl.BlockSpec((t, din), lambda i: (rev(i), 0)),
         pl.BlockSpec((SUBLANES, din), lambda i: (jnp.maximum(rev(i) * hb - 1, 0), 0)),
         pl.BlockSpec((SUBLANES, din), lambda i: (jnp.minimum((rev(i) + 1) * hb, last8), 0)),
         pl.BlockSpec((t, 2 * width), lambda i: (rev(i), 0)),
         pl.BlockSpec((2 * SUBLANES, 2 * width), lambda i: (jnp.minimum((rev(i) + 1) * (hb // 2), last8 // 2), 0)),
         pl.BlockSpec((t, width), lambda i: (rev(i), 0)),
         pl.BlockSpec((SUBLANES, width), lambda i: (jnp.maximum(rev(i) * hb - 1, 0), 0))]
        + [_full(a.shape) for a in small],
        [pl.BlockSpec((t, din), lambda i: (rev(i), 0)), _full((2 * SUBLANES, width)),
         _full(wa_bd.shape), _full(wx_bd.shape)],
        [jax.ShapeDtypeStruct((s, din), BF16), jax.ShapeDtypeStruct((2 * SUBLANES, width), F32),
         jax.ShapeDtypeStruct(wa_bd.shape, F32), jax.ShapeDtypeStruct(wx_bd.shape, F32)],
        [proj, proj, proj, dymix, dymix, h_all, h_all, *small],
        scratch=[pltpu.VMEM((SUBLANES, width), F32), pltpu.VMEM((SUBLANES, width), F32),
                 pltpu.VMEM((SUBLANES, width), F32), pltpu.VMEM((8, t + 2 * SUBLANES, LANES), F32)])


def _mix_in_bwd_dx(dproj, x2d, dx2, w_in_t, mod6, g_mix, tm):
    s, d = x2d.shape
    din = dproj.shape[1]

    def body(dp_ref, x_ref, dx2_ref, w_ref, mod_ref, g_ref, gx_ref, st_ref):
        i = pl.program_id(0)

        @pl.when(i == 0)
        def _():
            st_ref[...] = jnp.zeros_like(st_ref)

        dh = _dot(dp_ref[...], w_ref[...], NN)
        xhat, rstd = _rms(x_ref[...])
        dn = dh * (1.0 + mod_ref[1:2, :])
        gx_ref[...] = dx2_ref[...].astype(F32) + _rms_bwd(dn * g_ref[...], xhat, rstd)
        st_ref[0:1, :] += _colsum(dh)
        st_ref[1:2, :] += _colsum(dh * (xhat * g_ref[...]))
        st_ref[2:3, :] += _colsum(dn * xhat)

    tile = pl.BlockSpec((tm, d), lambda i: (i, 0))
    return _call(
        body, "mix_in_bwd_dx", (s // tm,),
        [pl.BlockSpec((tm, din), lambda i: (i, 0)), tile, tile, _full(w_in_t.shape), _full(mod6.shape),
         _full(g_mix.shape)],
        [tile, _full((SUBLANES, d))],
        [jax.ShapeDtypeStruct((s, d), F32), jax.ShapeDtypeStruct((SUBLANES, d), F32)],
        [dproj, x2d, dx2, w_in_t, mod6, g_mix], vmem=VMEM_LIMIT_BIG)


def _mix_in_bwd_dw(dproj, hn1, tm, tn):
    s, d = hn1.shape
    din = dproj.shape[1]

    def body(dp_ref, hn_ref, gw_ref):
        i = pl.program_id(1)

        @pl.when(i == 0)
        def _():
            gw_ref[...] = jnp.zeros_like(gw_ref)

        gw_ref[...] += _dot(dp_ref[...], hn_ref[...], TN)

    return _call(
        body, "mix_in_bwd_dw", (din // tn, s // tm),
        [pl.BlockSpec((tm, tn), lambda p, i: (i, p)), pl.BlockSpec((tm, d), lambda p, i: (i, 0))],
        [pl.BlockSpec((tn, d), lambda p, i: (p, 0))],
        [jax.ShapeDtypeStruct((din, d), F32)],
        [dproj, hn1])


def _adamw(w, g, m, v):
    m = ADAM_B1 * m + (1.0 - ADAM_B1) * g
    v = ADAM_B2 * v + (1.0 - ADAM_B2) * (g * g)
    m_hat = m / (1.0 - ADAM_B1 ** ADAM_STEP)
    v_hat = v / (1.0 - ADAM_B2 ** ADAM_STEP)
    delta = -ADAM_LR * (m_hat / (jnp.sqrt(v_hat) + ADAM_EPS) + ADAM_WD * w)
    return delta, m, v


def _pair_sum(g4s, h4s, core_chip, tr, name):
    na = len(g4s)
    _, _, r, n = g4s[0].shape

    def body(sc_ref, *refs):
        q = pl.program_id(1)
        for a in range(na):
            g_ref, h_ref = refs[2 * a], refs[2 * a + 1]
            sb_ref, own_ref = refs[2 * na + 2 * a], refs[2 * na + 2 * a + 1]
            ssum = g_ref[...] + h_ref[...]
            sb_ref[...] = ssum.astype(BF16)

            @pl.when(q == sc_ref[1])
            def _():
                own_ref[...] = ssum

    grid_spec = pltpu.PrefetchScalarGridSpec(
        num_scalar_prefetch=1, grid=(r // tr, 4),
        in_specs=[pl.BlockSpec((None, None, tr, n), lambda i, q, sc: (q, sc[0], i, 0)),
                  pl.BlockSpec((None, tr, n), lambda i, q, sc: (q, i, 0))] * na,
        out_specs=[pl.BlockSpec((None, tr, n), lambda i, q, sc: (q, i, 0)),
                   pl.BlockSpec((tr, n), lambda i, q, sc: (i, 0))] * na)
    outs = pl.pallas_call(
        body, name=name, grid_spec=grid_spec,
        out_shape=[jax.ShapeDtypeStruct((4, r, n), BF16), jax.ShapeDtypeStruct((r, n), F32)] * na,
        compiler_params=_params(("parallel", "arbitrary")),
    )(core_chip, *[x for pair in zip(g4s, h4s) for x in pair])
    return [(outs[2 * a], outs[2 * a + 1]) for a in range(na)]


def _sum4_adam(own, parts, w, m, v, tr, name, transposed):
    r, n = own.shape
    rows, cols = w.shape

    def body(o_ref, p_ref, w_ref, m_ref, v_ref, g_ref, d_ref, nm_ref, nv_ref):
        g = o_ref[...]
        for k in range(3):
            g = g + p_ref[k].astype(F32)
        if transposed:
            g = g.T
        g_ref[...] = g
        d_ref[...], nm_ref[...], nv_ref[...] = _adamw(w_ref[...], g, m_ref[...], v_ref[...])

    if transposed:
        g_specs = [pl.BlockSpec((r, tr), lambda i: (0, i)), pl.BlockSpec((3, r, tr), lambda i: (0, 0, i))]
    else:
        g_specs = [pl.BlockSpec((tr, n), lambda i: (i, 0)), pl.BlockSpec((3, tr, n), lambda i: (0, i, 0))]
    tile = pl.BlockSpec((tr, cols), lambda i: (i, 0))
    return pl.pallas_call(
        body, name=name, grid=(rows // tr,),
        in_specs=g_specs + [tile] * 3, out_specs=[tile] * 4,
        out_shape=[jax.ShapeDtypeStruct((rows, cols), F32)] * 4,
        compiler_params=_params(("parallel",)),
    )(own, parts, w, m, v)


def _sum8(parts, tr, name):
    _, rows, n = parts.shape

    def body(p_ref, o_ref):
        acc = p_ref[0]
        for k in range(1, N_DEV):
            acc = acc + p_ref[k]
        o_ref[...] = acc

    return pl.pallas_call(
        body, name=name, grid=(rows // tr,),
        in_specs=[pl.BlockSpec((N_DEV, tr, n), lambda i: (0, i, 0))],
        out_specs=pl.BlockSpec((tr, n), lambda i: (i, 0)),
        out_shape=jax.ShapeDtypeStruct((rows, n), F32),
        compiler_params=_params(("parallel",)),
    )(parts)


def _ada_bwd_adam(cact_t, dmod_cols, w, m, v, tr):
    rows, n = w.shape

    def body(c_ref, d_ref, w_ref, m_ref, v_ref, g_ref, dl_ref, nm_ref, nv_ref):
        def term(b):
            return c_ref[b].astype(BF16).astype(F32) * d_ref[b:b + 1, :].astype(BF16).astype(F32)

        g = term(0)
        for b in range(1, N_DEV):
            g = g + term(b)
        g_ref[...] = g
        dl_ref[...], nm_ref[...], nv_ref[...] = _adamw(w_ref[...], g, m_ref[...], v_ref[...])

    tile = pl.BlockSpec((tr, n), lambda i: (i, 0))
    return pl.pallas_call(
        body, name="ada_bwd_adam", grid=(rows // tr,),
        in_specs=[pl.BlockSpec((N_DEV, tr, 1), lambda i: (0, i, 0)), _full(dmod_cols.shape), tile, tile, tile],
        out_specs=[tile] * 4,
        out_shape=[jax.ShapeDtypeStruct((rows, n), F32)] * 4,
        compiler_params=_params(("parallel",)),
    )(cact_t, dmod_cols, w, m, v)


def _adam_small(ws, gs, ms, vs):
    n = len(ws)

    def body(*refs):
        w_r, g_r, m_r, v_r = refs[:n], refs[n:2 * n], refs[2 * n:3 * n], refs[3 * n:4 * n]
        d_r, nm_r, nv_r = refs[4 * n:5 * n], refs[5 * n:6 * n], refs[6 * n:7 * n]
        for k in range(n):
            d_r[k][...], nm_r[k][...], nv_r[k][...] = _adamw(w_r[k][...], g_r[k][...], m_r[k][...], v_r[k][...])

    shapes = [jax.ShapeDtypeStruct(w.shape, F32) for w in ws]
    outs = pl.pallas_call(
        body, name="adam_small", out_shape=shapes * 3, compiler_params=_params(),
    )(*ws, *gs, *ms, *vs)
    return outs[:n], outs[n:2 * n], outs[2 * n:]


def _block_diag(w):
    h, hd, _ = w.shape
    per = LANES // hd
    eye = jnp.eye(per, dtype=w.dtype)
    w5 = w.reshape(h // per, per, hd, 1, hd) * eye[None, :, None, :, None]
    return w5.reshape(h // per, LANES, LANES)


def _block_diag_grad(g, h, hd):
    per = LANES // hd
    g5 = g.reshape(h // per, per, hd, per, hd)
    return jnp.stack([g5[:, a, :, a, :] for a in range(per)], axis=1).reshape(h, hd, hd)


def kernel(x, c, w_ada, b_ada, g_mix, w_in, conv_w_sc, conv_w_lru, conv_b_lru, w_rg_a, b_rg_a, w_rg_x, b_rg_x, lru_lambda, w_out, g_mlp, w_up, w_down, g_final, loss_target, m_w_ada, m_b_ada, m_g_mix, m_w_in, m_conv_w_sc, m_conv_w_lru, m_conv_b_lru, m_w_rg_a, m_b_rg_a, m_w_rg_x, m_b_rg_x, m_lru_lambda, m_w_out, m_g_mlp, m_w_up, m_w_down, m_g_final, v_w_ada, v_b_ada, v_g_mix, v_w_in, v_conv_w_sc, v_conv_w_lru, v_conv_b_lru, v_w_rg_a, v_b_rg_a, v_w_rg_x, v_b_rg_x, v_lru_lambda, v_w_out, v_g_mlp, v_w_up, v_w_down, v_g_final):
    s, d = x.shape[1], x.shape[2]
    width = conv_b_lru.shape[1]
    heads, hd = w_rg_a.shape[1], w_rg_a.shape[2]
    n_ada = w_ada.shape[2]
    csh = conv_w_sc.shape[2]
    me = 4 * lax.axis_index("x") + 2 * lax.axis_index("y") + lax.axis_index("c")
    tm = min(512, s)
    tm_mlp = min(1024, s)
    tk = 512

    x2d = x[0]
    tgt = loss_target[0]

    pay = jnp.zeros((SUBLANES, d), F32)
    pay = pay.at[0:1, :].set(c)
    pay = pay.at[1:4, 0:csh].set(conv_w_sc[0])
    pay = pay.at[4:8, 0:csh].set(conv_w_lru[0])
    w_in_t_sh = w_in[0].T.astype(BF16)
    w_up_t_sh = w_up[0].T.astype(BF16)
    w_out_sh = w_out[0].astype(BF16)
    w_down_sh = w_down[0].astype(BF16)
    (w_in_t,) = _seq_gather2("gather_w_in", 10, [w_in_t_sh])
    (pay_all,) = _gather2("gather_in", [pay])
    w_in_t = w_in_t.reshape(-1, d)
    c_all = pay_all[:, 0, :]
    conv_sc = pay_all[:, 1:4, 0:csh].transpose(1, 0, 2).reshape(3, width)
    conv_lru = pay_all[:, 4:8, 0:csh].transpose(1, 0, 2).reshape(4, width)

    b_ada_sh = lax.dynamic_slice(b_ada, (0, me * n_ada), (1, n_ada))
    mod_cols, c_act = _ada_fwd(c_all, w_ada[0], b_ada_sh)
    (mod_rows,) = _exchange("scatter_mod", [], [mod_cols.reshape(N_DEV, 1, n_ada)])
    mod_rows, w_out_sh, w_up_t_sh, w_down_sh = lax.optimization_barrier((mod_rows, w_out_sh, w_up_t_sh, w_down_sh))
    (w_out_g,) = _seq_gather2("gather_w_out", 1, [w_out_sh])
    w_up_g, w_down_g = _seq_gather2("gather_mlp_weights", 2, [w_up_t_sh, w_down_sh])
    mod6 = jnp.zeros((SUBLANES, d), F32).at[0:6, :].set(mod_rows.reshape(6, d))

    wa_bd = _block_diag(w_rg_a[0]).astype(BF16)
    wx_bd = _block_diag(w_rg_x[0]).astype(BF16)
    ba = b_rg_a.reshape(1, width)
    bx = b_rg_x.reshape(1, width)
    g_fin = g_final.reshape(1, d)

    hn1, proj, ymix, h_all = _mix_in_mixer_fwd(x2d, mod6, g_mix, w_in_t, conv_sc, conv_lru, conv_b_lru,
                                               wa_bd, wx_bd, ba, bx, lru_lambda, width, tm)
    w_out_b = w_out_g.reshape(-1, d)
    mix, x2, hn2 = _mix_out_fwd(ymix, x2d, w_out_b, mod6, g_mlp, tm_mlp)
    w_up_t = w_up_g.reshape(-1, d)
    w_down_b = w_down_g.reshape(-1, d)
    z, dx3, dyb, st_fin = _mlp_fwd_loss(hn2, w_up_t, w_down_b, x2, tgt, mod6, g_fin, tm_mlp, 2 * tk)

    core_chip = jnp.stack([lax.axis_index("c"), 2 * lax.axis_index("x") + lax.axis_index("y")]).astype(jnp.int32)
    dz, dhn2 = _mlp_bwd_dx(dyb, z, w_down_b, w_up_t, tm_mlp, 2 * tk)
    g_down, g_up_t = _mlp_bwd_dw(z, dz, dyb, hn2, tm_mlp, 2 * tk)
    g_up4, g_down4 = g_up_t.reshape(4, 2, -1, d), g_down.reshape(4, 2, -1, d)
    h_up, h_down = _seq_pair_swap("swap_mlp_grads", 7, [g_up4, g_down4])
    dx2, dymix, g_out, st_out = _mix_out_bwd(dhn2, x2, dx3, mix, ymix, w_out_b, mod6, g_mlp, tm)
    h_up, h_down, g_out = lax.optimization_barrier((h_up, h_down, g_out))
    (sb_up, own_up), (sb_down, own_down) = _pair_sum([g_up4, g_down4], [h_up, h_down], core_chip, g_up4.shape[2], "pair_sum_mlp")
    g_out4 = g_out.reshape(4, 2, -1, d)
    (h_out,) = _seq_pair_swap("swap_w_out_grad", 8, [g_out4])
    p_up, p_down = _seq_chip_exchange("exchange_mlp_grads", 3, [sb_up, sb_down])
    dproj, g_small, g_wa, g_wx = _mixer_bwd(
        proj, dymix, h_all, conv_sc, conv_lru, conv_b_lru, wa_bd, wx_bd, ba, bx, lru_lambda, width)
    h_out, dproj = lax.optimization_barrier((h_out, dproj))
    ((sb_out, own_out),) = _pair_sum([g_out4], [h_out], core_chip, g_out4.shape[2], "pair_sum_w_out")
    (p_out,) = _seq_chip_exchange("exchange_w_out_grad", 4, [sb_out])
    grad_x, st_in = _mix_in_bwd_dx(dproj, x2d, dx2, w_in_t, mod6, g_mix, tm_mlp)

    small = jnp.concatenate([
        st_in[0:2], st_out[3:4], st_out[0:2], st_fin[1:2],
        st_in[2:3], st_out[2:3], st_fin[0:1],
        jnp.concatenate([g_small[7:8], g_small[10:11]], axis=1),
        jnp.concatenate([g_small[8:9], g_small[9:10]], axis=1),
        jnp.concatenate([jnp.concatenate([g_small[0:3], jnp.zeros((1, width), F32)], axis=0), g_small[3:7]], axis=1),
        st_fin[2:3],
        _block_diag_grad(g_wa, heads, hd).reshape(-1, d),
        _block_diag_grad(g_wx, heads, hd).reshape(-1, d),
    ], axis=0)

    (small_all,) = _seq_gather2("gather_small_grads", 5, [small])
    g_in_t, = _mix_in_bwd_dw(dproj, hn1, min(2048, s), dproj.shape[1] // 2)
    g_in4 = g_in_t.reshape(4, 2, -1, d)
    (h_in,) = _seq_pair_swap("swap_w_in_grad", 9, [g_in4])
    p_up, p_down, p_out, small_all, g_in_t = lax.optimization_barrier((p_up, p_down, p_out, small_all, g_in_t))

    ad_up = _sum4_adam(own_up, p_up, w_up[0], m_w_up[0], v_w_up[0], 256, "adam_w_up", True)
    h_in, ad_up = lax.optimization_barrier((h_in, ad_up))
    ((sb_in, own_in),) = _pair_sum([g_in4], [h_in], core_chip, g_in4.shape[2], "pair_sum_w_in")
    (p_in,) = _seq_chip_exchange("exchange_w_in_grad", 6, [sb_in])
    ad_out = _sum4_adam(own_out, p_out, w_out[0], m_w_out[0], v_w_out[0], w_out.shape[1], "adam_w_out", False)
    ad_down = _sum4_adam(own_down, p_down, w_down[0], m_w_down[0], v_w_down[0], 256, "adam_w_down", False)

    gsum = _sum8(small_all, SMALL_ROWS, "sum_small")
    loss = (0.5 / d) * jnp.sum(gsum[15])
    dmod_cols = lax.dynamic_slice(small_all[:, 0:6, :].reshape(N_DEV, 6 * d), (0, me * n_ada), (N_DEV, n_ada))
    g_ada, d_ada, nm_ada, nv_ada = _ada_bwd_adam(c_act[:, :, None], dmod_cols, w_ada[0], m_w_ada[0], v_w_ada[0], 256)

    g_conv = lax.dynamic_slice(gsum[11:15, 0:width], (0, me * csh), (4, csh))
    g_conv_l = lax.dynamic_slice(gsum[11:15, width:2 * width], (0, me * csh), (4, csh))
    small_g = [
        gsum[0:6].reshape(1, 6 * d),
        gsum[6:7],
        g_conv[0:3].reshape(1, 3, csh),
        g_conv_l.reshape(1, 4, csh),
        gsum[9:10, 0:width],
        gsum[16:48].reshape(1, heads, hd, hd),
        gsum[10:11, 0:width].reshape(1, heads, hd),
        gsum[48:80].reshape(1, heads, hd, hd),
        gsum[10:11, width:].reshape(1, heads, hd),
        gsum[9:10, width:],
        gsum[7:8],
        gsum[8],
    ]
    small_w = [b_ada, g_mix, conv_w_sc, conv_w_lru, conv_b_lru, w_rg_a, b_rg_a, w_rg_x, b_rg_x, lru_lambda, g_mlp, g_final]
    small_m = [m_b_ada, m_g_mix, m_conv_w_sc, m_conv_w_lru, m_conv_b_lru, m_w_rg_a, m_b_rg_a, m_w_rg_x, m_b_rg_x,
               m_lru_lambda, m_g_mlp, m_g_final]
    small_v = [v_b_ada, v_g_mix, v_conv_w_sc, v_conv_w_lru, v_conv_b_lru, v_w_rg_a, v_b_rg_a, v_w_rg_x, v_b_rg_x,
               v_lru_lambda, v_g_mlp, v_g_final]
    sd, snm, snv = _adam_small(small_w, small_g, small_m, small_v)
    p_in, ad_out, ad_down, (g_ada, d_ada, nm_ada, nv_ada), sd = lax.optimization_barrier(
        (p_in, ad_out, ad_down, (g_ada, d_ada, nm_ada, nv_ada), sd))
    ad_in = _sum4_adam(own_in, p_in, w_in[0].T, m_w_in[0].T, v_w_in[0].T, own_in.shape[0], "adam_w_in", False)
    ad_in = [a.T for a in ad_in]

    def order(ada, w_in_, w_out_, w_up_, w_down_, sm):
        return [ada[None], sm[0], sm[1], w_in_[None], sm[2], sm[3], sm[4], sm[5], sm[6], sm[7], sm[8], sm[9],
                w_out_[None], sm[10], w_up_[None], w_down_[None], sm[11]]

    grads = order(g_ada, ad_in[0], ad_out[0], ad_up[0], ad_down[0], small_g)
    deltas = order(d_ada, ad_in[1], ad_out[1], ad_up[1], ad_down[1], sd)
    new_m = order(nm_ada, ad_in[2], ad_out[2], ad_up[2], ad_down[2], snm)
    new_v = order(nv_ada, ad_in[3], ad_out[3], ad_up[3], ad_down[3], snv)
    return (loss, grad_x[None], *grads, *deltas, *new_m, *new_v)
```

```python
import jax
import jax.numpy as jnp
from jax import lax
from jax.experimental import pallas as pl
from jax.experimental.pallas import tpu as pltpu
from jax.experimental.pallas import tpu_sc as plsc

F32 = jnp.float32
BF16 = jnp.bfloat16
N_DEV = 8
EPS = 1e-6
RG_C = 8.0
GELU_K0 = 0.7978845608028654
GELU_K1 = 0.044715
ADAM_LR = 0.001
ADAM_B1 = 0.9
ADAM_B2 = 0.999
ADAM_EPS = 1e-08
ADAM_WD = 0.01
ADAM_STEP = 10
LANES = 128
SUBLANES = 8
VMEM_LIMIT = 52 * 1024 * 1024
VMEM_LIMIT_BIG = 58 * 1024 * 1024
MIX_ROWS = 256
SMALL_ROWS = 80

MESH = pl.DeviceIdType.MESH
ANY = pl.BlockSpec(memory_space=pl.ANY)
NN = ((1,), (0,))
NT = ((1,), (1,))
TN = ((0,), (0,))


def _dot(a, b, dims):
    return lax.dot_general(a, b, (dims, ((), ())), preferred_element_type=F32)


def _params(sem=None):
    return pltpu.CompilerParams(dimension_semantics=sem, vmem_limit_bytes=VMEM_LIMIT)


def _full(shape):
    nd = len(shape)
    return pl.BlockSpec(shape, lambda *_: (0,) * nd)


def _exchange(name, gathers, scatters):
    n_g = len(gathers)
    arrs = list(gathers) + list(scatters)
    n = len(arrs)
    out_shape = [jax.ShapeDtypeStruct((N_DEV,) + a.shape, a.dtype) for a in gathers]
    out_shape += [jax.ShapeDtypeStruct(a.shape, a.dtype) for a in scatters]

    def body(*refs):
        ins, outs = refs[:n], refs[n:2 * n]
        send_sems, recv_sems, local_sems = refs[2 * n:]
        x, y, c = lax.axis_index("x"), lax.axis_index("y"), lax.axis_index("c")
        me = 4 * x + 2 * y + c

        def src(a, dev):
            return ins[a] if a < n_g else ins[a].at[dev]

        def peer_of(k):
            px = 1 - x if (k >> 2) & 1 else x
            py = 1 - y if (k >> 1) & 1 else y
            pc = 1 - c if k & 1 else c
            return (px, py, pc), 4 * px + 2 * py + pc

        local = [pltpu.make_async_copy(src(a, me), outs[a].at[me], local_sems.at[a]) for a in range(n)]
        for cp in local:
            cp.start()
        sends = []
        for k in range(1, N_DEV):
            peer, pidx = peer_of(k)
            for a in range(n):
                cp = pltpu.make_async_remote_copy(
                    src_ref=src(a, pidx), dst_ref=outs[a].at[me],
                    send_sem=send_sems.at[a * (N_DEV - 1) + k - 1], recv_sem=recv_sems.at[a * (N_DEV - 1) + k - 1],
                    device_id=peer, device_id_type=MESH)
                cp.start()
                sends.append(cp)
        for k in range(1, N_DEV):
            peer, pidx = peer_of(k)
            for a in range(n):
                pltpu.make_async_remote_copy(
                    src_ref=src(a, pidx), dst_ref=outs[a].at[pidx],
                    send_sem=send_sems.at[a * (N_DEV - 1) + k - 1], recv_sem=recv_sems.at[a * (N_DEV - 1) + k - 1],
                    device_id=peer, device_id_type=MESH).wait_recv()
        for cp in sends:
            cp.wait_send()
        for cp in local:
            cp.wait()

    return pl.pallas_call(
        body, name=name, out_shape=out_shape,
        in_specs=[ANY] * n, out_specs=[ANY] * n,
        scratch_shapes=[pltpu.SemaphoreType.DMA((n * (N_DEV - 1),)),
                        pltpu.SemaphoreType.DMA((n * (N_DEV - 1),)),
                        pltpu.SemaphoreType.DMA((n,))],
    )(*arrs)


GATHER_SEMS = 7


def _gather_copies(ins, outs, send_sems, recv_sems, local_sems, x, y, c):
    n = len(ins)
    per = GATHER_SEMS
    sib = (x, y, 1 - c)
    xn, yn, dg = (1 - x, y), (x, 1 - y), (1 - x, 1 - y)
    fx, fy = x + (1 - c) * (1 - 2 * x), y + c * (1 - 2 * y)
    tx, ty = x + c * (1 - 2 * x), y + (1 - c) * (1 - 2 * y)

    def slot(a, px, py, pc):
        return outs[a].at[4 * px + 2 * py + pc]

    def copy(a, k, block, to, src=None):
        return pltpu.make_async_remote_copy(
            src_ref=slot(a, *block) if src is None else src, dst_ref=slot(a, *block),
            send_sem=send_sems.at[a * per + k], recv_sem=recv_sems.at[a * per + k],
            device_id=to, device_id_type=MESH)

    local = [pltpu.make_async_copy(ins[a], slot(a, x, y, c), local_sems.at[a]) for a in range(n)]
    for cp in local:
        cp.start()
    started = []
    for a in range(n):
        started += [copy(a, 1, (x, y, c), (*xn, c), src=ins[a]), copy(a, 2, (x, y, c), (*yn, c), src=ins[a])]
    for a in range(n):
        started.append(copy(a, 0, (x, y, c), sib, src=ins[a]))
    for cp in started:
        cp.start()
    for a in range(n):
        copy(a, 1, (*xn, c), (x, y, c)).wait_recv()
        copy(a, 2, (*yn, c), (x, y, c)).wait_recv()
        later = [copy(a, 3, (fx, fy, c), (tx, ty, c)), copy(a, 4, (*xn, c), sib), copy(a, 5, (*yn, c), sib)]
        for cp in later:
            cp.start()
        started += later
    for a in range(n):
        copy(a, 3, (*dg, c), (x, y, c)).wait_recv()
        cp = copy(a, 6, (*dg, c), sib)
        cp.start()
        started.append(cp)
    for a in range(n):
        copy(a, 0, sib, (x, y, c)).wait_recv()
        for k, chip in ((4, xn), (5, yn), (6, dg)):
            copy(a, k, (*chip, 1 - c), (x, y, c)).wait_recv()
    for cp in started:
        cp.wait_send()
    for cp in local:
        cp.wait()


def _gather2(name, arrs):
    n = len(arrs)
    per = GATHER_SEMS
    out_shape = [jax.ShapeDtypeStruct((N_DEV,) + a.shape, a.dtype) for a in arrs]

    def body(*refs):
        ins, outs = refs[:n], refs[n:2 * n]
        send_sems, recv_sems, local_sems = refs[2 * n:]
        x, y, c = lax.axis_index("x"), lax.axis_index("y"), lax.axis_index("c")
        _gather_copies(ins, outs, send_sems, recv_sems, local_sems, x, y, c)

    return pl.pallas_call(
        body, name=name, out_shape=out_shape,
        in_specs=[ANY] * n, out_specs=[ANY] * n,
        scratch_shapes=[pltpu.SemaphoreType.DMA((n * per,)), pltpu.SemaphoreType.DMA((n * per,)),
                        pltpu.SemaphoreType.DMA((n,))],
    )(*arrs)


def _seq_gather2(name, collective_id, arrs):
    n = len(arrs)
    per = GATHER_SEMS

    def body(*refs):
        ins, outs = refs[:n], refs[n:2 * n]
        send_sems, recv_sems, local_sems = refs[2 * n:]
        x, y, c = lax.axis_index("x"), lax.axis_index("y"), lax.axis_index("c")
        barrier = pltpu.get_barrier_semaphore()
        for peer in [(x, y, 1 - c), (1 - x, y, c), (x, 1 - y, c)]:
            pl.semaphore_signal(barrier, inc=1, device_id=peer, device_id_type=MESH)
        pl.semaphore_wait(barrier, 3)
        _gather_copies(ins, outs, send_sems, recv_sems, local_sems, x, y, c)

    return pl.kernel(
        body, out_type=[jax.ShapeDtypeStruct((N_DEV,) + a.shape, a.dtype) for a in arrs],
        mesh=plsc.ScalarSubcoreMesh(axis_name="seq", num_cores=1),
        scratch_types=[pltpu.SemaphoreType.DMA((n * per,)), pltpu.SemaphoreType.DMA((n * per,)),
                       pltpu.SemaphoreType.DMA((n,))],
        compiler_params=pltpu.CompilerParams(collective_id=collective_id), name=name,
    )(*arrs)


def _seq_chip_exchange(name, collective_id, arrs):
    n = len(arrs)

    def body(*refs):
        ins, outs = refs[:n], refs[n:2 * n]
        send_sems, recv_sems = refs[2 * n:]
        x, y, c = lax.axis_index("x"), lax.axis_index("y"), lax.axis_index("c")

        def peer(k):
            return (1 - x if (k >> 1) & 1 else x), (1 - y if k & 1 else y)

        barrier = pltpu.get_barrier_semaphore()
        for k in (1, 2, 3):
            pl.semaphore_signal(barrier, inc=1, device_id=(*peer(k), c), device_id_type=MESH)
        pl.semaphore_wait(barrier, 3)

        def copy(a, k):
            px, py = peer(k)
            return pltpu.make_async_remote_copy(
                src_ref=ins[a].at[2 * px + py], dst_ref=outs[a].at[k - 1],
                send_sem=send_sems.at[a * 3 + k - 1], recv_sem=recv_sems.at[a * 3 + k - 1],
                device_id=(px, py, c), device_id_type=MESH)

        cps = [copy(a, k) for a in range(n) for k in (1, 2, 3)]
        for cp in cps:
            cp.start()
        for cp in cps:
            cp.wait_recv()
        for cp in cps:
            cp.wait_send()

    return pl.kernel(
        body, out_type=[jax.ShapeDtypeStruct((3,) + a.shape[1:], a.dtype) for a in arrs],
        mesh=plsc.ScalarSubcoreMesh(axis_name="seq", num_cores=1),
        scratch_types=[pltpu.SemaphoreType.DMA((n * 3,)), pltpu.SemaphoreType.DMA((n * 3,))],
        compiler_params=pltpu.CompilerParams(collective_id=collective_id), name=name,
    )(*arrs)


def _seq_pair_swap(name, collective_id, arrs):
    n = len(arrs)

    def body(*refs):
        ins, outs = refs[:n], refs[n:2 * n]
        send_sems, recv_sems = refs[2 * n:]
        x, y, c = lax.axis_index("x"), lax.axis_index("y"), lax.axis_index("c")
        barrier = pltpu.get_barrier_semaphore()
        pl.semaphore_signal(barrier, inc=1, device_id=(x, y, 1 - c), device_id_type=MESH)
        pl.semaphore_wait(barrier, 1)

        def copy(a, q):
            return pltpu.make_async_remote_copy(
                src_ref=ins[a].at[q, 1 - c], dst_ref=outs[a].at[q],
                send_sem=send_sems.at[a * 4 + q], recv_sem=recv_sems.at[a * 4 + q],
                device_id=(x, y, 1 - c), device_id_type=MESH)

        cps = [copy(a, q) for a in range(n) for q in range(4)]
        for cp in cps:
            cp.start()
        for cp in cps:
            cp.wait_recv()
        for cp in cps:
            cp.wait_send()

    return pl.kernel(
        body, out_type=[jax.ShapeDtypeStruct((4,) + a.shape[2:], a.dtype) for a in arrs],
        mesh=plsc.ScalarSubcoreMesh(axis_name="seq", num_cores=1),
        scratch_types=[pltpu.SemaphoreType.DMA((n * 4,)), pltpu.SemaphoreType.DMA((n * 4,))],
        compiler_params=pltpu.CompilerParams(collective_id=collective_id), name=name,
    )(*arrs)


def _call(body, name, grid, in_specs, out_specs, out_shape, args, scratch=(), vmem=VMEM_LIMIT):
    return pl.pallas_call(
        body, name=name, grid=grid, in_specs=in_specs, out_specs=out_specs, out_shape=out_shape,
        scratch_shapes=list(scratch),
        compiler_params=pltpu.CompilerParams(dimension_semantics=("arbitrary",) * len(grid), vmem_limit_bytes=vmem),
    )(*args)


def _ada_fwd(c_all, w_ada_sh, b_ada_sh):
    nb, d = c_all.shape
    ncol = w_ada_sh.shape[1]

    def body(c_ref, w_ref, b_ref, mod_ref, cact_ref):
        cc = c_ref[...]
        ca = cc * jax.nn.sigmoid(cc)
        cact_ref[...] = ca
        mod_ref[...] = _dot(ca.astype(BF16), w_ref[...].astype(BF16), NN) + b_ref[...]

    return pl.pallas_call(
        body, name="ada_fwd",
        out_shape=[jax.ShapeDtypeStruct((nb, ncol), F32), jax.ShapeDtypeStruct((nb, d), F32)],
        compiler_params=_params(),
    )(c_all, w_ada_sh, b_ada_sh)


def _rms(xv):
    rstd = lax.rsqrt(jnp.mean(xv * xv, axis=-1, keepdims=True) + EPS)
    return xv * rstd, rstd


def _rms_bwd(dxhat, xhat, rstd):
    return rstd * (dxhat - xhat * jnp.mean(dxhat * xhat, axis=-1, keepdims=True))


def _colsum(v):
    return jnp.sum(v, axis=0, keepdims=True)


def _expm1(v, ev):
    series = v * (1.0 + v * (0.5 + v * (1.0 / 6.0 + v * (1.0 / 24.0 + v * (1.0 / 120.0)))))
    return jnp.where(jnp.abs(v) < 0.2, series, ev - 1.0)


def _softplus(v):
    return jnp.maximum(v, 0.0) + jnp.log1p(jnp.exp(-jnp.abs(v)))


def _gelu(v):
    t = jnp.tanh(v * (GELU_K0 + (GELU_K0 * GELU_K1) * (v * v)))
    return 0.5 * v * (1.0 + t), t


def _dgelu(v, t):
    return 0.5 * ((1.0 + t) + (v * (1.0 - t * t)) * (GELU_K0 + (3.0 * GELU_K0 * GELU_K1) * (v * v)))


def _scan_tile(a, b, x0, st, k0, reverse):
    t = a.shape[0]
    off = SUBLANES
    stage_a, stage_b = st.at[k0], st.at[k0 + 1]
    halo = slice(off + t, off + t + SUBLANES) if reverse else slice(0, SUBLANES)
    stage_a[halo, :] = jnp.ones((SUBLANES, a.shape[1]), F32)
    stage_b[halo, :] = jnp.zeros((SUBLANES, a.shape[1]), F32)
    s = 1
    while s < min(t, SUBLANES):
        stage_a[off:off + t, :] = a
        stage_b[off:off + t, :] = b
        at = off + s if reverse else off - s
        b = a * stage_b[at:at + t, :] + b
        a = a * stage_a[at:at + t, :]
        s *= 2
    while s < t:
        if reverse:
            b = jnp.concatenate([a[:t - s] * b[s:] + b[:t - s], b[t - s:]], axis=0)
            a = jnp.concatenate([a[:t - s] * a[s:], a[t - s:]], axis=0)
        else:
            b = jnp.concatenate([b[:s], a[s:] * b[:t - s] + b[s:]], axis=0)
            a = jnp.concatenate([a[:s], a[s:] * a[:t - s]], axis=0)
        s *= 2
    x = b + a * x0
    return x, (x[0:SUBLANES, :] if reverse else x[t - SUBLANES:t, :])


def _lru_gates(u, wa, wx, ba, bx, sp):
    ub = u.astype(BF16)
    r = jax.nn.sigmoid(_dot(ub, wa, NN) + ba)
    i = jax.nn.sigmoid(_dot(ub, wx, NN) + bx)
    log_a = (-RG_C * r) * sp
    a = jnp.exp(log_a)
    mult = jnp.sqrt(-_expm1(log_a, a) * (a + 1.0))
    return ub, r, i, a, mult


def _staged_shifts(stage, v, prev8, next8, downs, ups):
    t = v.shape[0]
    if prev8 is not None:
        stage[0:SUBLANES, :] = prev8
    stage[SUBLANES:SUBLANES + t, :] = v
    if next8 is not None:
        stage[SUBLANES + t:2 * SUBLANES + t, :] = next8
    return ([stage[SUBLANES - k:SUBLANES - k + t, :] for k in downs],
            [stage[SUBLANES + k:SUBLANES + k + t, :] for k in ups])


def _conv3(p, pp, w_ref, lo, stage):
    (p1, p2), _ = _staged_shifts(stage, p, pp, None, (1, 2), ())
    q = (w_ref[0:1, lo:lo + LANES] * p2 + w_ref[1:2, lo:lo + LANES] * p1) + w_ref[2:3, lo:lo + LANES] * p
    return q, p1, p2


def _conv4(xv, xp, w_ref, b_ref, lo, stage):
    (x1, x2, x3), _ = _staged_shifts(stage, xv, xp, None, (1, 2, 3), ())
    u = (((w_ref[0:1, lo:lo + LANES] * x3 + w_ref[1:2, lo:lo + LANES] * x2) + w_ref[2:3, lo:lo + LANES] * x1)
         + w_ref[3:4, lo:lo + LANES] * xv) + b_ref[:, lo:lo + LANES]
    return u, x1, x2, x3


def _mix_in_mixer_fwd(x2d, mod6, g_mix, w_in_t, conv_sc, conv_lru, conv_b, wa_bd, wx_bd, ba, bx, lam, width, tm):
    s, d = x2d.shape
    din = w_in_t.shape[0]
    nt = s // tm
    sub = min(MIX_ROWS, tm)
    nblk = width // LANES

    def body(x_ref, mod_ref, g_ref, w_ref, wsc_ref, wlru_ref, blru_ref, wa_ref, wx_ref, ba_ref, bx_ref, lam_ref,
             hn_ref, proj_ref, ymix_ref, h_ref, buf_ref, halo_ref, hc_ref, stage_ref):
        i = pl.program_id(0)

        @pl.when(i == 0)
        def _():
            buf_ref[1] = jnp.zeros((tm, din), F32)
            halo_ref[...] = jnp.zeros_like(halo_ref)

        @pl.when(i <= 1)
        def _():
            hc_ref[...] = jnp.zeros_like(hc_ref)

        def step(dst, src):
            xhat, _ = _rms(x_ref[...])
            hn = ((xhat * g_ref[...]) * (1.0 + mod_ref[1:2, :]) + mod_ref[0:1, :]).astype(BF16)
            hn_ref[...] = hn
            n_mix = (tm // sub) * nblk
            n_chunk = din // width

            def project(k):
                res = _dot(hn_ref[...], w_ref[k * width:(k + 1) * width, :], NT)
                proj_ref[:, k * width:(k + 1) * width] = res
                dst[:, k * width:(k + 1) * width] = res

            done = 0
            for half in range(tm // sub):
                r0 = half * sub
                rows = slice(r0, r0 + sub)
                for j in range(nblk):
                    lo = j * LANES
                    while done < n_chunk and done * n_mix <= (half * nblk + j) * n_chunk:
                        project(done)
                        done += 1

                    def col(p):
                        return src[rows, p * width + lo:p * width + lo + LANES]

                    def prev(p):
                        c0 = p * width + lo
                        if half == 0:
                            return halo_ref[:, c0:c0 + LANES]
                        return src[r0 - SUBLANES:r0, c0:c0 + LANES]

                    pp = col(1) * col(2)
                    q, _, _ = _conv3(pp, prev(1) * prev(2), wsc_ref, lo, stage_ref.at[0])
                    ymix_ref[rows, lo:lo + LANES] = (col(0) * q).astype(BF16)

                    u, _, _, _ = _conv4(col(4), prev(4), wlru_ref, blru_ref, lo, stage_ref.at[1])
                    sp = _softplus(-lam_ref[:, lo:lo + LANES])
                    _, r, ig, a, mult = _lru_gates(u, wa_ref[j], wx_ref[j], ba_ref[:, lo:lo + LANES],
                                                   bx_ref[:, lo:lo + LANES], sp)
                    h, ends = _scan_tile(a, mult * (ig * u), hc_ref[0:1, lo:lo + LANES], stage_ref, 2, False)
                    h_ref[rows, lo:lo + LANES] = h
                    hc_ref[0:1, lo:lo + LANES] = ends[SUBLANES - 1:SUBLANES, :]
                    gel, _ = _gelu(col(3))
                    ymix_ref[rows, width + lo:width + lo + LANES] = (gel * h).astype(BF16)
            while done < n_chunk:
                project(done)
                done += 1
            halo_ref[...] = src[tm - SUBLANES:tm, :]

        @pl.when(i % 2 == 0)
        def _():
            step(buf_ref.at[0], buf_ref.at[1])

        @pl.when(i % 2 == 1)
        def _():
            step(buf_ref.at[1], buf_ref.at[0])

    small = [conv_sc, conv_lru, conv_b, wa_bd, wx_bd, ba, bx, lam]
    cur = lambda i: (jnp.minimum(i, nt - 1), 0)
    last = lambda i: (jnp.maximum(i - 1, 0), 0)
    outs = _call(
        body, "mix_in_mixer_fwd", (nt + 1,),
        [pl.BlockSpec((tm, d), cur), _full(mod6.shape), _full(g_mix.shape), _full(w_in_t.shape)]
        + [_full(a.shape) for a in small],
        [pl.BlockSpec((tm, d), cur), pl.BlockSpec((tm, din), cur),
         pl.BlockSpec((tm, 2 * width), last), pl.BlockSpec((tm, width), last)],
        [jax.ShapeDtypeStruct((s, d), BF16), jax.ShapeDtypeStruct((s, din), F32),
         jax.ShapeDtypeStruct((s, 2 * width), BF16), jax.ShapeDtypeStruct((s, width), F32)],
        [x2d, mod6, g_mix, w_in_t, *small],
        scratch=[pltpu.VMEM((2, tm, din), F32), pltpu.VMEM((SUBLANES, din), F32), pltpu.VMEM((SUBLANES, width), F32),
                 pltpu.VMEM((4, sub + 2 * SUBLANES, LANES), F32)])
    return outs


def _mix_out_fwd(ymix, x2d, w_out, mod6, g_mlp, tm):
    s, d = x2d.shape

    def body(y_ref, x_ref, w_ref, mod_ref, g_ref, mix_ref, x2_ref, hn_ref):
        mix = _dot(y_ref[...], w_ref[...], NN)
        mix_ref[...] = mix.astype(BF16)
        x2 = x_ref[...] + mod_ref[2:3, :] * mix
        x2_ref[...] = x2
        xhat, _ = _rms(x2)
        hn_ref[...] = ((xhat * g_ref[...]) * (1.0 + mod_ref[4:5, :]) + mod_ref[3:4, :]).astype(BF16)

    tile = pl.BlockSpec((tm, d), lambda i: (i, 0))
    return _call(
        body, "mix_out_fwd", (s // tm,),
        [tile, tile, _full(w_out.shape), _full(mod6.shape), _full(g_mlp.shape)],
        [tile, tile, tile],
        [jax.ShapeDtypeStruct((s, d), BF16), jax.ShapeDtypeStruct((s, d), F32), jax.ShapeDtypeStruct((s, d), BF16)],
        [ymix, x2d, w_out, mod6, g_mlp])


def _mlp_fwd_loss(hn2, w_up_t, w_down, x2, target, mod6, g_final, tm, tk):
    s, d = hn2.shape
    f = w_up_t.shape[0]
    nk = f // tk

    def body(hn_ref, wu_ref, wd_ref, x2_hbm, t_hbm, mod_ref, g_ref, z_ref, dx3_ref, dyb_ref, st_ref,
             y_ref, x2_ref, t_ref, sems):
        i, k = pl.program_id(0), pl.program_id(1)

        def fetch():
            rows = pl.ds(pl.multiple_of(i * tm, tm), tm)
            return (pltpu.make_async_copy(x2_hbm.at[rows, :], x2_ref, sems.at[0]),
                    pltpu.make_async_copy(t_hbm.at[rows, :], t_ref, sems.at[1]))

        @pl.when(jnp.logical_and(i == 0, k == 0))
        def _():
            st_ref[...] = jnp.zeros_like(st_ref)

        @pl.when(k == 0)
        def _():
            for cp in fetch():
                cp.start()
            y_ref[...] = jnp.zeros_like(y_ref)

        z = jnp.maximum(_dot(hn_ref[...], wu_ref[...], NT), 0.0)
        z_ref[...] = z.astype(BF16)
        y_ref[...] += _dot((z * z).astype(BF16), wd_ref[...], NN)

        @pl.when(k == nk - 1)
        def _():
            for cp in fetch():
                cp.wait()
            gate = mod_ref[5:6, :]
            yv = y_ref[...]
            xhat, rstd = _rms(x2_ref[...] + gate * yv)
            diff = xhat * g_ref[...] - t_ref[...]
            dyo = diff * (1.0 / d)
            dx3 = _rms_bwd(dyo * g_ref[...], xhat, rstd)
            dx3_ref[...] = dx3.astype(BF16)
            dyb_ref[...] = (gate * dx3).astype(BF16)
            st_ref[0:1, :] += _colsum(dyo * xhat)
            st_ref[1:2, :] += _colsum(dx3 * yv)
            st_ref[2:3, :] += _colsum(diff * diff)

    tile = pl.BlockSpec((tm, d), lambda i, k: (i, 0))
    wblk = pl.BlockSpec((tk, d), lambda i, k: (k, 0))
    return pl.pallas_call(
        body, name="mlp_fwd_loss", grid=(s // tm, nk),
        in_specs=[tile, wblk, wblk, ANY, ANY, _full(mod6.shape), _full(g_final.shape)],
        out_specs=[pl.BlockSpec((tm, tk), lambda i, k: (i, k)), tile, tile, _full((SUBLANES, d))],
        out_shape=[jax.ShapeDtypeStruct((s, f), BF16), jax.ShapeDtypeStruct((s, d), BF16),
                   jax.ShapeDtypeStruct((s, d), BF16), jax.ShapeDtypeStruct((SUBLANES, d), F32)],
        scratch_shapes=[pltpu.VMEM((tm, d), F32), pltpu.VMEM((tm, d), F32), pltpu.VMEM((tm, d), F32),
                        pltpu.SemaphoreType.DMA((2,))],
        compiler_params=pltpu.CompilerParams(dimension_semantics=("arbitrary", "arbitrary"),
                                             vmem_limit_bytes=VMEM_LIMIT_BIG),
    )(hn2, w_up_t, w_down, x2, target, mod6, g_final)


def _mlp_bwd_dx(dyb, z, w_down, w_up_t, tm, tk):
    s, d = dyb.shape
    f = z.shape[1]

    nk = f // tk

    def body(dy_ref, z_ref, wd_ref, wu_ref, dz_ref, dh_ref, acc_ref):
        k = pl.program_id(1)

        @pl.when(k == 0)
        def _():
            acc_ref[...] = jnp.zeros_like(acc_ref)

        dz = ((2.0 * z_ref[...].astype(F32)) * _dot(dy_ref[...], wd_ref[...], NT)).astype(BF16)
        dz_ref[...] = dz
        acc_ref[...] += _dot(dz, wu_ref[...], NN)

        @pl.when(k == nk - 1)
        def _():
            dh_ref[...] = acc_ref[...].astype(BF16)

    return pl.pallas_call(
        body, name="mlp_bwd_dx", grid=(s // tm, nk),
        in_specs=[pl.BlockSpec((tm, d), lambda i, k: (i, 0)), pl.BlockSpec((tm, tk), lambda i, k: (i, k)),
                  pl.BlockSpec((tk, d), lambda i, k: (k, 0)), pl.BlockSpec((tk, d), lambda i, k: (k, 0))],
        out_specs=[pl.BlockSpec((tm, tk), lambda i, k: (i, k)), pl.BlockSpec((tm, d), lambda i, k: (i, 0))],
        out_shape=[jax.ShapeDtypeStruct((s, f), BF16), jax.ShapeDtypeStruct((s, d), BF16)],
        scratch_shapes=[pltpu.VMEM((tm, d), F32)],
        compiler_params=_params(("parallel", "arbitrary")),
    )(dyb, z, w_down, w_up_t)


def _mlp_bwd_dw(z, dz, dyb, hn2, tm, tk):
    s, d = dyb.shape
    f = z.shape[1]

    def body(z_ref, dz_ref, dy_ref, hn_ref, gd_ref, gu_ref):
        i = pl.program_id(1)

        @pl.when(i == 0)
        def _():
            gd_ref[...] = jnp.zeros_like(gd_ref)
            gu_ref[...] = jnp.zeros_like(gu_ref)

        zf = z_ref[...].astype(F32)
        gd_ref[...] += _dot((zf * zf).astype(BF16), dy_ref[...], TN)
        gu_ref[...] += _dot(dz_ref[...], hn_ref[...], TN)

    return pl.pallas_call(
        body, name="mlp_bwd_dw", grid=(f // tk, s // tm),
        in_specs=[pl.BlockSpec((tm, tk), lambda k, i: (i, k)), pl.BlockSpec((tm, tk), lambda k, i: (i, k)),
                  pl.BlockSpec((tm, d), lambda k, i: (i, 0)), pl.BlockSpec((tm, d), lambda k, i: (i, 0))],
        out_specs=[pl.BlockSpec((tk, d), lambda k, i: (k, 0)), pl.BlockSpec((tk, d), lambda k, i: (k, 0))],
        out_shape=[jax.ShapeDtypeStruct((f, d), F32), jax.ShapeDtypeStruct((f, d), F32)],
        compiler_params=_params(("parallel", "arbitrary")),
    )(z, dz, dyb, hn2)


def _mix_out_bwd(dhn2, x2, dx3, mix, ymix, w_out, mod6, g_mlp, tm):
    s, d = x2.shape

    def body(dh_ref, x2_ref, dx3_ref, mix_ref, y_ref, w_ref, mod_ref, g_ref, dx2_ref, dym_ref, gw_ref, st_ref):
        i = pl.program_id(0)

        @pl.when(i == 0)
        def _():
            st_ref[...] = jnp.zeros_like(st_ref)
            gw_ref[...] = jnp.zeros_like(gw_ref)

        dh = dh_ref[...].astype(F32)
        xhat, rstd = _rms(x2_ref[...])
        dn = dh * (1.0 + mod_ref[4:5, :])
        dx2 = dx3_ref[...].astype(F32) + _rms_bwd(dn * g_ref[...], xhat, rstd)
        dx2_ref[...] = dx2.astype(BF16)
        st_ref[0:1, :] += _colsum(dh)
        st_ref[1:2, :] += _colsum(dh * (xhat * g_ref[...]))
        st_ref[2:3, :] += _colsum(dn * xhat)
        st_ref[3:4, :] += _colsum(dx2 * mix_ref[...].astype(F32))
        dmix = (mod_ref[2:3, :] * dx2).astype(BF16)
        dym_ref[...] = _dot(dmix, w_ref[...], NT).astype(BF16)
        gw_ref[...] += _dot(y_ref[...], dmix, TN)

    tile = pl.BlockSpec((tm, d), lambda i: (i, 0))
    return _call(
        body, "mix_out_bwd", (s // tm,),
        [tile, tile, tile, tile, tile, _full(w_out.shape), _full(mod6.shape), _full(g_mlp.shape)],
        [tile, tile, _full((d, d)), _full((SUBLANES, d))],
        [jax.ShapeDtypeStruct((s, d), BF16), jax.ShapeDtypeStruct((s, d), BF16),
         jax.ShapeDtypeStruct((d, d), F32), jax.ShapeDtypeStruct((SUBLANES, d), F32)],
        [dhn2, x2, dx3, mix, ymix, w_out, mod6, g_mlp])


def _mixer_bwd(proj, dymix, h_all, conv_sc, conv_lru, conv_b, wa_bd, wx_bd, ba, bx, lam, width):
    s, din = proj.shape
    t = min(MIX_ROWS, s)
    nt = s // t
    nblk = width // LANES
    hb = t // SUBLANES
    last8 = s // SUBLANES - 1

    def body(proj_ref, projp_ref, projn_ref, dy_ref, dyn_ref, h_ref, hp_ref,
             wsc_ref, wlru_ref, blru_ref, wa_ref, wx_ref, ba_ref, bx_ref, lam_ref,
             dproj_ref, small_ref, gwa_ref, gwx_ref, an_ref, gn_ref, dun_ref, stage_ref):
        i = pl.program_id(0)

        @pl.when(i == 0)
        def _():
            small_ref[...] = jnp.zeros_like(small_ref)
            gwa_ref[...] = jnp.zeros_like(gwa_ref)
            gwx_ref[...] = jnp.zeros_like(gwx_ref)
            an_ref[...] = jnp.zeros_like(an_ref)
            gn_ref[...] = jnp.zeros_like(gn_ref)
            dun_ref[...] = jnp.zeros_like(dun_ref)

        has_prev = i < nt - 1
        has_next = i > 0
        for j in range(nblk):
            lo = j * LANES
            ls = slice(lo, lo + LANES)

            def col(p, ref=proj_ref):
                return ref[:, p * width + lo:p * width + lo + LANES]

            def prev(p):
                return jnp.where(has_prev, col(p, projp_ref), 0.0)

            def nxt(p):
                return jnp.where(has_next, col(p, projn_ref), 0.0)

            def add_row(r, v):
                small_ref[r:r + 1, ls] += _colsum(v)

            sc_b, sc_c, sc_x = col(0), col(1), col(2)
            p = sc_c * sc_x
            q, p1, p2 = _conv3(p, prev(1) * prev(2), wsc_ref, lo, stage_ref.at[0])
            dys = dy_ref[:, ls].astype(F32)
            dproj_ref[:, ls] = (dys * q).astype(BF16)
            dq = dys * sc_b
            dqn = jnp.where(has_next, dyn_ref[:, ls].astype(F32)[0:SUBLANES], 0.0) * nxt(0)
            _, (dq1, dq2) = _staged_shifts(stage_ref.at[1], dq, None, dqn, (), (1, 2))
            dp = (wsc_ref[2:3, ls] * dq + wsc_ref[1:2, ls] * dq1) + wsc_ref[0:1, ls] * dq2
            dproj_ref[:, width + lo:width + lo + LANES] = (dp * sc_x).astype(BF16)
            dproj_ref[:, 2 * width + lo:2 * width + lo + LANES] = (dp * sc_c).astype(BF16)
            add_row(0, dq * p2)
            add_row(1, dq * p1)
            add_row(2, dq * p)

            xv = col(4)
            u, x1, x2, x3 = _conv4(xv, prev(4), wlru_ref, blru_ref, lo, stage_ref.at[2])
            lam_v = lam_ref[:, ls]
            sp = _softplus(-lam_v)
            wa, wx = wa_ref[j], wx_ref[j]
            ub, r, ig, a, mult = _lru_gates(u, wa, wx, ba_ref[:, ls], bx_ref[:, ls], sp)
            iu = ig * u
            h = h_ref[:, ls]
            (hm1,), _ = _staged_shifts(stage_ref.at[3], h, jnp.where(has_prev, hp_ref[:, ls], 0.0), None, (1,), ())
            lyv = col(3)
            gel, th = _gelu(lyv)
            dyl = dy_ref[:, width + lo:width + lo + LANES].astype(F32)
            dproj_ref[:, 3 * width + lo:3 * width + lo + LANES] = (dyl * h * _dgelu(lyv, th)).astype(BF16)
            a_next = jnp.broadcast_to(an_ref[0:1, ls], (SUBLANES, LANES))
            _, (a_up,) = _staged_shifts(stage_ref.at[4], a, None, a_next, (), (1,))
            g, _ = _scan_tile(a_up, dyl * gel, gn_ref[0:1, ls], stage_ref, 5, True)
            an_ref[0:1, ls] = a[0:1, :]
            gn_ref[0:1, ls] = g[0:1, :]
            da = g * hm1
            dmult = g * iu
            diu = g * mult
            dlog_a = da * a - dmult * ((a * a) / mult)
            dpre_a = (dlog_a * (-RG_C * sp)) * (r * (1.0 - r))
            dpre_x = (diu * u) * (ig * (1.0 - ig))
            dab, dxb = dpre_a.astype(BF16), dpre_x.astype(BF16)
            du = diu * ig + _dot(dab, wa, NT) + _dot(dxb, wx, NT)
            gwa_ref[j] += _dot(ub, dab, TN)
            gwx_ref[j] += _dot(ub, dxb, TN)
            dun = dun_ref[:, ls]
            dun_ref[:, ls] = du[0:SUBLANES, :]
            _, (du1, du2, du3) = _staged_shifts(stage_ref.at[7], du, None, dun, (), (1, 2, 3))
            dlx = (((wlru_ref[3:4, ls] * du + wlru_ref[2:3, ls] * du1) + wlru_ref[1:2, ls] * du2)
                   + wlru_ref[0:1, ls] * du3)
            dproj_ref[:, 4 * width + lo:4 * width + lo + LANES] = dlx.astype(BF16)
            add_row(3, du * x3)
            add_row(4, du * x2)
            add_row(5, du * x1)
            add_row(6, du * xv)
            add_row(7, du)
            add_row(8, dpre_a)
            add_row(9, dpre_x)
            add_row(10, (dlog_a * (RG_C * r)) * jax.nn.sigmoid(-lam_v))

    small = [conv_sc, conv_lru, conv_b, wa_bd, wx_bd, ba, bx, lam]
    rev = lambda i: nt - 1 - i
    return _call(
        body, "mixer_bwd", (nt,),
        [pl.BlockSpec((t, din), lambda i: (rev(i), 0)),
         pl.BlockSpec((SUBLANES, din), lambda i: (jnp.maximum(rev(i) * hb - 1, 0), 0)),
         pl.BlockSpec((SUBLANES, din), lambda i: (jnp.minimum((rev(i) + 1) * hb, last8), 0)),
         pl.BlockSpec((t, 2 * width), lambda i: (rev(i), 0)),
         pl.BlockSpec((2 * SUBLANES, 2 * width), lambda i: (jnp.minimum((rev(i) + 1) * (hb // 2), last8 // 2), 0)),
         pl.BlockSpec((t, width), lambda i: (rev(i), 0)),
         pl.BlockSpec((SUBLANES, width), lambda i: (jnp.maximum(rev(i) * hb - 1, 0), 0))]
        + [_full(a.shape) for a in small],
        [pl.BlockSpec((t, din), lambda i: (rev(i), 0)), _full((2 * SUBLANES, width)),
         _full(wa_bd.shape), _full(wx_bd.shape)],
        [jax.ShapeDtypeStruct((s, din), BF16), jax.ShapeDtypeStruct((2 * SUBLANES, width), F32),
         jax.ShapeDtypeStruct(wa_bd.shape, F32), jax.ShapeDtypeStruct(wx_bd.shape, F32)],
        [proj, proj, proj, dymix, dymix, h_all, h_all, *small],
        scratch=[pltpu.VMEM((SUBLANES, width), F32), pltpu.VMEM((SUBLANES, width), F32),
                 pltpu.VMEM((SUBLANES, width), F32), pltpu.VMEM((8, t + 2 * SUBLANES, LANES), F32)])


def _mix_in_bwd_dx(dproj, x2d, dx2, w_in_t, mod6, g_mix, tm):
    s, d = x2d.shape
    din = dproj.shape[1]

    def body(dp_ref, x_ref, dx2_ref, w_ref, mod_ref, g_ref, gx_ref, st_ref):
        i = pl.program_id(0)

        @pl.when(i == 0)
        def _():
            st_ref[...] = jnp.zeros_like(st_ref)

        dh = _dot(dp_ref[...], w_ref[...], NN)
        xhat, rstd = _rms(x_ref[...])
        dn = dh * (1.0 + mod_ref[1:2, :])
        gx_ref[...] = dx2_ref[...].astype(F32) + _rms_bwd(dn * g_ref[...], xhat, rstd)
        st_ref[0:1, :] += _colsum(dh)
        st_ref[1:2, :] += _colsum(dh * (xhat * g_ref[...]))
        st_ref[2:3, :] += _colsum(dn * xhat)

    tile = pl.BlockSpec((tm, d), lambda i: (i, 0))
    return _call(
        body, "mix_in_bwd_dx", (s // tm,),
        [pl.BlockSpec((tm, din), lambda i: (i, 0)), tile, tile, _full(w_in_t.shape), _full(mod6.shape),
         _full(g_mix.shape)],
        [tile, _full((SUBLANES, d))],
        [jax.ShapeDtypeStruct((s, d), F32), jax.ShapeDtypeStruct((SUBLANES, d), F32)],
        [dproj, x2d, dx2, w_in_t, mod6, g_mix], vmem=VMEM_LIMIT_BIG)


def _mix_in_bwd_dw(dproj, hn1, tm, tn):
    s, d = hn1.shape
    din = dproj.shape[1]

    def body(dp_ref, hn_ref, gw_ref):
        i = pl.program_id(1)

        @pl.when(i == 0)
        def _():
            gw_ref[...] = jnp.zeros_like(gw_ref)

        gw_ref[...] += _dot(dp_ref[...], hn_ref[...], TN)

    return _call(
        body, "mix_in_bwd_dw", (din // tn, s // tm),
        [pl.BlockSpec((tm, tn), lambda p, i: (i, p)), pl.BlockSpec((tm, d), lambda p, i: (i, 0))],
        [pl.BlockSpec((tn, d), lambda p, i: (p, 0))],
        [jax.ShapeDtypeStruct((din, d), F32)],
        [dproj, hn1])


def _adamw(w, g, m, v):
    m = ADAM_B1 * m + (1.0 - ADAM_B1) * g
    v = ADAM_B2 * v + (1.0 - ADAM_B2) * (g * g)
    m_hat = m / (1.0 - ADAM_B1 ** ADAM_STEP)
    v_hat = v / (1.0 - ADAM_B2 ** ADAM_STEP)
    delta = -ADAM_LR * (m_hat / (jnp.sqrt(v_hat) + ADAM_EPS) + ADAM_WD * w)
    return delta, m, v


def _pair_sum(g4s, h4s, core_chip, tr, name):
    na = len(g4s)
    _, _, r, n = g4s[0].shape

    def body(sc_ref, *refs):
        q = pl.program_id(1)
        for a in range(na):
            g_ref, h_ref = refs[2 * a], refs[2 * a + 1]
            sb_ref, own_ref = refs[2 * na + 2 * a], refs[2 * na + 2 * a + 1]
            ssum = g_ref[...] + h_ref[...]
            sb_ref[...] = ssum.astype(BF16)

            @pl.when(q == sc_ref[1])
            def _():
                own_ref[...] = ssum

    grid_spec = pltpu.PrefetchScalarGridSpec(
        num_scalar_prefetch=1, grid=(r // tr, 4),
        in_specs=[pl.BlockSpec((None, None, tr, n), lambda i, q, sc: (q, sc[0], i, 0)),
                  pl.BlockSpec((None, tr, n), lambda i, q, sc: (q, i, 0))] * na,
        out_specs=[pl.BlockSpec((None, tr, n), lambda i, q, sc: (q, i, 0)),
                   pl.BlockSpec((tr, n), lambda i, q, sc: (i, 0))] * na)
    outs = pl.pallas_call(
        body, name=name, grid_spec=grid_spec,
        out_shape=[jax.ShapeDtypeStruct((4, r, n), BF16), jax.ShapeDtypeStruct((r, n), F32)] * na,
        compiler_params=_params(("parallel", "arbitrary")),
    )(core_chip, *[x for pair in zip(g4s, h4s) for x in pair])
    return [(outs[2 * a], outs[2 * a + 1]) for a in range(na)]


def _sum4_adam(own, parts, w, m, v, tr, name, transposed):
    r, n = own.shape
    rows, cols = w.shape

    def body(o_ref, p_ref, w_ref, m_ref, v_ref, g_ref, d_ref, nm_ref, nv_ref):
        g = o_ref[...]
        for k in range(3):
            g = g + p_ref[k].astype(F32)
        if transposed:
            g = g.T
        g_ref[...] = g
        d_ref[...], nm_ref[...], nv_ref[...] = _adamw(w_ref[...], g, m_ref[...], v_ref[...])

    if transposed:
        g_specs = [pl.BlockSpec((r, tr), lambda i: (0, i)), pl.BlockSpec((3, r, tr), lambda i: (0, 0, i))]
    else:
        g_specs = [pl.BlockSpec((tr, n), lambda i: (i, 0)), pl.BlockSpec((3, tr, n), lambda i: (0, i, 0))]
    tile = pl.BlockSpec((tr, cols), lambda i: (i, 0))
    return pl.pallas_call(
        body, name=name, grid=(rows // tr,),
        in_specs=g_specs + [tile] * 3, out_specs=[tile] * 4,
        out_shape=[jax.ShapeDtypeStruct((rows, cols), F32)] * 4,
        compiler_params=_params(("parallel",)),
    )(own, parts, w, m, v)


def _sum8(parts, tr, name):
    _, rows, n = parts.shape

    def body(p_ref, o_ref):
        acc = p_ref[0]
        for k in range(1, N_DEV):
            acc = acc + p_ref[k]
        o_ref[...] = acc

    return pl.pallas_call(
        body, name=name, grid=(rows // tr,),
        in_specs=[pl.BlockSpec((N_DEV, tr, n), lambda i: (0, i, 0))],
        out_specs=pl.BlockSpec((tr, n), lambda i: (i, 0)),
        out_shape=jax.ShapeDtypeStruct((rows, n), F32),
        compiler_params=_params(("parallel",)),
    )(parts)


def _ada_bwd_adam(cact_t, dmod_cols, w, m, v, tr):
    rows, n = w.shape

    def body(c_ref, d_ref, w_ref, m_ref, v_ref, g_ref, dl_ref, nm_ref, nv_ref):
        def term(b):
            return c_ref[b].astype(BF16).astype(F32) * d_ref[b:b + 1, :].astype(BF16).astype(F32)

        g = term(0)
        for b in range(1, N_DEV):
            g = g + term(b)
        g_ref[...] = g
        dl_ref[...], nm_ref[...], nv_ref[...] = _adamw(w_ref[...], g, m_ref[...], v_ref[...])

    tile = pl.BlockSpec((tr, n), lambda i: (i, 0))
    return pl.pallas_call(
        body, name="ada_bwd_adam", grid=(rows // tr,),
        in_specs=[pl.BlockSpec((N_DEV, tr, 1), lambda i: (0, i, 0)), _full(dmod_cols.shape), tile, tile, tile],
        out_specs=[tile] * 4,
        out_shape=[jax.ShapeDtypeStruct((rows, n), F32)] * 4,
        compiler_params=_params(("parallel",)),
    )(cact_t, dmod_cols, w, m, v)


def _adam_small(ws, gs, ms, vs):
    n = len(ws)

    def body(*refs):
        w_r, g_r, m_r, v_r = refs[:n], refs[n:2 * n], refs[2 * n:3 * n], refs[3 * n:4 * n]
        d_r, nm_r, nv_r = refs[4 * n:5 * n], refs[5 * n:6 * n], refs[6 * n:7 * n]
        for k in range(n):
            d_r[k][...], nm_r[k][...], nv_r[k][...] = _adamw(w_r[k][...], g_r[k][...], m_r[k][...], v_r[k][...])

    shapes = [jax.ShapeDtypeStruct(w.shape, F32) for w in ws]
    outs = pl.pallas_call(
        body, name="adam_small", out_shape=shapes * 3, compiler_params=_params(),
    )(*ws, *gs, *ms, *vs)
    return outs[:n], outs[n:2 * n], outs[2 * n:]


def _block_diag(w):
    h, hd, _ = w.shape
    per = LANES // hd
    eye = jnp.eye(per, dtype=w.dtype)
    w5 = w.reshape(h // per, per, hd, 1, hd) * eye[None, :, None, :, None]
    return w5.reshape(h // per, LANES, LANES)


def _block_diag_grad(g, h, hd):
    per = LANES // hd
    g5 = g.reshape(h // per, per, hd, per, hd)
    return jnp.stack([g5[:, a, :, a, :] for a in range(per)], axis=1).reshape(h, hd, hd)


def kernel(x, c, w_ada, b_ada, g_mix, w_in, conv_w_sc, conv_w_lru, conv_b_lru, w_rg_a, b_rg_a, w_rg_x, b_rg_x, lru_lambda, w_out, g_mlp, w_up, w_down, g_final, loss_target, m_w_ada, m_b_ada, m_g_mix, m_w_in, m_conv_w_sc, m_conv_w_lru, m_conv_b_lru, m_w_rg_a, m_b_rg_a, m_w_rg_x, m_b_rg_x, m_lru_lambda, m_w_out, m_g_mlp, m_w_up, m_w_down, m_g_final, v_w_ada, v_b_ada, v_g_mix, v_w_in, v_conv_w_sc, v_conv_w_lru, v_conv_b_lru, v_w_rg_a, v_b_rg_a, v_w_rg_x, v_b_rg_x, v_lru_lambda, v_w_out, v_g_mlp, v_w_up, v_w_down, v_g_final):
    s, d = x.shape[1], x.shape[2]
    width = conv_b_lru.shape[1]
    heads, hd = w_rg_a.shape[1], w_rg_a.shape[2]
    n_ada = w_ada.shape[2]
    csh = conv_w_sc.shape[2]
    me = 4 * lax.axis_index("x") + 2 * lax.axis_index("y") + lax.axis_index("c")
    tm = min(512, s)
    tm_mlp = min(1024, s)
    tk = 512

    x2d = x[0]
    tgt = loss_target[0]

    pay = jnp.zeros((SUBLANES, d), F32)
    pay = pay.at[0:1, :].set(c)
    pay = pay.at[1:4, 0:csh].set(conv_w_sc[0])
    pay = pay.at[4:8, 0:csh].set(conv_w_lru[0])
    w_in_t_sh = w_in[0].T.astype(BF16)
    w_up_t_sh = w_up[0].T.astype(BF16)
    w_out_sh = w_out[0].astype(BF16)
    w_down_sh = w_down[0].astype(BF16)
    (w_in_t,) = _seq_gather2("gather_w_in", 10, [w_in_t_sh])
    (pay_all,) = _exchange("gather_in", [pay], [])
    w_in_t = w_in_t.reshape(-1, d)
    c_all = pay_all[:, 0, :]
    conv_sc = pay_all[:, 1:4, 0:csh].transpose(1, 0, 2).reshape(3, width)
    conv_lru = pay_all[:, 4:8, 0:csh].transpose(1, 0, 2).reshape(4, width)

    b_ada_sh = lax.dynamic_slice(b_ada, (0, me * n_ada), (1, n_ada))
    mod_cols, c_act = _ada_fwd(c_all, w_ada[0], b_ada_sh)
    (mod_rows,) = _exchange("scatter_mod", [], [mod_cols.reshape(N_DEV, 1, n_ada)])
    mod_rows, w_out_sh, w_up_t_sh, w_down_sh = lax.optimization_barrier((mod_rows, w_out_sh, w_up_t_sh, w_down_sh))
    (w_out_g,) = _seq_gather2("gather_w_out", 1, [w_out_sh])
    w_up_g, w_down_g = _seq_gather2("gather_mlp_weights", 2, [w_up_t_sh, w_down_sh])
    mod6 = jnp.zeros((SUBLANES, d), F32).at[0:6, :].set(mod_rows.reshape(6, d))

    wa_bd = _block_diag(w_rg_a[0]).astype(BF16)
    wx_bd = _block_diag(w_rg_x[0]).astype(BF16)
    ba = b_rg_a.reshape(1, width)
    bx = b_rg_x.reshape(1, width)
    g_fin = g_final.reshape(1, d)

    hn1, proj, ymix, h_all = _mix_in_mixer_fwd(x2d, mod6, g_mix, w_in_t, conv_sc, conv_lru, conv_b_lru,
                                               wa_bd, wx_bd, ba, bx, lru_lambda, width, tm)
    w_out_b = w_out_g.reshape(-1, d)
    mix, x2, hn2 = _mix_out_fwd(ymix, x2d, w_out_b, mod6, g_mlp, tm_mlp)
    w_up_t = w_up_g.reshape(-1, d)
    w_down_b = w_down_g.reshape(-1, d)
    z, dx3, dyb, st_fin = _mlp_fwd_loss(hn2, w_up_t, w_down_b, x2, tgt, mod6, g_fin, tm_mlp, 2 * tk)

    core_chip = jnp.stack([lax.axis_index("c"), 2 * lax.axis_index("x") + lax.axis_index("y")]).astype(jnp.int32)
    dz, dhn2 = _mlp_bwd_dx(dyb, z, w_down_b, w_up_t, tm_mlp, 2 * tk)
    g_down, g_up_t = _mlp_bwd_dw(z, dz, dyb, hn2, tm_mlp, 2 * tk)
    g_up4, g_down4 = g_up_t.reshape(4, 2, -1, d), g_down.reshape(4, 2, -1, d)
    h_up, h_down = _seq_pair_swap("swap_mlp_grads", 7, [g_up4, g_down4])
    dx2, dymix, g_out, st_out = _mix_out_bwd(dhn2, x2, dx3, mix, ymix, w_out_b, mod6, g_mlp, tm)
    h_up, h_down, g_out = lax.optimization_barrier((h_up, h_down, g_out))
    (sb_up, own_up), (sb_down, own_down) = _pair_sum([g_up4, g_down4], [h_up, h_down], core_chip, g_up4.shape[2], "pair_sum_mlp")
    g_out4 = g_out.reshape(4, 2, -1, d)
    (h_out,) = _seq_pair_swap("swap_w_out_grad", 8, [g_out4])
    p_up, p_down = _seq_chip_exchange("exchange_mlp_grads", 3, [sb_up, sb_down])
    dproj, g_small, g_wa, g_wx = _mixer_bwd(
        proj, dymix, h_all, conv_sc, conv_lru, conv_b_lru, wa_bd, wx_bd, ba, bx, lru_lambda, width)
    h_out, dproj = lax.optimization_barrier((h_out, dproj))
    ((sb_out, own_out),) = _pair_sum([g_out4], [h_out], core_chip, g_out4.shape[2], "pair_sum_w_out")
    (p_out,) = _seq_chip_exchange("exchange_w_out_grad", 4, [sb_out])
    grad_x, st_in = _mix_in_bwd_dx(dproj, x2d, dx2, w_in_t, mod6, g_mix, tm_mlp)

    small = jnp.concatenate([
        st_in[0:2], st_out[3:4], st_out[0:2], st_fin[1:2],
        st_in[2:3], st_out[2:3], st_fin[0:1],
        jnp.concatenate([g_small[7:8], g_small[10:11]], axis=1),
        jnp.concatenate([g_small[8:9], g_small[9:10]], axis=1),
        jnp.concatenate([jnp.concatenate([g_small[0:3], jnp.zeros((1, width), F32)], axis=0), g_small[3:7]], axis=1),
        st_fin[2:3],
        _block_diag_grad(g_wa, heads, hd).reshape(-1, d),
        _block_diag_grad(g_wx, heads, hd).reshape(-1, d),
    ], axis=0)

    (small_all,) = _seq_gather2("gather_small_grads", 5, [small])
    g_in_t, = _mix_in_bwd_dw(dproj, hn1, min(2048, s), dproj.shape[1] // 2)
    g_in4 = g_in_t.reshape(4, 2, -1, d)
    (h_in,) = _seq_pair_swap("swap_w_in_grad", 9, [g_in4])
    p_up, p_down, p_out, small_all, g_in_t = lax.optimization_barrier((p_up, p_down, p_out, small_all, g_in_t))

    ad_up = _sum4_adam(own_up, p_up, w_up[0], m_w_up[0], v_w_up[0], 256, "adam_w_up", True)
    h_in, ad_up = lax.optimization_barrier((h_in, ad_up))
    ((sb_in, own_in),) = _pair_sum([g_in4], [h_in], core_chip, g_in4.shape[2], "pair_sum_w_in")
    (p_in,) = _seq_chip_exchange("exchange_w_in_grad", 6, [sb_in])
    ad_out = _sum4_adam(own_out, p_out, w_out[0], m_w_out[0], v_w_out[0], w_out.shape[1], "adam_w_out", False)
    ad_down = _sum4_adam(own_down, p_down, w_down[0], m_w_down[0], v_w_down[0], 256, "adam_w_down", False)

    gsum = _sum8(small_all, SMALL_ROWS, "sum_small")
    loss = (0.5 / d) * jnp.sum(gsum[15])
    dmod_cols = lax.dynamic_slice(small_all[:, 0:6, :].reshape(N_DEV, 6 * d), (0, me * n_ada), (N_DEV, n_ada))
    g_ada, d_ada, nm_ada, nv_ada = _ada_bwd_adam(c_act[:, :, None], dmod_cols, w_ada[0], m_w_ada[0], v_w_ada[0], 256)

    g_conv = lax.dynamic_slice(gsum[11:15, 0:width], (0, me * csh), (4, csh))
    g_conv_l = lax.dynamic_slice(gsum[11:15, width:2 * width], (0, me * csh), (4, csh))
    small_g = [
        gsum[0:6].reshape(1, 6 * d),
        gsum[6:7],
        g_conv[0:3].reshape(1, 3, csh),
        g_conv_l.reshape(1, 4, csh),
        gsum[9:10, 0:width],
        gsum[16:48].reshape(1, heads, hd, hd),
        gsum[10:11, 0:width].reshape(1, heads, hd),
        gsum[48:80].reshape(1, heads, hd, hd),
        gsum[10:11, width:].reshape(1, heads, hd),
        gsum[9:10, width:],
        gsum[7:8],
        gsum[8],
    ]
    small_w = [b_ada, g_mix, conv_w_sc, conv_w_lru, conv_b_lru, w_rg_a, b_rg_a, w_rg_x, b_rg_x, lru_lambda, g_mlp, g_final]
    small_m = [m_b_ada, m_g_mix, m_conv_w_sc, m_conv_w_lru, m_conv_b_lru, m_w_rg_a, m_b_rg_a, m_w_rg_x, m_b_rg_x,
               m_lru_lambda, m_g_mlp, m_g_final]
    small_v = [v_b_ada, v_g_mix, v_conv_w_sc, v_conv_w_lru, v_conv_b_lru, v_w_rg_a, v_b_rg_a, v_w_rg_x, v_b_rg_x,
               v_lru_lambda, v_g_mlp, v_g_final]
    sd, snm, snv = _adam_small(small_w, small_g, small_m, small_v)
    p_in, ad_out, ad_down, (g_ada, d_ada, nm_ada, nv_ada), sd = lax.optimization_barrier(
        (p_in, ad_out, ad_down, (g_ada, d_ada, nm_ada, nv_ada), sd))
    ad_in = _sum4_adam(own_in, p_in, w_in[0].T, m_w_in[0].T, v_w_in[0].T, own_in.shape[0], "adam_w_in", False)
    ad_in = [a.T for a in ad_in]

    def order(ada, w_in_, w_out_, w_up_, w_down_, sm):
        return [ada[None], sm[0], sm[1], w_in_[None], sm[2], sm[3], sm[4], sm[5], sm[6], sm[7], sm[8], sm[9],
                w_out_[None], sm[10], w_up_[None], w_down_[None], sm[11]]

    grads = order(g_ada, ad_in[0], ad_out[0], ad_up[0], ad_down[0], small_g)
    deltas = order(d_ada, ad_in[1], ad_out[1], ad_up[1], ad_down[1], sd)
    new_m = order(nm_ada, ad_in[2], ad_out[2], ad_up[2], ad_down[2], snm)
    new_v = order(nv_ada, ad_in[3], ad_out[3], ad_up[3], ad_down[3], snv)
    return (loss, grad_x[None], *grads, *deltas, *new_m, *new_v)
```

```python
import jax
import jax.numpy as jnp
from jax import lax
from jax.experimental import pallas as pl
from jax.experimental.pallas import tpu as pltpu
from jax.experimental.pallas import tpu_sc as plsc

F32 = jnp.float32
BF16 = jnp.bfloat16
N_DEV = 8
EPS = 1e-6
RG_C = 8.0
GELU_K0 = 0.7978845608028654
GELU_K1 = 0.044715
ADAM_LR = 0.001
ADAM_B1 = 0.9
ADAM_B2 = 0.999
ADAM_EPS = 1e-08
ADAM_WD = 0.01
ADAM_STEP = 10
LANES = 128
SUBLANES = 8
VMEM_LIMIT = 52 * 1024 * 1024
VMEM_LIMIT_BIG = 58 * 1024 * 1024
MIX_ROWS = 256
SMALL_ROWS = 80

MESH = pl.DeviceIdType.MESH
ANY = pl.BlockSpec(memory_space=pl.ANY)
NN = ((1,), (0,))
NT = ((1,), (1,))
TN = ((0,), (0,))


def _dot(a, b, dims):
    return lax.dot_general(a, b, (dims, ((), ())), preferred_element_type=F32)


def _params(sem=None):
    return pltpu.CompilerParams(dimension_semantics=sem, vmem_limit_bytes=VMEM_LIMIT)


def _full(shape):
    nd = len(shape)
    return pl.BlockSpec(shape, lambda *_: (0,) * nd)


def _exchange(name, gathers, scatters):
    n_g = len(gathers)
    arrs = list(gathers) + list(scatters)
    n = len(arrs)
    out_shape = [jax.ShapeDtypeStruct((N_DEV,) + a.shape, a.dtype) for a in gathers]
    out_shape += [jax.ShapeDtypeStruct(a.shape, a.dtype) for a in scatters]

    def body(*refs):
        ins, outs = refs[:n], refs[n:2 * n]
        send_sems, recv_sems, local_sems = refs[2 * n:]
        x, y, c = lax.axis_index("x"), lax.axis_index("y"), lax.axis_index("c")
        me = 4 * x + 2 * y + c

        def src(a, dev):
            return ins[a] if a < n_g else ins[a].at[dev]

        def peer_of(k):
            px = 1 - x if (k >> 2) & 1 else x
            py = 1 - y if (k >> 1) & 1 else y
            pc = 1 - c if k & 1 else c
            return (px, py, pc), 4 * px + 2 * py + pc

        local = [pltpu.make_async_copy(src(a, me), outs[a].at[me], local_sems.at[a]) for a in range(n)]
        for cp in local:
            cp.start()
        sends = []
        for k in range(1, N_DEV):
            peer, pidx = peer_of(k)
            for a in range(n):
                cp = pltpu.make_async_remote_copy(
                    src_ref=src(a, pidx), dst_ref=outs[a].at[me],
                    send_sem=send_sems.at[a * (N_DEV - 1) + k - 1], recv_sem=recv_sems.at[a * (N_DEV - 1) + k - 1],
                    device_id=peer, device_id_type=MESH)
                cp.start()
                sends.append(cp)
        for k in range(1, N_DEV):
            peer, pidx = peer_of(k)
            for a in range(n):
                pltpu.make_async_remote_copy(
                    src_ref=src(a, pidx), dst_ref=outs[a].at[pidx],
                    send_sem=send_sems.at[a * (N_DEV - 1) + k - 1], recv_sem=recv_sems.at[a * (N_DEV - 1) + k - 1],
                    device_id=peer, device_id_type=MESH).wait_recv()
        for cp in sends:
            cp.wait_send()
        for cp in local:
            cp.wait()

    return pl.pallas_call(
        body, name=name, out_shape=out_shape,
        in_specs=[ANY] * n, out_specs=[ANY] * n,
        scratch_shapes=[pltpu.SemaphoreType.DMA((n * (N_DEV - 1),)),
                        pltpu.SemaphoreType.DMA((n * (N_DEV - 1),)),
                        pltpu.SemaphoreType.DMA((n,))],
    )(*arrs)


GATHER_SEMS = 7


def _gather_copies(ins, outs, send_sems, recv_sems, local_sems, x, y, c):
    n = len(ins)
    per = GATHER_SEMS
    sib = (x, y, 1 - c)
    xn, yn, dg = (1 - x, y), (x, 1 - y), (1 - x, 1 - y)
    fx, fy = x + (1 - c) * (1 - 2 * x), y + c * (1 - 2 * y)
    tx, ty = x + c * (1 - 2 * x), y + (1 - c) * (1 - 2 * y)

    def slot(a, px, py, pc):
        return outs[a].at[4 * px + 2 * py + pc]

    def copy(a, k, block, to, src=None):
        return pltpu.make_async_remote_copy(
            src_ref=slot(a, *block) if src is None else src, dst_ref=slot(a, *block),
            send_sem=send_sems.at[a * per + k], recv_sem=recv_sems.at[a * per + k],
            device_id=to, device_id_type=MESH)

    local = [pltpu.make_async_copy(ins[a], slot(a, x, y, c), local_sems.at[a]) for a in range(n)]
    for cp in local:
        cp.start()
    started = []
    for a in range(n):
        started += [copy(a, 1, (x, y, c), (*xn, c), src=ins[a]), copy(a, 2, (x, y, c), (*yn, c), src=ins[a])]
    for a in range(n):
        started.append(copy(a, 0, (x, y, c), sib, src=ins[a]))
    for cp in started:
        cp.start()
    for a in range(n):
        copy(a, 1, (*xn, c), (x, y, c)).wait_recv()
        copy(a, 2, (*yn, c), (x, y, c)).wait_recv()
        later = [copy(a, 3, (fx, fy, c), (tx, ty, c)), copy(a, 4, (*xn, c), sib), copy(a, 5, (*yn, c), sib)]
        for cp in later:
            cp.start()
        started += later
    for a in range(n):
        copy(a, 3, (*dg, c), (x, y, c)).wait_recv()
        cp = copy(a, 6, (*dg, c), sib)
        cp.start()
        started.append(cp)
    for a in range(n):
        copy(a, 0, sib, (x, y, c)).wait_recv()
        for k, chip in ((4, xn), (5, yn), (6, dg)):
            copy(a, k, (*chip, 1 - c), (x, y, c)).wait_recv()
    for cp in started:
        cp.wait_send()
    for cp in local:
        cp.wait()


def _gather2(name, arrs):
    n = len(arrs)
    per = GATHER_SEMS
    out_shape = [jax.ShapeDtypeStruct((N_DEV,) + a.shape, a.dtype) for a in arrs]

    def body(*refs):
        ins, outs = refs[:n], refs[n:2 * n]
        send_sems, recv_sems, local_sems = refs[2 * n:]
        x, y, c = lax.axis_index("x"), lax.axis_index("y"), lax.axis_index("c")
        _gather_copies(ins, outs, send_sems, recv_sems, local_sems, x, y, c)

    return pl.pallas_call(
        body, name=name, out_shape=out_shape,
        in_specs=[ANY] * n, out_specs=[ANY] * n,
        scratch_shapes=[pltpu.SemaphoreType.DMA((n * per,)), pltpu.SemaphoreType.DMA((n * per,)),
                        pltpu.SemaphoreType.DMA((n,))],
    )(*arrs)


def _seq_gather2(name, collective_id, arrs):
    n = len(arrs)
    per = GATHER_SEMS

    def body(*refs):
        ins, outs = refs[:n], refs[n:2 * n]
        send_sems, recv_sems, local_sems = refs[2 * n:]
        x, y, c = lax.axis_index("x"), lax.axis_index("y"), lax.axis_index("c")
        barrier = pltpu.get_barrier_semaphore()
        for peer in [(x, y, 1 - c), (1 - x, y, c), (x, 1 - y, c)]:
            pl.semaphore_signal(barrier, inc=1, device_id=peer, device_id_type=MESH)
        pl.semaphore_wait(barrier, 3)
        _gather_copies(ins, outs, send_sems, recv_sems, local_sems, x, y, c)

    return pl.kernel(
        body, out_type=[jax.ShapeDtypeStruct((N_DEV,) + a.shape, a.dtype) for a in arrs],
        mesh=plsc.ScalarSubcoreMesh(axis_name="seq", num_cores=1),
        scratch_types=[pltpu.SemaphoreType.DMA((n * per,)), pltpu.SemaphoreType.DMA((n * per,)),
                       pltpu.SemaphoreType.DMA((n,))],
        compiler_params=pltpu.CompilerParams(collective_id=collective_id), name=name,
    )(*arrs)


def _seq_chip_exchange(name, collective_id, arrs):
    n = len(arrs)

    def body(*refs):
        ins, outs = refs[:n], refs[n:2 * n]
        send_sems, recv_sems = refs[2 * n:]
        x, y, c = lax.axis_index("x"), lax.axis_index("y"), lax.axis_index("c")

        def peer(k):
            return (1 - x if (k >> 1) & 1 else x), (1 - y if k & 1 else y)

        barrier = pltpu.get_barrier_semaphore()
        for k in (1, 2, 3):
            pl.semaphore_signal(barrier, inc=1, device_id=(*peer(k), c), device_id_type=MESH)
        pl.semaphore_wait(barrier, 3)

        def copy(a, k):
            px, py = peer(k)
            return pltpu.make_async_remote_copy(
                src_ref=ins[a].at[2 * px + py], dst_ref=outs[a].at[k - 1],
                send_sem=send_sems.at[a * 3 + k - 1], recv_sem=recv_sems.at[a * 3 + k - 1],
                device_id=(px, py, c), device_id_type=MESH)

        cps = [copy(a, k) for a in range(n) for k in (1, 2, 3)]
        for cp in cps:
            cp.start()
        for cp in cps:
            cp.wait_recv()
        for cp in cps:
            cp.wait_send()

    return pl.kernel(
        body, out_type=[jax.ShapeDtypeStruct((3,) + a.shape[1:], a.dtype) for a in arrs],
        mesh=plsc.ScalarSubcoreMesh(axis_name="seq", num_cores=1),
        scratch_types=[pltpu.SemaphoreType.DMA((n * 3,)), pltpu.SemaphoreType.DMA((n * 3,))],
        compiler_params=pltpu.CompilerParams(collective_id=collective_id), name=name,
    )(*arrs)


def _seq_pair_swap(name, collective_id, arrs):
    n = len(arrs)

    def body(*refs):
        ins, outs = refs[:n], refs[n:2 * n]
        send_sems, recv_sems = refs[2 * n:]
        x, y, c = lax.axis_index("x"), lax.axis_index("y"), lax.axis_index("c")
        barrier = pltpu.get_barrier_semaphore()
        pl.semaphore_signal(barrier, inc=1, device_id=(x, y, 1 - c), device_id_type=MESH)
        pl.semaphore_wait(barrier, 1)

        def copy(a, q):
            return pltpu.make_async_remote_copy(
                src_ref=ins[a].at[q, 1 - c], dst_ref=outs[a].at[q],
                send_sem=send_sems.at[a * 4 + q], recv_sem=recv_sems.at[a * 4 + q],
                device_id=(x, y, 1 - c), device_id_type=MESH)

        cps = [copy(a, q) for a in range(n) for q in range(4)]
        for cp in cps:
            cp.start()
        for cp in cps:
            cp.wait_recv()
        for cp in cps:
            cp.wait_send()

    return pl.kernel(
        body, out_type=[jax.ShapeDtypeStruct((4,) + a.shape[2:], a.dtype) for a in arrs],
        mesh=plsc.ScalarSubcoreMesh(axis_name="seq", num_cores=1),
        scratch_types=[pltpu.SemaphoreType.DMA((n * 4,)), pltpu.SemaphoreType.DMA((n * 4,))],
        compiler_params=pltpu.CompilerParams(collective_id=collective_id), name=name,
    )(*arrs)


def _call(body, name, grid, in_specs, out_specs, out_shape, args, scratch=(), vmem=VMEM_LIMIT):
    return pl.pallas_call(
        body, name=name, grid=grid, in_specs=in_specs, out_specs=out_specs, out_shape=out_shape,
        scratch_shapes=list(scratch),
        compiler_params=pltpu.CompilerParams(dimension_semantics=("arbitrary",) * len(grid), vmem_limit_bytes=vmem),
    )(*args)


def _ada_fwd(c_all, w_ada_sh, b_ada_sh):
    nb, d = c_all.shape
    ncol = w_ada_sh.shape[1]

    def body(c_ref, w_ref, b_ref, mod_ref, cact_ref):
        cc = c_ref[...]
        ca = cc * jax.nn.sigmoid(cc)
        cact_ref[...] = ca
        mod_ref[...] = _dot(ca.astype(BF16), w_ref[...].astype(BF16), NN) + b_ref[...]

    return pl.pallas_call(
        body, name="ada_fwd",
        out_shape=[jax.ShapeDtypeStruct((nb, ncol), F32), jax.ShapeDtypeStruct((nb, d), F32)],
        compiler_params=_params(),
    )(c_all, w_ada_sh, b_ada_sh)


def _rms(xv):
    rstd = lax.rsqrt(jnp.mean(xv * xv, axis=-1, keepdims=True) + EPS)
    return xv * rstd, rstd


def _rms_bwd(dxhat, xhat, rstd):
    return rstd * (dxhat - xhat * jnp.mean(dxhat * xhat, axis=-1, keepdims=True))


def _colsum(v):
    return jnp.sum(v, axis=0, keepdims=True)


def _expm1(v, ev):
    series = v * (1.0 + v * (0.5 + v * (1.0 / 6.0 + v * (1.0 / 24.0 + v * (1.0 / 120.0)))))
    return jnp.where(jnp.abs(v) < 0.2, series, ev - 1.0)


def _softplus(v):
    return jnp.maximum(v, 0.0) + jnp.log1p(jnp.exp(-jnp.abs(v)))


def _gelu(v):
    t = jnp.tanh(v * (GELU_K0 + (GELU_K0 * GELU_K1) * (v * v)))
    return 0.5 * v * (1.0 + t), t


def _dgelu(v, t):
    return 0.5 * ((1.0 + t) + (v * (1.0 - t * t)) * (GELU_K0 + (3.0 * GELU_K0 * GELU_K1) * (v * v)))


def _scan_tile(a, b, x0, st, k0, reverse):
    t = a.shape[0]
    off = SUBLANES
    stage_a, stage_b = st.at[k0], st.at[k0 + 1]
    halo = slice(off + t, off + t + SUBLANES) if reverse else slice(0, SUBLANES)
    stage_a[halo, :] = jnp.ones((SUBLANES, a.shape[1]), F32)
    stage_b[halo, :] = jnp.zeros((SUBLANES, a.shape[1]), F32)
    s = 1
    while s < min(t, SUBLANES):
        stage_a[off:off + t, :] = a
        stage_b[off:off + t, :] = b
        at = off + s if reverse else off - s
        b = a * stage_b[at:at + t, :] + b
        a = a * stage_a[at:at + t, :]
        s *= 2
    while s < t:
        if reverse:
            b = jnp.concatenate([a[:t - s] * b[s:] + b[:t - s], b[t - s:]], axis=0)
            a = jnp.concatenate([a[:t - s] * a[s:], a[t - s:]], axis=0)
        else:
            b = jnp.concatenate([b[:s], a[s:] * b[:t - s] + b[s:]], axis=0)
            a = jnp.concatenate([a[:s], a[s:] * a[:t - s]], axis=0)
        s *= 2
    x = b + a * x0
    return x, (x[0:SUBLANES, :] if reverse else x[t - SUBLANES:t, :])


def _lru_gates(u, wa, wx, ba, bx, sp):
    ub = u.astype(BF16)
    r = jax.nn.sigmoid(_dot(ub, wa, NN) + ba)
    i = jax.nn.sigmoid(_dot(ub, wx, NN) + bx)
    log_a = (-RG_C * r) * sp
    a = jnp.exp(log_a)
    mult = jnp.sqrt(-_expm1(log_a, a) * (a + 1.0))
    return ub, r, i, a, mult


def _staged_shifts(stage, v, prev8, next8, downs, ups):
    t = v.shape[0]
    if prev8 is not None:
        stage[0:SUBLANES, :] = prev8
    stage[SUBLANES:SUBLANES + t, :] = v
    if next8 is not None:
        stage[SUBLANES + t:2 * SUBLANES + t, :] = next8
    return ([stage[SUBLANES - k:SUBLANES - k + t, :] for k in downs],
            [stage[SUBLANES + k:SUBLANES + k + t, :] for k in ups])


def _conv3(p, pp, w_ref, lo, stage):
    (p1, p2), _ = _staged_shifts(stage, p, pp, None, (1, 2), ())
    q = (w_ref[0:1, lo:lo + LANES] * p2 + w_ref[1:2, lo:lo + LANES] * p1) + w_ref[2:3, lo:lo + LANES] * p
    return q, p1, p2


def _conv4(xv, xp, w_ref, b_ref, lo, stage):
    (x1, x2, x3), _ = _staged_shifts(stage, xv, xp, None, (1, 2, 3), ())
    u = (((w_ref[0:1, lo:lo + LANES] * x3 + w_ref[1:2, lo:lo + LANES] * x2) + w_ref[2:3, lo:lo + LANES] * x1)
         + w_ref[3:4, lo:lo + LANES] * xv) + b_ref[:, lo:lo + LANES]
    return u, x1, x2, x3


def _mix_in_mixer_fwd(x2d, mod6, g_mix, w_in_t, conv_sc, conv_lru, conv_b, wa_bd, wx_bd, ba, bx, lam, width, tm):
    s, d = x2d.shape
    din = w_in_t.shape[0]
    nt = s // tm
    sub = min(MIX_ROWS, tm)
    nblk = width // LANES

    def body(x_ref, mod_ref, g_ref, w_ref, wsc_ref, wlru_ref, blru_ref, wa_ref, wx_ref, ba_ref, bx_ref, lam_ref,
             hn_ref, proj_ref, ymix_ref, h_ref, buf_ref, halo_ref, hc_ref, stage_ref):
        i = pl.program_id(0)

        @pl.when(i == 0)
        def _():
            buf_ref[1] = jnp.zeros((tm, din), F32)
            halo_ref[...] = jnp.zeros_like(halo_ref)

        @pl.when(i <= 1)
        def _():
            hc_ref[...] = jnp.zeros_like(hc_ref)

        def step(dst, src):
            xhat, _ = _rms(x_ref[...])
            hn = ((xhat * g_ref[...]) * (1.0 + mod_ref[1:2, :]) + mod_ref[0:1, :]).astype(BF16)
            hn_ref[...] = hn
            n_mix = (tm // sub) * nblk
            n_chunk = din // width

            def project(k):
                res = _dot(hn_ref[...], w_ref[k * width:(k + 1) * width, :], NT)
                proj_ref[:, k * width:(k + 1) * width] = res
                dst[:, k * width:(k + 1) * width] = res

            done = 0
            for half in range(tm // sub):
                r0 = half * sub
                rows = slice(r0, r0 + sub)
                for j in range(nblk):
                    lo = j * LANES
                    while done < n_chunk and done * n_mix <= (half * nblk + j) * n_chunk:
                        project(done)
                        done += 1

                    def col(p):
                        return src[rows, p * width + lo:p * width + lo + LANES]

                    def prev(p):
                        c0 = p * width + lo
                        if half == 0:
                            return halo_ref[:, c0:c0 + LANES]
                        return src[r0 - SUBLANES:r0, c0:c0 + LANES]

                    pp = col(1) * col(2)
                    q, _, _ = _conv3(pp, prev(1) * prev(2), wsc_ref, lo, stage_ref.at[0])
                    ymix_ref[rows, lo:lo + LANES] = (col(0) * q).astype(BF16)

                    u, _, _, _ = _conv4(col(4), prev(4), wlru_ref, blru_ref, lo, stage_ref.at[1])
                    sp = _softplus(-lam_ref[:, lo:lo + LANES])
                    _, r, ig, a, mult = _lru_gates(u, wa_ref[j], wx_ref[j], ba_ref[:, lo:lo + LANES],
                                                   bx_ref[:, lo:lo + LANES], sp)
                    h, ends = _scan_tile(a, mult * (ig * u), hc_ref[0:1, lo:lo + LANES], stage_ref, 2, False)
                    h_ref[rows, lo:lo + LANES] = h
                    hc_ref[0:1, lo:lo + LANES] = ends[SUBLANES - 1:SUBLANES, :]
                    gel, _ = _gelu(col(3))
                    ymix_ref[rows, width + lo:width + lo + LANES] = (gel * h).astype(BF16)
            while done < n_chunk:
                project(done)
                done += 1
            halo_ref[...] = src[tm - SUBLANES:tm, :]

        @pl.when(i % 2 == 0)
        def _():
            step(buf_ref.at[0], buf_ref.at[1])

        @pl.when(i % 2 == 1)
        def _():
            step(buf_ref.at[1], buf_ref.at[0])

    small = [conv_sc, conv_lru, conv_b, wa_bd, wx_bd, ba, bx, lam]
    cur = lambda i: (jnp.minimum(i, nt - 1), 0)
    last = lambda i: (jnp.maximum(i - 1, 0), 0)
    outs = _call(
        body, "mix_in_mixer_fwd", (nt + 1,),
        [pl.BlockSpec((tm, d), cur), _full(mod6.shape), _full(g_mix.shape), _full(w_in_t.shape)]
        + [_full(a.shape) for a in small],
        [pl.BlockSpec((tm, d), cur), pl.BlockSpec((tm, din), cur),
         pl.BlockSpec((tm, 2 * width), last), pl.BlockSpec((tm, width), last)],
        [jax.ShapeDtypeStruct((s, d), BF16), jax.ShapeDtypeStruct((s, din), F32),
         jax.ShapeDtypeStruct((s, 2 * width), BF16), jax.ShapeDtypeStruct((s, width), F32)],
        [x2d, mod6, g_mix, w_in_t, *small],
        scratch=[pltpu.VMEM((2, tm, din), F32), pltpu.VMEM((SUBLANES, din), F32), pltpu.VMEM((SUBLANES, width), F32),
                 pltpu.VMEM((4, sub + 2 * SUBLANES, LANES), F32)])
    return outs


def _mix_out_fwd(ymix, x2d, w_out, mod6, g_mlp, tm):
    s, d = x2d.shape

    def body(y_ref, x_ref, w_ref, mod_ref, g_ref, mix_ref, x2_ref, hn_ref):
        mix = _dot(y_ref[...], w_ref[...], NN)
        mix_ref[...] = mix.astype(BF16)
        x2 = x_ref[...] + mod_ref[2:3, :] * mix
        x2_ref[...] = x2
        xhat, _ = _rms(x2)
        hn_ref[...] = ((xhat * g_ref[...]) * (1.0 + mod_ref[4:5, :]) + mod_ref[3:4, :]).astype(BF16)

    tile = pl.BlockSpec((tm, d), lambda i: (i, 0))
    return _call(
        body, "mix_out_fwd", (s // tm,),
        [tile, tile, _full(w_out.shape), _full(mod6.shape), _full(g_mlp.shape)],
        [tile, tile, tile],
        [jax.ShapeDtypeStruct((s, d), BF16), jax.ShapeDtypeStruct((s, d), F32), jax.ShapeDtypeStruct((s, d), BF16)],
        [ymix, x2d, w_out, mod6, g_mlp])


def _mlp_fwd_loss(hn2, w_up_t, w_down, x2, target, mod6, g_final, tm, tk):
    s, d = hn2.shape
    f = w_up_t.shape[0]
    nk = f // tk

    def body(hn_ref, wu_ref, wd_ref, x2_hbm, t_hbm, mod_ref, g_ref, z_ref, dx3_ref, dyb_ref, st_ref,
             y_ref, x2_ref, t_ref, sems):
        i, k = pl.program_id(0), pl.program_id(1)

        def fetch():
            rows = pl.ds(pl.multiple_of(i * tm, tm), tm)
            return (pltpu.make_async_copy(x2_hbm.at[rows, :], x2_ref, sems.at[0]),
                    pltpu.make_async_copy(t_hbm.at[rows, :], t_ref, sems.at[1]))

        @pl.when(jnp.logical_and(i == 0, k == 0))
        def _():
            st_ref[...] = jnp.zeros_like(st_ref)

        @pl.when(k == 0)
        def _():
            for cp in fetch():
                cp.start()
            y_ref[...] = jnp.zeros_like(y_ref)

        z = jnp.maximum(_dot(hn_ref[...], wu_ref[...], NT), 0.0)
        z_ref[...] = z.astype(BF16)
        y_ref[...] += _dot((z * z).astype(BF16), wd_ref[...], NN)

        @pl.when(k == nk - 1)
        def _():
            for cp in fetch():
                cp.wait()
            gate = mod_ref[5:6, :]
            yv = y_ref[...]
            xhat, rstd = _rms(x2_ref[...] + gate * yv)
            diff = xhat * g_ref[...] - t_ref[...]
            dyo = diff * (1.0 / d)
            dx3 = _rms_bwd(dyo * g_ref[...], xhat, rstd)
            dx3_ref[...] = dx3.astype(BF16)
            dyb_ref[...] = (gate * dx3).astype(BF16)
            st_ref[0:1, :] += _colsum(dyo * xhat)
            st_ref[1:2, :] += _colsum(dx3 * yv)
            st_ref[2:3, :] += _colsum(diff * diff)

    tile = pl.BlockSpec((tm, d), lambda i, k: (i, 0))
    wblk = pl.BlockSpec((tk, d), lambda i, k: (k, 0))
    return pl.pallas_call(
        body, name="mlp_fwd_loss", grid=(s // tm, nk),
        in_specs=[tile, wblk, wblk, ANY, ANY, _full(mod6.shape), _full(g_final.shape)],
        out_specs=[pl.BlockSpec((tm, tk), lambda i, k: (i, k)), tile, tile, _full((SUBLANES, d))],
        out_shape=[jax.ShapeDtypeStruct((s, f), BF16), jax.ShapeDtypeStruct((s, d), BF16),
                   jax.ShapeDtypeStruct((s, d), BF16), jax.ShapeDtypeStruct((SUBLANES, d), F32)],
        scratch_shapes=[pltpu.VMEM((tm, d), F32), pltpu.VMEM((tm, d), F32), pltpu.VMEM((tm, d), F32),
                        pltpu.SemaphoreType.DMA((2,))],
        compiler_params=pltpu.CompilerParams(dimension_semantics=("arbitrary", "arbitrary"),
                                             vmem_limit_bytes=VMEM_LIMIT_BIG),
    )(hn2, w_up_t, w_down, x2, target, mod6, g_final)


def _mlp_bwd_dx(dyb, z, w_down, w_up_t, tm, tk):
    s, d = dyb.shape
    f = z.shape[1]

    nk = f // tk

    def body(dy_ref, z_ref, wd_ref, wu_ref, dz_ref, dh_ref, acc_ref):
        k = pl.program_id(1)

        @pl.when(k == 0)
        def _():
            acc_ref[...] = jnp.zeros_like(acc_ref)

        dz = ((2.0 * z_ref[...].astype(F32)) * _dot(dy_ref[...], wd_ref[...], NT)).astype(BF16)
        dz_ref[...] = dz
        acc_ref[...] += _dot(dz, wu_ref[...], NN)

        @pl.when(k == nk - 1)
        def _():
            dh_ref[...] = acc_ref[...].astype(BF16)

    return pl.pallas_call(
        body, name="mlp_bwd_dx", grid=(s // tm, nk),
        in_specs=[pl.BlockSpec((tm, d), lambda i, k: (i, 0)), pl.BlockSpec((tm, tk), lambda i, k: (i, k)),
                  pl.BlockSpec((tk, d), lambda i, k: (k, 0)), pl.BlockSpec((tk, d), lambda i, k: (k, 0))],
        out_specs=[pl.BlockSpec((tm, tk), lambda i, k: (i, k)), pl.BlockSpec((tm, d), lambda i, k: (i, 0))],
        out_shape=[jax.ShapeDtypeStruct((s, f), BF16), jax.ShapeDtypeStruct((s, d), BF16)],
        scratch_shapes=[pltpu.VMEM((tm, d), F32)],
        compiler_params=_params(("parallel", "arbitrary")),
    )(dyb, z, w_down, w_up_t)


def _mlp_bwd_dw(z, dz, dyb, hn2, tm, tk):
    s, d = dyb.shape
    f = z.shape[1]

    def body(z_ref, dz_ref, dy_ref, hn_ref, gd_ref, gu_ref):
        i = pl.program_id(1)

        @pl.when(i == 0)
        def _():
            gd_ref[...] = jnp.zeros_like(gd_ref)
            gu_ref[...] = jnp.zeros_like(gu_ref)

        zf = z_ref[...].astype(F32)
        gd_ref[...] += _dot((zf * zf).astype(BF16), dy_ref[...], TN)
        gu_ref[...] += _dot(dz_ref[...], hn_ref[...], TN)

    return pl.pallas_call(
        body, name="mlp_bwd_dw", grid=(f // tk, s // tm),
        in_specs=[pl.BlockSpec((tm, tk), lambda k, i: (i, k)), pl.BlockSpec((tm, tk), lambda k, i: (i, k)),
                  pl.BlockSpec((tm, d), lambda k, i: (i, 0)), pl.BlockSpec((tm, d), lambda k, i: (i, 0))],
        out_specs=[pl.BlockSpec((tk, d), lambda k, i: (k, 0)), pl.BlockSpec((tk, d), lambda k, i: (k, 0))],
        out_shape=[jax.ShapeDtypeStruct((f, d), F32), jax.ShapeDtypeStruct((f, d), F32)],
        compiler_params=_params(("parallel", "arbitrary")),
    )(z, dz, dyb, hn2)


def _mix_out_bwd(dhn2, x2, dx3, mix, ymix, w_out, mod6, g_mlp, tm):
    s, d = x2.shape

    def body(dh_ref, x2_ref, dx3_ref, mix_ref, y_ref, w_ref, mod_ref, g_ref, dx2_ref, dym_ref, gw_ref, st_ref):
        i = pl.program_id(0)

        @pl.when(i == 0)
        def _():
            st_ref[...] = jnp.zeros_like(st_ref)
            gw_ref[...] = jnp.zeros_like(gw_ref)

        dh = dh_ref[...].astype(F32)
        xhat, rstd = _rms(x2_ref[...])
        dn = dh * (1.0 + mod_ref[4:5, :])
        dx2 = dx3_ref[...].astype(F32) + _rms_bwd(dn * g_ref[...], xhat, rstd)
        dx2_ref[...] = dx2.astype(BF16)
        st_ref[0:1, :] += _colsum(dh)
        st_ref[1:2, :] += _colsum(dh * (xhat * g_ref[...]))
        st_ref[2:3, :] += _colsum(dn * xhat)
        st_ref[3:4, :] += _colsum(dx2 * mix_ref[...].astype(F32))
        dmix = (mod_ref[2:3, :] * dx2).astype(BF16)
        dym_ref[...] = _dot(dmix, w_ref[...], NT).astype(BF16)
        gw_ref[...] += _dot(y_ref[...], dmix, TN)

    tile = pl.BlockSpec((tm, d), lambda i: (i, 0))
    return _call(
        body, "mix_out_bwd", (s // tm,),
        [tile, tile, tile, tile, tile, _full(w_out.shape), _full(mod6.shape), _full(g_mlp.shape)],
        [tile, tile, _full((d, d)), _full((SUBLANES, d))],
        [jax.ShapeDtypeStruct((s, d), BF16), jax.ShapeDtypeStruct((s, d), BF16),
         jax.ShapeDtypeStruct((d, d), F32), jax.ShapeDtypeStruct((SUBLANES, d), F32)],
        [dhn2, x2, dx3, mix, ymix, w_out, mod6, g_mlp])


def _mixer_bwd(proj, dymix, h_all, conv_sc, conv_lru, conv_b, wa_bd, wx_bd, ba, bx, lam, width):
    s, din = proj.shape
    t = min(MIX_ROWS, s)
    nt = s // t
    nblk = width // LANES
    hb = t // SUBLANES
    last8 = s // SUBLANES - 1

    def body(proj_ref, projp_ref, projn_ref, dy_ref, dyn_ref, h_ref, hp_ref,
             wsc_ref, wlru_ref, blru_ref, wa_ref, wx_ref, ba_ref, bx_ref, lam_ref,
             dproj_ref, small_ref, gwa_ref, gwx_ref, an_ref, gn_ref, dun_ref, stage_ref):
        i = pl.program_id(0)

        @pl.when(i == 0)
        def _():
            small_ref[...] = jnp.zeros_like(small_ref)
            gwa_ref[...] = jnp.zeros_like(gwa_ref)
            gwx_ref[...] = jnp.zeros_like(gwx_ref)
            an_ref[...] = jnp.zeros_like(an_ref)
            gn_ref[...] = jnp.zeros_like(gn_ref)
            dun_ref[...] = jnp.zeros_like(dun_ref)

        has_prev = i < nt - 1
        has_next = i > 0
        for j in range(nblk):
            lo = j * LANES
            ls = slice(lo, lo + LANES)

            def col(p, ref=proj_ref):
                return ref[:, p * width + lo:p * width + lo + LANES]

            def prev(p):
                return jnp.where(has_prev, col(p, projp_ref), 0.0)

            def nxt(p):
                return jnp.where(has_next, col(p, projn_ref), 0.0)

            def add_row(r, v):
                small_ref[r:r + 1, ls] += _colsum(v)

            sc_b, sc_c, sc_x = col(0), col(1), col(2)
            p = sc_c * sc_x
            q, p1, p2 = _conv3(p, prev(1) * prev(2), wsc_ref, lo, stage_ref.at[0])
            dys = dy_ref[:, ls].astype(F32)
            dproj_ref[:, ls] = (dys * q).astype(BF16)
            dq = dys * sc_b
            dqn = jnp.where(has_next, dyn_ref[:, ls].astype(F32)[0:SUBLANES], 0.0) * nxt(0)
            _, (dq1, dq2) = _staged_shifts(stage_ref.at[1], dq, None, dqn, (), (1, 2))
            dp = (wsc_ref[2:3, ls] * dq + wsc_ref[1:2, ls] * dq1) + wsc_ref[0:1, ls] * dq2
            dproj_ref[:, width + lo:width + lo + LANES] = (dp * sc_x).astype(BF16)
            dproj_ref[:, 2 * width + lo:2 * width + lo + LANES] = (dp * sc_c).astype(BF16)
            add_row(0, dq * p2)
            add_row(1, dq * p1)
            add_row(2, dq * p)

            xv = col(4)
            u, x1, x2, x3 = _conv4(xv, prev(4), wlru_ref, blru_ref, lo, stage_ref.at[2])
            lam_v = lam_ref[:, ls]
            sp = _softplus(-lam_v)
            wa, wx = wa_ref[j], wx_ref[j]
            ub, r, ig, a, mult = _lru_gates(u, wa, wx, ba_ref[:, ls], bx_ref[:, ls], sp)
            iu = ig * u
            h = h_ref[:, ls]
            (hm1,), _ = _staged_shifts(stage_ref.at[3], h, jnp.where(has_prev, hp_ref[:, ls], 0.0), None, (1,), ())
            lyv = col(3)
            gel, th = _gelu(lyv)
            dyl = dy_ref[:, width + lo:width + lo + LANES].astype(F32)
            dproj_ref[:, 3 * width + lo:3 * width + lo + LANES] = (dyl * h * _dgelu(lyv, th)).astype(BF16)
            a_next = jnp.broadcast_to(an_ref[0:1, ls], (SUBLANES, LANES))
            _, (a_up,) = _staged_shifts(stage_ref.at[4], a, None, a_next, (), (1,))
            g, _ = _scan_tile(a_up, dyl * gel, gn_ref[0:1, ls], stage_ref, 5, True)
            an_ref[0:1, ls] = a[0:1, :]
            gn_ref[0:1, ls] = g[0:1, :]
            da = g * hm1
            dmult = g * iu
            diu = g * mult
            dlog_a = da * a - dmult * ((a * a) / mult)
            dpre_a = (dlog_a * (-RG_C * sp)) * (r * (1.0 - r))
            dpre_x = (diu * u) * (ig * (1.0 - ig))
            dab, dxb = dpre_a.astype(BF16), dpre_x.astype(BF16)
            du = diu * ig + _dot(dab, wa, NT) + _dot(dxb, wx, NT)
            gwa_ref[j] += _dot(ub, dab, TN)
            gwx_ref[j] += _dot(ub, dxb, TN)
            dun = dun_ref[:, ls]
            dun_ref[:, ls] = du[0:SUBLANES, :]
            _, (du1, du2, du3) = _staged_shifts(stage_ref.at[7], du, None, dun, (), (1, 2, 3))
            dlx = (((wlru_ref[3:4, ls] * du + wlru_ref[2:3, ls] * du1) + wlru_ref[1:2, ls] * du2)
                   + wlru_ref[0:1, ls] * du3)
            dproj_ref[:, 4 * width + lo:4 * width + lo + LANES] = dlx.astype(BF16)
            add_row(3, du * x3)
            add_row(4, du * x2)
            add_row(5, du * x1)
            add_row(6, du * xv)
            add_row(7, du)
            add_row(8, dpre_a)
            add_row(9, dpre_x)
            add_row(10, (dlog_a * (RG_C * r)) * jax.nn.sigmoid(-lam_v))

    small = [conv_sc, conv_lru, conv_b, wa_bd, wx_bd, ba, bx, lam]
    rev = lambda i: nt - 1 - i
    return _call(
        body, "mixer_bwd", (nt,),
        [pl.BlockSpec((t, din), lambda i: (rev(i), 0)),
         pl.BlockSpec((SUBLANES, din), lambda i: (jnp.maximum(rev(i) * hb - 1, 0), 0)),
         pl.BlockSpec((SUBLANES, din), lambda i: (jnp.minimum((rev(i) + 1) * hb, last8), 0)),
         pl.BlockSpec((t, 2 * width), lambda i: (rev(i), 0)),
         pl.BlockSpec((2 * SUBLANES, 2 * width), lambda i: (jnp.minimum((rev(i) + 1) * (hb // 2), last8 // 2), 0)),
         pl.BlockSpec((t, width), lambda i: (rev(i), 0)),
         pl.BlockSpec((SUBLANES, width), lambda i: (jnp.maximum(rev(i) * hb - 1, 0), 0))]
        + [_full(a.shape) for a in small],
        [pl.BlockSpec((t, din), lambda i: (rev(i), 0)), _full((2 * SUBLANES, width)),
         _full(wa_bd.shape), _full(wx_bd.shape)],
        [jax.ShapeDtypeStruct((s, din), BF16), jax.ShapeDtypeStruct((2 * SUBLANES, width), F32),
         jax.ShapeDtypeStruct(wa_bd.shape, F32), jax.ShapeDtypeStruct(wx_bd.shape, F32)],
        [proj, proj, proj, dymix, dymix, h_all, h_all, *small],
        scratch=[pltpu.VMEM((SUBLANES, width), F32), pltpu.VMEM((SUBLANES, width), F32),
                 pltpu.VMEM((SUBLANES, width), F32), pltpu.VMEM((8, t + 2 * SUBLANES, LANES), F32)])


def _mix_in_bwd_dx(dproj, x2d, dx2, w_in_t, mod6, g_mix, tm):
    s, d = x2d.shape
    din = dproj.shape[1]

    def body(dp_ref, x_ref, dx2_ref, w_ref, mod_ref, g_ref, gx_ref, st_ref):
        i = pl.program_id(0)

        @pl.when(i == 0)
        def _():
            st_ref[...] = jnp.zeros_like(st_ref)

        dh = _dot(dp_ref[...], w_ref[...], NN)
        xhat, rstd = _rms(x_ref[...])
        dn = dh * (1.0 + mod_ref[1:2, :])
        gx_ref[...] = dx2_ref[...].astype(F32) + _rms_bwd(dn * g_ref[...], xhat, rstd)
        st_ref[0:1, :] += _colsum(dh)
        st_ref[1:2, :] += _colsum(dh * (xhat * g_ref[...]))
        st_ref[2:3, :] += _colsum(dn * xhat)

    tile = pl.BlockSpec((tm, d), lambda i: (i, 0))
    return _call(
        body, "mix_in_bwd_dx", (s // tm,),
        [pl.BlockSpec((tm, din), lambda i: (i, 0)), tile, tile, _full(w_in_t.shape), _full(mod6.shape),
         _full(g_mix.shape)],
        [tile, _full((SUBLANES, d))],
        [jax.ShapeDtypeStruct((s, d), F32), jax.ShapeDtypeStruct((SUBLANES, d), F32)],
        [dproj, x2d, dx2, w_in_t, mod6, g_mix], vmem=VMEM_LIMIT_BIG)


def _mix_in_bwd_dw(dproj, hn1, tm, tn):
    s, d = hn1.shape
    din = dproj.shape[1]

    def body(dp_ref, hn_ref, gw_ref):
        i = pl.program_id(1)

        @pl.when(i == 0)
        def _():
            gw_ref[...] = jnp.zeros_like(gw_ref)

        gw_ref[...] += _dot(dp_ref[...], hn_ref[...], TN)

    return _call(
        body, "mix_in_bwd_dw", (din // tn, s // tm),
        [pl.BlockSpec((tm, tn), lambda p, i: (i, p)), pl.BlockSpec((tm, d), lambda p, i: (i, 0))],
        [pl.BlockSpec((tn, d), lambda p, i: (p, 0))],
        [jax.ShapeDtypeStruct((din, d), F32)],
        [dproj, hn1])


def _adamw(w, g, m, v):
    m = ADAM_B1 * m + (1.0 - ADAM_B1) * g
    v = ADAM_B2 * v + (1.0 - ADAM_B2) * (g * g)
    m_hat = m / (1.0 - ADAM_B1 ** ADAM_STEP)
    v_hat = v / (1.0 - ADAM_B2 ** ADAM_STEP)
    delta = -ADAM_LR * (m_hat / (jnp.sqrt(v_hat) + ADAM_EPS) + ADAM_WD * w)
    return delta, m, v


def _pair_sum(g4s, h4s, core_chip, tr, name):
    na = len(g4s)
    _, _, r, n = g4s[0].shape

    def body(sc_ref, *refs):
        q = pl.program_id(1)
        for a in range(na):
            g_ref, h_ref = refs[2 * a], refs[2 * a + 1]
            sb_ref, own_ref = refs[2 * na + 2 * a], refs[2 * na + 2 * a + 1]
            ssum = g_ref[...] + h_ref[...]
            sb_ref[...] = ssum.astype(BF16)

            @pl.when(q == sc_ref[1])
            def _():
                own_ref[...] = ssum

    grid_spec = pltpu.PrefetchScalarGridSpec(
        num_scalar_prefetch=1, grid=(r // tr, 4),
        in_specs=[pl.BlockSpec((None, None, tr, n), lambda i, q, sc: (q, sc[0], i, 0)),
                  pl.BlockSpec((None, tr, n), lambda i, q, sc: (q, i, 0))] * na,
        out_specs=[pl.BlockSpec((None, tr, n), lambda i, q, sc: (q, i, 0)),
                   pl.BlockSpec((tr, n), lambda i, q, sc: (i, 0))] * na)
    outs = pl.pallas_call(
        body, name=name, grid_spec=grid_spec,
        out_shape=[jax.ShapeDtypeStruct((4, r, n), BF16), jax.ShapeDtypeStruct((r, n), F32)] * na,
        compiler_params=_params(("parallel", "arbitrary")),
    )(core_chip, *[x for pair in zip(g4s, h4s) for x in pair])
    return [(outs[2 * a], outs[2 * a + 1]) for a in range(na)]


def _sum4_adam(own, parts, w, m, v, tr, name, transposed):
    r, n = own.shape
    rows, cols = w.shape

    def body(o_ref, p_ref, w_ref, m_ref, v_ref, g_ref, d_ref, nm_ref, nv_ref):
        g = o_ref[...]
        for k in range(3):
            g = g + p_ref[k].astype(F32)
        if transposed:
            g = g.T
        g_ref[...] = g
        d_ref[...], nm_ref[...], nv_ref[...] = _adamw(w_ref[...], g, m_ref[...], v_ref[...])

    if transposed:
        g_specs = [pl.BlockSpec((r, tr), lambda i: (0, i)), pl.BlockSpec((3, r, tr), lambda i: (0, 0, i))]
    else:
        g_specs = [pl.BlockSpec((tr, n), lambda i: (i, 0)), pl.BlockSpec((3, tr, n), lambda i: (0, i, 0))]
    tile = pl.BlockSpec((tr, cols), lambda i: (i, 0))
    return pl.pallas_call(
        body, name=name, grid=(rows // tr,),
        in_specs=g_specs + [tile] * 3, out_specs=[tile] * 4,
        out_shape=[jax.ShapeDtypeStruct((rows, cols), F32)] * 4,
        compiler_params=_params(("parallel",)),
    )(own, parts, w, m, v)


def _sum8(parts, tr, name):
    _, rows, n = parts.shape

    def body(p_ref, o_ref):
        acc = p_ref[0]
        for k in range(1, N_DEV):
            acc = acc + p_ref[k]
        o_ref[...] = acc

    return pl.pallas_call(
        body, name=name, grid=(rows // tr,),
        in_specs=[pl.BlockSpec((N_DEV, tr, n), lambda i: (0, i, 0))],
        out_specs=pl.BlockSpec((tr, n), lambda i: (i, 0)),
        out_shape=jax.ShapeDtypeStruct((rows, n), F32),
        compiler_params=_params(("parallel",)),
    )(parts)


def _ada_bwd_adam(cact_t, dmod_cols, w, m, v, tr):
    rows, n = w.shape

    def body(c_ref, d_ref, w_ref, m_ref, v_ref, g_ref, dl_ref, nm_ref, nv_ref):
        pad = jnp.zeros((N_DEV, tr), F32)
        ca = jnp.concatenate([c_ref[...], pad], axis=0).astype(BF16)
        dm = jnp.concatenate([d_ref[...], jnp.zeros((N_DEV, n), F32)], axis=0).astype(BF16)
        g = _dot(ca, dm, TN)
        g_ref[...] = g
        dl_ref[...], nm_ref[...], nv_ref[...] = _adamw(w_ref[...], g, m_ref[...], v_ref[...])

    tile = pl.BlockSpec((tr, n), lambda i: (i, 0))
    return pl.pallas_call(
        body, name="ada_bwd_adam", grid=(rows // tr,),
        in_specs=[pl.BlockSpec((N_DEV, tr), lambda i: (0, i)), _full(dmod_cols.shape), tile, tile, tile],
        out_specs=[tile] * 4,
        out_shape=[jax.ShapeDtypeStruct((rows, n), F32)] * 4,
        compiler_params=_params(("parallel",)),
    )(cact_t, dmod_cols, w, m, v)


def _adam_small(ws, gs, ms, vs):
    n = len(ws)

    def body(*refs):
        w_r, g_r, m_r, v_r = refs[:n], refs[n:2 * n], refs[2 * n:3 * n], refs[3 * n:4 * n]
        d_r, nm_r, nv_r = refs[4 * n:5 * n], refs[5 * n:6 * n], refs[6 * n:7 * n]
        for k in range(n):
            d_r[k][...], nm_r[k][...], nv_r[k][...] = _adamw(w_r[k][...], g_r[k][...], m_r[k][...], v_r[k][...])

    shapes = [jax.ShapeDtypeStruct(w.shape, F32) for w in ws]
    outs = pl.pallas_call(
        body, name="adam_small", out_shape=shapes * 3, compiler_params=_params(),
    )(*ws, *gs, *ms, *vs)
    return outs[:n], outs[n:2 * n], outs[2 * n:]


def _block_diag(w):
    h, hd, _ = w.shape
    per = LANES // hd
    eye = jnp.eye(per, dtype=w.dtype)
    w5 = w.reshape(h // per, per, hd, 1, hd) * eye[None, :, None, :, None]
    return w5.reshape(h // per, LANES, LANES)


def _block_diag_grad(g, h, hd):
    per = LANES // hd
    g5 = g.reshape(h // per, per, hd, per, hd)
    return jnp.stack([g5[:, a, :, a, :] for a in range(per)], axis=1).reshape(h, hd, hd)


def kernel(x, c, w_ada, b_ada, g_mix, w_in, conv_w_sc, conv_w_lru, conv_b_lru, w_rg_a, b_rg_a, w_rg_x, b_rg_x, lru_lambda, w_out, g_mlp, w_up, w_down, g_final, loss_target, m_w_ada, m_b_ada, m_g_mix, m_w_in, m_conv_w_sc, m_conv_w_lru, m_conv_b_lru, m_w_rg_a, m_b_rg_a, m_w_rg_x, m_b_rg_x, m_lru_lambda, m_w_out, m_g_mlp, m_w_up, m_w_down, m_g_final, v_w_ada, v_b_ada, v_g_mix, v_w_in, v_conv_w_sc, v_conv_w_lru, v_conv_b_lru, v_w_rg_a, v_b_rg_a, v_w_rg_x, v_b_rg_x, v_lru_lambda, v_w_out, v_g_mlp, v_w_up, v_w_down, v_g_final):
    s, d = x.shape[1], x.shape[2]
    width = conv_b_lru.shape[1]
    heads, hd = w_rg_a.shape[1], w_rg_a.shape[2]
    n_ada = w_ada.shape[2]
    csh = conv_w_sc.shape[2]
    me = 4 * lax.axis_index("x") + 2 * lax.axis_index("y") + lax.axis_index("c")
    tm = min(512, s)
    tm_mlp = min(1024, s)
    tk = 512

    x2d = x[0]
    tgt = loss_target[0]

    pay = jnp.zeros((SUBLANES, d), F32)
    pay = pay.at[0:1, :].set(c)
    pay = pay.at[1:4, 0:csh].set(conv_w_sc[0])
    pay = pay.at[4:8, 0:csh].set(conv_w_lru[0])
    w_in_t_sh = w_in[0].T.astype(BF16)
    w_up_t_sh = w_up[0].T.astype(BF16)
    w_out_sh = w_out[0].astype(BF16)
    w_down_sh = w_down[0].astype(BF16)
    (w_in_t,) = _seq_gather2("gather_w_in", 10, [w_in_t_sh])
    (pay_all,) = _gather2("gather_in", [pay])
    w_in_t = w_in_t.reshape(-1, d)
    c_all = pay_all[:, 0, :]
    conv_sc = pay_all[:, 1:4, 0:csh].transpose(1, 0, 2).reshape(3, width)
    conv_lru = pay_all[:, 4:8, 0:csh].transpose(1, 0, 2).reshape(4, width)

    b_ada_sh = lax.dynamic_slice(b_ada, (0, me * n_ada), (1, n_ada))
    mod_cols, c_act = _ada_fwd(c_all, w_ada[0], b_ada_sh)
    (mod_rows,) = _exchange("scatter_mod", [], [mod_cols.reshape(N_DEV, 1, n_ada)])
    mod_rows, w_out_sh, w_up_t_sh, w_down_sh = lax.optimization_barrier((mod_rows, w_out_sh, w_up_t_sh, w_down_sh))
    (w_out_g,) = _seq_gather2("gather_w_out", 1, [w_out_sh])
    w_up_g, w_down_g = _seq_gather2("gather_mlp_weights", 2, [w_up_t_sh, w_down_sh])
    mod6 = jnp.zeros((SUBLANES, d), F32).at[0:6, :].set(mod_rows.reshape(6, d))

    wa_bd = _block_diag(w_rg_a[0]).astype(BF16)
    wx_bd = _block_diag(w_rg_x[0]).astype(BF16)
    ba = b_rg_a.reshape(1, width)
    bx = b_rg_x.reshape(1, width)
    g_fin = g_final.reshape(1, d)

    hn1, proj, ymix, h_all = _mix_in_mixer_fwd(x2d, mod6, g_mix, w_in_t, conv_sc, conv_lru, conv_b_lru,
                                               wa_bd, wx_bd, ba, bx, lru_lambda, width, tm)
    w_out_b = w_out_g.reshape(-1, d)
    mix, x2, hn2 = _mix_out_fwd(ymix, x2d, w_out_b, mod6, g_mlp, tm_mlp)
    w_up_t = w_up_g.reshape(-1, d)
    w_down_b = w_down_g.reshape(-1, d)
    z, dx3, dyb, st_fin = _mlp_fwd_loss(hn2, w_up_t, w_down_b, x2, tgt, mod6, g_fin, tm_mlp, 2 * tk)

    core_chip = jnp.stack([lax.axis_index("c"), 2 * lax.axis_index("x") + lax.axis_index("y")]).astype(jnp.int32)
    dz, dhn2 = _mlp_bwd_dx(dyb, z, w_down_b, w_up_t, tm_mlp, 2 * tk)
    g_down, g_up_t = _mlp_bwd_dw(z, dz, dyb, hn2, tm_mlp, 2 * tk)
    g_up4, g_down4 = g_up_t.reshape(4, 2, -1, d), g_down.reshape(4, 2, -1, d)
    h_up, h_down = _seq_pair_swap("swap_mlp_grads", 7, [g_up4, g_down4])
    dx2, dymix, g_out, st_out = _mix_out_bwd(dhn2, x2, dx3, mix, ymix, w_out_b, mod6, g_mlp, tm)
    h_up, h_down, g_out = lax.optimization_barrier((h_up, h_down, g_out))
    (sb_up, own_up), (sb_down, own_down) = _pair_sum([g_up4, g_down4], [h_up, h_down], core_chip, g_up4.shape[2], "pair_sum_mlp")
    g_out4 = g_out.reshape(4, 2, -1, d)
    (h_out,) = _seq_pair_swap("swap_w_out_grad", 8, [g_out4])
    p_up, p_down = _seq_chip_exchange("exchange_mlp_grads", 3, [sb_up, sb_down])
    dproj, g_small, g_wa, g_wx = _mixer_bwd(
        proj, dymix, h_all, conv_sc, conv_lru, conv_b_lru, wa_bd, wx_bd, ba, bx, lru_lambda, width)
    h_out, dproj = lax.optimization_barrier((h_out, dproj))
    ((sb_out, own_out),) = _pair_sum([g_out4], [h_out], core_chip, g_out4.shape[2], "pair_sum_w_out")
    (p_out,) = _seq_chip_exchange("exchange_w_out_grad", 4, [sb_out])
    grad_x, st_in = _mix_in_bwd_dx(dproj, x2d, dx2, w_in_t, mod6, g_mix, tm_mlp)

    small = jnp.concatenate([
        st_in[0:2], st_out[3:4], st_out[0:2], st_fin[1:2],
        st_in[2:3], st_out[2:3], st_fin[0:1],
        jnp.concatenate([g_small[7:8], g_small[10:11]], axis=1),
        jnp.concatenate([g_small[8:9], g_small[9:10]], axis=1),
        jnp.concatenate([jnp.concatenate([g_small[0:3], jnp.zeros((1, width), F32)], axis=0), g_small[3:7]], axis=1),
        st_fin[2:3],
        _block_diag_grad(g_wa, heads, hd).reshape(-1, d),
        _block_diag_grad(g_wx, heads, hd).reshape(-1, d),
    ], axis=0)

    (small_all,) = _seq_gather2("gather_small_grads", 5, [small])
    g_in_t, = _mix_in_bwd_dw(dproj, hn1, min(2048, s), dproj.shape[1] // 2)
    g_in4 = g_in_t.reshape(4, 2, -1, d)
    (h_in,) = _seq_pair_swap("swap_w_in_grad", 9, [g_in4])
    p_up, p_down, p_out, small_all, g_in_t = lax.optimization_barrier((p_up, p_down, p_out, small_all, g_in_t))

    ad_up = _sum4_adam(own_up, p_up, w_up[0], m_w_up[0], v_w_up[0], 256, "adam_w_up", True)
    h_in, ad_up = lax.optimization_barrier((h_in, ad_up))
    ((sb_in, own_in),) = _pair_sum([g_in4], [h_in], core_chip, g_in4.shape[2], "pair_sum_w_in")
    (p_in,) = _seq_chip_exchange("exchange_w_in_grad", 6, [sb_in])
    ad_out = _sum4_adam(own_out, p_out, w_out[0], m_w_out[0], v_w_out[0], w_out.shape[1], "adam_w_out", False)
    ad_down = _sum4_adam(own_down, p_down, w_down[0], m_w_down[0], v_w_down[0], 256, "adam_w_down", False)

    gsum = _sum8(small_all, SMALL_ROWS, "sum_small")
    loss = (0.5 / d) * jnp.sum(gsum[15])
    dmod_cols = lax.dynamic_slice(small_all[:, 0:6, :].reshape(N_DEV, 6 * d), (0, me * n_ada), (N_DEV, n_ada))
    g_ada, d_ada, nm_ada, nv_ada = _ada_bwd_adam(c_act, dmod_cols, w_ada[0], m_w_ada[0], v_w_ada[0], 256)

    g_conv = lax.dynamic_slice(gsum[11:15, 0:width], (0, me * csh), (4, csh))
    g_conv_l = lax.dynamic_slice(gsum[11:15, width:2 * width], (0, me * csh), (4, csh))
    small_g = [
        gsum[0:6].reshape(1, 6 * d),
        gsum[6:7],
        g_conv[0:3].reshape(1, 3, csh),
        g_conv_l.reshape(1, 4, csh),
        gsum[9:10, 0:width],
        gsum[16:48].reshape(1, heads, hd, hd),
        gsum[10:11, 0:width].reshape(1, heads, hd),
        gsum[48:80].reshape(1, heads, hd, hd),
        gsum[10:11, width:].reshape(1, heads, hd),
        gsum[9:10, width:],
        gsum[7:8],
        gsum[8],
    ]
    small_w = [b_ada, g_mix, conv_w_sc, conv_w_lru, conv_b_lru, w_rg_a, b_rg_a, w_rg_x, b_rg_x, lru_lambda, g_mlp, g_final]
    small_m = [m_b_ada, m_g_mix, m_conv_w_sc, m_conv_w_lru, m_conv_b_lru, m_w_rg_a, m_b_rg_a, m_w_rg_x, m_b_rg_x,
               m_lru_lambda, m_g_mlp, m_g_final]
    small_v = [v_b_ada, v_g_mix, v_conv_w_sc, v_conv_w_lru, v_conv_b_lru, v_w_rg_a, v_b_rg_a, v_w_rg_x, v_b_rg_x,
               v_lru_lambda, v_g_mlp, v_g_final]
    sd, snm, snv = _adam_small(small_w, small_g, small_m, small_v)
    p_in, ad_out, ad_down, (g_ada, d_ada, nm_ada, nv_ada), sd = lax.optimization_barrier(
        (p_in, ad_out, ad_down, (g_ada, d_ada, nm_ada, nv_ada), sd))
    ad_in = _sum4_adam(own_in, p_in, w_in[0].T, m_w_in[0].T, v_w_in[0].T, own_in.shape[0], "adam_w_in", False)
    ad_in = [a.T for a in ad_in]

    def order(ada, w_in_, w_out_, w_up_, w_down_, sm):
        return [ada[None], sm[0], sm[1], w_in_[None], sm[2], sm[3], sm[4], sm[5], sm[6], sm[7], sm[8], sm[9],
                w_out_[None], sm[10], w_up_[None], w_down_[None], sm[11]]

    grads = order(g_ada, ad_in[0], ad_out[0], ad_up[0], ad_down[0], small_g)
    deltas = order(d_ada, ad_in[1], ad_out[1], ad_up[1], ad_down[1], sd)
    new_m = order(nm_ada, ad_in[2], ad_out[2], ad_up[2], ad_down[2], snm)
    new_v = order(nv_ada, ad_in[3], ad_out[3], ad_up[3], ad_down[3], snv)
    return (loss, grad_x[None], *grads, *deltas, *new_m, *new_v)
```

```python
import jax
import jax.numpy as jnp
from jax import lax
from jax.experimental import pallas as pl
from jax.experimental.pallas import tpu as pltpu
from jax.experimental.pallas import tpu_sc as plsc

F32 = jnp.float32
BF16 = jnp.bfloat16
N_DEV = 8
EPS = 1e-6
RG_C = 8.0
GELU_K0 = 0.7978845608028654
GELU_K1 = 0.044715
ADAM_LR = 0.001
ADAM_B1 = 0.9
ADAM_B2 = 0.999
ADAM_EPS = 1e-08
ADAM_WD = 0.01
ADAM_STEP = 10
LANES = 128
SUBLANES = 8
VMEM_LIMIT = 52 * 1024 * 1024
VMEM_LIMIT_BIG = 58 * 1024 * 1024
MIX_ROWS = 256
SMALL_ROWS = 80

MESH = pl.DeviceIdType.MESH
ANY = pl.BlockSpec(memory_space=pl.ANY)
NN = ((1,), (0,))
NT = ((1,), (1,))
TN = ((0,), (0,))


def _dot(a, b, dims):
    return lax.dot_general(a, b, (dims, ((), ())), preferred_element_type=F32)


def _params(sem=None):
    return pltpu.CompilerParams(dimension_semantics=sem, vmem_limit_bytes=VMEM_LIMIT)


def _full(shape):
    nd = len(shape)
    return pl.BlockSpec(shape, lambda *_: (0,) * nd)


def _exchange(name, gathers, scatters):
    n_g = len(gathers)
    arrs = list(gathers) + list(scatters)
    n = len(arrs)
    out_shape = [jax.ShapeDtypeStruct((N_DEV,) + a.shape, a.dtype) for a in gathers]
    out_shape += [jax.ShapeDtypeStruct(a.shape, a.dtype) for a in scatters]

    def body(*refs):
        ins, outs = refs[:n], refs[n:2 * n]
        send_sems, recv_sems, local_sems = refs[2 * n:]
        x, y, c = lax.axis_index("x"), lax.axis_index("y"), lax.axis_index("c")
        me = 4 * x + 2 * y + c

        def src(a, dev):
            return ins[a] if a < n_g else ins[a].at[dev]

        def peer_of(k):
            px = 1 - x if (k >> 2) & 1 else x
            py = 1 - y if (k >> 1) & 1 else y
            pc = 1 - c if k & 1 else c
            return (px, py, pc), 4 * px + 2 * py + pc

        local = [pltpu.make_async_copy(src(a, me), outs[a].at[me], local_sems.at[a]) for a in range(n)]
        for cp in local:
            cp.start()
        sends = []
        for k in range(1, N_DEV):
            peer, pidx = peer_of(k)
            for a in range(n):
                cp = pltpu.make_async_remote_copy(
                    src_ref=src(a, pidx), dst_ref=outs[a].at[me],
                    send_sem=send_sems.at[a * (N_DEV - 1) + k - 1], recv_sem=recv_sems.at[a * (N_DEV - 1) + k - 1],
                    device_id=peer, device_id_type=MESH)
                cp.start()
                sends.append(cp)
        for k in range(1, N_DEV):
            peer, pidx = peer_of(k)
            for a in range(n):
                pltpu.make_async_remote_copy(
                    src_ref=src(a, pidx), dst_ref=outs[a].at[pidx],
                    send_sem=send_sems.at[a * (N_DEV - 1) + k - 1], recv_sem=recv_sems.at[a * (N_DEV - 1) + k - 1],
                    device_id=peer, device_id_type=MESH).wait_recv()
        for cp in sends:
            cp.wait_send()
        for cp in local:
            cp.wait()

    return pl.pallas_call(
        body, name=name, out_shape=out_shape,
        in_specs=[ANY] * n, out_specs=[ANY] * n,
        scratch_shapes=[pltpu.SemaphoreType.DMA((n * (N_DEV - 1),)),
                        pltpu.SemaphoreType.DMA((n * (N_DEV - 1),)),
                        pltpu.SemaphoreType.DMA((n,))],
    )(*arrs)


GATHER_SEMS = 7


def _gather_copies(ins, outs, send_sems, recv_sems, local_sems, x, y, c):
    n = len(ins)
    per = GATHER_SEMS
    sib = (x, y, 1 - c)
    xn, yn, dg = (1 - x, y), (x, 1 - y), (1 - x, 1 - y)
    fx, fy = x + (1 - c) * (1 - 2 * x), y + c * (1 - 2 * y)
    tx, ty = x + c * (1 - 2 * x), y + (1 - c) * (1 - 2 * y)

    def slot(a, px, py, pc):
        return outs[a].at[4 * px + 2 * py + pc]

    def copy(a, k, block, to, src=None):
        return pltpu.make_async_remote_copy(
            src_ref=slot(a, *block) if src is None else src, dst_ref=slot(a, *block),
            send_sem=send_sems.at[a * per + k], recv_sem=recv_sems.at[a * per + k],
            device_id=to, device_id_type=MESH)

    local = [pltpu.make_async_copy(ins[a], slot(a, x, y, c), local_sems.at[a]) for a in range(n)]
    for cp in local:
        cp.start()
    started = []
    for a in range(n):
        started += [copy(a, 1, (x, y, c), (*xn, c), src=ins[a]), copy(a, 2, (x, y, c), (*yn, c), src=ins[a])]
    for a in range(n):
        started.append(copy(a, 0, (x, y, c), sib, src=ins[a]))
    for cp in started:
        cp.start()
    for a in range(n):
        copy(a, 1, (*xn, c), (x, y, c)).wait_recv()
        copy(a, 2, (*yn, c), (x, y, c)).wait_recv()
        later = [copy(a, 3, (fx, fy, c), (tx, ty, c)), copy(a, 4, (*xn, c), sib), copy(a, 5, (*yn, c), sib)]
        for cp in later:
            cp.start()
        started += later
    for a in range(n):
        copy(a, 3, (*dg, c), (x, y, c)).wait_recv()
        cp = copy(a, 6, (*dg, c), sib)
        cp.start()
        started.append(cp)
    for a in range(n):
        copy(a, 0, sib, (x, y, c)).wait_recv()
        for k, chip in ((4, xn), (5, yn), (6, dg)):
            copy(a, k, (*chip, 1 - c), (x, y, c)).wait_recv()
    for cp in started:
        cp.wait_send()
    for cp in local:
        cp.wait()


def _gather2(name, arrs):
    n = len(arrs)
    per = GATHER_SEMS
    out_shape = [jax.ShapeDtypeStruct((N_DEV,) + a.shape, a.dtype) for a in arrs]

    def body(*refs):
        ins, outs = refs[:n], refs[n:2 * n]
        send_sems, recv_sems, local_sems = refs[2 * n:]
        x, y, c = lax.axis_index("x"), lax.axis_index("y"), lax.axis_index("c")
        _gather_copies(ins, outs, send_sems, recv_sems, local_sems, x, y, c)

    return pl.pallas_call(
        body, name=name, out_shape=out_shape,
        in_specs=[ANY] * n, out_specs=[ANY] * n,
        scratch_shapes=[pltpu.SemaphoreType.DMA((n * per,)), pltpu.SemaphoreType.DMA((n * per,)),
                        pltpu.SemaphoreType.DMA((n,))],
    )(*arrs)


def _seq_gather2(name, collective_id, arrs):
    n = len(arrs)
    per = GATHER_SEMS

    def body(*refs):
        ins, outs = refs[:n], refs[n:2 * n]
        send_sems, recv_sems, local_sems = refs[2 * n:]
        x, y, c = lax.axis_index("x"), lax.axis_index("y"), lax.axis_index("c")
        barrier = pltpu.get_barrier_semaphore()
        for peer in [(x, y, 1 - c), (1 - x, y, c), (x, 1 - y, c)]:
            pl.semaphore_signal(barrier, inc=1, device_id=peer, device_id_type=MESH)
        pl.semaphore_wait(barrier, 3)
        _gather_copies(ins, outs, send_sems, recv_sems, local_sems, x, y, c)

    return pl.kernel(
        body, out_type=[jax.ShapeDtypeStruct((N_DEV,) + a.shape, a.dtype) for a in arrs],
        mesh=plsc.ScalarSubcoreMesh(axis_name="seq", num_cores=1),
        scratch_types=[pltpu.SemaphoreType.DMA((n * per,)), pltpu.SemaphoreType.DMA((n * per,)),
                       pltpu.SemaphoreType.DMA((n,))],
        compiler_params=pltpu.CompilerParams(collective_id=collective_id), name=name,
    )(*arrs)


def _seq_chip_exchange(name, collective_id, arrs):
    n = len(arrs)

    def body(*refs):
        ins, outs = refs[:n], refs[n:2 * n]
        send_sems, recv_sems = refs[2 * n:]
        x, y, c = lax.axis_index("x"), lax.axis_index("y"), lax.axis_index("c")

        def peer(k):
            return (1 - x if (k >> 1) & 1 else x), (1 - y if k & 1 else y)

        barrier = pltpu.get_barrier_semaphore()
        for k in (1, 2, 3):
            pl.semaphore_signal(barrier, inc=1, device_id=(*peer(k), c), device_id_type=MESH)
        pl.semaphore_wait(barrier, 3)

        def copy(a, k):
            px, py = peer(k)
            return pltpu.make_async_remote_copy(
                src_ref=ins[a].at[2 * px + py], dst_ref=outs[a].at[k - 1],
                send_sem=send_sems.at[a * 3 + k - 1], recv_sem=recv_sems.at[a * 3 + k - 1],
                device_id=(px, py, c), device_id_type=MESH)

        cps = [copy(a, k) for a in range(n) for k in (1, 2, 3)]
        for cp in cps:
            cp.start()
        for cp in cps:
            cp.wait_recv()
        for cp in cps:
            cp.wait_send()

    return pl.kernel(
        body, out_type=[jax.ShapeDtypeStruct((3,) + a.shape[1:], a.dtype) for a in arrs],
        mesh=plsc.ScalarSubcoreMesh(axis_name="seq", num_cores=1),
        scratch_types=[pltpu.SemaphoreType.DMA((n * 3,)), pltpu.SemaphoreType.DMA((n * 3,))],
        compiler_params=pltpu.CompilerParams(collective_id=collective_id), name=name,
    )(*arrs)


def _seq_pair_swap(name, collective_id, arrs):
    n = len(arrs)

    def body(*refs):
        ins, outs = refs[:n], refs[n:2 * n]
        send_sems, recv_sems = refs[2 * n:]
        x, y, c = lax.axis_index("x"), lax.axis_index("y"), lax.axis_index("c")
        barrier = pltpu.get_barrier_semaphore()
        pl.semaphore_signal(barrier, inc=1, device_id=(x, y, 1 - c), device_id_type=MESH)
        pl.semaphore_wait(barrier, 1)

        def copy(a, q):
            return pltpu.make_async_remote_copy(
                src_ref=ins[a].at[q, 1 - c], dst_ref=outs[a].at[q],
                send_sem=send_sems.at[a * 4 + q], recv_sem=recv_sems.at[a * 4 + q],
                device_id=(x, y, 1 - c), device_id_type=MESH)

        cps = [copy(a, q) for a in range(n) for q in range(4)]
        for cp in cps:
            cp.start()
        for cp in cps:
            cp.wait_recv()
        for cp in cps:
            cp.wait_send()

    return pl.kernel(
        body, out_type=[jax.ShapeDtypeStruct((4,) + a.shape[2:], a.dtype) for a in arrs],
        mesh=plsc.ScalarSubcoreMesh(axis_name="seq", num_cores=1),
        scratch_types=[pltpu.SemaphoreType.DMA((n * 4,)), pltpu.SemaphoreType.DMA((n * 4,))],
        compiler_params=pltpu.CompilerParams(collective_id=collective_id), name=name,
    )(*arrs)


def _call(body, name, grid, in_specs, out_specs, out_shape, args, scratch=(), vmem=VMEM_LIMIT):
    return pl.pallas_call(
        body, name=name, grid=grid, in_specs=in_specs, out_specs=out_specs, out_shape=out_shape,
        scratch_shapes=list(scratch),
        compiler_params=pltpu.CompilerParams(dimension_semantics=("arbitrary",) * len(grid), vmem_limit_bytes=vmem),
    )(*args)


def _ada_fwd(c_all, w_ada_sh, b_ada_sh):
    nb, d = c_all.shape
    ncol = w_ada_sh.shape[1]

    def body(c_ref, w_ref, b_ref, mod_ref, cact_ref):
        cc = c_ref[...]
        ca = cc * jax.nn.sigmoid(cc)
        cact_ref[...] = ca
        mod_ref[...] = _dot(ca.astype(BF16), w_ref[...].astype(BF16), NN) + b_ref[...]

    return pl.pallas_call(
        body, name="ada_fwd",
        out_shape=[jax.ShapeDtypeStruct((nb, ncol), F32), jax.ShapeDtypeStruct((nb, d), F32)],
        compiler_params=_params(),
    )(c_all, w_ada_sh, b_ada_sh)


def _rms(xv):
    rstd = lax.rsqrt(jnp.mean(xv * xv, axis=-1, keepdims=True) + EPS)
    return xv * rstd, rstd


def _rms_bwd(dxhat, xhat, rstd):
    return rstd * (dxhat - xhat * jnp.mean(dxhat * xhat, axis=-1, keepdims=True))


def _colsum(v):
    return jnp.sum(v, axis=0, keepdims=True)


def _expm1(v, ev):
    series = v * (1.0 + v * (0.5 + v * (1.0 / 6.0 + v * (1.0 / 24.0 + v * (1.0 / 120.0)))))
    return jnp.where(jnp.abs(v) < 0.2, series, ev - 1.0)


def _softplus(v):
    return jnp.maximum(v, 0.0) + jnp.log1p(jnp.exp(-jnp.abs(v)))


def _gelu(v):
    t = jnp.tanh(v * (GELU_K0 + (GELU_K0 * GELU_K1) * (v * v)))
    return 0.5 * v * (1.0 + t), t


def _dgelu(v, t):
    return 0.5 * ((1.0 + t) + (v * (1.0 - t * t)) * (GELU_K0 + (3.0 * GELU_K0 * GELU_K1) * (v * v)))


def _scan_tile(a, b, x0, st, k0, reverse):
    t = a.shape[0]
    off = SUBLANES
    stage_a, stage_b = st.at[k0], st.at[k0 + 1]
    halo = slice(off + t, off + t + SUBLANES) if reverse else slice(0, SUBLANES)
    stage_a[halo, :] = jnp.ones((SUBLANES, a.shape[1]), F32)
    stage_b[halo, :] = jnp.zeros((SUBLANES, a.shape[1]), F32)
    s = 1
    while s < min(t, SUBLANES):
        stage_a[off:off + t, :] = a
        stage_b[off:off + t, :] = b
        at = off + s if reverse else off - s
        b = a * stage_b[at:at + t, :] + b
        a = a * stage_a[at:at + t, :]
        s *= 2
    while s < t:
        if reverse:
            b = jnp.concatenate([a[:t - s] * b[s:] + b[:t - s], b[t - s:]], axis=0)
            a = jnp.concatenate([a[:t - s] * a[s:], a[t - s:]], axis=0)
        else:
            b = jnp.concatenate([b[:s], a[s:] * b[:t - s] + b[s:]], axis=0)
            a = jnp.concatenate([a[:s], a[s:] * a[:t - s]], axis=0)
        s *= 2
    x = b + a * x0
    return x, (x[0:SUBLANES, :] if reverse else x[t - SUBLANES:t, :])


def _lru_gates(u, wa, wx, ba, bx, sp):
    ub = u.astype(BF16)
    r = jax.nn.sigmoid(_dot(ub, wa, NN) + ba)
    i = jax.nn.sigmoid(_dot(ub, wx, NN) + bx)
    log_a = (-RG_C * r) * sp
    a = jnp.exp(log_a)
    mult = jnp.sqrt(-_expm1(log_a, a) * (a + 1.0))
    return ub, r, i, a, mult


def _staged_shifts(stage, v, prev8, next8, downs, ups):
    t = v.shape[0]
    if prev8 is not None:
        stage[0:SUBLANES, :] = prev8
    stage[SUBLANES:SUBLANES + t, :] = v
    if next8 is not None:
        stage[SUBLANES + t:2 * SUBLANES + t, :] = next8
    return ([stage[SUBLANES - k:SUBLANES - k + t, :] for k in downs],
            [stage[SUBLANES + k:SUBLANES + k + t, :] for k in ups])


def _conv3(p, pp, w_ref, lo, stage):
    (p1, p2), _ = _staged_shifts(stage, p, pp, None, (1, 2), ())
    q = (w_ref[0:1, lo:lo + LANES] * p2 + w_ref[1:2, lo:lo + LANES] * p1) + w_ref[2:3, lo:lo + LANES] * p
    return q, p1, p2


def _conv4(xv, xp, w_ref, b_ref, lo, stage):
    (x1, x2, x3), _ = _staged_shifts(stage, xv, xp, None, (1, 2, 3), ())
    u = (((w_ref[0:1, lo:lo + LANES] * x3 + w_ref[1:2, lo:lo + LANES] * x2) + w_ref[2:3, lo:lo + LANES] * x1)
         + w_ref[3:4, lo:lo + LANES] * xv) + b_ref[:, lo:lo + LANES]
    return u, x1, x2, x3


def _mix_in_mixer_fwd(x2d, mod6, g_mix, w_in_t, conv_sc, conv_lru, conv_b, wa_bd, wx_bd, ba, bx, lam, width, tm):
    s, d = x2d.shape
    din = w_in_t.shape[0]
    nt = s // tm
    sub = min(MIX_ROWS, tm)
    nblk = width // LANES

    def body(x_ref, mod_ref, g_ref, w_ref, wsc_ref, wlru_ref, blru_ref, wa_ref, wx_ref, ba_ref, bx_ref, lam_ref,
             hn_ref, proj_ref, ymix_ref, h_ref, buf_ref, halo_ref, hc_ref, stage_ref):
        i = pl.program_id(0)

        @pl.when(i == 0)
        def _():
            buf_ref[1] = jnp.zeros((tm, din), F32)
            halo_ref[...] = jnp.zeros_like(halo_ref)

        @pl.when(i <= 1)
        def _():
            hc_ref[...] = jnp.zeros_like(hc_ref)

        def step(dst, src):
            xhat, _ = _rms(x_ref[...])
            hn = ((xhat * g_ref[...]) * (1.0 + mod_ref[1:2, :]) + mod_ref[0:1, :]).astype(BF16)
            hn_ref[...] = hn
            n_mix = (tm // sub) * nblk
            n_chunk = din // width

            def project(k):
                res = _dot(hn_ref[...], w_ref[k * width:(k + 1) * width, :], NT)
                proj_ref[:, k * width:(k + 1) * width] = res
                dst[:, k * width:(k + 1) * width] = res

            done = 0
            for half in range(tm // sub):
                r0 = half * sub
                rows = slice(r0, r0 + sub)
                for j in range(nblk):
                    lo = j * LANES
                    while done < n_chunk and done * n_mix <= (half * nblk + j) * n_chunk:
                        project(done)
                        done += 1

                    def col(p):
                        return src[rows, p * width + lo:p * width + lo + LANES]

                    def prev(p):
                        c0 = p * width + lo
                        if half == 0:
                            return halo_ref[:, c0:c0 + LANES]
                        return src[r0 - SUBLANES:r0, c0:c0 + LANES]

                    pp = col(1) * col(2)
                    q, _, _ = _conv3(pp, prev(1) * prev(2), wsc_ref, lo, stage_ref.at[0])
                    ymix_ref[rows, lo:lo + LANES] = (col(0) * q).astype(BF16)

                    u, _, _, _ = _conv4(col(4), prev(4), wlru_ref, blru_ref, lo, stage_ref.at[1])
                    sp = _softplus(-lam_ref[:, lo:lo + LANES])
                    _, r, ig, a, mult = _lru_gates(u, wa_ref[j], wx_ref[j], ba_ref[:, lo:lo + LANES],
                                                   bx_ref[:, lo:lo + LANES], sp)
                    h, ends = _scan_tile(a, mult * (ig * u), hc_ref[0:1, lo:lo + LANES], stage_ref, 2, False)
                    h_ref[rows, lo:lo + LANES] = h
                    hc_ref[0:1, lo:lo + LANES] = ends[SUBLANES - 1:SUBLANES, :]
                    gel, _ = _gelu(col(3))
                    ymix_ref[rows, width + lo:width + lo + LANES] = (gel * h).astype(BF16)
            while done < n_chunk:
                project(done)
                done += 1
            halo_ref[...] = src[tm - SUBLANES:tm, :]

        @pl.when(i % 2 == 0)
        def _():
            step(buf_ref.at[0], buf_ref.at[1])

        @pl.when(i % 2 == 1)
        def _():
            step(buf_ref.at[1], buf_ref.at[0])

    small = [conv_sc, conv_lru, conv_b, wa_bd, wx_bd, ba, bx, lam]
    cur = lambda i: (jnp.minimum(i, nt - 1), 0)
    last = lambda i: (jnp.maximum(i - 1, 0), 0)
    outs = _call(
        body, "mix_in_mixer_fwd", (nt + 1,),
        [pl.BlockSpec((tm, d), cur), _full(mod6.shape), _full(g_mix.shape), _full(w_in_t.shape)]
        + [_full(a.shape) for a in small],
        [pl.BlockSpec((tm, d), cur), pl.BlockSpec((tm, din), cur),
         pl.BlockSpec((tm, 2 * width), last), pl.BlockSpec((tm, width), last)],
        [jax.ShapeDtypeStruct((s, d), BF16), jax.ShapeDtypeStruct((s, din), F32),
         jax.ShapeDtypeStruct((s, 2 * width), BF16), jax.ShapeDtypeStruct((s, width), F32)],
        [x2d, mod6, g_mix, w_in_t, *small],
        scratch=[pltpu.VMEM((2, tm, din), F32), pltpu.VMEM((SUBLANES, din), F32), pltpu.VMEM((SUBLANES, width), F32),
                 pltpu.VMEM((4, sub + 2 * SUBLANES, LANES), F32)])
    return outs


def _mix_out_fwd(ymix, x2d, w_out, mod6, g_mlp, tm):
    s, d = x2d.shape

    def body(y_ref, x_ref, w_ref, mod_ref, g_ref, mix_ref, x2_ref, hn_ref):
        mix = _dot(y_ref[...], w_ref[...], NN)
        mix_ref[...] = mix.astype(BF16)
        x2 = x_ref[...] + mod_ref[2:3, :] * mix
        x2_ref[...] = x2
        xhat, _ = _rms(x2)
        hn_ref[...] = ((xhat * g_ref[...]) * (1.0 + mod_ref[4:5, :]) + mod_ref[3:4, :]).astype(BF16)

    tile = pl.BlockSpec((tm, d), lambda i: (i, 0))
    return _call(
        body, "mix_out_fwd", (s // tm,),
        [tile, tile, _full(w_out.shape), _full(mod6.shape), _full(g_mlp.shape)],
        [tile, tile, tile],
        [jax.ShapeDtypeStruct((s, d), BF16), jax.ShapeDtypeStruct((s, d), F32), jax.ShapeDtypeStruct((s, d), BF16)],
        [ymix, x2d, w_out, mod6, g_mlp])


def _mlp_fwd_loss(hn2, w_up_t, w_down, x2, target, mod6, g_final, tm, tk):
    s, d = hn2.shape
    f = w_up_t.shape[0]
    nk = f // tk

    def body(hn_ref, wu_ref, wd_ref, x2_hbm, t_hbm, mod_ref, g_ref, z_ref, dx3_ref, dyb_ref, st_ref,
             y_ref, x2_ref, t_ref, sems):
        i, k = pl.program_id(0), pl.program_id(1)

        def fetch():
            rows = pl.ds(pl.multiple_of(i * tm, tm), tm)
            return (pltpu.make_async_copy(x2_hbm.at[rows, :], x2_ref, sems.at[0]),
                    pltpu.make_async_copy(t_hbm.at[rows, :], t_ref, sems.at[1]))

        @pl.when(jnp.logical_and(i == 0, k == 0))
        def _():
            st_ref[...] = jnp.zeros_like(st_ref)

        @pl.when(k == 0)
        def _():
            for cp in fetch():
                cp.start()
            y_ref[...] = jnp.zeros_like(y_ref)

        z = jnp.maximum(_dot(hn_ref[...], wu_ref[...], NT), 0.0)
        z_ref[...] = z.astype(BF16)
        y_ref[...] += _dot((z * z).astype(BF16), wd_ref[...], NN)

        @pl.when(k == nk - 1)
        def _():
            for cp in fetch():
                cp.wait()
            gate = mod_ref[5:6, :]
            yv = y_ref[...]
            xhat, rstd = _rms(x2_ref[...] + gate * yv)
            diff = xhat * g_ref[...] - t_ref[...]
            dyo = diff * (1.0 / d)
            dx3 = _rms_bwd(dyo * g_ref[...], xhat, rstd)
            dx3_ref[...] = dx3.astype(BF16)
            dyb_ref[...] = (gate * dx3).astype(BF16)
            st_ref[0:1, :] += _colsum(dyo * xhat)
            st_ref[1:2, :] += _colsum(dx3 * yv)
            st_ref[2:3, :] += _colsum(diff * diff)

    tile = pl.BlockSpec((tm, d), lambda i, k: (i, 0))
    wblk = pl.BlockSpec((tk, d), lambda i, k: (k, 0))
    return pl.pallas_call(
        body, name="mlp_fwd_loss", grid=(s // tm, nk),
        in_specs=[tile, wblk, wblk, ANY, ANY, _full(mod6.shape), _full(g_final.shape)],
        out_specs=[pl.BlockSpec((tm, tk), lambda i, k: (i, k)), tile, tile, _full((SUBLANES, d))],
        out_shape=[jax.ShapeDtypeStruct((s, f), BF16), jax.ShapeDtypeStruct((s, d), BF16),
                   jax.ShapeDtypeStruct((s, d), BF16), jax.ShapeDtypeStruct((SUBLANES, d), F32)],
        scratch_shapes=[pltpu.VMEM((tm, d), F32), pltpu.VMEM((tm, d), F32), pltpu.VMEM((tm, d), F32),
                        pltpu.SemaphoreType.DMA((2,))],
        compiler_params=pltpu.CompilerParams(dimension_semantics=("arbitrary", "arbitrary"),
                                             vmem_limit_bytes=VMEM_LIMIT_BIG),
    )(hn2, w_up_t, w_down, x2, target, mod6, g_final)


def _mlp_bwd_dx(dyb, z, w_down, w_up_t, tm, tk):
    s, d = dyb.shape
    f = z.shape[1]

    nk = f // tk

    steps = (s // tm) * nk
    ring = 3

    def body(dy_ref, z_hbm, wd_ref, wu_ref, dz_ref, dh_ref, acc_ref, zbuf, zsem):
        k = pl.program_id(1)
        t = pl.program_id(0) * nk + k

        def zcopy(tt, slot):
            rows = pl.ds(pl.multiple_of((tt // nk) * tm, tm), tm)
            cols = pl.ds(pl.multiple_of((tt % nk) * tk, tk), tk)
            return pltpu.make_async_copy(z_hbm.at[rows, cols], zbuf.at[slot], zsem.at[slot])

        @pl.when(t == 0)
        def _():
            for j in range(min(ring, steps)):
                zcopy(j, j).start()

        @pl.when(k == 0)
        def _():
            acc_ref[...] = jnp.zeros_like(acc_ref)

        slot = t % ring
        zcopy(t, slot).wait()
        dz = ((2.0 * zbuf[slot].astype(F32)) * _dot(dy_ref[...], wd_ref[...], NT)).astype(BF16)
        dz_ref[...] = dz

        @pl.when(t + ring < steps)
        def _():
            zcopy(t + ring, slot).start()

        acc_ref[...] += _dot(dz, wu_ref[...], NN)

        @pl.when(k == nk - 1)
        def _():
            dh_ref[...] = acc_ref[...].astype(BF16)

    return pl.pallas_call(
        body, name="mlp_bwd_dx", grid=(s // tm, nk),
        in_specs=[pl.BlockSpec((tm, d), lambda i, k: (i, 0)), ANY,
                  pl.BlockSpec((tk, d), lambda i, k: (k, 0)), pl.BlockSpec((tk, d), lambda i, k: (k, 0))],
        out_specs=[pl.BlockSpec((tm, tk), lambda i, k: (i, k)), pl.BlockSpec((tm, d), lambda i, k: (i, 0))],
        out_shape=[jax.ShapeDtypeStruct((s, f), BF16), jax.ShapeDtypeStruct((s, d), BF16)],
        scratch_shapes=[pltpu.VMEM((tm, d), F32), pltpu.VMEM((ring, tm, tk), BF16), pltpu.SemaphoreType.DMA((ring,))],
        compiler_params=pltpu.CompilerParams(dimension_semantics=("arbitrary", "arbitrary"),
                                             vmem_limit_bytes=VMEM_LIMIT_BIG),
    )(dyb, z, w_down, w_up_t)


def _mlp_bwd_dw(z, dz, dyb, hn2, tm, tk):
    s, d = dyb.shape
    f = z.shape[1]

    def body(z_ref, dz_ref, dy_ref, hn_ref, gd_ref, gu_ref):
        i = pl.program_id(1)

        @pl.when(i == 0)
        def _():
            gd_ref[...] = jnp.zeros_like(gd_ref)
            gu_ref[...] = jnp.zeros_like(gu_ref)

        zf = z_ref[...].astype(F32)
        gd_ref[...] += _dot((zf * zf).astype(BF16), dy_ref[...], TN)
        gu_ref[...] += _dot(dz_ref[...], hn_ref[...], TN)

    return pl.pallas_call(
        body, name="mlp_bwd_dw", grid=(f // tk, s // tm),
        in_specs=[pl.BlockSpec((tm, tk), lambda k, i: (i, k)), pl.BlockSpec((tm, tk), lambda k, i: (i, k)),
                  pl.BlockSpec((tm, d), lambda k, i: (i, 0)), pl.BlockSpec((tm, d), lambda k, i: (i, 0))],
        out_specs=[pl.BlockSpec((tk, d), lambda k, i: (k, 0)), pl.BlockSpec((tk, d), lambda k, i: (k, 0))],
        out_shape=[jax.ShapeDtypeStruct((f, d), F32), jax.ShapeDtypeStruct((f, d), F32)],
        compiler_params=_params(("parallel", "arbitrary")),
    )(z, dz, dyb, hn2)


def _mix_out_bwd(dhn2, x2, dx3, mix, ymix, w_out, mod6, g_mlp, tm):
    s, d = x2.shape

    def body(dh_ref, x2_ref, dx3_ref, mix_ref, y_ref, w_ref, mod_ref, g_ref, dx2_ref, dym_ref, gw_ref, st_ref):
        i = pl.program_id(0)

        @pl.when(i == 0)
        def _():
            st_ref[...] = jnp.zeros_like(st_ref)
            gw_ref[...] = jnp.zeros_like(gw_ref)

        dh = dh_ref[...].astype(F32)
        xhat, rstd = _rms(x2_ref[...])
        dn = dh * (1.0 + mod_ref[4:5, :])
        dx2 = dx3_ref[...].astype(F32) + _rms_bwd(dn * g_ref[...], xhat, rstd)
        dx2_ref[...] = dx2.astype(BF16)
        st_ref[0:1, :] += _colsum(dh)
        st_ref[1:2, :] += _colsum(dh * (xhat * g_ref[...]))
        st_ref[2:3, :] += _colsum(dn * xhat)
        st_ref[3:4, :] += _colsum(dx2 * mix_ref[...].astype(F32))
        dmix = (mod_ref[2:3, :] * dx2).astype(BF16)
        dym_ref[...] = _dot(dmix, w_ref[...], NT).astype(BF16)
        gw_ref[...] += _dot(y_ref[...], dmix, TN)

    tile = pl.BlockSpec((tm, d), lambda i: (i, 0))
    return _call(
        body, "mix_out_bwd", (s // tm,),
        [tile, tile, tile, tile, tile, _full(w_out.shape), _full(mod6.shape), _full(g_mlp.shape)],
        [tile, tile, _full((d, d)), _full((SUBLANES, d))],
        [jax.ShapeDtypeStruct((s, d), BF16), jax.ShapeDtypeStruct((s, d), BF16),
         jax.ShapeDtypeStruct((d, d), F32), jax.ShapeDtypeStruct((SUBLANES, d), F32)],
        [dhn2, x2, dx3, mix, ymix, w_out, mod6, g_mlp])


def _mixer_bwd(proj, dymix, h_all, conv_sc, conv_lru, conv_b, wa_bd, wx_bd, ba, bx, lam, width):
    s, din = proj.shape
    t = min(MIX_ROWS, s)
    nt = s // t
    nblk = width // LANES
    hb = t // SUBLANES
    last8 = s // SUBLANES - 1

    def body(proj_ref, projp_ref, projn_ref, dy_ref, dyn_ref, h_ref, hp_ref,
             wsc_ref, wlru_ref, blru_ref, wa_ref, wx_ref, ba_ref, bx_ref, lam_ref,
             dproj_ref, small_ref, gwa_ref, gwx_ref, an_ref, gn_ref, dun_ref, stage_ref):
        i = pl.program_id(0)

        @pl.when(i == 0)
        def _():
            small_ref[...] = jnp.zeros_like(small_ref)
            gwa_ref[...] = jnp.zeros_like(gwa_ref)
            gwx_ref[...] = jnp.zeros_like(gwx_ref)
            an_ref[...] = jnp.zeros_like(an_ref)
            gn_ref[...] = jnp.zeros_like(gn_ref)
            dun_ref[...] = jnp.zeros_like(dun_ref)

        has_prev = i < nt - 1
        has_next = i > 0
        for j in range(nblk):
            lo = j * LANES
            ls = slice(lo, lo + LANES)

            def col(p, ref=proj_ref):
                return ref[:, p * width + lo:p * width + lo + LANES]

            def prev(p):
                return jnp.where(has_prev, col(p, projp_ref), 0.0)

            def nxt(p):
                return jnp.where(has_next, col(p, projn_ref), 0.0)

            def add_row(r, v):
                small_ref[r:r + 1, ls] += _colsum(v)

            sc_b, sc_c, sc_x = col(0), col(1), col(2)
            p = sc_c * sc_x
            q, p1, p2 = _conv3(p, prev(1) * prev(2), wsc_ref, lo, stage_ref.at[0])
            dys = dy_ref[:, ls].astype(F32)
            dproj_ref[:, ls] = (dys * q).astype(BF16)
            dq = dys * sc_b
            dqn = jnp.where(has_next, dyn_ref[:, ls].astype(F32)[0:SUBLANES], 0.0) * nxt(0)
            _, (dq1, dq2) = _staged_shifts(stage_ref.at[1], dq, None, dqn, (), (1, 2))
            dp = (wsc_ref[2:3, ls] * dq + wsc_ref[1:2, ls] * dq1) + wsc_ref[0:1, ls] * dq2
            dproj_ref[:, width + lo:width + lo + LANES] = (dp * sc_x).astype(BF16)
            dproj_ref[:, 2 * width + lo:2 * width + lo + LANES] = (dp * sc_c).astype(BF16)
            add_row(0, dq * p2)
            add_row(1, dq * p1)
            add_row(2, dq * p)

            xv = col(4)
            u, x1, x2, x3 = _conv4(xv, prev(4), wlru_ref, blru_ref, lo, stage_ref.at[2])
            lam_v = lam_ref[:, ls]
            sp = _softplus(-lam_v)
            wa, wx = wa_ref[j], wx_ref[j]
            ub, r, ig, a, mult = _lru_gates(u, wa, wx, ba_ref[:, ls], bx_ref[:, ls], sp)
            iu = ig * u
            h = h_ref[:, ls]
            (hm1,), _ = _staged_shifts(stage_ref.at[3], h, jnp.where(has_prev, hp_ref[:, ls], 0.0), None, (1,), ())
            lyv = col(3)
            gel, th = _gelu(lyv)
            dyl = dy_ref[:, width + lo:width + lo + LANES].astype(F32)
            dproj_ref[:, 3 * width + lo:3 * width + lo + LANES] = (dyl * h * _dgelu(lyv, th)).astype(BF16)
            a_next = jnp.broadcast_to(an_ref[0:1, ls], (SUBLANES, LANES))
            _, (a_up,) = _staged_shifts(stage_ref.at[4], a, None, a_next, (), (1,))
            g, _ = _scan_tile(a_up, dyl * gel, gn_ref[0:1, ls], stage_ref, 5, True)
            an_ref[0:1, ls] = a[0:1, :]
            gn_ref[0:1, ls] = g[0:1, :]
            da = g * hm1
            dmult = g * iu
            diu = g * mult
            dlog_a = da * a - dmult * ((a * a) / mult)
            dpre_a = (dlog_a * (-RG_C * sp)) * (r * (1.0 - r))
            dpre_x = (diu * u) * (ig * (1.0 - ig))
            dab, dxb = dpre_a.astype(BF16), dpre_x.astype(BF16)
            du = diu * ig + _dot(dab, wa, NT) + _dot(dxb, wx, NT)
            gwa_ref[j] += _dot(ub, dab, TN)
            gwx_ref[j] += _dot(ub, dxb, TN)
            dun = dun_ref[:, ls]
            dun_ref[:, ls] = du[0:SUBLANES, :]
            _, (du1, du2, du3) = _staged_shifts(stage_ref.at[7], du, None, dun, (), (1, 2, 3))
            dlx = (((wlru_ref[3:4, ls] * du + wlru_ref[2:3, ls] * du1) + wlru_ref[1:2, ls] * du2)
                   + wlru_ref[0:1, ls] * du3)
            dproj_ref[:, 4 * width + lo:4 * width + lo + LANES] = dlx.astype(BF16)
            add_row(3, du * x3)
            add_row(4, du * x2)
            add_row(5, du * x1)
            add_row(6, du * xv)
            add_row(7, du)
            add_row(8, dpre_a)
            add_row(9, dpre_x)
            add_row(10, (dlog_a * (RG_C * r)) * jax.nn.sigmoid(-lam_v))

    small = [conv_sc, conv_lru, conv_b, wa_bd, wx_bd, ba, bx, lam]
    rev = lambda i: nt - 1 - i
    return _call(
        body, "mixer_bwd", (nt,),
        [pl.BlockSpec((t, din), lambda i: (rev(i), 0)),
         pl.BlockSpec((SUBLANES, din), lambda i: (jnp.maximum(rev(i) * hb - 1, 0), 0)),
         pl.BlockSpec((SUBLANES, din), lambda i: (jnp.minimum((rev(i) + 1) * hb, last8), 0)),
         pl.BlockSpec((t, 2 * width), lambda i: (rev(i), 0)),
         pl.BlockSpec((2 * SUBLANES, 2 * width), lambda i: (jnp.minimum((rev(i) + 1) * (hb // 2), last8 // 2), 0)),
         pl.BlockSpec((t, width), lambda i: (rev(i), 0)),
         pl.BlockSpec((SUBLANES, width), lambda i: (jnp.maximum(rev(i) * hb - 1, 0), 0))]
        + [_full(a.shape) for a in small],
        [pl.BlockSpec((t, din), lambda i: (rev(i), 0)), _full((2 * SUBLANES, width)),
         _full(wa_bd.shape), _full(wx_bd.shape)],
        [jax.ShapeDtypeStruct((s, din), BF16), jax.ShapeDtypeStruct((2 * SUBLANES, width), F32),
         jax.ShapeDtypeStruct(wa_bd.shape, F32), jax.ShapeDtypeStruct(wx_bd.shape, F32)],
        [proj, proj, proj, dymix, dymix, h_all, h_all, *small],
        scratch=[pltpu.VMEM((SUBLANES, width), F32), pltpu.VMEM((SUBLANES, width), F32),
                 pltpu.VMEM((SUBLANES, width), F32), pltpu.VMEM((8, t + 2 * SUBLANES, LANES), F32)])


def _mix_in_bwd_dx(dproj, x2d, dx2, w_in_t, mod6, g_mix, tm):
    s, d = x2d.shape
    din = dproj.shape[1]

    def body(dp_ref, x_ref, dx2_ref, w_ref, mod_ref, g_ref, gx_ref, st_ref):
        i = pl.program_id(0)

        @pl.when(i == 0)
        def _():
            st_ref[...] = jnp.zeros_like(st_ref)

        dh = _dot(dp_ref[...], w_ref[...], NN)
        xhat, rstd = _rms(x_ref[...])
        dn = dh * (1.0 + mod_ref[1:2, :])
        gx_ref[...] = dx2_ref[...].astype(F32) + _rms_bwd(dn * g_ref[...], xhat, rstd)
        st_ref[0:1, :] += _colsum(dh)
        st_ref[1:2, :] += _colsum(dh * (xhat * g_ref[...]))
        st_ref[2:3, :] += _colsum(dn * xhat)

    tile = pl.BlockSpec((tm, d), lambda i: (i, 0))
    return _call(
        body, "mix_in_bwd_dx", (s // tm,),
        [pl.BlockSpec((tm, din), lambda i: (i, 0)), tile, tile, _full(w_in_t.shape), _full(mod6.shape),
         _full(g_mix.shape)],
        [tile, _full((SUBLANES, d))],
        [jax.ShapeDtypeStruct((s, d), F32), jax.ShapeDtypeStruct((SUBLANES, d), F32)],
        [dproj, x2d, dx2, w_in_t, mod6, g_mix], vmem=VMEM_LIMIT_BIG)


def _mix_in_bwd_dw(dproj, hn1, tm, tn):
    s, d = hn1.shape
    din = dproj.shape[1]

    def body(dp_ref, hn_ref, gw_ref):
        i = pl.program_id(1)

        @pl.when(i == 0)
        def _():
            gw_ref[...] = jnp.zeros_like(gw_ref)

        gw_ref[...] += _dot(dp_ref[...], hn_ref[...], TN)

    return _call(
        body, "mix_in_bwd_dw", (din // tn, s // tm),
        [pl.BlockSpec((tm, tn), lambda p, i: (i, p)), pl.BlockSpec((tm, d), lambda p, i: (i, 0))],
        [pl.BlockSpec((tn, d), lambda p, i: (p, 0))],
        [jax.ShapeDtypeStruct((din, d), F32)],
        [dproj, hn1])


def _adamw(w, g, m, v):
    m = ADAM_B1 * m + (1.0 - ADAM_B1) * g
    v = ADAM_B2 * v + (1.0 - ADAM_B2) * (g * g)
    m_hat = m / (1.0 - ADAM_B1 ** ADAM_STEP)
    v_hat = v / (1.0 - ADAM_B2 ** ADAM_STEP)
    delta = -ADAM_LR * (m_hat / (jnp.sqrt(v_hat) + ADAM_EPS) + ADAM_WD * w)
    return delta, m, v


def _pair_sum(g4s, h4s, core_chip, tr, name):
    na = len(g4s)
    _, _, r, n = g4s[0].shape

    def body(sc_ref, *refs):
        q = pl.program_id(1)
        for a in range(na):
            g_ref, h_ref = refs[2 * a], refs[2 * a + 1]
            sb_ref, own_ref = refs[2 * na + 2 * a], refs[2 * na + 2 * a + 1]
            ssum = g_ref[...] + h_ref[...]
            sb_ref[...] = ssum.astype(BF16)

            @pl.when(q == sc_ref[1])
            def _():
                own_ref[...] = ssum

    grid_spec = pltpu.PrefetchScalarGridSpec(
        num_scalar_prefetch=1, grid=(r // tr, 4),
        in_specs=[pl.BlockSpec((None, None, tr, n), lambda i, q, sc: (q, sc[0], i, 0)),
                  pl.BlockSpec((None, tr, n), lambda i, q, sc: (q, i, 0))] * na,
        out_specs=[pl.BlockSpec((None, tr, n), lambda i, q, sc: (q, i, 0)),
                   pl.BlockSpec((tr, n), lambda i, q, sc: (i, 0))] * na)
    outs = pl.pallas_call(
        body, name=name, grid_spec=grid_spec,
        out_shape=[jax.ShapeDtypeStruct((4, r, n), BF16), jax.ShapeDtypeStruct((r, n), F32)] * na,
        compiler_params=_params(("parallel", "arbitrary")),
    )(core_chip, *[x for pair in zip(g4s, h4s) for x in pair])
    return [(outs[2 * a], outs[2 * a + 1]) for a in range(na)]


def _sum4_adam(own, parts, w, m, v, tr, name, transposed):
    r, n = own.shape
    rows, cols = w.shape

    def body(o_ref, p_ref, w_ref, m_ref, v_ref, g_ref, d_ref, nm_ref, nv_ref):
        g = o_ref[...]
        for k in range(3):
            g = g + p_ref[k].astype(F32)
        if transposed:
            g = g.T
        g_ref[...] = g
        d_ref[...], nm_ref[...], nv_ref[...] = _adamw(w_ref[...], g, m_ref[...], v_ref[...])

    if transposed:
        g_specs = [pl.BlockSpec((r, tr), lambda i: (0, i)), pl.BlockSpec((3, r, tr), lambda i: (0, 0, i))]
    else:
        g_specs = [pl.BlockSpec((tr, n), lambda i: (i, 0)), pl.BlockSpec((3, tr, n), lambda i: (0, i, 0))]
    tile = pl.BlockSpec((tr, cols), lambda i: (i, 0))
    return pl.pallas_call(
        body, name=name, grid=(rows // tr,),
        in_specs=g_specs + [tile] * 3, out_specs=[tile] * 4,
        out_shape=[jax.ShapeDtypeStruct((rows, cols), F32)] * 4,
        compiler_params=_params(("parallel",)),
    )(own, parts, w, m, v)


def _sum8(parts, tr, name):
    _, rows, n = parts.shape

    def body(p_ref, o_ref):
        acc = p_ref[0]
        for k in range(1, N_DEV):
            acc = acc + p_ref[k]
        o_ref[...] = acc

    return pl.pallas_call(
        body, name=name, grid=(rows // tr,),
        in_specs=[pl.BlockSpec((N_DEV, tr, n), lambda i: (0, i, 0))],
        out_specs=pl.BlockSpec((tr, n), lambda i: (i, 0)),
        out_shape=jax.ShapeDtypeStruct((rows, n), F32),
        compiler_params=_params(("parallel",)),
    )(parts)


def _ada_bwd_adam(cact_t, dmod_cols, w, m, v, tr):
    rows, n = w.shape

    def body(c_ref, d_ref, w_ref, m_ref, v_ref, g_ref, dl_ref, nm_ref, nv_ref):
        pad = jnp.zeros((N_DEV, tr), F32)
        ca = jnp.concatenate([c_ref[...], pad], axis=0).astype(BF16)
        dm = jnp.concatenate([d_ref[...], jnp.zeros((N_DEV, n), F32)], axis=0).astype(BF16)
        g = _dot(ca, dm, TN)
        g_ref[...] = g
        dl_ref[...], nm_ref[...], nv_ref[...] = _adamw(w_ref[...], g, m_ref[...], v_ref[...])

    tile = pl.BlockSpec((tr, n), lambda i: (i, 0))
    return pl.pallas_call(
        body, name="ada_bwd_adam", grid=(rows // tr,),
        in_specs=[pl.BlockSpec((N_DEV, tr), lambda i: (0, i)), _full(dmod_cols.shape), tile, tile, tile],
        out_specs=[tile] * 4,
        out_shape=[jax.ShapeDtypeStruct((rows, n), F32)] * 4,
        compiler_params=_params(("parallel",)),
    )(cact_t, dmod_cols, w, m, v)


def _adam_small(ws, gs, ms, vs):
    n = len(ws)

    def body(*refs):
        w_r, g_r, m_r, v_r = refs[:n], refs[n:2 * n], refs[2 * n:3 * n], refs[3 * n:4 * n]
        d_r, nm_r, nv_r = refs[4 * n:5 * n], refs[5 * n:6 * n], refs[6 * n:7 * n]
        for k in range(n):
            d_r[k][...], nm_r[k][...], nv_r[k][...] = _adamw(w_r[k][...], g_r[k][...], m_r[k][...], v_r[k][...])

    shapes = [jax.ShapeDtypeStruct(w.shape, F32) for w in ws]
    outs = pl.pallas_call(
        body, name="adam_small", out_shape=shapes * 3, compiler_params=_params(),
    )(*ws, *gs, *ms, *vs)
    return outs[:n], outs[n:2 * n], outs[2 * n:]


def _block_diag(w):
    h, hd, _ = w.shape
    per = LANES // hd
    eye = jnp.eye(per, dtype=w.dtype)
    w5 = w.reshape(h // per, per, hd, 1, hd) * eye[None, :, None, :, None]
    return w5.reshape(h // per, LANES, LANES)


def _block_diag_grad(g, h, hd):
    per = LANES // hd
    g5 = g.reshape(h // per, per, hd, per, hd)
    return jnp.stack([g5[:, a, :, a, :] for a in range(per)], axis=1).reshape(h, hd, hd)


def kernel(x, c, w_ada, b_ada, g_mix, w_in, conv_w_sc, conv_w_lru, conv_b_lru, w_rg_a, b_rg_a, w_rg_x, b_rg_x, lru_lambda, w_out, g_mlp, w_up, w_down, g_final, loss_target, m_w_ada, m_b_ada, m_g_mix, m_w_in, m_conv_w_sc, m_conv_w_lru, m_conv_b_lru, m_w_rg_a, m_b_rg_a, m_w_rg_x, m_b_rg_x, m_lru_lambda, m_w_out, m_g_mlp, m_w_up, m_w_down, m_g_final, v_w_ada, v_b_ada, v_g_mix, v_w_in, v_conv_w_sc, v_conv_w_lru, v_conv_b_lru, v_w_rg_a, v_b_rg_a, v_w_rg_x, v_b_rg_x, v_lru_lambda, v_w_out, v_g_mlp, v_w_up, v_w_down, v_g_final):
    s, d = x.shape[1], x.shape[2]
    width = conv_b_lru.shape[1]
    heads, hd = w_rg_a.shape[1], w_rg_a.shape[2]
    n_ada = w_ada.shape[2]
    csh = conv_w_sc.shape[2]
    me = 4 * lax.axis_index("x") + 2 * lax.axis_index("y") + lax.axis_index("c")
    tm = min(512, s)
    tm_mlp = min(1024, s)
    tk = 512

    x2d = x[0]
    tgt = loss_target[0]

    pay = jnp.zeros((SUBLANES, d), F32)
    pay = pay.at[0:1, :].set(c)
    pay = pay.at[1:4, 0:csh].set(conv_w_sc[0])
    pay = pay.at[4:8, 0:csh].set(conv_w_lru[0])
    w_in_t_sh = w_in[0].T.astype(BF16)
    w_up_t_sh = w_up[0].T.astype(BF16)
    w_out_sh = w_out[0].astype(BF16)
    w_down_sh = w_down[0].astype(BF16)
    (w_in_t,) = _seq_gather2("gather_w_in", 10, [w_in_t_sh])
    (pay_all,) = _gather2("gather_in", [pay])
    w_in_t = w_in_t.reshape(-1, d)
    c_all = pay_all[:, 0, :]
    conv_sc = pay_all[:, 1:4, 0:csh].transpose(1, 0, 2).reshape(3, width)
    conv_lru = pay_all[:, 4:8, 0:csh].transpose(1, 0, 2).reshape(4, width)

    b_ada_sh = lax.dynamic_slice(b_ada, (0, me * n_ada), (1, n_ada))
    mod_cols, c_act = _ada_fwd(c_all, w_ada[0], b_ada_sh)
    (mod_rows,) = _exchange("scatter_mod", [], [mod_cols.reshape(N_DEV, 1, n_ada)])
    mod_rows, w_out_sh, w_up_t_sh, w_down_sh = lax.optimization_barrier((mod_rows, w_out_sh, w_up_t_sh, w_down_sh))
    (w_out_g,) = _seq_gather2("gather_w_out", 1, [w_out_sh])
    w_up_g, w_down_g = _seq_gather2("gather_mlp_weights", 2, [w_up_t_sh, w_down_sh])
    mod6 = jnp.zeros((SUBLANES, d), F32).at[0:6, :].set(mod_rows.reshape(6, d))

    wa_bd = _block_diag(w_rg_a[0]).astype(BF16)
    wx_bd = _block_diag(w_rg_x[0]).astype(BF16)
    ba = b_rg_a.reshape(1, width)
    bx = b_rg_x.reshape(1, width)
    g_fin = g_final.reshape(1, d)

    hn1, proj, ymix, h_all = _mix_in_mixer_fwd(x2d, mod6, g_mix, w_in_t, conv_sc, conv_lru, conv_b_lru,
                                               wa_bd, wx_bd, ba, bx, lru_lambda, width, tm)
    w_out_b = w_out_g.reshape(-1, d)
    mix, x2, hn2 = _mix_out_fwd(ymix, x2d, w_out_b, mod6, g_mlp, tm_mlp)
    w_up_t = w_up_g.reshape(-1, d)
    w_down_b = w_down_g.reshape(-1, d)
    z, dx3, dyb, st_fin = _mlp_fwd_loss(hn2, w_up_t, w_down_b, x2, tgt, mod6, g_fin, tm_mlp, 2 * tk)

    core_chip = jnp.stack([lax.axis_index("c"), 2 * lax.axis_index("x") + lax.axis_index("y")]).astype(jnp.int32)
    dz, dhn2 = _mlp_bwd_dx(dyb, z, w_down_b, w_up_t, tm_mlp, 2 * tk)
    g_down, g_up_t = _mlp_bwd_dw(z, dz, dyb, hn2, tm_mlp, 2 * tk)
    g_up4, g_down4 = g_up_t.reshape(4, 2, -1, d), g_down.reshape(4, 2, -1, d)
    h_up, h_down = _seq_pair_swap("swap_mlp_grads", 7, [g_up4, g_down4])
    dx2, dymix, g_out, st_out = _mix_out_bwd(dhn2, x2, dx3, mix, ymix, w_out_b, mod6, g_mlp, tm)
    h_up, h_down, g_out = lax.optimization_barrier((h_up, h_down, g_out))
    (sb_up, own_up), (sb_down, own_down) = _pair_sum([g_up4, g_down4], [h_up, h_down], core_chip, g_up4.shape[2], "pair_sum_mlp")
    g_out4 = g_out.reshape(4, 2, -1, d)
    (h_out,) = _seq_pair_swap("swap_w_out_grad", 8, [g_out4])
    p_up, p_down = _seq_chip_exchange("exchange_mlp_grads", 3, [sb_up, sb_down])
    dproj, g_small, g_wa, g_wx = _mixer_bwd(
        proj, dymix, h_all, conv_sc, conv_lru, conv_b_lru, wa_bd, wx_bd, ba, bx, lru_lambda, width)
    h_out, dproj = lax.optimization_barrier((h_out, dproj))
    ((sb_out, own_out),) = _pair_sum([g_out4], [h_out], core_chip, g_out4.shape[2], "pair_sum_w_out")
    (p_out,) = _seq_chip_exchange("exchange_w_out_grad", 4, [sb_out])
    grad_x, st_in = _mix_in_bwd_dx(dproj, x2d, dx2, w_in_t, mod6, g_mix, tm_mlp)

    small = jnp.concatenate([
        st_in[0:2], st_out[3:4], st_out[0:2], st_fin[1:2],
        st_in[2:3], st_out[2:3], st_fin[0:1],
        jnp.concatenate([g_small[7:8], g_small[10:11]], axis=1),
        jnp.concatenate([g_small[8:9], g_small[9:10]], axis=1),
        jnp.concatenate([jnp.concatenate([g_small[0:3], jnp.zeros((1, width), F32)], axis=0), g_small[3:7]], axis=1),
        st_fin[2:3],
        _block_diag_grad(g_wa, heads, hd).reshape(-1, d),
        _block_diag_grad(g_wx, heads, hd).reshape(-1, d),
    ], axis=0)

    (small_all,) = _seq_gather2("gather_small_grads", 5, [small])
    g_in_t, = _mix_in_bwd_dw(dproj, hn1, min(2048, s), dproj.shape[1] // 2)
    g_in4 = g_in_t.reshape(4, 2, -1, d)
    (h_in,) = _seq_pair_swap("swap_w_in_grad", 9, [g_in4])
    p_up, p_down, p_out, small_all, g_in_t = lax.optimization_barrier((p_up, p_down, p_out, small_all, g_in_t))

    ad_up = _sum4_adam(own_up, p_up, w_up[0], m_w_up[0], v_w_up[0], 256, "adam_w_up", True)
    h_in, ad_up = lax.optimization_barrier((h_in, ad_up))
    ((sb_in, own_in),) = _pair_sum([g_in4], [h_in], core_chip, g_in4.shape[2], "pair_sum_w_in")
    (p_in,) = _seq_chip_exchange("exchange_w_in_grad", 6, [sb_in])
    ad_out = _sum4_adam(own_out, p_out, w_out[0], m_w_out[0], v_w_out[0], w_out.shape[1], "adam_w_out", False)
    ad_down = _sum4_adam(own_down, p_down, w_down[0], m_w_down[0], v_w_down[0], 256, "adam_w_down", False)

    gsum = _sum8(small_all, SMALL_ROWS, "sum_small")
    loss = (0.5 / d) * jnp.sum(gsum[15])
    dmod_cols = lax.dynamic_slice(small_all[:, 0:6, :].reshape(N_DEV, 6 * d), (0, me * n_ada), (N_DEV, n_ada))
    g_ada, d_ada, nm_ada, nv_ada = _ada_bwd_adam(c_act, dmod_cols, w_ada[0], m_w_ada[0], v_w_ada[0], 256)

    g_conv = lax.dynamic_slice(gsum[11:15, 0:width], (0, me * csh), (4, csh))
    g_conv_l = lax.dynamic_slice(gsum[11:15, width:2 * width], (0, me * csh), (4, csh))
    small_g = [
        gsum[0:6].reshape(1, 6 * d),
        gsum[6:7],
        g_conv[0:3].reshape(1, 3, csh),
        g_conv_l.reshape(1, 4, csh),
        gsum[9:10, 0:width],
        gsum[16:48].reshape(1, heads, hd, hd),
        gsum[10:11, 0:width].reshape(1, heads, hd),
        gsum[48:80].reshape(1, heads, hd, hd),
        gsum[10:11, width:].reshape(1, heads, hd),
        gsum[9:10, width:],
        gsum[7:8],
        gsum[8],
    ]
    small_w = [b_ada, g_mix, conv_w_sc, conv_w_lru, conv_b_lru, w_rg_a, b_rg_a, w_rg_x, b_rg_x, lru_lambda, g_mlp, g_final]
    small_m = [m_b_ada, m_g_mix, m_conv_w_sc, m_conv_w_lru, m_conv_b_lru, m_w_rg_a, m_b_rg_a, m_w_rg_x, m_b_rg_x,
               m_lru_lambda, m_g_mlp, m_g_final]
    small_v = [v_b_ada, v_g_mix, v_conv_w_sc, v_conv_w_lru, v_conv_b_lru, v_w_rg_a, v_b_rg_a, v_w_rg_x, v_b_rg_x,
               v_lru_lambda, v_g_mlp, v_g_final]
    sd, snm, snv = _adam_small(small_w, small_g, small_m, small_v)
    p_in, ad_out, ad_down, (g_ada, d_ada, nm_ada, nv_ada), sd = lax.optimization_barrier(
        (p_in, ad_out, ad_down, (g_ada, d_ada, nm_ada, nv_ada), sd))
    ad_in = _sum4_adam(own_in, p_in, w_in[0].T, m_w_in[0].T, v_w_in[0].T, own_in.shape[0], "adam_w_in", False)
    ad_in = [a.T for a in ad_in]

    def order(ada, w_in_, w_out_, w_up_, w_down_, sm):
        return [ada[None], sm[0], sm[1], w_in_[None], sm[2], sm[3], sm[4], sm[5], sm[6], sm[7], sm[8], sm[9],
                w_out_[None], sm[10], w_up_[None], w_down_[None], sm[11]]

    grads = order(g_ada, ad_in[0], ad_out[0], ad_up[0], ad_down[0], small_g)
    deltas = order(d_ada, ad_in[1], ad_out[1], ad_up[1], ad_down[1], sd)
    new_m = order(nm_ada, ad_in[2], ad_out[2], ad_up[2], ad_down[2], snm)
    new_v = order(nv_ada, ad_in[3], ad_out[3], ad_up[3], ad_down[3], snv)
    return (loss, grad_x[None], *grads, *deltas, *new_m, *new_v)
```

```python
import jax
import jax.numpy as jnp
from jax import lax
from jax.experimental import pallas as pl
from jax.experimental.pallas import tpu as pltpu
from jax.experimental.pallas import tpu_sc as plsc

F32 = jnp.float32
BF16 = jnp.bfloat16
N_DEV = 8
EPS = 1e-6
RG_C = 8.0
GELU_K0 = 0.7978845608028654
GELU_K1 = 0.044715
ADAM_LR = 0.001
ADAM_B1 = 0.9
ADAM_B2 = 0.999
ADAM_EPS = 1e-08
ADAM_WD = 0.01
ADAM_STEP = 10
LANES = 128
SUBLANES = 8
VMEM_LIMIT = 52 * 1024 * 1024
VMEM_LIMIT_BIG = 58 * 1024 * 1024
MIX_ROWS = 256
SMALL_ROWS = 80

MESH = pl.DeviceIdType.MESH
ANY = pl.BlockSpec(memory_space=pl.ANY)
NN = ((1,), (0,))
NT = ((1,), (1,))
TN = ((0,), (0,))


def _dot(a, b, dims):
    return lax.dot_general(a, b, (dims, ((), ())), preferred_element_type=F32)


def _params(sem=None):
    return pltpu.CompilerParams(dimension_semantics=sem, vmem_limit_bytes=VMEM_LIMIT)


def _full(shape):
    nd = len(shape)
    return pl.BlockSpec(shape, lambda *_: (0,) * nd)


def _exchange(name, gathers, scatters):
    n_g = len(gathers)
    arrs = list(gathers) + list(scatters)
    n = len(arrs)
    out_shape = [jax.ShapeDtypeStruct((N_DEV,) + a.shape, a.dtype) for a in gathers]
    out_shape += [jax.ShapeDtypeStruct(a.shape, a.dtype) for a in scatters]

    def body(*refs):
        ins, outs = refs[:n], refs[n:2 * n]
        send_sems, recv_sems, local_sems = refs[2 * n:]
        x, y, c = lax.axis_index("x"), lax.axis_index("y"), lax.axis_index("c")
        me = 4 * x + 2 * y + c

        def src(a, dev):
            return ins[a] if a < n_g else ins[a].at[dev]

        def peer_of(k):
            px = 1 - x if (k >> 2) & 1 else x
            py = 1 - y if (k >> 1) & 1 else y
            pc = 1 - c if k & 1 else c
            return (px, py, pc), 4 * px + 2 * py + pc

        local = [pltpu.make_async_copy(src(a, me), outs[a].at[me], local_sems.at[a]) for a in range(n)]
        for cp in local:
            cp.start()
        sends = []
        for k in range(1, N_DEV):
            peer, pidx = peer_of(k)
            for a in range(n):
                cp = pltpu.make_async_remote_copy(
                    src_ref=src(a, pidx), dst_ref=outs[a].at[me],
                    send_sem=send_sems.at[a * (N_DEV - 1) + k - 1], recv_sem=recv_sems.at[a * (N_DEV - 1) + k - 1],
                    device_id=peer, device_id_type=MESH)
                cp.start()
                sends.append(cp)
        for k in range(1, N_DEV):
            peer, pidx = peer_of(k)
            for a in range(n):
                pltpu.make_async_remote_copy(
                    src_ref=src(a, pidx), dst_ref=outs[a].at[pidx],
                    send_sem=send_sems.at[a * (N_DEV - 1) + k - 1], recv_sem=recv_sems.at[a * (N_DEV - 1) + k - 1],
                    device_id=peer, device_id_type=MESH).wait_recv()
        for cp in sends:
            cp.wait_send()
        for cp in local:
            cp.wait()

    return pl.pallas_call(
        body, name=name, out_shape=out_shape,
        in_specs=[ANY] * n, out_specs=[ANY] * n,
        scratch_shapes=[pltpu.SemaphoreType.DMA((n * (N_DEV - 1),)),
                        pltpu.SemaphoreType.DMA((n * (N_DEV - 1),)),
                        pltpu.SemaphoreType.DMA((n,))],
    )(*arrs)


GATHER_SEMS = 7


def _gather_copies(ins, outs, send_sems, recv_sems, local_sems, x, y, c):
    n = len(ins)
    per = GATHER_SEMS
    sib = (x, y, 1 - c)
    xn, yn, dg = (1 - x, y), (x, 1 - y), (1 - x, 1 - y)
    fx, fy = x + (1 - c) * (1 - 2 * x), y + c * (1 - 2 * y)
    tx, ty = x + c * (1 - 2 * x), y + (1 - c) * (1 - 2 * y)

    def slot(a, px, py, pc):
        return outs[a].at[4 * px + 2 * py + pc]

    def copy(a, k, block, to, src=None):
        return pltpu.make_async_remote_copy(
            src_ref=slot(a, *block) if src is None else src, dst_ref=slot(a, *block),
            send_sem=send_sems.at[a * per + k], recv_sem=recv_sems.at[a * per + k],
            device_id=to, device_id_type=MESH)

    local = [pltpu.make_async_copy(ins[a], slot(a, x, y, c), local_sems.at[a]) for a in range(n)]
    for cp in local:
        cp.start()
    started = []
    for a in range(n):
        started += [copy(a, 1, (x, y, c), (*xn, c), src=ins[a]), copy(a, 2, (x, y, c), (*yn, c), src=ins[a])]
    for a in range(n):
        started.append(copy(a, 0, (x, y, c), sib, src=ins[a]))
    for cp in started:
        cp.start()
    for a in range(n):
        copy(a, 1, (*xn, c), (x, y, c)).wait_recv()
        copy(a, 2, (*yn, c), (x, y, c)).wait_recv()
        later = [copy(a, 3, (fx, fy, c), (tx, ty, c)), copy(a, 4, (*xn, c), sib), copy(a, 5, (*yn, c), sib)]
        for cp in later:
            cp.start()
        started += later
    for a in range(n):
        copy(a, 3, (*dg, c), (x, y, c)).wait_recv()
        cp = copy(a, 6, (*dg, c), sib)
        cp.start()
        started.append(cp)
    for a in range(n):
        copy(a, 0, sib, (x, y, c)).wait_recv()
        for k, chip in ((4, xn), (5, yn), (6, dg)):
            copy(a, k, (*chip, 1 - c), (x, y, c)).wait_recv()
    for cp in started:
        cp.wait_send()
    for cp in local:
        cp.wait()


def _gather2(name, arrs):
    n = len(arrs)
    per = GATHER_SEMS
    out_shape = [jax.ShapeDtypeStruct((N_DEV,) + a.shape, a.dtype) for a in arrs]

    def body(*refs):
        ins, outs = refs[:n], refs[n:2 * n]
        send_sems, recv_sems, local_sems = refs[2 * n:]
        x, y, c = lax.axis_index("x"), lax.axis_index("y"), lax.axis_index("c")
        _gather_copies(ins, outs, send_sems, recv_sems, local_sems, x, y, c)

    return pl.pallas_call(
        body, name=name, out_shape=out_shape,
        in_specs=[ANY] * n, out_specs=[ANY] * n,
        scratch_shapes=[pltpu.SemaphoreType.DMA((n * per,)), pltpu.SemaphoreType.DMA((n * per,)),
                        pltpu.SemaphoreType.DMA((n,))],
    )(*arrs)


def _seq_gather2(name, collective_id, arrs):
    n = len(arrs)
    per = GATHER_SEMS

    def body(*refs):
        ins, outs = refs[:n], refs[n:2 * n]
        send_sems, recv_sems, local_sems = refs[2 * n:]
        x, y, c = lax.axis_index("x"), lax.axis_index("y"), lax.axis_index("c")
        barrier = pltpu.get_barrier_semaphore()
        for peer in [(x, y, 1 - c), (1 - x, y, c), (x, 1 - y, c)]:
            pl.semaphore_signal(barrier, inc=1, device_id=peer, device_id_type=MESH)
        pl.semaphore_wait(barrier, 3)
        _gather_copies(ins, outs, send_sems, recv_sems, local_sems, x, y, c)

    return pl.kernel(
        body, out_type=[jax.ShapeDtypeStruct((N_DEV,) + a.shape, a.dtype) for a in arrs],
        mesh=plsc.ScalarSubcoreMesh(axis_name="seq", num_cores=1),
        scratch_types=[pltpu.SemaphoreType.DMA((n * per,)), pltpu.SemaphoreType.DMA((n * per,)),
                       pltpu.SemaphoreType.DMA((n,))],
        compiler_params=pltpu.CompilerParams(collective_id=collective_id), name=name,
    )(*arrs)


def _seq_chip_exchange(name, collective_id, arrs):
    n = len(arrs)

    def body(*refs):
        ins, outs = refs[:n], refs[n:2 * n]
        send_sems, recv_sems = refs[2 * n:]
        x, y, c = lax.axis_index("x"), lax.axis_index("y"), lax.axis_index("c")

        def peer(k):
            return (1 - x if (k >> 1) & 1 else x), (1 - y if k & 1 else y)

        barrier = pltpu.get_barrier_semaphore()
        for k in (1, 2, 3):
            pl.semaphore_signal(barrier, inc=1, device_id=(*peer(k), c), device_id_type=MESH)
        pl.semaphore_wait(barrier, 3)

        def copy(a, k):
            px, py = peer(k)
            return pltpu.make_async_remote_copy(
                src_ref=ins[a].at[2 * px + py], dst_ref=outs[a].at[k - 1],
                send_sem=send_sems.at[a * 3 + k - 1], recv_sem=recv_sems.at[a * 3 + k - 1],
                device_id=(px, py, c), device_id_type=MESH)

        cps = [copy(a, k) for a in range(n) for k in (1, 2, 3)]
        for cp in cps:
            cp.start()
        for cp in cps:
            cp.wait_recv()
        for cp in cps:
            cp.wait_send()

    return pl.kernel(
        body, out_type=[jax.ShapeDtypeStruct((3,) + a.shape[1:], a.dtype) for a in arrs],
        mesh=plsc.ScalarSubcoreMesh(axis_name="seq", num_cores=1),
        scratch_types=[pltpu.SemaphoreType.DMA((n * 3,)), pltpu.SemaphoreType.DMA((n * 3,))],
        compiler_params=pltpu.CompilerParams(collective_id=collective_id), name=name,
    )(*arrs)


def _seq_pair_swap(name, collective_id, arrs):
    n = len(arrs)

    def body(*refs):
        ins, outs = refs[:n], refs[n:2 * n]
        send_sems, recv_sems = refs[2 * n:]
        x, y, c = lax.axis_index("x"), lax.axis_index("y"), lax.axis_index("c")
        barrier = pltpu.get_barrier_semaphore()
        pl.semaphore_signal(barrier, inc=1, device_id=(x, y, 1 - c), device_id_type=MESH)
        pl.semaphore_wait(barrier, 1)

        def copy(a, q):
            return pltpu.make_async_remote_copy(
                src_ref=ins[a].at[q, 1 - c], dst_ref=outs[a].at[q],
                send_sem=send_sems.at[a * 4 + q], recv_sem=recv_sems.at[a * 4 + q],
                device_id=(x, y, 1 - c), device_id_type=MESH)

        cps = [copy(a, q) for a in range(n) for q in range(4)]
        for cp in cps:
            cp.start()
        for cp in cps:
            cp.wait_recv()
        for cp in cps:
            cp.wait_send()

    return pl.kernel(
        body, out_type=[jax.ShapeDtypeStruct((4,) + a.shape[2:], a.dtype) for a in arrs],
        mesh=plsc.ScalarSubcoreMesh(axis_name="seq", num_cores=1),
        scratch_types=[pltpu.SemaphoreType.DMA((n * 4,)), pltpu.SemaphoreType.DMA((n * 4,))],
        compiler_params=pltpu.CompilerParams(collective_id=collective_id), name=name,
    )(*arrs)


def _call(body, name, grid, in_specs, out_specs, out_shape, args, scratch=(), vmem=VMEM_LIMIT):
    return pl.pallas_call(
        body, name=name, grid=grid, in_specs=in_specs, out_specs=out_specs, out_shape=out_shape,
        scratch_shapes=list(scratch),
        compiler_params=pltpu.CompilerParams(dimension_semantics=("arbitrary",) * len(grid), vmem_limit_bytes=vmem),
    )(*args)


def _ada_fwd(c_all, w_ada_sh, b_ada_sh):
    nb, d = c_all.shape
    ncol = w_ada_sh.shape[1]

    def body(c_ref, w_ref, b_ref, mod_ref, cact_ref):
        cc = c_ref[...]
        ca = cc * jax.nn.sigmoid(cc)
        cact_ref[...] = ca
        mod_ref[...] = _dot(ca.astype(BF16), w_ref[...].astype(BF16), NN) + b_ref[...]

    return pl.pallas_call(
        body, name="ada_fwd",
        out_shape=[jax.ShapeDtypeStruct((nb, ncol), F32), jax.ShapeDtypeStruct((nb, d), F32)],
        compiler_params=_params(),
    )(c_all, w_ada_sh, b_ada_sh)


def _rms(xv):
    rstd = lax.rsqrt(jnp.mean(xv * xv, axis=-1, keepdims=True) + EPS)
    return xv * rstd, rstd


def _rms_bwd(dxhat, xhat, rstd):
    return rstd * (dxhat - xhat * jnp.mean(dxhat * xhat, axis=-1, keepdims=True))


def _colsum(v):
    return jnp.sum(v, axis=0, keepdims=True)


def _expm1(v, ev):
    series = v * (1.0 + v * (0.5 + v * (1.0 / 6.0 + v * (1.0 / 24.0 + v * (1.0 / 120.0)))))
    return jnp.where(jnp.abs(v) < 0.2, series, ev - 1.0)


def _softplus(v):
    return jnp.maximum(v, 0.0) + jnp.log1p(jnp.exp(-jnp.abs(v)))


def _gelu(v):
    t = jnp.tanh(v * (GELU_K0 + (GELU_K0 * GELU_K1) * (v * v)))
    return 0.5 * v * (1.0 + t), t


def _dgelu(v, t):
    return 0.5 * ((1.0 + t) + (v * (1.0 - t * t)) * (GELU_K0 + (3.0 * GELU_K0 * GELU_K1) * (v * v)))


def _scan_tile(a, b, x0, st, k0, reverse):
    t = a.shape[0]
    off = SUBLANES
    stage_a, stage_b = st.at[k0], st.at[k0 + 1]
    halo = slice(off + t, off + t + SUBLANES) if reverse else slice(0, SUBLANES)
    stage_a[halo, :] = jnp.ones((SUBLANES, a.shape[1]), F32)
    stage_b[halo, :] = jnp.zeros((SUBLANES, a.shape[1]), F32)
    s = 1
    while s < min(t, SUBLANES):
        stage_a[off:off + t, :] = a
        stage_b[off:off + t, :] = b
        at = off + s if reverse else off - s
        b = a * stage_b[at:at + t, :] + b
        a = a * stage_a[at:at + t, :]
        s *= 2
    while s < t:
        if reverse:
            b = jnp.concatenate([a[:t - s] * b[s:] + b[:t - s], b[t - s:]], axis=0)
            a = jnp.concatenate([a[:t - s] * a[s:], a[t - s:]], axis=0)
        else:
            b = jnp.concatenate([b[:s], a[s:] * b[:t - s] + b[s:]], axis=0)
            a = jnp.concatenate([a[:s], a[s:] * a[:t - s]], axis=0)
        s *= 2
    x = b + a * x0
    return x, (x[0:SUBLANES, :] if reverse else x[t - SUBLANES:t, :])


def _lru_gates(u, wa, wx, ba, bx, sp):
    ub = u.astype(BF16)
    r = jax.nn.sigmoid(_dot(ub, wa, NN) + ba)
    i = jax.nn.sigmoid(_dot(ub, wx, NN) + bx)
    log_a = (-RG_C * r) * sp
    a = jnp.exp(log_a)
    mult = jnp.sqrt(-_expm1(log_a, a) * (a + 1.0))
    return ub, r, i, a, mult


def _staged_shifts(stage, v, prev8, next8, downs, ups):
    t = v.shape[0]
    if prev8 is not None:
        stage[0:SUBLANES, :] = prev8
    stage[SUBLANES:SUBLANES + t, :] = v
    if next8 is not None:
        stage[SUBLANES + t:2 * SUBLANES + t, :] = next8
    return ([stage[SUBLANES - k:SUBLANES - k + t, :] for k in downs],
            [stage[SUBLANES + k:SUBLANES + k + t, :] for k in ups])


def _conv3(p, pp, w_ref, lo, stage):
    (p1, p2), _ = _staged_shifts(stage, p, pp, None, (1, 2), ())
    q = (w_ref[0:1, lo:lo + LANES] * p2 + w_ref[1:2, lo:lo + LANES] * p1) + w_ref[2:3, lo:lo + LANES] * p
    return q, p1, p2


def _conv4(xv, xp, w_ref, b_ref, lo, stage):
    (x1, x2, x3), _ = _staged_shifts(stage, xv, xp, None, (1, 2, 3), ())
    u = (((w_ref[0:1, lo:lo + LANES] * x3 + w_ref[1:2, lo:lo + LANES] * x2) + w_ref[2:3, lo:lo + LANES] * x1)
         + w_ref[3:4, lo:lo + LANES] * xv) + b_ref[:, lo:lo + LANES]
    return u, x1, x2, x3


def _mix_in_mixer_fwd(x2d, mod6, g_mix, w_in_t, conv_sc, conv_lru, conv_b, wa_bd, wx_bd, ba, bx, lam, width, tm):
    s, d = x2d.shape
    din = w_in_t.shape[0]
    nt = s // tm
    sub = min(MIX_ROWS, tm)
    nblk = width // LANES

    def body(x_ref, mod_ref, g_ref, w_ref, wsc_ref, wlru_ref, blru_ref, wa_ref, wx_ref, ba_ref, bx_ref, lam_ref,
             hn_ref, proj_ref, ymix_ref, h_ref, buf_ref, halo_ref, hc_ref, stage_ref):
        i = pl.program_id(0)

        @pl.when(i == 0)
        def _():
            buf_ref[1] = jnp.zeros((tm, din), F32)
            halo_ref[...] = jnp.zeros_like(halo_ref)

        @pl.when(i <= 1)
        def _():
            hc_ref[...] = jnp.zeros_like(hc_ref)

        def step(dst, src):
            xhat, _ = _rms(x_ref[...])
            hn = ((xhat * g_ref[...]) * (1.0 + mod_ref[1:2, :]) + mod_ref[0:1, :]).astype(BF16)
            hn_ref[...] = hn
            n_mix = (tm // sub) * nblk
            n_chunk = din // width

            def project(k):
                res = _dot(hn_ref[...], w_ref[k * width:(k + 1) * width, :], NT)
                proj_ref[:, k * width:(k + 1) * width] = res
                dst[:, k * width:(k + 1) * width] = res

            done = 0
            for half in range(tm // sub):
                r0 = half * sub
                rows = slice(r0, r0 + sub)
                for j in range(nblk):
                    lo = j * LANES
                    while done < n_chunk and done * n_mix <= (half * nblk + j) * n_chunk:
                        project(done)
                        done += 1

                    def col(p):
                        return src[rows, p * width + lo:p * width + lo + LANES]

                    def prev(p):
                        c0 = p * width + lo
                        if half == 0:
                            return halo_ref[:, c0:c0 + LANES]
                        return src[r0 - SUBLANES:r0, c0:c0 + LANES]

                    pp = col(1) * col(2)
                    q, _, _ = _conv3(pp, prev(1) * prev(2), wsc_ref, lo, stage_ref.at[0])
                    ymix_ref[rows, lo:lo + LANES] = (col(0) * q).astype(BF16)

                    u, _, _, _ = _conv4(col(4), prev(4), wlru_ref, blru_ref, lo, stage_ref.at[1])
                    sp = _softplus(-lam_ref[:, lo:lo + LANES])
                    _, r, ig, a, mult = _lru_gates(u, wa_ref[j], wx_ref[j], ba_ref[:, lo:lo + LANES],
                                                   bx_ref[:, lo:lo + LANES], sp)
                    h, ends = _scan_tile(a, mult * (ig * u), hc_ref[0:1, lo:lo + LANES], stage_ref, 2, False)
                    h_ref[rows, lo:lo + LANES] = h
                    hc_ref[0:1, lo:lo + LANES] = ends[SUBLANES - 1:SUBLANES, :]
                    gel, _ = _gelu(col(3))
                    ymix_ref[rows, width + lo:width + lo + LANES] = (gel * h).astype(BF16)
            while done < n_chunk:
                project(done)
                done += 1
            halo_ref[...] = src[tm - SUBLANES:tm, :]

        @pl.when(i % 2 == 0)
        def _():
            step(buf_ref.at[0], buf_ref.at[1])

        @pl.when(i % 2 == 1)
        def _():
            step(buf_ref.at[1], buf_ref.at[0])

    small = [conv_sc, conv_lru, conv_b, wa_bd, wx_bd, ba, bx, lam]
    cur = lambda i: (jnp.minimum(i, nt - 1), 0)
    last = lambda i: (jnp.maximum(i - 1, 0), 0)
    outs = _call(
        body, "mix_in_mixer_fwd", (nt + 1,),
        [pl.BlockSpec((tm, d), cur), _full(mod6.shape), _full(g_mix.shape), _full(w_in_t.shape)]
        + [_full(a.shape) for a in small],
        [pl.BlockSpec((tm, d), cur), pl.BlockSpec((tm, din), cur),
         pl.BlockSpec((tm, 2 * width), last), pl.BlockSpec((tm, width), last)],
        [jax.ShapeDtypeStruct((s, d), BF16), jax.ShapeDtypeStruct((s, din), F32),
         jax.ShapeDtypeStruct((s, 2 * width), BF16), jax.ShapeDtypeStruct((s, width), F32)],
        [x2d, mod6, g_mix, w_in_t, *small],
        scratch=[pltpu.VMEM((2, tm, din), F32), pltpu.VMEM((SUBLANES, din), F32), pltpu.VMEM((SUBLANES, width), F32),
                 pltpu.VMEM((4, sub + 2 * SUBLANES, LANES), F32)])
    return outs


def _mix_out_fwd(ymix, x2d, w_out, mod6, g_mlp, tm):
    s, d = x2d.shape

    def body(y_ref, x_ref, w_ref, mod_ref, g_ref, mix_ref, x2_ref, hn_ref):
        mix = _dot(y_ref[...], w_ref[...], NN)
        mix_ref[...] = mix.astype(BF16)
        x2 = x_ref[...] + mod_ref[2:3, :] * mix
        x2_ref[...] = x2
        xhat, _ = _rms(x2)
        hn_ref[...] = ((xhat * g_ref[...]) * (1.0 + mod_ref[4:5, :]) + mod_ref[3:4, :]).astype(BF16)

    tile = pl.BlockSpec((tm, d), lambda i: (i, 0))
    return _call(
        body, "mix_out_fwd", (s // tm,),
        [tile, tile, _full(w_out.shape), _full(mod6.shape), _full(g_mlp.shape)],
        [tile, tile, tile],
        [jax.ShapeDtypeStruct((s, d), BF16), jax.ShapeDtypeStruct((s, d), F32), jax.ShapeDtypeStruct((s, d), BF16)],
        [ymix, x2d, w_out, mod6, g_mlp])


def _mlp_fwd_loss(hn2, w_up_t, w_down, x2, target, mod6, g_final, tm, tk):
    s, d = hn2.shape
    f = w_up_t.shape[0]
    nk = f // tk

    def body(hn_ref, wu_ref, wd_ref, x2_hbm, t_hbm, mod_ref, g_ref, z_ref, dx3_ref, dyb_ref, st_ref,
             y_ref, x2_ref, t_ref, sems):
        i, k = pl.program_id(0), pl.program_id(1)

        def fetch():
            rows = pl.ds(pl.multiple_of(i * tm, tm), tm)
            return (pltpu.make_async_copy(x2_hbm.at[rows, :], x2_ref, sems.at[0]),
                    pltpu.make_async_copy(t_hbm.at[rows, :], t_ref, sems.at[1]))

        @pl.when(jnp.logical_and(i == 0, k == 0))
        def _():
            st_ref[...] = jnp.zeros_like(st_ref)

        @pl.when(k == 0)
        def _():
            for cp in fetch():
                cp.start()
            y_ref[...] = jnp.zeros_like(y_ref)

        z = jnp.maximum(_dot(hn_ref[...], wu_ref[...], NT), 0.0)
        z_ref[...] = z.astype(BF16)
        y_ref[...] += _dot((z * z).astype(BF16), wd_ref[...], NN)

        @pl.when(k == nk - 1)
        def _():
            for cp in fetch():
                cp.wait()
            gate = mod_ref[5:6, :]
            yv = y_ref[...]
            xhat, rstd = _rms(x2_ref[...] + gate * yv)
            diff = xhat * g_ref[...] - t_ref[...]
            dyo = diff * (1.0 / d)
            dx3 = _rms_bwd(dyo * g_ref[...], xhat, rstd)
            dx3_ref[...] = dx3.astype(BF16)
            dyb_ref[...] = (gate * dx3).astype(BF16)
            st_ref[0:1, :] += _colsum(dyo * xhat)
            st_ref[1:2, :] += _colsum(dx3 * yv)
            st_ref[2:3, :] += _colsum(diff * diff)

    tile = pl.BlockSpec((tm, d), lambda i, k: (i, 0))
    wblk = pl.BlockSpec((tk, d), lambda i, k: (k, 0))
    return pl.pallas_call(
        body, name="mlp_fwd_loss", grid=(s // tm, nk),
        in_specs=[tile, wblk, wblk, ANY, ANY, _full(mod6.shape), _full(g_final.shape)],
        out_specs=[pl.BlockSpec((tm, tk), lambda i, k: (i, k)), tile, tile, _full((SUBLANES, d))],
        out_shape=[jax.ShapeDtypeStruct((s, f), BF16), jax.ShapeDtypeStruct((s, d), BF16),
                   jax.ShapeDtypeStruct((s, d), BF16), jax.ShapeDtypeStruct((SUBLANES, d), F32)],
        scratch_shapes=[pltpu.VMEM((tm, d), F32), pltpu.VMEM((tm, d), F32), pltpu.VMEM((tm, d), F32),
                        pltpu.SemaphoreType.DMA((2,))],
        compiler_params=pltpu.CompilerParams(dimension_semantics=("arbitrary", "arbitrary"),
                                             vmem_limit_bytes=VMEM_LIMIT_BIG),
    )(hn2, w_up_t, w_down, x2, target, mod6, g_final)


def _mlp_bwd_dx(dyb, z, w_down, w_up_t, tm, tk):
    s, d = dyb.shape
    f = z.shape[1]

    nk = f // tk

    def body(dy_ref, z_ref, wd_ref, wu_ref, dz_ref, dh_ref, acc_ref):
        k = pl.program_id(1)

        @pl.when(k == 0)
        def _():
            acc_ref[...] = jnp.zeros_like(acc_ref)

        dz = ((2.0 * z_ref[...].astype(F32)) * _dot(dy_ref[...], wd_ref[...], NT)).astype(BF16)
        dz_ref[...] = dz
        acc_ref[...] += _dot(dz, wu_ref[...], NN)

        @pl.when(k == nk - 1)
        def _():
            dh_ref[...] = acc_ref[...].astype(BF16)

    return pl.pallas_call(
        body, name="mlp_bwd_dx", grid=(s // tm, nk),
        in_specs=[pl.BlockSpec((tm, d), lambda i, k: (i, 0)), pl.BlockSpec((tm, tk), lambda i, k: (i, k)),
                  pl.BlockSpec((tk, d), lambda i, k: (k, 0)), pl.BlockSpec((tk, d), lambda i, k: (k, 0))],
        out_specs=[pl.BlockSpec((tm, tk), lambda i, k: (i, k)), pl.BlockSpec((tm, d), lambda i, k: (i, 0))],
        out_shape=[jax.ShapeDtypeStruct((s, f), BF16), jax.ShapeDtypeStruct((s, d), BF16)],
        scratch_shapes=[pltpu.VMEM((tm, d), F32)],
        compiler_params=_params(("parallel", "arbitrary")),
    )(dyb, z, w_down, w_up_t)


def _mlp_bwd_dw(z, dz, dyb, hn2, tm, tk):
    s, d = dyb.shape
    f = z.shape[1]

    def body(z_ref, dz_ref, dy_ref, hn_ref, gd_ref, gu_ref):
        i = pl.program_id(1)

        @pl.when(i == 0)
        def _():
            gd_ref[...] = jnp.zeros_like(gd_ref)
            gu_ref[...] = jnp.zeros_like(gu_ref)

        zf = z_ref[...].astype(F32)
        gd_ref[...] += _dot((zf * zf).astype(BF16), dy_ref[...], TN)
        gu_ref[...] += _dot(dz_ref[...], hn_ref[...], TN)

    return pl.pallas_call(
        body, name="mlp_bwd_dw", grid=(f // tk, s // tm),
        in_specs=[pl.BlockSpec((tm, tk), lambda k, i: (i, k)), pl.BlockSpec((tm, tk), lambda k, i: (i, k)),
                  pl.BlockSpec((tm, d), lambda k, i: (i, 0)), pl.BlockSpec((tm, d), lambda k, i: (i, 0))],
        out_specs=[pl.BlockSpec((tk, d), lambda k, i: (k, 0)), pl.BlockSpec((tk, d), lambda k, i: (k, 0))],
        out_shape=[jax.ShapeDtypeStruct((f, d), F32), jax.ShapeDtypeStruct((f, d), F32)],
        compiler_params=_params(("parallel", "arbitrary")),
    )(z, dz, dyb, hn2)


def _mix_out_bwd(dhn2, x2, dx3, mix, ymix, w_out, mod6, g_mlp, tm):
    s, d = x2.shape

    def body(dh_ref, x2_ref, dx3_ref, mix_ref, y_ref, w_ref, mod_ref, g_ref, dx2_ref, dym_ref, gw_ref, st_ref):
        i = pl.program_id(0)

        @pl.when(i == 0)
        def _():
            st_ref[...] = jnp.zeros_like(st_ref)
            gw_ref[...] = jnp.zeros_like(gw_ref)

        dh = dh_ref[...].astype(F32)
        xhat, rstd = _rms(x2_ref[...])
        dn = dh * (1.0 + mod_ref[4:5, :])
        dx2 = dx3_ref[...].astype(F32) + _rms_bwd(dn * g_ref[...], xhat, rstd)
        dx2_ref[...] = dx2.astype(BF16)
        st_ref[0:1, :] += _colsum(dh)
        st_ref[1:2, :] += _colsum(dh * (xhat * g_ref[...]))
        st_ref[2:3, :] += _colsum(dn * xhat)
        st_ref[3:4, :] += _colsum(dx2 * mix_ref[...].astype(F32))
        dmix = (mod_ref[2:3, :] * dx2).astype(BF16)
        dym_ref[...] = _dot(dmix, w_ref[...], NT).astype(BF16)
        gw_ref[...] += _dot(y_ref[...], dmix, TN)

    tile = pl.BlockSpec((tm, d), lambda i: (i, 0))
    return _call(
        body, "mix_out_bwd", (s // tm,),
        [tile, tile, tile, tile, tile, _full(w_out.shape), _full(mod6.shape), _full(g_mlp.shape)],
        [tile, tile, _full((d, d)), _full((SUBLANES, d))],
        [jax.ShapeDtypeStruct((s, d), BF16), jax.ShapeDtypeStruct((s, d), BF16),
         jax.ShapeDtypeStruct((d, d), F32), jax.ShapeDtypeStruct((SUBLANES, d), F32)],
        [dhn2, x2, dx3, mix, ymix, w_out, mod6, g_mlp])


def _mixer_bwd(proj, dymix, h_all, conv_sc, conv_lru, conv_b, wa_bd, wx_bd, ba, bx, lam, width):
    s, din = proj.shape
    t = min(MIX_ROWS, s)
    nt = s // t
    nblk = width // LANES
    hb = t // SUBLANES
    last8 = s // SUBLANES - 1

    def body(proj_ref, projp_ref, projn_ref, dy_ref, dyn_ref, h_ref, hp_ref,
             wsc_ref, wlru_ref, blru_ref, wa_ref, wx_ref, ba_ref, bx_ref, lam_ref,
             dproj_ref, small_ref, gwa_ref, gwx_ref, an_ref, gn_ref, dun_ref, stage_ref):
        i = pl.program_id(0)

        @pl.when(i == 0)
        def _():
            small_ref[...] = jnp.zeros_like(small_ref)
            gwa_ref[...] = jnp.zeros_like(gwa_ref)
            gwx_ref[...] = jnp.zeros_like(gwx_ref)
            an_ref[...] = jnp.zeros_like(an_ref)
            gn_ref[...] = jnp.zeros_like(gn_ref)
            dun_ref[...] = jnp.zeros_like(dun_ref)

        has_prev = i < nt - 1
        has_next = i > 0
        for j in range(nblk):
            lo = j * LANES
            ls = slice(lo, lo + LANES)

            def col(p, ref=proj_ref):
                return ref[:, p * width + lo:p * width + lo + LANES]

            def prev(p):
                return jnp.where(has_prev, col(p, projp_ref), 0.0)

            def nxt(p):
                return jnp.where(has_next, col(p, projn_ref), 0.0)

            def add_row(r, v):
                small_ref[r:r + 1, ls] += _colsum(v)

            sc_b, sc_c, sc_x = col(0), col(1), col(2)
            p = sc_c * sc_x
            q, p1, p2 = _conv3(p, prev(1) * prev(2), wsc_ref, lo, stage_ref.at[0])
            dys = dy_ref[:, ls].astype(F32)
            dproj_ref[:, ls] = (dys * q).astype(BF16)
            dq = dys * sc_b
            dqn = jnp.where(has_next, dyn_ref[:, ls].astype(F32)[0:SUBLANES], 0.0) * nxt(0)
            _, (dq1, dq2) = _staged_shifts(stage_ref.at[1], dq, None, dqn, (), (1, 2))
            dp = (wsc_ref[2:3, ls] * dq + wsc_ref[1:2, ls] * dq1) + wsc_ref[0:1, ls] * dq2
            dproj_ref[:, width + lo:width + lo + LANES] = (dp * sc_x).astype(BF16)
            dproj_ref[:, 2 * width + lo:2 * width + lo + LANES] = (dp * sc_c).astype(BF16)
            add_row(0, dq * p2)
            add_row(1, dq * p1)
            add_row(2, dq * p)

            xv = col(4)
            u, x1, x2, x3 = _conv4(xv, prev(4), wlru_ref, blru_ref, lo, stage_ref.at[2])
            lam_v = lam_ref[:, ls]
            sp = _softplus(-lam_v)
            wa, wx = wa_ref[j], wx_ref[j]
            ub, r, ig, a, mult = _lru_gates(u, wa, wx, ba_ref[:, ls], bx_ref[:, ls], sp)
            iu = ig * u
            h = h_ref[:, ls]
            (hm1,), _ = _staged_shifts(stage_ref.at[3], h, jnp.where(has_prev, hp_ref[:, ls], 0.0), None, (1,), ())
            lyv = col(3)
            gel, th = _gelu(lyv)
            dyl = dy_ref[:, width + lo:width + lo + LANES].astype(F32)
            dproj_ref[:, 3 * width + lo:3 * width + lo + LANES] = (dyl * h * _dgelu(lyv, th)).astype(BF16)
            a_next = jnp.broadcast_to(an_ref[0:1, ls], (SUBLANES, LANES))
            _, (a_up,) = _staged_shifts(stage_ref.at[4], a, None, a_next, (), (1,))
            g, _ = _scan_tile(a_up, dyl * gel, gn_ref[0:1, ls], stage_ref, 5, True)
            an_ref[0:1, ls] = a[0:1, :]
            gn_ref[0:1, ls] = g[0:1, :]
            da = g * hm1
            dmult = g * iu
            diu = g * mult
            dlog_a = da * a - dmult * ((a * a) / mult)
            dpre_a = (dlog_a * (-RG_C * sp)) * (r * (1.0 - r))
            dpre_x = (diu * u) * (ig * (1.0 - ig))
            dab, dxb = dpre_a.astype(BF16), dpre_x.astype(BF16)
            du = diu * ig + _dot(dab, wa, NT) + _dot(dxb, wx, NT)
            gwa_ref[j] += _dot(ub, dab, TN)
            gwx_ref[j] += _dot(ub, dxb, TN)
            dun = dun_ref[:, ls]
            dun_ref[:, ls] = du[0:SUBLANES, :]
            _, (du1, du2, du3) = _staged_shifts(stage_ref.at[7], du, None, dun, (), (1, 2, 3))
            dlx = (((wlru_ref[3:4, ls] * du + wlru_ref[2:3, ls] * du1) + wlru_ref[1:2, ls] * du2)
                   + wlru_ref[0:1, ls] * du3)
            dproj_ref[:, 4 * width + lo:4 * width + lo + LANES] = dlx.astype(BF16)
            add_row(3, du * x3)
            add_row(4, du * x2)
            add_row(5, du * x1)
            add_row(6, du * xv)
            add_row(7, du)
            add_row(8, dpre_a)
            add_row(9, dpre_x)
            add_row(10, (dlog_a * (RG_C * r)) * jax.nn.sigmoid(-lam_v))

    small = [conv_sc, conv_lru, conv_b, wa_bd, wx_bd, ba, bx, lam]
    rev = lambda i: nt - 1 - i
    return _call(
        body, "mixer_bwd", (nt,),
        [pl.BlockSpec((t, din), lambda i: (rev(i), 0)),
         pl.BlockSpec((SUBLANES, din), lambda i: (jnp.maximum(rev(i) * hb - 1, 0), 0)),
         pl.BlockSpec((SUBLANES, din), lambda i: (jnp.minimum((rev(i) + 1) * hb, last8), 0)),
         pl.BlockSpec((t, 2 * width), lambda i: (rev(i), 0)),
         pl.BlockSpec((2 * SUBLANES, 2 * width), lambda i: (jnp.minimum((rev(i) + 1) * (hb // 2), last8 // 2), 0)),
         pl.BlockSpec((t, width), lambda i: (rev(i), 0)),
         pl.BlockSpec((SUBLANES, width), lambda i: (jnp.maximum(rev(i) * hb - 1, 0), 0))]
        + [_full(a.shape) for a in small],
        [pl.BlockSpec((t, din), lambda i: (rev(i), 0)), _full((2 * SUBLANES, width)),
         _full(wa_bd.shape), _full(wx_bd.shape)],
        [jax.ShapeDtypeStruct((s, din), BF16), jax.ShapeDtypeStruct((2 * SUBLANES, width), F32),
         jax.ShapeDtypeStruct(wa_bd.shape, F32), jax.ShapeDtypeStruct(wx_bd.shape, F32)],
        [proj, proj, proj, dymix, dymix, h_all, h_all, *small],
        scratch=[pltpu.VMEM((SUBLANES, width), F32), pltpu.VMEM((SUBLANES, width), F32),
                 pltpu.VMEM((SUBLANES, width), F32), pltpu.VMEM((8, t + 2 * SUBLANES, LANES), F32)])


def _mix_in_bwd_dx(dproj, x2d, dx2, w_in_t, mod6, g_mix, tm):
    s, d = x2d.shape
    din = dproj.shape[1]

    def body(dp_ref, x_ref, dx2_ref, w_ref, mod_ref, g_ref, gx_ref, st_ref):
        i = pl.program_id(0)

        @pl.when(i == 0)
        def _():
            st_ref[...] = jnp.zeros_like(st_ref)

        dh = _dot(dp_ref[...], w_ref[...], NN)
        xhat, rstd = _rms(x_ref[...])
        dn = dh * (1.0 + mod_ref[1:2, :])
        gx_ref[...] = dx2_ref[...].astype(F32) + _rms_bwd(dn * g_ref[...], xhat, rstd)
        st_ref[0:1, :] += _colsum(dh)
        st_ref[1:2, :] += _colsum(dh * (xhat * g_ref[...]))
        st_ref[2:3, :] += _colsum(dn * xhat)

    tile = pl.BlockSpec((tm, d), lambda i: (i, 0))
    return _call(
        body, "mix_in_bwd_dx", (s // tm,),
        [pl.BlockSpec((tm, din), lambda i: (i, 0)), tile, tile, _full(w_in_t.shape), _full(mod6.shape),
         _full(g_mix.shape)],
        [tile, _full((SUBLANES, d))],
        [jax.ShapeDtypeStruct((s, d), F32), jax.ShapeDtypeStruct((SUBLANES, d), F32)],
        [dproj, x2d, dx2, w_in_t, mod6, g_mix], vmem=VMEM_LIMIT_BIG)


def _mix_in_bwd_dw(dproj, hn1, tm, tn):
    s, d = hn1.shape
    din = dproj.shape[1]

    def body(dp_ref, hn_ref, gw_ref):
        i = pl.program_id(1)

        @pl.when(i == 0)
        def _():
            gw_ref[...] = jnp.zeros_like(gw_ref)

        gw_ref[...] += _dot(dp_ref[...], hn_ref[...], TN)

    return _call(
        body, "mix_in_bwd_dw", (din // tn, s // tm),
        [pl.BlockSpec((tm, tn), lambda p, i: (i, p)), pl.BlockSpec((tm, d), lambda p, i: (i, 0))],
        [pl.BlockSpec((tn, d), lambda p, i: (p, 0))],
        [jax.ShapeDtypeStruct((din, d), F32)],
        [dproj, hn1])


def _adamw(w, g, m, v):
    m = ADAM_B1 * m + (1.0 - ADAM_B1) * g
    v = ADAM_B2 * v + (1.0 - ADAM_B2) * (g * g)
    m_hat = m / (1.0 - ADAM_B1 ** ADAM_STEP)
    v_hat = v / (1.0 - ADAM_B2 ** ADAM_STEP)
    delta = -ADAM_LR * (m_hat / (jnp.sqrt(v_hat) + ADAM_EPS) + ADAM_WD * w)
    return delta, m, v


def _pair_sum(g4s, h4s, core_chip, tr, name):
    na = len(g4s)
    _, _, r, n = g4s[0].shape

    def body(sc_ref, *refs):
        q = pl.program_id(1)
        for a in range(na):
            g_ref, h_ref = refs[2 * a], refs[2 * a + 1]
            sb_ref, own_ref = refs[2 * na + 2 * a], refs[2 * na + 2 * a + 1]
            ssum = g_ref[...] + h_ref[...]
            sb_ref[...] = ssum.astype(BF16)

            @pl.when(q == sc_ref[1])
            def _():
                own_ref[...] = ssum

    grid_spec = pltpu.PrefetchScalarGridSpec(
        num_scalar_prefetch=1, grid=(r // tr, 4),
        in_specs=[pl.BlockSpec((None, None, tr, n), lambda i, q, sc: (q, sc[0], i, 0)),
                  pl.BlockSpec((None, tr, n), lambda i, q, sc: (q, i, 0))] * na,
        out_specs=[pl.BlockSpec((None, tr, n), lambda i, q, sc: (q, i, 0)),
                   pl.BlockSpec((tr, n), lambda i, q, sc: (i, 0))] * na)
    outs = pl.pallas_call(
        body, name=name, grid_spec=grid_spec,
        out_shape=[jax.ShapeDtypeStruct((4, r, n), BF16), jax.ShapeDtypeStruct((r, n), F32)] * na,
        compiler_params=_params(("parallel", "arbitrary")),
    )(core_chip, *[x for pair in zip(g4s, h4s) for x in pair])
    return [(outs[2 * a], outs[2 * a + 1]) for a in range(na)]


def _sum4_adam(own, parts, w, m, v, tr, name, transposed):
    r, n = own.shape
    rows, cols = w.shape

    def body(o_ref, p_ref, w_ref, m_ref, v_ref, g_ref, d_ref, nm_ref, nv_ref):
        g = o_ref[...]
        for k in range(3):
            g = g + p_ref[k].astype(F32)
        if transposed:
            g = g.T
        g_ref[...] = g
        d_ref[...], nm_ref[...], nv_ref[...] = _adamw(w_ref[...], g, m_ref[...], v_ref[...])

    if transposed:
        g_specs = [pl.BlockSpec((r, tr), lambda i: (0, i)), pl.BlockSpec((3, r, tr), lambda i: (0, 0, i))]
    else:
        g_specs = [pl.BlockSpec((tr, n), lambda i: (i, 0)), pl.BlockSpec((3, tr, n), lambda i: (0, i, 0))]
    tile = pl.BlockSpec((tr, cols), lambda i: (i, 0))
    return pl.pallas_call(
        body, name=name, grid=(rows // tr,),
        in_specs=g_specs + [tile] * 3, out_specs=[tile] * 4,
        out_shape=[jax.ShapeDtypeStruct((rows, cols), F32)] * 4,
        compiler_params=_params(("parallel",)),
    )(own, parts, w, m, v)


def _sum4_adam_multi(items, steps, name):
    na = len(items)

    def body(*refs):
        ins, outs = refs[:5 * na], refs[5 * na:]
        for a in range(na):
            o_ref, p_ref, w_ref, m_ref, v_ref = ins[5 * a:5 * a + 5]
            g_ref, d_ref, nm_ref, nv_ref = outs[4 * a:4 * a + 4]
            g = o_ref[...]
            for k in range(3):
                g = g + p_ref[k].astype(F32)
            g_ref[...] = g
            d_ref[...], nm_ref[...], nv_ref[...] = _adamw(w_ref[...], g, m_ref[...], v_ref[...])

    in_specs, out_specs, out_shape = [], [], []
    for own, parts, w, m, v in items:
        rows, cols = w.shape
        tr = rows // steps
        tile = pl.BlockSpec((tr, cols), lambda i: (i, 0))
        in_specs += [tile, pl.BlockSpec((3, tr, cols), lambda i: (0, i, 0)), tile, tile, tile]
        out_specs += [tile] * 4
        out_shape += [jax.ShapeDtypeStruct((rows, cols), F32)] * 4
    outs = pl.pallas_call(
        body, name=name, grid=(steps,), in_specs=in_specs, out_specs=out_specs, out_shape=out_shape,
        compiler_params=_params(("parallel",)),
    )(*[x for item in items for x in item])
    return [outs[4 * a:4 * a + 4] for a in range(na)]


def _sum8(parts, tr, name):
    _, rows, n = parts.shape

    def body(p_ref, o_ref):
        acc = p_ref[0]
        for k in range(1, N_DEV):
            acc = acc + p_ref[k]
        o_ref[...] = acc

    return pl.pallas_call(
        body, name=name, grid=(rows // tr,),
        in_specs=[pl.BlockSpec((N_DEV, tr, n), lambda i: (0, i, 0))],
        out_specs=pl.BlockSpec((tr, n), lambda i: (i, 0)),
        out_shape=jax.ShapeDtypeStruct((rows, n), F32),
        compiler_params=_params(("parallel",)),
    )(parts)


def _ada_bwd_adam(cact_t, dmod_cols, w, m, v, tr):
    rows, n = w.shape

    def body(c_ref, d_ref, w_ref, m_ref, v_ref, g_ref, dl_ref, nm_ref, nv_ref):
        pad = jnp.zeros((N_DEV, tr), F32)
        ca = jnp.concatenate([c_ref[...], pad], axis=0).astype(BF16)
        dm = jnp.concatenate([d_ref[...], jnp.zeros((N_DEV, n), F32)], axis=0).astype(BF16)
        g = _dot(ca, dm, TN)
        g_ref[...] = g
        dl_ref[...], nm_ref[...], nv_ref[...] = _adamw(w_ref[...], g, m_ref[...], v_ref[...])

    tile = pl.BlockSpec((tr, n), lambda i: (i, 0))
    return pl.pallas_call(
        body, name="ada_bwd_adam", grid=(rows // tr,),
        in_specs=[pl.BlockSpec((N_DEV, tr), lambda i: (0, i)), _full(dmod_cols.shape), tile, tile, tile],
        out_specs=[tile] * 4,
        out_shape=[jax.ShapeDtypeStruct((rows, n), F32)] * 4,
        compiler_params=_params(("parallel",)),
    )(cact_t, dmod_cols, w, m, v)


def _adam_small(ws, gs, ms, vs):
    n = len(ws)

    def body(*refs):
        w_r, g_r, m_r, v_r = refs[:n], refs[n:2 * n], refs[2 * n:3 * n], refs[3 * n:4 * n]
        d_r, nm_r, nv_r = refs[4 * n:5 * n], refs[5 * n:6 * n], refs[6 * n:7 * n]
        for k in range(n):
            d_r[k][...], nm_r[k][...], nv_r[k][...] = _adamw(w_r[k][...], g_r[k][...], m_r[k][...], v_r[k][...])

    shapes = [jax.ShapeDtypeStruct(w.shape, F32) for w in ws]
    outs = pl.pallas_call(
        body, name="adam_small", out_shape=shapes * 3, compiler_params=_params(),
    )(*ws, *gs, *ms, *vs)
    return outs[:n], outs[n:2 * n], outs[2 * n:]


def _block_diag(w):
    h, hd, _ = w.shape
    per = LANES // hd
    eye = jnp.eye(per, dtype=w.dtype)
    w5 = w.reshape(h // per, per, hd, 1, hd) * eye[None, :, None, :, None]
    return w5.reshape(h // per, LANES, LANES)


def _block_diag_grad(g, h, hd):
    per = LANES // hd
    g5 = g.reshape(h // per, per, hd, per, hd)
    return jnp.stack([g5[:, a, :, a, :] for a in range(per)], axis=1).reshape(h, hd, hd)


def kernel(x, c, w_ada, b_ada, g_mix, w_in, conv_w_sc, conv_w_lru, conv_b_lru, w_rg_a, b_rg_a, w_rg_x, b_rg_x, lru_lambda, w_out, g_mlp, w_up, w_down, g_final, loss_target, m_w_ada, m_b_ada, m_g_mix, m_w_in, m_conv_w_sc, m_conv_w_lru, m_conv_b_lru, m_w_rg_a, m_b_rg_a, m_w_rg_x, m_b_rg_x, m_lru_lambda, m_w_out, m_g_mlp, m_w_up, m_w_down, m_g_final, v_w_ada, v_b_ada, v_g_mix, v_w_in, v_conv_w_sc, v_conv_w_lru, v_conv_b_lru, v_w_rg_a, v_b_rg_a, v_w_rg_x, v_b_rg_x, v_lru_lambda, v_w_out, v_g_mlp, v_w_up, v_w_down, v_g_final):
    s, d = x.shape[1], x.shape[2]
    width = conv_b_lru.shape[1]
    heads, hd = w_rg_a.shape[1], w_rg_a.shape[2]
    n_ada = w_ada.shape[2]
    csh = conv_w_sc.shape[2]
    me = 4 * lax.axis_index("x") + 2 * lax.axis_index("y") + lax.axis_index("c")
    tm = min(512, s)
    tm_mlp = min(1024, s)
    tk = 512

    x2d = x[0]
    tgt = loss_target[0]

    pay = jnp.zeros((SUBLANES, d), F32)
    pay = pay.at[0:1, :].set(c)
    pay = pay.at[1:4, 0:csh].set(conv_w_sc[0])
    pay = pay.at[4:8, 0:csh].set(conv_w_lru[0])
    w_in_t_sh = w_in[0].T.astype(BF16)
    w_up_t_sh = w_up[0].T.astype(BF16)
    w_out_sh = w_out[0].astype(BF16)
    w_down_sh = w_down[0].astype(BF16)
    (w_in_t,) = _seq_gather2("gather_w_in", 10, [w_in_t_sh])
    (pay_all,) = _gather2("gather_in", [pay])
    w_in_t = w_in_t.reshape(-1, d)
    c_all = pay_all[:, 0, :]
    conv_sc = pay_all[:, 1:4, 0:csh].transpose(1, 0, 2).reshape(3, width)
    conv_lru = pay_all[:, 4:8, 0:csh].transpose(1, 0, 2).reshape(4, width)

    b_ada_sh = lax.dynamic_slice(b_ada, (0, me * n_ada), (1, n_ada))
    mod_cols, c_act = _ada_fwd(c_all, w_ada[0], b_ada_sh)
    (mod_rows,) = _exchange("scatter_mod", [], [mod_cols.reshape(N_DEV, 1, n_ada)])
    mod_rows, w_out_sh, w_up_t_sh, w_down_sh = lax.optimization_barrier((mod_rows, w_out_sh, w_up_t_sh, w_down_sh))
    (w_out_g,) = _seq_gather2("gather_w_out", 1, [w_out_sh])
    w_up_g, w_down_g = _seq_gather2("gather_mlp_weights", 2, [w_up_t_sh, w_down_sh])
    mod6 = jnp.zeros((SUBLANES, d), F32).at[0:6, :].set(mod_rows.reshape(6, d))

    wa_bd = _block_diag(w_rg_a[0]).astype(BF16)
    wx_bd = _block_diag(w_rg_x[0]).astype(BF16)
    ba = b_rg_a.reshape(1, width)
    bx = b_rg_x.reshape(1, width)
    g_fin = g_final.reshape(1, d)

    hn1, proj, ymix, h_all = _mix_in_mixer_fwd(x2d, mod6, g_mix, w_in_t, conv_sc, conv_lru, conv_b_lru,
                                               wa_bd, wx_bd, ba, bx, lru_lambda, width, tm)
    w_out_b = w_out_g.reshape(-1, d)
    mix, x2, hn2 = _mix_out_fwd(ymix, x2d, w_out_b, mod6, g_mlp, tm_mlp)
    w_up_t = w_up_g.reshape(-1, d)
    w_down_b = w_down_g.reshape(-1, d)
    z, dx3, dyb, st_fin = _mlp_fwd_loss(hn2, w_up_t, w_down_b, x2, tgt, mod6, g_fin, tm_mlp, 2 * tk)

    core_chip = jnp.stack([lax.axis_index("c"), 2 * lax.axis_index("x") + lax.axis_index("y")]).astype(jnp.int32)
    dz, dhn2 = _mlp_bwd_dx(dyb, z, w_down_b, w_up_t, tm_mlp, 2 * tk)
    g_down, g_up_t = _mlp_bwd_dw(z, dz, dyb, hn2, tm_mlp, 2 * tk)
    g_up4, g_down4 = g_up_t.reshape(4, 2, -1, d), g_down.reshape(4, 2, -1, d)
    h_up, h_down = _seq_pair_swap("swap_mlp_grads", 7, [g_up4, g_down4])
    dx2, dymix, g_out, st_out = _mix_out_bwd(dhn2, x2, dx3, mix, ymix, w_out_b, mod6, g_mlp, tm)
    h_up, h_down, g_out = lax.optimization_barrier((h_up, h_down, g_out))
    (sb_up, own_up), (sb_down, own_down) = _pair_sum([g_up4, g_down4], [h_up, h_down], core_chip, g_up4.shape[2], "pair_sum_mlp")
    g_out4 = g_out.reshape(4, 2, -1, d)
    (h_out,) = _seq_pair_swap("swap_w_out_grad", 8, [g_out4])
    p_up, p_down = _seq_chip_exchange("exchange_mlp_grads", 3, [sb_up, sb_down])
    dproj, g_small, g_wa, g_wx = _mixer_bwd(
        proj, dymix, h_all, conv_sc, conv_lru, conv_b_lru, wa_bd, wx_bd, ba, bx, lru_lambda, width)
    h_out, dproj = lax.optimization_barrier((h_out, dproj))
    ((sb_out, own_out),) = _pair_sum([g_out4], [h_out], core_chip, g_out4.shape[2], "pair_sum_w_out")
    (p_out,) = _seq_chip_exchange("exchange_w_out_grad", 4, [sb_out])
    grad_x, st_in = _mix_in_bwd_dx(dproj, x2d, dx2, w_in_t, mod6, g_mix, tm_mlp)

    small = jnp.concatenate([
        st_in[0:2], st_out[3:4], st_out[0:2], st_fin[1:2],
        st_in[2:3], st_out[2:3], st_fin[0:1],
        jnp.concatenate([g_small[7:8], g_small[10:11]], axis=1),
        jnp.concatenate([g_small[8:9], g_small[9:10]], axis=1),
        jnp.concatenate([jnp.concatenate([g_small[0:3], jnp.zeros((1, width), F32)], axis=0), g_small[3:7]], axis=1),
        st_fin[2:3],
        _block_diag_grad(g_wa, heads, hd).reshape(-1, d),
        _block_diag_grad(g_wx, heads, hd).reshape(-1, d),
    ], axis=0)

    (small_all,) = _seq_gather2("gather_small_grads", 5, [small])
    g_in_t, = _mix_in_bwd_dw(dproj, hn1, min(2048, s), dproj.shape[1] // 2)
    g_in4 = g_in_t.reshape(4, 2, -1, d)
    (h_in,) = _seq_pair_swap("swap_w_in_grad", 9, [g_in4])
    p_up, p_down, p_out, small_all, g_in_t = lax.optimization_barrier((p_up, p_down, p_out, small_all, g_in_t))

    ad_up = _sum4_adam(own_up, p_up, w_up[0], m_w_up[0], v_w_up[0], 256, "adam_w_up", True)
    h_in, ad_up = lax.optimization_barrier((h_in, ad_up))
    ((sb_in, own_in),) = _pair_sum([g_in4], [h_in], core_chip, g_in4.shape[2], "pair_sum_w_in")
    (p_in,) = _seq_chip_exchange("exchange_w_in_grad", 6, [sb_in])
    ad_out, ad_down = _sum4_adam_multi(
        [(own_out, p_out, w_out[0], m_w_out[0], v_w_out[0]), (own_down, p_down, w_down[0], m_w_down[0], v_w_down[0])],
        2, "adam_w_out_w_down")

    gsum = _sum8(small_all, SMALL_ROWS, "sum_small")
    loss = (0.5 / d) * jnp.sum(gsum[15])
    dmod_cols = lax.dynamic_slice(small_all[:, 0:6, :].reshape(N_DEV, 6 * d), (0, me * n_ada), (N_DEV, n_ada))
    g_ada, d_ada, nm_ada, nv_ada = _ada_bwd_adam(c_act, dmod_cols, w_ada[0], m_w_ada[0], v_w_ada[0], 256)

    g_conv = lax.dynamic_slice(gsum[11:15, 0:width], (0, me * csh), (4, csh))
    g_conv_l = lax.dynamic_slice(gsum[11:15, width:2 * width], (0, me * csh), (4, csh))
    small_g = [
        gsum[0:6].reshape(1, 6 * d),
        gsum[6:7],
        g_conv[0:3].reshape(1, 3, csh),
        g_conv_l.reshape(1, 4, csh),
        gsum[9:10, 0:width],
        gsum[16:48].reshape(1, heads, hd, hd),
        gsum[10:11, 0:width].reshape(1, heads, hd),
        gsum[48:80].reshape(1, heads, hd, hd),
        gsum[10:11, width:].reshape(1, heads, hd),
        gsum[9:10, width:],
        gsum[7:8],
        gsum[8],
    ]
    small_w = [b_ada, g_mix, conv_w_sc, conv_w_lru, conv_b_lru, w_rg_a, b_rg_a, w_rg_x, b_rg_x, lru_lambda, g_mlp, g_final]
    small_m = [m_b_ada, m_g_mix, m_conv_w_sc, m_conv_w_lru, m_conv_b_lru, m_w_rg_a, m_b_rg_a, m_w_rg_x, m_b_rg_x,
               m_lru_lambda, m_g_mlp, m_g_final]
    small_v = [v_b_ada, v_g_mix, v_conv_w_sc, v_conv_w_lru, v_conv_b_lru, v_w_rg_a, v_b_rg_a, v_w_rg_x, v_b_rg_x,
               v_lru_lambda, v_g_mlp, v_g_final]
    sd, snm, snv = _adam_small(small_w, small_g, small_m, small_v)
    p_in, ad_out, ad_down, (g_ada, d_ada, nm_ada, nv_ada), sd = lax.optimization_barrier(
        (p_in, ad_out, ad_down, (g_ada, d_ada, nm_ada, nv_ada), sd))
    ad_in = _sum4_adam(own_in, p_in, w_in[0].T, m_w_in[0].T, v_w_in[0].T, own_in.shape[0], "adam_w_in", False)
    ad_in = [a.T for a in ad_in]

    def order(ada, w_in_, w_out_, w_up_, w_down_, sm):
        return [ada[None], sm[0], sm[1], w_in_[None], sm[2], sm[3], sm[4], sm[5], sm[6], sm[7], sm[8], sm[9],
                w_out_[None], sm[10], w_up_[None], w_down_[None], sm[11]]

    grads = order(g_ada, ad_in[0], ad_out[0], ad_up[0], ad_down[0], small_g)
    deltas = order(d_ada, ad_in[1], ad_out[1], ad_up[1], ad_down[1], sd)
    new_m = order(nm_ada, ad_in[2], ad_out[2], ad_up[2], ad_down[2], snm)
    new_v = order(nv_ada, ad_in[3], ad_out[3], ad_up[3], ad_down[3], snv)
    return (loss, grad_x[None], *grads, *deltas, *new_m, *new_v)
```
